```python
import math
import jax, jax.numpy as jnp
from jax import lax
import numpy as np

D_MODEL = 1024
BATCH = 16
SEQ = 2048
DEPTH = 1

HEAD_DIM = 64
HEADS_PER_GROUP = 4
ATTN_GROUPS = ((128, 1), (512, 4), (2048, 16))
N_ATTN_HEADS = HEADS_PER_GROUP * len(ATTN_GROUPS)
ATTN_OUT_W = HEADS_PER_GROUP * HEAD_DIM
ROPE_DIM = HEAD_DIM // 4
ROPE_THETA = 500000.0
BLOCK = 128
SSM_CH_PER_GROUP = 16
SSM_GROUPS = 32
SSM_W = SSM_CH_PER_GROUP * SSM_GROUPS
SSM_STATE = 64
D_FF = -(-8 * D_MODEL // (3 * 256)) * 256
QKV_W = 3 * N_ATTN_HEADS * HEAD_DIM
GATE_W = 2 * D_MODEL
IN_W = QKV_W + SSM_W + GATE_W
RMS_EPS = 1e-6
NEG_INF = -1e30

kernel_name = "hybrid_dilated_attn_s5_gated_block"


def rmsnorm(x, g):
    x32 = x.astype(jnp.float32)
    y = x32 * lax.rsqrt(jnp.mean(x32 * x32, axis=-1, keepdims=True) + RMS_EPS)
    return (y * g.astype(jnp.float32)).astype(x.dtype)


def partial_rope(t, pos):
    half = ROPE_DIM // 2
    inv = jnp.power(jnp.float32(ROPE_THETA), -jnp.arange(half, dtype=jnp.float32) * 2.0 / ROPE_DIM)
    ang = pos[:, None] * inv[None, :]
    cos = jnp.cos(ang)[None, :, None, :]
    sin = jnp.sin(ang)[None, :, None, :]
    tr = t[..., :ROPE_DIM].astype(jnp.float32)
    t1, t2 = tr[..., :half], tr[..., half:]
    rot = jnp.concatenate([t1 * cos - t2 * sin, t2 * cos + t1 * sin], axis=-1)
    return jnp.concatenate([rot.astype(t.dtype), t[..., ROPE_DIM:]], axis=-1)


def dilated_group_attention(q, k, v, window, dilation):
    B, S, H, E = q.shape
    span = window // dilation
    L = S // dilation
    nb = -(-L // BLOCK)
    Lp = nb * BLOCK

    def to_blocks(t):
        t = t.reshape(B, L, dilation, H, E).transpose(0, 2, 1, 3, 4)
        t = jnp.pad(t, ((0, 0), (0, 0), (0, Lp - L), (0, 0), (0, 0)))
        return t.reshape(B, dilation, nb, BLOCK, H, E)

    def with_prev(t):
        prev = jnp.pad(t[:, :, :-1], ((0, 0), (0, 0), (1, 0), (0, 0), (0, 0), (0, 0)))
        return jnp.concatenate([prev, t], axis=3)

    qb = to_blocks(q)
    kk = with_prev(to_blocks(k))
    vv = with_prev(to_blocks(v)).astype(jnp.float32)
    s = jnp.einsum('bdnqhe,bdnkhe->bdnhqk', qb, kk).astype(jnp.float32) * (HEAD_DIM ** -0.5)
    qi = jnp.arange(BLOCK)[:, None]
    ki = jnp.arange(2 * BLOCK)[None, :]
    dist = qi + BLOCK - ki
    band = (dist >= 0) & (dist <= span)
    blk = jnp.arange(nb)[:, None, None]
    valid = band[None] & ((blk > 0) | (ki >= BLOCK)[None])
    s = jnp.where(valid[None, None, :, None], s, NEG_INF)
    m = jnp.max(s, axis=-1, keepdims=True)
    p = jnp.exp(s - m)
    l = jnp.sum(p, axis=-1, keepdims=True)
    o = jnp.einsum('bdnhqk,bdnkhe->bdnhqe', p, vv) / l
    lse = (m + jnp.log(l))[..., 0]
    o = o.transpose(0, 1, 2, 4, 3, 5).reshape(B, dilation, Lp, H, E)[:, :, :L]
    o = o.transpose(0, 2, 1, 3, 4).reshape(B, S, H, E)
    lse = lse.transpose(0, 1, 2, 4, 3).reshape(B, dilation, Lp, H)[:, :, :L]
    lse = lse.transpose(0, 2, 1, 3).reshape(B, S, H)
    return o, lse


def s5_branch(u, a_re, a_im, log_dt, b_re, b_im, c_re, c_im, d_skip, w_glu):
    B, S, _ = u.shape
    u32 = u.astype(jnp.float32).reshape(B, S, SSM_GROUPS, SSM_CH_PER_GROUP)
    lr, li = a_re.astype(jnp.float32), a_im.astype(jnp.float32)
    dt = jnp.exp(log_dt.astype(jnp.float32))[:, None]
    mag = jnp.exp(lr * dt)
    ab_re, ab_im = mag * jnp.cos(li * dt), mag * jnp.sin(li * dt)
    den = lr * lr + li * li
    nr, ni = ab_re - 1.0, ab_im
    f_re = (nr * lr + ni * li) / den
    f_im = (ni * lr - nr * li) / den
    br, bi = b_re.astype(jnp.float32), b_im.astype(jnp.float32)
    bb_re = f_re[..., None] * br - f_im[..., None] * bi
    bb_im = f_re[..., None] * bi + f_im[..., None] * br
    bu_re = jnp.einsum('bsgc,gnc->bsgn', u32, bb_re)
    bu_im = jnp.einsum('bsgc,gnc->bsgn', u32, bb_im)
    a_r = jnp.broadcast_to(ab_re, bu_re.shape)
    a_i = jnp.broadcast_to(ab_im, bu_im.shape)

    def combine(e1, e2):
        a1r, a1i, b1r, b1i = e1
        a2r, a2i, b2r, b2i = e2
        return (a2r * a1r - a2i * a1i,
                a2r * a1i + a2i * a1r,
                a2r * b1r - a2i * b1i + b2r,
                a2r * b1i + a2i * b1r + b2i)

    _, _, xr, xi = lax.associative_scan(combine, (a_r, a_i, bu_re, bu_im), axis=1)
    y = (jnp.einsum('bsgn,gcn->bsgc', xr, c_re.astype(jnp.float32))
         - jnp.einsum('bsgn,gcn->bsgc', xi, c_im.astype(jnp.float32))
         + d_skip.astype(jnp.float32) * u32)
    y = jax.nn.gelu(y.reshape(B, S, SSM_W)).astype(u.dtype)
    z = y @ w_glu
    za, zb = z[..., :D_MODEL], z[..., D_MODEL:]
    return za * jax.nn.sigmoid(zb)


def _fwd_setup_inputs(seed: int = 0) -> dict:
    key = jax.random.key(seed)
    ks = jax.random.split(key, 24)
    f32 = jnp.float32
    nrm = lambda k, shape, scale: jax.random.normal(k, shape, f32) * scale
    x = jax.random.normal(ks[0], (BATCH, SEQ, D_MODEL), f32)
    norm_mix_g = 1.0 + nrm(ks[1], (DEPTH, D_MODEL), 0.05)
    w_in = nrm(ks[2], (DEPTH, D_MODEL, IN_W), D_MODEL ** -0.5)
    n_idx = jnp.arange(SSM_STATE, dtype=f32)
    ssm_a_re = -0.5 * jnp.exp(nrm(ks[3], (DEPTH, SSM_GROUPS, SSM_STATE), 0.05))
    ssm_a_im = math.pi * n_idx + nrm(ks[4], (DEPTH, SSM_GROUPS, SSM_STATE), 0.01)
    ssm_log_dt = jax.random.uniform(ks[5], (DEPTH, SSM_GROUPS), f32, math.log(1e-3), math.log(1e-1))
    ssm_b_re = nrm(ks[6], (DEPTH, SSM_GROUPS, SSM_STATE, SSM_CH_PER_GROUP), (2.0 * SSM_CH_PER_GROUP) ** -0.5)
    ssm_b_im = nrm(ks[7], (DEPTH, SSM_GROUPS, SSM_STATE, SSM_CH_PER_GROUP), (2.0 * SSM_CH_PER_GROUP) ** -0.5)
    ssm_c_re = nrm(ks[8], (DEPTH, SSM_GROUPS, SSM_CH_PER_GROUP, SSM_STATE), SSM_STATE ** -0.5)
    ssm_c_im = nrm(ks[9], (DEPTH, SSM_GROUPS, SSM_CH_PER_GROUP, SSM_STATE), SSM_STATE ** -0.5)
    ssm_d = nrm(ks[10], (DEPTH, SSM_GROUPS, SSM_CH_PER_GROUP), 1.0)
    w_glu = nrm(ks[11], (DEPTH, SSM_W, 2 * D_MODEL), SSM_W ** -0.5)
    w_attn_out = nrm(ks[12], (DEPTH, ATTN_OUT_W, D_MODEL), ATTN_OUT_W ** -0.5)
    w_out = nrm(ks[13], (DEPTH, D_MODEL, D_MODEL), D_MODEL ** -0.5)
    norm_ffn_g = 1.0 + nrm(ks[14], (DEPTH, D_MODEL), 0.05)
    w_ffn_gate = nrm(ks[15], (DEPTH, D_MODEL, D_FF), D_MODEL ** -0.5)
    w_ffn_up = nrm(ks[16], (DEPTH, D_MODEL, D_FF), D_MODEL ** -0.5)
    w_ffn_down = nrm(ks[17], (DEPTH, D_FF, D_MODEL), D_FF ** -0.5)
    norm_final_g = 1.0 + nrm(ks[18], (D_MODEL,), 0.05)
    return {"x": x, "norm_mix_g": norm_mix_g, "w_in": w_in,
            "ssm_a_re": ssm_a_re, "ssm_a_im": ssm_a_im, "ssm_log_dt": ssm_log_dt,
            "ssm_b_re": ssm_b_re, "ssm_b_im": ssm_b_im, "ssm_c_re": ssm_c_re,
            "ssm_c_im": ssm_c_im, "ssm_d": ssm_d, "w_glu": w_glu,
            "w_attn_out": w_attn_out, "w_out": w_out, "norm_ffn_g": norm_ffn_g,
            "w_ffn_gate": w_ffn_gate, "w_ffn_up": w_ffn_up, "w_ffn_down": w_ffn_down,
            "norm_final_g": norm_final_g}


def _fwd_reference(x, norm_mix_g, w_in, ssm_a_re, ssm_a_im, ssm_log_dt, ssm_b_re, ssm_b_im,
              ssm_c_re, ssm_c_im, ssm_d, w_glu, w_attn_out, w_out, norm_ffn_g,
              w_ffn_gate, w_ffn_up, w_ffn_down, norm_final_g):
    B, S, D = x.shape
    pos = jnp.arange(S, dtype=jnp.float32)
    for layer in range(DEPTH):
        h = rmsnorm(x, norm_mix_g[layer])
        proj = h @ w_in[layer]
        qkv = proj[..., :QKV_W].reshape(B, S, 3, N_ATTN_HEADS, HEAD_DIM)
        u = proj[..., QKV_W:QKV_W + SSM_W]
        gate = jax.nn.sigmoid(proj[..., QKV_W + SSM_W:].astype(jnp.float32)).reshape(B, S, 2, D)
        q = partial_rope(qkv[:, :, 0], pos)
        k = partial_rope(qkv[:, :, 1], pos)
        v = qkv[:, :, 2]
        outs, lses = [], []
        for gi, (window, dilation) in enumerate(ATTN_GROUPS):
            sl = slice(gi * HEADS_PER_GROUP, (gi + 1) * HEADS_PER_GROUP)
            o_g, lse_g = dilated_group_attention(q[:, :, sl], k[:, :, sl], v[:, :, sl], window, dilation)
            outs.append(o_g)
            lses.append(lse_g)
        outs = jnp.stack(outs, axis=0)
        alpha = jax.nn.softmax(jnp.stack(lses, axis=0), axis=0)
        attn = jnp.sum(alpha[..., None] * outs, axis=0).reshape(B, S, ATTN_OUT_W).astype(x.dtype)
        attn_d = attn @ w_attn_out[layer]
        ssm_out = s5_branch(u, ssm_a_re[layer], ssm_a_im[layer], ssm_log_dt[layer],
                            ssm_b_re[layer], ssm_b_im[layer], ssm_c_re[layer], ssm_c_im[layer],
                            ssm_d[layer], w_glu[layer])
        merged = (gate[:, :, 0] * attn_d.astype(jnp.float32)
                  + gate[:, :, 1] * ssm_out.astype(jnp.float32)).astype(x.dtype)
        x = x + merged @ w_out[layer]
        h2 = rmsnorm(x, norm_ffn_g[layer])
        ff = (jax.nn.silu(h2 @ w_ffn_gate[layer]) * (h2 @ w_ffn_up[layer])) @ w_ffn_down[layer]
        x = x + ff
    return rmsnorm(x, norm_final_g)


import jax as _jax
import jax.numpy as _jnp

TWIN_FORMAT = 'train_step'
FWD_PARAMS = ['x', 'norm_mix_g', 'w_in', 'ssm_a_re', 'ssm_a_im', 'ssm_log_dt', 'ssm_b_re', 'ssm_b_im', 'ssm_c_re', 'ssm_c_im', 'ssm_d', 'w_glu', 'w_attn_out', 'w_out', 'norm_ffn_g', 'w_ffn_gate', 'w_ffn_up', 'w_ffn_down', 'norm_final_g']
TWIN_WEIGHTS = ['norm_mix_g', 'w_in', 'ssm_a_re', 'ssm_a_im', 'ssm_log_dt', 'ssm_b_re', 'ssm_b_im', 'ssm_c_re', 'ssm_c_im', 'ssm_d', 'w_glu', 'w_attn_out', 'w_out', 'norm_ffn_g', 'w_ffn_gate', 'w_ffn_up', 'w_ffn_down', 'norm_final_g']
TWIN_DIFF_INPUT = 'x'
TWIN_INPUTS = ['x', 'norm_mix_g', 'w_in', 'ssm_a_re', 'ssm_a_im', 'ssm_log_dt', 'ssm_b_re', 'ssm_b_im', 'ssm_c_re', 'ssm_c_im', 'ssm_d', 'w_glu', 'w_attn_out', 'w_out', 'norm_ffn_g', 'w_ffn_gate', 'w_ffn_up', 'w_ffn_down', 'norm_final_g', 'loss_target', 'm_norm_mix_g', 'm_w_in', 'm_ssm_a_re', 'm_ssm_a_im', 'm_ssm_log_dt', 'm_ssm_b_re', 'm_ssm_b_im', 'm_ssm_c_re', 'm_ssm_c_im', 'm_ssm_d', 'm_w_glu', 'm_w_attn_out', 'm_w_out', 'm_norm_ffn_g', 'm_w_ffn_gate', 'm_w_ffn_up', 'm_w_ffn_down', 'm_norm_final_g', 'v_norm_mix_g', 'v_w_in', 'v_ssm_a_re', 'v_ssm_a_im', 'v_ssm_log_dt', 'v_ssm_b_re', 'v_ssm_b_im', 'v_ssm_c_re', 'v_ssm_c_im', 'v_ssm_d', 'v_w_glu', 'v_w_attn_out', 'v_w_out', 'v_norm_ffn_g', 'v_w_ffn_gate', 'v_w_ffn_up', 'v_w_ffn_down', 'v_norm_final_g']
TWIN_OUTPUTS = ['loss', 'grad_x', 'grad_norm_mix_g', 'grad_w_in', 'grad_ssm_a_re', 'grad_ssm_a_im', 'grad_ssm_log_dt', 'grad_ssm_b_re', 'grad_ssm_b_im', 'grad_ssm_c_re', 'grad_ssm_c_im', 'grad_ssm_d', 'grad_w_glu', 'grad_w_attn_out', 'grad_w_out', 'grad_norm_ffn_g', 'grad_w_ffn_gate', 'grad_w_ffn_up', 'grad_w_ffn_down', 'grad_norm_final_g', 'delta_norm_mix_g', 'delta_w_in', 'delta_ssm_a_re', 'delta_ssm_a_im', 'delta_ssm_log_dt', 'delta_ssm_b_re', 'delta_ssm_b_im', 'delta_ssm_c_re', 'delta_ssm_c_im', 'delta_ssm_d', 'delta_w_glu', 'delta_w_attn_out', 'delta_w_out', 'delta_norm_ffn_g', 'delta_w_ffn_gate', 'delta_w_ffn_up', 'delta_w_ffn_down', 'delta_norm_final_g', 'new_m_norm_mix_g', 'new_m_w_in', 'new_m_ssm_a_re', 'new_m_ssm_a_im', 'new_m_ssm_log_dt', 'new_m_ssm_b_re', 'new_m_ssm_b_im', 'new_m_ssm_c_re', 'new_m_ssm_c_im', 'new_m_ssm_d', 'new_m_w_glu', 'new_m_w_attn_out', 'new_m_w_out', 'new_m_norm_ffn_g', 'new_m_w_ffn_gate', 'new_m_w_ffn_up', 'new_m_w_ffn_down', 'new_m_norm_final_g', 'new_v_norm_mix_g', 'new_v_w_in', 'new_v_ssm_a_re', 'new_v_ssm_a_im', 'new_v_ssm_log_dt', 'new_v_ssm_b_re', 'new_v_ssm_b_im', 'new_v_ssm_c_re', 'new_v_ssm_c_im', 'new_v_ssm_d', 'new_v_w_glu', 'new_v_w_attn_out', 'new_v_w_out', 'new_v_norm_ffn_g', 'new_v_w_ffn_gate', 'new_v_w_ffn_up', 'new_v_w_ffn_down', 'new_v_norm_final_g']
TWIN_LEAF_KINDS = {'loss': 'loss', 'grad_x': 'grad_x', 'grad_norm_mix_g': 'grad_w', 'grad_w_in': 'grad_w', 'grad_ssm_a_re': 'grad_w', 'grad_ssm_a_im': 'grad_w', 'grad_ssm_log_dt': 'grad_w', 'grad_ssm_b_re': 'grad_w', 'grad_ssm_b_im': 'grad_w', 'grad_ssm_c_re': 'grad_w', 'grad_ssm_c_im': 'grad_w', 'grad_ssm_d': 'grad_w', 'grad_w_glu': 'grad_w', 'grad_w_attn_out': 'grad_w', 'grad_w_out': 'grad_w', 'grad_norm_ffn_g': 'grad_w', 'grad_w_ffn_gate': 'grad_w', 'grad_w_ffn_up': 'grad_w', 'grad_w_ffn_down': 'grad_w', 'grad_norm_final_g': 'grad_w', 'delta_norm_mix_g': 'delta_w', 'delta_w_in': 'delta_w', 'delta_ssm_a_re': 'delta_w', 'delta_ssm_a_im': 'delta_w', 'delta_ssm_log_dt': 'delta_w', 'delta_ssm_b_re': 'delta_w', 'delta_ssm_b_im': 'delta_w', 'delta_ssm_c_re': 'delta_w', 'delta_ssm_c_im': 'delta_w', 'delta_ssm_d': 'delta_w', 'delta_w_glu': 'delta_w', 'delta_w_attn_out': 'delta_w', 'delta_w_out': 'delta_w', 'delta_norm_ffn_g': 'delta_w', 'delta_w_ffn_gate': 'delta_w', 'delta_w_ffn_up': 'delta_w', 'delta_w_ffn_down': 'delta_w', 'delta_norm_final_g': 'delta_w', 'new_m_norm_mix_g': 'new_m', 'new_m_w_in': 'new_m', 'new_m_ssm_a_re': 'new_m', 'new_m_ssm_a_im': 'new_m', 'new_m_ssm_log_dt': 'new_m', 'new_m_ssm_b_re': 'new_m', 'new_m_ssm_b_im': 'new_m', 'new_m_ssm_c_re': 'new_m', 'new_m_ssm_c_im': 'new_m', 'new_m_ssm_d': 'new_m', 'new_m_w_glu': 'new_m', 'new_m_w_attn_out': 'new_m', 'new_m_w_out': 'new_m', 'new_m_norm_ffn_g': 'new_m', 'new_m_w_ffn_gate': 'new_m', 'new_m_w_ffn_up': 'new_m', 'new_m_w_ffn_down': 'new_m', 'new_m_norm_final_g': 'new_m', 'new_v_norm_mix_g': 'new_v', 'new_v_w_in': 'new_v', 'new_v_ssm_a_re': 'new_v', 'new_v_ssm_a_im': 'new_v', 'new_v_ssm_log_dt': 'new_v', 'new_v_ssm_b_re': 'new_v', 'new_v_ssm_b_im': 'new_v', 'new_v_ssm_c_re': 'new_v', 'new_v_ssm_c_im': 'new_v', 'new_v_ssm_d': 'new_v', 'new_v_w_glu': 'new_v', 'new_v_w_attn_out': 'new_v', 'new_v_w_out': 'new_v', 'new_v_norm_ffn_g': 'new_v', 'new_v_w_ffn_gate': 'new_v', 'new_v_w_ffn_up': 'new_v', 'new_v_w_ffn_down': 'new_v', 'new_v_norm_final_g': 'new_v'}


def _forward(args):
    return _fwd_reference(*[args[k] for k in FWD_PARAMS])


def _output_shape():
    out = _jax.eval_shape(lambda: _forward(_fwd_setup_inputs(0)))
    return out.shape, out.dtype

N_MICROBATCH = 1
ADAM_LR = 0.001
ADAM_B1 = 0.9
ADAM_B2 = 0.999
ADAM_EPS = 1e-08
ADAM_WD = 0.01
ADAM_STEP = 10
PER_EXAMPLE_BATCH_AXIS = {'x': 0, 'loss_target': 0}
SHARED_INPUTS = []
_WEIGHT_DTYPES = {'norm_mix_g': _jnp.float32, 'w_in': _jnp.float32, 'ssm_a_re': _jnp.float32, 'ssm_a_im': _jnp.float32, 'ssm_log_dt': _jnp.float32, 'ssm_b_re': _jnp.float32, 'ssm_b_im': _jnp.float32, 'ssm_c_re': _jnp.float32, 'ssm_c_im': _jnp.float32, 'ssm_d': _jnp.float32, 'w_glu': _jnp.float32, 'w_attn_out': _jnp.float32, 'w_out': _jnp.float32, 'norm_ffn_g': _jnp.float32, 'w_ffn_gate': _jnp.float32, 'w_ffn_up': _jnp.float32, 'w_ffn_down': _jnp.float32, 'norm_final_g': _jnp.float32}
MOMENT_SCALE = {'norm_mix_g': 5.308834e-02, 'w_in': 2.510978e-02, 'ssm_a_re': 3.973967e-03, 'ssm_a_im': 4.312852e-03, 'ssm_log_dt': 1.348447e+00, 'ssm_b_re': 2.745941e-03, 'ssm_b_im': 2.814421e-03, 'ssm_c_re': 3.969562e-03, 'ssm_c_im': 3.871806e-03, 'ssm_d': 6.034637e-02, 'w_glu': 2.974573e-02, 'w_attn_out': 1.993047e-02, 'w_out': 4.365970e-02, 'norm_ffn_g': 1.277630e-01, 'w_ffn_gate': 5.625481e-02, 'w_ffn_up': 5.463860e-02, 'w_ffn_down': 9.092008e-02, 'norm_final_g': 3.207409e+01}


def _to_microbatches(a, axis):
    t = _jnp.moveaxis(a, axis, 0)
    t = t.reshape((N_MICROBATCH, t.shape[0] // N_MICROBATCH) + t.shape[1:])
    return _jnp.moveaxis(t, 1, axis + 1)


def setup_inputs(seed: int = 0) -> dict:
    inp = _fwd_setup_inputs(seed)
    key = _jax.random.fold_in(_jax.random.key(seed), 7919)
    shape, _ = _output_shape()
    out = dict(inp)
    out["loss_target"] = _jax.random.normal(_jax.random.fold_in(key, 0), shape, _jnp.float32)
    for i, name in enumerate(TWIN_WEIGHTS):
        w = inp[name].astype(_jnp.float32)
        if MOMENT_SCALE is None:
            s = _jnp.sqrt(_jnp.mean(_jnp.square(w)) + 1e-30)
        else:
            s = MOMENT_SCALE[name]
        km, kv = _jax.random.split(_jax.random.fold_in(key, i + 1))
        out[name] = w
        out["m_" + name] = s * _jax.random.normal(km, w.shape, _jnp.float32)
        out["v_" + name] = (s * s) * _jax.random.uniform(kv, w.shape, _jnp.float32, 0.5, 1.5)
    if N_MICROBATCH > 1:
        for name, axis in PER_EXAMPLE_BATCH_AXIS.items():
            out[name] = _to_microbatches(out[name], axis)
    return {'x': out['x'], 'norm_mix_g': out['norm_mix_g'], 'w_in': out['w_in'], 'ssm_a_re': out['ssm_a_re'], 'ssm_a_im': out['ssm_a_im'], 'ssm_log_dt': out['ssm_log_dt'], 'ssm_b_re': out['ssm_b_re'], 'ssm_b_im': out['ssm_b_im'], 'ssm_c_re': out['ssm_c_re'], 'ssm_c_im': out['ssm_c_im'], 'ssm_d': out['ssm_d'], 'w_glu': out['w_glu'], 'w_attn_out': out['w_attn_out'], 'w_out': out['w_out'], 'norm_ffn_g': out['norm_ffn_g'], 'w_ffn_gate': out['w_ffn_gate'], 'w_ffn_up': out['w_ffn_up'], 'w_ffn_down': out['w_ffn_down'], 'norm_final_g': out['norm_final_g'], 'loss_target': out['loss_target'], 'm_norm_mix_g': out['m_norm_mix_g'], 'm_w_in': out['m_w_in'], 'm_ssm_a_re': out['m_ssm_a_re'], 'm_ssm_a_im': out['m_ssm_a_im'], 'm_ssm_log_dt': out['m_ssm_log_dt'], 'm_ssm_b_re': out['m_ssm_b_re'], 'm_ssm_b_im': out['m_ssm_b_im'], 'm_ssm_c_re': out['m_ssm_c_re'], 'm_ssm_c_im': out['m_ssm_c_im'], 'm_ssm_d': out['m_ssm_d'], 'm_w_glu': out['m_w_glu'], 'm_w_attn_out': out['m_w_attn_out'], 'm_w_out': out['m_w_out'], 'm_norm_ffn_g': out['m_norm_ffn_g'], 'm_w_ffn_gate': out['m_w_ffn_gate'], 'm_w_ffn_up': out['m_w_ffn_up'], 'm_w_ffn_down': out['m_w_ffn_down'], 'm_norm_final_g': out['m_norm_final_g'], 'v_norm_mix_g': out['v_norm_mix_g'], 'v_w_in': out['v_w_in'], 'v_ssm_a_re': out['v_ssm_a_re'], 'v_ssm_a_im': out['v_ssm_a_im'], 'v_ssm_log_dt': out['v_ssm_log_dt'], 'v_ssm_b_re': out['v_ssm_b_re'], 'v_ssm_b_im': out['v_ssm_b_im'], 'v_ssm_c_re': out['v_ssm_c_re'], 'v_ssm_c_im': out['v_ssm_c_im'], 'v_ssm_d': out['v_ssm_d'], 'v_w_glu': out['v_w_glu'], 'v_w_attn_out': out['v_w_attn_out'], 'v_w_out': out['v_w_out'], 'v_norm_ffn_g': out['v_norm_ffn_g'], 'v_w_ffn_gate': out['v_w_ffn_gate'], 'v_w_ffn_up': out['v_w_ffn_up'], 'v_w_ffn_down': out['v_w_ffn_down'], 'v_norm_final_g': out['v_norm_final_g']}


def _loss(weights, diff, rest, loss_target):
    with _jax.named_scope("forward"):
        args = {**rest, TWIN_DIFF_INPUT: diff, **{k: w.astype(_WEIGHT_DTYPES[k]) for k, w in weights.items()}}
        y = _forward(args)
    with _jax.named_scope("loss_head"):
        err = _jnp.square(y.astype(_jnp.float32) - loss_target)
        return 0.5 * _jnp.sum(_jnp.mean(err, axis=-1)) if err.ndim else 0.5 * err


def _adamw(w, g, m, v):
    m = ADAM_B1 * m + (1.0 - ADAM_B1) * g
    v = ADAM_B2 * v + (1.0 - ADAM_B2) * _jnp.square(g)
    m_hat = m / (1.0 - ADAM_B1 ** ADAM_STEP)
    v_hat = v / (1.0 - ADAM_B2 ** ADAM_STEP)
    delta = -ADAM_LR * (m_hat / (_jnp.sqrt(v_hat) + ADAM_EPS) + ADAM_WD * w)
    return delta, m, v


def reference(x, norm_mix_g, w_in, ssm_a_re, ssm_a_im, ssm_log_dt, ssm_b_re, ssm_b_im, ssm_c_re, ssm_c_im, ssm_d, w_glu, w_attn_out, w_out, norm_ffn_g, w_ffn_gate, w_ffn_up, w_ffn_down, norm_final_g, loss_target, m_norm_mix_g, m_w_in, m_ssm_a_re, m_ssm_a_im, m_ssm_log_dt, m_ssm_b_re, m_ssm_b_im, m_ssm_c_re, m_ssm_c_im, m_ssm_d, m_w_glu, m_w_attn_out, m_w_out, m_norm_ffn_g, m_w_ffn_gate, m_w_ffn_up, m_w_ffn_down, m_norm_final_g, v_norm_mix_g, v_w_in, v_ssm_a_re, v_ssm_a_im, v_ssm_log_dt, v_ssm_b_re, v_ssm_b_im, v_ssm_c_re, v_ssm_c_im, v_ssm_d, v_w_glu, v_w_attn_out, v_w_out, v_norm_ffn_g, v_w_ffn_gate, v_w_ffn_up, v_w_ffn_down, v_norm_final_g):
    given = dict(x=x, norm_mix_g=norm_mix_g, w_in=w_in, ssm_a_re=ssm_a_re, ssm_a_im=ssm_a_im, ssm_log_dt=ssm_log_dt, ssm_b_re=ssm_b_re, ssm_b_im=ssm_b_im, ssm_c_re=ssm_c_re, ssm_c_im=ssm_c_im, ssm_d=ssm_d, w_glu=w_glu, w_attn_out=w_attn_out, w_out=w_out, norm_ffn_g=norm_ffn_g, w_ffn_gate=w_ffn_gate, w_ffn_up=w_ffn_up, w_ffn_down=w_ffn_down, norm_final_g=norm_final_g, loss_target=loss_target, m_norm_mix_g=m_norm_mix_g, m_w_in=m_w_in, m_ssm_a_re=m_ssm_a_re, m_ssm_a_im=m_ssm_a_im, m_ssm_log_dt=m_ssm_log_dt, m_ssm_b_re=m_ssm_b_re, m_ssm_b_im=m_ssm_b_im, m_ssm_c_re=m_ssm_c_re, m_ssm_c_im=m_ssm_c_im, m_ssm_d=m_ssm_d, m_w_glu=m_w_glu, m_w_attn_out=m_w_attn_out, m_w_out=m_w_out, m_norm_ffn_g=m_norm_ffn_g, m_w_ffn_gate=m_w_ffn_gate, m_w_ffn_up=m_w_ffn_up, m_w_ffn_down=m_w_ffn_down, m_norm_final_g=m_norm_final_g, v_norm_mix_g=v_norm_mix_g, v_w_in=v_w_in, v_ssm_a_re=v_ssm_a_re, v_ssm_a_im=v_ssm_a_im, v_ssm_log_dt=v_ssm_log_dt, v_ssm_b_re=v_ssm_b_re, v_ssm_b_im=v_ssm_b_im, v_ssm_c_re=v_ssm_c_re, v_ssm_c_im=v_ssm_c_im, v_ssm_d=v_ssm_d, v_w_glu=v_w_glu, v_w_attn_out=v_w_attn_out, v_w_out=v_w_out, v_norm_ffn_g=v_norm_ffn_g, v_w_ffn_gate=v_w_ffn_gate, v_w_ffn_up=v_w_ffn_up, v_w_ffn_down=v_w_ffn_down, v_norm_final_g=v_norm_final_g)
    weights = {n: given[n] for n in TWIN_WEIGHTS}
    shared = {n: given[n] for n in SHARED_INPUTS}
    per_example = {n: given[n] for n in ['x']}
    grad_fn = _jax.value_and_grad(_loss, argnums=(0, 1))

    def one_microbatch(ex, loss_target):
        ex = dict(ex)
        diff = ex.pop(TWIN_DIFF_INPUT)
        return grad_fn(weights, diff, {**shared, **ex}, loss_target)

    if N_MICROBATCH == 1:
        loss, (grad_w, grad_x) = one_microbatch(per_example, given["loss_target"])
    else:
        def body(carry, xs):
            loss_sum, grad_sum = carry
            l_k, (gw_k, gx_k) = one_microbatch(xs[0], xs[1])
            with _jax.named_scope("update"):
                return (loss_sum + l_k, _jax.tree.map(_jnp.add, grad_sum, gw_k)), gx_k

        init = (_jnp.zeros((), _jnp.float32), _jax.tree.map(_jnp.zeros_like, weights))
        (loss, grad_w), grad_x = _jax.lax.scan(body, init, (per_example, given["loss_target"]))
    with _jax.named_scope("update"):
        delta_w, new_m, new_v = {}, {}, {}
        for n in TWIN_WEIGHTS:
            delta_w[n], new_m[n], new_v[n] = _adamw(weights[n], grad_w[n], given["m_" + n], given["v_" + n])
    return (loss, grad_x, *[grad_w[n] for n in TWIN_WEIGHTS], *[delta_w[n] for n in TWIN_WEIGHTS],
            *[new_m[n] for n in TWIN_WEIGHTS], *[new_v[n] for n in TWIN_WEIGHTS])
```

```python
import functools
import math

import jax
import jax.numpy as jnp
from jax import lax
from jax.experimental import pallas as pl
from jax.experimental.pallas import tpu as pltpu

F32 = jnp.float32
BF16 = jnp.bfloat16
MXU_DTYPE = jnp.bfloat16

N_DEV = 8
D_MODEL = 1024
SEQ = 2048
HEAD_DIM = 64
HEADS_PER_GROUP = 4
GROUP_W = HEADS_PER_GROUP * HEAD_DIM
DILATIONS = (1, 4, 16)
QKV_W = 3 * len(DILATIONS) * GROUP_W
Q_W = len(DILATIONS) * GROUP_W
ATT_BLOCK = 128
ROPE_DIM = 16
ROPE_THETA = 500000.0
SSM_W = 512
SSM_GROUPS = 32
SSM_CH = 16
SSM_STATE = 64
N_STATE = SSM_GROUPS * SSM_STATE
D_FF = 2816
IN_W = QKV_W + SSM_W + 2 * D_MODEL
RMS_EPS = 1e-6
NEG_INF = -1e30
LANES = 128

SCAN_SEG_PER_SAMPLE = 8
SCAN_LEN = SEQ // SCAN_SEG_PER_SAMPLE
SCAN_WC = 128
SCAN_NBLK = N_STATE // SCAN_WC

ADAM_LR = 0.001
ADAM_B1 = 0.9
ADAM_B2 = 0.999
ADAM_EPS = 1e-08
ADAM_WD = 0.01
ADAM_STEP = 10

VMEM_BIG = 48 * 1024 * 1024
VMEM_MID = 32 * 1024 * 1024

BIG_WEIGHTS = ("w_in", "w_glu", "w_attn_out", "w_out", "w_ffn_gate", "w_ffn_up", "w_ffn_down")
ROW_SHARDED = ("w_out", "w_ffn_down")
SMALL_WEIGHTS = ("norm_mix_g", "ssm_a_re", "ssm_a_im", "ssm_log_dt", "ssm_b_re", "ssm_b_im", "ssm_c_re", "ssm_c_im",
                 "ssm_d", "norm_ffn_g", "norm_final_g")
ALL_WEIGHTS = ("norm_mix_g", "w_in", "ssm_a_re", "ssm_a_im", "ssm_log_dt", "ssm_b_re", "ssm_b_im", "ssm_c_re", "ssm_c_im",
               "ssm_d", "w_glu", "w_attn_out", "w_out", "norm_ffn_g", "w_ffn_gate", "w_ffn_up", "w_ffn_down", "norm_final_g")


def _sigmoid(x):
    return 1.0 / (1.0 + jnp.exp(-x))


def _mm(a, b, mode, name, tm, tn, out_dtype=F32, add=None, vmem=VMEM_BIG):
    if mode == "nn":
        (m, k), (_, n) = a.shape, b.shape
        a_spec = pl.BlockSpec((tm, k), lambda i, j: (i, 0))
        b_spec = pl.BlockSpec((k, tn), lambda i, j: (0, j))
        dims = (((1,), (0,)), ((), ()))
    elif mode == "nt":
        (m, k), (n, _) = a.shape, b.shape
        a_spec = pl.BlockSpec((tm, k), lambda i, j: (i, 0))
        b_spec = pl.BlockSpec((tn, k), lambda i, j: (j, 0))
        dims = (((1,), (1,)), ((), ()))
    else:
        (k, m), (_, n) = a.shape, b.shape
        a_spec = pl.BlockSpec((k, tm), lambda i, j: (0, i))
        b_spec = pl.BlockSpec((k, tn), lambda i, j: (0, j))
        dims = (((0,), (0,)), ((), ()))
    assert m % tm == 0 and n % tn == 0, (name, m, n, tm, tn)
    o_spec = pl.BlockSpec((tm, tn), lambda i, j: (i, j))
    has_add = add is not None

    def body(*refs):
        a_ref, b_ref = refs[0], refs[1]
        o_ref = refs[-1]
        acc = lax.dot_general(a_ref[...].astype(MXU_DTYPE), b_ref[...].astype(MXU_DTYPE), dims,
                              preferred_element_type=F32)
        if has_add:
            acc = acc + refs[2][...]
        o_ref[...] = acc.astype(out_dtype)

    ins = [a, b] + ([add] if has_add else [])
    in_specs = [a_spec, b_spec] + ([o_spec] if has_add else [])
    return pl.pallas_call(
        body, name=name, grid=(m // tm, n // tn), in_specs=in_specs, out_specs=o_spec,
        out_shape=jax.ShapeDtypeStruct((m, n), out_dtype),
        compiler_params=pltpu.CompilerParams(dimension_semantics=("parallel", "parallel"), vmem_limit_bytes=vmem),
    )(*ins)


def _rows(body, name, n_rows, tm, ins, outs, vmem=VMEM_MID):
    assert n_rows % tm == 0
    arrays, in_specs = [], []
    for kind, arr in ins:
        arrays.append(arr)
        if kind == "row":
            assert arr.shape[0] == n_rows, (name, arr.shape)
            in_specs.append(pl.BlockSpec((tm, arr.shape[1]), lambda i: (i, 0)))
        elif kind == "tab":
            nblk = arr.shape[0] // tm
            in_specs.append(pl.BlockSpec((tm, arr.shape[1]), lambda i, nblk=nblk: (i % nblk, 0)))
        else:
            in_specs.append(pl.BlockSpec(arr.shape, lambda i, nd=arr.ndim: (0,) * nd))
    out_specs, out_shape = [], []
    for kind, shp, dt in outs:
        if kind == "row":
            out_specs.append(pl.BlockSpec((tm, shp), lambda i: (i, 0)))
            out_shape.append(jax.ShapeDtypeStruct((n_rows, shp), dt))
        else:
            out_specs.append(pl.BlockSpec(shp, lambda i, nd=len(shp): (0,) * nd))
            out_shape.append(jax.ShapeDtypeStruct(shp, dt))
    res = pl.pallas_call(
        body, name=name, grid=(n_rows // tm,), in_specs=in_specs, out_specs=out_specs, out_shape=out_shape,
        compiler_params=pltpu.CompilerParams(dimension_semantics=("arbitrary",), vmem_limit_bytes=vmem),
    )(*arrays)
    return res


def _first_step():
    return pl.program_id(0) == 0


def _rms_fwd(x, g, name):
    def body(x_ref, g_ref, h_ref):
        xv = x_ref[...]
        r = lax.rsqrt(jnp.mean(xv * xv, axis=-1, keepdims=True) + RMS_EPS)
        h_ref[...] = ((xv * r) * g_ref[...]).astype(BF16)

    return _rows(body, name, x.shape[0], 512, [("row", x), ("const", g)], [("row", x.shape[1], BF16)])[0]


def _rms_bwd(x, g, dh, dres, name):
    def body(x_ref, g_ref, dh_ref, dres_ref, dx_ref, dxb_ref, gg_ref):
        @pl.when(_first_step())
        def _():
            gg_ref[...] = jnp.zeros_like(gg_ref)

        xv = x_ref[...]
        r = lax.rsqrt(jnp.mean(xv * xv, axis=-1, keepdims=True) + RMS_EPS)
        n = xv * r
        dh_v = dh_ref[...]
        gg_ref[...] += jnp.sum(dh_v * n, axis=0, keepdims=True)
        dn = dh_v * g_ref[...]
        dx = dres_ref[...] + r * (dn - n * jnp.mean(dn * n, axis=-1, keepdims=True))
        dx_ref[...] = dx
        dxb_ref[...] = dx.astype(BF16)

    d = x.shape[1]
    return _rows(body, name, x.shape[0], 256, [("row", x), ("const", g), ("row", dh), ("row", dres)],
                 [("row", d, F32), ("row", d, BF16), ("acc", (1, d), F32)])


def _rope_tables():
    half = ROPE_DIM // 2
    inv = jnp.power(jnp.float32(ROPE_THETA), -jnp.arange(half, dtype=F32) * 2.0 / ROPE_DIM)
    ang = jnp.arange(SEQ, dtype=F32)[:, None] * inv[None, :]
    lane = jnp.arange(LANES) % HEAD_DIM
    cosl = jnp.cos(ang)[:, lane % half]
    sinl = jnp.sin(ang)[:, lane % half]
    tab_c = jnp.where(lane < ROPE_DIM, cosl, 1.0)
    tab_lo = jnp.where(lane < half, -sinl, 0.0)
    tab_hi = jnp.where((lane >= half) & (lane < ROPE_DIM), sinl, 0.0)
    return tab_c.astype(F32), tab_lo.astype(F32), tab_hi.astype(F32)


def _rope_apply(t, tc, tlo, thi):
    half = ROPE_DIM // 2
    return t * tc + pltpu.roll(t, LANES - half, 1) * tlo + pltpu.roll(t, half, 1) * thi


def _rope_transpose(dt, tc, tlo, thi):
    half = ROPE_DIM // 2
    return dt * tc + pltpu.roll(dt * tlo, half, 1) + pltpu.roll(dt * thi, LANES - half, 1)


def _rope_split(proj, tabs):
    def body(p_ref, tc_ref, tlo_ref, thi_ref, q_ref, k_ref, v_ref, u_ref, g_ref):
        tc, tlo, thi = tc_ref[...], tlo_ref[...], thi_ref[...]
        for ch in range(Q_W // LANES):
            sl = slice(ch * LANES, (ch + 1) * LANES)
            q_ref[:, sl] = _rope_apply(p_ref[:, sl], tc, tlo, thi).astype(BF16)
            k_ref[:, sl] = _rope_apply(p_ref[:, Q_W + ch * LANES:Q_W + (ch + 1) * LANES], tc, tlo, thi).astype(BF16)
        v_ref[...] = p_ref[:, 2 * Q_W:QKV_W].astype(BF16)
        u_ref[...] = p_ref[:, QKV_W:QKV_W + SSM_W]
        g_ref[...] = _sigmoid(p_ref[:, QKV_W + SSM_W:])

    t = proj.shape[0]
    return _rows(body, "rope_split", t, 256,
                 [("row", proj), ("tab", tabs[0]), ("tab", tabs[1]), ("tab", tabs[2])],
                 [("row", Q_W, BF16), ("row", Q_W, BF16), ("row", Q_W, BF16), ("row", SSM_W, F32),
                  ("row", 2 * D_MODEL, F32)])


def _pack_dproj(dqs, dks, dvs, du, dgpre, tabs):
    def body(*refs):
        dq_refs, dk_refs, dv_refs = refs[0:3], refs[3:6], refs[6:9]
        du_ref, dg_ref, tc_ref, tlo_ref, thi_ref, o_ref = refs[9:15]
        tc, tlo, thi = tc_ref[...], tlo_ref[...], thi_ref[...]
        for g in range(3):
            for half in range(GROUP_W // LANES):
                sl = slice(half * LANES, (half + 1) * LANES)
                col = g * GROUP_W + half * LANES
                o_ref[:, col:col + LANES] = _rope_transpose(dq_refs[g][:, sl], tc, tlo, thi).astype(BF16)
                o_ref[:, Q_W + col:Q_W + col + LANES] = _rope_transpose(dk_refs[g][:, sl], tc, tlo, thi).astype(BF16)
            o_ref[:, 2 * Q_W + g * GROUP_W:2 * Q_W + (g + 1) * GROUP_W] = dv_refs[g][...].astype(BF16)
        o_ref[:, QKV_W:QKV_W + SSM_W] = du_ref[...].astype(BF16)
        o_ref[:, QKV_W + SSM_W:] = dg_ref[...].astype(BF16)

    t = du.shape[0]
    ins = [("row", a) for a in (*dqs, *dks, *dvs, du, dgpre)] + [("tab", tb) for tb in tabs]
    return _rows(body, "pack_dproj", t, 256, ins, [("row", IN_W, BF16)])[0]


def _attn_merge(os_, lses):
    def body(o0, o1, o2, l0, l1, l2, a_ref, lt_ref):
        la, lb, lc = l0[...], l1[...], l2[...]
        m = jnp.maximum(jnp.maximum(la, lb), lc)
        ea, eb, ec = jnp.exp(la - m), jnp.exp(lb - m), jnp.exp(lc - m)
        ssum = ea + eb + ec
        a_ref[...] = (ea / ssum) * o0[...] + (eb / ssum) * o1[...] + (ec / ssum) * o2[...]
        lt_ref[...] = m + jnp.log(ssum)

    t = os_[0].shape[0]
    return _rows(body, "attn_merge", t, 512, [("row", a) for a in (*os_, *lses)],
                 [("row", GROUP_W, F32), ("row", GROUP_W, F32)])


def _head_sum_matrix():
    r = jnp.arange(GROUP_W) // HEAD_DIM
    return (r[:, None] == r[None, :]).astype(F32)


def _attn_rowdot(dattn, attn):
    def body(da_ref, a_ref, ones_ref, d_ref):
        d_ref[...] = jnp.dot(da_ref[...] * a_ref[...], ones_ref[...], preferred_element_type=F32,
                             precision=lax.Precision.HIGHEST)

    t = attn.shape[0]
    return _rows(body, "attn_rowdot", t, 512, [("row", dattn), ("row", attn), ("const", _head_sum_matrix())],
                 [("row", GROUP_W, F32)])[0]


def _mix(attn_d, z, gates):
    def body(ad_ref, z_ref, g_ref, m_ref):
        za, zb = z_ref[:, :D_MODEL], z_ref[:, D_MODEL:]
        s_out = za * _sigmoid(zb)
        m_ref[...] = (g_ref[:, :D_MODEL] * ad_ref[...] + g_ref[:, D_MODEL:] * s_out).astype(BF16)

    t = attn_d.shape[0]
    return _rows(body, "mix", t, 256, [("row", attn_d), ("row", z), ("row", gates)], [("row", D_MODEL, BF16)])[0]


def _mix_bwd(dmerged, gates, attn_d, z):
    def body(dm_ref, g_ref, ad_ref, z_ref, dad_ref, dz_ref, dg_ref):
        dm = dm_ref[...]
        g0, g1 = g_ref[:, :D_MODEL], g_ref[:, D_MODEL:]
        za, zb = z_ref[:, :D_MODEL], z_ref[:, D_MODEL:]
        sb = _sigmoid(zb)
        s_out = za * sb
        dad_ref[...] = (dm * g0).astype(BF16)
        ds = dm * g1
        dz_ref[:, :D_MODEL] = (ds * sb).astype(BF16)
        dz_ref[:, D_MODEL:] = (ds * za * sb * (1.0 - sb)).astype(BF16)
        dg_ref[:, :D_MODEL] = dm * ad_ref[...] * g0 * (1.0 - g0)
        dg_ref[:, D_MODEL:] = dm * s_out * g1 * (1.0 - g1)

    t = dmerged.shape[0]
    return _rows(body, "mix_bwd", t, 256, [("row", dmerged), ("row", gates), ("row", attn_d), ("row", z)],
                 [("row", D_MODEL, BF16), ("row", 2 * D_MODEL, BF16), ("row", 2 * D_MODEL, F32)])


def _swiglu(ab):
    def body(ab_ref, f_ref):
        a, b = ab_ref[:, :D_FF], ab_ref[:, D_FF:]
        f_ref[...] = (a * _sigmoid(a) * b).astype(BF16)

    return _rows(body, "swiglu", ab.shape[0], 256, [("row", ab)], [("row", D_FF, BF16)])[0]


def _swiglu_bwd(df, ab):
    def body(df_ref, ab_ref, o_ref):
        a, b = ab_ref[:, :D_FF], ab_ref[:, D_FF:]
        d = df_ref[...]
        sg = _sigmoid(a)
        o_ref[:, :D_FF] = (d * b * sg * (1.0 + a * (1.0 - sg))).astype(BF16)
        o_ref[:, D_FF:] = (d * a * sg).astype(BF16)

    return _rows(body, "swiglu_bwd", ab.shape[0], 256, [("row", df), ("row", ab)], [("row", 2 * D_FF, BF16)])[0]


def _final(x2, target, g):
    def body(x_ref, t_ref, g_ref, dx_ref, dxb_ref, loss_ref, gg_ref):
        @pl.when(_first_step())
        def _():
            loss_ref[...] = jnp.zeros_like(loss_ref)
            gg_ref[...] = jnp.zeros_like(gg_ref)

        xv = x_ref[...]
        gv = g_ref[...]
        r = lax.rsqrt(jnp.mean(xv * xv, axis=-1, keepdims=True) + RMS_EPS)
        n = xv * r
        diff = n * gv - t_ref[...]
        per_tok = jnp.mean(diff * diff, axis=-1, keepdims=True)
        loss_ref[...] += 0.5 * jnp.sum(per_tok, axis=0, keepdims=True)
        dy = diff / xv.shape[-1]
        gg_ref[...] += jnp.sum(dy * n, axis=0, keepdims=True)
        dn = dy * gv
        dx = r * (dn - n * jnp.mean(dn * n, axis=-1, keepdims=True))
        dx_ref[...] = dx
        dxb_ref[...] = dx.astype(BF16)

    d = x2.shape[1]
    return _rows(body, "final_loss", x2.shape[0], 256, [("row", x2), ("row", target), ("const", g)],
                 [("row", d, F32), ("row", d, BF16), ("acc", (8, LANES), F32), ("acc", (1, d), F32)])


_GELU_C = math.sqrt(2.0 / math.pi)


def _ssm_act(ypre, u_perm, dskip):
    def body(y_ref, u_ref, d_ref, yt_ref, yg_ref):
        yt = y_ref[...] + d_ref[...] * u_ref[...]
        yt_ref[...] = yt
        th = jnp.tanh(_GELU_C * (yt + 0.044715 * (yt * yt * yt)))
        yg_ref[...] = (0.5 * yt * (1.0 + th)).astype(BF16)

    t = ypre.shape[0]
    return _rows(body, "ssm_act", t, 512, [("row", ypre), ("row", u_perm), ("const", dskip)],
                 [("row", SSM_W, F32), ("row", SSM_W, BF16)])


def _ssm_act_bwd(dyg, ytot, u_perm, dskip):
    def body(dyg_ref, yt_ref, u_ref, d_ref, dy_ref, dus_ref, dd_ref):
        @pl.when(_first_step())
        def _():
            dd_ref[...] = jnp.zeros_like(dd_ref)

        yt = yt_ref[...]
        th = jnp.tanh(_GELU_C * (yt + 0.044715 * (yt * yt * yt)))
        dgelu = 0.5 * (1.0 + th) + 0.5 * yt * (1.0 - th * th) * _GELU_C * (1.0 + 3.0 * 0.044715 * yt * yt)
        dy = dyg_ref[...] * dgelu
        dy_ref[...] = dy.astype(BF16)
        dus_ref[...] = dy * d_ref[...]
        dd_ref[...] += jnp.sum(dy * u_ref[...], axis=0, keepdims=True)

    t = dyg.shape[0]
    return _rows(body, "ssm_act_bwd", t, 512, [("row", dyg), ("row", ytot), ("row", u_perm), ("const", dskip)],
                 [("row", SSM_W, BF16), ("row", SSM_W, F32), ("acc", (1, SSM_W), F32)])


def _head_masks():
    lane = lax.broadcasted_iota(jnp.int32, (1, GROUP_W), 1)
    return [(lane // HEAD_DIM) == h for h in range(HEADS_PER_GROUP)]


def _band_mask(first):
    nk = ATT_BLOCK if first else 2 * ATT_BLOCK
    qi = lax.broadcasted_iota(jnp.int32, (ATT_BLOCK, nk), 0)
    ki = lax.broadcasted_iota(jnp.int32, (ATT_BLOCK, nk), 1)
    dist = qi - ki + (0 if first else ATT_BLOCK)
    return (dist >= 0) & (dist <= ATT_BLOCK)


_NT = (((1,), (1,)), ((), ()))
_TN = (((0,), (0,)), ((), ()))


def _attn_fwd(q, k, v, group, n_samples):
    d = DILATIONS[group]
    length = SEQ // d
    nb = length // ATT_BLOCK

    def body(q_ref, k_ref, v_ref, o_ref, l_ref):
        masks = _head_masks()

        def block(qs, ks, first):
            nk = ATT_BLOCK if first else 2 * ATT_BLOCK
            qb = q_ref[0, pl.ds(qs, ATT_BLOCK), :]
            kc = k_ref[0, pl.ds(ks, nk), :]
            vc = v_ref[0, pl.ds(ks, nk), :]
            valid = _band_mask(first)
            o_acc = jnp.zeros((ATT_BLOCK, GROUP_W), F32)
            l_acc = jnp.zeros((ATT_BLOCK, GROUP_W), F32)
            for h in range(HEADS_PER_GROUP):
                qh = jnp.where(masks[h], qb, jnp.zeros_like(qb))
                s = lax.dot_general(qh, kc, _NT, preferred_element_type=F32) * (HEAD_DIM ** -0.5)
                s = jnp.where(valid, s, NEG_INF)
                m = jnp.max(s, axis=-1, keepdims=True)
                p = jnp.exp(s - m)
                l = jnp.sum(p, axis=-1, keepdims=True)
                pv = jnp.dot(p.astype(MXU_DTYPE), vc, preferred_element_type=F32)
                o_acc = jnp.where(masks[h], pv / l, o_acc)
                l_acc = jnp.where(masks[h], m + jnp.log(l), l_acc)
            o_ref[0, pl.ds(qs, ATT_BLOCK), :] = o_acc
            l_ref[0, pl.ds(qs, ATT_BLOCK), :] = l_acc

        block(0, 0, True)
        if nb > 1:
            def loop(n, carry):
                block(pl.multiple_of(n * ATT_BLOCK, ATT_BLOCK), pl.multiple_of((n - 1) * ATT_BLOCK, ATT_BLOCK), False)
                return carry

            lax.fori_loop(1, nb, loop, 0)

    qv = q.reshape(n_samples, length, d * Q_W)
    kv = k.reshape(n_samples, length, d * Q_W)
    vv = v.reshape(n_samples, length, d * Q_W)
    in_spec = pl.BlockSpec((1, length, GROUP_W), lambda b, r: (b, 0, 3 * r + group))
    out_spec = pl.BlockSpec((1, length, GROUP_W), lambda b, r: (b, 0, r))
    shp = jax.ShapeDtypeStruct((n_samples, length, d * GROUP_W), F32)
    o, lse = pl.pallas_call(
        body, name=f"attn_fwd_g{group}", grid=(n_samples, d), in_specs=[in_spec] * 3, out_specs=[out_spec] * 2,
        out_shape=[shp, shp],
        compiler_params=pltpu.CompilerParams(dimension_semantics=("parallel", "parallel"), vmem_limit_bytes=VMEM_MID),
    )(qv, kv, vv)
    return o.reshape(n_samples * SEQ, GROUP_W), lse.reshape(n_samples * SEQ, GROUP_W)


def _attn_bwd(q, k, v, dattn, lse_tot, rowdot, group, n_samples):
    d = DILATIONS[group]
    length = SEQ // d
    nb = length // ATT_BLOCK

    def body(q_ref, k_ref, v_ref, da_ref, lt_ref, rd_ref, dq_ref, dk_ref, dv_ref):
        masks = _head_masks()
        dk_ref[...] = jnp.zeros_like(dk_ref)
        dv_ref[...] = jnp.zeros_like(dv_ref)

        def block(qs, ks, first):
            nk = ATT_BLOCK if first else 2 * ATT_BLOCK
            qb = q_ref[0, pl.ds(qs, ATT_BLOCK), :]
            kc = k_ref[0, pl.ds(ks, nk), :]
            vc = v_ref[0, pl.ds(ks, nk), :]
            da = da_ref[0, pl.ds(qs, ATT_BLOCK), :]
            lt = lt_ref[0, pl.ds(qs, ATT_BLOCK), :]
            rd = rd_ref[0, pl.ds(qs, ATT_BLOCK), :]
            valid = _band_mask(first)
            dq_acc = jnp.zeros((ATT_BLOCK, GROUP_W), F32)
            dk_acc = jnp.zeros((nk, GROUP_W), F32)
            dv_acc = jnp.zeros((nk, GROUP_W), F32)
            for h in range(HEADS_PER_GROUP):
                qh = jnp.where(masks[h], qb, jnp.zeros_like(qb))
                dah = jnp.where(masks[h], da, 0.0).astype(MXU_DTYPE)
                lt_h = jnp.max(jnp.where(masks[h], lt, -jnp.inf), axis=-1, keepdims=True)
                rd_h = jnp.max(jnp.where(masks[h], rd, -jnp.inf), axis=-1, keepdims=True)
                s = lax.dot_general(qh, kc, _NT, preferred_element_type=F32) * (HEAD_DIM ** -0.5)
                s = jnp.where(valid, s, NEG_INF)
                p = jnp.exp(s - lt_h)
                dp = lax.dot_general(dah, vc, _NT, preferred_element_type=F32)
                ds = (p * (dp - rd_h) * (HEAD_DIM ** -0.5)).astype(MXU_DTYPE)
                dq_h = jnp.dot(ds, kc, preferred_element_type=F32)
                dq_acc = jnp.where(masks[h], dq_h, dq_acc)
                dk_acc = dk_acc + lax.dot_general(ds, qh, _TN, preferred_element_type=F32)
                dv_acc = dv_acc + lax.dot_general(p.astype(MXU_DTYPE), dah, _TN, preferred_element_type=F32)
            dq_ref[0, pl.ds(qs, ATT_BLOCK), :] = dq_acc
            dk_ref[0, pl.ds(ks, nk), :] += dk_acc
            dv_ref[0, pl.ds(ks, nk), :] += dv_acc

        block(0, 0, True)
        if nb > 1:
            def loop(n, carry):
                block(pl.multiple_of(n * ATT_BLOCK, ATT_BLOCK), pl.multiple_of((n - 1) * ATT_BLOCK, ATT_BLOCK), False)
                return carry

            lax.fori_loop(1, nb, loop, 0)

    qv = q.reshape(n_samples, length, d * Q_W)
    kv = k.reshape(n_samples, length, d * Q_W)
    vv = v.reshape(n_samples, length, d * Q_W)
    nat = lambda a: a.reshape(n_samples, length, d * GROUP_W)
    in_spec = pl.BlockSpec((1, length, GROUP_W), lambda b, r: (b, 0, 3 * r + group))
    nat_spec = pl.BlockSpec((1, length, GROUP_W), lambda b, r: (b, 0, r))
    shp = jax.ShapeDtypeStruct((n_samples, length, d * GROUP_W), F32)
    dq, dk, dv = pl.pallas_call(
        body, name=f"attn_bwd_g{group}", grid=(n_samples, d), in_specs=[in_spec] * 3 + [nat_spec] * 3,
        out_specs=[nat_spec] * 3, out_shape=[shp, shp, shp],
        compiler_params=pltpu.CompilerParams(dimension_semantics=("parallel", "parallel"), vmem_limit_bytes=VMEM_MID),
    )(qv, kv, vv, nat(dattn), nat(lse_tot), nat(rowdot))
    t = n_samples * SEQ
    return dq.reshape(t, GROUP_W), dk.reshape(t, GROUP_W), dv.reshape(t, GROUP_W)


def _disc(lr, li, ldt, br, bi):
    dt = jnp.exp(ldt)
    mag = jnp.exp(lr * dt)
    ab_re, ab_im = mag * jnp.cos(li * dt), mag * jnp.sin(li * dt)
    den = lr * lr + li * li
    nr, ni = ab_re - 1.0, ab_im
    f_re = (nr * lr + ni * li) / den
    f_im = (ni * lr - nr * li) / den
    return ab_re, ab_im, f_re * br - f_im * bi, f_re * bi + f_im * br


def _ssm_disc(lr, li, ldt, br, bi):
    def body(lr_ref, li_ref, ldt_ref, br_ref, bi_ref, ar_ref, ai_ref, bbr_ref, bbi_ref):
        ar, ai, bbr, bbi = _disc(lr_ref[...], li_ref[...], ldt_ref[...], br_ref[...], bi_ref[...])
        ar_ref[...] = ar
        ai_ref[...] = ai
        bbr_ref[...] = bbr
        bbi_ref[...] = bbi

    v1 = jax.ShapeDtypeStruct((1, N_STATE), F32)
    v16 = jax.ShapeDtypeStruct((SSM_CH, N_STATE), F32)
    return pl.pallas_call(body, name="ssm_disc", out_shape=[v1, v1, v16, v16])(lr, li, ldt, br, bi)


def _group_indicator():
    s = jnp.arange(N_STATE) // SSM_STATE
    return (s[:, None] == jnp.arange(SSM_GROUPS)[None, :]).astype(F32)


def _blockdiag_mask():
    r = jnp.arange(SSM_W) // SSM_CH
    s = jnp.arange(N_STATE) // SSM_STATE
    return (r[:, None] == s[None, :]).astype(F32)


def _ssm_param_bwd(lr, li, ldt, br, bi, d_ar, d_ai, dbb_re_full, dbb_im_full, dc_re_full, dc_im_full):
    def body(lr_ref, li_ref, ldt_ref, br_ref, bi_ref, dar_ref, dai_ref, dbr_f, dbi_f, dcr_f, dci_f, mask_ref, ind_ref,
             glr_ref, gli_ref, gldt_ref, gbr_ref, gbi_ref, gcr_ref, gci_ref):
        mask = mask_ref[...]

        def diag(ref):
            return jnp.sum((ref[...] * mask).reshape(SSM_GROUPS, SSM_CH, N_STATE), axis=0)

        dbbr, dbbi = diag(dbr_f), diag(dbi_f)
        gcr_ref[...] = diag(dcr_f)
        gci_ref[...] = diag(dci_f)
        _, vjp = jax.vjp(_disc, lr_ref[...], li_ref[...], ldt_ref[...], br_ref[...], bi_ref[...])
        glr, gli, gldt, gbr, gbi = vjp((dar_ref[...], dai_ref[...], dbbr, dbbi))
        glr_ref[...] = glr
        gli_ref[...] = gli
        gldt8 = jnp.broadcast_to(gldt, (8, N_STATE))
        gldt_ref[...] = jnp.dot(gldt8, ind_ref[...], preferred_element_type=F32, precision=lax.Precision.HIGHEST)
        gbr_ref[...] = gbr
        gbi_ref[...] = gbi

    v1 = jax.ShapeDtypeStruct((1, N_STATE), F32)
    v16 = jax.ShapeDtypeStruct((SSM_CH, N_STATE), F32)
    vdt = jax.ShapeDtypeStruct((8, SSM_GROUPS), F32)
    return pl.pallas_call(
        body, name="ssm_param_bwd", out_shape=[v1, v1, vdt, v16, v16, v16, v16],
        compiler_params=pltpu.CompilerParams(vmem_limit_bytes=VMEM_BIG),
    )(lr, li, ldt, br, bi, d_ar, d_ai, dbb_re_full, dbb_im_full, dc_re_full, dc_im_full, _blockdiag_mask(),
      _group_indicator())


def _to_state_layout(re, im):
    r = re.shape[0]
    return jnp.stack([re.reshape(r, SCAN_NBLK, SCAN_WC), im.reshape(r, SCAN_NBLK, SCAN_WC)], axis=2).reshape(r, 2 * N_STATE)


def _from_state_layout(cat):
    r = cat.shape[0]
    c4 = cat.reshape(r, SCAN_NBLK, 2, SCAN_WC)
    return c4[:, :, 0].reshape(r, N_STATE), c4[:, :, 1].reshape(r, N_STATE)


def _cmul(ar, ai, br, bi):
    return ar * br - ai * bi, ar * bi + ai * br


def _scan_fwd(bu, a_cat, n_rows):
    w = SCAN_WC

    def body(bu_ref, a_ref, xs_ref, ein_ref):
        ar = jnp.broadcast_to(a_ref[:, :w], (n_rows, w))
        ai = jnp.broadcast_to(a_ref[:, w:], (n_rows, w))
        zero = jnp.zeros((n_rows, w), F32)

        def step(i, carry, store):
            xr, xi = carry
            blk = bu_ref[i]
            nr = ar * xr - ai * xi + blk[:, :w]
            ni = ar * xi + ai * xr + blk[:, w:]
            if store:
                xs_ref[i, :, :w] = nr
                xs_ref[i, :, w:] = ni
            return nr, ni

        er, ei = lax.fori_loop(0, SCAN_LEN, functools.partial(step, store=False), (zero, zero))
        qr, qi = ar, ai
        for _ in range(int(math.log2(SCAN_LEN))):
            qr, qi = _cmul(qr, qi, qr, qi)
        seg = lax.broadcasted_iota(jnp.int32, (n_rows, w), 0) % SCAN_SEG_PER_SAMPLE
        shift = 1
        while shift < SCAN_SEG_PER_SAMPLE:
            keep = seg >= shift
            sr = jnp.where(keep, pltpu.roll(er, shift, 0), 0.0)
            si = jnp.where(keep, pltpu.roll(ei, shift, 0), 0.0)
            pr, pi = _cmul(qr, qi, sr, si)
            er, ei = er + pr, ei + pi
            qr, qi = _cmul(qr, qi, qr, qi)
            shift *= 2
        cr = jnp.where(seg >= 1, pltpu.roll(er, 1, 0), 0.0)
        ci = jnp.where(seg >= 1, pltpu.roll(ei, 1, 0), 0.0)
        ein_ref[:, :w] = cr
        ein_ref[:, w:] = ci
        lax.fori_loop(0, SCAN_LEN, functools.partial(step, store=True), (cr, ci))

    blk3 = pl.BlockSpec((SCAN_LEN, n_rows, 2 * w), lambda c: (0, 0, c))
    return pl.pallas_call(
        body, name="ssm_scan_fwd", grid=(SCAN_NBLK,),
        in_specs=[blk3, pl.BlockSpec((1, 2 * w), lambda c: (0, c))],
        out_specs=[blk3, pl.BlockSpec((n_rows, 2 * w), lambda c: (0, c))],
        out_shape=[jax.ShapeDtypeStruct(bu.shape, F32), jax.ShapeDtypeStruct((n_rows, 2 * N_STATE), F32)],
        compiler_params=pltpu.CompilerParams(dimension_semantics=("parallel",), vmem_limit_bytes=VMEM_BIG),
    )(bu, a_cat)


def _scan_bwd(dxs, xs, ein, a_cat, n_rows):
    w = SCAN_WC

    def body(dx_ref, xs_ref, ein_ref, a_ref, g_ref, da_ref):
        ar = jnp.broadcast_to(a_ref[:, :w], (n_rows, w))
        ai = jnp.broadcast_to(a_ref[:, w:], (n_rows, w))
        zero = jnp.zeros((n_rows, w), F32)

        def back(gr, gi, blk):
            return blk[:, :w] + ar * gr + ai * gi, blk[:, w:] + ar * gi - ai * gr

        def step1(ii, carry):
            return back(carry[0], carry[1], dx_ref[SCAN_LEN - 1 - ii])

        sr, si = lax.fori_loop(0, SCAN_LEN, step1, (zero, zero))
        qr, qi = ar, ai
        for _ in range(int(math.log2(SCAN_LEN))):
            qr, qi = _cmul(qr, qi, qr, qi)
        seg = lax.broadcasted_iota(jnp.int32, (n_rows, w), 0) % SCAN_SEG_PER_SAMPLE
        shift = 1
        while shift < SCAN_SEG_PER_SAMPLE:
            keep = seg < SCAN_SEG_PER_SAMPLE - shift
            tr = jnp.where(keep, pltpu.roll(sr, n_rows - shift, 0), 0.0)
            ti = jnp.where(keep, pltpu.roll(si, n_rows - shift, 0), 0.0)
            sr, si = sr + qr * tr + qi * ti, si + qr * ti - qi * tr
            qr, qi = _cmul(qr, qi, qr, qi)
            shift *= 2
        last = seg < SCAN_SEG_PER_SAMPLE - 1
        gr0 = jnp.where(last, pltpu.roll(sr, n_rows - 1, 0), 0.0)
        gi0 = jnp.where(last, pltpu.roll(si, n_rows - 1, 0), 0.0)

        def accum(gr, gi, xpr, xpi, dar, dai):
            return dar + gr * xpr + gi * xpi, dai + gi * xpr - gr * xpi

        def step2(ii, carry):
            gr, gi, dar, dai = carry
            i = SCAN_LEN - 1 - ii
            gr, gi = back(gr, gi, dx_ref[i])
            g_ref[i, :, :w] = gr
            g_ref[i, :, w:] = gi
            xp = xs_ref[i - 1]
            dar, dai = accum(gr, gi, xp[:, :w], xp[:, w:], dar, dai)
            return gr, gi, dar, dai

        gr, gi, dar, dai = lax.fori_loop(0, SCAN_LEN - 1, step2, (gr0, gi0, zero, zero))
        gr, gi = back(gr, gi, dx_ref[0])
        g_ref[0, :, :w] = gr
        g_ref[0, :, w:] = gi
        dar, dai = accum(gr, gi, ein_ref[:, :w], ein_ref[:, w:], dar, dai)
        da_ref[:, :w] = jnp.sum(dar, axis=0, keepdims=True)
        da_ref[:, w:] = jnp.sum(dai, axis=0, keepdims=True)

    blk3 = pl.BlockSpec((SCAN_LEN, n_rows, 2 * w), lambda c: (0, 0, c))
    row = pl.BlockSpec((1, 2 * w), lambda c: (0, c))
    return pl.pallas_call(
        body, name="ssm_scan_bwd", grid=(SCAN_NBLK,),
        in_specs=[blk3, blk3, pl.BlockSpec((n_rows, 2 * w), lambda c: (0, c)), row],
        out_specs=[blk3, row],
        out_shape=[jax.ShapeDtypeStruct(dxs.shape, F32), jax.ShapeDtypeStruct((1, 2 * N_STATE), F32)],
        compiler_params=pltpu.CompilerParams(dimension_semantics=("parallel",), vmem_limit_bytes=VMEM_BIG),
    )(dxs, xs, ein, a_cat)


def _to_scan_rows(a, n_samples):
    c = a.shape[1]
    return a.reshape(n_samples, SCAN_SEG_PER_SAMPLE, SCAN_LEN, c).transpose(2, 0, 1, 3).reshape(-1, c)


def _from_scan_rows(a, n_samples):
    c = a.shape[1]
    return a.reshape(SCAN_LEN, n_samples, SCAN_SEG_PER_SAMPLE, c).transpose(1, 2, 0, 3).reshape(-1, c)


def _local_step(x, target, w, small):
    t = x.shape[0]
    n_samples = t // SEQ
    n_rows = n_samples * SCAN_SEG_PER_SAMPLE
    tabs = _rope_tables()
    g_mix = small["norm_mix_g"].reshape(1, D_MODEL)
    g_ffn = small["norm_ffn_g"].reshape(1, D_MODEL)
    g_fin = small["norm_final_g"].reshape(1, D_MODEL)

    lr = small["ssm_a_re"].reshape(1, N_STATE)
    li = small["ssm_a_im"].reshape(1, N_STATE)
    ldt = jnp.repeat(small["ssm_log_dt"].reshape(SSM_GROUPS), SSM_STATE).reshape(1, N_STATE)
    br = small["ssm_b_re"].reshape(SSM_GROUPS, SSM_STATE, SSM_CH).transpose(2, 0, 1).reshape(SSM_CH, N_STATE)
    bi = small["ssm_b_im"].reshape(SSM_GROUPS, SSM_STATE, SSM_CH).transpose(2, 0, 1).reshape(SSM_CH, N_STATE)
    cr = small["ssm_c_re"].reshape(SSM_GROUPS, SSM_CH, SSM_STATE).transpose(1, 0, 2).reshape(SSM_CH, N_STATE)
    ci = small["ssm_c_im"].reshape(SSM_GROUPS, SSM_CH, SSM_STATE).transpose(1, 0, 2).reshape(SSM_CH, N_STATE)
    dskip = small["ssm_d"].reshape(1, SSM_W)
    a_re, a_im, bb_re, bb_im = _ssm_disc(lr, li, ldt, br, bi)
    bd_mask = _blockdiag_mask()
    full = lambda comp: jnp.tile(comp, (SSM_GROUPS, 1)) * bd_mask
    bb_cat = _to_state_layout(full(bb_re), full(bb_im)).astype(MXU_DTYPE)
    c_cat = _to_state_layout(full(cr), -full(ci)).astype(MXU_DTYPE)
    a_cat = _to_state_layout(a_re, a_im)

    h0 = _rms_fwd(x, g_mix, "rms_mix")
    proj = _mm(h0, w["w_in"], "nn", "mm_proj", 512, IN_W // 2)
    q, k, v, u, gates = _rope_split(proj, tabs)
    os_, lses = [], []
    for g in range(3):
        o_g, l_g = _attn_fwd(q, k, v, g, n_samples)
        os_.append(o_g)
        lses.append(l_g)
    attn, lse_tot = _attn_merge(os_, lses)
    attn_d = _mm(attn, w["w_attn_out"], "nn", "mm_attn_out", 512, D_MODEL)

    u_perm = _to_scan_rows(u, n_samples)
    bu = _mm(u_perm, bb_cat, "nn", "mm_bu", 512, N_STATE)
    xs3, ein = _scan_fwd(bu.reshape(SCAN_LEN, n_rows, 2 * N_STATE), a_cat, n_rows)
    xs = xs3.reshape(t, 2 * N_STATE)
    ypre = _mm(xs, c_cat, "nt", "mm_ssm_y", 256, SSM_W)
    ytot, yg_perm = _ssm_act(ypre, u_perm, dskip)
    yg = _from_scan_rows(yg_perm, n_samples)
    z = _mm(yg, w["w_glu"], "nn", "mm_glu", 512, 2 * D_MODEL)

    merged = _mix(attn_d, z, gates)
    x1 = _mm(merged, w["w_out"], "nn", "mm_out", 512, D_MODEL, add=x)
    h2 = _rms_fwd(x1, g_ffn, "rms_ffn")
    w_gu = jnp.concatenate([w["w_ffn_gate"], w["w_ffn_up"]], axis=1)
    ab = _mm(h2, w_gu, "nn", "mm_ffn_in", 512, D_FF)
    f = _swiglu(ab)
    x2 = _mm(f, w["w_ffn_down"], "nn", "mm_ffn_down", 512, D_MODEL, add=x1)
    dx2, dx2b, loss_blk, g_gfin = _final(x2, target, g_fin)

    df = _mm(dx2b, w["w_ffn_down"], "nt", "mm_d_f", 512, D_FF)
    dab = _swiglu_bwd(df, ab)
    g_wdown = _mm(f, dx2b, "tn", "mm_g_down", 256, 512)
    g_wgu = _mm(h2, dab, "tn", "mm_g_gu", 512, 512)
    dh2 = _mm(dab, w_gu, "nt", "mm_d_h2", 512, 512)
    dx1, dx1b, g_gffn = _rms_bwd(x1, g_ffn, dh2, dx2, "rms_ffn_bwd")

    dmerged = _mm(dx1b, w["w_out"], "nt", "mm_d_merged", 512, D_MODEL)
    g_wout = _mm(merged, dx1b, "tn", "mm_g_out", 512, 512)
    dattn_d, dz, dgpre = _mix_bwd(dmerged, gates, attn_d, z)

    dattn = _mm(dattn_d, w["w_attn_out"], "nt", "mm_d_attn", 512, GROUP_W)
    g_wao = _mm(attn, dattn_d, "tn", "mm_g_attn_out", GROUP_W, 512)
    rowdot = _attn_rowdot(dattn, attn)
    dqs, dks, dvs = [], [], []
    for g in range(3):
        dq_g, dk_g, dv_g = _attn_bwd(q, k, v, dattn, lse_tot, rowdot, g, n_samples)
        dqs.append(dq_g)
        dks.append(dk_g)
        dvs.append(dv_g)

    dyg = _mm(dz, w["w_glu"], "nt", "mm_d_yg", 512, SSM_W)
    g_wglu = _mm(yg, dz, "tn", "mm_g_glu", 512, 512)
    dyg_perm = _to_scan_rows(dyg, n_samples)
    dypre, du_skip, g_dskip = _ssm_act_bwd(dyg_perm, ytot, u_perm, dskip)
    dxs = _mm(dypre, c_cat, "nn", "mm_d_xs", 512, N_STATE)
    gs3, da_cat = _scan_bwd(dxs.reshape(SCAN_LEN, n_rows, 2 * N_STATE), xs3, ein, a_cat, n_rows)
    gs = gs3.reshape(t, 2 * N_STATE)
    du_perm = _mm(gs, bb_cat, "nt", "mm_d_u", 256, SSM_W, add=du_skip)
    du = _from_scan_rows(du_perm, n_samples)
    dbb_full = _mm(u_perm, gs, "tn", "mm_g_bb", 256, 256)
    dc_full = _mm(dypre, xs, "tn", "mm_g_c", 256, 256)
    dbb_re_f, dbb_im_f = _from_state_layout(dbb_full)
    dc_re_f, dc_im_f = _from_state_layout(dc_full)
    d_ar, d_ai = _from_state_layout(da_cat)
    g_lr, g_li, g_ldt8, g_br, g_bi, g_cr, g_cim_neg = _ssm_param_bwd(
        lr, li, ldt, br, bi, d_ar, d_ai, dbb_re_f, dbb_im_f, dc_re_f, dc_im_f)

    dproj = _pack_dproj(dqs, dks, dvs, du, dgpre, tabs)
    g_win = _mm(h0, dproj, "tn", "mm_g_in", 512, 256)
    dh0 = _mm(dproj, w["w_in"], "nt", "mm_d_h0", 512, 512)
    grad_x, _, g_gmix = _rms_bwd(x, g_mix, dh0, dx1, "rms_mix_bwd")

    big = {
        "w_in": g_win, "w_glu": g_wglu, "w_attn_out": g_wao, "w_out": g_wout,
        "w_ffn_gate": g_wgu[:, :D_FF], "w_ffn_up": g_wgu[:, D_FF:], "w_ffn_down": g_wdown,
    }
    unflat_b = lambda a: a.reshape(SSM_CH, SSM_GROUPS, SSM_STATE).transpose(1, 2, 0)
    unflat_c = lambda a: a.reshape(SSM_CH, SSM_GROUPS, SSM_STATE).transpose(1, 0, 2)
    small_g = {
        "norm_mix_g": g_gmix, "ssm_a_re": g_lr, "ssm_a_im": g_li, "ssm_log_dt": g_ldt8[0:1],
        "ssm_b_re": unflat_b(g_br), "ssm_b_im": unflat_b(g_bi), "ssm_c_re": unflat_c(g_cr),
        "ssm_c_im": unflat_c(-g_cim_neg), "ssm_d": g_dskip, "norm_ffn_g": g_gffn, "norm_final_g": g_gfin,
    }
    return loss_blk, grad_x, big, small_g


_MESH = pl.DeviceIdType.MESH


def _all_gather(block, name):
    rows, lanes = block.shape

    def body(x_ref, out_ref, send_sems, recv_sems, local_sem):
        x, y, c = lax.axis_index("x"), lax.axis_index("y"), lax.axis_index("c")
        me, sibling = (x, y, c), (x, y, 1 - c)
        chips = [(1 - x, y), (x, 1 - y), (1 - x, 1 - y)]

        def slot(px, py, pc):
            return out_ref.at[4 * px + 2 * py + pc]

        def copy(k, blk, to, src=None):
            return pltpu.make_async_remote_copy(
                src_ref=slot(*blk) if src is None else src, dst_ref=slot(*blk), send_sem=send_sems.at[k],
                recv_sem=recv_sems.at[k], device_id=to, device_id_type=_MESH)

        mine = pltpu.make_async_copy(x_ref, slot(*me), local_sem)
        mine.start()
        first = [copy(0, me, sibling, src=x_ref)]
        first += [copy(1 + j, me, (*chip, c), src=x_ref) for j, chip in enumerate(chips)]
        for cp in first:
            cp.start()
        passed = [copy(4 + j, (*chip, c), sibling) for j, chip in enumerate(chips)]
        for j, chip in enumerate(chips):
            copy(1 + j, (*chip, c), me).wait_recv()
            passed[j].start()
        copy(0, sibling, me).wait_recv()
        for j, chip in enumerate(chips):
            copy(4 + j, (*chip, 1 - c), me).wait_recv()
        for cp in first + passed:
            cp.wait_send()
        mine.wait()

    return pl.pallas_call(
        body, name=name, out_shape=jax.ShapeDtypeStruct((N_DEV, rows, lanes), block.dtype),
        in_specs=[pl.BlockSpec(memory_space=pl.ANY)], out_specs=pl.BlockSpec(memory_space=pl.ANY),
        scratch_shapes=[pltpu.SemaphoreType.DMA((7,)), pltpu.SemaphoreType.DMA((7,)), pltpu.SemaphoreType.DMA],
    )(block)


def _all_to_all(send, name):
    def body(s_ref, r_ref, send_sems, recv_sems, local_sem):
        x, y, c = lax.axis_index("x"), lax.axis_index("y"), lax.axis_index("c")
        my = 4 * x + 2 * y + c
        local = pltpu.make_async_copy(s_ref.at[my], r_ref.at[my], local_sem)
        local.start()
        copies = []
        for kk in range(1, N_DEV):
            px = 1 - x if kk & 4 else x
            py = 1 - y if kk & 2 else y
            pc = 1 - c if kk & 1 else c
            peer = 4 * px + 2 * py + pc
            cp = pltpu.make_async_remote_copy(
                src_ref=s_ref.at[peer], dst_ref=r_ref.at[my], send_sem=send_sems.at[kk - 1],
                recv_sem=recv_sems.at[kk - 1], device_id=(px, py, pc), device_id_type=_MESH)
            cp.start()
            copies.append(cp)
        for cp in copies:
            cp.wait()
        local.wait()

    return pl.pallas_call(
        body, name=name, out_shape=jax.ShapeDtypeStruct(send.shape, send.dtype),
        in_specs=[pl.BlockSpec(memory_space=pl.ANY)], out_specs=pl.BlockSpec(memory_space=pl.ANY),
        scratch_shapes=[pltpu.SemaphoreType.DMA((7,)), pltpu.SemaphoreType.DMA((7,)), pltpu.SemaphoreType.DMA],
    )(send)


def _adam(partials, w, m, v, name, tm):
    def body(p_ref, w_ref, m_ref, v_ref, g_ref, d_ref, nm_ref, nv_ref):
        g = p_ref[0].astype(F32)
        for s in range(1, N_DEV):
            g = g + p_ref[s].astype(F32)
        m_new = ADAM_B1 * m_ref[...] + (1.0 - ADAM_B1) * g
        v_new = ADAM_B2 * v_ref[...] + (1.0 - ADAM_B2) * jnp.square(g)
        m_hat = m_new / (1.0 - ADAM_B1 ** ADAM_STEP)
        v_hat = v_new / (1.0 - ADAM_B2 ** ADAM_STEP)
        g_ref[...] = g
        d_ref[...] = -ADAM_LR * (m_hat / (jnp.sqrt(v_hat) + ADAM_EPS) + ADAM_WD * w_ref[...])
        nm_ref[...] = m_new
        nv_ref[...] = v_new

    rows = w.shape[0]
    assert rows % tm == 0
    row = pl.BlockSpec((tm, LANES), lambda i: (i, 0))
    shp = jax.ShapeDtypeStruct((rows, LANES), F32)
    return pl.pallas_call(
        body, name=name, grid=(rows // tm,),
        in_specs=[pl.BlockSpec((N_DEV, tm, LANES), lambda i: (0, i, 0)), row, row, row],
        out_specs=[row] * 4, out_shape=[shp] * 4,
        compiler_params=pltpu.CompilerParams(dimension_semantics=("parallel",), vmem_limit_bytes=VMEM_MID),
    )(partials, w, m, v)


def _pad_rows(a2, mult):
    pad = (-a2.shape[0]) % mult
    return jnp.pad(a2, ((0, pad), (0, 0))) if pad else a2


def _pack_small(arrs):
    parts = []
    for a in arrs:
        flat = a.reshape(-1)
        flat = jnp.pad(flat, (0, (-flat.shape[0]) % LANES))
        parts.append(_pad_rows(flat.reshape(-1, LANES), 8))
    return jnp.concatenate(parts, axis=0)


def _unpack_small(packed, like):
    out, off = [], 0
    for a in like:
        n = math.prod(a.shape)
        rows = -(-n // LANES)
        out.append(packed[off:off + rows].reshape(-1)[:n].reshape(a.shape))
        off += -(-rows // 8) * 8
    return out


def kernel(x, norm_mix_g, w_in, ssm_a_re, ssm_a_im, ssm_log_dt, ssm_b_re, ssm_b_im, ssm_c_re, ssm_c_im, ssm_d, w_glu, w_attn_out, w_out, norm_ffn_g, w_ffn_gate, w_ffn_up, w_ffn_down, norm_final_g, loss_target, m_norm_mix_g, m_w_in, m_ssm_a_re, m_ssm_a_im, m_ssm_log_dt, m_ssm_b_re, m_ssm_b_im, m_ssm_c_re, m_ssm_c_im, m_ssm_d, m_w_glu, m_w_attn_out, m_w_out, m_norm_ffn_g, m_w_ffn_gate, m_w_ffn_up, m_w_ffn_down, m_norm_final_g, v_norm_mix_g, v_w_in, v_ssm_a_re, v_ssm_a_im, v_ssm_log_dt, v_ssm_b_re, v_ssm_b_im, v_ssm_c_re, v_ssm_c_im, v_ssm_d, v_w_glu, v_w_attn_out, v_w_out, v_norm_ffn_g, v_w_ffn_gate, v_w_ffn_up, v_w_ffn_down, v_norm_final_g):
    args = dict(locals())
    wts = {n: args[n] for n in ALL_WEIGHTS}
    moms = {n: args["m_" + n] for n in ALL_WEIGHTS}
    vars_ = {n: args["v_" + n] for n in ALL_WEIGHTS}
    n_samples = x.shape[0]
    t = n_samples * SEQ

    shard2d = {n: wts[n].reshape(wts[n].shape[1:]) for n in BIG_WEIGHTS}
    pack_rows = {n: shard2d[n].size // LANES for n in BIG_WEIGHTS}
    wpack = jnp.concatenate([shard2d[n].reshape(-1, LANES) for n in BIG_WEIGHTS], axis=0)
    gathered = _all_gather(wpack.astype(BF16), "allgather_weights")
    full_w, off = {}, 0
    for n in BIG_WEIGHTS:
        r, c = shard2d[n].shape
        part = gathered[:, off:off + pack_rows[n]].reshape(N_DEV, r, c)
        full_w[n] = part.reshape(N_DEV * r, c) if n in ROW_SHARDED else part.transpose(1, 0, 2).reshape(r, N_DEV * c)
        off += pack_rows[n]

    small = {n: wts[n] for n in SMALL_WEIGHTS}
    loss_blk, grad_x, big_g, small_g = _local_step(x.reshape(t, D_MODEL), loss_target.reshape(t, D_MODEL), full_w, small)

    sends = []
    for n in BIG_WEIGHTS:
        r, c = shard2d[n].shape
        g = big_g[n]
        per_owner = g.reshape(N_DEV, r, c) if n in ROW_SHARDED else g.reshape(r, N_DEV, c).transpose(1, 0, 2)
        sends.append(per_owner.reshape(N_DEV, pack_rows[n], LANES))
    send = jnp.concatenate(sends, axis=1).astype(BF16)
    recv = _all_to_all(send, "alltoall_grads")
    mpack = jnp.concatenate([moms[n].reshape(-1, LANES) for n in BIG_WEIGHTS], axis=0)
    vpack = jnp.concatenate([vars_[n].reshape(-1, LANES) for n in BIG_WEIGHTS], axis=0)
    big_out = _adam(recv, wpack, mpack, vpack, "adam_big", wpack.shape[0] // 8)

    small_like = [wts[n] for n in SMALL_WEIGHTS]
    spack = jnp.concatenate([_pack_small([small_g[n] for n in SMALL_WEIGHTS]), loss_blk], axis=0)
    sgath = _all_gather(spack, "allgather_small_grads")
    zero_blk = jnp.zeros((8, LANES), F32)
    swpack = jnp.concatenate([_pack_small(small_like), zero_blk], axis=0)
    smpack = jnp.concatenate([_pack_small([moms[n] for n in SMALL_WEIGHTS]), zero_blk], axis=0)
    svpack = jnp.concatenate([_pack_small([vars_[n] for n in SMALL_WEIGHTS]), zero_blk], axis=0)
    small_out = _adam(sgath, swpack, smpack, svpack, "adam_small", spack.shape[0])

    results = {}
    for kind, bpk, spk in zip(("grad", "delta", "new_m", "new_v"), big_out, small_out):
        off = 0
        for n in BIG_WEIGHTS:
            results[kind, n] = bpk[off:off + pack_rows[n]].reshape(wts[n].shape)
            off += pack_rows[n]
        for n, a in zip(SMALL_WEIGHTS, _unpack_small(spk, small_like)):
            results[kind, n] = a
    loss = small_out[0][spack.shape[0] - 8, 0]
    outs = [loss, grad_x.reshape(x.shape)]
    for kind in ("grad", "delta", "new_m", "new_v"):
        outs += [results[kind, n] for n in ALL_WEIGHTS]
    return tuple(outs)
```

```python
import functools
import math

import jax
import jax.numpy as jnp
from jax import lax
from jax.experimental import pallas as pl
from jax.experimental.pallas import tpu as pltpu

F32 = jnp.float32
BF16 = jnp.bfloat16
MXU_DTYPE = jnp.bfloat16

N_DEV = 8
D_MODEL = 1024
SEQ = 2048
HEAD_DIM = 64
HEADS_PER_GROUP = 4
GROUP_W = HEADS_PER_GROUP * HEAD_DIM
DILATIONS = (1, 4, 16)
QKV_W = 3 * len(DILATIONS) * GROUP_W
Q_W = len(DILATIONS) * GROUP_W
ATT_BLOCK = 128
ROPE_DIM = 16
ROPE_THETA = 500000.0
SSM_W = 512
SSM_GROUPS = 32
SSM_CH = 16
SSM_STATE = 64
N_STATE = SSM_GROUPS * SSM_STATE
D_FF = 2816
IN_W = QKV_W + SSM_W + 2 * D_MODEL
RMS_EPS = 1e-6
NEG_INF = -1e30
LANES = 128

SCAN_SEG_PER_SAMPLE = 8
SCAN_LEN = SEQ // SCAN_SEG_PER_SAMPLE
SCAN_WC = 128
SCAN_NBLK = N_STATE // SCAN_WC

ADAM_LR = 0.001
ADAM_B1 = 0.9
ADAM_B2 = 0.999
ADAM_EPS = 1e-08
ADAM_WD = 0.01
ADAM_STEP = 10

VMEM_BIG = 48 * 1024 * 1024
VMEM_MID = 32 * 1024 * 1024

BIG_WEIGHTS = ("w_in", "w_glu", "w_attn_out", "w_out", "w_ffn_gate", "w_ffn_up", "w_ffn_down")
ROW_SHARDED = ("w_out", "w_ffn_down")
SMALL_WEIGHTS = ("norm_mix_g", "ssm_a_re", "ssm_a_im", "ssm_log_dt", "ssm_b_re", "ssm_b_im", "ssm_c_re", "ssm_c_im",
                 "ssm_d", "norm_ffn_g", "norm_final_g")
ALL_WEIGHTS = ("norm_mix_g", "w_in", "ssm_a_re", "ssm_a_im", "ssm_log_dt", "ssm_b_re", "ssm_b_im", "ssm_c_re", "ssm_c_im",
               "ssm_d", "w_glu", "w_attn_out", "w_out", "norm_ffn_g", "w_ffn_gate", "w_ffn_up", "w_ffn_down", "norm_final_g")


def _sigmoid(x):
    return 1.0 / (1.0 + jnp.exp(-x))


class _Comm:
    def __init__(self, ins, out_shapes, n_sem, n_local, start, finish):
        self.ins, self.out_shapes, self.n_sem, self.n_local = ins, out_shapes, n_sem, n_local
        self.start, self.finish = start, finish


def _mm(a, b, mode, name, tm, tn, out_dtype=F32, add=None, vmem=VMEM_BIG, comm=None):
    if mode == "nn":
        (m, k), (_, n) = a.shape, b.shape
        a_spec = pl.BlockSpec((tm, k), lambda i, j: (i, 0))
        b_spec = pl.BlockSpec((k, tn), lambda i, j: (0, j))
        dims = (((1,), (0,)), ((), ()))
    elif mode == "nt":
        (m, k), (n, _) = a.shape, b.shape
        a_spec = pl.BlockSpec((tm, k), lambda i, j: (i, 0))
        b_spec = pl.BlockSpec((tn, k), lambda i, j: (j, 0))
        dims = (((1,), (1,)), ((), ()))
    else:
        (k, m), (_, n) = a.shape, b.shape
        a_spec = pl.BlockSpec((k, tm), lambda i, j: (0, i))
        b_spec = pl.BlockSpec((k, tn), lambda i, j: (0, j))
        dims = (((0,), (0,)), ((), ()))
    assert m % tm == 0 and n % tn == 0, (name, m, n, tm, tn)
    o_spec = pl.BlockSpec((tm, tn), lambda i, j: (i, j))
    has_add = add is not None
    n_in = 3 if has_add else 2
    n_cin = len(comm.ins) if comm else 0
    n_cout = len(comm.out_shapes) if comm else 0
    ni, nj = m // tm, n // tn

    def body(*refs):
        a_ref, b_ref = refs[0], refs[1]
        o_ref = refs[n_in + n_cin]
        if comm:
            c_args = (refs[n_in:n_in + n_cin], refs[n_in + n_cin + 1:n_in + n_cin + 1 + n_cout], *refs[-3:])

            @pl.when((pl.program_id(0) == 0) & (pl.program_id(1) == 0))
            def _():
                comm.start(*c_args)

        acc = lax.dot_general(a_ref[...].astype(MXU_DTYPE), b_ref[...].astype(MXU_DTYPE), dims,
                              preferred_element_type=F32)
        if has_add:
            acc = acc + refs[2][...]
        o_ref[...] = acc.astype(out_dtype)
        if comm:
            @pl.when((pl.program_id(0) == ni - 1) & (pl.program_id(1) == nj - 1))
            def _():
                comm.finish(*c_args)

    ins = [a, b] + ([add] if has_add else [])
    in_specs = [a_spec, b_spec] + ([o_spec] if has_add else [])
    out_shape = jax.ShapeDtypeStruct((m, n), out_dtype)
    if not comm:
        return pl.pallas_call(
            body, name=name, grid=(ni, nj), in_specs=in_specs, out_specs=o_spec, out_shape=out_shape,
            compiler_params=pltpu.CompilerParams(dimension_semantics=("parallel", "parallel"), vmem_limit_bytes=vmem),
        )(*ins)
    hbm = pl.BlockSpec(memory_space=pl.ANY)
    return pl.pallas_call(
        body, name=name, grid=(ni, nj), in_specs=in_specs + [hbm] * n_cin, out_specs=[o_spec] + [hbm] * n_cout,
        out_shape=[out_shape] + list(comm.out_shapes),
        scratch_shapes=[pltpu.SemaphoreType.DMA((comm.n_sem,)), pltpu.SemaphoreType.DMA((comm.n_sem,)),
                        pltpu.SemaphoreType.DMA((comm.n_local,))],
        compiler_params=pltpu.CompilerParams(dimension_semantics=("arbitrary", "arbitrary"), vmem_limit_bytes=vmem),
    )(*ins, *comm.ins)


def _rows(body, name, n_rows, tm, ins, outs, vmem=VMEM_MID):
    assert n_rows % tm == 0
    arrays, in_specs = [], []
    for kind, arr in ins:
        arrays.append(arr)
        if kind == "row":
            assert arr.shape[0] == n_rows, (name, arr.shape)
            in_specs.append(pl.BlockSpec((tm, arr.shape[1]), lambda i: (i, 0)))
        elif kind == "tab":
            nblk = arr.shape[0] // tm
            in_specs.append(pl.BlockSpec((tm, arr.shape[1]), lambda i, nblk=nblk: (i % nblk, 0)))
        else:
            in_specs.append(pl.BlockSpec(arr.shape, lambda i, nd=arr.ndim: (0,) * nd))
    out_specs, out_shape = [], []
    for kind, shp, dt in outs:
        if kind == "row":
            out_specs.append(pl.BlockSpec((tm, shp), lambda i: (i, 0)))
            out_shape.append(jax.ShapeDtypeStruct((n_rows, shp), dt))
        else:
            out_specs.append(pl.BlockSpec(shp, lambda i, nd=len(shp): (0,) * nd))
            out_shape.append(jax.ShapeDtypeStruct(shp, dt))
    res = pl.pallas_call(
        body, name=name, grid=(n_rows // tm,), in_specs=in_specs, out_specs=out_specs, out_shape=out_shape,
        compiler_params=pltpu.CompilerParams(dimension_semantics=("arbitrary",), vmem_limit_bytes=vmem),
    )(*arrays)
    return res


def _first_step():
    return pl.program_id(0) == 0


def _rms_fwd(x, g, name):
    def body(x_ref, g_ref, h_ref):
        xv = x_ref[...]
        r = lax.rsqrt(jnp.mean(xv * xv, axis=-1, keepdims=True) + RMS_EPS)
        h_ref[...] = ((xv * r) * g_ref[...]).astype(BF16)

    return _rows(body, name, x.shape[0], 512, [("row", x), ("const", g)], [("row", x.shape[1], BF16)])[0]


def _rms_bwd(x, g, dh, dres, name):
    def body(x_ref, g_ref, dh_ref, dres_ref, dx_ref, dxb_ref, gg_ref):
        @pl.when(_first_step())
        def _():
            gg_ref[...] = jnp.zeros_like(gg_ref)

        xv = x_ref[...]
        r = lax.rsqrt(jnp.mean(xv * xv, axis=-1, keepdims=True) + RMS_EPS)
        n = xv * r
        dh_v = dh_ref[...]
        gg_ref[...] += jnp.sum(dh_v * n, axis=0, keepdims=True)
        dn = dh_v * g_ref[...]
        dx = dres_ref[...] + r * (dn - n * jnp.mean(dn * n, axis=-1, keepdims=True))
        dx_ref[...] = dx
        dxb_ref[...] = dx.astype(BF16)

    d = x.shape[1]
    return _rows(body, name, x.shape[0], 256, [("row", x), ("const", g), ("row", dh), ("row", dres)],
                 [("row", d, F32), ("row", d, BF16), ("acc", (1, d), F32)])


def _rope_tables():
    half = ROPE_DIM // 2
    inv = jnp.power(jnp.float32(ROPE_THETA), -jnp.arange(half, dtype=F32) * 2.0 / ROPE_DIM)
    ang = jnp.arange(SEQ, dtype=F32)[:, None] * inv[None, :]
    lane = jnp.arange(LANES) % HEAD_DIM
    cosl = jnp.cos(ang)[:, lane % half]
    sinl = jnp.sin(ang)[:, lane % half]
    tab_c = jnp.where(lane < ROPE_DIM, cosl, 1.0)
    tab_lo = jnp.where(lane < half, -sinl, 0.0)
    tab_hi = jnp.where((lane >= half) & (lane < ROPE_DIM), sinl, 0.0)
    return tab_c.astype(F32), tab_lo.astype(F32), tab_hi.astype(F32)


def _rope_apply(t, tc, tlo, thi):
    half = ROPE_DIM // 2
    return t * tc + pltpu.roll(t, LANES - half, 1) * tlo + pltpu.roll(t, half, 1) * thi


def _rope_transpose(dt, tc, tlo, thi):
    half = ROPE_DIM // 2
    return dt * tc + pltpu.roll(dt * tlo, half, 1) + pltpu.roll(dt * thi, LANES - half, 1)


def _rope_split(proj, tabs):
    def body(p_ref, tc_ref, tlo_ref, thi_ref, q_ref, k_ref, v_ref, u_ref, g_ref):
        tc, tlo, thi = tc_ref[...], tlo_ref[...], thi_ref[...]
        for ch in range(Q_W // LANES):
            sl = slice(ch * LANES, (ch + 1) * LANES)
            q_ref[:, sl] = _rope_apply(p_ref[:, sl], tc, tlo, thi).astype(BF16)
            k_ref[:, sl] = _rope_apply(p_ref[:, Q_W + ch * LANES:Q_W + (ch + 1) * LANES], tc, tlo, thi).astype(BF16)
        v_ref[...] = p_ref[:, 2 * Q_W:QKV_W].astype(BF16)
        u_ref[...] = p_ref[:, QKV_W:QKV_W + SSM_W]
        g_ref[...] = _sigmoid(p_ref[:, QKV_W + SSM_W:])

    t = proj.shape[0]
    return _rows(body, "rope_split", t, 256,
                 [("row", proj), ("tab", tabs[0]), ("tab", tabs[1]), ("tab", tabs[2])],
                 [("row", Q_W, BF16), ("row", Q_W, BF16), ("row", Q_W, BF16), ("row", SSM_W, F32),
                  ("row", 2 * D_MODEL, F32)])


def _pack_dproj(dqs, dks, dvs, du, dgpre, tabs):
    def body(*refs):
        dq_refs, dk_refs, dv_refs = refs[0:3], refs[3:6], refs[6:9]
        du_ref, dg_ref, tc_ref, tlo_ref, thi_ref, o_ref = refs[9:15]
        tc, tlo, thi = tc_ref[...], tlo_ref[...], thi_ref[...]
        for g in range(3):
            for half in range(GROUP_W // LANES):
                sl = slice(half * LANES, (half + 1) * LANES)
                col = g * GROUP_W + half * LANES
                o_ref[:, col:col + LANES] = _rope_transpose(dq_refs[g][:, sl], tc, tlo, thi).astype(BF16)
                o_ref[:, Q_W + col:Q_W + col + LANES] = _rope_transpose(dk_refs[g][:, sl], tc, tlo, thi).astype(BF16)
            o_ref[:, 2 * Q_W + g * GROUP_W:2 * Q_W + (g + 1) * GROUP_W] = dv_refs[g][...].astype(BF16)
        o_ref[:, QKV_W:QKV_W + SSM_W] = du_ref[...].astype(BF16)
        o_ref[:, QKV_W + SSM_W:] = dg_ref[...].astype(BF16)

    t = du.shape[0]
    ins = [("row", a) for a in (*dqs, *dks, *dvs, du, dgpre)] + [("tab", tb) for tb in tabs]
    return _rows(body, "pack_dproj", t, 256, ins, [("row", IN_W, BF16)])[0]


def _attn_merge(os_, lses):
    def body(o0, o1, o2, l0, l1, l2, a_ref, lt_ref):
        la, lb, lc = l0[...], l1[...], l2[...]
        m = jnp.maximum(jnp.maximum(la, lb), lc)
        ea, eb, ec = jnp.exp(la - m), jnp.exp(lb - m), jnp.exp(lc - m)
        ssum = ea + eb + ec
        a_ref[...] = (ea / ssum) * o0[...] + (eb / ssum) * o1[...] + (ec / ssum) * o2[...]
        lt_ref[...] = m + jnp.log(ssum)

    t = os_[0].shape[0]
    return _rows(body, "attn_merge", t, 512, [("row", a) for a in (*os_, *lses)],
                 [("row", GROUP_W, F32), ("row", GROUP_W, F32)])


def _head_sum_matrix():
    r = jnp.arange(GROUP_W) // HEAD_DIM
    return (r[:, None] == r[None, :]).astype(F32)


def _attn_rowdot(dattn, attn):
    def body(da_ref, a_ref, ones_ref, d_ref):
        d_ref[...] = jnp.dot(da_ref[...] * a_ref[...], ones_ref[...], preferred_element_type=F32,
                             precision=lax.Precision.HIGHEST)

    t = attn.shape[0]
    return _rows(body, "attn_rowdot", t, 512, [("row", dattn), ("row", attn), ("const", _head_sum_matrix())],
                 [("row", GROUP_W, F32)])[0]


def _mix(attn_d, z, gates):
    def body(ad_ref, z_ref, g_ref, m_ref):
        za, zb = z_ref[:, :D_MODEL], z_ref[:, D_MODEL:]
        s_out = za * _sigmoid(zb)
        m_ref[...] = (g_ref[:, :D_MODEL] * ad_ref[...] + g_ref[:, D_MODEL:] * s_out).astype(BF16)

    t = attn_d.shape[0]
    return _rows(body, "mix", t, 256, [("row", attn_d), ("row", z), ("row", gates)], [("row", D_MODEL, BF16)])[0]


def _mix_bwd(dmerged, gates, attn_d, z):
    def body(dm_ref, g_ref, ad_ref, z_ref, dad_ref, dz_ref, dg_ref):
        dm = dm_ref[...]
        g0, g1 = g_ref[:, :D_MODEL], g_ref[:, D_MODEL:]
        za, zb = z_ref[:, :D_MODEL], z_ref[:, D_MODEL:]
        sb = _sigmoid(zb)
        s_out = za * sb
        dad_ref[...] = (dm * g0).astype(BF16)
        ds = dm * g1
        dz_ref[:, :D_MODEL] = (ds * sb).astype(BF16)
        dz_ref[:, D_MODEL:] = (ds * za * sb * (1.0 - sb)).astype(BF16)
        dg_ref[:, :D_MODEL] = dm * ad_ref[...] * g0 * (1.0 - g0)
        dg_ref[:, D_MODEL:] = dm * s_out * g1 * (1.0 - g1)

    t = dmerged.shape[0]
    return _rows(body, "mix_bwd", t, 256, [("row", dmerged), ("row", gates), ("row", attn_d), ("row", z)],
                 [("row", D_MODEL, BF16), ("row", 2 * D_MODEL, BF16), ("row", 2 * D_MODEL, F32)])


def _swiglu(ab):
    def body(ab_ref, f_ref):
        a, b = ab_ref[:, :D_FF], ab_ref[:, D_FF:]
        f_ref[...] = (a * _sigmoid(a) * b).astype(BF16)

    return _rows(body, "swiglu", ab.shape[0], 256, [("row", ab)], [("row", D_FF, BF16)])[0]


def _swiglu_bwd(df, ab):
    def body(df_ref, ab_ref, o_ref):
        a, b = ab_ref[:, :D_FF], ab_ref[:, D_FF:]
        d = df_ref[...]
        sg = _sigmoid(a)
        o_ref[:, :D_FF] = (d * b * sg * (1.0 + a * (1.0 - sg))).astype(BF16)
        o_ref[:, D_FF:] = (d * a * sg).astype(BF16)

    return _rows(body, "swiglu_bwd", ab.shape[0], 256, [("row", df), ("row", ab)], [("row", 2 * D_FF, BF16)])[0]


def _final(x2, target, g):
    def body(x_ref, t_ref, g_ref, dx_ref, dxb_ref, loss_ref, gg_ref):
        @pl.when(_first_step())
        def _():
            loss_ref[...] = jnp.zeros_like(loss_ref)
            gg_ref[...] = jnp.zeros_like(gg_ref)

        xv = x_ref[...]
        gv = g_ref[...]
        r = lax.rsqrt(jnp.mean(xv * xv, axis=-1, keepdims=True) + RMS_EPS)
        n = xv * r
        diff = n * gv - t_ref[...]
        per_tok = jnp.mean(diff * diff, axis=-1, keepdims=True)
        loss_ref[...] += 0.5 * jnp.sum(per_tok, axis=0, keepdims=True)
        dy = diff / xv.shape[-1]
        gg_ref[...] += jnp.sum(dy * n, axis=0, keepdims=True)
        dn = dy * gv
        dx = r * (dn - n * jnp.mean(dn * n, axis=-1, keepdims=True))
        dx_ref[...] = dx
        dxb_ref[...] = dx.astype(BF16)

    d = x2.shape[1]
    return _rows(body, "final_loss", x2.shape[0], 256, [("row", x2), ("row", target), ("const", g)],
                 [("row", d, F32), ("row", d, BF16), ("acc", (8, LANES), F32), ("acc", (1, d), F32)])


_GELU_C = math.sqrt(2.0 / math.pi)


def _ssm_act(ypre, u_perm, dskip):
    def body(y_ref, u_ref, d_ref, yt_ref, yg_ref):
        yt = y_ref[...] + d_ref[...] * u_ref[...]
        yt_ref[...] = yt
        th = jnp.tanh(_GELU_C * (yt + 0.044715 * (yt * yt * yt)))
        yg_ref[...] = (0.5 * yt * (1.0 + th)).astype(BF16)

    t = ypre.shape[0]
    return _rows(body, "ssm_act", t, 512, [("row", ypre), ("row", u_perm), ("const", dskip)],
                 [("row", SSM_W, F32), ("row", SSM_W, BF16)])


def _ssm_act_bwd(dyg, ytot, u_perm, dskip):
    def body(dyg_ref, yt_ref, u_ref, d_ref, dy_ref, dus_ref, dd_ref):
        @pl.when(_first_step())
        def _():
            dd_ref[...] = jnp.zeros_like(dd_ref)

        yt = yt_ref[...]
        th = jnp.tanh(_GELU_C * (yt + 0.044715 * (yt * yt * yt)))
        dgelu = 0.5 * (1.0 + th) + 0.5 * yt * (1.0 - th * th) * _GELU_C * (1.0 + 3.0 * 0.044715 * yt * yt)
        dy = dyg_ref[...] * dgelu
        dy_ref[...] = dy.astype(BF16)
        dus_ref[...] = dy * d_ref[...]
        dd_ref[...] += jnp.sum(dy * u_ref[...], axis=0, keepdims=True)

    t = dyg.shape[0]
    return _rows(body, "ssm_act_bwd", t, 512, [("row", dyg), ("row", ytot), ("row", u_perm), ("const", dskip)],
                 [("row", SSM_W, BF16), ("row", SSM_W, F32), ("acc", (1, SSM_W), F32)])


def _head_masks():
    lane = lax.broadcasted_iota(jnp.int32, (1, GROUP_W), 1)
    return [(lane // HEAD_DIM) == h for h in range(HEADS_PER_GROUP)]


def _band_mask(first):
    nk = ATT_BLOCK if first else 2 * ATT_BLOCK
    qi = lax.broadcasted_iota(jnp.int32, (ATT_BLOCK, nk), 0)
    ki = lax.broadcasted_iota(jnp.int32, (ATT_BLOCK, nk), 1)
    dist = qi - ki + (0 if first else ATT_BLOCK)
    return (dist >= 0) & (dist <= ATT_BLOCK)


_NT = (((1,), (1,)), ((), ()))
_TN = (((0,), (0,)), ((), ()))


def _attn_fwd(q, k, v, group, n_samples):
    d = DILATIONS[group]
    length = SEQ // d
    nb = length // ATT_BLOCK

    def body(q_ref, k_ref, v_ref, o_ref, l_ref):
        masks = _head_masks()

        def block(qs, ks, first):
            nk = ATT_BLOCK if first else 2 * ATT_BLOCK
            qb = q_ref[0, pl.ds(qs, ATT_BLOCK), :]
            kc = k_ref[0, pl.ds(ks, nk), :]
            vc = v_ref[0, pl.ds(ks, nk), :]
            valid = _band_mask(first)
            o_acc = jnp.zeros((ATT_BLOCK, GROUP_W), F32)
            l_acc = jnp.zeros((ATT_BLOCK, GROUP_W), F32)
            for h in range(HEADS_PER_GROUP):
                qh = jnp.where(masks[h], qb, jnp.zeros_like(qb))
                s = lax.dot_general(qh, kc, _NT, preferred_element_type=F32) * (HEAD_DIM ** -0.5)
                s = jnp.where(valid, s, NEG_INF)
                m = jnp.max(s, axis=-1, keepdims=True)
                p = jnp.exp(s - m)
                l = jnp.sum(p, axis=-1, keepdims=True)
                pv = jnp.dot(p.astype(MXU_DTYPE), vc, preferred_element_type=F32)
                o_acc = jnp.where(masks[h], pv / l, o_acc)
                l_acc = jnp.where(masks[h], m + jnp.log(l), l_acc)
            o_ref[0, pl.ds(qs, ATT_BLOCK), :] = o_acc
            l_ref[0, pl.ds(qs, ATT_BLOCK), :] = l_acc

        block(0, 0, True)
        if nb > 1:
            def loop(n, carry):
                block(pl.multiple_of(n * ATT_BLOCK, ATT_BLOCK), pl.multiple_of((n - 1) * ATT_BLOCK, ATT_BLOCK), False)
                return carry

            lax.fori_loop(1, nb, loop, 0)

    qv = q.reshape(n_samples, length, d * Q_W)
    kv = k.reshape(n_samples, length, d * Q_W)
    vv = v.reshape(n_samples, length, d * Q_W)
    in_spec = pl.BlockSpec((1, length, GROUP_W), lambda b, r: (b, 0, 3 * r + group))
    out_spec = pl.BlockSpec((1, length, GROUP_W), lambda b, r: (b, 0, r))
    shp = jax.ShapeDtypeStruct((n_samples, length, d * GROUP_W), F32)
    o, lse = pl.pallas_call(
        body, name=f"attn_fwd_g{group}", grid=(n_samples, d), in_specs=[in_spec] * 3, out_specs=[out_spec] * 2,
        out_shape=[shp, shp],
        compiler_params=pltpu.CompilerParams(dimension_semantics=("parallel", "parallel"), vmem_limit_bytes=VMEM_MID),
    )(qv, kv, vv)
    return o.reshape(n_samples * SEQ, GROUP_W), lse.reshape(n_samples * SEQ, GROUP_W)


def _attn_bwd(q, k, v, dattn, lse_tot, rowdot, group, n_samples):
    d = DILATIONS[group]
    length = SEQ // d
    nb = length // ATT_BLOCK

    def body(q_ref, k_ref, v_ref, da_ref, lt_ref, rd_ref, dq_ref, dk_ref, dv_ref):
        masks = _head_masks()
        dk_ref[...] = jnp.zeros_like(dk_ref)
        dv_ref[...] = jnp.zeros_like(dv_ref)

        def block(qs, ks, first):
            nk = ATT_BLOCK if first else 2 * ATT_BLOCK
            qb = q_ref[0, pl.ds(qs, ATT_BLOCK), :]
            kc = k_ref[0, pl.ds(ks, nk), :]
            vc = v_ref[0, pl.ds(ks, nk), :]
            da = da_ref[0, pl.ds(qs, ATT_BLOCK), :]
            lt = lt_ref[0, pl.ds(qs, ATT_BLOCK), :]
            rd = rd_ref[0, pl.ds(qs, ATT_BLOCK), :]
            valid = _band_mask(first)
            dq_acc = jnp.zeros((ATT_BLOCK, GROUP_W), F32)
            dk_acc = jnp.zeros((nk, GROUP_W), F32)
            dv_acc = jnp.zeros((nk, GROUP_W), F32)
            for h in range(HEADS_PER_GROUP):
                qh = jnp.where(masks[h], qb, jnp.zeros_like(qb))
                dah = jnp.where(masks[h], da, 0.0).astype(MXU_DTYPE)
                lt_h = jnp.max(jnp.where(masks[h], lt, -jnp.inf), axis=-1, keepdims=True)
                rd_h = jnp.max(jnp.where(masks[h], rd, -jnp.inf), axis=-1, keepdims=True)
                s = lax.dot_general(qh, kc, _NT, preferred_element_type=F32) * (HEAD_DIM ** -0.5)
                s = jnp.where(valid, s, NEG_INF)
                p = jnp.exp(s - lt_h)
                dp = lax.dot_general(dah, vc, _NT, preferred_element_type=F32)
                ds = (p * (dp - rd_h) * (HEAD_DIM ** -0.5)).astype(MXU_DTYPE)
                dq_h = jnp.dot(ds, kc, preferred_element_type=F32)
                dq_acc = jnp.where(masks[h], dq_h, dq_acc)
                dk_acc = dk_acc + lax.dot_general(ds, qh, _TN, preferred_element_type=F32)
                dv_acc = dv_acc + lax.dot_general(p.astype(MXU_DTYPE), dah, _TN, preferred_element_type=F32)
            dq_ref[0, pl.ds(qs, ATT_BLOCK), :] = dq_acc
            dk_ref[0, pl.ds(ks, nk), :] += dk_acc
            dv_ref[0, pl.ds(ks, nk), :] += dv_acc

        block(0, 0, True)
        if nb > 1:
            def loop(n, carry):
                block(pl.multiple_of(n * ATT_BLOCK, ATT_BLOCK), pl.multiple_of((n - 1) * ATT_BLOCK, ATT_BLOCK), False)
                return carry

            lax.fori_loop(1, nb, loop, 0)

    qv = q.reshape(n_samples, length, d * Q_W)
    kv = k.reshape(n_samples, length, d * Q_W)
    vv = v.reshape(n_samples, length, d * Q_W)
    nat = lambda a: a.reshape(n_samples, length, d * GROUP_W)
    in_spec = pl.BlockSpec((1, length, GROUP_W), lambda b, r: (b, 0, 3 * r + group))
    nat_spec = pl.BlockSpec((1, length, GROUP_W), lambda b, r: (b, 0, r))
    shp = jax.ShapeDtypeStruct((n_samples, length, d * GROUP_W), F32)
    dq, dk, dv = pl.pallas_call(
        body, name=f"attn_bwd_g{group}", grid=(n_samples, d), in_specs=[in_spec] * 3 + [nat_spec] * 3,
        out_specs=[nat_spec] * 3, out_shape=[shp, shp, shp],
        compiler_params=pltpu.CompilerParams(dimension_semantics=("parallel", "parallel"), vmem_limit_bytes=VMEM_MID),
    )(qv, kv, vv, nat(dattn), nat(lse_tot), nat(rowdot))
    t = n_samples * SEQ
    return dq.reshape(t, GROUP_W), dk.reshape(t, GROUP_W), dv.reshape(t, GROUP_W)


def _disc(lr, li, ldt, br, bi):
    dt = jnp.exp(ldt)
    mag = jnp.exp(lr * dt)
    ab_re, ab_im = mag * jnp.cos(li * dt), mag * jnp.sin(li * dt)
    den = lr * lr + li * li
    nr, ni = ab_re - 1.0, ab_im
    f_re = (nr * lr + ni * li) / den
    f_im = (ni * lr - nr * li) / den
    return ab_re, ab_im, f_re * br - f_im * bi, f_re * bi + f_im * br


def _state_mask(cb):
    row_g = lax.broadcasted_iota(jnp.int32, (SSM_W, SCAN_WC), 0) // SSM_CH
    col_g = (cb * SCAN_WC + lax.broadcasted_iota(jnp.int32, (SSM_W, SCAN_WC), 1)) // SSM_STATE
    return row_g == col_g


def _ssm_disc(lr, li, ldt, br, bi, cr, ci):
    w = SCAN_WC

    def body(lr_ref, li_ref, ldt_ref, br_ref, bi_ref, cr_ref, ci_ref, a_ref, bb_ref, c_ref):
        ar, ai, bbr, bbi = _disc(lr_ref[...], li_ref[...], ldt_ref[...], br_ref[...], bi_ref[...])
        crv, civ = cr_ref[...], ci_ref[...]
        for cb in range(SCAN_NBLK):
            sl = slice(cb * w, (cb + 1) * w)
            lo, hi = slice(2 * cb * w, (2 * cb + 1) * w), slice((2 * cb + 1) * w, (2 * cb + 2) * w)
            mask = _state_mask(cb)
            dense = lambda comp: jnp.where(mask, jnp.tile(comp[:, sl], (SSM_GROUPS, 1)), 0.0)
            a_ref[:, lo] = ar[:, sl]
            a_ref[:, hi] = ai[:, sl]
            bb_ref[:, lo] = dense(bbr).astype(MXU_DTYPE)
            bb_ref[:, hi] = dense(bbi).astype(MXU_DTYPE)
            c_ref[:, lo] = dense(crv).astype(MXU_DTYPE)
            c_ref[:, hi] = (-dense(civ)).astype(MXU_DTYPE)

    return pl.pallas_call(
        body, name="ssm_disc",
        out_shape=[jax.ShapeDtypeStruct((1, 2 * N_STATE), F32), jax.ShapeDtypeStruct((SSM_W, 2 * N_STATE), MXU_DTYPE),
                   jax.ShapeDtypeStruct((SSM_W, 2 * N_STATE), MXU_DTYPE)],
        compiler_params=pltpu.CompilerParams(vmem_limit_bytes=VMEM_MID),
    )(lr, li, ldt, br, bi, cr, ci)


def _group_indicator():
    s = jnp.arange(N_STATE) // SSM_STATE
    return (s[:, None] == jnp.arange(LANES)[None, :]).astype(F32)


def _ssm_param_bwd(lr, li, ldt, br, bi, da_cat, dbb_full, dc_full):
    w = SCAN_WC

    def body(lr_ref, li_ref, ldt_ref, br_ref, bi_ref, da_ref, dbb_ref, dc_ref, ind_ref,
             glr_ref, gli_ref, gldt_ref, gbr_ref, gbi_ref, gcr_ref, gci_ref):
        def parts(ref, diag):
            res = ([], [])
            for cb in range(SCAN_NBLK):
                for part in range(2):
                    blk = ref[:, (2 * cb + part) * w:(2 * cb + part + 1) * w]
                    if diag:
                        blk = jnp.sum(jnp.where(_state_mask(cb), blk, 0.0).reshape(SSM_GROUPS, SSM_CH, w), axis=0)
                    res[part].append(blk)
            return jnp.concatenate(res[0], axis=1), jnp.concatenate(res[1], axis=1)

        dar, dai = parts(da_ref, False)
        dbbr, dbbi = parts(dbb_ref, True)
        dcr, dci_neg = parts(dc_ref, True)
        gcr_ref[...] = dcr
        gci_ref[...] = -dci_neg
        _, vjp = jax.vjp(_disc, lr_ref[...], li_ref[...], ldt_ref[...], br_ref[...], bi_ref[...])
        glr, gli, gldt, gbr, gbi = vjp((dar, dai, dbbr, dbbi))
        glr_ref[...] = glr
        gli_ref[...] = gli
        gldt_ref[...] = jnp.dot(jnp.broadcast_to(gldt, (8, N_STATE)), ind_ref[...], preferred_element_type=F32,
                                precision=lax.Precision.HIGHEST)
        gbr_ref[...] = gbr
        gbi_ref[...] = gbi

    v1 = jax.ShapeDtypeStruct((1, N_STATE), F32)
    v16 = jax.ShapeDtypeStruct((SSM_CH, N_STATE), F32)
    vdt = jax.ShapeDtypeStruct((8, LANES), F32)
    return pl.pallas_call(
        body, name="ssm_param_bwd", out_shape=[v1, v1, vdt, v16, v16, v16, v16],
        compiler_params=pltpu.CompilerParams(vmem_limit_bytes=VMEM_BIG),
    )(lr, li, ldt, br, bi, da_cat, dbb_full, dc_full, _group_indicator())


def _cmul(ar, ai, br, bi):
    return ar * br - ai * bi, ar * bi + ai * br


def _scan_fwd(bu, a_cat, n_rows):
    w = SCAN_WC

    def body(bu_ref, a_ref, xs_ref, ein_ref):
        ar = jnp.broadcast_to(a_ref[:, :w], (n_rows, w))
        ai = jnp.broadcast_to(a_ref[:, w:], (n_rows, w))
        zero = jnp.zeros((n_rows, w), F32)

        def step(i, carry, store):
            xr, xi = carry
            blk = bu_ref[i]
            nr = ar * xr - ai * xi + blk[:, :w]
            ni = ar * xi + ai * xr + blk[:, w:]
            if store:
                xs_ref[i, :, :w] = nr
                xs_ref[i, :, w:] = ni
            return nr, ni

        er, ei = lax.fori_loop(0, SCAN_LEN, functools.partial(step, store=False), (zero, zero))
        qr, qi = ar, ai
        for _ in range(int(math.log2(SCAN_LEN))):
            qr, qi = _cmul(qr, qi, qr, qi)
        seg = lax.broadcasted_iota(jnp.int32, (n_rows, w), 0) % SCAN_SEG_PER_SAMPLE
        shift = 1
        while shift < SCAN_SEG_PER_SAMPLE:
            keep = seg >= shift
            sr = jnp.where(keep, pltpu.roll(er, shift, 0), 0.0)
            si = jnp.where(keep, pltpu.roll(ei, shift, 0), 0.0)
            pr, pi = _cmul(qr, qi, sr, si)
            er, ei = er + pr, ei + pi
            qr, qi = _cmul(qr, qi, qr, qi)
            shift *= 2
        cr = jnp.where(seg >= 1, pltpu.roll(er, 1, 0), 0.0)
        ci = jnp.where(seg >= 1, pltpu.roll(ei, 1, 0), 0.0)
        ein_ref[:, :w] = cr
        ein_ref[:, w:] = ci
        lax.fori_loop(0, SCAN_LEN, functools.partial(step, store=True), (cr, ci))

    blk3 = pl.BlockSpec((SCAN_LEN, n_rows, 2 * w), lambda c: (0, 0, c))
    return pl.pallas_call(
        body, name="ssm_scan_fwd", grid=(SCAN_NBLK,),
        in_specs=[blk3, pl.BlockSpec((1, 2 * w), lambda c: (0, c))],
        out_specs=[blk3, pl.BlockSpec((n_rows, 2 * w), lambda c: (0, c))],
        out_shape=[jax.ShapeDtypeStruct(bu.shape, F32), jax.ShapeDtypeStruct((n_rows, 2 * N_STATE), F32)],
        compiler_params=pltpu.CompilerParams(dimension_semantics=("parallel",), vmem_limit_bytes=VMEM_BIG),
    )(bu, a_cat)


def _scan_bwd(dxs, xs, ein, a_cat, n_rows):
    w = SCAN_WC

    def body(dx_ref, xs_ref, ein_ref, a_ref, g_ref, da_ref):
        ar = jnp.broadcast_to(a_ref[:, :w], (n_rows, w))
        ai = jnp.broadcast_to(a_ref[:, w:], (n_rows, w))
        zero = jnp.zeros((n_rows, w), F32)

        def back(gr, gi, blk):
            return blk[:, :w] + ar * gr + ai * gi, blk[:, w:] + ar * gi - ai * gr

        def step1(ii, carry):
            return back(carry[0], carry[1], dx_ref[SCAN_LEN - 1 - ii])

        sr, si = lax.fori_loop(0, SCAN_LEN, step1, (zero, zero))
        qr, qi = ar, ai
        for _ in range(int(math.log2(SCAN_LEN))):
            qr, qi = _cmul(qr, qi, qr, qi)
        seg = lax.broadcasted_iota(jnp.int32, (n_rows, w), 0) % SCAN_SEG_PER_SAMPLE
        shift = 1
        while shift < SCAN_SEG_PER_SAMPLE:
            keep = seg < SCAN_SEG_PER_SAMPLE - shift
            tr = jnp.where(keep, pltpu.roll(sr, n_rows - shift, 0), 0.0)
            ti = jnp.where(keep, pltpu.roll(si, n_rows - shift, 0), 0.0)
            sr, si = sr + qr * tr + qi * ti, si + qr * ti - qi * tr
            qr, qi = _cmul(qr, qi, qr, qi)
            shift *= 2
        last = seg < SCAN_SEG_PER_SAMPLE - 1
        gr0 = jnp.where(last, pltpu.roll(sr, n_rows - 1, 0), 0.0)
        gi0 = jnp.where(last, pltpu.roll(si, n_rows - 1, 0), 0.0)

        def accum(gr, gi, xpr, xpi, dar, dai):
            return dar + gr * xpr + gi * xpi, dai + gi * xpr - gr * xpi

        def step2(ii, carry):
            gr, gi, dar, dai = carry
            i = SCAN_LEN - 1 - ii
            gr, gi = back(gr, gi, dx_ref[i])
            g_ref[i, :, :w] = gr
            g_ref[i, :, w:] = gi
            xp = xs_ref[i - 1]
            dar, dai = accum(gr, gi, xp[:, :w], xp[:, w:], dar, dai)
            return gr, gi, dar, dai

        gr, gi, dar, dai = lax.fori_loop(0, SCAN_LEN - 1, step2, (gr0, gi0, zero, zero))
        gr, gi = back(gr, gi, dx_ref[0])
        g_ref[0, :, :w] = gr
        g_ref[0, :, w:] = gi
        dar, dai = accum(gr, gi, ein_ref[:, :w], ein_ref[:, w:], dar, dai)
        da_ref[:, :w] = jnp.sum(dar, axis=0, keepdims=True)
        da_ref[:, w:] = jnp.sum(dai, axis=0, keepdims=True)

    blk3 = pl.BlockSpec((SCAN_LEN, n_rows, 2 * w), lambda c: (0, 0, c))
    row = pl.BlockSpec((1, 2 * w), lambda c: (0, c))
    return pl.pallas_call(
        body, name="ssm_scan_bwd", grid=(SCAN_NBLK,),
        in_specs=[blk3, blk3, pl.BlockSpec((n_rows, 2 * w), lambda c: (0, c)), row],
        out_specs=[blk3, row],
        out_shape=[jax.ShapeDtypeStruct(dxs.shape, F32), jax.ShapeDtypeStruct((1, 2 * N_STATE), F32)],
        compiler_params=pltpu.CompilerParams(dimension_semantics=("parallel",), vmem_limit_bytes=VMEM_BIG),
    )(dxs, xs, ein, a_cat)


def _to_scan_rows(a, n_samples):
    c = a.shape[1]
    return a.reshape(n_samples, SCAN_SEG_PER_SAMPLE, SCAN_LEN, c).transpose(2, 0, 1, 3).reshape(-1, c)


def _from_scan_rows(a, n_samples):
    c = a.shape[1]
    return a.reshape(SCAN_LEN, n_samples, SCAN_SEG_PER_SAMPLE, c).transpose(1, 2, 0, 3).reshape(-1, c)


def _flat_small(small):
    perm_b = lambda a: a.reshape(SSM_GROUPS, SSM_STATE, SSM_CH).transpose(2, 0, 1).reshape(SSM_CH, N_STATE)
    perm_c = lambda a: a.reshape(SSM_GROUPS, SSM_CH, SSM_STATE).transpose(1, 0, 2).reshape(SSM_CH, N_STATE)
    return dict(
        g_mix=small["norm_mix_g"].reshape(1, D_MODEL), g_ffn=small["norm_ffn_g"].reshape(1, D_MODEL),
        g_fin=small["norm_final_g"].reshape(1, D_MODEL),
        lr=small["ssm_a_re"].reshape(1, N_STATE), li=small["ssm_a_im"].reshape(1, N_STATE),
        ldt=jnp.repeat(small["ssm_log_dt"].reshape(SSM_GROUPS), SSM_STATE).reshape(1, N_STATE),
        br=perm_b(small["ssm_b_re"]), bi=perm_b(small["ssm_b_im"]),
        cr=perm_c(small["ssm_c_re"]), ci=perm_c(small["ssm_c_im"]), dskip=small["ssm_d"].reshape(1, SSM_W))


AG_HOSTS = {"mm_proj": ("w_glu", "w_attn_out", "w_out", "w_gu"), "mm_bu": ("w_ffn_down",)}
A2A_HOSTS = {"mm_d_h2": ("w_ffn_down",), "mm_g_bb": ("w_ffn_gate",), "mm_g_c": ("w_ffn_up",),
             "mm_g_in": ("w_out", "w_attn_out", "w_glu"), "mm_d_h0": ("w_in",)}
GRAD_SOURCE = {"w_ffn_gate": ("w_gu", 0), "w_ffn_up": ("w_gu", N_DEV)}


def _local_step(x, target, w, small, shards=None):
    t = x.shape[0]
    n_samples = t // SEQ
    n_rows = n_samples * SCAN_SEG_PER_SAMPLE
    tabs = _rope_tables()
    w = dict(w)
    fs = _flat_small(small)
    g_mix, g_ffn, g_fin, dskip = fs["g_mix"], fs["g_ffn"], fs["g_fin"], fs["dskip"]
    a_cat, bb_cat, c_cat = _ssm_disc(fs["lr"], fs["li"], fs["ldt"], fs["br"], fs["bi"], fs["cr"], fs["ci"])
    big, recv = {}, {}

    def mm(a, b, mode, name, tm, tn, **kw):
        if shards is None or (name not in AG_HOSTS and name not in A2A_HOSTS):
            return _mm(a, b, mode, name, tm, tn, **kw)
        if name in AG_HOSTS:
            names = AG_HOSTS[name]
            items, bufs = [], []
            for n in names:
                parts = ("w_ffn_gate", "w_ffn_up") if n == "w_gu" else (n,)
                c, k = shards[parts[0]].shape
                for j, p in enumerate(parts):
                    items.append((shards[p], len(bufs), j * N_DEV))
                bufs.append((len(parts) * N_DEV, c, k))
            out, *gathered = _mm(a, b, mode, name, tm, tn, comm=_ag_comm(items, bufs), **kw)
            for n, g3 in zip(names, gathered):
                w[n] = g3.reshape(-1, g3.shape[2])
            return out
        items = []
        for n in A2A_HOSTS[name]:
            src, slot0 = GRAD_SOURCE.get(n, (n, 0))
            c, k = shards[n].shape
            items.append((big[src].reshape(-1, c, k), slot0))
        out, *got = _mm(a, b, mode, name, tm, tn, comm=_a2a_comm(items), **kw)
        for n, r3 in zip(A2A_HOSTS[name], got):
            recv[n] = r3
        return out

    h0 = _rms_fwd(x, g_mix, "rms_mix")
    proj = mm(h0, w["w_in"], "nt", "mm_proj", 512, IN_W // 2)
    q, k, v, u, gates = _rope_split(proj, tabs)
    os_, lses = [], []
    for g in range(3):
        o_g, l_g = _attn_fwd(q, k, v, g, n_samples)
        os_.append(o_g)
        lses.append(l_g)
    attn, lse_tot = _attn_merge(os_, lses)
    attn_d = mm(attn, w["w_attn_out"], "nt", "mm_attn_out", 512, D_MODEL)

    u_perm = _to_scan_rows(u, n_samples)
    bu = mm(u_perm, bb_cat, "nn", "mm_bu", 512, N_STATE)
    xs3, ein = _scan_fwd(bu.reshape(SCAN_LEN, n_rows, 2 * N_STATE), a_cat, n_rows)
    xs = xs3.reshape(t, 2 * N_STATE)
    ypre = mm(xs, c_cat, "nt", "mm_ssm_y", 256, SSM_W)
    ytot, yg_perm = _ssm_act(ypre, u_perm, dskip)
    yg = _from_scan_rows(yg_perm, n_samples)
    z = mm(yg, w["w_glu"], "nt", "mm_glu", 512, 2 * D_MODEL)

    merged = _mix(attn_d, z, gates)
    x1 = mm(merged, w["w_out"], "nn", "mm_out", 512, D_MODEL, add=x)
    h2 = _rms_fwd(x1, g_ffn, "rms_ffn")
    ab = mm(h2, w["w_gu"], "nt", "mm_ffn_in", 512, D_FF)
    f = _swiglu(ab)
    x2 = mm(f, w["w_ffn_down"], "nn", "mm_ffn_down", 512, D_MODEL, add=x1)
    dx2, dx2b, loss_blk, g_gfin = _final(x2, target, g_fin)

    df = mm(dx2b, w["w_ffn_down"], "nt", "mm_d_f", 512, D_FF)
    dab = _swiglu_bwd(df, ab)
    big["w_ffn_down"] = mm(f, dx2b, "tn", "mm_g_down", 256, 512, out_dtype=BF16)
    big["w_gu"] = mm(dab, h2, "tn", "mm_g_gu", 512, 512, out_dtype=BF16)
    dh2 = mm(dab, w["w_gu"], "nn", "mm_d_h2", 512, 512)
    dx1, dx1b, g_gffn = _rms_bwd(x1, g_ffn, dh2, dx2, "rms_ffn_bwd")

    dmerged = mm(dx1b, w["w_out"], "nt", "mm_d_merged", 512, D_MODEL)
    big["w_out"] = mm(merged, dx1b, "tn", "mm_g_out", 512, 512, out_dtype=BF16)
    dattn_d, dz, dgpre = _mix_bwd(dmerged, gates, attn_d, z)

    dattn = mm(dattn_d, w["w_attn_out"], "nn", "mm_d_attn", 512, GROUP_W)
    big["w_attn_out"] = mm(dattn_d, attn, "tn", "mm_g_attn_out", 512, GROUP_W, out_dtype=BF16)
    rowdot = _attn_rowdot(dattn, attn)
    dqs, dks, dvs = [], [], []
    for g in range(3):
        dq_g, dk_g, dv_g = _attn_bwd(q, k, v, dattn, lse_tot, rowdot, g, n_samples)
        dqs.append(dq_g)
        dks.append(dk_g)
        dvs.append(dv_g)

    dyg = mm(dz, w["w_glu"], "nn", "mm_d_yg", 512, SSM_W)
    big["w_glu"] = mm(dz, yg, "tn", "mm_g_glu", 512, 512, out_dtype=BF16)
    dyg_perm = _to_scan_rows(dyg, n_samples)
    dypre, du_skip, g_dskip = _ssm_act_bwd(dyg_perm, ytot, u_perm, dskip)
    dxs = mm(dypre, c_cat, "nn", "mm_d_xs", 512, N_STATE)
    gs3, da_cat = _scan_bwd(dxs.reshape(SCAN_LEN, n_rows, 2 * N_STATE), xs3, ein, a_cat, n_rows)
    gs = gs3.reshape(t, 2 * N_STATE)
    du_perm = mm(gs, bb_cat, "nt", "mm_d_u", 256, SSM_W, add=du_skip)
    du = _from_scan_rows(du_perm, n_samples)
    dbb_full = mm(u_perm, gs, "tn", "mm_g_bb", 256, 256)
    dc_full = mm(dypre, xs, "tn", "mm_g_c", 256, 256)
    g_lr, g_li, g_ldt, g_br, g_bi, g_cr, g_ci = _ssm_param_bwd(
        fs["lr"], fs["li"], fs["ldt"], fs["br"], fs["bi"], da_cat, dbb_full, dc_full)

    dproj = _pack_dproj(dqs, dks, dvs, du, dgpre, tabs)
    big["w_in"] = mm(dproj, h0, "tn", "mm_g_in", 256, 512, out_dtype=BF16)
    dh0 = mm(dproj, w["w_in"], "nn", "mm_d_h0", 512, 512)
    grad_x, _, g_gmix = _rms_bwd(x, g_mix, dh0, dx1, "rms_mix_bwd")

    small_g = dict(lr=g_lr, li=g_li, ldt=g_ldt, br=g_br, bi=g_bi, cr=g_cr, ci=g_ci, dskip=g_dskip,
                   g_mix=g_gmix, g_ffn=g_gffn, g_fin=g_gfin, loss=loss_blk)
    return loss_blk, grad_x, (big if shards is None else recv), small_g


_MESH = pl.DeviceIdType.MESH


def _all_gather(block, name):
    rows, lanes = block.shape

    def body(x_ref, out_ref, send_sems, recv_sems, local_sem):
        x, y, c = lax.axis_index("x"), lax.axis_index("y"), lax.axis_index("c")
        me, sibling = (x, y, c), (x, y, 1 - c)
        chips = [(1 - x, y), (x, 1 - y), (1 - x, 1 - y)]

        def slot(px, py, pc):
            return out_ref.at[4 * px + 2 * py + pc]

        def copy(k, blk, to, src=None):
            return pltpu.make_async_remote_copy(
                src_ref=slot(*blk) if src is None else src, dst_ref=slot(*blk), send_sem=send_sems.at[k],
                recv_sem=recv_sems.at[k], device_id=to, device_id_type=_MESH)

        mine = pltpu.make_async_copy(x_ref, slot(*me), local_sem)
        mine.start()
        first = [copy(0, me, sibling, src=x_ref)]
        first += [copy(1 + j, me, (*chip, c), src=x_ref) for j, chip in enumerate(chips)]
        for cp in first:
            cp.start()
        passed = [copy(4 + j, (*chip, c), sibling) for j, chip in enumerate(chips)]
        for j, chip in enumerate(chips):
            copy(1 + j, (*chip, c), me).wait_recv()
            passed[j].start()
        copy(0, sibling, me).wait_recv()
        for j, chip in enumerate(chips):
            copy(4 + j, (*chip, 1 - c), me).wait_recv()
        for cp in first + passed:
            cp.wait_send()
        mine.wait()

    return pl.pallas_call(
        body, name=name, out_shape=jax.ShapeDtypeStruct((N_DEV, rows, lanes), block.dtype),
        in_specs=[pl.BlockSpec(memory_space=pl.ANY)], out_specs=pl.BlockSpec(memory_space=pl.ANY),
        scratch_shapes=[pltpu.SemaphoreType.DMA((7,)), pltpu.SemaphoreType.DMA((7,)), pltpu.SemaphoreType.DMA],
    )(block)


def _ag_comm(items, bufs):
    def plan(in_refs, out_refs, send_sems, recv_sems, local_sems):
        x, y, c = lax.axis_index("x"), lax.axis_index("y"), lax.axis_index("c")
        me, sibling = (x, y, c), (x, y, 1 - c)
        chips = [(1 - x, y), (x, 1 - y), (1 - x, 1 - y)]
        plans = []
        for t, (_, buf, slot0) in enumerate(items):
            x_ref, out_ref = in_refs[t], out_refs[buf]

            def slot(px, py, pc, out_ref=out_ref, slot0=slot0):
                return out_ref.at[slot0 + 4 * px + 2 * py + pc]

            def copy(k, blk, to, src=None, t=t, slot=slot):
                return pltpu.make_async_remote_copy(
                    src_ref=slot(*blk) if src is None else src, dst_ref=slot(*blk), send_sem=send_sems.at[7 * t + k],
                    recv_sem=recv_sems.at[7 * t + k], device_id=to, device_id_type=_MESH)

            plans.append(dict(
                mine=pltpu.make_async_copy(x_ref, slot(*me), local_sems.at[t]),
                first=[copy(0, me, sibling, src=x_ref)] + [copy(1 + j, me, (*chip, c), src=x_ref)
                                                           for j, chip in enumerate(chips)],
                passed=[copy(4 + j, (*chip, c), sibling) for j, chip in enumerate(chips)],
                from_ici=[copy(1 + j, (*chip, c), me) for j, chip in enumerate(chips)],
                from_sibling=[copy(0, sibling, me)] + [copy(4 + j, (*chip, 1 - c), me) for j, chip in enumerate(chips)]))
        return plans

    def start(*refs):
        for p in plan(*refs):
            p["mine"].start()
            for cp in p["first"]:
                cp.start()

    def finish(*refs):
        plans = plan(*refs)
        for p in plans:
            for arrived, onward in zip(p["from_ici"], p["passed"]):
                arrived.wait_recv()
                onward.start()
        for p in plans:
            for arrived in p["from_sibling"]:
                arrived.wait_recv()
            for cp in p["first"] + p["passed"]:
                cp.wait_send()
            p["mine"].wait()

    out_shapes = [jax.ShapeDtypeStruct(b, items[0][0].dtype) for b in bufs]
    return _Comm([it[0] for it in items], out_shapes, 7 * len(items), len(items), start, finish)


def _a2a_comm(items):
    def plan(in_refs, out_refs, send_sems, recv_sems, local_sems):
        x, y, c = lax.axis_index("x"), lax.axis_index("y"), lax.axis_index("c")
        my = 4 * x + 2 * y + c
        copies, locals_ = [], []
        for t, (_, slot0) in enumerate(items):
            s_ref, r_ref = in_refs[t], out_refs[t]
            locals_.append(pltpu.make_async_copy(s_ref.at[slot0 + my], r_ref.at[my], local_sems.at[t]))
            for kk in range(1, N_DEV):
                px = 1 - x if kk & 4 else x
                py = 1 - y if kk & 2 else y
                pc = 1 - c if kk & 1 else c
                copies.append(pltpu.make_async_remote_copy(
                    src_ref=s_ref.at[slot0 + 4 * px + 2 * py + pc], dst_ref=r_ref.at[my],
                    send_sem=send_sems.at[7 * t + kk - 1], recv_sem=recv_sems.at[7 * t + kk - 1],
                    device_id=(px, py, pc), device_id_type=_MESH))
        return copies, locals_

    def start(*refs):
        copies, locals_ = plan(*refs)
        for cp in locals_ + copies:
            cp.start()

    def finish(*refs):
        copies, locals_ = plan(*refs)
        for cp in copies + locals_:
            cp.wait()

    out_shapes = [jax.ShapeDtypeStruct((N_DEV,) + it[0].shape[1:], it[0].dtype) for it in items]
    return _Comm([it[0] for it in items], out_shapes, 7 * len(items), len(items), start, finish)


def _adam_math(g, w, m, v):
    m_new = ADAM_B1 * m + (1.0 - ADAM_B1) * g
    v_new = ADAM_B2 * v + (1.0 - ADAM_B2) * jnp.square(g)
    m_hat = m_new / (1.0 - ADAM_B1 ** ADAM_STEP)
    v_hat = v_new / (1.0 - ADAM_B2 ** ADAM_STEP)
    return -ADAM_LR * (m_hat / (jnp.sqrt(v_hat) + ADAM_EPS) + ADAM_WD * w), m_new, v_new


def _sum_partials(partials, name, tm):
    n, rows, cols = partials.shape

    def body(p_ref, g_ref):
        g = p_ref[0].astype(F32)
        for s in range(1, n):
            g = g + p_ref[s].astype(F32)
        g_ref[...] = g

    return pl.pallas_call(
        body, name=name, grid=(rows // tm,), in_specs=[pl.BlockSpec((n, tm, cols), lambda i: (0, i, 0))],
        out_specs=pl.BlockSpec((tm, cols), lambda i: (i, 0)), out_shape=jax.ShapeDtypeStruct((rows, cols), F32),
        compiler_params=pltpu.CompilerParams(dimension_semantics=("parallel",), vmem_limit_bytes=VMEM_MID),
    )(partials)


def _adam(partials, w, m, v, name, tm):
    n, rows, cols = partials.shape

    def body(p_ref, w_ref, m_ref, v_ref, g_ref, d_ref, nm_ref, nv_ref):
        g = p_ref[0].astype(F32)
        for s in range(1, n):
            g = g + p_ref[s].astype(F32)
        g_ref[...] = g
        d_ref[...], nm_ref[...], nv_ref[...] = _adam_math(g, w_ref[...], m_ref[...], v_ref[...])

    assert rows % tm == 0
    row = pl.BlockSpec((tm, cols), lambda i: (i, 0))
    shp = jax.ShapeDtypeStruct((rows, cols), F32)
    return pl.pallas_call(
        body, name=name, grid=(rows // tm,),
        in_specs=[pl.BlockSpec((n, tm, cols), lambda i: (0, i, 0)), row, row, row],
        out_specs=[row] * 4, out_shape=[shp] * 4,
        compiler_params=pltpu.CompilerParams(dimension_semantics=("parallel",), vmem_limit_bytes=VMEM_MID),
    )(partials, w, m, v)


_PK_LR, _PK_LI, _PK_GAINS, _PK_MISC, _PK_BR, _PK_BI, _PK_CR, _PK_CI, _PK_ROWS = 0, 1, 2, 3, 8, 24, 40, 56, 72
_PK_LDT_LANE, _PK_LOSS_LANE = D_MODEL + SSM_W, D_MODEL + SSM_W + LANES


def _pack_small(sg):
    names = ("lr", "li", "g_mix", "g_ffn", "g_fin", "dskip", "ldt", "loss", "br", "bi", "cr", "ci")

    def body(lr, li, gmix, gffn, gfin, dskip, ldt, loss, br, bi, cr, ci, o_ref):
        o_ref[...] = jnp.zeros_like(o_ref)
        o_ref[_PK_LR:_PK_LR + 1, :] = lr[...]
        o_ref[_PK_LI:_PK_LI + 1, :] = li[...]
        o_ref[_PK_GAINS:_PK_GAINS + 1, :D_MODEL] = gmix[...]
        o_ref[_PK_GAINS:_PK_GAINS + 1, D_MODEL:] = gffn[...]
        o_ref[_PK_MISC:_PK_MISC + 1, :D_MODEL] = gfin[...]
        o_ref[_PK_MISC:_PK_MISC + 1, D_MODEL:D_MODEL + SSM_W] = dskip[...]
        o_ref[_PK_MISC:_PK_MISC + 1, _PK_LDT_LANE:_PK_LDT_LANE + LANES] = ldt[0:1, :]
        o_ref[_PK_MISC:_PK_MISC + 1, _PK_LOSS_LANE:_PK_LOSS_LANE + LANES] = loss[0:1, :]
        o_ref[_PK_BR:_PK_BR + SSM_CH, :] = br[...]
        o_ref[_PK_BI:_PK_BI + SSM_CH, :] = bi[...]
        o_ref[_PK_CR:_PK_CR + SSM_CH, :] = cr[...]
        o_ref[_PK_CI:_PK_CI + SSM_CH, :] = ci[...]

    return pl.pallas_call(body, name="pack_small", out_shape=jax.ShapeDtypeStruct((_PK_ROWS, N_STATE), F32))(
        *[sg[n] for n in names])


def _unpack_small(s):
    unflat_b = lambda a: a.reshape(SSM_CH, SSM_GROUPS, SSM_STATE).transpose(1, 2, 0)[None]
    unflat_c = lambda a: a.reshape(SSM_CH, SSM_GROUPS, SSM_STATE).transpose(1, 0, 2)[None]
    grads = {
        "norm_mix_g": s[_PK_GAINS, :D_MODEL].reshape(1, D_MODEL), "norm_ffn_g": s[_PK_GAINS, D_MODEL:].reshape(1, D_MODEL),
        "norm_final_g": s[_PK_MISC, :D_MODEL],
        "ssm_a_re": s[_PK_LR].reshape(1, SSM_GROUPS, SSM_STATE), "ssm_a_im": s[_PK_LI].reshape(1, SSM_GROUPS, SSM_STATE),
        "ssm_log_dt": s[_PK_MISC, _PK_LDT_LANE:_PK_LDT_LANE + SSM_GROUPS].reshape(1, SSM_GROUPS),
        "ssm_d": s[_PK_MISC, D_MODEL:D_MODEL + SSM_W].reshape(1, SSM_GROUPS, SSM_CH),
        "ssm_b_re": unflat_b(s[_PK_BR:_PK_BR + SSM_CH]), "ssm_b_im": unflat_b(s[_PK_BI:_PK_BI + SSM_CH]),
        "ssm_c_re": unflat_c(s[_PK_CR:_PK_CR + SSM_CH]), "ssm_c_im": unflat_c(s[_PK_CI:_PK_CI + SSM_CH]),
    }
    return s[_PK_MISC, _PK_LOSS_LANE], grads


def _adam_small(grads, wts, moms, vars_):
    n = len(SMALL_WEIGHTS)
    as2d = lambda a: a.reshape(1, -1) if a.ndim == 1 else a

    def body(*refs):
        ins, outs = refs[:4 * n], refs[4 * n:]
        for i in range(n):
            g, w, m, v = (ins[j * n + i][...] for j in range(4))
            outs[i][...], outs[n + i][...], outs[2 * n + i][...] = _adam_math(g, w, m, v)

    operands = [as2d(d[k]) for d in (grads, wts, moms, vars_) for k in SMALL_WEIGHTS]
    shapes = [jax.ShapeDtypeStruct(as2d(wts[k]).shape, F32) for k in SMALL_WEIGHTS] * 3
    res = pl.pallas_call(body, name="adam_small", out_shape=shapes,
                         compiler_params=pltpu.CompilerParams(vmem_limit_bytes=VMEM_BIG))(*operands)
    out = {}
    for j, kind in enumerate(("delta", "new_m", "new_v")):
        for i, k in enumerate(SMALL_WEIGHTS):
            out[kind, k] = res[j * n + i].reshape(wts[k].shape)
    return out


def kernel(x, norm_mix_g, w_in, ssm_a_re, ssm_a_im, ssm_log_dt, ssm_b_re, ssm_b_im, ssm_c_re, ssm_c_im, ssm_d, w_glu, w_attn_out, w_out, norm_ffn_g, w_ffn_gate, w_ffn_up, w_ffn_down, norm_final_g, loss_target, m_norm_mix_g, m_w_in, m_ssm_a_re, m_ssm_a_im, m_ssm_log_dt, m_ssm_b_re, m_ssm_b_im, m_ssm_c_re, m_ssm_c_im, m_ssm_d, m_w_glu, m_w_attn_out, m_w_out, m_norm_ffn_g, m_w_ffn_gate, m_w_ffn_up, m_w_ffn_down, m_norm_final_g, v_norm_mix_g, v_w_in, v_ssm_a_re, v_ssm_a_im, v_ssm_log_dt, v_ssm_b_re, v_ssm_b_im, v_ssm_c_re, v_ssm_c_im, v_ssm_d, v_w_glu, v_w_attn_out, v_w_out, v_norm_ffn_g, v_w_ffn_gate, v_w_ffn_up, v_w_ffn_down, v_norm_final_g):
    args = dict(locals())
    wts = {n: args[n] for n in ALL_WEIGHTS}
    moms = {n: args["m_" + n] for n in ALL_WEIGHTS}
    vars_ = {n: args["v_" + n] for n in ALL_WEIGHTS}
    n_samples = x.shape[0]
    t = n_samples * SEQ

    shards = {n: (wts[n][0] if n in ROW_SHARDED else wts[n][0].T).astype(BF16) for n in BIG_WEIGHTS}
    w_in_t = _all_gather(shards["w_in"], "allgather_w_in").reshape(IN_W, D_MODEL)

    small = {n: wts[n] for n in SMALL_WEIGHTS}
    _, grad_x, recv, small_g = _local_step(x.reshape(t, D_MODEL), loss_target.reshape(t, D_MODEL), {"w_in": w_in_t},
                                           small, shards)

    results = {}
    for n in BIG_WEIGHTS:
        c, k = shards[n].shape
        w2, m2, v2 = wts[n][0], moms[n][0], vars_[n][0]
        if n in ROW_SHARDED:
            res = _adam(recv[n], w2, m2, v2, "adam_" + n, c // 2)
        else:
            g_t = _sum_partials(recv[n], "sum_" + n, c // 2)
            res = _adam(g_t.T[None], w2, m2, v2, "adam_" + n, k // 2)
        for kind, a in zip(("grad", "delta", "new_m", "new_v"), res):
            results[kind, n] = a[None]

    sgath = _all_gather(_pack_small(small_g), "allgather_small_grads")
    loss, sgrads = _unpack_small(_sum_partials(sgath, "sum_small", _PK_ROWS))
    for n in SMALL_WEIGHTS:
        results["grad", n] = sgrads[n]
    results.update(_adam_small(sgrads, wts, moms, vars_))
    outs = [loss, grad_x.reshape(x.shape)]
    for kind in ("grad", "delta", "new_m", "new_v"):
        outs += [results[kind, n] for n in ALL_WEIGHTS]
    return tuple(outs)
```

```python
import functools
import math

import jax
import jax.numpy as jnp
from jax import lax
from jax.experimental import pallas as pl
from jax.experimental.pallas import tpu as pltpu

F32 = jnp.float32
BF16 = jnp.bfloat16
MXU_DTYPE = jnp.bfloat16

N_DEV = 8
D_MODEL = 1024
SEQ = 2048
HEAD_DIM = 64
HEADS_PER_GROUP = 4
GROUP_W = HEADS_PER_GROUP * HEAD_DIM
DILATIONS = (1, 4, 16)
QKV_W = 3 * len(DILATIONS) * GROUP_W
Q_W = len(DILATIONS) * GROUP_W
ATT_BLOCK = 128
ROPE_DIM = 16
ROPE_THETA = 500000.0
SSM_W = 512
SSM_GROUPS = 32
SSM_CH = 16
SSM_STATE = 64
N_STATE = SSM_GROUPS * SSM_STATE
D_FF = 2816
IN_W = QKV_W + SSM_W + 2 * D_MODEL
RMS_EPS = 1e-6
NEG_INF = -1e30
LANES = 128

SCAN_SEG_PER_SAMPLE = 8
SCAN_LEN = SEQ // SCAN_SEG_PER_SAMPLE
SCAN_WC = 128
SCAN_NBLK = N_STATE // SCAN_WC

ADAM_LR = 0.001
ADAM_B1 = 0.9
ADAM_B2 = 0.999
ADAM_EPS = 1e-08
ADAM_WD = 0.01
ADAM_STEP = 10

VMEM_BIG = 48 * 1024 * 1024
VMEM_MID = 32 * 1024 * 1024

BIG_WEIGHTS = ("w_in", "w_glu", "w_attn_out", "w_out", "w_ffn_gate", "w_ffn_up", "w_ffn_down")
ROW_SHARDED = ("w_out", "w_ffn_down")
SMALL_WEIGHTS = ("norm_mix_g", "ssm_a_re", "ssm_a_im", "ssm_log_dt", "ssm_b_re", "ssm_b_im", "ssm_c_re", "ssm_c_im",
                 "ssm_d", "norm_ffn_g", "norm_final_g")
ALL_WEIGHTS = ("norm_mix_g", "w_in", "ssm_a_re", "ssm_a_im", "ssm_log_dt", "ssm_b_re", "ssm_b_im", "ssm_c_re", "ssm_c_im",
               "ssm_d", "w_glu", "w_attn_out", "w_out", "norm_ffn_g", "w_ffn_gate", "w_ffn_up", "w_ffn_down", "norm_final_g")


def _sigmoid(x):
    return 1.0 / (1.0 + jnp.exp(-x))


def _pallas_call(body, *, out_shape, **kw):
    single = not isinstance(out_shape, (list, tuple))
    shapes = [pltpu.HBM(s.shape, s.dtype) for s in ([out_shape] if single else out_shape)]
    call = pl.pallas_call(body, out_shape=shapes[0] if single else shapes, **kw)
    return lambda *operands: call(*[pltpu.with_memory_space_constraint(o, pltpu.HBM) for o in operands])


class _Comm:
    def __init__(self, ins, out_shapes, n_sem, n_local, start, finish):
        self.ins, self.out_shapes, self.n_sem, self.n_local = ins, out_shapes, n_sem, n_local
        self.start, self.finish = start, finish


def _mm(a, b, mode, name, tm, tn, out_dtype=F32, add=None, vmem=VMEM_BIG, comm=None):
    if mode == "nn":
        (m, k), (_, n) = a.shape, b.shape
        a_spec = pl.BlockSpec((tm, k), lambda i, j: (i, 0))
        b_spec = pl.BlockSpec((k, tn), lambda i, j: (0, j))
        dims = (((1,), (0,)), ((), ()))
    elif mode == "nt":
        (m, k), (n, _) = a.shape, b.shape
        a_spec = pl.BlockSpec((tm, k), lambda i, j: (i, 0))
        b_spec = pl.BlockSpec((tn, k), lambda i, j: (j, 0))
        dims = (((1,), (1,)), ((), ()))
    else:
        (k, m), (_, n) = a.shape, b.shape
        a_spec = pl.BlockSpec((k, tm), lambda i, j: (0, i))
        b_spec = pl.BlockSpec((k, tn), lambda i, j: (0, j))
        dims = (((0,), (0,)), ((), ()))
    assert m % tm == 0 and n % tn == 0, (name, m, n, tm, tn)
    o_spec = pl.BlockSpec((tm, tn), lambda i, j: (i, j))
    has_add = add is not None
    n_in = 3 if has_add else 2
    n_cin = len(comm.ins) if comm else 0
    n_cout = len(comm.out_shapes) if comm else 0
    ni, nj = m // tm, n // tn

    def body(*refs):
        a_ref, b_ref = refs[0], refs[1]
        o_ref = refs[n_in + n_cin]
        if comm:
            c_args = (refs[n_in:n_in + n_cin], refs[n_in + n_cin + 1:n_in + n_cin + 1 + n_cout], *refs[-3:])

            @pl.when((pl.program_id(0) == 0) & (pl.program_id(1) == 0))
            def _():
                comm.start(*c_args)

        acc = lax.dot_general(a_ref[...].astype(MXU_DTYPE), b_ref[...].astype(MXU_DTYPE), dims,
                              preferred_element_type=F32)
        if has_add:
            acc = acc + refs[2][...]
        o_ref[...] = acc.astype(out_dtype)
        if comm:
            @pl.when((pl.program_id(0) == ni - 1) & (pl.program_id(1) == nj - 1))
            def _():
                comm.finish(*c_args)

    ins = [a, b] + ([add] if has_add else [])
    in_specs = [a_spec, b_spec] + ([o_spec] if has_add else [])
    out_shape = jax.ShapeDtypeStruct((m, n), out_dtype)
    if not comm:
        return _pallas_call(
            body, name=name, grid=(ni, nj), in_specs=in_specs, out_specs=o_spec, out_shape=out_shape,
            compiler_params=pltpu.CompilerParams(dimension_semantics=("parallel", "parallel"), vmem_limit_bytes=vmem),
        )(*ins)
    hbm = pl.BlockSpec(memory_space=pl.ANY)
    return _pallas_call(
        body, name=name, grid=(ni, nj), in_specs=in_specs + [hbm] * n_cin, out_specs=[o_spec] + [hbm] * n_cout,
        out_shape=[out_shape] + list(comm.out_shapes),
        scratch_shapes=[pltpu.SemaphoreType.DMA((comm.n_sem,)), pltpu.SemaphoreType.DMA((comm.n_sem,)),
                        pltpu.SemaphoreType.DMA((comm.n_local,))],
        compiler_params=pltpu.CompilerParams(dimension_semantics=("arbitrary", "arbitrary"), vmem_limit_bytes=vmem),
    )(*ins, *comm.ins)


def _rows(body, name, n_rows, tm, ins, outs, vmem=VMEM_MID):
    assert n_rows % tm == 0
    arrays, in_specs = [], []
    for kind, arr in ins:
        arrays.append(arr)
        if kind == "row":
            assert arr.shape[0] == n_rows, (name, arr.shape)
            in_specs.append(pl.BlockSpec((tm, arr.shape[1]), lambda i: (i, 0)))
        elif kind == "tab":
            nblk = arr.shape[0] // tm
            in_specs.append(pl.BlockSpec((tm, arr.shape[1]), lambda i, nblk=nblk: (i % nblk, 0)))
        else:
            in_specs.append(pl.BlockSpec(arr.shape, lambda i, nd=arr.ndim: (0,) * nd))
    out_specs, out_shape = [], []
    for kind, shp, dt in outs:
        if kind == "row":
            out_specs.append(pl.BlockSpec((tm, shp), lambda i: (i, 0)))
            out_shape.append(jax.ShapeDtypeStruct((n_rows, shp), dt))
        else:
            out_specs.append(pl.BlockSpec(shp, lambda i, nd=len(shp): (0,) * nd))
            out_shape.append(jax.ShapeDtypeStruct(shp, dt))
    res = _pallas_call(
        body, name=name, grid=(n_rows // tm,), in_specs=in_specs, out_specs=out_specs, out_shape=out_shape,
        compiler_params=pltpu.CompilerParams(dimension_semantics=("arbitrary",), vmem_limit_bytes=vmem),
    )(*arrays)
    return res


def _first_step():
    return pl.program_id(0) == 0


def _rms_fwd(x, g, name):
    def body(x_ref, g_ref, h_ref):
        xv = x_ref[...]
        r = lax.rsqrt(jnp.mean(xv * xv, axis=-1, keepdims=True) + RMS_EPS)
        h_ref[...] = ((xv * r) * g_ref[...]).astype(BF16)

    return _rows(body, name, x.shape[0], 512, [("row", x), ("const", g)], [("row", x.shape[1], BF16)])[0]


def _rms_bwd(x, g, dh, dres, name):
    def body(x_ref, g_ref, dh_ref, dres_ref, dx_ref, dxb_ref, gg_ref):
        @pl.when(_first_step())
        def _():
            gg_ref[...] = jnp.zeros_like(gg_ref)

        xv = x_ref[...]
        r = lax.rsqrt(jnp.mean(xv * xv, axis=-1, keepdims=True) + RMS_EPS)
        n = xv * r
        dh_v = dh_ref[...]
        gg_ref[...] += jnp.sum(dh_v * n, axis=0, keepdims=True)
        dn = dh_v * g_ref[...]
        dx = dres_ref[...] + r * (dn - n * jnp.mean(dn * n, axis=-1, keepdims=True))
        dx_ref[...] = dx
        dxb_ref[...] = dx.astype(BF16)

    d = x.shape[1]
    return _rows(body, name, x.shape[0], 256, [("row", x), ("const", g), ("row", dh), ("row", dres)],
                 [("row", d, F32), ("row", d, BF16), ("acc", (1, d), F32)])


def _rope_tables():
    half = ROPE_DIM // 2
    inv = jnp.power(jnp.float32(ROPE_THETA), -jnp.arange(half, dtype=F32) * 2.0 / ROPE_DIM)
    ang = jnp.arange(SEQ, dtype=F32)[:, None] * inv[None, :]
    lane = jnp.arange(LANES) % HEAD_DIM
    cosl = jnp.cos(ang)[:, lane % half]
    sinl = jnp.sin(ang)[:, lane % half]
    tab_c = jnp.where(lane < ROPE_DIM, cosl, 1.0)
    tab_lo = jnp.where(lane < half, -sinl, 0.0)
    tab_hi = jnp.where((lane >= half) & (lane < ROPE_DIM), sinl, 0.0)
    return tab_c.astype(F32), tab_lo.astype(F32), tab_hi.astype(F32)


def _rope_apply(t, tc, tlo, thi):
    half = ROPE_DIM // 2
    return t * tc + pltpu.roll(t, LANES - half, 1) * tlo + pltpu.roll(t, half, 1) * thi


def _rope_transpose(dt, tc, tlo, thi):
    half = ROPE_DIM // 2
    return dt * tc + pltpu.roll(dt * tlo, half, 1) + pltpu.roll(dt * thi, LANES - half, 1)


def _rope_split(proj, tabs):
    def body(p_ref, tc_ref, tlo_ref, thi_ref, q_ref, k_ref, v_ref, u_ref, g_ref):
        tc, tlo, thi = tc_ref[...], tlo_ref[...], thi_ref[...]
        for ch in range(Q_W // LANES):
            sl = slice(ch * LANES, (ch + 1) * LANES)
            q_ref[:, sl] = _rope_apply(p_ref[:, sl], tc, tlo, thi).astype(BF16)
            k_ref[:, sl] = _rope_apply(p_ref[:, Q_W + ch * LANES:Q_W + (ch + 1) * LANES], tc, tlo, thi).astype(BF16)
        v_ref[...] = p_ref[:, 2 * Q_W:QKV_W].astype(BF16)
        u_ref[...] = p_ref[:, QKV_W:QKV_W + SSM_W]
        g_ref[...] = _sigmoid(p_ref[:, QKV_W + SSM_W:])

    t = proj.shape[0]
    return _rows(body, "rope_split", t, 256,
                 [("row", proj), ("tab", tabs[0]), ("tab", tabs[1]), ("tab", tabs[2])],
                 [("row", Q_W, BF16), ("row", Q_W, BF16), ("row", Q_W, BF16), ("row", SSM_W, F32),
                  ("row", 2 * D_MODEL, F32)])


def _pack_dproj(dqs, dks, dvs, du, dgpre, tabs):
    def body(*refs):
        dq_refs, dk_refs, dv_refs = refs[0:3], refs[3:6], refs[6:9]
        du_ref, dg_ref, tc_ref, tlo_ref, thi_ref, o_ref = refs[9:15]
        tc, tlo, thi = tc_ref[...], tlo_ref[...], thi_ref[...]
        for g in range(3):
            for half in range(GROUP_W // LANES):
                sl = slice(half * LANES, (half + 1) * LANES)
                col = g * GROUP_W + half * LANES
                o_ref[:, col:col + LANES] = _rope_transpose(dq_refs[g][:, sl], tc, tlo, thi).astype(BF16)
                o_ref[:, Q_W + col:Q_W + col + LANES] = _rope_transpose(dk_refs[g][:, sl], tc, tlo, thi).astype(BF16)
            o_ref[:, 2 * Q_W + g * GROUP_W:2 * Q_W + (g + 1) * GROUP_W] = dv_refs[g][...].astype(BF16)
        o_ref[:, QKV_W:QKV_W + SSM_W] = du_ref[...].astype(BF16)
        o_ref[:, QKV_W + SSM_W:] = dg_ref[...].astype(BF16)

    t = du.shape[0]
    ins = [("row", a) for a in (*dqs, *dks, *dvs, du, dgpre)] + [("tab", tb) for tb in tabs]
    return _rows(body, "pack_dproj", t, 256, ins, [("row", IN_W, BF16)])[0]


def _attn_merge(os_, lses):
    def body(o0, o1, o2, l0, l1, l2, a_ref, lt_ref):
        la, lb, lc = l0[...], l1[...], l2[...]
        m = jnp.maximum(jnp.maximum(la, lb), lc)
        ea, eb, ec = jnp.exp(la - m), jnp.exp(lb - m), jnp.exp(lc - m)
        ssum = ea + eb + ec
        a_ref[...] = (ea / ssum) * o0[...] + (eb / ssum) * o1[...] + (ec / ssum) * o2[...]
        lt_ref[...] = m + jnp.log(ssum)

    t = os_[0].shape[0]
    return _rows(body, "attn_merge", t, 512, [("row", a) for a in (*os_, *lses)],
                 [("row", GROUP_W, F32), ("row", GROUP_W, F32)])


def _head_sum_matrix():
    r = jnp.arange(GROUP_W) // HEAD_DIM
    return (r[:, None] == r[None, :]).astype(F32)


def _attn_rowdot(dattn, attn):
    def body(da_ref, a_ref, ones_ref, d_ref):
        d_ref[...] = jnp.dot(da_ref[...] * a_ref[...], ones_ref[...], preferred_element_type=F32,
                             precision=lax.Precision.HIGHEST)

    t = attn.shape[0]
    return _rows(body, "attn_rowdot", t, 512, [("row", dattn), ("row", attn), ("const", _head_sum_matrix())],
                 [("row", GROUP_W, F32)])[0]


def _mix(attn_d, z, gates):
    def body(ad_ref, z_ref, g_ref, m_ref):
        za, zb = z_ref[:, :D_MODEL], z_ref[:, D_MODEL:]
        s_out = za * _sigmoid(zb)
        m_ref[...] = (g_ref[:, :D_MODEL] * ad_ref[...] + g_ref[:, D_MODEL:] * s_out).astype(BF16)

    t = attn_d.shape[0]
    return _rows(body, "mix", t, 256, [("row", attn_d), ("row", z), ("row", gates)], [("row", D_MODEL, BF16)])[0]


def _mix_bwd(dmerged, gates, attn_d, z):
    def body(dm_ref, g_ref, ad_ref, z_ref, dad_ref, dz_ref, dg_ref):
        dm = dm_ref[...]
        g0, g1 = g_ref[:, :D_MODEL], g_ref[:, D_MODEL:]
        za, zb = z_ref[:, :D_MODEL], z_ref[:, D_MODEL:]
        sb = _sigmoid(zb)
        s_out = za * sb
        dad_ref[...] = (dm * g0).astype(BF16)
        ds = dm * g1
        dz_ref[:, :D_MODEL] = (ds * sb).astype(BF16)
        dz_ref[:, D_MODEL:] = (ds * za * sb * (1.0 - sb)).astype(BF16)
        dg_ref[:, :D_MODEL] = dm * ad_ref[...] * g0 * (1.0 - g0)
        dg_ref[:, D_MODEL:] = dm * s_out * g1 * (1.0 - g1)

    t = dmerged.shape[0]
    return _rows(body, "mix_bwd", t, 256, [("row", dmerged), ("row", gates), ("row", attn_d), ("row", z)],
                 [("row", D_MODEL, BF16), ("row", 2 * D_MODEL, BF16), ("row", 2 * D_MODEL, F32)])


def _swiglu(ab):
    def body(ab_ref, f_ref):
        a, b = ab_ref[:, :D_FF], ab_ref[:, D_FF:]
        f_ref[...] = (a * _sigmoid(a) * b).astype(BF16)

    return _rows(body, "swiglu", ab.shape[0], 256, [("row", ab)], [("row", D_FF, BF16)])[0]


def _swiglu_bwd(df, ab):
    def body(df_ref, ab_ref, o_ref):
        a, b = ab_ref[:, :D_FF], ab_ref[:, D_FF:]
        d = df_ref[...]
        sg = _sigmoid(a)
        o_ref[:, :D_FF] = (d * b * sg * (1.0 + a * (1.0 - sg))).astype(BF16)
        o_ref[:, D_FF:] = (d * a * sg).astype(BF16)

    return _rows(body, "swiglu_bwd", ab.shape[0], 256, [("row", df), ("row", ab)], [("row", 2 * D_FF, BF16)])[0]


def _final(x2, target, g):
    def body(x_ref, t_ref, g_ref, dx_ref, dxb_ref, loss_ref, gg_ref):
        @pl.when(_first_step())
        def _():
            loss_ref[...] = jnp.zeros_like(loss_ref)
            gg_ref[...] = jnp.zeros_like(gg_ref)

        xv = x_ref[...]
        gv = g_ref[...]
        r = lax.rsqrt(jnp.mean(xv * xv, axis=-1, keepdims=True) + RMS_EPS)
        n = xv * r
        diff = n * gv - t_ref[...]
        per_tok = jnp.mean(diff * diff, axis=-1, keepdims=True)
        loss_ref[...] += 0.5 * jnp.sum(per_tok, axis=0, keepdims=True)
        dy = diff / xv.shape[-1]
        gg_ref[...] += jnp.sum(dy * n, axis=0, keepdims=True)
        dn = dy * gv
        dx = r * (dn - n * jnp.mean(dn * n, axis=-1, keepdims=True))
        dx_ref[...] = dx
        dxb_ref[...] = dx.astype(BF16)

    d = x2.shape[1]
    return _rows(body, "final_loss", x2.shape[0], 256, [("row", x2), ("row", target), ("const", g)],
                 [("row", d, F32), ("row", d, BF16), ("acc", (8, LANES), F32), ("acc", (1, d), F32)])


_GELU_C = math.sqrt(2.0 / math.pi)


def _ssm_act(ypre, u_perm, dskip):
    def body(y_ref, u_ref, d_ref, yt_ref, yg_ref):
        yt = y_ref[...] + d_ref[...] * u_ref[...]
        yt_ref[...] = yt
        th = jnp.tanh(_GELU_C * (yt + 0.044715 * (yt * yt * yt)))
        yg_ref[...] = (0.5 * yt * (1.0 + th)).astype(BF16)

    t = ypre.shape[0]
    return _rows(body, "ssm_act", t, 512, [("row", ypre), ("row", u_perm), ("const", dskip)],
                 [("row", SSM_W, F32), ("row", SSM_W, BF16)])


def _ssm_act_bwd(dyg, ytot, u_perm, dskip):
    def body(dyg_ref, yt_ref, u_ref, d_ref, dy_ref, dus_ref, dd_ref):
        @pl.when(_first_step())
        def _():
            dd_ref[...] = jnp.zeros_like(dd_ref)

        yt = yt_ref[...]
        th = jnp.tanh(_GELU_C * (yt + 0.044715 * (yt * yt * yt)))
        dgelu = 0.5 * (1.0 + th) + 0.5 * yt * (1.0 - th * th) * _GELU_C * (1.0 + 3.0 * 0.044715 * yt * yt)
        dy = dyg_ref[...] * dgelu
        dy_ref[...] = dy.astype(BF16)
        dus_ref[...] = dy * d_ref[...]
        dd_ref[...] += jnp.sum(dy * u_ref[...], axis=0, keepdims=True)

    t = dyg.shape[0]
    return _rows(body, "ssm_act_bwd", t, 512, [("row", dyg), ("row", ytot), ("row", u_perm), ("const", dskip)],
                 [("row", SSM_W, BF16), ("row", SSM_W, F32), ("acc", (1, SSM_W), F32)])


def _head_masks():
    lane = lax.broadcasted_iota(jnp.int32, (1, GROUP_W), 1)
    return [(lane // HEAD_DIM) == h for h in range(HEADS_PER_GROUP)]


def _band_mask(first):
    nk = ATT_BLOCK if first else 2 * ATT_BLOCK
    qi = lax.broadcasted_iota(jnp.int32, (ATT_BLOCK, nk), 0)
    ki = lax.broadcasted_iota(jnp.int32, (ATT_BLOCK, nk), 1)
    dist = qi - ki + (0 if first else ATT_BLOCK)
    return (dist >= 0) & (dist <= ATT_BLOCK)


_NT = (((1,), (1,)), ((), ()))
_TN = (((0,), (0,)), ((), ()))


def _attn_fwd(q, k, v, group, n_samples):
    d = DILATIONS[group]
    length = SEQ // d
    nb = length // ATT_BLOCK

    def body(q_ref, k_ref, v_ref, o_ref, l_ref):
        masks = _head_masks()

        def block(qs, ks, first):
            nk = ATT_BLOCK if first else 2 * ATT_BLOCK
            qb = q_ref[0, pl.ds(qs, ATT_BLOCK), :]
            kc = k_ref[0, pl.ds(ks, nk), :]
            vc = v_ref[0, pl.ds(ks, nk), :]
            valid = _band_mask(first)
            o_acc = jnp.zeros((ATT_BLOCK, GROUP_W), F32)
            l_acc = jnp.zeros((ATT_BLOCK, GROUP_W), F32)
            for h in range(HEADS_PER_GROUP):
                qh = jnp.where(masks[h], qb, jnp.zeros_like(qb))
                s = lax.dot_general(qh, kc, _NT, preferred_element_type=F32) * (HEAD_DIM ** -0.5)
                s = jnp.where(valid, s, NEG_INF)
                m = jnp.max(s, axis=-1, keepdims=True)
                p = jnp.exp(s - m)
                l = jnp.sum(p, axis=-1, keepdims=True)
                pv = jnp.dot(p.astype(MXU_DTYPE), vc, preferred_element_type=F32)
                o_acc = jnp.where(masks[h], pv / l, o_acc)
                l_acc = jnp.where(masks[h], m + jnp.log(l), l_acc)
            o_ref[0, pl.ds(qs, ATT_BLOCK), :] = o_acc
            l_ref[0, pl.ds(qs, ATT_BLOCK), :] = l_acc

        block(0, 0, True)
        if nb > 1:
            def loop(n, carry):
                block(pl.multiple_of(n * ATT_BLOCK, ATT_BLOCK), pl.multiple_of((n - 1) * ATT_BLOCK, ATT_BLOCK), False)
                return carry

            lax.fori_loop(1, nb, loop, 0)

    qv = q.reshape(n_samples, length, d * Q_W)
    kv = k.reshape(n_samples, length, d * Q_W)
    vv = v.reshape(n_samples, length, d * Q_W)
    in_spec = pl.BlockSpec((1, length, GROUP_W), lambda b, r: (b, 0, 3 * r + group))
    out_spec = pl.BlockSpec((1, length, GROUP_W), lambda b, r: (b, 0, r))
    shp = jax.ShapeDtypeStruct((n_samples, length, d * GROUP_W), F32)
    o, lse = _pallas_call(
        body, name=f"attn_fwd_g{group}", grid=(n_samples, d), in_specs=[in_spec] * 3, out_specs=[out_spec] * 2,
        out_shape=[shp, shp],
        compiler_params=pltpu.CompilerParams(dimension_semantics=("parallel", "parallel"), vmem_limit_bytes=VMEM_MID),
    )(qv, kv, vv)
    return o.reshape(n_samples * SEQ, GROUP_W), lse.reshape(n_samples * SEQ, GROUP_W)


def _attn_bwd(q, k, v, dattn, lse_tot, rowdot, group, n_samples):
    d = DILATIONS[group]
    length = SEQ // d
    nb = length // ATT_BLOCK

    def body(q_ref, k_ref, v_ref, da_ref, lt_ref, rd_ref, dq_ref, dk_ref, dv_ref):
        masks = _head_masks()
        dk_ref[...] = jnp.zeros_like(dk_ref)
        dv_ref[...] = jnp.zeros_like(dv_ref)

        def block(qs, ks, first):
            nk = ATT_BLOCK if first else 2 * ATT_BLOCK
            qb = q_ref[0, pl.ds(qs, ATT_BLOCK), :]
            kc = k_ref[0, pl.ds(ks, nk), :]
            vc = v_ref[0, pl.ds(ks, nk), :]
            da = da_ref[0, pl.ds(qs, ATT_BLOCK), :]
            lt = lt_ref[0, pl.ds(qs, ATT_BLOCK), :]
            rd = rd_ref[0, pl.ds(qs, ATT_BLOCK), :]
            valid = _band_mask(first)
            dq_acc = jnp.zeros((ATT_BLOCK, GROUP_W), F32)
            dk_acc = jnp.zeros((nk, GROUP_W), F32)
            dv_acc = jnp.zeros((nk, GROUP_W), F32)
            for h in range(HEADS_PER_GROUP):
                qh = jnp.where(masks[h], qb, jnp.zeros_like(qb))
                dah = jnp.where(masks[h], da, 0.0).astype(MXU_DTYPE)
                lt_h = jnp.max(jnp.where(masks[h], lt, -jnp.inf), axis=-1, keepdims=True)
                rd_h = jnp.max(jnp.where(masks[h], rd, -jnp.inf), axis=-1, keepdims=True)
                s = lax.dot_general(qh, kc, _NT, preferred_element_type=F32) * (HEAD_DIM ** -0.5)
                s = jnp.where(valid, s, NEG_INF)
                p = jnp.exp(s - lt_h)
                dp = lax.dot_general(dah, vc, _NT, preferred_element_type=F32)
                ds = (p * (dp - rd_h) * (HEAD_DIM ** -0.5)).astype(MXU_DTYPE)
                dq_h = jnp.dot(ds, kc, preferred_element_type=F32)
                dq_acc = jnp.where(masks[h], dq_h, dq_acc)
                dk_acc = dk_acc + lax.dot_general(ds, qh, _TN, preferred_element_type=F32)
                dv_acc = dv_acc + lax.dot_general(p.astype(MXU_DTYPE), dah, _TN, preferred_element_type=F32)
            dq_ref[0, pl.ds(qs, ATT_BLOCK), :] = dq_acc
            dk_ref[0, pl.ds(ks, nk), :] += dk_acc
            dv_ref[0, pl.ds(ks, nk), :] += dv_acc

        block(0, 0, True)
        if nb > 1:
            def loop(n, carry):
                block(pl.multiple_of(n * ATT_BLOCK, ATT_BLOCK), pl.multiple_of((n - 1) * ATT_BLOCK, ATT_BLOCK), False)
                return carry

            lax.fori_loop(1, nb, loop, 0)

    qv = q.reshape(n_samples, length, d * Q_W)
    kv = k.reshape(n_samples, length, d * Q_W)
    vv = v.reshape(n_samples, length, d * Q_W)
    nat = lambda a: a.reshape(n_samples, length, d * GROUP_W)
    in_spec = pl.BlockSpec((1, length, GROUP_W), lambda b, r: (b, 0, 3 * r + group))
    nat_spec = pl.BlockSpec((1, length, GROUP_W), lambda b, r: (b, 0, r))
    shp = jax.ShapeDtypeStruct((n_samples, length, d * GROUP_W), F32)
    dq, dk, dv = _pallas_call(
        body, name=f"attn_bwd_g{group}", grid=(n_samples, d), in_specs=[in_spec] * 3 + [nat_spec] * 3,
        out_specs=[nat_spec] * 3, out_shape=[shp, shp, shp],
        compiler_params=pltpu.CompilerParams(dimension_semantics=("parallel", "parallel"), vmem_limit_bytes=VMEM_MID),
    )(qv, kv, vv, nat(dattn), nat(lse_tot), nat(rowdot))
    t = n_samples * SEQ
    return dq.reshape(t, GROUP_W), dk.reshape(t, GROUP_W), dv.reshape(t, GROUP_W)


def _disc(lr, li, ldt, br, bi):
    dt = jnp.exp(ldt)
    mag = jnp.exp(lr * dt)
    ab_re, ab_im = mag * jnp.cos(li * dt), mag * jnp.sin(li * dt)
    den = lr * lr + li * li
    nr, ni = ab_re - 1.0, ab_im
    f_re = (nr * lr + ni * li) / den
    f_im = (ni * lr - nr * li) / den
    return ab_re, ab_im, f_re * br - f_im * bi, f_re * bi + f_im * br


def _state_mask(cb):
    row_g = lax.broadcasted_iota(jnp.int32, (SSM_W, SCAN_WC), 0) // SSM_CH
    col_g = (cb * SCAN_WC + lax.broadcasted_iota(jnp.int32, (SSM_W, SCAN_WC), 1)) // SSM_STATE
    return row_g == col_g


def _ssm_disc(lr, li, ldt, br, bi, cr, ci):
    w = SCAN_WC

    def body(lr_ref, li_ref, ldt_ref, br_ref, bi_ref, cr_ref, ci_ref, a_ref, bb_ref, c_ref):
        ar, ai, bbr, bbi = _disc(lr_ref[...], li_ref[...], ldt_ref[...], br_ref[...], bi_ref[...])
        crv, civ = cr_ref[...], ci_ref[...]
        for cb in range(SCAN_NBLK):
            sl = slice(cb * w, (cb + 1) * w)
            lo, hi = slice(2 * cb * w, (2 * cb + 1) * w), slice((2 * cb + 1) * w, (2 * cb + 2) * w)
            mask = _state_mask(cb)
            dense = lambda comp: jnp.where(mask, jnp.tile(comp[:, sl], (SSM_GROUPS, 1)), 0.0)
            a_ref[:, lo] = ar[:, sl]
            a_ref[:, hi] = ai[:, sl]
            bb_ref[:, lo] = dense(bbr).astype(MXU_DTYPE)
            bb_ref[:, hi] = dense(bbi).astype(MXU_DTYPE)
            c_ref[:, lo] = dense(crv).astype(MXU_DTYPE)
            c_ref[:, hi] = (-dense(civ)).astype(MXU_DTYPE)

    return _pallas_call(
        body, name="ssm_disc",
        out_shape=[jax.ShapeDtypeStruct((1, 2 * N_STATE), F32), jax.ShapeDtypeStruct((SSM_W, 2 * N_STATE), MXU_DTYPE),
                   jax.ShapeDtypeStruct((SSM_W, 2 * N_STATE), MXU_DTYPE)],
        compiler_params=pltpu.CompilerParams(vmem_limit_bytes=VMEM_MID),
    )(lr, li, ldt, br, bi, cr, ci)


def _group_indicator():
    s = jnp.arange(N_STATE) // SSM_STATE
    return (s[:, None] == jnp.arange(LANES)[None, :]).astype(F32)


def _ssm_param_bwd(lr, li, ldt, br, bi, da_cat, dbb_full, dc_full):
    w = SCAN_WC

    def body(lr_ref, li_ref, ldt_ref, br_ref, bi_ref, da_ref, dbb_ref, dc_ref, ind_ref,
             glr_ref, gli_ref, gldt_ref, gbr_ref, gbi_ref, gcr_ref, gci_ref):
        def parts(ref, diag):
            res = ([], [])
            for cb in range(SCAN_NBLK):
                for part in range(2):
                    blk = ref[:, (2 * cb + part) * w:(2 * cb + part + 1) * w]
                    if diag:
                        blk = jnp.sum(jnp.where(_state_mask(cb), blk, 0.0).reshape(SSM_GROUPS, SSM_CH, w), axis=0)
                    res[part].append(blk)
            return jnp.concatenate(res[0], axis=1), jnp.concatenate(res[1], axis=1)

        dar, dai = parts(da_ref, False)
        dbbr, dbbi = parts(dbb_ref, True)
        dcr, dci_neg = parts(dc_ref, True)
        gcr_ref[...] = dcr
        gci_ref[...] = -dci_neg
        _, vjp = jax.vjp(_disc, lr_ref[...], li_ref[...], ldt_ref[...], br_ref[...], bi_ref[...])
        glr, gli, gldt, gbr, gbi = vjp((dar, dai, dbbr, dbbi))
        glr_ref[...] = glr
        gli_ref[...] = gli
        gldt_ref[...] = jnp.dot(jnp.broadcast_to(gldt, (8, N_STATE)), ind_ref[...], preferred_element_type=F32,
                                precision=lax.Precision.HIGHEST)
        gbr_ref[...] = gbr
        gbi_ref[...] = gbi

    v1 = jax.ShapeDtypeStruct((1, N_STATE), F32)
    v16 = jax.ShapeDtypeStruct((SSM_CH, N_STATE), F32)
    vdt = jax.ShapeDtypeStruct((8, LANES), F32)
    return _pallas_call(
        body, name="ssm_param_bwd", out_shape=[v1, v1, vdt, v16, v16, v16, v16],
        compiler_params=pltpu.CompilerParams(vmem_limit_bytes=VMEM_BIG),
    )(lr, li, ldt, br, bi, da_cat, dbb_full, dc_full, _group_indicator())


def _cmul(ar, ai, br, bi):
    return ar * br - ai * bi, ar * bi + ai * br


def _scan_fwd(bu, a_cat, n_rows):
    w = SCAN_WC

    def body(bu_ref, a_ref, xs_ref, ein_ref):
        ar = jnp.broadcast_to(a_ref[:, :w], (n_rows, w))
        ai = jnp.broadcast_to(a_ref[:, w:], (n_rows, w))
        zero = jnp.zeros((n_rows, w), F32)

        def step(i, carry, store):
            xr, xi = carry
            blk = bu_ref[i]
            nr = ar * xr - ai * xi + blk[:, :w]
            ni = ar * xi + ai * xr + blk[:, w:]
            if store:
                xs_ref[i, :, :w] = nr
                xs_ref[i, :, w:] = ni
            return nr, ni

        er, ei = lax.fori_loop(0, SCAN_LEN, functools.partial(step, store=False), (zero, zero))
        qr, qi = ar, ai
        for _ in range(int(math.log2(SCAN_LEN))):
            qr, qi = _cmul(qr, qi, qr, qi)
        seg = lax.broadcasted_iota(jnp.int32, (n_rows, w), 0) % SCAN_SEG_PER_SAMPLE
        shift = 1
        while shift < SCAN_SEG_PER_SAMPLE:
            keep = seg >= shift
            sr = jnp.where(keep, pltpu.roll(er, shift, 0), 0.0)
            si = jnp.where(keep, pltpu.roll(ei, shift, 0), 0.0)
            pr, pi = _cmul(qr, qi, sr, si)
            er, ei = er + pr, ei + pi
            qr, qi = _cmul(qr, qi, qr, qi)
            shift *= 2
        cr = jnp.where(seg >= 1, pltpu.roll(er, 1, 0), 0.0)
        ci = jnp.where(seg >= 1, pltpu.roll(ei, 1, 0), 0.0)
        ein_ref[:, :w] = cr
        ein_ref[:, w:] = ci
        lax.fori_loop(0, SCAN_LEN, functools.partial(step, store=True), (cr, ci))

    blk3 = pl.BlockSpec((SCAN_LEN, n_rows, 2 * w), lambda c: (0, 0, c))
    return _pallas_call(
        body, name="ssm_scan_fwd", grid=(SCAN_NBLK,),
        in_specs=[blk3, pl.BlockSpec((1, 2 * w), lambda c: (0, c))],
        out_specs=[blk3, pl.BlockSpec((n_rows, 2 * w), lambda c: (0, c))],
        out_shape=[jax.ShapeDtypeStruct(bu.shape, F32), jax.ShapeDtypeStruct((n_rows, 2 * N_STATE), F32)],
        compiler_params=pltpu.CompilerParams(dimension_semantics=("parallel",), vmem_limit_bytes=VMEM_BIG),
    )(bu, a_cat)


def _scan_bwd(dxs, xs, ein, a_cat, n_rows):
    w = SCAN_WC

    def body(dx_ref, xs_ref, ein_ref, a_ref, g_ref, da_ref):
        ar = jnp.broadcast_to(a_ref[:, :w], (n_rows, w))
        ai = jnp.broadcast_to(a_ref[:, w:], (n_rows, w))
        zero = jnp.zeros((n_rows, w), F32)

        def back(gr, gi, blk):
            return blk[:, :w] + ar * gr + ai * gi, blk[:, w:] + ar * gi - ai * gr

        def step1(ii, carry):
            return back(carry[0], carry[1], dx_ref[SCAN_LEN - 1 - ii])

        sr, si = lax.fori_loop(0, SCAN_LEN, step1, (zero, zero))
        qr, qi = ar, ai
        for _ in range(int(math.log2(SCAN_LEN))):
            qr, qi = _cmul(qr, qi, qr, qi)
        seg = lax.broadcasted_iota(jnp.int32, (n_rows, w), 0) % SCAN_SEG_PER_SAMPLE
        shift = 1
        while shift < SCAN_SEG_PER_SAMPLE:
            keep = seg < SCAN_SEG_PER_SAMPLE - shift
            tr = jnp.where(keep, pltpu.roll(sr, n_rows - shift, 0), 0.0)
            ti = jnp.where(keep, pltpu.roll(si, n_rows - shift, 0), 0.0)
            sr, si = sr + qr * tr + qi * ti, si + qr * ti - qi * tr
            qr, qi = _cmul(qr, qi, qr, qi)
            shift *= 2
        last = seg < SCAN_SEG_PER_SAMPLE - 1
        gr0 = jnp.where(last, pltpu.roll(sr, n_rows - 1, 0), 0.0)
        gi0 = jnp.where(last, pltpu.roll(si, n_rows - 1, 0), 0.0)

        def accum(gr, gi, xpr, xpi, dar, dai):
            return dar + gr * xpr + gi * xpi, dai + gi * xpr - gr * xpi

        def step2(ii, carry):
            gr, gi, dar, dai = carry
            i = SCAN_LEN - 1 - ii
            gr, gi = back(gr, gi, dx_ref[i])
            g_ref[i, :, :w] = gr
            g_ref[i, :, w:] = gi
            xp = xs_ref[i - 1]
            dar, dai = accum(gr, gi, xp[:, :w], xp[:, w:], dar, dai)
            return gr, gi, dar, dai

        gr, gi, dar, dai = lax.fori_loop(0, SCAN_LEN - 1, step2, (gr0, gi0, zero, zero))
        gr, gi = back(gr, gi, dx_ref[0])
        g_ref[0, :, :w] = gr
        g_ref[0, :, w:] = gi
        dar, dai = accum(gr, gi, ein_ref[:, :w], ein_ref[:, w:], dar, dai)
        da_ref[:, :w] = jnp.sum(dar, axis=0, keepdims=True)
        da_ref[:, w:] = jnp.sum(dai, axis=0, keepdims=True)

    blk3 = pl.BlockSpec((SCAN_LEN, n_rows, 2 * w), lambda c: (0, 0, c))
    row = pl.BlockSpec((1, 2 * w), lambda c: (0, c))
    return _pallas_call(
        body, name="ssm_scan_bwd", grid=(SCAN_NBLK,),
        in_specs=[blk3, blk3, pl.BlockSpec((n_rows, 2 * w), lambda c: (0, c)), row],
        out_specs=[blk3, row],
        out_shape=[jax.ShapeDtypeStruct(dxs.shape, F32), jax.ShapeDtypeStruct((1, 2 * N_STATE), F32)],
        compiler_params=pltpu.CompilerParams(dimension_semantics=("parallel",), vmem_limit_bytes=VMEM_BIG),
    )(dxs, xs, ein, a_cat)


def _to_scan_rows(a, n_samples):
    c = a.shape[1]
    return a.reshape(n_samples, SCAN_SEG_PER_SAMPLE, SCAN_LEN, c).transpose(2, 0, 1, 3).reshape(-1, c)


def _from_scan_rows(a, n_samples):
    c = a.shape[1]
    return a.reshape(SCAN_LEN, n_samples, SCAN_SEG_PER_SAMPLE, c).transpose(1, 2, 0, 3).reshape(-1, c)


def _flat_small(small):
    perm_b = lambda a: a.reshape(SSM_GROUPS, SSM_STATE, SSM_CH).transpose(2, 0, 1).reshape(SSM_CH, N_STATE)
    perm_c = lambda a: a.reshape(SSM_GROUPS, SSM_CH, SSM_STATE).transpose(1, 0, 2).reshape(SSM_CH, N_STATE)
    return dict(
        g_mix=small["norm_mix_g"].reshape(1, D_MODEL), g_ffn=small["norm_ffn_g"].reshape(1, D_MODEL),
        g_fin=small["norm_final_g"].reshape(1, D_MODEL),
        lr=small["ssm_a_re"].reshape(1, N_STATE), li=small["ssm_a_im"].reshape(1, N_STATE),
        ldt=jnp.repeat(small["ssm_log_dt"].reshape(SSM_GROUPS), SSM_STATE).reshape(1, N_STATE),
        br=perm_b(small["ssm_b_re"]), bi=perm_b(small["ssm_b_im"]),
        cr=perm_c(small["ssm_c_re"]), ci=perm_c(small["ssm_c_im"]), dskip=small["ssm_d"].reshape(1, SSM_W))


AG_HOSTS = {"mm_proj": ("w_glu", "w_attn_out", "w_out", "w_gu"), "mm_bu": ("w_ffn_down",)}
A2A_HOSTS = {"mm_d_h2": ("w_ffn_down",), "mm_g_bb": ("w_ffn_gate",), "mm_g_c": ("w_ffn_up",),
             "mm_g_in": ("w_out", "w_attn_out", "w_glu"), "mm_d_h0": ("w_in",)}
GRAD_SOURCE = {"w_ffn_gate": ("w_gu", 0), "w_ffn_up": ("w_gu", N_DEV)}


def _local_step(x, target, w, small, shards=None):
    t = x.shape[0]
    n_samples = t // SEQ
    n_rows = n_samples * SCAN_SEG_PER_SAMPLE
    tabs = _rope_tables()
    w = dict(w)
    fs = _flat_small(small)
    g_mix, g_ffn, g_fin, dskip = fs["g_mix"], fs["g_ffn"], fs["g_fin"], fs["dskip"]
    a_cat, bb_cat, c_cat = _ssm_disc(fs["lr"], fs["li"], fs["ldt"], fs["br"], fs["bi"], fs["cr"], fs["ci"])
    big, recv = {}, {}

    def mm(a, b, mode, name, tm, tn, **kw):
        if shards is None or (name not in AG_HOSTS and name not in A2A_HOSTS):
            return _mm(a, b, mode, name, tm, tn, **kw)
        if name in AG_HOSTS:
            names = AG_HOSTS[name]
            items, bufs = [], []
            for n in names:
                parts = ("w_ffn_gate", "w_ffn_up") if n == "w_gu" else (n,)
                c, k = shards[parts[0]].shape
                for j, p in enumerate(parts):
                    items.append((shards[p], len(bufs), j * N_DEV))
                bufs.append((len(parts) * N_DEV, c, k))
            out, *gathered = _mm(a, b, mode, name, tm, tn, comm=_ag_comm(items, bufs), **kw)
            for n, g3 in zip(names, gathered):
                w[n] = g3.reshape(-1, g3.shape[2])
            return out
        items = []
        for n in A2A_HOSTS[name]:
            src, slot0 = GRAD_SOURCE.get(n, (n, 0))
            c, k = shards[n].shape
            items.append((big[src].reshape(-1, c, k), slot0))
        out, *got = _mm(a, b, mode, name, tm, tn, comm=_a2a_comm(items), **kw)
        for n, r3 in zip(A2A_HOSTS[name], got):
            recv[n] = r3
        return out

    h0 = _rms_fwd(x, g_mix, "rms_mix")
    proj = mm(h0, w["w_in"], "nt", "mm_proj", 512, IN_W // 2)
    q, k, v, u, gates = _rope_split(proj, tabs)
    os_, lses = [], []
    for g in range(3):
        o_g, l_g = _attn_fwd(q, k, v, g, n_samples)
        os_.append(o_g)
        lses.append(l_g)
    attn, lse_tot = _attn_merge(os_, lses)
    attn_d = mm(attn, w["w_attn_out"], "nt", "mm_attn_out", 512, D_MODEL)

    u_perm = _to_scan_rows(u, n_samples)
    bu = mm(u_perm, bb_cat, "nn", "mm_bu", 512, N_STATE)
    xs3, ein = _scan_fwd(bu.reshape(SCAN_LEN, n_rows, 2 * N_STATE), a_cat, n_rows)
    xs = xs3.reshape(t, 2 * N_STATE)
    ypre = mm(xs, c_cat, "nt", "mm_ssm_y", 256, SSM_W)
    ytot, yg_perm = _ssm_act(ypre, u_perm, dskip)
    yg = _from_scan_rows(yg_perm, n_samples)
    z = mm(yg, w["w_glu"], "nt", "mm_glu", 512, 2 * D_MODEL)

    merged = _mix(attn_d, z, gates)
    x1 = mm(merged, w["w_out"], "nn", "mm_out", 512, D_MODEL, add=x)
    h2 = _rms_fwd(x1, g_ffn, "rms_ffn")
    ab = mm(h2, w["w_gu"], "nt", "mm_ffn_in", 512, D_FF)
    f = _swiglu(ab)
    x2 = mm(f, w["w_ffn_down"], "nn", "mm_ffn_down", 512, D_MODEL, add=x1)
    dx2, dx2b, loss_blk, g_gfin = _final(x2, target, g_fin)

    df = mm(dx2b, w["w_ffn_down"], "nt", "mm_d_f", 512, D_FF)
    dab = _swiglu_bwd(df, ab)
    big["w_ffn_down"] = mm(f, dx2b, "tn", "mm_g_down", 256, 512, out_dtype=BF16)
    big["w_gu"] = mm(dab, h2, "tn", "mm_g_gu", 512, 512, out_dtype=BF16)
    dh2 = mm(dab, w["w_gu"], "nn", "mm_d_h2", 512, 512)
    dx1, dx1b, g_gffn = _rms_bwd(x1, g_ffn, dh2, dx2, "rms_ffn_bwd")

    dmerged = mm(dx1b, w["w_out"], "nt", "mm_d_merged", 512, D_MODEL)
    big["w_out"] = mm(merged, dx1b, "tn", "mm_g_out", 512, 512, out_dtype=BF16)
    dattn_d, dz, dgpre = _mix_bwd(dmerged, gates, attn_d, z)

    dattn = mm(dattn_d, w["w_attn_out"], "nn", "mm_d_attn", 512, GROUP_W)
    big["w_attn_out"] = mm(dattn_d, attn, "tn", "mm_g_attn_out", 512, GROUP_W, out_dtype=BF16)
    rowdot = _attn_rowdot(dattn, attn)
    dqs, dks, dvs = [], [], []
    for g in range(3):
        dq_g, dk_g, dv_g = _attn_bwd(q, k, v, dattn, lse_tot, rowdot, g, n_samples)
        dqs.append(dq_g)
        dks.append(dk_g)
        dvs.append(dv_g)

    dyg = mm(dz, w["w_glu"], "nn", "mm_d_yg", 512, SSM_W)
    big["w_glu"] = mm(dz, yg, "tn", "mm_g_glu", 512, 512, out_dtype=BF16)
    dyg_perm = _to_scan_rows(dyg, n_samples)
    dypre, du_skip, g_dskip = _ssm_act_bwd(dyg_perm, ytot, u_perm, dskip)
    dxs = mm(dypre, c_cat, "nn", "mm_d_xs", 512, N_STATE)
    gs3, da_cat = _scan_bwd(dxs.reshape(SCAN_LEN, n_rows, 2 * N_STATE), xs3, ein, a_cat, n_rows)
    gs = gs3.reshape(t, 2 * N_STATE)
    du_perm = mm(gs, bb_cat, "nt", "mm_d_u", 256, SSM_W, add=du_skip)
    du = _from_scan_rows(du_perm, n_samples)
    dbb_full = mm(u_perm, gs, "tn", "mm_g_bb", 256, 256)
    dc_full = mm(dypre, xs, "tn", "mm_g_c", 256, 256)
    g_lr, g_li, g_ldt, g_br, g_bi, g_cr, g_ci = _ssm_param_bwd(
        fs["lr"], fs["li"], fs["ldt"], fs["br"], fs["bi"], da_cat, dbb_full, dc_full)

    dproj = _pack_dproj(dqs, dks, dvs, du, dgpre, tabs)
    big["w_in"] = mm(dproj, h0, "tn", "mm_g_in", 256, 512, out_dtype=BF16)
    dh0 = mm(dproj, w["w_in"], "nn", "mm_d_h0", 512, 512)
    grad_x, _, g_gmix = _rms_bwd(x, g_mix, dh0, dx1, "rms_mix_bwd")

    small_g = dict(lr=g_lr, li=g_li, ldt=g_ldt, br=g_br, bi=g_bi, cr=g_cr, ci=g_ci, dskip=g_dskip,
                   g_mix=g_gmix, g_ffn=g_gffn, g_fin=g_gfin, loss=loss_blk)
    return loss_blk, grad_x, (big if shards is None else recv), small_g


_MESH = pl.DeviceIdType.MESH


def _all_gather(block, name):
    rows, lanes = block.shape

    def body(x_ref, out_ref, send_sems, recv_sems, local_sem):
        x, y, c = lax.axis_index("x"), lax.axis_index("y"), lax.axis_index("c")
        me, sibling = (x, y, c), (x, y, 1 - c)
        chips = [(1 - x, y), (x, 1 - y), (1 - x, 1 - y)]

        def slot(px, py, pc):
            return out_ref.at[4 * px + 2 * py + pc]

        def copy(k, blk, to, src=None):
            return pltpu.make_async_remote_copy(
                src_ref=slot(*blk) if src is None else src, dst_ref=slot(*blk), send_sem=send_sems.at[k],
                recv_sem=recv_sems.at[k], device_id=to, device_id_type=_MESH)

        mine = pltpu.make_async_copy(x_ref, slot(*me), local_sem)
        mine.start()
        first = [copy(0, me, sibling, src=x_ref)]
        first += [copy(1 + j, me, (*chip, c), src=x_ref) for j, chip in enumerate(chips)]
        for cp in first:
            cp.start()
        passed = [copy(4 + j, (*chip, c), sibling) for j, chip in enumerate(chips)]
        for j, chip in enumerate(chips):
            copy(1 + j, (*chip, c), me).wait_recv()
            passed[j].start()
        copy(0, sibling, me).wait_recv()
        for j, chip in enumerate(chips):
            copy(4 + j, (*chip, 1 - c), me).wait_recv()
        for cp in first + passed:
            cp.wait_send()
        mine.wait()

    return _pallas_call(
        body, name=name, out_shape=jax.ShapeDtypeStruct((N_DEV, rows, lanes), block.dtype),
        in_specs=[pl.BlockSpec(memory_space=pl.ANY)], out_specs=pl.BlockSpec(memory_space=pl.ANY),
        scratch_shapes=[pltpu.SemaphoreType.DMA((7,)), pltpu.SemaphoreType.DMA((7,)), pltpu.SemaphoreType.DMA],
    )(block)


def _ag_comm(items, bufs):
    def plan(in_refs, out_refs, send_sems, recv_sems, local_sems):
        x, y, c = lax.axis_index("x"), lax.axis_index("y"), lax.axis_index("c")
        me, sibling = (x, y, c), (x, y, 1 - c)
        chips = [(1 - x, y), (x, 1 - y), (1 - x, 1 - y)]
        plans = []
        for t, (_, buf, slot0) in enumerate(items):
            x_ref, out_ref = in_refs[t], out_refs[buf]

            def slot(px, py, pc, out_ref=out_ref, slot0=slot0):
                return out_ref.at[slot0 + 4 * px + 2 * py + pc]

            def copy(k, blk, to, src=None, t=t, slot=slot):
                return pltpu.make_async_remote_copy(
                    src_ref=slot(*blk) if src is None else src, dst_ref=slot(*blk), send_sem=send_sems.at[7 * t + k],
                    recv_sem=recv_sems.at[7 * t + k], device_id=to, device_id_type=_MESH)

            plans.append(dict(
                mine=pltpu.make_async_copy(x_ref, slot(*me), local_sems.at[t]),
                first=[copy(0, me, sibling, src=x_ref)] + [copy(1 + j, me, (*chip, c), src=x_ref)
                                                           for j, chip in enumerate(chips)],
                passed=[copy(4 + j, (*chip, c), sibling) for j, chip in enumerate(chips)],
                from_ici=[copy(1 + j, (*chip, c), me) for j, chip in enumerate(chips)],
                from_sibling=[copy(0, sibling, me)] + [copy(4 + j, (*chip, 1 - c), me) for j, chip in enumerate(chips)]))
        return plans

    def start(*refs):
        for p in plan(*refs):
            p["mine"].start()
            for cp in p["first"]:
                cp.start()

    def finish(*refs):
        plans = plan(*refs)
        for p in plans:
            for arrived, onward in zip(p["from_ici"], p["passed"]):
                arrived.wait_recv()
                onward.start()
        for p in plans:
            for arrived in p["from_sibling"]:
                arrived.wait_recv()
            for cp in p["first"] + p["passed"]:
                cp.wait_send()
            p["mine"].wait()

    out_shapes = [jax.ShapeDtypeStruct(b, items[0][0].dtype) for b in bufs]
    return _Comm([it[0] for it in items], out_shapes, 7 * len(items), len(items), start, finish)


def _a2a_comm(items):
    def plan(in_refs, out_refs, send_sems, recv_sems, local_sems):
        x, y, c = lax.axis_index("x"), lax.axis_index("y"), lax.axis_index("c")
        my = 4 * x + 2 * y + c
        copies, locals_ = [], []
        for t, (_, slot0) in enumerate(items):
            s_ref, r_ref = in_refs[t], out_refs[t]
            locals_.append(pltpu.make_async_copy(s_ref.at[slot0 + my], r_ref.at[my], local_sems.at[t]))
            for kk in range(1, N_DEV):
                px = 1 - x if kk & 4 else x
                py = 1 - y if kk & 2 else y
                pc = 1 - c if kk & 1 else c
                copies.append(pltpu.make_async_remote_copy(
                    src_ref=s_ref.at[slot0 + 4 * px + 2 * py + pc], dst_ref=r_ref.at[my],
                    send_sem=send_sems.at[7 * t + kk - 1], recv_sem=recv_sems.at[7 * t + kk - 1],
                    device_id=(px, py, pc), device_id_type=_MESH))
        return copies, locals_

    def start(*refs):
        copies, locals_ = plan(*refs)
        for cp in locals_ + copies:
            cp.start()

    def finish(*refs):
        copies, locals_ = plan(*refs)
        for cp in copies + locals_:
            cp.wait()

    out_shapes = [jax.ShapeDtypeStruct((N_DEV,) + it[0].shape[1:], it[0].dtype) for it in items]
    return _Comm([it[0] for it in items], out_shapes, 7 * len(items), len(items), start, finish)


def _adam_math(g, w, m, v):
    m_new = ADAM_B1 * m + (1.0 - ADAM_B1) * g
    v_new = ADAM_B2 * v + (1.0 - ADAM_B2) * jnp.square(g)
    m_hat = m_new / (1.0 - ADAM_B1 ** ADAM_STEP)
    v_hat = v_new / (1.0 - ADAM_B2 ** ADAM_STEP)
    return -ADAM_LR * (m_hat / (jnp.sqrt(v_hat) + ADAM_EPS) + ADAM_WD * w), m_new, v_new


def _sum_partials(partials, name, tm):
    n, rows, cols = partials.shape

    def body(p_ref, g_ref):
        g = p_ref[0].astype(F32)
        for s in range(1, n):
            g = g + p_ref[s].astype(F32)
        g_ref[...] = g

    return _pallas_call(
        body, name=name, grid=(rows // tm,), in_specs=[pl.BlockSpec((n, tm, cols), lambda i: (0, i, 0))],
        out_specs=pl.BlockSpec((tm, cols), lambda i: (i, 0)), out_shape=jax.ShapeDtypeStruct((rows, cols), F32),
        compiler_params=pltpu.CompilerParams(dimension_semantics=("parallel",), vmem_limit_bytes=VMEM_MID),
    )(partials)


def _adam(partials, w, m, v, name, tm):
    n, rows, cols = partials.shape

    def body(p_ref, w_ref, m_ref, v_ref, g_ref, d_ref, nm_ref, nv_ref):
        g = p_ref[0].astype(F32)
        for s in range(1, n):
            g = g + p_ref[s].astype(F32)
        g_ref[...] = g
        d_ref[...], nm_ref[...], nv_ref[...] = _adam_math(g, w_ref[...], m_ref[...], v_ref[...])

    assert rows % tm == 0
    row = pl.BlockSpec((tm, cols), lambda i: (i, 0))
    shp = jax.ShapeDtypeStruct((rows, cols), F32)
    return _pallas_call(
        body, name=name, grid=(rows // tm,),
        in_specs=[pl.BlockSpec((n, tm, cols), lambda i: (0, i, 0)), row, row, row],
        out_specs=[row] * 4, out_shape=[shp] * 4,
        compiler_params=pltpu.CompilerParams(dimension_semantics=("parallel",), vmem_limit_bytes=VMEM_MID),
    )(partials, w, m, v)


_PK_LR, _PK_LI, _PK_GAINS, _PK_MISC, _PK_BR, _PK_BI, _PK_CR, _PK_CI, _PK_ROWS = 0, 1, 2, 3, 8, 24, 40, 56, 72
_PK_LDT_LANE, _PK_LOSS_LANE = D_MODEL + SSM_W, D_MODEL + SSM_W + LANES


def _pack_small(sg):
    names = ("lr", "li", "g_mix", "g_ffn", "g_fin", "dskip", "ldt", "loss", "br", "bi", "cr", "ci")

    def body(lr, li, gmix, gffn, gfin, dskip, ldt, loss, br, bi, cr, ci, o_ref):
        o_ref[...] = jnp.zeros_like(o_ref)
        o_ref[_PK_LR:_PK_LR + 1, :] = lr[...]
        o_ref[_PK_LI:_PK_LI + 1, :] = li[...]
        o_ref[_PK_GAINS:_PK_GAINS + 1, :D_MODEL] = gmix[...]
        o_ref[_PK_GAINS:_PK_GAINS + 1, D_MODEL:] = gffn[...]
        o_ref[_PK_MISC:_PK_MISC + 1, :D_MODEL] = gfin[...]
        o_ref[_PK_MISC:_PK_MISC + 1, D_MODEL:D_MODEL + SSM_W] = dskip[...]
        o_ref[_PK_MISC:_PK_MISC + 1, _PK_LDT_LANE:_PK_LDT_LANE + LANES] = ldt[0:1, :]
        o_ref[_PK_MISC:_PK_MISC + 1, _PK_LOSS_LANE:_PK_LOSS_LANE + LANES] = loss[0:1, :]
        o_ref[_PK_BR:_PK_BR + SSM_CH, :] = br[...]
        o_ref[_PK_BI:_PK_BI + SSM_CH, :] = bi[...]
        o_ref[_PK_CR:_PK_CR + SSM_CH, :] = cr[...]
        o_ref[_PK_CI:_PK_CI + SSM_CH, :] = ci[...]

    return _pallas_call(body, name="pack_small", out_shape=jax.ShapeDtypeStruct((_PK_ROWS, N_STATE), F32))(
        *[sg[n] for n in names])


def _unpack_small(s):
    unflat_b = lambda a: a.reshape(SSM_CH, SSM_GROUPS, SSM_STATE).transpose(1, 2, 0)[None]
    unflat_c = lambda a: a.reshape(SSM_CH, SSM_GROUPS, SSM_STATE).transpose(1, 0, 2)[None]
    grads = {
        "norm_mix_g": s[_PK_GAINS, :D_MODEL].reshape(1, D_MODEL), "norm_ffn_g": s[_PK_GAINS, D_MODEL:].reshape(1, D_MODEL),
        "norm_final_g": s[_PK_MISC, :D_MODEL],
        "ssm_a_re": s[_PK_LR].reshape(1, SSM_GROUPS, SSM_STATE), "ssm_a_im": s[_PK_LI].reshape(1, SSM_GROUPS, SSM_STATE),
        "ssm_log_dt": s[_PK_MISC, _PK_LDT_LANE:_PK_LDT_LANE + SSM_GROUPS].reshape(1, SSM_GROUPS),
        "ssm_d": s[_PK_MISC, D_MODEL:D_MODEL + SSM_W].reshape(1, SSM_GROUPS, SSM_CH),
        "ssm_b_re": unflat_b(s[_PK_BR:_PK_BR + SSM_CH]), "ssm_b_im": unflat_b(s[_PK_BI:_PK_BI + SSM_CH]),
        "ssm_c_re": unflat_c(s[_PK_CR:_PK_CR + SSM_CH]), "ssm_c_im": unflat_c(s[_PK_CI:_PK_CI + SSM_CH]),
    }
    return s[_PK_MISC, _PK_LOSS_LANE], grads


def _adam_small(grads, wts, moms, vars_):
    n = len(SMALL_WEIGHTS)
    as2d = lambda a: a.reshape(1, -1) if a.ndim == 1 else a

    def body(*refs):
        ins, outs = refs[:4 * n], refs[4 * n:]
        for i in range(n):
            g, w, m, v = (ins[j * n + i][...] for j in range(4))
            outs[i][...], outs[n + i][...], outs[2 * n + i][...] = _adam_math(g, w, m, v)

    operands = [as2d(d[k]) for d in (grads, wts, moms, vars_) for k in SMALL_WEIGHTS]
    shapes = [jax.ShapeDtypeStruct(as2d(wts[k]).shape, F32) for k in SMALL_WEIGHTS] * 3
    res = _pallas_call(body, name="adam_small", out_shape=shapes,
                         compiler_params=pltpu.CompilerParams(vmem_limit_bytes=VMEM_BIG))(*operands)
    out = {}
    for j, kind in enumerate(("delta", "new_m", "new_v")):
        for i, k in enumerate(SMALL_WEIGHTS):
            out[kind, k] = res[j * n + i].reshape(wts[k].shape)
    return out


def kernel(x, norm_mix_g, w_in, ssm_a_re, ssm_a_im, ssm_log_dt, ssm_b_re, ssm_b_im, ssm_c_re, ssm_c_im, ssm_d, w_glu, w_attn_out, w_out, norm_ffn_g, w_ffn_gate, w_ffn_up, w_ffn_down, norm_final_g, loss_target, m_norm_mix_g, m_w_in, m_ssm_a_re, m_ssm_a_im, m_ssm_log_dt, m_ssm_b_re, m_ssm_b_im, m_ssm_c_re, m_ssm_c_im, m_ssm_d, m_w_glu, m_w_attn_out, m_w_out, m_norm_ffn_g, m_w_ffn_gate, m_w_ffn_up, m_w_ffn_down, m_norm_final_g, v_norm_mix_g, v_w_in, v_ssm_a_re, v_ssm_a_im, v_ssm_log_dt, v_ssm_b_re, v_ssm_b_im, v_ssm_c_re, v_ssm_c_im, v_ssm_d, v_w_glu, v_w_attn_out, v_w_out, v_norm_ffn_g, v_w_ffn_gate, v_w_ffn_up, v_w_ffn_down, v_norm_final_g):
    args = dict(locals())
    wts = {n: args[n] for n in ALL_WEIGHTS}
    moms = {n: args["m_" + n] for n in ALL_WEIGHTS}
    vars_ = {n: args["v_" + n] for n in ALL_WEIGHTS}
    n_samples = x.shape[0]
    t = n_samples * SEQ

    shards = {n: (wts[n][0] if n in ROW_SHARDED else wts[n][0].T).astype(BF16) for n in BIG_WEIGHTS}
    w_in_t = _all_gather(shards["w_in"], "allgather_w_in").reshape(IN_W, D_MODEL)

    small = {n: wts[n] for n in SMALL_WEIGHTS}
    _, grad_x, recv, small_g = _local_step(x.reshape(t, D_MODEL), loss_target.reshape(t, D_MODEL), {"w_in": w_in_t},
                                           small, shards)

    results = {}
    for n in BIG_WEIGHTS:
        c, k = shards[n].shape
        w2, m2, v2 = wts[n][0], moms[n][0], vars_[n][0]
        if n in ROW_SHARDED:
            res = _adam(recv[n], w2, m2, v2, "adam_" + n, c // 2)
        else:
            g_t = _sum_partials(recv[n], "sum_" + n, c // 2)
            res = _adam(g_t.T[None], w2, m2, v2, "adam_" + n, k // 2)
        for kind, a in zip(("grad", "delta", "new_m", "new_v"), res):
            results[kind, n] = a[None]

    sgath = _all_gather(_pack_small(small_g), "allgather_small_grads")
    loss, sgrads = _unpack_small(_sum_partials(sgath, "sum_small", _PK_ROWS))
    for n in SMALL_WEIGHTS:
        results["grad", n] = sgrads[n]
    results.update(_adam_small(sgrads, wts, moms, vars_))
    outs = [loss, grad_x.reshape(x.shape)]
    for kind in ("grad", "delta", "new_m", "new_v"):
        outs += [results[kind, n] for n in ALL_WEIGHTS]
    return tuple(outs)
```

```python
import functools
import math

import jax
import jax.numpy as jnp
from jax import lax
from jax.experimental import pallas as pl
from jax.experimental.pallas import tpu as pltpu

F32 = jnp.float32
BF16 = jnp.bfloat16
MXU_DTYPE = jnp.bfloat16

N_DEV = 8
D_MODEL = 1024
SEQ = 2048
HEAD_DIM = 64
HEADS_PER_GROUP = 4
GROUP_W = HEADS_PER_GROUP * HEAD_DIM
DILATIONS = (1, 4, 16)
QKV_W = 3 * len(DILATIONS) * GROUP_W
Q_W = len(DILATIONS) * GROUP_W
ATT_BLOCK = 128
ROPE_DIM = 16
ROPE_THETA = 500000.0
SSM_W = 512
SSM_GROUPS = 32
SSM_CH = 16
SSM_STATE = 64
N_STATE = SSM_GROUPS * SSM_STATE
D_FF = 2816
IN_W = QKV_W + SSM_W + 2 * D_MODEL
RMS_EPS = 1e-6
NEG_INF = -1e30
LANES = 128

SCAN_SEG_PER_SAMPLE = 8
SCAN_LEN = SEQ // SCAN_SEG_PER_SAMPLE
SCAN_WC = 512
SCAN_NBLK = N_STATE // SCAN_WC
SCAN_CH = SSM_W // SCAN_NBLK
SCAN_CHUNK = 32

ADAM_LR = 0.001
ADAM_B1 = 0.9
ADAM_B2 = 0.999
ADAM_EPS = 1e-08
ADAM_WD = 0.01
ADAM_STEP = 10

VMEM_BIG = 48 * 1024 * 1024
VMEM_MID = 32 * 1024 * 1024

BIG_WEIGHTS = ("w_in", "w_glu", "w_attn_out", "w_out", "w_ffn_gate", "w_ffn_up", "w_ffn_down")
ROW_SHARDED = ("w_out", "w_ffn_down")
SMALL_WEIGHTS = ("norm_mix_g", "ssm_a_re", "ssm_a_im", "ssm_log_dt", "ssm_b_re", "ssm_b_im", "ssm_c_re", "ssm_c_im",
                 "ssm_d", "norm_ffn_g", "norm_final_g")
ALL_WEIGHTS = ("norm_mix_g", "w_in", "ssm_a_re", "ssm_a_im", "ssm_log_dt", "ssm_b_re", "ssm_b_im", "ssm_c_re", "ssm_c_im",
               "ssm_d", "w_glu", "w_attn_out", "w_out", "norm_ffn_g", "w_ffn_gate", "w_ffn_up", "w_ffn_down", "norm_final_g")


def _sigmoid(x):
    return 1.0 / (1.0 + jnp.exp(-x))


def _pallas_call(body, *, out_shape, **kw):
    single = not isinstance(out_shape, (list, tuple))
    shapes = [pltpu.HBM(s.shape, s.dtype) for s in ([out_shape] if single else out_shape)]
    call = pl.pallas_call(body, out_shape=shapes[0] if single else shapes, **kw)
    return lambda *operands: call(*[pltpu.with_memory_space_constraint(o, pltpu.HBM) for o in operands])


class _Comm:
    def __init__(self, ins, out_shapes, n_sem, n_local, start, finish):
        self.ins, self.out_shapes, self.n_sem, self.n_local = ins, out_shapes, n_sem, n_local
        self.start, self.finish = start, finish


def _mm(a, b, mode, name, tm, tn, out_dtype=F32, add=None, vmem=VMEM_BIG, comm=None):
    if mode == "nn":
        (m, k), (_, n) = a.shape, b.shape
        a_spec = pl.BlockSpec((tm, k), lambda i, j: (i, 0))
        b_spec = pl.BlockSpec((k, tn), lambda i, j: (0, j))
        dims = (((1,), (0,)), ((), ()))
    elif mode == "nt":
        (m, k), (n, _) = a.shape, b.shape
        a_spec = pl.BlockSpec((tm, k), lambda i, j: (i, 0))
        b_spec = pl.BlockSpec((tn, k), lambda i, j: (j, 0))
        dims = (((1,), (1,)), ((), ()))
    else:
        (k, m), (_, n) = a.shape, b.shape
        a_spec = pl.BlockSpec((k, tm), lambda i, j: (0, i))
        b_spec = pl.BlockSpec((k, tn), lambda i, j: (0, j))
        dims = (((0,), (0,)), ((), ()))
    assert m % tm == 0 and n % tn == 0, (name, m, n, tm, tn)
    o_spec = pl.BlockSpec((tm, tn), lambda i, j: (i, j))
    has_add = add is not None

    def body(*refs):
        a_ref, b_ref, o_ref = refs[0], refs[1], refs[-1]
        acc = lax.dot_general(a_ref[...].astype(MXU_DTYPE), b_ref[...].astype(MXU_DTYPE), dims,
                              preferred_element_type=F32)
        if has_add:
            acc = acc + refs[2][...]
        o_ref[...] = acc.astype(out_dtype)

    ins = [a, b] + ([add] if has_add else [])
    in_specs = [a_spec, b_spec] + ([o_spec] if has_add else [])
    return _grid_call(body, name, (m // tm, n // tn), ins, in_specs, [o_spec],
                      [jax.ShapeDtypeStruct((m, n), out_dtype)], vmem, comm)


def _grid_call(body, name, grid, ins, in_specs, out_specs, out_shapes, vmem, comm=None):
    if comm is None:
        single = len(out_shapes) == 1
        return _pallas_call(
            body, name=name, grid=grid, in_specs=in_specs, out_specs=out_specs[0] if single else out_specs,
            out_shape=out_shapes[0] if single else out_shapes,
            compiler_params=pltpu.CompilerParams(dimension_semantics=("parallel", "parallel"), vmem_limit_bytes=vmem),
        )(*ins)
    n_in, n_out, n_cin, n_cout = len(ins), len(out_shapes), len(comm.ins), len(comm.out_shapes)

    def carrying(*refs):
        own = refs[:n_in] + refs[n_in + n_cin:n_in + n_cin + n_out]
        c_args = (refs[n_in:n_in + n_cin], refs[n_in + n_cin + n_out:n_in + n_cin + n_out + n_cout], *refs[-3:])

        @pl.when((pl.program_id(0) == 0) & (pl.program_id(1) == 0))
        def _():
            comm.start(*c_args)

        body(*own)

        @pl.when((pl.program_id(0) == grid[0] - 1) & (pl.program_id(1) == grid[1] - 1))
        def _():
            comm.finish(*c_args)

    hbm = pl.BlockSpec(memory_space=pl.ANY)
    return _pallas_call(
        carrying, name=name, grid=grid, in_specs=list(in_specs) + [hbm] * n_cin,
        out_specs=list(out_specs) + [hbm] * n_cout, out_shape=list(out_shapes) + list(comm.out_shapes),
        scratch_shapes=[pltpu.SemaphoreType.DMA((comm.n_sem,)), pltpu.SemaphoreType.DMA((comm.n_sem,)),
                        pltpu.SemaphoreType.DMA((comm.n_local,))],
        compiler_params=pltpu.CompilerParams(dimension_semantics=("arbitrary", "arbitrary"), vmem_limit_bytes=vmem),
    )(*ins, *comm.ins)


def _rows(body, name, n_rows, tm, ins, outs, vmem=VMEM_MID):
    assert n_rows % tm == 0
    arrays, in_specs = [], []
    for kind, arr in ins:
        arrays.append(arr)
        if kind == "row":
            assert arr.shape[0] == n_rows, (name, arr.shape)
            in_specs.append(pl.BlockSpec((tm, arr.shape[1]), lambda i: (i, 0)))
        elif kind == "tab":
            nblk = arr.shape[0] // tm
            in_specs.append(pl.BlockSpec((tm, arr.shape[1]), lambda i, nblk=nblk: (i % nblk, 0)))
        else:
            in_specs.append(pl.BlockSpec(arr.shape, lambda i, nd=arr.ndim: (0,) * nd))
    out_specs, out_shape = [], []
    for kind, shp, dt in outs:
        if kind == "row":
            out_specs.append(pl.BlockSpec((tm, shp), lambda i: (i, 0)))
            out_shape.append(jax.ShapeDtypeStruct((n_rows, shp), dt))
        else:
            out_specs.append(pl.BlockSpec(shp, lambda i, nd=len(shp): (0,) * nd))
            out_shape.append(jax.ShapeDtypeStruct(shp, dt))
    res = _pallas_call(
        body, name=name, grid=(n_rows // tm,), in_specs=in_specs, out_specs=out_specs, out_shape=out_shape,
        compiler_params=pltpu.CompilerParams(dimension_semantics=("arbitrary",), vmem_limit_bytes=vmem),
    )(*arrays)
    return res


def _first_step():
    return pl.program_id(0) == 0


def _rms_fwd(x, g, name):
    def body(x_ref, g_ref, h_ref):
        xv = x_ref[...]
        r = lax.rsqrt(jnp.mean(xv * xv, axis=-1, keepdims=True) + RMS_EPS)
        h_ref[...] = ((xv * r) * g_ref[...]).astype(BF16)

    return _rows(body, name, x.shape[0], 512, [("row", x), ("const", g)], [("row", x.shape[1], BF16)])[0]


def _rms_bwd(x, g, dh, dres, name):
    def body(x_ref, g_ref, dh_ref, dres_ref, dx_ref, dxb_ref, gg_ref):
        @pl.when(_first_step())
        def _():
            gg_ref[...] = jnp.zeros_like(gg_ref)

        xv = x_ref[...]
        r = lax.rsqrt(jnp.mean(xv * xv, axis=-1, keepdims=True) + RMS_EPS)
        n = xv * r
        dh_v = dh_ref[...]
        gg_ref[...] += jnp.sum(dh_v * n, axis=0, keepdims=True)
        dn = dh_v * g_ref[...]
        dx = dres_ref[...] + r * (dn - n * jnp.mean(dn * n, axis=-1, keepdims=True))
        dx_ref[...] = dx
        dxb_ref[...] = dx.astype(BF16)

    d = x.shape[1]
    return _rows(body, name, x.shape[0], 256, [("row", x), ("const", g), ("row", dh), ("row", dres)],
                 [("row", d, F32), ("row", d, BF16), ("acc", (1, d), F32)])


def _rope_tables():
    half = ROPE_DIM // 2
    inv = jnp.power(jnp.float32(ROPE_THETA), -jnp.arange(half, dtype=F32) * 2.0 / ROPE_DIM)
    ang = jnp.arange(SEQ, dtype=F32)[:, None] * inv[None, :]
    lane = jnp.arange(LANES) % HEAD_DIM
    cosl = jnp.cos(ang)[:, lane % half]
    sinl = jnp.sin(ang)[:, lane % half]
    tab_c = jnp.where(lane < ROPE_DIM, cosl, 1.0)
    tab_lo = jnp.where(lane < half, -sinl, 0.0)
    tab_hi = jnp.where((lane >= half) & (lane < ROPE_DIM), sinl, 0.0)
    return tab_c.astype(F32), tab_lo.astype(F32), tab_hi.astype(F32)


def _rope_apply(t, tc, tlo, thi):
    half = ROPE_DIM // 2
    return t * tc + pltpu.roll(t, LANES - half, 1) * tlo + pltpu.roll(t, half, 1) * thi


def _rope_transpose(dt, tc, tlo, thi):
    half = ROPE_DIM // 2
    return dt * tc + pltpu.roll(dt * tlo, half, 1) + pltpu.roll(dt * thi, LANES - half, 1)


def _rope_split(proj, tabs):
    def body(p_ref, tc_ref, tlo_ref, thi_ref, q_ref, k_ref, v_ref, u_ref, g_ref):
        tc, tlo, thi = tc_ref[...], tlo_ref[...], thi_ref[...]
        for ch in range(Q_W // LANES):
            sl = slice(ch * LANES, (ch + 1) * LANES)
            q_ref[:, sl] = _rope_apply(p_ref[:, sl], tc, tlo, thi).astype(BF16)
            k_ref[:, sl] = _rope_apply(p_ref[:, Q_W + ch * LANES:Q_W + (ch + 1) * LANES], tc, tlo, thi).astype(BF16)
        v_ref[...] = p_ref[:, 2 * Q_W:QKV_W].astype(BF16)
        u_ref[...] = p_ref[:, QKV_W:QKV_W + SSM_W]
        g_ref[...] = _sigmoid(p_ref[:, QKV_W + SSM_W:])

    t = proj.shape[0]
    return _rows(body, "rope_split", t, 256,
                 [("row", proj), ("tab", tabs[0]), ("tab", tabs[1]), ("tab", tabs[2])],
                 [("row", Q_W, BF16), ("row", Q_W, BF16), ("row", Q_W, BF16), ("row", SSM_W, F32),
                  ("row", 2 * D_MODEL, F32)])


def _pack_dproj(dqs, dks, dvs, du, dgpre, tabs):
    def body(*refs):
        dq_refs, dk_refs, dv_refs = refs[0:3], refs[3:6], refs[6:9]
        du_ref, dg_ref, tc_ref, tlo_ref, thi_ref, o_ref = refs[9:15]
        tc, tlo, thi = tc_ref[...], tlo_ref[...], thi_ref[...]
        for g in range(3):
            for half in range(GROUP_W // LANES):
                sl = slice(half * LANES, (half + 1) * LANES)
                col = g * GROUP_W + half * LANES
                o_ref[:, col:col + LANES] = _rope_transpose(dq_refs[g][:, sl], tc, tlo, thi).astype(BF16)
                o_ref[:, Q_W + col:Q_W + col + LANES] = _rope_transpose(dk_refs[g][:, sl], tc, tlo, thi).astype(BF16)
            o_ref[:, 2 * Q_W + g * GROUP_W:2 * Q_W + (g + 1) * GROUP_W] = dv_refs[g][...].astype(BF16)
        o_ref[:, QKV_W:QKV_W + SSM_W] = du_ref[...].astype(BF16)
        o_ref[:, QKV_W + SSM_W:] = dg_ref[...].astype(BF16)

    t = du.shape[0]
    ins = [("row", a) for a in (*dqs, *dks, *dvs, du, dgpre)] + [("tab", tb) for tb in tabs]
    return _rows(body, "pack_dproj", t, 256, ins, [("row", IN_W, BF16)])[0]


def _attn_merge(os_, lses):
    def body(o0, o1, o2, l0, l1, l2, a_ref, lt_ref):
        la, lb, lc = l0[...], l1[...], l2[...]
        m = jnp.maximum(jnp.maximum(la, lb), lc)
        ea, eb, ec = jnp.exp(la - m), jnp.exp(lb - m), jnp.exp(lc - m)
        ssum = ea + eb + ec
        a_ref[...] = (ea / ssum) * o0[...] + (eb / ssum) * o1[...] + (ec / ssum) * o2[...]
        lt_ref[...] = m + jnp.log(ssum)

    t = os_[0].shape[0]
    return _rows(body, "attn_merge", t, 512, [("row", a) for a in (*os_, *lses)],
                 [("row", GROUP_W, F32), ("row", GROUP_W, F32)])


def _head_sum_matrix():
    r = jnp.arange(GROUP_W) // HEAD_DIM
    return (r[:, None] == r[None, :]).astype(F32)


def _attn_rowdot(dattn, attn):
    def body(da_ref, a_ref, ones_ref, d_ref):
        d_ref[...] = jnp.dot(da_ref[...] * a_ref[...], ones_ref[...], preferred_element_type=F32,
                             precision=lax.Precision.HIGHEST)

    t = attn.shape[0]
    return _rows(body, "attn_rowdot", t, 512, [("row", dattn), ("row", attn), ("const", _head_sum_matrix())],
                 [("row", GROUP_W, F32)])[0]


def _mix(attn_d, z, gates):
    def body(ad_ref, z_ref, g_ref, m_ref):
        za, zb = z_ref[:, :D_MODEL], z_ref[:, D_MODEL:]
        s_out = za * _sigmoid(zb)
        m_ref[...] = (g_ref[:, :D_MODEL] * ad_ref[...] + g_ref[:, D_MODEL:] * s_out).astype(BF16)

    t = attn_d.shape[0]
    return _rows(body, "mix", t, 256, [("row", attn_d), ("row", z), ("row", gates)], [("row", D_MODEL, BF16)])[0]


def _mix_bwd(dmerged, gates, attn_d, z):
    def body(dm_ref, g_ref, ad_ref, z_ref, dad_ref, dz_ref, dg_ref):
        dm = dm_ref[...]
        g0, g1 = g_ref[:, :D_MODEL], g_ref[:, D_MODEL:]
        za, zb = z_ref[:, :D_MODEL], z_ref[:, D_MODEL:]
        sb = _sigmoid(zb)
        s_out = za * sb
        dad_ref[...] = (dm * g0).astype(BF16)
        ds = dm * g1
        dz_ref[:, :D_MODEL] = (ds * sb).astype(BF16)
        dz_ref[:, D_MODEL:] = (ds * za * sb * (1.0 - sb)).astype(BF16)
        dg_ref[:, :D_MODEL] = dm * ad_ref[...] * g0 * (1.0 - g0)
        dg_ref[:, D_MODEL:] = dm * s_out * g1 * (1.0 - g1)

    t = dmerged.shape[0]
    return _rows(body, "mix_bwd", t, 256, [("row", dmerged), ("row", gates), ("row", attn_d), ("row", z)],
                 [("row", D_MODEL, BF16), ("row", 2 * D_MODEL, BF16), ("row", 2 * D_MODEL, F32)])


def _swiglu(ab):
    def body(ab_ref, f_ref):
        a, b = ab_ref[:, :D_FF], ab_ref[:, D_FF:]
        f_ref[...] = (a * _sigmoid(a) * b).astype(BF16)

    return _rows(body, "swiglu", ab.shape[0], 256, [("row", ab)], [("row", D_FF, BF16)])[0]


def _swiglu_bwd(df, ab):
    def body(df_ref, ab_ref, o_ref):
        a, b = ab_ref[:, :D_FF], ab_ref[:, D_FF:]
        d = df_ref[...]
        sg = _sigmoid(a)
        o_ref[:, :D_FF] = (d * b * sg * (1.0 + a * (1.0 - sg))).astype(BF16)
        o_ref[:, D_FF:] = (d * a * sg).astype(BF16)

    return _rows(body, "swiglu_bwd", ab.shape[0], 256, [("row", df), ("row", ab)], [("row", 2 * D_FF, BF16)])[0]


def _final(x2, target, g):
    def body(x_ref, t_ref, g_ref, dx_ref, dxb_ref, loss_ref, gg_ref):
        @pl.when(_first_step())
        def _():
            loss_ref[...] = jnp.zeros_like(loss_ref)
            gg_ref[...] = jnp.zeros_like(gg_ref)

        xv = x_ref[...]
        gv = g_ref[...]
        r = lax.rsqrt(jnp.mean(xv * xv, axis=-1, keepdims=True) + RMS_EPS)
        n = xv * r
        diff = n * gv - t_ref[...]
        per_tok = jnp.mean(diff * diff, axis=-1, keepdims=True)
        loss_ref[...] += 0.5 * jnp.sum(per_tok, axis=0, keepdims=True)
        dy = diff / xv.shape[-1]
        gg_ref[...] += jnp.sum(dy * n, axis=0, keepdims=True)
        dn = dy * gv
        dx = r * (dn - n * jnp.mean(dn * n, axis=-1, keepdims=True))
        dx_ref[...] = dx
        dxb_ref[...] = dx.astype(BF16)

    d = x2.shape[1]
    return _rows(body, "final_loss", x2.shape[0], 256, [("row", x2), ("row", target), ("const", g)],
                 [("row", d, F32), ("row", d, BF16), ("acc", (8, LANES), F32), ("acc", (1, d), F32)])


_GELU_C = math.sqrt(2.0 / math.pi)


def _ssm_act(ypre, u_perm, dskip):
    def body(y_ref, u_ref, d_ref, yt_ref, yg_ref):
        yt = y_ref[...] + d_ref[...] * u_ref[...]
        yt_ref[...] = yt
        th = jnp.tanh(_GELU_C * (yt + 0.044715 * (yt * yt * yt)))
        yg_ref[...] = (0.5 * yt * (1.0 + th)).astype(BF16)

    t = ypre.shape[0]
    return _rows(body, "ssm_act", t, 512, [("row", ypre), ("row", u_perm), ("const", dskip)],
                 [("row", SSM_W, F32), ("row", SSM_W, BF16)])


def _ssm_act_bwd(dyg, ytot, u_perm, dskip):
    def body(dyg_ref, yt_ref, u_ref, d_ref, dy_ref, dus_ref, dd_ref):
        @pl.when(_first_step())
        def _():
            dd_ref[...] = jnp.zeros_like(dd_ref)

        yt = yt_ref[...]
        th = jnp.tanh(_GELU_C * (yt + 0.044715 * (yt * yt * yt)))
        dgelu = 0.5 * (1.0 + th) + 0.5 * yt * (1.0 - th * th) * _GELU_C * (1.0 + 3.0 * 0.044715 * yt * yt)
        dy = dyg_ref[...] * dgelu
        dy_ref[...] = dy.astype(BF16)
        dus_ref[...] = dy * d_ref[...]
        dd_ref[...] += jnp.sum(dy * u_ref[...], axis=0, keepdims=True)

    t = dyg.shape[0]
    return _rows(body, "ssm_act_bwd", t, 512, [("row", dyg), ("row", ytot), ("row", u_perm), ("const", dskip)],
                 [("row", SSM_W, BF16), ("row", SSM_W, F32), ("acc", (1, SSM_W), F32)])


def _head_masks():
    lane = lax.broadcasted_iota(jnp.int32, (1, GROUP_W), 1)
    return [(lane // HEAD_DIM) == h for h in range(HEADS_PER_GROUP)]


def _band_mask(first):
    nk = ATT_BLOCK if first else 2 * ATT_BLOCK
    qi = lax.broadcasted_iota(jnp.int32, (ATT_BLOCK, nk), 0)
    ki = lax.broadcasted_iota(jnp.int32, (ATT_BLOCK, nk), 1)
    dist = qi - ki + (0 if first else ATT_BLOCK)
    return (dist >= 0) & (dist <= ATT_BLOCK)


_NT = (((1,), (1,)), ((), ()))
_TN = (((0,), (0,)), ((), ()))


def _attn_fwd(q, k, v, group, n_samples, comm=None):
    d = DILATIONS[group]
    length = SEQ // d
    nb = length // ATT_BLOCK

    def body(q_ref, k_ref, v_ref, o_ref, l_ref):
        masks = _head_masks()

        def block(qs, ks, first):
            nk = ATT_BLOCK if first else 2 * ATT_BLOCK
            qb = q_ref[0, pl.ds(qs, ATT_BLOCK), :]
            kc = k_ref[0, pl.ds(ks, nk), :]
            vc = v_ref[0, pl.ds(ks, nk), :]
            valid = _band_mask(first)
            o_acc = jnp.zeros((ATT_BLOCK, GROUP_W), F32)
            l_acc = jnp.zeros((ATT_BLOCK, GROUP_W), F32)
            for h in range(HEADS_PER_GROUP):
                qh = jnp.where(masks[h], qb, jnp.zeros_like(qb))
                s = lax.dot_general(qh, kc, _NT, preferred_element_type=F32) * (HEAD_DIM ** -0.5)
                s = jnp.where(valid, s, NEG_INF)
                m = jnp.max(s, axis=-1, keepdims=True)
                p = jnp.exp(s - m)
                l = jnp.sum(p, axis=-1, keepdims=True)
                pv = jnp.dot(p.astype(MXU_DTYPE), vc, preferred_element_type=F32)
                o_acc = jnp.where(masks[h], pv / l, o_acc)
                l_acc = jnp.where(masks[h], m + jnp.log(l), l_acc)
            o_ref[0, pl.ds(qs, ATT_BLOCK), :] = o_acc
            l_ref[0, pl.ds(qs, ATT_BLOCK), :] = l_acc

        block(0, 0, True)
        if nb > 1:
            def loop(n, carry):
                block(pl.multiple_of(n * ATT_BLOCK, ATT_BLOCK), pl.multiple_of((n - 1) * ATT_BLOCK, ATT_BLOCK), False)
                return carry

            lax.fori_loop(1, nb, loop, 0)

    qv = q.reshape(n_samples, length, d * Q_W)
    kv = k.reshape(n_samples, length, d * Q_W)
    vv = v.reshape(n_samples, length, d * Q_W)
    in_spec = pl.BlockSpec((1, length, GROUP_W), lambda b, r: (b, 0, 3 * r + group))
    out_spec = pl.BlockSpec((1, length, GROUP_W), lambda b, r: (b, 0, r))
    shp = jax.ShapeDtypeStruct((n_samples, length, d * GROUP_W), F32)
    o, lse, *carried = _grid_call(body, f"attn_fwd_g{group}", (n_samples, d), [qv, kv, vv], [in_spec] * 3,
                                  [out_spec] * 2, [shp, shp], VMEM_MID, comm)
    return o.reshape(n_samples * SEQ, GROUP_W), lse.reshape(n_samples * SEQ, GROUP_W), carried


def _attn_bwd(q, k, v, dattn, lse_tot, rowdot, group, n_samples, comm=None):
    d = DILATIONS[group]
    length = SEQ // d
    nb = length // ATT_BLOCK

    def body(q_ref, k_ref, v_ref, da_ref, lt_ref, rd_ref, dq_ref, dk_ref, dv_ref):
        masks = _head_masks()
        dk_ref[...] = jnp.zeros_like(dk_ref)
        dv_ref[...] = jnp.zeros_like(dv_ref)

        def block(qs, ks, first):
            nk = ATT_BLOCK if first else 2 * ATT_BLOCK
            qb = q_ref[0, pl.ds(qs, ATT_BLOCK), :]
            kc = k_ref[0, pl.ds(ks, nk), :]
            vc = v_ref[0, pl.ds(ks, nk), :]
            da = da_ref[0, pl.ds(qs, ATT_BLOCK), :]
            lt = lt_ref[0, pl.ds(qs, ATT_BLOCK), :]
            rd = rd_ref[0, pl.ds(qs, ATT_BLOCK), :]
            valid = _band_mask(first)
            dq_acc = jnp.zeros((ATT_BLOCK, GROUP_W), F32)
            dk_acc = jnp.zeros((nk, GROUP_W), F32)
            dv_acc = jnp.zeros((nk, GROUP_W), F32)
            for h in range(HEADS_PER_GROUP):
                qh = jnp.where(masks[h], qb, jnp.zeros_like(qb))
                dah = jnp.where(masks[h], da, 0.0).astype(MXU_DTYPE)
                lt_h = jnp.max(jnp.where(masks[h], lt, -jnp.inf), axis=-1, keepdims=True)
                rd_h = jnp.max(jnp.where(masks[h], rd, -jnp.inf), axis=-1, keepdims=True)
                s = lax.dot_general(qh, kc, _NT, preferred_element_type=F32) * (HEAD_DIM ** -0.5)
                s = jnp.where(valid, s, NEG_INF)
                p = jnp.exp(s - lt_h)
                dp = lax.dot_general(dah, vc, _NT, preferred_element_type=F32)
                ds = (p * (dp - rd_h) * (HEAD_DIM ** -0.5)).astype(MXU_DTYPE)
                dq_h = jnp.dot(ds, kc, preferred_element_type=F32)
                dq_acc = jnp.where(masks[h], dq_h, dq_acc)
                dk_acc = dk_acc + lax.dot_general(ds, qh, _TN, preferred_element_type=F32)
                dv_acc = dv_acc + lax.dot_general(p.astype(MXU_DTYPE), dah, _TN, preferred_element_type=F32)
            dq_ref[0, pl.ds(qs, ATT_BLOCK), :] = dq_acc
            dk_ref[0, pl.ds(ks, nk), :] += dk_acc
            dv_ref[0, pl.ds(ks, nk), :] += dv_acc

        block(0, 0, True)
        if nb > 1:
            def loop(n, carry):
                block(pl.multiple_of(n * ATT_BLOCK, ATT_BLOCK), pl.multiple_of((n - 1) * ATT_BLOCK, ATT_BLOCK), False)
                return carry

            lax.fori_loop(1, nb, loop, 0)

    qv = q.reshape(n_samples, length, d * Q_W)
    kv = k.reshape(n_samples, length, d * Q_W)
    vv = v.reshape(n_samples, length, d * Q_W)
    nat = lambda a: a.reshape(n_samples, length, d * GROUP_W)
    in_spec = pl.BlockSpec((1, length, GROUP_W), lambda b, r: (b, 0, 3 * r + group))
    nat_spec = pl.BlockSpec((1, length, GROUP_W), lambda b, r: (b, 0, r))
    shp = jax.ShapeDtypeStruct((n_samples, length, d * GROUP_W), F32)
    dq, dk, dv, *carried = _grid_call(
        body, f"attn_bwd_g{group}", (n_samples, d), [qv, kv, vv, nat(dattn), nat(lse_tot), nat(rowdot)],
        [in_spec] * 3 + [nat_spec] * 3, [nat_spec] * 3, [shp, shp, shp], VMEM_MID, comm)
    t = n_samples * SEQ
    return dq.reshape(t, GROUP_W), dk.reshape(t, GROUP_W), dv.reshape(t, GROUP_W), carried


def _disc(lr, li, ldt, br, bi):
    dt = jnp.exp(ldt)
    mag = jnp.exp(lr * dt)
    ab_re, ab_im = mag * jnp.cos(li * dt), mag * jnp.sin(li * dt)
    den = lr * lr + li * li
    nr, ni = ab_re - 1.0, ab_im
    f_re = (nr * lr + ni * li) / den
    f_im = (ni * lr - nr * li) / den
    return ab_re, ab_im, f_re * br - f_im * bi, f_re * bi + f_im * br


def _state_mask():
    row_g = lax.broadcasted_iota(jnp.int32, (SCAN_CH, SCAN_WC), 0) // SSM_CH
    col_g = lax.broadcasted_iota(jnp.int32, (SCAN_CH, SCAN_WC), 1) // SSM_STATE
    return row_g == col_g


def _ssm_disc(lr, li, ldt, br, bi, cr, ci):
    w = SCAN_WC

    def body(lr_ref, li_ref, ldt_ref, br_ref, bi_ref, cr_ref, ci_ref, a_ref, bb_ref, c_ref):
        ar, ai, bbr, bbi = _disc(lr_ref[...], li_ref[...], ldt_ref[...], br_ref[...], bi_ref[...])
        crv, civ = cr_ref[...], ci_ref[...]
        mask = _state_mask()
        for cb in range(SCAN_NBLK):
            sl = slice(cb * w, (cb + 1) * w)
            rows = slice(cb * SCAN_CH, (cb + 1) * SCAN_CH)
            dense = lambda comp: jnp.where(mask, jnp.tile(comp[:, sl], (SCAN_CH // SSM_CH, 1)), 0.0)
            a_ref[:, 2 * cb * w:(2 * cb + 1) * w] = ar[:, sl]
            a_ref[:, (2 * cb + 1) * w:(2 * cb + 2) * w] = ai[:, sl]
            bb_ref[rows, :w] = dense(bbr).astype(MXU_DTYPE)
            bb_ref[rows, w:] = dense(bbi).astype(MXU_DTYPE)
            c_ref[rows, :w] = dense(crv).astype(MXU_DTYPE)
            c_ref[rows, w:] = (-dense(civ)).astype(MXU_DTYPE)

    return _pallas_call(
        body, name="ssm_disc",
        out_shape=[jax.ShapeDtypeStruct((1, 2 * N_STATE), F32), jax.ShapeDtypeStruct((SSM_W, 2 * w), MXU_DTYPE),
                   jax.ShapeDtypeStruct((SSM_W, 2 * w), MXU_DTYPE)],
        compiler_params=pltpu.CompilerParams(vmem_limit_bytes=VMEM_MID),
    )(lr, li, ldt, br, bi, cr, ci)


def _group_indicator():
    s = jnp.arange(N_STATE) // SSM_STATE
    return (s[:, None] == jnp.arange(LANES)[None, :]).astype(F32)


def _ssm_param_bwd(lr, li, ldt, br, bi, da_cat, dbb_full, dc_full):
    w = SCAN_WC

    def body(lr_ref, li_ref, ldt_ref, br_ref, bi_ref, da_ref, dbb_ref, dc_ref, ind_ref,
             glr_ref, gli_ref, gldt_ref, gbr_ref, gbi_ref, gcr_ref, gci_ref):
        mask = _state_mask()

        def diag_parts(ref):
            res = ([], [])
            for cb in range(SCAN_NBLK):
                for part in range(2):
                    blk = ref[cb * SCAN_CH:(cb + 1) * SCAN_CH, part * w:(part + 1) * w]
                    res[part].append(jnp.sum(jnp.where(mask, blk, 0.0).reshape(SCAN_CH // SSM_CH, SSM_CH, w), axis=0))
            return jnp.concatenate(res[0], axis=1), jnp.concatenate(res[1], axis=1)

        dar = jnp.concatenate([da_ref[:, 2 * cb * w:(2 * cb + 1) * w] for cb in range(SCAN_NBLK)], axis=1)
        dai = jnp.concatenate([da_ref[:, (2 * cb + 1) * w:(2 * cb + 2) * w] for cb in range(SCAN_NBLK)], axis=1)
        dbbr, dbbi = diag_parts(dbb_ref)
        dcr, dci_neg = diag_parts(dc_ref)
        gcr_ref[...] = dcr
        gci_ref[...] = -dci_neg
        _, vjp = jax.vjp(_disc, lr_ref[...], li_ref[...], ldt_ref[...], br_ref[...], bi_ref[...])
        glr, gli, gldt, gbr, gbi = vjp((dar, dai, dbbr, dbbi))
        glr_ref[...] = glr
        gli_ref[...] = gli
        gldt_ref[...] = jnp.dot(jnp.broadcast_to(gldt, (8, N_STATE)), ind_ref[...], preferred_element_type=F32,
                                precision=lax.Precision.HIGHEST)
        gbr_ref[...] = gbr
        gbi_ref[...] = gbi

    v1 = jax.ShapeDtypeStruct((1, N_STATE), F32)
    v16 = jax.ShapeDtypeStruct((SSM_CH, N_STATE), F32)
    vdt = jax.ShapeDtypeStruct((8, LANES), F32)
    return _pallas_call(
        body, name="ssm_param_bwd", out_shape=[v1, v1, vdt, v16, v16, v16, v16],
        compiler_params=pltpu.CompilerParams(vmem_limit_bytes=VMEM_BIG),
    )(lr, li, ldt, br, bi, da_cat, dbb_full, dc_full, _group_indicator())


def _cmul(ar, ai, br, bi):
    return ar * br - ai * bi, ar * bi + ai * br


def _gelu_tanh(y):
    return jnp.tanh(_GELU_C * (y + 0.044715 * (y * y * y)))


def _segment_carry(er, ei, ar, ai, n_rows, reverse):
    qr, qi = ar, ai
    for _ in range(int(math.log2(SCAN_LEN))):
        qr, qi = _cmul(qr, qi, qr, qi)
    seg = lax.broadcasted_iota(jnp.int32, er.shape, 0) % SCAN_SEG_PER_SAMPLE
    shift = 1
    while shift < SCAN_SEG_PER_SAMPLE:
        keep = (seg < SCAN_SEG_PER_SAMPLE - shift) if reverse else (seg >= shift)
        amount = n_rows - shift if reverse else shift
        sr = jnp.where(keep, pltpu.roll(er, amount, 0), 0.0)
        si = jnp.where(keep, pltpu.roll(ei, amount, 0), 0.0)
        if reverse:
            er, ei = er + qr * sr + qi * si, ei + qr * si - qi * sr
        else:
            er, ei = er + qr * sr - qi * si, ei + qr * si + qi * sr
        qr, qi = _cmul(qr, qi, qr, qi)
        shift *= 2
    keep = (seg < SCAN_SEG_PER_SAMPLE - 1) if reverse else (seg >= 1)
    amount = n_rows - 1 if reverse else 1
    return jnp.where(keep, pltpu.roll(er, amount, 0), 0.0), jnp.where(keep, pltpu.roll(ei, amount, 0), 0.0)


def _ssm_fwd(u_perm, a_cat, bbc, cc, dskip, n_rows):
    t = u_perm.shape[0]
    w = SCAN_WC
    rows_c = SCAN_CHUNK * n_rows
    n_chunks = t // rows_c

    def body(u_ref, a_ref, bb_ref, c_ref, d_ref, yt_ref, yg_ref, ein_ref, bu_s, xs_s):
        ar = jnp.broadcast_to(a_ref[:, :w], (n_rows, w))
        ai = jnp.broadcast_to(a_ref[:, w:], (n_rows, w))

        def sweep(carry, store):
            def chunk(ch, carry):
                r0 = pl.multiple_of(ch * rows_c, rows_c)
                u_c = u_ref[pl.ds(r0, rows_c), :]
                bu_s[...] = jnp.dot(u_c.astype(MXU_DTYPE), bb_ref[...], preferred_element_type=F32)

                def step(i, c):
                    o = pl.multiple_of(i * n_rows, n_rows)
                    blk = bu_s[pl.ds(o, n_rows), :]
                    nr = ar * c[0] - ai * c[1] + blk[:, :w]
                    ni = ar * c[1] + ai * c[0] + blk[:, w:]
                    if store:
                        xs_s[pl.ds(o, n_rows), :w] = nr
                        xs_s[pl.ds(o, n_rows), w:] = ni
                    return nr, ni

                carry = lax.fori_loop(0, SCAN_CHUNK, step, carry)
                if store:
                    y = lax.dot_general(xs_s[...].astype(MXU_DTYPE), c_ref[...], _NT, preferred_element_type=F32)
                    yt = y + d_ref[...] * u_c
                    yt_ref[pl.ds(r0, rows_c), :] = yt
                    yg_ref[pl.ds(r0, rows_c), :] = (0.5 * yt * (1.0 + _gelu_tanh(yt))).astype(BF16)
                return carry

            return lax.fori_loop(0, n_chunks, chunk, carry)

        zero = jnp.zeros((n_rows, w), F32)
        er, ei = sweep((zero, zero), False)
        cr, ci = _segment_carry(er, ei, ar, ai, n_rows, False)
        ein_ref[:, :w] = cr
        ein_ref[:, w:] = ci
        sweep((cr, ci), True)

    col = lambda width: pl.BlockSpec((t, width), lambda c: (0, c))
    wgt = pl.BlockSpec((SCAN_CH, 2 * w), lambda c: (c, 0))
    return _pallas_call(
        body, name="ssm_fwd", grid=(SCAN_NBLK,),
        in_specs=[col(SCAN_CH), pl.BlockSpec((1, 2 * w), lambda c: (0, c)), wgt, wgt,
                  pl.BlockSpec((1, SCAN_CH), lambda c: (0, c))],
        out_specs=[col(SCAN_CH), col(SCAN_CH), pl.BlockSpec((n_rows, 2 * w), lambda c: (0, c))],
        out_shape=[jax.ShapeDtypeStruct((t, SSM_W), F32), jax.ShapeDtypeStruct((t, SSM_W), BF16),
                   jax.ShapeDtypeStruct((n_rows, 2 * N_STATE), F32)],
        scratch_shapes=[pltpu.VMEM((rows_c, 2 * w), F32), pltpu.VMEM((rows_c, 2 * w), F32)],
        compiler_params=pltpu.CompilerParams(dimension_semantics=("parallel",), vmem_limit_bytes=VMEM_BIG),
    )(u_perm, a_cat, bbc, cc, dskip)


def _ssm_bwd(u_perm, dypre, du_skip, a_cat, bbc, cc, ein, n_rows):
    t = u_perm.shape[0]
    w = SCAN_WC
    rows_c = SCAN_CHUNK * n_rows
    n_chunks = t // rows_c

    def body(u_ref, dy_ref, dus_ref, a_ref, bb_ref, c_ref, ein_ref, du_ref, da_ref, dbb_ref, dc_ref, xs_all, tmp_s, g_s):
        ar = jnp.broadcast_to(a_ref[:, :w], (n_rows, w))
        ai = jnp.broadcast_to(a_ref[:, w:], (n_rows, w))
        zero = jnp.zeros((n_rows, w), F32)

        xs_all[0:n_rows, :] = ein_ref[...]

        def fwd_chunk(ch, carry):
            r0 = pl.multiple_of(ch * rows_c, rows_c)
            tmp_s[...] = jnp.dot(u_ref[pl.ds(r0, rows_c), :].astype(MXU_DTYPE), bb_ref[...], preferred_element_type=F32)

            def step(i, c):
                o = pl.multiple_of(i * n_rows, n_rows)
                blk = tmp_s[pl.ds(o, n_rows), :]
                nr = ar * c[0] - ai * c[1] + blk[:, :w]
                ni = ar * c[1] + ai * c[0] + blk[:, w:]
                xs_all[pl.ds(n_rows + r0 + o, n_rows), :w] = nr
                xs_all[pl.ds(n_rows + r0 + o, n_rows), w:] = ni
                return nr, ni

            return lax.fori_loop(0, SCAN_CHUNK, step, carry)

        lax.fori_loop(0, n_chunks, fwd_chunk, (ein_ref[:, :w], ein_ref[:, w:]))

        def load_dx(ch):
            r0 = pl.multiple_of(ch * rows_c, rows_c)
            tmp_s[...] = jnp.dot(dy_ref[pl.ds(r0, rows_c), :], c_ref[...], preferred_element_type=F32)
            return r0

        def back_steps(carry, store):
            def step(ii, c):
                o = pl.multiple_of((SCAN_CHUNK - 1 - ii) * n_rows, n_rows)
                blk = tmp_s[pl.ds(o, n_rows), :]
                gr = blk[:, :w] + ar * c[0] + ai * c[1]
                gi = blk[:, w:] + ar * c[1] - ai * c[0]
                if store:
                    g_s[pl.ds(o, n_rows), :w] = gr
                    g_s[pl.ds(o, n_rows), w:] = gi
                return gr, gi

            return lax.fori_loop(0, SCAN_CHUNK, step, carry)

        def first_sweep(cc_, carry):
            load_dx(n_chunks - 1 - cc_)
            return back_steps(carry, False)

        sr, si = lax.fori_loop(0, n_chunks, first_sweep, (zero, zero))
        gr0, gi0 = _segment_carry(sr, si, ar, ai, n_rows, True)

        dbb_ref[...] = jnp.zeros_like(dbb_ref)
        dc_ref[...] = jnp.zeros_like(dc_ref)
        da_ref[...] = jnp.zeros_like(da_ref)

        def second_sweep(cc_, carry):
            r0 = load_dx(n_chunks - 1 - cc_)
            carry = back_steps(carry, True)
            g = g_s[...]
            xp = xs_all[pl.ds(r0, rows_c), :]
            xc = xs_all[pl.ds(r0 + n_rows, rows_c), :]
            da_ref[:, :w] += jnp.sum(g[:, :w] * xp[:, :w] + g[:, w:] * xp[:, w:], axis=0, keepdims=True)
            da_ref[:, w:] += jnp.sum(g[:, w:] * xp[:, :w] - g[:, :w] * xp[:, w:], axis=0, keepdims=True)
            gb = g.astype(MXU_DTYPE)
            du_ref[pl.ds(r0, rows_c), :] = (lax.dot_general(gb, bb_ref[...], _NT, preferred_element_type=F32)
                                            + dus_ref[pl.ds(r0, rows_c), :])
            dbb_ref[...] += lax.dot_general(u_ref[pl.ds(r0, rows_c), :].astype(MXU_DTYPE), gb, _TN,
                                            preferred_element_type=F32)
            dc_ref[...] += lax.dot_general(dy_ref[pl.ds(r0, rows_c), :], xc.astype(MXU_DTYPE), _TN,
                                           preferred_element_type=F32)
            return carry

        lax.fori_loop(0, n_chunks, second_sweep, (gr0, gi0))

    col = lambda width: pl.BlockSpec((t, width), lambda c: (0, c))
    wgt = pl.BlockSpec((SCAN_CH, 2 * w), lambda c: (c, 0))
    row = pl.BlockSpec((1, 2 * w), lambda c: (0, c))
    return _pallas_call(
        body, name="ssm_bwd", grid=(SCAN_NBLK,),
        in_specs=[col(SCAN_CH), col(SCAN_CH), col(SCAN_CH), row, wgt, wgt, pl.BlockSpec((n_rows, 2 * w), lambda c: (0, c))],
        out_specs=[col(SCAN_CH), row, wgt, wgt],
        out_shape=[jax.ShapeDtypeStruct((t, SSM_W), F32), jax.ShapeDtypeStruct((1, 2 * N_STATE), F32),
                   jax.ShapeDtypeStruct((SSM_W, 2 * w), F32), jax.ShapeDtypeStruct((SSM_W, 2 * w), F32)],
        scratch_shapes=[pltpu.VMEM((t + n_rows, 2 * w), F32), pltpu.VMEM((rows_c, 2 * w), F32),
                        pltpu.VMEM((rows_c, 2 * w), F32)],
        compiler_params=pltpu.CompilerParams(dimension_semantics=("parallel",), vmem_limit_bytes=56 * 1024 * 1024),
    )(u_perm, dypre, du_skip, a_cat, bbc, cc, ein)


def _to_scan_rows(a, n_samples):
    c = a.shape[1]
    return a.reshape(n_samples, SCAN_SEG_PER_SAMPLE, SCAN_LEN, c).transpose(2, 0, 1, 3).reshape(-1, c)


def _from_scan_rows(a, n_samples):
    c = a.shape[1]
    return a.reshape(SCAN_LEN, n_samples, SCAN_SEG_PER_SAMPLE, c).transpose(1, 2, 0, 3).reshape(-1, c)


def _flat_small(small):
    perm_b = lambda a: a.reshape(SSM_GROUPS, SSM_STATE, SSM_CH).transpose(2, 0, 1).reshape(SSM_CH, N_STATE)
    perm_c = lambda a: a.reshape(SSM_GROUPS, SSM_CH, SSM_STATE).transpose(1, 0, 2).reshape(SSM_CH, N_STATE)
    return dict(
        g_mix=small["norm_mix_g"].reshape(1, D_MODEL), g_ffn=small["norm_ffn_g"].reshape(1, D_MODEL),
        g_fin=small["norm_final_g"].reshape(1, D_MODEL),
        lr=small["ssm_a_re"].reshape(1, N_STATE), li=small["ssm_a_im"].reshape(1, N_STATE),
        ldt=jnp.repeat(small["ssm_log_dt"].reshape(SSM_GROUPS), SSM_STATE).reshape(1, N_STATE),
        br=perm_b(small["ssm_b_re"]), bi=perm_b(small["ssm_b_im"]),
        cr=perm_c(small["ssm_c_re"]), ci=perm_c(small["ssm_c_im"]), dskip=small["ssm_d"].reshape(1, SSM_W))


AG_HOSTS = {"mm_proj": ("w_glu", "w_attn_out", "w_out"), "attn_fwd_g0": ("w_gu",), "mm_ffn_in": ("w_ffn_down",)}
A2A_HOSTS = {"mm_d_h2": ("w_ffn_down",), "attn_bwd_g0": ("w_ffn_gate",), "attn_bwd_g1": ("w_ffn_up",),
             "mm_g_in": ("w_out", "w_attn_out", "w_glu"), "mm_d_h0": ("w_in",)}
GRAD_SOURCE = {"w_ffn_gate": ("w_gu", 0), "w_ffn_up": ("w_gu", N_DEV)}


def _local_step(x, target, w, small, shards=None):
    t = x.shape[0]
    n_samples = t // SEQ
    n_rows = n_samples * SCAN_SEG_PER_SAMPLE
    tabs = _rope_tables()
    w = dict(w)
    fs = _flat_small(small)
    g_mix, g_ffn, g_fin, dskip = fs["g_mix"], fs["g_ffn"], fs["g_fin"], fs["dskip"]
    a_cat, bbc, cc = _ssm_disc(fs["lr"], fs["li"], fs["ldt"], fs["br"], fs["bi"], fs["cr"], fs["ci"])
    big, recv = {}, {}

    def comm_of(name):
        if shards is None:
            return None
        if name in AG_HOSTS:
            items, bufs = [], []
            for n in AG_HOSTS[name]:
                parts = ("w_ffn_gate", "w_ffn_up") if n == "w_gu" else (n,)
                c, k = shards[parts[0]].shape
                for j, p in enumerate(parts):
                    items.append((shards[p], len(bufs), j * N_DEV))
                bufs.append((len(parts) * N_DEV, c, k))
            return _ag_comm(items, bufs)
        if name in A2A_HOSTS:
            items = []
            for n in A2A_HOSTS[name]:
                src, slot0 = GRAD_SOURCE.get(n, (n, 0))
                c, k = shards[n].shape
                items.append((big[src].reshape(-1, c, k), slot0))
            return _a2a_comm(items)
        return None

    def absorb(name, carried):
        for n, a3 in zip(AG_HOSTS.get(name, ()), carried):
            w[n] = a3.reshape(-1, a3.shape[2])
        for n, a3 in zip(A2A_HOSTS.get(name, ()), carried):
            recv[n] = a3

    def mm(a, b, mode, name, tm, tn, **kw):
        comm = comm_of(name)
        if comm is None:
            return _mm(a, b, mode, name, tm, tn, **kw)
        out, *carried = _mm(a, b, mode, name, tm, tn, comm=comm, **kw)
        absorb(name, carried)
        return out

    h0 = _rms_fwd(x, g_mix, "rms_mix")
    proj = mm(h0, w["w_in"], "nt", "mm_proj", 512, IN_W // 2)
    q, k, v, u, gates = _rope_split(proj, tabs)
    os_, lses = [], []
    for g in range(3):
        o_g, l_g, carried = _attn_fwd(q, k, v, g, n_samples, comm_of(f"attn_fwd_g{g}"))
        absorb(f"attn_fwd_g{g}", carried)
        os_.append(o_g)
        lses.append(l_g)
    attn, lse_tot = _attn_merge(os_, lses)
    attn_d = mm(attn, w["w_attn_out"], "nt", "mm_attn_out", 512, D_MODEL)

    u_perm = _to_scan_rows(u, n_samples)
    ytot, yg_perm, ein = _ssm_fwd(u_perm, a_cat, bbc, cc, dskip, n_rows)
    yg = _from_scan_rows(yg_perm, n_samples)
    z = mm(yg, w["w_glu"], "nt", "mm_glu", 512, 2 * D_MODEL)

    merged = _mix(attn_d, z, gates)
    x1 = mm(merged, w["w_out"], "nn", "mm_out", 512, D_MODEL, add=x)
    h2 = _rms_fwd(x1, g_ffn, "rms_ffn")
    ab = mm(h2, w["w_gu"], "nt", "mm_ffn_in", 512, D_FF)
    f = _swiglu(ab)
    x2 = mm(f, w["w_ffn_down"], "nn", "mm_ffn_down", 512, D_MODEL, add=x1)
    dx2, dx2b, loss_blk, g_gfin = _final(x2, target, g_fin)

    df = mm(dx2b, w["w_ffn_down"], "nt", "mm_d_f", 512, D_FF)
    dab = _swiglu_bwd(df, ab)
    big["w_ffn_down"] = mm(f, dx2b, "tn", "mm_g_down", 256, 512, out_dtype=BF16)
    big["w_gu"] = mm(dab, h2, "tn", "mm_g_gu", 512, 512, out_dtype=BF16)
    dh2 = mm(dab, w["w_gu"], "nn", "mm_d_h2", 512, 512)
    dx1, dx1b, g_gffn = _rms_bwd(x1, g_ffn, dh2, dx2, "rms_ffn_bwd")

    dmerged = mm(dx1b, w["w_out"], "nt", "mm_d_merged", 512, D_MODEL)
    big["w_out"] = mm(merged, dx1b, "tn", "mm_g_out", 512, 512, out_dtype=BF16)
    dattn_d, dz, dgpre = _mix_bwd(dmerged, gates, attn_d, z)

    dattn = mm(dattn_d, w["w_attn_out"], "nn", "mm_d_attn", 512, GROUP_W)
    big["w_attn_out"] = mm(dattn_d, attn, "tn", "mm_g_attn_out", 512, GROUP_W, out_dtype=BF16)
    rowdot = _attn_rowdot(dattn, attn)
    dqs, dks, dvs = [], [], []
    for g in range(3):
        dq_g, dk_g, dv_g, carried = _attn_bwd(q, k, v, dattn, lse_tot, rowdot, g, n_samples, comm_of(f"attn_bwd_g{g}"))
        absorb(f"attn_bwd_g{g}", carried)
        dqs.append(dq_g)
        dks.append(dk_g)
        dvs.append(dv_g)

    dyg = mm(dz, w["w_glu"], "nn", "mm_d_yg", 512, SSM_W)
    big["w_glu"] = mm(dz, yg, "tn", "mm_g_glu", 512, 512, out_dtype=BF16)
    dyg_perm = _to_scan_rows(dyg, n_samples)
    dypre, du_skip, g_dskip = _ssm_act_bwd(dyg_perm, ytot, u_perm, dskip)
    du_perm, da_cat, dbb_full, dc_full = _ssm_bwd(u_perm, dypre, du_skip, a_cat, bbc, cc, ein, n_rows)
    du = _from_scan_rows(du_perm, n_samples)
    g_lr, g_li, g_ldt, g_br, g_bi, g_cr, g_ci = _ssm_param_bwd(
        fs["lr"], fs["li"], fs["ldt"], fs["br"], fs["bi"], da_cat, dbb_full, dc_full)

    dproj = _pack_dproj(dqs, dks, dvs, du, dgpre, tabs)
    big["w_in"] = mm(dproj, h0, "tn", "mm_g_in", 256, 512, out_dtype=BF16)
    dh0 = mm(dproj, w["w_in"], "nn", "mm_d_h0", 512, 512)
    grad_x, _, g_gmix = _rms_bwd(x, g_mix, dh0, dx1, "rms_mix_bwd")

    small_g = dict(lr=g_lr, li=g_li, ldt=g_ldt, br=g_br, bi=g_bi, cr=g_cr, ci=g_ci, dskip=g_dskip,
                   g_mix=g_gmix, g_ffn=g_gffn, g_fin=g_gfin, loss=loss_blk)
    return loss_blk, grad_x, (big if shards is None else recv), small_g


_MESH = pl.DeviceIdType.MESH


def _all_gather(block, name):
    rows, lanes = block.shape

    def body(x_ref, out_ref, send_sems, recv_sems, local_sem):
        x, y, c = lax.axis_index("x"), lax.axis_index("y"), lax.axis_index("c")
        me, sibling = (x, y, c), (x, y, 1 - c)
        chips = [(1 - x, y), (x, 1 - y), (1 - x, 1 - y)]

        def slot(px, py, pc):
            return out_ref.at[4 * px + 2 * py + pc]

        def copy(k, blk, to, src=None):
            return pltpu.make_async_remote_copy(
                src_ref=slot(*blk) if src is None else src, dst_ref=slot(*blk), send_sem=send_sems.at[k],
                recv_sem=recv_sems.at[k], device_id=to, device_id_type=_MESH)

        mine = pltpu.make_async_copy(x_ref, slot(*me), local_sem)
        mine.start()
        first = [copy(0, me, sibling, src=x_ref)]
        first += [copy(1 + j, me, (*chip, c), src=x_ref) for j, chip in enumerate(chips)]
        for cp in first:
            cp.start()
        passed = [copy(4 + j, (*chip, c), sibling) for j, chip in enumerate(chips)]
        for j, chip in enumerate(chips):
            copy(1 + j, (*chip, c), me).wait_recv()
            passed[j].start()
        copy(0, sibling, me).wait_recv()
        for j, chip in enumerate(chips):
            copy(4 + j, (*chip, 1 - c), me).wait_recv()
        for cp in first + passed:
            cp.wait_send()
        mine.wait()

    return _pallas_call(
        body, name=name, out_shape=jax.ShapeDtypeStruct((N_DEV, rows, lanes), block.dtype),
        in_specs=[pl.BlockSpec(memory_space=pl.ANY)], out_specs=pl.BlockSpec(memory_space=pl.ANY),
        scratch_shapes=[pltpu.SemaphoreType.DMA((7,)), pltpu.SemaphoreType.DMA((7,)), pltpu.SemaphoreType.DMA],
    )(block)


def _ag_comm(items, bufs):
    def plan(in_refs, out_refs, send_sems, recv_sems, local_sems):
        x, y, c = lax.axis_index("x"), lax.axis_index("y"), lax.axis_index("c")
        me, sibling = (x, y, c), (x, y, 1 - c)
        chips = [(1 - x, y), (x, 1 - y), (1 - x, 1 - y)]
        plans = []
        for t, (_, buf, slot0) in enumerate(items):
            x_ref, out_ref = in_refs[t], out_refs[buf]

            def slot(px, py, pc, out_ref=out_ref, slot0=slot0):
                return out_ref.at[slot0 + 4 * px + 2 * py + pc]

            def copy(k, blk, to, src=None, t=t, slot=slot):
                return pltpu.make_async_remote_copy(
                    src_ref=slot(*blk) if src is None else src, dst_ref=slot(*blk), send_sem=send_sems.at[7 * t + k],
                    recv_sem=recv_sems.at[7 * t + k], device_id=to, device_id_type=_MESH)

            plans.append(dict(
                mine=pltpu.make_async_copy(x_ref, slot(*me), local_sems.at[t]),
                first=[copy(0, me, sibling, src=x_ref)] + [copy(1 + j, me, (*chip, c), src=x_ref)
                                                           for j, chip in enumerate(chips)],
                passed=[copy(4 + j, (*chip, c), sibling) for j, chip in enumerate(chips)],
                from_ici=[copy(1 + j, (*chip, c), me) for j, chip in enumerate(chips)],
                from_sibling=[copy(0, sibling, me)] + [copy(4 + j, (*chip, 1 - c), me) for j, chip in enumerate(chips)]))
        return plans

    def start(*refs):
        for p in plan(*refs):
            p["mine"].start()
            for cp in p["first"]:
                cp.start()

    def finish(*refs):
        plans = plan(*refs)
        for p in plans:
            for arrived, onward in zip(p["from_ici"], p["passed"]):
                arrived.wait_recv()
                onward.start()
        for p in plans:
            for arrived in p["from_sibling"]:
                arrived.wait_recv()
            for cp in p["first"] + p["passed"]:
                cp.wait_send()
            p["mine"].wait()

    out_shapes = [jax.ShapeDtypeStruct(b, items[0][0].dtype) for b in bufs]
    return _Comm([it[0] for it in items], out_shapes, 7 * len(items), len(items), start, finish)


def _a2a_comm(items):
    def plan(in_refs, out_refs, send_sems, recv_sems, local_sems):
        x, y, c = lax.axis_index("x"), lax.axis_index("y"), lax.axis_index("c")
        my = 4 * x + 2 * y + c
        copies, locals_ = [], []
        for t, (_, slot0) in enumerate(items):
            s_ref, r_ref = in_refs[t], out_refs[t]
            locals_.append(pltpu.make_async_copy(s_ref.at[slot0 + my], r_ref.at[my], local_sems.at[t]))
            for kk in range(1, N_DEV):
                px = 1 - x if kk & 4 else x
                py = 1 - y if kk & 2 else y
                pc = 1 - c if kk & 1 else c
                copies.append(pltpu.make_async_remote_copy(
                    src_ref=s_ref.at[slot0 + 4 * px + 2 * py + pc], dst_ref=r_ref.at[my],
                    send_sem=send_sems.at[7 * t + kk - 1], recv_sem=recv_sems.at[7 * t + kk - 1],
                    device_id=(px, py, pc), device_id_type=_MESH))
        return copies, locals_

    def start(*refs):
        copies, locals_ = plan(*refs)
        for cp in locals_ + copies:
            cp.start()

    def finish(*refs):
        copies, locals_ = plan(*refs)
        for cp in copies + locals_:
            cp.wait()

    out_shapes = [jax.ShapeDtypeStruct((N_DEV,) + it[0].shape[1:], it[0].dtype) for it in items]
    return _Comm([it[0] for it in items], out_shapes, 7 * len(items), len(items), start, finish)


def _adam_math(g, w, m, v):
    m_new = ADAM_B1 * m + (1.0 - ADAM_B1) * g
    v_new = ADAM_B2 * v + (1.0 - ADAM_B2) * jnp.square(g)
    m_hat = m_new / (1.0 - ADAM_B1 ** ADAM_STEP)
    v_hat = v_new / (1.0 - ADAM_B2 ** ADAM_STEP)
    return -ADAM_LR * (m_hat / (jnp.sqrt(v_hat) + ADAM_EPS) + ADAM_WD * w), m_new, v_new


def _sum_partials(partials, name, tm):
    n, rows, cols = partials.shape

    def body(p_ref, g_ref):
        g = p_ref[0].astype(F32)
        for s in range(1, n):
            g = g + p_ref[s].astype(F32)
        g_ref[...] = g

    return _pallas_call(
        body, name=name, grid=(rows // tm,), in_specs=[pl.BlockSpec((n, tm, cols), lambda i: (0, i, 0))],
        out_specs=pl.BlockSpec((tm, cols), lambda i: (i, 0)), out_shape=jax.ShapeDtypeStruct((rows, cols), F32),
        compiler_params=pltpu.CompilerParams(dimension_semantics=("parallel",), vmem_limit_bytes=VMEM_MID),
    )(partials)


def _adam(partials, w, m, v, name, tm):
    n, rows, cols = partials.shape

    def body(p_ref, w_ref, m_ref, v_ref, g_ref, d_ref, nm_ref, nv_ref):
        g = p_ref[0].astype(F32)
        for s in range(1, n):
            g = g + p_ref[s].astype(F32)
        g_ref[...] = g
        d_ref[...], nm_ref[...], nv_ref[...] = _adam_math(g, w_ref[...], m_ref[...], v_ref[...])

    assert rows % tm == 0
    row = pl.BlockSpec((tm, cols), lambda i: (i, 0))
    shp = jax.ShapeDtypeStruct((rows, cols), F32)
    return _pallas_call(
        body, name=name, grid=(rows // tm,),
        in_specs=[pl.BlockSpec((n, tm, cols), lambda i: (0, i, 0)), row, row, row],
        out_specs=[row] * 4, out_shape=[shp] * 4,
        compiler_params=pltpu.CompilerParams(dimension_semantics=("parallel",), vmem_limit_bytes=VMEM_MID),
    )(partials, w, m, v)


_PK_LR, _PK_LI, _PK_GAINS, _PK_MISC, _PK_BR, _PK_BI, _PK_CR, _PK_CI, _PK_ROWS = 0, 1, 2, 3, 8, 24, 40, 56, 72
_PK_LDT_LANE, _PK_LOSS_LANE = D_MODEL + SSM_W, D_MODEL + SSM_W + LANES


def _pack_small(sg):
    names = ("lr", "li", "g_mix", "g_ffn", "g_fin", "dskip", "ldt", "loss", "br", "bi", "cr", "ci")

    def body(lr, li, gmix, gffn, gfin, dskip, ldt, loss, br, bi, cr, ci, o_ref):
        o_ref[...] = jnp.zeros_like(o_ref)
        o_ref[_PK_LR:_PK_LR + 1, :] = lr[...]
        o_ref[_PK_LI:_PK_LI + 1, :] = li[...]
        o_ref[_PK_GAINS:_PK_GAINS + 1, :D_MODEL] = gmix[...]
        o_ref[_PK_GAINS:_PK_GAINS + 1, D_MODEL:] = gffn[...]
        o_ref[_PK_MISC:_PK_MISC + 1, :D_MODEL] = gfin[...]
        o_ref[_PK_MISC:_PK_MISC + 1, D_MODEL:D_MODEL + SSM_W] = dskip[...]
        o_ref[_PK_MISC:_PK_MISC + 1, _PK_LDT_LANE:_PK_LDT_LANE + LANES] = ldt[0:1, :]
        o_ref[_PK_MISC:_PK_MISC + 1, _PK_LOSS_LANE:_PK_LOSS_LANE + LANES] = loss[0:1, :]
        o_ref[_PK_BR:_PK_BR + SSM_CH, :] = br[...]
        o_ref[_PK_BI:_PK_BI + SSM_CH, :] = bi[...]
        o_ref[_PK_CR:_PK_CR + SSM_CH, :] = cr[...]
        o_ref[_PK_CI:_PK_CI + SSM_CH, :] = ci[...]

    return _pallas_call(body, name="pack_small", out_shape=jax.ShapeDtypeStruct((_PK_ROWS, N_STATE), F32))(
        *[sg[n] for n in names])


def _unpack_small(s):
    unflat_b = lambda a: a.reshape(SSM_CH, SSM_GROUPS, SSM_STATE).transpose(1, 2, 0)[None]
    unflat_c = lambda a: a.reshape(SSM_CH, SSM_GROUPS, SSM_STATE).transpose(1, 0, 2)[None]
    grads = {
        "norm_mix_g": s[_PK_GAINS, :D_MODEL].reshape(1, D_MODEL), "norm_ffn_g": s[_PK_GAINS, D_MODEL:].reshape(1, D_MODEL),
        "norm_final_g": s[_PK_MISC, :D_MODEL],
        "ssm_a_re": s[_PK_LR].reshape(1, SSM_GROUPS, SSM_STATE), "ssm_a_im": s[_PK_LI].reshape(1, SSM_GROUPS, SSM_STATE),
        "ssm_log_dt": s[_PK_MISC, _PK_LDT_LANE:_PK_LDT_LANE + SSM_GROUPS].reshape(1, SSM_GROUPS),
        "ssm_d": s[_PK_MISC, D_MODEL:D_MODEL + SSM_W].reshape(1, SSM_GROUPS, SSM_CH),
        "ssm_b_re": unflat_b(s[_PK_BR:_PK_BR + SSM_CH]), "ssm_b_im": unflat_b(s[_PK_BI:_PK_BI + SSM_CH]),
        "ssm_c_re": unflat_c(s[_PK_CR:_PK_CR + SSM_CH]), "ssm_c_im": unflat_c(s[_PK_CI:_PK_CI + SSM_CH]),
    }
    return s[_PK_MISC, _PK_LOSS_LANE], grads


def _adam_small(grads, wts, moms, vars_):
    n = len(SMALL_WEIGHTS)
    as2d = lambda a: a.reshape(1, -1) if a.ndim == 1 else a

    def body(*refs):
        ins, outs = refs[:4 * n], refs[4 * n:]
        for i in range(n):
            g, w, m, v = (ins[j * n + i][...] for j in range(4))
            outs[i][...], outs[n + i][...], outs[2 * n + i][...] = _adam_math(g, w, m, v)

    operands = [as2d(d[k]) for d in (grads, wts, moms, vars_) for k in SMALL_WEIGHTS]
    shapes = [jax.ShapeDtypeStruct(as2d(wts[k]).shape, F32) for k in SMALL_WEIGHTS] * 3
    res = _pallas_call(body, name="adam_small", out_shape=shapes,
                         compiler_params=pltpu.CompilerParams(vmem_limit_bytes=VMEM_BIG))(*operands)
    out = {}
    for j, kind in enumerate(("delta", "new_m", "new_v")):
        for i, k in enumerate(SMALL_WEIGHTS):
            out[kind, k] = res[j * n + i].reshape(wts[k].shape)
    return out


def kernel(x, norm_mix_g, w_in, ssm_a_re, ssm_a_im, ssm_log_dt, ssm_b_re, ssm_b_im, ssm_c_re, ssm_c_im, ssm_d, w_glu, w_attn_out, w_out, norm_ffn_g, w_ffn_gate, w_ffn_up, w_ffn_down, norm_final_g, loss_target, m_norm_mix_g, m_w_in, m_ssm_a_re, m_ssm_a_im, m_ssm_log_dt, m_ssm_b_re, m_ssm_b_im, m_ssm_c_re, m_ssm_c_im, m_ssm_d, m_w_glu, m_w_attn_out, m_w_out, m_norm_ffn_g, m_w_ffn_gate, m_w_ffn_up, m_w_ffn_down, m_norm_final_g, v_norm_mix_g, v_w_in, v_ssm_a_re, v_ssm_a_im, v_ssm_log_dt, v_ssm_b_re, v_ssm_b_im, v_ssm_c_re, v_ssm_c_im, v_ssm_d, v_w_glu, v_w_attn_out, v_w_out, v_norm_ffn_g, v_w_ffn_gate, v_w_ffn_up, v_w_ffn_down, v_norm_final_g):
    args = dict(locals())
    wts = {n: args[n] for n in ALL_WEIGHTS}
    moms = {n: args["m_" + n] for n in ALL_WEIGHTS}
    vars_ = {n: args["v_" + n] for n in ALL_WEIGHTS}
    n_samples = x.shape[0]
    t = n_samples * SEQ

    shards = {n: (wts[n][0] if n in ROW_SHARDED else wts[n][0].T).astype(BF16) for n in BIG_WEIGHTS}
    w_in_t = _all_gather(shards["w_in"], "allgather_w_in").reshape(IN_W, D_MODEL)

    small = {n: wts[n] for n in SMALL_WEIGHTS}
    _, grad_x, recv, small_g = _local_step(x.reshape(t, D_MODEL), loss_target.reshape(t, D_MODEL), {"w_in": w_in_t},
                                           small, shards)

    results = {}
    for n in BIG_WEIGHTS:
        c, k = shards[n].shape
        w2, m2, v2 = wts[n][0], moms[n][0], vars_[n][0]
        if n in ROW_SHARDED:
            res = _adam(recv[n], w2, m2, v2, "adam_" + n, c // 2)
        else:
            g_t = _sum_partials(recv[n], "sum_" + n, c // 2)
            res = _adam(g_t.T[None], w2, m2, v2, "adam_" + n, k // 2)
        for kind, a in zip(("grad", "delta", "new_m", "new_v"), res):
            results[kind, n] = a[None]

    sgath = _all_gather(_pack_small(small_g), "allgather_small_grads")
    loss, sgrads = _unpack_small(_sum_partials(sgath, "sum_small", _PK_ROWS))
    for n in SMALL_WEIGHTS:
        results["grad", n] = sgrads[n]
    results.update(_adam_small(sgrads, wts, moms, vars_))
    outs = [loss, grad_x.reshape(x.shape)]
    for kind in ("grad", "delta", "new_m", "new_v"):
        outs += [results[kind, n] for n in ALL_WEIGHTS]
    return tuple(outs)
```

```python
import functools
import math

import jax
import jax.numpy as jnp
from jax import lax
from jax.experimental import pallas as pl
from jax.experimental.pallas import tpu as pltpu

F32 = jnp.float32
BF16 = jnp.bfloat16
MXU_DTYPE = jnp.bfloat16

N_DEV = 8
D_MODEL = 1024
SEQ = 2048
HEAD_DIM = 64
HEADS_PER_GROUP = 4
GROUP_W = HEADS_PER_GROUP * HEAD_DIM
DILATIONS = (1, 4, 16)
QKV_W = 3 * len(DILATIONS) * GROUP_W
Q_W = len(DILATIONS) * GROUP_W
ATT_BLOCK = 128
ROPE_DIM = 16
ROPE_THETA = 500000.0
SSM_W = 512
SSM_GROUPS = 32
SSM_CH = 16
SSM_STATE = 64
N_STATE = SSM_GROUPS * SSM_STATE
D_FF = 2816
IN_W = QKV_W + SSM_W + 2 * D_MODEL
RMS_EPS = 1e-6
NEG_INF = -1e30
LANES = 128

SCAN_SEG_PER_SAMPLE = 8
SCAN_LEN = SEQ // SCAN_SEG_PER_SAMPLE
SCAN_WC = 512
SCAN_NBLK = N_STATE // SCAN_WC
SCAN_CH = SSM_W // SCAN_NBLK
SCAN_CHUNK = 32

ADAM_LR = 0.001
ADAM_B1 = 0.9
ADAM_B2 = 0.999
ADAM_EPS = 1e-08
ADAM_WD = 0.01
ADAM_STEP = 10

VMEM_BIG = 48 * 1024 * 1024
VMEM_MID = 32 * 1024 * 1024

BIG_WEIGHTS = ("w_in", "w_glu", "w_attn_out", "w_out", "w_ffn_gate", "w_ffn_up", "w_ffn_down")
ROW_SHARDED = ("w_out", "w_ffn_down")
SMALL_WEIGHTS = ("norm_mix_g", "ssm_a_re", "ssm_a_im", "ssm_log_dt", "ssm_b_re", "ssm_b_im", "ssm_c_re", "ssm_c_im",
                 "ssm_d", "norm_ffn_g", "norm_final_g")
ALL_WEIGHTS = ("norm_mix_g", "w_in", "ssm_a_re", "ssm_a_im", "ssm_log_dt", "ssm_b_re", "ssm_b_im", "ssm_c_re", "ssm_c_im",
               "ssm_d", "w_glu", "w_attn_out", "w_out", "norm_ffn_g", "w_ffn_gate", "w_ffn_up", "w_ffn_down", "norm_final_g")


def _sigmoid(x):
    return 1.0 / (1.0 + jnp.exp(-x))


def _pallas_call(body, *, out_shape, **kw):
    single = not isinstance(out_shape, (list, tuple))
    shapes = [pltpu.HBM(s.shape, s.dtype) for s in ([out_shape] if single else out_shape)]
    call = pl.pallas_call(body, out_shape=shapes[0] if single else shapes, **kw)
    return lambda *operands: call(*[pltpu.with_memory_space_constraint(o, pltpu.HBM) for o in operands])


class _Comm:
    def __init__(self, ins, out_shapes, n_sem, n_local, start, finish):
        self.ins, self.out_shapes, self.n_sem, self.n_local = ins, out_shapes, n_sem, n_local
        self.start, self.finish = start, finish


def _mm(a, b, mode, name, tm, tn, out_dtype=F32, add=None, vmem=VMEM_BIG, comm=None):
    if mode == "nn":
        (m, k), (_, n) = a.shape, b.shape
        a_spec = pl.BlockSpec((tm, k), lambda i, j: (i, 0))
        b_spec = pl.BlockSpec((k, tn), lambda i, j: (0, j))
        dims = (((1,), (0,)), ((), ()))
    elif mode == "nt":
        (m, k), (n, _) = a.shape, b.shape
        a_spec = pl.BlockSpec((tm, k), lambda i, j: (i, 0))
        b_spec = pl.BlockSpec((tn, k), lambda i, j: (j, 0))
        dims = (((1,), (1,)), ((), ()))
    else:
        (k, m), (_, n) = a.shape, b.shape
        a_spec = pl.BlockSpec((k, tm), lambda i, j: (0, i))
        b_spec = pl.BlockSpec((k, tn), lambda i, j: (0, j))
        dims = (((0,), (0,)), ((), ()))
    assert m % tm == 0 and n % tn == 0, (name, m, n, tm, tn)
    o_spec = pl.BlockSpec((tm, tn), lambda i, j: (i, j))
    has_add = add is not None

    def body(*refs):
        a_ref, b_ref, o_ref = refs[0], refs[1], refs[-1]
        acc = lax.dot_general(a_ref[...].astype(MXU_DTYPE), b_ref[...].astype(MXU_DTYPE), dims,
                              preferred_element_type=F32)
        if has_add:
            acc = acc + refs[2][...]
        o_ref[...] = acc.astype(out_dtype)

    ins = [a, b] + ([add] if has_add else [])
    in_specs = [a_spec, b_spec] + ([o_spec] if has_add else [])
    return _grid_call(body, name, (m // tm, n // tn), ins, in_specs, [o_spec],
                      [jax.ShapeDtypeStruct((m, n), out_dtype)], vmem, comm)


def _grid_call(body, name, grid, ins, in_specs, out_specs, out_shapes, vmem, comm=None, sequential=False):
    if comm is None:
        single = len(out_shapes) == 1
        semantics = ("arbitrary", "arbitrary") if sequential else ("parallel", "parallel")
        return _pallas_call(
            body, name=name, grid=grid, in_specs=in_specs, out_specs=out_specs[0] if single else out_specs,
            out_shape=out_shapes[0] if single else out_shapes,
            compiler_params=pltpu.CompilerParams(dimension_semantics=semantics, vmem_limit_bytes=vmem),
        )(*ins)
    n_in, n_out, n_cin, n_cout = len(ins), len(out_shapes), len(comm.ins), len(comm.out_shapes)

    def carrying(*refs):
        own = refs[:n_in] + refs[n_in + n_cin:n_in + n_cin + n_out]
        c_args = (refs[n_in:n_in + n_cin], refs[n_in + n_cin + n_out:n_in + n_cin + n_out + n_cout], *refs[-3:])

        @pl.when((pl.program_id(0) == 0) & (pl.program_id(1) == 0))
        def _():
            comm.start(*c_args)

        body(*own)

        @pl.when((pl.program_id(0) == grid[0] - 1) & (pl.program_id(1) == grid[1] - 1))
        def _():
            comm.finish(*c_args)

    hbm = pl.BlockSpec(memory_space=pl.ANY)
    return _pallas_call(
        carrying, name=name, grid=grid, in_specs=list(in_specs) + [hbm] * n_cin,
        out_specs=list(out_specs) + [hbm] * n_cout, out_shape=list(out_shapes) + list(comm.out_shapes),
        scratch_shapes=[pltpu.SemaphoreType.DMA((comm.n_sem,)), pltpu.SemaphoreType.DMA((comm.n_sem,)),
                        pltpu.SemaphoreType.DMA((comm.n_local,))],
        compiler_params=pltpu.CompilerParams(dimension_semantics=("arbitrary", "arbitrary"), vmem_limit_bytes=vmem),
    )(*ins, *comm.ins)


def _rows(body, name, n_rows, tm, ins, outs, vmem=VMEM_MID):
    assert n_rows % tm == 0
    arrays, in_specs = [], []
    for kind, arr in ins:
        arrays.append(arr)
        if kind == "row":
            assert arr.shape[0] == n_rows, (name, arr.shape)
            in_specs.append(pl.BlockSpec((tm, arr.shape[1]), lambda i: (i, 0)))
        elif kind == "tab":
            nblk = arr.shape[0] // tm
            in_specs.append(pl.BlockSpec((tm, arr.shape[1]), lambda i, nblk=nblk: (i % nblk, 0)))
        else:
            in_specs.append(pl.BlockSpec(arr.shape, lambda i, nd=arr.ndim: (0,) * nd))
    out_specs, out_shape = [], []
    for kind, shp, dt in outs:
        if kind == "row":
            out_specs.append(pl.BlockSpec((tm, shp), lambda i: (i, 0)))
            out_shape.append(jax.ShapeDtypeStruct((n_rows, shp), dt))
        else:
            out_specs.append(pl.BlockSpec(shp, lambda i, nd=len(shp): (0,) * nd))
            out_shape.append(jax.ShapeDtypeStruct(shp, dt))
    res = _pallas_call(
        body, name=name, grid=(n_rows // tm,), in_specs=in_specs, out_specs=out_specs, out_shape=out_shape,
        compiler_params=pltpu.CompilerParams(dimension_semantics=("arbitrary",), vmem_limit_bytes=vmem),
    )(*arrays)
    return res


def _first_step():
    return pl.program_id(0) == 0


def _rope_tables():
    half = ROPE_DIM // 2
    inv = jnp.power(jnp.float32(ROPE_THETA), -jnp.arange(half, dtype=F32) * 2.0 / ROPE_DIM)
    ang = jnp.arange(SEQ, dtype=F32)[:, None] * inv[None, :]
    lane = jnp.arange(LANES) % HEAD_DIM
    cosl = jnp.cos(ang)[:, lane % half]
    sinl = jnp.sin(ang)[:, lane % half]
    tab_c = jnp.where(lane < ROPE_DIM, cosl, 1.0)
    tab_lo = jnp.where(lane < half, -sinl, 0.0)
    tab_hi = jnp.where((lane >= half) & (lane < ROPE_DIM), sinl, 0.0)
    return tab_c.astype(F32), tab_lo.astype(F32), tab_hi.astype(F32)


def _rope_apply(t, tc, tlo, thi):
    half = ROPE_DIM // 2
    return t * tc + pltpu.roll(t, LANES - half, 1) * tlo + pltpu.roll(t, half, 1) * thi


def _rope_transpose(dt, tc, tlo, thi):
    half = ROPE_DIM // 2
    return dt * tc + pltpu.roll(dt * tlo, half, 1) + pltpu.roll(dt * thi, LANES - half, 1)


def _pack_dproj(dqs, dks, dvs, du, dgpre, tabs):
    def body(*refs):
        dq_refs, dk_refs, dv_refs = refs[0:3], refs[3:6], refs[6:9]
        du_ref, dg_ref, tc_ref, tlo_ref, thi_ref, o_ref = refs[9:15]
        tc, tlo, thi = tc_ref[...], tlo_ref[...], thi_ref[...]
        for g in range(3):
            for half in range(GROUP_W // LANES):
                sl = slice(half * LANES, (half + 1) * LANES)
                col = g * GROUP_W + half * LANES
                o_ref[:, col:col + LANES] = _rope_transpose(dq_refs[g][:, sl], tc, tlo, thi).astype(BF16)
                o_ref[:, Q_W + col:Q_W + col + LANES] = _rope_transpose(dk_refs[g][:, sl], tc, tlo, thi).astype(BF16)
            o_ref[:, 2 * Q_W + g * GROUP_W:2 * Q_W + (g + 1) * GROUP_W] = dv_refs[g][...].astype(BF16)
        o_ref[:, QKV_W:QKV_W + SSM_W] = du_ref[...].astype(BF16)
        o_ref[:, QKV_W + SSM_W:] = dg_ref[...].astype(BF16)

    t = du.shape[0]
    ins = [("row", a) for a in (*dqs, *dks, *dvs, du, dgpre)] + [("tab", tb) for tb in tabs]
    return _rows(body, "pack_dproj", t, 256, ins, [("row", IN_W, BF16)])[0]


def _attn_merge(os_, lses):
    def body(o0, o1, o2, l0, l1, l2, a_ref, lt_ref):
        la, lb, lc = l0[...], l1[...], l2[...]
        m = jnp.maximum(jnp.maximum(la, lb), lc)
        ea, eb, ec = jnp.exp(la - m), jnp.exp(lb - m), jnp.exp(lc - m)
        ssum = ea + eb + ec
        a_ref[...] = (ea / ssum) * o0[...] + (eb / ssum) * o1[...] + (ec / ssum) * o2[...]
        lt_ref[...] = m + jnp.log(ssum)

    t = os_[0].shape[0]
    return _rows(body, "attn_merge", t, 512, [("row", a) for a in (*os_, *lses)],
                 [("row", GROUP_W, F32), ("row", GROUP_W, F32)])


def _head_sum_matrix():
    r = jnp.arange(GROUP_W) // HEAD_DIM
    return (r[:, None] == r[None, :]).astype(F32)


def _attn_rowdot(dattn, attn):
    def body(da_ref, a_ref, ones_ref, d_ref):
        d_ref[...] = jnp.dot(da_ref[...] * a_ref[...], ones_ref[...], preferred_element_type=F32,
                             precision=lax.Precision.HIGHEST)

    t = attn.shape[0]
    return _rows(body, "attn_rowdot", t, 512, [("row", dattn), ("row", attn), ("const", _head_sum_matrix())],
                 [("row", GROUP_W, F32)])[0]


def _mix(attn_d, z, gates):
    def body(ad_ref, z_ref, g_ref, m_ref):
        za, zb = z_ref[:, :D_MODEL], z_ref[:, D_MODEL:]
        s_out = za * _sigmoid(zb)
        m_ref[...] = (g_ref[:, :D_MODEL] * ad_ref[...] + g_ref[:, D_MODEL:] * s_out).astype(BF16)

    t = attn_d.shape[0]
    return _rows(body, "mix", t, 256, [("row", attn_d), ("row", z), ("row", gates)], [("row", D_MODEL, BF16)])[0]


def _mix_bwd(dmerged, gates, attn_d, z):
    def body(dm_ref, g_ref, ad_ref, z_ref, dad_ref, dz_ref, dg_ref):
        dm = dm_ref[...]
        g0, g1 = g_ref[:, :D_MODEL], g_ref[:, D_MODEL:]
        za, zb = z_ref[:, :D_MODEL], z_ref[:, D_MODEL:]
        sb = _sigmoid(zb)
        s_out = za * sb
        dad_ref[...] = (dm * g0).astype(BF16)
        ds = dm * g1
        dz_ref[:, :D_MODEL] = (ds * sb).astype(BF16)
        dz_ref[:, D_MODEL:] = (ds * za * sb * (1.0 - sb)).astype(BF16)
        dg_ref[:, :D_MODEL] = dm * ad_ref[...] * g0 * (1.0 - g0)
        dg_ref[:, D_MODEL:] = dm * s_out * g1 * (1.0 - g1)

    t = dmerged.shape[0]
    return _rows(body, "mix_bwd", t, 256, [("row", dmerged), ("row", gates), ("row", attn_d), ("row", z)],
                 [("row", D_MODEL, BF16), ("row", 2 * D_MODEL, BF16), ("row", 2 * D_MODEL, F32)])


_GELU_C = math.sqrt(2.0 / math.pi)


def _ssm_act_bwd(dyg, ytot, u_perm, dskip):
    def body(dyg_ref, yt_ref, u_ref, d_ref, dy_ref, dus_ref, dd_ref):
        @pl.when(_first_step())
        def _():
            dd_ref[...] = jnp.zeros_like(dd_ref)

        yt = yt_ref[...]
        th = jnp.tanh(_GELU_C * (yt + 0.044715 * (yt * yt * yt)))
        dgelu = 0.5 * (1.0 + th) + 0.5 * yt * (1.0 - th * th) * _GELU_C * (1.0 + 3.0 * 0.044715 * yt * yt)
        dy = dyg_ref[...] * dgelu
        dy_ref[...] = dy.astype(BF16)
        dus_ref[...] = dy * d_ref[...]
        dd_ref[...] += jnp.sum(dy * u_ref[...], axis=0, keepdims=True)

    t = dyg.shape[0]
    return _rows(body, "ssm_act_bwd", t, 512, [("row", dyg), ("row", ytot), ("row", u_perm), ("const", dskip)],
                 [("row", SSM_W, BF16), ("row", SSM_W, F32), ("acc", (1, SSM_W), F32)])


def _head_masks():
    lane = lax.broadcasted_iota(jnp.int32, (1, GROUP_W), 1)
    return [(lane // HEAD_DIM) == h for h in range(HEADS_PER_GROUP)]


def _band_mask(first):
    nk = ATT_BLOCK if first else 2 * ATT_BLOCK
    qi = lax.broadcasted_iota(jnp.int32, (ATT_BLOCK, nk), 0)
    ki = lax.broadcasted_iota(jnp.int32, (ATT_BLOCK, nk), 1)
    dist = qi - ki + (0 if first else ATT_BLOCK)
    return (dist >= 0) & (dist <= ATT_BLOCK)


_NT = (((1,), (1,)), ((), ()))
_TN = (((0,), (0,)), ((), ()))


def _attn_fwd(q, k, v, group, n_samples, comm=None):
    d = DILATIONS[group]
    length = SEQ // d
    nb = length // ATT_BLOCK

    def body(q_ref, k_ref, v_ref, o_ref, l_ref):
        masks = _head_masks()

        def block(qs, ks, first):
            nk = ATT_BLOCK if first else 2 * ATT_BLOCK
            qb = q_ref[0, pl.ds(qs, ATT_BLOCK), :]
            kc = k_ref[0, pl.ds(ks, nk), :]
            vc = v_ref[0, pl.ds(ks, nk), :]
            valid = _band_mask(first)
            o_acc = jnp.zeros((ATT_BLOCK, GROUP_W), F32)
            l_acc = jnp.zeros((ATT_BLOCK, GROUP_W), F32)
            for h in range(HEADS_PER_GROUP):
                qh = jnp.where(masks[h], qb, jnp.zeros_like(qb))
                s = lax.dot_general(qh, kc, _NT, preferred_element_type=F32) * (HEAD_DIM ** -0.5)
                s = jnp.where(valid, s, NEG_INF)
                m = jnp.max(s, axis=-1, keepdims=True)
                p = jnp.exp(s - m)
                l = jnp.sum(p, axis=-1, keepdims=True)
                pv = jnp.dot(p.astype(MXU_DTYPE), vc, preferred_element_type=F32)
                o_acc = jnp.where(masks[h], pv / l, o_acc)
                l_acc = jnp.where(masks[h], m + jnp.log(l), l_acc)
            o_ref[0, pl.ds(qs, ATT_BLOCK), :] = o_acc
            l_ref[0, pl.ds(qs, ATT_BLOCK), :] = l_acc

        block(0, 0, True)
        if nb > 1:
            def loop(n, carry):
                block(pl.multiple_of(n * ATT_BLOCK, ATT_BLOCK), pl.multiple_of((n - 1) * ATT_BLOCK, ATT_BLOCK), False)
                return carry

            lax.fori_loop(1, nb, loop, 0)

    qv = q.reshape(n_samples, length, d * Q_W)
    kv = k.reshape(n_samples, length, d * Q_W)
    vv = v.reshape(n_samples, length, d * Q_W)
    in_spec = pl.BlockSpec((1, length, GROUP_W), lambda b, r: (b, 0, 3 * r + group))
    out_spec = pl.BlockSpec((1, length, GROUP_W), lambda b, r: (b, 0, r))
    shp = jax.ShapeDtypeStruct((n_samples, length, d * GROUP_W), F32)
    o, lse, *carried = _grid_call(body, f"attn_fwd_g{group}", (n_samples, d), [qv, kv, vv], [in_spec] * 3,
                                  [out_spec] * 2, [shp, shp], VMEM_MID, comm)
    return o.reshape(n_samples * SEQ, GROUP_W), lse.reshape(n_samples * SEQ, GROUP_W), carried


def _attn_bwd(q, k, v, dattn, lse_tot, rowdot, group, n_samples, comm=None):
    d = DILATIONS[group]
    length = SEQ // d
    nb = length // ATT_BLOCK

    def body(q_ref, k_ref, v_ref, da_ref, lt_ref, rd_ref, dq_ref, dk_ref, dv_ref):
        masks = _head_masks()
        dk_ref[...] = jnp.zeros_like(dk_ref)
        dv_ref[...] = jnp.zeros_like(dv_ref)

        def block(qs, ks, first):
            nk = ATT_BLOCK if first else 2 * ATT_BLOCK
            qb = q_ref[0, pl.ds(qs, ATT_BLOCK), :]
            kc = k_ref[0, pl.ds(ks, nk), :]
            vc = v_ref[0, pl.ds(ks, nk), :]
            da = da_ref[0, pl.ds(qs, ATT_BLOCK), :]
            lt = lt_ref[0, pl.ds(qs, ATT_BLOCK), :]
            rd = rd_ref[0, pl.ds(qs, ATT_BLOCK), :]
            valid = _band_mask(first)
            dq_acc = jnp.zeros((ATT_BLOCK, GROUP_W), F32)
            dk_acc = jnp.zeros((nk, GROUP_W), F32)
            dv_acc = jnp.zeros((nk, GROUP_W), F32)
            for h in range(HEADS_PER_GROUP):
                qh = jnp.where(masks[h], qb, jnp.zeros_like(qb))
                dah = jnp.where(masks[h], da, 0.0).astype(MXU_DTYPE)
                lt_h = jnp.max(jnp.where(masks[h], lt, -jnp.inf), axis=-1, keepdims=True)
                rd_h = jnp.max(jnp.where(masks[h], rd, -jnp.inf), axis=-1, keepdims=True)
                s = lax.dot_general(qh, kc, _NT, preferred_element_type=F32) * (HEAD_DIM ** -0.5)
                s = jnp.where(valid, s, NEG_INF)
                p = jnp.exp(s - lt_h)
                dp = lax.dot_general(dah, vc, _NT, preferred_element_type=F32)
                ds = (p * (dp - rd_h) * (HEAD_DIM ** -0.5)).astype(MXU_DTYPE)
                dq_h = jnp.dot(ds, kc, preferred_element_type=F32)
                dq_acc = jnp.where(masks[h], dq_h, dq_acc)
                dk_acc = dk_acc + lax.dot_general(ds, qh, _TN, preferred_element_type=F32)
                dv_acc = dv_acc + lax.dot_general(p.astype(MXU_DTYPE), dah, _TN, preferred_element_type=F32)
            dq_ref[0, pl.ds(qs, ATT_BLOCK), :] = dq_acc
            dk_ref[0, pl.ds(ks, nk), :] += dk_acc
            dv_ref[0, pl.ds(ks, nk), :] += dv_acc

        block(0, 0, True)
        if nb > 1:
            def loop(n, carry):
                block(pl.multiple_of(n * ATT_BLOCK, ATT_BLOCK), pl.multiple_of((n - 1) * ATT_BLOCK, ATT_BLOCK), False)
                return carry

            lax.fori_loop(1, nb, loop, 0)

    qv = q.reshape(n_samples, length, d * Q_W)
    kv = k.reshape(n_samples, length, d * Q_W)
    vv = v.reshape(n_samples, length, d * Q_W)
    nat = lambda a: a.reshape(n_samples, length, d * GROUP_W)
    in_spec = pl.BlockSpec((1, length, GROUP_W), lambda b, r: (b, 0, 3 * r + group))
    nat_spec = pl.BlockSpec((1, length, GROUP_W), lambda b, r: (b, 0, r))
    shp = jax.ShapeDtypeStruct((n_samples, length, d * GROUP_W), F32)
    dq, dk, dv, *carried = _grid_call(
        body, f"attn_bwd_g{group}", (n_samples, d), [qv, kv, vv, nat(dattn), nat(lse_tot), nat(rowdot)],
        [in_spec] * 3 + [nat_spec] * 3, [nat_spec] * 3, [shp, shp, shp], VMEM_MID, comm)
    t = n_samples * SEQ
    return dq.reshape(t, GROUP_W), dk.reshape(t, GROUP_W), dv.reshape(t, GROUP_W), carried


def _disc(lr, li, ldt, br, bi):
    dt = jnp.exp(ldt)
    mag = jnp.exp(lr * dt)
    ab_re, ab_im = mag * jnp.cos(li * dt), mag * jnp.sin(li * dt)
    den = lr * lr + li * li
    nr, ni = ab_re - 1.0, ab_im
    f_re = (nr * lr + ni * li) / den
    f_im = (ni * lr - nr * li) / den
    return ab_re, ab_im, f_re * br - f_im * bi, f_re * bi + f_im * br


def _state_mask():
    row_g = lax.broadcasted_iota(jnp.int32, (SCAN_CH, SCAN_WC), 0) // SSM_CH
    col_g = lax.broadcasted_iota(jnp.int32, (SCAN_CH, SCAN_WC), 1) // SSM_STATE
    return row_g == col_g


def _ssm_disc(lr, li, ldt, br, bi, cr, ci):
    w = SCAN_WC

    def body(lr_ref, li_ref, ldt_ref, br_ref, bi_ref, cr_ref, ci_ref, a_ref, bb_ref, c_ref):
        ar, ai, bbr, bbi = _disc(lr_ref[...], li_ref[...], ldt_ref[...], br_ref[...], bi_ref[...])
        crv, civ = cr_ref[...], ci_ref[...]
        mask = _state_mask()
        for cb in range(SCAN_NBLK):
            sl = slice(cb * w, (cb + 1) * w)
            rows = slice(cb * SCAN_CH, (cb + 1) * SCAN_CH)
            dense = lambda comp: jnp.where(mask, jnp.tile(comp[:, sl], (SCAN_CH // SSM_CH, 1)), 0.0)
            a_ref[:, 2 * cb * w:(2 * cb + 1) * w] = ar[:, sl]
            a_ref[:, (2 * cb + 1) * w:(2 * cb + 2) * w] = ai[:, sl]
            bb_ref[rows, :w] = dense(bbr).astype(MXU_DTYPE)
            bb_ref[rows, w:] = dense(bbi).astype(MXU_DTYPE)
            c_ref[rows, :w] = dense(crv).astype(MXU_DTYPE)
            c_ref[rows, w:] = (-dense(civ)).astype(MXU_DTYPE)

    return _pallas_call(
        body, name="ssm_disc",
        out_shape=[jax.ShapeDtypeStruct((1, 2 * N_STATE), F32), jax.ShapeDtypeStruct((SSM_W, 2 * w), MXU_DTYPE),
                   jax.ShapeDtypeStruct((SSM_W, 2 * w), MXU_DTYPE)],
        compiler_params=pltpu.CompilerParams(vmem_limit_bytes=VMEM_MID),
    )(lr, li, ldt, br, bi, cr, ci)


def _group_indicator():
    s = jnp.arange(N_STATE) // SSM_STATE
    return (s[:, None] == jnp.arange(LANES)[None, :]).astype(F32)


def _ssm_param_bwd(lr, li, ldt, br, bi, da_cat, dbb_full, dc_full):
    w = SCAN_WC

    def body(lr_ref, li_ref, ldt_ref, br_ref, bi_ref, da_ref, dbb_ref, dc_ref, ind_ref,
             glr_ref, gli_ref, gldt_ref, gbr_ref, gbi_ref, gcr_ref, gci_ref):
        mask = _state_mask()

        def diag_parts(ref):
            res = ([], [])
            for cb in range(SCAN_NBLK):
                for part in range(2):
                    blk = ref[cb * SCAN_CH:(cb + 1) * SCAN_CH, part * w:(part + 1) * w]
                    res[part].append(jnp.sum(jnp.where(mask, blk, 0.0).reshape(SCAN_CH // SSM_CH, SSM_CH, w), axis=0))
            return jnp.concatenate(res[0], axis=1), jnp.concatenate(res[1], axis=1)

        dar = jnp.concatenate([da_ref[:, 2 * cb * w:(2 * cb + 1) * w] for cb in range(SCAN_NBLK)], axis=1)
        dai = jnp.concatenate([da_ref[:, (2 * cb + 1) * w:(2 * cb + 2) * w] for cb in range(SCAN_NBLK)], axis=1)
        dbbr, dbbi = diag_parts(dbb_ref)
        dcr, dci_neg = diag_parts(dc_ref)
        gcr_ref[...] = dcr
        gci_ref[...] = -dci_neg
        _, vjp = jax.vjp(_disc, lr_ref[...], li_ref[...], ldt_ref[...], br_ref[...], bi_ref[...])
        glr, gli, gldt, gbr, gbi = vjp((dar, dai, dbbr, dbbi))
        glr_ref[...] = glr
        gli_ref[...] = gli
        gldt_ref[...] = jnp.dot(jnp.broadcast_to(gldt, (8, N_STATE)), ind_ref[...], preferred_element_type=F32,
                                precision=lax.Precision.HIGHEST)
        gbr_ref[...] = gbr
        gbi_ref[...] = gbi

    v1 = jax.ShapeDtypeStruct((1, N_STATE), F32)
    v16 = jax.ShapeDtypeStruct((SSM_CH, N_STATE), F32)
    vdt = jax.ShapeDtypeStruct((8, LANES), F32)
    return _pallas_call(
        body, name="ssm_param_bwd", out_shape=[v1, v1, vdt, v16, v16, v16, v16],
        compiler_params=pltpu.CompilerParams(vmem_limit_bytes=VMEM_BIG),
    )(lr, li, ldt, br, bi, da_cat, dbb_full, dc_full, _group_indicator())


def _cmul(ar, ai, br, bi):
    return ar * br - ai * bi, ar * bi + ai * br


def _gelu_tanh(y):
    return jnp.tanh(_GELU_C * (y + 0.044715 * (y * y * y)))


def _segment_carry(er, ei, ar, ai, n_rows, reverse):
    qr, qi = ar, ai
    for _ in range(int(math.log2(SCAN_LEN))):
        qr, qi = _cmul(qr, qi, qr, qi)
    seg = lax.broadcasted_iota(jnp.int32, er.shape, 0) % SCAN_SEG_PER_SAMPLE
    shift = 1
    while shift < SCAN_SEG_PER_SAMPLE:
        keep = (seg < SCAN_SEG_PER_SAMPLE - shift) if reverse else (seg >= shift)
        amount = n_rows - shift if reverse else shift
        sr = jnp.where(keep, pltpu.roll(er, amount, 0), 0.0)
        si = jnp.where(keep, pltpu.roll(ei, amount, 0), 0.0)
        if reverse:
            er, ei = er + qr * sr + qi * si, ei + qr * si - qi * sr
        else:
            er, ei = er + qr * sr - qi * si, ei + qr * si + qi * sr
        qr, qi = _cmul(qr, qi, qr, qi)
        shift *= 2
    keep = (seg < SCAN_SEG_PER_SAMPLE - 1) if reverse else (seg >= 1)
    amount = n_rows - 1 if reverse else 1
    return jnp.where(keep, pltpu.roll(er, amount, 0), 0.0), jnp.where(keep, pltpu.roll(ei, amount, 0), 0.0)


def _ssm_fwd(u_perm, a_cat, bbc, cc, dskip, n_rows):
    t = u_perm.shape[0]
    w = SCAN_WC
    rows_c = SCAN_CHUNK * n_rows
    n_chunks = t // rows_c

    def body(u_ref, a_ref, bb_ref, c_ref, d_ref, yt_ref, yg_ref, ein_ref, bu_s, xs_s):
        ar = jnp.broadcast_to(a_ref[:, :w], (n_rows, w))
        ai = jnp.broadcast_to(a_ref[:, w:], (n_rows, w))

        def sweep(carry, store):
            def chunk(ch, carry):
                r0 = pl.multiple_of(ch * rows_c, rows_c)
                u_c = u_ref[pl.ds(r0, rows_c), :]
                bu_s[...] = jnp.dot(u_c.astype(MXU_DTYPE), bb_ref[...], preferred_element_type=F32)

                def step(i, c):
                    o = pl.multiple_of(i * n_rows, n_rows)
                    blk = bu_s[pl.ds(o, n_rows), :]
                    nr = ar * c[0] - ai * c[1] + blk[:, :w]
                    ni = ar * c[1] + ai * c[0] + blk[:, w:]
                    if store:
                        xs_s[pl.ds(o, n_rows), :w] = nr
                        xs_s[pl.ds(o, n_rows), w:] = ni
                    return nr, ni

                carry = lax.fori_loop(0, SCAN_CHUNK, step, carry)
                if store:
                    y = lax.dot_general(xs_s[...].astype(MXU_DTYPE), c_ref[...], _NT, preferred_element_type=F32)
                    yt = y + d_ref[...] * u_c
                    yt_ref[pl.ds(r0, rows_c), :] = yt
                    yg_ref[pl.ds(r0, rows_c), :] = (0.5 * yt * (1.0 + _gelu_tanh(yt))).astype(BF16)
                return carry

            return lax.fori_loop(0, n_chunks, chunk, carry)

        zero = jnp.zeros((n_rows, w), F32)
        er, ei = sweep((zero, zero), False)
        cr, ci = _segment_carry(er, ei, ar, ai, n_rows, False)
        ein_ref[:, :w] = cr
        ein_ref[:, w:] = ci
        sweep((cr, ci), True)

    col = lambda width: pl.BlockSpec((t, width), lambda c: (0, c))
    wgt = pl.BlockSpec((SCAN_CH, 2 * w), lambda c: (c, 0))
    return _pallas_call(
        body, name="ssm_fwd", grid=(SCAN_NBLK,),
        in_specs=[col(SCAN_CH), pl.BlockSpec((1, 2 * w), lambda c: (0, c)), wgt, wgt,
                  pl.BlockSpec((1, SCAN_CH), lambda c: (0, c))],
        out_specs=[col(SCAN_CH), col(SCAN_CH), pl.BlockSpec((n_rows, 2 * w), lambda c: (0, c))],
        out_shape=[jax.ShapeDtypeStruct((t, SSM_W), F32), jax.ShapeDtypeStruct((t, SSM_W), BF16),
                   jax.ShapeDtypeStruct((n_rows, 2 * N_STATE), F32)],
        scratch_shapes=[pltpu.VMEM((rows_c, 2 * w), F32), pltpu.VMEM((rows_c, 2 * w), F32)],
        compiler_params=pltpu.CompilerParams(dimension_semantics=("parallel",), vmem_limit_bytes=VMEM_BIG),
    )(u_perm, a_cat, bbc, cc, dskip)


def _ssm_bwd(u_perm, dypre, du_skip, a_cat, bbc, cc, ein, n_rows):
    t = u_perm.shape[0]
    w = SCAN_WC
    rows_c = SCAN_CHUNK * n_rows
    n_chunks = t // rows_c

    def body(u_ref, dy_ref, dus_ref, a_ref, bb_ref, c_ref, ein_ref, du_ref, da_ref, dbb_ref, dc_ref, xs_all, tmp_s, g_s):
        ar = jnp.broadcast_to(a_ref[:, :w], (n_rows, w))
        ai = jnp.broadcast_to(a_ref[:, w:], (n_rows, w))
        zero = jnp.zeros((n_rows, w), F32)

        xs_all[0:n_rows, :] = ein_ref[...]

        def fwd_chunk(ch, carry):
            r0 = pl.multiple_of(ch * rows_c, rows_c)
            tmp_s[...] = jnp.dot(u_ref[pl.ds(r0, rows_c), :].astype(MXU_DTYPE), bb_ref[...], preferred_element_type=F32)

            def step(i, c):
                o = pl.multiple_of(i * n_rows, n_rows)
                blk = tmp_s[pl.ds(o, n_rows), :]
                nr = ar * c[0] - ai * c[1] + blk[:, :w]
                ni = ar * c[1] + ai * c[0] + blk[:, w:]
                xs_all[pl.ds(n_rows + r0 + o, n_rows), :w] = nr
                xs_all[pl.ds(n_rows + r0 + o, n_rows), w:] = ni
                return nr, ni

            return lax.fori_loop(0, SCAN_CHUNK, step, carry)

        lax.fori_loop(0, n_chunks, fwd_chunk, (ein_ref[:, :w], ein_ref[:, w:]))

        def load_dx(ch):
            r0 = pl.multiple_of(ch * rows_c, rows_c)
            tmp_s[...] = jnp.dot(dy_ref[pl.ds(r0, rows_c), :], c_ref[...], preferred_element_type=F32)
            return r0

        def back_steps(carry, store):
            def step(ii, c):
                o = pl.multiple_of((SCAN_CHUNK - 1 - ii) * n_rows, n_rows)
                blk = tmp_s[pl.ds(o, n_rows), :]
                gr = blk[:, :w] + ar * c[0] + ai * c[1]
                gi = blk[:, w:] + ar * c[1] - ai * c[0]
                if store:
                    g_s[pl.ds(o, n_rows), :w] = gr
                    g_s[pl.ds(o, n_rows), w:] = gi
                return gr, gi

            return lax.fori_loop(0, SCAN_CHUNK, step, carry)

        def first_sweep(cc_, carry):
            load_dx(n_chunks - 1 - cc_)
            return back_steps(carry, False)

        sr, si = lax.fori_loop(0, n_chunks, first_sweep, (zero, zero))
        gr0, gi0 = _segment_carry(sr, si, ar, ai, n_rows, True)

        dbb_ref[...] = jnp.zeros_like(dbb_ref)
        dc_ref[...] = jnp.zeros_like(dc_ref)
        da_ref[...] = jnp.zeros_like(da_ref)

        def second_sweep(cc_, carry):
            r0 = load_dx(n_chunks - 1 - cc_)
            carry = back_steps(carry, True)
            g = g_s[...]
            xp = xs_all[pl.ds(r0, rows_c), :]
            xc = xs_all[pl.ds(r0 + n_rows, rows_c), :]
            da_ref[:, :w] += jnp.sum(g[:, :w] * xp[:, :w] + g[:, w:] * xp[:, w:], axis=0, keepdims=True)
            da_ref[:, w:] += jnp.sum(g[:, w:] * xp[:, :w] - g[:, :w] * xp[:, w:], axis=0, keepdims=True)
            gb = g.astype(MXU_DTYPE)
            du_ref[pl.ds(r0, rows_c), :] = (lax.dot_general(gb, bb_ref[...], _NT, preferred_element_type=F32)
                                            + dus_ref[pl.ds(r0, rows_c), :])
            dbb_ref[...] += lax.dot_general(u_ref[pl.ds(r0, rows_c), :].astype(MXU_DTYPE), gb, _TN,
                                            preferred_element_type=F32)
            dc_ref[...] += lax.dot_general(dy_ref[pl.ds(r0, rows_c), :], xc.astype(MXU_DTYPE), _TN,
                                           preferred_element_type=F32)
            return carry

        lax.fori_loop(0, n_chunks, second_sweep, (gr0, gi0))

    col = lambda width: pl.BlockSpec((t, width), lambda c: (0, c))
    wgt = pl.BlockSpec((SCAN_CH, 2 * w), lambda c: (c, 0))
    row = pl.BlockSpec((1, 2 * w), lambda c: (0, c))
    return _pallas_call(
        body, name="ssm_bwd", grid=(SCAN_NBLK,),
        in_specs=[col(SCAN_CH), col(SCAN_CH), col(SCAN_CH), row, wgt, wgt, pl.BlockSpec((n_rows, 2 * w), lambda c: (0, c))],
        out_specs=[col(SCAN_CH), row, wgt, wgt],
        out_shape=[jax.ShapeDtypeStruct((t, SSM_W), F32), jax.ShapeDtypeStruct((1, 2 * N_STATE), F32),
                   jax.ShapeDtypeStruct((SSM_W, 2 * w), F32), jax.ShapeDtypeStruct((SSM_W, 2 * w), F32)],
        scratch_shapes=[pltpu.VMEM((t + n_rows, 2 * w), F32), pltpu.VMEM((rows_c, 2 * w), F32),
                        pltpu.VMEM((rows_c, 2 * w), F32)],
        compiler_params=pltpu.CompilerParams(dimension_semantics=("parallel",), vmem_limit_bytes=56 * 1024 * 1024),
    )(u_perm, dypre, du_skip, a_cat, bbc, cc, ein)


def _to_scan_rows(a, n_samples):
    c = a.shape[1]
    return a.reshape(n_samples, SCAN_SEG_PER_SAMPLE, SCAN_LEN, c).transpose(2, 0, 1, 3).reshape(-1, c)


def _from_scan_rows(a, n_samples):
    c = a.shape[1]
    return a.reshape(SCAN_LEN, n_samples, SCAN_SEG_PER_SAMPLE, c).transpose(1, 2, 0, 3).reshape(-1, c)


def _row_spec(tm, width):
    return pl.BlockSpec((tm, width), lambda i, j: (i, 0))


def _whole(arr):
    return pl.BlockSpec(arr.shape, lambda i, j: (0,) * arr.ndim)


def _proj_rope(x, g, w_in_t, tabs, comm=None):
    t = x.shape[0]
    tm = 256

    def body(x_ref, g_ref, w_ref, tc_ref, tlo_ref, thi_ref, h_ref, q_ref, k_ref, v_ref, u_ref, gate_ref):
        xv = x_ref[...]
        r = lax.rsqrt(jnp.mean(xv * xv, axis=-1, keepdims=True) + RMS_EPS)
        h = ((xv * r) * g_ref[...]).astype(BF16)
        h_ref[...] = h
        p = lax.dot_general(h.astype(MXU_DTYPE), w_ref[...], _NT, preferred_element_type=F32)
        tc, tlo, thi = tc_ref[...], tlo_ref[...], thi_ref[...]
        for ch in range(Q_W // LANES):
            sl = slice(ch * LANES, (ch + 1) * LANES)
            q_ref[:, sl] = _rope_apply(p[:, sl], tc, tlo, thi).astype(BF16)
            k_ref[:, sl] = _rope_apply(p[:, Q_W + ch * LANES:Q_W + (ch + 1) * LANES], tc, tlo, thi).astype(BF16)
        v_ref[...] = p[:, 2 * Q_W:QKV_W].astype(BF16)
        u_ref[...] = p[:, QKV_W:QKV_W + SSM_W]
        gate_ref[...] = _sigmoid(p[:, QKV_W + SSM_W:])

    tab = pl.BlockSpec((tm, LANES), lambda i, j: (i % (SEQ // tm), 0))
    widths = [(D_MODEL, BF16), (Q_W, BF16), (Q_W, BF16), (Q_W, BF16), (SSM_W, F32), (2 * D_MODEL, F32)]
    return _grid_call(
        body, "proj_rope", (t // tm, 1), [x, g, w_in_t, *tabs],
        [_row_spec(tm, D_MODEL), _whole(g), _whole(w_in_t), tab, tab, tab],
        [_row_spec(tm, wd) for wd, _ in widths], [jax.ShapeDtypeStruct((t, wd), dt) for wd, dt in widths], VMEM_BIG, comm)


def _out_rms(merged, w_out, x, g):
    t = x.shape[0]
    tm = 512

    def body(m_ref, w_ref, x_ref, g_ref, x1_ref, h_ref):
        x1 = x_ref[...] + jnp.dot(m_ref[...].astype(MXU_DTYPE), w_ref[...], preferred_element_type=F32)
        x1_ref[...] = x1
        r = lax.rsqrt(jnp.mean(x1 * x1, axis=-1, keepdims=True) + RMS_EPS)
        h_ref[...] = ((x1 * r) * g_ref[...]).astype(BF16)

    return _grid_call(
        body, "out_rms", (t // tm, 1), [merged, w_out, x, g],
        [_row_spec(tm, D_MODEL), _whole(w_out), _row_spec(tm, D_MODEL), _whole(g)],
        [_row_spec(tm, D_MODEL)] * 2, [jax.ShapeDtypeStruct((t, D_MODEL), F32), jax.ShapeDtypeStruct((t, D_MODEL), BF16)],
        VMEM_BIG)


FFN_TN = D_FF // 2


def _ffn_in_swiglu(h2, w_gu, comm=None):
    t = h2.shape[0]
    tm = 512
    nj = D_FF // FFN_TN

    def body(h_ref, wg_ref, wu_ref, a_ref, b_ref, f_ref):
        h = h_ref[...].astype(MXU_DTYPE)
        a = lax.dot_general(h, wg_ref[...], _NT, preferred_element_type=F32)
        b = lax.dot_general(h, wu_ref[...], _NT, preferred_element_type=F32)
        a_ref[...] = a
        b_ref[...] = b
        f_ref[...] = (a * _sigmoid(a) * b).astype(BF16)

    tile = pl.BlockSpec((tm, FFN_TN), lambda i, j: (i, j))
    return _grid_call(
        body, "ffn_in_swiglu", (t // tm, nj), [h2, w_gu, w_gu],
        [_row_spec(tm, D_MODEL), pl.BlockSpec((FFN_TN, D_MODEL), lambda i, j: (j, 0)),
         pl.BlockSpec((FFN_TN, D_MODEL), lambda i, j: (j + nj, 0))],
        [tile] * 3, [jax.ShapeDtypeStruct((t, D_FF), F32)] * 2 + [jax.ShapeDtypeStruct((t, D_FF), BF16)], VMEM_BIG, comm)


def _ffn_down_final(f, w_down, x1, target, g):
    t = x1.shape[0]
    tm = 256

    def body(f_ref, w_ref, x1_ref, t_ref, g_ref, dx_ref, dxb_ref, loss_ref, gg_ref):
        @pl.when(pl.program_id(0) == 0)
        def _():
            loss_ref[...] = jnp.zeros_like(loss_ref)
            gg_ref[...] = jnp.zeros_like(gg_ref)

        xv = x1_ref[...] + jnp.dot(f_ref[...].astype(MXU_DTYPE), w_ref[...], preferred_element_type=F32)
        gv = g_ref[...]
        r = lax.rsqrt(jnp.mean(xv * xv, axis=-1, keepdims=True) + RMS_EPS)
        n = xv * r
        diff = n * gv - t_ref[...]
        per_tok = jnp.mean(diff * diff, axis=-1, keepdims=True)
        loss_ref[...] += 0.5 * jnp.sum(per_tok, axis=0, keepdims=True)
        dy = diff / xv.shape[-1]
        gg_ref[...] += jnp.sum(dy * n, axis=0, keepdims=True)
        dn = dy * gv
        dx = r * (dn - n * jnp.mean(dn * n, axis=-1, keepdims=True))
        dx_ref[...] = dx
        dxb_ref[...] = dx.astype(BF16)

    acc = lambda shp: pl.BlockSpec(shp, lambda i, j: (0, 0))
    return _grid_call(
        body, "ffn_down_final", (t // tm, 1), [f, w_down, x1, target, g],
        [_row_spec(tm, D_FF), _whole(w_down), _row_spec(tm, D_MODEL), _row_spec(tm, D_MODEL), _whole(g)],
        [_row_spec(tm, D_MODEL)] * 2 + [acc((8, LANES)), acc((1, D_MODEL))],
        [jax.ShapeDtypeStruct((t, D_MODEL), F32), jax.ShapeDtypeStruct((t, D_MODEL), BF16),
         jax.ShapeDtypeStruct((8, LANES), F32), jax.ShapeDtypeStruct((1, D_MODEL), F32)], VMEM_BIG, sequential=True)


def _d_f_swiglu_bwd(dx2b, w_down, a, b):
    t = a.shape[0]
    tm = 512

    def body(dx_ref, w_ref, a_ref, b_ref, da_ref, db_ref):
        d = lax.dot_general(dx_ref[...], w_ref[...], _NT, preferred_element_type=F32)
        av, bv = a_ref[...], b_ref[...]
        sg = _sigmoid(av)
        da_ref[...] = (d * bv * sg * (1.0 + av * (1.0 - sg))).astype(BF16)
        db_ref[...] = (d * av * sg).astype(BF16)

    tile = pl.BlockSpec((tm, FFN_TN), lambda i, j: (i, j))
    return _grid_call(
        body, "d_f_swiglu_bwd", (t // tm, D_FF // FFN_TN), [dx2b, w_down, a, b],
        [_row_spec(tm, D_MODEL), pl.BlockSpec((FFN_TN, D_MODEL), lambda i, j: (j, 0)), tile, tile],
        [tile] * 2, [jax.ShapeDtypeStruct((t, D_FF), BF16)] * 2, VMEM_BIG)


def _mm_rms_bwd(operands, w, x, g, dres, name, comm=None):
    t = x.shape[0]
    tm = 256
    n_op = len(operands)
    offsets = [sum(a.shape[1] for a in operands[:k]) for k in range(n_op)]

    def body(*refs):
        a_refs, (w_ref, x_ref, g_ref, dres_ref, dx_ref, dxb_ref, gg_ref) = refs[:n_op], refs[n_op:]

        @pl.when(pl.program_id(0) == 0)
        def _():
            gg_ref[...] = jnp.zeros_like(gg_ref)

        dh = None
        for a_ref, off in zip(a_refs, offsets):
            part = jnp.dot(a_ref[...].astype(MXU_DTYPE), w_ref[off:off + a_ref.shape[1], :], preferred_element_type=F32)
            dh = part if dh is None else dh + part
        xv = x_ref[...]
        r = lax.rsqrt(jnp.mean(xv * xv, axis=-1, keepdims=True) + RMS_EPS)
        n = xv * r
        gg_ref[...] += jnp.sum(dh * n, axis=0, keepdims=True)
        dn = dh * g_ref[...]
        dx = dres_ref[...] + r * (dn - n * jnp.mean(dn * n, axis=-1, keepdims=True))
        dx_ref[...] = dx
        dxb_ref[...] = dx.astype(BF16)

    d = x.shape[1]
    return _grid_call(
        body, name, (t // tm, 1), [*operands, w, x, g, dres],
        [_row_spec(tm, a.shape[1]) for a in operands] + [_whole(w), _row_spec(tm, d), _whole(g), _row_spec(tm, d)],
        [_row_spec(tm, d)] * 2 + [pl.BlockSpec((1, d), lambda i, j: (0, 0))],
        [jax.ShapeDtypeStruct((t, d), F32), jax.ShapeDtypeStruct((t, d), BF16), jax.ShapeDtypeStruct((1, d), F32)],
        VMEM_BIG, comm, sequential=True)


def _flat_small(small):
    perm_b = lambda a: a.reshape(SSM_GROUPS, SSM_STATE, SSM_CH).transpose(2, 0, 1).reshape(SSM_CH, N_STATE)
    perm_c = lambda a: a.reshape(SSM_GROUPS, SSM_CH, SSM_STATE).transpose(1, 0, 2).reshape(SSM_CH, N_STATE)
    return dict(
        g_mix=small["norm_mix_g"].reshape(1, D_MODEL), g_ffn=small["norm_ffn_g"].reshape(1, D_MODEL),
        g_fin=small["norm_final_g"].reshape(1, D_MODEL),
        lr=small["ssm_a_re"].reshape(1, N_STATE), li=small["ssm_a_im"].reshape(1, N_STATE),
        ldt=jnp.repeat(small["ssm_log_dt"].reshape(SSM_GROUPS), SSM_STATE).reshape(1, N_STATE),
        br=perm_b(small["ssm_b_re"]), bi=perm_b(small["ssm_b_im"]),
        cr=perm_c(small["ssm_c_re"]), ci=perm_c(small["ssm_c_im"]), dskip=small["ssm_d"].reshape(1, SSM_W))


AG_HOSTS = {"proj_rope": ("w_glu", "w_attn_out", "w_out"), "attn_fwd_g0": ("w_gu",), "ffn_in_swiglu": ("w_ffn_down",)}
A2A_HOSTS = {"d_h2_rms": ("w_ffn_down",), "attn_bwd_g0": ("w_ffn_gate",), "attn_bwd_g1": ("w_ffn_up",),
             "mm_g_in": ("w_out", "w_attn_out", "w_glu"), "d_h0_rms": ("w_in",)}


def _local_step(x, target, w, small, shards=None):
    t = x.shape[0]
    n_samples = t // SEQ
    n_rows = n_samples * SCAN_SEG_PER_SAMPLE
    tabs = _rope_tables()
    w = dict(w)
    fs = _flat_small(small)
    g_mix, g_ffn, g_fin, dskip = fs["g_mix"], fs["g_ffn"], fs["g_fin"], fs["dskip"]
    a_cat, bbc, cc = _ssm_disc(fs["lr"], fs["li"], fs["ldt"], fs["br"], fs["bi"], fs["cr"], fs["ci"])
    big, recv = {}, {}

    def comm_of(name):
        if shards is None:
            return None
        if name in AG_HOSTS:
            items, bufs = [], []
            for n in AG_HOSTS[name]:
                parts = ("w_ffn_gate", "w_ffn_up") if n == "w_gu" else (n,)
                c, k = shards[parts[0]].shape
                for j, p in enumerate(parts):
                    items.append((shards[p], len(bufs), j * N_DEV))
                bufs.append((len(parts) * N_DEV, c, k))
            return _ag_comm(items, bufs)
        if name in A2A_HOSTS:
            items = []
            for n in A2A_HOSTS[name]:
                c, k = shards[n].shape
                items.append((big[n].reshape(N_DEV, c, k), 0))
            return _a2a_comm(items)
        return None

    def absorb(name, carried):
        for n, a3 in zip(AG_HOSTS.get(name, ()), carried):
            w[n] = a3.reshape(-1, a3.shape[2])
        for n, a3 in zip(A2A_HOSTS.get(name, ()), carried):
            recv[n] = a3

    def mm(a, b, mode, name, tm, tn, **kw):
        comm = comm_of(name)
        if comm is None:
            return _mm(a, b, mode, name, tm, tn, **kw)
        out, *carried = _mm(a, b, mode, name, tm, tn, comm=comm, **kw)
        absorb(name, carried)
        return out

    h0, q, k, v, u, gates, *carried = _proj_rope(x, g_mix, w["w_in"], tabs, comm_of("proj_rope"))
    absorb("proj_rope", carried)
    os_, lses = [], []
    for g in range(3):
        o_g, l_g, carried = _attn_fwd(q, k, v, g, n_samples, comm_of(f"attn_fwd_g{g}"))
        absorb(f"attn_fwd_g{g}", carried)
        os_.append(o_g)
        lses.append(l_g)
    attn, lse_tot = _attn_merge(os_, lses)
    attn_d = mm(attn, w["w_attn_out"], "nt", "mm_attn_out", 512, D_MODEL)

    u_perm = _to_scan_rows(u, n_samples)
    ytot, yg_perm, ein = _ssm_fwd(u_perm, a_cat, bbc, cc, dskip, n_rows)
    yg = _from_scan_rows(yg_perm, n_samples)
    z = mm(yg, w["w_glu"], "nt", "mm_glu", 512, 2 * D_MODEL)

    merged = _mix(attn_d, z, gates)
    x1, h2 = _out_rms(merged, w["w_out"], x, g_ffn)
    ffn_a, ffn_b, f, *carried = _ffn_in_swiglu(h2, w["w_gu"], comm_of("ffn_in_swiglu"))
    absorb("ffn_in_swiglu", carried)
    dx2, dx2b, loss_blk, g_gfin = _ffn_down_final(f, w["w_ffn_down"], x1, target, g_fin)

    da, db = _d_f_swiglu_bwd(dx2b, w["w_ffn_down"], ffn_a, ffn_b)
    big["w_ffn_down"] = mm(f, dx2b, "tn", "mm_g_down", 256, 512, out_dtype=BF16)
    big["w_ffn_gate"] = mm(da, h2, "tn", "mm_g_gate", 256, 512, out_dtype=BF16)
    big["w_ffn_up"] = mm(db, h2, "tn", "mm_g_up", 256, 512, out_dtype=BF16)
    dx1, dx1b, g_gffn, *carried = _mm_rms_bwd([da, db], w["w_gu"], x1, g_ffn, dx2, "d_h2_rms", comm_of("d_h2_rms"))
    absorb("d_h2_rms", carried)

    dmerged = mm(dx1b, w["w_out"], "nt", "mm_d_merged", 512, D_MODEL)
    big["w_out"] = mm(merged, dx1b, "tn", "mm_g_out", 512, 512, out_dtype=BF16)
    dattn_d, dz, dgpre = _mix_bwd(dmerged, gates, attn_d, z)

    dattn = mm(dattn_d, w["w_attn_out"], "nn", "mm_d_attn", 512, GROUP_W)
    big["w_attn_out"] = mm(dattn_d, attn, "tn", "mm_g_attn_out", 512, GROUP_W, out_dtype=BF16)
    rowdot = _attn_rowdot(dattn, attn)
    dqs, dks, dvs = [], [], []
    for g in range(3):
        dq_g, dk_g, dv_g, carried = _attn_bwd(q, k, v, dattn, lse_tot, rowdot, g, n_samples, comm_of(f"attn_bwd_g{g}"))
        absorb(f"attn_bwd_g{g}", carried)
        dqs.append(dq_g)
        dks.append(dk_g)
        dvs.append(dv_g)

    dyg = mm(dz, w["w_glu"], "nn", "mm_d_yg", 512, SSM_W)
    big["w_glu"] = mm(dz, yg, "tn", "mm_g_glu", 512, 512, out_dtype=BF16)
    dyg_perm = _to_scan_rows(dyg, n_samples)
    dypre, du_skip, g_dskip = _ssm_act_bwd(dyg_perm, ytot, u_perm, dskip)
    du_perm, da_cat, dbb_full, dc_full = _ssm_bwd(u_perm, dypre, du_skip, a_cat, bbc, cc, ein, n_rows)
    du = _from_scan_rows(du_perm, n_samples)
    g_lr, g_li, g_ldt, g_br, g_bi, g_cr, g_ci = _ssm_param_bwd(
        fs["lr"], fs["li"], fs["ldt"], fs["br"], fs["bi"], da_cat, dbb_full, dc_full)

    dproj = _pack_dproj(dqs, dks, dvs, du, dgpre, tabs)
    big["w_in"] = mm(dproj, h0, "tn", "mm_g_in", 256, 512, out_dtype=BF16)
    grad_x, _, g_gmix, *carried = _mm_rms_bwd([dproj], w["w_in"], x, g_mix, dx1, "d_h0_rms", comm_of("d_h0_rms"))
    absorb("d_h0_rms", carried)

    small_g = dict(lr=g_lr, li=g_li, ldt=g_ldt, br=g_br, bi=g_bi, cr=g_cr, ci=g_ci, dskip=g_dskip,
                   g_mix=g_gmix, g_ffn=g_gffn, g_fin=g_gfin, loss=loss_blk)
    return loss_blk, grad_x, (big if shards is None else recv), small_g


_MESH = pl.DeviceIdType.MESH


def _all_gather(block, name):
    rows, lanes = block.shape

    def body(x_ref, out_ref, send_sems, recv_sems, local_sem):
        x, y, c = lax.axis_index("x"), lax.axis_index("y"), lax.axis_index("c")
        me, sibling = (x, y, c), (x, y, 1 - c)
        chips = [(1 - x, y), (x, 1 - y), (1 - x, 1 - y)]

        def slot(px, py, pc):
            return out_ref.at[4 * px + 2 * py + pc]

        def copy(k, blk, to, src=None):
            return pltpu.make_async_remote_copy(
                src_ref=slot(*blk) if src is None else src, dst_ref=slot(*blk), send_sem=send_sems.at[k],
                recv_sem=recv_sems.at[k], device_id=to, device_id_type=_MESH)

        mine = pltpu.make_async_copy(x_ref, slot(*me), local_sem)
        mine.start()
        first = [copy(0, me, sibling, src=x_ref)]
        first += [copy(1 + j, me, (*chip, c), src=x_ref) for j, chip in enumerate(chips)]
        for cp in first:
            cp.start()
        passed = [copy(4 + j, (*chip, c), sibling) for j, chip in enumerate(chips)]
        for j, chip in enumerate(chips):
            copy(1 + j, (*chip, c), me).wait_recv()
            passed[j].start()
        copy(0, sibling, me).wait_recv()
        for j, chip in enumerate(chips):
            copy(4 + j, (*chip, 1 - c), me).wait_recv()
        for cp in first + passed:
            cp.wait_send()
        mine.wait()

    return _pallas_call(
        body, name=name, out_shape=jax.ShapeDtypeStruct((N_DEV, rows, lanes), block.dtype),
        in_specs=[pl.BlockSpec(memory_space=pl.ANY)], out_specs=pl.BlockSpec(memory_space=pl.ANY),
        scratch_shapes=[pltpu.SemaphoreType.DMA((7,)), pltpu.SemaphoreType.DMA((7,)), pltpu.SemaphoreType.DMA],
    )(block)


def _ag_comm(items, bufs):
    def plan(in_refs, out_refs, send_sems, recv_sems, local_sems):
        x, y, c = lax.axis_index("x"), lax.axis_index("y"), lax.axis_index("c")
        me, sibling = (x, y, c), (x, y, 1 - c)
        chips = [(1 - x, y), (x, 1 - y), (1 - x, 1 - y)]
        plans = []
        for t, (_, buf, slot0) in enumerate(items):
            x_ref, out_ref = in_refs[t], out_refs[buf]

            def slot(px, py, pc, out_ref=out_ref, slot0=slot0):
                return out_ref.at[slot0 + 4 * px + 2 * py + pc]

            def copy(k, blk, to, src=None, t=t, slot=slot):
                return pltpu.make_async_remote_copy(
                    src_ref=slot(*blk) if src is None else src, dst_ref=slot(*blk), send_sem=send_sems.at[7 * t + k],
                    recv_sem=recv_sems.at[7 * t + k], device_id=to, device_id_type=_MESH)

            plans.append(dict(
                mine=pltpu.make_async_copy(x_ref, slot(*me), local_sems.at[t]),
                first=[copy(0, me, sibling, src=x_ref)] + [copy(1 + j, me, (*chip, c), src=x_ref)
                                                           for j, chip in enumerate(chips)],
                passed=[copy(4 + j, (*chip, c), sibling) for j, chip in enumerate(chips)],
                from_ici=[copy(1 + j, (*chip, c), me) for j, chip in enumerate(chips)],
                from_sibling=[copy(0, sibling, me)] + [copy(4 + j, (*chip, 1 - c), me) for j, chip in enumerate(chips)]))
        return plans

    def start(*refs):
        for p in plan(*refs):
            p["mine"].start()
            for cp in p["first"]:
                cp.start()

    def finish(*refs):
        plans = plan(*refs)
        for p in plans:
            for arrived, onward in zip(p["from_ici"], p["passed"]):
                arrived.wait_recv()
                onward.start()
        for p in plans:
            for arrived in p["from_sibling"]:
                arrived.wait_recv()
            for cp in p["first"] + p["passed"]:
                cp.wait_send()
            p["mine"].wait()

    out_shapes = [jax.ShapeDtypeStruct(b, items[0][0].dtype) for b in bufs]
    return _Comm([it[0] for it in items], out_shapes, 7 * len(items), len(items), start, finish)


def _a2a_comm(items):
    def plan(in_refs, out_refs, send_sems, recv_sems, local_sems):
        x, y, c = lax.axis_index("x"), lax.axis_index("y"), lax.axis_index("c")
        my = 4 * x + 2 * y + c
        copies, locals_ = [], []
        for t, (_, slot0) in enumerate(items):
            s_ref, r_ref = in_refs[t], out_refs[t]
            locals_.append(pltpu.make_async_copy(s_ref.at[slot0 + my], r_ref.at[my], local_sems.at[t]))
            for kk in range(1, N_DEV):
                px = 1 - x if kk & 4 else x
                py = 1 - y if kk & 2 else y
                pc = 1 - c if kk & 1 else c
                copies.append(pltpu.make_async_remote_copy(
                    src_ref=s_ref.at[slot0 + 4 * px + 2 * py + pc], dst_ref=r_ref.at[my],
                    send_sem=send_sems.at[7 * t + kk - 1], recv_sem=recv_sems.at[7 * t + kk - 1],
                    device_id=(px, py, pc), device_id_type=_MESH))
        return copies, locals_

    def start(*refs):
        copies, locals_ = plan(*refs)
        for cp in locals_ + copies:
            cp.start()

    def finish(*refs):
        copies, locals_ = plan(*refs)
        for cp in copies + locals_:
            cp.wait()

    out_shapes = [jax.ShapeDtypeStruct((N_DEV,) + it[0].shape[1:], it[0].dtype) for it in items]
    return _Comm([it[0] for it in items], out_shapes, 7 * len(items), len(items), start, finish)


def _adam_math(g, w, m, v):
    m_new = ADAM_B1 * m + (1.0 - ADAM_B1) * g
    v_new = ADAM_B2 * v + (1.0 - ADAM_B2) * jnp.square(g)
    m_hat = m_new / (1.0 - ADAM_B1 ** ADAM_STEP)
    v_hat = v_new / (1.0 - ADAM_B2 ** ADAM_STEP)
    return -ADAM_LR * (m_hat / (jnp.sqrt(v_hat) + ADAM_EPS) + ADAM_WD * w), m_new, v_new


def _sum_partials(partials, name, tm):
    n, rows, cols = partials.shape

    def body(p_ref, g_ref):
        g = p_ref[0].astype(F32)
        for s in range(1, n):
            g = g + p_ref[s].astype(F32)
        g_ref[...] = g

    return _pallas_call(
        body, name=name, grid=(rows // tm,), in_specs=[pl.BlockSpec((n, tm, cols), lambda i: (0, i, 0))],
        out_specs=pl.BlockSpec((tm, cols), lambda i: (i, 0)), out_shape=jax.ShapeDtypeStruct((rows, cols), F32),
        compiler_params=pltpu.CompilerParams(dimension_semantics=("parallel",), vmem_limit_bytes=VMEM_MID),
    )(partials)


def _adam(partials, w, m, v, name, tm):
    n, rows, cols = partials.shape

    def body(p_ref, w_ref, m_ref, v_ref, g_ref, d_ref, nm_ref, nv_ref):
        g = p_ref[0].astype(F32)
        for s in range(1, n):
            g = g + p_ref[s].astype(F32)
        g_ref[...] = g
        d_ref[...], nm_ref[...], nv_ref[...] = _adam_math(g, w_ref[...], m_ref[...], v_ref[...])

    assert rows % tm == 0
    row = pl.BlockSpec((tm, cols), lambda i: (i, 0))
    shp = jax.ShapeDtypeStruct((rows, cols), F32)
    return _pallas_call(
        body, name=name, grid=(rows // tm,),
        in_specs=[pl.BlockSpec((n, tm, cols), lambda i: (0, i, 0)), row, row, row],
        out_specs=[row] * 4, out_shape=[shp] * 4,
        compiler_params=pltpu.CompilerParams(dimension_semantics=("parallel",), vmem_limit_bytes=VMEM_MID),
    )(partials, w, m, v)


_PK_LR, _PK_LI, _PK_GAINS, _PK_MISC, _PK_BR, _PK_BI, _PK_CR, _PK_CI, _PK_ROWS = 0, 1, 2, 3, 8, 24, 40, 56, 72
_PK_LDT_LANE, _PK_LOSS_LANE = D_MODEL + SSM_W, D_MODEL + SSM_W + LANES


def _pack_small(sg):
    names = ("lr", "li", "g_mix", "g_ffn", "g_fin", "dskip", "ldt", "loss", "br", "bi", "cr", "ci")

    def body(lr, li, gmix, gffn, gfin, dskip, ldt, loss, br, bi, cr, ci, o_ref):
        o_ref[...] = jnp.zeros_like(o_ref)
        o_ref[_PK_LR:_PK_LR + 1, :] = lr[...]
        o_ref[_PK_LI:_PK_LI + 1, :] = li[...]
        o_ref[_PK_GAINS:_PK_GAINS + 1, :D_MODEL] = gmix[...]
        o_ref[_PK_GAINS:_PK_GAINS + 1, D_MODEL:] = gffn[...]
        o_ref[_PK_MISC:_PK_MISC + 1, :D_MODEL] = gfin[...]
        o_ref[_PK_MISC:_PK_MISC + 1, D_MODEL:D_MODEL + SSM_W] = dskip[...]
        o_ref[_PK_MISC:_PK_MISC + 1, _PK_LDT_LANE:_PK_LDT_LANE + LANES] = ldt[0:1, :]
        o_ref[_PK_MISC:_PK_MISC + 1, _PK_LOSS_LANE:_PK_LOSS_LANE + LANES] = loss[0:1, :]
        o_ref[_PK_BR:_PK_BR + SSM_CH, :] = br[...]
        o_ref[_PK_BI:_PK_BI + SSM_CH, :] = bi[...]
        o_ref[_PK_CR:_PK_CR + SSM_CH, :] = cr[...]
        o_ref[_PK_CI:_PK_CI + SSM_CH, :] = ci[...]

    return _pallas_call(body, name="pack_small", out_shape=jax.ShapeDtypeStruct((_PK_ROWS, N_STATE), F32))(
        *[sg[n] for n in names])


def _unpack_small(s):
    unflat_b = lambda a: a.reshape(SSM_CH, SSM_GROUPS, SSM_STATE).transpose(1, 2, 0)[None]
    unflat_c = lambda a: a.reshape(SSM_CH, SSM_GROUPS, SSM_STATE).transpose(1, 0, 2)[None]
    grads = {
        "norm_mix_g": s[_PK_GAINS, :D_MODEL].reshape(1, D_MODEL), "norm_ffn_g": s[_PK_GAINS, D_MODEL:].reshape(1, D_MODEL),
        "norm_final_g": s[_PK_MISC, :D_MODEL],
        "ssm_a_re": s[_PK_LR].reshape(1, SSM_GROUPS, SSM_STATE), "ssm_a_im": s[_PK_LI].reshape(1, SSM_GROUPS, SSM_STATE),
        "ssm_log_dt": s[_PK_MISC, _PK_LDT_LANE:_PK_LDT_LANE + SSM_GROUPS].reshape(1, SSM_GROUPS),
        "ssm_d": s[_PK_MISC, D_MODEL:D_MODEL + SSM_W].reshape(1, SSM_GROUPS, SSM_CH),
        "ssm_b_re": unflat_b(s[_PK_BR:_PK_BR + SSM_CH]), "ssm_b_im": unflat_b(s[_PK_BI:_PK_BI + SSM_CH]),
        "ssm_c_re": unflat_c(s[_PK_CR:_PK_CR + SSM_CH]), "ssm_c_im": unflat_c(s[_PK_CI:_PK_CI + SSM_CH]),
    }
    return s[_PK_MISC, _PK_LOSS_LANE], grads


def _adam_small(grads, wts, moms, vars_):
    n = len(SMALL_WEIGHTS)
    as2d = lambda a: a.reshape(1, -1) if a.ndim == 1 else a

    def body(*refs):
        ins, outs = refs[:4 * n], refs[4 * n:]
        for i in range(n):
            g, w, m, v = (ins[j * n + i][...] for j in range(4))
            outs[i][...], outs[n + i][...], outs[2 * n + i][...] = _adam_math(g, w, m, v)

    operands = [as2d(d[k]) for d in (grads, wts, moms, vars_) for k in SMALL_WEIGHTS]
    shapes = [jax.ShapeDtypeStruct(as2d(wts[k]).shape, F32) for k in SMALL_WEIGHTS] * 3
    res = _pallas_call(body, name="adam_small", out_shape=shapes,
                         compiler_params=pltpu.CompilerParams(vmem_limit_bytes=VMEM_BIG))(*operands)
    out = {}
    for j, kind in enumerate(("delta", "new_m", "new_v")):
        for i, k in enumerate(SMALL_WEIGHTS):
            out[kind, k] = res[j * n + i].reshape(wts[k].shape)
    return out


def kernel(x, norm_mix_g, w_in, ssm_a_re, ssm_a_im, ssm_log_dt, ssm_b_re, ssm_b_im, ssm_c_re, ssm_c_im, ssm_d, w_glu, w_attn_out, w_out, norm_ffn_g, w_ffn_gate, w_ffn_up, w_ffn_down, norm_final_g, loss_target, m_norm_mix_g, m_w_in, m_ssm_a_re, m_ssm_a_im, m_ssm_log_dt, m_ssm_b_re, m_ssm_b_im, m_ssm_c_re, m_ssm_c_im, m_ssm_d, m_w_glu, m_w_attn_out, m_w_out, m_norm_ffn_g, m_w_ffn_gate, m_w_ffn_up, m_w_ffn_down, m_norm_final_g, v_norm_mix_g, v_w_in, v_ssm_a_re, v_ssm_a_im, v_ssm_log_dt, v_ssm_b_re, v_ssm_b_im, v_ssm_c_re, v_ssm_c_im, v_ssm_d, v_w_glu, v_w_attn_out, v_w_out, v_norm_ffn_g, v_w_ffn_gate, v_w_ffn_up, v_w_ffn_down, v_norm_final_g):
    args = dict(locals())
    wts = {n: args[n] for n in ALL_WEIGHTS}
    moms = {n: args["m_" + n] for n in ALL_WEIGHTS}
    vars_ = {n: args["v_" + n] for n in ALL_WEIGHTS}
    n_samples = x.shape[0]
    t = n_samples * SEQ

    shards = {n: (wts[n][0] if n in ROW_SHARDED else wts[n][0].T).astype(BF16) for n in BIG_WEIGHTS}
    w_in_t = _all_gather(shards["w_in"], "allgather_w_in").reshape(IN_W, D_MODEL)

    small = {n: wts[n] for n in SMALL_WEIGHTS}
    _, grad_x, recv, small_g = _local_step(x.reshape(t, D_MODEL), loss_target.reshape(t, D_MODEL), {"w_in": w_in_t},
                                           small, shards)

    results = {}
    for n in BIG_WEIGHTS:
        c, k = shards[n].shape
        w2, m2, v2 = wts[n][0], moms[n][0], vars_[n][0]
        if n in ROW_SHARDED:
            res = _adam(recv[n], w2, m2, v2, "adam_" + n, c // 2)
        else:
            g_t = _sum_partials(recv[n], "sum_" + n, c // 2)
            res = _adam(g_t.T[None], w2, m2, v2, "adam_" + n, k // 2)
        for kind, a in zip(("grad", "delta", "new_m", "new_v"), res):
            results[kind, n] = a[None]

    sgath = _all_gather(_pack_small(small_g), "allgather_small_grads")
    loss, sgrads = _unpack_small(_sum_partials(sgath, "sum_small", _PK_ROWS))
    for n in SMALL_WEIGHTS:
        results["grad", n] = sgrads[n]
    results.update(_adam_small(sgrads, wts, moms, vars_))
    outs = [loss, grad_x.reshape(x.shape)]
    for kind in ("grad", "delta", "new_m", "new_v"):
        outs += [results[kind, n] for n in ALL_WEIGHTS]
    return tuple(outs)
```

```python
import functools
import math

import jax
import jax.numpy as jnp
from jax import lax
from jax.experimental import pallas as pl
from jax.experimental.pallas import tpu as pltpu

F32 = jnp.float32
BF16 = jnp.bfloat16
MXU_DTYPE = jnp.bfloat16

N_DEV = 8
D_MODEL = 1024
SEQ = 2048
HEAD_DIM = 64
HEADS_PER_GROUP = 4
GROUP_W = HEADS_PER_GROUP * HEAD_DIM
DILATIONS = (1, 4, 16)
QKV_W = 3 * len(DILATIONS) * GROUP_W
Q_W = len(DILATIONS) * GROUP_W
ATT_BLOCK = 128
ROPE_DIM = 16
ROPE_THETA = 500000.0
SSM_W = 512
SSM_GROUPS = 32
SSM_CH = 16
SSM_STATE = 64
N_STATE = SSM_GROUPS * SSM_STATE
D_FF = 2816
IN_W = QKV_W + SSM_W + 2 * D_MODEL
RMS_EPS = 1e-6
NEG_INF = -1e30
LANES = 128

SCAN_SEG_PER_SAMPLE = 8
SCAN_LEN = SEQ // SCAN_SEG_PER_SAMPLE
SCAN_WC = 512
SCAN_NBLK = N_STATE // SCAN_WC
SCAN_CH = SSM_W // SCAN_NBLK
SCAN_CHUNK = 32

ADAM_LR = 0.001
ADAM_B1 = 0.9
ADAM_B2 = 0.999
ADAM_EPS = 1e-08
ADAM_WD = 0.01
ADAM_STEP = 10

VMEM_BIG = 48 * 1024 * 1024
VMEM_MID = 32 * 1024 * 1024

BIG_WEIGHTS = ("w_in", "w_glu", "w_attn_out", "w_out", "w_ffn_gate", "w_ffn_up", "w_ffn_down")
ROW_SHARDED = ("w_out", "w_ffn_down")
SMALL_WEIGHTS = ("norm_mix_g", "ssm_a_re", "ssm_a_im", "ssm_log_dt", "ssm_b_re", "ssm_b_im", "ssm_c_re", "ssm_c_im",
                 "ssm_d", "norm_ffn_g", "norm_final_g")
ALL_WEIGHTS = ("norm_mix_g", "w_in", "ssm_a_re", "ssm_a_im", "ssm_log_dt", "ssm_b_re", "ssm_b_im", "ssm_c_re", "ssm_c_im",
               "ssm_d", "w_glu", "w_attn_out", "w_out", "norm_ffn_g", "w_ffn_gate", "w_ffn_up", "w_ffn_down", "norm_final_g")


def _sigmoid(x):
    return 1.0 / (1.0 + jnp.exp(-x))


def _pallas_call(body, *, out_shape, **kw):
    single = not isinstance(out_shape, (list, tuple))
    shapes = [pltpu.HBM(s.shape, s.dtype) for s in ([out_shape] if single else out_shape)]
    call = pl.pallas_call(body, out_shape=shapes[0] if single else shapes, **kw)
    return lambda *operands: call(*[pltpu.with_memory_space_constraint(o, pltpu.HBM) for o in operands])


class _Comm:
    def __init__(self, ins, out_shapes, n_sem, n_local, start, finish):
        self.ins, self.out_shapes, self.n_sem, self.n_local = ins, out_shapes, n_sem, n_local
        self.start, self.finish = start, finish


def _mm(a, b, mode, name, tm, tn, out_dtype=F32, add=None, vmem=VMEM_BIG, comm=None, cols=None):
    if mode == "nn":
        (m, k), (_, n) = a.shape, b.shape
        a_spec = pl.BlockSpec((tm, k), lambda i, j: (i, 0))
        b_spec = pl.BlockSpec((k, tn), lambda i, j: (0, j))
        dims = (((1,), (0,)), ((), ()))
    elif mode == "nt":
        (m, k), (n, _) = a.shape, b.shape
        a_spec = pl.BlockSpec((tm, k), lambda i, j: (i, 0))
        b_spec = pl.BlockSpec((tn, k), lambda i, j: (j, 0))
        dims = (((1,), (1,)), ((), ()))
    else:
        (k, m), (_, n) = a.shape, b.shape
        first, n = cols if cols else (0, n)
        a_spec = pl.BlockSpec((k, tm), lambda i, j: (0, i))
        b_spec = pl.BlockSpec((k, tn), lambda i, j: (0, j + first // tn))
        dims = (((0,), (0,)), ((), ()))
    assert m % tm == 0 and n % tn == 0, (name, m, n, tm, tn)
    o_spec = pl.BlockSpec((tm, tn), lambda i, j: (i, j))
    has_add = add is not None

    def body(*refs):
        a_ref, b_ref, o_ref = refs[0], refs[1], refs[-1]
        acc = lax.dot_general(a_ref[...].astype(MXU_DTYPE), b_ref[...].astype(MXU_DTYPE), dims,
                              preferred_element_type=F32)
        if has_add:
            acc = acc + refs[2][...]
        o_ref[...] = acc.astype(out_dtype)

    ins = [a, b] + ([add] if has_add else [])
    in_specs = [a_spec, b_spec] + ([o_spec] if has_add else [])
    return _grid_call(body, name, (m // tm, n // tn), ins, in_specs, [o_spec],
                      [jax.ShapeDtypeStruct((m, n), out_dtype)], vmem, comm)


def _grid_call(body, name, grid, ins, in_specs, out_specs, out_shapes, vmem, comm=None, sequential=False, scratch=()):
    if comm is None:
        single = len(out_shapes) == 1
        semantics = ("arbitrary", "arbitrary") if sequential else ("parallel", "parallel")
        return _pallas_call(
            body, name=name, grid=grid, in_specs=in_specs, out_specs=out_specs[0] if single else out_specs,
            out_shape=out_shapes[0] if single else out_shapes, scratch_shapes=list(scratch),
            compiler_params=pltpu.CompilerParams(dimension_semantics=semantics, vmem_limit_bytes=vmem),
        )(*ins)
    n_in, n_out, n_cin, n_cout = len(ins), len(out_shapes), len(comm.ins), len(comm.out_shapes)
    n_io = n_in + n_cin + n_out + n_cout

    def carrying(*refs):
        own = refs[:n_in] + refs[n_in + n_cin:n_in + n_cin + n_out] + refs[n_io:len(refs) - 3]
        c_args = (refs[n_in:n_in + n_cin], refs[n_in + n_cin + n_out:n_io], *refs[-3:])

        @pl.when((pl.program_id(0) == 0) & (pl.program_id(1) == 0))
        def _():
            comm.start(*c_args)

        body(*own)

        @pl.when((pl.program_id(0) == grid[0] - 1) & (pl.program_id(1) == grid[1] - 1))
        def _():
            comm.finish(*c_args)

    hbm = pl.BlockSpec(memory_space=pl.ANY)
    return _pallas_call(
        carrying, name=name, grid=grid, in_specs=list(in_specs) + [hbm] * n_cin,
        out_specs=list(out_specs) + [hbm] * n_cout, out_shape=list(out_shapes) + list(comm.out_shapes),
        scratch_shapes=list(scratch) + [pltpu.SemaphoreType.DMA((comm.n_sem,)), pltpu.SemaphoreType.DMA((comm.n_sem,)),
                                        pltpu.SemaphoreType.DMA((comm.n_local,))],
        compiler_params=pltpu.CompilerParams(dimension_semantics=("arbitrary", "arbitrary"), vmem_limit_bytes=vmem),
    )(*ins, *comm.ins)


def _rows(body, name, n_rows, tm, ins, outs, vmem=VMEM_MID, scratch=()):
    assert n_rows % tm == 0
    arrays, in_specs = [], []
    for kind, arr in ins:
        arrays.append(arr)
        if kind == "row":
            assert n_rows % arr.shape[0] == 0, (name, arr.shape)
            in_specs.append(pl.BlockSpec((tm * arr.shape[0] // n_rows, arr.shape[1]), lambda i: (i, 0)))
        elif kind == "tab":
            nblk = arr.shape[0] // tm
            in_specs.append(pl.BlockSpec((tm, arr.shape[1]), lambda i, nblk=nblk: (i % nblk, 0)))
        else:
            in_specs.append(pl.BlockSpec(arr.shape, lambda i, nd=arr.ndim: (0,) * nd))
    out_specs, out_shape = [], []
    for kind, shp, dt in outs:
        if kind == "row":
            out_specs.append(pl.BlockSpec((tm, shp), lambda i: (i, 0)))
            out_shape.append(jax.ShapeDtypeStruct((n_rows, shp), dt))
        elif kind == "dil":
            d, wd = shp
            out_specs.append(pl.BlockSpec((tm // d, d * wd), lambda i: (i, 0)))
            out_shape.append(jax.ShapeDtypeStruct((n_rows // d, d * wd), dt))
        else:
            out_specs.append(pl.BlockSpec(shp, lambda i, nd=len(shp): (0,) * nd))
            out_shape.append(jax.ShapeDtypeStruct(shp, dt))
    res = _pallas_call(
        body, name=name, grid=(n_rows // tm,), in_specs=in_specs, out_specs=out_specs, out_shape=out_shape,
        scratch_shapes=list(scratch),
        compiler_params=pltpu.CompilerParams(dimension_semantics=("arbitrary",), vmem_limit_bytes=vmem),
    )(*arrays)
    return res


def _gather_residue(stage, ch, r, d, n):
    return stage[ch, pl.ds(r, n, stride=d), :] if d > 1 else stage[ch]


def _scatter_residue(stage, ch, r, d, n, val):
    if d > 1:
        stage[ch, pl.ds(r, n, stride=d), :] = val
    else:
        stage[ch] = val


def _lane_chunk(ch):
    return slice(ch * LANES, (ch + 1) * LANES)


def _first_step():
    return pl.program_id(0) == 0


def _rope_tables():
    half = ROPE_DIM // 2
    inv = jnp.power(jnp.float32(ROPE_THETA), -jnp.arange(half, dtype=F32) * 2.0 / ROPE_DIM)
    ang = jnp.arange(SEQ, dtype=F32)[:, None] * inv[None, :]
    lane = jnp.arange(LANES) % HEAD_DIM
    cosl = jnp.cos(ang)[:, lane % half]
    sinl = jnp.sin(ang)[:, lane % half]
    tab_c = jnp.where(lane < ROPE_DIM, cosl, 1.0)
    tab_lo = jnp.where(lane < half, -sinl, 0.0)
    tab_hi = jnp.where((lane >= half) & (lane < ROPE_DIM), sinl, 0.0)
    return tab_c.astype(F32), tab_lo.astype(F32), tab_hi.astype(F32)


def _rope_apply(t, tc, tlo, thi):
    half = ROPE_DIM // 2
    return t * tc + pltpu.roll(t, LANES - half, 1) * tlo + pltpu.roll(t, half, 1) * thi


def _rope_transpose(dt, tc, tlo, thi):
    half = ROPE_DIM // 2
    return dt * tc + pltpu.roll(dt * tlo, half, 1) + pltpu.roll(dt * thi, LANES - half, 1)


def _pack_dproj(dqs, dks, dvs, du, dgpre, tabs):
    tm = 256

    def body(*refs):
        dq_refs, dk_refs, dv_refs = refs[0:3], refs[3:6], refs[6:9]
        du_ref, dg_ref, tc_ref, tlo_ref, thi_ref, o_ref, stage = refs[9:16]
        n_ch = QKV_W // LANES
        halves = GROUP_W // LANES
        for grp, d in enumerate(DILATIONS):
            for which, src in enumerate((dq_refs[grp], dk_refs[grp], dv_refs[grp])):
                for res in range(d):
                    for half in range(halves):
                        _scatter_residue(stage, which * (n_ch // 3) + grp * halves + half, res, d, tm // d,
                                         src[:, _lane_chunk(res * halves + half)])
        tc, tlo, thi = tc_ref[...], tlo_ref[...], thi_ref[...]
        for ch in range(n_ch):
            piece = stage[ch]
            o_ref[:, _lane_chunk(ch)] = (_rope_transpose(piece, tc, tlo, thi) if ch < 2 * n_ch // 3 else piece).astype(BF16)
        o_ref[:, QKV_W:QKV_W + SSM_W] = du_ref[...].astype(BF16)
        o_ref[:, QKV_W + SSM_W:] = dg_ref[...].astype(BF16)

    t = du.shape[0]
    ins = [("row", a) for a in (*dqs, *dks, *dvs, du, dgpre)] + [("tab", tb) for tb in tabs]
    return _rows(body, "pack_dproj", t, tm, ins, [("row", IN_W, BF16)],
                 scratch=[pltpu.VMEM((QKV_W // LANES, tm, LANES), F32)])[0]


def _attn_merge(os_, lses):
    tm = 256

    halves = GROUP_W // LANES

    def body(o0, o1, o2, l0, l1, l2, a_ref, lt_ref, nat):
        for grp, d in enumerate(DILATIONS[1:], start=1):
            for j, src in enumerate(((o0, o1, o2)[grp], (l0, l1, l2)[grp])):
                for res in range(d):
                    for half in range(halves):
                        _scatter_residue(nat, (grp - 1) * 4 + j * 2 + half, res, d, tm // d,
                                         src[:, _lane_chunk(res * halves + half)])
        for half in range(halves):
            sl = _lane_chunk(half)
            la, lb, lc = l0[:, sl], nat[2 + half], nat[6 + half]
            m = jnp.maximum(jnp.maximum(la, lb), lc)
            ea, eb, ec = jnp.exp(la - m), jnp.exp(lb - m), jnp.exp(lc - m)
            ssum = ea + eb + ec
            a_ref[:, sl] = (ea / ssum) * o0[:, sl] + (eb / ssum) * nat[half] + (ec / ssum) * nat[4 + half]
            lt_ref[:, sl] = m + jnp.log(ssum)

    t = os_[0].shape[0]
    return _rows(body, "attn_merge", t, tm, [("row", a) for a in (*os_, *lses)],
                 [("row", GROUP_W, F32), ("row", GROUP_W, F32)], scratch=[pltpu.VMEM((8, tm, LANES), F32)])


def _head_sum_matrix():
    r = jnp.arange(GROUP_W) // HEAD_DIM
    return (r[:, None] == r[None, :]).astype(F32)


def _attn_rowdot(dattn, attn, lse_tot):
    tm = 256

    halves = GROUP_W // LANES

    def body(da_ref, a_ref, lt_ref, ones_ref, rd_ref, *rest):
        dil, stage = rest[:6], rest[6]
        rd = jnp.dot(da_ref[...] * a_ref[...], ones_ref[...], preferred_element_type=F32, precision=lax.Precision.HIGHEST)
        rd_ref[...] = rd
        for half in range(halves):
            stage[half] = da_ref[:, _lane_chunk(half)]
            stage[2 + half] = lt_ref[:, _lane_chunk(half)]
            stage[4 + half] = rd[:, _lane_chunk(half)]
        for grp, d in enumerate(DILATIONS[1:], start=1):
            for j in range(3):
                for res in range(d):
                    for half in range(halves):
                        dil[3 * (grp - 1) + j][:, _lane_chunk(res * halves + half)] = _gather_residue(
                            stage, 2 * j + half, res, d, tm // d)

    t = attn.shape[0]
    outs = [("row", GROUP_W, F32)] + [("dil", (d, GROUP_W), F32) for d in DILATIONS[1:] for _ in range(3)]
    rd, *dil = _rows(body, "attn_rowdot", t, tm,
                     [("row", dattn), ("row", attn), ("row", lse_tot), ("const", _head_sum_matrix())], outs,
                     scratch=[pltpu.VMEM((6, tm, LANES), F32)])
    return [(dattn, lse_tot, rd), tuple(dil[:3]), tuple(dil[3:])]


def _mix(attn_d, z, gates):
    def body(ad_ref, z_ref, g_ref, m_ref):
        za, zb = z_ref[:, :D_MODEL], z_ref[:, D_MODEL:]
        s_out = za * _sigmoid(zb)
        m_ref[...] = (g_ref[:, :D_MODEL] * ad_ref[...] + g_ref[:, D_MODEL:] * s_out).astype(BF16)

    t = attn_d.shape[0]
    return _rows(body, "mix", t, 256, [("row", attn_d), ("row", z), ("row", gates)], [("row", D_MODEL, BF16)])[0]


def _mix_bwd(dmerged, gates, attn_d, z):
    def body(dm_ref, g_ref, ad_ref, z_ref, dad_ref, dz_ref, dg_ref):
        dm = dm_ref[...]
        g0, g1 = g_ref[:, :D_MODEL], g_ref[:, D_MODEL:]
        za, zb = z_ref[:, :D_MODEL], z_ref[:, D_MODEL:]
        sb = _sigmoid(zb)
        s_out = za * sb
        dad_ref[...] = (dm * g0).astype(BF16)
        ds = dm * g1
        dz_ref[:, :D_MODEL] = (ds * sb).astype(BF16)
        dz_ref[:, D_MODEL:] = (ds * za * sb * (1.0 - sb)).astype(BF16)
        dg_ref[:, :D_MODEL] = dm * ad_ref[...] * g0 * (1.0 - g0)
        dg_ref[:, D_MODEL:] = dm * s_out * g1 * (1.0 - g1)

    t = dmerged.shape[0]
    return _rows(body, "mix_bwd", t, 256, [("row", dmerged), ("row", gates), ("row", attn_d), ("row", z)],
                 [("row", D_MODEL, BF16), ("row", 2 * D_MODEL, BF16), ("row", 2 * D_MODEL, F32)])


_GELU_C = math.sqrt(2.0 / math.pi)


def _ssm_act_bwd(dyg, ytot, u_perm, dskip):
    def body(dyg_ref, yt_ref, u_ref, d_ref, dy_ref, dus_ref, dd_ref):
        @pl.when(_first_step())
        def _():
            dd_ref[...] = jnp.zeros_like(dd_ref)

        yt = yt_ref[...]
        th = jnp.tanh(_GELU_C * (yt + 0.044715 * (yt * yt * yt)))
        dgelu = 0.5 * (1.0 + th) + 0.5 * yt * (1.0 - th * th) * _GELU_C * (1.0 + 3.0 * 0.044715 * yt * yt)
        dy = dyg_ref[...] * dgelu
        dy_ref[...] = dy.astype(BF16)
        dus_ref[...] = dy * d_ref[...]
        dd_ref[...] += jnp.sum(dy * u_ref[...], axis=0, keepdims=True)

    t = dyg.shape[0]
    return _rows(body, "ssm_act_bwd", t, 512, [("row", dyg), ("row", ytot), ("row", u_perm), ("const", dskip)],
                 [("row", SSM_W, BF16), ("row", SSM_W, F32), ("acc", (1, SSM_W), F32)])


def _head_masks():
    lane = lax.broadcasted_iota(jnp.int32, (1, GROUP_W), 1)
    return [(lane // HEAD_DIM) == h for h in range(HEADS_PER_GROUP)]


def _band_mask(first):
    nk = ATT_BLOCK if first else 2 * ATT_BLOCK
    qi = lax.broadcasted_iota(jnp.int32, (ATT_BLOCK, nk), 0)
    ki = lax.broadcasted_iota(jnp.int32, (ATT_BLOCK, nk), 1)
    dist = qi - ki + (0 if first else ATT_BLOCK)
    return (dist >= 0) & (dist <= ATT_BLOCK)


_NT = (((1,), (1,)), ((), ()))
_TN = (((0,), (0,)), ((), ()))


def _attn_fwd(q, k, v, group, n_samples, comm=None):
    d = DILATIONS[group]
    length = SEQ // d
    nb = length // ATT_BLOCK

    def body(q_ref, k_ref, v_ref, o_ref, l_ref):
        masks = _head_masks()

        def block(qs, ks, first):
            nk = ATT_BLOCK if first else 2 * ATT_BLOCK
            qb = q_ref[0, pl.ds(qs, ATT_BLOCK), :]
            kc = k_ref[0, pl.ds(ks, nk), :]
            vc = v_ref[0, pl.ds(ks, nk), :]
            valid = _band_mask(first)
            o_acc = jnp.zeros((ATT_BLOCK, GROUP_W), F32)
            l_acc = jnp.zeros((ATT_BLOCK, GROUP_W), F32)
            for h in range(HEADS_PER_GROUP):
                qh = jnp.where(masks[h], qb, jnp.zeros_like(qb))
                s = lax.dot_general(qh, kc, _NT, preferred_element_type=F32) * (HEAD_DIM ** -0.5)
                s = jnp.where(valid, s, NEG_INF)
                m = jnp.max(s, axis=-1, keepdims=True)
                p = jnp.exp(s - m)
                l = jnp.sum(p, axis=-1, keepdims=True)
                pv = jnp.dot(p.astype(MXU_DTYPE), vc, preferred_element_type=F32)
                o_acc = jnp.where(masks[h], pv / l, o_acc)
                l_acc = jnp.where(masks[h], m + jnp.log(l), l_acc)
            o_ref[0, pl.ds(qs, ATT_BLOCK), :] = o_acc
            l_ref[0, pl.ds(qs, ATT_BLOCK), :] = l_acc

        block(0, 0, True)
        if nb > 1:
            def loop(n, carry):
                block(pl.multiple_of(n * ATT_BLOCK, ATT_BLOCK), pl.multiple_of((n - 1) * ATT_BLOCK, ATT_BLOCK), False)
                return carry

            lax.fori_loop(1, nb, loop, 0)

    per_sample = lambda a: a.reshape(n_samples, length, d * GROUP_W)
    spec = pl.BlockSpec((1, length, GROUP_W), lambda b, r: (b, 0, r))
    shp = jax.ShapeDtypeStruct((n_samples, length, d * GROUP_W), F32)
    o, lse, *carried = _grid_call(body, f"attn_fwd_g{group}", (n_samples, d), [per_sample(a) for a in (q, k, v)],
                                  [spec] * 3, [spec] * 2, [shp, shp], VMEM_MID, comm)
    flat = lambda a: a.reshape(n_samples * length, d * GROUP_W)
    return flat(o), flat(lse), carried


def _attn_bwd(q, k, v, dattn, lse_tot, rowdot, group, n_samples, comm=None):
    d = DILATIONS[group]
    length = SEQ // d
    nb = length // ATT_BLOCK

    def body(q_ref, k_ref, v_ref, da_ref, lt_ref, rd_ref, dq_ref, dk_ref, dv_ref):
        masks = _head_masks()
        dk_ref[...] = jnp.zeros_like(dk_ref)
        dv_ref[...] = jnp.zeros_like(dv_ref)

        def block(qs, ks, first):
            nk = ATT_BLOCK if first else 2 * ATT_BLOCK
            qb = q_ref[0, pl.ds(qs, ATT_BLOCK), :]
            kc = k_ref[0, pl.ds(ks, nk), :]
            vc = v_ref[0, pl.ds(ks, nk), :]
            da = da_ref[0, pl.ds(qs, ATT_BLOCK), :]
            lt = lt_ref[0, pl.ds(qs, ATT_BLOCK), :]
            rd = rd_ref[0, pl.ds(qs, ATT_BLOCK), :]
            valid = _band_mask(first)
            dq_acc = jnp.zeros((ATT_BLOCK, GROUP_W), F32)
            dk_acc = jnp.zeros((nk, GROUP_W), F32)
            dv_acc = jnp.zeros((nk, GROUP_W), F32)
            for h in range(HEADS_PER_GROUP):
                qh = jnp.where(masks[h], qb, jnp.zeros_like(qb))
                dah = jnp.where(masks[h], da, 0.0).astype(MXU_DTYPE)
                lt_h = jnp.max(jnp.where(masks[h], lt, -jnp.inf), axis=-1, keepdims=True)
                rd_h = jnp.max(jnp.where(masks[h], rd, -jnp.inf), axis=-1, keepdims=True)
                s = lax.dot_general(qh, kc, _NT, preferred_element_type=F32) * (HEAD_DIM ** -0.5)
                s = jnp.where(valid, s, NEG_INF)
                p = jnp.exp(s - lt_h)
                dp = lax.dot_general(dah, vc, _NT, preferred_element_type=F32)
                ds = (p * (dp - rd_h) * (HEAD_DIM ** -0.5)).astype(MXU_DTYPE)
                dq_h = jnp.dot(ds, kc, preferred_element_type=F32)
                dq_acc = jnp.where(masks[h], dq_h, dq_acc)
                dk_acc = dk_acc + lax.dot_general(ds, qh, _TN, preferred_element_type=F32)
                dv_acc = dv_acc + lax.dot_general(p.astype(MXU_DTYPE), dah, _TN, preferred_element_type=F32)
            dq_ref[0, pl.ds(qs, ATT_BLOCK), :] = dq_acc
            dk_ref[0, pl.ds(ks, nk), :] += dk_acc
            dv_ref[0, pl.ds(ks, nk), :] += dv_acc

        block(0, 0, True)
        if nb > 1:
            def loop(n, carry):
                block(pl.multiple_of(n * ATT_BLOCK, ATT_BLOCK), pl.multiple_of((n - 1) * ATT_BLOCK, ATT_BLOCK), False)
                return carry

            lax.fori_loop(1, nb, loop, 0)

    per_sample = lambda a: a.reshape(n_samples, length, d * GROUP_W)
    spec = pl.BlockSpec((1, length, GROUP_W), lambda b, r: (b, 0, r))
    shp = jax.ShapeDtypeStruct((n_samples, length, d * GROUP_W), F32)
    dq, dk, dv, *carried = _grid_call(
        body, f"attn_bwd_g{group}", (n_samples, d), [per_sample(a) for a in (q, k, v, dattn, lse_tot, rowdot)],
        [spec] * 6, [spec] * 3, [shp, shp, shp], VMEM_MID, comm)
    flat = lambda a: a.reshape(n_samples * length, d * GROUP_W)
    return flat(dq), flat(dk), flat(dv), carried


def _disc(lr, li, ldt, br, bi):
    dt = jnp.exp(ldt)
    mag = jnp.exp(lr * dt)
    ab_re, ab_im = mag * jnp.cos(li * dt), mag * jnp.sin(li * dt)
    den = lr * lr + li * li
    nr, ni = ab_re - 1.0, ab_im
    f_re = (nr * lr + ni * li) / den
    f_im = (ni * lr - nr * li) / den
    return ab_re, ab_im, f_re * br - f_im * bi, f_re * bi + f_im * br


def _state_mask():
    row_g = lax.broadcasted_iota(jnp.int32, (SCAN_CH, SCAN_WC), 0) // SSM_CH
    col_g = lax.broadcasted_iota(jnp.int32, (SCAN_CH, SCAN_WC), 1) // SSM_STATE
    return row_g == col_g


def _ssm_disc(lr, li, ldt, br, bi, cr, ci):
    w = SCAN_WC

    def body(lr_ref, li_ref, ldt_ref, br_ref, bi_ref, cr_ref, ci_ref, a_ref, bb_ref, c_ref):
        ar, ai, bbr, bbi = _disc(lr_ref[...], li_ref[...], ldt_ref[...], br_ref[...], bi_ref[...])
        crv, civ = cr_ref[...], ci_ref[...]
        mask = _state_mask()
        for cb in range(SCAN_NBLK):
            sl = slice(cb * w, (cb + 1) * w)
            rows = slice(cb * SCAN_CH, (cb + 1) * SCAN_CH)
            dense = lambda comp: jnp.where(mask, jnp.tile(comp[:, sl], (SCAN_CH // SSM_CH, 1)), 0.0)
            a_ref[:, 2 * cb * w:(2 * cb + 1) * w] = ar[:, sl]
            a_ref[:, (2 * cb + 1) * w:(2 * cb + 2) * w] = ai[:, sl]
            bb_ref[rows, :w] = dense(bbr).astype(MXU_DTYPE)
            bb_ref[rows, w:] = dense(bbi).astype(MXU_DTYPE)
            c_ref[rows, :w] = dense(crv).astype(MXU_DTYPE)
            c_ref[rows, w:] = (-dense(civ)).astype(MXU_DTYPE)

    return _pallas_call(
        body, name="ssm_disc",
        out_shape=[jax.ShapeDtypeStruct((1, 2 * N_STATE), F32), jax.ShapeDtypeStruct((SSM_W, 2 * w), MXU_DTYPE),
                   jax.ShapeDtypeStruct((SSM_W, 2 * w), MXU_DTYPE)],
        compiler_params=pltpu.CompilerParams(vmem_limit_bytes=VMEM_MID),
    )(lr, li, ldt, br, bi, cr, ci)


def _group_indicator():
    s = jnp.arange(N_STATE) // SSM_STATE
    return (s[:, None] == jnp.arange(LANES)[None, :]).astype(F32)


def _ssm_param_bwd(lr, li, ldt, br, bi, da_cat, dbb_full, dc_full):
    w = SCAN_WC

    def body(lr_ref, li_ref, ldt_ref, br_ref, bi_ref, da_ref, dbb_ref, dc_ref, ind_ref,
             glr_ref, gli_ref, gldt_ref, gbr_ref, gbi_ref, gcr_ref, gci_ref):
        mask = _state_mask()

        def diag_parts(ref):
            res = ([], [])
            for cb in range(SCAN_NBLK):
                for part in range(2):
                    blk = ref[cb * SCAN_CH:(cb + 1) * SCAN_CH, part * w:(part + 1) * w]
                    res[part].append(jnp.sum(jnp.where(mask, blk, 0.0).reshape(SCAN_CH // SSM_CH, SSM_CH, w), axis=0))
            return jnp.concatenate(res[0], axis=1), jnp.concatenate(res[1], axis=1)

        dar = jnp.concatenate([da_ref[:, 2 * cb * w:(2 * cb + 1) * w] for cb in range(SCAN_NBLK)], axis=1)
        dai = jnp.concatenate([da_ref[:, (2 * cb + 1) * w:(2 * cb + 2) * w] for cb in range(SCAN_NBLK)], axis=1)
        dbbr, dbbi = diag_parts(dbb_ref)
        dcr, dci_neg = diag_parts(dc_ref)
        gcr_ref[...] = dcr
        gci_ref[...] = -dci_neg
        _, vjp = jax.vjp(_disc, lr_ref[...], li_ref[...], ldt_ref[...], br_ref[...], bi_ref[...])
        glr, gli, gldt, gbr, gbi = vjp((dar, dai, dbbr, dbbi))
        glr_ref[...] = glr
        gli_ref[...] = gli
        gldt_ref[...] = jnp.dot(jnp.broadcast_to(gldt, (8, N_STATE)), ind_ref[...], preferred_element_type=F32,
                                precision=lax.Precision.HIGHEST)
        gbr_ref[...] = gbr
        gbi_ref[...] = gbi

    v1 = jax.ShapeDtypeStruct((1, N_STATE), F32)
    v16 = jax.ShapeDtypeStruct((SSM_CH, N_STATE), F32)
    vdt = jax.ShapeDtypeStruct((8, LANES), F32)
    return _pallas_call(
        body, name="ssm_param_bwd", out_shape=[v1, v1, vdt, v16, v16, v16, v16],
        compiler_params=pltpu.CompilerParams(vmem_limit_bytes=VMEM_BIG),
    )(lr, li, ldt, br, bi, da_cat, dbb_full, dc_full, _group_indicator())


def _cmul(ar, ai, br, bi):
    return ar * br - ai * bi, ar * bi + ai * br


def _gelu_tanh(y):
    return jnp.tanh(_GELU_C * (y + 0.044715 * (y * y * y)))


def _segment_carry(er, ei, ar, ai, n_rows, reverse):
    qr, qi = ar, ai
    for _ in range(int(math.log2(SCAN_LEN))):
        qr, qi = _cmul(qr, qi, qr, qi)
    seg = lax.broadcasted_iota(jnp.int32, er.shape, 0) % SCAN_SEG_PER_SAMPLE
    shift = 1
    while shift < SCAN_SEG_PER_SAMPLE:
        keep = (seg < SCAN_SEG_PER_SAMPLE - shift) if reverse else (seg >= shift)
        amount = n_rows - shift if reverse else shift
        sr = jnp.where(keep, pltpu.roll(er, amount, 0), 0.0)
        si = jnp.where(keep, pltpu.roll(ei, amount, 0), 0.0)
        if reverse:
            er, ei = er + qr * sr + qi * si, ei + qr * si - qi * sr
        else:
            er, ei = er + qr * sr - qi * si, ei + qr * si + qi * sr
        qr, qi = _cmul(qr, qi, qr, qi)
        shift *= 2
    keep = (seg < SCAN_SEG_PER_SAMPLE - 1) if reverse else (seg >= 1)
    amount = n_rows - 1 if reverse else 1
    return jnp.where(keep, pltpu.roll(er, amount, 0), 0.0), jnp.where(keep, pltpu.roll(ei, amount, 0), 0.0)


def _ssm_fwd(u_perm, a_cat, bbc, cc, dskip, n_rows):
    t = u_perm.shape[0]
    w = SCAN_WC
    rows_c = SCAN_CHUNK * n_rows
    n_chunks = t // rows_c

    def body(u_ref, a_ref, bb_ref, c_ref, d_ref, yt_ref, yg_ref, ein_ref, bu_s, xs_s):
        ar = jnp.broadcast_to(a_ref[:, :w], (n_rows, w))
        ai = jnp.broadcast_to(a_ref[:, w:], (n_rows, w))

        def sweep(carry, store):
            def chunk(ch, carry):
                r0 = pl.multiple_of(ch * rows_c, rows_c)
                u_c = u_ref[pl.ds(r0, rows_c), :]
                bu_s[...] = jnp.dot(u_c.astype(MXU_DTYPE), bb_ref[...], preferred_element_type=F32)

                def step(i, c):
                    o = pl.multiple_of(i * n_rows, n_rows)
                    blk = bu_s[pl.ds(o, n_rows), :]
                    nr = ar * c[0] - ai * c[1] + blk[:, :w]
                    ni = ar * c[1] + ai * c[0] + blk[:, w:]
                    if store:
                        xs_s[pl.ds(o, n_rows), :w] = nr
                        xs_s[pl.ds(o, n_rows), w:] = ni
                    return nr, ni

                carry = lax.fori_loop(0, SCAN_CHUNK, step, carry)
                if store:
                    y = lax.dot_general(xs_s[...].astype(MXU_DTYPE), c_ref[...], _NT, preferred_element_type=F32)
                    yt = y + d_ref[...] * u_c
                    yt_ref[pl.ds(r0, rows_c), :] = yt
                    yg_ref[pl.ds(r0, rows_c), :] = (0.5 * yt * (1.0 + _gelu_tanh(yt))).astype(BF16)
                return carry

            return lax.fori_loop(0, n_chunks, chunk, carry)

        zero = jnp.zeros((n_rows, w), F32)
        er, ei = sweep((zero, zero), False)
        cr, ci = _segment_carry(er, ei, ar, ai, n_rows, False)
        ein_ref[:, :w] = cr
        ein_ref[:, w:] = ci
        sweep((cr, ci), True)

    col = lambda width: pl.BlockSpec((t, width), lambda c: (0, c))
    wgt = pl.BlockSpec((SCAN_CH, 2 * w), lambda c: (c, 0))
    return _pallas_call(
        body, name="ssm_fwd", grid=(SCAN_NBLK,),
        in_specs=[col(SCAN_CH), pl.BlockSpec((1, 2 * w), lambda c: (0, c)), wgt, wgt,
                  pl.BlockSpec((1, SCAN_CH), lambda c: (0, c))],
        out_specs=[col(SCAN_CH), col(SCAN_CH), pl.BlockSpec((n_rows, 2 * w), lambda c: (0, c))],
        out_shape=[jax.ShapeDtypeStruct((t, SSM_W), F32), jax.ShapeDtypeStruct((t, SSM_W), BF16),
                   jax.ShapeDtypeStruct((n_rows, 2 * N_STATE), F32)],
        scratch_shapes=[pltpu.VMEM((rows_c, 2 * w), F32), pltpu.VMEM((rows_c, 2 * w), F32)],
        compiler_params=pltpu.CompilerParams(dimension_semantics=("parallel",), vmem_limit_bytes=VMEM_BIG),
    )(u_perm, a_cat, bbc, cc, dskip)


def _ssm_bwd(u_perm, dypre, du_skip, a_cat, bbc, cc, ein, n_rows, comm=None):
    t = u_perm.shape[0]
    w = SCAN_WC
    rows_c = SCAN_CHUNK * n_rows
    n_chunks = t // rows_c

    def body(u_ref, dy_ref, dus_ref, a_ref, bb_ref, c_ref, ein_ref, du_ref, da_ref, dbb_ref, dc_ref, xs_all, tmp_s, g_s):
        ar = jnp.broadcast_to(a_ref[:, :w], (n_rows, w))
        ai = jnp.broadcast_to(a_ref[:, w:], (n_rows, w))
        zero = jnp.zeros((n_rows, w), F32)

        xs_all[0:n_rows, :] = ein_ref[...]

        def fwd_chunk(ch, carry):
            r0 = pl.multiple_of(ch * rows_c, rows_c)
            tmp_s[...] = jnp.dot(u_ref[pl.ds(r0, rows_c), :].astype(MXU_DTYPE), bb_ref[...], preferred_element_type=F32)

            def step(i, c):
                o = pl.multiple_of(i * n_rows, n_rows)
                blk = tmp_s[pl.ds(o, n_rows), :]
                nr = ar * c[0] - ai * c[1] + blk[:, :w]
                ni = ar * c[1] + ai * c[0] + blk[:, w:]
                xs_all[pl.ds(n_rows + r0 + o, n_rows), :w] = nr
                xs_all[pl.ds(n_rows + r0 + o, n_rows), w:] = ni
                return nr, ni

            return lax.fori_loop(0, SCAN_CHUNK, step, carry)

        lax.fori_loop(0, n_chunks, fwd_chunk, (ein_ref[:, :w], ein_ref[:, w:]))

        def load_dx(ch):
            r0 = pl.multiple_of(ch * rows_c, rows_c)
            tmp_s[...] = jnp.dot(dy_ref[pl.ds(r0, rows_c), :], c_ref[...], preferred_element_type=F32)
            return r0

        def back_steps(carry, store):
            def step(ii, c):
                o = pl.multiple_of((SCAN_CHUNK - 1 - ii) * n_rows, n_rows)
                blk = tmp_s[pl.ds(o, n_rows), :]
                gr = blk[:, :w] + ar * c[0] + ai * c[1]
                gi = blk[:, w:] + ar * c[1] - ai * c[0]
                if store:
                    g_s[pl.ds(o, n_rows), :w] = gr
                    g_s[pl.ds(o, n_rows), w:] = gi
                return gr, gi

            return lax.fori_loop(0, SCAN_CHUNK, step, carry)

        def first_sweep(cc_, carry):
            load_dx(n_chunks - 1 - cc_)
            return back_steps(carry, False)

        sr, si = lax.fori_loop(0, n_chunks, first_sweep, (zero, zero))
        gr0, gi0 = _segment_carry(sr, si, ar, ai, n_rows, True)

        dbb_ref[...] = jnp.zeros_like(dbb_ref)
        dc_ref[...] = jnp.zeros_like(dc_ref)
        da_ref[...] = jnp.zeros_like(da_ref)

        def second_sweep(cc_, carry):
            r0 = load_dx(n_chunks - 1 - cc_)
            carry = back_steps(carry, True)
            g = g_s[...]
            xp = xs_all[pl.ds(r0, rows_c), :]
            xc = xs_all[pl.ds(r0 + n_rows, rows_c), :]
            da_ref[:, :w] += jnp.sum(g[:, :w] * xp[:, :w] + g[:, w:] * xp[:, w:], axis=0, keepdims=True)
            da_ref[:, w:] += jnp.sum(g[:, w:] * xp[:, :w] - g[:, :w] * xp[:, w:], axis=0, keepdims=True)
            gb = g.astype(MXU_DTYPE)
            du_ref[pl.ds(r0, rows_c), :] = (lax.dot_general(gb, bb_ref[...], _NT, preferred_element_type=F32)
                                            + dus_ref[pl.ds(r0, rows_c), :])
            dbb_ref[...] += lax.dot_general(u_ref[pl.ds(r0, rows_c), :].astype(MXU_DTYPE), gb, _TN,
                                            preferred_element_type=F32)
            dc_ref[...] += lax.dot_general(dy_ref[pl.ds(r0, rows_c), :], xc.astype(MXU_DTYPE), _TN,
                                           preferred_element_type=F32)
            return carry

        lax.fori_loop(0, n_chunks, second_sweep, (gr0, gi0))

    col = lambda width: pl.BlockSpec((t, width), lambda c, j: (0, c))
    wgt = pl.BlockSpec((SCAN_CH, 2 * w), lambda c, j: (c, 0))
    row = pl.BlockSpec((1, 2 * w), lambda c, j: (0, c))
    return _grid_call(
        body, "ssm_bwd", (SCAN_NBLK, 1), [u_perm, dypre, du_skip, a_cat, bbc, cc, ein],
        [col(SCAN_CH), col(SCAN_CH), col(SCAN_CH), row, wgt, wgt, pl.BlockSpec((n_rows, 2 * w), lambda c, j: (0, c))],
        [col(SCAN_CH), row, wgt, wgt],
        [jax.ShapeDtypeStruct((t, SSM_W), F32), jax.ShapeDtypeStruct((1, 2 * N_STATE), F32),
         jax.ShapeDtypeStruct((SSM_W, 2 * w), F32), jax.ShapeDtypeStruct((SSM_W, 2 * w), F32)],
        56 * 1024 * 1024, comm,
        scratch=[pltpu.VMEM((t + n_rows, 2 * w), F32), pltpu.VMEM((rows_c, 2 * w), F32), pltpu.VMEM((rows_c, 2 * w), F32)])


def _to_scan_rows(a, n_samples):
    c = a.shape[1]
    return a.reshape(n_samples, SCAN_SEG_PER_SAMPLE, SCAN_LEN, c).transpose(2, 0, 1, 3).reshape(-1, c)


def _from_scan_rows(a, n_samples):
    c = a.shape[1]
    return a.reshape(SCAN_LEN, n_samples, SCAN_SEG_PER_SAMPLE, c).transpose(1, 2, 0, 3).reshape(-1, c)


def _row_spec(tm, width):
    return pl.BlockSpec((tm, width), lambda i, j: (i, 0))


def _whole(arr):
    return pl.BlockSpec(arr.shape, lambda i, j: (0,) * arr.ndim)


def _proj_rope(x, g, w_in_t, tabs, comm=None):
    t = x.shape[0]
    tm = 256

    def body(x_ref, g_ref, w_ref, tc_ref, tlo_ref, thi_ref, h_ref, u_ref, gate_ref, *rest):
        qkv_refs, stage = rest[:9], rest[9]
        xv = x_ref[...]
        r = lax.rsqrt(jnp.mean(xv * xv, axis=-1, keepdims=True) + RMS_EPS)
        h = ((xv * r) * g_ref[...]).astype(BF16)
        h_ref[...] = h
        p = lax.dot_general(h.astype(MXU_DTYPE), w_ref[...], _NT, preferred_element_type=F32)
        u_ref[...] = p[:, QKV_W:QKV_W + SSM_W]
        gate_ref[...] = _sigmoid(p[:, QKV_W + SSM_W:])
        tc, tlo, thi = tc_ref[...], tlo_ref[...], thi_ref[...]
        n_ch = QKV_W // LANES
        for ch in range(n_ch):
            piece = p[:, _lane_chunk(ch)]
            stage[ch] = _rope_apply(piece, tc, tlo, thi) if ch < 2 * n_ch // 3 else piece
        halves = GROUP_W // LANES
        for grp, d in enumerate(DILATIONS):
            for which in range(3):
                out = qkv_refs[3 * grp + which]
                for res in range(d):
                    for half in range(halves):
                        ch = which * (n_ch // 3) + grp * halves + half
                        out[:, _lane_chunk(res * halves + half)] = _gather_residue(stage, ch, res, d, tm // d).astype(BF16)

    tab = pl.BlockSpec((tm, LANES), lambda i, j: (i % (SEQ // tm), 0))
    widths = [(D_MODEL, BF16), (SSM_W, F32), (2 * D_MODEL, F32)]
    out_specs = [_row_spec(tm, wd) for wd, _ in widths]
    out_shapes = [jax.ShapeDtypeStruct((t, wd), dt) for wd, dt in widths]
    for d in DILATIONS:
        out_specs += [_row_spec(tm // d, d * GROUP_W)] * 3
        out_shapes += [jax.ShapeDtypeStruct((t // d, d * GROUP_W), BF16)] * 3
    return _grid_call(
        body, "proj_rope", (t // tm, 1), [x, g, w_in_t, *tabs],
        [_row_spec(tm, D_MODEL), _whole(g), _whole(w_in_t), tab, tab, tab], out_specs, out_shapes, VMEM_BIG, comm,
        scratch=[pltpu.VMEM((QKV_W // LANES, tm, LANES), F32)])


def _out_rms(merged, w_out, x, g):
    t = x.shape[0]
    tm = 512

    def body(m_ref, w_ref, x_ref, g_ref, x1_ref, h_ref):
        x1 = x_ref[...] + jnp.dot(m_ref[...].astype(MXU_DTYPE), w_ref[...], preferred_element_type=F32)
        x1_ref[...] = x1
        r = lax.rsqrt(jnp.mean(x1 * x1, axis=-1, keepdims=True) + RMS_EPS)
        h_ref[...] = ((x1 * r) * g_ref[...]).astype(BF16)

    return _grid_call(
        body, "out_rms", (t // tm, 1), [merged, w_out, x, g],
        [_row_spec(tm, D_MODEL), _whole(w_out), _row_spec(tm, D_MODEL), _whole(g)],
        [_row_spec(tm, D_MODEL)] * 2, [jax.ShapeDtypeStruct((t, D_MODEL), F32), jax.ShapeDtypeStruct((t, D_MODEL), BF16)],
        VMEM_BIG)


FFN_TN = D_FF // 2


def _ffn_in_swiglu(h2, w_gate_t, w_up_t, comm=None):
    t = h2.shape[0]
    tm = 512

    def body(h_ref, wg_ref, wu_ref, a_ref, b_ref, f_ref):
        h = h_ref[...].astype(MXU_DTYPE)
        a = lax.dot_general(h, wg_ref[...], _NT, preferred_element_type=F32)
        b = lax.dot_general(h, wu_ref[...], _NT, preferred_element_type=F32)
        a_ref[...] = a
        b_ref[...] = b
        f_ref[...] = (a * _sigmoid(a) * b).astype(BF16)

    tile = pl.BlockSpec((tm, FFN_TN), lambda i, j: (i, j))
    wspec = pl.BlockSpec((FFN_TN, D_MODEL), lambda i, j: (j, 0))
    return _grid_call(
        body, "ffn_in_swiglu", (t // tm, D_FF // FFN_TN), [h2, w_gate_t, w_up_t], [_row_spec(tm, D_MODEL), wspec, wspec],
        [tile] * 3, [jax.ShapeDtypeStruct((t, D_FF), F32)] * 2 + [jax.ShapeDtypeStruct((t, D_FF), BF16)], VMEM_BIG, comm)


def _ffn_down_final(f, w_down, x1, target, g):
    t = x1.shape[0]
    tm = 256

    def body(f_ref, w_ref, x1_ref, t_ref, g_ref, dx_ref, dxb_ref, loss_ref, gg_ref):
        @pl.when(pl.program_id(0) == 0)
        def _():
            loss_ref[...] = jnp.zeros_like(loss_ref)
            gg_ref[...] = jnp.zeros_like(gg_ref)

        xv = x1_ref[...] + jnp.dot(f_ref[...].astype(MXU_DTYPE), w_ref[...], preferred_element_type=F32)
        gv = g_ref[...]
        r = lax.rsqrt(jnp.mean(xv * xv, axis=-1, keepdims=True) + RMS_EPS)
        n = xv * r
        diff = n * gv - t_ref[...]
        per_tok = jnp.mean(diff * diff, axis=-1, keepdims=True)
        loss_ref[...] += 0.5 * jnp.sum(per_tok, axis=0, keepdims=True)
        dy = diff / xv.shape[-1]
        gg_ref[...] += jnp.sum(dy * n, axis=0, keepdims=True)
        dn = dy * gv
        dx = r * (dn - n * jnp.mean(dn * n, axis=-1, keepdims=True))
        dx_ref[...] = dx
        dxb_ref[...] = dx.astype(BF16)

    acc = lambda shp: pl.BlockSpec(shp, lambda i, j: (0, 0))
    return _grid_call(
        body, "ffn_down_final", (t // tm, 1), [f, w_down, x1, target, g],
        [_row_spec(tm, D_FF), _whole(w_down), _row_spec(tm, D_MODEL), _row_spec(tm, D_MODEL), _whole(g)],
        [_row_spec(tm, D_MODEL)] * 2 + [acc((8, LANES)), acc((1, D_MODEL))],
        [jax.ShapeDtypeStruct((t, D_MODEL), F32), jax.ShapeDtypeStruct((t, D_MODEL), BF16),
         jax.ShapeDtypeStruct((8, LANES), F32), jax.ShapeDtypeStruct((1, D_MODEL), F32)], VMEM_BIG, sequential=True)


def _d_f_swiglu_bwd(dx2b, w_down, a, b):
    t = a.shape[0]
    tm = 512

    def body(dx_ref, w_ref, a_ref, b_ref, da_ref, db_ref):
        d = lax.dot_general(dx_ref[...], w_ref[...], _NT, preferred_element_type=F32)
        av, bv = a_ref[...], b_ref[...]
        sg = _sigmoid(av)
        da_ref[...] = (d * bv * sg * (1.0 + av * (1.0 - sg))).astype(BF16)
        db_ref[...] = (d * av * sg).astype(BF16)

    tile = pl.BlockSpec((tm, FFN_TN), lambda i, j: (i, j))
    return _grid_call(
        body, "d_f_swiglu_bwd", (t // tm, D_FF // FFN_TN), [dx2b, w_down, a, b],
        [_row_spec(tm, D_MODEL), pl.BlockSpec((FFN_TN, D_MODEL), lambda i, j: (j, 0)), tile, tile],
        [tile] * 2, [jax.ShapeDtypeStruct((t, D_FF), BF16)] * 2, VMEM_BIG)


def _mm_rms_bwd(operands, weights, x, g, dres, name, comm=None):
    t = x.shape[0]
    tm = 256
    n_op = len(operands)

    def body(*refs):
        a_refs, w_refs = refs[:n_op], refs[n_op:2 * n_op]
        x_ref, g_ref, dres_ref, dx_ref, dxb_ref, gg_ref = refs[2 * n_op:]

        @pl.when(pl.program_id(0) == 0)
        def _():
            gg_ref[...] = jnp.zeros_like(gg_ref)

        dh = None
        for a_ref, w_ref in zip(a_refs, w_refs):
            part = jnp.dot(a_ref[...].astype(MXU_DTYPE), w_ref[...], preferred_element_type=F32)
            dh = part if dh is None else dh + part
        xv = x_ref[...]
        r = lax.rsqrt(jnp.mean(xv * xv, axis=-1, keepdims=True) + RMS_EPS)
        n = xv * r
        gg_ref[...] += jnp.sum(dh * n, axis=0, keepdims=True)
        dn = dh * g_ref[...]
        dx = dres_ref[...] + r * (dn - n * jnp.mean(dn * n, axis=-1, keepdims=True))
        dx_ref[...] = dx
        dxb_ref[...] = dx.astype(BF16)

    d = x.shape[1]
    return _grid_call(
        body, name, (t // tm, 1), [*operands, *weights, x, g, dres],
        [_row_spec(tm, a.shape[1]) for a in operands] + [_whole(wk) for wk in weights]
        + [_row_spec(tm, d), _whole(g), _row_spec(tm, d)],
        [_row_spec(tm, d)] * 2 + [pl.BlockSpec((1, d), lambda i, j: (0, 0))],
        [jax.ShapeDtypeStruct((t, d), F32), jax.ShapeDtypeStruct((t, d), BF16), jax.ShapeDtypeStruct((1, d), F32)],
        VMEM_BIG, comm, sequential=True)


def _flat_small(small):
    perm_b = lambda a: a.reshape(SSM_GROUPS, SSM_STATE, SSM_CH).transpose(2, 0, 1).reshape(SSM_CH, N_STATE)
    perm_c = lambda a: a.reshape(SSM_GROUPS, SSM_CH, SSM_STATE).transpose(1, 0, 2).reshape(SSM_CH, N_STATE)
    return dict(
        g_mix=small["norm_mix_g"].reshape(1, D_MODEL), g_ffn=small["norm_ffn_g"].reshape(1, D_MODEL),
        g_fin=small["norm_final_g"].reshape(1, D_MODEL),
        lr=small["ssm_a_re"].reshape(1, N_STATE), li=small["ssm_a_im"].reshape(1, N_STATE),
        ldt=jnp.repeat(small["ssm_log_dt"].reshape(SSM_GROUPS), SSM_STATE).reshape(1, N_STATE),
        br=perm_b(small["ssm_b_re"]), bi=perm_b(small["ssm_b_im"]),
        cr=perm_c(small["ssm_c_re"]), ci=perm_c(small["ssm_c_im"]), dskip=small["ssm_d"].reshape(1, SSM_W))


AG_HOSTS = {"proj_rope": ("w_glu", "w_attn_out", "w_out"), "attn_fwd_g0": ("w_ffn_gate",), "attn_fwd_g1": ("w_ffn_up",),
            "ffn_in_swiglu": ("w_ffn_down",)}
HALVED = ("w_ffn_gate", "w_ffn_up", "w_in")
A2A_HOSTS = {"d_h2_rms": ("w_ffn_down",), "attn_bwd_g0": ("w_ffn_gate:0",), "attn_bwd_g1": ("w_ffn_gate:1",),
             "attn_bwd_g2": ("w_ffn_up:0",), "ssm_bwd": ("w_ffn_up:1", "w_out", "w_attn_out", "w_glu"),
             "mm_g_in1": ("w_in:0",), "d_h0_rms": ("w_in:1",)}


def _local_step(x, target, w, small, shards=None):
    t = x.shape[0]
    n_samples = t // SEQ
    n_rows = n_samples * SCAN_SEG_PER_SAMPLE
    tabs = _rope_tables()
    w = dict(w)
    fs = _flat_small(small)
    g_mix, g_ffn, g_fin, dskip = fs["g_mix"], fs["g_ffn"], fs["g_fin"], fs["dskip"]
    a_cat, bbc, cc = _ssm_disc(fs["lr"], fs["li"], fs["ldt"], fs["br"], fs["bi"], fs["cr"], fs["ci"])
    big, recv = {}, {}

    def comm_of(name):
        if shards is None:
            return None
        if name in AG_HOSTS:
            names = AG_HOSTS[name]
            return _ag_comm([(shards[n], j, 0) for j, n in enumerate(names)], [(N_DEV, *shards[n].shape) for n in names])
        if name in A2A_HOSTS:
            return _a2a_comm([(big[n].reshape(N_DEV, -1, big[n].shape[1]), 0) for n in A2A_HOSTS[name]])
        return None

    def absorb(name, carried):
        for n, a3 in zip(AG_HOSTS.get(name, ()), carried):
            w[n] = a3.reshape(-1, a3.shape[2])
        for n, a3 in zip(A2A_HOSTS.get(name, ()), carried):
            recv[n] = a3

    def mm(a, b, mode, name, tm, tn, **kw):
        comm = comm_of(name)
        if comm is None:
            return _mm(a, b, mode, name, tm, tn, **kw)
        out, *carried = _mm(a, b, mode, name, tm, tn, comm=comm, **kw)
        absorb(name, carried)
        return out

    h0, u, gates, *rest = _proj_rope(x, g_mix, w["w_in"], tabs, comm_of("proj_rope"))
    qkv = [rest[3 * g:3 * g + 3] for g in range(3)]
    absorb("proj_rope", rest[9:])
    os_, lses = [], []
    for g in range(3):
        o_g, l_g, carried = _attn_fwd(*qkv[g], g, n_samples, comm_of(f"attn_fwd_g{g}"))
        absorb(f"attn_fwd_g{g}", carried)
        os_.append(o_g)
        lses.append(l_g)
    attn, lse_tot = _attn_merge(os_, lses)
    attn_d = mm(attn, w["w_attn_out"], "nt", "mm_attn_out", 512, D_MODEL)

    u_perm = _to_scan_rows(u, n_samples)
    ytot, yg_perm, ein = _ssm_fwd(u_perm, a_cat, bbc, cc, dskip, n_rows)
    yg = _from_scan_rows(yg_perm, n_samples)
    z = mm(yg, w["w_glu"], "nt", "mm_glu", 512, 2 * D_MODEL)

    merged = _mix(attn_d, z, gates)
    x1, h2 = _out_rms(merged, w["w_out"], x, g_ffn)
    ffn_a, ffn_b, f, *carried = _ffn_in_swiglu(h2, w["w_ffn_gate"], w["w_ffn_up"], comm_of("ffn_in_swiglu"))
    absorb("ffn_in_swiglu", carried)
    dx2, dx2b, loss_blk, g_gfin = _ffn_down_final(f, w["w_ffn_down"], x1, target, g_fin)

    da, db = _d_f_swiglu_bwd(dx2b, w["w_ffn_down"], ffn_a, ffn_b)
    big["w_ffn_down"] = mm(f, dx2b, "tn", "mm_g_down", 256, 512, out_dtype=BF16)
    half = D_MODEL // 2
    for hf in range(2):
        big[f"w_ffn_gate:{hf}"] = mm(da, h2, "tn", f"mm_g_gate{hf}", 256, half, out_dtype=BF16, cols=(hf * half, half))
        big[f"w_ffn_up:{hf}"] = mm(db, h2, "tn", f"mm_g_up{hf}", 256, half, out_dtype=BF16, cols=(hf * half, half))
    dx1, dx1b, g_gffn, *carried = _mm_rms_bwd([da, db], [w["w_ffn_gate"], w["w_ffn_up"]], x1, g_ffn, dx2, "d_h2_rms",
                                              comm_of("d_h2_rms"))
    absorb("d_h2_rms", carried)

    dmerged = mm(dx1b, w["w_out"], "nt", "mm_d_merged", 512, D_MODEL)
    big["w_out"] = mm(merged, dx1b, "tn", "mm_g_out", 512, 512, out_dtype=BF16)
    dattn_d, dz, dgpre = _mix_bwd(dmerged, gates, attn_d, z)

    dattn = mm(dattn_d, w["w_attn_out"], "nn", "mm_d_attn", 512, GROUP_W)
    big["w_attn_out"] = mm(dattn_d, attn, "tn", "mm_g_attn_out", 512, GROUP_W, out_dtype=BF16)
    cot = _attn_rowdot(dattn, attn, lse_tot)
    dqs, dks, dvs = [], [], []
    for g in range(3):
        dq_g, dk_g, dv_g, carried = _attn_bwd(*qkv[g], *cot[g], g, n_samples, comm_of(f"attn_bwd_g{g}"))
        absorb(f"attn_bwd_g{g}", carried)
        dqs.append(dq_g)
        dks.append(dk_g)
        dvs.append(dv_g)

    dyg = mm(dz, w["w_glu"], "nn", "mm_d_yg", 512, SSM_W)
    big["w_glu"] = mm(dz, yg, "tn", "mm_g_glu", 512, 512, out_dtype=BF16)
    dyg_perm = _to_scan_rows(dyg, n_samples)
    dypre, du_skip, g_dskip = _ssm_act_bwd(dyg_perm, ytot, u_perm, dskip)
    du_perm, da_cat, dbb_full, dc_full, *carried = _ssm_bwd(u_perm, dypre, du_skip, a_cat, bbc, cc, ein, n_rows,
                                                          comm_of("ssm_bwd"))
    absorb("ssm_bwd", carried)
    du = _from_scan_rows(du_perm, n_samples)
    g_lr, g_li, g_ldt, g_br, g_bi, g_cr, g_ci = _ssm_param_bwd(
        fs["lr"], fs["li"], fs["ldt"], fs["br"], fs["bi"], da_cat, dbb_full, dc_full)

    dproj = _pack_dproj(dqs, dks, dvs, du, dgpre, tabs)
    for hf in range(2):
        big[f"w_in:{hf}"] = mm(dproj, h0, "tn", f"mm_g_in{hf}", 256, half, out_dtype=BF16, cols=(hf * half, half))
    grad_x, _, g_gmix, *carried = _mm_rms_bwd([dproj], [w["w_in"]], x, g_mix, dx1, "d_h0_rms", comm_of("d_h0_rms"))
    absorb("d_h0_rms", carried)

    small_g = dict(lr=g_lr, li=g_li, ldt=g_ldt, br=g_br, bi=g_bi, cr=g_cr, ci=g_ci, dskip=g_dskip,
                   g_mix=g_gmix, g_ffn=g_gffn, g_fin=g_gfin, loss=loss_blk)
    return loss_blk, grad_x, (big if shards is None else recv), small_g


_MESH = pl.DeviceIdType.MESH


def _all_gather(block, name):
    rows, lanes = block.shape

    def body(x_ref, out_ref, send_sems, recv_sems, local_sem):
        x, y, c = lax.axis_index("x"), lax.axis_index("y"), lax.axis_index("c")
        me, sibling = (x, y, c), (x, y, 1 - c)
        chips = [(1 - x, y), (x, 1 - y), (1 - x, 1 - y)]

        def slot(px, py, pc):
            return out_ref.at[4 * px + 2 * py + pc]

        def copy(k, blk, to, src=None):
            return pltpu.make_async_remote_copy(
                src_ref=slot(*blk) if src is None else src, dst_ref=slot(*blk), send_sem=send_sems.at[k],
                recv_sem=recv_sems.at[k], device_id=to, device_id_type=_MESH)

        mine = pltpu.make_async_copy(x_ref, slot(*me), local_sem)
        mine.start()
        first = [copy(0, me, sibling, src=x_ref)]
        first += [copy(1 + j, me, (*chip, c), src=x_ref) for j, chip in enumerate(chips)]
        for cp in first:
            cp.start()
        passed = [copy(4 + j, (*chip, c), sibling) for j, chip in enumerate(chips)]
        for j, chip in enumerate(chips):
            copy(1 + j, (*chip, c), me).wait_recv()
            passed[j].start()
        copy(0, sibling, me).wait_recv()
        for j, chip in enumerate(chips):
            copy(4 + j, (*chip, 1 - c), me).wait_recv()
        for cp in first + passed:
            cp.wait_send()
        mine.wait()

    return _pallas_call(
        body, name=name, out_shape=jax.ShapeDtypeStruct((N_DEV, rows, lanes), block.dtype),
        in_specs=[pl.BlockSpec(memory_space=pl.ANY)], out_specs=pl.BlockSpec(memory_space=pl.ANY),
        scratch_shapes=[pltpu.SemaphoreType.DMA((7,)), pltpu.SemaphoreType.DMA((7,)), pltpu.SemaphoreType.DMA],
    )(block)


def _ag_comm(items, bufs):
    def plan(in_refs, out_refs, send_sems, recv_sems, local_sems):
        x, y, c = lax.axis_index("x"), lax.axis_index("y"), lax.axis_index("c")
        me, sibling = (x, y, c), (x, y, 1 - c)
        chips = [(1 - x, y), (x, 1 - y), (1 - x, 1 - y)]
        plans = []
        for t, (_, buf, slot0) in enumerate(items):
            x_ref, out_ref = in_refs[t], out_refs[buf]

            def slot(px, py, pc, out_ref=out_ref, slot0=slot0):
                return out_ref.at[slot0 + 4 * px + 2 * py + pc]

            def copy(k, blk, to, src=None, t=t, slot=slot):
                return pltpu.make_async_remote_copy(
                    src_ref=slot(*blk) if src is None else src, dst_ref=slot(*blk), send_sem=send_sems.at[7 * t + k],
                    recv_sem=recv_sems.at[7 * t + k], device_id=to, device_id_type=_MESH)

            plans.append(dict(
                mine=pltpu.make_async_copy(x_ref, slot(*me), local_sems.at[t]),
                first=[copy(0, me, sibling, src=x_ref)] + [copy(1 + j, me, (*chip, c), src=x_ref)
                                                           for j, chip in enumerate(chips)],
                passed=[copy(4 + j, (*chip, c), sibling) for j, chip in enumerate(chips)],
                from_ici=[copy(1 + j, (*chip, c), me) for j, chip in enumerate(chips)],
                from_sibling=[copy(0, sibling, me)] + [copy(4 + j, (*chip, 1 - c), me) for j, chip in enumerate(chips)]))
        return plans

    def start(*refs):
        for p in plan(*refs):
            p["mine"].start()
            for cp in p["first"]:
                cp.start()

    def finish(*refs):
        plans = plan(*refs)
        for p in plans:
            for arrived, onward in zip(p["from_ici"], p["passed"]):
                arrived.wait_recv()
                onward.start()
        for p in plans:
            for arrived in p["from_sibling"]:
                arrived.wait_recv()
            for cp in p["first"] + p["passed"]:
                cp.wait_send()
            p["mine"].wait()

    out_shapes = [jax.ShapeDtypeStruct(b, items[0][0].dtype) for b in bufs]
    return _Comm([it[0] for it in items], out_shapes, 7 * len(items), len(items), start, finish)


def _a2a_comm(items):
    def plan(in_refs, out_refs, send_sems, recv_sems, local_sems):
        x, y, c = lax.axis_index("x"), lax.axis_index("y"), lax.axis_index("c")
        my = 4 * x + 2 * y + c
        copies, locals_ = [], []
        for t, (_, slot0) in enumerate(items):
            s_ref, r_ref = in_refs[t], out_refs[t]
            locals_.append(pltpu.make_async_copy(s_ref.at[slot0 + my], r_ref.at[my], local_sems.at[t]))
            for kk in range(1, N_DEV):
                px = 1 - x if kk & 4 else x
                py = 1 - y if kk & 2 else y
                pc = 1 - c if kk & 1 else c
                copies.append(pltpu.make_async_remote_copy(
                    src_ref=s_ref.at[slot0 + 4 * px + 2 * py + pc], dst_ref=r_ref.at[my],
                    send_sem=send_sems.at[7 * t + kk - 1], recv_sem=recv_sems.at[7 * t + kk - 1],
                    device_id=(px, py, pc), device_id_type=_MESH))
        return copies, locals_

    def start(*refs):
        copies, locals_ = plan(*refs)
        for cp in locals_ + copies:
            cp.start()

    def finish(*refs):
        copies, locals_ = plan(*refs)
        for cp in copies + locals_:
            cp.wait()

    out_shapes = [jax.ShapeDtypeStruct((N_DEV,) + it[0].shape[1:], it[0].dtype) for it in items]
    return _Comm([it[0] for it in items], out_shapes, 7 * len(items), len(items), start, finish)


def _adam_math(g, w, m, v):
    m_new = ADAM_B1 * m + (1.0 - ADAM_B1) * g
    v_new = ADAM_B2 * v + (1.0 - ADAM_B2) * jnp.square(g)
    m_hat = m_new / (1.0 - ADAM_B1 ** ADAM_STEP)
    v_hat = v_new / (1.0 - ADAM_B2 ** ADAM_STEP)
    return -ADAM_LR * (m_hat / (jnp.sqrt(v_hat) + ADAM_EPS) + ADAM_WD * w), m_new, v_new


def _sum_partials(parts, name, tm):
    n, rows, _ = parts[0].shape
    widths = [p.shape[2] for p in parts]

    def body(*refs):
        g_ref, off = refs[-1], 0
        for p_ref, wd in zip(refs[:-1], widths):
            g = p_ref[0].astype(F32)
            for s in range(1, n):
                g = g + p_ref[s].astype(F32)
            g_ref[:, off:off + wd] = g
            off += wd

    return _pallas_call(
        body, name=name, grid=(rows // tm,), in_specs=[pl.BlockSpec((n, tm, wd), lambda i: (0, i, 0)) for wd in widths],
        out_specs=pl.BlockSpec((tm, sum(widths)), lambda i: (i, 0)),
        out_shape=jax.ShapeDtypeStruct((rows, sum(widths)), F32),
        compiler_params=pltpu.CompilerParams(dimension_semantics=("parallel",), vmem_limit_bytes=VMEM_MID),
    )(*parts)


def _adam(partials, w, m, v, name, tm):
    n, rows, cols = partials.shape

    def body(p_ref, w_ref, m_ref, v_ref, g_ref, d_ref, nm_ref, nv_ref):
        g = p_ref[0].astype(F32)
        for s in range(1, n):
            g = g + p_ref[s].astype(F32)
        g_ref[...] = g
        d_ref[...], nm_ref[...], nv_ref[...] = _adam_math(g, w_ref[...], m_ref[...], v_ref[...])

    assert rows % tm == 0
    row = pl.BlockSpec((tm, cols), lambda i: (i, 0))
    shp = jax.ShapeDtypeStruct((rows, cols), F32)
    return _pallas_call(
        body, name=name, grid=(rows // tm,),
        in_specs=[pl.BlockSpec((n, tm, cols), lambda i: (0, i, 0)), row, row, row],
        out_specs=[row] * 4, out_shape=[shp] * 4,
        compiler_params=pltpu.CompilerParams(dimension_semantics=("parallel",), vmem_limit_bytes=VMEM_MID),
    )(partials, w, m, v)


_PK_LR, _PK_LI, _PK_GAINS, _PK_MISC, _PK_BR, _PK_BI, _PK_CR, _PK_CI, _PK_ROWS = 0, 1, 2, 3, 8, 24, 40, 56, 72
_PK_LDT_LANE, _PK_LOSS_LANE = D_MODEL + SSM_W, D_MODEL + SSM_W + LANES


def _pack_small(sg):
    names = ("lr", "li", "g_mix", "g_ffn", "g_fin", "dskip", "ldt", "loss", "br", "bi", "cr", "ci")

    def body(lr, li, gmix, gffn, gfin, dskip, ldt, loss, br, bi, cr, ci, o_ref):
        o_ref[...] = jnp.zeros_like(o_ref)
        o_ref[_PK_LR:_PK_LR + 1, :] = lr[...]
        o_ref[_PK_LI:_PK_LI + 1, :] = li[...]
        o_ref[_PK_GAINS:_PK_GAINS + 1, :D_MODEL] = gmix[...]
        o_ref[_PK_GAINS:_PK_GAINS + 1, D_MODEL:] = gffn[...]
        o_ref[_PK_MISC:_PK_MISC + 1, :D_MODEL] = gfin[...]
        o_ref[_PK_MISC:_PK_MISC + 1, D_MODEL:D_MODEL + SSM_W] = dskip[...]
        o_ref[_PK_MISC:_PK_MISC + 1, _PK_LDT_LANE:_PK_LDT_LANE + LANES] = ldt[0:1, :]
        o_ref[_PK_MISC:_PK_MISC + 1, _PK_LOSS_LANE:_PK_LOSS_LANE + LANES] = loss[0:1, :]
        o_ref[_PK_BR:_PK_BR + SSM_CH, :] = br[...]
        o_ref[_PK_BI:_PK_BI + SSM_CH, :] = bi[...]
        o_ref[_PK_CR:_PK_CR + SSM_CH, :] = cr[...]
        o_ref[_PK_CI:_PK_CI + SSM_CH, :] = ci[...]

    return _pallas_call(body, name="pack_small", out_shape=jax.ShapeDtypeStruct((_PK_ROWS, N_STATE), F32))(
        *[sg[n] for n in names])


def _unpack_small(s):
    unflat_b = lambda a: a.reshape(SSM_CH, SSM_GROUPS, SSM_STATE).transpose(1, 2, 0)[None]
    unflat_c = lambda a: a.reshape(SSM_CH, SSM_GROUPS, SSM_STATE).transpose(1, 0, 2)[None]
    grads = {
        "norm_mix_g": s[_PK_GAINS, :D_MODEL].reshape(1, D_MODEL), "norm_ffn_g": s[_PK_GAINS, D_MODEL:].reshape(1, D_MODEL),
        "norm_final_g": s[_PK_MISC, :D_MODEL],
        "ssm_a_re": s[_PK_LR].reshape(1, SSM_GROUPS, SSM_STATE), "ssm_a_im": s[_PK_LI].reshape(1, SSM_GROUPS, SSM_STATE),
        "ssm_log_dt": s[_PK_MISC, _PK_LDT_LANE:_PK_LDT_LANE + SSM_GROUPS].reshape(1, SSM_GROUPS),
        "ssm_d": s[_PK_MISC, D_MODEL:D_MODEL + SSM_W].reshape(1, SSM_GROUPS, SSM_CH),
        "ssm_b_re": unflat_b(s[_PK_BR:_PK_BR + SSM_CH]), "ssm_b_im": unflat_b(s[_PK_BI:_PK_BI + SSM_CH]),
        "ssm_c_re": unflat_c(s[_PK_CR:_PK_CR + SSM_CH]), "ssm_c_im": unflat_c(s[_PK_CI:_PK_CI + SSM_CH]),
    }
    return s[_PK_MISC, _PK_LOSS_LANE], grads


def _adam_small(grads, wts, moms, vars_):
    n = len(SMALL_WEIGHTS)
    as2d = lambda a: a.reshape(1, -1) if a.ndim == 1 else a

    def body(*refs):
        ins, outs = refs[:4 * n], refs[4 * n:]
        for i in range(n):
            g, w, m, v = (ins[j * n + i][...] for j in range(4))
            outs[i][...], outs[n + i][...], outs[2 * n + i][...] = _adam_math(g, w, m, v)

    operands = [as2d(d[k]) for d in (grads, wts, moms, vars_) for k in SMALL_WEIGHTS]
    shapes = [jax.ShapeDtypeStruct(as2d(wts[k]).shape, F32) for k in SMALL_WEIGHTS] * 3
    res = _pallas_call(body, name="adam_small", out_shape=shapes,
                         compiler_params=pltpu.CompilerParams(vmem_limit_bytes=VMEM_BIG))(*operands)
    out = {}
    for j, kind in enumerate(("delta", "new_m", "new_v")):
        for i, k in enumerate(SMALL_WEIGHTS):
            out[kind, k] = res[j * n + i].reshape(wts[k].shape)
    return out


def kernel(x, norm_mix_g, w_in, ssm_a_re, ssm_a_im, ssm_log_dt, ssm_b_re, ssm_b_im, ssm_c_re, ssm_c_im, ssm_d, w_glu, w_attn_out, w_out, norm_ffn_g, w_ffn_gate, w_ffn_up, w_ffn_down, norm_final_g, loss_target, m_norm_mix_g, m_w_in, m_ssm_a_re, m_ssm_a_im, m_ssm_log_dt, m_ssm_b_re, m_ssm_b_im, m_ssm_c_re, m_ssm_c_im, m_ssm_d, m_w_glu, m_w_attn_out, m_w_out, m_norm_ffn_g, m_w_ffn_gate, m_w_ffn_up, m_w_ffn_down, m_norm_final_g, v_norm_mix_g, v_w_in, v_ssm_a_re, v_ssm_a_im, v_ssm_log_dt, v_ssm_b_re, v_ssm_b_im, v_ssm_c_re, v_ssm_c_im, v_ssm_d, v_w_glu, v_w_attn_out, v_w_out, v_norm_ffn_g, v_w_ffn_gate, v_w_ffn_up, v_w_ffn_down, v_norm_final_g):
    args = dict(locals())
    wts = {n: args[n] for n in ALL_WEIGHTS}
    moms = {n: args["m_" + n] for n in ALL_WEIGHTS}
    vars_ = {n: args["v_" + n] for n in ALL_WEIGHTS}
    n_samples = x.shape[0]
    t = n_samples * SEQ

    shards = {n: (wts[n][0] if n in ROW_SHARDED else wts[n][0].T).astype(BF16) for n in BIG_WEIGHTS}
    w_in_t = _all_gather(shards["w_in"], "allgather_w_in").reshape(IN_W, D_MODEL)

    small = {n: wts[n] for n in SMALL_WEIGHTS}
    _, grad_x, recv, small_g = _local_step(x.reshape(t, D_MODEL), loss_target.reshape(t, D_MODEL), {"w_in": w_in_t},
                                           small, shards)

    results = {}
    for n in BIG_WEIGHTS:
        c, k = shards[n].shape
        w2, m2, v2 = wts[n][0], moms[n][0], vars_[n][0]
        if n in ROW_SHARDED:
            res = _adam(recv[n], w2, m2, v2, "adam_" + n, c // 2)
        else:
            parts = [recv[f"{n}:{hf}"] for hf in range(2)] if n in HALVED else [recv[n]]
            g_t = _sum_partials(parts, "sum_" + n, c // 2)
            res = _adam(g_t.T[None], w2, m2, v2, "adam_" + n, k // 2)
        for kind, a in zip(("grad", "delta", "new_m", "new_v"), res):
            results[kind, n] = a[None]

    sgath = _all_gather(_pack_small(small_g), "allgather_small_grads")
    loss, sgrads = _unpack_small(_sum_partials([sgath], "sum_small", _PK_ROWS))
    for n in SMALL_WEIGHTS:
        results["grad", n] = sgrads[n]
    results.update(_adam_small(sgrads, wts, moms, vars_))
    outs = [loss, grad_x.reshape(x.shape)]
    for kind in ("grad", "delta", "new_m", "new_v"):
        outs += [results[kind, n] for n in ALL_WEIGHTS]
    return tuple(outs)
```

```python
import functools
import math

import jax
import jax.numpy as jnp
from jax import lax
from jax.experimental import pallas as pl
from jax.experimental.pallas import tpu as pltpu

F32 = jnp.float32
BF16 = jnp.bfloat16
MXU_DTYPE = jnp.bfloat16

N_DEV = 8
D_MODEL = 1024
SEQ = 2048
HEAD_DIM = 64
HEADS_PER_GROUP = 4
GROUP_W = HEADS_PER_GROUP * HEAD_DIM
DILATIONS = (1, 4, 16)
QKV_W = 3 * len(DILATIONS) * GROUP_W
Q_W = len(DILATIONS) * GROUP_W
ATT_BLOCK = 128
ROPE_DIM = 16
ROPE_THETA = 500000.0
SSM_W = 512
SSM_GROUPS = 32
SSM_CH = 16
SSM_STATE = 64
N_STATE = SSM_GROUPS * SSM_STATE
D_FF = 2816
IN_W = QKV_W + SSM_W + 2 * D_MODEL
RMS_EPS = 1e-6
NEG_INF = -1e30
LANES = 128

SCAN_SEG_PER_SAMPLE = 8
SCAN_LEN = SEQ // SCAN_SEG_PER_SAMPLE
SCAN_WC = 512
SCAN_NBLK = N_STATE // SCAN_WC
SCAN_CH = SSM_W // SCAN_NBLK
SCAN_CHUNK = 32

ADAM_LR = 0.001
ADAM_B1 = 0.9
ADAM_B2 = 0.999
ADAM_EPS = 1e-08
ADAM_WD = 0.01
ADAM_STEP = 10

VMEM_BIG = 48 * 1024 * 1024
VMEM_MID = 32 * 1024 * 1024

BIG_WEIGHTS = ("w_in", "w_glu", "w_attn_out", "w_out", "w_ffn_gate", "w_ffn_up", "w_ffn_down")
ROW_SHARDED = ("w_out", "w_ffn_down")
SMALL_WEIGHTS = ("norm_mix_g", "ssm_a_re", "ssm_a_im", "ssm_log_dt", "ssm_b_re", "ssm_b_im", "ssm_c_re", "ssm_c_im",
                 "ssm_d", "norm_ffn_g", "norm_final_g")
ALL_WEIGHTS = ("norm_mix_g", "w_in", "ssm_a_re", "ssm_a_im", "ssm_log_dt", "ssm_b_re", "ssm_b_im", "ssm_c_re", "ssm_c_im",
               "ssm_d", "w_glu", "w_attn_out", "w_out", "norm_ffn_g", "w_ffn_gate", "w_ffn_up", "w_ffn_down", "norm_final_g")


def _sigmoid(x):
    return 1.0 / (1.0 + jnp.exp(-x))


def _pallas_call(body, *, out_shape, **kw):
    single = not isinstance(out_shape, (list, tuple))
    shapes = [pltpu.HBM(s.shape, s.dtype) for s in ([out_shape] if single else out_shape)]
    call = pl.pallas_call(body, out_shape=shapes[0] if single else shapes, **kw)
    return lambda *operands: call(*[pltpu.with_memory_space_constraint(o, pltpu.HBM) for o in operands])


class _Comm:
    def __init__(self, ins, out_shapes, n_sem, n_local, start, finish):
        self.ins, self.out_shapes, self.n_sem, self.n_local = ins, out_shapes, n_sem, n_local
        self.start, self.finish = start, finish


def _mm(a, b, mode, name, tm, tn, out_dtype=F32, add=None, vmem=VMEM_BIG, comm=None, cols=None):
    if mode == "nn":
        (m, k), (_, n) = a.shape, b.shape
        a_spec = pl.BlockSpec((tm, k), lambda i, j: (i, 0))
        b_spec = pl.BlockSpec((k, tn), lambda i, j: (0, j))
        dims = (((1,), (0,)), ((), ()))
    elif mode == "nt":
        (m, k), (n, _) = a.shape, b.shape
        a_spec = pl.BlockSpec((tm, k), lambda i, j: (i, 0))
        b_spec = pl.BlockSpec((tn, k), lambda i, j: (j, 0))
        dims = (((1,), (1,)), ((), ()))
    else:
        (k, m), (_, n) = a.shape, b.shape
        first, n = cols if cols else (0, n)
        a_spec = pl.BlockSpec((k, tm), lambda i, j: (0, i))
        b_spec = pl.BlockSpec((k, tn), lambda i, j: (0, j + first // tn))
        dims = (((0,), (0,)), ((), ()))
    assert m % tm == 0 and n % tn == 0, (name, m, n, tm, tn)
    o_spec = pl.BlockSpec((tm, tn), lambda i, j: (i, j))
    has_add = add is not None

    def body(*refs):
        a_ref, b_ref, o_ref = refs[0], refs[1], refs[-1]
        acc = lax.dot_general(a_ref[...].astype(MXU_DTYPE), b_ref[...].astype(MXU_DTYPE), dims,
                              preferred_element_type=F32)
        if has_add:
            acc = acc + refs[2][...]
        o_ref[...] = acc.astype(out_dtype)

    ins = [a, b] + ([add] if has_add else [])
    in_specs = [a_spec, b_spec] + ([o_spec] if has_add else [])
    return _grid_call(body, name, (m // tm, n // tn), ins, in_specs, [o_spec],
                      [jax.ShapeDtypeStruct((m, n), out_dtype)], vmem, comm)


def _grid_call(body, name, grid, ins, in_specs, out_specs, out_shapes, vmem, comm=None, sequential=False, scratch=()):
    if comm is None:
        single = len(out_shapes) == 1
        semantics = ("arbitrary", "arbitrary") if sequential else ("parallel", "parallel")
        return _pallas_call(
            body, name=name, grid=grid, in_specs=in_specs, out_specs=out_specs[0] if single else out_specs,
            out_shape=out_shapes[0] if single else out_shapes, scratch_shapes=list(scratch),
            compiler_params=pltpu.CompilerParams(dimension_semantics=semantics, vmem_limit_bytes=vmem),
        )(*ins)
    n_in, n_out, n_cin, n_cout = len(ins), len(out_shapes), len(comm.ins), len(comm.out_shapes)
    n_io = n_in + n_cin + n_out + n_cout

    def carrying(*refs):
        own = refs[:n_in] + refs[n_in + n_cin:n_in + n_cin + n_out] + refs[n_io:len(refs) - 3]
        c_args = (refs[n_in:n_in + n_cin], refs[n_in + n_cin + n_out:n_io], *refs[-3:])

        @pl.when((pl.program_id(0) == 0) & (pl.program_id(1) == 0))
        def _():
            comm.start(*c_args)

        body(*own)

        @pl.when((pl.program_id(0) == grid[0] - 1) & (pl.program_id(1) == grid[1] - 1))
        def _():
            comm.finish(*c_args)

    hbm = pl.BlockSpec(memory_space=pl.ANY)
    return _pallas_call(
        carrying, name=name, grid=grid, in_specs=list(in_specs) + [hbm] * n_cin,
        out_specs=list(out_specs) + [hbm] * n_cout, out_shape=list(out_shapes) + list(comm.out_shapes),
        scratch_shapes=list(scratch) + [pltpu.SemaphoreType.DMA((comm.n_sem,)), pltpu.SemaphoreType.DMA((comm.n_sem,)),
                                        pltpu.SemaphoreType.DMA((comm.n_local,))],
        compiler_params=pltpu.CompilerParams(dimension_semantics=("arbitrary", "arbitrary"), vmem_limit_bytes=vmem),
    )(*ins, *comm.ins)


def _rows(body, name, n_rows, tm, ins, outs, vmem=VMEM_MID, scratch=()):
    assert n_rows % tm == 0
    arrays, in_specs = [], []
    for kind, arr in ins:
        arrays.append(arr)
        if kind == "row":
            assert n_rows % arr.shape[0] == 0, (name, arr.shape)
            in_specs.append(pl.BlockSpec((tm * arr.shape[0] // n_rows, arr.shape[1]), lambda i: (i, 0)))
        elif kind == "tab":
            nblk = arr.shape[0] // tm
            in_specs.append(pl.BlockSpec((tm, arr.shape[1]), lambda i, nblk=nblk: (i % nblk, 0)))
        else:
            in_specs.append(pl.BlockSpec(arr.shape, lambda i, nd=arr.ndim: (0,) * nd))
    out_specs, out_shape = [], []
    for kind, shp, dt in outs:
        if kind == "row":
            out_specs.append(pl.BlockSpec((tm, shp), lambda i: (i, 0)))
            out_shape.append(jax.ShapeDtypeStruct((n_rows, shp), dt))
        elif kind == "dil":
            d, wd = shp
            out_specs.append(pl.BlockSpec((tm // d, d * wd), lambda i: (i, 0)))
            out_shape.append(jax.ShapeDtypeStruct((n_rows // d, d * wd), dt))
        else:
            out_specs.append(pl.BlockSpec(shp, lambda i, nd=len(shp): (0,) * nd))
            out_shape.append(jax.ShapeDtypeStruct(shp, dt))
    res = _pallas_call(
        body, name=name, grid=(n_rows // tm,), in_specs=in_specs, out_specs=out_specs, out_shape=out_shape,
        scratch_shapes=list(scratch),
        compiler_params=pltpu.CompilerParams(dimension_semantics=("arbitrary",), vmem_limit_bytes=vmem),
    )(*arrays)
    return res


def _gather_residue(stage, ch, r, d, n):
    return stage[ch, pl.ds(r, n, stride=d), :] if d > 1 else stage[ch]


def _scatter_residue(stage, ch, r, d, n, val):
    if d > 1:
        stage[ch, pl.ds(r, n, stride=d), :] = val
    else:
        stage[ch] = val


def _lane_chunk(ch):
    return slice(ch * LANES, (ch + 1) * LANES)


def _first_step():
    return pl.program_id(0) == 0


def _rope_tables():
    half = ROPE_DIM // 2
    inv = jnp.power(jnp.float32(ROPE_THETA), -jnp.arange(half, dtype=F32) * 2.0 / ROPE_DIM)
    ang = jnp.arange(SEQ, dtype=F32)[:, None] * inv[None, :]
    lane = jnp.arange(LANES) % HEAD_DIM
    cosl = jnp.cos(ang)[:, lane % half]
    sinl = jnp.sin(ang)[:, lane % half]
    tab_c = jnp.where(lane < ROPE_DIM, cosl, 1.0)
    tab_lo = jnp.where(lane < half, -sinl, 0.0)
    tab_hi = jnp.where((lane >= half) & (lane < ROPE_DIM), sinl, 0.0)
    return tab_c.astype(F32), tab_lo.astype(F32), tab_hi.astype(F32)


def _rope_apply(t, tc, tlo, thi):
    half = ROPE_DIM // 2
    return t * tc + pltpu.roll(t, LANES - half, 1) * tlo + pltpu.roll(t, half, 1) * thi


def _rope_transpose(dt, tc, tlo, thi):
    half = ROPE_DIM // 2
    return dt * tc + pltpu.roll(dt * tlo, half, 1) + pltpu.roll(dt * thi, LANES - half, 1)


def _pack_dproj(dqs, dks, dvs, du, dgpre, tabs):
    tm = 256

    def body(*refs):
        dq_refs, dk_refs, dv_refs = refs[0:3], refs[3:6], refs[6:9]
        du_ref, dg_ref, tc_ref, tlo_ref, thi_ref, o_ref, stage = refs[9:16]
        n_ch = QKV_W // LANES
        halves = GROUP_W // LANES
        for grp, d in enumerate(DILATIONS):
            for which, src in enumerate((dq_refs[grp], dk_refs[grp], dv_refs[grp])):
                for res in range(d):
                    for half in range(halves):
                        _scatter_residue(stage, which * (n_ch // 3) + grp * halves + half, res, d, tm // d,
                                         src[:, _lane_chunk(res * halves + half)])
        tc, tlo, thi = tc_ref[...], tlo_ref[...], thi_ref[...]
        for ch in range(n_ch):
            piece = stage[ch]
            o_ref[:, _lane_chunk(ch)] = (_rope_transpose(piece, tc, tlo, thi) if ch < 2 * n_ch // 3 else piece).astype(BF16)
        o_ref[:, QKV_W:QKV_W + SSM_W] = du_ref[...].astype(BF16)
        o_ref[:, QKV_W + SSM_W:] = dg_ref[...].astype(BF16)

    t = du.shape[0]
    ins = [("row", a) for a in (*dqs, *dks, *dvs, du, dgpre)] + [("tab", tb) for tb in tabs]
    return _rows(body, "pack_dproj", t, tm, ins, [("row", IN_W, BF16)],
                 scratch=[pltpu.VMEM((QKV_W // LANES, tm, LANES), F32)])[0]


def _attn_merge(os_, lses):
    tm = 256

    halves = GROUP_W // LANES

    def body(o0, o1, o2, l0, l1, l2, a_ref, lt_ref, nat):
        for grp, d in enumerate(DILATIONS[1:], start=1):
            for j, src in enumerate(((o0, o1, o2)[grp], (l0, l1, l2)[grp])):
                for res in range(d):
                    for half in range(halves):
                        _scatter_residue(nat, (grp - 1) * 4 + j * 2 + half, res, d, tm // d,
                                         src[:, _lane_chunk(res * halves + half)])
        for half in range(halves):
            sl = _lane_chunk(half)
            la, lb, lc = l0[:, sl], nat[2 + half], nat[6 + half]
            m = jnp.maximum(jnp.maximum(la, lb), lc)
            ea, eb, ec = jnp.exp(la - m), jnp.exp(lb - m), jnp.exp(lc - m)
            ssum = ea + eb + ec
            a_ref[:, sl] = (ea / ssum) * o0[:, sl] + (eb / ssum) * nat[half] + (ec / ssum) * nat[4 + half]
            lt_ref[:, sl] = m + jnp.log(ssum)

    t = os_[0].shape[0]
    return _rows(body, "attn_merge", t, tm, [("row", a) for a in (*os_, *lses)],
                 [("row", GROUP_W, F32), ("row", GROUP_W, F32)], scratch=[pltpu.VMEM((8, tm, LANES), F32)])


def _head_sum_matrix():
    r = jnp.arange(GROUP_W) // HEAD_DIM
    return (r[:, None] == r[None, :]).astype(F32)


def _attn_rowdot(dattn, attn, lse_tot):
    tm = 256

    halves = GROUP_W // LANES

    def body(da_ref, a_ref, lt_ref, ones_ref, rd_ref, *rest):
        dil, stage = rest[:6], rest[6]
        rd = jnp.dot(da_ref[...] * a_ref[...], ones_ref[...], preferred_element_type=F32, precision=lax.Precision.HIGHEST)
        rd_ref[...] = rd
        for half in range(halves):
            stage[half] = da_ref[:, _lane_chunk(half)]
            stage[2 + half] = lt_ref[:, _lane_chunk(half)]
            stage[4 + half] = rd[:, _lane_chunk(half)]
        for grp, d in enumerate(DILATIONS[1:], start=1):
            for j in range(3):
                for res in range(d):
                    for half in range(halves):
                        dil[3 * (grp - 1) + j][:, _lane_chunk(res * halves + half)] = _gather_residue(
                            stage, 2 * j + half, res, d, tm // d)

    t = attn.shape[0]
    outs = [("row", GROUP_W, F32)] + [("dil", (d, GROUP_W), F32) for d in DILATIONS[1:] for _ in range(3)]
    rd, *dil = _rows(body, "attn_rowdot", t, tm,
                     [("row", dattn), ("row", attn), ("row", lse_tot), ("const", _head_sum_matrix())], outs,
                     scratch=[pltpu.VMEM((6, tm, LANES), F32)])
    return [(dattn, lse_tot, rd), tuple(dil[:3]), tuple(dil[3:])]


def _mix(attn_d, z, gates):
    def body(ad_ref, z_ref, g_ref, m_ref):
        za, zb = z_ref[:, :D_MODEL], z_ref[:, D_MODEL:]
        s_out = za * _sigmoid(zb)
        m_ref[...] = (g_ref[:, :D_MODEL] * ad_ref[...] + g_ref[:, D_MODEL:] * s_out).astype(BF16)

    t = attn_d.shape[0]
    return _rows(body, "mix", t, 256, [("row", attn_d), ("row", z), ("row", gates)], [("row", D_MODEL, BF16)])[0]


def _mix_bwd(dmerged, gates, attn_d, z):
    def body(dm_ref, g_ref, ad_ref, z_ref, dad_ref, dz_ref, dg_ref):
        dm = dm_ref[...]
        g0, g1 = g_ref[:, :D_MODEL], g_ref[:, D_MODEL:]
        za, zb = z_ref[:, :D_MODEL], z_ref[:, D_MODEL:]
        sb = _sigmoid(zb)
        s_out = za * sb
        dad_ref[...] = (dm * g0).astype(BF16)
        ds = dm * g1
        dz_ref[:, :D_MODEL] = (ds * sb).astype(BF16)
        dz_ref[:, D_MODEL:] = (ds * za * sb * (1.0 - sb)).astype(BF16)
        dg_ref[:, :D_MODEL] = dm * ad_ref[...] * g0 * (1.0 - g0)
        dg_ref[:, D_MODEL:] = dm * s_out * g1 * (1.0 - g1)

    t = dmerged.shape[0]
    return _rows(body, "mix_bwd", t, 256, [("row", dmerged), ("row", gates), ("row", attn_d), ("row", z)],
                 [("row", D_MODEL, BF16), ("row", 2 * D_MODEL, BF16), ("row", 2 * D_MODEL, F32)])


_GELU_C = math.sqrt(2.0 / math.pi)


def _ssm_act_bwd(dyg, ytot, u_perm, dskip):
    def body(dyg_ref, yt_ref, u_ref, d_ref, dy_ref, dus_ref, dd_ref):
        @pl.when(_first_step())
        def _():
            dd_ref[...] = jnp.zeros_like(dd_ref)

        yt = yt_ref[...]
        th = jnp.tanh(_GELU_C * (yt + 0.044715 * (yt * yt * yt)))
        dgelu = 0.5 * (1.0 + th) + 0.5 * yt * (1.0 - th * th) * _GELU_C * (1.0 + 3.0 * 0.044715 * yt * yt)
        dy = dyg_ref[...] * dgelu
        dy_ref[...] = dy.astype(BF16)
        dus_ref[...] = dy * d_ref[...]
        dd_ref[...] += jnp.sum(dy * u_ref[...], axis=0, keepdims=True)

    t = dyg.shape[0]
    return _rows(body, "ssm_act_bwd", t, 512, [("row", dyg), ("row", ytot), ("row", u_perm), ("const", dskip)],
                 [("row", SSM_W, BF16), ("row", SSM_W, F32), ("acc", (1, SSM_W), F32)])


def _head_masks():
    lane = lax.broadcasted_iota(jnp.int32, (1, GROUP_W), 1)
    return [(lane // HEAD_DIM) == h for h in range(HEADS_PER_GROUP)]


def _band_mask(first):
    nk = ATT_BLOCK if first else 2 * ATT_BLOCK
    qi = lax.broadcasted_iota(jnp.int32, (ATT_BLOCK, nk), 0)
    ki = lax.broadcasted_iota(jnp.int32, (ATT_BLOCK, nk), 1)
    dist = qi - ki + (0 if first else ATT_BLOCK)
    return (dist >= 0) & (dist <= ATT_BLOCK)


_NT = (((1,), (1,)), ((), ()))
_TN = (((0,), (0,)), ((), ()))


def _attn_fwd(q, k, v, group, n_samples, comm=None):
    d = DILATIONS[group]
    length = SEQ // d
    nb = length // ATT_BLOCK

    def body(q_ref, k_ref, v_ref, o_ref, l_ref):
        masks = _head_masks()

        def block(qs, ks, first):
            nk = ATT_BLOCK if first else 2 * ATT_BLOCK
            qb = q_ref[0, pl.ds(qs, ATT_BLOCK), :]
            kc = k_ref[0, pl.ds(ks, nk), :]
            vc = v_ref[0, pl.ds(ks, nk), :]
            valid = _band_mask(first)
            o_acc = jnp.zeros((ATT_BLOCK, GROUP_W), F32)
            l_acc = jnp.zeros((ATT_BLOCK, GROUP_W), F32)
            for h in range(HEADS_PER_GROUP):
                qh = jnp.where(masks[h], qb, jnp.zeros_like(qb))
                s = lax.dot_general(qh, kc, _NT, preferred_element_type=F32) * (HEAD_DIM ** -0.5)
                s = jnp.where(valid, s, NEG_INF)
                m = jnp.max(s, axis=-1, keepdims=True)
                p = jnp.exp(s - m)
                l = jnp.sum(p, axis=-1, keepdims=True)
                pv = jnp.dot(p.astype(MXU_DTYPE), vc, preferred_element_type=F32)
                o_acc = jnp.where(masks[h], pv / l, o_acc)
                l_acc = jnp.where(masks[h], m + jnp.log(l), l_acc)
            o_ref[0, pl.ds(qs, ATT_BLOCK), :] = o_acc
            l_ref[0, pl.ds(qs, ATT_BLOCK), :] = l_acc

        block(0, 0, True)
        if nb > 1:
            def loop(n, carry):
                block(pl.multiple_of(n * ATT_BLOCK, ATT_BLOCK), pl.multiple_of((n - 1) * ATT_BLOCK, ATT_BLOCK), False)
                return carry

            lax.fori_loop(1, nb, loop, 0)

    per_sample = lambda a: a.reshape(n_samples, length, d * GROUP_W)
    spec = pl.BlockSpec((1, length, GROUP_W), lambda b, r: (b, 0, r))
    shp = jax.ShapeDtypeStruct((n_samples, length, d * GROUP_W), F32)
    o, lse, *carried = _grid_call(body, f"attn_fwd_g{group}", (n_samples, d), [per_sample(a) for a in (q, k, v)],
                                  [spec] * 3, [spec] * 2, [shp, shp], VMEM_MID, comm)
    flat = lambda a: a.reshape(n_samples * length, d * GROUP_W)
    return flat(o), flat(lse), carried


def _attn_bwd(q, k, v, dattn, lse_tot, rowdot, group, n_samples, comm=None):
    d = DILATIONS[group]
    length = SEQ // d
    nb = length // ATT_BLOCK

    def body(q_ref, k_ref, v_ref, da_ref, lt_ref, rd_ref, dq_ref, dk_ref, dv_ref):
        masks = _head_masks()
        dk_ref[...] = jnp.zeros_like(dk_ref)
        dv_ref[...] = jnp.zeros_like(dv_ref)

        def block(qs, ks, first):
            nk = ATT_BLOCK if first else 2 * ATT_BLOCK
            qb = q_ref[0, pl.ds(qs, ATT_BLOCK), :]
            kc = k_ref[0, pl.ds(ks, nk), :]
            vc = v_ref[0, pl.ds(ks, nk), :]
            da = da_ref[0, pl.ds(qs, ATT_BLOCK), :]
            lt = lt_ref[0, pl.ds(qs, ATT_BLOCK), :]
            rd = rd_ref[0, pl.ds(qs, ATT_BLOCK), :]
            valid = _band_mask(first)
            dq_acc = jnp.zeros((ATT_BLOCK, GROUP_W), F32)
            dk_acc = jnp.zeros((nk, GROUP_W), F32)
            dv_acc = jnp.zeros((nk, GROUP_W), F32)
            for h in range(HEADS_PER_GROUP):
                qh = jnp.where(masks[h], qb, jnp.zeros_like(qb))
                dah = jnp.where(masks[h], da, 0.0).astype(MXU_DTYPE)
                lt_h = jnp.max(jnp.where(masks[h], lt, -jnp.inf), axis=-1, keepdims=True)
                rd_h = jnp.max(jnp.where(masks[h], rd, -jnp.inf), axis=-1, keepdims=True)
                s = lax.dot_general(qh, kc, _NT, preferred_element_type=F32) * (HEAD_DIM ** -0.5)
                s = jnp.where(valid, s, NEG_INF)
                p = jnp.exp(s - lt_h)
                dp = lax.dot_general(dah, vc, _NT, preferred_element_type=F32)
                ds = (p * (dp - rd_h) * (HEAD_DIM ** -0.5)).astype(MXU_DTYPE)
                dq_h = jnp.dot(ds, kc, preferred_element_type=F32)
                dq_acc = jnp.where(masks[h], dq_h, dq_acc)
                dk_acc = dk_acc + lax.dot_general(ds, qh, _TN, preferred_element_type=F32)
                dv_acc = dv_acc + lax.dot_general(p.astype(MXU_DTYPE), dah, _TN, preferred_element_type=F32)
            dq_ref[0, pl.ds(qs, ATT_BLOCK), :] = dq_acc
            dk_ref[0, pl.ds(ks, nk), :] += dk_acc
            dv_ref[0, pl.ds(ks, nk), :] += dv_acc

        block(0, 0, True)
        if nb > 1:
            def loop(n, carry):
                block(pl.multiple_of(n * ATT_BLOCK, ATT_BLOCK), pl.multiple_of((n - 1) * ATT_BLOCK, ATT_BLOCK), False)
                return carry

            lax.fori_loop(1, nb, loop, 0)

    per_sample = lambda a: a.reshape(n_samples, length, d * GROUP_W)
    spec = pl.BlockSpec((1, length, GROUP_W), lambda b, r: (b, 0, r))
    shp = jax.ShapeDtypeStruct((n_samples, length, d * GROUP_W), F32)
    dq, dk, dv, *carried = _grid_call(
        body, f"attn_bwd_g{group}", (n_samples, d), [per_sample(a) for a in (q, k, v, dattn, lse_tot, rowdot)],
        [spec] * 6, [spec] * 3, [shp, shp, shp], VMEM_MID, comm)
    flat = lambda a: a.reshape(n_samples * length, d * GROUP_W)
    return flat(dq), flat(dk), flat(dv), carried


def _disc(lr, li, ldt, br, bi):
    dt = jnp.exp(ldt)
    mag = jnp.exp(lr * dt)
    ab_re, ab_im = mag * jnp.cos(li * dt), mag * jnp.sin(li * dt)
    den = lr * lr + li * li
    nr, ni = ab_re - 1.0, ab_im
    f_re = (nr * lr + ni * li) / den
    f_im = (ni * lr - nr * li) / den
    return ab_re, ab_im, f_re * br - f_im * bi, f_re * bi + f_im * br


def _state_mask():
    row_g = lax.broadcasted_iota(jnp.int32, (SCAN_CH, SCAN_WC), 0) // SSM_CH
    col_g = lax.broadcasted_iota(jnp.int32, (SCAN_CH, SCAN_WC), 1) // SSM_STATE
    return row_g == col_g


def _ssm_disc(lr, li, ldt, br, bi, cr, ci):
    w = SCAN_WC

    def body(lr_ref, li_ref, ldt_ref, br_ref, bi_ref, cr_ref, ci_ref, a_ref, bb_ref, c_ref):
        ar, ai, bbr, bbi = _disc(lr_ref[...], li_ref[...], ldt_ref[...], br_ref[...], bi_ref[...])
        crv, civ = cr_ref[...], ci_ref[...]
        mask = _state_mask()
        for cb in range(SCAN_NBLK):
            sl = slice(cb * w, (cb + 1) * w)
            rows = slice(cb * SCAN_CH, (cb + 1) * SCAN_CH)
            dense = lambda comp: jnp.where(mask, jnp.tile(comp[:, sl], (SCAN_CH // SSM_CH, 1)), 0.0)
            a_ref[:, 2 * cb * w:(2 * cb + 1) * w] = ar[:, sl]
            a_ref[:, (2 * cb + 1) * w:(2 * cb + 2) * w] = ai[:, sl]
            bb_ref[rows, :w] = dense(bbr).astype(MXU_DTYPE)
            bb_ref[rows, w:] = dense(bbi).astype(MXU_DTYPE)
            c_ref[rows, :w] = dense(crv).astype(MXU_DTYPE)
            c_ref[rows, w:] = (-dense(civ)).astype(MXU_DTYPE)

    return _pallas_call(
        body, name="ssm_disc",
        out_shape=[jax.ShapeDtypeStruct((1, 2 * N_STATE), F32), jax.ShapeDtypeStruct((SSM_W, 2 * w), MXU_DTYPE),
                   jax.ShapeDtypeStruct((SSM_W, 2 * w), MXU_DTYPE)],
        compiler_params=pltpu.CompilerParams(vmem_limit_bytes=VMEM_MID),
    )(lr, li, ldt, br, bi, cr, ci)


def _group_indicator():
    s = jnp.arange(N_STATE) // SSM_STATE
    return (s[:, None] == jnp.arange(LANES)[None, :]).astype(F32)


def _ssm_param_bwd(lr, li, ldt, br, bi, da_cat, dbb_full, dc_full):
    w = SCAN_WC

    def body(lr_ref, li_ref, ldt_ref, br_ref, bi_ref, da_ref, dbb_ref, dc_ref, ind_ref,
             glr_ref, gli_ref, gldt_ref, gbr_ref, gbi_ref, gcr_ref, gci_ref):
        mask = _state_mask()

        def diag_parts(ref):
            res = ([], [])
            for cb in range(SCAN_NBLK):
                for part in range(2):
                    blk = ref[cb * SCAN_CH:(cb + 1) * SCAN_CH, part * w:(part + 1) * w]
                    res[part].append(jnp.sum(jnp.where(mask, blk, 0.0).reshape(SCAN_CH // SSM_CH, SSM_CH, w), axis=0))
            return jnp.concatenate(res[0], axis=1), jnp.concatenate(res[1], axis=1)

        dar = jnp.concatenate([da_ref[:, 2 * cb * w:(2 * cb + 1) * w] for cb in range(SCAN_NBLK)], axis=1)
        dai = jnp.concatenate([da_ref[:, (2 * cb + 1) * w:(2 * cb + 2) * w] for cb in range(SCAN_NBLK)], axis=1)
        dbbr, dbbi = diag_parts(dbb_ref)
        dcr, dci_neg = diag_parts(dc_ref)
        gcr_ref[...] = dcr
        gci_ref[...] = -dci_neg
        _, vjp = jax.vjp(_disc, lr_ref[...], li_ref[...], ldt_ref[...], br_ref[...], bi_ref[...])
        glr, gli, gldt, gbr, gbi = vjp((dar, dai, dbbr, dbbi))
        glr_ref[...] = glr
        gli_ref[...] = gli
        gldt_ref[...] = jnp.dot(jnp.broadcast_to(gldt, (8, N_STATE)), ind_ref[...], preferred_element_type=F32,
                                precision=lax.Precision.HIGHEST)
        gbr_ref[...] = gbr
        gbi_ref[...] = gbi

    v1 = jax.ShapeDtypeStruct((1, N_STATE), F32)
    v16 = jax.ShapeDtypeStruct((SSM_CH, N_STATE), F32)
    vdt = jax.ShapeDtypeStruct((8, LANES), F32)
    return _pallas_call(
        body, name="ssm_param_bwd", out_shape=[v1, v1, vdt, v16, v16, v16, v16],
        compiler_params=pltpu.CompilerParams(vmem_limit_bytes=VMEM_BIG),
    )(lr, li, ldt, br, bi, da_cat, dbb_full, dc_full, _group_indicator())


def _cmul(ar, ai, br, bi):
    return ar * br - ai * bi, ar * bi + ai * br


def _gelu_tanh(y):
    return jnp.tanh(_GELU_C * (y + 0.044715 * (y * y * y)))


def _segment_carry(er, ei, ar, ai, n_rows, reverse):
    qr, qi = ar, ai
    for _ in range(int(math.log2(SCAN_LEN))):
        qr, qi = _cmul(qr, qi, qr, qi)
    seg = lax.broadcasted_iota(jnp.int32, er.shape, 0) % SCAN_SEG_PER_SAMPLE
    shift = 1
    while shift < SCAN_SEG_PER_SAMPLE:
        keep = (seg < SCAN_SEG_PER_SAMPLE - shift) if reverse else (seg >= shift)
        amount = n_rows - shift if reverse else shift
        sr = jnp.where(keep, pltpu.roll(er, amount, 0), 0.0)
        si = jnp.where(keep, pltpu.roll(ei, amount, 0), 0.0)
        if reverse:
            er, ei = er + qr * sr + qi * si, ei + qr * si - qi * sr
        else:
            er, ei = er + qr * sr - qi * si, ei + qr * si + qi * sr
        qr, qi = _cmul(qr, qi, qr, qi)
        shift *= 2
    keep = (seg < SCAN_SEG_PER_SAMPLE - 1) if reverse else (seg >= 1)
    amount = n_rows - 1 if reverse else 1
    return jnp.where(keep, pltpu.roll(er, amount, 0), 0.0), jnp.where(keep, pltpu.roll(ei, amount, 0), 0.0)


def _ssm_fwd(u_perm, a_cat, bbc, cc, dskip, n_rows):
    t = u_perm.shape[0]
    w = SCAN_WC
    rows_c = SCAN_CHUNK * n_rows
    n_chunks = t // rows_c

    def body(u_ref, a_ref, bb_ref, c_ref, d_ref, yt_ref, yg_ref, ein_ref, bu_s, xs_s):
        ar = jnp.broadcast_to(a_ref[:, :w], (n_rows, w))
        ai = jnp.broadcast_to(a_ref[:, w:], (n_rows, w))

        def sweep(carry, store):
            def chunk(ch, carry):
                r0 = pl.multiple_of(ch * rows_c, rows_c)
                u_c = u_ref[pl.ds(r0, rows_c), :]
                if not store:
                    bu_s[pl.ds(r0, rows_c), :] = jnp.dot(u_c.astype(MXU_DTYPE), bb_ref[...], preferred_element_type=F32)

                def step(i, c):
                    o = pl.multiple_of(i * n_rows, n_rows)
                    blk = bu_s[pl.ds(r0 + o, n_rows), :]
                    nr = ar * c[0] - ai * c[1] + blk[:, :w]
                    ni = ar * c[1] + ai * c[0] + blk[:, w:]
                    if store:
                        xs_s[pl.ds(o, n_rows), :w] = nr
                        xs_s[pl.ds(o, n_rows), w:] = ni
                    return nr, ni

                carry = lax.fori_loop(0, SCAN_CHUNK, step, carry)
                if store:
                    y = lax.dot_general(xs_s[...].astype(MXU_DTYPE), c_ref[...], _NT, preferred_element_type=F32)
                    yt = y + d_ref[...] * u_c
                    yt_ref[pl.ds(r0, rows_c), :] = yt
                    yg_ref[pl.ds(r0, rows_c), :] = (0.5 * yt * (1.0 + _gelu_tanh(yt))).astype(BF16)
                return carry

            return lax.fori_loop(0, n_chunks, chunk, carry)

        zero = jnp.zeros((n_rows, w), F32)
        er, ei = sweep((zero, zero), False)
        cr, ci = _segment_carry(er, ei, ar, ai, n_rows, False)
        ein_ref[:, :w] = cr
        ein_ref[:, w:] = ci
        sweep((cr, ci), True)

    col = lambda width: pl.BlockSpec((t, width), lambda c: (0, c))
    wgt = pl.BlockSpec((SCAN_CH, 2 * w), lambda c: (c, 0))
    return _pallas_call(
        body, name="ssm_fwd", grid=(SCAN_NBLK,),
        in_specs=[col(SCAN_CH), pl.BlockSpec((1, 2 * w), lambda c: (0, c)), wgt, wgt,
                  pl.BlockSpec((1, SCAN_CH), lambda c: (0, c))],
        out_specs=[col(SCAN_CH), col(SCAN_CH), pl.BlockSpec((n_rows, 2 * w), lambda c: (0, c))],
        out_shape=[jax.ShapeDtypeStruct((t, SSM_W), F32), jax.ShapeDtypeStruct((t, SSM_W), BF16),
                   jax.ShapeDtypeStruct((n_rows, 2 * N_STATE), F32)],
        scratch_shapes=[pltpu.VMEM((t, 2 * w), F32), pltpu.VMEM((rows_c, 2 * w), F32)],
        compiler_params=pltpu.CompilerParams(dimension_semantics=("parallel",), vmem_limit_bytes=VMEM_BIG),
    )(u_perm, a_cat, bbc, cc, dskip)


def _ssm_bwd(u_perm, dypre, du_skip, a_cat, bbc, cc, ein, n_rows, comm=None):
    t = u_perm.shape[0]
    w = SCAN_WC
    rows_c = SCAN_CHUNK * n_rows
    n_chunks = t // rows_c

    def body(u_ref, dy_ref, dus_ref, a_ref, bb_ref, c_ref, ein_ref, du_ref, da_ref, dbb_ref, dc_ref, xs_all, tmp_s, g_s):
        ar = jnp.broadcast_to(a_ref[:, :w], (n_rows, w))
        ai = jnp.broadcast_to(a_ref[:, w:], (n_rows, w))
        zero = jnp.zeros((n_rows, w), F32)

        xs_all[0:n_rows, :] = ein_ref[...]

        def fwd_chunk(ch, carry):
            r0 = pl.multiple_of(ch * rows_c, rows_c)
            tmp_s[...] = jnp.dot(u_ref[pl.ds(r0, rows_c), :].astype(MXU_DTYPE), bb_ref[...], preferred_element_type=F32)

            def step(i, c):
                o = pl.multiple_of(i * n_rows, n_rows)
                blk = tmp_s[pl.ds(o, n_rows), :]
                nr = ar * c[0] - ai * c[1] + blk[:, :w]
                ni = ar * c[1] + ai * c[0] + blk[:, w:]
                xs_all[pl.ds(n_rows + r0 + o, n_rows), :w] = nr
                xs_all[pl.ds(n_rows + r0 + o, n_rows), w:] = ni
                return nr, ni

            return lax.fori_loop(0, SCAN_CHUNK, step, carry)

        lax.fori_loop(0, n_chunks, fwd_chunk, (ein_ref[:, :w], ein_ref[:, w:]))

        def load_dx(ch):
            r0 = pl.multiple_of(ch * rows_c, rows_c)
            tmp_s[...] = jnp.dot(dy_ref[pl.ds(r0, rows_c), :], c_ref[...], preferred_element_type=F32)
            return r0

        def back_steps(carry, store):
            def step(ii, c):
                o = pl.multiple_of((SCAN_CHUNK - 1 - ii) * n_rows, n_rows)
                blk = tmp_s[pl.ds(o, n_rows), :]
                gr = blk[:, :w] + ar * c[0] + ai * c[1]
                gi = blk[:, w:] + ar * c[1] - ai * c[0]
                if store:
                    g_s[pl.ds(o, n_rows), :w] = gr
                    g_s[pl.ds(o, n_rows), w:] = gi
                return gr, gi

            return lax.fori_loop(0, SCAN_CHUNK, step, carry)

        def first_sweep(cc_, carry):
            load_dx(n_chunks - 1 - cc_)
            return back_steps(carry, False)

        sr, si = lax.fori_loop(0, n_chunks, first_sweep, (zero, zero))
        gr0, gi0 = _segment_carry(sr, si, ar, ai, n_rows, True)

        dbb_ref[...] = jnp.zeros_like(dbb_ref)
        dc_ref[...] = jnp.zeros_like(dc_ref)
        da_ref[...] = jnp.zeros_like(da_ref)

        def second_sweep(cc_, carry):
            r0 = load_dx(n_chunks - 1 - cc_)
            carry = back_steps(carry, True)
            g = g_s[...]
            xp = xs_all[pl.ds(r0, rows_c), :]
            xc = xs_all[pl.ds(r0 + n_rows, rows_c), :]
            da_ref[:, :w] += jnp.sum(g[:, :w] * xp[:, :w] + g[:, w:] * xp[:, w:], axis=0, keepdims=True)
            da_ref[:, w:] += jnp.sum(g[:, w:] * xp[:, :w] - g[:, :w] * xp[:, w:], axis=0, keepdims=True)
            gb = g.astype(MXU_DTYPE)
            du_ref[pl.ds(r0, rows_c), :] = (lax.dot_general(gb, bb_ref[...], _NT, preferred_element_type=F32)
                                            + dus_ref[pl.ds(r0, rows_c), :])
            dbb_ref[...] += lax.dot_general(u_ref[pl.ds(r0, rows_c), :].astype(MXU_DTYPE), gb, _TN,
                                            preferred_element_type=F32)
            dc_ref[...] += lax.dot_general(dy_ref[pl.ds(r0, rows_c), :], xc.astype(MXU_DTYPE), _TN,
                                           preferred_element_type=F32)
            return carry

        lax.fori_loop(0, n_chunks, second_sweep, (gr0, gi0))

    col = lambda width: pl.BlockSpec((t, width), lambda c, j: (0, c))
    wgt = pl.BlockSpec((SCAN_CH, 2 * w), lambda c, j: (c, 0))
    row = pl.BlockSpec((1, 2 * w), lambda c, j: (0, c))
    return _grid_call(
        body, "ssm_bwd", (SCAN_NBLK, 1), [u_perm, dypre, du_skip, a_cat, bbc, cc, ein],
        [col(SCAN_CH), col(SCAN_CH), col(SCAN_CH), row, wgt, wgt, pl.BlockSpec((n_rows, 2 * w), lambda c, j: (0, c))],
        [col(SCAN_CH), row, wgt, wgt],
        [jax.ShapeDtypeStruct((t, SSM_W), F32), jax.ShapeDtypeStruct((1, 2 * N_STATE), F32),
         jax.ShapeDtypeStruct((SSM_W, 2 * w), F32), jax.ShapeDtypeStruct((SSM_W, 2 * w), F32)],
        56 * 1024 * 1024, comm,
        scratch=[pltpu.VMEM((t + n_rows, 2 * w), F32), pltpu.VMEM((rows_c, 2 * w), F32), pltpu.VMEM((rows_c, 2 * w), F32)])


def _to_scan_rows(a, n_samples):
    c = a.shape[1]
    return a.reshape(n_samples, SCAN_SEG_PER_SAMPLE, SCAN_LEN, c).transpose(2, 0, 1, 3).reshape(-1, c)


def _from_scan_rows(a, n_samples):
    c = a.shape[1]
    return a.reshape(SCAN_LEN, n_samples, SCAN_SEG_PER_SAMPLE, c).transpose(1, 2, 0, 3).reshape(-1, c)


def _row_spec(tm, width):
    return pl.BlockSpec((tm, width), lambda i, j: (i, 0))


def _whole(arr):
    return pl.BlockSpec(arr.shape, lambda i, j: (0,) * arr.ndim)


def _proj_rope(x, g, w_in_t, tabs, comm=None):
    t = x.shape[0]
    tm = 256

    def body(x_ref, g_ref, w_ref, tc_ref, tlo_ref, thi_ref, h_ref, u_ref, gate_ref, *rest):
        qkv_refs, stage = rest[:9], rest[9]
        xv = x_ref[...]
        r = lax.rsqrt(jnp.mean(xv * xv, axis=-1, keepdims=True) + RMS_EPS)
        h = ((xv * r) * g_ref[...]).astype(BF16)
        h_ref[...] = h
        p = lax.dot_general(h.astype(MXU_DTYPE), w_ref[...], _NT, preferred_element_type=F32)
        u_ref[...] = p[:, QKV_W:QKV_W + SSM_W]
        gate_ref[...] = _sigmoid(p[:, QKV_W + SSM_W:])
        tc, tlo, thi = tc_ref[...], tlo_ref[...], thi_ref[...]
        n_ch = QKV_W // LANES
        for ch in range(n_ch):
            piece = p[:, _lane_chunk(ch)]
            stage[ch] = _rope_apply(piece, tc, tlo, thi) if ch < 2 * n_ch // 3 else piece
        halves = GROUP_W // LANES
        for grp, d in enumerate(DILATIONS):
            for which in range(3):
                out = qkv_refs[3 * grp + which]
                for res in range(d):
                    for half in range(halves):
                        ch = which * (n_ch // 3) + grp * halves + half
                        out[:, _lane_chunk(res * halves + half)] = _gather_residue(stage, ch, res, d, tm // d).astype(BF16)

    tab = pl.BlockSpec((tm, LANES), lambda i, j: (i % (SEQ // tm), 0))
    widths = [(D_MODEL, BF16), (SSM_W, F32), (2 * D_MODEL, F32)]
    out_specs = [_row_spec(tm, wd) for wd, _ in widths]
    out_shapes = [jax.ShapeDtypeStruct((t, wd), dt) for wd, dt in widths]
    for d in DILATIONS:
        out_specs += [_row_spec(tm // d, d * GROUP_W)] * 3
        out_shapes += [jax.ShapeDtypeStruct((t // d, d * GROUP_W), BF16)] * 3
    return _grid_call(
        body, "proj_rope", (t // tm, 1), [x, g, w_in_t, *tabs],
        [_row_spec(tm, D_MODEL), _whole(g), _whole(w_in_t), tab, tab, tab], out_specs, out_shapes, VMEM_BIG, comm,
        scratch=[pltpu.VMEM((QKV_W // LANES, tm, LANES), F32)])


def _out_rms(merged, w_out, x, g):
    t = x.shape[0]
    tm = 512

    def body(m_ref, w_ref, x_ref, g_ref, x1_ref, h_ref):
        x1 = x_ref[...] + jnp.dot(m_ref[...].astype(MXU_DTYPE), w_ref[...], preferred_element_type=F32)
        x1_ref[...] = x1
        r = lax.rsqrt(jnp.mean(x1 * x1, axis=-1, keepdims=True) + RMS_EPS)
        h_ref[...] = ((x1 * r) * g_ref[...]).astype(BF16)

    return _grid_call(
        body, "out_rms", (t // tm, 1), [merged, w_out, x, g],
        [_row_spec(tm, D_MODEL), _whole(w_out), _row_spec(tm, D_MODEL), _whole(g)],
        [_row_spec(tm, D_MODEL)] * 2, [jax.ShapeDtypeStruct((t, D_MODEL), F32), jax.ShapeDtypeStruct((t, D_MODEL), BF16)],
        VMEM_BIG)


FFN_TN = D_FF // 2
MXU_COLS = 256


def _ffn_in_swiglu(h2, w_gate_t, w_up_t, comm=None):
    t = h2.shape[0]
    tm = 512

    def body(h_ref, wg_ref, wu_ref, a_ref, b_ref, f_ref):
        h = h_ref[...].astype(MXU_DTYPE)
        for c0 in range(0, FFN_TN, MXU_COLS):
            sl = slice(c0, min(c0 + MXU_COLS, FFN_TN))
            a = lax.dot_general(h, wg_ref[sl, :], _NT, preferred_element_type=F32)
            b = lax.dot_general(h, wu_ref[sl, :], _NT, preferred_element_type=F32)
            a_ref[:, sl] = a
            b_ref[:, sl] = b
            f_ref[:, sl] = (a * _sigmoid(a) * b).astype(BF16)

    tile = pl.BlockSpec((tm, FFN_TN), lambda j, i: (i, j))
    wspec = pl.BlockSpec((FFN_TN, D_MODEL), lambda j, i: (j, 0))
    return _grid_call(
        body, "ffn_in_swiglu", (D_FF // FFN_TN, t // tm), [h2, w_gate_t, w_up_t],
        [pl.BlockSpec((tm, D_MODEL), lambda j, i: (i, 0)), wspec, wspec],
        [tile] * 3, [jax.ShapeDtypeStruct((t, D_FF), F32)] * 2 + [jax.ShapeDtypeStruct((t, D_FF), BF16)], VMEM_BIG, comm)


def _ffn_down_final(f, w_down, x1, target, g):
    t = x1.shape[0]
    tm = 256

    def body(f_ref, w_ref, x1_ref, t_ref, g_ref, dx_ref, dxb_ref, loss_ref, gg_ref):
        @pl.when(pl.program_id(0) == 0)
        def _():
            loss_ref[...] = jnp.zeros_like(loss_ref)
            gg_ref[...] = jnp.zeros_like(gg_ref)

        xv = x1_ref[...] + jnp.dot(f_ref[...].astype(MXU_DTYPE), w_ref[...], preferred_element_type=F32)
        gv = g_ref[...]
        r = lax.rsqrt(jnp.mean(xv * xv, axis=-1, keepdims=True) + RMS_EPS)
        n = xv * r
        diff = n * gv - t_ref[...]
        per_tok = jnp.mean(diff * diff, axis=-1, keepdims=True)
        loss_ref[...] += 0.5 * jnp.sum(per_tok, axis=0, keepdims=True)
        dy = diff / xv.shape[-1]
        gg_ref[...] += jnp.sum(dy * n, axis=0, keepdims=True)
        dn = dy * gv
        dx = r * (dn - n * jnp.mean(dn * n, axis=-1, keepdims=True))
        dx_ref[...] = dx
        dxb_ref[...] = dx.astype(BF16)

    acc = lambda shp: pl.BlockSpec(shp, lambda i, j: (0, 0))
    return _grid_call(
        body, "ffn_down_final", (t // tm, 1), [f, w_down, x1, target, g],
        [_row_spec(tm, D_FF), _whole(w_down), _row_spec(tm, D_MODEL), _row_spec(tm, D_MODEL), _whole(g)],
        [_row_spec(tm, D_MODEL)] * 2 + [acc((8, LANES)), acc((1, D_MODEL))],
        [jax.ShapeDtypeStruct((t, D_MODEL), F32), jax.ShapeDtypeStruct((t, D_MODEL), BF16),
         jax.ShapeDtypeStruct((8, LANES), F32), jax.ShapeDtypeStruct((1, D_MODEL), F32)], VMEM_BIG, sequential=True)


def _d_f_swiglu_bwd(dx2b, w_down, a, b):
    t = a.shape[0]
    tm = 512

    def body(dx_ref, w_ref, a_ref, b_ref, da_ref, db_ref):
        d = lax.dot_general(dx_ref[...], w_ref[...], _NT, preferred_element_type=F32)
        av, bv = a_ref[...], b_ref[...]
        sg = _sigmoid(av)
        da_ref[...] = (d * bv * sg * (1.0 + av * (1.0 - sg))).astype(BF16)
        db_ref[...] = (d * av * sg).astype(BF16)

    tile = pl.BlockSpec((tm, FFN_TN), lambda j, i: (i, j))
    return _grid_call(
        body, "d_f_swiglu_bwd", (D_FF // FFN_TN, t // tm), [dx2b, w_down, a, b],
        [pl.BlockSpec((tm, D_MODEL), lambda j, i: (i, 0)), pl.BlockSpec((FFN_TN, D_MODEL), lambda j, i: (j, 0)), tile, tile],
        [tile] * 2, [jax.ShapeDtypeStruct((t, D_FF), BF16)] * 2, VMEM_BIG)


def _mm_rms_bwd(operands, weights, x, g, dres, name, comm=None):
    t = x.shape[0]
    tm = 256
    n_op = len(operands)

    def body(*refs):
        a_refs, w_refs = refs[:n_op], refs[n_op:2 * n_op]
        x_ref, g_ref, dres_ref, dx_ref, dxb_ref, gg_ref = refs[2 * n_op:]

        @pl.when(pl.program_id(0) == 0)
        def _():
            gg_ref[...] = jnp.zeros_like(gg_ref)

        dh = None
        for a_ref, w_ref in zip(a_refs, w_refs):
            part = jnp.dot(a_ref[...].astype(MXU_DTYPE), w_ref[...], preferred_element_type=F32)
            dh = part if dh is None else dh + part
        xv = x_ref[...]
        r = lax.rsqrt(jnp.mean(xv * xv, axis=-1, keepdims=True) + RMS_EPS)
        n = xv * r
        gg_ref[...] += jnp.sum(dh * n, axis=0, keepdims=True)
        dn = dh * g_ref[...]
        dx = dres_ref[...] + r * (dn - n * jnp.mean(dn * n, axis=-1, keepdims=True))
        dx_ref[...] = dx
        dxb_ref[...] = dx.astype(BF16)

    d = x.shape[1]
    return _grid_call(
        body, name, (t // tm, 1), [*operands, *weights, x, g, dres],
        [_row_spec(tm, a.shape[1]) for a in operands] + [_whole(wk) for wk in weights]
        + [_row_spec(tm, d), _whole(g), _row_spec(tm, d)],
        [_row_spec(tm, d)] * 2 + [pl.BlockSpec((1, d), lambda i, j: (0, 0))],
        [jax.ShapeDtypeStruct((t, d), F32), jax.ShapeDtypeStruct((t, d), BF16), jax.ShapeDtypeStruct((1, d), F32)],
        VMEM_BIG, comm, sequential=True)


def _flat_small(small):
    perm_b = lambda a: a.reshape(SSM_GROUPS, SSM_STATE, SSM_CH).transpose(2, 0, 1).reshape(SSM_CH, N_STATE)
    perm_c = lambda a: a.reshape(SSM_GROUPS, SSM_CH, SSM_STATE).transpose(1, 0, 2).reshape(SSM_CH, N_STATE)
    return dict(
        g_mix=small["norm_mix_g"].reshape(1, D_MODEL), g_ffn=small["norm_ffn_g"].reshape(1, D_MODEL),
        g_fin=small["norm_final_g"].reshape(1, D_MODEL),
        lr=small["ssm_a_re"].reshape(1, N_STATE), li=small["ssm_a_im"].reshape(1, N_STATE),
        ldt=jnp.repeat(small["ssm_log_dt"].reshape(SSM_GROUPS), SSM_STATE).reshape(1, N_STATE),
        br=perm_b(small["ssm_b_re"]), bi=perm_b(small["ssm_b_im"]),
        cr=perm_c(small["ssm_c_re"]), ci=perm_c(small["ssm_c_im"]), dskip=small["ssm_d"].reshape(1, SSM_W))


AG_HOSTS = {"proj_rope": ("w_glu", "w_attn_out", "w_out"), "attn_fwd_g0": ("w_ffn_gate",), "attn_fwd_g1": ("w_ffn_up",),
            "ffn_in_swiglu": ("w_ffn_down",)}
HALVED = ("w_ffn_gate", "w_ffn_up", "w_in")
A2A_HOSTS = {"d_h2_rms": ("w_ffn_down",), "attn_bwd_g0": ("w_ffn_gate:0",), "attn_bwd_g1": ("w_ffn_gate:1",),
             "attn_bwd_g2": ("w_ffn_up:0",), "ssm_bwd": ("w_ffn_up:1", "w_out", "w_attn_out", "w_glu"),
             "mm_g_in1": ("w_in:0",), "d_h0_rms": ("w_in:1",)}
SMALL_HOST = "mm_g_in0"


def _local_step(x, target, w, small, shards=None):
    t = x.shape[0]
    n_samples = t // SEQ
    n_rows = n_samples * SCAN_SEG_PER_SAMPLE
    tabs = _rope_tables()
    w = dict(w)
    fs = _flat_small(small)
    g_mix, g_ffn, g_fin, dskip = fs["g_mix"], fs["g_ffn"], fs["g_fin"], fs["dskip"]
    a_cat, bbc, cc = _ssm_disc(fs["lr"], fs["li"], fs["ldt"], fs["br"], fs["bi"], fs["cr"], fs["ci"])
    big, recv, small_pack = {}, {}, []

    def comm_of(name):
        if shards is None:
            return None
        if name == SMALL_HOST:
            return _ag_comm([(small_pack[0], 0, 0)], [(N_DEV, *small_pack[0].shape)])
        if name in AG_HOSTS:
            names = AG_HOSTS[name]
            return _ag_comm([(shards[n], j, 0) for j, n in enumerate(names)], [(N_DEV, *shards[n].shape) for n in names])
        if name in A2A_HOSTS:
            return _a2a_comm([(big[n].reshape(N_DEV, -1, big[n].shape[1]), 0) for n in A2A_HOSTS[name]])
        return None

    def absorb(name, carried):
        if name == SMALL_HOST:
            recv["small"] = carried[0]
        for n, a3 in zip(AG_HOSTS.get(name, ()), carried):
            w[n] = a3.reshape(-1, a3.shape[2])
        for n, a3 in zip(A2A_HOSTS.get(name, ()), carried):
            recv[n] = a3

    def mm(a, b, mode, name, tm, tn, **kw):
        comm = comm_of(name)
        if comm is None:
            return _mm(a, b, mode, name, tm, tn, **kw)
        out, *carried = _mm(a, b, mode, name, tm, tn, comm=comm, **kw)
        absorb(name, carried)
        return out

    h0, u, gates, *rest = _proj_rope(x, g_mix, w["w_in"], tabs, comm_of("proj_rope"))
    qkv = [rest[3 * g:3 * g + 3] for g in range(3)]
    absorb("proj_rope", rest[9:])
    os_, lses = [], []
    for g in range(3):
        o_g, l_g, carried = _attn_fwd(*qkv[g], g, n_samples, comm_of(f"attn_fwd_g{g}"))
        absorb(f"attn_fwd_g{g}", carried)
        os_.append(o_g)
        lses.append(l_g)
    attn, lse_tot = _attn_merge(os_, lses)
    attn_d = mm(attn, w["w_attn_out"], "nt", "mm_attn_out", 512, D_MODEL)

    u_perm = _to_scan_rows(u, n_samples)
    ytot, yg_perm, ein = _ssm_fwd(u_perm, a_cat, bbc, cc, dskip, n_rows)
    yg = _from_scan_rows(yg_perm, n_samples)
    z = mm(yg, w["w_glu"], "nt", "mm_glu", 512, 2 * D_MODEL)

    merged = _mix(attn_d, z, gates)
    x1, h2 = _out_rms(merged, w["w_out"], x, g_ffn)
    ffn_a, ffn_b, f, *carried = _ffn_in_swiglu(h2, w["w_ffn_gate"], w["w_ffn_up"], comm_of("ffn_in_swiglu"))
    absorb("ffn_in_swiglu", carried)
    dx2, dx2b, loss_blk, g_gfin = _ffn_down_final(f, w["w_ffn_down"], x1, target, g_fin)

    da, db = _d_f_swiglu_bwd(dx2b, w["w_ffn_down"], ffn_a, ffn_b)
    big["w_ffn_down"] = mm(f, dx2b, "tn", "mm_g_down", 256, D_MODEL, out_dtype=BF16)
    half = D_MODEL // 2
    for hf in range(2):
        big[f"w_ffn_gate:{hf}"] = mm(da, h2, "tn", f"mm_g_gate{hf}", 256, half, out_dtype=BF16, cols=(hf * half, half))
        big[f"w_ffn_up:{hf}"] = mm(db, h2, "tn", f"mm_g_up{hf}", 256, half, out_dtype=BF16, cols=(hf * half, half))
    dx1, dx1b, g_gffn, *carried = _mm_rms_bwd([da, db], [w["w_ffn_gate"], w["w_ffn_up"]], x1, g_ffn, dx2, "d_h2_rms",
                                              comm_of("d_h2_rms"))
    absorb("d_h2_rms", carried)

    dmerged = mm(dx1b, w["w_out"], "nt", "mm_d_merged", 512, D_MODEL)
    big["w_out"] = mm(merged, dx1b, "tn", "mm_g_out", 256, D_MODEL, out_dtype=BF16)
    dattn_d, dz, dgpre = _mix_bwd(dmerged, gates, attn_d, z)

    dattn = mm(dattn_d, w["w_attn_out"], "nn", "mm_d_attn", 512, GROUP_W)
    big["w_attn_out"] = mm(dattn_d, attn, "tn", "mm_g_attn_out", 512, GROUP_W, out_dtype=BF16)
    cot = _attn_rowdot(dattn, attn, lse_tot)
    dqs, dks, dvs = [], [], []
    for g in range(3):
        dq_g, dk_g, dv_g, carried = _attn_bwd(*qkv[g], *cot[g], g, n_samples, comm_of(f"attn_bwd_g{g}"))
        absorb(f"attn_bwd_g{g}", carried)
        dqs.append(dq_g)
        dks.append(dk_g)
        dvs.append(dv_g)

    dyg = mm(dz, w["w_glu"], "nn", "mm_d_yg", 512, SSM_W)
    big["w_glu"] = mm(dz, yg, "tn", "mm_g_glu", 512, 512, out_dtype=BF16)
    dyg_perm = _to_scan_rows(dyg, n_samples)
    dypre, du_skip, g_dskip = _ssm_act_bwd(dyg_perm, ytot, u_perm, dskip)
    du_perm, da_cat, dbb_full, dc_full, *carried = _ssm_bwd(u_perm, dypre, du_skip, a_cat, bbc, cc, ein, n_rows,
                                                          comm_of("ssm_bwd"))
    absorb("ssm_bwd", carried)
    du = _from_scan_rows(du_perm, n_samples)
    g_lr, g_li, g_ldt, g_br, g_bi, g_cr, g_ci = _ssm_param_bwd(
        fs["lr"], fs["li"], fs["ldt"], fs["br"], fs["bi"], da_cat, dbb_full, dc_full)

    small_pack.append(_pack_small(dict(lr=g_lr, li=g_li, ldt=g_ldt, br=g_br, bi=g_bi, cr=g_cr, ci=g_ci, dskip=g_dskip,
                                       g_ffn=g_gffn, g_fin=g_gfin, loss=loss_blk)))

    dproj = _pack_dproj(dqs, dks, dvs, du, dgpre, tabs)
    for hf in range(2):
        big[f"w_in:{hf}"] = mm(dproj, h0, "tn", f"mm_g_in{hf}", 256, half, out_dtype=BF16, cols=(hf * half, half))
    grad_x, _, g_gmix, *carried = _mm_rms_bwd([dproj], [w["w_in"]], x, g_mix, dx1, "d_h0_rms", comm_of("d_h0_rms"))
    absorb("d_h0_rms", carried)
    return grad_x, (big if shards is None else recv), small_pack[0], g_gmix


_MESH = pl.DeviceIdType.MESH


def _all_gather(block, name):
    rows, lanes = block.shape

    def body(x_ref, out_ref, send_sems, recv_sems, local_sem):
        x, y, c = lax.axis_index("x"), lax.axis_index("y"), lax.axis_index("c")
        me, sibling = (x, y, c), (x, y, 1 - c)
        chips = [(1 - x, y), (x, 1 - y), (1 - x, 1 - y)]

        def slot(px, py, pc):
            return out_ref.at[4 * px + 2 * py + pc]

        def copy(k, blk, to, src=None):
            return pltpu.make_async_remote_copy(
                src_ref=slot(*blk) if src is None else src, dst_ref=slot(*blk), send_sem=send_sems.at[k],
                recv_sem=recv_sems.at[k], device_id=to, device_id_type=_MESH)

        mine = pltpu.make_async_copy(x_ref, slot(*me), local_sem)
        mine.start()
        first = [copy(0, me, sibling, src=x_ref)]
        first += [copy(1 + j, me, (*chip, c), src=x_ref) for j, chip in enumerate(chips)]
        for cp in first:
            cp.start()
        passed = [copy(4 + j, (*chip, c), sibling) for j, chip in enumerate(chips)]
        for j, chip in enumerate(chips):
            copy(1 + j, (*chip, c), me).wait_recv()
            passed[j].start()
        copy(0, sibling, me).wait_recv()
        for j, chip in enumerate(chips):
            copy(4 + j, (*chip, 1 - c), me).wait_recv()
        for cp in first + passed:
            cp.wait_send()
        mine.wait()

    return _pallas_call(
        body, name=name, out_shape=jax.ShapeDtypeStruct((N_DEV, rows, lanes), block.dtype),
        in_specs=[pl.BlockSpec(memory_space=pl.ANY)], out_specs=pl.BlockSpec(memory_space=pl.ANY),
        scratch_shapes=[pltpu.SemaphoreType.DMA((7,)), pltpu.SemaphoreType.DMA((7,)), pltpu.SemaphoreType.DMA],
    )(block)


def _ag_comm(items, bufs):
    def plan(in_refs, out_refs, send_sems, recv_sems, local_sems):
        x, y, c = lax.axis_index("x"), lax.axis_index("y"), lax.axis_index("c")
        me, sibling = (x, y, c), (x, y, 1 - c)
        chips = [(1 - x, y), (x, 1 - y), (1 - x, 1 - y)]
        plans = []
        for t, (_, buf, slot0) in enumerate(items):
            x_ref, out_ref = in_refs[t], out_refs[buf]

            def slot(px, py, pc, out_ref=out_ref, slot0=slot0):
                return out_ref.at[slot0 + 4 * px + 2 * py + pc]

            def copy(k, blk, to, src=None, t=t, slot=slot):
                return pltpu.make_async_remote_copy(
                    src_ref=slot(*blk) if src is None else src, dst_ref=slot(*blk), send_sem=send_sems.at[7 * t + k],
                    recv_sem=recv_sems.at[7 * t + k], device_id=to, device_id_type=_MESH)

            plans.append(dict(
                mine=pltpu.make_async_copy(x_ref, slot(*me), local_sems.at[t]),
                first=[copy(0, me, sibling, src=x_ref)] + [copy(1 + j, me, (*chip, c), src=x_ref)
                                                           for j, chip in enumerate(chips)],
                passed=[copy(4 + j, (*chip, c), sibling) for j, chip in enumerate(chips)],
                from_ici=[copy(1 + j, (*chip, c), me) for j, chip in enumerate(chips)],
                from_sibling=[copy(0, sibling, me)] + [copy(4 + j, (*chip, 1 - c), me) for j, chip in enumerate(chips)]))
        return plans

    def start(*refs):
        for p in plan(*refs):
            p["mine"].start()
            for cp in p["first"]:
                cp.start()

    def finish(*refs):
        plans = plan(*refs)
        for p in plans:
            for arrived, onward in zip(p["from_ici"], p["passed"]):
                arrived.wait_recv()
                onward.start()
        for p in plans:
            for arrived in p["from_sibling"]:
                arrived.wait_recv()
            for cp in p["first"] + p["passed"]:
                cp.wait_send()
            p["mine"].wait()

    dtype_of = {buf: shard.dtype for shard, buf, _ in items}
    out_shapes = [jax.ShapeDtypeStruct(b, dtype_of[j]) for j, b in enumerate(bufs)]
    return _Comm([it[0] for it in items], out_shapes, 7 * len(items), len(items), start, finish)


def _a2a_comm(items):
    def plan(in_refs, out_refs, send_sems, recv_sems, local_sems):
        x, y, c = lax.axis_index("x"), lax.axis_index("y"), lax.axis_index("c")
        my = 4 * x + 2 * y + c
        copies, locals_ = [], []
        for t, (_, slot0) in enumerate(items):
            s_ref, r_ref = in_refs[t], out_refs[t]
            locals_.append(pltpu.make_async_copy(s_ref.at[slot0 + my], r_ref.at[my], local_sems.at[t]))
            for kk in range(1, N_DEV):
                px = 1 - x if kk & 4 else x
                py = 1 - y if kk & 2 else y
                pc = 1 - c if kk & 1 else c
                copies.append(pltpu.make_async_remote_copy(
                    src_ref=s_ref.at[slot0 + 4 * px + 2 * py + pc], dst_ref=r_ref.at[my],
                    send_sem=send_sems.at[7 * t + kk - 1], recv_sem=recv_sems.at[7 * t + kk - 1],
                    device_id=(px, py, pc), device_id_type=_MESH))
        return copies, locals_

    def start(*refs):
        copies, locals_ = plan(*refs)
        for cp in locals_ + copies:
            cp.start()

    def finish(*refs):
        copies, locals_ = plan(*refs)
        for cp in copies + locals_:
            cp.wait()

    out_shapes = [jax.ShapeDtypeStruct((N_DEV,) + it[0].shape[1:], it[0].dtype) for it in items]
    return _Comm([it[0] for it in items], out_shapes, 7 * len(items), len(items), start, finish)


def _adam_math(g, w, m, v):
    m_new = ADAM_B1 * m + (1.0 - ADAM_B1) * g
    v_new = ADAM_B2 * v + (1.0 - ADAM_B2) * jnp.square(g)
    m_hat = m_new / (1.0 - ADAM_B1 ** ADAM_STEP)
    v_hat = v_new / (1.0 - ADAM_B2 ** ADAM_STEP)
    return -ADAM_LR * (m_hat / (jnp.sqrt(v_hat) + ADAM_EPS) + ADAM_WD * w), m_new, v_new


def _sum_partials(parts, name, tm):
    n, rows, _ = parts[0].shape
    widths = [p.shape[2] for p in parts]

    def body(*refs):
        g_ref, off = refs[-1], 0
        for p_ref, wd in zip(refs[:-1], widths):
            g = p_ref[0].astype(F32)
            for s in range(1, n):
                g = g + p_ref[s].astype(F32)
            g_ref[:, off:off + wd] = g
            off += wd

    return _pallas_call(
        body, name=name, grid=(rows // tm,), in_specs=[pl.BlockSpec((n, tm, wd), lambda i: (0, i, 0)) for wd in widths],
        out_specs=pl.BlockSpec((tm, sum(widths)), lambda i: (i, 0)),
        out_shape=jax.ShapeDtypeStruct((rows, sum(widths)), F32),
        compiler_params=pltpu.CompilerParams(dimension_semantics=("parallel",), vmem_limit_bytes=VMEM_MID),
    )(*parts)


def _adam(partials, w, m, v, name, tm):
    n, rows, cols = partials.shape

    def body(p_ref, w_ref, m_ref, v_ref, g_ref, d_ref, nm_ref, nv_ref):
        g = p_ref[0].astype(F32)
        for s in range(1, n):
            g = g + p_ref[s].astype(F32)
        g_ref[...] = g
        d_ref[...], nm_ref[...], nv_ref[...] = _adam_math(g, w_ref[...], m_ref[...], v_ref[...])

    assert rows % tm == 0
    row = pl.BlockSpec((tm, cols), lambda i: (i, 0))
    shp = jax.ShapeDtypeStruct((rows, cols), F32)
    return _pallas_call(
        body, name=name, grid=(rows // tm,),
        in_specs=[pl.BlockSpec((n, tm, cols), lambda i: (0, i, 0)), row, row, row],
        out_specs=[row] * 4, out_shape=[shp] * 4,
        compiler_params=pltpu.CompilerParams(dimension_semantics=("parallel",), vmem_limit_bytes=VMEM_MID),
    )(partials, w, m, v)


_PK_LR, _PK_LI, _PK_GAINS, _PK_MISC, _PK_BR, _PK_BI, _PK_CR, _PK_CI, _PK_ROWS = 0, 1, 2, 3, 8, 24, 40, 56, 72
_PK_LDT_LANE, _PK_LOSS_LANE = D_MODEL + SSM_W, D_MODEL + SSM_W + LANES


def _pack_small(sg):
    names = ("lr", "li", "g_ffn", "g_fin", "dskip", "ldt", "loss", "br", "bi", "cr", "ci")

    def body(lr, li, gffn, gfin, dskip, ldt, loss, br, bi, cr, ci, o_ref):
        o_ref[...] = jnp.zeros_like(o_ref)
        o_ref[_PK_LR:_PK_LR + 1, :] = lr[...]
        o_ref[_PK_LI:_PK_LI + 1, :] = li[...]
        o_ref[_PK_GAINS:_PK_GAINS + 1, D_MODEL:] = gffn[...]
        o_ref[_PK_MISC:_PK_MISC + 1, :D_MODEL] = gfin[...]
        o_ref[_PK_MISC:_PK_MISC + 1, D_MODEL:D_MODEL + SSM_W] = dskip[...]
        o_ref[_PK_MISC:_PK_MISC + 1, _PK_LDT_LANE:_PK_LDT_LANE + LANES] = ldt[0:1, :]
        o_ref[_PK_MISC:_PK_MISC + 1, _PK_LOSS_LANE:_PK_LOSS_LANE + LANES] = loss[0:1, :]
        o_ref[_PK_BR:_PK_BR + SSM_CH, :] = br[...]
        o_ref[_PK_BI:_PK_BI + SSM_CH, :] = bi[...]
        o_ref[_PK_CR:_PK_CR + SSM_CH, :] = cr[...]
        o_ref[_PK_CI:_PK_CI + SSM_CH, :] = ci[...]

    return _pallas_call(body, name="pack_small", out_shape=jax.ShapeDtypeStruct((_PK_ROWS, N_STATE), F32))(
        *[sg[n] for n in names])


def _unpack_small(s, g_mix):
    unflat_b = lambda a: a.reshape(SSM_CH, SSM_GROUPS, SSM_STATE).transpose(1, 2, 0)[None]
    unflat_c = lambda a: a.reshape(SSM_CH, SSM_GROUPS, SSM_STATE).transpose(1, 0, 2)[None]
    grads = {
        "norm_mix_g": g_mix, "norm_ffn_g": s[_PK_GAINS, D_MODEL:].reshape(1, D_MODEL),
        "norm_final_g": s[_PK_MISC, :D_MODEL],
        "ssm_a_re": s[_PK_LR].reshape(1, SSM_GROUPS, SSM_STATE), "ssm_a_im": s[_PK_LI].reshape(1, SSM_GROUPS, SSM_STATE),
        "ssm_log_dt": s[_PK_MISC, _PK_LDT_LANE:_PK_LDT_LANE + SSM_GROUPS].reshape(1, SSM_GROUPS),
        "ssm_d": s[_PK_MISC, D_MODEL:D_MODEL + SSM_W].reshape(1, SSM_GROUPS, SSM_CH),
        "ssm_b_re": unflat_b(s[_PK_BR:_PK_BR + SSM_CH]), "ssm_b_im": unflat_b(s[_PK_BI:_PK_BI + SSM_CH]),
        "ssm_c_re": unflat_c(s[_PK_CR:_PK_CR + SSM_CH]), "ssm_c_im": unflat_c(s[_PK_CI:_PK_CI + SSM_CH]),
    }
    return s[_PK_MISC, _PK_LOSS_LANE], grads


def _adam_small(grads, wts, moms, vars_):
    n = len(SMALL_WEIGHTS)
    as2d = lambda a: a.reshape(1, -1) if a.ndim == 1 else a

    def body(*refs):
        ins, outs = refs[:4 * n], refs[4 * n:]
        for i in range(n):
            g, w, m, v = (ins[j * n + i][...] for j in range(4))
            outs[i][...], outs[n + i][...], outs[2 * n + i][...] = _adam_math(g, w, m, v)

    operands = [as2d(d[k]) for d in (grads, wts, moms, vars_) for k in SMALL_WEIGHTS]
    shapes = [jax.ShapeDtypeStruct(as2d(wts[k]).shape, F32) for k in SMALL_WEIGHTS] * 3
    res = _pallas_call(body, name="adam_small", out_shape=shapes,
                         compiler_params=pltpu.CompilerParams(vmem_limit_bytes=VMEM_BIG))(*operands)
    out = {}
    for j, kind in enumerate(("delta", "new_m", "new_v")):
        for i, k in enumerate(SMALL_WEIGHTS):
            out[kind, k] = res[j * n + i].reshape(wts[k].shape)
    return out


def kernel(x, norm_mix_g, w_in, ssm_a_re, ssm_a_im, ssm_log_dt, ssm_b_re, ssm_b_im, ssm_c_re, ssm_c_im, ssm_d, w_glu, w_attn_out, w_out, norm_ffn_g, w_ffn_gate, w_ffn_up, w_ffn_down, norm_final_g, loss_target, m_norm_mix_g, m_w_in, m_ssm_a_re, m_ssm_a_im, m_ssm_log_dt, m_ssm_b_re, m_ssm_b_im, m_ssm_c_re, m_ssm_c_im, m_ssm_d, m_w_glu, m_w_attn_out, m_w_out, m_norm_ffn_g, m_w_ffn_gate, m_w_ffn_up, m_w_ffn_down, m_norm_final_g, v_norm_mix_g, v_w_in, v_ssm_a_re, v_ssm_a_im, v_ssm_log_dt, v_ssm_b_re, v_ssm_b_im, v_ssm_c_re, v_ssm_c_im, v_ssm_d, v_w_glu, v_w_attn_out, v_w_out, v_norm_ffn_g, v_w_ffn_gate, v_w_ffn_up, v_w_ffn_down, v_norm_final_g):
    args = dict(locals())
    wts = {n: args[n] for n in ALL_WEIGHTS}
    moms = {n: args["m_" + n] for n in ALL_WEIGHTS}
    vars_ = {n: args["v_" + n] for n in ALL_WEIGHTS}
    n_samples = x.shape[0]
    t = n_samples * SEQ

    shards = {n: (wts[n][0] if n in ROW_SHARDED else wts[n][0].T).astype(BF16) for n in BIG_WEIGHTS}
    w_in_t = _all_gather(shards["w_in"], "allgather_w_in").reshape(IN_W, D_MODEL)

    small = {n: wts[n] for n in SMALL_WEIGHTS}
    grad_x, recv, _, g_mix_part = _local_step(x.reshape(t, D_MODEL), loss_target.reshape(t, D_MODEL), {"w_in": w_in_t},
                                              small, shards)

    results = {}
    for n in BIG_WEIGHTS:
        c, k = shards[n].shape
        w2, m2, v2 = wts[n][0], moms[n][0], vars_[n][0]
        if n in ROW_SHARDED:
            res = _adam(recv[n], w2, m2, v2, "adam_" + n, c // 2)
        else:
            parts = [recv[f"{n}:{hf}"] for hf in range(2)] if n in HALVED else [recv[n]]
            g_t = _sum_partials(parts, "sum_" + n, c // 2)
            res = _adam(g_t.T[None], w2, m2, v2, "adam_" + n, k // 2)
        for kind, a in zip(("grad", "delta", "new_m", "new_v"), res):
            results[kind, n] = a[None]

    g_mix_all = _all_gather(jnp.pad(g_mix_part, ((0, 7), (0, 0))), "allgather_g_mix")
    g_mix = _sum_partials([g_mix_all], "sum_g_mix", 8)[0:1]
    loss, sgrads = _unpack_small(_sum_partials([recv["small"]], "sum_small", _PK_ROWS), g_mix)
    for n in SMALL_WEIGHTS:
        results["grad", n] = sgrads[n]
    results.update(_adam_small(sgrads, wts, moms, vars_))
    outs = [loss, grad_x.reshape(x.shape)]
    for kind in ("grad", "delta", "new_m", "new_v"):
        outs += [results[kind, n] for n in ALL_WEIGHTS]
    return tuple(outs)
```

```python
import functools
import math

import jax
import jax.numpy as jnp
from jax import lax
from jax.experimental import pallas as pl
from jax.experimental.pallas import tpu as pltpu

F32 = jnp.float32
BF16 = jnp.bfloat16
MXU_DTYPE = jnp.bfloat16

N_DEV = 8
D_MODEL = 1024
SEQ = 2048
HEAD_DIM = 64
HEADS_PER_GROUP = 4
GROUP_W = HEADS_PER_GROUP * HEAD_DIM
DILATIONS = (1, 4, 16)
QKV_W = 3 * len(DILATIONS) * GROUP_W
Q_W = len(DILATIONS) * GROUP_W
ATT_BLOCK = 128
ROPE_DIM = 16
ROPE_THETA = 500000.0
SSM_W = 512
SSM_GROUPS = 32
SSM_CH = 16
SSM_STATE = 64
N_STATE = SSM_GROUPS * SSM_STATE
D_FF = 2816
IN_W = QKV_W + SSM_W + 2 * D_MODEL
RMS_EPS = 1e-6
NEG_INF = -1e30
LANES = 128

SCAN_SEG_PER_SAMPLE = 8
SCAN_LEN = SEQ // SCAN_SEG_PER_SAMPLE
SCAN_WC = 512
SCAN_NBLK = N_STATE // SCAN_WC
SCAN_CH = SSM_W // SCAN_NBLK
SCAN_CHUNK = 32

ADAM_LR = 0.001
ADAM_B1 = 0.9
ADAM_B2 = 0.999
ADAM_EPS = 1e-08
ADAM_WD = 0.01
ADAM_STEP = 10

VMEM_BIG = 48 * 1024 * 1024
VMEM_MID = 32 * 1024 * 1024

BIG_WEIGHTS = ("w_in", "w_glu", "w_attn_out", "w_out", "w_ffn_gate", "w_ffn_up", "w_ffn_down")
ROW_SHARDED = ("w_out", "w_ffn_down")
SMALL_WEIGHTS = ("norm_mix_g", "ssm_a_re", "ssm_a_im", "ssm_log_dt", "ssm_b_re", "ssm_b_im", "ssm_c_re", "ssm_c_im",
                 "ssm_d", "norm_ffn_g", "norm_final_g")
ALL_WEIGHTS = ("norm_mix_g", "w_in", "ssm_a_re", "ssm_a_im", "ssm_log_dt", "ssm_b_re", "ssm_b_im", "ssm_c_re", "ssm_c_im",
               "ssm_d", "w_glu", "w_attn_out", "w_out", "norm_ffn_g", "w_ffn_gate", "w_ffn_up", "w_ffn_down", "norm_final_g")


def _sigmoid(x):
    return 1.0 / (1.0 + jnp.exp(-x))


def _pallas_call(body, *, out_shape, **kw):
    single = not isinstance(out_shape, (list, tuple))
    shapes = [pltpu.HBM(s.shape, s.dtype) for s in ([out_shape] if single else out_shape)]
    call = pl.pallas_call(body, out_shape=shapes[0] if single else shapes, **kw)
    return lambda *operands: call(*[pltpu.with_memory_space_constraint(o, pltpu.HBM) for o in operands])


class _Comm:
    def __init__(self, ins, out_shapes, n_sem, n_local, start, finish):
        self.ins, self.out_shapes, self.n_sem, self.n_local = ins, out_shapes, n_sem, n_local
        self.start, self.finish = start, finish


def _mm(a, b, mode, name, tm, tn, out_dtype=F32, add=None, vmem=VMEM_BIG, comm=None, cols=None):
    if mode == "nn":
        (m, k), (_, n) = a.shape, b.shape
        a_spec = pl.BlockSpec((tm, k), lambda i, j: (i, 0))
        b_spec = pl.BlockSpec((k, tn), lambda i, j: (0, j))
        dims = (((1,), (0,)), ((), ()))
    elif mode == "nt":
        (m, k), (n, _) = a.shape, b.shape
        a_spec = pl.BlockSpec((tm, k), lambda i, j: (i, 0))
        b_spec = pl.BlockSpec((tn, k), lambda i, j: (j, 0))
        dims = (((1,), (1,)), ((), ()))
    else:
        (k, m), (_, n) = a.shape, b.shape
        first, n = cols if cols else (0, n)
        a_spec = pl.BlockSpec((k, tm), lambda i, j: (0, i))
        b_spec = pl.BlockSpec((k, tn), lambda i, j: (0, j + first // tn))
        dims = (((0,), (0,)), ((), ()))
    assert m % tm == 0 and n % tn == 0, (name, m, n, tm, tn)
    o_spec = pl.BlockSpec((tm, tn), lambda i, j: (i, j))
    has_add = add is not None

    def body(*refs):
        a_ref, b_ref, o_ref = refs[0], refs[1], refs[-1]
        acc = lax.dot_general(a_ref[...].astype(MXU_DTYPE), b_ref[...].astype(MXU_DTYPE), dims,
                              preferred_element_type=F32)
        if has_add:
            acc = acc + refs[2][...]
        o_ref[...] = acc.astype(out_dtype)

    ins = [a, b] + ([add] if has_add else [])
    in_specs = [a_spec, b_spec] + ([o_spec] if has_add else [])
    return _grid_call(body, name, (m // tm, n // tn), ins, in_specs, [o_spec],
                      [jax.ShapeDtypeStruct((m, n), out_dtype)], vmem, comm)


def _grid_call(body, name, grid, ins, in_specs, out_specs, out_shapes, vmem, comm=None, sequential=False, scratch=()):
    if comm is None:
        single = len(out_shapes) == 1
        semantics = ("arbitrary", "arbitrary") if sequential else ("parallel", "parallel")
        return _pallas_call(
            body, name=name, grid=grid, in_specs=in_specs, out_specs=out_specs[0] if single else out_specs,
            out_shape=out_shapes[0] if single else out_shapes, scratch_shapes=list(scratch),
            compiler_params=pltpu.CompilerParams(dimension_semantics=semantics, vmem_limit_bytes=vmem),
        )(*ins)
    n_in, n_out, n_cin, n_cout = len(ins), len(out_shapes), len(comm.ins), len(comm.out_shapes)
    n_io = n_in + n_cin + n_out + n_cout

    def carrying(*refs):
        own = refs[:n_in] + refs[n_in + n_cin:n_in + n_cin + n_out] + refs[n_io:len(refs) - 3]
        c_args = (refs[n_in:n_in + n_cin], refs[n_in + n_cin + n_out:n_io], *refs[-3:])

        @pl.when((pl.program_id(0) == 0) & (pl.program_id(1) == 0))
        def _():
            comm.start(*c_args)

        body(*own)

        @pl.when((pl.program_id(0) == grid[0] - 1) & (pl.program_id(1) == grid[1] - 1))
        def _():
            comm.finish(*c_args)

    hbm = pl.BlockSpec(memory_space=pl.ANY)
    return _pallas_call(
        carrying, name=name, grid=grid, in_specs=list(in_specs) + [hbm] * n_cin,
        out_specs=list(out_specs) + [hbm] * n_cout, out_shape=list(out_shapes) + list(comm.out_shapes),
        scratch_shapes=list(scratch) + [pltpu.SemaphoreType.DMA((comm.n_sem,)), pltpu.SemaphoreType.DMA((comm.n_sem,)),
                                        pltpu.SemaphoreType.DMA((comm.n_local,))],
        compiler_params=pltpu.CompilerParams(dimension_semantics=("arbitrary", "arbitrary"), vmem_limit_bytes=vmem),
    )(*ins, *comm.ins)


def _rows(body, name, n_rows, tm, ins, outs, vmem=VMEM_MID, scratch=()):
    assert n_rows % tm == 0
    arrays, in_specs = [], []
    for kind, arr in ins:
        arrays.append(arr)
        if kind == "row":
            assert n_rows % arr.shape[0] == 0, (name, arr.shape)
            in_specs.append(pl.BlockSpec((tm * arr.shape[0] // n_rows, arr.shape[1]), lambda i: (i, 0)))
        elif kind == "tab":
            nblk = arr.shape[0] // tm
            in_specs.append(pl.BlockSpec((tm, arr.shape[1]), lambda i, nblk=nblk: (i % nblk, 0)))
        else:
            in_specs.append(pl.BlockSpec(arr.shape, lambda i, nd=arr.ndim: (0,) * nd))
    out_specs, out_shape = [], []
    for kind, shp, dt in outs:
        if kind == "row":
            out_specs.append(pl.BlockSpec((tm, shp), lambda i: (i, 0)))
            out_shape.append(jax.ShapeDtypeStruct((n_rows, shp), dt))
        elif kind == "dil":
            d, wd = shp
            out_specs.append(pl.BlockSpec((tm // d, d * wd), lambda i: (i, 0)))
            out_shape.append(jax.ShapeDtypeStruct((n_rows // d, d * wd), dt))
        else:
            out_specs.append(pl.BlockSpec(shp, lambda i, nd=len(shp): (0,) * nd))
            out_shape.append(jax.ShapeDtypeStruct(shp, dt))
    res = _pallas_call(
        body, name=name, grid=(n_rows // tm,), in_specs=in_specs, out_specs=out_specs, out_shape=out_shape,
        scratch_shapes=list(scratch),
        compiler_params=pltpu.CompilerParams(dimension_semantics=("arbitrary",), vmem_limit_bytes=vmem),
    )(*arrays)
    return res


def _gather_residue(stage, ch, r, d, n):
    return stage[ch, pl.ds(r, n, stride=d), :] if d > 1 else stage[ch]


def _scatter_residue(stage, ch, r, d, n, val):
    if d > 1:
        stage[ch, pl.ds(r, n, stride=d), :] = val
    else:
        stage[ch] = val


def _lane_chunk(ch):
    return slice(ch * LANES, (ch + 1) * LANES)


def _first_step():
    return pl.program_id(0) == 0


def _rope_tables():
    half = ROPE_DIM // 2
    inv = jnp.power(jnp.float32(ROPE_THETA), -jnp.arange(half, dtype=F32) * 2.0 / ROPE_DIM)
    ang = jnp.arange(SEQ, dtype=F32)[:, None] * inv[None, :]
    lane = jnp.arange(LANES) % HEAD_DIM
    cosl = jnp.cos(ang)[:, lane % half]
    sinl = jnp.sin(ang)[:, lane % half]
    tab_c = jnp.where(lane < ROPE_DIM, cosl, 1.0)
    tab_lo = jnp.where(lane < half, -sinl, 0.0)
    tab_hi = jnp.where((lane >= half) & (lane < ROPE_DIM), sinl, 0.0)
    return tab_c.astype(F32), tab_lo.astype(F32), tab_hi.astype(F32)


def _rope_apply(t, tc, tlo, thi):
    half = ROPE_DIM // 2
    return t * tc + pltpu.roll(t, LANES - half, 1) * tlo + pltpu.roll(t, half, 1) * thi


def _rope_transpose(dt, tc, tlo, thi):
    half = ROPE_DIM // 2
    return dt * tc + pltpu.roll(dt * tlo, half, 1) + pltpu.roll(dt * thi, LANES - half, 1)


def _pack_dproj(dqs, dks, dvs, du, dgpre, tabs):
    tm = 256

    def body(*refs):
        dq_refs, dk_refs, dv_refs = refs[0:3], refs[3:6], refs[6:9]
        du_ref, dg_ref, tc_ref, tlo_ref, thi_ref, o_ref, stage = refs[9:16]
        n_ch = QKV_W // LANES
        halves = GROUP_W // LANES
        for grp, d in enumerate(DILATIONS):
            for which, src in enumerate((dq_refs[grp], dk_refs[grp], dv_refs[grp])):
                for res in range(d):
                    for half in range(halves):
                        _scatter_residue(stage, which * (n_ch // 3) + grp * halves + half, res, d, tm // d,
                                         src[:, _lane_chunk(res * halves + half)])
        tc, tlo, thi = tc_ref[...], tlo_ref[...], thi_ref[...]
        for ch in range(n_ch):
            piece = stage[ch]
            o_ref[:, _lane_chunk(ch)] = (_rope_transpose(piece, tc, tlo, thi) if ch < 2 * n_ch // 3 else piece).astype(BF16)
        o_ref[:, QKV_W:QKV_W + SSM_W] = du_ref[...].astype(BF16)
        o_ref[:, QKV_W + SSM_W:] = dg_ref[...].astype(BF16)

    t = du.shape[0]
    ins = [("row", a) for a in (*dqs, *dks, *dvs, du, dgpre)] + [("tab", tb) for tb in tabs]
    return _rows(body, "pack_dproj", t, tm, ins, [("row", IN_W, BF16)],
                 scratch=[pltpu.VMEM((QKV_W // LANES, tm, LANES), F32)])[0]


def _attn_merge(os_, lses):
    tm = 256

    halves = GROUP_W // LANES

    def body(o0, o1, o2, l0, l1, l2, a_ref, lt_ref, nat):
        for grp, d in enumerate(DILATIONS[1:], start=1):
            for j, src in enumerate(((o0, o1, o2)[grp], (l0, l1, l2)[grp])):
                for res in range(d):
                    for half in range(halves):
                        _scatter_residue(nat, (grp - 1) * 4 + j * 2 + half, res, d, tm // d,
                                         src[:, _lane_chunk(res * halves + half)])
        for half in range(halves):
            sl = _lane_chunk(half)
            la, lb, lc = l0[:, sl], nat[2 + half], nat[6 + half]
            m = jnp.maximum(jnp.maximum(la, lb), lc)
            ea, eb, ec = jnp.exp(la - m), jnp.exp(lb - m), jnp.exp(lc - m)
            ssum = ea + eb + ec
            a_ref[:, sl] = (ea / ssum) * o0[:, sl] + (eb / ssum) * nat[half] + (ec / ssum) * nat[4 + half]
            lt_ref[:, sl] = m + jnp.log(ssum)

    t = os_[0].shape[0]
    return _rows(body, "attn_merge", t, tm, [("row", a) for a in (*os_, *lses)],
                 [("row", GROUP_W, F32), ("row", GROUP_W, F32)], scratch=[pltpu.VMEM((8, tm, LANES), F32)])


def _head_sum_matrix():
    r = jnp.arange(GROUP_W) // HEAD_DIM
    return (r[:, None] == r[None, :]).astype(F32)


def _attn_rowdot(dattn, attn, lse_tot):
    tm = 256

    halves = GROUP_W // LANES

    def body(da_ref, a_ref, lt_ref, ones_ref, rd_ref, *rest):
        dil, stage = rest[:6], rest[6]
        rd = jnp.dot(da_ref[...] * a_ref[...], ones_ref[...], preferred_element_type=F32, precision=lax.Precision.HIGHEST)
        rd_ref[...] = rd
        for half in range(halves):
            stage[half] = da_ref[:, _lane_chunk(half)]
            stage[2 + half] = lt_ref[:, _lane_chunk(half)]
            stage[4 + half] = rd[:, _lane_chunk(half)]
        for grp, d in enumerate(DILATIONS[1:], start=1):
            for j in range(3):
                for res in range(d):
                    for half in range(halves):
                        dil[3 * (grp - 1) + j][:, _lane_chunk(res * halves + half)] = _gather_residue(
                            stage, 2 * j + half, res, d, tm // d)

    t = attn.shape[0]
    outs = [("row", GROUP_W, F32)] + [("dil", (d, GROUP_W), F32) for d in DILATIONS[1:] for _ in range(3)]
    rd, *dil = _rows(body, "attn_rowdot", t, tm,
                     [("row", dattn), ("row", attn), ("row", lse_tot), ("const", _head_sum_matrix())], outs,
                     scratch=[pltpu.VMEM((6, tm, LANES), F32)])
    return [(dattn, lse_tot, rd), tuple(dil[:3]), tuple(dil[3:])]


def _mix(attn_d, z, gates):
    def body(ad_ref, z_ref, g_ref, m_ref):
        za, zb = z_ref[:, :D_MODEL], z_ref[:, D_MODEL:]
        s_out = za * _sigmoid(zb)
        m_ref[...] = (g_ref[:, :D_MODEL] * ad_ref[...] + g_ref[:, D_MODEL:] * s_out).astype(BF16)

    t = attn_d.shape[0]
    return _rows(body, "mix", t, 256, [("row", attn_d), ("row", z), ("row", gates)], [("row", D_MODEL, BF16)])[0]


def _mix_bwd(dmerged, gates, attn_d, z):
    def body(dm_ref, g_ref, ad_ref, z_ref, dad_ref, dz_ref, dg_ref):
        dm = dm_ref[...]
        g0, g1 = g_ref[:, :D_MODEL], g_ref[:, D_MODEL:]
        za, zb = z_ref[:, :D_MODEL], z_ref[:, D_MODEL:]
        sb = _sigmoid(zb)
        s_out = za * sb
        dad_ref[...] = (dm * g0).astype(BF16)
        ds = dm * g1
        dz_ref[:, :D_MODEL] = (ds * sb).astype(BF16)
        dz_ref[:, D_MODEL:] = (ds * za * sb * (1.0 - sb)).astype(BF16)
        dg_ref[:, :D_MODEL] = (dm * ad_ref[...] * g0 * (1.0 - g0)).astype(BF16)
        dg_ref[:, D_MODEL:] = (dm * s_out * g1 * (1.0 - g1)).astype(BF16)

    t = dmerged.shape[0]
    return _rows(body, "mix_bwd", t, 256, [("row", dmerged), ("row", gates), ("row", attn_d), ("row", z)],
                 [("row", D_MODEL, BF16), ("row", 2 * D_MODEL, BF16), ("row", 2 * D_MODEL, BF16)])


_GELU_C = math.sqrt(2.0 / math.pi)


def _ssm_act_bwd(dyg, ytot, u_perm, dskip):
    def body(dyg_ref, yt_ref, u_ref, d_ref, dy_ref, dus_ref, dd_ref):
        @pl.when(_first_step())
        def _():
            dd_ref[...] = jnp.zeros_like(dd_ref)

        yt = yt_ref[...]
        th = jnp.tanh(_GELU_C * (yt + 0.044715 * (yt * yt * yt)))
        dgelu = 0.5 * (1.0 + th) + 0.5 * yt * (1.0 - th * th) * _GELU_C * (1.0 + 3.0 * 0.044715 * yt * yt)
        dy = dyg_ref[...] * dgelu
        dy_ref[...] = dy.astype(BF16)
        dus_ref[...] = dy * d_ref[...]
        dd_ref[...] += jnp.sum(dy * u_ref[...], axis=0, keepdims=True)

    t = dyg.shape[0]
    return _rows(body, "ssm_act_bwd", t, 512, [("row", dyg), ("row", ytot), ("row", u_perm), ("const", dskip)],
                 [("row", SSM_W, BF16), ("row", SSM_W, F32), ("acc", (1, SSM_W), F32)])


def _head_masks():
    lane = lax.broadcasted_iota(jnp.int32, (1, GROUP_W), 1)
    return [(lane // HEAD_DIM) == h for h in range(HEADS_PER_GROUP)]


def _stack_heads(blk, masks, fill=0.0):
    return jnp.concatenate([jnp.where(mk, blk, jnp.full_like(blk, fill)) for mk in masks], axis=0)


def _unstack_heads(stacked, masks):
    rows = stacked.shape[0] // len(masks)
    out = stacked[:rows]
    for h in range(1, len(masks)):
        out = jnp.where(masks[h], stacked[h * rows:(h + 1) * rows], out)
    return out


def _band_mask(first):
    nk = ATT_BLOCK if first else 2 * ATT_BLOCK
    qi = lax.broadcasted_iota(jnp.int32, (ATT_BLOCK, nk), 0)
    ki = lax.broadcasted_iota(jnp.int32, (ATT_BLOCK, nk), 1)
    dist = qi - ki + (0 if first else ATT_BLOCK)
    return (dist >= 0) & (dist <= ATT_BLOCK)


_NT = (((1,), (1,)), ((), ()))
_TN = (((0,), (0,)), ((), ()))


def _attn_fwd(q, k, v, group, n_samples, comm=None):
    d = DILATIONS[group]
    length = SEQ // d
    nb = length // ATT_BLOCK

    def body(q_ref, k_ref, v_ref, o_ref, l_ref):
        masks = _head_masks()

        def block(qs, ks, first):
            nk = ATT_BLOCK if first else 2 * ATT_BLOCK
            qb = q_ref[0, pl.ds(qs, ATT_BLOCK), :]
            kc = k_ref[0, pl.ds(ks, nk), :]
            vc = v_ref[0, pl.ds(ks, nk), :]
            q4 = _stack_heads(qb, masks)
            valid = jnp.tile(_band_mask(first), (HEADS_PER_GROUP, 1))
            s = lax.dot_general(q4, kc, _NT, preferred_element_type=F32) * (HEAD_DIM ** -0.5)
            s = jnp.where(valid, s, NEG_INF)
            m = jnp.max(s, axis=-1, keepdims=True)
            p = jnp.exp(s - m)
            l = jnp.sum(p, axis=-1, keepdims=True)
            o4 = jnp.dot(p.astype(MXU_DTYPE), vc, preferred_element_type=F32) / l
            lse4 = jnp.broadcast_to(m + jnp.log(l), o4.shape)
            o_ref[0, pl.ds(qs, ATT_BLOCK), :] = _unstack_heads(o4, masks)
            l_ref[0, pl.ds(qs, ATT_BLOCK), :] = _unstack_heads(lse4, masks)

        block(0, 0, True)
        if nb > 1:
            def loop(n, carry):
                block(pl.multiple_of(n * ATT_BLOCK, ATT_BLOCK), pl.multiple_of((n - 1) * ATT_BLOCK, ATT_BLOCK), False)
                return carry

            lax.fori_loop(1, nb, loop, 0)

    per_sample = lambda a: a.reshape(n_samples, length, d * GROUP_W)
    spec = pl.BlockSpec((1, length, GROUP_W), lambda b, r: (b, 0, r))
    shp = jax.ShapeDtypeStruct((n_samples, length, d * GROUP_W), F32)
    o, lse, *carried = _grid_call(body, f"attn_fwd_g{group}", (n_samples, d), [per_sample(a) for a in (q, k, v)],
                                  [spec] * 3, [spec] * 2, [shp, shp], VMEM_MID, comm)
    flat = lambda a: a.reshape(n_samples * length, d * GROUP_W)
    return flat(o), flat(lse), carried


def _attn_bwd(q, k, v, dattn, lse_tot, rowdot, group, n_samples, comm=None):
    d = DILATIONS[group]
    length = SEQ // d
    nb = length // ATT_BLOCK

    def body(q_ref, k_ref, v_ref, da_ref, lt_ref, rd_ref, dq_ref, dk_ref, dv_ref):
        masks = _head_masks()
        dk_ref[...] = jnp.zeros_like(dk_ref)
        dv_ref[...] = jnp.zeros_like(dv_ref)

        def block(qs, ks, first):
            nk = ATT_BLOCK if first else 2 * ATT_BLOCK
            qb = q_ref[0, pl.ds(qs, ATT_BLOCK), :]
            kc = k_ref[0, pl.ds(ks, nk), :]
            vc = v_ref[0, pl.ds(ks, nk), :]
            da = da_ref[0, pl.ds(qs, ATT_BLOCK), :]
            lt = lt_ref[0, pl.ds(qs, ATT_BLOCK), :]
            rd = rd_ref[0, pl.ds(qs, ATT_BLOCK), :]
            q4 = _stack_heads(qb, masks)
            da4 = _stack_heads(da, masks).astype(MXU_DTYPE)
            lt4 = jnp.max(_stack_heads(lt, masks, -jnp.inf), axis=-1, keepdims=True)
            rd4 = jnp.max(_stack_heads(rd, masks, -jnp.inf), axis=-1, keepdims=True)
            valid = jnp.tile(_band_mask(first), (HEADS_PER_GROUP, 1))
            s = lax.dot_general(q4, kc, _NT, preferred_element_type=F32) * (HEAD_DIM ** -0.5)
            s = jnp.where(valid, s, NEG_INF)
            p = jnp.exp(s - lt4)
            dp = lax.dot_general(da4, vc, _NT, preferred_element_type=F32)
            ds = (p * (dp - rd4) * (HEAD_DIM ** -0.5)).astype(MXU_DTYPE)
            dq_ref[0, pl.ds(qs, ATT_BLOCK), :] = _unstack_heads(jnp.dot(ds, kc, preferred_element_type=F32), masks)
            dk_ref[0, pl.ds(ks, nk), :] += lax.dot_general(ds, q4, _TN, preferred_element_type=F32)
            dv_ref[0, pl.ds(ks, nk), :] += lax.dot_general(p.astype(MXU_DTYPE), da4, _TN, preferred_element_type=F32)

        block(0, 0, True)
        if nb > 1:
            def loop(n, carry):
                block(pl.multiple_of(n * ATT_BLOCK, ATT_BLOCK), pl.multiple_of((n - 1) * ATT_BLOCK, ATT_BLOCK), False)
                return carry

            lax.fori_loop(1, nb, loop, 0)

    per_sample = lambda a: a.reshape(n_samples, length, d * GROUP_W)
    spec = pl.BlockSpec((1, length, GROUP_W), lambda b, r: (b, 0, r))
    shp = jax.ShapeDtypeStruct((n_samples, length, d * GROUP_W), F32)
    dq, dk, dv, *carried = _grid_call(
        body, f"attn_bwd_g{group}", (n_samples, d), [per_sample(a) for a in (q, k, v, dattn, lse_tot, rowdot)],
        [spec] * 6, [spec] * 3, [shp, shp, shp], VMEM_MID, comm)
    flat = lambda a: a.reshape(n_samples * length, d * GROUP_W)
    return flat(dq), flat(dk), flat(dv), carried


def _disc(lr, li, ldt, br, bi):
    dt = jnp.exp(ldt)
    mag = jnp.exp(lr * dt)
    ab_re, ab_im = mag * jnp.cos(li * dt), mag * jnp.sin(li * dt)
    den = lr * lr + li * li
    nr, ni = ab_re - 1.0, ab_im
    f_re = (nr * lr + ni * li) / den
    f_im = (ni * lr - nr * li) / den
    return ab_re, ab_im, f_re * br - f_im * bi, f_re * bi + f_im * br


def _state_mask():
    row_g = lax.broadcasted_iota(jnp.int32, (SCAN_CH, SCAN_WC), 0) // SSM_CH
    col_g = lax.broadcasted_iota(jnp.int32, (SCAN_CH, SCAN_WC), 1) // SSM_STATE
    return row_g == col_g


def _ssm_disc(lr, li, ldt, br, bi, cr, ci):
    w = SCAN_WC

    def body(lr_ref, li_ref, ldt_ref, br_ref, bi_ref, cr_ref, ci_ref, a_ref, bb_ref, c_ref):
        ar, ai, bbr, bbi = _disc(lr_ref[...], li_ref[...], ldt_ref[...], br_ref[...], bi_ref[...])
        crv, civ = cr_ref[...], ci_ref[...]
        mask = _state_mask()
        for cb in range(SCAN_NBLK):
            sl = slice(cb * w, (cb + 1) * w)
            rows = slice(cb * SCAN_CH, (cb + 1) * SCAN_CH)
            dense = lambda comp: jnp.where(mask, jnp.tile(comp[:, sl], (SCAN_CH // SSM_CH, 1)), 0.0)
            a_ref[:, 2 * cb * w:(2 * cb + 1) * w] = ar[:, sl]
            a_ref[:, (2 * cb + 1) * w:(2 * cb + 2) * w] = ai[:, sl]
            bb_ref[rows, :w] = dense(bbr).astype(MXU_DTYPE)
            bb_ref[rows, w:] = dense(bbi).astype(MXU_DTYPE)
            c_ref[rows, :w] = dense(crv).astype(MXU_DTYPE)
            c_ref[rows, w:] = (-dense(civ)).astype(MXU_DTYPE)

    return _pallas_call(
        body, name="ssm_disc",
        out_shape=[jax.ShapeDtypeStruct((1, 2 * N_STATE), F32), jax.ShapeDtypeStruct((SSM_W, 2 * w), MXU_DTYPE),
                   jax.ShapeDtypeStruct((SSM_W, 2 * w), MXU_DTYPE)],
        compiler_params=pltpu.CompilerParams(vmem_limit_bytes=VMEM_MID),
    )(lr, li, ldt, br, bi, cr, ci)


def _group_indicator():
    s = jnp.arange(N_STATE) // SSM_STATE
    return (s[:, None] == jnp.arange(LANES)[None, :]).astype(F32)


def _ssm_param_bwd(lr, li, ldt, br, bi, da_cat, dbb_full, dc_full):
    w = SCAN_WC

    def body(lr_ref, li_ref, ldt_ref, br_ref, bi_ref, da_ref, dbb_ref, dc_ref, ind_ref,
             glr_ref, gli_ref, gldt_ref, gbr_ref, gbi_ref, gcr_ref, gci_ref):
        mask = _state_mask()

        def diag_parts(ref):
            res = ([], [])
            for cb in range(SCAN_NBLK):
                for part in range(2):
                    blk = ref[cb * SCAN_CH:(cb + 1) * SCAN_CH, part * w:(part + 1) * w]
                    res[part].append(jnp.sum(jnp.where(mask, blk, 0.0).reshape(SCAN_CH // SSM_CH, SSM_CH, w), axis=0))
            return jnp.concatenate(res[0], axis=1), jnp.concatenate(res[1], axis=1)

        dar = jnp.concatenate([da_ref[:, 2 * cb * w:(2 * cb + 1) * w] for cb in range(SCAN_NBLK)], axis=1)
        dai = jnp.concatenate([da_ref[:, (2 * cb + 1) * w:(2 * cb + 2) * w] for cb in range(SCAN_NBLK)], axis=1)
        dbbr, dbbi = diag_parts(dbb_ref)
        dcr, dci_neg = diag_parts(dc_ref)
        gcr_ref[...] = dcr
        gci_ref[...] = -dci_neg
        _, vjp = jax.vjp(_disc, lr_ref[...], li_ref[...], ldt_ref[...], br_ref[...], bi_ref[...])
        glr, gli, gldt, gbr, gbi = vjp((dar, dai, dbbr, dbbi))
        glr_ref[...] = glr
        gli_ref[...] = gli
        gldt_ref[...] = jnp.dot(jnp.broadcast_to(gldt, (8, N_STATE)), ind_ref[...], preferred_element_type=F32,
                                precision=lax.Precision.HIGHEST)
        gbr_ref[...] = gbr
        gbi_ref[...] = gbi

    v1 = jax.ShapeDtypeStruct((1, N_STATE), F32)
    v16 = jax.ShapeDtypeStruct((SSM_CH, N_STATE), F32)
    vdt = jax.ShapeDtypeStruct((8, LANES), F32)
    return _pallas_call(
        body, name="ssm_param_bwd", out_shape=[v1, v1, vdt, v16, v16, v16, v16],
        compiler_params=pltpu.CompilerParams(vmem_limit_bytes=VMEM_BIG),
    )(lr, li, ldt, br, bi, da_cat, dbb_full, dc_full, _group_indicator())


def _cmul(ar, ai, br, bi):
    return ar * br - ai * bi, ar * bi + ai * br


def _gelu_tanh(y):
    return jnp.tanh(_GELU_C * (y + 0.044715 * (y * y * y)))


def _segment_carry(er, ei, ar, ai, n_rows, reverse):
    qr, qi = ar, ai
    for _ in range(int(math.log2(SCAN_LEN))):
        qr, qi = _cmul(qr, qi, qr, qi)
    seg = lax.broadcasted_iota(jnp.int32, er.shape, 0) % SCAN_SEG_PER_SAMPLE
    shift = 1
    while shift < SCAN_SEG_PER_SAMPLE:
        keep = (seg < SCAN_SEG_PER_SAMPLE - shift) if reverse else (seg >= shift)
        amount = n_rows - shift if reverse else shift
        sr = jnp.where(keep, pltpu.roll(er, amount, 0), 0.0)
        si = jnp.where(keep, pltpu.roll(ei, amount, 0), 0.0)
        if reverse:
            er, ei = er + qr * sr + qi * si, ei + qr * si - qi * sr
        else:
            er, ei = er + qr * sr - qi * si, ei + qr * si + qi * sr
        qr, qi = _cmul(qr, qi, qr, qi)
        shift *= 2
    keep = (seg < SCAN_SEG_PER_SAMPLE - 1) if reverse else (seg >= 1)
    amount = n_rows - 1 if reverse else 1
    return jnp.where(keep, pltpu.roll(er, amount, 0), 0.0), jnp.where(keep, pltpu.roll(ei, amount, 0), 0.0)


def _ssm_fwd(u_perm, a_cat, bbc, cc, dskip, n_rows):
    t = u_perm.shape[0]
    w = SCAN_WC
    rows_c = SCAN_CHUNK * n_rows
    n_chunks = t // rows_c

    def body(u_ref, a_ref, bb_ref, c_ref, d_ref, yt_ref, yg_ref, ein_ref, bu_s, xs_s):
        ar = jnp.broadcast_to(a_ref[:, :w], (n_rows, w))
        ai = jnp.broadcast_to(a_ref[:, w:], (n_rows, w))

        def sweep(carry, store):
            def chunk(ch, carry):
                r0 = pl.multiple_of(ch * rows_c, rows_c)
                u_c = u_ref[pl.ds(r0, rows_c), :]
                if not store:
                    bu_s[pl.ds(r0, rows_c), :] = jnp.dot(u_c.astype(MXU_DTYPE), bb_ref[...], preferred_element_type=F32)

                def step(i, c):
                    o = pl.multiple_of(i * n_rows, n_rows)
                    blk = bu_s[pl.ds(r0 + o, n_rows), :]
                    nr = ar * c[0] - ai * c[1] + blk[:, :w]
                    ni = ar * c[1] + ai * c[0] + blk[:, w:]
                    if store:
                        xs_s[pl.ds(o, n_rows), :w] = nr
                        xs_s[pl.ds(o, n_rows), w:] = ni
                    return nr, ni

                carry = lax.fori_loop(0, SCAN_CHUNK, step, carry)
                if store:
                    y = lax.dot_general(xs_s[...].astype(MXU_DTYPE), c_ref[...], _NT, preferred_element_type=F32)
                    yt = y + d_ref[...] * u_c
                    yt_ref[pl.ds(r0, rows_c), :] = yt
                    yg_ref[pl.ds(r0, rows_c), :] = (0.5 * yt * (1.0 + _gelu_tanh(yt))).astype(BF16)
                return carry

            return lax.fori_loop(0, n_chunks, chunk, carry)

        zero = jnp.zeros((n_rows, w), F32)
        er, ei = sweep((zero, zero), False)
        cr, ci = _segment_carry(er, ei, ar, ai, n_rows, False)
        ein_ref[:, :w] = cr
        ein_ref[:, w:] = ci
        sweep((cr, ci), True)

    col = lambda width: pl.BlockSpec((t, width), lambda c: (0, c))
    wgt = pl.BlockSpec((SCAN_CH, 2 * w), lambda c: (c, 0))
    return _pallas_call(
        body, name="ssm_fwd", grid=(SCAN_NBLK,),
        in_specs=[col(SCAN_CH), pl.BlockSpec((1, 2 * w), lambda c: (0, c)), wgt, wgt,
                  pl.BlockSpec((1, SCAN_CH), lambda c: (0, c))],
        out_specs=[col(SCAN_CH), col(SCAN_CH), pl.BlockSpec((n_rows, 2 * w), lambda c: (0, c))],
        out_shape=[jax.ShapeDtypeStruct((t, SSM_W), F32), jax.ShapeDtypeStruct((t, SSM_W), BF16),
                   jax.ShapeDtypeStruct((n_rows, 2 * N_STATE), F32)],
        scratch_shapes=[pltpu.VMEM((t, 2 * w), F32), pltpu.VMEM((rows_c, 2 * w), F32)],
        compiler_params=pltpu.CompilerParams(dimension_semantics=("parallel",), vmem_limit_bytes=VMEM_BIG),
    )(u_perm, a_cat, bbc, cc, dskip)


def _ssm_bwd(u_perm, dypre, du_skip, a_cat, bbc, cc, ein, n_rows, comm=None):
    t = u_perm.shape[0]
    w = SCAN_WC
    rows_c = SCAN_CHUNK * n_rows
    n_chunks = t // rows_c

    def body(u_ref, dy_ref, dus_ref, a_ref, bb_ref, c_ref, ein_ref, du_ref, da_ref, dbb_ref, dc_ref, xs_all, tmp_s, g_s):
        ar = jnp.broadcast_to(a_ref[:, :w], (n_rows, w))
        ai = jnp.broadcast_to(a_ref[:, w:], (n_rows, w))
        zero = jnp.zeros((n_rows, w), F32)

        xs_all[0:n_rows, :] = ein_ref[...]

        def fwd_chunk(ch, carry):
            r0 = pl.multiple_of(ch * rows_c, rows_c)
            tmp_s[...] = jnp.dot(u_ref[pl.ds(r0, rows_c), :].astype(MXU_DTYPE), bb_ref[...], preferred_element_type=F32)

            def step(i, c):
                o = pl.multiple_of(i * n_rows, n_rows)
                blk = tmp_s[pl.ds(o, n_rows), :]
                nr = ar * c[0] - ai * c[1] + blk[:, :w]
                ni = ar * c[1] + ai * c[0] + blk[:, w:]
                xs_all[pl.ds(n_rows + r0 + o, n_rows), :w] = nr
                xs_all[pl.ds(n_rows + r0 + o, n_rows), w:] = ni
                return nr, ni

            return lax.fori_loop(0, SCAN_CHUNK, step, carry)

        lax.fori_loop(0, n_chunks, fwd_chunk, (ein_ref[:, :w], ein_ref[:, w:]))

        def load_dx(ch):
            r0 = pl.multiple_of(ch * rows_c, rows_c)
            tmp_s[...] = jnp.dot(dy_ref[pl.ds(r0, rows_c), :], c_ref[...], preferred_element_type=F32)
            return r0

        def back_steps(carry, store):
            def step(ii, c):
                o = pl.multiple_of((SCAN_CHUNK - 1 - ii) * n_rows, n_rows)
                blk = tmp_s[pl.ds(o, n_rows), :]
                gr = blk[:, :w] + ar * c[0] + ai * c[1]
                gi = blk[:, w:] + ar * c[1] - ai * c[0]
                if store:
                    g_s[pl.ds(o, n_rows), :w] = gr
                    g_s[pl.ds(o, n_rows), w:] = gi
                return gr, gi

            return lax.fori_loop(0, SCAN_CHUNK, step, carry)

        def first_sweep(cc_, carry):
            load_dx(n_chunks - 1 - cc_)
            return back_steps(carry, False)

        sr, si = lax.fori_loop(0, n_chunks, first_sweep, (zero, zero))
        gr0, gi0 = _segment_carry(sr, si, ar, ai, n_rows, True)

        dbb_ref[...] = jnp.zeros_like(dbb_ref)
        dc_ref[...] = jnp.zeros_like(dc_ref)
        da_ref[...] = jnp.zeros_like(da_ref)

        def second_sweep(cc_, carry):
            r0 = load_dx(n_chunks - 1 - cc_)
            carry = back_steps(carry, True)
            g = g_s[...]
            xp = xs_all[pl.ds(r0, rows_c), :]
            xc = xs_all[pl.ds(r0 + n_rows, rows_c), :]
            da_ref[:, :w] += jnp.sum(g[:, :w] * xp[:, :w] + g[:, w:] * xp[:, w:], axis=0, keepdims=True)
            da_ref[:, w:] += jnp.sum(g[:, w:] * xp[:, :w] - g[:, :w] * xp[:, w:], axis=0, keepdims=True)
            gb = g.astype(MXU_DTYPE)
            du_ref[pl.ds(r0, rows_c), :] = (lax.dot_general(gb, bb_ref[...], _NT, preferred_element_type=F32)
                                            + dus_ref[pl.ds(r0, rows_c), :])
            dbb_ref[...] += lax.dot_general(u_ref[pl.ds(r0, rows_c), :].astype(MXU_DTYPE), gb, _TN,
                                            preferred_element_type=F32)
            dc_ref[...] += lax.dot_general(dy_ref[pl.ds(r0, rows_c), :], xc.astype(MXU_DTYPE), _TN,
                                           preferred_element_type=F32)
            return carry

        lax.fori_loop(0, n_chunks, second_sweep, (gr0, gi0))

    col = lambda width: pl.BlockSpec((t, width), lambda c, j: (0, c))
    wgt = pl.BlockSpec((SCAN_CH, 2 * w), lambda c, j: (c, 0))
    row = pl.BlockSpec((1, 2 * w), lambda c, j: (0, c))
    return _grid_call(
        body, "ssm_bwd", (SCAN_NBLK, 1), [u_perm, dypre, du_skip, a_cat, bbc, cc, ein],
        [col(SCAN_CH), col(SCAN_CH), col(SCAN_CH), row, wgt, wgt, pl.BlockSpec((n_rows, 2 * w), lambda c, j: (0, c))],
        [col(SCAN_CH), row, wgt, wgt],
        [jax.ShapeDtypeStruct((t, SSM_W), F32), jax.ShapeDtypeStruct((1, 2 * N_STATE), F32),
         jax.ShapeDtypeStruct((SSM_W, 2 * w), F32), jax.ShapeDtypeStruct((SSM_W, 2 * w), F32)],
        56 * 1024 * 1024, comm,
        scratch=[pltpu.VMEM((t + n_rows, 2 * w), F32), pltpu.VMEM((rows_c, 2 * w), F32), pltpu.VMEM((rows_c, 2 * w), F32)])


def _to_scan_rows(a, n_samples):
    c = a.shape[1]
    return a.reshape(n_samples, SCAN_SEG_PER_SAMPLE, SCAN_LEN, c).transpose(2, 0, 1, 3).reshape(-1, c)


def _from_scan_rows(a, n_samples):
    c = a.shape[1]
    return a.reshape(SCAN_LEN, n_samples, SCAN_SEG_PER_SAMPLE, c).transpose(1, 2, 0, 3).reshape(-1, c)


def _row_spec(tm, width):
    return pl.BlockSpec((tm, width), lambda i, j: (i, 0))


def _whole(arr):
    return pl.BlockSpec(arr.shape, lambda i, j: (0,) * arr.ndim)


def _proj_rope(x, g, w_in_t, tabs, comm=None):
    t = x.shape[0]
    tm = 256

    def body(x_ref, g_ref, w_ref, tc_ref, tlo_ref, thi_ref, h_ref, u_ref, gate_ref, *rest):
        qkv_refs, stage = rest[:9], rest[9]
        xv = x_ref[...]
        r = lax.rsqrt(jnp.mean(xv * xv, axis=-1, keepdims=True) + RMS_EPS)
        h = ((xv * r) * g_ref[...]).astype(BF16)
        h_ref[...] = h
        p = lax.dot_general(h.astype(MXU_DTYPE), w_ref[...], _NT, preferred_element_type=F32)
        u_ref[...] = p[:, QKV_W:QKV_W + SSM_W]
        gate_ref[...] = _sigmoid(p[:, QKV_W + SSM_W:])
        tc, tlo, thi = tc_ref[...], tlo_ref[...], thi_ref[...]
        n_ch = QKV_W // LANES
        for ch in range(n_ch):
            piece = p[:, _lane_chunk(ch)]
            stage[ch] = _rope_apply(piece, tc, tlo, thi) if ch < 2 * n_ch // 3 else piece
        halves = GROUP_W // LANES
        for grp, d in enumerate(DILATIONS):
            for which in range(3):
                out = qkv_refs[3 * grp + which]
                for res in range(d):
                    for half in range(halves):
                        ch = which * (n_ch // 3) + grp * halves + half
                        out[:, _lane_chunk(res * halves + half)] = _gather_residue(stage, ch, res, d, tm // d).astype(BF16)

    tab = pl.BlockSpec((tm, LANES), lambda i, j: (i % (SEQ // tm), 0))
    widths = [(D_MODEL, BF16), (SSM_W, F32), (2 * D_MODEL, F32)]
    out_specs = [_row_spec(tm, wd) for wd, _ in widths]
    out_shapes = [jax.ShapeDtypeStruct((t, wd), dt) for wd, dt in widths]
    for d in DILATIONS:
        out_specs += [_row_spec(tm // d, d * GROUP_W)] * 3
        out_shapes += [jax.ShapeDtypeStruct((t // d, d * GROUP_W), BF16)] * 3
    return _grid_call(
        body, "proj_rope", (t // tm, 1), [x, g, w_in_t, *tabs],
        [_row_spec(tm, D_MODEL), _whole(g), _whole(w_in_t), tab, tab, tab], out_specs, out_shapes, VMEM_BIG, comm,
        scratch=[pltpu.VMEM((QKV_W // LANES, tm, LANES), F32)])


def _out_rms(merged, w_out, x, g):
    t = x.shape[0]
    tm = 512

    def body(m_ref, w_ref, x_ref, g_ref, x1_ref, h_ref):
        x1 = x_ref[...] + jnp.dot(m_ref[...].astype(MXU_DTYPE), w_ref[...], preferred_element_type=F32)
        x1_ref[...] = x1
        r = lax.rsqrt(jnp.mean(x1 * x1, axis=-1, keepdims=True) + RMS_EPS)
        h_ref[...] = ((x1 * r) * g_ref[...]).astype(BF16)

    return _grid_call(
        body, "out_rms", (t // tm, 1), [merged, w_out, x, g],
        [_row_spec(tm, D_MODEL), _whole(w_out), _row_spec(tm, D_MODEL), _whole(g)],
        [_row_spec(tm, D_MODEL)] * 2, [jax.ShapeDtypeStruct((t, D_MODEL), F32), jax.ShapeDtypeStruct((t, D_MODEL), BF16)],
        VMEM_BIG)


FFN_TN = D_FF // 2
MXU_COLS = 256


def _ffn_in_swiglu(h2, w_gate_t, w_up_t, comm=None):
    t = h2.shape[0]
    tm = 512

    def body(h_ref, wg_ref, wu_ref, a_ref, b_ref, f_ref):
        h = h_ref[...].astype(MXU_DTYPE)
        for c0 in range(0, FFN_TN, MXU_COLS):
            sl = slice(c0, min(c0 + MXU_COLS, FFN_TN))
            a = lax.dot_general(h, wg_ref[sl, :], _NT, preferred_element_type=F32)
            b = lax.dot_general(h, wu_ref[sl, :], _NT, preferred_element_type=F32)
            a_ref[:, sl] = a
            b_ref[:, sl] = b
            f_ref[:, sl] = (a * _sigmoid(a) * b).astype(BF16)

    tile = pl.BlockSpec((tm, FFN_TN), lambda j, i: (i, j))
    wspec = pl.BlockSpec((FFN_TN, D_MODEL), lambda j, i: (j, 0))
    return _grid_call(
        body, "ffn_in_swiglu", (D_FF // FFN_TN, t // tm), [h2, w_gate_t, w_up_t],
        [pl.BlockSpec((tm, D_MODEL), lambda j, i: (i, 0)), wspec, wspec],
        [tile] * 3, [jax.ShapeDtypeStruct((t, D_FF), F32)] * 2 + [jax.ShapeDtypeStruct((t, D_FF), BF16)], VMEM_BIG, comm)


def _ffn_down_final(f, w_down, x1, target, g):
    t = x1.shape[0]
    tm = 256

    def body(f_ref, w_ref, x1_ref, t_ref, g_ref, dx_ref, dxb_ref, loss_ref, gg_ref):
        @pl.when(pl.program_id(0) == 0)
        def _():
            loss_ref[...] = jnp.zeros_like(loss_ref)
            gg_ref[...] = jnp.zeros_like(gg_ref)

        xv = x1_ref[...] + jnp.dot(f_ref[...].astype(MXU_DTYPE), w_ref[...], preferred_element_type=F32)
        gv = g_ref[...]
        r = lax.rsqrt(jnp.mean(xv * xv, axis=-1, keepdims=True) + RMS_EPS)
        n = xv * r
        diff = n * gv - t_ref[...]
        per_tok = jnp.mean(diff * diff, axis=-1, keepdims=True)
        loss_ref[...] += 0.5 * jnp.sum(per_tok, axis=0, keepdims=True)
        dy = diff / xv.shape[-1]
        gg_ref[...] += jnp.sum(dy * n, axis=0, keepdims=True)
        dn = dy * gv
        dx = r * (dn - n * jnp.mean(dn * n, axis=-1, keepdims=True))
        dx_ref[...] = dx
        dxb_ref[...] = dx.astype(BF16)

    acc = lambda shp: pl.BlockSpec(shp, lambda i, j: (0, 0))
    return _grid_call(
        body, "ffn_down_final", (t // tm, 1), [f, w_down, x1, target, g],
        [_row_spec(tm, D_FF), _whole(w_down), _row_spec(tm, D_MODEL), _row_spec(tm, D_MODEL), _whole(g)],
        [_row_spec(tm, D_MODEL)] * 2 + [acc((8, LANES)), acc((1, D_MODEL))],
        [jax.ShapeDtypeStruct((t, D_MODEL), F32), jax.ShapeDtypeStruct((t, D_MODEL), BF16),
         jax.ShapeDtypeStruct((8, LANES), F32), jax.ShapeDtypeStruct((1, D_MODEL), F32)], VMEM_BIG, sequential=True)


def _d_f_swiglu_bwd(dx2b, w_down, a, b):
    t = a.shape[0]
    tm = 512

    def body(dx_ref, w_ref, a_ref, b_ref, da_ref, db_ref):
        d = lax.dot_general(dx_ref[...], w_ref[...], _NT, preferred_element_type=F32)
        av, bv = a_ref[...], b_ref[...]
        sg = _sigmoid(av)
        da_ref[...] = (d * bv * sg * (1.0 + av * (1.0 - sg))).astype(BF16)
        db_ref[...] = (d * av * sg).astype(BF16)

    tile = pl.BlockSpec((tm, FFN_TN), lambda j, i: (i, j))
    return _grid_call(
        body, "d_f_swiglu_bwd", (D_FF // FFN_TN, t // tm), [dx2b, w_down, a, b],
        [pl.BlockSpec((tm, D_MODEL), lambda j, i: (i, 0)), pl.BlockSpec((FFN_TN, D_MODEL), lambda j, i: (j, 0)), tile, tile],
        [tile] * 2, [jax.ShapeDtypeStruct((t, D_FF), BF16)] * 2, VMEM_BIG)


def _mm_rms_bwd(operands, weights, x, g, dres, name, comm=None):
    t = x.shape[0]
    tm = 256
    n_op = len(operands)

    def body(*refs):
        a_refs, w_refs = refs[:n_op], refs[n_op:2 * n_op]
        x_ref, g_ref, dres_ref, dx_ref, dxb_ref, gg_ref = refs[2 * n_op:]

        @pl.when(pl.program_id(0) == 0)
        def _():
            gg_ref[...] = jnp.zeros_like(gg_ref)

        dh = None
        for a_ref, w_ref in zip(a_refs, w_refs):
            part = jnp.dot(a_ref[...].astype(MXU_DTYPE), w_ref[...], preferred_element_type=F32)
            dh = part if dh is None else dh + part
        xv = x_ref[...]
        r = lax.rsqrt(jnp.mean(xv * xv, axis=-1, keepdims=True) + RMS_EPS)
        n = xv * r
        gg_ref[...] += jnp.sum(dh * n, axis=0, keepdims=True)
        dn = dh * g_ref[...]
        dx = dres_ref[...] + r * (dn - n * jnp.mean(dn * n, axis=-1, keepdims=True))
        dx_ref[...] = dx
        dxb_ref[...] = dx.astype(BF16)

    d = x.shape[1]
    return _grid_call(
        body, name, (t // tm, 1), [*operands, *weights, x, g, dres],
        [_row_spec(tm, a.shape[1]) for a in operands] + [_whole(wk) for wk in weights]
        + [_row_spec(tm, d), _whole(g), _row_spec(tm, d)],
        [_row_spec(tm, d)] * 2 + [pl.BlockSpec((1, d), lambda i, j: (0, 0))],
        [jax.ShapeDtypeStruct((t, d), F32), jax.ShapeDtypeStruct((t, d), BF16), jax.ShapeDtypeStruct((1, d), F32)],
        VMEM_BIG, comm, sequential=True)


def _flat_small(small):
    perm_b = lambda a: a.reshape(SSM_GROUPS, SSM_STATE, SSM_CH).transpose(2, 0, 1).reshape(SSM_CH, N_STATE)
    perm_c = lambda a: a.reshape(SSM_GROUPS, SSM_CH, SSM_STATE).transpose(1, 0, 2).reshape(SSM_CH, N_STATE)
    return dict(
        g_mix=small["norm_mix_g"].reshape(1, D_MODEL), g_ffn=small["norm_ffn_g"].reshape(1, D_MODEL),
        g_fin=small["norm_final_g"].reshape(1, D_MODEL),
        lr=small["ssm_a_re"].reshape(1, N_STATE), li=small["ssm_a_im"].reshape(1, N_STATE),
        ldt=jnp.repeat(small["ssm_log_dt"].reshape(SSM_GROUPS), SSM_STATE).reshape(1, N_STATE),
        br=perm_b(small["ssm_b_re"]), bi=perm_b(small["ssm_b_im"]),
        cr=perm_c(small["ssm_c_re"]), ci=perm_c(small["ssm_c_im"]), dskip=small["ssm_d"].reshape(1, SSM_W))


AG_HOSTS = {"proj_rope": ("w_glu", "w_attn_out", "w_out"), "attn_fwd_g0": ("w_ffn_gate",), "attn_fwd_g1": ("w_ffn_up",),
            "ffn_in_swiglu": ("w_ffn_down",)}
HALVED = ("w_ffn_gate", "w_ffn_up", "w_in")
A2A_HOSTS = {"d_h2_rms": ("w_ffn_down",), "attn_bwd_g0": ("w_ffn_gate:0",), "attn_bwd_g1": ("w_ffn_gate:1",),
             "attn_bwd_g2": ("w_ffn_up:0",), "ssm_bwd": ("w_ffn_up:1", "w_out", "w_attn_out", "w_glu"),
             "mm_g_in1": ("w_in:0",), "d_h0_rms": ("w_in:1",)}
SMALL_HOST = "mm_g_in0"


def _local_step(x, target, w, small, shards=None):
    t = x.shape[0]
    n_samples = t // SEQ
    n_rows = n_samples * SCAN_SEG_PER_SAMPLE
    tabs = _rope_tables()
    w = dict(w)
    fs = _flat_small(small)
    g_mix, g_ffn, g_fin, dskip = fs["g_mix"], fs["g_ffn"], fs["g_fin"], fs["dskip"]
    a_cat, bbc, cc = _ssm_disc(fs["lr"], fs["li"], fs["ldt"], fs["br"], fs["bi"], fs["cr"], fs["ci"])
    big, recv, small_pack = {}, {}, []

    def comm_of(name):
        if shards is None:
            return None
        if name == SMALL_HOST:
            return _ag_comm([(small_pack[0], 0, 0)], [(N_DEV, *small_pack[0].shape)])
        if name in AG_HOSTS:
            names = AG_HOSTS[name]
            return _ag_comm([(shards[n], j, 0) for j, n in enumerate(names)], [(N_DEV, *shards[n].shape) for n in names])
        if name in A2A_HOSTS:
            return _a2a_comm([(big[n].reshape(N_DEV, -1, big[n].shape[1]), 0) for n in A2A_HOSTS[name]])
        return None

    def absorb(name, carried):
        if name == SMALL_HOST:
            recv["small"] = carried[0]
        for n, a3 in zip(AG_HOSTS.get(name, ()), carried):
            w[n] = a3.reshape(-1, a3.shape[2])
        for n, a3 in zip(A2A_HOSTS.get(name, ()), carried):
            recv[n] = a3

    def mm(a, b, mode, name, tm, tn, **kw):
        comm = comm_of(name)
        if comm is None:
            return _mm(a, b, mode, name, tm, tn, **kw)
        out, *carried = _mm(a, b, mode, name, tm, tn, comm=comm, **kw)
        absorb(name, carried)
        return out

    h0, u, gates, *rest = _proj_rope(x, g_mix, w["w_in"], tabs, comm_of("proj_rope"))
    qkv = [rest[3 * g:3 * g + 3] for g in range(3)]
    absorb("proj_rope", rest[9:])
    os_, lses = [], []
    for g in range(3):
        o_g, l_g, carried = _attn_fwd(*qkv[g], g, n_samples, comm_of(f"attn_fwd_g{g}"))
        absorb(f"attn_fwd_g{g}", carried)
        os_.append(o_g)
        lses.append(l_g)
    attn, lse_tot = _attn_merge(os_, lses)
    attn_d = mm(attn, w["w_attn_out"], "nt", "mm_attn_out", 512, D_MODEL)

    u_perm = _to_scan_rows(u, n_samples)
    ytot, yg_perm, ein = _ssm_fwd(u_perm, a_cat, bbc, cc, dskip, n_rows)
    yg = _from_scan_rows(yg_perm, n_samples)
    z = mm(yg, w["w_glu"], "nt", "mm_glu", 512, 2 * D_MODEL)

    merged = _mix(attn_d, z, gates)
    x1, h2 = _out_rms(merged, w["w_out"], x, g_ffn)
    ffn_a, ffn_b, f, *carried = _ffn_in_swiglu(h2, w["w_ffn_gate"], w["w_ffn_up"], comm_of("ffn_in_swiglu"))
    absorb("ffn_in_swiglu", carried)
    dx2, dx2b, loss_blk, g_gfin = _ffn_down_final(f, w["w_ffn_down"], x1, target, g_fin)

    da, db = _d_f_swiglu_bwd(dx2b, w["w_ffn_down"], ffn_a, ffn_b)
    big["w_ffn_down"] = mm(f, dx2b, "tn", "mm_g_down", 256, D_MODEL, out_dtype=BF16)
    half = D_MODEL // 2
    for hf in range(2):
        big[f"w_ffn_gate:{hf}"] = mm(da, h2, "tn", f"mm_g_gate{hf}", 256, half, out_dtype=BF16, cols=(hf * half, half))
        big[f"w_ffn_up:{hf}"] = mm(db, h2, "tn", f"mm_g_up{hf}", 256, half, out_dtype=BF16, cols=(hf * half, half))
    dx1, dx1b, g_gffn, *carried = _mm_rms_bwd([da, db], [w["w_ffn_gate"], w["w_ffn_up"]], x1, g_ffn, dx2, "d_h2_rms",
                                              comm_of("d_h2_rms"))
    absorb("d_h2_rms", carried)

    dmerged = mm(dx1b, w["w_out"], "nt", "mm_d_merged", 512, D_MODEL)
    big["w_out"] = mm(merged, dx1b, "tn", "mm_g_out", 256, D_MODEL, out_dtype=BF16)
    dattn_d, dz, dgpre = _mix_bwd(dmerged, gates, attn_d, z)

    dattn = mm(dattn_d, w["w_attn_out"], "nn", "mm_d_attn", 512, GROUP_W)
    big["w_attn_out"] = mm(dattn_d, attn, "tn", "mm_g_attn_out", 512, GROUP_W, out_dtype=BF16)
    cot = _attn_rowdot(dattn, attn, lse_tot)
    dqs, dks, dvs = [], [], []
    for g in range(3):
        dq_g, dk_g, dv_g, carried = _attn_bwd(*qkv[g], *cot[g], g, n_samples, comm_of(f"attn_bwd_g{g}"))
        absorb(f"attn_bwd_g{g}", carried)
        dqs.append(dq_g)
        dks.append(dk_g)
        dvs.append(dv_g)

    dyg = mm(dz, w["w_glu"], "nn", "mm_d_yg", 512, SSM_W)
    big["w_glu"] = mm(dz, yg, "tn", "mm_g_glu", 512, 512, out_dtype=BF16)
    dyg_perm = _to_scan_rows(dyg, n_samples)
    dypre, du_skip, g_dskip = _ssm_act_bwd(dyg_perm, ytot, u_perm, dskip)
    du_perm, da_cat, dbb_full, dc_full, *carried = _ssm_bwd(u_perm, dypre, du_skip, a_cat, bbc, cc, ein, n_rows,
                                                          comm_of("ssm_bwd"))
    absorb("ssm_bwd", carried)
    du = _from_scan_rows(du_perm, n_samples)
    g_lr, g_li, g_ldt, g_br, g_bi, g_cr, g_ci = _ssm_param_bwd(
        fs["lr"], fs["li"], fs["ldt"], fs["br"], fs["bi"], da_cat, dbb_full, dc_full)

    small_pack.append(_pack_small(dict(lr=g_lr, li=g_li, ldt=g_ldt, br=g_br, bi=g_bi, cr=g_cr, ci=g_ci, dskip=g_dskip,
                                       g_ffn=g_gffn, g_fin=g_gfin, loss=loss_blk)))

    dproj = _pack_dproj(dqs, dks, dvs, du, dgpre, tabs)
    for hf in range(2):
        big[f"w_in:{hf}"] = mm(dproj, h0, "tn", f"mm_g_in{hf}", 256, half, out_dtype=BF16, cols=(hf * half, half))
    grad_x, _, g_gmix, *carried = _mm_rms_bwd([dproj], [w["w_in"]], x, g_mix, dx1, "d_h0_rms", comm_of("d_h0_rms"))
    absorb("d_h0_rms", carried)
    return grad_x, (big if shards is None else recv), small_pack[0], g_gmix


_MESH = pl.DeviceIdType.MESH


def _all_gather(block, name):
    rows, lanes = block.shape

    def body(x_ref, out_ref, send_sems, recv_sems, local_sem):
        x, y, c = lax.axis_index("x"), lax.axis_index("y"), lax.axis_index("c")
        me, sibling = (x, y, c), (x, y, 1 - c)
        chips = [(1 - x, y), (x, 1 - y), (1 - x, 1 - y)]

        def slot(px, py, pc):
            return out_ref.at[4 * px + 2 * py + pc]

        def copy(k, blk, to, src=None):
            return pltpu.make_async_remote_copy(
                src_ref=slot(*blk) if src is None else src, dst_ref=slot(*blk), send_sem=send_sems.at[k],
                recv_sem=recv_sems.at[k], device_id=to, device_id_type=_MESH)

        mine = pltpu.make_async_copy(x_ref, slot(*me), local_sem)
        mine.start()
        first = [copy(0, me, sibling, src=x_ref)]
        first += [copy(1 + j, me, (*chip, c), src=x_ref) for j, chip in enumerate(chips)]
        for cp in first:
            cp.start()
        passed = [copy(4 + j, (*chip, c), sibling) for j, chip in enumerate(chips)]
        for j, chip in enumerate(chips):
            copy(1 + j, (*chip, c), me).wait_recv()
            passed[j].start()
        copy(0, sibling, me).wait_recv()
        for j, chip in enumerate(chips):
            copy(4 + j, (*chip, 1 - c), me).wait_recv()
        for cp in first + passed:
            cp.wait_send()
        mine.wait()

    return _pallas_call(
        body, name=name, out_shape=jax.ShapeDtypeStruct((N_DEV, rows, lanes), block.dtype),
        in_specs=[pl.BlockSpec(memory_space=pl.ANY)], out_specs=pl.BlockSpec(memory_space=pl.ANY),
        scratch_shapes=[pltpu.SemaphoreType.DMA((7,)), pltpu.SemaphoreType.DMA((7,)), pltpu.SemaphoreType.DMA],
    )(block)


def _ag_comm(items, bufs):
    def plan(in_refs, out_refs, send_sems, recv_sems, local_sems):
        x, y, c = lax.axis_index("x"), lax.axis_index("y"), lax.axis_index("c")
        me, sibling = (x, y, c), (x, y, 1 - c)
        chips = [(1 - x, y), (x, 1 - y), (1 - x, 1 - y)]
        plans = []
        for t, (_, buf, slot0) in enumerate(items):
            x_ref, out_ref = in_refs[t], out_refs[buf]

            def slot(px, py, pc, out_ref=out_ref, slot0=slot0):
                return out_ref.at[slot0 + 4 * px + 2 * py + pc]

            def copy(k, blk, to, src=None, t=t, slot=slot):
                return pltpu.make_async_remote_copy(
                    src_ref=slot(*blk) if src is None else src, dst_ref=slot(*blk), send_sem=send_sems.at[7 * t + k],
                    recv_sem=recv_sems.at[7 * t + k], device_id=to, device_id_type=_MESH)

            plans.append(dict(
                mine=pltpu.make_async_copy(x_ref, slot(*me), local_sems.at[t]),
                first=[copy(0, me, sibling, src=x_ref)] + [copy(1 + j, me, (*chip, c), src=x_ref)
                                                           for j, chip in enumerate(chips)],
                passed=[copy(4 + j, (*chip, c), sibling) for j, chip in enumerate(chips)],
                from_ici=[copy(1 + j, (*chip, c), me) for j, chip in enumerate(chips)],
                from_sibling=[copy(0, sibling, me)] + [copy(4 + j, (*chip, 1 - c), me) for j, chip in enumerate(chips)]))
        return plans

    def start(*refs):
        for p in plan(*refs):
            p["mine"].start()
            for cp in p["first"]:
                cp.start()

    def finish(*refs):
        plans = plan(*refs)
        for p in plans:
            for arrived, onward in zip(p["from_ici"], p["passed"]):
                arrived.wait_recv()
                onward.start()
        for p in plans:
            for arrived in p["from_sibling"]:
                arrived.wait_recv()
            for cp in p["first"] + p["passed"]:
                cp.wait_send()
            p["mine"].wait()

    dtype_of = {buf: shard.dtype for shard, buf, _ in items}
    out_shapes = [jax.ShapeDtypeStruct(b, dtype_of[j]) for j, b in enumerate(bufs)]
    return _Comm([it[0] for it in items], out_shapes, 7 * len(items), len(items), start, finish)


def _a2a_comm(items):
    def plan(in_refs, out_refs, send_sems, recv_sems, local_sems):
        x, y, c = lax.axis_index("x"), lax.axis_index("y"), lax.axis_index("c")
        my = 4 * x + 2 * y + c
        copies, locals_ = [], []
        for t, (_, slot0) in enumerate(items):
            s_ref, r_ref = in_refs[t], out_refs[t]
            locals_.append(pltpu.make_async_copy(s_ref.at[slot0 + my], r_ref.at[my], local_sems.at[t]))
            for kk in range(1, N_DEV):
                px = 1 - x if kk & 4 else x
                py = 1 - y if kk & 2 else y
                pc = 1 - c if kk & 1 else c
                copies.append(pltpu.make_async_remote_copy(
                    src_ref=s_ref.at[slot0 + 4 * px + 2 * py + pc], dst_ref=r_ref.at[my],
                    send_sem=send_sems.at[7 * t + kk - 1], recv_sem=recv_sems.at[7 * t + kk - 1],
                    device_id=(px, py, pc), device_id_type=_MESH))
        return copies, locals_

    def start(*refs):
        copies, locals_ = plan(*refs)
        for cp in locals_ + copies:
            cp.start()

    def finish(*refs):
        copies, locals_ = plan(*refs)
        for cp in copies + locals_:
            cp.wait()

    out_shapes = [jax.ShapeDtypeStruct((N_DEV,) + it[0].shape[1:], it[0].dtype) for it in items]
    return _Comm([it[0] for it in items], out_shapes, 7 * len(items), len(items), start, finish)


def _adam_math(g, w, m, v):
    m_new = ADAM_B1 * m + (1.0 - ADAM_B1) * g
    v_new = ADAM_B2 * v + (1.0 - ADAM_B2) * jnp.square(g)
    m_hat = m_new / (1.0 - ADAM_B1 ** ADAM_STEP)
    v_hat = v_new / (1.0 - ADAM_B2 ** ADAM_STEP)
    return -ADAM_LR * (m_hat / (jnp.sqrt(v_hat) + ADAM_EPS) + ADAM_WD * w), m_new, v_new


def _sum_partials(parts, name, tm):
    n, rows, _ = parts[0].shape
    widths = [p.shape[2] for p in parts]

    def body(*refs):
        g_ref, off = refs[-1], 0
        for p_ref, wd in zip(refs[:-1], widths):
            g = p_ref[0].astype(F32)
            for s in range(1, n):
                g = g + p_ref[s].astype(F32)
            g_ref[:, off:off + wd] = g
            off += wd

    return _pallas_call(
        body, name=name, grid=(rows // tm,), in_specs=[pl.BlockSpec((n, tm, wd), lambda i: (0, i, 0)) for wd in widths],
        out_specs=pl.BlockSpec((tm, sum(widths)), lambda i: (i, 0)),
        out_shape=jax.ShapeDtypeStruct((rows, sum(widths)), F32),
        compiler_params=pltpu.CompilerParams(dimension_semantics=("parallel",), vmem_limit_bytes=VMEM_MID),
    )(*parts)


def _adam(partials, w, m, v, name, tm):
    n, rows, cols = partials.shape

    def body(p_ref, w_ref, m_ref, v_ref, g_ref, d_ref, nm_ref, nv_ref):
        g = p_ref[0].astype(F32)
        for s in range(1, n):
            g = g + p_ref[s].astype(F32)
        g_ref[...] = g
        d_ref[...], nm_ref[...], nv_ref[...] = _adam_math(g, w_ref[...], m_ref[...], v_ref[...])

    assert rows % tm == 0
    row = pl.BlockSpec((tm, cols), lambda i: (i, 0))
    shp = jax.ShapeDtypeStruct((rows, cols), F32)
    return _pallas_call(
        body, name=name, grid=(rows // tm,),
        in_specs=[pl.BlockSpec((n, tm, cols), lambda i: (0, i, 0)), row, row, row],
        out_specs=[row] * 4, out_shape=[shp] * 4,
        compiler_params=pltpu.CompilerParams(dimension_semantics=("parallel",), vmem_limit_bytes=VMEM_MID),
    )(partials, w, m, v)


_PK_LR, _PK_LI, _PK_GAINS, _PK_MISC, _PK_BR, _PK_BI, _PK_CR, _PK_CI, _PK_ROWS = 0, 1, 2, 3, 8, 24, 40, 56, 72
_PK_LDT_LANE, _PK_LOSS_LANE = D_MODEL + SSM_W, D_MODEL + SSM_W + LANES


def _pack_small(sg):
    names = ("lr", "li", "g_ffn", "g_fin", "dskip", "ldt", "loss", "br", "bi", "cr", "ci")

    def body(lr, li, gffn, gfin, dskip, ldt, loss, br, bi, cr, ci, o_ref):
        o_ref[...] = jnp.zeros_like(o_ref)
        o_ref[_PK_LR:_PK_LR + 1, :] = lr[...]
        o_ref[_PK_LI:_PK_LI + 1, :] = li[...]
        o_ref[_PK_GAINS:_PK_GAINS + 1, D_MODEL:] = gffn[...]
        o_ref[_PK_MISC:_PK_MISC + 1, :D_MODEL] = gfin[...]
        o_ref[_PK_MISC:_PK_MISC + 1, D_MODEL:D_MODEL + SSM_W] = dskip[...]
        o_ref[_PK_MISC:_PK_MISC + 1, _PK_LDT_LANE:_PK_LDT_LANE + LANES] = ldt[0:1, :]
        o_ref[_PK_MISC:_PK_MISC + 1, _PK_LOSS_LANE:_PK_LOSS_LANE + LANES] = loss[0:1, :]
        o_ref[_PK_BR:_PK_BR + SSM_CH, :] = br[...]
        o_ref[_PK_BI:_PK_BI + SSM_CH, :] = bi[...]
        o_ref[_PK_CR:_PK_CR + SSM_CH, :] = cr[...]
        o_ref[_PK_CI:_PK_CI + SSM_CH, :] = ci[...]

    return _pallas_call(body, name="pack_small", out_shape=jax.ShapeDtypeStruct((_PK_ROWS, N_STATE), F32))(
        *[sg[n] for n in names])


def _unpack_small(s, g_mix):
    unflat_b = lambda a: a.reshape(SSM_CH, SSM_GROUPS, SSM_STATE).transpose(1, 2, 0)[None]
    unflat_c = lambda a: a.reshape(SSM_CH, SSM_GROUPS, SSM_STATE).transpose(1, 0, 2)[None]
    grads = {
        "norm_mix_g": g_mix, "norm_ffn_g": s[_PK_GAINS, D_MODEL:].reshape(1, D_MODEL),
        "norm_final_g": s[_PK_MISC, :D_MODEL],
        "ssm_a_re": s[_PK_LR].reshape(1, SSM_GROUPS, SSM_STATE), "ssm_a_im": s[_PK_LI].reshape(1, SSM_GROUPS, SSM_STATE),
        "ssm_log_dt": s[_PK_MISC, _PK_LDT_LANE:_PK_LDT_LANE + SSM_GROUPS].reshape(1, SSM_GROUPS),
        "ssm_d": s[_PK_MISC, D_MODEL:D_MODEL + SSM_W].reshape(1, SSM_GROUPS, SSM_CH),
        "ssm_b_re": unflat_b(s[_PK_BR:_PK_BR + SSM_CH]), "ssm_b_im": unflat_b(s[_PK_BI:_PK_BI + SSM_CH]),
        "ssm_c_re": unflat_c(s[_PK_CR:_PK_CR + SSM_CH]), "ssm_c_im": unflat_c(s[_PK_CI:_PK_CI + SSM_CH]),
    }
    return s[_PK_MISC, _PK_LOSS_LANE], grads


def _adam_small(grads, wts, moms, vars_):
    n = len(SMALL_WEIGHTS)
    as2d = lambda a: a.reshape(1, -1) if a.ndim == 1 else a

    def body(*refs):
        ins, outs = refs[:4 * n], refs[4 * n:]
        for i in range(n):
            g, w, m, v = (ins[j * n + i][...] for j in range(4))
            outs[i][...], outs[n + i][...], outs[2 * n + i][...] = _adam_math(g, w, m, v)

    operands = [as2d(d[k]) for d in (grads, wts, moms, vars_) for k in SMALL_WEIGHTS]
    shapes = [jax.ShapeDtypeStruct(as2d(wts[k]).shape, F32) for k in SMALL_WEIGHTS] * 3
    res = _pallas_call(body, name="adam_small", out_shape=shapes,
                         compiler_params=pltpu.CompilerParams(vmem_limit_bytes=VMEM_BIG))(*operands)
    out = {}
    for j, kind in enumerate(("delta", "new_m", "new_v")):
        for i, k in enumerate(SMALL_WEIGHTS):
            out[kind, k] = res[j * n + i].reshape(wts[k].shape)
    return out


def kernel(x, norm_mix_g, w_in, ssm_a_re, ssm_a_im, ssm_log_dt, ssm_b_re, ssm_b_im, ssm_c_re, ssm_c_im, ssm_d, w_glu, w_attn_out, w_out, norm_ffn_g, w_ffn_gate, w_ffn_up, w_ffn_down, norm_final_g, loss_target, m_norm_mix_g, m_w_in, m_ssm_a_re, m_ssm_a_im, m_ssm_log_dt, m_ssm_b_re, m_ssm_b_im, m_ssm_c_re, m_ssm_c_im, m_ssm_d, m_w_glu, m_w_attn_out, m_w_out, m_norm_ffn_g, m_w_ffn_gate, m_w_ffn_up, m_w_ffn_down, m_norm_final_g, v_norm_mix_g, v_w_in, v_ssm_a_re, v_ssm_a_im, v_ssm_log_dt, v_ssm_b_re, v_ssm_b_im, v_ssm_c_re, v_ssm_c_im, v_ssm_d, v_w_glu, v_w_attn_out, v_w_out, v_norm_ffn_g, v_w_ffn_gate, v_w_ffn_up, v_w_ffn_down, v_norm_final_g):
    args = dict(locals())
    wts = {n: args[n] for n in ALL_WEIGHTS}
    moms = {n: args["m_" + n] for n in ALL_WEIGHTS}
    vars_ = {n: args["v_" + n] for n in ALL_WEIGHTS}
    n_samples = x.shape[0]
    t = n_samples * SEQ

    shards = {n: (wts[n][0] if n in ROW_SHARDED else wts[n][0].T).astype(BF16) for n in BIG_WEIGHTS}
    w_in_t = _all_gather(shards["w_in"], "allgather_w_in").reshape(IN_W, D_MODEL)

    small = {n: wts[n] for n in SMALL_WEIGHTS}
    grad_x, recv, _, g_mix_part = _local_step(x.reshape(t, D_MODEL), loss_target.reshape(t, D_MODEL), {"w_in": w_in_t},
                                              small, shards)

    results = {}
    for n in BIG_WEIGHTS:
        c, k = shards[n].shape
        w2, m2, v2 = wts[n][0], moms[n][0], vars_[n][0]
        if n in ROW_SHARDED:
            res = _adam(recv[n], w2, m2, v2, "adam_" + n, c // 2)
        else:
            parts = [recv[f"{n}:{hf}"] for hf in range(2)] if n in HALVED else [recv[n]]
            g_t = _sum_partials(parts, "sum_" + n, c // 2)
            res = _adam(g_t.T[None], w2, m2, v2, "adam_" + n, k // 2)
        for kind, a in zip(("grad", "delta", "new_m", "new_v"), res):
            results[kind, n] = a[None]

    g_mix_all = _all_gather(jnp.pad(g_mix_part, ((0, 7), (0, 0))), "allgather_g_mix")
    g_mix = _sum_partials([g_mix_all], "sum_g_mix", 8)[0:1]
    loss, sgrads = _unpack_small(_sum_partials([recv["small"]], "sum_small", _PK_ROWS), g_mix)
    for n in SMALL_WEIGHTS:
        results["grad", n] = sgrads[n]
    results.update(_adam_small(sgrads, wts, moms, vars_))
    outs = [loss, grad_x.reshape(x.shape)]
    for kind in ("grad", "delta", "new_m", "new_v"):
        outs += [results[kind, n] for n in ALL_WEIGHTS]
    return tuple(outs)
```

```python
import functools
import math

import jax
import jax.numpy as jnp
from jax import lax
from jax.experimental import pallas as pl
from jax.experimental.pallas import tpu as pltpu

F32 = jnp.float32
BF16 = jnp.bfloat16
MXU_DTYPE = jnp.bfloat16

N_DEV = 8
D_MODEL = 1024
SEQ = 2048
HEAD_DIM = 64
HEADS_PER_GROUP = 4
GROUP_W = HEADS_PER_GROUP * HEAD_DIM
DILATIONS = (1, 4, 16)
QKV_W = 3 * len(DILATIONS) * GROUP_W
Q_W = len(DILATIONS) * GROUP_W
ATT_BLOCK = 128
ROPE_DIM = 16
ROPE_THETA = 500000.0
SSM_W = 512
SSM_GROUPS = 32
SSM_CH = 16
SSM_STATE = 64
N_STATE = SSM_GROUPS * SSM_STATE
D_FF = 2816
IN_W = QKV_W + SSM_W + 2 * D_MODEL
RMS_EPS = 1e-6
NEG_INF = -1e30
LANES = 128

SCAN_SEG_PER_SAMPLE = 8
SCAN_LEN = SEQ // SCAN_SEG_PER_SAMPLE
SCAN_WC = 512
SCAN_NBLK = N_STATE // SCAN_WC
SCAN_CH = SSM_W // SCAN_NBLK
SCAN_CHUNK = 32

ADAM_LR = 0.001
ADAM_B1 = 0.9
ADAM_B2 = 0.999
ADAM_EPS = 1e-08
ADAM_WD = 0.01
ADAM_STEP = 10

VMEM_BIG = 48 * 1024 * 1024
VMEM_MID = 32 * 1024 * 1024

BIG_WEIGHTS = ("w_in", "w_glu", "w_attn_out", "w_out", "w_ffn_gate", "w_ffn_up", "w_ffn_down")
ROW_SHARDED = ("w_out", "w_ffn_down")
SMALL_WEIGHTS = ("norm_mix_g", "ssm_a_re", "ssm_a_im", "ssm_log_dt", "ssm_b_re", "ssm_b_im", "ssm_c_re", "ssm_c_im",
                 "ssm_d", "norm_ffn_g", "norm_final_g")
ALL_WEIGHTS = ("norm_mix_g", "w_in", "ssm_a_re", "ssm_a_im", "ssm_log_dt", "ssm_b_re", "ssm_b_im", "ssm_c_re", "ssm_c_im",
               "ssm_d", "w_glu", "w_attn_out", "w_out", "norm_ffn_g", "w_ffn_gate", "w_ffn_up", "w_ffn_down", "norm_final_g")


def _sigmoid(x):
    return 1.0 / (1.0 + jnp.exp(-x))


def _pallas_call(body, *, out_shape, **kw):
    single = not isinstance(out_shape, (list, tuple))
    shapes = [pltpu.HBM(s.shape, s.dtype) for s in ([out_shape] if single else out_shape)]
    call = pl.pallas_call(body, out_shape=shapes[0] if single else shapes, **kw)
    return lambda *operands: call(*[pltpu.with_memory_space_constraint(o, pltpu.HBM) for o in operands])


class _Comm:
    def __init__(self, ins, out_shapes, n_sem, n_local, start, finish):
        self.ins, self.out_shapes, self.n_sem, self.n_local = ins, out_shapes, n_sem, n_local
        self.start, self.finish = start, finish


def _mm(a, b, mode, name, tm, tn, out_dtype=F32, add=None, vmem=VMEM_BIG, comm=None, cols=None):
    if mode == "nn":
        (m, k), (_, n) = a.shape, b.shape
        a_spec = pl.BlockSpec((tm, k), lambda i, j: (i, 0))
        b_spec = pl.BlockSpec((k, tn), lambda i, j: (0, j))
        dims = (((1,), (0,)), ((), ()))
    elif mode == "nt":
        (m, k), (n, _) = a.shape, b.shape
        a_spec = pl.BlockSpec((tm, k), lambda i, j: (i, 0))
        b_spec = pl.BlockSpec((tn, k), lambda i, j: (j, 0))
        dims = (((1,), (1,)), ((), ()))
    else:
        (k, m), (_, n) = a.shape, b.shape
        first, n = cols if cols else (0, n)
        a_spec = pl.BlockSpec((k, tm), lambda i, j: (0, i))
        b_spec = pl.BlockSpec((k, tn), lambda i, j: (0, j + first // tn))
        dims = (((0,), (0,)), ((), ()))
    assert m % tm == 0 and n % tn == 0, (name, m, n, tm, tn)
    o_spec = pl.BlockSpec((tm, tn), lambda i, j: (i, j))
    has_add = add is not None

    def body(*refs):
        a_ref, b_ref, o_ref = refs[0], refs[1], refs[-1]
        acc = lax.dot_general(a_ref[...].astype(MXU_DTYPE), b_ref[...].astype(MXU_DTYPE), dims,
                              preferred_element_type=F32)
        if has_add:
            acc = acc + refs[2][...]
        o_ref[...] = acc.astype(out_dtype)

    ins = [a, b] + ([add] if has_add else [])
    in_specs = [a_spec, b_spec] + ([o_spec] if has_add else [])
    return _grid_call(body, name, (m // tm, n // tn), ins, in_specs, [o_spec],
                      [jax.ShapeDtypeStruct((m, n), out_dtype)], vmem, comm)


def _grid_call(body, name, grid, ins, in_specs, out_specs, out_shapes, vmem, comm=None, sequential=False, scratch=()):
    if comm is None:
        single = len(out_shapes) == 1
        semantics = ("arbitrary", "arbitrary") if sequential else ("parallel", "parallel")
        return _pallas_call(
            body, name=name, grid=grid, in_specs=in_specs, out_specs=out_specs[0] if single else out_specs,
            out_shape=out_shapes[0] if single else out_shapes, scratch_shapes=list(scratch),
            compiler_params=pltpu.CompilerParams(dimension_semantics=semantics, vmem_limit_bytes=vmem),
        )(*ins)
    n_in, n_out, n_cin, n_cout = len(ins), len(out_shapes), len(comm.ins), len(comm.out_shapes)
    n_io = n_in + n_cin + n_out + n_cout

    def carrying(*refs):
        own = refs[:n_in] + refs[n_in + n_cin:n_in + n_cin + n_out] + refs[n_io:len(refs) - 3]
        c_args = (refs[n_in:n_in + n_cin], refs[n_in + n_cin + n_out:n_io], *refs[-3:])

        @pl.when((pl.program_id(0) == 0) & (pl.program_id(1) == 0))
        def _():
            comm.start(*c_args)

        body(*own)

        @pl.when((pl.program_id(0) == grid[0] - 1) & (pl.program_id(1) == grid[1] - 1))
        def _():
            comm.finish(*c_args)

    hbm = pl.BlockSpec(memory_space=pl.ANY)
    return _pallas_call(
        carrying, name=name, grid=grid, in_specs=list(in_specs) + [hbm] * n_cin,
        out_specs=list(out_specs) + [hbm] * n_cout, out_shape=list(out_shapes) + list(comm.out_shapes),
        scratch_shapes=list(scratch) + [pltpu.SemaphoreType.DMA((comm.n_sem,)), pltpu.SemaphoreType.DMA((comm.n_sem,)),
                                        pltpu.SemaphoreType.DMA((comm.n_local,))],
        compiler_params=pltpu.CompilerParams(dimension_semantics=("arbitrary", "arbitrary"), vmem_limit_bytes=vmem),
    )(*ins, *comm.ins)


def _rows(body, name, n_rows, tm, ins, outs, vmem=VMEM_MID, scratch=()):
    assert n_rows % tm == 0
    arrays, in_specs = [], []
    for kind, arr in ins:
        arrays.append(arr)
        if kind == "row":
            assert n_rows % arr.shape[0] == 0, (name, arr.shape)
            in_specs.append(pl.BlockSpec((tm * arr.shape[0] // n_rows, arr.shape[1]), lambda i: (i, 0)))
        elif kind == "tab":
            nblk = arr.shape[0] // tm
            in_specs.append(pl.BlockSpec((tm, arr.shape[1]), lambda i, nblk=nblk: (i % nblk, 0)))
        else:
            in_specs.append(pl.BlockSpec(arr.shape, lambda i, nd=arr.ndim: (0,) * nd))
    out_specs, out_shape = [], []
    for kind, shp, dt in outs:
        if kind == "row":
            out_specs.append(pl.BlockSpec((tm, shp), lambda i: (i, 0)))
            out_shape.append(jax.ShapeDtypeStruct((n_rows, shp), dt))
        elif kind == "dil":
            d, wd = shp
            out_specs.append(pl.BlockSpec((tm // d, d * wd), lambda i: (i, 0)))
            out_shape.append(jax.ShapeDtypeStruct((n_rows // d, d * wd), dt))
        else:
            out_specs.append(pl.BlockSpec(shp, lambda i, nd=len(shp): (0,) * nd))
            out_shape.append(jax.ShapeDtypeStruct(shp, dt))
    res = _pallas_call(
        body, name=name, grid=(n_rows // tm,), in_specs=in_specs, out_specs=out_specs, out_shape=out_shape,
        scratch_shapes=list(scratch),
        compiler_params=pltpu.CompilerParams(dimension_semantics=("arbitrary",), vmem_limit_bytes=vmem),
    )(*arrays)
    return res


def _gather_residue(stage, ch, r, d, n):
    return stage[ch, pl.ds(r, n, stride=d), :] if d > 1 else stage[ch]


def _scatter_residue(stage, ch, r, d, n, val):
    if d > 1:
        stage[ch, pl.ds(r, n, stride=d), :] = val
    else:
        stage[ch] = val


def _lane_chunk(ch):
    return slice(ch * LANES, (ch + 1) * LANES)


def _first_step():
    return pl.program_id(0) == 0


def _rope_tables():
    half = ROPE_DIM // 2
    inv = jnp.power(jnp.float32(ROPE_THETA), -jnp.arange(half, dtype=F32) * 2.0 / ROPE_DIM)
    ang = jnp.arange(SEQ, dtype=F32)[:, None] * inv[None, :]
    lane = jnp.arange(LANES) % HEAD_DIM
    cosl = jnp.cos(ang)[:, lane % half]
    sinl = jnp.sin(ang)[:, lane % half]
    tab_c = jnp.where(lane < ROPE_DIM, cosl, 1.0)
    tab_lo = jnp.where(lane < half, -sinl, 0.0)
    tab_hi = jnp.where((lane >= half) & (lane < ROPE_DIM), sinl, 0.0)
    return tab_c.astype(F32), tab_lo.astype(F32), tab_hi.astype(F32)


def _rope_apply(t, tc, tlo, thi):
    half = ROPE_DIM // 2
    return t * tc + pltpu.roll(t, LANES - half, 1) * tlo + pltpu.roll(t, half, 1) * thi


def _rope_transpose(dt, tc, tlo, thi):
    half = ROPE_DIM // 2
    return dt * tc + pltpu.roll(dt * tlo, half, 1) + pltpu.roll(dt * thi, LANES - half, 1)


def _pack_dproj(dqs, dks, dvs, du, dgpre, tabs):
    tm = 256

    def body(*refs):
        dq_refs, dk_refs, dv_refs = refs[0:3], refs[3:6], refs[6:9]
        du_ref, dg_ref, tc_ref, tlo_ref, thi_ref, o_ref, stage = refs[9:16]
        n_ch = QKV_W // LANES
        halves = GROUP_W // LANES
        for grp, d in enumerate(DILATIONS):
            for which, src in enumerate((dq_refs[grp], dk_refs[grp], dv_refs[grp])):
                for res in range(d):
                    for half in range(halves):
                        _scatter_residue(stage, which * (n_ch // 3) + grp * halves + half, res, d, tm // d,
                                         src[:, _lane_chunk(res * halves + half)])
        tc, tlo, thi = tc_ref[...], tlo_ref[...], thi_ref[...]
        for ch in range(n_ch):
            piece = stage[ch]
            o_ref[:, _lane_chunk(ch)] = (_rope_transpose(piece, tc, tlo, thi) if ch < 2 * n_ch // 3 else piece).astype(BF16)
        o_ref[:, QKV_W:QKV_W + SSM_W] = du_ref[...].astype(BF16)
        o_ref[:, QKV_W + SSM_W:] = dg_ref[...].astype(BF16)

    t = du.shape[0]
    ins = [("row", a) for a in (*dqs, *dks, *dvs, du, dgpre)] + [("tab", tb) for tb in tabs]
    return _rows(body, "pack_dproj", t, tm, ins, [("row", IN_W, BF16)],
                 scratch=[pltpu.VMEM((QKV_W // LANES, tm, LANES), F32)])[0]


def _attn_merge(os_, lses):
    tm = 256

    halves = GROUP_W // LANES

    def body(o0, o1, o2, l0, l1, l2, a_ref, lt_ref, nat):
        for grp, d in enumerate(DILATIONS[1:], start=1):
            for j, src in enumerate(((o0, o1, o2)[grp], (l0, l1, l2)[grp])):
                for res in range(d):
                    for half in range(halves):
                        _scatter_residue(nat, (grp - 1) * 4 + j * 2 + half, res, d, tm // d,
                                         src[:, _lane_chunk(res * halves + half)])
        for half in range(halves):
            sl = _lane_chunk(half)
            la, lb, lc = l0[:, sl], nat[2 + half], nat[6 + half]
            m = jnp.maximum(jnp.maximum(la, lb), lc)
            ea, eb, ec = jnp.exp(la - m), jnp.exp(lb - m), jnp.exp(lc - m)
            ssum = ea + eb + ec
            a_ref[:, sl] = (ea / ssum) * o0[:, sl] + (eb / ssum) * nat[half] + (ec / ssum) * nat[4 + half]
            lt_ref[:, sl] = m + jnp.log(ssum)

    t = os_[0].shape[0]
    return _rows(body, "attn_merge", t, tm, [("row", a) for a in (*os_, *lses)],
                 [("row", GROUP_W, F32), ("row", GROUP_W, F32)], scratch=[pltpu.VMEM((8, tm, LANES), F32)])


def _head_sum_matrix():
    r = jnp.arange(GROUP_W) // HEAD_DIM
    return (r[:, None] == r[None, :]).astype(F32)


def _attn_rowdot(dattn, attn, lse_tot):
    tm = 256

    halves = GROUP_W // LANES

    def body(da_ref, a_ref, lt_ref, ones_ref, rd_ref, *rest):
        dil, stage = rest[:6], rest[6]
        rd = jnp.dot(da_ref[...] * a_ref[...], ones_ref[...], preferred_element_type=F32, precision=lax.Precision.HIGHEST)
        rd_ref[...] = rd
        for half in range(halves):
            stage[half] = da_ref[:, _lane_chunk(half)]
            stage[2 + half] = lt_ref[:, _lane_chunk(half)]
            stage[4 + half] = rd[:, _lane_chunk(half)]
        for grp, d in enumerate(DILATIONS[1:], start=1):
            for j in range(3):
                for res in range(d):
                    for half in range(halves):
                        dil[3 * (grp - 1) + j][:, _lane_chunk(res * halves + half)] = _gather_residue(
                            stage, 2 * j + half, res, d, tm // d)

    t = attn.shape[0]
    outs = [("row", GROUP_W, F32)] + [("dil", (d, GROUP_W), F32) for d in DILATIONS[1:] for _ in range(3)]
    rd, *dil = _rows(body, "attn_rowdot", t, tm,
                     [("row", dattn), ("row", attn), ("row", lse_tot), ("const", _head_sum_matrix())], outs,
                     scratch=[pltpu.VMEM((6, tm, LANES), F32)])
    return [(dattn, lse_tot, rd), tuple(dil[:3]), tuple(dil[3:])]


def _mix(attn_d, z, gates):
    def body(ad_ref, z_ref, g_ref, m_ref):
        za, zb = z_ref[:, :D_MODEL], z_ref[:, D_MODEL:]
        s_out = za * _sigmoid(zb)
        m_ref[...] = (g_ref[:, :D_MODEL] * ad_ref[...] + g_ref[:, D_MODEL:] * s_out).astype(BF16)

    t = attn_d.shape[0]
    return _rows(body, "mix", t, 256, [("row", attn_d), ("row", z), ("row", gates)], [("row", D_MODEL, BF16)])[0]


def _mix_bwd(dmerged, gates, attn_d, z):
    def body(dm_ref, g_ref, ad_ref, z_ref, dad_ref, dz_ref, dg_ref):
        dm = dm_ref[...]
        g0, g1 = g_ref[:, :D_MODEL], g_ref[:, D_MODEL:]
        za, zb = z_ref[:, :D_MODEL], z_ref[:, D_MODEL:]
        sb = _sigmoid(zb)
        s_out = za * sb
        dad_ref[...] = (dm * g0).astype(BF16)
        ds = dm * g1
        dz_ref[:, :D_MODEL] = (ds * sb).astype(BF16)
        dz_ref[:, D_MODEL:] = (ds * za * sb * (1.0 - sb)).astype(BF16)
        dg_ref[:, :D_MODEL] = (dm * ad_ref[...] * g0 * (1.0 - g0)).astype(BF16)
        dg_ref[:, D_MODEL:] = (dm * s_out * g1 * (1.0 - g1)).astype(BF16)

    t = dmerged.shape[0]
    return _rows(body, "mix_bwd", t, 256, [("row", dmerged), ("row", gates), ("row", attn_d), ("row", z)],
                 [("row", D_MODEL, BF16), ("row", 2 * D_MODEL, BF16), ("row", 2 * D_MODEL, BF16)])


_GELU_C = math.sqrt(2.0 / math.pi)


def _ssm_act_bwd(dyg, ytot, u_perm, dskip):
    def body(dyg_ref, yt_ref, u_ref, d_ref, dy_ref, dus_ref, dd_ref):
        @pl.when(_first_step())
        def _():
            dd_ref[...] = jnp.zeros_like(dd_ref)

        yt = yt_ref[...]
        th = jnp.tanh(_GELU_C * (yt + 0.044715 * (yt * yt * yt)))
        dgelu = 0.5 * (1.0 + th) + 0.5 * yt * (1.0 - th * th) * _GELU_C * (1.0 + 3.0 * 0.044715 * yt * yt)
        dy = dyg_ref[...] * dgelu
        dy_ref[...] = dy.astype(BF16)
        dus_ref[...] = dy * d_ref[...]
        dd_ref[...] += jnp.sum(dy * u_ref[...], axis=0, keepdims=True)

    t = dyg.shape[0]
    return _rows(body, "ssm_act_bwd", t, 512, [("row", dyg), ("row", ytot), ("row", u_perm), ("const", dskip)],
                 [("row", SSM_W, BF16), ("row", SSM_W, F32), ("acc", (1, SSM_W), F32)])


def _head_masks():
    lane = lax.broadcasted_iota(jnp.int32, (1, GROUP_W), 1)
    return [(lane // HEAD_DIM) == h for h in range(HEADS_PER_GROUP)]


def _stack_heads(blk, masks, fill=0.0):
    return jnp.concatenate([jnp.where(mk, blk, jnp.full_like(blk, fill)) for mk in masks], axis=0)


def _unstack_heads(stacked, masks):
    rows = stacked.shape[0] // len(masks)
    out = stacked[:rows]
    for h in range(1, len(masks)):
        out = jnp.where(masks[h], stacked[h * rows:(h + 1) * rows], out)
    return out


def _band_mask(first):
    nk = ATT_BLOCK if first else 2 * ATT_BLOCK
    qi = lax.broadcasted_iota(jnp.int32, (ATT_BLOCK, nk), 0)
    ki = lax.broadcasted_iota(jnp.int32, (ATT_BLOCK, nk), 1)
    dist = qi - ki + (0 if first else ATT_BLOCK)
    return (dist >= 0) & (dist <= ATT_BLOCK)


_NT = (((1,), (1,)), ((), ()))
_TN = (((0,), (0,)), ((), ()))


def _attn_fwd(q, k, v, group, n_samples, comm=None):
    d = DILATIONS[group]
    length = SEQ // d
    nb = length // ATT_BLOCK

    def body(q_ref, k_ref, v_ref, o_ref, l_ref):
        masks = _head_masks()

        def block(qs, ks, first):
            nk = ATT_BLOCK if first else 2 * ATT_BLOCK
            qb = q_ref[0, pl.ds(qs, ATT_BLOCK), :]
            kc = k_ref[0, pl.ds(ks, nk), :]
            vc = v_ref[0, pl.ds(ks, nk), :]
            q4 = _stack_heads(qb, masks)
            valid = jnp.tile(_band_mask(first), (HEADS_PER_GROUP, 1))
            s = lax.dot_general(q4, kc, _NT, preferred_element_type=F32) * (HEAD_DIM ** -0.5)
            s = jnp.where(valid, s, NEG_INF)
            m = jnp.max(s, axis=-1, keepdims=True)
            p = jnp.exp(s - m)
            l = jnp.sum(p, axis=-1, keepdims=True)
            o4 = jnp.dot(p.astype(MXU_DTYPE), vc, preferred_element_type=F32) / l
            lse4 = jnp.broadcast_to(m + jnp.log(l), o4.shape)
            o_ref[0, pl.ds(qs, ATT_BLOCK), :] = _unstack_heads(o4, masks)
            l_ref[0, pl.ds(qs, ATT_BLOCK), :] = _unstack_heads(lse4, masks)

        block(0, 0, True)
        if nb > 1:
            def loop(n, carry):
                block(pl.multiple_of(n * ATT_BLOCK, ATT_BLOCK), pl.multiple_of((n - 1) * ATT_BLOCK, ATT_BLOCK), False)
                return carry

            lax.fori_loop(1, nb, loop, 0)

    per_sample = lambda a: a.reshape(n_samples, length, d * GROUP_W)
    spec = pl.BlockSpec((1, length, GROUP_W), lambda b, r: (b, 0, r))
    shp = jax.ShapeDtypeStruct((n_samples, length, d * GROUP_W), F32)
    o, lse, *carried = _grid_call(body, f"attn_fwd_g{group}", (n_samples, d), [per_sample(a) for a in (q, k, v)],
                                  [spec] * 3, [spec] * 2, [shp, shp], VMEM_MID, comm)
    flat = lambda a: a.reshape(n_samples * length, d * GROUP_W)
    return flat(o), flat(lse), carried


def _attn_bwd(q, k, v, dattn, lse_tot, rowdot, group, n_samples, comm=None):
    d = DILATIONS[group]
    length = SEQ // d
    nb = length // ATT_BLOCK

    def body(q_ref, k_ref, v_ref, da_ref, lt_ref, rd_ref, dq_ref, dk_ref, dv_ref):
        masks = _head_masks()
        dk_ref[...] = jnp.zeros_like(dk_ref)
        dv_ref[...] = jnp.zeros_like(dv_ref)

        def block(qs, ks, first):
            nk = ATT_BLOCK if first else 2 * ATT_BLOCK
            qb = q_ref[0, pl.ds(qs, ATT_BLOCK), :]
            kc = k_ref[0, pl.ds(ks, nk), :]
            vc = v_ref[0, pl.ds(ks, nk), :]
            da = da_ref[0, pl.ds(qs, ATT_BLOCK), :]
            lt = lt_ref[0, pl.ds(qs, ATT_BLOCK), :]
            rd = rd_ref[0, pl.ds(qs, ATT_BLOCK), :]
            q4 = _stack_heads(qb, masks)
            da4 = _stack_heads(da, masks).astype(MXU_DTYPE)
            lt4 = jnp.max(_stack_heads(lt, masks, -jnp.inf), axis=-1, keepdims=True)
            rd4 = jnp.max(_stack_heads(rd, masks, -jnp.inf), axis=-1, keepdims=True)
            valid = jnp.tile(_band_mask(first), (HEADS_PER_GROUP, 1))
            s = lax.dot_general(q4, kc, _NT, preferred_element_type=F32) * (HEAD_DIM ** -0.5)
            s = jnp.where(valid, s, NEG_INF)
            p = jnp.exp(s - lt4)
            dp = lax.dot_general(da4, vc, _NT, preferred_element_type=F32)
            ds = (p * (dp - rd4) * (HEAD_DIM ** -0.5)).astype(MXU_DTYPE)
            dq_ref[0, pl.ds(qs, ATT_BLOCK), :] = _unstack_heads(jnp.dot(ds, kc, preferred_element_type=F32), masks)
            dk_ref[0, pl.ds(ks, nk), :] += lax.dot_general(ds, q4, _TN, preferred_element_type=F32)
            dv_ref[0, pl.ds(ks, nk), :] += lax.dot_general(p.astype(MXU_DTYPE), da4, _TN, preferred_element_type=F32)

        block(0, 0, True)
        if nb > 1:
            def loop(n, carry):
                block(pl.multiple_of(n * ATT_BLOCK, ATT_BLOCK), pl.multiple_of((n - 1) * ATT_BLOCK, ATT_BLOCK), False)
                return carry

            lax.fori_loop(1, nb, loop, 0)

    per_sample = lambda a: a.reshape(n_samples, length, d * GROUP_W)
    spec = pl.BlockSpec((1, length, GROUP_W), lambda b, r: (b, 0, r))
    shp = jax.ShapeDtypeStruct((n_samples, length, d * GROUP_W), F32)
    dq, dk, dv, *carried = _grid_call(
        body, f"attn_bwd_g{group}", (n_samples, d), [per_sample(a) for a in (q, k, v, dattn, lse_tot, rowdot)],
        [spec] * 6, [spec] * 3, [shp, shp, shp], VMEM_MID, comm)
    flat = lambda a: a.reshape(n_samples * length, d * GROUP_W)
    return flat(dq), flat(dk), flat(dv), carried


def _disc(lr, li, ldt, br, bi):
    dt = jnp.exp(ldt)
    mag = jnp.exp(lr * dt)
    ab_re, ab_im = mag * jnp.cos(li * dt), mag * jnp.sin(li * dt)
    den = lr * lr + li * li
    nr, ni = ab_re - 1.0, ab_im
    f_re = (nr * lr + ni * li) / den
    f_im = (ni * lr - nr * li) / den
    return ab_re, ab_im, f_re * br - f_im * bi, f_re * bi + f_im * br


def _state_mask():
    row_g = lax.broadcasted_iota(jnp.int32, (SCAN_CH, SCAN_WC), 0) // SSM_CH
    col_g = lax.broadcasted_iota(jnp.int32, (SCAN_CH, SCAN_WC), 1) // SSM_STATE
    return row_g == col_g


def _ssm_disc(lr, li, ldt, br, bi, cr, ci):
    w = SCAN_WC

    def body(lr_ref, li_ref, ldt_ref, br_ref, bi_ref, cr_ref, ci_ref, a_ref, bb_ref, c_ref):
        ar, ai, bbr, bbi = _disc(lr_ref[...], li_ref[...], ldt_ref[...], br_ref[...], bi_ref[...])
        crv, civ = cr_ref[...], ci_ref[...]
        mask = _state_mask()
        for cb in range(SCAN_NBLK):
            sl = slice(cb * w, (cb + 1) * w)
            rows = slice(cb * SCAN_CH, (cb + 1) * SCAN_CH)
            dense = lambda comp: jnp.where(mask, jnp.tile(comp[:, sl], (SCAN_CH // SSM_CH, 1)), 0.0)
            a_ref[:, 2 * cb * w:(2 * cb + 1) * w] = ar[:, sl]
            a_ref[:, (2 * cb + 1) * w:(2 * cb + 2) * w] = ai[:, sl]
            bb_ref[rows, :w] = dense(bbr).astype(MXU_DTYPE)
            bb_ref[rows, w:] = dense(bbi).astype(MXU_DTYPE)
            c_ref[rows, :w] = dense(crv).astype(MXU_DTYPE)
            c_ref[rows, w:] = (-dense(civ)).astype(MXU_DTYPE)

    return _pallas_call(
        body, name="ssm_disc",
        out_shape=[jax.ShapeDtypeStruct((1, 2 * N_STATE), F32), jax.ShapeDtypeStruct((SSM_W, 2 * w), MXU_DTYPE),
                   jax.ShapeDtypeStruct((SSM_W, 2 * w), MXU_DTYPE)],
        compiler_params=pltpu.CompilerParams(vmem_limit_bytes=VMEM_MID),
    )(lr, li, ldt, br, bi, cr, ci)


def _group_indicator():
    s = jnp.arange(N_STATE) // SSM_STATE
    return (s[:, None] == jnp.arange(LANES)[None, :]).astype(F32)


def _ssm_param_bwd(lr, li, ldt, br, bi, da_cat, dbb_full, dc_full):
    w = SCAN_WC

    def body(lr_ref, li_ref, ldt_ref, br_ref, bi_ref, da_ref, dbb_ref, dc_ref, ind_ref,
             glr_ref, gli_ref, gldt_ref, gbr_ref, gbi_ref, gcr_ref, gci_ref):
        mask = _state_mask()

        def diag_parts(ref):
            res = ([], [])
            for cb in range(SCAN_NBLK):
                for part in range(2):
                    blk = ref[cb * SCAN_CH:(cb + 1) * SCAN_CH, part * w:(part + 1) * w]
                    res[part].append(jnp.sum(jnp.where(mask, blk, 0.0).reshape(SCAN_CH // SSM_CH, SSM_CH, w), axis=0))
            return jnp.concatenate(res[0], axis=1), jnp.concatenate(res[1], axis=1)

        dar = jnp.concatenate([da_ref[:, 2 * cb * w:(2 * cb + 1) * w] for cb in range(SCAN_NBLK)], axis=1)
        dai = jnp.concatenate([da_ref[:, (2 * cb + 1) * w:(2 * cb + 2) * w] for cb in range(SCAN_NBLK)], axis=1)
        dbbr, dbbi = diag_parts(dbb_ref)
        dcr, dci_neg = diag_parts(dc_ref)
        gcr_ref[...] = dcr
        gci_ref[...] = -dci_neg
        _, vjp = jax.vjp(_disc, lr_ref[...], li_ref[...], ldt_ref[...], br_ref[...], bi_ref[...])
        glr, gli, gldt, gbr, gbi = vjp((dar, dai, dbbr, dbbi))
        glr_ref[...] = glr
        gli_ref[...] = gli
        gldt_ref[...] = jnp.dot(jnp.broadcast_to(gldt, (8, N_STATE)), ind_ref[...], preferred_element_type=F32,
                                precision=lax.Precision.HIGHEST)
        gbr_ref[...] = gbr
        gbi_ref[...] = gbi

    v1 = jax.ShapeDtypeStruct((1, N_STATE), F32)
    v16 = jax.ShapeDtypeStruct((SSM_CH, N_STATE), F32)
    vdt = jax.ShapeDtypeStruct((8, LANES), F32)
    return _pallas_call(
        body, name="ssm_param_bwd", out_shape=[v1, v1, vdt, v16, v16, v16, v16],
        compiler_params=pltpu.CompilerParams(vmem_limit_bytes=VMEM_BIG),
    )(lr, li, ldt, br, bi, da_cat, dbb_full, dc_full, _group_indicator())


def _cmul(ar, ai, br, bi):
    return ar * br - ai * bi, ar * bi + ai * br


def _gelu_tanh(y):
    return jnp.tanh(_GELU_C * (y + 0.044715 * (y * y * y)))


def _segment_carry(er, ei, ar, ai, n_rows, reverse):
    qr, qi = ar, ai
    for _ in range(int(math.log2(SCAN_LEN))):
        qr, qi = _cmul(qr, qi, qr, qi)
    seg = lax.broadcasted_iota(jnp.int32, er.shape, 0) % SCAN_SEG_PER_SAMPLE
    shift = 1
    while shift < SCAN_SEG_PER_SAMPLE:
        keep = (seg < SCAN_SEG_PER_SAMPLE - shift) if reverse else (seg >= shift)
        amount = n_rows - shift if reverse else shift
        sr = jnp.where(keep, pltpu.roll(er, amount, 0), 0.0)
        si = jnp.where(keep, pltpu.roll(ei, amount, 0), 0.0)
        if reverse:
            er, ei = er + qr * sr + qi * si, ei + qr * si - qi * sr
        else:
            er, ei = er + qr * sr - qi * si, ei + qr * si + qi * sr
        qr, qi = _cmul(qr, qi, qr, qi)
        shift *= 2
    keep = (seg < SCAN_SEG_PER_SAMPLE - 1) if reverse else (seg >= 1)
    amount = n_rows - 1 if reverse else 1
    return jnp.where(keep, pltpu.roll(er, amount, 0), 0.0), jnp.where(keep, pltpu.roll(ei, amount, 0), 0.0)


def _ssm_fwd(u_perm, a_cat, bbc, cc, dskip, n_rows):
    t = u_perm.shape[0]
    w = SCAN_WC
    rows_c = SCAN_CHUNK * n_rows
    n_chunks = t // rows_c

    def body(u_ref, a_ref, bb_ref, c_ref, d_ref, yt_ref, yg_ref, ein_ref, bu_s, xs_s):
        ar = jnp.broadcast_to(a_ref[:, :w], (n_rows, w))
        ai = jnp.broadcast_to(a_ref[:, w:], (n_rows, w))

        def sweep(carry, store):
            def chunk(ch, carry):
                r0 = pl.multiple_of(ch * rows_c, rows_c)
                u_c = u_ref[pl.ds(r0, rows_c), :]
                if not store:
                    bu_s[pl.ds(r0, rows_c), :] = jnp.dot(u_c.astype(MXU_DTYPE), bb_ref[...], preferred_element_type=F32)

                def step(i, c):
                    o = pl.multiple_of(i * n_rows, n_rows)
                    blk = bu_s[pl.ds(r0 + o, n_rows), :]
                    nr = ar * c[0] - ai * c[1] + blk[:, :w]
                    ni = ar * c[1] + ai * c[0] + blk[:, w:]
                    if store:
                        xs_s[pl.ds(o, n_rows), :w] = nr
                        xs_s[pl.ds(o, n_rows), w:] = ni
                    return nr, ni

                carry = lax.fori_loop(0, SCAN_CHUNK, step, carry)
                if store:
                    y = lax.dot_general(xs_s[...].astype(MXU_DTYPE), c_ref[...], _NT, preferred_element_type=F32)
                    yt = y + d_ref[...] * u_c
                    yt_ref[pl.ds(r0, rows_c), :] = yt
                    yg_ref[pl.ds(r0, rows_c), :] = (0.5 * yt * (1.0 + _gelu_tanh(yt))).astype(BF16)
                return carry

            return lax.fori_loop(0, n_chunks, chunk, carry)

        zero = jnp.zeros((n_rows, w), F32)
        er, ei = sweep((zero, zero), False)
        cr, ci = _segment_carry(er, ei, ar, ai, n_rows, False)
        ein_ref[:, :w] = cr
        ein_ref[:, w:] = ci
        sweep((cr, ci), True)

    col = lambda width: pl.BlockSpec((t, width), lambda c: (0, c))
    wgt = pl.BlockSpec((SCAN_CH, 2 * w), lambda c: (c, 0))
    return _pallas_call(
        body, name="ssm_fwd", grid=(SCAN_NBLK,),
        in_specs=[col(SCAN_CH), pl.BlockSpec((1, 2 * w), lambda c: (0, c)), wgt, wgt,
                  pl.BlockSpec((1, SCAN_CH), lambda c: (0, c))],
        out_specs=[col(SCAN_CH), col(SCAN_CH), pl.BlockSpec((n_rows, 2 * w), lambda c: (0, c))],
        out_shape=[jax.ShapeDtypeStruct((t, SSM_W), F32), jax.ShapeDtypeStruct((t, SSM_W), BF16),
                   jax.ShapeDtypeStruct((n_rows, 2 * N_STATE), F32)],
        scratch_shapes=[pltpu.VMEM((t, 2 * w), F32), pltpu.VMEM((rows_c, 2 * w), F32)],
        compiler_params=pltpu.CompilerParams(dimension_semantics=("parallel",), vmem_limit_bytes=VMEM_BIG),
    )(u_perm, a_cat, bbc, cc, dskip)


def _ssm_bwd(u_perm, dypre, du_skip, a_cat, bbc, cc, ein, n_rows, comm=None):
    t = u_perm.shape[0]
    w = SCAN_WC
    rows_c = SCAN_CHUNK * n_rows
    n_chunks = t // rows_c

    def body(u_ref, dy_ref, dus_ref, a_ref, bb_ref, c_ref, ein_ref, du_ref, da_ref, dbb_ref, dc_ref, xs_all, tmp_s, g_s):
        ar = jnp.broadcast_to(a_ref[:, :w], (n_rows, w))
        ai = jnp.broadcast_to(a_ref[:, w:], (n_rows, w))
        zero = jnp.zeros((n_rows, w), F32)

        xs_all[0:n_rows, :] = ein_ref[...]

        def fwd_chunk(ch, carry):
            r0 = pl.multiple_of(ch * rows_c, rows_c)
            tmp_s[...] = jnp.dot(u_ref[pl.ds(r0, rows_c), :].astype(MXU_DTYPE), bb_ref[...], preferred_element_type=F32)

            def step(i, c):
                o = pl.multiple_of(i * n_rows, n_rows)
                blk = tmp_s[pl.ds(o, n_rows), :]
                nr = ar * c[0] - ai * c[1] + blk[:, :w]
                ni = ar * c[1] + ai * c[0] + blk[:, w:]
                xs_all[pl.ds(n_rows + r0 + o, n_rows), :w] = nr
                xs_all[pl.ds(n_rows + r0 + o, n_rows), w:] = ni
                return nr, ni

            return lax.fori_loop(0, SCAN_CHUNK, step, carry)

        lax.fori_loop(0, n_chunks, fwd_chunk, (ein_ref[:, :w], ein_ref[:, w:]))

        def load_dx(ch):
            r0 = pl.multiple_of(ch * rows_c, rows_c)
            tmp_s[...] = jnp.dot(dy_ref[pl.ds(r0, rows_c), :], c_ref[...], preferred_element_type=F32)
            return r0

        def back_steps(carry, store):
            def step(ii, c):
                o = pl.multiple_of((SCAN_CHUNK - 1 - ii) * n_rows, n_rows)
                blk = tmp_s[pl.ds(o, n_rows), :]
                gr = blk[:, :w] + ar * c[0] + ai * c[1]
                gi = blk[:, w:] + ar * c[1] - ai * c[0]
                if store:
                    g_s[pl.ds(o, n_rows), :w] = gr
                    g_s[pl.ds(o, n_rows), w:] = gi
                return gr, gi

            return lax.fori_loop(0, SCAN_CHUNK, step, carry)

        def first_sweep(cc_, carry):
            load_dx(n_chunks - 1 - cc_)
            return back_steps(carry, False)

        sr, si = lax.fori_loop(0, n_chunks, first_sweep, (zero, zero))
        gr0, gi0 = _segment_carry(sr, si, ar, ai, n_rows, True)

        dbb_ref[...] = jnp.zeros_like(dbb_ref)
        dc_ref[...] = jnp.zeros_like(dc_ref)
        da_ref[...] = jnp.zeros_like(da_ref)

        def second_sweep(cc_, carry):
            r0 = load_dx(n_chunks - 1 - cc_)
            carry = back_steps(carry, True)
            g = g_s[...]
            xp = xs_all[pl.ds(r0, rows_c), :]
            xc = xs_all[pl.ds(r0 + n_rows, rows_c), :]
            da_ref[:, :w] += jnp.sum(g[:, :w] * xp[:, :w] + g[:, w:] * xp[:, w:], axis=0, keepdims=True)
            da_ref[:, w:] += jnp.sum(g[:, w:] * xp[:, :w] - g[:, :w] * xp[:, w:], axis=0, keepdims=True)
            gb = g.astype(MXU_DTYPE)
            du_ref[pl.ds(r0, rows_c), :] = (lax.dot_general(gb, bb_ref[...], _NT, preferred_element_type=F32)
                                            + dus_ref[pl.ds(r0, rows_c), :])
            dbb_ref[...] += lax.dot_general(u_ref[pl.ds(r0, rows_c), :].astype(MXU_DTYPE), gb, _TN,
                                            preferred_element_type=F32)
            dc_ref[...] += lax.dot_general(dy_ref[pl.ds(r0, rows_c), :], xc.astype(MXU_DTYPE), _TN,
                                           preferred_element_type=F32)
            return carry

        lax.fori_loop(0, n_chunks, second_sweep, (gr0, gi0))

    col = lambda width: pl.BlockSpec((t, width), lambda c, j: (0, c))
    wgt = pl.BlockSpec((SCAN_CH, 2 * w), lambda c, j: (c, 0))
    row = pl.BlockSpec((1, 2 * w), lambda c, j: (0, c))
    return _grid_call(
        body, "ssm_bwd", (SCAN_NBLK, 1), [u_perm, dypre, du_skip, a_cat, bbc, cc, ein],
        [col(SCAN_CH), col(SCAN_CH), col(SCAN_CH), row, wgt, wgt, pl.BlockSpec((n_rows, 2 * w), lambda c, j: (0, c))],
        [col(SCAN_CH), row, wgt, wgt],
        [jax.ShapeDtypeStruct((t, SSM_W), F32), jax.ShapeDtypeStruct((1, 2 * N_STATE), F32),
         jax.ShapeDtypeStruct((SSM_W, 2 * w), F32), jax.ShapeDtypeStruct((SSM_W, 2 * w), F32)],
        56 * 1024 * 1024, comm,
        scratch=[pltpu.VMEM((t + n_rows, 2 * w), F32), pltpu.VMEM((rows_c, 2 * w), F32), pltpu.VMEM((rows_c, 2 * w), F32)])


def _to_scan_rows(a, n_samples):
    c = a.shape[1]
    return a.reshape(n_samples, SCAN_SEG_PER_SAMPLE, SCAN_LEN, c).transpose(2, 0, 1, 3).reshape(-1, c)


def _from_scan_rows(a, n_samples):
    c = a.shape[1]
    return a.reshape(SCAN_LEN, n_samples, SCAN_SEG_PER_SAMPLE, c).transpose(1, 2, 0, 3).reshape(-1, c)


def _row_spec(tm, width):
    return pl.BlockSpec((tm, width), lambda i, j: (i, 0))


def _whole(arr):
    return pl.BlockSpec(arr.shape, lambda i, j: (0,) * arr.ndim)


def _proj_rope(x, g, w_in_t, tabs, comm=None):
    t = x.shape[0]
    tm = 256

    def body(x_ref, g_ref, w_ref, tc_ref, tlo_ref, thi_ref, h_ref, u_ref, gate_ref, *rest):
        qkv_refs, stage = rest[:9], rest[9]
        xv = x_ref[...]
        r = lax.rsqrt(jnp.mean(xv * xv, axis=-1, keepdims=True) + RMS_EPS)
        h = ((xv * r) * g_ref[...]).astype(BF16)
        h_ref[...] = h
        p = lax.dot_general(h.astype(MXU_DTYPE), w_ref[...], _NT, preferred_element_type=F32)
        u_ref[...] = p[:, QKV_W:QKV_W + SSM_W]
        gate_ref[...] = _sigmoid(p[:, QKV_W + SSM_W:])
        tc, tlo, thi = tc_ref[...], tlo_ref[...], thi_ref[...]
        n_ch = QKV_W // LANES
        for ch in range(n_ch):
            piece = p[:, _lane_chunk(ch)]
            stage[ch] = _rope_apply(piece, tc, tlo, thi) if ch < 2 * n_ch // 3 else piece
        halves = GROUP_W // LANES
        for grp, d in enumerate(DILATIONS):
            for which in range(3):
                out = qkv_refs[3 * grp + which]
                for res in range(d):
                    for half in range(halves):
                        ch = which * (n_ch // 3) + grp * halves + half
                        out[:, _lane_chunk(res * halves + half)] = _gather_residue(stage, ch, res, d, tm // d).astype(BF16)

    tab = pl.BlockSpec((tm, LANES), lambda i, j: (i % (SEQ // tm), 0))
    widths = [(D_MODEL, BF16), (SSM_W, F32), (2 * D_MODEL, F32)]
    out_specs = [_row_spec(tm, wd) for wd, _ in widths]
    out_shapes = [jax.ShapeDtypeStruct((t, wd), dt) for wd, dt in widths]
    for d in DILATIONS:
        out_specs += [_row_spec(tm // d, d * GROUP_W)] * 3
        out_shapes += [jax.ShapeDtypeStruct((t // d, d * GROUP_W), BF16)] * 3
    return _grid_call(
        body, "proj_rope", (t // tm, 1), [x, g, w_in_t, *tabs],
        [_row_spec(tm, D_MODEL), _whole(g), _whole(w_in_t), tab, tab, tab], out_specs, out_shapes, VMEM_BIG, comm,
        scratch=[pltpu.VMEM((QKV_W // LANES, tm, LANES), F32)])


def _out_rms(merged, w_out, x, g):
    t = x.shape[0]
    tm = 512

    def body(m_ref, w_ref, x_ref, g_ref, x1_ref, h_ref):
        x1 = x_ref[...] + jnp.dot(m_ref[...].astype(MXU_DTYPE), w_ref[...], preferred_element_type=F32)
        x1_ref[...] = x1
        r = lax.rsqrt(jnp.mean(x1 * x1, axis=-1, keepdims=True) + RMS_EPS)
        h_ref[...] = ((x1 * r) * g_ref[...]).astype(BF16)

    return _grid_call(
        body, "out_rms", (t // tm, 1), [merged, w_out, x, g],
        [_row_spec(tm, D_MODEL), _whole(w_out), _row_spec(tm, D_MODEL), _whole(g)],
        [_row_spec(tm, D_MODEL)] * 2, [jax.ShapeDtypeStruct((t, D_MODEL), F32), jax.ShapeDtypeStruct((t, D_MODEL), BF16)],
        VMEM_BIG)


FFN_TN = D_FF // 2
MXU_COLS = 256


def _ffn_in_swiglu(h2, w_gate_t, w_up_t, comm=None):
    t = h2.shape[0]
    tm = 512

    def body(h_ref, wg_ref, wu_ref, a_ref, b_ref, f_ref):
        h = h_ref[...].astype(MXU_DTYPE)
        for c0 in range(0, FFN_TN, MXU_COLS):
            sl = slice(c0, min(c0 + MXU_COLS, FFN_TN))
            a = lax.dot_general(h, wg_ref[sl, :], _NT, preferred_element_type=F32)
            b = lax.dot_general(h, wu_ref[sl, :], _NT, preferred_element_type=F32)
            a_ref[:, sl] = a.astype(BF16)
            b_ref[:, sl] = b.astype(BF16)
            f_ref[:, sl] = (a * _sigmoid(a) * b).astype(BF16)

    tile = pl.BlockSpec((tm, FFN_TN), lambda j, i: (i, j))
    wspec = pl.BlockSpec((FFN_TN, D_MODEL), lambda j, i: (j, 0))
    return _grid_call(
        body, "ffn_in_swiglu", (D_FF // FFN_TN, t // tm), [h2, w_gate_t, w_up_t],
        [pl.BlockSpec((tm, D_MODEL), lambda j, i: (i, 0)), wspec, wspec],
        [tile] * 3, [jax.ShapeDtypeStruct((t, D_FF), BF16)] * 3, VMEM_BIG, comm)


def _ffn_down_final(f, w_down, x1, target, g):
    t = x1.shape[0]
    tm = 256

    def body(f_ref, w_ref, x1_ref, t_ref, g_ref, dx_ref, dxb_ref, loss_ref, gg_ref):
        @pl.when(pl.program_id(0) == 0)
        def _():
            loss_ref[...] = jnp.zeros_like(loss_ref)
            gg_ref[...] = jnp.zeros_like(gg_ref)

        xv = x1_ref[...] + jnp.dot(f_ref[...].astype(MXU_DTYPE), w_ref[...], preferred_element_type=F32)
        gv = g_ref[...]
        r = lax.rsqrt(jnp.mean(xv * xv, axis=-1, keepdims=True) + RMS_EPS)
        n = xv * r
        diff = n * gv - t_ref[...]
        per_tok = jnp.mean(diff * diff, axis=-1, keepdims=True)
        loss_ref[...] += 0.5 * jnp.sum(per_tok, axis=0, keepdims=True)
        dy = diff / xv.shape[-1]
        gg_ref[...] += jnp.sum(dy * n, axis=0, keepdims=True)
        dn = dy * gv
        dx = r * (dn - n * jnp.mean(dn * n, axis=-1, keepdims=True))
        dx_ref[...] = dx
        dxb_ref[...] = dx.astype(BF16)

    acc = lambda shp: pl.BlockSpec(shp, lambda i, j: (0, 0))
    return _grid_call(
        body, "ffn_down_final", (t // tm, 1), [f, w_down, x1, target, g],
        [_row_spec(tm, D_FF), _whole(w_down), _row_spec(tm, D_MODEL), _row_spec(tm, D_MODEL), _whole(g)],
        [_row_spec(tm, D_MODEL)] * 2 + [acc((8, LANES)), acc((1, D_MODEL))],
        [jax.ShapeDtypeStruct((t, D_MODEL), F32), jax.ShapeDtypeStruct((t, D_MODEL), BF16),
         jax.ShapeDtypeStruct((8, LANES), F32), jax.ShapeDtypeStruct((1, D_MODEL), F32)], VMEM_BIG, sequential=True)


def _d_f_swiglu_bwd(dx2b, w_down, a, b):
    t = a.shape[0]
    tm = 512

    def body(dx_ref, w_ref, a_ref, b_ref, da_ref, db_ref):
        d = lax.dot_general(dx_ref[...], w_ref[...], _NT, preferred_element_type=F32)
        av, bv = a_ref[...].astype(F32), b_ref[...].astype(F32)
        sg = _sigmoid(av)
        da_ref[...] = (d * bv * sg * (1.0 + av * (1.0 - sg))).astype(BF16)
        db_ref[...] = (d * av * sg).astype(BF16)

    tile = pl.BlockSpec((tm, FFN_TN), lambda j, i: (i, j))
    return _grid_call(
        body, "d_f_swiglu_bwd", (D_FF // FFN_TN, t // tm), [dx2b, w_down, a, b],
        [pl.BlockSpec((tm, D_MODEL), lambda j, i: (i, 0)), pl.BlockSpec((FFN_TN, D_MODEL), lambda j, i: (j, 0)), tile, tile],
        [tile] * 2, [jax.ShapeDtypeStruct((t, D_FF), BF16)] * 2, VMEM_BIG)


def _mm_rms_bwd(operands, weights, x, g, dres, name, comm=None):
    t = x.shape[0]
    tm = 256
    n_op = len(operands)

    def body(*refs):
        a_refs, w_refs = refs[:n_op], refs[n_op:2 * n_op]
        x_ref, g_ref, dres_ref, dx_ref, dxb_ref, gg_ref = refs[2 * n_op:]

        @pl.when(pl.program_id(0) == 0)
        def _():
            gg_ref[...] = jnp.zeros_like(gg_ref)

        dh = None
        for a_ref, w_ref in zip(a_refs, w_refs):
            part = jnp.dot(a_ref[...].astype(MXU_DTYPE), w_ref[...], preferred_element_type=F32)
            dh = part if dh is None else dh + part
        xv = x_ref[...]
        r = lax.rsqrt(jnp.mean(xv * xv, axis=-1, keepdims=True) + RMS_EPS)
        n = xv * r
        gg_ref[...] += jnp.sum(dh * n, axis=0, keepdims=True)
        dn = dh * g_ref[...]
        dx = dres_ref[...] + r * (dn - n * jnp.mean(dn * n, axis=-1, keepdims=True))
        dx_ref[...] = dx
        dxb_ref[...] = dx.astype(BF16)

    d = x.shape[1]
    return _grid_call(
        body, name, (t // tm, 1), [*operands, *weights, x, g, dres],
        [_row_spec(tm, a.shape[1]) for a in operands] + [_whole(wk) for wk in weights]
        + [_row_spec(tm, d), _whole(g), _row_spec(tm, d)],
        [_row_spec(tm, d)] * 2 + [pl.BlockSpec((1, d), lambda i, j: (0, 0))],
        [jax.ShapeDtypeStruct((t, d), F32), jax.ShapeDtypeStruct((t, d), BF16), jax.ShapeDtypeStruct((1, d), F32)],
        VMEM_BIG, comm, sequential=True)


def _flat_small(small):
    perm_b = lambda a: a.reshape(SSM_GROUPS, SSM_STATE, SSM_CH).transpose(2, 0, 1).reshape(SSM_CH, N_STATE)
    perm_c = lambda a: a.reshape(SSM_GROUPS, SSM_CH, SSM_STATE).transpose(1, 0, 2).reshape(SSM_CH, N_STATE)
    return dict(
        g_mix=small["norm_mix_g"].reshape(1, D_MODEL), g_ffn=small["norm_ffn_g"].reshape(1, D_MODEL),
        g_fin=small["norm_final_g"].reshape(1, D_MODEL),
        lr=small["ssm_a_re"].reshape(1, N_STATE), li=small["ssm_a_im"].reshape(1, N_STATE),
        ldt=jnp.repeat(small["ssm_log_dt"].reshape(SSM_GROUPS), SSM_STATE).reshape(1, N_STATE),
        br=perm_b(small["ssm_b_re"]), bi=perm_b(small["ssm_b_im"]),
        cr=perm_c(small["ssm_c_re"]), ci=perm_c(small["ssm_c_im"]), dskip=small["ssm_d"].reshape(1, SSM_W))


AG_HOSTS = {"proj_rope": ("w_glu", "w_attn_out", "w_out"), "attn_fwd_g0": ("w_ffn_gate",), "attn_fwd_g1": ("w_ffn_up",),
            "ffn_in_swiglu": ("w_ffn_down",)}
HALVED = ("w_ffn_gate", "w_ffn_up", "w_in")
A2A_HOSTS = {"d_h2_rms": ("w_ffn_down",), "attn_bwd_g0": ("w_ffn_gate:0",), "attn_bwd_g1": ("w_ffn_gate:1",),
             "attn_bwd_g2": ("w_ffn_up:0",), "ssm_bwd": ("w_ffn_up:1", "w_out", "w_attn_out", "w_glu"),
             "mm_g_in1": ("w_in:0",), "d_h0_rms": ("w_in:1",)}
SMALL_HOST = "mm_g_in0"


def _local_step(x, target, w, small, shards=None):
    t = x.shape[0]
    n_samples = t // SEQ
    n_rows = n_samples * SCAN_SEG_PER_SAMPLE
    tabs = _rope_tables()
    w = dict(w)
    fs = _flat_small(small)
    g_mix, g_ffn, g_fin, dskip = fs["g_mix"], fs["g_ffn"], fs["g_fin"], fs["dskip"]
    a_cat, bbc, cc = _ssm_disc(fs["lr"], fs["li"], fs["ldt"], fs["br"], fs["bi"], fs["cr"], fs["ci"])
    big, recv, small_pack = {}, {}, []

    def comm_of(name):
        if shards is None:
            return None
        if name == SMALL_HOST:
            return _ag_comm([(small_pack[0], 0, 0)], [(N_DEV, *small_pack[0].shape)])
        if name in AG_HOSTS:
            names = AG_HOSTS[name]
            return _ag_comm([(shards[n], j, 0) for j, n in enumerate(names)], [(N_DEV, *shards[n].shape) for n in names])
        if name in A2A_HOSTS:
            return _a2a_comm([(big[n].reshape(N_DEV, -1, big[n].shape[1]), 0) for n in A2A_HOSTS[name]])
        return None

    def absorb(name, carried):
        if name == SMALL_HOST:
            recv["small"] = carried[0]
        for n, a3 in zip(AG_HOSTS.get(name, ()), carried):
            w[n] = a3.reshape(-1, a3.shape[2])
        for n, a3 in zip(A2A_HOSTS.get(name, ()), carried):
            recv[n] = a3

    def mm(a, b, mode, name, tm, tn, **kw):
        comm = comm_of(name)
        if comm is None:
            return _mm(a, b, mode, name, tm, tn, **kw)
        out, *carried = _mm(a, b, mode, name, tm, tn, comm=comm, **kw)
        absorb(name, carried)
        return out

    h0, u, gates, *rest = _proj_rope(x, g_mix, w["w_in"], tabs, comm_of("proj_rope"))
    qkv = [rest[3 * g:3 * g + 3] for g in range(3)]
    absorb("proj_rope", rest[9:])
    os_, lses = [], []
    for g in range(3):
        o_g, l_g, carried = _attn_fwd(*qkv[g], g, n_samples, comm_of(f"attn_fwd_g{g}"))
        absorb(f"attn_fwd_g{g}", carried)
        os_.append(o_g)
        lses.append(l_g)
    attn, lse_tot = _attn_merge(os_, lses)
    attn_d = mm(attn, w["w_attn_out"], "nt", "mm_attn_out", 512, D_MODEL)

    u_perm = _to_scan_rows(u, n_samples)
    ytot, yg_perm, ein = _ssm_fwd(u_perm, a_cat, bbc, cc, dskip, n_rows)
    yg = _from_scan_rows(yg_perm, n_samples)
    z = mm(yg, w["w_glu"], "nt", "mm_glu", 512, 2 * D_MODEL)

    merged = _mix(attn_d, z, gates)
    x1, h2 = _out_rms(merged, w["w_out"], x, g_ffn)
    ffn_a, ffn_b, f, *carried = _ffn_in_swiglu(h2, w["w_ffn_gate"], w["w_ffn_up"], comm_of("ffn_in_swiglu"))
    absorb("ffn_in_swiglu", carried)
    dx2, dx2b, loss_blk, g_gfin = _ffn_down_final(f, w["w_ffn_down"], x1, target, g_fin)

    da, db = _d_f_swiglu_bwd(dx2b, w["w_ffn_down"], ffn_a, ffn_b)
    big["w_ffn_down"] = mm(f, dx2b, "tn", "mm_g_down", 256, D_MODEL, out_dtype=BF16)
    half = D_MODEL // 2
    for hf in range(2):
        big[f"w_ffn_gate:{hf}"] = mm(da, h2, "tn", f"mm_g_gate{hf}", 256, half, out_dtype=BF16, cols=(hf * half, half))
        big[f"w_ffn_up:{hf}"] = mm(db, h2, "tn", f"mm_g_up{hf}", 256, half, out_dtype=BF16, cols=(hf * half, half))
    dx1, dx1b, g_gffn, *carried = _mm_rms_bwd([da, db], [w["w_ffn_gate"], w["w_ffn_up"]], x1, g_ffn, dx2, "d_h2_rms",
                                              comm_of("d_h2_rms"))
    absorb("d_h2_rms", carried)

    dmerged = mm(dx1b, w["w_out"], "nt", "mm_d_merged", 512, D_MODEL)
    big["w_out"] = mm(merged, dx1b, "tn", "mm_g_out", 256, D_MODEL, out_dtype=BF16)
    dattn_d, dz, dgpre = _mix_bwd(dmerged, gates, attn_d, z)

    dattn = mm(dattn_d, w["w_attn_out"], "nn", "mm_d_attn", 512, GROUP_W)
    big["w_attn_out"] = mm(dattn_d, attn, "tn", "mm_g_attn_out", 512, GROUP_W, out_dtype=BF16)
    cot = _attn_rowdot(dattn, attn, lse_tot)
    dqs, dks, dvs = [], [], []
    for g in range(3):
        dq_g, dk_g, dv_g, carried = _attn_bwd(*qkv[g], *cot[g], g, n_samples, comm_of(f"attn_bwd_g{g}"))
        absorb(f"attn_bwd_g{g}", carried)
        dqs.append(dq_g)
        dks.append(dk_g)
        dvs.append(dv_g)

    dyg = mm(dz, w["w_glu"], "nn", "mm_d_yg", 512, SSM_W)
    big["w_glu"] = mm(dz, yg, "tn", "mm_g_glu", 512, 512, out_dtype=BF16)
    dyg_perm = _to_scan_rows(dyg, n_samples)
    dypre, du_skip, g_dskip = _ssm_act_bwd(dyg_perm, ytot, u_perm, dskip)
    du_perm, da_cat, dbb_full, dc_full, *carried = _ssm_bwd(u_perm, dypre, du_skip, a_cat, bbc, cc, ein, n_rows,
                                                          comm_of("ssm_bwd"))
    absorb("ssm_bwd", carried)
    du = _from_scan_rows(du_perm, n_samples)
    g_lr, g_li, g_ldt, g_br, g_bi, g_cr, g_ci = _ssm_param_bwd(
        fs["lr"], fs["li"], fs["ldt"], fs["br"], fs["bi"], da_cat, dbb_full, dc_full)

    small_pack.append(_pack_small(dict(lr=g_lr, li=g_li, ldt=g_ldt, br=g_br, bi=g_bi, cr=g_cr, ci=g_ci, dskip=g_dskip,
                                       g_ffn=g_gffn, g_fin=g_gfin, loss=loss_blk)))

    dproj = _pack_dproj(dqs, dks, dvs, du, dgpre, tabs)
    for hf in range(2):
        big[f"w_in:{hf}"] = mm(dproj, h0, "tn", f"mm_g_in{hf}", 256, half, out_dtype=BF16, cols=(hf * half, half))
    grad_x, _, g_gmix, *carried = _mm_rms_bwd([dproj], [w["w_in"]], x, g_mix, dx1, "d_h0_rms", comm_of("d_h0_rms"))
    absorb("d_h0_rms", carried)
    return grad_x, (big if shards is None else recv), small_pack[0], g_gmix


_MESH = pl.DeviceIdType.MESH


def _all_gather(block, name):
    rows, lanes = block.shape

    def body(x_ref, out_ref, send_sems, recv_sems, local_sem):
        x, y, c = lax.axis_index("x"), lax.axis_index("y"), lax.axis_index("c")
        me, sibling = (x, y, c), (x, y, 1 - c)
        chips = [(1 - x, y), (x, 1 - y), (1 - x, 1 - y)]

        def slot(px, py, pc):
            return out_ref.at[4 * px + 2 * py + pc]

        def copy(k, blk, to, src=None):
            return pltpu.make_async_remote_copy(
                src_ref=slot(*blk) if src is None else src, dst_ref=slot(*blk), send_sem=send_sems.at[k],
                recv_sem=recv_sems.at[k], device_id=to, device_id_type=_MESH)

        mine = pltpu.make_async_copy(x_ref, slot(*me), local_sem)
        mine.start()
        first = [copy(0, me, sibling, src=x_ref)]
        first += [copy(1 + j, me, (*chip, c), src=x_ref) for j, chip in enumerate(chips)]
        for cp in first:
            cp.start()
        passed = [copy(4 + j, (*chip, c), sibling) for j, chip in enumerate(chips)]
        for j, chip in enumerate(chips):
            copy(1 + j, (*chip, c), me).wait_recv()
            passed[j].start()
        copy(0, sibling, me).wait_recv()
        for j, chip in enumerate(chips):
            copy(4 + j, (*chip, 1 - c), me).wait_recv()
        for cp in first + passed:
            cp.wait_send()
        mine.wait()

    return _pallas_call(
        body, name=name, out_shape=jax.ShapeDtypeStruct((N_DEV, rows, lanes), block.dtype),
        in_specs=[pl.BlockSpec(memory_space=pl.ANY)], out_specs=pl.BlockSpec(memory_space=pl.ANY),
        scratch_shapes=[pltpu.SemaphoreType.DMA((7,)), pltpu.SemaphoreType.DMA((7,)), pltpu.SemaphoreType.DMA],
    )(block)


def _ag_comm(items, bufs):
    def plan(in_refs, out_refs, send_sems, recv_sems, local_sems):
        x, y, c = lax.axis_index("x"), lax.axis_index("y"), lax.axis_index("c")
        me, sibling = (x, y, c), (x, y, 1 - c)
        chips = [(1 - x, y), (x, 1 - y), (1 - x, 1 - y)]
        plans = []
        for t, (_, buf, slot0) in enumerate(items):
            x_ref, out_ref = in_refs[t], out_refs[buf]

            def slot(px, py, pc, out_ref=out_ref, slot0=slot0):
                return out_ref.at[slot0 + 4 * px + 2 * py + pc]

            def copy(k, blk, to, src=None, t=t, slot=slot):
                return pltpu.make_async_remote_copy(
                    src_ref=slot(*blk) if src is None else src, dst_ref=slot(*blk), send_sem=send_sems.at[7 * t + k],
                    recv_sem=recv_sems.at[7 * t + k], device_id=to, device_id_type=_MESH)

            plans.append(dict(
                mine=pltpu.make_async_copy(x_ref, slot(*me), local_sems.at[t]),
                first=[copy(0, me, sibling, src=x_ref)] + [copy(1 + j, me, (*chip, c), src=x_ref)
                                                           for j, chip in enumerate(chips)],
                passed=[copy(4 + j, (*chip, c), sibling) for j, chip in enumerate(chips)],
                from_ici=[copy(1 + j, (*chip, c), me) for j, chip in enumerate(chips)],
                from_sibling=[copy(0, sibling, me)] + [copy(4 + j, (*chip, 1 - c), me) for j, chip in enumerate(chips)]))
        return plans

    def start(*refs):
        for p in plan(*refs):
            p["mine"].start()
            for cp in p["first"]:
                cp.start()

    def finish(*refs):
        plans = plan(*refs)
        for p in plans:
            for arrived, onward in zip(p["from_ici"], p["passed"]):
                arrived.wait_recv()
                onward.start()
        for p in plans:
            for arrived in p["from_sibling"]:
                arrived.wait_recv()
            for cp in p["first"] + p["passed"]:
                cp.wait_send()
            p["mine"].wait()

    dtype_of = {buf: shard.dtype for shard, buf, _ in items}
    out_shapes = [jax.ShapeDtypeStruct(b, dtype_of[j]) for j, b in enumerate(bufs)]
    return _Comm([it[0] for it in items], out_shapes, 7 * len(items), len(items), start, finish)


def _a2a_comm(items):
    def plan(in_refs, out_refs, send_sems, recv_sems, local_sems):
        x, y, c = lax.axis_index("x"), lax.axis_index("y"), lax.axis_index("c")
        my = 4 * x + 2 * y + c
        copies, locals_ = [], []
        for t, (_, slot0) in enumerate(items):
            s_ref, r_ref = in_refs[t], out_refs[t]
            locals_.append(pltpu.make_async_copy(s_ref.at[slot0 + my], r_ref.at[my], local_sems.at[t]))
            for kk in range(1, N_DEV):
                px = 1 - x if kk & 4 else x
                py = 1 - y if kk & 2 else y
                pc = 1 - c if kk & 1 else c
                copies.append(pltpu.make_async_remote_copy(
                    src_ref=s_ref.at[slot0 + 4 * px + 2 * py + pc], dst_ref=r_ref.at[my],
                    send_sem=send_sems.at[7 * t + kk - 1], recv_sem=recv_sems.at[7 * t + kk - 1],
                    device_id=(px, py, pc), device_id_type=_MESH))
        return copies, locals_

    def start(*refs):
        copies, locals_ = plan(*refs)
        for cp in locals_ + copies:
            cp.start()

    def finish(*refs):
        copies, locals_ = plan(*refs)
        for cp in copies + locals_:
            cp.wait()

    out_shapes = [jax.ShapeDtypeStruct((N_DEV,) + it[0].shape[1:], it[0].dtype) for it in items]
    return _Comm([it[0] for it in items], out_shapes, 7 * len(items), len(items), start, finish)


def _adam_math(g, w, m, v):
    m_new = ADAM_B1 * m + (1.0 - ADAM_B1) * g
    v_new = ADAM_B2 * v + (1.0 - ADAM_B2) * jnp.square(g)
    m_hat = m_new / (1.0 - ADAM_B1 ** ADAM_STEP)
    v_hat = v_new / (1.0 - ADAM_B2 ** ADAM_STEP)
    return -ADAM_LR * (m_hat / (jnp.sqrt(v_hat) + ADAM_EPS) + ADAM_WD * w), m_new, v_new


def _sum_partials(parts, name, tm):
    n, rows, _ = parts[0].shape
    widths = [p.shape[2] for p in parts]

    def body(*refs):
        g_ref, off = refs[-1], 0
        for p_ref, wd in zip(refs[:-1], widths):
            g = p_ref[0].astype(F32)
            for s in range(1, n):
                g = g + p_ref[s].astype(F32)
            g_ref[:, off:off + wd] = g
            off += wd

    return _pallas_call(
        body, name=name, grid=(rows // tm,), in_specs=[pl.BlockSpec((n, tm, wd), lambda i: (0, i, 0)) for wd in widths],
        out_specs=pl.BlockSpec((tm, sum(widths)), lambda i: (i, 0)),
        out_shape=jax.ShapeDtypeStruct((rows, sum(widths)), F32),
        compiler_params=pltpu.CompilerParams(dimension_semantics=("parallel",), vmem_limit_bytes=VMEM_MID),
    )(*parts)


def _adam(partials, w, m, v, name, tm):
    n, rows, cols = partials.shape

    def body(p_ref, w_ref, m_ref, v_ref, g_ref, d_ref, nm_ref, nv_ref):
        g = p_ref[0].astype(F32)
        for s in range(1, n):
            g = g + p_ref[s].astype(F32)
        g_ref[...] = g
        d_ref[...], nm_ref[...], nv_ref[...] = _adam_math(g, w_ref[...], m_ref[...], v_ref[...])

    assert rows % tm == 0
    row = pl.BlockSpec((tm, cols), lambda i: (i, 0))
    shp = jax.ShapeDtypeStruct((rows, cols), F32)
    return _pallas_call(
        body, name=name, grid=(rows // tm,),
        in_specs=[pl.BlockSpec((n, tm, cols), lambda i: (0, i, 0)), row, row, row],
        out_specs=[row] * 4, out_shape=[shp] * 4,
        compiler_params=pltpu.CompilerParams(dimension_semantics=("parallel",), vmem_limit_bytes=VMEM_MID),
    )(partials, w, m, v)


_PK_LR, _PK_LI, _PK_GAINS, _PK_MISC, _PK_BR, _PK_BI, _PK_CR, _PK_CI, _PK_ROWS = 0, 1, 2, 3, 8, 24, 40, 56, 72
_PK_LDT_LANE, _PK_LOSS_LANE = D_MODEL + SSM_W, D_MODEL + SSM_W + LANES


def _pack_small(sg):
    names = ("lr", "li", "g_ffn", "g_fin", "dskip", "ldt", "loss", "br", "bi", "cr", "ci")

    def body(lr, li, gffn, gfin, dskip, ldt, loss, br, bi, cr, ci, o_ref):
        o_ref[...] = jnp.zeros_like(o_ref)
        o_ref[_PK_LR:_PK_LR + 1, :] = lr[...]
        o_ref[_PK_LI:_PK_LI + 1, :] = li[...]
        o_ref[_PK_GAINS:_PK_GAINS + 1, D_MODEL:] = gffn[...]
        o_ref[_PK_MISC:_PK_MISC + 1, :D_MODEL] = gfin[...]
        o_ref[_PK_MISC:_PK_MISC + 1, D_MODEL:D_MODEL + SSM_W] = dskip[...]
        o_ref[_PK_MISC:_PK_MISC + 1, _PK_LDT_LANE:_PK_LDT_LANE + LANES] = ldt[0:1, :]
        o_ref[_PK_MISC:_PK_MISC + 1, _PK_LOSS_LANE:_PK_LOSS_LANE + LANES] = loss[0:1, :]
        o_ref[_PK_BR:_PK_BR + SSM_CH, :] = br[...]
        o_ref[_PK_BI:_PK_BI + SSM_CH, :] = bi[...]
        o_ref[_PK_CR:_PK_CR + SSM_CH, :] = cr[...]
        o_ref[_PK_CI:_PK_CI + SSM_CH, :] = ci[...]

    return _pallas_call(body, name="pack_small", out_shape=jax.ShapeDtypeStruct((_PK_ROWS, N_STATE), F32))(
        *[sg[n] for n in names])


def _unpack_small(s, g_mix):
    unflat_b = lambda a: a.reshape(SSM_CH, SSM_GROUPS, SSM_STATE).transpose(1, 2, 0)[None]
    unflat_c = lambda a: a.reshape(SSM_CH, SSM_GROUPS, SSM_STATE).transpose(1, 0, 2)[None]
    grads = {
        "norm_mix_g": g_mix, "norm_ffn_g": s[_PK_GAINS, D_MODEL:].reshape(1, D_MODEL),
        "norm_final_g": s[_PK_MISC, :D_MODEL],
        "ssm_a_re": s[_PK_LR].reshape(1, SSM_GROUPS, SSM_STATE), "ssm_a_im": s[_PK_LI].reshape(1, SSM_GROUPS, SSM_STATE),
        "ssm_log_dt": s[_PK_MISC, _PK_LDT_LANE:_PK_LDT_LANE + SSM_GROUPS].reshape(1, SSM_GROUPS),
        "ssm_d": s[_PK_MISC, D_MODEL:D_MODEL + SSM_W].reshape(1, SSM_GROUPS, SSM_CH),
        "ssm_b_re": unflat_b(s[_PK_BR:_PK_BR + SSM_CH]), "ssm_b_im": unflat_b(s[_PK_BI:_PK_BI + SSM_CH]),
        "ssm_c_re": unflat_c(s[_PK_CR:_PK_CR + SSM_CH]), "ssm_c_im": unflat_c(s[_PK_CI:_PK_CI + SSM_CH]),
    }
    return s[_PK_MISC, _PK_LOSS_LANE], grads


def _adam_small(grads, wts, moms, vars_):
    n = len(SMALL_WEIGHTS)
    as2d = lambda a: a.reshape(1, -1) if a.ndim == 1 else a

    def body(*refs):
        ins, outs = refs[:4 * n], refs[4 * n:]
        for i in range(n):
            g, w, m, v = (ins[j * n + i][...] for j in range(4))
            outs[i][...], outs[n + i][...], outs[2 * n + i][...] = _adam_math(g, w, m, v)

    operands = [as2d(d[k]) for d in (grads, wts, moms, vars_) for k in SMALL_WEIGHTS]
    shapes = [jax.ShapeDtypeStruct(as2d(wts[k]).shape, F32) for k in SMALL_WEIGHTS] * 3
    res = _pallas_call(body, name="adam_small", out_shape=shapes,
                         compiler_params=pltpu.CompilerParams(vmem_limit_bytes=VMEM_BIG))(*operands)
    out = {}
    for j, kind in enumerate(("delta", "new_m", "new_v")):
        for i, k in enumerate(SMALL_WEIGHTS):
            out[kind, k] = res[j * n + i].reshape(wts[k].shape)
    return out


def kernel(x, norm_mix_g, w_in, ssm_a_re, ssm_a_im, ssm_log_dt, ssm_b_re, ssm_b_im, ssm_c_re, ssm_c_im, ssm_d, w_glu, w_attn_out, w_out, norm_ffn_g, w_ffn_gate, w_ffn_up, w_ffn_down, norm_final_g, loss_target, m_norm_mix_g, m_w_in, m_ssm_a_re, m_ssm_a_im, m_ssm_log_dt, m_ssm_b_re, m_ssm_b_im, m_ssm_c_re, m_ssm_c_im, m_ssm_d, m_w_glu, m_w_attn_out, m_w_out, m_norm_ffn_g, m_w_ffn_gate, m_w_ffn_up, m_w_ffn_down, m_norm_final_g, v_norm_mix_g, v_w_in, v_ssm_a_re, v_ssm_a_im, v_ssm_log_dt, v_ssm_b_re, v_ssm_b_im, v_ssm_c_re, v_ssm_c_im, v_ssm_d, v_w_glu, v_w_attn_out, v_w_out, v_norm_ffn_g, v_w_ffn_gate, v_w_ffn_up, v_w_ffn_down, v_norm_final_g):
    args = dict(locals())
    wts = {n: args[n] for n in ALL_WEIGHTS}
    moms = {n: args["m_" + n] for n in ALL_WEIGHTS}
    vars_ = {n: args["v_" + n] for n in ALL_WEIGHTS}
    n_samples = x.shape[0]
    t = n_samples * SEQ

    shards = {n: (wts[n][0] if n in ROW_SHARDED else wts[n][0].T).astype(BF16) for n in BIG_WEIGHTS}
    w_in_t = _all_gather(shards["w_in"], "allgather_w_in").reshape(IN_W, D_MODEL)

    small = {n: wts[n] for n in SMALL_WEIGHTS}
    grad_x, recv, _, g_mix_part = _local_step(x.reshape(t, D_MODEL), loss_target.reshape(t, D_MODEL), {"w_in": w_in_t},
                                              small, shards)

    results = {}
    for n in BIG_WEIGHTS:
        c, k = shards[n].shape
        w2, m2, v2 = wts[n][0], moms[n][0], vars_[n][0]
        if n in ROW_SHARDED:
            res = _adam(recv[n], w2, m2, v2, "adam_" + n, c // 2)
        else:
            parts = [recv[f"{n}:{hf}"] for hf in range(2)] if n in HALVED else [recv[n]]
            g_t = _sum_partials(parts, "sum_" + n, c // 2)
            res = _adam(g_t.T[None], w2, m2, v2, "adam_" + n, k // 2)
        for kind, a in zip(("grad", "delta", "new_m", "new_v"), res):
            results[kind, n] = a[None]

    g_mix_all = _all_gather(jnp.pad(g_mix_part, ((0, 7), (0, 0))), "allgather_g_mix")
    g_mix = _sum_partials([g_mix_all], "sum_g_mix", 8)[0:1]
    loss, sgrads = _unpack_small(_sum_partials([recv["small"]], "sum_small", _PK_ROWS), g_mix)
    for n in SMALL_WEIGHTS:
        results["grad", n] = sgrads[n]
    results.update(_adam_small(sgrads, wts, moms, vars_))
    outs = [loss, grad_x.reshape(x.shape)]
    for kind in ("grad", "delta", "new_m", "new_v"):
        outs += [results[kind, n] for n in ALL_WEIGHTS]
    return tuple(outs)
```

```python
import functools
import math

import jax
import jax.numpy as jnp
from jax import lax
from jax.experimental import pallas as pl
from jax.experimental.pallas import tpu as pltpu

F32 = jnp.float32
BF16 = jnp.bfloat16
MXU_DTYPE = jnp.bfloat16

N_DEV = 8
D_MODEL = 1024
SEQ = 2048
HEAD_DIM = 64
HEADS_PER_GROUP = 4
GROUP_W = HEADS_PER_GROUP * HEAD_DIM
DILATIONS = (1, 4, 16)
QKV_W = 3 * len(DILATIONS) * GROUP_W
Q_W = len(DILATIONS) * GROUP_W
ATT_BLOCK = 128
ROPE_DIM = 16
ROPE_THETA = 500000.0
SSM_W = 512
SSM_GROUPS = 32
SSM_CH = 16
SSM_STATE = 64
N_STATE = SSM_GROUPS * SSM_STATE
D_FF = 2816
IN_W = QKV_W + SSM_W + 2 * D_MODEL
RMS_EPS = 1e-6
NEG_INF = -1e30
LANES = 128

SCAN_SEG_PER_SAMPLE = 8
SCAN_LEN = SEQ // SCAN_SEG_PER_SAMPLE
SCAN_WC = 512
SCAN_NBLK = N_STATE // SCAN_WC
SCAN_CH = SSM_W // SCAN_NBLK
SCAN_CHUNK = 32

ADAM_LR = 0.001
ADAM_B1 = 0.9
ADAM_B2 = 0.999
ADAM_EPS = 1e-08
ADAM_WD = 0.01
ADAM_STEP = 10

VMEM_BIG = 48 * 1024 * 1024
VMEM_MID = 32 * 1024 * 1024

BIG_WEIGHTS = ("w_in", "w_glu", "w_attn_out", "w_out", "w_ffn_gate", "w_ffn_up", "w_ffn_down")
ROW_SHARDED = ("w_out", "w_ffn_down")
SMALL_WEIGHTS = ("norm_mix_g", "ssm_a_re", "ssm_a_im", "ssm_log_dt", "ssm_b_re", "ssm_b_im", "ssm_c_re", "ssm_c_im",
                 "ssm_d", "norm_ffn_g", "norm_final_g")
ALL_WEIGHTS = ("norm_mix_g", "w_in", "ssm_a_re", "ssm_a_im", "ssm_log_dt", "ssm_b_re", "ssm_b_im", "ssm_c_re", "ssm_c_im",
               "ssm_d", "w_glu", "w_attn_out", "w_out", "norm_ffn_g", "w_ffn_gate", "w_ffn_up", "w_ffn_down", "norm_final_g")


def _sigmoid(x):
    return 1.0 / (1.0 + jnp.exp(-x))


def _pallas_call(body, *, out_shape, **kw):
    single = not isinstance(out_shape, (list, tuple))
    shapes = [pltpu.HBM(s.shape, s.dtype) for s in ([out_shape] if single else out_shape)]
    call = pl.pallas_call(body, out_shape=shapes[0] if single else shapes, **kw)
    return lambda *operands: call(*[pltpu.with_memory_space_constraint(o, pltpu.HBM) for o in operands])


class _Comm:
    def __init__(self, ins, out_shapes, n_sem, n_local, start, finish):
        self.ins, self.out_shapes, self.n_sem, self.n_local = ins, out_shapes, n_sem, n_local
        self.start, self.finish = start, finish


def _mm(a, b, mode, name, tm, tn, out_dtype=F32, add=None, vmem=VMEM_BIG, comm=None, cols=None):
    if mode == "nn":
        (m, k), (_, n) = a.shape, b.shape
        a_spec = pl.BlockSpec((tm, k), lambda i, j: (i, 0))
        b_spec = pl.BlockSpec((k, tn), lambda i, j: (0, j))
        dims = (((1,), (0,)), ((), ()))
    elif mode == "nt":
        (m, k), (n, _) = a.shape, b.shape
        a_spec = pl.BlockSpec((tm, k), lambda i, j: (i, 0))
        b_spec = pl.BlockSpec((tn, k), lambda i, j: (j, 0))
        dims = (((1,), (1,)), ((), ()))
    else:
        (k, m), (_, n) = a.shape, b.shape
        first, n = cols if cols else (0, n)
        a_spec = pl.BlockSpec((k, tm), lambda i, j: (0, i))
        b_spec = pl.BlockSpec((k, tn), lambda i, j: (0, j + first // tn))
        dims = (((0,), (0,)), ((), ()))
    assert m % tm == 0 and n % tn == 0, (name, m, n, tm, tn)
    o_spec = pl.BlockSpec((tm, tn), lambda i, j: (i, j))
    has_add = add is not None

    def body(*refs):
        a_ref, b_ref, o_ref = refs[0], refs[1], refs[-1]
        acc = lax.dot_general(a_ref[...].astype(MXU_DTYPE), b_ref[...].astype(MXU_DTYPE), dims,
                              preferred_element_type=F32)
        if has_add:
            acc = acc + refs[2][...]
        o_ref[...] = acc.astype(out_dtype)

    ins = [a, b] + ([add] if has_add else [])
    in_specs = [a_spec, b_spec] + ([o_spec] if has_add else [])
    return _grid_call(body, name, (m // tm, n // tn), ins, in_specs, [o_spec],
                      [jax.ShapeDtypeStruct((m, n), out_dtype)], vmem, comm)


def _grid_call(body, name, grid, ins, in_specs, out_specs, out_shapes, vmem, comm=None, sequential=False, scratch=()):
    if comm is None:
        single = len(out_shapes) == 1
        semantics = ("arbitrary", "arbitrary") if sequential else ("parallel", "parallel")
        return _pallas_call(
            body, name=name, grid=grid, in_specs=in_specs, out_specs=out_specs[0] if single else out_specs,
            out_shape=out_shapes[0] if single else out_shapes, scratch_shapes=list(scratch),
            compiler_params=pltpu.CompilerParams(dimension_semantics=semantics, vmem_limit_bytes=vmem),
        )(*ins)
    n_in, n_out, n_cin, n_cout = len(ins), len(out_shapes), len(comm.ins), len(comm.out_shapes)
    n_io = n_in + n_cin + n_out + n_cout

    def carrying(*refs):
        own = refs[:n_in] + refs[n_in + n_cin:n_in + n_cin + n_out] + refs[n_io:len(refs) - 3]
        c_args = (refs[n_in:n_in + n_cin], refs[n_in + n_cin + n_out:n_io], *refs[-3:])

        @pl.when((pl.program_id(0) == 0) & (pl.program_id(1) == 0))
        def _():
            comm.start(*c_args)

        body(*own)

        @pl.when((pl.program_id(0) == grid[0] - 1) & (pl.program_id(1) == grid[1] - 1))
        def _():
            comm.finish(*c_args)

    hbm = pl.BlockSpec(memory_space=pl.ANY)
    return _pallas_call(
        carrying, name=name, grid=grid, in_specs=list(in_specs) + [hbm] * n_cin,
        out_specs=list(out_specs) + [hbm] * n_cout, out_shape=list(out_shapes) + list(comm.out_shapes),
        scratch_shapes=list(scratch) + [pltpu.SemaphoreType.DMA((comm.n_sem,)), pltpu.SemaphoreType.DMA((comm.n_sem,)),
                                        pltpu.SemaphoreType.DMA((comm.n_local,))],
        compiler_params=pltpu.CompilerParams(dimension_semantics=("arbitrary", "arbitrary"), vmem_limit_bytes=vmem),
    )(*ins, *comm.ins)


def _rows(body, name, n_rows, tm, ins, outs, vmem=VMEM_MID, scratch=()):
    assert n_rows % tm == 0
    arrays, in_specs = [], []
    for kind, arr in ins:
        arrays.append(arr)
        if kind == "row":
            assert n_rows % arr.shape[0] == 0, (name, arr.shape)
            in_specs.append(pl.BlockSpec((tm * arr.shape[0] // n_rows, arr.shape[1]), lambda i: (i, 0)))
        elif kind == "tab":
            nblk = arr.shape[0] // tm
            in_specs.append(pl.BlockSpec((tm, arr.shape[1]), lambda i, nblk=nblk: (i % nblk, 0)))
        else:
            in_specs.append(pl.BlockSpec(arr.shape, lambda i, nd=arr.ndim: (0,) * nd))
    out_specs, out_shape = [], []
    for kind, shp, dt in outs:
        if kind == "row":
            out_specs.append(pl.BlockSpec((tm, shp), lambda i: (i, 0)))
            out_shape.append(jax.ShapeDtypeStruct((n_rows, shp), dt))
        elif kind == "dil":
            d, wd = shp
            out_specs.append(pl.BlockSpec((tm // d, d * wd), lambda i: (i, 0)))
            out_shape.append(jax.ShapeDtypeStruct((n_rows // d, d * wd), dt))
        else:
            out_specs.append(pl.BlockSpec(shp, lambda i, nd=len(shp): (0,) * nd))
            out_shape.append(jax.ShapeDtypeStruct(shp, dt))
    res = _pallas_call(
        body, name=name, grid=(n_rows // tm,), in_specs=in_specs, out_specs=out_specs, out_shape=out_shape,
        scratch_shapes=list(scratch),
        compiler_params=pltpu.CompilerParams(dimension_semantics=("arbitrary",), vmem_limit_bytes=vmem),
    )(*arrays)
    return res


def _gather_residue(stage, ch, r, d, n):
    return stage[ch, pl.ds(r, n, stride=d), :] if d > 1 else stage[ch]


def _scatter_residue(stage, ch, r, d, n, val):
    if d > 1:
        stage[ch, pl.ds(r, n, stride=d), :] = val
    else:
        stage[ch] = val


def _lane_chunk(ch):
    return slice(ch * LANES, (ch + 1) * LANES)


def _first_step():
    return pl.program_id(0) == 0


def _rope_tables():
    half = ROPE_DIM // 2
    inv = jnp.power(jnp.float32(ROPE_THETA), -jnp.arange(half, dtype=F32) * 2.0 / ROPE_DIM)
    ang = jnp.arange(SEQ, dtype=F32)[:, None] * inv[None, :]
    lane = jnp.arange(LANES) % HEAD_DIM
    cosl = jnp.cos(ang)[:, lane % half]
    sinl = jnp.sin(ang)[:, lane % half]
    tab_c = jnp.where(lane < ROPE_DIM, cosl, 1.0)
    tab_lo = jnp.where(lane < half, -sinl, 0.0)
    tab_hi = jnp.where((lane >= half) & (lane < ROPE_DIM), sinl, 0.0)
    return tab_c.astype(F32), tab_lo.astype(F32), tab_hi.astype(F32)


def _rope_apply(t, tc, tlo, thi):
    half = ROPE_DIM // 2
    return t * tc + pltpu.roll(t, LANES - half, 1) * tlo + pltpu.roll(t, half, 1) * thi


def _rope_transpose(dt, tc, tlo, thi):
    half = ROPE_DIM // 2
    return dt * tc + pltpu.roll(dt * tlo, half, 1) + pltpu.roll(dt * thi, LANES - half, 1)


def _pack_dproj(dqs, dks, dvs, du, dgpre, tabs):
    tm = 256

    def body(*refs):
        dq_refs, dk_refs, dv_refs = refs[0:3], refs[3:6], refs[6:9]
        du_ref, dg_ref, tc_ref, tlo_ref, thi_ref, o_ref, stage = refs[9:16]
        n_ch = QKV_W // LANES
        halves = GROUP_W // LANES
        for grp, d in enumerate(DILATIONS):
            for which, src in enumerate((dq_refs[grp], dk_refs[grp], dv_refs[grp])):
                for res in range(d):
                    for half in range(halves):
                        _scatter_residue(stage, which * (n_ch // 3) + grp * halves + half, res, d, tm // d,
                                         src[:, _lane_chunk(res * halves + half)])
        tc, tlo, thi = tc_ref[...], tlo_ref[...], thi_ref[...]
        for ch in range(n_ch):
            piece = stage[ch]
            o_ref[:, _lane_chunk(ch)] = (_rope_transpose(piece, tc, tlo, thi) if ch < 2 * n_ch // 3 else piece).astype(BF16)
        o_ref[:, QKV_W:QKV_W + SSM_W] = du_ref[...].astype(BF16)
        o_ref[:, QKV_W + SSM_W:] = dg_ref[...].astype(BF16)

    t = du.shape[0]
    ins = [("row", a) for a in (*dqs, *dks, *dvs, du, dgpre)] + [("tab", tb) for tb in tabs]
    return _rows(body, "pack_dproj", t, tm, ins, [("row", IN_W, BF16)],
                 scratch=[pltpu.VMEM((QKV_W // LANES, tm, LANES), F32)])[0]


def _attn_merge(os_, lses):
    tm = 256

    halves = GROUP_W // LANES

    def body(o0, o1, o2, l0, l1, l2, a_ref, lt_ref, nat):
        for grp, d in enumerate(DILATIONS[1:], start=1):
            for j, src in enumerate(((o0, o1, o2)[grp], (l0, l1, l2)[grp])):
                for res in range(d):
                    for half in range(halves):
                        _scatter_residue(nat, (grp - 1) * 4 + j * 2 + half, res, d, tm // d,
                                         src[:, _lane_chunk(res * halves + half)])
        for half in range(halves):
            sl = _lane_chunk(half)
            la, lb, lc = l0[:, sl], nat[2 + half], nat[6 + half]
            m = jnp.maximum(jnp.maximum(la, lb), lc)
            ea, eb, ec = jnp.exp(la - m), jnp.exp(lb - m), jnp.exp(lc - m)
            ssum = ea + eb + ec
            a_ref[:, sl] = (ea / ssum) * o0[:, sl] + (eb / ssum) * nat[half] + (ec / ssum) * nat[4 + half]
            lt_ref[:, sl] = m + jnp.log(ssum)

    t = os_[0].shape[0]
    return _rows(body, "attn_merge", t, tm, [("row", a) for a in (*os_, *lses)],
                 [("row", GROUP_W, F32), ("row", GROUP_W, F32)], scratch=[pltpu.VMEM((8, tm, LANES), F32)])


def _head_sum_matrix():
    r = jnp.arange(GROUP_W) // HEAD_DIM
    return (r[:, None] == r[None, :]).astype(F32)


def _attn_rowdot(dattn, attn, lse_tot):
    tm = 256

    halves = GROUP_W // LANES

    def body(da_ref, a_ref, lt_ref, ones_ref, rd_ref, *rest):
        dil, stage = rest[:6], rest[6]
        rd = jnp.dot(da_ref[...] * a_ref[...], ones_ref[...], preferred_element_type=F32, precision=lax.Precision.HIGHEST)
        rd_ref[...] = rd
        for half in range(halves):
            stage[half] = da_ref[:, _lane_chunk(half)]
            stage[2 + half] = lt_ref[:, _lane_chunk(half)]
            stage[4 + half] = rd[:, _lane_chunk(half)]
        for grp, d in enumerate(DILATIONS[1:], start=1):
            for j in range(3):
                for res in range(d):
                    for half in range(halves):
                        dil[3 * (grp - 1) + j][:, _lane_chunk(res * halves + half)] = _gather_residue(
                            stage, 2 * j + half, res, d, tm // d)

    t = attn.shape[0]
    outs = [("row", GROUP_W, F32)] + [("dil", (d, GROUP_W), F32) for d in DILATIONS[1:] for _ in range(3)]
    rd, *dil = _rows(body, "attn_rowdot", t, tm,
                     [("row", dattn), ("row", attn), ("row", lse_tot), ("const", _head_sum_matrix())], outs,
                     scratch=[pltpu.VMEM((6, tm, LANES), F32)])
    return [(dattn, lse_tot, rd), tuple(dil[:3]), tuple(dil[3:])]


def _mix(attn_d, z, gates):
    def body(ad_ref, z_ref, g_ref, m_ref):
        za, zb = z_ref[:, :D_MODEL], z_ref[:, D_MODEL:]
        s_out = za * _sigmoid(zb)
        m_ref[...] = (g_ref[:, :D_MODEL] * ad_ref[...] + g_ref[:, D_MODEL:] * s_out).astype(BF16)

    t = attn_d.shape[0]
    return _rows(body, "mix", t, 256, [("row", attn_d), ("row", z), ("row", gates)], [("row", D_MODEL, BF16)])[0]


def _mix_bwd(dmerged, gates, attn_d, z):
    def body(dm_ref, g_ref, ad_ref, z_ref, dad_ref, dz_ref, dg_ref):
        dm = dm_ref[...]
        g0, g1 = g_ref[:, :D_MODEL], g_ref[:, D_MODEL:]
        za, zb = z_ref[:, :D_MODEL], z_ref[:, D_MODEL:]
        sb = _sigmoid(zb)
        s_out = za * sb
        dad_ref[...] = (dm * g0).astype(BF16)
        ds = dm * g1
        dz_ref[:, :D_MODEL] = (ds * sb).astype(BF16)
        dz_ref[:, D_MODEL:] = (ds * za * sb * (1.0 - sb)).astype(BF16)
        dg_ref[:, :D_MODEL] = (dm * ad_ref[...] * g0 * (1.0 - g0)).astype(BF16)
        dg_ref[:, D_MODEL:] = (dm * s_out * g1 * (1.0 - g1)).astype(BF16)

    t = dmerged.shape[0]
    return _rows(body, "mix_bwd", t, 256, [("row", dmerged), ("row", gates), ("row", attn_d), ("row", z)],
                 [("row", D_MODEL, BF16), ("row", 2 * D_MODEL, BF16), ("row", 2 * D_MODEL, BF16)])


_GELU_C = math.sqrt(2.0 / math.pi)


def _ssm_act_bwd(dyg, ytot, u_perm, dskip):
    def body(dyg_ref, yt_ref, u_ref, d_ref, dy_ref, dus_ref, dd_ref):
        @pl.when(_first_step())
        def _():
            dd_ref[...] = jnp.zeros_like(dd_ref)

        yt = yt_ref[...]
        th = jnp.tanh(_GELU_C * (yt + 0.044715 * (yt * yt * yt)))
        dgelu = 0.5 * (1.0 + th) + 0.5 * yt * (1.0 - th * th) * _GELU_C * (1.0 + 3.0 * 0.044715 * yt * yt)
        dy = dyg_ref[...] * dgelu
        dy_ref[...] = dy.astype(BF16)
        dus_ref[...] = dy * d_ref[...]
        dd_ref[...] += jnp.sum(dy * u_ref[...], axis=0, keepdims=True)

    t = dyg.shape[0]
    return _rows(body, "ssm_act_bwd", t, 512, [("row", dyg), ("row", ytot), ("row", u_perm), ("const", dskip)],
                 [("row", SSM_W, BF16), ("row", SSM_W, F32), ("acc", (1, SSM_W), F32)])


def _head_masks():
    lane = lax.broadcasted_iota(jnp.int32, (1, GROUP_W), 1)
    return [(lane // HEAD_DIM) == h for h in range(HEADS_PER_GROUP)]


def _stack_heads(blk, masks, fill=0.0):
    return jnp.concatenate([jnp.where(mk, blk, jnp.full_like(blk, fill)) for mk in masks], axis=0)


def _unstack_heads(stacked, masks):
    rows = stacked.shape[0] // len(masks)
    out = stacked[:rows]
    for h in range(1, len(masks)):
        out = jnp.where(masks[h], stacked[h * rows:(h + 1) * rows], out)
    return out


def _band_mask(first):
    nk = ATT_BLOCK if first else 2 * ATT_BLOCK
    qi = lax.broadcasted_iota(jnp.int32, (ATT_BLOCK, nk), 0)
    ki = lax.broadcasted_iota(jnp.int32, (ATT_BLOCK, nk), 1)
    dist = qi - ki + (0 if first else ATT_BLOCK)
    return (dist >= 0) & (dist <= ATT_BLOCK)


_NT = (((1,), (1,)), ((), ()))
_TN = (((0,), (0,)), ((), ()))


def _attn_fwd(q, k, v, group, n_samples, comm=None):
    d = DILATIONS[group]
    length = SEQ // d
    nb = length // ATT_BLOCK

    def body(q_ref, k_ref, v_ref, o_ref, l_ref):
        masks = _head_masks()

        def block(qs, ks, first):
            nk = ATT_BLOCK if first else 2 * ATT_BLOCK
            qb = q_ref[0, pl.ds(qs, ATT_BLOCK), :]
            kc = k_ref[0, pl.ds(ks, nk), :]
            vc = v_ref[0, pl.ds(ks, nk), :]
            q4 = _stack_heads(qb, masks)
            valid = jnp.tile(_band_mask(first), (HEADS_PER_GROUP, 1))
            s = lax.dot_general(q4, kc, _NT, preferred_element_type=F32) * (HEAD_DIM ** -0.5)
            s = jnp.where(valid, s, NEG_INF)
            m = jnp.max(s, axis=-1, keepdims=True)
            p = jnp.exp(s - m)
            l = jnp.sum(p, axis=-1, keepdims=True)
            o4 = jnp.dot(p.astype(MXU_DTYPE), vc, preferred_element_type=F32) / l
            lse4 = jnp.broadcast_to(m + jnp.log(l), o4.shape)
            o_ref[0, pl.ds(qs, ATT_BLOCK), :] = _unstack_heads(o4, masks)
            l_ref[0, pl.ds(qs, ATT_BLOCK), :] = _unstack_heads(lse4, masks)

        block(0, 0, True)
        if nb > 1:
            def loop(n, carry):
                block(pl.multiple_of(n * ATT_BLOCK, ATT_BLOCK), pl.multiple_of((n - 1) * ATT_BLOCK, ATT_BLOCK), False)
                return carry

            lax.fori_loop(1, nb, loop, 0)

    per_sample = lambda a: a.reshape(n_samples, length, d * GROUP_W)
    spec = pl.BlockSpec((1, length, GROUP_W), lambda b, r: (b, 0, r))
    shp = jax.ShapeDtypeStruct((n_samples, length, d * GROUP_W), F32)
    o, lse, *carried = _grid_call(body, f"attn_fwd_g{group}", (n_samples, d), [per_sample(a) for a in (q, k, v)],
                                  [spec] * 3, [spec] * 2, [shp, shp], VMEM_MID, comm)
    flat = lambda a: a.reshape(n_samples * length, d * GROUP_W)
    return flat(o), flat(lse), carried


def _attn_bwd(q, k, v, dattn, lse_tot, rowdot, group, n_samples, comm=None):
    d = DILATIONS[group]
    length = SEQ // d
    nb = length // ATT_BLOCK

    def body(q_ref, k_ref, v_ref, da_ref, lt_ref, rd_ref, dq_ref, dk_ref, dv_ref):
        masks = _head_masks()
        dk_ref[...] = jnp.zeros_like(dk_ref)
        dv_ref[...] = jnp.zeros_like(dv_ref)

        def block(qs, ks, first):
            nk = ATT_BLOCK if first else 2 * ATT_BLOCK
            qb = q_ref[0, pl.ds(qs, ATT_BLOCK), :]
            kc = k_ref[0, pl.ds(ks, nk), :]
            vc = v_ref[0, pl.ds(ks, nk), :]
            da = da_ref[0, pl.ds(qs, ATT_BLOCK), :]
            lt = lt_ref[0, pl.ds(qs, ATT_BLOCK), :]
            rd = rd_ref[0, pl.ds(qs, ATT_BLOCK), :]
            q4 = _stack_heads(qb, masks)
            da4 = _stack_heads(da, masks).astype(MXU_DTYPE)
            lt4 = jnp.max(_stack_heads(lt, masks, -jnp.inf), axis=-1, keepdims=True)
            rd4 = jnp.max(_stack_heads(rd, masks, -jnp.inf), axis=-1, keepdims=True)
            valid = jnp.tile(_band_mask(first), (HEADS_PER_GROUP, 1))
            s = lax.dot_general(q4, kc, _NT, preferred_element_type=F32) * (HEAD_DIM ** -0.5)
            s = jnp.where(valid, s, NEG_INF)
            p = jnp.exp(s - lt4)
            dp = lax.dot_general(da4, vc, _NT, preferred_element_type=F32)
            ds = (p * (dp - rd4) * (HEAD_DIM ** -0.5)).astype(MXU_DTYPE)
            dq_ref[0, pl.ds(qs, ATT_BLOCK), :] = _unstack_heads(jnp.dot(ds, kc, preferred_element_type=F32), masks)
            dk_ref[0, pl.ds(ks, nk), :] += lax.dot_general(ds, q4, _TN, preferred_element_type=F32)
            dv_ref[0, pl.ds(ks, nk), :] += lax.dot_general(p.astype(MXU_DTYPE), da4, _TN, preferred_element_type=F32)

        block(0, 0, True)
        if nb > 1:
            def loop(n, carry):
                block(pl.multiple_of(n * ATT_BLOCK, ATT_BLOCK), pl.multiple_of((n - 1) * ATT_BLOCK, ATT_BLOCK), False)
                return carry

            lax.fori_loop(1, nb, loop, 0)

    per_sample = lambda a: a.reshape(n_samples, length, d * GROUP_W)
    spec = pl.BlockSpec((1, length, GROUP_W), lambda b, r: (b, 0, r))
    shp = jax.ShapeDtypeStruct((n_samples, length, d * GROUP_W), F32)
    dq, dk, dv, *carried = _grid_call(
        body, f"attn_bwd_g{group}", (n_samples, d), [per_sample(a) for a in (q, k, v, dattn, lse_tot, rowdot)],
        [spec] * 6, [spec] * 3, [shp, shp, shp], VMEM_MID, comm)
    flat = lambda a: a.reshape(n_samples * length, d * GROUP_W)
    return flat(dq), flat(dk), flat(dv), carried


def _disc(lr, li, ldt, br, bi):
    dt = jnp.exp(ldt)
    mag = jnp.exp(lr * dt)
    ab_re, ab_im = mag * jnp.cos(li * dt), mag * jnp.sin(li * dt)
    den = lr * lr + li * li
    nr, ni = ab_re - 1.0, ab_im
    f_re = (nr * lr + ni * li) / den
    f_im = (ni * lr - nr * li) / den
    return ab_re, ab_im, f_re * br - f_im * bi, f_re * bi + f_im * br


def _state_mask():
    row_g = lax.broadcasted_iota(jnp.int32, (SCAN_CH, SCAN_WC), 0) // SSM_CH
    col_g = lax.broadcasted_iota(jnp.int32, (SCAN_CH, SCAN_WC), 1) // SSM_STATE
    return row_g == col_g


def _ssm_disc(lr, li, ldt, br, bi, cr, ci):
    w = SCAN_WC

    def body(lr_ref, li_ref, ldt_ref, br_ref, bi_ref, cr_ref, ci_ref, a_ref, bb_ref, c_ref):
        ar, ai, bbr, bbi = _disc(lr_ref[...], li_ref[...], ldt_ref[...], br_ref[...], bi_ref[...])
        crv, civ = cr_ref[...], ci_ref[...]
        mask = _state_mask()
        for cb in range(SCAN_NBLK):
            sl = slice(cb * w, (cb + 1) * w)
            rows = slice(cb * SCAN_CH, (cb + 1) * SCAN_CH)
            dense = lambda comp: jnp.where(mask, jnp.tile(comp[:, sl], (SCAN_CH // SSM_CH, 1)), 0.0)
            a_ref[:, 2 * cb * w:(2 * cb + 1) * w] = ar[:, sl]
            a_ref[:, (2 * cb + 1) * w:(2 * cb + 2) * w] = ai[:, sl]
            bb_ref[rows, :w] = dense(bbr).astype(MXU_DTYPE)
            bb_ref[rows, w:] = dense(bbi).astype(MXU_DTYPE)
            c_ref[rows, :w] = dense(crv).astype(MXU_DTYPE)
            c_ref[rows, w:] = (-dense(civ)).astype(MXU_DTYPE)

    return _pallas_call(
        body, name="ssm_disc",
        out_shape=[jax.ShapeDtypeStruct((1, 2 * N_STATE), F32), jax.ShapeDtypeStruct((SSM_W, 2 * w), MXU_DTYPE),
                   jax.ShapeDtypeStruct((SSM_W, 2 * w), MXU_DTYPE)],
        compiler_params=pltpu.CompilerParams(vmem_limit_bytes=VMEM_MID),
    )(lr, li, ldt, br, bi, cr, ci)


def _group_indicator():
    s = jnp.arange(N_STATE) // SSM_STATE
    return (s[:, None] == jnp.arange(LANES)[None, :]).astype(F32)


def _ssm_param_bwd(lr, li, ldt, br, bi, da_cat, dbb_full, dc_full):
    w = SCAN_WC

    def body(lr_ref, li_ref, ldt_ref, br_ref, bi_ref, da_ref, dbb_ref, dc_ref, ind_ref,
             glr_ref, gli_ref, gldt_ref, gbr_ref, gbi_ref, gcr_ref, gci_ref):
        mask = _state_mask()

        def diag_parts(ref):
            res = ([], [])
            for cb in range(SCAN_NBLK):
                for part in range(2):
                    blk = ref[cb * SCAN_CH:(cb + 1) * SCAN_CH, part * w:(part + 1) * w]
                    res[part].append(jnp.sum(jnp.where(mask, blk, 0.0).reshape(SCAN_CH // SSM_CH, SSM_CH, w), axis=0))
            return jnp.concatenate(res[0], axis=1), jnp.concatenate(res[1], axis=1)

        dar = jnp.concatenate([da_ref[:, 2 * cb * w:(2 * cb + 1) * w] for cb in range(SCAN_NBLK)], axis=1)
        dai = jnp.concatenate([da_ref[:, (2 * cb + 1) * w:(2 * cb + 2) * w] for cb in range(SCAN_NBLK)], axis=1)
        dbbr, dbbi = diag_parts(dbb_ref)
        dcr, dci_neg = diag_parts(dc_ref)
        gcr_ref[...] = dcr
        gci_ref[...] = -dci_neg
        _, vjp = jax.vjp(_disc, lr_ref[...], li_ref[...], ldt_ref[...], br_ref[...], bi_ref[...])
        glr, gli, gldt, gbr, gbi = vjp((dar, dai, dbbr, dbbi))
        glr_ref[...] = glr
        gli_ref[...] = gli
        gldt_ref[...] = jnp.dot(jnp.broadcast_to(gldt, (8, N_STATE)), ind_ref[...], preferred_element_type=F32,
                                precision=lax.Precision.HIGHEST)
        gbr_ref[...] = gbr
        gbi_ref[...] = gbi

    v1 = jax.ShapeDtypeStruct((1, N_STATE), F32)
    v16 = jax.ShapeDtypeStruct((SSM_CH, N_STATE), F32)
    vdt = jax.ShapeDtypeStruct((8, LANES), F32)
    return _pallas_call(
        body, name="ssm_param_bwd", out_shape=[v1, v1, vdt, v16, v16, v16, v16],
        compiler_params=pltpu.CompilerParams(vmem_limit_bytes=VMEM_BIG),
    )(lr, li, ldt, br, bi, da_cat, dbb_full, dc_full, _group_indicator())


def _cmul(ar, ai, br, bi):
    return ar * br - ai * bi, ar * bi + ai * br


def _gelu_tanh(y):
    return jnp.tanh(_GELU_C * (y + 0.044715 * (y * y * y)))


def _segment_carry(er, ei, ar, ai, n_rows, reverse):
    qr, qi = ar, ai
    for _ in range(int(math.log2(SCAN_LEN))):
        qr, qi = _cmul(qr, qi, qr, qi)
    seg = lax.broadcasted_iota(jnp.int32, er.shape, 0) % SCAN_SEG_PER_SAMPLE
    shift = 1
    while shift < SCAN_SEG_PER_SAMPLE:
        keep = (seg < SCAN_SEG_PER_SAMPLE - shift) if reverse else (seg >= shift)
        amount = n_rows - shift if reverse else shift
        sr = jnp.where(keep, pltpu.roll(er, amount, 0), 0.0)
        si = jnp.where(keep, pltpu.roll(ei, amount, 0), 0.0)
        if reverse:
            er, ei = er + qr * sr + qi * si, ei + qr * si - qi * sr
        else:
            er, ei = er + qr * sr - qi * si, ei + qr * si + qi * sr
        qr, qi = _cmul(qr, qi, qr, qi)
        shift *= 2
    keep = (seg < SCAN_SEG_PER_SAMPLE - 1) if reverse else (seg >= 1)
    amount = n_rows - 1 if reverse else 1
    return jnp.where(keep, pltpu.roll(er, amount, 0), 0.0), jnp.where(keep, pltpu.roll(ei, amount, 0), 0.0)


def _ssm_fwd(u_perm, a_cat, bbc, cc, dskip, n_rows):
    t = u_perm.shape[0]
    w = SCAN_WC
    rows_c = SCAN_CHUNK * n_rows
    n_chunks = t // rows_c

    assert n_chunks % 2 == 0

    def body(u_ref, a_ref, bb_ref, c_ref, d_ref, yt_ref, yg_ref, ein_ref, bu_all, st_a, st_b, xs_a, xs_b):
        ar = jnp.broadcast_to(a_ref[:, :w], (n_rows, w))
        ai = jnp.broadcast_to(a_ref[:, w:], (n_rows, w))
        start = lambda ch: pl.multiple_of(ch * rows_c, rows_c)

        def project(ch, stage):
            res = jnp.dot(u_ref[pl.ds(start(ch), rows_c), :].astype(MXU_DTYPE), bb_ref[...], preferred_element_type=F32)
            stage[...] = res
            bu_all[pl.ds(start(ch), rows_c), :] = res

        def steps(src, r0, carry, xs=None):
            for i in range(SCAN_CHUNK):
                blk = src[pl.ds(r0 + i * n_rows, n_rows), :]
                carry = (ar * carry[0] - ai * carry[1] + blk[:, :w], ar * carry[1] + ai * carry[0] + blk[:, w:])
                if xs is not None:
                    xs[i * n_rows:(i + 1) * n_rows, :w] = carry[0]
                    xs[i * n_rows:(i + 1) * n_rows, w:] = carry[1]
            return carry

        def emit(xs, ch):
            y = lax.dot_general(xs[...].astype(MXU_DTYPE), c_ref[...], _NT, preferred_element_type=F32)
            yt = y + d_ref[...] * u_ref[pl.ds(start(ch), rows_c), :]
            yt_ref[pl.ds(start(ch), rows_c), :] = yt
            yg_ref[pl.ds(start(ch), rows_c), :] = (0.5 * yt * (1.0 + _gelu_tanh(yt))).astype(BF16)

        project(0, st_a)

        def pair1(p, carry):
            project(2 * p + 1, st_b)
            carry = steps(st_a, 0, carry)
            project(jnp.minimum(2 * p + 2, n_chunks - 1), st_a)
            return steps(st_b, 0, carry)

        zero = jnp.zeros((n_rows, w), F32)
        er, ei = lax.fori_loop(0, n_chunks // 2, pair1, (zero, zero))
        cr, ci = _segment_carry(er, ei, ar, ai, n_rows, False)
        ein_ref[:, :w] = cr
        ein_ref[:, w:] = ci

        xs_b[...] = jnp.zeros_like(xs_b)

        def pair2(p, carry):
            emit(xs_b, jnp.maximum(2 * p - 1, 0))
            carry = steps(bu_all, start(2 * p), carry, xs_a)
            emit(xs_a, 2 * p)
            return steps(bu_all, start(2 * p + 1), carry, xs_b)

        lax.fori_loop(0, n_chunks // 2, pair2, (cr, ci))
        emit(xs_b, n_chunks - 1)

    col = lambda width: pl.BlockSpec((t, width), lambda c: (0, c))
    wgt = pl.BlockSpec((SCAN_CH, 2 * w), lambda c: (c, 0))
    return _pallas_call(
        body, name="ssm_fwd", grid=(SCAN_NBLK,),
        in_specs=[col(SCAN_CH), pl.BlockSpec((1, 2 * w), lambda c: (0, c)), wgt, wgt,
                  pl.BlockSpec((1, SCAN_CH), lambda c: (0, c))],
        out_specs=[col(SCAN_CH), col(SCAN_CH), pl.BlockSpec((n_rows, 2 * w), lambda c: (0, c))],
        out_shape=[jax.ShapeDtypeStruct((t, SSM_W), F32), jax.ShapeDtypeStruct((t, SSM_W), BF16),
                   jax.ShapeDtypeStruct((n_rows, 2 * N_STATE), F32)],
        scratch_shapes=[pltpu.VMEM((t, 2 * w), F32)] + [pltpu.VMEM((rows_c, 2 * w), F32)] * 4,
        compiler_params=pltpu.CompilerParams(dimension_semantics=("parallel",), vmem_limit_bytes=VMEM_BIG),
    )(u_perm, a_cat, bbc, cc, dskip)


def _ssm_bwd(u_perm, dypre, du_skip, a_cat, bbc, cc, ein, n_rows, comm=None):
    t = u_perm.shape[0]
    w = SCAN_WC
    rows_c = SCAN_CHUNK * n_rows
    n_chunks = t // rows_c

    assert n_chunks % 2 == 0
    last = n_chunks - 1

    def body(u_ref, dy_ref, dus_ref, a_ref, bb_ref, c_ref, ein_ref, du_ref, da_ref, dbb_ref, dc_ref,
             xs_all, st_a, st_b, buf_a, buf_b):
        ar = jnp.broadcast_to(a_ref[:, :w], (n_rows, w))
        ai = jnp.broadcast_to(a_ref[:, w:], (n_rows, w))
        zero = jnp.zeros((n_rows, w), F32)
        start = lambda ch: pl.multiple_of(ch * rows_c, rows_c)
        dbb_ref[...] = jnp.zeros_like(dbb_ref)
        dc_ref[...] = jnp.zeros_like(dc_ref)
        da_ref[...] = jnp.zeros_like(da_ref)

        xs_all[0:n_rows, :] = ein_ref[...]

        def project(ch, stage):
            stage[...] = jnp.dot(u_ref[pl.ds(start(ch), rows_c), :].astype(MXU_DTYPE), bb_ref[...],
                                 preferred_element_type=F32)

        def fwd_steps(stage, ch, carry, xs):
            for i in range(SCAN_CHUNK):
                blk = stage[i * n_rows:(i + 1) * n_rows, :]
                carry = (ar * carry[0] - ai * carry[1] + blk[:, :w], ar * carry[1] + ai * carry[0] + blk[:, w:])
                for half, val in enumerate(carry):
                    xs[i * n_rows:(i + 1) * n_rows, half * w:(half + 1) * w] = val
                    xs_all[pl.ds(start(ch) + (i + 1) * n_rows, n_rows), half * w:(half + 1) * w] = val
            return carry

        def add_dc(xs, ch):
            dc_ref[...] += lax.dot_general(dy_ref[pl.ds(start(ch), rows_c), :], xs[...].astype(MXU_DTYPE), _TN,
                                           preferred_element_type=F32)

        project(0, st_a)

        def fwd_pair(p, carry):
            project(2 * p + 1, st_b)
            carry = fwd_steps(st_a, 2 * p, carry, buf_a)
            add_dc(buf_a, 2 * p)
            project(jnp.minimum(2 * p + 2, last), st_a)
            carry = fwd_steps(st_b, 2 * p + 1, carry, buf_b)
            add_dc(buf_b, 2 * p + 1)
            return carry

        lax.fori_loop(0, n_chunks // 2, fwd_pair, (ein_ref[:, :w], ein_ref[:, w:]))

        def project_dx(ch, stage):
            stage[...] = jnp.dot(dy_ref[pl.ds(start(ch), rows_c), :], c_ref[...], preferred_element_type=F32)

        def back_steps(stage, carry, g_buf=None):
            for i in reversed(range(SCAN_CHUNK)):
                blk = stage[i * n_rows:(i + 1) * n_rows, :]
                carry = (blk[:, :w] + ar * carry[0] + ai * carry[1], blk[:, w:] + ar * carry[1] - ai * carry[0])
                if g_buf is not None:
                    g_buf[i * n_rows:(i + 1) * n_rows, :w] = carry[0]
                    g_buf[i * n_rows:(i + 1) * n_rows, w:] = carry[1]
            return carry

        def first_pair(p, carry):
            project_dx(last - 2 * p - 1, st_b)
            carry = back_steps(st_a, carry)
            project_dx(jnp.maximum(last - 2 * p - 2, 0), st_a)
            return back_steps(st_b, carry)

        project_dx(last, st_a)
        sr, si = lax.fori_loop(0, n_chunks // 2, first_pair, (zero, zero))
        gr0, gi0 = _segment_carry(sr, si, ar, ai, n_rows, True)

        def post(g_buf, ch):
            g = g_buf[...]
            xp = xs_all[pl.ds(start(ch), rows_c), :]
            da_ref[:, :w] += jnp.sum(g[:, :w] * xp[:, :w] + g[:, w:] * xp[:, w:], axis=0, keepdims=True)
            da_ref[:, w:] += jnp.sum(g[:, w:] * xp[:, :w] - g[:, :w] * xp[:, w:], axis=0, keepdims=True)
            gb = g.astype(MXU_DTYPE)
            du_ref[pl.ds(start(ch), rows_c), :] = (lax.dot_general(gb, bb_ref[...], _NT, preferred_element_type=F32)
                                                   + dus_ref[pl.ds(start(ch), rows_c), :])
            dbb_ref[...] += lax.dot_general(u_ref[pl.ds(start(ch), rows_c), :].astype(MXU_DTYPE), gb, _TN,
                                            preferred_element_type=F32)

        def second_pair(p, carry):
            c1 = last - 2 * p
            project_dx(c1 - 1, st_b)
            post(buf_b, jnp.minimum(c1 + 1, last))
            carry = back_steps(st_a, carry, buf_a)
            project_dx(jnp.maximum(c1 - 2, 0), st_a)
            post(buf_a, c1)
            return back_steps(st_b, carry, buf_b)

        project_dx(last, st_a)
        buf_b[...] = jnp.zeros_like(buf_b)
        lax.fori_loop(0, n_chunks // 2, second_pair, (gr0, gi0))
        post(buf_b, 0)

    col = lambda width: pl.BlockSpec((t, width), lambda c, j: (0, c))
    wgt = pl.BlockSpec((SCAN_CH, 2 * w), lambda c, j: (c, 0))
    row = pl.BlockSpec((1, 2 * w), lambda c, j: (0, c))
    return _grid_call(
        body, "ssm_bwd", (SCAN_NBLK, 1), [u_perm, dypre, du_skip, a_cat, bbc, cc, ein],
        [col(SCAN_CH), col(SCAN_CH), col(SCAN_CH), row, wgt, wgt, pl.BlockSpec((n_rows, 2 * w), lambda c, j: (0, c))],
        [col(SCAN_CH), row, wgt, wgt],
        [jax.ShapeDtypeStruct((t, SSM_W), F32), jax.ShapeDtypeStruct((1, 2 * N_STATE), F32),
         jax.ShapeDtypeStruct((SSM_W, 2 * w), F32), jax.ShapeDtypeStruct((SSM_W, 2 * w), F32)],
        56 * 1024 * 1024, comm,
        scratch=[pltpu.VMEM((t + n_rows, 2 * w), F32)] + [pltpu.VMEM((rows_c, 2 * w), F32)] * 4)


def _to_scan_rows(a, n_samples):
    c = a.shape[1]
    return a.reshape(n_samples, SCAN_SEG_PER_SAMPLE, SCAN_LEN, c).transpose(2, 0, 1, 3).reshape(-1, c)


def _from_scan_rows(a, n_samples):
    c = a.shape[1]
    return a.reshape(SCAN_LEN, n_samples, SCAN_SEG_PER_SAMPLE, c).transpose(1, 2, 0, 3).reshape(-1, c)


def _row_spec(tm, width):
    return pl.BlockSpec((tm, width), lambda i, j: (i, 0))


def _whole(arr):
    return pl.BlockSpec(arr.shape, lambda i, j: (0,) * arr.ndim)


def _proj_rope(x, g, w_in_t, tabs, comm=None):
    t = x.shape[0]
    tm = 256

    def body(x_ref, g_ref, w_ref, tc_ref, tlo_ref, thi_ref, h_ref, u_ref, gate_ref, *rest):
        qkv_refs, stage = rest[:9], rest[9]
        xv = x_ref[...]
        r = lax.rsqrt(jnp.mean(xv * xv, axis=-1, keepdims=True) + RMS_EPS)
        h = ((xv * r) * g_ref[...]).astype(BF16)
        h_ref[...] = h
        p = lax.dot_general(h.astype(MXU_DTYPE), w_ref[...], _NT, preferred_element_type=F32)
        u_ref[...] = p[:, QKV_W:QKV_W + SSM_W]
        gate_ref[...] = _sigmoid(p[:, QKV_W + SSM_W:])
        tc, tlo, thi = tc_ref[...], tlo_ref[...], thi_ref[...]
        n_ch = QKV_W // LANES
        for ch in range(n_ch):
            piece = p[:, _lane_chunk(ch)]
            stage[ch] = _rope_apply(piece, tc, tlo, thi) if ch < 2 * n_ch // 3 else piece
        halves = GROUP_W // LANES
        for grp, d in enumerate(DILATIONS):
            for which in range(3):
                out = qkv_refs[3 * grp + which]
                for res in range(d):
                    for half in range(halves):
                        ch = which * (n_ch // 3) + grp * halves + half
                        out[:, _lane_chunk(res * halves + half)] = _gather_residue(stage, ch, res, d, tm // d).astype(BF16)

    tab = pl.BlockSpec((tm, LANES), lambda i, j: (i % (SEQ // tm), 0))
    widths = [(D_MODEL, BF16), (SSM_W, F32), (2 * D_MODEL, F32)]
    out_specs = [_row_spec(tm, wd) for wd, _ in widths]
    out_shapes = [jax.ShapeDtypeStruct((t, wd), dt) for wd, dt in widths]
    for d in DILATIONS:
        out_specs += [_row_spec(tm // d, d * GROUP_W)] * 3
        out_shapes += [jax.ShapeDtypeStruct((t // d, d * GROUP_W), BF16)] * 3
    return _grid_call(
        body, "proj_rope", (t // tm, 1), [x, g, w_in_t, *tabs],
        [_row_spec(tm, D_MODEL), _whole(g), _whole(w_in_t), tab, tab, tab], out_specs, out_shapes, VMEM_BIG, comm,
        scratch=[pltpu.VMEM((QKV_W // LANES, tm, LANES), F32)])


def _out_rms(merged, w_out, x, g):
    t = x.shape[0]
    tm = 512

    def body(m_ref, w_ref, x_ref, g_ref, x1_ref, h_ref):
        x1 = x_ref[...] + jnp.dot(m_ref[...].astype(MXU_DTYPE), w_ref[...], preferred_element_type=F32)
        x1_ref[...] = x1
        r = lax.rsqrt(jnp.mean(x1 * x1, axis=-1, keepdims=True) + RMS_EPS)
        h_ref[...] = ((x1 * r) * g_ref[...]).astype(BF16)

    return _grid_call(
        body, "out_rms", (t // tm, 1), [merged, w_out, x, g],
        [_row_spec(tm, D_MODEL), _whole(w_out), _row_spec(tm, D_MODEL), _whole(g)],
        [_row_spec(tm, D_MODEL)] * 2, [jax.ShapeDtypeStruct((t, D_MODEL), F32), jax.ShapeDtypeStruct((t, D_MODEL), BF16)],
        VMEM_BIG)


FFN_TN = D_FF // 2
MXU_COLS = 256


def _ffn_in_swiglu(h2, w_gate_t, w_up_t, comm=None):
    t = h2.shape[0]
    tm = 512

    def body(h_ref, wg_ref, wu_ref, a_ref, b_ref, f_ref):
        h = h_ref[...].astype(MXU_DTYPE)
        for c0 in range(0, FFN_TN, MXU_COLS):
            sl = slice(c0, min(c0 + MXU_COLS, FFN_TN))
            a = lax.dot_general(h, wg_ref[sl, :], _NT, preferred_element_type=F32)
            b = lax.dot_general(h, wu_ref[sl, :], _NT, preferred_element_type=F32)
            a_ref[:, sl] = a
            b_ref[:, sl] = b
            f_ref[:, sl] = (a * _sigmoid(a) * b).astype(BF16)

    tile = pl.BlockSpec((tm, FFN_TN), lambda j, i: (i, j))
    wspec = pl.BlockSpec((FFN_TN, D_MODEL), lambda j, i: (j, 0))
    return _grid_call(
        body, "ffn_in_swiglu", (D_FF // FFN_TN, t // tm), [h2, w_gate_t, w_up_t],
        [pl.BlockSpec((tm, D_MODEL), lambda j, i: (i, 0)), wspec, wspec],
        [tile] * 3, [jax.ShapeDtypeStruct((t, D_FF), F32)] * 2 + [jax.ShapeDtypeStruct((t, D_FF), BF16)], VMEM_BIG, comm)


def _ffn_down_final(f, w_down, x1, target, g):
    t = x1.shape[0]
    tm = 256

    def body(f_ref, w_ref, x1_ref, t_ref, g_ref, dx_ref, dxb_ref, loss_ref, gg_ref):
        @pl.when(pl.program_id(0) == 0)
        def _():
            loss_ref[...] = jnp.zeros_like(loss_ref)
            gg_ref[...] = jnp.zeros_like(gg_ref)

        xv = x1_ref[...] + jnp.dot(f_ref[...].astype(MXU_DTYPE), w_ref[...], preferred_element_type=F32)
        gv = g_ref[...]
        r = lax.rsqrt(jnp.mean(xv * xv, axis=-1, keepdims=True) + RMS_EPS)
        n = xv * r
        diff = n * gv - t_ref[...]
        per_tok = jnp.mean(diff * diff, axis=-1, keepdims=True)
        loss_ref[...] += 0.5 * jnp.sum(per_tok, axis=0, keepdims=True)
        dy = diff / xv.shape[-1]
        gg_ref[...] += jnp.sum(dy * n, axis=0, keepdims=True)
        dn = dy * gv
        dx = r * (dn - n * jnp.mean(dn * n, axis=-1, keepdims=True))
        dx_ref[...] = dx
        dxb_ref[...] = dx.astype(BF16)

    acc = lambda shp: pl.BlockSpec(shp, lambda i, j: (0, 0))
    return _grid_call(
        body, "ffn_down_final", (t // tm, 1), [f, w_down, x1, target, g],
        [_row_spec(tm, D_FF), _whole(w_down), _row_spec(tm, D_MODEL), _row_spec(tm, D_MODEL), _whole(g)],
        [_row_spec(tm, D_MODEL)] * 2 + [acc((8, LANES)), acc((1, D_MODEL))],
        [jax.ShapeDtypeStruct((t, D_MODEL), F32), jax.ShapeDtypeStruct((t, D_MODEL), BF16),
         jax.ShapeDtypeStruct((8, LANES), F32), jax.ShapeDtypeStruct((1, D_MODEL), F32)], VMEM_BIG, sequential=True)


def _d_f_swiglu_bwd(dx2b, w_down, a, b):
    t = a.shape[0]
    tm = 512

    def body(dx_ref, w_ref, a_ref, b_ref, da_ref, db_ref):
        d = lax.dot_general(dx_ref[...], w_ref[...], _NT, preferred_element_type=F32)
        av, bv = a_ref[...], b_ref[...]
        sg = _sigmoid(av)
        da_ref[...] = (d * bv * sg * (1.0 + av * (1.0 - sg))).astype(BF16)
        db_ref[...] = (d * av * sg).astype(BF16)

    tile = pl.BlockSpec((tm, FFN_TN), lambda j, i: (i, j))
    return _grid_call(
        body, "d_f_swiglu_bwd", (D_FF // FFN_TN, t // tm), [dx2b, w_down, a, b],
        [pl.BlockSpec((tm, D_MODEL), lambda j, i: (i, 0)), pl.BlockSpec((FFN_TN, D_MODEL), lambda j, i: (j, 0)), tile, tile],
        [tile] * 2, [jax.ShapeDtypeStruct((t, D_FF), BF16)] * 2, VMEM_BIG)


def _mm_rms_bwd(operands, weights, x, g, dres, name, comm=None):
    t = x.shape[0]
    tm = 256
    n_op = len(operands)

    def body(*refs):
        a_refs, w_refs = refs[:n_op], refs[n_op:2 * n_op]
        x_ref, g_ref, dres_ref, dx_ref, dxb_ref, gg_ref = refs[2 * n_op:]

        @pl.when(pl.program_id(0) == 0)
        def _():
            gg_ref[...] = jnp.zeros_like(gg_ref)

        dh = None
        for a_ref, w_ref in zip(a_refs, w_refs):
            part = jnp.dot(a_ref[...].astype(MXU_DTYPE), w_ref[...], preferred_element_type=F32)
            dh = part if dh is None else dh + part
        xv = x_ref[...]
        r = lax.rsqrt(jnp.mean(xv * xv, axis=-1, keepdims=True) + RMS_EPS)
        n = xv * r
        gg_ref[...] += jnp.sum(dh * n, axis=0, keepdims=True)
        dn = dh * g_ref[...]
        dx = dres_ref[...] + r * (dn - n * jnp.mean(dn * n, axis=-1, keepdims=True))
        dx_ref[...] = dx
        dxb_ref[...] = dx.astype(BF16)

    d = x.shape[1]
    return _grid_call(
        body, name, (t // tm, 1), [*operands, *weights, x, g, dres],
        [_row_spec(tm, a.shape[1]) for a in operands] + [_whole(wk) for wk in weights]
        + [_row_spec(tm, d), _whole(g), _row_spec(tm, d)],
        [_row_spec(tm, d)] * 2 + [pl.BlockSpec((1, d), lambda i, j: (0, 0))],
        [jax.ShapeDtypeStruct((t, d), F32), jax.ShapeDtypeStruct((t, d), BF16), jax.ShapeDtypeStruct((1, d), F32)],
        VMEM_BIG, comm, sequential=True)


def _flat_small(small):
    perm_b = lambda a: a.reshape(SSM_GROUPS, SSM_STATE, SSM_CH).transpose(2, 0, 1).reshape(SSM_CH, N_STATE)
    perm_c = lambda a: a.reshape(SSM_GROUPS, SSM_CH, SSM_STATE).transpose(1, 0, 2).reshape(SSM_CH, N_STATE)
    return dict(
        g_mix=small["norm_mix_g"].reshape(1, D_MODEL), g_ffn=small["norm_ffn_g"].reshape(1, D_MODEL),
        g_fin=small["norm_final_g"].reshape(1, D_MODEL),
        lr=small["ssm_a_re"].reshape(1, N_STATE), li=small["ssm_a_im"].reshape(1, N_STATE),
        ldt=jnp.repeat(small["ssm_log_dt"].reshape(SSM_GROUPS), SSM_STATE).reshape(1, N_STATE),
        br=perm_b(small["ssm_b_re"]), bi=perm_b(small["ssm_b_im"]),
        cr=perm_c(small["ssm_c_re"]), ci=perm_c(small["ssm_c_im"]), dskip=small["ssm_d"].reshape(1, SSM_W))


AG_HOSTS = {"proj_rope": ("w_glu", "w_attn_out", "w_out"), "attn_fwd_g0": ("w_ffn_gate",), "attn_fwd_g1": ("w_ffn_up",),
            "ffn_in_swiglu": ("w_ffn_down",)}
HALVED = ("w_ffn_gate", "w_ffn_up", "w_in")
A2A_HOSTS = {"d_h2_rms": ("w_ffn_down",), "attn_bwd_g0": ("w_ffn_gate:0",), "attn_bwd_g1": ("w_ffn_gate:1",),
             "attn_bwd_g2": ("w_ffn_up:0",), "ssm_bwd": ("w_ffn_up:1", "w_out", "w_attn_out", "w_glu"),
             "mm_g_in1": ("w_in:0",), "d_h0_rms": ("w_in:1",)}
SMALL_HOST = "mm_g_in0"


def _local_step(x, target, w, small, shards=None):
    t = x.shape[0]
    n_samples = t // SEQ
    n_rows = n_samples * SCAN_SEG_PER_SAMPLE
    tabs = _rope_tables()
    w = dict(w)
    fs = _flat_small(small)
    g_mix, g_ffn, g_fin, dskip = fs["g_mix"], fs["g_ffn"], fs["g_fin"], fs["dskip"]
    a_cat, bbc, cc = _ssm_disc(fs["lr"], fs["li"], fs["ldt"], fs["br"], fs["bi"], fs["cr"], fs["ci"])
    big, recv, small_pack = {}, {}, []

    def comm_of(name):
        if shards is None:
            return None
        if name == SMALL_HOST:
            return _ag_comm([(small_pack[0], 0, 0)], [(N_DEV, *small_pack[0].shape)])
        if name in AG_HOSTS:
            names = AG_HOSTS[name]
            return _ag_comm([(shards[n], j, 0) for j, n in enumerate(names)], [(N_DEV, *shards[n].shape) for n in names])
        if name in A2A_HOSTS:
            return _a2a_comm([(big[n].reshape(N_DEV, -1, big[n].shape[1]), 0) for n in A2A_HOSTS[name]])
        return None

    def absorb(name, carried):
        if name == SMALL_HOST:
            recv["small"] = carried[0]
        for n, a3 in zip(AG_HOSTS.get(name, ()), carried):
            w[n] = a3.reshape(-1, a3.shape[2])
        for n, a3 in zip(A2A_HOSTS.get(name, ()), carried):
            recv[n] = a3

    def mm(a, b, mode, name, tm, tn, **kw):
        comm = comm_of(name)
        if comm is None:
            return _mm(a, b, mode, name, tm, tn, **kw)
        out, *carried = _mm(a, b, mode, name, tm, tn, comm=comm, **kw)
        absorb(name, carried)
        return out

    h0, u, gates, *rest = _proj_rope(x, g_mix, w["w_in"], tabs, comm_of("proj_rope"))
    qkv = [rest[3 * g:3 * g + 3] for g in range(3)]
    absorb("proj_rope", rest[9:])
    os_, lses = [], []
    for g in range(3):
        o_g, l_g, carried = _attn_fwd(*qkv[g], g, n_samples, comm_of(f"attn_fwd_g{g}"))
        absorb(f"attn_fwd_g{g}", carried)
        os_.append(o_g)
        lses.append(l_g)
    attn, lse_tot = _attn_merge(os_, lses)
    attn_d = mm(attn, w["w_attn_out"], "nt", "mm_attn_out", 512, D_MODEL)

    u_perm = _to_scan_rows(u, n_samples)
    ytot, yg_perm, ein = _ssm_fwd(u_perm, a_cat, bbc, cc, dskip, n_rows)
    yg = _from_scan_rows(yg_perm, n_samples)
    z = mm(yg, w["w_glu"], "nt", "mm_glu", 512, 2 * D_MODEL)

    merged = _mix(attn_d, z, gates)
    x1, h2 = _out_rms(merged, w["w_out"], x, g_ffn)
    ffn_a, ffn_b, f, *carried = _ffn_in_swiglu(h2, w["w_ffn_gate"], w["w_ffn_up"], comm_of("ffn_in_swiglu"))
    absorb("ffn_in_swiglu", carried)
    dx2, dx2b, loss_blk, g_gfin = _ffn_down_final(f, w["w_ffn_down"], x1, target, g_fin)

    da, db = _d_f_swiglu_bwd(dx2b, w["w_ffn_down"], ffn_a, ffn_b)
    big["w_ffn_down"] = mm(f, dx2b, "tn", "mm_g_down", 256, D_MODEL, out_dtype=BF16)
    half = D_MODEL // 2
    for hf in range(2):
        big[f"w_ffn_gate:{hf}"] = mm(da, h2, "tn", f"mm_g_gate{hf}", 256, half, out_dtype=BF16, cols=(hf * half, half))
        big[f"w_ffn_up:{hf}"] = mm(db, h2, "tn", f"mm_g_up{hf}", 256, half, out_dtype=BF16, cols=(hf * half, half))
    dx1, dx1b, g_gffn, *carried = _mm_rms_bwd([da, db], [w["w_ffn_gate"], w["w_ffn_up"]], x1, g_ffn, dx2, "d_h2_rms",
                                              comm_of("d_h2_rms"))
    absorb("d_h2_rms", carried)

    dmerged = mm(dx1b, w["w_out"], "nt", "mm_d_merged", 512, D_MODEL)
    big["w_out"] = mm(merged, dx1b, "tn", "mm_g_out", 256, D_MODEL, out_dtype=BF16)
    dattn_d, dz, dgpre = _mix_bwd(dmerged, gates, attn_d, z)

    dattn = mm(dattn_d, w["w_attn_out"], "nn", "mm_d_attn", 512, GROUP_W)
    big["w_attn_out"] = mm(dattn_d, attn, "tn", "mm_g_attn_out", 512, GROUP_W, out_dtype=BF16)
    cot = _attn_rowdot(dattn, attn, lse_tot)
    dqs, dks, dvs = [], [], []
    for g in range(3):
        dq_g, dk_g, dv_g, carried = _attn_bwd(*qkv[g], *cot[g], g, n_samples, comm_of(f"attn_bwd_g{g}"))
        absorb(f"attn_bwd_g{g}", carried)
        dqs.append(dq_g)
        dks.append(dk_g)
        dvs.append(dv_g)

    dyg = mm(dz, w["w_glu"], "nn", "mm_d_yg", 512, SSM_W)
    big["w_glu"] = mm(dz, yg, "tn", "mm_g_glu", 512, 512, out_dtype=BF16)
    dyg_perm = _to_scan_rows(dyg, n_samples)
    dypre, du_skip, g_dskip = _ssm_act_bwd(dyg_perm, ytot, u_perm, dskip)
    du_perm, da_cat, dbb_full, dc_full, *carried = _ssm_bwd(u_perm, dypre, du_skip, a_cat, bbc, cc, ein, n_rows,
                                                          comm_of("ssm_bwd"))
    absorb("ssm_bwd", carried)
    du = _from_scan_rows(du_perm, n_samples)
    g_lr, g_li, g_ldt, g_br, g_bi, g_cr, g_ci = _ssm_param_bwd(
        fs["lr"], fs["li"], fs["ldt"], fs["br"], fs["bi"], da_cat, dbb_full, dc_full)

    small_pack.append(_pack_small(dict(lr=g_lr, li=g_li, ldt=g_ldt, br=g_br, bi=g_bi, cr=g_cr, ci=g_ci, dskip=g_dskip,
                                       g_ffn=g_gffn, g_fin=g_gfin, loss=loss_blk)))

    dproj = _pack_dproj(dqs, dks, dvs, du, dgpre, tabs)
    for hf in range(2):
        big[f"w_in:{hf}"] = mm(dproj, h0, "tn", f"mm_g_in{hf}", 256, half, out_dtype=BF16, cols=(hf * half, half))
    grad_x, _, g_gmix, *carried = _mm_rms_bwd([dproj], [w["w_in"]], x, g_mix, dx1, "d_h0_rms", comm_of("d_h0_rms"))
    absorb("d_h0_rms", carried)
    return grad_x, (big if shards is None else recv), small_pack[0], g_gmix


_MESH = pl.DeviceIdType.MESH


def _all_gather(block, name):
    rows, lanes = block.shape

    def body(x_ref, out_ref, send_sems, recv_sems, local_sem):
        x, y, c = lax.axis_index("x"), lax.axis_index("y"), lax.axis_index("c")
        me, sibling = (x, y, c), (x, y, 1 - c)
        chips = [(1 - x, y), (x, 1 - y), (1 - x, 1 - y)]

        def slot(px, py, pc):
            return out_ref.at[4 * px + 2 * py + pc]

        def copy(k, blk, to, src=None):
            return pltpu.make_async_remote_copy(
                src_ref=slot(*blk) if src is None else src, dst_ref=slot(*blk), send_sem=send_sems.at[k],
                recv_sem=recv_sems.at[k], device_id=to, device_id_type=_MESH)

        mine = pltpu.make_async_copy(x_ref, slot(*me), local_sem)
        mine.start()
        first = [copy(0, me, sibling, src=x_ref)]
        first += [copy(1 + j, me, (*chip, c), src=x_ref) for j, chip in enumerate(chips)]
        for cp in first:
            cp.start()
        passed = [copy(4 + j, (*chip, c), sibling) for j, chip in enumerate(chips)]
        for j, chip in enumerate(chips):
            copy(1 + j, (*chip, c), me).wait_recv()
            passed[j].start()
        copy(0, sibling, me).wait_recv()
        for j, chip in enumerate(chips):
            copy(4 + j, (*chip, 1 - c), me).wait_recv()
        for cp in first + passed:
            cp.wait_send()
        mine.wait()

    return _pallas_call(
        body, name=name, out_shape=jax.ShapeDtypeStruct((N_DEV, rows, lanes), block.dtype),
        in_specs=[pl.BlockSpec(memory_space=pl.ANY)], out_specs=pl.BlockSpec(memory_space=pl.ANY),
        scratch_shapes=[pltpu.SemaphoreType.DMA((7,)), pltpu.SemaphoreType.DMA((7,)), pltpu.SemaphoreType.DMA],
    )(block)


def _ag_comm(items, bufs):
    def plan(in_refs, out_refs, send_sems, recv_sems, local_sems):
        x, y, c = lax.axis_index("x"), lax.axis_index("y"), lax.axis_index("c")
        me, sibling = (x, y, c), (x, y, 1 - c)
        chips = [(1 - x, y), (x, 1 - y), (1 - x, 1 - y)]
        plans = []
        for t, (_, buf, slot0) in enumerate(items):
            x_ref, out_ref = in_refs[t], out_refs[buf]

            def slot(px, py, pc, out_ref=out_ref, slot0=slot0):
                return out_ref.at[slot0 + 4 * px + 2 * py + pc]

            def copy(k, blk, to, src=None, t=t, slot=slot):
                return pltpu.make_async_remote_copy(
                    src_ref=slot(*blk) if src is None else src, dst_ref=slot(*blk), send_sem=send_sems.at[7 * t + k],
                    recv_sem=recv_sems.at[7 * t + k], device_id=to, device_id_type=_MESH)

            plans.append(dict(
                mine=pltpu.make_async_copy(x_ref, slot(*me), local_sems.at[t]),
                first=[copy(0, me, sibling, src=x_ref)] + [copy(1 + j, me, (*chip, c), src=x_ref)
                                                           for j, chip in enumerate(chips)],
                passed=[copy(4 + j, (*chip, c), sibling) for j, chip in enumerate(chips)],
                from_ici=[copy(1 + j, (*chip, c), me) for j, chip in enumerate(chips)],
                from_sibling=[copy(0, sibling, me)] + [copy(4 + j, (*chip, 1 - c), me) for j, chip in enumerate(chips)]))
        return plans

    def start(*refs):
        for p in plan(*refs):
            p["mine"].start()
            for cp in p["first"]:
                cp.start()

    def finish(*refs):
        plans = plan(*refs)
        for p in plans:
            for arrived, onward in zip(p["from_ici"], p["passed"]):
                arrived.wait_recv()
                onward.start()
        for p in plans:
            for arrived in p["from_sibling"]:
                arrived.wait_recv()
            for cp in p["first"] + p["passed"]:
                cp.wait_send()
            p["mine"].wait()

    dtype_of = {buf: shard.dtype for shard, buf, _ in items}
    out_shapes = [jax.ShapeDtypeStruct(b, dtype_of[j]) for j, b in enumerate(bufs)]
    return _Comm([it[0] for it in items], out_shapes, 7 * len(items), len(items), start, finish)


def _a2a_comm(items):
    def plan(in_refs, out_refs, send_sems, recv_sems, local_sems):
        x, y, c = lax.axis_index("x"), lax.axis_index("y"), lax.axis_index("c")
        my = 4 * x + 2 * y + c
        copies, locals_ = [], []
        for t, (_, slot0) in enumerate(items):
            s_ref, r_ref = in_refs[t], out_refs[t]
            locals_.append(pltpu.make_async_copy(s_ref.at[slot0 + my], r_ref.at[my], local_sems.at[t]))
            for kk in range(1, N_DEV):
                px = 1 - x if kk & 4 else x
                py = 1 - y if kk & 2 else y
                pc = 1 - c if kk & 1 else c
                copies.append(pltpu.make_async_remote_copy(
                    src_ref=s_ref.at[slot0 + 4 * px + 2 * py + pc], dst_ref=r_ref.at[my],
                    send_sem=send_sems.at[7 * t + kk - 1], recv_sem=recv_sems.at[7 * t + kk - 1],
                    device_id=(px, py, pc), device_id_type=_MESH))
        return copies, locals_

    def start(*refs):
        copies, locals_ = plan(*refs)
        for cp in locals_ + copies:
            cp.start()

    def finish(*refs):
        copies, locals_ = plan(*refs)
        for cp in copies + locals_:
            cp.wait()

    out_shapes = [jax.ShapeDtypeStruct((N_DEV,) + it[0].shape[1:], it[0].dtype) for it in items]
    return _Comm([it[0] for it in items], out_shapes, 7 * len(items), len(items), start, finish)


def _adam_math(g, w, m, v):
    m_new = ADAM_B1 * m + (1.0 - ADAM_B1) * g
    v_new = ADAM_B2 * v + (1.0 - ADAM_B2) * jnp.square(g)
    m_hat = m_new / (1.0 - ADAM_B1 ** ADAM_STEP)
    v_hat = v_new / (1.0 - ADAM_B2 ** ADAM_STEP)
    return -ADAM_LR * (m_hat / (jnp.sqrt(v_hat) + ADAM_EPS) + ADAM_WD * w), m_new, v_new


def _sum_partials(parts, name, tm):
    n, rows, _ = parts[0].shape
    widths = [p.shape[2] for p in parts]

    def body(*refs):
        g_ref, off = refs[-1], 0
        for p_ref, wd in zip(refs[:-1], widths):
            g = p_ref[0].astype(F32)
            for s in range(1, n):
                g = g + p_ref[s].astype(F32)
            g_ref[:, off:off + wd] = g
            off += wd

    return _pallas_call(
        body, name=name, grid=(rows // tm,), in_specs=[pl.BlockSpec((n, tm, wd), lambda i: (0, i, 0)) for wd in widths],
        out_specs=pl.BlockSpec((tm, sum(widths)), lambda i: (i, 0)),
        out_shape=jax.ShapeDtypeStruct((rows, sum(widths)), F32),
        compiler_params=pltpu.CompilerParams(dimension_semantics=("parallel",), vmem_limit_bytes=VMEM_MID),
    )(*parts)


def _adam(partials, w, m, v, name, tm):
    n, rows, cols = partials.shape

    def body(p_ref, w_ref, m_ref, v_ref, g_ref, d_ref, nm_ref, nv_ref):
        g = p_ref[0].astype(F32)
        for s in range(1, n):
            g = g + p_ref[s].astype(F32)
        g_ref[...] = g
        d_ref[...], nm_ref[...], nv_ref[...] = _adam_math(g, w_ref[...], m_ref[...], v_ref[...])

    assert rows % tm == 0
    row = pl.BlockSpec((tm, cols), lambda i: (i, 0))
    shp = jax.ShapeDtypeStruct((rows, cols), F32)
    return _pallas_call(
        body, name=name, grid=(rows // tm,),
        in_specs=[pl.BlockSpec((n, tm, cols), lambda i: (0, i, 0)), row, row, row],
        out_specs=[row] * 4, out_shape=[shp] * 4,
        compiler_params=pltpu.CompilerParams(dimension_semantics=("parallel",), vmem_limit_bytes=VMEM_MID),
    )(partials, w, m, v)


_PK_LR, _PK_LI, _PK_GAINS, _PK_MISC, _PK_BR, _PK_BI, _PK_CR, _PK_CI, _PK_ROWS = 0, 1, 2, 3, 8, 24, 40, 56, 72
_PK_LDT_LANE, _PK_LOSS_LANE = D_MODEL + SSM_W, D_MODEL + SSM_W + LANES


def _pack_small(sg):
    names = ("lr", "li", "g_ffn", "g_fin", "dskip", "ldt", "loss", "br", "bi", "cr", "ci")

    def body(lr, li, gffn, gfin, dskip, ldt, loss, br, bi, cr, ci, o_ref):
        o_ref[...] = jnp.zeros_like(o_ref)
        o_ref[_PK_LR:_PK_LR + 1, :] = lr[...]
        o_ref[_PK_LI:_PK_LI + 1, :] = li[...]
        o_ref[_PK_GAINS:_PK_GAINS + 1, D_MODEL:] = gffn[...]
        o_ref[_PK_MISC:_PK_MISC + 1, :D_MODEL] = gfin[...]
        o_ref[_PK_MISC:_PK_MISC + 1, D_MODEL:D_MODEL + SSM_W] = dskip[...]
        o_ref[_PK_MISC:_PK_MISC + 1, _PK_LDT_LANE:_PK_LDT_LANE + LANES] = ldt[0:1, :]
        o_ref[_PK_MISC:_PK_MISC + 1, _PK_LOSS_LANE:_PK_LOSS_LANE + LANES] = loss[0:1, :]
        o_ref[_PK_BR:_PK_BR + SSM_CH, :] = br[...]
        o_ref[_PK_BI:_PK_BI + SSM_CH, :] = bi[...]
        o_ref[_PK_CR:_PK_CR + SSM_CH, :] = cr[...]
        o_ref[_PK_CI:_PK_CI + SSM_CH, :] = ci[...]

    return _pallas_call(body, name="pack_small", out_shape=jax.ShapeDtypeStruct((_PK_ROWS, N_STATE), F32))(
        *[sg[n] for n in names])


def _unpack_small(s, g_mix):
    unflat_b = lambda a: a.reshape(SSM_CH, SSM_GROUPS, SSM_STATE).transpose(1, 2, 0)[None]
    unflat_c = lambda a: a.reshape(SSM_CH, SSM_GROUPS, SSM_STATE).transpose(1, 0, 2)[None]
    grads = {
        "norm_mix_g": g_mix, "norm_ffn_g": s[_PK_GAINS, D_MODEL:].reshape(1, D_MODEL),
        "norm_final_g": s[_PK_MISC, :D_MODEL],
        "ssm_a_re": s[_PK_LR].reshape(1, SSM_GROUPS, SSM_STATE), "ssm_a_im": s[_PK_LI].reshape(1, SSM_GROUPS, SSM_STATE),
        "ssm_log_dt": s[_PK_MISC, _PK_LDT_LANE:_PK_LDT_LANE + SSM_GROUPS].reshape(1, SSM_GROUPS),
        "ssm_d": s[_PK_MISC, D_MODEL:D_MODEL + SSM_W].reshape(1, SSM_GROUPS, SSM_CH),
        "ssm_b_re": unflat_b(s[_PK_BR:_PK_BR + SSM_CH]), "ssm_b_im": unflat_b(s[_PK_BI:_PK_BI + SSM_CH]),
        "ssm_c_re": unflat_c(s[_PK_CR:_PK_CR + SSM_CH]), "ssm_c_im": unflat_c(s[_PK_CI:_PK_CI + SSM_CH]),
    }
    return s[_PK_MISC, _PK_LOSS_LANE], grads


def _adam_small(grads, wts, moms, vars_):
    n = len(SMALL_WEIGHTS)
    as2d = lambda a: a.reshape(1, -1) if a.ndim == 1 else a

    def body(*refs):
        ins, outs = refs[:4 * n], refs[4 * n:]
        for i in range(n):
            g, w, m, v = (ins[j * n + i][...] for j in range(4))
            outs[i][...], outs[n + i][...], outs[2 * n + i][...] = _adam_math(g, w, m, v)

    operands = [as2d(d[k]) for d in (grads, wts, moms, vars_) for k in SMALL_WEIGHTS]
    shapes = [jax.ShapeDtypeStruct(as2d(wts[k]).shape, F32) for k in SMALL_WEIGHTS] * 3
    res = _pallas_call(body, name="adam_small", out_shape=shapes,
                         compiler_params=pltpu.CompilerParams(vmem_limit_bytes=VMEM_BIG))(*operands)
    out = {}
    for j, kind in enumerate(("delta", "new_m", "new_v")):
        for i, k in enumerate(SMALL_WEIGHTS):
            out[kind, k] = res[j * n + i].reshape(wts[k].shape)
    return out


def kernel(x, norm_mix_g, w_in, ssm_a_re, ssm_a_im, ssm_log_dt, ssm_b_re, ssm_b_im, ssm_c_re, ssm_c_im, ssm_d, w_glu, w_attn_out, w_out, norm_ffn_g, w_ffn_gate, w_ffn_up, w_ffn_down, norm_final_g, loss_target, m_norm_mix_g, m_w_in, m_ssm_a_re, m_ssm_a_im, m_ssm_log_dt, m_ssm_b_re, m_ssm_b_im, m_ssm_c_re, m_ssm_c_im, m_ssm_d, m_w_glu, m_w_attn_out, m_w_out, m_norm_ffn_g, m_w_ffn_gate, m_w_ffn_up, m_w_ffn_down, m_norm_final_g, v_norm_mix_g, v_w_in, v_ssm_a_re, v_ssm_a_im, v_ssm_log_dt, v_ssm_b_re, v_ssm_b_im, v_ssm_c_re, v_ssm_c_im, v_ssm_d, v_w_glu, v_w_attn_out, v_w_out, v_norm_ffn_g, v_w_ffn_gate, v_w_ffn_up, v_w_ffn_down, v_norm_final_g):
    args = dict(locals())
    wts = {n: args[n] for n in ALL_WEIGHTS}
    moms = {n: args["m_" + n] for n in ALL_WEIGHTS}
    vars_ = {n: args["v_" + n] for n in ALL_WEIGHTS}
    n_samples = x.shape[0]
    t = n_samples * SEQ

    shards = {n: (wts[n][0] if n in ROW_SHARDED else wts[n][0].T).astype(BF16) for n in BIG_WEIGHTS}
    w_in_t = _all_gather(shards["w_in"], "allgather_w_in").reshape(IN_W, D_MODEL)

    small = {n: wts[n] for n in SMALL_WEIGHTS}
    grad_x, recv, _, g_mix_part = _local_step(x.reshape(t, D_MODEL), loss_target.reshape(t, D_MODEL), {"w_in": w_in_t},
                                              small, shards)

    results = {}
    for n in BIG_WEIGHTS:
        c, k = shards[n].shape
        w2, m2, v2 = wts[n][0], moms[n][0], vars_[n][0]
        if n in ROW_SHARDED:
            res = _adam(recv[n], w2, m2, v2, "adam_" + n, c // 2)
        else:
            parts = [recv[f"{n}:{hf}"] for hf in range(2)] if n in HALVED else [recv[n]]
            g_t = _sum_partials(parts, "sum_" + n, c // 2)
            res = _adam(g_t.T[None], w2, m2, v2, "adam_" + n, k // 2)
        for kind, a in zip(("grad", "delta", "new_m", "new_v"), res):
            results[kind, n] = a[None]

    g_mix_all = _all_gather(jnp.pad(g_mix_part, ((0, 7), (0, 0))), "allgather_g_mix")
    g_mix = _sum_partials([g_mix_all], "sum_g_mix", 8)[0:1]
    loss, sgrads = _unpack_small(_sum_partials([recv["small"]], "sum_small", _PK_ROWS), g_mix)
    for n in SMALL_WEIGHTS:
        results["grad", n] = sgrads[n]
    results.update(_adam_small(sgrads, wts, moms, vars_))
    outs = [loss, grad_x.reshape(x.shape)]
    for kind in ("grad", "delta", "new_m", "new_v"):
        outs += [results[kind, n] for n in ALL_WEIGHTS]
    return tuple(outs)
```

```python
import functools
import math

import jax
import jax.numpy as jnp
from jax import lax
from jax.experimental import pallas as pl
from jax.experimental.pallas import tpu as pltpu

F32 = jnp.float32
BF16 = jnp.bfloat16
MXU_DTYPE = jnp.bfloat16

N_DEV = 8
D_MODEL = 1024
SEQ = 2048
HEAD_DIM = 64
HEADS_PER_GROUP = 4
GROUP_W = HEADS_PER_GROUP * HEAD_DIM
DILATIONS = (1, 4, 16)
QKV_W = 3 * len(DILATIONS) * GROUP_W
Q_W = len(DILATIONS) * GROUP_W
ATT_BLOCK = 128
ROPE_DIM = 16
ROPE_THETA = 500000.0
SSM_W = 512
SSM_GROUPS = 32
SSM_CH = 16
SSM_STATE = 64
N_STATE = SSM_GROUPS * SSM_STATE
D_FF = 2816
IN_W = QKV_W + SSM_W + 2 * D_MODEL
RMS_EPS = 1e-6
NEG_INF = -1e30
LANES = 128

SCAN_SEG_PER_SAMPLE = 8
SCAN_LEN = SEQ // SCAN_SEG_PER_SAMPLE
SCAN_WC = 512
SCAN_NBLK = N_STATE // SCAN_WC
SCAN_CH = SSM_W // SCAN_NBLK
SCAN_CHUNK = 32

ADAM_LR = 0.001
ADAM_B1 = 0.9
ADAM_B2 = 0.999
ADAM_EPS = 1e-08
ADAM_WD = 0.01
ADAM_STEP = 10

VMEM_BIG = 48 * 1024 * 1024
VMEM_MID = 32 * 1024 * 1024

BIG_WEIGHTS = ("w_in", "w_glu", "w_attn_out", "w_out", "w_ffn_gate", "w_ffn_up", "w_ffn_down")
ROW_SHARDED = ("w_out", "w_ffn_down")
SMALL_WEIGHTS = ("norm_mix_g", "ssm_a_re", "ssm_a_im", "ssm_log_dt", "ssm_b_re", "ssm_b_im", "ssm_c_re", "ssm_c_im",
                 "ssm_d", "norm_ffn_g", "norm_final_g")
ALL_WEIGHTS = ("norm_mix_g", "w_in", "ssm_a_re", "ssm_a_im", "ssm_log_dt", "ssm_b_re", "ssm_b_im", "ssm_c_re", "ssm_c_im",
               "ssm_d", "w_glu", "w_attn_out", "w_out", "norm_ffn_g", "w_ffn_gate", "w_ffn_up", "w_ffn_down", "norm_final_g")


def _sigmoid(x):
    return 1.0 / (1.0 + jnp.exp(-x))


def _pallas_call(body, *, out_shape, **kw):
    single = not isinstance(out_shape, (list, tuple))
    shapes = [pltpu.HBM(s.shape, s.dtype) for s in ([out_shape] if single else out_shape)]
    call = pl.pallas_call(body, out_shape=shapes[0] if single else shapes, **kw)
    return lambda *operands: call(*[pltpu.with_memory_space_constraint(o, pltpu.HBM) for o in operands])


class _Comm:
    def __init__(self, ins, out_shapes, n_sem, n_local, start, finish):
        self.ins, self.out_shapes, self.n_sem, self.n_local = ins, out_shapes, n_sem, n_local
        self.start, self.finish = start, finish


def _mm(a, b, mode, name, tm, tn, out_dtype=F32, add=None, vmem=VMEM_BIG, comm=None, cols=None):
    if mode == "nn":
        (m, k), (_, n) = a.shape, b.shape
        a_spec = pl.BlockSpec((tm, k), lambda i, j: (i, 0))
        b_spec = pl.BlockSpec((k, tn), lambda i, j: (0, j))
        dims = (((1,), (0,)), ((), ()))
    elif mode == "nt":
        (m, k), (n, _) = a.shape, b.shape
        a_spec = pl.BlockSpec((tm, k), lambda i, j: (i, 0))
        b_spec = pl.BlockSpec((tn, k), lambda i, j: (j, 0))
        dims = (((1,), (1,)), ((), ()))
    else:
        (k, m), (_, n) = a.shape, b.shape
        first, n = cols if cols else (0, n)
        a_spec = pl.BlockSpec((k, tm), lambda i, j: (0, i))
        b_spec = pl.BlockSpec((k, tn), lambda i, j: (0, j + first // tn))
        dims = (((0,), (0,)), ((), ()))
    assert m % tm == 0 and n % tn == 0, (name, m, n, tm, tn)
    o_spec = pl.BlockSpec((tm, tn), lambda i, j: (i, j))
    has_add = add is not None

    def body(*refs):
        a_ref, b_ref, o_ref = refs[0], refs[1], refs[-1]
        acc = lax.dot_general(a_ref[...].astype(MXU_DTYPE), b_ref[...].astype(MXU_DTYPE), dims,
                              preferred_element_type=F32)
        if has_add:
            acc = acc + refs[2][...]
        o_ref[...] = acc.astype(out_dtype)

    ins = [a, b] + ([add] if has_add else [])
    in_specs = [a_spec, b_spec] + ([o_spec] if has_add else [])
    return _grid_call(body, name, (m // tm, n // tn), ins, in_specs, [o_spec],
                      [jax.ShapeDtypeStruct((m, n), out_dtype)], vmem, comm)


def _grid_call(body, name, grid, ins, in_specs, out_specs, out_shapes, vmem, comm=None, sequential=False, scratch=()):
    if comm is None:
        single = len(out_shapes) == 1
        semantics = ("arbitrary", "arbitrary") if sequential else ("parallel", "parallel")
        return _pallas_call(
            body, name=name, grid=grid, in_specs=in_specs, out_specs=out_specs[0] if single else out_specs,
            out_shape=out_shapes[0] if single else out_shapes, scratch_shapes=list(scratch),
            compiler_params=pltpu.CompilerParams(dimension_semantics=semantics, vmem_limit_bytes=vmem),
        )(*ins)
    n_in, n_out, n_cin, n_cout = len(ins), len(out_shapes), len(comm.ins), len(comm.out_shapes)
    n_io = n_in + n_cin + n_out + n_cout

    def carrying(*refs):
        own = refs[:n_in] + refs[n_in + n_cin:n_in + n_cin + n_out] + refs[n_io:len(refs) - 3]
        c_args = (refs[n_in:n_in + n_cin], refs[n_in + n_cin + n_out:n_io], *refs[-3:])

        @pl.when((pl.program_id(0) == 0) & (pl.program_id(1) == 0))
        def _():
            comm.start(*c_args)

        body(*own)

        @pl.when((pl.program_id(0) == grid[0] - 1) & (pl.program_id(1) == grid[1] - 1))
        def _():
            comm.finish(*c_args)

    hbm = pl.BlockSpec(memory_space=pl.ANY)
    return _pallas_call(
        carrying, name=name, grid=grid, in_specs=list(in_specs) + [hbm] * n_cin,
        out_specs=list(out_specs) + [hbm] * n_cout, out_shape=list(out_shapes) + list(comm.out_shapes),
        scratch_shapes=list(scratch) + [pltpu.SemaphoreType.DMA((comm.n_sem,)), pltpu.SemaphoreType.DMA((comm.n_sem,)),
                                        pltpu.SemaphoreType.DMA((comm.n_local,))],
        compiler_params=pltpu.CompilerParams(dimension_semantics=("arbitrary", "arbitrary"), vmem_limit_bytes=vmem),
    )(*ins, *comm.ins)


def _rows(body, name, n_rows, tm, ins, outs, vmem=VMEM_MID, scratch=()):
    assert n_rows % tm == 0
    arrays, in_specs = [], []
    for kind, arr in ins:
        arrays.append(arr)
        if kind == "row":
            assert n_rows % arr.shape[0] == 0, (name, arr.shape)
            in_specs.append(pl.BlockSpec((tm * arr.shape[0] // n_rows, arr.shape[1]), lambda i: (i, 0)))
        elif kind == "tab":
            nblk = arr.shape[0] // tm
            in_specs.append(pl.BlockSpec((tm, arr.shape[1]), lambda i, nblk=nblk: (i % nblk, 0)))
        else:
            in_specs.append(pl.BlockSpec(arr.shape, lambda i, nd=arr.ndim: (0,) * nd))
    out_specs, out_shape = [], []
    for kind, shp, dt in outs:
        if kind == "row":
            out_specs.append(pl.BlockSpec((tm, shp), lambda i: (i, 0)))
            out_shape.append(jax.ShapeDtypeStruct((n_rows, shp), dt))
        elif kind == "dil":
            d, wd = shp
            out_specs.append(pl.BlockSpec((tm // d, d * wd), lambda i: (i, 0)))
            out_shape.append(jax.ShapeDtypeStruct((n_rows // d, d * wd), dt))
        else:
            out_specs.append(pl.BlockSpec(shp, lambda i, nd=len(shp): (0,) * nd))
            out_shape.append(jax.ShapeDtypeStruct(shp, dt))
    res = _pallas_call(
        body, name=name, grid=(n_rows // tm,), in_specs=in_specs, out_specs=out_specs, out_shape=out_shape,
        scratch_shapes=list(scratch),
        compiler_params=pltpu.CompilerParams(dimension_semantics=("arbitrary",), vmem_limit_bytes=vmem),
    )(*arrays)
    return res


def _gather_residue(stage, ch, r, d, n):
    return stage[ch, pl.ds(r, n, stride=d), :] if d > 1 else stage[ch]


def _scatter_residue(stage, ch, r, d, n, val):
    if d > 1:
        stage[ch, pl.ds(r, n, stride=d), :] = val
    else:
        stage[ch] = val


def _lane_chunk(ch):
    return slice(ch * LANES, (ch + 1) * LANES)


def _first_step():
    return pl.program_id(0) == 0


def _rope_tables():
    half = ROPE_DIM // 2
    inv = jnp.power(jnp.float32(ROPE_THETA), -jnp.arange(half, dtype=F32) * 2.0 / ROPE_DIM)
    ang = jnp.arange(SEQ, dtype=F32)[:, None] * inv[None, :]
    lane = jnp.arange(LANES) % HEAD_DIM
    cosl = jnp.cos(ang)[:, lane % half]
    sinl = jnp.sin(ang)[:, lane % half]
    tab_c = jnp.where(lane < ROPE_DIM, cosl, 1.0)
    tab_lo = jnp.where(lane < half, -sinl, 0.0)
    tab_hi = jnp.where((lane >= half) & (lane < ROPE_DIM), sinl, 0.0)
    return tab_c.astype(F32), tab_lo.astype(F32), tab_hi.astype(F32)


def _rope_apply(t, tc, tlo, thi):
    half = ROPE_DIM // 2
    return t * tc + pltpu.roll(t, LANES - half, 1) * tlo + pltpu.roll(t, half, 1) * thi


def _rope_transpose(dt, tc, tlo, thi):
    half = ROPE_DIM // 2
    return dt * tc + pltpu.roll(dt * tlo, half, 1) + pltpu.roll(dt * thi, LANES - half, 1)


def _pack_dproj(dqs, dks, dvs, du, dgpre, tabs):
    tm = 256

    def body(*refs):
        dq_refs, dk_refs, dv_refs = refs[0:3], refs[3:6], refs[6:9]
        du_ref, dg_ref, tc_ref, tlo_ref, thi_ref, o_ref, stage = refs[9:16]
        n_ch = QKV_W // LANES
        halves = GROUP_W // LANES
        for grp, d in enumerate(DILATIONS):
            for which, src in enumerate((dq_refs[grp], dk_refs[grp], dv_refs[grp])):
                for res in range(d):
                    for half in range(halves):
                        _scatter_residue(stage, which * (n_ch // 3) + grp * halves + half, res, d, tm // d,
                                         src[:, _lane_chunk(res * halves + half)])
        tc, tlo, thi = tc_ref[...], tlo_ref[...], thi_ref[...]
        for ch in range(n_ch):
            piece = stage[ch]
            o_ref[:, _lane_chunk(ch)] = (_rope_transpose(piece, tc, tlo, thi) if ch < 2 * n_ch // 3 else piece).astype(BF16)
        o_ref[:, QKV_W:QKV_W + SSM_W] = du_ref[...].astype(BF16)
        o_ref[:, QKV_W + SSM_W:] = dg_ref[...].astype(BF16)

    t = du.shape[0]
    ins = [("row", a) for a in (*dqs, *dks, *dvs, du, dgpre)] + [("tab", tb) for tb in tabs]
    return _rows(body, "pack_dproj", t, tm, ins, [("row", IN_W, BF16)],
                 scratch=[pltpu.VMEM((QKV_W // LANES, tm, LANES), F32)])[0]


def _attn_merge(os_, lses):
    tm = 256

    halves = GROUP_W // LANES

    def body(o0, o1, o2, l0, l1, l2, a_ref, lt_ref, nat):
        for grp, d in enumerate(DILATIONS[1:], start=1):
            for j, src in enumerate(((o0, o1, o2)[grp], (l0, l1, l2)[grp])):
                for res in range(d):
                    for half in range(halves):
                        _scatter_residue(nat, (grp - 1) * 4 + j * 2 + half, res, d, tm // d,
                                         src[:, _lane_chunk(res * halves + half)])
        for half in range(halves):
            sl = _lane_chunk(half)
            la, lb, lc = l0[:, sl], nat[2 + half], nat[6 + half]
            m = jnp.maximum(jnp.maximum(la, lb), lc)
            ea, eb, ec = jnp.exp(la - m), jnp.exp(lb - m), jnp.exp(lc - m)
            ssum = ea + eb + ec
            a_ref[:, sl] = (ea / ssum) * o0[:, sl] + (eb / ssum) * nat[half] + (ec / ssum) * nat[4 + half]
            lt_ref[:, sl] = m + jnp.log(ssum)

    t = os_[0].shape[0]
    return _rows(body, "attn_merge", t, tm, [("row", a) for a in (*os_, *lses)],
                 [("row", GROUP_W, F32), ("row", GROUP_W, F32)], scratch=[pltpu.VMEM((8, tm, LANES), F32)])


def _head_sum_matrix():
    r = jnp.arange(GROUP_W) // HEAD_DIM
    return (r[:, None] == r[None, :]).astype(F32)


def _attn_rowdot(dattn, attn, lse_tot):
    tm = 256

    halves = GROUP_W // LANES

    def body(da_ref, a_ref, lt_ref, ones_ref, rd_ref, *rest):
        dil, stage = rest[:6], rest[6]
        rd = jnp.dot(da_ref[...] * a_ref[...], ones_ref[...], preferred_element_type=F32, precision=lax.Precision.HIGHEST)
        rd_ref[...] = rd
        for half in range(halves):
            stage[half] = da_ref[:, _lane_chunk(half)]
            stage[2 + half] = lt_ref[:, _lane_chunk(half)]
            stage[4 + half] = rd[:, _lane_chunk(half)]
        for grp, d in enumerate(DILATIONS[1:], start=1):
            for j in range(3):
                for res in range(d):
                    for half in range(halves):
                        dil[3 * (grp - 1) + j][:, _lane_chunk(res * halves + half)] = _gather_residue(
                            stage, 2 * j + half, res, d, tm // d)

    t = attn.shape[0]
    outs = [("row", GROUP_W, F32)] + [("dil", (d, GROUP_W), F32) for d in DILATIONS[1:] for _ in range(3)]
    rd, *dil = _rows(body, "attn_rowdot", t, tm,
                     [("row", dattn), ("row", attn), ("row", lse_tot), ("const", _head_sum_matrix())], outs,
                     scratch=[pltpu.VMEM((6, tm, LANES), F32)])
    return [(dattn, lse_tot, rd), tuple(dil[:3]), tuple(dil[3:])]


_GELU_C = math.sqrt(2.0 / math.pi)


def _ssm_act_bwd(dyg, ytot, u_perm, dskip):
    def body(dyg_ref, yt_ref, u_ref, d_ref, dy_ref, dus_ref, dd_ref):
        @pl.when(_first_step())
        def _():
            dd_ref[...] = jnp.zeros_like(dd_ref)

        yt = yt_ref[...]
        th = jnp.tanh(_GELU_C * (yt + 0.044715 * (yt * yt * yt)))
        dgelu = 0.5 * (1.0 + th) + 0.5 * yt * (1.0 - th * th) * _GELU_C * (1.0 + 3.0 * 0.044715 * yt * yt)
        dy = dyg_ref[...] * dgelu
        dy_ref[...] = dy.astype(BF16)
        dus_ref[...] = dy * d_ref[...]
        dd_ref[...] += jnp.sum(dy * u_ref[...], axis=0, keepdims=True)

    t = dyg.shape[0]
    return _rows(body, "ssm_act_bwd", t, 512, [("row", dyg), ("row", ytot), ("row", u_perm), ("const", dskip)],
                 [("row", SSM_W, BF16), ("row", SSM_W, F32), ("acc", (1, SSM_W), F32)])


def _head_masks():
    lane = lax.broadcasted_iota(jnp.int32, (1, GROUP_W), 1)
    return [(lane // HEAD_DIM) == h for h in range(HEADS_PER_GROUP)]


def _stack_heads(blk, masks, fill=0.0):
    return jnp.concatenate([jnp.where(mk, blk, jnp.full_like(blk, fill)) for mk in masks], axis=0)


def _unstack_heads(stacked, masks):
    rows = stacked.shape[0] // len(masks)
    out = stacked[:rows]
    for h in range(1, len(masks)):
        out = jnp.where(masks[h], stacked[h * rows:(h + 1) * rows], out)
    return out


def _band_mask(first):
    nk = ATT_BLOCK if first else 2 * ATT_BLOCK
    qi = lax.broadcasted_iota(jnp.int32, (ATT_BLOCK, nk), 0)
    ki = lax.broadcasted_iota(jnp.int32, (ATT_BLOCK, nk), 1)
    dist = qi - ki + (0 if first else ATT_BLOCK)
    return (dist >= 0) & (dist <= ATT_BLOCK)


_NT = (((1,), (1,)), ((), ()))
_TN = (((0,), (0,)), ((), ()))


def _attn_fwd(q, k, v, group, n_samples, comm=None):
    d = DILATIONS[group]
    length = SEQ // d
    nb = length // ATT_BLOCK

    def body(q_ref, k_ref, v_ref, o_ref, l_ref):
        masks = _head_masks()

        def block(qs, ks, first):
            nk = ATT_BLOCK if first else 2 * ATT_BLOCK
            qb = q_ref[0, pl.ds(qs, ATT_BLOCK), :]
            kc = k_ref[0, pl.ds(ks, nk), :]
            vc = v_ref[0, pl.ds(ks, nk), :]
            q4 = _stack_heads(qb, masks)
            valid = jnp.tile(_band_mask(first), (HEADS_PER_GROUP, 1))
            s = lax.dot_general(q4, kc, _NT, preferred_element_type=F32) * (HEAD_DIM ** -0.5)
            s = jnp.where(valid, s, NEG_INF)
            m = jnp.max(s, axis=-1, keepdims=True)
            p = jnp.exp(s - m)
            l = jnp.sum(p, axis=-1, keepdims=True)
            o4 = jnp.dot(p.astype(MXU_DTYPE), vc, preferred_element_type=F32) / l
            lse4 = jnp.broadcast_to(m + jnp.log(l), o4.shape)
            o_ref[0, pl.ds(qs, ATT_BLOCK), :] = _unstack_heads(o4, masks)
            l_ref[0, pl.ds(qs, ATT_BLOCK), :] = _unstack_heads(lse4, masks)

        block(0, 0, True)
        if nb > 1:
            def loop(n, carry):
                block(pl.multiple_of(n * ATT_BLOCK, ATT_BLOCK), pl.multiple_of((n - 1) * ATT_BLOCK, ATT_BLOCK), False)
                return carry

            lax.fori_loop(1, nb, loop, 0)

    per_sample = lambda a: a.reshape(n_samples, length, d * GROUP_W)
    spec = pl.BlockSpec((1, length, GROUP_W), lambda b, r: (b, 0, r))
    shp = jax.ShapeDtypeStruct((n_samples, length, d * GROUP_W), F32)
    o, lse, *carried = _grid_call(body, f"attn_fwd_g{group}", (n_samples, d), [per_sample(a) for a in (q, k, v)],
                                  [spec] * 3, [spec] * 2, [shp, shp], VMEM_MID, comm)
    flat = lambda a: a.reshape(n_samples * length, d * GROUP_W)
    return flat(o), flat(lse), carried


def _attn_bwd(q, k, v, dattn, lse_tot, rowdot, group, n_samples, comm=None):
    d = DILATIONS[group]
    length = SEQ // d
    nb = length // ATT_BLOCK

    def body(q_ref, k_ref, v_ref, da_ref, lt_ref, rd_ref, dq_ref, dk_ref, dv_ref):
        masks = _head_masks()
        dk_ref[...] = jnp.zeros_like(dk_ref)
        dv_ref[...] = jnp.zeros_like(dv_ref)

        def block(qs, ks, first):
            nk = ATT_BLOCK if first else 2 * ATT_BLOCK
            qb = q_ref[0, pl.ds(qs, ATT_BLOCK), :]
            kc = k_ref[0, pl.ds(ks, nk), :]
            vc = v_ref[0, pl.ds(ks, nk), :]
            da = da_ref[0, pl.ds(qs, ATT_BLOCK), :]
            lt = lt_ref[0, pl.ds(qs, ATT_BLOCK), :]
            rd = rd_ref[0, pl.ds(qs, ATT_BLOCK), :]
            q4 = _stack_heads(qb, masks)
            da4 = _stack_heads(da, masks).astype(MXU_DTYPE)
            lt4 = jnp.max(_stack_heads(lt, masks, -jnp.inf), axis=-1, keepdims=True)
            rd4 = jnp.max(_stack_heads(rd, masks, -jnp.inf), axis=-1, keepdims=True)
            valid = jnp.tile(_band_mask(first), (HEADS_PER_GROUP, 1))
            s = lax.dot_general(q4, kc, _NT, preferred_element_type=F32) * (HEAD_DIM ** -0.5)
            s = jnp.where(valid, s, NEG_INF)
            p = jnp.exp(s - lt4)
            dp = lax.dot_general(da4, vc, _NT, preferred_element_type=F32)
            ds = (p * (dp - rd4) * (HEAD_DIM ** -0.5)).astype(MXU_DTYPE)
            dq_ref[0, pl.ds(qs, ATT_BLOCK), :] = _unstack_heads(jnp.dot(ds, kc, preferred_element_type=F32), masks)
            dk_ref[0, pl.ds(ks, nk), :] += lax.dot_general(ds, q4, _TN, preferred_element_type=F32)
            dv_ref[0, pl.ds(ks, nk), :] += lax.dot_general(p.astype(MXU_DTYPE), da4, _TN, preferred_element_type=F32)

        block(0, 0, True)
        if nb > 1:
            def loop(n, carry):
                block(pl.multiple_of(n * ATT_BLOCK, ATT_BLOCK), pl.multiple_of((n - 1) * ATT_BLOCK, ATT_BLOCK), False)
                return carry

            lax.fori_loop(1, nb, loop, 0)

    per_sample = lambda a: a.reshape(n_samples, length, d * GROUP_W)
    spec = pl.BlockSpec((1, length, GROUP_W), lambda b, r: (b, 0, r))
    shp = jax.ShapeDtypeStruct((n_samples, length, d * GROUP_W), F32)
    dq, dk, dv, *carried = _grid_call(
        body, f"attn_bwd_g{group}", (n_samples, d), [per_sample(a) for a in (q, k, v, dattn, lse_tot, rowdot)],
        [spec] * 6, [spec] * 3, [shp, shp, shp], VMEM_MID, comm)
    flat = lambda a: a.reshape(n_samples * length, d * GROUP_W)
    return flat(dq), flat(dk), flat(dv), carried


def _disc(lr, li, ldt, br, bi):
    dt = jnp.exp(ldt)
    mag = jnp.exp(lr * dt)
    ab_re, ab_im = mag * jnp.cos(li * dt), mag * jnp.sin(li * dt)
    den = lr * lr + li * li
    nr, ni = ab_re - 1.0, ab_im
    f_re = (nr * lr + ni * li) / den
    f_im = (ni * lr - nr * li) / den
    return ab_re, ab_im, f_re * br - f_im * bi, f_re * bi + f_im * br


def _state_mask():
    row_g = lax.broadcasted_iota(jnp.int32, (SCAN_CH, SCAN_WC), 0) // SSM_CH
    col_g = lax.broadcasted_iota(jnp.int32, (SCAN_CH, SCAN_WC), 1) // SSM_STATE
    return row_g == col_g


def _ssm_disc(lr, li, ldt, br, bi, cr, ci):
    w = SCAN_WC

    def body(lr_ref, li_ref, ldt_ref, br_ref, bi_ref, cr_ref, ci_ref, a_ref, bb_ref, c_ref):
        ar, ai, bbr, bbi = _disc(lr_ref[...], li_ref[...], ldt_ref[...], br_ref[...], bi_ref[...])
        crv, civ = cr_ref[...], ci_ref[...]
        mask = _state_mask()
        for cb in range(SCAN_NBLK):
            sl = slice(cb * w, (cb + 1) * w)
            rows = slice(cb * SCAN_CH, (cb + 1) * SCAN_CH)
            dense = lambda comp: jnp.where(mask, jnp.tile(comp[:, sl], (SCAN_CH // SSM_CH, 1)), 0.0)
            a_ref[:, 2 * cb * w:(2 * cb + 1) * w] = ar[:, sl]
            a_ref[:, (2 * cb + 1) * w:(2 * cb + 2) * w] = ai[:, sl]
            bb_ref[rows, :w] = dense(bbr).astype(MXU_DTYPE)
            bb_ref[rows, w:] = dense(bbi).astype(MXU_DTYPE)
            c_ref[rows, :w] = dense(crv).astype(MXU_DTYPE)
            c_ref[rows, w:] = (-dense(civ)).astype(MXU_DTYPE)

    return _pallas_call(
        body, name="ssm_disc",
        out_shape=[jax.ShapeDtypeStruct((1, 2 * N_STATE), F32), jax.ShapeDtypeStruct((SSM_W, 2 * w), MXU_DTYPE),
                   jax.ShapeDtypeStruct((SSM_W, 2 * w), MXU_DTYPE)],
        compiler_params=pltpu.CompilerParams(vmem_limit_bytes=VMEM_MID),
    )(lr, li, ldt, br, bi, cr, ci)


def _group_indicator():
    s = jnp.arange(N_STATE) // SSM_STATE
    return (s[:, None] == jnp.arange(LANES)[None, :]).astype(F32)


def _ssm_param_bwd(lr, li, ldt, br, bi, da_cat, dbb_full, dc_full):
    w = SCAN_WC

    def body(lr_ref, li_ref, ldt_ref, br_ref, bi_ref, da_ref, dbb_ref, dc_ref, ind_ref,
             glr_ref, gli_ref, gldt_ref, gbr_ref, gbi_ref, gcr_ref, gci_ref):
        mask = _state_mask()

        def diag_parts(ref):
            res = ([], [])
            for cb in range(SCAN_NBLK):
                for part in range(2):
                    blk = ref[cb * SCAN_CH:(cb + 1) * SCAN_CH, part * w:(part + 1) * w]
                    res[part].append(jnp.sum(jnp.where(mask, blk, 0.0).reshape(SCAN_CH // SSM_CH, SSM_CH, w), axis=0))
            return jnp.concatenate(res[0], axis=1), jnp.concatenate(res[1], axis=1)

        dar = jnp.concatenate([da_ref[:, 2 * cb * w:(2 * cb + 1) * w] for cb in range(SCAN_NBLK)], axis=1)
        dai = jnp.concatenate([da_ref[:, (2 * cb + 1) * w:(2 * cb + 2) * w] for cb in range(SCAN_NBLK)], axis=1)
        dbbr, dbbi = diag_parts(dbb_ref)
        dcr, dci_neg = diag_parts(dc_ref)
        gcr_ref[...] = dcr
        gci_ref[...] = -dci_neg
        _, vjp = jax.vjp(_disc, lr_ref[...], li_ref[...], ldt_ref[...], br_ref[...], bi_ref[...])
        glr, gli, gldt, gbr, gbi = vjp((dar, dai, dbbr, dbbi))
        glr_ref[...] = glr
        gli_ref[...] = gli
        gldt_ref[...] = jnp.dot(jnp.broadcast_to(gldt, (8, N_STATE)), ind_ref[...], preferred_element_type=F32,
                                precision=lax.Precision.HIGHEST)
        gbr_ref[...] = gbr
        gbi_ref[...] = gbi

    v1 = jax.ShapeDtypeStruct((1, N_STATE), F32)
    v16 = jax.ShapeDtypeStruct((SSM_CH, N_STATE), F32)
    vdt = jax.ShapeDtypeStruct((8, LANES), F32)
    return _pallas_call(
        body, name="ssm_param_bwd", out_shape=[v1, v1, vdt, v16, v16, v16, v16],
        compiler_params=pltpu.CompilerParams(vmem_limit_bytes=VMEM_BIG),
    )(lr, li, ldt, br, bi, da_cat, dbb_full, dc_full, _group_indicator())


def _cmul(ar, ai, br, bi):
    return ar * br - ai * bi, ar * bi + ai * br


def _gelu_tanh(y):
    return jnp.tanh(_GELU_C * (y + 0.044715 * (y * y * y)))


def _segment_carry(er, ei, ar, ai, n_rows, reverse):
    qr, qi = ar, ai
    for _ in range(int(math.log2(SCAN_LEN))):
        qr, qi = _cmul(qr, qi, qr, qi)
    seg = lax.broadcasted_iota(jnp.int32, er.shape, 0) % SCAN_SEG_PER_SAMPLE
    shift = 1
    while shift < SCAN_SEG_PER_SAMPLE:
        keep = (seg < SCAN_SEG_PER_SAMPLE - shift) if reverse else (seg >= shift)
        amount = n_rows - shift if reverse else shift
        sr = jnp.where(keep, pltpu.roll(er, amount, 0), 0.0)
        si = jnp.where(keep, pltpu.roll(ei, amount, 0), 0.0)
        if reverse:
            er, ei = er + qr * sr + qi * si, ei + qr * si - qi * sr
        else:
            er, ei = er + qr * sr - qi * si, ei + qr * si + qi * sr
        qr, qi = _cmul(qr, qi, qr, qi)
        shift *= 2
    keep = (seg < SCAN_SEG_PER_SAMPLE - 1) if reverse else (seg >= 1)
    amount = n_rows - 1 if reverse else 1
    return jnp.where(keep, pltpu.roll(er, amount, 0), 0.0), jnp.where(keep, pltpu.roll(ei, amount, 0), 0.0)


def _ssm_fwd(u_perm, a_cat, bbc, cc, dskip, n_rows):
    t = u_perm.shape[0]
    w = SCAN_WC
    rows_c = SCAN_CHUNK * n_rows
    n_chunks = t // rows_c

    assert n_chunks % 2 == 0

    def body(u_ref, a_ref, bb_ref, c_ref, d_ref, yt_ref, yg_ref, ein_ref, bu_all, st_a, st_b, xs_a, xs_b):
        ar = jnp.broadcast_to(a_ref[:, :w], (n_rows, w))
        ai = jnp.broadcast_to(a_ref[:, w:], (n_rows, w))
        start = lambda ch: pl.multiple_of(ch * rows_c, rows_c)

        def project(ch, stage):
            res = jnp.dot(u_ref[pl.ds(start(ch), rows_c), :].astype(MXU_DTYPE), bb_ref[...], preferred_element_type=F32)
            stage[...] = res
            bu_all[pl.ds(start(ch), rows_c), :] = res

        def steps(src, r0, carry, xs=None):
            for i in range(SCAN_CHUNK):
                blk = src[pl.ds(r0 + i * n_rows, n_rows), :]
                carry = (ar * carry[0] - ai * carry[1] + blk[:, :w], ar * carry[1] + ai * carry[0] + blk[:, w:])
                if xs is not None:
                    xs[i * n_rows:(i + 1) * n_rows, :w] = carry[0]
                    xs[i * n_rows:(i + 1) * n_rows, w:] = carry[1]
            return carry

        def emit(xs, ch):
            y = lax.dot_general(xs[...].astype(MXU_DTYPE), c_ref[...], _NT, preferred_element_type=F32)
            yt = y + d_ref[...] * u_ref[pl.ds(start(ch), rows_c), :]
            yt_ref[pl.ds(start(ch), rows_c), :] = yt
            yg_ref[pl.ds(start(ch), rows_c), :] = (0.5 * yt * (1.0 + _gelu_tanh(yt))).astype(BF16)

        project(0, st_a)

        def pair1(p, carry):
            project(2 * p + 1, st_b)
            carry = steps(st_a, 0, carry)
            project(jnp.minimum(2 * p + 2, n_chunks - 1), st_a)
            return steps(st_b, 0, carry)

        zero = jnp.zeros((n_rows, w), F32)
        er, ei = lax.fori_loop(0, n_chunks // 2, pair1, (zero, zero))
        cr, ci = _segment_carry(er, ei, ar, ai, n_rows, False)
        ein_ref[:, :w] = cr
        ein_ref[:, w:] = ci

        xs_b[...] = jnp.zeros_like(xs_b)

        def pair2(p, carry):
            emit(xs_b, jnp.maximum(2 * p - 1, 0))
            carry = steps(bu_all, start(2 * p), carry, xs_a)
            emit(xs_a, 2 * p)
            return steps(bu_all, start(2 * p + 1), carry, xs_b)

        lax.fori_loop(0, n_chunks // 2, pair2, (cr, ci))
        emit(xs_b, n_chunks - 1)

    col = lambda width: pl.BlockSpec((t, width), lambda c: (0, c))
    wgt = pl.BlockSpec((SCAN_CH, 2 * w), lambda c: (c, 0))
    return _pallas_call(
        body, name="ssm_fwd", grid=(SCAN_NBLK,),
        in_specs=[col(SCAN_CH), pl.BlockSpec((1, 2 * w), lambda c: (0, c)), wgt, wgt,
                  pl.BlockSpec((1, SCAN_CH), lambda c: (0, c))],
        out_specs=[col(SCAN_CH), col(SCAN_CH), pl.BlockSpec((n_rows, 2 * w), lambda c: (0, c))],
        out_shape=[jax.ShapeDtypeStruct((t, SSM_W), F32), jax.ShapeDtypeStruct((t, SSM_W), BF16),
                   jax.ShapeDtypeStruct((n_rows, 2 * N_STATE), F32)],
        scratch_shapes=[pltpu.VMEM((t, 2 * w), F32)] + [pltpu.VMEM((rows_c, 2 * w), F32)] * 4,
        compiler_params=pltpu.CompilerParams(dimension_semantics=("parallel",), vmem_limit_bytes=VMEM_BIG),
    )(u_perm, a_cat, bbc, cc, dskip)


def _ssm_bwd(u_perm, dypre, du_skip, a_cat, bbc, cc, ein, n_rows, comm=None):
    t = u_perm.shape[0]
    w = SCAN_WC
    rows_c = SCAN_CHUNK * n_rows
    n_chunks = t // rows_c

    assert n_chunks % 2 == 0
    last = n_chunks - 1

    def body(u_ref, dy_ref, dus_ref, a_ref, bb_ref, c_ref, ein_ref, du_ref, da_ref, dbb_ref, dc_ref,
             xs_all, st_a, st_b, buf_a, buf_b):
        ar = jnp.broadcast_to(a_ref[:, :w], (n_rows, w))
        ai = jnp.broadcast_to(a_ref[:, w:], (n_rows, w))
        zero = jnp.zeros((n_rows, w), F32)
        start = lambda ch: pl.multiple_of(ch * rows_c, rows_c)
        dbb_ref[...] = jnp.zeros_like(dbb_ref)
        dc_ref[...] = jnp.zeros_like(dc_ref)
        da_ref[...] = jnp.zeros_like(da_ref)

        xs_all[0:n_rows, :] = ein_ref[...]

        def project(ch, stage):
            stage[...] = jnp.dot(u_ref[pl.ds(start(ch), rows_c), :].astype(MXU_DTYPE), bb_ref[...],
                                 preferred_element_type=F32)

        def fwd_steps(stage, ch, carry, xs):
            for i in range(SCAN_CHUNK):
                blk = stage[i * n_rows:(i + 1) * n_rows, :]
                carry = (ar * carry[0] - ai * carry[1] + blk[:, :w], ar * carry[1] + ai * carry[0] + blk[:, w:])
                for half, val in enumerate(carry):
                    xs[i * n_rows:(i + 1) * n_rows, half * w:(half + 1) * w] = val
                    xs_all[pl.ds(start(ch) + (i + 1) * n_rows, n_rows), half * w:(half + 1) * w] = val
            return carry

        def add_dc(xs, ch):
            dc_ref[...] += lax.dot_general(dy_ref[pl.ds(start(ch), rows_c), :], xs[...].astype(MXU_DTYPE), _TN,
                                           preferred_element_type=F32)

        project(0, st_a)

        def fwd_pair(p, carry):
            project(2 * p + 1, st_b)
            carry = fwd_steps(st_a, 2 * p, carry, buf_a)
            add_dc(buf_a, 2 * p)
            project(jnp.minimum(2 * p + 2, last), st_a)
            carry = fwd_steps(st_b, 2 * p + 1, carry, buf_b)
            add_dc(buf_b, 2 * p + 1)
            return carry

        lax.fori_loop(0, n_chunks // 2, fwd_pair, (ein_ref[:, :w], ein_ref[:, w:]))

        def project_dx(ch, stage):
            stage[...] = jnp.dot(dy_ref[pl.ds(start(ch), rows_c), :], c_ref[...], preferred_element_type=F32)

        def back_steps(stage, carry, g_buf=None):
            for i in reversed(range(SCAN_CHUNK)):
                blk = stage[i * n_rows:(i + 1) * n_rows, :]
                carry = (blk[:, :w] + ar * carry[0] + ai * carry[1], blk[:, w:] + ar * carry[1] - ai * carry[0])
                if g_buf is not None:
                    g_buf[i * n_rows:(i + 1) * n_rows, :w] = carry[0]
                    g_buf[i * n_rows:(i + 1) * n_rows, w:] = carry[1]
            return carry

        def first_pair(p, carry):
            project_dx(last - 2 * p - 1, st_b)
            carry = back_steps(st_a, carry)
            project_dx(jnp.maximum(last - 2 * p - 2, 0), st_a)
            return back_steps(st_b, carry)

        project_dx(last, st_a)
        sr, si = lax.fori_loop(0, n_chunks // 2, first_pair, (zero, zero))
        gr0, gi0 = _segment_carry(sr, si, ar, ai, n_rows, True)

        def post(g_buf, ch):
            g = g_buf[...]
            xp = xs_all[pl.ds(start(ch), rows_c), :]
            da_ref[:, :w] += jnp.sum(g[:, :w] * xp[:, :w] + g[:, w:] * xp[:, w:], axis=0, keepdims=True)
            da_ref[:, w:] += jnp.sum(g[:, w:] * xp[:, :w] - g[:, :w] * xp[:, w:], axis=0, keepdims=True)
            gb = g.astype(MXU_DTYPE)
            du_ref[pl.ds(start(ch), rows_c), :] = (lax.dot_general(gb, bb_ref[...], _NT, preferred_element_type=F32)
                                                   + dus_ref[pl.ds(start(ch), rows_c), :])
            dbb_ref[...] += lax.dot_general(u_ref[pl.ds(start(ch), rows_c), :].astype(MXU_DTYPE), gb, _TN,
                                            preferred_element_type=F32)

        def second_pair(p, carry):
            c1 = last - 2 * p
            project_dx(c1 - 1, st_b)
            post(buf_b, jnp.minimum(c1 + 1, last))
            carry = back_steps(st_a, carry, buf_a)
            project_dx(jnp.maximum(c1 - 2, 0), st_a)
            post(buf_a, c1)
            return back_steps(st_b, carry, buf_b)

        project_dx(last, st_a)
        buf_b[...] = jnp.zeros_like(buf_b)
        lax.fori_loop(0, n_chunks // 2, second_pair, (gr0, gi0))
        post(buf_b, 0)

    col = lambda width: pl.BlockSpec((t, width), lambda c, j: (0, c))
    wgt = pl.BlockSpec((SCAN_CH, 2 * w), lambda c, j: (c, 0))
    row = pl.BlockSpec((1, 2 * w), lambda c, j: (0, c))
    return _grid_call(
        body, "ssm_bwd", (SCAN_NBLK, 1), [u_perm, dypre, du_skip, a_cat, bbc, cc, ein],
        [col(SCAN_CH), col(SCAN_CH), col(SCAN_CH), row, wgt, wgt, pl.BlockSpec((n_rows, 2 * w), lambda c, j: (0, c))],
        [col(SCAN_CH), row, wgt, wgt],
        [jax.ShapeDtypeStruct((t, SSM_W), F32), jax.ShapeDtypeStruct((1, 2 * N_STATE), F32),
         jax.ShapeDtypeStruct((SSM_W, 2 * w), F32), jax.ShapeDtypeStruct((SSM_W, 2 * w), F32)],
        56 * 1024 * 1024, comm,
        scratch=[pltpu.VMEM((t + n_rows, 2 * w), F32)] + [pltpu.VMEM((rows_c, 2 * w), F32)] * 4)


def _to_scan_rows(a, n_samples):
    c = a.shape[1]
    return a.reshape(n_samples, SCAN_SEG_PER_SAMPLE, SCAN_LEN, c).transpose(2, 0, 1, 3).reshape(-1, c)


def _from_scan_rows(a, n_samples):
    c = a.shape[1]
    return a.reshape(SCAN_LEN, n_samples, SCAN_SEG_PER_SAMPLE, c).transpose(1, 2, 0, 3).reshape(-1, c)


def _row_spec(tm, width):
    return pl.BlockSpec((tm, width), lambda i, j: (i, 0))


def _whole(arr):
    return pl.BlockSpec(arr.shape, lambda i, j: (0,) * arr.ndim)


def _proj_rope(x, g, w_in_t, tabs, comm=None):
    t = x.shape[0]
    tm = 256

    def body(x_ref, g_ref, w_ref, tc_ref, tlo_ref, thi_ref, h_ref, u_ref, gate_ref, *rest):
        qkv_refs, stage = rest[:9], rest[9]
        xv = x_ref[...]
        r = lax.rsqrt(jnp.mean(xv * xv, axis=-1, keepdims=True) + RMS_EPS)
        h = ((xv * r) * g_ref[...]).astype(BF16)
        h_ref[...] = h
        p = lax.dot_general(h.astype(MXU_DTYPE), w_ref[...], _NT, preferred_element_type=F32)
        u_ref[...] = p[:, QKV_W:QKV_W + SSM_W]
        gate_ref[...] = _sigmoid(p[:, QKV_W + SSM_W:])
        tc, tlo, thi = tc_ref[...], tlo_ref[...], thi_ref[...]
        n_ch = QKV_W // LANES
        for ch in range(n_ch):
            piece = p[:, _lane_chunk(ch)]
            stage[ch] = _rope_apply(piece, tc, tlo, thi) if ch < 2 * n_ch // 3 else piece
        halves = GROUP_W // LANES
        for grp, d in enumerate(DILATIONS):
            for which in range(3):
                out = qkv_refs[3 * grp + which]
                for res in range(d):
                    for half in range(halves):
                        ch = which * (n_ch // 3) + grp * halves + half
                        out[:, _lane_chunk(res * halves + half)] = _gather_residue(stage, ch, res, d, tm // d).astype(BF16)

    tab = pl.BlockSpec((tm, LANES), lambda i, j: (i % (SEQ // tm), 0))
    widths = [(D_MODEL, BF16), (SSM_W, F32), (2 * D_MODEL, F32)]
    out_specs = [_row_spec(tm, wd) for wd, _ in widths]
    out_shapes = [jax.ShapeDtypeStruct((t, wd), dt) for wd, dt in widths]
    for d in DILATIONS:
        out_specs += [_row_spec(tm // d, d * GROUP_W)] * 3
        out_shapes += [jax.ShapeDtypeStruct((t // d, d * GROUP_W), BF16)] * 3
    return _grid_call(
        body, "proj_rope", (t // tm, 1), [x, g, w_in_t, *tabs],
        [_row_spec(tm, D_MODEL), _whole(g), _whole(w_in_t), tab, tab, tab], out_specs, out_shapes, VMEM_BIG, comm,
        scratch=[pltpu.VMEM((QKV_W // LANES, tm, LANES), F32)])


def _branch_outputs(attn_ref, yg_ref, wao_ref, wglu_ref):
    attn_d = lax.dot_general(attn_ref[...].astype(MXU_DTYPE), wao_ref[...], _NT, preferred_element_type=F32)
    z = lax.dot_general(yg_ref[...].astype(MXU_DTYPE), wglu_ref[...], _NT, preferred_element_type=F32)
    return attn_d, z[:, :D_MODEL], _sigmoid(z[:, D_MODEL:])


def _mix_out_rms(attn, yg, gates, x, w_ao_t, w_glu_t, w_out, g):
    t = x.shape[0]
    tm = 256

    def body(attn_ref, yg_ref, gate_ref, x_ref, wao_ref, wglu_ref, wout_ref, g_ref, m_ref, x1_ref, h_ref):
        attn_d, za, sb = _branch_outputs(attn_ref, yg_ref, wao_ref, wglu_ref)
        merged = (gate_ref[:, :D_MODEL] * attn_d + gate_ref[:, D_MODEL:] * (za * sb)).astype(BF16)
        m_ref[...] = merged
        x1 = x_ref[...] + jnp.dot(merged.astype(MXU_DTYPE), wout_ref[...], preferred_element_type=F32)
        x1_ref[...] = x1
        r = lax.rsqrt(jnp.mean(x1 * x1, axis=-1, keepdims=True) + RMS_EPS)
        h_ref[...] = ((x1 * r) * g_ref[...]).astype(BF16)

    return _grid_call(
        body, "mix_out_rms", (t // tm, 1), [attn, yg, gates, x, w_ao_t, w_glu_t, w_out, g],
        [_row_spec(tm, GROUP_W), _row_spec(tm, SSM_W), _row_spec(tm, 2 * D_MODEL), _row_spec(tm, D_MODEL),
         _whole(w_ao_t), _whole(w_glu_t), _whole(w_out), _whole(g)],
        [_row_spec(tm, D_MODEL)] * 3,
        [jax.ShapeDtypeStruct((t, D_MODEL), BF16), jax.ShapeDtypeStruct((t, D_MODEL), F32),
         jax.ShapeDtypeStruct((t, D_MODEL), BF16)], VMEM_BIG)


def _mix_bwd(dx1b, attn, yg, gates, w_ao_t, w_glu_t, w_out, comm=None):
    t = dx1b.shape[0]
    tm = 256

    def body(dx_ref, attn_ref, yg_ref, gate_ref, wao_ref, wglu_ref, wout_ref, dad_ref, dz_ref, dg_ref, da_ref, dyg_ref):
        dm = lax.dot_general(dx_ref[...], wout_ref[...], _NT, preferred_element_type=F32)
        attn_d, za, sb = _branch_outputs(attn_ref, yg_ref, wao_ref, wglu_ref)
        g0, g1 = gate_ref[:, :D_MODEL], gate_ref[:, D_MODEL:]
        dad = (dm * g0).astype(BF16)
        dad_ref[...] = dad
        ds = dm * g1
        dza, dzb = (ds * sb).astype(BF16), (ds * za * sb * (1.0 - sb)).astype(BF16)
        dz_ref[:, :D_MODEL] = dza
        dz_ref[:, D_MODEL:] = dzb
        dg_ref[:, :D_MODEL] = (dm * attn_d * g0 * (1.0 - g0)).astype(BF16)
        dg_ref[:, D_MODEL:] = (dm * (za * sb) * g1 * (1.0 - g1)).astype(BF16)
        da_ref[...] = jnp.dot(dad.astype(MXU_DTYPE), wao_ref[...], preferred_element_type=F32)
        dyg_ref[...] = (jnp.dot(dza.astype(MXU_DTYPE), wglu_ref[:D_MODEL, :], preferred_element_type=F32)
                        + jnp.dot(dzb.astype(MXU_DTYPE), wglu_ref[D_MODEL:, :], preferred_element_type=F32))

    widths = [(D_MODEL, BF16), (2 * D_MODEL, BF16), (2 * D_MODEL, BF16), (GROUP_W, F32), (SSM_W, F32)]
    return _grid_call(
        body, "mix_bwd", (t // tm, 1), [dx1b, attn, yg, gates, w_ao_t, w_glu_t, w_out],
        [_row_spec(tm, D_MODEL), _row_spec(tm, GROUP_W), _row_spec(tm, SSM_W), _row_spec(tm, 2 * D_MODEL),
         _whole(w_ao_t), _whole(w_glu_t), _whole(w_out)],
        [_row_spec(tm, wd) for wd, _ in widths], [jax.ShapeDtypeStruct((t, wd), dt) for wd, dt in widths], VMEM_BIG, comm)


FFN_TN = D_FF // 2
MXU_COLS = 256


def _ffn_in_swiglu(h2, w_gate_t, w_up_t, comm=None):
    t = h2.shape[0]
    tm = 512

    def body(h_ref, wg_ref, wu_ref, a_ref, b_ref, f_ref):
        h = h_ref[...].astype(MXU_DTYPE)
        for c0 in range(0, FFN_TN, MXU_COLS):
            sl = slice(c0, min(c0 + MXU_COLS, FFN_TN))
            a = lax.dot_general(h, wg_ref[sl, :], _NT, preferred_element_type=F32)
            b = lax.dot_general(h, wu_ref[sl, :], _NT, preferred_element_type=F32)
            a_ref[:, sl] = a
            b_ref[:, sl] = b
            f_ref[:, sl] = (a * _sigmoid(a) * b).astype(BF16)

    tile = pl.BlockSpec((tm, FFN_TN), lambda j, i: (i, j))
    wspec = pl.BlockSpec((FFN_TN, D_MODEL), lambda j, i: (j, 0))
    return _grid_call(
        body, "ffn_in_swiglu", (D_FF // FFN_TN, t // tm), [h2, w_gate_t, w_up_t],
        [pl.BlockSpec((tm, D_MODEL), lambda j, i: (i, 0)), wspec, wspec],
        [tile] * 3, [jax.ShapeDtypeStruct((t, D_FF), F32)] * 2 + [jax.ShapeDtypeStruct((t, D_FF), BF16)], VMEM_BIG, comm)


def _ffn_down_final(f, w_down, x1, target, g):
    t = x1.shape[0]
    tm = 256

    def body(f_ref, w_ref, x1_ref, t_ref, g_ref, dx_ref, dxb_ref, loss_ref, gg_ref):
        @pl.when(pl.program_id(0) == 0)
        def _():
            loss_ref[...] = jnp.zeros_like(loss_ref)
            gg_ref[...] = jnp.zeros_like(gg_ref)

        xv = x1_ref[...] + jnp.dot(f_ref[...].astype(MXU_DTYPE), w_ref[...], preferred_element_type=F32)
        gv = g_ref[...]
        r = lax.rsqrt(jnp.mean(xv * xv, axis=-1, keepdims=True) + RMS_EPS)
        n = xv * r
        diff = n * gv - t_ref[...]
        per_tok = jnp.mean(diff * diff, axis=-1, keepdims=True)
        loss_ref[...] += 0.5 * jnp.sum(per_tok, axis=0, keepdims=True)
        dy = diff / xv.shape[-1]
        gg_ref[...] += jnp.sum(dy * n, axis=0, keepdims=True)
        dn = dy * gv
        dx = r * (dn - n * jnp.mean(dn * n, axis=-1, keepdims=True))
        dx_ref[...] = dx
        dxb_ref[...] = dx.astype(BF16)

    acc = lambda shp: pl.BlockSpec(shp, lambda i, j: (0, 0))
    return _grid_call(
        body, "ffn_down_final", (t // tm, 1), [f, w_down, x1, target, g],
        [_row_spec(tm, D_FF), _whole(w_down), _row_spec(tm, D_MODEL), _row_spec(tm, D_MODEL), _whole(g)],
        [_row_spec(tm, D_MODEL)] * 2 + [acc((8, LANES)), acc((1, D_MODEL))],
        [jax.ShapeDtypeStruct((t, D_MODEL), F32), jax.ShapeDtypeStruct((t, D_MODEL), BF16),
         jax.ShapeDtypeStruct((8, LANES), F32), jax.ShapeDtypeStruct((1, D_MODEL), F32)], VMEM_BIG, sequential=True)


def _d_f_swiglu_bwd(dx2b, w_down, a, b):
    t = a.shape[0]
    tm = 512

    def body(dx_ref, w_ref, a_ref, b_ref, da_ref, db_ref):
        d = lax.dot_general(dx_ref[...], w_ref[...], _NT, preferred_element_type=F32)
        av, bv = a_ref[...], b_ref[...]
        sg = _sigmoid(av)
        da_ref[...] = (d * bv * sg * (1.0 + av * (1.0 - sg))).astype(BF16)
        db_ref[...] = (d * av * sg).astype(BF16)

    tile = pl.BlockSpec((tm, FFN_TN), lambda j, i: (i, j))
    return _grid_call(
        body, "d_f_swiglu_bwd", (D_FF // FFN_TN, t // tm), [dx2b, w_down, a, b],
        [pl.BlockSpec((tm, D_MODEL), lambda j, i: (i, 0)), pl.BlockSpec((FFN_TN, D_MODEL), lambda j, i: (j, 0)), tile, tile],
        [tile] * 2, [jax.ShapeDtypeStruct((t, D_FF), BF16)] * 2, VMEM_BIG)


def _mm_rms_bwd(operands, weights, x, g, dres, name, comm=None):
    t = x.shape[0]
    tm = 256
    n_op = len(operands)

    def body(*refs):
        a_refs, w_refs = refs[:n_op], refs[n_op:2 * n_op]
        x_ref, g_ref, dres_ref, dx_ref, dxb_ref, gg_ref = refs[2 * n_op:]

        @pl.when(pl.program_id(0) == 0)
        def _():
            gg_ref[...] = jnp.zeros_like(gg_ref)

        dh = None
        for a_ref, w_ref in zip(a_refs, w_refs):
            part = jnp.dot(a_ref[...].astype(MXU_DTYPE), w_ref[...], preferred_element_type=F32)
            dh = part if dh is None else dh + part
        xv = x_ref[...]
        r = lax.rsqrt(jnp.mean(xv * xv, axis=-1, keepdims=True) + RMS_EPS)
        n = xv * r
        gg_ref[...] += jnp.sum(dh * n, axis=0, keepdims=True)
        dn = dh * g_ref[...]
        dx = dres_ref[...] + r * (dn - n * jnp.mean(dn * n, axis=-1, keepdims=True))
        dx_ref[...] = dx
        dxb_ref[...] = dx.astype(BF16)

    d = x.shape[1]
    return _grid_call(
        body, name, (t // tm, 1), [*operands, *weights, x, g, dres],
        [_row_spec(tm, a.shape[1]) for a in operands] + [_whole(wk) for wk in weights]
        + [_row_spec(tm, d), _whole(g), _row_spec(tm, d)],
        [_row_spec(tm, d)] * 2 + [pl.BlockSpec((1, d), lambda i, j: (0, 0))],
        [jax.ShapeDtypeStruct((t, d), F32), jax.ShapeDtypeStruct((t, d), BF16), jax.ShapeDtypeStruct((1, d), F32)],
        VMEM_BIG, comm, sequential=True)


def _flat_small(small):
    perm_b = lambda a: a.reshape(SSM_GROUPS, SSM_STATE, SSM_CH).transpose(2, 0, 1).reshape(SSM_CH, N_STATE)
    perm_c = lambda a: a.reshape(SSM_GROUPS, SSM_CH, SSM_STATE).transpose(1, 0, 2).reshape(SSM_CH, N_STATE)
    return dict(
        g_mix=small["norm_mix_g"].reshape(1, D_MODEL), g_ffn=small["norm_ffn_g"].reshape(1, D_MODEL),
        g_fin=small["norm_final_g"].reshape(1, D_MODEL),
        lr=small["ssm_a_re"].reshape(1, N_STATE), li=small["ssm_a_im"].reshape(1, N_STATE),
        ldt=jnp.repeat(small["ssm_log_dt"].reshape(SSM_GROUPS), SSM_STATE).reshape(1, N_STATE),
        br=perm_b(small["ssm_b_re"]), bi=perm_b(small["ssm_b_im"]),
        cr=perm_c(small["ssm_c_re"]), ci=perm_c(small["ssm_c_im"]), dskip=small["ssm_d"].reshape(1, SSM_W))


AG_HOSTS = {"proj_rope": ("w_glu", "w_attn_out", "w_out"), "attn_fwd_g0": ("w_ffn_gate",), "attn_fwd_g1": ("w_ffn_up",),
            "ffn_in_swiglu": ("w_ffn_down",)}
HALVED = ("w_ffn_gate", "w_ffn_up", "w_in")
A2A_HOSTS = {"d_h2_rms": ("w_ffn_down",), "attn_bwd_g0": ("w_ffn_gate:0",), "attn_bwd_g1": ("w_ffn_gate:1",),
             "mix_bwd": ("w_out",), "attn_bwd_g2": ("w_ffn_up:0",), "ssm_bwd": ("w_ffn_up:1", "w_attn_out", "w_glu"),
             "mm_g_in1": ("w_in:0",), "d_h0_rms": ("w_in:1",)}
SMALL_HOST = "mm_g_in0"


def _local_step(x, target, w, small, shards=None):
    t = x.shape[0]
    n_samples = t // SEQ
    n_rows = n_samples * SCAN_SEG_PER_SAMPLE
    tabs = _rope_tables()
    w = dict(w)
    fs = _flat_small(small)
    g_mix, g_ffn, g_fin, dskip = fs["g_mix"], fs["g_ffn"], fs["g_fin"], fs["dskip"]
    a_cat, bbc, cc = _ssm_disc(fs["lr"], fs["li"], fs["ldt"], fs["br"], fs["bi"], fs["cr"], fs["ci"])
    big, recv, small_pack = {}, {}, []

    def comm_of(name):
        if shards is None:
            return None
        if name == SMALL_HOST:
            return _ag_comm([(small_pack[0], 0, 0)], [(N_DEV, *small_pack[0].shape)])
        if name in AG_HOSTS:
            names = AG_HOSTS[name]
            return _ag_comm([(shards[n], j, 0) for j, n in enumerate(names)], [(N_DEV, *shards[n].shape) for n in names])
        if name in A2A_HOSTS:
            return _a2a_comm([(big[n].reshape(N_DEV, -1, big[n].shape[1]), 0) for n in A2A_HOSTS[name]])
        return None

    def absorb(name, carried):
        if name == SMALL_HOST:
            recv["small"] = carried[0]
        for n, a3 in zip(AG_HOSTS.get(name, ()), carried):
            w[n] = a3.reshape(-1, a3.shape[2])
        for n, a3 in zip(A2A_HOSTS.get(name, ()), carried):
            recv[n] = a3

    def mm(a, b, mode, name, tm, tn, **kw):
        comm = comm_of(name)
        if comm is None:
            return _mm(a, b, mode, name, tm, tn, **kw)
        out, *carried = _mm(a, b, mode, name, tm, tn, comm=comm, **kw)
        absorb(name, carried)
        return out

    h0, u, gates, *rest = _proj_rope(x, g_mix, w["w_in"], tabs, comm_of("proj_rope"))
    qkv = [rest[3 * g:3 * g + 3] for g in range(3)]
    absorb("proj_rope", rest[9:])
    os_, lses = [], []
    for g in range(3):
        o_g, l_g, carried = _attn_fwd(*qkv[g], g, n_samples, comm_of(f"attn_fwd_g{g}"))
        absorb(f"attn_fwd_g{g}", carried)
        os_.append(o_g)
        lses.append(l_g)
    attn, lse_tot = _attn_merge(os_, lses)

    u_perm = _to_scan_rows(u, n_samples)
    ytot, yg_perm, ein = _ssm_fwd(u_perm, a_cat, bbc, cc, dskip, n_rows)
    yg = _from_scan_rows(yg_perm, n_samples)

    merged, x1, h2 = _mix_out_rms(attn, yg, gates, x, w["w_attn_out"], w["w_glu"], w["w_out"], g_ffn)
    ffn_a, ffn_b, f, *carried = _ffn_in_swiglu(h2, w["w_ffn_gate"], w["w_ffn_up"], comm_of("ffn_in_swiglu"))
    absorb("ffn_in_swiglu", carried)
    dx2, dx2b, loss_blk, g_gfin = _ffn_down_final(f, w["w_ffn_down"], x1, target, g_fin)

    da, db = _d_f_swiglu_bwd(dx2b, w["w_ffn_down"], ffn_a, ffn_b)
    big["w_ffn_down"] = mm(f, dx2b, "tn", "mm_g_down", 256, D_MODEL, out_dtype=BF16)
    half = D_MODEL // 2
    for hf in range(2):
        big[f"w_ffn_gate:{hf}"] = mm(da, h2, "tn", f"mm_g_gate{hf}", 256, half, out_dtype=BF16, cols=(hf * half, half))
        big[f"w_ffn_up:{hf}"] = mm(db, h2, "tn", f"mm_g_up{hf}", 256, half, out_dtype=BF16, cols=(hf * half, half))
    dx1, dx1b, g_gffn, *carried = _mm_rms_bwd([da, db], [w["w_ffn_gate"], w["w_ffn_up"]], x1, g_ffn, dx2, "d_h2_rms",
                                              comm_of("d_h2_rms"))
    absorb("d_h2_rms", carried)

    big["w_out"] = mm(merged, dx1b, "tn", "mm_g_out", 256, D_MODEL, out_dtype=BF16)
    dattn_d, dz, dgpre, dattn, dyg, *carried = _mix_bwd(dx1b, attn, yg, gates, w["w_attn_out"], w["w_glu"], w["w_out"],
                                                        comm_of("mix_bwd"))
    absorb("mix_bwd", carried)

    big["w_attn_out"] = mm(dattn_d, attn, "tn", "mm_g_attn_out", 512, GROUP_W, out_dtype=BF16)
    cot = _attn_rowdot(dattn, attn, lse_tot)
    dqs, dks, dvs = [], [], []
    for g in range(3):
        dq_g, dk_g, dv_g, carried = _attn_bwd(*qkv[g], *cot[g], g, n_samples, comm_of(f"attn_bwd_g{g}"))
        absorb(f"attn_bwd_g{g}", carried)
        dqs.append(dq_g)
        dks.append(dk_g)
        dvs.append(dv_g)

    big["w_glu"] = mm(dz, yg, "tn", "mm_g_glu", 512, 512, out_dtype=BF16)
    dyg_perm = _to_scan_rows(dyg, n_samples)
    dypre, du_skip, g_dskip = _ssm_act_bwd(dyg_perm, ytot, u_perm, dskip)
    du_perm, da_cat, dbb_full, dc_full, *carried = _ssm_bwd(u_perm, dypre, du_skip, a_cat, bbc, cc, ein, n_rows,
                                                          comm_of("ssm_bwd"))
    absorb("ssm_bwd", carried)
    du = _from_scan_rows(du_perm, n_samples)
    g_lr, g_li, g_ldt, g_br, g_bi, g_cr, g_ci = _ssm_param_bwd(
        fs["lr"], fs["li"], fs["ldt"], fs["br"], fs["bi"], da_cat, dbb_full, dc_full)

    small_pack.append(_pack_small(dict(lr=g_lr, li=g_li, ldt=g_ldt, br=g_br, bi=g_bi, cr=g_cr, ci=g_ci, dskip=g_dskip,
                                       g_ffn=g_gffn, g_fin=g_gfin, loss=loss_blk)))

    dproj = _pack_dproj(dqs, dks, dvs, du, dgpre, tabs)
    for hf in range(2):
        big[f"w_in:{hf}"] = mm(dproj, h0, "tn", f"mm_g_in{hf}", 256, half, out_dtype=BF16, cols=(hf * half, half))
    grad_x, _, g_gmix, *carried = _mm_rms_bwd([dproj], [w["w_in"]], x, g_mix, dx1, "d_h0_rms", comm_of("d_h0_rms"))
    absorb("d_h0_rms", carried)
    return grad_x, (big if shards is None else recv), small_pack[0], g_gmix


_MESH = pl.DeviceIdType.MESH


def _all_gather(block, name):
    rows, lanes = block.shape

    def body(x_ref, out_ref, send_sems, recv_sems, local_sem):
        x, y, c = lax.axis_index("x"), lax.axis_index("y"), lax.axis_index("c")
        me, sibling = (x, y, c), (x, y, 1 - c)
        chips = [(1 - x, y), (x, 1 - y), (1 - x, 1 - y)]

        def slot(px, py, pc):
            return out_ref.at[4 * px + 2 * py + pc]

        def copy(k, blk, to, src=None):
            return pltpu.make_async_remote_copy(
                src_ref=slot(*blk) if src is None else src, dst_ref=slot(*blk), send_sem=send_sems.at[k],
                recv_sem=recv_sems.at[k], device_id=to, device_id_type=_MESH)

        mine = pltpu.make_async_copy(x_ref, slot(*me), local_sem)
        mine.start()
        first = [copy(0, me, sibling, src=x_ref)]
        first += [copy(1 + j, me, (*chip, c), src=x_ref) for j, chip in enumerate(chips)]
        for cp in first:
            cp.start()
        passed = [copy(4 + j, (*chip, c), sibling) for j, chip in enumerate(chips)]
        for j, chip in enumerate(chips):
            copy(1 + j, (*chip, c), me).wait_recv()
            passed[j].start()
        copy(0, sibling, me).wait_recv()
        for j, chip in enumerate(chips):
            copy(4 + j, (*chip, 1 - c), me).wait_recv()
        for cp in first + passed:
            cp.wait_send()
        mine.wait()

    return _pallas_call(
        body, name=name, out_shape=jax.ShapeDtypeStruct((N_DEV, rows, lanes), block.dtype),
        in_specs=[pl.BlockSpec(memory_space=pl.ANY)], out_specs=pl.BlockSpec(memory_space=pl.ANY),
        scratch_shapes=[pltpu.SemaphoreType.DMA((7,)), pltpu.SemaphoreType.DMA((7,)), pltpu.SemaphoreType.DMA],
    )(block)


def _ag_comm(items, bufs):
    def plan(in_refs, out_refs, send_sems, recv_sems, local_sems):
        x, y, c = lax.axis_index("x"), lax.axis_index("y"), lax.axis_index("c")
        me, sibling = (x, y, c), (x, y, 1 - c)
        chips = [(1 - x, y), (x, 1 - y), (1 - x, 1 - y)]
        plans = []
        for t, (_, buf, slot0) in enumerate(items):
            x_ref, out_ref = in_refs[t], out_refs[buf]

            def slot(px, py, pc, out_ref=out_ref, slot0=slot0):
                return out_ref.at[slot0 + 4 * px + 2 * py + pc]

            def copy(k, blk, to, src=None, t=t, slot=slot):
                return pltpu.make_async_remote_copy(
                    src_ref=slot(*blk) if src is None else src, dst_ref=slot(*blk), send_sem=send_sems.at[7 * t + k],
                    recv_sem=recv_sems.at[7 * t + k], device_id=to, device_id_type=_MESH)

            plans.append(dict(
                mine=pltpu.make_async_copy(x_ref, slot(*me), local_sems.at[t]),
                first=[copy(0, me, sibling, src=x_ref)] + [copy(1 + j, me, (*chip, c), src=x_ref)
                                                           for j, chip in enumerate(chips)],
                passed=[copy(4 + j, (*chip, c), sibling) for j, chip in enumerate(chips)],
                from_ici=[copy(1 + j, (*chip, c), me) for j, chip in enumerate(chips)],
                from_sibling=[copy(0, sibling, me)] + [copy(4 + j, (*chip, 1 - c), me) for j, chip in enumerate(chips)]))
        return plans

    def start(*refs):
        for p in plan(*refs):
            p["mine"].start()
            for cp in p["first"]:
                cp.start()

    def finish(*refs):
        plans = plan(*refs)
        for p in plans:
            for arrived, onward in zip(p["from_ici"], p["passed"]):
                arrived.wait_recv()
                onward.start()
        for p in plans:
            for arrived in p["from_sibling"]:
                arrived.wait_recv()
            for cp in p["first"] + p["passed"]:
                cp.wait_send()
            p["mine"].wait()

    dtype_of = {buf: shard.dtype for shard, buf, _ in items}
    out_shapes = [jax.ShapeDtypeStruct(b, dtype_of[j]) for j, b in enumerate(bufs)]
    return _Comm([it[0] for it in items], out_shapes, 7 * len(items), len(items), start, finish)


def _a2a_comm(items):
    def plan(in_refs, out_refs, send_sems, recv_sems, local_sems):
        x, y, c = lax.axis_index("x"), lax.axis_index("y"), lax.axis_index("c")
        my = 4 * x + 2 * y + c
        copies, locals_ = [], []
        for t, (_, slot0) in enumerate(items):
            s_ref, r_ref = in_refs[t], out_refs[t]
            locals_.append(pltpu.make_async_copy(s_ref.at[slot0 + my], r_ref.at[my], local_sems.at[t]))
            for kk in range(1, N_DEV):
                px = 1 - x if kk & 4 else x
                py = 1 - y if kk & 2 else y
                pc = 1 - c if kk & 1 else c
                copies.append(pltpu.make_async_remote_copy(
                    src_ref=s_ref.at[slot0 + 4 * px + 2 * py + pc], dst_ref=r_ref.at[my],
                    send_sem=send_sems.at[7 * t + kk - 1], recv_sem=recv_sems.at[7 * t + kk - 1],
                    device_id=(px, py, pc), device_id_type=_MESH))
        return copies, locals_

    def start(*refs):
        copies, locals_ = plan(*refs)
        for cp in locals_ + copies:
            cp.start()

    def finish(*refs):
        copies, locals_ = plan(*refs)
        for cp in copies + locals_:
            cp.wait()

    out_shapes = [jax.ShapeDtypeStruct((N_DEV,) + it[0].shape[1:], it[0].dtype) for it in items]
    return _Comm([it[0] for it in items], out_shapes, 7 * len(items), len(items), start, finish)


def _adam_math(g, w, m, v):
    m_new = ADAM_B1 * m + (1.0 - ADAM_B1) * g
    v_new = ADAM_B2 * v + (1.0 - ADAM_B2) * jnp.square(g)
    m_hat = m_new / (1.0 - ADAM_B1 ** ADAM_STEP)
    v_hat = v_new / (1.0 - ADAM_B2 ** ADAM_STEP)
    return -ADAM_LR * (m_hat / (jnp.sqrt(v_hat) + ADAM_EPS) + ADAM_WD * w), m_new, v_new


def _sum_partials(parts, name, tm):
    n, rows, _ = parts[0].shape
    widths = [p.shape[2] for p in parts]

    def body(*refs):
        g_ref, off = refs[-1], 0
        for p_ref, wd in zip(refs[:-1], widths):
            g = p_ref[0].astype(F32)
            for s in range(1, n):
                g = g + p_ref[s].astype(F32)
            g_ref[:, off:off + wd] = g
            off += wd

    return _pallas_call(
        body, name=name, grid=(rows // tm,), in_specs=[pl.BlockSpec((n, tm, wd), lambda i: (0, i, 0)) for wd in widths],
        out_specs=pl.BlockSpec((tm, sum(widths)), lambda i: (i, 0)),
        out_shape=jax.ShapeDtypeStruct((rows, sum(widths)), F32),
        compiler_params=pltpu.CompilerParams(dimension_semantics=("parallel",), vmem_limit_bytes=VMEM_MID),
    )(*parts)


def _adam(partials, w, m, v, name, tm):
    n, rows, cols = partials.shape

    def body(p_ref, w_ref, m_ref, v_ref, g_ref, d_ref, nm_ref, nv_ref):
        g = p_ref[0].astype(F32)
        for s in range(1, n):
            g = g + p_ref[s].astype(F32)
        g_ref[...] = g
        d_ref[...], nm_ref[...], nv_ref[...] = _adam_math(g, w_ref[...], m_ref[...], v_ref[...])

    assert rows % tm == 0
    row = pl.BlockSpec((tm, cols), lambda i: (i, 0))
    shp = jax.ShapeDtypeStruct((rows, cols), F32)
    return _pallas_call(
        body, name=name, grid=(rows // tm,),
        in_specs=[pl.BlockSpec((n, tm, cols), lambda i: (0, i, 0)), row, row, row],
        out_specs=[row] * 4, out_shape=[shp] * 4,
        compiler_params=pltpu.CompilerParams(dimension_semantics=("parallel",), vmem_limit_bytes=VMEM_MID),
    )(partials, w, m, v)


_PK_LR, _PK_LI, _PK_GAINS, _PK_MISC, _PK_BR, _PK_BI, _PK_CR, _PK_CI, _PK_ROWS = 0, 1, 2, 3, 8, 24, 40, 56, 72
_PK_LDT_LANE, _PK_LOSS_LANE = D_MODEL + SSM_W, D_MODEL + SSM_W + LANES


def _pack_small(sg):
    names = ("lr", "li", "g_ffn", "g_fin", "dskip", "ldt", "loss", "br", "bi", "cr", "ci")

    def body(lr, li, gffn, gfin, dskip, ldt, loss, br, bi, cr, ci, o_ref):
        o_ref[...] = jnp.zeros_like(o_ref)
        o_ref[_PK_LR:_PK_LR + 1, :] = lr[...]
        o_ref[_PK_LI:_PK_LI + 1, :] = li[...]
        o_ref[_PK_GAINS:_PK_GAINS + 1, D_MODEL:] = gffn[...]
        o_ref[_PK_MISC:_PK_MISC + 1, :D_MODEL] = gfin[...]
        o_ref[_PK_MISC:_PK_MISC + 1, D_MODEL:D_MODEL + SSM_W] = dskip[...]
        o_ref[_PK_MISC:_PK_MISC + 1, _PK_LDT_LANE:_PK_LDT_LANE + LANES] = ldt[0:1, :]
        o_ref[_PK_MISC:_PK_MISC + 1, _PK_LOSS_LANE:_PK_LOSS_LANE + LANES] = loss[0:1, :]
        o_ref[_PK_BR:_PK_BR + SSM_CH, :] = br[...]
        o_ref[_PK_BI:_PK_BI + SSM_CH, :] = bi[...]
        o_ref[_PK_CR:_PK_CR + SSM_CH, :] = cr[...]
        o_ref[_PK_CI:_PK_CI + SSM_CH, :] = ci[...]

    return _pallas_call(body, name="pack_small", out_shape=jax.ShapeDtypeStruct((_PK_ROWS, N_STATE), F32))(
        *[sg[n] for n in names])


def _unpack_small(s, g_mix):
    unflat_b = lambda a: a.reshape(SSM_CH, SSM_GROUPS, SSM_STATE).transpose(1, 2, 0)[None]
    unflat_c = lambda a: a.reshape(SSM_CH, SSM_GROUPS, SSM_STATE).transpose(1, 0, 2)[None]
    grads = {
        "norm_mix_g": g_mix, "norm_ffn_g": s[_PK_GAINS, D_MODEL:].reshape(1, D_MODEL),
        "norm_final_g": s[_PK_MISC, :D_MODEL],
        "ssm_a_re": s[_PK_LR].reshape(1, SSM_GROUPS, SSM_STATE), "ssm_a_im": s[_PK_LI].reshape(1, SSM_GROUPS, SSM_STATE),
        "ssm_log_dt": s[_PK_MISC, _PK_LDT_LANE:_PK_LDT_LANE + SSM_GROUPS].reshape(1, SSM_GROUPS),
        "ssm_d": s[_PK_MISC, D_MODEL:D_MODEL + SSM_W].reshape(1, SSM_GROUPS, SSM_CH),
        "ssm_b_re": unflat_b(s[_PK_BR:_PK_BR + SSM_CH]), "ssm_b_im": unflat_b(s[_PK_BI:_PK_BI + SSM_CH]),
        "ssm_c_re": unflat_c(s[_PK_CR:_PK_CR + SSM_CH]), "ssm_c_im": unflat_c(s[_PK_CI:_PK_CI + SSM_CH]),
    }
    return s[_PK_MISC, _PK_LOSS_LANE], grads


def _adam_small(grads, wts, moms, vars_):
    n = len(SMALL_WEIGHTS)
    as2d = lambda a: a.reshape(1, -1) if a.ndim == 1 else a

    def body(*refs):
        ins, outs = refs[:4 * n], refs[4 * n:]
        for i in range(n):
            g, w, m, v = (ins[j * n + i][...] for j in range(4))
            outs[i][...], outs[n + i][...], outs[2 * n + i][...] = _adam_math(g, w, m, v)

    operands = [as2d(d[k]) for d in (grads, wts, moms, vars_) for k in SMALL_WEIGHTS]
    shapes = [jax.ShapeDtypeStruct(as2d(wts[k]).shape, F32) for k in SMALL_WEIGHTS] * 3
    res = _pallas_call(body, name="adam_small", out_shape=shapes,
                         compiler_params=pltpu.CompilerParams(vmem_limit_bytes=VMEM_BIG))(*operands)
    out = {}
    for j, kind in enumerate(("delta", "new_m", "new_v")):
        for i, k in enumerate(SMALL_WEIGHTS):
            out[kind, k] = res[j * n + i].reshape(wts[k].shape)
    return out


def kernel(x, norm_mix_g, w_in, ssm_a_re, ssm_a_im, ssm_log_dt, ssm_b_re, ssm_b_im, ssm_c_re, ssm_c_im, ssm_d, w_glu, w_attn_out, w_out, norm_ffn_g, w_ffn_gate, w_ffn_up, w_ffn_down, norm_final_g, loss_target, m_norm_mix_g, m_w_in, m_ssm_a_re, m_ssm_a_im, m_ssm_log_dt, m_ssm_b_re, m_ssm_b_im, m_ssm_c_re, m_ssm_c_im, m_ssm_d, m_w_glu, m_w_attn_out, m_w_out, m_norm_ffn_g, m_w_ffn_gate, m_w_ffn_up, m_w_ffn_down, m_norm_final_g, v_norm_mix_g, v_w_in, v_ssm_a_re, v_ssm_a_im, v_ssm_log_dt, v_ssm_b_re, v_ssm_b_im, v_ssm_c_re, v_ssm_c_im, v_ssm_d, v_w_glu, v_w_attn_out, v_w_out, v_norm_ffn_g, v_w_ffn_gate, v_w_ffn_up, v_w_ffn_down, v_norm_final_g):
    args = dict(locals())
    wts = {n: args[n] for n in ALL_WEIGHTS}
    moms = {n: args["m_" + n] for n in ALL_WEIGHTS}
    vars_ = {n: args["v_" + n] for n in ALL_WEIGHTS}
    n_samples = x.shape[0]
    t = n_samples * SEQ

    shards = {n: (wts[n][0] if n in ROW_SHARDED else wts[n][0].T).astype(BF16) for n in BIG_WEIGHTS}
    w_in_t = _all_gather(shards["w_in"], "allgather_w_in").reshape(IN_W, D_MODEL)

    small = {n: wts[n] for n in SMALL_WEIGHTS}
    grad_x, recv, _, g_mix_part = _local_step(x.reshape(t, D_MODEL), loss_target.reshape(t, D_MODEL), {"w_in": w_in_t},
                                              small, shards)

    results = {}
    for n in BIG_WEIGHTS:
        c, k = shards[n].shape
        w2, m2, v2 = wts[n][0], moms[n][0], vars_[n][0]
        if n in ROW_SHARDED:
            res = _adam(recv[n], w2, m2, v2, "adam_" + n, c // 2)
        else:
            parts = [recv[f"{n}:{hf}"] for hf in range(2)] if n in HALVED else [recv[n]]
            g_t = _sum_partials(parts, "sum_" + n, c // 2)
            res = _adam(g_t.T[None], w2, m2, v2, "adam_" + n, k // 2)
        for kind, a in zip(("grad", "delta", "new_m", "new_v"), res):
            results[kind, n] = a[None]

    g_mix_all = _all_gather(jnp.pad(g_mix_part, ((0, 7), (0, 0))), "allgather_g_mix")
    g_mix = _sum_partials([g_mix_all], "sum_g_mix", 8)[0:1]
    loss, sgrads = _unpack_small(_sum_partials([recv["small"]], "sum_small", _PK_ROWS), g_mix)
    for n in SMALL_WEIGHTS:
        results["grad", n] = sgrads[n]
    results.update(_adam_small(sgrads, wts, moms, vars_))
    outs = [loss, grad_x.reshape(x.shape)]
    for kind in ("grad", "delta", "new_m", "new_v"):
        outs += [results[kind, n] for n in ALL_WEIGHTS]
    return tuple(outs)
```

```python
import functools
import math

import jax
import jax.numpy as jnp
from jax import lax
from jax.experimental import pallas as pl
from jax.experimental.pallas import tpu as pltpu

F32 = jnp.float32
BF16 = jnp.bfloat16
MXU_DTYPE = jnp.bfloat16

N_DEV = 8
D_MODEL = 1024
SEQ = 2048
HEAD_DIM = 64
HEADS_PER_GROUP = 4
GROUP_W = HEADS_PER_GROUP * HEAD_DIM
DILATIONS = (1, 4, 16)
QKV_W = 3 * len(DILATIONS) * GROUP_W
Q_W = len(DILATIONS) * GROUP_W
ATT_BLOCK = 128
ROPE_DIM = 16
ROPE_THETA = 500000.0
SSM_W = 512
SSM_GROUPS = 32
SSM_CH = 16
SSM_STATE = 64
N_STATE = SSM_GROUPS * SSM_STATE
D_FF = 2816
IN_W = QKV_W + SSM_W + 2 * D_MODEL
RMS_EPS = 1e-6
NEG_INF = -1e30
LANES = 128

SCAN_SEG_PER_SAMPLE = 8
SCAN_LEN = SEQ // SCAN_SEG_PER_SAMPLE
SCAN_WC = 512
SCAN_NBLK = N_STATE // SCAN_WC
SCAN_CH = SSM_W // SCAN_NBLK
SCAN_CHUNK = 32

ADAM_LR = 0.001
ADAM_B1 = 0.9
ADAM_B2 = 0.999
ADAM_EPS = 1e-08
ADAM_WD = 0.01
ADAM_STEP = 10

VMEM_BIG = 48 * 1024 * 1024
VMEM_MID = 32 * 1024 * 1024

BIG_WEIGHTS = ("w_in", "w_glu", "w_attn_out", "w_out", "w_ffn_gate", "w_ffn_up", "w_ffn_down")
ROW_SHARDED = ("w_out", "w_ffn_down")
SMALL_WEIGHTS = ("norm_mix_g", "ssm_a_re", "ssm_a_im", "ssm_log_dt", "ssm_b_re", "ssm_b_im", "ssm_c_re", "ssm_c_im",
                 "ssm_d", "norm_ffn_g", "norm_final_g")
ALL_WEIGHTS = ("norm_mix_g", "w_in", "ssm_a_re", "ssm_a_im", "ssm_log_dt", "ssm_b_re", "ssm_b_im", "ssm_c_re", "ssm_c_im",
               "ssm_d", "w_glu", "w_attn_out", "w_out", "norm_ffn_g", "w_ffn_gate", "w_ffn_up", "w_ffn_down", "norm_final_g")


def _sigmoid(x):
    return 1.0 / (1.0 + jnp.exp(-x))


def _pallas_call(body, *, out_shape, **kw):
    single = not isinstance(out_shape, (list, tuple))
    shapes = [pltpu.HBM(s.shape, s.dtype) for s in ([out_shape] if single else out_shape)]
    call = pl.pallas_call(body, out_shape=shapes[0] if single else shapes, **kw)
    return lambda *operands: call(*[pltpu.with_memory_space_constraint(o, pltpu.HBM) for o in operands])


class _Comm:
    def __init__(self, ins, out_shapes, n_sem, n_local, start, finish):
        self.ins, self.out_shapes, self.n_sem, self.n_local = ins, out_shapes, n_sem, n_local
        self.start, self.finish = start, finish


def _mm(a, b, mode, name, tm, tn, out_dtype=F32, add=None, vmem=VMEM_BIG, comm=None, cols=None):
    if mode == "nn":
        (m, k), (_, n) = a.shape, b.shape
        a_spec = pl.BlockSpec((tm, k), lambda i, j: (i, 0))
        b_spec = pl.BlockSpec((k, tn), lambda i, j: (0, j))
        dims = (((1,), (0,)), ((), ()))
    elif mode == "nt":
        (m, k), (n, _) = a.shape, b.shape
        a_spec = pl.BlockSpec((tm, k), lambda i, j: (i, 0))
        b_spec = pl.BlockSpec((tn, k), lambda i, j: (j, 0))
        dims = (((1,), (1,)), ((), ()))
    else:
        (k, m), (_, n) = a.shape, b.shape
        first, n = cols if cols else (0, n)
        a_spec = pl.BlockSpec((k, tm), lambda i, j: (0, i))
        b_spec = pl.BlockSpec((k, tn), lambda i, j: (0, j + first // tn))
        dims = (((0,), (0,)), ((), ()))
    assert m % tm == 0 and n % tn == 0, (name, m, n, tm, tn)
    o_spec = pl.BlockSpec((tm, tn), lambda i, j: (i, j))
    has_add = add is not None

    def body(*refs):
        a_ref, b_ref, o_ref = refs[0], refs[1], refs[-1]
        acc = lax.dot_general(a_ref[...].astype(MXU_DTYPE), b_ref[...].astype(MXU_DTYPE), dims,
                              preferred_element_type=F32)
        if has_add:
            acc = acc + refs[2][...]
        o_ref[...] = acc.astype(out_dtype)

    ins = [a, b] + ([add] if has_add else [])
    in_specs = [a_spec, b_spec] + ([o_spec] if has_add else [])
    return _grid_call(body, name, (m // tm, n // tn), ins, in_specs, [o_spec],
                      [jax.ShapeDtypeStruct((m, n), out_dtype)], vmem, comm)


def _grid_call(body, name, grid, ins, in_specs, out_specs, out_shapes, vmem, comm=None, sequential=False, scratch=()):
    if comm is None:
        single = len(out_shapes) == 1
        semantics = ("arbitrary", "arbitrary") if sequential else ("parallel", "parallel")
        return _pallas_call(
            body, name=name, grid=grid, in_specs=in_specs, out_specs=out_specs[0] if single else out_specs,
            out_shape=out_shapes[0] if single else out_shapes, scratch_shapes=list(scratch),
            compiler_params=pltpu.CompilerParams(dimension_semantics=semantics, vmem_limit_bytes=vmem),
        )(*ins)
    n_in, n_out, n_cin, n_cout = len(ins), len(out_shapes), len(comm.ins), len(comm.out_shapes)
    n_io = n_in + n_cin + n_out + n_cout

    def carrying(*refs):
        own = refs[:n_in] + refs[n_in + n_cin:n_in + n_cin + n_out] + refs[n_io:len(refs) - 3]
        c_args = (refs[n_in:n_in + n_cin], refs[n_in + n_cin + n_out:n_io], *refs[-3:])

        @pl.when((pl.program_id(0) == 0) & (pl.program_id(1) == 0))
        def _():
            comm.start(*c_args)

        body(*own)

        @pl.when((pl.program_id(0) == grid[0] - 1) & (pl.program_id(1) == grid[1] - 1))
        def _():
            comm.finish(*c_args)

    hbm = pl.BlockSpec(memory_space=pl.ANY)
    return _pallas_call(
        carrying, name=name, grid=grid, in_specs=list(in_specs) + [hbm] * n_cin,
        out_specs=list(out_specs) + [hbm] * n_cout, out_shape=list(out_shapes) + list(comm.out_shapes),
        scratch_shapes=list(scratch) + [pltpu.SemaphoreType.DMA((comm.n_sem,)), pltpu.SemaphoreType.DMA((comm.n_sem,)),
                                        pltpu.SemaphoreType.DMA((comm.n_local,))],
        compiler_params=pltpu.CompilerParams(dimension_semantics=("arbitrary", "arbitrary"), vmem_limit_bytes=vmem),
    )(*ins, *comm.ins)


def _rows(body, name, n_rows, tm, ins, outs, vmem=VMEM_MID, scratch=()):
    assert n_rows % tm == 0
    arrays, in_specs = [], []
    for kind, arr in ins:
        arrays.append(arr)
        if kind == "row":
            assert n_rows % arr.shape[0] == 0, (name, arr.shape)
            in_specs.append(pl.BlockSpec((tm * arr.shape[0] // n_rows, arr.shape[1]), lambda i: (i, 0)))
        elif kind == "tab":
            nblk = arr.shape[0] // tm
            in_specs.append(pl.BlockSpec((tm, arr.shape[1]), lambda i, nblk=nblk: (i % nblk, 0)))
        else:
            in_specs.append(pl.BlockSpec(arr.shape, lambda i, nd=arr.ndim: (0,) * nd))
    out_specs, out_shape = [], []
    for kind, shp, dt in outs:
        if kind == "row":
            out_specs.append(pl.BlockSpec((tm, shp), lambda i: (i, 0)))
            out_shape.append(jax.ShapeDtypeStruct((n_rows, shp), dt))
        elif kind == "dil":
            d, wd = shp
            out_specs.append(pl.BlockSpec((tm // d, d * wd), lambda i: (i, 0)))
            out_shape.append(jax.ShapeDtypeStruct((n_rows // d, d * wd), dt))
        else:
            out_specs.append(pl.BlockSpec(shp, lambda i, nd=len(shp): (0,) * nd))
            out_shape.append(jax.ShapeDtypeStruct(shp, dt))
    res = _pallas_call(
        body, name=name, grid=(n_rows // tm,), in_specs=in_specs, out_specs=out_specs, out_shape=out_shape,
        scratch_shapes=list(scratch),
        compiler_params=pltpu.CompilerParams(dimension_semantics=("arbitrary",), vmem_limit_bytes=vmem),
    )(*arrays)
    return res


def _gather_residue(stage, ch, r, d, n):
    return stage[ch, pl.ds(r, n, stride=d), :] if d > 1 else stage[ch]


def _scatter_residue(stage, ch, r, d, n, val):
    if d > 1:
        stage[ch, pl.ds(r, n, stride=d), :] = val
    else:
        stage[ch] = val


def _lane_chunk(ch):
    return slice(ch * LANES, (ch + 1) * LANES)


def _first_step():
    return pl.program_id(0) == 0


def _rope_tables():
    half = ROPE_DIM // 2
    inv = jnp.power(jnp.float32(ROPE_THETA), -jnp.arange(half, dtype=F32) * 2.0 / ROPE_DIM)
    ang = jnp.arange(SEQ, dtype=F32)[:, None] * inv[None, :]
    lane = jnp.arange(LANES) % HEAD_DIM
    cosl = jnp.cos(ang)[:, lane % half]
    sinl = jnp.sin(ang)[:, lane % half]
    tab_c = jnp.where(lane < ROPE_DIM, cosl, 1.0)
    tab_lo = jnp.where(lane < half, -sinl, 0.0)
    tab_hi = jnp.where((lane >= half) & (lane < ROPE_DIM), sinl, 0.0)
    return tab_c.astype(F32), tab_lo.astype(F32), tab_hi.astype(F32)


def _rope_apply(t, tc, tlo, thi):
    half = ROPE_DIM // 2
    return t * tc + pltpu.roll(t, LANES - half, 1) * tlo + pltpu.roll(t, half, 1) * thi


def _rope_transpose(dt, tc, tlo, thi):
    half = ROPE_DIM // 2
    return dt * tc + pltpu.roll(dt * tlo, half, 1) + pltpu.roll(dt * thi, LANES - half, 1)


def _pack_dproj(dqs, dks, dvs, du, dgpre, tabs):
    tm = 256

    def body(*refs):
        dq_refs, dk_refs, dv_refs = refs[0:3], refs[3:6], refs[6:9]
        du_ref, dg_ref, tc_ref, tlo_ref, thi_ref, o_ref, stage = refs[9:16]
        n_ch = QKV_W // LANES
        halves = GROUP_W // LANES
        for grp, d in enumerate(DILATIONS):
            for which, src in enumerate((dq_refs[grp], dk_refs[grp], dv_refs[grp])):
                for res in range(d):
                    for half in range(halves):
                        _scatter_residue(stage, which * (n_ch // 3) + grp * halves + half, res, d, tm // d,
                                         src[:, _lane_chunk(res * halves + half)])
        tc, tlo, thi = tc_ref[...], tlo_ref[...], thi_ref[...]
        for ch in range(n_ch):
            piece = stage[ch]
            o_ref[:, _lane_chunk(ch)] = (_rope_transpose(piece, tc, tlo, thi) if ch < 2 * n_ch // 3 else piece).astype(BF16)
        o_ref[:, QKV_W:QKV_W + SSM_W] = du_ref[...].astype(BF16)
        o_ref[:, QKV_W + SSM_W:] = dg_ref[...].astype(BF16)

    t = du.shape[0]
    ins = [("row", a) for a in (*dqs, *dks, *dvs, du, dgpre)] + [("tab", tb) for tb in tabs]
    return _rows(body, "pack_dproj", t, tm, ins, [("row", IN_W, BF16)],
                 scratch=[pltpu.VMEM((QKV_W // LANES, tm, LANES), F32)])[0]


def _merge_groups(o_refs, l_refs, a_ref, lt_ref, nat, tm):
    halves = GROUP_W // LANES
    for grp, d in enumerate(DILATIONS[1:], start=1):
        for j, src in enumerate((o_refs[grp], l_refs[grp])):
            for res in range(d):
                for half in range(halves):
                    _scatter_residue(nat, (grp - 1) * 4 + j * 2 + half, res, d, tm // d,
                                     src[:, _lane_chunk(res * halves + half)])
    for half in range(halves):
        sl = _lane_chunk(half)
        la, lb, lc = l_refs[0][:, sl], nat[2 + half], nat[6 + half]
        m = jnp.maximum(jnp.maximum(la, lb), lc)
        ea, eb, ec = jnp.exp(la - m), jnp.exp(lb - m), jnp.exp(lc - m)
        ssum = ea + eb + ec
        a_ref[:, sl] = (ea / ssum) * o_refs[0][:, sl] + (eb / ssum) * nat[half] + (ec / ssum) * nat[4 + half]
        lt_ref[:, sl] = m + jnp.log(ssum)


def _head_sum_matrix():
    r = jnp.arange(GROUP_W) // HEAD_DIM
    return (r[:, None] == r[None, :]).astype(F32)


def _attention_cotangents(da, attn, lt, ones, rd_ref, dil, stage, tm):
    halves = GROUP_W // LANES
    rd = jnp.dot(da * attn, ones, preferred_element_type=F32, precision=lax.Precision.HIGHEST)
    rd_ref[...] = rd
    for half in range(halves):
        for j, val in enumerate((da, lt, rd)):
            stage[2 * j + half] = val[:, _lane_chunk(half)]
    for grp, d in enumerate(DILATIONS[1:], start=1):
        for j in range(3):
            for res in range(d):
                for half in range(halves):
                    dil[3 * (grp - 1) + j][:, _lane_chunk(res * halves + half)] = _gather_residue(
                        stage, 2 * j + half, res, d, tm // d)


_GELU_C = math.sqrt(2.0 / math.pi)


def _ssm_act_bwd(dyg, ytot, u_perm, dskip):
    def body(dyg_ref, yt_ref, u_ref, d_ref, dy_ref, dus_ref, dd_ref):
        @pl.when(_first_step())
        def _():
            dd_ref[...] = jnp.zeros_like(dd_ref)

        yt = yt_ref[...]
        th = jnp.tanh(_GELU_C * (yt + 0.044715 * (yt * yt * yt)))
        dgelu = 0.5 * (1.0 + th) + 0.5 * yt * (1.0 - th * th) * _GELU_C * (1.0 + 3.0 * 0.044715 * yt * yt)
        dy = dyg_ref[...] * dgelu
        dy_ref[...] = dy.astype(BF16)
        dus_ref[...] = dy * d_ref[...]
        dd_ref[...] += jnp.sum(dy * u_ref[...], axis=0, keepdims=True)

    t = dyg.shape[0]
    return _rows(body, "ssm_act_bwd", t, 512, [("row", dyg), ("row", ytot), ("row", u_perm), ("const", dskip)],
                 [("row", SSM_W, BF16), ("row", SSM_W, F32), ("acc", (1, SSM_W), F32)])


def _head_masks():
    lane = lax.broadcasted_iota(jnp.int32, (1, GROUP_W), 1)
    return [(lane // HEAD_DIM) == h for h in range(HEADS_PER_GROUP)]


def _stack_heads(blk, masks, fill=0.0):
    return jnp.concatenate([jnp.where(mk, blk, jnp.full_like(blk, fill)) for mk in masks], axis=0)


def _unstack_heads(stacked, masks):
    rows = stacked.shape[0] // len(masks)
    out = stacked[:rows]
    for h in range(1, len(masks)):
        out = jnp.where(masks[h], stacked[h * rows:(h + 1) * rows], out)
    return out


def _band_mask(first):
    nk = ATT_BLOCK if first else 2 * ATT_BLOCK
    qi = lax.broadcasted_iota(jnp.int32, (ATT_BLOCK, nk), 0)
    ki = lax.broadcasted_iota(jnp.int32, (ATT_BLOCK, nk), 1)
    dist = qi - ki + (0 if first else ATT_BLOCK)
    return (dist >= 0) & (dist <= ATT_BLOCK)


_NT = (((1,), (1,)), ((), ()))
_TN = (((0,), (0,)), ((), ()))


def _attn_fwd(q, k, v, group, n_samples, comm=None):
    d = DILATIONS[group]
    length = SEQ // d
    nb = length // ATT_BLOCK

    def body(q_ref, k_ref, v_ref, o_ref, l_ref):
        masks = _head_masks()

        def block(qs, ks, first):
            nk = ATT_BLOCK if first else 2 * ATT_BLOCK
            qb = q_ref[0, pl.ds(qs, ATT_BLOCK), :]
            kc = k_ref[0, pl.ds(ks, nk), :]
            vc = v_ref[0, pl.ds(ks, nk), :]
            q4 = _stack_heads(qb, masks)
            valid = jnp.tile(_band_mask(first), (HEADS_PER_GROUP, 1))
            s = lax.dot_general(q4, kc, _NT, preferred_element_type=F32) * (HEAD_DIM ** -0.5)
            s = jnp.where(valid, s, NEG_INF)
            m = jnp.max(s, axis=-1, keepdims=True)
            p = jnp.exp(s - m)
            l = jnp.sum(p, axis=-1, keepdims=True)
            o4 = jnp.dot(p.astype(MXU_DTYPE), vc, preferred_element_type=F32) / l
            lse4 = jnp.broadcast_to(m + jnp.log(l), o4.shape)
            o_ref[0, pl.ds(qs, ATT_BLOCK), :] = _unstack_heads(o4, masks)
            l_ref[0, pl.ds(qs, ATT_BLOCK), :] = _unstack_heads(lse4, masks)

        block(0, 0, True)
        if nb > 1:
            def loop(n, carry):
                block(pl.multiple_of(n * ATT_BLOCK, ATT_BLOCK), pl.multiple_of((n - 1) * ATT_BLOCK, ATT_BLOCK), False)
                return carry

            lax.fori_loop(1, nb, loop, 0)

    per_sample = lambda a: a.reshape(n_samples, length, d * GROUP_W)
    spec = pl.BlockSpec((1, length, GROUP_W), lambda b, r: (b, 0, r))
    shp = jax.ShapeDtypeStruct((n_samples, length, d * GROUP_W), F32)
    o, lse, *carried = _grid_call(body, f"attn_fwd_g{group}", (n_samples, d), [per_sample(a) for a in (q, k, v)],
                                  [spec] * 3, [spec] * 2, [shp, shp], VMEM_MID, comm)
    flat = lambda a: a.reshape(n_samples * length, d * GROUP_W)
    return flat(o), flat(lse), carried


def _attn_bwd(q, k, v, dattn, lse_tot, rowdot, group, n_samples, comm=None):
    d = DILATIONS[group]
    length = SEQ // d
    nb = length // ATT_BLOCK

    def body(q_ref, k_ref, v_ref, da_ref, lt_ref, rd_ref, dq_ref, dk_ref, dv_ref):
        masks = _head_masks()
        dk_ref[...] = jnp.zeros_like(dk_ref)
        dv_ref[...] = jnp.zeros_like(dv_ref)

        def block(qs, ks, first):
            nk = ATT_BLOCK if first else 2 * ATT_BLOCK
            qb = q_ref[0, pl.ds(qs, ATT_BLOCK), :]
            kc = k_ref[0, pl.ds(ks, nk), :]
            vc = v_ref[0, pl.ds(ks, nk), :]
            da = da_ref[0, pl.ds(qs, ATT_BLOCK), :]
            lt = lt_ref[0, pl.ds(qs, ATT_BLOCK), :]
            rd = rd_ref[0, pl.ds(qs, ATT_BLOCK), :]
            q4 = _stack_heads(qb, masks)
            da4 = _stack_heads(da, masks).astype(MXU_DTYPE)
            lt4 = jnp.max(_stack_heads(lt, masks, -jnp.inf), axis=-1, keepdims=True)
            rd4 = jnp.max(_stack_heads(rd, masks, -jnp.inf), axis=-1, keepdims=True)
            valid = jnp.tile(_band_mask(first), (HEADS_PER_GROUP, 1))
            s = lax.dot_general(q4, kc, _NT, preferred_element_type=F32) * (HEAD_DIM ** -0.5)
            s = jnp.where(valid, s, NEG_INF)
            p = jnp.exp(s - lt4)
            dp = lax.dot_general(da4, vc, _NT, preferred_element_type=F32)
            ds = (p * (dp - rd4) * (HEAD_DIM ** -0.5)).astype(MXU_DTYPE)
            dq_ref[0, pl.ds(qs, ATT_BLOCK), :] = _unstack_heads(jnp.dot(ds, kc, preferred_element_type=F32), masks)
            dk_ref[0, pl.ds(ks, nk), :] += lax.dot_general(ds, q4, _TN, preferred_element_type=F32)
            dv_ref[0, pl.ds(ks, nk), :] += lax.dot_general(p.astype(MXU_DTYPE), da4, _TN, preferred_element_type=F32)

        block(0, 0, True)
        if nb > 1:
            def loop(n, carry):
                block(pl.multiple_of(n * ATT_BLOCK, ATT_BLOCK), pl.multiple_of((n - 1) * ATT_BLOCK, ATT_BLOCK), False)
                return carry

            lax.fori_loop(1, nb, loop, 0)

    per_sample = lambda a: a.reshape(n_samples, length, d * GROUP_W)
    spec = pl.BlockSpec((1, length, GROUP_W), lambda b, r: (b, 0, r))
    shp = jax.ShapeDtypeStruct((n_samples, length, d * GROUP_W), F32)
    dq, dk, dv, *carried = _grid_call(
        body, f"attn_bwd_g{group}", (n_samples, d), [per_sample(a) for a in (q, k, v, dattn, lse_tot, rowdot)],
        [spec] * 6, [spec] * 3, [shp, shp, shp], VMEM_MID, comm)
    flat = lambda a: a.reshape(n_samples * length, d * GROUP_W)
    return flat(dq), flat(dk), flat(dv), carried


def _disc(lr, li, ldt, br, bi):
    dt = jnp.exp(ldt)
    mag = jnp.exp(lr * dt)
    ab_re, ab_im = mag * jnp.cos(li * dt), mag * jnp.sin(li * dt)
    den = lr * lr + li * li
    nr, ni = ab_re - 1.0, ab_im
    f_re = (nr * lr + ni * li) / den
    f_im = (ni * lr - nr * li) / den
    return ab_re, ab_im, f_re * br - f_im * bi, f_re * bi + f_im * br


def _state_mask():
    row_g = lax.broadcasted_iota(jnp.int32, (SCAN_CH, SCAN_WC), 0) // SSM_CH
    col_g = lax.broadcasted_iota(jnp.int32, (SCAN_CH, SCAN_WC), 1) // SSM_STATE
    return row_g == col_g


def _ssm_disc(lr, li, ldt, br, bi, cr, ci):
    w = SCAN_WC

    def body(lr_ref, li_ref, ldt_ref, br_ref, bi_ref, cr_ref, ci_ref, a_ref, bb_ref, c_ref):
        ar, ai, bbr, bbi = _disc(lr_ref[...], li_ref[...], ldt_ref[...], br_ref[...], bi_ref[...])
        crv, civ = cr_ref[...], ci_ref[...]
        mask = _state_mask()
        for cb in range(SCAN_NBLK):
            sl = slice(cb * w, (cb + 1) * w)
            rows = slice(cb * SCAN_CH, (cb + 1) * SCAN_CH)
            dense = lambda comp: jnp.where(mask, jnp.tile(comp[:, sl], (SCAN_CH // SSM_CH, 1)), 0.0)
            a_ref[:, 2 * cb * w:(2 * cb + 1) * w] = ar[:, sl]
            a_ref[:, (2 * cb + 1) * w:(2 * cb + 2) * w] = ai[:, sl]
            bb_ref[rows, :w] = dense(bbr).astype(MXU_DTYPE)
            bb_ref[rows, w:] = dense(bbi).astype(MXU_DTYPE)
            c_ref[rows, :w] = dense(crv).astype(MXU_DTYPE)
            c_ref[rows, w:] = (-dense(civ)).astype(MXU_DTYPE)

    return _pallas_call(
        body, name="ssm_disc",
        out_shape=[jax.ShapeDtypeStruct((1, 2 * N_STATE), F32), jax.ShapeDtypeStruct((SSM_W, 2 * w), MXU_DTYPE),
                   jax.ShapeDtypeStruct((SSM_W, 2 * w), MXU_DTYPE)],
        compiler_params=pltpu.CompilerParams(vmem_limit_bytes=VMEM_MID),
    )(lr, li, ldt, br, bi, cr, ci)


def _group_indicator():
    s = jnp.arange(N_STATE) // SSM_STATE
    return (s[:, None] == jnp.arange(LANES)[None, :]).astype(F32)


def _ssm_param_bwd(lr, li, ldt, br, bi, da_cat, dbb_full, dc_full):
    w = SCAN_WC

    def body(lr_ref, li_ref, ldt_ref, br_ref, bi_ref, da_ref, dbb_ref, dc_ref, ind_ref,
             glr_ref, gli_ref, gldt_ref, gbr_ref, gbi_ref, gcr_ref, gci_ref):
        mask = _state_mask()

        def diag_parts(ref):
            res = ([], [])
            for cb in range(SCAN_NBLK):
                for part in range(2):
                    blk = ref[cb * SCAN_CH:(cb + 1) * SCAN_CH, part * w:(part + 1) * w]
                    res[part].append(jnp.sum(jnp.where(mask, blk, 0.0).reshape(SCAN_CH // SSM_CH, SSM_CH, w), axis=0))
            return jnp.concatenate(res[0], axis=1), jnp.concatenate(res[1], axis=1)

        dar = jnp.concatenate([da_ref[:, 2 * cb * w:(2 * cb + 1) * w] for cb in range(SCAN_NBLK)], axis=1)
        dai = jnp.concatenate([da_ref[:, (2 * cb + 1) * w:(2 * cb + 2) * w] for cb in range(SCAN_NBLK)], axis=1)
        dbbr, dbbi = diag_parts(dbb_ref)
        dcr, dci_neg = diag_parts(dc_ref)
        gcr_ref[...] = dcr
        gci_ref[...] = -dci_neg
        _, vjp = jax.vjp(_disc, lr_ref[...], li_ref[...], ldt_ref[...], br_ref[...], bi_ref[...])
        glr, gli, gldt, gbr, gbi = vjp((dar, dai, dbbr, dbbi))
        glr_ref[...] = glr
        gli_ref[...] = gli
        gldt_ref[...] = jnp.dot(jnp.broadcast_to(gldt, (8, N_STATE)), ind_ref[...], preferred_element_type=F32,
                                precision=lax.Precision.HIGHEST)
        gbr_ref[...] = gbr
        gbi_ref[...] = gbi

    v1 = jax.ShapeDtypeStruct((1, N_STATE), F32)
    v16 = jax.ShapeDtypeStruct((SSM_CH, N_STATE), F32)
    vdt = jax.ShapeDtypeStruct((8, LANES), F32)
    return _pallas_call(
        body, name="ssm_param_bwd", out_shape=[v1, v1, vdt, v16, v16, v16, v16],
        compiler_params=pltpu.CompilerParams(vmem_limit_bytes=VMEM_BIG),
    )(lr, li, ldt, br, bi, da_cat, dbb_full, dc_full, _group_indicator())


def _cmul(ar, ai, br, bi):
    return ar * br - ai * bi, ar * bi + ai * br


def _gelu_tanh(y):
    return jnp.tanh(_GELU_C * (y + 0.044715 * (y * y * y)))


def _segment_carry(er, ei, ar, ai, n_rows, reverse):
    qr, qi = ar, ai
    for _ in range(int(math.log2(SCAN_LEN))):
        qr, qi = _cmul(qr, qi, qr, qi)
    seg = lax.broadcasted_iota(jnp.int32, er.shape, 0) % SCAN_SEG_PER_SAMPLE
    shift = 1
    while shift < SCAN_SEG_PER_SAMPLE:
        keep = (seg < SCAN_SEG_PER_SAMPLE - shift) if reverse else (seg >= shift)
        amount = n_rows - shift if reverse else shift
        sr = jnp.where(keep, pltpu.roll(er, amount, 0), 0.0)
        si = jnp.where(keep, pltpu.roll(ei, amount, 0), 0.0)
        if reverse:
            er, ei = er + qr * sr + qi * si, ei + qr * si - qi * sr
        else:
            er, ei = er + qr * sr - qi * si, ei + qr * si + qi * sr
        qr, qi = _cmul(qr, qi, qr, qi)
        shift *= 2
    keep = (seg < SCAN_SEG_PER_SAMPLE - 1) if reverse else (seg >= 1)
    amount = n_rows - 1 if reverse else 1
    return jnp.where(keep, pltpu.roll(er, amount, 0), 0.0), jnp.where(keep, pltpu.roll(ei, amount, 0), 0.0)


def _ssm_fwd(u_perm, a_cat, bbc, cc, dskip, n_rows):
    t = u_perm.shape[0]
    w = SCAN_WC
    rows_c = SCAN_CHUNK * n_rows
    n_chunks = t // rows_c

    assert n_chunks % 2 == 0

    def body(u_ref, a_ref, bb_ref, c_ref, d_ref, yt_ref, yg_ref, ein_ref, bu_all, st_a, st_b, xs_a, xs_b):
        ar = jnp.broadcast_to(a_ref[:, :w], (n_rows, w))
        ai = jnp.broadcast_to(a_ref[:, w:], (n_rows, w))
        start = lambda ch: pl.multiple_of(ch * rows_c, rows_c)

        def project(ch, stage):
            res = jnp.dot(u_ref[pl.ds(start(ch), rows_c), :].astype(MXU_DTYPE), bb_ref[...], preferred_element_type=F32)
            stage[...] = res
            bu_all[pl.ds(start(ch), rows_c), :] = res

        def steps(src, r0, carry, xs=None):
            for i in range(SCAN_CHUNK):
                blk = src[pl.ds(r0 + i * n_rows, n_rows), :]
                carry = (ar * carry[0] - ai * carry[1] + blk[:, :w], ar * carry[1] + ai * carry[0] + blk[:, w:])
                if xs is not None:
                    xs[i * n_rows:(i + 1) * n_rows, :w] = carry[0]
                    xs[i * n_rows:(i + 1) * n_rows, w:] = carry[1]
            return carry

        def emit(xs, ch):
            y = lax.dot_general(xs[...].astype(MXU_DTYPE), c_ref[...], _NT, preferred_element_type=F32)
            yt = y + d_ref[...] * u_ref[pl.ds(start(ch), rows_c), :]
            yt_ref[pl.ds(start(ch), rows_c), :] = yt
            yg_ref[pl.ds(start(ch), rows_c), :] = (0.5 * yt * (1.0 + _gelu_tanh(yt))).astype(BF16)

        project(0, st_a)

        def pair1(p, carry):
            project(2 * p + 1, st_b)
            carry = steps(st_a, 0, carry)
            project(jnp.minimum(2 * p + 2, n_chunks - 1), st_a)
            return steps(st_b, 0, carry)

        zero = jnp.zeros((n_rows, w), F32)
        er, ei = lax.fori_loop(0, n_chunks // 2, pair1, (zero, zero))
        cr, ci = _segment_carry(er, ei, ar, ai, n_rows, False)
        ein_ref[:, :w] = cr
        ein_ref[:, w:] = ci

        xs_b[...] = jnp.zeros_like(xs_b)

        def pair2(p, carry):
            emit(xs_b, jnp.maximum(2 * p - 1, 0))
            carry = steps(bu_all, start(2 * p), carry, xs_a)
            emit(xs_a, 2 * p)
            return steps(bu_all, start(2 * p + 1), carry, xs_b)

        lax.fori_loop(0, n_chunks // 2, pair2, (cr, ci))
        emit(xs_b, n_chunks - 1)

    col = lambda width: pl.BlockSpec((t, width), lambda c: (0, c))
    wgt = pl.BlockSpec((SCAN_CH, 2 * w), lambda c: (c, 0))
    return _pallas_call(
        body, name="ssm_fwd", grid=(SCAN_NBLK,),
        in_specs=[col(SCAN_CH), pl.BlockSpec((1, 2 * w), lambda c: (0, c)), wgt, wgt,
                  pl.BlockSpec((1, SCAN_CH), lambda c: (0, c))],
        out_specs=[col(SCAN_CH), col(SCAN_CH), pl.BlockSpec((n_rows, 2 * w), lambda c: (0, c))],
        out_shape=[jax.ShapeDtypeStruct((t, SSM_W), F32), jax.ShapeDtypeStruct((t, SSM_W), BF16),
                   jax.ShapeDtypeStruct((n_rows, 2 * N_STATE), F32)],
        scratch_shapes=[pltpu.VMEM((t, 2 * w), F32)] + [pltpu.VMEM((rows_c, 2 * w), F32)] * 4,
        compiler_params=pltpu.CompilerParams(dimension_semantics=("parallel",), vmem_limit_bytes=VMEM_BIG),
    )(u_perm, a_cat, bbc, cc, dskip)


def _ssm_bwd(u_perm, dypre, du_skip, a_cat, bbc, cc, ein, n_rows, comm=None):
    t = u_perm.shape[0]
    w = SCAN_WC
    rows_c = SCAN_CHUNK * n_rows
    n_chunks = t // rows_c

    assert n_chunks % 2 == 0
    last = n_chunks - 1

    def body(u_ref, dy_ref, dus_ref, a_ref, bb_ref, c_ref, ein_ref, du_ref, da_ref, dbb_ref, dc_ref,
             xs_all, st_a, st_b, buf_a, buf_b):
        ar = jnp.broadcast_to(a_ref[:, :w], (n_rows, w))
        ai = jnp.broadcast_to(a_ref[:, w:], (n_rows, w))
        zero = jnp.zeros((n_rows, w), F32)
        start = lambda ch: pl.multiple_of(ch * rows_c, rows_c)
        dbb_ref[...] = jnp.zeros_like(dbb_ref)
        dc_ref[...] = jnp.zeros_like(dc_ref)
        da_ref[...] = jnp.zeros_like(da_ref)

        xs_all[0:n_rows, :] = ein_ref[...]

        def project(ch, stage):
            stage[...] = jnp.dot(u_ref[pl.ds(start(ch), rows_c), :].astype(MXU_DTYPE), bb_ref[...],
                                 preferred_element_type=F32)

        def fwd_steps(stage, ch, carry, xs):
            for i in range(SCAN_CHUNK):
                blk = stage[i * n_rows:(i + 1) * n_rows, :]
                carry = (ar * carry[0] - ai * carry[1] + blk[:, :w], ar * carry[1] + ai * carry[0] + blk[:, w:])
                for half, val in enumerate(carry):
                    xs[i * n_rows:(i + 1) * n_rows, half * w:(half + 1) * w] = val
                    xs_all[pl.ds(start(ch) + (i + 1) * n_rows, n_rows), half * w:(half + 1) * w] = val
            return carry

        def add_dc(xs, ch):
            dc_ref[...] += lax.dot_general(dy_ref[pl.ds(start(ch), rows_c), :], xs[...].astype(MXU_DTYPE), _TN,
                                           preferred_element_type=F32)

        project(0, st_a)

        def fwd_pair(p, carry):
            project(2 * p + 1, st_b)
            carry = fwd_steps(st_a, 2 * p, carry, buf_a)
            add_dc(buf_a, 2 * p)
            project(jnp.minimum(2 * p + 2, last), st_a)
            carry = fwd_steps(st_b, 2 * p + 1, carry, buf_b)
            add_dc(buf_b, 2 * p + 1)
            return carry

        lax.fori_loop(0, n_chunks // 2, fwd_pair, (ein_ref[:, :w], ein_ref[:, w:]))

        def project_dx(ch, stage):
            stage[...] = jnp.dot(dy_ref[pl.ds(start(ch), rows_c), :], c_ref[...], preferred_element_type=F32)

        def back_steps(stage, carry, g_buf=None):
            for i in reversed(range(SCAN_CHUNK)):
                blk = stage[i * n_rows:(i + 1) * n_rows, :]
                carry = (blk[:, :w] + ar * carry[0] + ai * carry[1], blk[:, w:] + ar * carry[1] - ai * carry[0])
                if g_buf is not None:
                    g_buf[i * n_rows:(i + 1) * n_rows, :w] = carry[0]
                    g_buf[i * n_rows:(i + 1) * n_rows, w:] = carry[1]
            return carry

        def first_pair(p, carry):
            project_dx(last - 2 * p - 1, st_b)
            carry = back_steps(st_a, carry)
            project_dx(jnp.maximum(last - 2 * p - 2, 0), st_a)
            return back_steps(st_b, carry)

        project_dx(last, st_a)
        sr, si = lax.fori_loop(0, n_chunks // 2, first_pair, (zero, zero))
        gr0, gi0 = _segment_carry(sr, si, ar, ai, n_rows, True)

        def post(g_buf, ch):
            g = g_buf[...]
            xp = xs_all[pl.ds(start(ch), rows_c), :]
            da_ref[:, :w] += jnp.sum(g[:, :w] * xp[:, :w] + g[:, w:] * xp[:, w:], axis=0, keepdims=True)
            da_ref[:, w:] += jnp.sum(g[:, w:] * xp[:, :w] - g[:, :w] * xp[:, w:], axis=0, keepdims=True)
            gb = g.astype(MXU_DTYPE)
            du_ref[pl.ds(start(ch), rows_c), :] = (lax.dot_general(gb, bb_ref[...], _NT, preferred_element_type=F32)
                                                   + dus_ref[pl.ds(start(ch), rows_c), :])
            dbb_ref[...] += lax.dot_general(u_ref[pl.ds(start(ch), rows_c), :].astype(MXU_DTYPE), gb, _TN,
                                            preferred_element_type=F32)

        def second_pair(p, carry):
            c1 = last - 2 * p
            project_dx(c1 - 1, st_b)
            post(buf_b, jnp.minimum(c1 + 1, last))
            carry = back_steps(st_a, carry, buf_a)
            project_dx(jnp.maximum(c1 - 2, 0), st_a)
            post(buf_a, c1)
            return back_steps(st_b, carry, buf_b)

        project_dx(last, st_a)
        buf_b[...] = jnp.zeros_like(buf_b)
        lax.fori_loop(0, n_chunks // 2, second_pair, (gr0, gi0))
        post(buf_b, 0)

    col = lambda width: pl.BlockSpec((t, width), lambda c, j: (0, c))
    wgt = pl.BlockSpec((SCAN_CH, 2 * w), lambda c, j: (c, 0))
    row = pl.BlockSpec((1, 2 * w), lambda c, j: (0, c))
    return _grid_call(
        body, "ssm_bwd", (SCAN_NBLK, 1), [u_perm, dypre, du_skip, a_cat, bbc, cc, ein],
        [col(SCAN_CH), col(SCAN_CH), col(SCAN_CH), row, wgt, wgt, pl.BlockSpec((n_rows, 2 * w), lambda c, j: (0, c))],
        [col(SCAN_CH), row, wgt, wgt],
        [jax.ShapeDtypeStruct((t, SSM_W), F32), jax.ShapeDtypeStruct((1, 2 * N_STATE), F32),
         jax.ShapeDtypeStruct((SSM_W, 2 * w), F32), jax.ShapeDtypeStruct((SSM_W, 2 * w), F32)],
        56 * 1024 * 1024, comm,
        scratch=[pltpu.VMEM((t + n_rows, 2 * w), F32)] + [pltpu.VMEM((rows_c, 2 * w), F32)] * 4)


def _to_scan_rows(a, n_samples):
    c = a.shape[1]
    return a.reshape(n_samples, SCAN_SEG_PER_SAMPLE, SCAN_LEN, c).transpose(2, 0, 1, 3).reshape(-1, c)


def _from_scan_rows(a, n_samples):
    c = a.shape[1]
    return a.reshape(SCAN_LEN, n_samples, SCAN_SEG_PER_SAMPLE, c).transpose(1, 2, 0, 3).reshape(-1, c)


def _row_spec(tm, width):
    return pl.BlockSpec((tm, width), lambda i, j: (i, 0))


def _whole(arr):
    return pl.BlockSpec(arr.shape, lambda i, j: (0,) * arr.ndim)


def _proj_rope(x, g, w_in_t, tabs, comm=None):
    t = x.shape[0]
    tm = 256

    def body(x_ref, g_ref, w_ref, tc_ref, tlo_ref, thi_ref, h_ref, u_ref, gate_ref, *rest):
        qkv_refs, stage = rest[:9], rest[9]
        xv = x_ref[...]
        r = lax.rsqrt(jnp.mean(xv * xv, axis=-1, keepdims=True) + RMS_EPS)
        h = ((xv * r) * g_ref[...]).astype(BF16)
        h_ref[...] = h
        p = lax.dot_general(h.astype(MXU_DTYPE), w_ref[...], _NT, preferred_element_type=F32)
        u_ref[...] = p[:, QKV_W:QKV_W + SSM_W]
        gate_ref[...] = _sigmoid(p[:, QKV_W + SSM_W:])
        tc, tlo, thi = tc_ref[...], tlo_ref[...], thi_ref[...]
        n_ch = QKV_W // LANES
        for ch in range(n_ch):
            piece = p[:, _lane_chunk(ch)]
            stage[ch] = _rope_apply(piece, tc, tlo, thi) if ch < 2 * n_ch // 3 else piece
        halves = GROUP_W // LANES
        for grp, d in enumerate(DILATIONS):
            for which in range(3):
                out = qkv_refs[3 * grp + which]
                for res in range(d):
                    for half in range(halves):
                        ch = which * (n_ch // 3) + grp * halves + half
                        out[:, _lane_chunk(res * halves + half)] = _gather_residue(stage, ch, res, d, tm // d).astype(BF16)

    tab = pl.BlockSpec((tm, LANES), lambda i, j: (i % (SEQ // tm), 0))
    widths = [(D_MODEL, BF16), (SSM_W, F32), (2 * D_MODEL, F32)]
    out_specs = [_row_spec(tm, wd) for wd, _ in widths]
    out_shapes = [jax.ShapeDtypeStruct((t, wd), dt) for wd, dt in widths]
    for d in DILATIONS:
        out_specs += [_row_spec(tm // d, d * GROUP_W)] * 3
        out_shapes += [jax.ShapeDtypeStruct((t // d, d * GROUP_W), BF16)] * 3
    return _grid_call(
        body, "proj_rope", (t // tm, 1), [x, g, w_in_t, *tabs],
        [_row_spec(tm, D_MODEL), _whole(g), _whole(w_in_t), tab, tab, tab], out_specs, out_shapes, VMEM_BIG, comm,
        scratch=[pltpu.VMEM((QKV_W // LANES, tm, LANES), F32)])


def _branch_outputs(attn_ref, yg_ref, wao_ref, wglu_ref):
    attn_d = lax.dot_general(attn_ref[...].astype(MXU_DTYPE), wao_ref[...], _NT, preferred_element_type=F32)
    z = lax.dot_general(yg_ref[...].astype(MXU_DTYPE), wglu_ref[...], _NT, preferred_element_type=F32)
    return attn_d, z[:, :D_MODEL], _sigmoid(z[:, D_MODEL:])


def _mix_out_rms(os_, lses, yg, gates, x, w_ao_t, w_glu_t, w_out, g):
    t = x.shape[0]
    tm = 256

    def body(o0, o1, o2, l0, l1, l2, yg_ref, gate_ref, x_ref, wao_ref, wglu_ref, wout_ref, g_ref,
             attn_ref, lt_ref, m_ref, x1_ref, h_ref, nat):
        _merge_groups((o0, o1, o2), (l0, l1, l2), attn_ref, lt_ref, nat, tm)
        attn_d, za, sb = _branch_outputs(attn_ref, yg_ref, wao_ref, wglu_ref)
        merged = (gate_ref[:, :D_MODEL] * attn_d + gate_ref[:, D_MODEL:] * (za * sb)).astype(BF16)
        m_ref[...] = merged
        x1 = x_ref[...] + jnp.dot(merged.astype(MXU_DTYPE), wout_ref[...], preferred_element_type=F32)
        x1_ref[...] = x1
        r = lax.rsqrt(jnp.mean(x1 * x1, axis=-1, keepdims=True) + RMS_EPS)
        h_ref[...] = ((x1 * r) * g_ref[...]).astype(BF16)

    dil_specs = [_row_spec(tm // d, d * GROUP_W) for d in DILATIONS] * 2
    return _grid_call(
        body, "mix_out_rms", (t // tm, 1), [*os_, *lses, yg, gates, x, w_ao_t, w_glu_t, w_out, g],
        dil_specs + [_row_spec(tm, SSM_W), _row_spec(tm, 2 * D_MODEL), _row_spec(tm, D_MODEL),
                     _whole(w_ao_t), _whole(w_glu_t), _whole(w_out), _whole(g)],
        [_row_spec(tm, GROUP_W)] * 2 + [_row_spec(tm, D_MODEL)] * 3,
        [jax.ShapeDtypeStruct((t, GROUP_W), F32)] * 2
        + [jax.ShapeDtypeStruct((t, D_MODEL), BF16), jax.ShapeDtypeStruct((t, D_MODEL), F32),
           jax.ShapeDtypeStruct((t, D_MODEL), BF16)], VMEM_BIG, scratch=[pltpu.VMEM((8, tm, LANES), F32)])


def _mix_bwd(dx1b, attn, lse_tot, yg, gates, w_ao_t, w_glu_t, w_out, comm=None):
    t = dx1b.shape[0]
    tm = 256

    def body(dx_ref, attn_ref, lt_ref, yg_ref, gate_ref, wao_ref, wglu_ref, wout_ref, ones_ref,
             dad_ref, dz_ref, dg_ref, da_ref, dyg_ref, rd_ref, *rest):
        dm = lax.dot_general(dx_ref[...], wout_ref[...], _NT, preferred_element_type=F32)
        attn_d, za, sb = _branch_outputs(attn_ref, yg_ref, wao_ref, wglu_ref)
        g0, g1 = gate_ref[:, :D_MODEL], gate_ref[:, D_MODEL:]
        dad = (dm * g0).astype(BF16)
        dad_ref[...] = dad
        ds = dm * g1
        dza, dzb = (ds * sb).astype(BF16), (ds * za * sb * (1.0 - sb)).astype(BF16)
        dz_ref[:, :D_MODEL] = dza
        dz_ref[:, D_MODEL:] = dzb
        dg_ref[:, :D_MODEL] = (dm * attn_d * g0 * (1.0 - g0)).astype(BF16)
        dg_ref[:, D_MODEL:] = (dm * (za * sb) * g1 * (1.0 - g1)).astype(BF16)
        da = jnp.dot(dad.astype(MXU_DTYPE), wao_ref[...], preferred_element_type=F32)
        da_ref[...] = da
        dyg_ref[...] = (jnp.dot(dza.astype(MXU_DTYPE), wglu_ref[:D_MODEL, :], preferred_element_type=F32)
                        + jnp.dot(dzb.astype(MXU_DTYPE), wglu_ref[D_MODEL:, :], preferred_element_type=F32))
        _attention_cotangents(da, attn_ref[...], lt_ref[...], ones_ref[...], rd_ref, rest[:6], rest[6], tm)

    widths = [(D_MODEL, BF16), (2 * D_MODEL, BF16), (2 * D_MODEL, BF16), (GROUP_W, F32), (SSM_W, F32), (GROUP_W, F32)]
    out_specs = [_row_spec(tm, wd) for wd, _ in widths]
    out_shapes = [jax.ShapeDtypeStruct((t, wd), dt) for wd, dt in widths]
    for d in DILATIONS[1:]:
        out_specs += [_row_spec(tm // d, d * GROUP_W)] * 3
        out_shapes += [jax.ShapeDtypeStruct((t // d, d * GROUP_W), F32)] * 3
    ones = _head_sum_matrix()
    return _grid_call(
        body, "mix_bwd", (t // tm, 1), [dx1b, attn, lse_tot, yg, gates, w_ao_t, w_glu_t, w_out, ones],
        [_row_spec(tm, D_MODEL), _row_spec(tm, GROUP_W), _row_spec(tm, GROUP_W), _row_spec(tm, SSM_W),
         _row_spec(tm, 2 * D_MODEL), _whole(w_ao_t), _whole(w_glu_t), _whole(w_out), _whole(ones)],
        out_specs, out_shapes, VMEM_BIG, comm, scratch=[pltpu.VMEM((6, tm, LANES), F32)])


FFN_TN = D_FF // 2
MXU_COLS = 256


def _ffn_in_swiglu(h2, w_gate_t, w_up_t, comm=None):
    t = h2.shape[0]
    tm = 512

    def body(h_ref, wg_ref, wu_ref, a_ref, b_ref, f_ref):
        h = h_ref[...].astype(MXU_DTYPE)
        for c0 in range(0, FFN_TN, MXU_COLS):
            sl = slice(c0, min(c0 + MXU_COLS, FFN_TN))
            a = lax.dot_general(h, wg_ref[sl, :], _NT, preferred_element_type=F32)
            b = lax.dot_general(h, wu_ref[sl, :], _NT, preferred_element_type=F32)
            a_ref[:, sl] = a
            b_ref[:, sl] = b
            f_ref[:, sl] = (a * _sigmoid(a) * b).astype(BF16)

    tile = pl.BlockSpec((tm, FFN_TN), lambda j, i: (i, j))
    wspec = pl.BlockSpec((FFN_TN, D_MODEL), lambda j, i: (j, 0))
    return _grid_call(
        body, "ffn_in_swiglu", (D_FF // FFN_TN, t // tm), [h2, w_gate_t, w_up_t],
        [pl.BlockSpec((tm, D_MODEL), lambda j, i: (i, 0)), wspec, wspec],
        [tile] * 3, [jax.ShapeDtypeStruct((t, D_FF), F32)] * 2 + [jax.ShapeDtypeStruct((t, D_FF), BF16)], VMEM_BIG, comm)


def _ffn_down_final(f, w_down, x1, target, g):
    t = x1.shape[0]
    tm = 256

    def body(f_ref, w_ref, x1_ref, t_ref, g_ref, dx_ref, dxb_ref, loss_ref, gg_ref):
        @pl.when(pl.program_id(0) == 0)
        def _():
            loss_ref[...] = jnp.zeros_like(loss_ref)
            gg_ref[...] = jnp.zeros_like(gg_ref)

        xv = x1_ref[...] + jnp.dot(f_ref[...].astype(MXU_DTYPE), w_ref[...], preferred_element_type=F32)
        gv = g_ref[...]
        r = lax.rsqrt(jnp.mean(xv * xv, axis=-1, keepdims=True) + RMS_EPS)
        n = xv * r
        diff = n * gv - t_ref[...]
        per_tok = jnp.mean(diff * diff, axis=-1, keepdims=True)
        loss_ref[...] += 0.5 * jnp.sum(per_tok, axis=0, keepdims=True)
        dy = diff / xv.shape[-1]
        gg_ref[...] += jnp.sum(dy * n, axis=0, keepdims=True)
        dn = dy * gv
        dx = r * (dn - n * jnp.mean(dn * n, axis=-1, keepdims=True))
        dx_ref[...] = dx
        dxb_ref[...] = dx.astype(BF16)

    acc = lambda shp: pl.BlockSpec(shp, lambda i, j: (0, 0))
    return _grid_call(
        body, "ffn_down_final", (t // tm, 1), [f, w_down, x1, target, g],
        [_row_spec(tm, D_FF), _whole(w_down), _row_spec(tm, D_MODEL), _row_spec(tm, D_MODEL), _whole(g)],
        [_row_spec(tm, D_MODEL)] * 2 + [acc((8, LANES)), acc((1, D_MODEL))],
        [jax.ShapeDtypeStruct((t, D_MODEL), F32), jax.ShapeDtypeStruct((t, D_MODEL), BF16),
         jax.ShapeDtypeStruct((8, LANES), F32), jax.ShapeDtypeStruct((1, D_MODEL), F32)], VMEM_BIG, sequential=True)


def _d_f_swiglu_bwd(dx2b, w_down, a, b):
    t = a.shape[0]
    tm = 512

    def body(dx_ref, w_ref, a_ref, b_ref, da_ref, db_ref):
        d = lax.dot_general(dx_ref[...], w_ref[...], _NT, preferred_element_type=F32)
        av, bv = a_ref[...], b_ref[...]
        sg = _sigmoid(av)
        da_ref[...] = (d * bv * sg * (1.0 + av * (1.0 - sg))).astype(BF16)
        db_ref[...] = (d * av * sg).astype(BF16)

    tile = pl.BlockSpec((tm, FFN_TN), lambda j, i: (i, j))
    return _grid_call(
        body, "d_f_swiglu_bwd", (D_FF // FFN_TN, t // tm), [dx2b, w_down, a, b],
        [pl.BlockSpec((tm, D_MODEL), lambda j, i: (i, 0)), pl.BlockSpec((FFN_TN, D_MODEL), lambda j, i: (j, 0)), tile, tile],
        [tile] * 2, [jax.ShapeDtypeStruct((t, D_FF), BF16)] * 2, VMEM_BIG)


def _mm_rms_bwd(operands, weights, x, g, dres, name, comm=None):
    t = x.shape[0]
    tm = 256
    n_op = len(operands)

    def body(*refs):
        a_refs, w_refs = refs[:n_op], refs[n_op:2 * n_op]
        x_ref, g_ref, dres_ref, dx_ref, dxb_ref, gg_ref = refs[2 * n_op:]

        @pl.when(pl.program_id(0) == 0)
        def _():
            gg_ref[...] = jnp.zeros_like(gg_ref)

        dh = None
        for a_ref, w_ref in zip(a_refs, w_refs):
            part = jnp.dot(a_ref[...].astype(MXU_DTYPE), w_ref[...], preferred_element_type=F32)
            dh = part if dh is None else dh + part
        xv = x_ref[...]
        r = lax.rsqrt(jnp.mean(xv * xv, axis=-1, keepdims=True) + RMS_EPS)
        n = xv * r
        gg_ref[...] += jnp.sum(dh * n, axis=0, keepdims=True)
        dn = dh * g_ref[...]
        dx = dres_ref[...] + r * (dn - n * jnp.mean(dn * n, axis=-1, keepdims=True))
        dx_ref[...] = dx
        dxb_ref[...] = dx.astype(BF16)

    d = x.shape[1]
    return _grid_call(
        body, name, (t // tm, 1), [*operands, *weights, x, g, dres],
        [_row_spec(tm, a.shape[1]) for a in operands] + [_whole(wk) for wk in weights]
        + [_row_spec(tm, d), _whole(g), _row_spec(tm, d)],
        [_row_spec(tm, d)] * 2 + [pl.BlockSpec((1, d), lambda i, j: (0, 0))],
        [jax.ShapeDtypeStruct((t, d), F32), jax.ShapeDtypeStruct((t, d), BF16), jax.ShapeDtypeStruct((1, d), F32)],
        VMEM_BIG, comm, sequential=True)


def _flat_small(small):
    perm_b = lambda a: a.reshape(SSM_GROUPS, SSM_STATE, SSM_CH).transpose(2, 0, 1).reshape(SSM_CH, N_STATE)
    perm_c = lambda a: a.reshape(SSM_GROUPS, SSM_CH, SSM_STATE).transpose(1, 0, 2).reshape(SSM_CH, N_STATE)
    return dict(
        g_mix=small["norm_mix_g"].reshape(1, D_MODEL), g_ffn=small["norm_ffn_g"].reshape(1, D_MODEL),
        g_fin=small["norm_final_g"].reshape(1, D_MODEL),
        lr=small["ssm_a_re"].reshape(1, N_STATE), li=small["ssm_a_im"].reshape(1, N_STATE),
        ldt=jnp.repeat(small["ssm_log_dt"].reshape(SSM_GROUPS), SSM_STATE).reshape(1, N_STATE),
        br=perm_b(small["ssm_b_re"]), bi=perm_b(small["ssm_b_im"]),
        cr=perm_c(small["ssm_c_re"]), ci=perm_c(small["ssm_c_im"]), dskip=small["ssm_d"].reshape(1, SSM_W))


AG_HOSTS = {"proj_rope": ("w_glu", "w_attn_out", "w_out"), "attn_fwd_g0": ("w_ffn_gate",), "attn_fwd_g1": ("w_ffn_up",),
            "ffn_in_swiglu": ("w_ffn_down",)}
HALVED = ("w_ffn_gate", "w_ffn_up", "w_in")
A2A_HOSTS = {"d_h2_rms": ("w_ffn_down",), "attn_bwd_g0": ("w_ffn_gate:0",), "attn_bwd_g1": ("w_ffn_gate:1",),
             "mix_bwd": ("w_out",), "attn_bwd_g2": ("w_ffn_up:0",), "ssm_bwd": ("w_ffn_up:1", "w_attn_out", "w_glu"),
             "mm_g_in1": ("w_in:0",), "d_h0_rms": ("w_in:1",)}
SMALL_HOST = "mm_g_in0"


def _local_step(x, target, w, small, shards=None):
    t = x.shape[0]
    n_samples = t // SEQ
    n_rows = n_samples * SCAN_SEG_PER_SAMPLE
    tabs = _rope_tables()
    w = dict(w)
    fs = _flat_small(small)
    g_mix, g_ffn, g_fin, dskip = fs["g_mix"], fs["g_ffn"], fs["g_fin"], fs["dskip"]
    a_cat, bbc, cc = _ssm_disc(fs["lr"], fs["li"], fs["ldt"], fs["br"], fs["bi"], fs["cr"], fs["ci"])
    big, recv, small_pack = {}, {}, []

    def comm_of(name):
        if shards is None:
            return None
        if name == SMALL_HOST:
            return _ag_comm([(small_pack[0], 0, 0)], [(N_DEV, *small_pack[0].shape)])
        if name in AG_HOSTS:
            names = AG_HOSTS[name]
            return _ag_comm([(shards[n], j, 0) for j, n in enumerate(names)], [(N_DEV, *shards[n].shape) for n in names])
        if name in A2A_HOSTS:
            return _a2a_comm([(big[n].reshape(N_DEV, -1, big[n].shape[1]), 0) for n in A2A_HOSTS[name]])
        return None

    def absorb(name, carried):
        if name == SMALL_HOST:
            recv["small"] = carried[0]
        for n, a3 in zip(AG_HOSTS.get(name, ()), carried):
            w[n] = a3.reshape(-1, a3.shape[2])
        for n, a3 in zip(A2A_HOSTS.get(name, ()), carried):
            recv[n] = a3

    def mm(a, b, mode, name, tm, tn, **kw):
        comm = comm_of(name)
        if comm is None:
            return _mm(a, b, mode, name, tm, tn, **kw)
        out, *carried = _mm(a, b, mode, name, tm, tn, comm=comm, **kw)
        absorb(name, carried)
        return out

    h0, u, gates, *rest = _proj_rope(x, g_mix, w["w_in"], tabs, comm_of("proj_rope"))
    qkv = [rest[3 * g:3 * g + 3] for g in range(3)]
    absorb("proj_rope", rest[9:])
    os_, lses = [], []
    for g in range(3):
        o_g, l_g, carried = _attn_fwd(*qkv[g], g, n_samples, comm_of(f"attn_fwd_g{g}"))
        absorb(f"attn_fwd_g{g}", carried)
        os_.append(o_g)
        lses.append(l_g)
    u_perm = _to_scan_rows(u, n_samples)
    ytot, yg_perm, ein = _ssm_fwd(u_perm, a_cat, bbc, cc, dskip, n_rows)
    yg = _from_scan_rows(yg_perm, n_samples)

    attn, lse_tot, merged, x1, h2 = _mix_out_rms(os_, lses, yg, gates, x, w["w_attn_out"], w["w_glu"], w["w_out"], g_ffn)
    ffn_a, ffn_b, f, *carried = _ffn_in_swiglu(h2, w["w_ffn_gate"], w["w_ffn_up"], comm_of("ffn_in_swiglu"))
    absorb("ffn_in_swiglu", carried)
    dx2, dx2b, loss_blk, g_gfin = _ffn_down_final(f, w["w_ffn_down"], x1, target, g_fin)

    da, db = _d_f_swiglu_bwd(dx2b, w["w_ffn_down"], ffn_a, ffn_b)
    big["w_ffn_down"] = mm(f, dx2b, "tn", "mm_g_down", 256, D_MODEL, out_dtype=BF16)
    half = D_MODEL // 2
    for hf in range(2):
        big[f"w_ffn_gate:{hf}"] = mm(da, h2, "tn", f"mm_g_gate{hf}", 256, half, out_dtype=BF16, cols=(hf * half, half))
        big[f"w_ffn_up:{hf}"] = mm(db, h2, "tn", f"mm_g_up{hf}", 256, half, out_dtype=BF16, cols=(hf * half, half))
    dx1, dx1b, g_gffn, *carried = _mm_rms_bwd([da, db], [w["w_ffn_gate"], w["w_ffn_up"]], x1, g_ffn, dx2, "d_h2_rms",
                                              comm_of("d_h2_rms"))
    absorb("d_h2_rms", carried)

    big["w_out"] = mm(merged, dx1b, "tn", "mm_g_out", 256, D_MODEL, out_dtype=BF16)
    dattn_d, dz, dgpre, dattn, dyg, rowdot, *rest = _mix_bwd(dx1b, attn, lse_tot, yg, gates, w["w_attn_out"], w["w_glu"],
                                                             w["w_out"], comm_of("mix_bwd"))
    cot = [(dattn, lse_tot, rowdot), tuple(rest[:3]), tuple(rest[3:6])]
    absorb("mix_bwd", rest[6:])

    big["w_attn_out"] = mm(dattn_d, attn, "tn", "mm_g_attn_out", 512, GROUP_W, out_dtype=BF16)
    dqs, dks, dvs = [], [], []
    for g in range(3):
        dq_g, dk_g, dv_g, carried = _attn_bwd(*qkv[g], *cot[g], g, n_samples, comm_of(f"attn_bwd_g{g}"))
        absorb(f"attn_bwd_g{g}", carried)
        dqs.append(dq_g)
        dks.append(dk_g)
        dvs.append(dv_g)

    big["w_glu"] = mm(dz, yg, "tn", "mm_g_glu", 512, 512, out_dtype=BF16)
    dyg_perm = _to_scan_rows(dyg, n_samples)
    dypre, du_skip, g_dskip = _ssm_act_bwd(dyg_perm, ytot, u_perm, dskip)
    du_perm, da_cat, dbb_full, dc_full, *carried = _ssm_bwd(u_perm, dypre, du_skip, a_cat, bbc, cc, ein, n_rows,
                                                          comm_of("ssm_bwd"))
    absorb("ssm_bwd", carried)
    du = _from_scan_rows(du_perm, n_samples)
    g_lr, g_li, g_ldt, g_br, g_bi, g_cr, g_ci = _ssm_param_bwd(
        fs["lr"], fs["li"], fs["ldt"], fs["br"], fs["bi"], da_cat, dbb_full, dc_full)

    small_pack.append(_pack_small(dict(lr=g_lr, li=g_li, ldt=g_ldt, br=g_br, bi=g_bi, cr=g_cr, ci=g_ci, dskip=g_dskip,
                                       g_ffn=g_gffn, g_fin=g_gfin, loss=loss_blk)))

    dproj = _pack_dproj(dqs, dks, dvs, du, dgpre, tabs)
    for hf in range(2):
        big[f"w_in:{hf}"] = mm(dproj, h0, "tn", f"mm_g_in{hf}", 256, half, out_dtype=BF16, cols=(hf * half, half))
    grad_x, _, g_gmix, *carried = _mm_rms_bwd([dproj], [w["w_in"]], x, g_mix, dx1, "d_h0_rms", comm_of("d_h0_rms"))
    absorb("d_h0_rms", carried)
    return grad_x, (big if shards is None else recv), small_pack[0], g_gmix


_MESH = pl.DeviceIdType.MESH


def _all_gather(block, name):
    rows, lanes = block.shape

    def body(x_ref, out_ref, send_sems, recv_sems, local_sem):
        x, y, c = lax.axis_index("x"), lax.axis_index("y"), lax.axis_index("c")
        me, sibling = (x, y, c), (x, y, 1 - c)
        chips = [(1 - x, y), (x, 1 - y), (1 - x, 1 - y)]

        def slot(px, py, pc):
            return out_ref.at[4 * px + 2 * py + pc]

        def copy(k, blk, to, src=None):
            return pltpu.make_async_remote_copy(
                src_ref=slot(*blk) if src is None else src, dst_ref=slot(*blk), send_sem=send_sems.at[k],
                recv_sem=recv_sems.at[k], device_id=to, device_id_type=_MESH)

        mine = pltpu.make_async_copy(x_ref, slot(*me), local_sem)
        mine.start()
        first = [copy(0, me, sibling, src=x_ref)]
        first += [copy(1 + j, me, (*chip, c), src=x_ref) for j, chip in enumerate(chips)]
        for cp in first:
            cp.start()
        passed = [copy(4 + j, (*chip, c), sibling) for j, chip in enumerate(chips)]
        for j, chip in enumerate(chips):
            copy(1 + j, (*chip, c), me).wait_recv()
            passed[j].start()
        copy(0, sibling, me).wait_recv()
        for j, chip in enumerate(chips):
            copy(4 + j, (*chip, 1 - c), me).wait_recv()
        for cp in first + passed:
            cp.wait_send()
        mine.wait()

    return _pallas_call(
        body, name=name, out_shape=jax.ShapeDtypeStruct((N_DEV, rows, lanes), block.dtype),
        in_specs=[pl.BlockSpec(memory_space=pl.ANY)], out_specs=pl.BlockSpec(memory_space=pl.ANY),
        scratch_shapes=[pltpu.SemaphoreType.DMA((7,)), pltpu.SemaphoreType.DMA((7,)), pltpu.SemaphoreType.DMA],
    )(block)


def _ag_comm(items, bufs):
    def plan(in_refs, out_refs, send_sems, recv_sems, local_sems):
        x, y, c = lax.axis_index("x"), lax.axis_index("y"), lax.axis_index("c")
        me, sibling = (x, y, c), (x, y, 1 - c)
        chips = [(1 - x, y), (x, 1 - y), (1 - x, 1 - y)]
        plans = []
        for t, (_, buf, slot0) in enumerate(items):
            x_ref, out_ref = in_refs[t], out_refs[buf]

            def slot(px, py, pc, out_ref=out_ref, slot0=slot0):
                return out_ref.at[slot0 + 4 * px + 2 * py + pc]

            def copy(k, blk, to, src=None, t=t, slot=slot):
                return pltpu.make_async_remote_copy(
                    src_ref=slot(*blk) if src is None else src, dst_ref=slot(*blk), send_sem=send_sems.at[7 * t + k],
                    recv_sem=recv_sems.at[7 * t + k], device_id=to, device_id_type=_MESH)

            plans.append(dict(
                mine=pltpu.make_async_copy(x_ref, slot(*me), local_sems.at[t]),
                first=[copy(0, me, sibling, src=x_ref)] + [copy(1 + j, me, (*chip, c), src=x_ref)
                                                           for j, chip in enumerate(chips)],
                passed=[copy(4 + j, (*chip, c), sibling) for j, chip in enumerate(chips)],
                from_ici=[copy(1 + j, (*chip, c), me) for j, chip in enumerate(chips)],
                from_sibling=[copy(0, sibling, me)] + [copy(4 + j, (*chip, 1 - c), me) for j, chip in enumerate(chips)]))
        return plans

    def start(*refs):
        for p in plan(*refs):
            p["mine"].start()
            for cp in p["first"]:
                cp.start()

    def finish(*refs):
        plans = plan(*refs)
        for p in plans:
            for arrived, onward in zip(p["from_ici"], p["passed"]):
                arrived.wait_recv()
                onward.start()
        for p in plans:
            for arrived in p["from_sibling"]:
                arrived.wait_recv()
            for cp in p["first"] + p["passed"]:
                cp.wait_send()
            p["mine"].wait()

    dtype_of = {buf: shard.dtype for shard, buf, _ in items}
    out_shapes = [jax.ShapeDtypeStruct(b, dtype_of[j]) for j, b in enumerate(bufs)]
    return _Comm([it[0] for it in items], out_shapes, 7 * len(items), len(items), start, finish)


def _a2a_comm(items):
    def plan(in_refs, out_refs, send_sems, recv_sems, local_sems):
        x, y, c = lax.axis_index("x"), lax.axis_index("y"), lax.axis_index("c")
        my = 4 * x + 2 * y + c
        copies, locals_ = [], []
        for t, (_, slot0) in enumerate(items):
            s_ref, r_ref = in_refs[t], out_refs[t]
            locals_.append(pltpu.make_async_copy(s_ref.at[slot0 + my], r_ref.at[my], local_sems.at[t]))
            for kk in range(1, N_DEV):
                px = 1 - x if kk & 4 else x
                py = 1 - y if kk & 2 else y
                pc = 1 - c if kk & 1 else c
                copies.append(pltpu.make_async_remote_copy(
                    src_ref=s_ref.at[slot0 + 4 * px + 2 * py + pc], dst_ref=r_ref.at[my],
                    send_sem=send_sems.at[7 * t + kk - 1], recv_sem=recv_sems.at[7 * t + kk - 1],
                    device_id=(px, py, pc), device_id_type=_MESH))
        return copies, locals_

    def start(*refs):
        copies, locals_ = plan(*refs)
        for cp in locals_ + copies:
            cp.start()

    def finish(*refs):
        copies, locals_ = plan(*refs)
        for cp in copies + locals_:
            cp.wait()

    out_shapes = [jax.ShapeDtypeStruct((N_DEV,) + it[0].shape[1:], it[0].dtype) for it in items]
    return _Comm([it[0] for it in items], out_shapes, 7 * len(items), len(items), start, finish)


def _adam_math(g, w, m, v):
    m_new = ADAM_B1 * m + (1.0 - ADAM_B1) * g
    v_new = ADAM_B2 * v + (1.0 - ADAM_B2) * jnp.square(g)
    m_hat = m_new / (1.0 - ADAM_B1 ** ADAM_STEP)
    v_hat = v_new / (1.0 - ADAM_B2 ** ADAM_STEP)
    return -ADAM_LR * (m_hat / (jnp.sqrt(v_hat) + ADAM_EPS) + ADAM_WD * w), m_new, v_new


def _sum_partials(parts, name, tm):
    n, rows, _ = parts[0].shape
    widths = [p.shape[2] for p in parts]

    def body(*refs):
        g_ref, off = refs[-1], 0
        for p_ref, wd in zip(refs[:-1], widths):
            g = p_ref[0].astype(F32)
            for s in range(1, n):
                g = g + p_ref[s].astype(F32)
            g_ref[:, off:off + wd] = g
            off += wd

    return _pallas_call(
        body, name=name, grid=(rows // tm,), in_specs=[pl.BlockSpec((n, tm, wd), lambda i: (0, i, 0)) for wd in widths],
        out_specs=pl.BlockSpec((tm, sum(widths)), lambda i: (i, 0)),
        out_shape=jax.ShapeDtypeStruct((rows, sum(widths)), F32),
        compiler_params=pltpu.CompilerParams(dimension_semantics=("parallel",), vmem_limit_bytes=VMEM_MID),
    )(*parts)


def _adam(partials, w, m, v, name, tm):
    n, rows, cols = partials.shape

    def body(p_ref, w_ref, m_ref, v_ref, g_ref, d_ref, nm_ref, nv_ref):
        g = p_ref[0].astype(F32)
        for s in range(1, n):
            g = g + p_ref[s].astype(F32)
        g_ref[...] = g
        d_ref[...], nm_ref[...], nv_ref[...] = _adam_math(g, w_ref[...], m_ref[...], v_ref[...])

    assert rows % tm == 0
    row = pl.BlockSpec((tm, cols), lambda i: (i, 0))
    shp = jax.ShapeDtypeStruct((rows, cols), F32)
    return _pallas_call(
        body, name=name, grid=(rows // tm,),
        in_specs=[pl.BlockSpec((n, tm, cols), lambda i: (0, i, 0)), row, row, row],
        out_specs=[row] * 4, out_shape=[shp] * 4,
        compiler_params=pltpu.CompilerParams(dimension_semantics=("parallel",), vmem_limit_bytes=VMEM_MID),
    )(partials, w, m, v)


_PK_LR, _PK_LI, _PK_GAINS, _PK_MISC, _PK_BR, _PK_BI, _PK_CR, _PK_CI, _PK_ROWS = 0, 1, 2, 3, 8, 24, 40, 56, 72
_PK_LDT_LANE, _PK_LOSS_LANE = D_MODEL + SSM_W, D_MODEL + SSM_W + LANES


def _pack_small(sg):
    names = ("lr", "li", "g_ffn", "g_fin", "dskip", "ldt", "loss", "br", "bi", "cr", "ci")

    def body(lr, li, gffn, gfin, dskip, ldt, loss, br, bi, cr, ci, o_ref):
        o_ref[...] = jnp.zeros_like(o_ref)
        o_ref[_PK_LR:_PK_LR + 1, :] = lr[...]
        o_ref[_PK_LI:_PK_LI + 1, :] = li[...]
        o_ref[_PK_GAINS:_PK_GAINS + 1, D_MODEL:] = gffn[...]
        o_ref[_PK_MISC:_PK_MISC + 1, :D_MODEL] = gfin[...]
        o_ref[_PK_MISC:_PK_MISC + 1, D_MODEL:D_MODEL + SSM_W] = dskip[...]
        o_ref[_PK_MISC:_PK_MISC + 1, _PK_LDT_LANE:_PK_LDT_LANE + LANES] = ldt[0:1, :]
        o_ref[_PK_MISC:_PK_MISC + 1, _PK_LOSS_LANE:_PK_LOSS_LANE + LANES] = loss[0:1, :]
        o_ref[_PK_BR:_PK_BR + SSM_CH, :] = br[...]
        o_ref[_PK_BI:_PK_BI + SSM_CH, :] = bi[...]
        o_ref[_PK_CR:_PK_CR + SSM_CH, :] = cr[...]
        o_ref[_PK_CI:_PK_CI + SSM_CH, :] = ci[...]

    return _pallas_call(body, name="pack_small", out_shape=jax.ShapeDtypeStruct((_PK_ROWS, N_STATE), F32))(
        *[sg[n] for n in names])


def _unpack_small(s, g_mix):
    unflat_b = lambda a: a.reshape(SSM_CH, SSM_GROUPS, SSM_STATE).transpose(1, 2, 0)[None]
    unflat_c = lambda a: a.reshape(SSM_CH, SSM_GROUPS, SSM_STATE).transpose(1, 0, 2)[None]
    grads = {
        "norm_mix_g": g_mix, "norm_ffn_g": s[_PK_GAINS, D_MODEL:].reshape(1, D_MODEL),
        "norm_final_g": s[_PK_MISC, :D_MODEL],
        "ssm_a_re": s[_PK_LR].reshape(1, SSM_GROUPS, SSM_STATE), "ssm_a_im": s[_PK_LI].reshape(1, SSM_GROUPS, SSM_STATE),
        "ssm_log_dt": s[_PK_MISC, _PK_LDT_LANE:_PK_LDT_LANE + SSM_GROUPS].reshape(1, SSM_GROUPS),
        "ssm_d": s[_PK_MISC, D_MODEL:D_MODEL + SSM_W].reshape(1, SSM_GROUPS, SSM_CH),
        "ssm_b_re": unflat_b(s[_PK_BR:_PK_BR + SSM_CH]), "ssm_b_im": unflat_b(s[_PK_BI:_PK_BI + SSM_CH]),
        "ssm_c_re": unflat_c(s[_PK_CR:_PK_CR + SSM_CH]), "ssm_c_im": unflat_c(s[_PK_CI:_PK_CI + SSM_CH]),
    }
    return s[_PK_MISC, _PK_LOSS_LANE], grads


def _adam_small(grads, wts, moms, vars_):
    n = len(SMALL_WEIGHTS)
    as2d = lambda a: a.reshape(1, -1) if a.ndim == 1 else a

    def body(*refs):
        ins, outs = refs[:4 * n], refs[4 * n:]
        for i in range(n):
            g, w, m, v = (ins[j * n + i][...] for j in range(4))
            outs[i][...], outs[n + i][...], outs[2 * n + i][...] = _adam_math(g, w, m, v)

    operands = [as2d(d[k]) for d in (grads, wts, moms, vars_) for k in SMALL_WEIGHTS]
    shapes = [jax.ShapeDtypeStruct(as2d(wts[k]).shape, F32) for k in SMALL_WEIGHTS] * 3
    res = _pallas_call(body, name="adam_small", out_shape=shapes,
                         compiler_params=pltpu.CompilerParams(vmem_limit_bytes=VMEM_BIG))(*operands)
    out = {}
    for j, kind in enumerate(("delta", "new_m", "new_v")):
        for i, k in enumerate(SMALL_WEIGHTS):
            out[kind, k] = res[j * n + i].reshape(wts[k].shape)
    return out


def kernel(x, norm_mix_g, w_in, ssm_a_re, ssm_a_im, ssm_log_dt, ssm_b_re, ssm_b_im, ssm_c_re, ssm_c_im, ssm_d, w_glu, w_attn_out, w_out, norm_ffn_g, w_ffn_gate, w_ffn_up, w_ffn_down, norm_final_g, loss_target, m_norm_mix_g, m_w_in, m_ssm_a_re, m_ssm_a_im, m_ssm_log_dt, m_ssm_b_re, m_ssm_b_im, m_ssm_c_re, m_ssm_c_im, m_ssm_d, m_w_glu, m_w_attn_out, m_w_out, m_norm_ffn_g, m_w_ffn_gate, m_w_ffn_up, m_w_ffn_down, m_norm_final_g, v_norm_mix_g, v_w_in, v_ssm_a_re, v_ssm_a_im, v_ssm_log_dt, v_ssm_b_re, v_ssm_b_im, v_ssm_c_re, v_ssm_c_im, v_ssm_d, v_w_glu, v_w_attn_out, v_w_out, v_norm_ffn_g, v_w_ffn_gate, v_w_ffn_up, v_w_ffn_down, v_norm_final_g):
    args = dict(locals())
    wts = {n: args[n] for n in ALL_WEIGHTS}
    moms = {n: args["m_" + n] for n in ALL_WEIGHTS}
    vars_ = {n: args["v_" + n] for n in ALL_WEIGHTS}
    n_samples = x.shape[0]
    t = n_samples * SEQ

    shards = {n: (wts[n][0] if n in ROW_SHARDED else wts[n][0].T).astype(BF16) for n in BIG_WEIGHTS}
    w_in_t = _all_gather(shards["w_in"], "allgather_w_in").reshape(IN_W, D_MODEL)

    small = {n: wts[n] for n in SMALL_WEIGHTS}
    grad_x, recv, _, g_mix_part = _local_step(x.reshape(t, D_MODEL), loss_target.reshape(t, D_MODEL), {"w_in": w_in_t},
                                              small, shards)

    results = {}
    for n in BIG_WEIGHTS:
        c, k = shards[n].shape
        w2, m2, v2 = wts[n][0], moms[n][0], vars_[n][0]
        if n in ROW_SHARDED:
            res = _adam(recv[n], w2, m2, v2, "adam_" + n, c // 2)
        else:
            parts = [recv[f"{n}:{hf}"] for hf in range(2)] if n in HALVED else [recv[n]]
            g_t = _sum_partials(parts, "sum_" + n, c // 2)
            res = _adam(g_t.T[None], w2, m2, v2, "adam_" + n, k // 2)
        for kind, a in zip(("grad", "delta", "new_m", "new_v"), res):
            results[kind, n] = a[None]

    g_mix_all = _all_gather(jnp.pad(g_mix_part, ((0, 7), (0, 0))), "allgather_g_mix")
    g_mix = _sum_partials([g_mix_all], "sum_g_mix", 8)[0:1]
    loss, sgrads = _unpack_small(_sum_partials([recv["small"]], "sum_small", _PK_ROWS), g_mix)
    for n in SMALL_WEIGHTS:
        results["grad", n] = sgrads[n]
    results.update(_adam_small(sgrads, wts, moms, vars_))
    outs = [loss, grad_x.reshape(x.shape)]
    for kind in ("grad", "delta", "new_m", "new_v"):
        outs += [results[kind, n] for n in ALL_WEIGHTS]
    return tuple(outs)
```

```python
import functools
import math

import jax
import jax.numpy as jnp
from jax import lax
from jax.experimental import pallas as pl
from jax.experimental.pallas import tpu as pltpu

F32 = jnp.float32
BF16 = jnp.bfloat16
MXU_DTYPE = jnp.bfloat16

N_DEV = 8
D_MODEL = 1024
SEQ = 2048
HEAD_DIM = 64
HEADS_PER_GROUP = 4
GROUP_W = HEADS_PER_GROUP * HEAD_DIM
DILATIONS = (1, 4, 16)
QKV_W = 3 * len(DILATIONS) * GROUP_W
Q_W = len(DILATIONS) * GROUP_W
ATT_BLOCK = 128
ROPE_DIM = 16
ROPE_THETA = 500000.0
SSM_W = 512
SSM_GROUPS = 32
SSM_CH = 16
SSM_STATE = 64
N_STATE = SSM_GROUPS * SSM_STATE
D_FF = 2816
IN_W = QKV_W + SSM_W + 2 * D_MODEL
RMS_EPS = 1e-6
NEG_INF = -1e30
LANES = 128

SCAN_SEG_PER_SAMPLE = 8
SCAN_LEN = SEQ // SCAN_SEG_PER_SAMPLE
SCAN_WC = 512
SCAN_NBLK = N_STATE // SCAN_WC
SCAN_CH = SSM_W // SCAN_NBLK
SCAN_CHUNK = 32

ADAM_LR = 0.001
ADAM_B1 = 0.9
ADAM_B2 = 0.999
ADAM_EPS = 1e-08
ADAM_WD = 0.01
ADAM_STEP = 10

VMEM_BIG = 48 * 1024 * 1024
VMEM_MID = 32 * 1024 * 1024

BIG_WEIGHTS = ("w_in", "w_glu", "w_attn_out", "w_out", "w_ffn_gate", "w_ffn_up", "w_ffn_down")
ROW_SHARDED = ("w_out", "w_ffn_down")
SMALL_WEIGHTS = ("norm_mix_g", "ssm_a_re", "ssm_a_im", "ssm_log_dt", "ssm_b_re", "ssm_b_im", "ssm_c_re", "ssm_c_im",
                 "ssm_d", "norm_ffn_g", "norm_final_g")
ALL_WEIGHTS = ("norm_mix_g", "w_in", "ssm_a_re", "ssm_a_im", "ssm_log_dt", "ssm_b_re", "ssm_b_im", "ssm_c_re", "ssm_c_im",
               "ssm_d", "w_glu", "w_attn_out", "w_out", "norm_ffn_g", "w_ffn_gate", "w_ffn_up", "w_ffn_down", "norm_final_g")


def _sigmoid(x):
    return 1.0 / (1.0 + jnp.exp(-x))


def _pallas_call(body, *, out_shape, **kw):
    single = not isinstance(out_shape, (list, tuple))
    shapes = [pltpu.HBM(s.shape, s.dtype) for s in ([out_shape] if single else out_shape)]
    call = pl.pallas_call(body, out_shape=shapes[0] if single else shapes, **kw)
    return lambda *operands: call(*[pltpu.with_memory_space_constraint(o, pltpu.HBM) for o in operands])


class _Comm:
    def __init__(self, ins, out_shapes, n_sem, n_local, start, finish):
        self.ins, self.out_shapes, self.n_sem, self.n_local = ins, out_shapes, n_sem, n_local
        self.start, self.finish = start, finish


def _mm(a, b, mode, name, tm, tn, out_dtype=F32, add=None, vmem=VMEM_BIG, comm=None, cols=None):
    if mode == "nn":
        (m, k), (_, n) = a.shape, b.shape
        a_spec = pl.BlockSpec((tm, k), lambda i, j: (i, 0))
        b_spec = pl.BlockSpec((k, tn), lambda i, j: (0, j))
        dims = (((1,), (0,)), ((), ()))
    elif mode == "nt":
        (m, k), (n, _) = a.shape, b.shape
        a_spec = pl.BlockSpec((tm, k), lambda i, j: (i, 0))
        b_spec = pl.BlockSpec((tn, k), lambda i, j: (j, 0))
        dims = (((1,), (1,)), ((), ()))
    else:
        (k, m), (_, n) = a.shape, b.shape
        first, n = cols if cols else (0, n)
        a_spec = pl.BlockSpec((k, tm), lambda i, j: (0, i))
        b_spec = pl.BlockSpec((k, tn), lambda i, j: (0, j + first // tn))
        dims = (((0,), (0,)), ((), ()))
    assert m % tm == 0 and n % tn == 0, (name, m, n, tm, tn)
    o_spec = pl.BlockSpec((tm, tn), lambda i, j: (i, j))
    has_add = add is not None

    def body(*refs):
        a_ref, b_ref, o_ref = refs[0], refs[1], refs[-1]
        acc = lax.dot_general(a_ref[...].astype(MXU_DTYPE), b_ref[...].astype(MXU_DTYPE), dims,
                              preferred_element_type=F32)
        if has_add:
            acc = acc + refs[2][...]
        o_ref[...] = acc.astype(out_dtype)

    ins = [a, b] + ([add] if has_add else [])
    in_specs = [a_spec, b_spec] + ([o_spec] if has_add else [])
    return _grid_call(body, name, (m // tm, n // tn), ins, in_specs, [o_spec],
                      [jax.ShapeDtypeStruct((m, n), out_dtype)], vmem, comm)


def _grid_call(body, name, grid, ins, in_specs, out_specs, out_shapes, vmem, comm=None, sequential=False, scratch=()):
    if comm is None:
        single = len(out_shapes) == 1
        semantics = ("arbitrary", "arbitrary") if sequential else ("parallel", "parallel")
        return _pallas_call(
            body, name=name, grid=grid, in_specs=in_specs, out_specs=out_specs[0] if single else out_specs,
            out_shape=out_shapes[0] if single else out_shapes, scratch_shapes=list(scratch),
            compiler_params=pltpu.CompilerParams(dimension_semantics=semantics, vmem_limit_bytes=vmem),
        )(*ins)
    n_in, n_out, n_cin, n_cout = len(ins), len(out_shapes), len(comm.ins), len(comm.out_shapes)
    n_io = n_in + n_cin + n_out + n_cout

    def carrying(*refs):
        own = refs[:n_in] + refs[n_in + n_cin:n_in + n_cin + n_out] + refs[n_io:len(refs) - 3]
        c_args = (refs[n_in:n_in + n_cin], refs[n_in + n_cin + n_out:n_io], *refs[-3:])

        @pl.when((pl.program_id(0) == 0) & (pl.program_id(1) == 0))
        def _():
            comm.start(*c_args)

        body(*own)

        @pl.when((pl.program_id(0) == grid[0] - 1) & (pl.program_id(1) == grid[1] - 1))
        def _():
            comm.finish(*c_args)

    hbm = pl.BlockSpec(memory_space=pl.ANY)
    return _pallas_call(
        carrying, name=name, grid=grid, in_specs=list(in_specs) + [hbm] * n_cin,
        out_specs=list(out_specs) + [hbm] * n_cout, out_shape=list(out_shapes) + list(comm.out_shapes),
        scratch_shapes=list(scratch) + [pltpu.SemaphoreType.DMA((comm.n_sem,)), pltpu.SemaphoreType.DMA((comm.n_sem,)),
                                        pltpu.SemaphoreType.DMA((comm.n_local,))],
        compiler_params=pltpu.CompilerParams(dimension_semantics=("arbitrary", "arbitrary"), vmem_limit_bytes=vmem),
    )(*ins, *comm.ins)


def _rows(body, name, n_rows, tm, ins, outs, vmem=VMEM_MID, scratch=()):
    assert n_rows % tm == 0
    arrays, in_specs = [], []
    for kind, arr in ins:
        arrays.append(arr)
        if kind == "row":
            assert n_rows % arr.shape[0] == 0, (name, arr.shape)
            in_specs.append(pl.BlockSpec((tm * arr.shape[0] // n_rows, arr.shape[1]), lambda i: (i, 0)))
        elif kind == "tab":
            nblk = arr.shape[0] // tm
            in_specs.append(pl.BlockSpec((tm, arr.shape[1]), lambda i, nblk=nblk: (i % nblk, 0)))
        else:
            in_specs.append(pl.BlockSpec(arr.shape, lambda i, nd=arr.ndim: (0,) * nd))
    out_specs, out_shape = [], []
    for kind, shp, dt in outs:
        if kind == "row":
            out_specs.append(pl.BlockSpec((tm, shp), lambda i: (i, 0)))
            out_shape.append(jax.ShapeDtypeStruct((n_rows, shp), dt))
        elif kind == "dil":
            d, wd = shp
            out_specs.append(pl.BlockSpec((tm // d, d * wd), lambda i: (i, 0)))
            out_shape.append(jax.ShapeDtypeStruct((n_rows // d, d * wd), dt))
        else:
            out_specs.append(pl.BlockSpec(shp, lambda i, nd=len(shp): (0,) * nd))
            out_shape.append(jax.ShapeDtypeStruct(shp, dt))
    res = _pallas_call(
        body, name=name, grid=(n_rows // tm,), in_specs=in_specs, out_specs=out_specs, out_shape=out_shape,
        scratch_shapes=list(scratch),
        compiler_params=pltpu.CompilerParams(dimension_semantics=("arbitrary",), vmem_limit_bytes=vmem),
    )(*arrays)
    return res


def _gather_residue(stage, ch, r, d, n):
    return stage[ch, pl.ds(r, n, stride=d), :] if d > 1 else stage[ch]


def _scatter_residue(stage, ch, r, d, n, val):
    if d > 1:
        stage[ch, pl.ds(r, n, stride=d), :] = val
    else:
        stage[ch] = val


def _lane_chunk(ch):
    return slice(ch * LANES, (ch + 1) * LANES)


def _first_step():
    return pl.program_id(0) == 0


def _rope_tables():
    half = ROPE_DIM // 2
    inv = jnp.power(jnp.float32(ROPE_THETA), -jnp.arange(half, dtype=F32) * 2.0 / ROPE_DIM)
    ang = jnp.arange(SEQ, dtype=F32)[:, None] * inv[None, :]
    lane = jnp.arange(LANES) % HEAD_DIM
    cosl = jnp.cos(ang)[:, lane % half]
    sinl = jnp.sin(ang)[:, lane % half]
    tab_c = jnp.where(lane < ROPE_DIM, cosl, 1.0)
    tab_lo = jnp.where(lane < half, -sinl, 0.0)
    tab_hi = jnp.where((lane >= half) & (lane < ROPE_DIM), sinl, 0.0)
    return tab_c.astype(F32), tab_lo.astype(F32), tab_hi.astype(F32)


def _rope_apply(t, tc, tlo, thi):
    half = ROPE_DIM // 2
    return t * tc + pltpu.roll(t, LANES - half, 1) * tlo + pltpu.roll(t, half, 1) * thi


def _rope_transpose(dt, tc, tlo, thi):
    half = ROPE_DIM // 2
    return dt * tc + pltpu.roll(dt * tlo, half, 1) + pltpu.roll(dt * thi, LANES - half, 1)


def _pack_dproj(dqs, dks, dvs, du, dgpre, tabs):
    tm = 256

    def body(*refs):
        dq_refs, dk_refs, dv_refs = refs[0:3], refs[3:6], refs[6:9]
        du_ref, dg_ref, tc_ref, tlo_ref, thi_ref, o_ref, stage = refs[9:16]
        n_ch = QKV_W // LANES
        halves = GROUP_W // LANES
        for grp, d in enumerate(DILATIONS):
            for which, src in enumerate((dq_refs[grp], dk_refs[grp], dv_refs[grp])):
                for res in range(d):
                    for half in range(halves):
                        _scatter_residue(stage, which * (n_ch // 3) + grp * halves + half, res, d, tm // d,
                                         src[:, _lane_chunk(res * halves + half)])
        tc, tlo, thi = tc_ref[...], tlo_ref[...], thi_ref[...]
        for ch in range(n_ch):
            piece = stage[ch]
            o_ref[:, _lane_chunk(ch)] = (_rope_transpose(piece, tc, tlo, thi) if ch < 2 * n_ch // 3 else piece).astype(BF16)
        o_ref[:, QKV_W:QKV_W + SSM_W] = du_ref[...].astype(BF16)
        o_ref[:, QKV_W + SSM_W:] = dg_ref[...].astype(BF16)

    t = du.shape[0]
    ins = [("row", a) for a in (*dqs, *dks, *dvs, du, dgpre)] + [("tab", tb) for tb in tabs]
    return _rows(body, "pack_dproj", t, tm, ins, [("row", IN_W, BF16)],
                 scratch=[pltpu.VMEM((QKV_W // LANES, tm, LANES), F32)])[0]


def _merge_groups(o_refs, l_refs, a_ref, lt_ref, nat, tm):
    halves = GROUP_W // LANES
    for grp, d in enumerate(DILATIONS[1:], start=1):
        for j, src in enumerate((o_refs[grp], l_refs[grp])):
            for res in range(d):
                for half in range(halves):
                    _scatter_residue(nat, (grp - 1) * 4 + j * 2 + half, res, d, tm // d,
                                     src[:, _lane_chunk(res * halves + half)])
    for half in range(halves):
        sl = _lane_chunk(half)
        la, lb, lc = l_refs[0][:, sl], nat[2 + half], nat[6 + half]
        m = jnp.maximum(jnp.maximum(la, lb), lc)
        ea, eb, ec = jnp.exp(la - m), jnp.exp(lb - m), jnp.exp(lc - m)
        ssum = ea + eb + ec
        a_ref[:, sl] = (ea / ssum) * o_refs[0][:, sl] + (eb / ssum) * nat[half] + (ec / ssum) * nat[4 + half]
        lt_ref[:, sl] = m + jnp.log(ssum)


def _head_sum_matrix():
    r = jnp.arange(GROUP_W) // HEAD_DIM
    return (r[:, None] == r[None, :]).astype(F32)


def _attention_cotangents(da, attn, lt, ones, rd_ref, dil, stage, tm):
    halves = GROUP_W // LANES
    rd = jnp.dot(da * attn, ones, preferred_element_type=F32, precision=lax.Precision.HIGHEST)
    rd_ref[...] = rd
    for half in range(halves):
        for j, val in enumerate((da, lt, rd)):
            stage[2 * j + half] = val[:, _lane_chunk(half)]
    for grp, d in enumerate(DILATIONS[1:], start=1):
        for j in range(3):
            for res in range(d):
                for half in range(halves):
                    dil[3 * (grp - 1) + j][:, _lane_chunk(res * halves + half)] = _gather_residue(
                        stage, 2 * j + half, res, d, tm // d)


_GELU_C = math.sqrt(2.0 / math.pi)


def _ssm_act_bwd(dyg, ytot, u_perm, dskip):
    def body(dyg_ref, yt_ref, u_ref, d_ref, dy_ref, dus_ref, dd_ref):
        @pl.when(_first_step())
        def _():
            dd_ref[...] = jnp.zeros_like(dd_ref)

        yt = yt_ref[...]
        th = jnp.tanh(_GELU_C * (yt + 0.044715 * (yt * yt * yt)))
        dgelu = 0.5 * (1.0 + th) + 0.5 * yt * (1.0 - th * th) * _GELU_C * (1.0 + 3.0 * 0.044715 * yt * yt)
        dy = dyg_ref[...] * dgelu
        dy_ref[...] = dy.astype(BF16)
        dus_ref[...] = dy * d_ref[...]
        dd_ref[...] += jnp.sum(dy * u_ref[...], axis=0, keepdims=True)

    t = dyg.shape[0]
    return _rows(body, "ssm_act_bwd", t, 512, [("row", dyg), ("row", ytot), ("row", u_perm), ("const", dskip)],
                 [("row", SSM_W, BF16), ("row", SSM_W, F32), ("acc", (1, SSM_W), F32)])


def _head_masks():
    lane = lax.broadcasted_iota(jnp.int32, (1, GROUP_W), 1)
    return [(lane // HEAD_DIM) == h for h in range(HEADS_PER_GROUP)]


def _stack_heads(blk, masks, fill=0.0):
    return jnp.concatenate([jnp.where(mk, blk, jnp.full_like(blk, fill)) for mk in masks], axis=0)


def _unstack_heads(stacked, masks):
    rows = stacked.shape[0] // len(masks)
    out = stacked[:rows]
    for h in range(1, len(masks)):
        out = jnp.where(masks[h], stacked[h * rows:(h + 1) * rows], out)
    return out


def _band_mask(first):
    nk = ATT_BLOCK if first else 2 * ATT_BLOCK
    qi = lax.broadcasted_iota(jnp.int32, (ATT_BLOCK, nk), 0)
    ki = lax.broadcasted_iota(jnp.int32, (ATT_BLOCK, nk), 1)
    dist = qi - ki + (0 if first else ATT_BLOCK)
    return (dist >= 0) & (dist <= ATT_BLOCK)


_NT = (((1,), (1,)), ((), ()))
_TN = (((0,), (0,)), ((), ()))


def _attn_fwd(q, k, v, group, n_samples, comm=None):
    d = DILATIONS[group]
    length = SEQ // d
    nb = length // ATT_BLOCK

    def body(q_ref, k_ref, v_ref, o_ref, l_ref):
        masks = _head_masks()

        def block(qs, ks, first):
            nk = ATT_BLOCK if first else 2 * ATT_BLOCK
            qb = q_ref[0, pl.ds(qs, ATT_BLOCK), :]
            kc = k_ref[0, pl.ds(ks, nk), :]
            vc = v_ref[0, pl.ds(ks, nk), :]
            q4 = _stack_heads(qb, masks)
            valid = jnp.tile(_band_mask(first), (HEADS_PER_GROUP, 1))
            s = lax.dot_general(q4, kc, _NT, preferred_element_type=F32) * (HEAD_DIM ** -0.5)
            s = jnp.where(valid, s, NEG_INF)
            m = jnp.max(s, axis=-1, keepdims=True)
            p = jnp.exp(s - m)
            l = jnp.sum(p, axis=-1, keepdims=True)
            o4 = jnp.dot(p.astype(MXU_DTYPE), vc, preferred_element_type=F32) / l
            lse4 = jnp.broadcast_to(m + jnp.log(l), o4.shape)
            o_ref[0, pl.ds(qs, ATT_BLOCK), :] = _unstack_heads(o4, masks)
            l_ref[0, pl.ds(qs, ATT_BLOCK), :] = _unstack_heads(lse4, masks)

        block(0, 0, True)
        if nb > 1:
            def loop(n, carry):
                block(pl.multiple_of(n * ATT_BLOCK, ATT_BLOCK), pl.multiple_of((n - 1) * ATT_BLOCK, ATT_BLOCK), False)
                return carry

            lax.fori_loop(1, nb, loop, 0)

    per_sample = lambda a: a.reshape(n_samples, length, d * GROUP_W)
    spec = pl.BlockSpec((1, length, GROUP_W), lambda b, r: (b, 0, r))
    shp = jax.ShapeDtypeStruct((n_samples, length, d * GROUP_W), F32)
    o, lse, *carried = _grid_call(body, f"attn_fwd_g{group}", (n_samples, d), [per_sample(a) for a in (q, k, v)],
                                  [spec] * 3, [spec] * 2, [shp, shp], VMEM_MID, comm)
    flat = lambda a: a.reshape(n_samples * length, d * GROUP_W)
    return flat(o), flat(lse), carried


def _attn_bwd(q, k, v, dattn, lse_tot, rowdot, group, n_samples, comm=None):
    d = DILATIONS[group]
    length = SEQ // d
    nb = length // ATT_BLOCK

    def body(q_ref, k_ref, v_ref, da_ref, lt_ref, rd_ref, dq_ref, dk_ref, dv_ref):
        masks = _head_masks()
        dk_ref[...] = jnp.zeros_like(dk_ref)
        dv_ref[...] = jnp.zeros_like(dv_ref)

        def block(qs, ks, first):
            nk = ATT_BLOCK if first else 2 * ATT_BLOCK
            qb = q_ref[0, pl.ds(qs, ATT_BLOCK), :]
            kc = k_ref[0, pl.ds(ks, nk), :]
            vc = v_ref[0, pl.ds(ks, nk), :]
            da = da_ref[0, pl.ds(qs, ATT_BLOCK), :]
            lt = lt_ref[0, pl.ds(qs, ATT_BLOCK), :]
            rd = rd_ref[0, pl.ds(qs, ATT_BLOCK), :]
            q4 = _stack_heads(qb, masks)
            da4 = _stack_heads(da, masks).astype(MXU_DTYPE)
            lt4 = jnp.max(_stack_heads(lt, masks, -jnp.inf), axis=-1, keepdims=True)
            rd4 = jnp.max(_stack_heads(rd, masks, -jnp.inf), axis=-1, keepdims=True)
            valid = jnp.tile(_band_mask(first), (HEADS_PER_GROUP, 1))
            s = lax.dot_general(q4, kc, _NT, preferred_element_type=F32) * (HEAD_DIM ** -0.5)
            s = jnp.where(valid, s, NEG_INF)
            p = jnp.exp(s - lt4)
            dp = lax.dot_general(da4, vc, _NT, preferred_element_type=F32)
            ds = (p * (dp - rd4) * (HEAD_DIM ** -0.5)).astype(MXU_DTYPE)
            dq_ref[0, pl.ds(qs, ATT_BLOCK), :] = _unstack_heads(jnp.dot(ds, kc, preferred_element_type=F32), masks)
            dk_ref[0, pl.ds(ks, nk), :] += lax.dot_general(ds, q4, _TN, preferred_element_type=F32)
            dv_ref[0, pl.ds(ks, nk), :] += lax.dot_general(p.astype(MXU_DTYPE), da4, _TN, preferred_element_type=F32)

        block(0, 0, True)
        if nb > 1:
            def loop(n, carry):
                block(pl.multiple_of(n * ATT_BLOCK, ATT_BLOCK), pl.multiple_of((n - 1) * ATT_BLOCK, ATT_BLOCK), False)
                return carry

            lax.fori_loop(1, nb, loop, 0)

    per_sample = lambda a: a.reshape(n_samples, length, d * GROUP_W)
    spec = pl.BlockSpec((1, length, GROUP_W), lambda b, r: (b, 0, r))
    shp = jax.ShapeDtypeStruct((n_samples, length, d * GROUP_W), F32)
    dq, dk, dv, *carried = _grid_call(
        body, f"attn_bwd_g{group}", (n_samples, d), [per_sample(a) for a in (q, k, v, dattn, lse_tot, rowdot)],
        [spec] * 6, [spec] * 3, [shp, shp, shp], VMEM_MID, comm)
    flat = lambda a: a.reshape(n_samples * length, d * GROUP_W)
    return flat(dq), flat(dk), flat(dv), carried


def _disc(lr, li, ldt, br, bi):
    dt = jnp.exp(ldt)
    mag = jnp.exp(lr * dt)
    ab_re, ab_im = mag * jnp.cos(li * dt), mag * jnp.sin(li * dt)
    den = lr * lr + li * li
    nr, ni = ab_re - 1.0, ab_im
    f_re = (nr * lr + ni * li) / den
    f_im = (ni * lr - nr * li) / den
    return ab_re, ab_im, f_re * br - f_im * bi, f_re * bi + f_im * br


def _state_mask():
    row_g = lax.broadcasted_iota(jnp.int32, (SCAN_CH, SCAN_WC), 0) // SSM_CH
    col_g = lax.broadcasted_iota(jnp.int32, (SCAN_CH, SCAN_WC), 1) // SSM_STATE
    return row_g == col_g


def _ssm_disc(lr, li, ldt, br, bi, cr, ci):
    w = SCAN_WC

    def body(lr_ref, li_ref, ldt_ref, br_ref, bi_ref, cr_ref, ci_ref, a_ref, bb_ref, c_ref):
        ar, ai, bbr, bbi = _disc(lr_ref[...], li_ref[...], ldt_ref[...], br_ref[...], bi_ref[...])
        crv, civ = cr_ref[...], ci_ref[...]
        mask = _state_mask()
        for cb in range(SCAN_NBLK):
            sl = slice(cb * w, (cb + 1) * w)
            rows = slice(cb * SCAN_CH, (cb + 1) * SCAN_CH)
            dense = lambda comp: jnp.where(mask, jnp.tile(comp[:, sl], (SCAN_CH // SSM_CH, 1)), 0.0)
            a_ref[:, 2 * cb * w:(2 * cb + 1) * w] = ar[:, sl]
            a_ref[:, (2 * cb + 1) * w:(2 * cb + 2) * w] = ai[:, sl]
            bb_ref[rows, :w] = dense(bbr).astype(MXU_DTYPE)
            bb_ref[rows, w:] = dense(bbi).astype(MXU_DTYPE)
            c_ref[rows, :w] = dense(crv).astype(MXU_DTYPE)
            c_ref[rows, w:] = (-dense(civ)).astype(MXU_DTYPE)

    return _pallas_call(
        body, name="ssm_disc",
        out_shape=[jax.ShapeDtypeStruct((1, 2 * N_STATE), F32), jax.ShapeDtypeStruct((SSM_W, 2 * w), MXU_DTYPE),
                   jax.ShapeDtypeStruct((SSM_W, 2 * w), MXU_DTYPE)],
        compiler_params=pltpu.CompilerParams(vmem_limit_bytes=VMEM_MID),
    )(lr, li, ldt, br, bi, cr, ci)


def _group_indicator():
    s = jnp.arange(N_STATE) // SSM_STATE
    return (s[:, None] == jnp.arange(LANES)[None, :]).astype(F32)


def _ssm_param_bwd(lr, li, ldt, br, bi, da_cat, dbb_full, dc_full):
    w = SCAN_WC

    def body(lr_ref, li_ref, ldt_ref, br_ref, bi_ref, da_ref, dbb_ref, dc_ref, ind_ref,
             glr_ref, gli_ref, gldt_ref, gbr_ref, gbi_ref, gcr_ref, gci_ref):
        mask = _state_mask()

        def diag_parts(ref):
            res = ([], [])
            for cb in range(SCAN_NBLK):
                for part in range(2):
                    blk = ref[cb * SCAN_CH:(cb + 1) * SCAN_CH, part * w:(part + 1) * w]
                    res[part].append(jnp.sum(jnp.where(mask, blk, 0.0).reshape(SCAN_CH // SSM_CH, SSM_CH, w), axis=0))
            return jnp.concatenate(res[0], axis=1), jnp.concatenate(res[1], axis=1)

        dar = jnp.concatenate([da_ref[:, 2 * cb * w:(2 * cb + 1) * w] for cb in range(SCAN_NBLK)], axis=1)
        dai = jnp.concatenate([da_ref[:, (2 * cb + 1) * w:(2 * cb + 2) * w] for cb in range(SCAN_NBLK)], axis=1)
        dbbr, dbbi = diag_parts(dbb_ref)
        dcr, dci_neg = diag_parts(dc_ref)
        gcr_ref[...] = dcr
        gci_ref[...] = -dci_neg
        _, vjp = jax.vjp(_disc, lr_ref[...], li_ref[...], ldt_ref[...], br_ref[...], bi_ref[...])
        glr, gli, gldt, gbr, gbi = vjp((dar, dai, dbbr, dbbi))
        glr_ref[...] = glr
        gli_ref[...] = gli
        gldt_ref[...] = jnp.dot(jnp.broadcast_to(gldt, (8, N_STATE)), ind_ref[...], preferred_element_type=F32,
                                precision=lax.Precision.HIGHEST)
        gbr_ref[...] = gbr
        gbi_ref[...] = gbi

    v1 = jax.ShapeDtypeStruct((1, N_STATE), F32)
    v16 = jax.ShapeDtypeStruct((SSM_CH, N_STATE), F32)
    vdt = jax.ShapeDtypeStruct((8, LANES), F32)
    return _pallas_call(
        body, name="ssm_param_bwd", out_shape=[v1, v1, vdt, v16, v16, v16, v16],
        compiler_params=pltpu.CompilerParams(vmem_limit_bytes=VMEM_BIG),
    )(lr, li, ldt, br, bi, da_cat, dbb_full, dc_full, _group_indicator())


def _cmul(ar, ai, br, bi):
    return ar * br - ai * bi, ar * bi + ai * br


def _gelu_tanh(y):
    return jnp.tanh(_GELU_C * (y + 0.044715 * (y * y * y)))


def _segment_carry(er, ei, ar, ai, n_rows, reverse):
    qr, qi = ar, ai
    for _ in range(int(math.log2(SCAN_LEN))):
        qr, qi = _cmul(qr, qi, qr, qi)
    seg = lax.broadcasted_iota(jnp.int32, er.shape, 0) % SCAN_SEG_PER_SAMPLE
    shift = 1
    while shift < SCAN_SEG_PER_SAMPLE:
        keep = (seg < SCAN_SEG_PER_SAMPLE - shift) if reverse else (seg >= shift)
        amount = n_rows - shift if reverse else shift
        sr = jnp.where(keep, pltpu.roll(er, amount, 0), 0.0)
        si = jnp.where(keep, pltpu.roll(ei, amount, 0), 0.0)
        if reverse:
            er, ei = er + qr * sr + qi * si, ei + qr * si - qi * sr
        else:
            er, ei = er + qr * sr - qi * si, ei + qr * si + qi * sr
        qr, qi = _cmul(qr, qi, qr, qi)
        shift *= 2
    keep = (seg < SCAN_SEG_PER_SAMPLE - 1) if reverse else (seg >= 1)
    amount = n_rows - 1 if reverse else 1
    return jnp.where(keep, pltpu.roll(er, amount, 0), 0.0), jnp.where(keep, pltpu.roll(ei, amount, 0), 0.0)


def _ssm_fwd(u_perm, a_cat, bbc, cc, dskip, n_rows):
    t = u_perm.shape[0]
    w = SCAN_WC
    rows_c = SCAN_CHUNK * n_rows
    n_chunks = t // rows_c

    assert n_chunks % 2 == 0

    def body(u_ref, a_ref, bb_ref, c_ref, d_ref, yt_ref, yg_ref, ein_ref, bu_all, st_a, st_b, xs_a, xs_b):
        ar = jnp.broadcast_to(a_ref[:, :w], (n_rows, w))
        ai = jnp.broadcast_to(a_ref[:, w:], (n_rows, w))
        start = lambda ch: pl.multiple_of(ch * rows_c, rows_c)

        def project(ch, stage):
            res = jnp.dot(u_ref[pl.ds(start(ch), rows_c), :].astype(MXU_DTYPE), bb_ref[...], preferred_element_type=F32)
            stage[...] = res
            bu_all[pl.ds(start(ch), rows_c), :] = res

        def steps(src, r0, carry, xs=None):
            for i in range(SCAN_CHUNK):
                blk = src[pl.ds(r0 + i * n_rows, n_rows), :]
                carry = (ar * carry[0] - ai * carry[1] + blk[:, :w], ar * carry[1] + ai * carry[0] + blk[:, w:])
                if xs is not None:
                    xs[i * n_rows:(i + 1) * n_rows, :w] = carry[0]
                    xs[i * n_rows:(i + 1) * n_rows, w:] = carry[1]
            return carry

        def emit(xs, ch):
            y = lax.dot_general(xs[...].astype(MXU_DTYPE), c_ref[...], _NT, preferred_element_type=F32)
            yt = y + d_ref[...] * u_ref[pl.ds(start(ch), rows_c), :]
            yt_ref[pl.ds(start(ch), rows_c), :] = yt
            yg_ref[pl.ds(start(ch), rows_c), :] = (0.5 * yt * (1.0 + _gelu_tanh(yt))).astype(BF16)

        project(0, st_a)

        def pair1(p, carry):
            project(2 * p + 1, st_b)
            carry = steps(st_a, 0, carry)
            project(jnp.minimum(2 * p + 2, n_chunks - 1), st_a)
            return steps(st_b, 0, carry)

        zero = jnp.zeros((n_rows, w), F32)
        er, ei = lax.fori_loop(0, n_chunks // 2, pair1, (zero, zero))
        cr, ci = _segment_carry(er, ei, ar, ai, n_rows, False)
        ein_ref[:, :w] = cr
        ein_ref[:, w:] = ci

        xs_b[...] = jnp.zeros_like(xs_b)

        def pair2(p, carry):
            emit(xs_b, jnp.maximum(2 * p - 1, 0))
            carry = steps(bu_all, start(2 * p), carry, xs_a)
            emit(xs_a, 2 * p)
            return steps(bu_all, start(2 * p + 1), carry, xs_b)

        lax.fori_loop(0, n_chunks // 2, pair2, (cr, ci))
        emit(xs_b, n_chunks - 1)

    col = lambda width: pl.BlockSpec((t, width), lambda c: (0, c))
    wgt = pl.BlockSpec((SCAN_CH, 2 * w), lambda c: (c, 0))
    return _pallas_call(
        body, name="ssm_fwd", grid=(SCAN_NBLK,),
        in_specs=[col(SCAN_CH), pl.BlockSpec((1, 2 * w), lambda c: (0, c)), wgt, wgt,
                  pl.BlockSpec((1, SCAN_CH), lambda c: (0, c))],
        out_specs=[col(SCAN_CH), col(SCAN_CH), pl.BlockSpec((n_rows, 2 * w), lambda c: (0, c))],
        out_shape=[jax.ShapeDtypeStruct((t, SSM_W), F32), jax.ShapeDtypeStruct((t, SSM_W), BF16),
                   jax.ShapeDtypeStruct((n_rows, 2 * N_STATE), F32)],
        scratch_shapes=[pltpu.VMEM((t, 2 * w), F32)] + [pltpu.VMEM((rows_c, 2 * w), F32)] * 4,
        compiler_params=pltpu.CompilerParams(dimension_semantics=("parallel",), vmem_limit_bytes=VMEM_BIG),
    )(u_perm, a_cat, bbc, cc, dskip)


def _ssm_bwd(u_perm, dypre, du_skip, a_cat, bbc, cc, ein, n_rows, comm=None):
    t = u_perm.shape[0]
    w = SCAN_WC
    rows_c = SCAN_CHUNK * n_rows
    n_chunks = t // rows_c

    assert n_chunks % 2 == 0
    last = n_chunks - 1

    def body(u_ref, dy_ref, dus_ref, a_ref, bb_ref, c_ref, ein_ref, du_ref, da_ref, dbb_ref, dc_ref,
             xs_all, st_a, st_b, buf_a, buf_b):
        ar = jnp.broadcast_to(a_ref[:, :w], (n_rows, w))
        ai = jnp.broadcast_to(a_ref[:, w:], (n_rows, w))
        zero = jnp.zeros((n_rows, w), F32)
        start = lambda ch: pl.multiple_of(ch * rows_c, rows_c)
        dbb_ref[...] = jnp.zeros_like(dbb_ref)
        dc_ref[...] = jnp.zeros_like(dc_ref)
        da_ref[...] = jnp.zeros_like(da_ref)

        xs_all[0:n_rows, :] = ein_ref[...]

        def project(ch, stage):
            stage[...] = jnp.dot(u_ref[pl.ds(start(ch), rows_c), :].astype(MXU_DTYPE), bb_ref[...],
                                 preferred_element_type=F32)

        def fwd_steps(stage, ch, carry, xs):
            for i in range(SCAN_CHUNK):
                blk = stage[i * n_rows:(i + 1) * n_rows, :]
                carry = (ar * carry[0] - ai * carry[1] + blk[:, :w], ar * carry[1] + ai * carry[0] + blk[:, w:])
                for half, val in enumerate(carry):
                    xs[i * n_rows:(i + 1) * n_rows, half * w:(half + 1) * w] = val
                    xs_all[pl.ds(start(ch) + (i + 1) * n_rows, n_rows), half * w:(half + 1) * w] = val
            return carry

        def add_dc(xs, ch):
            dc_ref[...] += lax.dot_general(dy_ref[pl.ds(start(ch), rows_c), :], xs[...].astype(MXU_DTYPE), _TN,
                                           preferred_element_type=F32)

        project(0, st_a)

        def fwd_pair(p, carry):
            project(2 * p + 1, st_b)
            carry = fwd_steps(st_a, 2 * p, carry, buf_a)
            add_dc(buf_a, 2 * p)
            project(jnp.minimum(2 * p + 2, last), st_a)
            carry = fwd_steps(st_b, 2 * p + 1, carry, buf_b)
            add_dc(buf_b, 2 * p + 1)
            return carry

        lax.fori_loop(0, n_chunks // 2, fwd_pair, (ein_ref[:, :w], ein_ref[:, w:]))

        def project_dx(ch, stage):
            stage[...] = jnp.dot(dy_ref[pl.ds(start(ch), rows_c), :], c_ref[...], preferred_element_type=F32)

        def back_steps(stage, carry, g_buf=None):
            for i in reversed(range(SCAN_CHUNK)):
                blk = stage[i * n_rows:(i + 1) * n_rows, :]
                carry = (blk[:, :w] + ar * carry[0] + ai * carry[1], blk[:, w:] + ar * carry[1] - ai * carry[0])
                if g_buf is not None:
                    g_buf[i * n_rows:(i + 1) * n_rows, :w] = carry[0]
                    g_buf[i * n_rows:(i + 1) * n_rows, w:] = carry[1]
            return carry

        def first_pair(p, carry):
            project_dx(last - 2 * p - 1, st_b)
            carry = back_steps(st_a, carry)
            project_dx(jnp.maximum(last - 2 * p - 2, 0), st_a)
            return back_steps(st_b, carry)

        project_dx(last, st_a)
        sr, si = lax.fori_loop(0, n_chunks // 2, first_pair, (zero, zero))
        gr0, gi0 = _segment_carry(sr, si, ar, ai, n_rows, True)

        def post(g_buf, ch):
            g = g_buf[...]
            xp = xs_all[pl.ds(start(ch), rows_c), :]
            da_ref[:, :w] += jnp.sum(g[:, :w] * xp[:, :w] + g[:, w:] * xp[:, w:], axis=0, keepdims=True)
            da_ref[:, w:] += jnp.sum(g[:, w:] * xp[:, :w] - g[:, :w] * xp[:, w:], axis=0, keepdims=True)
            gb = g.astype(MXU_DTYPE)
            du_ref[pl.ds(start(ch), rows_c), :] = (lax.dot_general(gb, bb_ref[...], _NT, preferred_element_type=F32)
                                                   + dus_ref[pl.ds(start(ch), rows_c), :])
            dbb_ref[...] += lax.dot_general(u_ref[pl.ds(start(ch), rows_c), :].astype(MXU_DTYPE), gb, _TN,
                                            preferred_element_type=F32)

        def second_pair(p, carry):
            c1 = last - 2 * p
            project_dx(c1 - 1, st_b)
            post(buf_b, jnp.minimum(c1 + 1, last))
            carry = back_steps(st_a, carry, buf_a)
            project_dx(jnp.maximum(c1 - 2, 0), st_a)
            post(buf_a, c1)
            return back_steps(st_b, carry, buf_b)

        project_dx(last, st_a)
        buf_b[...] = jnp.zeros_like(buf_b)
        lax.fori_loop(0, n_chunks // 2, second_pair, (gr0, gi0))
        post(buf_b, 0)

    col = lambda width: pl.BlockSpec((t, width), lambda c, j: (0, c))
    wgt = pl.BlockSpec((SCAN_CH, 2 * w), lambda c, j: (c, 0))
    row = pl.BlockSpec((1, 2 * w), lambda c, j: (0, c))
    return _grid_call(
        body, "ssm_bwd", (SCAN_NBLK, 1), [u_perm, dypre, du_skip, a_cat, bbc, cc, ein],
        [col(SCAN_CH), col(SCAN_CH), col(SCAN_CH), row, wgt, wgt, pl.BlockSpec((n_rows, 2 * w), lambda c, j: (0, c))],
        [col(SCAN_CH), row, wgt, wgt],
        [jax.ShapeDtypeStruct((t, SSM_W), F32), jax.ShapeDtypeStruct((1, 2 * N_STATE), F32),
         jax.ShapeDtypeStruct((SSM_W, 2 * w), F32), jax.ShapeDtypeStruct((SSM_W, 2 * w), F32)],
        56 * 1024 * 1024, comm,
        scratch=[pltpu.VMEM((t + n_rows, 2 * w), F32)] + [pltpu.VMEM((rows_c, 2 * w), F32)] * 4)


def _to_scan_rows(a, n_samples):
    c = a.shape[1]
    return a.reshape(n_samples, SCAN_SEG_PER_SAMPLE, SCAN_LEN, c).transpose(2, 0, 1, 3).reshape(-1, c)


def _from_scan_rows(a, n_samples):
    c = a.shape[1]
    return a.reshape(SCAN_LEN, n_samples, SCAN_SEG_PER_SAMPLE, c).transpose(1, 2, 0, 3).reshape(-1, c)


def _row_spec(tm, width):
    return pl.BlockSpec((tm, width), lambda i, j: (i, 0))


def _whole(arr):
    return pl.BlockSpec(arr.shape, lambda i, j: (0,) * arr.ndim)


def _proj_rope(x, g, w_in_t, tabs, comm=None):
    t = x.shape[0]
    tm = 256

    def body(x_ref, g_ref, w_ref, tc_ref, tlo_ref, thi_ref, h_ref, u_ref, gate_ref, *rest):
        qkv_refs, stage = rest[:9], rest[9]
        xv = x_ref[...]
        r = lax.rsqrt(jnp.mean(xv * xv, axis=-1, keepdims=True) + RMS_EPS)
        h = ((xv * r) * g_ref[...]).astype(BF16)
        h_ref[...] = h
        p = lax.dot_general(h.astype(MXU_DTYPE), w_ref[...], _NT, preferred_element_type=F32)
        u_ref[...] = p[:, QKV_W:QKV_W + SSM_W]
        gate_ref[...] = _sigmoid(p[:, QKV_W + SSM_W:])
        tc, tlo, thi = tc_ref[...], tlo_ref[...], thi_ref[...]
        n_ch = QKV_W // LANES
        for ch in range(n_ch):
            piece = p[:, _lane_chunk(ch)]
            stage[ch] = _rope_apply(piece, tc, tlo, thi) if ch < 2 * n_ch // 3 else piece
        halves = GROUP_W // LANES
        for grp, d in enumerate(DILATIONS):
            for which in range(3):
                out = qkv_refs[3 * grp + which]
                for res in range(d):
                    for half in range(halves):
                        ch = which * (n_ch // 3) + grp * halves + half
                        out[:, _lane_chunk(res * halves + half)] = _gather_residue(stage, ch, res, d, tm // d).astype(BF16)

    tab = pl.BlockSpec((tm, LANES), lambda i, j: (i % (SEQ // tm), 0))
    widths = [(D_MODEL, BF16), (SSM_W, F32), (2 * D_MODEL, F32)]
    out_specs = [_row_spec(tm, wd) for wd, _ in widths]
    out_shapes = [jax.ShapeDtypeStruct((t, wd), dt) for wd, dt in widths]
    for d in DILATIONS:
        out_specs += [_row_spec(tm // d, d * GROUP_W)] * 3
        out_shapes += [jax.ShapeDtypeStruct((t // d, d * GROUP_W), BF16)] * 3
    return _grid_call(
        body, "proj_rope", (t // tm, 1), [x, g, w_in_t, *tabs],
        [_row_spec(tm, D_MODEL), _whole(g), _whole(w_in_t), tab, tab, tab], out_specs, out_shapes, VMEM_BIG, comm,
        scratch=[pltpu.VMEM((QKV_W // LANES, tm, LANES), F32)])


def _branch_outputs(attn_ref, yg_ref, wao_ref, wglu_ref):
    attn_d = lax.dot_general(attn_ref[...].astype(MXU_DTYPE), wao_ref[...], _NT, preferred_element_type=F32)
    z = lax.dot_general(yg_ref[...].astype(MXU_DTYPE), wglu_ref[...], _NT, preferred_element_type=F32)
    return attn_d, z[:, :D_MODEL], _sigmoid(z[:, D_MODEL:])


def _mix_out_rms(os_, lses, yg, gates, x, w_ao_t, w_glu_t, w_out, g, comm=None):
    t = x.shape[0]
    tm = 256

    def body(o0, o1, o2, l0, l1, l2, yg_ref, gate_ref, x_ref, wao_ref, wglu_ref, wout_ref, g_ref,
             attn_ref, lt_ref, m_ref, x1_ref, h_ref, nat):
        _merge_groups((o0, o1, o2), (l0, l1, l2), attn_ref, lt_ref, nat, tm)
        attn_d, za, sb = _branch_outputs(attn_ref, yg_ref, wao_ref, wglu_ref)
        merged = (gate_ref[:, :D_MODEL] * attn_d + gate_ref[:, D_MODEL:] * (za * sb)).astype(BF16)
        m_ref[...] = merged
        x1 = x_ref[...] + jnp.dot(merged.astype(MXU_DTYPE), wout_ref[...], preferred_element_type=F32)
        x1_ref[...] = x1
        r = lax.rsqrt(jnp.mean(x1 * x1, axis=-1, keepdims=True) + RMS_EPS)
        h_ref[...] = ((x1 * r) * g_ref[...]).astype(BF16)

    dil_specs = [_row_spec(tm // d, d * GROUP_W) for d in DILATIONS] * 2
    return _grid_call(
        body, "mix_out_rms", (t // tm, 1), [*os_, *lses, yg, gates, x, w_ao_t, w_glu_t, w_out, g],
        dil_specs + [_row_spec(tm, SSM_W), _row_spec(tm, 2 * D_MODEL), _row_spec(tm, D_MODEL),
                     _whole(w_ao_t), _whole(w_glu_t), _whole(w_out), _whole(g)],
        [_row_spec(tm, GROUP_W)] * 2 + [_row_spec(tm, D_MODEL)] * 3,
        [jax.ShapeDtypeStruct((t, GROUP_W), F32)] * 2
        + [jax.ShapeDtypeStruct((t, D_MODEL), BF16), jax.ShapeDtypeStruct((t, D_MODEL), F32),
           jax.ShapeDtypeStruct((t, D_MODEL), BF16)], VMEM_BIG, comm, scratch=[pltpu.VMEM((8, tm, LANES), F32)])


def _mix_bwd(dx1b, attn, lse_tot, yg, gates, w_ao_t, w_glu_t, w_out, comm=None):
    t = dx1b.shape[0]
    tm = 256

    def body(dx_ref, attn_ref, lt_ref, yg_ref, gate_ref, wao_ref, wglu_ref, wout_ref, ones_ref,
             dad_ref, dz_ref, dg_ref, da_ref, dyg_ref, rd_ref, *rest):
        dm = lax.dot_general(dx_ref[...], wout_ref[...], _NT, preferred_element_type=F32)
        attn_d, za, sb = _branch_outputs(attn_ref, yg_ref, wao_ref, wglu_ref)
        g0, g1 = gate_ref[:, :D_MODEL], gate_ref[:, D_MODEL:]
        dad = (dm * g0).astype(BF16)
        dad_ref[...] = dad
        ds = dm * g1
        dza, dzb = (ds * sb).astype(BF16), (ds * za * sb * (1.0 - sb)).astype(BF16)
        dz_ref[:, :D_MODEL] = dza
        dz_ref[:, D_MODEL:] = dzb
        dg_ref[:, :D_MODEL] = (dm * attn_d * g0 * (1.0 - g0)).astype(BF16)
        dg_ref[:, D_MODEL:] = (dm * (za * sb) * g1 * (1.0 - g1)).astype(BF16)
        da = jnp.dot(dad.astype(MXU_DTYPE), wao_ref[...], preferred_element_type=F32)
        da_ref[...] = da
        dyg_ref[...] = (jnp.dot(dza.astype(MXU_DTYPE), wglu_ref[:D_MODEL, :], preferred_element_type=F32)
                        + jnp.dot(dzb.astype(MXU_DTYPE), wglu_ref[D_MODEL:, :], preferred_element_type=F32))
        _attention_cotangents(da, attn_ref[...], lt_ref[...], ones_ref[...], rd_ref, rest[:6], rest[6], tm)

    widths = [(D_MODEL, BF16), (2 * D_MODEL, BF16), (2 * D_MODEL, BF16), (GROUP_W, F32), (SSM_W, F32), (GROUP_W, F32)]
    out_specs = [_row_spec(tm, wd) for wd, _ in widths]
    out_shapes = [jax.ShapeDtypeStruct((t, wd), dt) for wd, dt in widths]
    for d in DILATIONS[1:]:
        out_specs += [_row_spec(tm // d, d * GROUP_W)] * 3
        out_shapes += [jax.ShapeDtypeStruct((t // d, d * GROUP_W), F32)] * 3
    ones = _head_sum_matrix()
    return _grid_call(
        body, "mix_bwd", (t // tm, 1), [dx1b, attn, lse_tot, yg, gates, w_ao_t, w_glu_t, w_out, ones],
        [_row_spec(tm, D_MODEL), _row_spec(tm, GROUP_W), _row_spec(tm, GROUP_W), _row_spec(tm, SSM_W),
         _row_spec(tm, 2 * D_MODEL), _whole(w_ao_t), _whole(w_glu_t), _whole(w_out), _whole(ones)],
        out_specs, out_shapes, VMEM_BIG, comm, scratch=[pltpu.VMEM((6, tm, LANES), F32)])


FFN_TN = D_FF // 2
MXU_COLS = 256


def _ffn_in_swiglu(h2, w_gate_t, w_up_t, comm=None):
    t = h2.shape[0]
    tm = 512

    def body(h_ref, wg_ref, wu_ref, a_ref, b_ref, f_ref):
        h = h_ref[...].astype(MXU_DTYPE)
        for c0 in range(0, FFN_TN, MXU_COLS):
            sl = slice(c0, min(c0 + MXU_COLS, FFN_TN))
            a = lax.dot_general(h, wg_ref[sl, :], _NT, preferred_element_type=F32)
            b = lax.dot_general(h, wu_ref[sl, :], _NT, preferred_element_type=F32)
            a_ref[:, sl] = a
            b_ref[:, sl] = b
            f_ref[:, sl] = (a * _sigmoid(a) * b).astype(BF16)

    tile = pl.BlockSpec((tm, FFN_TN), lambda j, i: (i, j))
    wspec = pl.BlockSpec((FFN_TN, D_MODEL), lambda j, i: (j, 0))
    return _grid_call(
        body, "ffn_in_swiglu", (D_FF // FFN_TN, t // tm), [h2, w_gate_t, w_up_t],
        [pl.BlockSpec((tm, D_MODEL), lambda j, i: (i, 0)), wspec, wspec],
        [tile] * 3, [jax.ShapeDtypeStruct((t, D_FF), F32)] * 2 + [jax.ShapeDtypeStruct((t, D_FF), BF16)], VMEM_BIG, comm)


def _ffn_down_final(f, w_down, x1, target, g):
    t = x1.shape[0]
    tm = 256

    def body(f_ref, w_ref, x1_ref, t_ref, g_ref, dx_ref, dxb_ref, loss_ref, gg_ref):
        @pl.when(pl.program_id(0) == 0)
        def _():
            loss_ref[...] = jnp.zeros_like(loss_ref)
            gg_ref[...] = jnp.zeros_like(gg_ref)

        xv = x1_ref[...] + jnp.dot(f_ref[...].astype(MXU_DTYPE), w_ref[...], preferred_element_type=F32)
        gv = g_ref[...]
        r = lax.rsqrt(jnp.mean(xv * xv, axis=-1, keepdims=True) + RMS_EPS)
        n = xv * r
        diff = n * gv - t_ref[...]
        per_tok = jnp.mean(diff * diff, axis=-1, keepdims=True)
        loss_ref[...] += 0.5 * jnp.sum(per_tok, axis=0, keepdims=True)
        dy = diff / xv.shape[-1]
        gg_ref[...] += jnp.sum(dy * n, axis=0, keepdims=True)
        dn = dy * gv
        dx = r * (dn - n * jnp.mean(dn * n, axis=-1, keepdims=True))
        dx_ref[...] = dx
        dxb_ref[...] = dx.astype(BF16)

    acc = lambda shp: pl.BlockSpec(shp, lambda i, j: (0, 0))
    return _grid_call(
        body, "ffn_down_final", (t // tm, 1), [f, w_down, x1, target, g],
        [_row_spec(tm, D_FF), _whole(w_down), _row_spec(tm, D_MODEL), _row_spec(tm, D_MODEL), _whole(g)],
        [_row_spec(tm, D_MODEL)] * 2 + [acc((8, LANES)), acc((1, D_MODEL))],
        [jax.ShapeDtypeStruct((t, D_MODEL), F32), jax.ShapeDtypeStruct((t, D_MODEL), BF16),
         jax.ShapeDtypeStruct((8, LANES), F32), jax.ShapeDtypeStruct((1, D_MODEL), F32)], VMEM_BIG, sequential=True)


def _d_f_swiglu_bwd(dx2b, w_down, a, b):
    t = a.shape[0]
    tm = 512

    def body(dx_ref, w_ref, a_ref, b_ref, da_ref, db_ref):
        d = lax.dot_general(dx_ref[...], w_ref[...], _NT, preferred_element_type=F32)
        av, bv = a_ref[...], b_ref[...]
        sg = _sigmoid(av)
        da_ref[...] = (d * bv * sg * (1.0 + av * (1.0 - sg))).astype(BF16)
        db_ref[...] = (d * av * sg).astype(BF16)

    tile = pl.BlockSpec((tm, FFN_TN), lambda j, i: (i, j))
    return _grid_call(
        body, "d_f_swiglu_bwd", (D_FF // FFN_TN, t // tm), [dx2b, w_down, a, b],
        [pl.BlockSpec((tm, D_MODEL), lambda j, i: (i, 0)), pl.BlockSpec((FFN_TN, D_MODEL), lambda j, i: (j, 0)), tile, tile],
        [tile] * 2, [jax.ShapeDtypeStruct((t, D_FF), BF16)] * 2, VMEM_BIG)


def _mm_rms_bwd(operands, weights, x, g, dres, name, comm=None):
    t = x.shape[0]
    tm = 256
    n_op = len(operands)

    def body(*refs):
        a_refs, w_refs = refs[:n_op], refs[n_op:2 * n_op]
        x_ref, g_ref, dres_ref, dx_ref, dxb_ref, gg_ref = refs[2 * n_op:]

        @pl.when(pl.program_id(0) == 0)
        def _():
            gg_ref[...] = jnp.zeros_like(gg_ref)

        dh = None
        for a_ref, w_ref in zip(a_refs, w_refs):
            part = jnp.dot(a_ref[...].astype(MXU_DTYPE), w_ref[...], preferred_element_type=F32)
            dh = part if dh is None else dh + part
        xv = x_ref[...]
        r = lax.rsqrt(jnp.mean(xv * xv, axis=-1, keepdims=True) + RMS_EPS)
        n = xv * r
        gg_ref[...] += jnp.sum(dh * n, axis=0, keepdims=True)
        dn = dh * g_ref[...]
        dx = dres_ref[...] + r * (dn - n * jnp.mean(dn * n, axis=-1, keepdims=True))
        dx_ref[...] = dx
        dxb_ref[...] = dx.astype(BF16)

    d = x.shape[1]
    return _grid_call(
        body, name, (t // tm, 1), [*operands, *weights, x, g, dres],
        [_row_spec(tm, a.shape[1]) for a in operands] + [_whole(wk) for wk in weights]
        + [_row_spec(tm, d), _whole(g), _row_spec(tm, d)],
        [_row_spec(tm, d)] * 2 + [pl.BlockSpec((1, d), lambda i, j: (0, 0))],
        [jax.ShapeDtypeStruct((t, d), F32), jax.ShapeDtypeStruct((t, d), BF16), jax.ShapeDtypeStruct((1, d), F32)],
        VMEM_BIG, comm, sequential=True)


def _flat_small(small):
    perm_b = lambda a: a.reshape(SSM_GROUPS, SSM_STATE, SSM_CH).transpose(2, 0, 1).reshape(SSM_CH, N_STATE)
    perm_c = lambda a: a.reshape(SSM_GROUPS, SSM_CH, SSM_STATE).transpose(1, 0, 2).reshape(SSM_CH, N_STATE)
    return dict(
        g_mix=small["norm_mix_g"].reshape(1, D_MODEL), g_ffn=small["norm_ffn_g"].reshape(1, D_MODEL),
        g_fin=small["norm_final_g"].reshape(1, D_MODEL),
        lr=small["ssm_a_re"].reshape(1, N_STATE), li=small["ssm_a_im"].reshape(1, N_STATE),
        ldt=jnp.repeat(small["ssm_log_dt"].reshape(SSM_GROUPS), SSM_STATE).reshape(1, N_STATE),
        br=perm_b(small["ssm_b_re"]), bi=perm_b(small["ssm_b_im"]),
        cr=perm_c(small["ssm_c_re"]), ci=perm_c(small["ssm_c_im"]), dskip=small["ssm_d"].reshape(1, SSM_W))


AG_HOSTS = {"proj_rope": ("w_glu", "w_attn_out", "w_out", "w_ffn_gate"), "mix_out_rms": ("w_ffn_up",),
            "ffn_in_swiglu": ("w_ffn_down",)}
HALVED = ("w_ffn_gate", "w_ffn_up", "w_in")
A2A_HOSTS = {"d_h2_rms": ("w_ffn_down",), "mix_bwd": ("w_ffn_gate:0", "w_out"), "attn_bwd_g0": ("w_ffn_up:1",),
             "attn_bwd_g1": ("w_glu",), "attn_bwd_g2": ("w_attn_out",), "ssm_bwd": ("w_ffn_gate:1", "w_ffn_up:0"),
             "mm_g_in1": ("w_in:0",), "d_h0_rms": ("w_in:1",)}
SMALL_HOST = "mm_g_in0"


def _local_step(x, target, w, small, shards=None):
    t = x.shape[0]
    n_samples = t // SEQ
    n_rows = n_samples * SCAN_SEG_PER_SAMPLE
    tabs = _rope_tables()
    w = dict(w)
    fs = _flat_small(small)
    g_mix, g_ffn, g_fin, dskip = fs["g_mix"], fs["g_ffn"], fs["g_fin"], fs["dskip"]
    a_cat, bbc, cc = _ssm_disc(fs["lr"], fs["li"], fs["ldt"], fs["br"], fs["bi"], fs["cr"], fs["ci"])
    big, recv, small_pack = {}, {}, []

    def comm_of(name):
        if shards is None:
            return None
        if name == SMALL_HOST:
            return _ag_comm([(small_pack[0], 0, 0)], [(N_DEV, *small_pack[0].shape)])
        if name in AG_HOSTS:
            names = AG_HOSTS[name]
            return _ag_comm([(shards[n], j, 0) for j, n in enumerate(names)], [(N_DEV, *shards[n].shape) for n in names])
        if name in A2A_HOSTS:
            return _a2a_comm([(big[n].reshape(N_DEV, -1, big[n].shape[1]), 0) for n in A2A_HOSTS[name]])
        return None

    def absorb(name, carried):
        if name == SMALL_HOST:
            recv["small"] = carried[0]
        for n, a3 in zip(AG_HOSTS.get(name, ()), carried):
            w[n] = a3.reshape(-1, a3.shape[2])
        for n, a3 in zip(A2A_HOSTS.get(name, ()), carried):
            recv[n] = a3

    def mm(a, b, mode, name, tm, tn, **kw):
        comm = comm_of(name)
        if comm is None:
            return _mm(a, b, mode, name, tm, tn, **kw)
        out, *carried = _mm(a, b, mode, name, tm, tn, comm=comm, **kw)
        absorb(name, carried)
        return out

    h0, u, gates, *rest = _proj_rope(x, g_mix, w["w_in"], tabs, comm_of("proj_rope"))
    qkv = [rest[3 * g:3 * g + 3] for g in range(3)]
    absorb("proj_rope", rest[9:])
    os_, lses = [], []
    for g in range(3):
        o_g, l_g, carried = _attn_fwd(*qkv[g], g, n_samples, comm_of(f"attn_fwd_g{g}"))
        absorb(f"attn_fwd_g{g}", carried)
        os_.append(o_g)
        lses.append(l_g)
    u_perm = _to_scan_rows(u, n_samples)
    ytot, yg_perm, ein = _ssm_fwd(u_perm, a_cat, bbc, cc, dskip, n_rows)
    yg = _from_scan_rows(yg_perm, n_samples)

    attn, lse_tot, merged, x1, h2, *carried = _mix_out_rms(os_, lses, yg, gates, x, w["w_attn_out"], w["w_glu"], w["w_out"],
                                                           g_ffn, comm_of("mix_out_rms"))
    absorb("mix_out_rms", carried)
    ffn_a, ffn_b, f, *carried = _ffn_in_swiglu(h2, w["w_ffn_gate"], w["w_ffn_up"], comm_of("ffn_in_swiglu"))
    absorb("ffn_in_swiglu", carried)
    dx2, dx2b, loss_blk, g_gfin = _ffn_down_final(f, w["w_ffn_down"], x1, target, g_fin)

    da, db = _d_f_swiglu_bwd(dx2b, w["w_ffn_down"], ffn_a, ffn_b)
    big["w_ffn_down"] = mm(f, dx2b, "tn", "mm_g_down", 256, D_MODEL, out_dtype=BF16)
    half = D_MODEL // 2
    for hf in range(2):
        big[f"w_ffn_gate:{hf}"] = mm(da, h2, "tn", f"mm_g_gate{hf}", 256, half, out_dtype=BF16, cols=(hf * half, half))
        big[f"w_ffn_up:{hf}"] = mm(db, h2, "tn", f"mm_g_up{hf}", 256, half, out_dtype=BF16, cols=(hf * half, half))
    dx1, dx1b, g_gffn, *carried = _mm_rms_bwd([da, db], [w["w_ffn_gate"], w["w_ffn_up"]], x1, g_ffn, dx2, "d_h2_rms",
                                              comm_of("d_h2_rms"))
    absorb("d_h2_rms", carried)

    big["w_out"] = mm(merged, dx1b, "tn", "mm_g_out", 256, D_MODEL, out_dtype=BF16)
    dattn_d, dz, dgpre, dattn, dyg, rowdot, *rest = _mix_bwd(dx1b, attn, lse_tot, yg, gates, w["w_attn_out"], w["w_glu"],
                                                             w["w_out"], comm_of("mix_bwd"))
    cot = [(dattn, lse_tot, rowdot), tuple(rest[:3]), tuple(rest[3:6])]
    absorb("mix_bwd", rest[6:])

    big["w_attn_out"] = mm(dattn_d, attn, "tn", "mm_g_attn_out", 512, GROUP_W, out_dtype=BF16)
    big["w_glu"] = mm(dz, yg, "tn", "mm_g_glu", 512, 512, out_dtype=BF16)
    dqs, dks, dvs = [], [], []
    for g in range(3):
        dq_g, dk_g, dv_g, carried = _attn_bwd(*qkv[g], *cot[g], g, n_samples, comm_of(f"attn_bwd_g{g}"))
        absorb(f"attn_bwd_g{g}", carried)
        dqs.append(dq_g)
        dks.append(dk_g)
        dvs.append(dv_g)

    dyg_perm = _to_scan_rows(dyg, n_samples)
    dypre, du_skip, g_dskip = _ssm_act_bwd(dyg_perm, ytot, u_perm, dskip)
    du_perm, da_cat, dbb_full, dc_full, *carried = _ssm_bwd(u_perm, dypre, du_skip, a_cat, bbc, cc, ein, n_rows,
                                                          comm_of("ssm_bwd"))
    absorb("ssm_bwd", carried)
    du = _from_scan_rows(du_perm, n_samples)
    g_lr, g_li, g_ldt, g_br, g_bi, g_cr, g_ci = _ssm_param_bwd(
        fs["lr"], fs["li"], fs["ldt"], fs["br"], fs["bi"], da_cat, dbb_full, dc_full)

    small_pack.append(_pack_small(dict(lr=g_lr, li=g_li, ldt=g_ldt, br=g_br, bi=g_bi, cr=g_cr, ci=g_ci, dskip=g_dskip,
                                       g_ffn=g_gffn, g_fin=g_gfin, loss=loss_blk)))

    dproj = _pack_dproj(dqs, dks, dvs, du, dgpre, tabs)
    for hf in range(2):
        big[f"w_in:{hf}"] = mm(dproj, h0, "tn", f"mm_g_in{hf}", 256, half, out_dtype=BF16, cols=(hf * half, half))
    grad_x, _, g_gmix, *carried = _mm_rms_bwd([dproj], [w["w_in"]], x, g_mix, dx1, "d_h0_rms", comm_of("d_h0_rms"))
    absorb("d_h0_rms", carried)
    return grad_x, (big if shards is None else recv), small_pack[0], g_gmix


_MESH = pl.DeviceIdType.MESH


def _all_gather(block, name):
    rows, lanes = block.shape

    def body(x_ref, out_ref, send_sems, recv_sems, local_sem):
        x, y, c = lax.axis_index("x"), lax.axis_index("y"), lax.axis_index("c")
        me, sibling = (x, y, c), (x, y, 1 - c)
        chips = [(1 - x, y), (x, 1 - y), (1 - x, 1 - y)]

        def slot(px, py, pc):
            return out_ref.at[4 * px + 2 * py + pc]

        def copy(k, blk, to, src=None):
            return pltpu.make_async_remote_copy(
                src_ref=slot(*blk) if src is None else src, dst_ref=slot(*blk), send_sem=send_sems.at[k],
                recv_sem=recv_sems.at[k], device_id=to, device_id_type=_MESH)

        mine = pltpu.make_async_copy(x_ref, slot(*me), local_sem)
        mine.start()
        first = [copy(0, me, sibling, src=x_ref)]
        first += [copy(1 + j, me, (*chip, c), src=x_ref) for j, chip in enumerate(chips)]
        for cp in first:
            cp.start()
        passed = [copy(4 + j, (*chip, c), sibling) for j, chip in enumerate(chips)]
        for j, chip in enumerate(chips):
            copy(1 + j, (*chip, c), me).wait_recv()
            passed[j].start()
        copy(0, sibling, me).wait_recv()
        for j, chip in enumerate(chips):
            copy(4 + j, (*chip, 1 - c), me).wait_recv()
        for cp in first + passed:
            cp.wait_send()
        mine.wait()

    return _pallas_call(
        body, name=name, out_shape=jax.ShapeDtypeStruct((N_DEV, rows, lanes), block.dtype),
        in_specs=[pl.BlockSpec(memory_space=pl.ANY)], out_specs=pl.BlockSpec(memory_space=pl.ANY),
        scratch_shapes=[pltpu.SemaphoreType.DMA((7,)), pltpu.SemaphoreType.DMA((7,)), pltpu.SemaphoreType.DMA],
    )(block)


def _ag_comm(items, bufs):
    def plan(in_refs, out_refs, send_sems, recv_sems, local_sems):
        x, y, c = lax.axis_index("x"), lax.axis_index("y"), lax.axis_index("c")
        me, sibling = (x, y, c), (x, y, 1 - c)
        chips = [(1 - x, y), (x, 1 - y), (1 - x, 1 - y)]
        plans = []
        for t, (_, buf, slot0) in enumerate(items):
            x_ref, out_ref = in_refs[t], out_refs[buf]

            def slot(px, py, pc, out_ref=out_ref, slot0=slot0):
                return out_ref.at[slot0 + 4 * px + 2 * py + pc]

            def copy(k, blk, to, src=None, t=t, slot=slot):
                return pltpu.make_async_remote_copy(
                    src_ref=slot(*blk) if src is None else src, dst_ref=slot(*blk), send_sem=send_sems.at[7 * t + k],
                    recv_sem=recv_sems.at[7 * t + k], device_id=to, device_id_type=_MESH)

            plans.append(dict(
                mine=pltpu.make_async_copy(x_ref, slot(*me), local_sems.at[t]),
                first=[copy(0, me, sibling, src=x_ref)] + [copy(1 + j, me, (*chip, c), src=x_ref)
                                                           for j, chip in enumerate(chips)],
                passed=[copy(4 + j, (*chip, c), sibling) for j, chip in enumerate(chips)],
                from_ici=[copy(1 + j, (*chip, c), me) for j, chip in enumerate(chips)],
                from_sibling=[copy(0, sibling, me)] + [copy(4 + j, (*chip, 1 - c), me) for j, chip in enumerate(chips)]))
        return plans

    def start(*refs):
        for p in plan(*refs):
            p["mine"].start()
            for cp in p["first"]:
                cp.start()

    def finish(*refs):
        plans = plan(*refs)
        for p in plans:
            for arrived, onward in zip(p["from_ici"], p["passed"]):
                arrived.wait_recv()
                onward.start()
        for p in plans:
            for arrived in p["from_sibling"]:
                arrived.wait_recv()
            for cp in p["first"] + p["passed"]:
                cp.wait_send()
            p["mine"].wait()

    dtype_of = {buf: shard.dtype for shard, buf, _ in items}
    out_shapes = [jax.ShapeDtypeStruct(b, dtype_of[j]) for j, b in enumerate(bufs)]
    return _Comm([it[0] for it in items], out_shapes, 7 * len(items), len(items), start, finish)


def _a2a_comm(items):
    def plan(in_refs, out_refs, send_sems, recv_sems, local_sems):
        x, y, c = lax.axis_index("x"), lax.axis_index("y"), lax.axis_index("c")
        my = 4 * x + 2 * y + c
        copies, locals_ = [], []
        for t, (_, slot0) in enumerate(items):
            s_ref, r_ref = in_refs[t], out_refs[t]
            locals_.append(pltpu.make_async_copy(s_ref.at[slot0 + my], r_ref.at[my], local_sems.at[t]))
            for kk in range(1, N_DEV):
                px = 1 - x if kk & 4 else x
                py = 1 - y if kk & 2 else y
                pc = 1 - c if kk & 1 else c
                copies.append(pltpu.make_async_remote_copy(
                    src_ref=s_ref.at[slot0 + 4 * px + 2 * py + pc], dst_ref=r_ref.at[my],
                    send_sem=send_sems.at[7 * t + kk - 1], recv_sem=recv_sems.at[7 * t + kk - 1],
                    device_id=(px, py, pc), device_id_type=_MESH))
        return copies, locals_

    def start(*refs):
        copies, locals_ = plan(*refs)
        for cp in locals_ + copies:
            cp.start()

    def finish(*refs):
        copies, locals_ = plan(*refs)
        for cp in copies + locals_:
            cp.wait()

    out_shapes = [jax.ShapeDtypeStruct((N_DEV,) + it[0].shape[1:], it[0].dtype) for it in items]
    return _Comm([it[0] for it in items], out_shapes, 7 * len(items), len(items), start, finish)


def _adam_math(g, w, m, v):
    m_new = ADAM_B1 * m + (1.0 - ADAM_B1) * g
    v_new = ADAM_B2 * v + (1.0 - ADAM_B2) * jnp.square(g)
    m_hat = m_new / (1.0 - ADAM_B1 ** ADAM_STEP)
    v_hat = v_new / (1.0 - ADAM_B2 ** ADAM_STEP)
    return -ADAM_LR * (m_hat / (jnp.sqrt(v_hat) + ADAM_EPS) + ADAM_WD * w), m_new, v_new


def _sum_partials(parts, name, tm):
    n, rows, _ = parts[0].shape
    widths = [p.shape[2] for p in parts]

    def body(*refs):
        g_ref, off = refs[-1], 0
        for p_ref, wd in zip(refs[:-1], widths):
            g = p_ref[0].astype(F32)
            for s in range(1, n):
                g = g + p_ref[s].astype(F32)
            g_ref[:, off:off + wd] = g
            off += wd

    return _pallas_call(
        body, name=name, grid=(rows // tm,), in_specs=[pl.BlockSpec((n, tm, wd), lambda i: (0, i, 0)) for wd in widths],
        out_specs=pl.BlockSpec((tm, sum(widths)), lambda i: (i, 0)),
        out_shape=jax.ShapeDtypeStruct((rows, sum(widths)), F32),
        compiler_params=pltpu.CompilerParams(dimension_semantics=("parallel",), vmem_limit_bytes=VMEM_MID),
    )(*parts)


def _adam(partials, w, m, v, name, tm):
    n, rows, cols = partials.shape

    def body(p_ref, w_ref, m_ref, v_ref, g_ref, d_ref, nm_ref, nv_ref):
        g = p_ref[0].astype(F32)
        for s in range(1, n):
            g = g + p_ref[s].astype(F32)
        g_ref[...] = g
        d_ref[...], nm_ref[...], nv_ref[...] = _adam_math(g, w_ref[...], m_ref[...], v_ref[...])

    assert rows % tm == 0
    row = pl.BlockSpec((tm, cols), lambda i: (i, 0))
    shp = jax.ShapeDtypeStruct((rows, cols), F32)
    return _pallas_call(
        body, name=name, grid=(rows // tm,),
        in_specs=[pl.BlockSpec((n, tm, cols), lambda i: (0, i, 0)), row, row, row],
        out_specs=[row] * 4, out_shape=[shp] * 4,
        compiler_params=pltpu.CompilerParams(dimension_semantics=("parallel",), vmem_limit_bytes=VMEM_MID),
    )(partials, w, m, v)


_PK_LR, _PK_LI, _PK_GAINS, _PK_MISC, _PK_BR, _PK_BI, _PK_CR, _PK_CI, _PK_ROWS = 0, 1, 2, 3, 8, 24, 40, 56, 72
_PK_LDT_LANE, _PK_LOSS_LANE = D_MODEL + SSM_W, D_MODEL + SSM_W + LANES


def _pack_small(sg):
    names = ("lr", "li", "g_ffn", "g_fin", "dskip", "ldt", "loss", "br", "bi", "cr", "ci")

    def body(lr, li, gffn, gfin, dskip, ldt, loss, br, bi, cr, ci, o_ref):
        o_ref[...] = jnp.zeros_like(o_ref)
        o_ref[_PK_LR:_PK_LR + 1, :] = lr[...]
        o_ref[_PK_LI:_PK_LI + 1, :] = li[...]
        o_ref[_PK_GAINS:_PK_GAINS + 1, D_MODEL:] = gffn[...]
        o_ref[_PK_MISC:_PK_MISC + 1, :D_MODEL] = gfin[...]
        o_ref[_PK_MISC:_PK_MISC + 1, D_MODEL:D_MODEL + SSM_W] = dskip[...]
        o_ref[_PK_MISC:_PK_MISC + 1, _PK_LDT_LANE:_PK_LDT_LANE + LANES] = ldt[0:1, :]
        o_ref[_PK_MISC:_PK_MISC + 1, _PK_LOSS_LANE:_PK_LOSS_LANE + LANES] = loss[0:1, :]
        o_ref[_PK_BR:_PK_BR + SSM_CH, :] = br[...]
        o_ref[_PK_BI:_PK_BI + SSM_CH, :] = bi[...]
        o_ref[_PK_CR:_PK_CR + SSM_CH, :] = cr[...]
        o_ref[_PK_CI:_PK_CI + SSM_CH, :] = ci[...]

    return _pallas_call(body, name="pack_small", out_shape=jax.ShapeDtypeStruct((_PK_ROWS, N_STATE), F32))(
        *[sg[n] for n in names])


def _unpack_small(s, g_mix):
    unflat_b = lambda a: a.reshape(SSM_CH, SSM_GROUPS, SSM_STATE).transpose(1, 2, 0)[None]
    unflat_c = lambda a: a.reshape(SSM_CH, SSM_GROUPS, SSM_STATE).transpose(1, 0, 2)[None]
    grads = {
        "norm_mix_g": g_mix, "norm_ffn_g": s[_PK_GAINS, D_MODEL:].reshape(1, D_MODEL),
        "norm_final_g": s[_PK_MISC, :D_MODEL],
        "ssm_a_re": s[_PK_LR].reshape(1, SSM_GROUPS, SSM_STATE), "ssm_a_im": s[_PK_LI].reshape(1, SSM_GROUPS, SSM_STATE),
        "ssm_log_dt": s[_PK_MISC, _PK_LDT_LANE:_PK_LDT_LANE + SSM_GROUPS].reshape(1, SSM_GROUPS),
        "ssm_d": s[_PK_MISC, D_MODEL:D_MODEL + SSM_W].reshape(1, SSM_GROUPS, SSM_CH),
        "ssm_b_re": unflat_b(s[_PK_BR:_PK_BR + SSM_CH]), "ssm_b_im": unflat_b(s[_PK_BI:_PK_BI + SSM_CH]),
        "ssm_c_re": unflat_c(s[_PK_CR:_PK_CR + SSM_CH]), "ssm_c_im": unflat_c(s[_PK_CI:_PK_CI + SSM_CH]),
    }
    return s[_PK_MISC, _PK_LOSS_LANE], grads


def _adam_small(grads, wts, moms, vars_):
    n = len(SMALL_WEIGHTS)
    as2d = lambda a: a.reshape(1, -1) if a.ndim == 1 else a

    def body(*refs):
        ins, outs = refs[:4 * n], refs[4 * n:]
        for i in range(n):
            g, w, m, v = (ins[j * n + i][...] for j in range(4))
            outs[i][...], outs[n + i][...], outs[2 * n + i][...] = _adam_math(g, w, m, v)

    operands = [as2d(d[k]) for d in (grads, wts, moms, vars_) for k in SMALL_WEIGHTS]
    shapes = [jax.ShapeDtypeStruct(as2d(wts[k]).shape, F32) for k in SMALL_WEIGHTS] * 3
    res = _pallas_call(body, name="adam_small", out_shape=shapes,
                         compiler_params=pltpu.CompilerParams(vmem_limit_bytes=VMEM_BIG))(*operands)
    out = {}
    for j, kind in enumerate(("delta", "new_m", "new_v")):
        for i, k in enumerate(SMALL_WEIGHTS):
            out[kind, k] = res[j * n + i].reshape(wts[k].shape)
    return out


def kernel(x, norm_mix_g, w_in, ssm_a_re, ssm_a_im, ssm_log_dt, ssm_b_re, ssm_b_im, ssm_c_re, ssm_c_im, ssm_d, w_glu, w_attn_out, w_out, norm_ffn_g, w_ffn_gate, w_ffn_up, w_ffn_down, norm_final_g, loss_target, m_norm_mix_g, m_w_in, m_ssm_a_re, m_ssm_a_im, m_ssm_log_dt, m_ssm_b_re, m_ssm_b_im, m_ssm_c_re, m_ssm_c_im, m_ssm_d, m_w_glu, m_w_attn_out, m_w_out, m_norm_ffn_g, m_w_ffn_gate, m_w_ffn_up, m_w_ffn_down, m_norm_final_g, v_norm_mix_g, v_w_in, v_ssm_a_re, v_ssm_a_im, v_ssm_log_dt, v_ssm_b_re, v_ssm_b_im, v_ssm_c_re, v_ssm_c_im, v_ssm_d, v_w_glu, v_w_attn_out, v_w_out, v_norm_ffn_g, v_w_ffn_gate, v_w_ffn_up, v_w_ffn_down, v_norm_final_g):
    args = dict(locals())
    wts = {n: args[n] for n in ALL_WEIGHTS}
    moms = {n: args["m_" + n] for n in ALL_WEIGHTS}
    vars_ = {n: args["v_" + n] for n in ALL_WEIGHTS}
    n_samples = x.shape[0]
    t = n_samples * SEQ

    shards = {n: (wts[n][0] if n in ROW_SHARDED else wts[n][0].T).astype(BF16) for n in BIG_WEIGHTS}
    w_in_t = _all_gather(shards["w_in"], "allgather_w_in").reshape(IN_W, D_MODEL)

    small = {n: wts[n] for n in SMALL_WEIGHTS}
    grad_x, recv, _, g_mix_part = _local_step(x.reshape(t, D_MODEL), loss_target.reshape(t, D_MODEL), {"w_in": w_in_t},
                                              small, shards)

    results = {}
    for n in BIG_WEIGHTS:
        c, k = shards[n].shape
        w2, m2, v2 = wts[n][0], moms[n][0], vars_[n][0]
        if n in ROW_SHARDED:
            res = _adam(recv[n], w2, m2, v2, "adam_" + n, c // 2)
        else:
            parts = [recv[f"{n}:{hf}"] for hf in range(2)] if n in HALVED else [recv[n]]
            g_t = _sum_partials(parts, "sum_" + n, c // 2)
            res = _adam(g_t.T[None], w2, m2, v2, "adam_" + n, k // 2)
        for kind, a in zip(("grad", "delta", "new_m", "new_v"), res):
            results[kind, n] = a[None]

    g_mix_all = _all_gather(jnp.pad(g_mix_part, ((0, 7), (0, 0))), "allgather_g_mix")
    g_mix = _sum_partials([g_mix_all], "sum_g_mix", 8)[0:1]
    loss, sgrads = _unpack_small(_sum_partials([recv["small"]], "sum_small", _PK_ROWS), g_mix)
    for n in SMALL_WEIGHTS:
        results["grad", n] = sgrads[n]
    results.update(_adam_small(sgrads, wts, moms, vars_))
    outs = [loss, grad_x.reshape(x.shape)]
    for kind in ("grad", "delta", "new_m", "new_v"):
        outs += [results[kind, n] for n in ALL_WEIGHTS]
    return tuple(outs)
```

```python
import functools
import math

import jax
import jax.numpy as jnp
from jax import lax
from jax.experimental import pallas as pl
from jax.experimental.pallas import tpu as pltpu

F32 = jnp.float32
BF16 = jnp.bfloat16
MXU_DTYPE = jnp.bfloat16

N_DEV = 8
D_MODEL = 1024
SEQ = 2048
HEAD_DIM = 64
HEADS_PER_GROUP = 4
GROUP_W = HEADS_PER_GROUP * HEAD_DIM
DILATIONS = (1, 4, 16)
QKV_W = 3 * len(DILATIONS) * GROUP_W
Q_W = len(DILATIONS) * GROUP_W
ATT_BLOCK = 128
ROPE_DIM = 16
ROPE_THETA = 500000.0
SSM_W = 512
SSM_GROUPS = 32
SSM_CH = 16
SSM_STATE = 64
N_STATE = SSM_GROUPS * SSM_STATE
D_FF = 2816
IN_W = QKV_W + SSM_W + 2 * D_MODEL
RMS_EPS = 1e-6
NEG_INF = -1e30
LANES = 128

SCAN_SEG_PER_SAMPLE = 8
SCAN_LEN = SEQ // SCAN_SEG_PER_SAMPLE
SCAN_WC = 512
SCAN_NBLK = N_STATE // SCAN_WC
SCAN_CH = SSM_W // SCAN_NBLK
SCAN_CHUNK = 32

ADAM_LR = 0.001
ADAM_B1 = 0.9
ADAM_B2 = 0.999
ADAM_EPS = 1e-08
ADAM_WD = 0.01
ADAM_STEP = 10

VMEM_BIG = 48 * 1024 * 1024
VMEM_MID = 32 * 1024 * 1024

BIG_WEIGHTS = ("w_in", "w_glu", "w_attn_out", "w_out", "w_ffn_gate", "w_ffn_up", "w_ffn_down")
ROW_SHARDED = ("w_out", "w_ffn_down")
SMALL_WEIGHTS = ("norm_mix_g", "ssm_a_re", "ssm_a_im", "ssm_log_dt", "ssm_b_re", "ssm_b_im", "ssm_c_re", "ssm_c_im",
                 "ssm_d", "norm_ffn_g", "norm_final_g")
ALL_WEIGHTS = ("norm_mix_g", "w_in", "ssm_a_re", "ssm_a_im", "ssm_log_dt", "ssm_b_re", "ssm_b_im", "ssm_c_re", "ssm_c_im",
               "ssm_d", "w_glu", "w_attn_out", "w_out", "norm_ffn_g", "w_ffn_gate", "w_ffn_up", "w_ffn_down", "norm_final_g")


def _sigmoid(x):
    return 1.0 / (1.0 + jnp.exp(-x))


def _pallas_call(body, *, out_shape, **kw):
    single = not isinstance(out_shape, (list, tuple))
    shapes = [pltpu.HBM(s.shape, s.dtype) for s in ([out_shape] if single else out_shape)]
    call = pl.pallas_call(body, out_shape=shapes[0] if single else shapes, **kw)
    return lambda *operands: call(*[pltpu.with_memory_space_constraint(o, pltpu.HBM) for o in operands])


class _Comm:
    def __init__(self, ins, out_shapes, n_sem, n_local, start, finish):
        self.ins, self.out_shapes, self.n_sem, self.n_local = ins, out_shapes, n_sem, n_local
        self.start, self.finish = start, finish


def _mm(a, b, mode, name, tm, tn, out_dtype=F32, add=None, vmem=VMEM_BIG, comm=None, cols=None):
    if mode == "nn":
        (m, k), (_, n) = a.shape, b.shape
        a_spec = pl.BlockSpec((tm, k), lambda i, j: (i, 0))
        b_spec = pl.BlockSpec((k, tn), lambda i, j: (0, j))
        dims = (((1,), (0,)), ((), ()))
    elif mode == "nt":
        (m, k), (n, _) = a.shape, b.shape
        a_spec = pl.BlockSpec((tm, k), lambda i, j: (i, 0))
        b_spec = pl.BlockSpec((tn, k), lambda i, j: (j, 0))
        dims = (((1,), (1,)), ((), ()))
    else:
        (k, m), (_, n) = a.shape, b.shape
        first, n = cols if cols else (0, n)
        a_spec = pl.BlockSpec((k, tm), lambda i, j: (0, i))
        b_spec = pl.BlockSpec((k, tn), lambda i, j: (0, j + first // tn))
        dims = (((0,), (0,)), ((), ()))
    assert m % tm == 0 and n % tn == 0, (name, m, n, tm, tn)
    o_spec = pl.BlockSpec((tm, tn), lambda i, j: (i, j))
    has_add = add is not None

    def body(*refs):
        a_ref, b_ref, o_ref = refs[0], refs[1], refs[-1]
        acc = lax.dot_general(a_ref[...].astype(MXU_DTYPE), b_ref[...].astype(MXU_DTYPE), dims,
                              preferred_element_type=F32)
        if has_add:
            acc = acc + refs[2][...]
        o_ref[...] = acc.astype(out_dtype)

    ins = [a, b] + ([add] if has_add else [])
    in_specs = [a_spec, b_spec] + ([o_spec] if has_add else [])
    return _grid_call(body, name, (m // tm, n // tn), ins, in_specs, [o_spec],
                      [jax.ShapeDtypeStruct((m, n), out_dtype)], vmem, comm)


def _grid_call(body, name, grid, ins, in_specs, out_specs, out_shapes, vmem, comm=None, sequential=False, scratch=()):
    if comm is None:
        single = len(out_shapes) == 1
        semantics = ("arbitrary", "arbitrary") if sequential else ("parallel", "parallel")
        return _pallas_call(
            body, name=name, grid=grid, in_specs=in_specs, out_specs=out_specs[0] if single else out_specs,
            out_shape=out_shapes[0] if single else out_shapes, scratch_shapes=list(scratch),
            compiler_params=pltpu.CompilerParams(dimension_semantics=semantics, vmem_limit_bytes=vmem),
        )(*ins)
    n_in, n_out, n_cin, n_cout = len(ins), len(out_shapes), len(comm.ins), len(comm.out_shapes)
    n_io = n_in + n_cin + n_out + n_cout

    def carrying(*refs):
        own = refs[:n_in] + refs[n_in + n_cin:n_in + n_cin + n_out] + refs[n_io:len(refs) - 3]
        c_args = (refs[n_in:n_in + n_cin], refs[n_in + n_cin + n_out:n_io], *refs[-3:])

        @pl.when((pl.program_id(0) == 0) & (pl.program_id(1) == 0))
        def _():
            comm.start(*c_args)

        body(*own)

        @pl.when((pl.program_id(0) == grid[0] - 1) & (pl.program_id(1) == grid[1] - 1))
        def _():
            comm.finish(*c_args)

    hbm = pl.BlockSpec(memory_space=pl.ANY)
    return _pallas_call(
        carrying, name=name, grid=grid, in_specs=list(in_specs) + [hbm] * n_cin,
        out_specs=list(out_specs) + [hbm] * n_cout, out_shape=list(out_shapes) + list(comm.out_shapes),
        scratch_shapes=list(scratch) + [pltpu.SemaphoreType.DMA((comm.n_sem,)), pltpu.SemaphoreType.DMA((comm.n_sem,)),
                                        pltpu.SemaphoreType.DMA((comm.n_local,))],
        compiler_params=pltpu.CompilerParams(dimension_semantics=("arbitrary", "arbitrary"), vmem_limit_bytes=vmem),
    )(*ins, *comm.ins)


def _rows(body, name, n_rows, tm, ins, outs, vmem=VMEM_MID, scratch=()):
    assert n_rows % tm == 0
    arrays, in_specs = [], []
    for kind, arr in ins:
        arrays.append(arr)
        if kind == "row":
            assert n_rows % arr.shape[0] == 0, (name, arr.shape)
            in_specs.append(pl.BlockSpec((tm * arr.shape[0] // n_rows, arr.shape[1]), lambda i: (i, 0)))
        elif kind == "tab":
            nblk = arr.shape[0] // tm
            in_specs.append(pl.BlockSpec((tm, arr.shape[1]), lambda i, nblk=nblk: (i % nblk, 0)))
        else:
            in_specs.append(pl.BlockSpec(arr.shape, lambda i, nd=arr.ndim: (0,) * nd))
    out_specs, out_shape = [], []
    for kind, shp, dt in outs:
        if kind == "row":
            out_specs.append(pl.BlockSpec((tm, shp), lambda i: (i, 0)))
            out_shape.append(jax.ShapeDtypeStruct((n_rows, shp), dt))
        elif kind == "dil":
            d, wd = shp
            out_specs.append(pl.BlockSpec((tm // d, d * wd), lambda i: (i, 0)))
            out_shape.append(jax.ShapeDtypeStruct((n_rows // d, d * wd), dt))
        else:
            out_specs.append(pl.BlockSpec(shp, lambda i, nd=len(shp): (0,) * nd))
            out_shape.append(jax.ShapeDtypeStruct(shp, dt))
    res = _pallas_call(
        body, name=name, grid=(n_rows // tm,), in_specs=in_specs, out_specs=out_specs, out_shape=out_shape,
        scratch_shapes=list(scratch),
        compiler_params=pltpu.CompilerParams(dimension_semantics=("arbitrary",), vmem_limit_bytes=vmem),
    )(*arrays)
    return res


def _gather_residue(stage, ch, r, d, n):
    return stage[ch, pl.ds(r, n, stride=d), :] if d > 1 else stage[ch]


def _scatter_residue(stage, ch, r, d, n, val):
    if d > 1:
        stage[ch, pl.ds(r, n, stride=d), :] = val
    else:
        stage[ch] = val


def _lane_chunk(ch):
    return slice(ch * LANES, (ch + 1) * LANES)


def _first_step():
    return pl.program_id(0) == 0


def _rope_tables():
    half = ROPE_DIM // 2
    inv = jnp.power(jnp.float32(ROPE_THETA), -jnp.arange(half, dtype=F32) * 2.0 / ROPE_DIM)
    ang = jnp.arange(SEQ, dtype=F32)[:, None] * inv[None, :]
    lane = jnp.arange(LANES) % HEAD_DIM
    cosl = jnp.cos(ang)[:, lane % half]
    sinl = jnp.sin(ang)[:, lane % half]
    tab_c = jnp.where(lane < ROPE_DIM, cosl, 1.0)
    tab_lo = jnp.where(lane < half, -sinl, 0.0)
    tab_hi = jnp.where((lane >= half) & (lane < ROPE_DIM), sinl, 0.0)
    return tab_c.astype(F32), tab_lo.astype(F32), tab_hi.astype(F32)


def _rope_apply(t, tc, tlo, thi):
    half = ROPE_DIM // 2
    return t * tc + pltpu.roll(t, LANES - half, 1) * tlo + pltpu.roll(t, half, 1) * thi


def _rope_transpose(dt, tc, tlo, thi):
    half = ROPE_DIM // 2
    return dt * tc + pltpu.roll(dt * tlo, half, 1) + pltpu.roll(dt * thi, LANES - half, 1)


def _pack_dproj(dqs, dks, dvs, du, dgpre, tabs):
    tm = 256

    def body(*refs):
        dq_refs, dk_refs, dv_refs = refs[0:3], refs[3:6], refs[6:9]
        du_ref, dg_ref, tc_ref, tlo_ref, thi_ref, o_ref, stage = refs[9:16]
        n_ch = QKV_W // LANES
        halves = GROUP_W // LANES
        for grp, d in enumerate(DILATIONS):
            for which, src in enumerate((dq_refs[grp], dk_refs[grp], dv_refs[grp])):
                for res in range(d):
                    for half in range(halves):
                        _scatter_residue(stage, which * (n_ch // 3) + grp * halves + half, res, d, tm // d,
                                         src[:, _lane_chunk(res * halves + half)])
        tc, tlo, thi = tc_ref[...], tlo_ref[...], thi_ref[...]
        for ch in range(n_ch):
            piece = stage[ch]
            o_ref[:, _lane_chunk(ch)] = (_rope_transpose(piece, tc, tlo, thi) if ch < 2 * n_ch // 3 else piece).astype(BF16)
        o_ref[:, QKV_W:QKV_W + SSM_W] = du_ref[...].astype(BF16)
        o_ref[:, QKV_W + SSM_W:] = dg_ref[...].astype(BF16)

    t = du.shape[0]
    ins = [("row", a) for a in (*dqs, *dks, *dvs, du, dgpre)] + [("tab", tb) for tb in tabs]
    return _rows(body, "pack_dproj", t, tm, ins, [("row", IN_W, BF16)],
                 scratch=[pltpu.VMEM((QKV_W // LANES, tm, LANES), F32)])[0]


def _merge_groups(o_refs, l_refs, a_ref, lt_ref, nat, tm):
    halves = GROUP_W // LANES
    for grp, d in enumerate(DILATIONS[1:], start=1):
        for j, src in enumerate((o_refs[grp], l_refs[grp])):
            for res in range(d):
                for half in range(halves):
                    _scatter_residue(nat, (grp - 1) * 4 + j * 2 + half, res, d, tm // d,
                                     src[:, _lane_chunk(res * halves + half)])
    for half in range(halves):
        sl = _lane_chunk(half)
        la, lb, lc = l_refs[0][:, sl], nat[2 + half], nat[6 + half]
        m = jnp.maximum(jnp.maximum(la, lb), lc)
        ea, eb, ec = jnp.exp(la - m), jnp.exp(lb - m), jnp.exp(lc - m)
        ssum = ea + eb + ec
        a_ref[:, sl] = (ea / ssum) * o_refs[0][:, sl] + (eb / ssum) * nat[half] + (ec / ssum) * nat[4 + half]
        lt_ref[:, sl] = m + jnp.log(ssum)


def _head_sum_matrix():
    r = jnp.arange(GROUP_W) // HEAD_DIM
    return (r[:, None] == r[None, :]).astype(F32)


def _attention_cotangents(da, attn, lt, ones, rd_ref, dil, stage, tm):
    halves = GROUP_W // LANES
    rd = jnp.dot(da * attn, ones, preferred_element_type=F32, precision=lax.Precision.HIGHEST)
    rd_ref[...] = rd
    for half in range(halves):
        for j, val in enumerate((da, lt, rd)):
            stage[2 * j + half] = val[:, _lane_chunk(half)]
    for grp, d in enumerate(DILATIONS[1:], start=1):
        for j in range(3):
            for res in range(d):
                for half in range(halves):
                    dil[3 * (grp - 1) + j][:, _lane_chunk(res * halves + half)] = _gather_residue(
                        stage, 2 * j + half, res, d, tm // d)


_GELU_C = math.sqrt(2.0 / math.pi)


def _ssm_act_bwd(dyg, ytot, u_perm, dskip):
    def body(dyg_ref, yt_ref, u_ref, d_ref, dy_ref, dus_ref, dd_ref):
        @pl.when(_first_step())
        def _():
            dd_ref[...] = jnp.zeros_like(dd_ref)

        yt = yt_ref[...]
        th = jnp.tanh(_GELU_C * (yt + 0.044715 * (yt * yt * yt)))
        dgelu = 0.5 * (1.0 + th) + 0.5 * yt * (1.0 - th * th) * _GELU_C * (1.0 + 3.0 * 0.044715 * yt * yt)
        dy = dyg_ref[...] * dgelu
        dy_ref[...] = dy.astype(BF16)
        dus_ref[...] = dy * d_ref[...]
        dd_ref[...] += jnp.sum(dy * u_ref[...], axis=0, keepdims=True)

    t = dyg.shape[0]
    return _rows(body, "ssm_act_bwd", t, 512, [("row", dyg), ("row", ytot), ("row", u_perm), ("const", dskip)],
                 [("row", SSM_W, BF16), ("row", SSM_W, F32), ("acc", (1, SSM_W), F32)])


def _head_masks():
    lane = lax.broadcasted_iota(jnp.int32, (1, GROUP_W), 1)
    return [(lane // HEAD_DIM) == h for h in range(HEADS_PER_GROUP)]


def _stack_heads(blk, masks, fill=0.0):
    return jnp.concatenate([jnp.where(mk, blk, jnp.full_like(blk, fill)) for mk in masks], axis=0)


def _unstack_heads(stacked, masks):
    rows = stacked.shape[0] // len(masks)
    out = stacked[:rows]
    for h in range(1, len(masks)):
        out = jnp.where(masks[h], stacked[h * rows:(h + 1) * rows], out)
    return out


def _band_mask(first):
    nk = ATT_BLOCK if first else 2 * ATT_BLOCK
    qi = lax.broadcasted_iota(jnp.int32, (ATT_BLOCK, nk), 0)
    ki = lax.broadcasted_iota(jnp.int32, (ATT_BLOCK, nk), 1)
    dist = qi - ki + (0 if first else ATT_BLOCK)
    return (dist >= 0) & (dist <= ATT_BLOCK)


_NT = (((1,), (1,)), ((), ()))
_TN = (((0,), (0,)), ((), ()))


def _residues_per_step(d):
    return min(d, 4)


def _attn_fwd(q, k, v, group, n_samples, comm=None):
    d = DILATIONS[group]
    length = SEQ // d
    nb = length // ATT_BLOCK

    rps = _residues_per_step(d)

    def body(q_ref, k_ref, v_ref, o_ref, l_ref):
        for rl in range(rps):
            residue(q_ref, k_ref, v_ref, o_ref, l_ref, slice(rl * GROUP_W, (rl + 1) * GROUP_W))

    def residue(q_ref, k_ref, v_ref, o_ref, l_ref, cols):
        masks = _head_masks()

        def block(qs, ks, first):
            nk = ATT_BLOCK if first else 2 * ATT_BLOCK
            qb = q_ref[0, pl.ds(qs, ATT_BLOCK), cols]
            kc = k_ref[0, pl.ds(ks, nk), cols]
            vc = v_ref[0, pl.ds(ks, nk), cols]
            q4 = _stack_heads(qb, masks)
            valid = jnp.tile(_band_mask(first), (HEADS_PER_GROUP, 1))
            s = lax.dot_general(q4, kc, _NT, preferred_element_type=F32) * (HEAD_DIM ** -0.5)
            s = jnp.where(valid, s, NEG_INF)
            m = jnp.max(s, axis=-1, keepdims=True)
            p = jnp.exp(s - m)
            l = jnp.sum(p, axis=-1, keepdims=True)
            o4 = jnp.dot(p.astype(MXU_DTYPE), vc, preferred_element_type=F32) / l
            lse4 = jnp.broadcast_to(m + jnp.log(l), o4.shape)
            o_ref[0, pl.ds(qs, ATT_BLOCK), cols] = _unstack_heads(o4, masks)
            l_ref[0, pl.ds(qs, ATT_BLOCK), cols] = _unstack_heads(lse4, masks)

        block(0, 0, True)
        if nb > 1:
            def loop(n, carry):
                block(pl.multiple_of(n * ATT_BLOCK, ATT_BLOCK), pl.multiple_of((n - 1) * ATT_BLOCK, ATT_BLOCK), False)
                return carry

            lax.fori_loop(1, nb, loop, 0)

    per_sample = lambda a: a.reshape(n_samples, length, d * GROUP_W)
    spec = pl.BlockSpec((1, length, rps * GROUP_W), lambda b, r: (b, 0, r))
    shp = jax.ShapeDtypeStruct((n_samples, length, d * GROUP_W), F32)
    o, lse, *carried = _grid_call(body, f"attn_fwd_g{group}", (n_samples, d // rps), [per_sample(a) for a in (q, k, v)],
                                  [spec] * 3, [spec] * 2, [shp, shp], VMEM_MID, comm)
    flat = lambda a: a.reshape(n_samples * length, d * GROUP_W)
    return flat(o), flat(lse), carried


def _attn_bwd(q, k, v, dattn, lse_tot, rowdot, group, n_samples, comm=None):
    d = DILATIONS[group]
    length = SEQ // d
    nb = length // ATT_BLOCK

    rps = _residues_per_step(d)

    def body(q_ref, k_ref, v_ref, da_ref, lt_ref, rd_ref, dq_ref, dk_ref, dv_ref):
        dk_ref[...] = jnp.zeros_like(dk_ref)
        dv_ref[...] = jnp.zeros_like(dv_ref)
        for rl in range(rps):
            residue(q_ref, k_ref, v_ref, da_ref, lt_ref, rd_ref, dq_ref, dk_ref, dv_ref,
                    slice(rl * GROUP_W, (rl + 1) * GROUP_W))

    def residue(q_ref, k_ref, v_ref, da_ref, lt_ref, rd_ref, dq_ref, dk_ref, dv_ref, cols):
        masks = _head_masks()

        def block(qs, ks, first):
            nk = ATT_BLOCK if first else 2 * ATT_BLOCK
            qb = q_ref[0, pl.ds(qs, ATT_BLOCK), cols]
            kc = k_ref[0, pl.ds(ks, nk), cols]
            vc = v_ref[0, pl.ds(ks, nk), cols]
            da = da_ref[0, pl.ds(qs, ATT_BLOCK), cols]
            lt = lt_ref[0, pl.ds(qs, ATT_BLOCK), cols]
            rd = rd_ref[0, pl.ds(qs, ATT_BLOCK), cols]
            q4 = _stack_heads(qb, masks)
            da4 = _stack_heads(da, masks).astype(MXU_DTYPE)
            lt4 = jnp.max(_stack_heads(lt, masks, -jnp.inf), axis=-1, keepdims=True)
            rd4 = jnp.max(_stack_heads(rd, masks, -jnp.inf), axis=-1, keepdims=True)
            valid = jnp.tile(_band_mask(first), (HEADS_PER_GROUP, 1))
            s = lax.dot_general(q4, kc, _NT, preferred_element_type=F32) * (HEAD_DIM ** -0.5)
            s = jnp.where(valid, s, NEG_INF)
            p = jnp.exp(s - lt4)
            dp = lax.dot_general(da4, vc, _NT, preferred_element_type=F32)
            ds = (p * (dp - rd4) * (HEAD_DIM ** -0.5)).astype(MXU_DTYPE)
            dq_ref[0, pl.ds(qs, ATT_BLOCK), cols] = _unstack_heads(jnp.dot(ds, kc, preferred_element_type=F32), masks)
            dk_ref[0, pl.ds(ks, nk), cols] += lax.dot_general(ds, q4, _TN, preferred_element_type=F32)
            dv_ref[0, pl.ds(ks, nk), cols] += lax.dot_general(p.astype(MXU_DTYPE), da4, _TN, preferred_element_type=F32)

        block(0, 0, True)
        if nb > 1:
            def loop(n, carry):
                block(pl.multiple_of(n * ATT_BLOCK, ATT_BLOCK), pl.multiple_of((n - 1) * ATT_BLOCK, ATT_BLOCK), False)
                return carry

            lax.fori_loop(1, nb, loop, 0)

    per_sample = lambda a: a.reshape(n_samples, length, d * GROUP_W)
    spec = pl.BlockSpec((1, length, rps * GROUP_W), lambda b, r: (b, 0, r))
    shp = jax.ShapeDtypeStruct((n_samples, length, d * GROUP_W), F32)
    dq, dk, dv, *carried = _grid_call(
        body, f"attn_bwd_g{group}", (n_samples, d // rps), [per_sample(a) for a in (q, k, v, dattn, lse_tot, rowdot)],
        [spec] * 6, [spec] * 3, [shp, shp, shp], VMEM_MID, comm)
    flat = lambda a: a.reshape(n_samples * length, d * GROUP_W)
    return flat(dq), flat(dk), flat(dv), carried


def _disc(lr, li, ldt, br, bi):
    dt = jnp.exp(ldt)
    mag = jnp.exp(lr * dt)
    ab_re, ab_im = mag * jnp.cos(li * dt), mag * jnp.sin(li * dt)
    den = lr * lr + li * li
    nr, ni = ab_re - 1.0, ab_im
    f_re = (nr * lr + ni * li) / den
    f_im = (ni * lr - nr * li) / den
    return ab_re, ab_im, f_re * br - f_im * bi, f_re * bi + f_im * br


def _state_mask():
    row_g = lax.broadcasted_iota(jnp.int32, (SCAN_CH, SCAN_WC), 0) // SSM_CH
    col_g = lax.broadcasted_iota(jnp.int32, (SCAN_CH, SCAN_WC), 1) // SSM_STATE
    return row_g == col_g


def _ssm_disc(lr, li, ldt, br, bi, cr, ci):
    w = SCAN_WC

    def body(lr_ref, li_ref, ldt_ref, br_ref, bi_ref, cr_ref, ci_ref, a_ref, bb_ref, c_ref):
        ar, ai, bbr, bbi = _disc(lr_ref[...], li_ref[...], ldt_ref[...], br_ref[...], bi_ref[...])
        crv, civ = cr_ref[...], ci_ref[...]
        mask = _state_mask()
        for cb in range(SCAN_NBLK):
            sl = slice(cb * w, (cb + 1) * w)
            rows = slice(cb * SCAN_CH, (cb + 1) * SCAN_CH)
            dense = lambda comp: jnp.where(mask, jnp.tile(comp[:, sl], (SCAN_CH // SSM_CH, 1)), 0.0)
            a_ref[:, 2 * cb * w:(2 * cb + 1) * w] = ar[:, sl]
            a_ref[:, (2 * cb + 1) * w:(2 * cb + 2) * w] = ai[:, sl]
            bb_ref[rows, :w] = dense(bbr).astype(MXU_DTYPE)
            bb_ref[rows, w:] = dense(bbi).astype(MXU_DTYPE)
            c_ref[rows, :w] = dense(crv).astype(MXU_DTYPE)
            c_ref[rows, w:] = (-dense(civ)).astype(MXU_DTYPE)

    return _pallas_call(
        body, name="ssm_disc",
        out_shape=[jax.ShapeDtypeStruct((1, 2 * N_STATE), F32), jax.ShapeDtypeStruct((SSM_W, 2 * w), MXU_DTYPE),
                   jax.ShapeDtypeStruct((SSM_W, 2 * w), MXU_DTYPE)],
        compiler_params=pltpu.CompilerParams(vmem_limit_bytes=VMEM_MID),
    )(lr, li, ldt, br, bi, cr, ci)


def _group_indicator():
    s = jnp.arange(N_STATE) // SSM_STATE
    return (s[:, None] == jnp.arange(LANES)[None, :]).astype(F32)


def _ssm_param_bwd(lr, li, ldt, br, bi, da_cat, dbb_full, dc_full):
    w = SCAN_WC

    def body(lr_ref, li_ref, ldt_ref, br_ref, bi_ref, da_ref, dbb_ref, dc_ref, ind_ref,
             glr_ref, gli_ref, gldt_ref, gbr_ref, gbi_ref, gcr_ref, gci_ref):
        mask = _state_mask()

        def diag_parts(ref):
            res = ([], [])
            for cb in range(SCAN_NBLK):
                for part in range(2):
                    blk = ref[cb * SCAN_CH:(cb + 1) * SCAN_CH, part * w:(part + 1) * w]
                    res[part].append(jnp.sum(jnp.where(mask, blk, 0.0).reshape(SCAN_CH // SSM_CH, SSM_CH, w), axis=0))
            return jnp.concatenate(res[0], axis=1), jnp.concatenate(res[1], axis=1)

        dar = jnp.concatenate([da_ref[:, 2 * cb * w:(2 * cb + 1) * w] for cb in range(SCAN_NBLK)], axis=1)
        dai = jnp.concatenate([da_ref[:, (2 * cb + 1) * w:(2 * cb + 2) * w] for cb in range(SCAN_NBLK)], axis=1)
        dbbr, dbbi = diag_parts(dbb_ref)
        dcr, dci_neg = diag_parts(dc_ref)
        gcr_ref[...] = dcr
        gci_ref[...] = -dci_neg
        _, vjp = jax.vjp(_disc, lr_ref[...], li_ref[...], ldt_ref[...], br_ref[...], bi_ref[...])
        glr, gli, gldt, gbr, gbi = vjp((dar, dai, dbbr, dbbi))
        glr_ref[...] = glr
        gli_ref[...] = gli
        gldt_ref[...] = jnp.dot(jnp.broadcast_to(gldt, (8, N_STATE)), ind_ref[...], preferred_element_type=F32,
                                precision=lax.Precision.HIGHEST)
        gbr_ref[...] = gbr
        gbi_ref[...] = gbi

    v1 = jax.ShapeDtypeStruct((1, N_STATE), F32)
    v16 = jax.ShapeDtypeStruct((SSM_CH, N_STATE), F32)
    vdt = jax.ShapeDtypeStruct((8, LANES), F32)
    return _pallas_call(
        body, name="ssm_param_bwd", out_shape=[v1, v1, vdt, v16, v16, v16, v16],
        compiler_params=pltpu.CompilerParams(vmem_limit_bytes=VMEM_BIG),
    )(lr, li, ldt, br, bi, da_cat, dbb_full, dc_full, _group_indicator())


def _cmul(ar, ai, br, bi):
    return ar * br - ai * bi, ar * bi + ai * br


def _gelu_tanh(y):
    return jnp.tanh(_GELU_C * (y + 0.044715 * (y * y * y)))


def _segment_carry(er, ei, ar, ai, n_rows, reverse):
    qr, qi = ar, ai
    for _ in range(int(math.log2(SCAN_LEN))):
        qr, qi = _cmul(qr, qi, qr, qi)
    seg = lax.broadcasted_iota(jnp.int32, er.shape, 0) % SCAN_SEG_PER_SAMPLE
    shift = 1
    while shift < SCAN_SEG_PER_SAMPLE:
        keep = (seg < SCAN_SEG_PER_SAMPLE - shift) if reverse else (seg >= shift)
        amount = n_rows - shift if reverse else shift
        sr = jnp.where(keep, pltpu.roll(er, amount, 0), 0.0)
        si = jnp.where(keep, pltpu.roll(ei, amount, 0), 0.0)
        if reverse:
            er, ei = er + qr * sr + qi * si, ei + qr * si - qi * sr
        else:
            er, ei = er + qr * sr - qi * si, ei + qr * si + qi * sr
        qr, qi = _cmul(qr, qi, qr, qi)
        shift *= 2
    keep = (seg < SCAN_SEG_PER_SAMPLE - 1) if reverse else (seg >= 1)
    amount = n_rows - 1 if reverse else 1
    return jnp.where(keep, pltpu.roll(er, amount, 0), 0.0), jnp.where(keep, pltpu.roll(ei, amount, 0), 0.0)


def _ssm_fwd(u_perm, a_cat, bbc, cc, dskip, n_rows):
    t = u_perm.shape[0]
    w = SCAN_WC
    rows_c = SCAN_CHUNK * n_rows
    n_chunks = t // rows_c

    assert n_chunks % 2 == 0

    def body(u_ref, a_ref, bb_ref, c_ref, d_ref, yt_ref, yg_ref, ein_ref, bu_all, st_a, st_b, xs_a, xs_b):
        ar = jnp.broadcast_to(a_ref[:, :w], (n_rows, w))
        ai = jnp.broadcast_to(a_ref[:, w:], (n_rows, w))
        start = lambda ch: pl.multiple_of(ch * rows_c, rows_c)

        def project(ch, stage):
            res = jnp.dot(u_ref[pl.ds(start(ch), rows_c), :].astype(MXU_DTYPE), bb_ref[...], preferred_element_type=F32)
            stage[...] = res
            bu_all[pl.ds(start(ch), rows_c), :] = res

        def steps(src, r0, carry, xs=None):
            for i in range(SCAN_CHUNK):
                blk = src[pl.ds(r0 + i * n_rows, n_rows), :]
                carry = (ar * carry[0] - ai * carry[1] + blk[:, :w], ar * carry[1] + ai * carry[0] + blk[:, w:])
                if xs is not None:
                    xs[i * n_rows:(i + 1) * n_rows, :w] = carry[0]
                    xs[i * n_rows:(i + 1) * n_rows, w:] = carry[1]
            return carry

        def emit(xs, ch):
            y = lax.dot_general(xs[...].astype(MXU_DTYPE), c_ref[...], _NT, preferred_element_type=F32)
            yt = y + d_ref[...] * u_ref[pl.ds(start(ch), rows_c), :]
            yt_ref[pl.ds(start(ch), rows_c), :] = yt
            yg_ref[pl.ds(start(ch), rows_c), :] = (0.5 * yt * (1.0 + _gelu_tanh(yt))).astype(BF16)

        project(0, st_a)

        def pair1(p, carry):
            project(2 * p + 1, st_b)
            carry = steps(st_a, 0, carry)
            project(jnp.minimum(2 * p + 2, n_chunks - 1), st_a)
            return steps(st_b, 0, carry)

        zero = jnp.zeros((n_rows, w), F32)
        er, ei = lax.fori_loop(0, n_chunks // 2, pair1, (zero, zero))
        cr, ci = _segment_carry(er, ei, ar, ai, n_rows, False)
        ein_ref[:, :w] = cr
        ein_ref[:, w:] = ci

        xs_b[...] = jnp.zeros_like(xs_b)

        def pair2(p, carry):
            emit(xs_b, jnp.maximum(2 * p - 1, 0))
            carry = steps(bu_all, start(2 * p), carry, xs_a)
            emit(xs_a, 2 * p)
            return steps(bu_all, start(2 * p + 1), carry, xs_b)

        lax.fori_loop(0, n_chunks // 2, pair2, (cr, ci))
        emit(xs_b, n_chunks - 1)

    col = lambda width: pl.BlockSpec((t, width), lambda c: (0, c))
    wgt = pl.BlockSpec((SCAN_CH, 2 * w), lambda c: (c, 0))
    return _pallas_call(
        body, name="ssm_fwd", grid=(SCAN_NBLK,),
        in_specs=[col(SCAN_CH), pl.BlockSpec((1, 2 * w), lambda c: (0, c)), wgt, wgt,
                  pl.BlockSpec((1, SCAN_CH), lambda c: (0, c))],
        out_specs=[col(SCAN_CH), col(SCAN_CH), pl.BlockSpec((n_rows, 2 * w), lambda c: (0, c))],
        out_shape=[jax.ShapeDtypeStruct((t, SSM_W), F32), jax.ShapeDtypeStruct((t, SSM_W), BF16),
                   jax.ShapeDtypeStruct((n_rows, 2 * N_STATE), F32)],
        scratch_shapes=[pltpu.VMEM((t, 2 * w), F32)] + [pltpu.VMEM((rows_c, 2 * w), F32)] * 4,
        compiler_params=pltpu.CompilerParams(dimension_semantics=("parallel",), vmem_limit_bytes=VMEM_BIG),
    )(u_perm, a_cat, bbc, cc, dskip)


def _ssm_bwd(u_perm, dypre, du_skip, a_cat, bbc, cc, ein, n_rows, comm=None):
    t = u_perm.shape[0]
    w = SCAN_WC
    rows_c = SCAN_CHUNK * n_rows
    n_chunks = t // rows_c

    assert n_chunks % 2 == 0
    last = n_chunks - 1

    def body(u_ref, dy_ref, dus_ref, a_ref, bb_ref, c_ref, ein_ref, du_ref, da_ref, dbb_ref, dc_ref,
             xs_all, st_a, st_b, buf_a, buf_b):
        ar = jnp.broadcast_to(a_ref[:, :w], (n_rows, w))
        ai = jnp.broadcast_to(a_ref[:, w:], (n_rows, w))
        zero = jnp.zeros((n_rows, w), F32)
        start = lambda ch: pl.multiple_of(ch * rows_c, rows_c)
        dbb_ref[...] = jnp.zeros_like(dbb_ref)
        dc_ref[...] = jnp.zeros_like(dc_ref)
        da_ref[...] = jnp.zeros_like(da_ref)

        xs_all[0:n_rows, :] = ein_ref[...]

        def project(ch, stage):
            stage[...] = jnp.dot(u_ref[pl.ds(start(ch), rows_c), :].astype(MXU_DTYPE), bb_ref[...],
                                 preferred_element_type=F32)

        def fwd_steps(stage, ch, carry, xs):
            for i in range(SCAN_CHUNK):
                blk = stage[i * n_rows:(i + 1) * n_rows, :]
                carry = (ar * carry[0] - ai * carry[1] + blk[:, :w], ar * carry[1] + ai * carry[0] + blk[:, w:])
                for half, val in enumerate(carry):
                    xs[i * n_rows:(i + 1) * n_rows, half * w:(half + 1) * w] = val
                    xs_all[pl.ds(start(ch) + (i + 1) * n_rows, n_rows), half * w:(half + 1) * w] = val
            return carry

        def add_dc(xs, ch):
            dc_ref[...] += lax.dot_general(dy_ref[pl.ds(start(ch), rows_c), :], xs[...].astype(MXU_DTYPE), _TN,
                                           preferred_element_type=F32)

        project(0, st_a)

        def fwd_pair(p, carry):
            project(2 * p + 1, st_b)
            carry = fwd_steps(st_a, 2 * p, carry, buf_a)
            add_dc(buf_a, 2 * p)
            project(jnp.minimum(2 * p + 2, last), st_a)
            carry = fwd_steps(st_b, 2 * p + 1, carry, buf_b)
            add_dc(buf_b, 2 * p + 1)
            return carry

        lax.fori_loop(0, n_chunks // 2, fwd_pair, (ein_ref[:, :w], ein_ref[:, w:]))

        def project_dx(ch, stage):
            stage[...] = jnp.dot(dy_ref[pl.ds(start(ch), rows_c), :], c_ref[...], preferred_element_type=F32)

        def back_steps(stage, carry, g_buf=None):
            for i in reversed(range(SCAN_CHUNK)):
                blk = stage[i * n_rows:(i + 1) * n_rows, :]
                carry = (blk[:, :w] + ar * carry[0] + ai * carry[1], blk[:, w:] + ar * carry[1] - ai * carry[0])
                if g_buf is not None:
                    g_buf[i * n_rows:(i + 1) * n_rows, :w] = carry[0]
                    g_buf[i * n_rows:(i + 1) * n_rows, w:] = carry[1]
            return carry

        def first_pair(p, carry):
            project_dx(last - 2 * p - 1, st_b)
            carry = back_steps(st_a, carry)
            project_dx(jnp.maximum(last - 2 * p - 2, 0), st_a)
            return back_steps(st_b, carry)

        project_dx(last, st_a)
        sr, si = lax.fori_loop(0, n_chunks // 2, first_pair, (zero, zero))
        gr0, gi0 = _segment_carry(sr, si, ar, ai, n_rows, True)

        def post(g_buf, ch):
            g = g_buf[...]
            xp = xs_all[pl.ds(start(ch), rows_c), :]
            da_ref[:, :w] += jnp.sum(g[:, :w] * xp[:, :w] + g[:, w:] * xp[:, w:], axis=0, keepdims=True)
            da_ref[:, w:] += jnp.sum(g[:, w:] * xp[:, :w] - g[:, :w] * xp[:, w:], axis=0, keepdims=True)
            gb = g.astype(MXU_DTYPE)
            du_ref[pl.ds(start(ch), rows_c), :] = (lax.dot_general(gb, bb_ref[...], _NT, preferred_element_type=F32)
                                                   + dus_ref[pl.ds(start(ch), rows_c), :])
            dbb_ref[...] += lax.dot_general(u_ref[pl.ds(start(ch), rows_c), :].astype(MXU_DTYPE), gb, _TN,
                                            preferred_element_type=F32)

        def second_pair(p, carry):
            c1 = last - 2 * p
            project_dx(c1 - 1, st_b)
            post(buf_b, jnp.minimum(c1 + 1, last))
            carry = back_steps(st_a, carry, buf_a)
            project_dx(jnp.maximum(c1 - 2, 0), st_a)
            post(buf_a, c1)
            return back_steps(st_b, carry, buf_b)

        project_dx(last, st_a)
        buf_b[...] = jnp.zeros_like(buf_b)
        lax.fori_loop(0, n_chunks // 2, second_pair, (gr0, gi0))
        post(buf_b, 0)

    col = lambda width: pl.BlockSpec((t, width), lambda c, j: (0, c))
    wgt = pl.BlockSpec((SCAN_CH, 2 * w), lambda c, j: (c, 0))
    row = pl.BlockSpec((1, 2 * w), lambda c, j: (0, c))
    return _grid_call(
        body, "ssm_bwd", (SCAN_NBLK, 1), [u_perm, dypre, du_skip, a_cat, bbc, cc, ein],
        [col(SCAN_CH), col(SCAN_CH), col(SCAN_CH), row, wgt, wgt, pl.BlockSpec((n_rows, 2 * w), lambda c, j: (0, c))],
        [col(SCAN_CH), row, wgt, wgt],
        [jax.ShapeDtypeStruct((t, SSM_W), F32), jax.ShapeDtypeStruct((1, 2 * N_STATE), F32),
         jax.ShapeDtypeStruct((SSM_W, 2 * w), F32), jax.ShapeDtypeStruct((SSM_W, 2 * w), F32)],
        56 * 1024 * 1024, comm,
        scratch=[pltpu.VMEM((t + n_rows, 2 * w), F32)] + [pltpu.VMEM((rows_c, 2 * w), F32)] * 4)


def _to_scan_rows(a, n_samples):
    c = a.shape[1]
    return a.reshape(n_samples, SCAN_SEG_PER_SAMPLE, SCAN_LEN, c).transpose(2, 0, 1, 3).reshape(-1, c)


def _from_scan_rows(a, n_samples):
    c = a.shape[1]
    return a.reshape(SCAN_LEN, n_samples, SCAN_SEG_PER_SAMPLE, c).transpose(1, 2, 0, 3).reshape(-1, c)


def _row_spec(tm, width):
    return pl.BlockSpec((tm, width), lambda i, j: (i, 0))


def _whole(arr):
    return pl.BlockSpec(arr.shape, lambda i, j: (0,) * arr.ndim)


def _proj_rope(x, g, w_in_t, tabs, comm=None):
    t = x.shape[0]
    tm = 256

    def body(x_ref, g_ref, w_ref, tc_ref, tlo_ref, thi_ref, h_ref, u_ref, gate_ref, *rest):
        qkv_refs, stage = rest[:9], rest[9]
        xv = x_ref[...]
        r = lax.rsqrt(jnp.mean(xv * xv, axis=-1, keepdims=True) + RMS_EPS)
        h = ((xv * r) * g_ref[...]).astype(BF16)
        h_ref[...] = h
        p = lax.dot_general(h.astype(MXU_DTYPE), w_ref[...], _NT, preferred_element_type=F32)
        u_ref[...] = p[:, QKV_W:QKV_W + SSM_W]
        gate_ref[...] = _sigmoid(p[:, QKV_W + SSM_W:])
        tc, tlo, thi = tc_ref[...], tlo_ref[...], thi_ref[...]
        n_ch = QKV_W // LANES
        for ch in range(n_ch):
            piece = p[:, _lane_chunk(ch)]
            stage[ch] = _rope_apply(piece, tc, tlo, thi) if ch < 2 * n_ch // 3 else piece
        halves = GROUP_W // LANES
        for grp, d in enumerate(DILATIONS):
            for which in range(3):
                out = qkv_refs[3 * grp + which]
                for res in range(d):
                    for half in range(halves):
                        ch = which * (n_ch // 3) + grp * halves + half
                        out[:, _lane_chunk(res * halves + half)] = _gather_residue(stage, ch, res, d, tm // d).astype(BF16)

    tab = pl.BlockSpec((tm, LANES), lambda i, j: (i % (SEQ // tm), 0))
    widths = [(D_MODEL, BF16), (SSM_W, F32), (2 * D_MODEL, F32)]
    out_specs = [_row_spec(tm, wd) for wd, _ in widths]
    out_shapes = [jax.ShapeDtypeStruct((t, wd), dt) for wd, dt in widths]
    for d in DILATIONS:
        out_specs += [_row_spec(tm // d, d * GROUP_W)] * 3
        out_shapes += [jax.ShapeDtypeStruct((t // d, d * GROUP_W), BF16)] * 3
    return _grid_call(
        body, "proj_rope", (t // tm, 1), [x, g, w_in_t, *tabs],
        [_row_spec(tm, D_MODEL), _whole(g), _whole(w_in_t), tab, tab, tab], out_specs, out_shapes, VMEM_BIG, comm,
        scratch=[pltpu.VMEM((QKV_W // LANES, tm, LANES), F32)])


def _branch_outputs(attn_ref, yg_ref, wao_ref, wglu_ref):
    attn_d = lax.dot_general(attn_ref[...].astype(MXU_DTYPE), wao_ref[...], _NT, preferred_element_type=F32)
    z = lax.dot_general(yg_ref[...].astype(MXU_DTYPE), wglu_ref[...], _NT, preferred_element_type=F32)
    return attn_d, z[:, :D_MODEL], _sigmoid(z[:, D_MODEL:])


def _mix_out_rms(os_, lses, yg, gates, x, w_ao_t, w_glu_t, w_out, g, comm=None):
    t = x.shape[0]
    tm = 256

    def body(o0, o1, o2, l0, l1, l2, yg_ref, gate_ref, x_ref, wao_ref, wglu_ref, wout_ref, g_ref,
             attn_ref, lt_ref, m_ref, x1_ref, h_ref, nat):
        _merge_groups((o0, o1, o2), (l0, l1, l2), attn_ref, lt_ref, nat, tm)
        attn_d, za, sb = _branch_outputs(attn_ref, yg_ref, wao_ref, wglu_ref)
        merged = (gate_ref[:, :D_MODEL] * attn_d + gate_ref[:, D_MODEL:] * (za * sb)).astype(BF16)
        m_ref[...] = merged
        x1 = x_ref[...] + jnp.dot(merged.astype(MXU_DTYPE), wout_ref[...], preferred_element_type=F32)
        x1_ref[...] = x1
        r = lax.rsqrt(jnp.mean(x1 * x1, axis=-1, keepdims=True) + RMS_EPS)
        h_ref[...] = ((x1 * r) * g_ref[...]).astype(BF16)

    dil_specs = [_row_spec(tm // d, d * GROUP_W) for d in DILATIONS] * 2
    return _grid_call(
        body, "mix_out_rms", (t // tm, 1), [*os_, *lses, yg, gates, x, w_ao_t, w_glu_t, w_out, g],
        dil_specs + [_row_spec(tm, SSM_W), _row_spec(tm, 2 * D_MODEL), _row_spec(tm, D_MODEL),
                     _whole(w_ao_t), _whole(w_glu_t), _whole(w_out), _whole(g)],
        [_row_spec(tm, GROUP_W)] * 2 + [_row_spec(tm, D_MODEL)] * 3,
        [jax.ShapeDtypeStruct((t, GROUP_W), F32)] * 2
        + [jax.ShapeDtypeStruct((t, D_MODEL), BF16), jax.ShapeDtypeStruct((t, D_MODEL), F32),
           jax.ShapeDtypeStruct((t, D_MODEL), BF16)], VMEM_BIG, comm, scratch=[pltpu.VMEM((8, tm, LANES), F32)])


def _mix_bwd(dx1b, attn, lse_tot, yg, gates, w_ao_t, w_glu_t, w_out, comm=None):
    t = dx1b.shape[0]
    tm = 256

    def body(dx_ref, attn_ref, lt_ref, yg_ref, gate_ref, wao_ref, wglu_ref, wout_ref, ones_ref,
             dad_ref, dz_ref, dg_ref, da_ref, dyg_ref, rd_ref, *rest):
        dm = lax.dot_general(dx_ref[...], wout_ref[...], _NT, preferred_element_type=F32)
        attn_d, za, sb = _branch_outputs(attn_ref, yg_ref, wao_ref, wglu_ref)
        g0, g1 = gate_ref[:, :D_MODEL], gate_ref[:, D_MODEL:]
        dad = (dm * g0).astype(BF16)
        dad_ref[...] = dad
        ds = dm * g1
        dza, dzb = (ds * sb).astype(BF16), (ds * za * sb * (1.0 - sb)).astype(BF16)
        dz_ref[:, :D_MODEL] = dza
        dz_ref[:, D_MODEL:] = dzb
        dg_ref[:, :D_MODEL] = (dm * attn_d * g0 * (1.0 - g0)).astype(BF16)
        dg_ref[:, D_MODEL:] = (dm * (za * sb) * g1 * (1.0 - g1)).astype(BF16)
        da = jnp.dot(dad.astype(MXU_DTYPE), wao_ref[...], preferred_element_type=F32)
        da_ref[...] = da
        dyg_ref[...] = (jnp.dot(dza.astype(MXU_DTYPE), wglu_ref[:D_MODEL, :], preferred_element_type=F32)
                        + jnp.dot(dzb.astype(MXU_DTYPE), wglu_ref[D_MODEL:, :], preferred_element_type=F32))
        _attention_cotangents(da, attn_ref[...], lt_ref[...], ones_ref[...], rd_ref, rest[:6], rest[6], tm)

    widths = [(D_MODEL, BF16), (2 * D_MODEL, BF16), (2 * D_MODEL, BF16), (GROUP_W, F32), (SSM_W, F32), (GROUP_W, F32)]
    out_specs = [_row_spec(tm, wd) for wd, _ in widths]
    out_shapes = [jax.ShapeDtypeStruct((t, wd), dt) for wd, dt in widths]
    for d in DILATIONS[1:]:
        out_specs += [_row_spec(tm // d, d * GROUP_W)] * 3
        out_shapes += [jax.ShapeDtypeStruct((t // d, d * GROUP_W), F32)] * 3
    ones = _head_sum_matrix()
    return _grid_call(
        body, "mix_bwd", (t // tm, 1), [dx1b, attn, lse_tot, yg, gates, w_ao_t, w_glu_t, w_out, ones],
        [_row_spec(tm, D_MODEL), _row_spec(tm, GROUP_W), _row_spec(tm, GROUP_W), _row_spec(tm, SSM_W),
         _row_spec(tm, 2 * D_MODEL), _whole(w_ao_t), _whole(w_glu_t), _whole(w_out), _whole(ones)],
        out_specs, out_shapes, VMEM_BIG, comm, scratch=[pltpu.VMEM((6, tm, LANES), F32)])


FFN_TN = D_FF // 2
MXU_COLS = 256


def _ffn_in_swiglu(h2, w_gate_t, w_up_t, comm=None):
    t = h2.shape[0]
    tm = 512

    def body(h_ref, wg_ref, wu_ref, a_ref, b_ref, f_ref):
        h = h_ref[...].astype(MXU_DTYPE)
        for c0 in range(0, FFN_TN, MXU_COLS):
            sl = slice(c0, min(c0 + MXU_COLS, FFN_TN))
            a = lax.dot_general(h, wg_ref[sl, :], _NT, preferred_element_type=F32)
            b = lax.dot_general(h, wu_ref[sl, :], _NT, preferred_element_type=F32)
            a_ref[:, sl] = a
            b_ref[:, sl] = b
            f_ref[:, sl] = (a * _sigmoid(a) * b).astype(BF16)

    tile = pl.BlockSpec((tm, FFN_TN), lambda j, i: (i, j))
    wspec = pl.BlockSpec((FFN_TN, D_MODEL), lambda j, i: (j, 0))
    return _grid_call(
        body, "ffn_in_swiglu", (D_FF // FFN_TN, t // tm), [h2, w_gate_t, w_up_t],
        [pl.BlockSpec((tm, D_MODEL), lambda j, i: (i, 0)), wspec, wspec],
        [tile] * 3, [jax.ShapeDtypeStruct((t, D_FF), F32)] * 2 + [jax.ShapeDtypeStruct((t, D_FF), BF16)], VMEM_BIG, comm)


def _ffn_down_final(f, w_down, x1, target, g):
    t = x1.shape[0]
    tm = 256

    def body(f_ref, w_ref, x1_ref, t_ref, g_ref, dx_ref, dxb_ref, loss_ref, gg_ref):
        @pl.when(pl.program_id(0) == 0)
        def _():
            loss_ref[...] = jnp.zeros_like(loss_ref)
            gg_ref[...] = jnp.zeros_like(gg_ref)

        xv = x1_ref[...] + jnp.dot(f_ref[...].astype(MXU_DTYPE), w_ref[...], preferred_element_type=F32)
        gv = g_ref[...]
        r = lax.rsqrt(jnp.mean(xv * xv, axis=-1, keepdims=True) + RMS_EPS)
        n = xv * r
        diff = n * gv - t_ref[...]
        per_tok = jnp.mean(diff * diff, axis=-1, keepdims=True)
        loss_ref[...] += 0.5 * jnp.sum(per_tok, axis=0, keepdims=True)
        dy = diff / xv.shape[-1]
        gg_ref[...] += jnp.sum(dy * n, axis=0, keepdims=True)
        dn = dy * gv
        dx = r * (dn - n * jnp.mean(dn * n, axis=-1, keepdims=True))
        dx_ref[...] = dx
        dxb_ref[...] = dx.astype(BF16)

    acc = lambda shp: pl.BlockSpec(shp, lambda i, j: (0, 0))
    return _grid_call(
        body, "ffn_down_final", (t // tm, 1), [f, w_down, x1, target, g],
        [_row_spec(tm, D_FF), _whole(w_down), _row_spec(tm, D_MODEL), _row_spec(tm, D_MODEL), _whole(g)],
        [_row_spec(tm, D_MODEL)] * 2 + [acc((8, LANES)), acc((1, D_MODEL))],
        [jax.ShapeDtypeStruct((t, D_MODEL), F32), jax.ShapeDtypeStruct((t, D_MODEL), BF16),
         jax.ShapeDtypeStruct((8, LANES), F32), jax.ShapeDtypeStruct((1, D_MODEL), F32)], VMEM_BIG, sequential=True)


def _d_f_swiglu_bwd(dx2b, w_down, a, b):
    t = a.shape[0]
    tm = 512

    def body(dx_ref, w_ref, a_ref, b_ref, da_ref, db_ref):
        d = lax.dot_general(dx_ref[...], w_ref[...], _NT, preferred_element_type=F32)
        av, bv = a_ref[...], b_ref[...]
        sg = _sigmoid(av)
        da_ref[...] = (d * bv * sg * (1.0 + av * (1.0 - sg))).astype(BF16)
        db_ref[...] = (d * av * sg).astype(BF16)

    tile = pl.BlockSpec((tm, FFN_TN), lambda j, i: (i, j))
    return _grid_call(
        body, "d_f_swiglu_bwd", (D_FF // FFN_TN, t // tm), [dx2b, w_down, a, b],
        [pl.BlockSpec((tm, D_MODEL), lambda j, i: (i, 0)), pl.BlockSpec((FFN_TN, D_MODEL), lambda j, i: (j, 0)), tile, tile],
        [tile] * 2, [jax.ShapeDtypeStruct((t, D_FF), BF16)] * 2, VMEM_BIG)


def _mm_rms_bwd(operands, weights, x, g, dres, name, comm=None):
    t = x.shape[0]
    tm = 256
    n_op = len(operands)

    def body(*refs):
        a_refs, w_refs = refs[:n_op], refs[n_op:2 * n_op]
        x_ref, g_ref, dres_ref, dx_ref, dxb_ref, gg_ref = refs[2 * n_op:]

        @pl.when(pl.program_id(0) == 0)
        def _():
            gg_ref[...] = jnp.zeros_like(gg_ref)

        dh = None
        for a_ref, w_ref in zip(a_refs, w_refs):
            part = jnp.dot(a_ref[...].astype(MXU_DTYPE), w_ref[...], preferred_element_type=F32)
            dh = part if dh is None else dh + part
        xv = x_ref[...]
        r = lax.rsqrt(jnp.mean(xv * xv, axis=-1, keepdims=True) + RMS_EPS)
        n = xv * r
        gg_ref[...] += jnp.sum(dh * n, axis=0, keepdims=True)
        dn = dh * g_ref[...]
        dx = dres_ref[...] + r * (dn - n * jnp.mean(dn * n, axis=-1, keepdims=True))
        dx_ref[...] = dx
        dxb_ref[...] = dx.astype(BF16)

    d = x.shape[1]
    return _grid_call(
        body, name, (t // tm, 1), [*operands, *weights, x, g, dres],
        [_row_spec(tm, a.shape[1]) for a in operands] + [_whole(wk) for wk in weights]
        + [_row_spec(tm, d), _whole(g), _row_spec(tm, d)],
        [_row_spec(tm, d)] * 2 + [pl.BlockSpec((1, d), lambda i, j: (0, 0))],
        [jax.ShapeDtypeStruct((t, d), F32), jax.ShapeDtypeStruct((t, d), BF16), jax.ShapeDtypeStruct((1, d), F32)],
        VMEM_BIG, comm, sequential=True)


def _flat_small(small):
    perm_b = lambda a: a.reshape(SSM_GROUPS, SSM_STATE, SSM_CH).transpose(2, 0, 1).reshape(SSM_CH, N_STATE)
    perm_c = lambda a: a.reshape(SSM_GROUPS, SSM_CH, SSM_STATE).transpose(1, 0, 2).reshape(SSM_CH, N_STATE)
    return dict(
        g_mix=small["norm_mix_g"].reshape(1, D_MODEL), g_ffn=small["norm_ffn_g"].reshape(1, D_MODEL),
        g_fin=small["norm_final_g"].reshape(1, D_MODEL),
        lr=small["ssm_a_re"].reshape(1, N_STATE), li=small["ssm_a_im"].reshape(1, N_STATE),
        ldt=jnp.repeat(small["ssm_log_dt"].reshape(SSM_GROUPS), SSM_STATE).reshape(1, N_STATE),
        br=perm_b(small["ssm_b_re"]), bi=perm_b(small["ssm_b_im"]),
        cr=perm_c(small["ssm_c_re"]), ci=perm_c(small["ssm_c_im"]), dskip=small["ssm_d"].reshape(1, SSM_W))


AG_HOSTS = {"proj_rope": ("w_glu", "w_attn_out", "w_out", "w_ffn_gate"), "mix_out_rms": ("w_ffn_up",),
            "ffn_in_swiglu": ("w_ffn_down",)}
HALVED = ("w_ffn_gate", "w_ffn_up", "w_in")
A2A_HOSTS = {"d_h2_rms": ("w_ffn_down",), "mix_bwd": ("w_ffn_gate:0", "w_out"), "attn_bwd_g0": ("w_ffn_up:1",),
             "attn_bwd_g1": ("w_glu",), "attn_bwd_g2": ("w_attn_out",), "ssm_bwd": ("w_ffn_gate:1", "w_ffn_up:0"),
             "mm_g_in1": ("w_in:0",), "d_h0_rms": ("w_in:1",)}
SMALL_HOST = "mm_g_in0"


def _local_step(x, target, w, small, shards=None):
    t = x.shape[0]
    n_samples = t // SEQ
    n_rows = n_samples * SCAN_SEG_PER_SAMPLE
    tabs = _rope_tables()
    w = dict(w)
    fs = _flat_small(small)
    g_mix, g_ffn, g_fin, dskip = fs["g_mix"], fs["g_ffn"], fs["g_fin"], fs["dskip"]
    a_cat, bbc, cc = _ssm_disc(fs["lr"], fs["li"], fs["ldt"], fs["br"], fs["bi"], fs["cr"], fs["ci"])
    big, recv, small_pack = {}, {}, []

    def comm_of(name):
        if shards is None:
            return None
        if name == SMALL_HOST:
            return _ag_comm([(small_pack[0], 0, 0)], [(N_DEV, *small_pack[0].shape)])
        if name in AG_HOSTS:
            names = AG_HOSTS[name]
            return _ag_comm([(shards[n], j, 0) for j, n in enumerate(names)], [(N_DEV, *shards[n].shape) for n in names])
        if name in A2A_HOSTS:
            return _a2a_comm([(big[n].reshape(N_DEV, -1, big[n].shape[1]), 0) for n in A2A_HOSTS[name]])
        return None

    def absorb(name, carried):
        if name == SMALL_HOST:
            recv["small"] = carried[0]
        for n, a3 in zip(AG_HOSTS.get(name, ()), carried):
            w[n] = a3.reshape(-1, a3.shape[2])
        for n, a3 in zip(A2A_HOSTS.get(name, ()), carried):
            recv[n] = a3

    def mm(a, b, mode, name, tm, tn, **kw):
        comm = comm_of(name)
        if comm is None:
            return _mm(a, b, mode, name, tm, tn, **kw)
        out, *carried = _mm(a, b, mode, name, tm, tn, comm=comm, **kw)
        absorb(name, carried)
        return out

    h0, u, gates, *rest = _proj_rope(x, g_mix, w["w_in"], tabs, comm_of("proj_rope"))
    qkv = [rest[3 * g:3 * g + 3] for g in range(3)]
    absorb("proj_rope", rest[9:])
    os_, lses = [], []
    for g in range(3):
        o_g, l_g, carried = _attn_fwd(*qkv[g], g, n_samples, comm_of(f"attn_fwd_g{g}"))
        absorb(f"attn_fwd_g{g}", carried)
        os_.append(o_g)
        lses.append(l_g)
    u_perm = _to_scan_rows(u, n_samples)
    ytot, yg_perm, ein = _ssm_fwd(u_perm, a_cat, bbc, cc, dskip, n_rows)
    yg = _from_scan_rows(yg_perm, n_samples)

    attn, lse_tot, merged, x1, h2, *carried = _mix_out_rms(os_, lses, yg, gates, x, w["w_attn_out"], w["w_glu"], w["w_out"],
                                                           g_ffn, comm_of("mix_out_rms"))
    absorb("mix_out_rms", carried)
    ffn_a, ffn_b, f, *carried = _ffn_in_swiglu(h2, w["w_ffn_gate"], w["w_ffn_up"], comm_of("ffn_in_swiglu"))
    absorb("ffn_in_swiglu", carried)
    dx2, dx2b, loss_blk, g_gfin = _ffn_down_final(f, w["w_ffn_down"], x1, target, g_fin)

    da, db = _d_f_swiglu_bwd(dx2b, w["w_ffn_down"], ffn_a, ffn_b)
    big["w_ffn_down"] = mm(f, dx2b, "tn", "mm_g_down", 256, D_MODEL, out_dtype=BF16)
    half = D_MODEL // 2
    for hf in range(2):
        big[f"w_ffn_gate:{hf}"] = mm(da, h2, "tn", f"mm_g_gate{hf}", 256, half, out_dtype=BF16, cols=(hf * half, half))
        big[f"w_ffn_up:{hf}"] = mm(db, h2, "tn", f"mm_g_up{hf}", 256, half, out_dtype=BF16, cols=(hf * half, half))
    dx1, dx1b, g_gffn, *carried = _mm_rms_bwd([da, db], [w["w_ffn_gate"], w["w_ffn_up"]], x1, g_ffn, dx2, "d_h2_rms",
                                              comm_of("d_h2_rms"))
    absorb("d_h2_rms", carried)

    big["w_out"] = mm(merged, dx1b, "tn", "mm_g_out", 256, D_MODEL, out_dtype=BF16)
    dattn_d, dz, dgpre, dattn, dyg, rowdot, *rest = _mix_bwd(dx1b, attn, lse_tot, yg, gates, w["w_attn_out"], w["w_glu"],
                                                             w["w_out"], comm_of("mix_bwd"))
    cot = [(dattn, lse_tot, rowdot), tuple(rest[:3]), tuple(rest[3:6])]
    absorb("mix_bwd", rest[6:])

    big["w_attn_out"] = mm(dattn_d, attn, "tn", "mm_g_attn_out", 512, GROUP_W, out_dtype=BF16)
    big["w_glu"] = mm(dz, yg, "tn", "mm_g_glu", 512, 512, out_dtype=BF16)
    dqs, dks, dvs = [], [], []
    for g in range(3):
        dq_g, dk_g, dv_g, carried = _attn_bwd(*qkv[g], *cot[g], g, n_samples, comm_of(f"attn_bwd_g{g}"))
        absorb(f"attn_bwd_g{g}", carried)
        dqs.append(dq_g)
        dks.append(dk_g)
        dvs.append(dv_g)

    dyg_perm = _to_scan_rows(dyg, n_samples)
    dypre, du_skip, g_dskip = _ssm_act_bwd(dyg_perm, ytot, u_perm, dskip)
    du_perm, da_cat, dbb_full, dc_full, *carried = _ssm_bwd(u_perm, dypre, du_skip, a_cat, bbc, cc, ein, n_rows,
                                                          comm_of("ssm_bwd"))
    absorb("ssm_bwd", carried)
    du = _from_scan_rows(du_perm, n_samples)
    g_lr, g_li, g_ldt, g_br, g_bi, g_cr, g_ci = _ssm_param_bwd(
        fs["lr"], fs["li"], fs["ldt"], fs["br"], fs["bi"], da_cat, dbb_full, dc_full)

    small_pack.append(_pack_small(dict(lr=g_lr, li=g_li, ldt=g_ldt, br=g_br, bi=g_bi, cr=g_cr, ci=g_ci, dskip=g_dskip,
                                       g_ffn=g_gffn, g_fin=g_gfin, loss=loss_blk)))

    dproj = _pack_dproj(dqs, dks, dvs, du, dgpre, tabs)
    for hf in range(2):
        big[f"w_in:{hf}"] = mm(dproj, h0, "tn", f"mm_g_in{hf}", 256, half, out_dtype=BF16, cols=(hf * half, half))
    grad_x, _, g_gmix, *carried = _mm_rms_bwd([dproj], [w["w_in"]], x, g_mix, dx1, "d_h0_rms", comm_of("d_h0_rms"))
    absorb("d_h0_rms", carried)
    return grad_x, (big if shards is None else recv), small_pack[0], g_gmix


_MESH = pl.DeviceIdType.MESH


def _all_gather(block, name):
    rows, lanes = block.shape

    def body(x_ref, out_ref, send_sems, recv_sems, local_sem):
        x, y, c = lax.axis_index("x"), lax.axis_index("y"), lax.axis_index("c")
        me, sibling = (x, y, c), (x, y, 1 - c)
        chips = [(1 - x, y), (x, 1 - y), (1 - x, 1 - y)]

        def slot(px, py, pc):
            return out_ref.at[4 * px + 2 * py + pc]

        def copy(k, blk, to, src=None):
            return pltpu.make_async_remote_copy(
                src_ref=slot(*blk) if src is None else src, dst_ref=slot(*blk), send_sem=send_sems.at[k],
                recv_sem=recv_sems.at[k], device_id=to, device_id_type=_MESH)

        mine = pltpu.make_async_copy(x_ref, slot(*me), local_sem)
        mine.start()
        first = [copy(0, me, sibling, src=x_ref)]
        first += [copy(1 + j, me, (*chip, c), src=x_ref) for j, chip in enumerate(chips)]
        for cp in first:
            cp.start()
        passed = [copy(4 + j, (*chip, c), sibling) for j, chip in enumerate(chips)]
        for j, chip in enumerate(chips):
            copy(1 + j, (*chip, c), me).wait_recv()
            passed[j].start()
        copy(0, sibling, me).wait_recv()
        for j, chip in enumerate(chips):
            copy(4 + j, (*chip, 1 - c), me).wait_recv()
        for cp in first + passed:
            cp.wait_send()
        mine.wait()

    return _pallas_call(
        body, name=name, out_shape=jax.ShapeDtypeStruct((N_DEV, rows, lanes), block.dtype),
        in_specs=[pl.BlockSpec(memory_space=pl.ANY)], out_specs=pl.BlockSpec(memory_space=pl.ANY),
        scratch_shapes=[pltpu.SemaphoreType.DMA((7,)), pltpu.SemaphoreType.DMA((7,)), pltpu.SemaphoreType.DMA],
    )(block)


def _ag_comm(items, bufs):
    def plan(in_refs, out_refs, send_sems, recv_sems, local_sems):
        x, y, c = lax.axis_index("x"), lax.axis_index("y"), lax.axis_index("c")
        me, sibling = (x, y, c), (x, y, 1 - c)
        chips = [(1 - x, y), (x, 1 - y), (1 - x, 1 - y)]
        plans = []
        for t, (_, buf, slot0) in enumerate(items):
            x_ref, out_ref = in_refs[t], out_refs[buf]

            def slot(px, py, pc, out_ref=out_ref, slot0=slot0):
                return out_ref.at[slot0 + 4 * px + 2 * py + pc]

            def copy(k, blk, to, src=None, t=t, slot=slot):
                return pltpu.make_async_remote_copy(
                    src_ref=slot(*blk) if src is None else src, dst_ref=slot(*blk), send_sem=send_sems.at[7 * t + k],
                    recv_sem=recv_sems.at[7 * t + k], device_id=to, device_id_type=_MESH)

            plans.append(dict(
                mine=pltpu.make_async_copy(x_ref, slot(*me), local_sems.at[t]),
                first=[copy(0, me, sibling, src=x_ref)] + [copy(1 + j, me, (*chip, c), src=x_ref)
                                                           for j, chip in enumerate(chips)],
                passed=[copy(4 + j, (*chip, c), sibling) for j, chip in enumerate(chips)],
                from_ici=[copy(1 + j, (*chip, c), me) for j, chip in enumerate(chips)],
                from_sibling=[copy(0, sibling, me)] + [copy(4 + j, (*chip, 1 - c), me) for j, chip in enumerate(chips)]))
        return plans

    def start(*refs):
        for p in plan(*refs):
            p["mine"].start()
            for cp in p["first"]:
                cp.start()

    def finish(*refs):
        plans = plan(*refs)
        for p in plans:
            for arrived, onward in zip(p["from_ici"], p["passed"]):
                arrived.wait_recv()
                onward.start()
        for p in plans:
            for arrived in p["from_sibling"]:
                arrived.wait_recv()
            for cp in p["first"] + p["passed"]:
                cp.wait_send()
            p["mine"].wait()

    dtype_of = {buf: shard.dtype for shard, buf, _ in items}
    out_shapes = [jax.ShapeDtypeStruct(b, dtype_of[j]) for j, b in enumerate(bufs)]
    return _Comm([it[0] for it in items], out_shapes, 7 * len(items), len(items), start, finish)


def _a2a_comm(items):
    def plan(in_refs, out_refs, send_sems, recv_sems, local_sems):
        x, y, c = lax.axis_index("x"), lax.axis_index("y"), lax.axis_index("c")
        my = 4 * x + 2 * y + c
        copies, locals_ = [], []
        for t, (_, slot0) in enumerate(items):
            s_ref, r_ref = in_refs[t], out_refs[t]
            locals_.append(pltpu.make_async_copy(s_ref.at[slot0 + my], r_ref.at[my], local_sems.at[t]))
            for kk in range(1, N_DEV):
                px = 1 - x if kk & 4 else x
                py = 1 - y if kk & 2 else y
                pc = 1 - c if kk & 1 else c
                copies.append(pltpu.make_async_remote_copy(
                    src_ref=s_ref.at[slot0 + 4 * px + 2 * py + pc], dst_ref=r_ref.at[my],
                    send_sem=send_sems.at[7 * t + kk - 1], recv_sem=recv_sems.at[7 * t + kk - 1],
                    device_id=(px, py, pc), device_id_type=_MESH))
        return copies, locals_

    def start(*refs):
        copies, locals_ = plan(*refs)
        for cp in locals_ + copies:
            cp.start()

    def finish(*refs):
        copies, locals_ = plan(*refs)
        for cp in copies + locals_:
            cp.wait()

    out_shapes = [jax.ShapeDtypeStruct((N_DEV,) + it[0].shape[1:], it[0].dtype) for it in items]
    return _Comm([it[0] for it in items], out_shapes, 7 * len(items), len(items), start, finish)


def _adam_math(g, w, m, v):
    m_new = ADAM_B1 * m + (1.0 - ADAM_B1) * g
    v_new = ADAM_B2 * v + (1.0 - ADAM_B2) * jnp.square(g)
    m_hat = m_new / (1.0 - ADAM_B1 ** ADAM_STEP)
    v_hat = v_new / (1.0 - ADAM_B2 ** ADAM_STEP)
    return -ADAM_LR * (m_hat / (jnp.sqrt(v_hat) + ADAM_EPS) + ADAM_WD * w), m_new, v_new


def _sum_partials(parts, name, tm):
    n, rows, _ = parts[0].shape
    widths = [p.shape[2] for p in parts]

    def body(*refs):
        g_ref, off = refs[-1], 0
        for p_ref, wd in zip(refs[:-1], widths):
            g = p_ref[0].astype(F32)
            for s in range(1, n):
                g = g + p_ref[s].astype(F32)
            g_ref[:, off:off + wd] = g
            off += wd

    return _pallas_call(
        body, name=name, grid=(rows // tm,), in_specs=[pl.BlockSpec((n, tm, wd), lambda i: (0, i, 0)) for wd in widths],
        out_specs=pl.BlockSpec((tm, sum(widths)), lambda i: (i, 0)),
        out_shape=jax.ShapeDtypeStruct((rows, sum(widths)), F32),
        compiler_params=pltpu.CompilerParams(dimension_semantics=("parallel",), vmem_limit_bytes=VMEM_MID),
    )(*parts)


def _adam(partials, w, m, v, name, tm):
    n, rows, cols = partials.shape

    def body(p_ref, w_ref, m_ref, v_ref, g_ref, d_ref, nm_ref, nv_ref):
        g = p_ref[0].astype(F32)
        for s in range(1, n):
            g = g + p_ref[s].astype(F32)
        g_ref[...] = g
        d_ref[...], nm_ref[...], nv_ref[...] = _adam_math(g, w_ref[...], m_ref[...], v_ref[...])

    assert rows % tm == 0
    row = pl.BlockSpec((tm, cols), lambda i: (i, 0))
    shp = jax.ShapeDtypeStruct((rows, cols), F32)
    return _pallas_call(
        body, name=name, grid=(rows // tm,),
        in_specs=[pl.BlockSpec((n, tm, cols), lambda i: (0, i, 0)), row, row, row],
        out_specs=[row] * 4, out_shape=[shp] * 4,
        compiler_params=pltpu.CompilerParams(dimension_semantics=("parallel",), vmem_limit_bytes=VMEM_MID),
    )(partials, w, m, v)


_PK_LR, _PK_LI, _PK_GAINS, _PK_MISC, _PK_BR, _PK_BI, _PK_CR, _PK_CI, _PK_ROWS = 0, 1, 2, 3, 8, 24, 40, 56, 72
_PK_LDT_LANE, _PK_LOSS_LANE = D_MODEL + SSM_W, D_MODEL + SSM_W + LANES


def _pack_small(sg):
    names = ("lr", "li", "g_ffn", "g_fin", "dskip", "ldt", "loss", "br", "bi", "cr", "ci")

    def body(lr, li, gffn, gfin, dskip, ldt, loss, br, bi, cr, ci, o_ref):
        o_ref[...] = jnp.zeros_like(o_ref)
        o_ref[_PK_LR:_PK_LR + 1, :] = lr[...]
        o_ref[_PK_LI:_PK_LI + 1, :] = li[...]
        o_ref[_PK_GAINS:_PK_GAINS + 1, D_MODEL:] = gffn[...]
        o_ref[_PK_MISC:_PK_MISC + 1, :D_MODEL] = gfin[...]
        o_ref[_PK_MISC:_PK_MISC + 1, D_MODEL:D_MODEL + SSM_W] = dskip[...]
        o_ref[_PK_MISC:_PK_MISC + 1, _PK_LDT_LANE:_PK_LDT_LANE + LANES] = ldt[0:1, :]
        o_ref[_PK_MISC:_PK_MISC + 1, _PK_LOSS_LANE:_PK_LOSS_LANE + LANES] = loss[0:1, :]
        o_ref[_PK_BR:_PK_BR + SSM_CH, :] = br[...]
        o_ref[_PK_BI:_PK_BI + SSM_CH, :] = bi[...]
        o_ref[_PK_CR:_PK_CR + SSM_CH, :] = cr[...]
        o_ref[_PK_CI:_PK_CI + SSM_CH, :] = ci[...]

    return _pallas_call(body, name="pack_small", out_shape=jax.ShapeDtypeStruct((_PK_ROWS, N_STATE), F32))(
        *[sg[n] for n in names])


def _unpack_small(s, g_mix):
    unflat_b = lambda a: a.reshape(SSM_CH, SSM_GROUPS, SSM_STATE).transpose(1, 2, 0)[None]
    unflat_c = lambda a: a.reshape(SSM_CH, SSM_GROUPS, SSM_STATE).transpose(1, 0, 2)[None]
    grads = {
        "norm_mix_g": g_mix, "norm_ffn_g": s[_PK_GAINS, D_MODEL:].reshape(1, D_MODEL),
        "norm_final_g": s[_PK_MISC, :D_MODEL],
        "ssm_a_re": s[_PK_LR].reshape(1, SSM_GROUPS, SSM_STATE), "ssm_a_im": s[_PK_LI].reshape(1, SSM_GROUPS, SSM_STATE),
        "ssm_log_dt": s[_PK_MISC, _PK_LDT_LANE:_PK_LDT_LANE + SSM_GROUPS].reshape(1, SSM_GROUPS),
        "ssm_d": s[_PK_MISC, D_MODEL:D_MODEL + SSM_W].reshape(1, SSM_GROUPS, SSM_CH),
        "ssm_b_re": unflat_b(s[_PK_BR:_PK_BR + SSM_CH]), "ssm_b_im": unflat_b(s[_PK_BI:_PK_BI + SSM_CH]),
        "ssm_c_re": unflat_c(s[_PK_CR:_PK_CR + SSM_CH]), "ssm_c_im": unflat_c(s[_PK_CI:_PK_CI + SSM_CH]),
    }
    return s[_PK_MISC, _PK_LOSS_LANE], grads


def _adam_small(grads, wts, moms, vars_):
    n = len(SMALL_WEIGHTS)
    as2d = lambda a: a.reshape(1, -1) if a.ndim == 1 else a

    def body(*refs):
        ins, outs = refs[:4 * n], refs[4 * n:]
        for i in range(n):
            g, w, m, v = (ins[j * n + i][...] for j in range(4))
            outs[i][...], outs[n + i][...], outs[2 * n + i][...] = _adam_math(g, w, m, v)

    operands = [as2d(d[k]) for d in (grads, wts, moms, vars_) for k in SMALL_WEIGHTS]
    shapes = [jax.ShapeDtypeStruct(as2d(wts[k]).shape, F32) for k in SMALL_WEIGHTS] * 3
    res = _pallas_call(body, name="adam_small", out_shape=shapes,
                         compiler_params=pltpu.CompilerParams(vmem_limit_bytes=VMEM_BIG))(*operands)
    out = {}
    for j, kind in enumerate(("delta", "new_m", "new_v")):
        for i, k in enumerate(SMALL_WEIGHTS):
            out[kind, k] = res[j * n + i].reshape(wts[k].shape)
    return out


def kernel(x, norm_mix_g, w_in, ssm_a_re, ssm_a_im, ssm_log_dt, ssm_b_re, ssm_b_im, ssm_c_re, ssm_c_im, ssm_d, w_glu, w_attn_out, w_out, norm_ffn_g, w_ffn_gate, w_ffn_up, w_ffn_down, norm_final_g, loss_target, m_norm_mix_g, m_w_in, m_ssm_a_re, m_ssm_a_im, m_ssm_log_dt, m_ssm_b_re, m_ssm_b_im, m_ssm_c_re, m_ssm_c_im, m_ssm_d, m_w_glu, m_w_attn_out, m_w_out, m_norm_ffn_g, m_w_ffn_gate, m_w_ffn_up, m_w_ffn_down, m_norm_final_g, v_norm_mix_g, v_w_in, v_ssm_a_re, v_ssm_a_im, v_ssm_log_dt, v_ssm_b_re, v_ssm_b_im, v_ssm_c_re, v_ssm_c_im, v_ssm_d, v_w_glu, v_w_attn_out, v_w_out, v_norm_ffn_g, v_w_ffn_gate, v_w_ffn_up, v_w_ffn_down, v_norm_final_g):
    args = dict(locals())
    wts = {n: args[n] for n in ALL_WEIGHTS}
    moms = {n: args["m_" + n] for n in ALL_WEIGHTS}
    vars_ = {n: args["v_" + n] for n in ALL_WEIGHTS}
    n_samples = x.shape[0]
    t = n_samples * SEQ

    shards = {n: (wts[n][0] if n in ROW_SHARDED else wts[n][0].T).astype(BF16) for n in BIG_WEIGHTS}
    w_in_t = _all_gather(shards["w_in"], "allgather_w_in").reshape(IN_W, D_MODEL)

    small = {n: wts[n] for n in SMALL_WEIGHTS}
    grad_x, recv, _, g_mix_part = _local_step(x.reshape(t, D_MODEL), loss_target.reshape(t, D_MODEL), {"w_in": w_in_t},
                                              small, shards)

    results = {}
    for n in BIG_WEIGHTS:
        c, k = shards[n].shape
        w2, m2, v2 = wts[n][0], moms[n][0], vars_[n][0]
        if n in ROW_SHARDED:
            res = _adam(recv[n], w2, m2, v2, "adam_" + n, c // 2)
        else:
            parts = [recv[f"{n}:{hf}"] for hf in range(2)] if n in HALVED else [recv[n]]
            g_t = _sum_partials(parts, "sum_" + n, c // 2)
            res = _adam(g_t.T[None], w2, m2, v2, "adam_" + n, k // 2)
        for kind, a in zip(("grad", "delta", "new_m", "new_v"), res):
            results[kind, n] = a[None]

    g_mix_all = _all_gather(jnp.pad(g_mix_part, ((0, 7), (0, 0))), "allgather_g_mix")
    g_mix = _sum_partials([g_mix_all], "sum_g_mix", 8)[0:1]
    loss, sgrads = _unpack_small(_sum_partials([recv["small"]], "sum_small", _PK_ROWS), g_mix)
    for n in SMALL_WEIGHTS:
        results["grad", n] = sgrads[n]
    results.update(_adam_small(sgrads, wts, moms, vars_))
    outs = [loss, grad_x.reshape(x.shape)]
    for kind in ("grad", "delta", "new_m", "new_v"):
        outs += [results[kind, n] for n in ALL_WEIGHTS]
    return tuple(outs)
```

```python
import functools
import math

import jax
import jax.numpy as jnp
from jax import lax
from jax.experimental import pallas as pl
from jax.experimental.pallas import tpu as pltpu

F32 = jnp.float32
BF16 = jnp.bfloat16
MXU_DTYPE = jnp.bfloat16

N_DEV = 8
D_MODEL = 1024
SEQ = 2048
HEAD_DIM = 64
HEADS_PER_GROUP = 4
GROUP_W = HEADS_PER_GROUP * HEAD_DIM
DILATIONS = (1, 4, 16)
QKV_W = 3 * len(DILATIONS) * GROUP_W
Q_W = len(DILATIONS) * GROUP_W
ATT_BLOCK = 128
ROPE_DIM = 16
ROPE_THETA = 500000.0
SSM_W = 512
SSM_GROUPS = 32
SSM_CH = 16
SSM_STATE = 64
N_STATE = SSM_GROUPS * SSM_STATE
D_FF = 2816
IN_W = QKV_W + SSM_W + 2 * D_MODEL
RMS_EPS = 1e-6
NEG_INF = -1e30
LANES = 128

SCAN_SEG_PER_SAMPLE = 8
SCAN_LEN = SEQ // SCAN_SEG_PER_SAMPLE
SCAN_WC = 512
SCAN_NBLK = N_STATE // SCAN_WC
SCAN_CH = SSM_W // SCAN_NBLK
SCAN_CHUNK = 32

ADAM_LR = 0.001
ADAM_B1 = 0.9
ADAM_B2 = 0.999
ADAM_EPS = 1e-08
ADAM_WD = 0.01
ADAM_STEP = 10

VMEM_BIG = 48 * 1024 * 1024
VMEM_MID = 32 * 1024 * 1024

BIG_WEIGHTS = ("w_in", "w_glu", "w_attn_out", "w_out", "w_ffn_gate", "w_ffn_up", "w_ffn_down")
ROW_SHARDED = ("w_out", "w_ffn_down")
SMALL_WEIGHTS = ("norm_mix_g", "ssm_a_re", "ssm_a_im", "ssm_log_dt", "ssm_b_re", "ssm_b_im", "ssm_c_re", "ssm_c_im",
                 "ssm_d", "norm_ffn_g", "norm_final_g")
ALL_WEIGHTS = ("norm_mix_g", "w_in", "ssm_a_re", "ssm_a_im", "ssm_log_dt", "ssm_b_re", "ssm_b_im", "ssm_c_re", "ssm_c_im",
               "ssm_d", "w_glu", "w_attn_out", "w_out", "norm_ffn_g", "w_ffn_gate", "w_ffn_up", "w_ffn_down", "norm_final_g")


def _sigmoid(x):
    return 1.0 / (1.0 + jnp.exp(-x))


def _pallas_call(body, *, out_shape, **kw):
    single = not isinstance(out_shape, (list, tuple))
    shapes = [pltpu.HBM(s.shape, s.dtype) for s in ([out_shape] if single else out_shape)]
    call = pl.pallas_call(body, out_shape=shapes[0] if single else shapes, **kw)
    return lambda *operands: call(*[pltpu.with_memory_space_constraint(o, pltpu.HBM) for o in operands])


class _Comm:
    def __init__(self, ins, out_shapes, n_sem, n_local, start, finish):
        self.ins, self.out_shapes, self.n_sem, self.n_local = ins, out_shapes, n_sem, n_local
        self.start, self.finish = start, finish


def _mm(a, b, mode, name, tm, tn, out_dtype=F32, add=None, vmem=VMEM_BIG, comm=None, cols=None):
    if mode == "nn":
        (m, k), (_, n) = a.shape, b.shape
        a_spec = pl.BlockSpec((tm, k), lambda i, j: (i, 0))
        b_spec = pl.BlockSpec((k, tn), lambda i, j: (0, j))
        dims = (((1,), (0,)), ((), ()))
    elif mode == "nt":
        (m, k), (n, _) = a.shape, b.shape
        a_spec = pl.BlockSpec((tm, k), lambda i, j: (i, 0))
        b_spec = pl.BlockSpec((tn, k), lambda i, j: (j, 0))
        dims = (((1,), (1,)), ((), ()))
    else:
        (k, m), (_, n) = a.shape, b.shape
        first, n = cols if cols else (0, n)
        a_spec = pl.BlockSpec((k, tm), lambda i, j: (0, i))
        b_spec = pl.BlockSpec((k, tn), lambda i, j: (0, j + first // tn))
        dims = (((0,), (0,)), ((), ()))
    assert m % tm == 0 and n % tn == 0, (name, m, n, tm, tn)
    o_spec = pl.BlockSpec((tm, tn), lambda i, j: (i, j))
    has_add = add is not None

    def body(*refs):
        a_ref, b_ref, o_ref = refs[0], refs[1], refs[-1]
        acc = lax.dot_general(a_ref[...].astype(MXU_DTYPE), b_ref[...].astype(MXU_DTYPE), dims,
                              preferred_element_type=F32)
        if has_add:
            acc = acc + refs[2][...]
        o_ref[...] = acc.astype(out_dtype)

    ins = [a, b] + ([add] if has_add else [])
    in_specs = [a_spec, b_spec] + ([o_spec] if has_add else [])
    return _grid_call(body, name, (m // tm, n // tn), ins, in_specs, [o_spec],
                      [jax.ShapeDtypeStruct((m, n), out_dtype)], vmem, comm)


def _grid_call(body, name, grid, ins, in_specs, out_specs, out_shapes, vmem, comm=None, sequential=False, scratch=()):
    if comm is None:
        single = len(out_shapes) == 1
        semantics = ("arbitrary", "arbitrary") if sequential else ("parallel", "parallel")
        return _pallas_call(
            body, name=name, grid=grid, in_specs=in_specs, out_specs=out_specs[0] if single else out_specs,
            out_shape=out_shapes[0] if single else out_shapes, scratch_shapes=list(scratch),
            compiler_params=pltpu.CompilerParams(dimension_semantics=semantics, vmem_limit_bytes=vmem),
        )(*ins)
    n_in, n_out, n_cin, n_cout = len(ins), len(out_shapes), len(comm.ins), len(comm.out_shapes)
    n_io = n_in + n_cin + n_out + n_cout

    def carrying(*refs):
        own = refs[:n_in] + refs[n_in + n_cin:n_in + n_cin + n_out] + refs[n_io:len(refs) - 3]
        c_args = (refs[n_in:n_in + n_cin], refs[n_in + n_cin + n_out:n_io], *refs[-3:])

        @pl.when((pl.program_id(0) == 0) & (pl.program_id(1) == 0))
        def _():
            comm.start(*c_args)

        body(*own)

        @pl.when((pl.program_id(0) == grid[0] - 1) & (pl.program_id(1) == grid[1] - 1))
        def _():
            comm.finish(*c_args)

    hbm = pl.BlockSpec(memory_space=pl.ANY)
    return _pallas_call(
        carrying, name=name, grid=grid, in_specs=list(in_specs) + [hbm] * n_cin,
        out_specs=list(out_specs) + [hbm] * n_cout, out_shape=list(out_shapes) + list(comm.out_shapes),
        scratch_shapes=list(scratch) + [pltpu.SemaphoreType.DMA((comm.n_sem,)), pltpu.SemaphoreType.DMA((comm.n_sem,)),
                                        pltpu.SemaphoreType.DMA((comm.n_local,))],
        compiler_params=pltpu.CompilerParams(dimension_semantics=("arbitrary", "arbitrary"), vmem_limit_bytes=vmem),
    )(*ins, *comm.ins)


def _rows(body, name, n_rows, tm, ins, outs, vmem=VMEM_MID, scratch=()):
    assert n_rows % tm == 0
    arrays, in_specs = [], []
    for kind, arr in ins:
        arrays.append(arr)
        if kind == "row":
            assert n_rows % arr.shape[0] == 0, (name, arr.shape)
            in_specs.append(pl.BlockSpec((tm * arr.shape[0] // n_rows, arr.shape[1]), lambda i: (i, 0)))
        elif kind == "tab":
            nblk = arr.shape[0] // tm
            in_specs.append(pl.BlockSpec((tm, arr.shape[1]), lambda i, nblk=nblk: (i % nblk, 0)))
        else:
            in_specs.append(pl.BlockSpec(arr.shape, lambda i, nd=arr.ndim: (0,) * nd))
    out_specs, out_shape = [], []
    for kind, shp, dt in outs:
        if kind == "row":
            out_specs.append(pl.BlockSpec((tm, shp), lambda i: (i, 0)))
            out_shape.append(jax.ShapeDtypeStruct((n_rows, shp), dt))
        elif kind == "dil":
            d, wd = shp
            out_specs.append(pl.BlockSpec((tm // d, d * wd), lambda i: (i, 0)))
            out_shape.append(jax.ShapeDtypeStruct((n_rows // d, d * wd), dt))
        else:
            out_specs.append(pl.BlockSpec(shp, lambda i, nd=len(shp): (0,) * nd))
            out_shape.append(jax.ShapeDtypeStruct(shp, dt))
    res = _pallas_call(
        body, name=name, grid=(n_rows // tm,), in_specs=in_specs, out_specs=out_specs, out_shape=out_shape,
        scratch_shapes=list(scratch),
        compiler_params=pltpu.CompilerParams(dimension_semantics=("arbitrary",), vmem_limit_bytes=vmem),
    )(*arrays)
    return res


def _gather_residue(stage, ch, r, d, n):
    return stage[ch, pl.ds(r, n, stride=d), :] if d > 1 else stage[ch]


def _scatter_residue(stage, ch, r, d, n, val):
    if d > 1:
        stage[ch, pl.ds(r, n, stride=d), :] = val
    else:
        stage[ch] = val


def _lane_chunk(ch):
    return slice(ch * LANES, (ch + 1) * LANES)


def _rope_tables():
    half = ROPE_DIM // 2
    inv = jnp.power(jnp.float32(ROPE_THETA), -jnp.arange(half, dtype=F32) * 2.0 / ROPE_DIM)
    ang = jnp.arange(SEQ, dtype=F32)[:, None] * inv[None, :]
    lane = jnp.arange(LANES) % HEAD_DIM
    cosl = jnp.cos(ang)[:, lane % half]
    sinl = jnp.sin(ang)[:, lane % half]
    tab_c = jnp.where(lane < ROPE_DIM, cosl, 1.0)
    tab_lo = jnp.where(lane < half, -sinl, 0.0)
    tab_hi = jnp.where((lane >= half) & (lane < ROPE_DIM), sinl, 0.0)
    return tab_c.astype(F32), tab_lo.astype(F32), tab_hi.astype(F32)


def _rope_apply(t, tc, tlo, thi):
    half = ROPE_DIM // 2
    return t * tc + pltpu.roll(t, LANES - half, 1) * tlo + pltpu.roll(t, half, 1) * thi


def _rope_transpose(dt, tc, tlo, thi):
    half = ROPE_DIM // 2
    return dt * tc + pltpu.roll(dt * tlo, half, 1) + pltpu.roll(dt * thi, LANES - half, 1)


def _pack_dproj(dqs, dks, dvs, du, dgpre, tabs):
    tm = 256

    def body(*refs):
        dq_refs, dk_refs, dv_refs = refs[0:3], refs[3:6], refs[6:9]
        du_ref, dg_ref, tc_ref, tlo_ref, thi_ref, o_ref, stage = refs[9:16]
        n_ch = QKV_W // LANES
        halves = GROUP_W // LANES
        for grp, d in enumerate(DILATIONS):
            for which, src in enumerate((dq_refs[grp], dk_refs[grp], dv_refs[grp])):
                for res in range(d):
                    for half in range(halves):
                        _scatter_residue(stage, which * (n_ch // 3) + grp * halves + half, res, d, tm // d,
                                         src[:, _lane_chunk(res * halves + half)])
        tc, tlo, thi = tc_ref[...], tlo_ref[...], thi_ref[...]
        for ch in range(n_ch):
            piece = stage[ch]
            o_ref[:, _lane_chunk(ch)] = (_rope_transpose(piece, tc, tlo, thi) if ch < 2 * n_ch // 3 else piece).astype(BF16)
        o_ref[:, QKV_W:QKV_W + SSM_W] = du_ref[...].astype(BF16)
        o_ref[:, QKV_W + SSM_W:] = dg_ref[...].astype(BF16)

    t = du.shape[0]
    ins = [("row", a) for a in (*dqs, *dks, *dvs, du, dgpre)] + [("tab", tb) for tb in tabs]
    return _rows(body, "pack_dproj", t, tm, ins, [("row", IN_W, BF16)],
                 scratch=[pltpu.VMEM((QKV_W // LANES, tm, LANES), F32)])[0]


def _merge_groups(o_refs, l_refs, a_ref, lt_ref, nat, tm):
    halves = GROUP_W // LANES
    for grp, d in enumerate(DILATIONS[1:], start=1):
        for j, src in enumerate((o_refs[grp], l_refs[grp])):
            for res in range(d):
                for half in range(halves):
                    _scatter_residue(nat, (grp - 1) * 4 + j * 2 + half, res, d, tm // d,
                                     src[:, _lane_chunk(res * halves + half)])
    for half in range(halves):
        sl = _lane_chunk(half)
        la, lb, lc = l_refs[0][:, sl], nat[2 + half], nat[6 + half]
        m = jnp.maximum(jnp.maximum(la, lb), lc)
        ea, eb, ec = jnp.exp(la - m), jnp.exp(lb - m), jnp.exp(lc - m)
        ssum = ea + eb + ec
        a_ref[:, sl] = (ea / ssum) * o_refs[0][:, sl] + (eb / ssum) * nat[half] + (ec / ssum) * nat[4 + half]
        lt_ref[:, sl] = m + jnp.log(ssum)


def _head_sum_matrix():
    r = jnp.arange(GROUP_W) // HEAD_DIM
    return (r[:, None] == r[None, :]).astype(F32)


def _attention_cotangents(da, attn, lt, ones, rd_ref, dil, stage, tm):
    halves = GROUP_W // LANES
    rd = jnp.dot(da * attn, ones, preferred_element_type=F32, precision=lax.Precision.HIGHEST)
    rd_ref[...] = rd
    for half in range(halves):
        for j, val in enumerate((da, lt, rd)):
            stage[2 * j + half] = val[:, _lane_chunk(half)]
    for grp, d in enumerate(DILATIONS[1:], start=1):
        for j in range(3):
            for res in range(d):
                for half in range(halves):
                    dil[3 * (grp - 1) + j][:, _lane_chunk(res * halves + half)] = _gather_residue(
                        stage, 2 * j + half, res, d, tm // d)


_GELU_C = math.sqrt(2.0 / math.pi)


def _head_masks():
    lane = lax.broadcasted_iota(jnp.int32, (1, GROUP_W), 1)
    return [(lane // HEAD_DIM) == h for h in range(HEADS_PER_GROUP)]


def _stack_heads(blk, masks, fill=0.0):
    return jnp.concatenate([jnp.where(mk, blk, jnp.full_like(blk, fill)) for mk in masks], axis=0)


def _unstack_heads(stacked, masks):
    rows = stacked.shape[0] // len(masks)
    out = stacked[:rows]
    for h in range(1, len(masks)):
        out = jnp.where(masks[h], stacked[h * rows:(h + 1) * rows], out)
    return out


def _band_mask(first):
    nk = ATT_BLOCK if first else 2 * ATT_BLOCK
    qi = lax.broadcasted_iota(jnp.int32, (ATT_BLOCK, nk), 0)
    ki = lax.broadcasted_iota(jnp.int32, (ATT_BLOCK, nk), 1)
    dist = qi - ki + (0 if first else ATT_BLOCK)
    return (dist >= 0) & (dist <= ATT_BLOCK)


_NT = (((1,), (1,)), ((), ()))
_TN = (((0,), (0,)), ((), ()))


def _residues_per_step(d):
    return min(d, 4)


def _attn_fwd(q, k, v, group, n_samples, comm=None):
    d = DILATIONS[group]
    length = SEQ // d
    nb = length // ATT_BLOCK

    rps = _residues_per_step(d)

    def body(q_ref, k_ref, v_ref, o_ref, l_ref):
        for rl in range(rps):
            residue(q_ref, k_ref, v_ref, o_ref, l_ref, slice(rl * GROUP_W, (rl + 1) * GROUP_W))

    def residue(q_ref, k_ref, v_ref, o_ref, l_ref, cols):
        masks = _head_masks()

        def block(qs, ks, first):
            nk = ATT_BLOCK if first else 2 * ATT_BLOCK
            qb = q_ref[0, pl.ds(qs, ATT_BLOCK), cols]
            kc = k_ref[0, pl.ds(ks, nk), cols]
            vc = v_ref[0, pl.ds(ks, nk), cols]
            q4 = _stack_heads(qb, masks)
            valid = jnp.tile(_band_mask(first), (HEADS_PER_GROUP, 1))
            s = lax.dot_general(q4, kc, _NT, preferred_element_type=F32) * (HEAD_DIM ** -0.5)
            s = jnp.where(valid, s, NEG_INF)
            m = jnp.max(s, axis=-1, keepdims=True)
            p = jnp.exp(s - m)
            l = jnp.sum(p, axis=-1, keepdims=True)
            o4 = jnp.dot(p.astype(MXU_DTYPE), vc, preferred_element_type=F32) / l
            lse4 = jnp.broadcast_to(m + jnp.log(l), o4.shape)
            o_ref[0, pl.ds(qs, ATT_BLOCK), cols] = _unstack_heads(o4, masks)
            l_ref[0, pl.ds(qs, ATT_BLOCK), cols] = _unstack_heads(lse4, masks)

        block(0, 0, True)
        if nb > 1:
            def loop(n, carry):
                block(pl.multiple_of(n * ATT_BLOCK, ATT_BLOCK), pl.multiple_of((n - 1) * ATT_BLOCK, ATT_BLOCK), False)
                return carry

            lax.fori_loop(1, nb, loop, 0)

    per_sample = lambda a: a.reshape(n_samples, length, d * GROUP_W)
    spec = pl.BlockSpec((1, length, rps * GROUP_W), lambda b, r: (b, 0, r))
    shp = jax.ShapeDtypeStruct((n_samples, length, d * GROUP_W), F32)
    o, lse, *carried = _grid_call(body, f"attn_fwd_g{group}", (n_samples, d // rps), [per_sample(a) for a in (q, k, v)],
                                  [spec] * 3, [spec] * 2, [shp, shp], VMEM_MID, comm)
    flat = lambda a: a.reshape(n_samples * length, d * GROUP_W)
    return flat(o), flat(lse), carried


def _attn_bwd(q, k, v, dattn, lse_tot, rowdot, group, n_samples, comm=None):
    d = DILATIONS[group]
    length = SEQ // d
    nb = length // ATT_BLOCK

    rps = _residues_per_step(d)

    def body(q_ref, k_ref, v_ref, da_ref, lt_ref, rd_ref, dq_ref, dk_ref, dv_ref):
        dk_ref[...] = jnp.zeros_like(dk_ref)
        dv_ref[...] = jnp.zeros_like(dv_ref)
        for rl in range(rps):
            residue(q_ref, k_ref, v_ref, da_ref, lt_ref, rd_ref, dq_ref, dk_ref, dv_ref,
                    slice(rl * GROUP_W, (rl + 1) * GROUP_W))

    def residue(q_ref, k_ref, v_ref, da_ref, lt_ref, rd_ref, dq_ref, dk_ref, dv_ref, cols):
        masks = _head_masks()

        def block(qs, ks, first):
            nk = ATT_BLOCK if first else 2 * ATT_BLOCK
            qb = q_ref[0, pl.ds(qs, ATT_BLOCK), cols]
            kc = k_ref[0, pl.ds(ks, nk), cols]
            vc = v_ref[0, pl.ds(ks, nk), cols]
            da = da_ref[0, pl.ds(qs, ATT_BLOCK), cols]
            lt = lt_ref[0, pl.ds(qs, ATT_BLOCK), cols]
            rd = rd_ref[0, pl.ds(qs, ATT_BLOCK), cols]
            q4 = _stack_heads(qb, masks)
            da4 = _stack_heads(da, masks).astype(MXU_DTYPE)
            lt4 = jnp.max(_stack_heads(lt, masks, -jnp.inf), axis=-1, keepdims=True)
            rd4 = jnp.max(_stack_heads(rd, masks, -jnp.inf), axis=-1, keepdims=True)
            valid = jnp.tile(_band_mask(first), (HEADS_PER_GROUP, 1))
            s = lax.dot_general(q4, kc, _NT, preferred_element_type=F32) * (HEAD_DIM ** -0.5)
            s = jnp.where(valid, s, NEG_INF)
            p = jnp.exp(s - lt4)
            dp = lax.dot_general(da4, vc, _NT, preferred_element_type=F32)
            ds = (p * (dp - rd4) * (HEAD_DIM ** -0.5)).astype(MXU_DTYPE)
            dq_ref[0, pl.ds(qs, ATT_BLOCK), cols] = _unstack_heads(jnp.dot(ds, kc, preferred_element_type=F32), masks)
            dk_ref[0, pl.ds(ks, nk), cols] += lax.dot_general(ds, q4, _TN, preferred_element_type=F32)
            dv_ref[0, pl.ds(ks, nk), cols] += lax.dot_general(p.astype(MXU_DTYPE), da4, _TN, preferred_element_type=F32)

        block(0, 0, True)
        if nb > 1:
            def loop(n, carry):
                block(pl.multiple_of(n * ATT_BLOCK, ATT_BLOCK), pl.multiple_of((n - 1) * ATT_BLOCK, ATT_BLOCK), False)
                return carry

            lax.fori_loop(1, nb, loop, 0)

    per_sample = lambda a: a.reshape(n_samples, length, d * GROUP_W)
    spec = pl.BlockSpec((1, length, rps * GROUP_W), lambda b, r: (b, 0, r))
    shp = jax.ShapeDtypeStruct((n_samples, length, d * GROUP_W), F32)
    dq, dk, dv, *carried = _grid_call(
        body, f"attn_bwd_g{group}", (n_samples, d // rps), [per_sample(a) for a in (q, k, v, dattn, lse_tot, rowdot)],
        [spec] * 6, [spec] * 3, [shp, shp, shp], VMEM_MID, comm)
    flat = lambda a: a.reshape(n_samples * length, d * GROUP_W)
    return flat(dq), flat(dk), flat(dv), carried


def _disc(lr, li, ldt, br, bi):
    dt = jnp.exp(ldt)
    mag = jnp.exp(lr * dt)
    ab_re, ab_im = mag * jnp.cos(li * dt), mag * jnp.sin(li * dt)
    den = lr * lr + li * li
    nr, ni = ab_re - 1.0, ab_im
    f_re = (nr * lr + ni * li) / den
    f_im = (ni * lr - nr * li) / den
    return ab_re, ab_im, f_re * br - f_im * bi, f_re * bi + f_im * br


def _state_mask():
    row_g = lax.broadcasted_iota(jnp.int32, (SCAN_CH, SCAN_WC), 0) // SSM_CH
    col_g = lax.broadcasted_iota(jnp.int32, (SCAN_CH, SCAN_WC), 1) // SSM_STATE
    return row_g == col_g


def _ssm_disc(lr, li, ldt, br, bi, cr, ci):
    w = SCAN_WC

    def body(lr_ref, li_ref, ldt_ref, br_ref, bi_ref, cr_ref, ci_ref, a_ref, bb_ref, c_ref):
        ar, ai, bbr, bbi = _disc(lr_ref[...], li_ref[...], ldt_ref[...], br_ref[...], bi_ref[...])
        crv, civ = cr_ref[...], ci_ref[...]
        mask = _state_mask()
        for cb in range(SCAN_NBLK):
            sl = slice(cb * w, (cb + 1) * w)
            rows = slice(cb * SCAN_CH, (cb + 1) * SCAN_CH)
            dense = lambda comp: jnp.where(mask, jnp.tile(comp[:, sl], (SCAN_CH // SSM_CH, 1)), 0.0)
            a_ref[:, 2 * cb * w:(2 * cb + 1) * w] = ar[:, sl]
            a_ref[:, (2 * cb + 1) * w:(2 * cb + 2) * w] = ai[:, sl]
            bb_ref[rows, :w] = dense(bbr).astype(MXU_DTYPE)
            bb_ref[rows, w:] = dense(bbi).astype(MXU_DTYPE)
            c_ref[rows, :w] = dense(crv).astype(MXU_DTYPE)
            c_ref[rows, w:] = (-dense(civ)).astype(MXU_DTYPE)

    return _pallas_call(
        body, name="ssm_disc",
        out_shape=[jax.ShapeDtypeStruct((1, 2 * N_STATE), F32), jax.ShapeDtypeStruct((SSM_W, 2 * w), MXU_DTYPE),
                   jax.ShapeDtypeStruct((SSM_W, 2 * w), MXU_DTYPE)],
        compiler_params=pltpu.CompilerParams(vmem_limit_bytes=VMEM_MID),
    )(lr, li, ldt, br, bi, cr, ci)


def _group_indicator():
    s = jnp.arange(N_STATE) // SSM_STATE
    return (s[:, None] == jnp.arange(LANES)[None, :]).astype(F32)


def _ssm_param_bwd(lr, li, ldt, br, bi, da_cat, dbb_full, dc_full):
    w = SCAN_WC

    def body(lr_ref, li_ref, ldt_ref, br_ref, bi_ref, da_ref, dbb_ref, dc_ref, ind_ref,
             glr_ref, gli_ref, gldt_ref, gbr_ref, gbi_ref, gcr_ref, gci_ref):
        mask = _state_mask()

        def diag_parts(ref):
            res = ([], [])
            for cb in range(SCAN_NBLK):
                for part in range(2):
                    blk = ref[cb * SCAN_CH:(cb + 1) * SCAN_CH, part * w:(part + 1) * w]
                    res[part].append(jnp.sum(jnp.where(mask, blk, 0.0).reshape(SCAN_CH // SSM_CH, SSM_CH, w), axis=0))
            return jnp.concatenate(res[0], axis=1), jnp.concatenate(res[1], axis=1)

        dar = jnp.concatenate([da_ref[:, 2 * cb * w:(2 * cb + 1) * w] for cb in range(SCAN_NBLK)], axis=1)
        dai = jnp.concatenate([da_ref[:, (2 * cb + 1) * w:(2 * cb + 2) * w] for cb in range(SCAN_NBLK)], axis=1)
        dbbr, dbbi = diag_parts(dbb_ref)
        dcr, dci_neg = diag_parts(dc_ref)
        gcr_ref[...] = dcr
        gci_ref[...] = -dci_neg
        _, vjp = jax.vjp(_disc, lr_ref[...], li_ref[...], ldt_ref[...], br_ref[...], bi_ref[...])
        glr, gli, gldt, gbr, gbi = vjp((dar, dai, dbbr, dbbi))
        glr_ref[...] = glr
        gli_ref[...] = gli
        gldt_ref[...] = jnp.dot(jnp.broadcast_to(gldt, (8, N_STATE)), ind_ref[...], preferred_element_type=F32,
                                precision=lax.Precision.HIGHEST)
        gbr_ref[...] = gbr
        gbi_ref[...] = gbi

    v1 = jax.ShapeDtypeStruct((1, N_STATE), F32)
    v16 = jax.ShapeDtypeStruct((SSM_CH, N_STATE), F32)
    vdt = jax.ShapeDtypeStruct((8, LANES), F32)
    return _pallas_call(
        body, name="ssm_param_bwd", out_shape=[v1, v1, vdt, v16, v16, v16, v16],
        compiler_params=pltpu.CompilerParams(vmem_limit_bytes=VMEM_BIG),
    )(lr, li, ldt, br, bi, da_cat, dbb_full, dc_full, _group_indicator())


def _cmul(ar, ai, br, bi):
    return ar * br - ai * bi, ar * bi + ai * br


def _gelu_tanh(y):
    return jnp.tanh(_GELU_C * (y + 0.044715 * (y * y * y)))


def _segment_carry(er, ei, ar, ai, n_rows, reverse):
    qr, qi = ar, ai
    for _ in range(int(math.log2(SCAN_LEN))):
        qr, qi = _cmul(qr, qi, qr, qi)
    seg = lax.broadcasted_iota(jnp.int32, er.shape, 0) % SCAN_SEG_PER_SAMPLE
    shift = 1
    while shift < SCAN_SEG_PER_SAMPLE:
        keep = (seg < SCAN_SEG_PER_SAMPLE - shift) if reverse else (seg >= shift)
        amount = n_rows - shift if reverse else shift
        sr = jnp.where(keep, pltpu.roll(er, amount, 0), 0.0)
        si = jnp.where(keep, pltpu.roll(ei, amount, 0), 0.0)
        if reverse:
            er, ei = er + qr * sr + qi * si, ei + qr * si - qi * sr
        else:
            er, ei = er + qr * sr - qi * si, ei + qr * si + qi * sr
        qr, qi = _cmul(qr, qi, qr, qi)
        shift *= 2
    keep = (seg < SCAN_SEG_PER_SAMPLE - 1) if reverse else (seg >= 1)
    amount = n_rows - 1 if reverse else 1
    return jnp.where(keep, pltpu.roll(er, amount, 0), 0.0), jnp.where(keep, pltpu.roll(ei, amount, 0), 0.0)


def _ssm_fwd(u_perm, a_cat, bbc, cc, dskip, n_rows):
    t = u_perm.shape[0]
    w = SCAN_WC
    rows_c = SCAN_CHUNK * n_rows
    n_chunks = t // rows_c

    assert n_chunks % 2 == 0

    def body(u_ref, a_ref, bb_ref, c_ref, d_ref, yt_ref, yg_ref, ein_ref, bu_all, st_a, st_b, xs_a, xs_b):
        ar = jnp.broadcast_to(a_ref[:, :w], (n_rows, w))
        ai = jnp.broadcast_to(a_ref[:, w:], (n_rows, w))
        start = lambda ch: pl.multiple_of(ch * rows_c, rows_c)

        def project(ch, stage):
            res = jnp.dot(u_ref[pl.ds(start(ch), rows_c), :].astype(MXU_DTYPE), bb_ref[...], preferred_element_type=F32)
            stage[...] = res
            bu_all[pl.ds(start(ch), rows_c), :] = res

        def steps(src, r0, carry, xs=None):
            for i in range(SCAN_CHUNK):
                blk = src[pl.ds(r0 + i * n_rows, n_rows), :]
                carry = (ar * carry[0] - ai * carry[1] + blk[:, :w], ar * carry[1] + ai * carry[0] + blk[:, w:])
                if xs is not None:
                    xs[i * n_rows:(i + 1) * n_rows, :w] = carry[0]
                    xs[i * n_rows:(i + 1) * n_rows, w:] = carry[1]
            return carry

        def emit(xs, ch):
            y = lax.dot_general(xs[...].astype(MXU_DTYPE), c_ref[...], _NT, preferred_element_type=F32)
            yt = y + d_ref[...] * u_ref[pl.ds(start(ch), rows_c), :]
            yt_ref[pl.ds(start(ch), rows_c), :] = yt
            yg_ref[pl.ds(start(ch), rows_c), :] = (0.5 * yt * (1.0 + _gelu_tanh(yt))).astype(BF16)

        project(0, st_a)

        def pair1(p, carry):
            project(2 * p + 1, st_b)
            carry = steps(st_a, 0, carry)
            project(jnp.minimum(2 * p + 2, n_chunks - 1), st_a)
            return steps(st_b, 0, carry)

        zero = jnp.zeros((n_rows, w), F32)
        er, ei = lax.fori_loop(0, n_chunks // 2, pair1, (zero, zero))
        cr, ci = _segment_carry(er, ei, ar, ai, n_rows, False)
        ein_ref[:, :w] = cr
        ein_ref[:, w:] = ci

        xs_b[...] = jnp.zeros_like(xs_b)

        def pair2(p, carry):
            emit(xs_b, jnp.maximum(2 * p - 1, 0))
            carry = steps(bu_all, start(2 * p), carry, xs_a)
            emit(xs_a, 2 * p)
            return steps(bu_all, start(2 * p + 1), carry, xs_b)

        lax.fori_loop(0, n_chunks // 2, pair2, (cr, ci))
        emit(xs_b, n_chunks - 1)

    col = lambda width: pl.BlockSpec((t, width), lambda c: (0, c))
    wgt = pl.BlockSpec((SCAN_CH, 2 * w), lambda c: (c, 0))
    return _pallas_call(
        body, name="ssm_fwd", grid=(SCAN_NBLK,),
        in_specs=[col(SCAN_CH), pl.BlockSpec((1, 2 * w), lambda c: (0, c)), wgt, wgt,
                  pl.BlockSpec((1, SCAN_CH), lambda c: (0, c))],
        out_specs=[col(SCAN_CH), col(SCAN_CH), pl.BlockSpec((n_rows, 2 * w), lambda c: (0, c))],
        out_shape=[jax.ShapeDtypeStruct((t, SSM_W), F32), jax.ShapeDtypeStruct((t, SSM_W), BF16),
                   jax.ShapeDtypeStruct((n_rows, 2 * N_STATE), F32)],
        scratch_shapes=[pltpu.VMEM((t, 2 * w), F32)] + [pltpu.VMEM((rows_c, 2 * w), F32)] * 4,
        compiler_params=pltpu.CompilerParams(dimension_semantics=("parallel",), vmem_limit_bytes=VMEM_BIG),
    )(u_perm, a_cat, bbc, cc, dskip)


def _ssm_bwd(u_perm, dyg, ytot, dskip, a_cat, bbc, cc, ein, n_rows, comm=None):
    t = u_perm.shape[0]
    w = SCAN_WC
    rows_c = SCAN_CHUNK * n_rows
    n_chunks = t // rows_c

    assert n_chunks % 2 == 0
    last = n_chunks - 1

    def body(u_ref, dyg_ref, yt_ref, dk_ref, a_ref, bb_ref, c_ref, ein_ref, du_ref, gd_ref, da_ref, dbb_ref, dc_ref,
             xs_all, dy_s, st_a, st_b, buf_a, buf_b):
        ar = jnp.broadcast_to(a_ref[:, :w], (n_rows, w))
        ai = jnp.broadcast_to(a_ref[:, w:], (n_rows, w))
        zero = jnp.zeros((n_rows, w), F32)
        start = lambda ch: pl.multiple_of(ch * rows_c, rows_c)
        dbb_ref[...] = jnp.zeros_like(dbb_ref)
        dc_ref[...] = jnp.zeros_like(dc_ref)
        da_ref[...] = jnp.zeros_like(da_ref)

        yt = yt_ref[...]
        th = _gelu_tanh(yt)
        dgelu = 0.5 * (1.0 + th) + 0.5 * yt * (1.0 - th * th) * _GELU_C * (1.0 + 3.0 * 0.044715 * yt * yt)
        dy_all = dyg_ref[...] * dgelu
        dy_s[...] = dy_all
        gd_ref[...] = jnp.sum(dy_all * u_ref[...], axis=0, keepdims=True)
        dy_chunk = lambda ch: dy_s[pl.ds(start(ch), rows_c), :].astype(MXU_DTYPE)

        xs_all[0:n_rows, :] = ein_ref[...]

        def project(ch, stage):
            stage[...] = jnp.dot(u_ref[pl.ds(start(ch), rows_c), :].astype(MXU_DTYPE), bb_ref[...],
                                 preferred_element_type=F32)

        def fwd_steps(stage, ch, carry, xs):
            for i in range(SCAN_CHUNK):
                blk = stage[i * n_rows:(i + 1) * n_rows, :]
                carry = (ar * carry[0] - ai * carry[1] + blk[:, :w], ar * carry[1] + ai * carry[0] + blk[:, w:])
                for half, val in enumerate(carry):
                    xs[i * n_rows:(i + 1) * n_rows, half * w:(half + 1) * w] = val
                    xs_all[pl.ds(start(ch) + (i + 1) * n_rows, n_rows), half * w:(half + 1) * w] = val
            return carry

        def add_dc(xs, ch):
            dc_ref[...] += lax.dot_general(dy_chunk(ch), xs[...].astype(MXU_DTYPE), _TN, preferred_element_type=F32)

        project(0, st_a)

        def fwd_pair(p, carry):
            project(2 * p + 1, st_b)
            carry = fwd_steps(st_a, 2 * p, carry, buf_a)
            add_dc(buf_a, 2 * p)
            project(jnp.minimum(2 * p + 2, last), st_a)
            carry = fwd_steps(st_b, 2 * p + 1, carry, buf_b)
            add_dc(buf_b, 2 * p + 1)
            return carry

        lax.fori_loop(0, n_chunks // 2, fwd_pair, (ein_ref[:, :w], ein_ref[:, w:]))

        def project_dx(ch, stage):
            stage[...] = jnp.dot(dy_chunk(ch), c_ref[...], preferred_element_type=F32)

        def back_steps(stage, carry, g_buf=None):
            for i in reversed(range(SCAN_CHUNK)):
                blk = stage[i * n_rows:(i + 1) * n_rows, :]
                carry = (blk[:, :w] + ar * carry[0] + ai * carry[1], blk[:, w:] + ar * carry[1] - ai * carry[0])
                if g_buf is not None:
                    g_buf[i * n_rows:(i + 1) * n_rows, :w] = carry[0]
                    g_buf[i * n_rows:(i + 1) * n_rows, w:] = carry[1]
            return carry

        def first_pair(p, carry):
            project_dx(last - 2 * p - 1, st_b)
            carry = back_steps(st_a, carry)
            project_dx(jnp.maximum(last - 2 * p - 2, 0), st_a)
            return back_steps(st_b, carry)

        project_dx(last, st_a)
        sr, si = lax.fori_loop(0, n_chunks // 2, first_pair, (zero, zero))
        gr0, gi0 = _segment_carry(sr, si, ar, ai, n_rows, True)

        def post(g_buf, ch):
            g = g_buf[...]
            xp = xs_all[pl.ds(start(ch), rows_c), :]
            da_ref[:, :w] += jnp.sum(g[:, :w] * xp[:, :w] + g[:, w:] * xp[:, w:], axis=0, keepdims=True)
            da_ref[:, w:] += jnp.sum(g[:, w:] * xp[:, :w] - g[:, :w] * xp[:, w:], axis=0, keepdims=True)
            gb = g.astype(MXU_DTYPE)
            du_ref[pl.ds(start(ch), rows_c), :] = (lax.dot_general(gb, bb_ref[...], _NT, preferred_element_type=F32)
                                                   + dy_s[pl.ds(start(ch), rows_c), :] * dk_ref[...])
            dbb_ref[...] += lax.dot_general(u_ref[pl.ds(start(ch), rows_c), :].astype(MXU_DTYPE), gb, _TN,
                                            preferred_element_type=F32)

        def second_pair(p, carry):
            c1 = last - 2 * p
            project_dx(c1 - 1, st_b)
            post(buf_b, jnp.minimum(c1 + 1, last))
            carry = back_steps(st_a, carry, buf_a)
            project_dx(jnp.maximum(c1 - 2, 0), st_a)
            post(buf_a, c1)
            return back_steps(st_b, carry, buf_b)

        project_dx(last, st_a)
        buf_b[...] = jnp.zeros_like(buf_b)
        lax.fori_loop(0, n_chunks // 2, second_pair, (gr0, gi0))
        post(buf_b, 0)

    col = lambda width: pl.BlockSpec((t, width), lambda c, j: (0, c))
    wgt = pl.BlockSpec((SCAN_CH, 2 * w), lambda c, j: (c, 0))
    row = pl.BlockSpec((1, 2 * w), lambda c, j: (0, c))
    chan = pl.BlockSpec((1, SCAN_CH), lambda c, j: (0, c))
    return _grid_call(
        body, "ssm_bwd", (SCAN_NBLK, 1), [u_perm, dyg, ytot, dskip, a_cat, bbc, cc, ein],
        [col(SCAN_CH), col(SCAN_CH), col(SCAN_CH), chan, row, wgt, wgt,
         pl.BlockSpec((n_rows, 2 * w), lambda c, j: (0, c))],
        [col(SCAN_CH), chan, row, wgt, wgt],
        [jax.ShapeDtypeStruct((t, SSM_W), F32), jax.ShapeDtypeStruct((1, SSM_W), F32),
         jax.ShapeDtypeStruct((1, 2 * N_STATE), F32), jax.ShapeDtypeStruct((SSM_W, 2 * w), F32),
         jax.ShapeDtypeStruct((SSM_W, 2 * w), F32)],
        56 * 1024 * 1024, comm,
        scratch=[pltpu.VMEM((t + n_rows, 2 * w), F32), pltpu.VMEM((t, SCAN_CH), F32)]
        + [pltpu.VMEM((rows_c, 2 * w), F32)] * 4)


def _to_scan_rows(a, n_samples):
    c = a.shape[1]
    return a.reshape(n_samples, SCAN_SEG_PER_SAMPLE, SCAN_LEN, c).transpose(2, 0, 1, 3).reshape(-1, c)


def _from_scan_rows(a, n_samples):
    c = a.shape[1]
    return a.reshape(SCAN_LEN, n_samples, SCAN_SEG_PER_SAMPLE, c).transpose(1, 2, 0, 3).reshape(-1, c)


def _row_spec(tm, width):
    return pl.BlockSpec((tm, width), lambda i, j: (i, 0))


def _whole(arr):
    return pl.BlockSpec(arr.shape, lambda i, j: (0,) * arr.ndim)


def _proj_rope(x, g, w_in_t, tabs, comm=None):
    t = x.shape[0]
    tm = 256

    def body(x_ref, g_ref, w_ref, tc_ref, tlo_ref, thi_ref, h_ref, u_ref, gate_ref, *rest):
        qkv_refs, stage = rest[:9], rest[9]
        xv = x_ref[...]
        r = lax.rsqrt(jnp.mean(xv * xv, axis=-1, keepdims=True) + RMS_EPS)
        h = ((xv * r) * g_ref[...]).astype(BF16)
        h_ref[...] = h
        p = lax.dot_general(h.astype(MXU_DTYPE), w_ref[...], _NT, preferred_element_type=F32)
        u_ref[...] = p[:, QKV_W:QKV_W + SSM_W]
        gate_ref[...] = _sigmoid(p[:, QKV_W + SSM_W:])
        tc, tlo, thi = tc_ref[...], tlo_ref[...], thi_ref[...]
        n_ch = QKV_W // LANES
        for ch in range(n_ch):
            piece = p[:, _lane_chunk(ch)]
            stage[ch] = _rope_apply(piece, tc, tlo, thi) if ch < 2 * n_ch // 3 else piece
        halves = GROUP_W // LANES
        for grp, d in enumerate(DILATIONS):
            for which in range(3):
                out = qkv_refs[3 * grp + which]
                for res in range(d):
                    for half in range(halves):
                        ch = which * (n_ch // 3) + grp * halves + half
                        out[:, _lane_chunk(res * halves + half)] = _gather_residue(stage, ch, res, d, tm // d).astype(BF16)

    tab = pl.BlockSpec((tm, LANES), lambda i, j: (i % (SEQ // tm), 0))
    widths = [(D_MODEL, BF16), (SSM_W, F32), (2 * D_MODEL, F32)]
    out_specs = [_row_spec(tm, wd) for wd, _ in widths]
    out_shapes = [jax.ShapeDtypeStruct((t, wd), dt) for wd, dt in widths]
    for d in DILATIONS:
        out_specs += [_row_spec(tm // d, d * GROUP_W)] * 3
        out_shapes += [jax.ShapeDtypeStruct((t // d, d * GROUP_W), BF16)] * 3
    return _grid_call(
        body, "proj_rope", (t // tm, 1), [x, g, w_in_t, *tabs],
        [_row_spec(tm, D_MODEL), _whole(g), _whole(w_in_t), tab, tab, tab], out_specs, out_shapes, VMEM_BIG, comm,
        scratch=[pltpu.VMEM((QKV_W // LANES, tm, LANES), F32)])


def _branch_outputs(attn_ref, yg_ref, wao_ref, wglu_ref):
    attn_d = lax.dot_general(attn_ref[...].astype(MXU_DTYPE), wao_ref[...], _NT, preferred_element_type=F32)
    z = lax.dot_general(yg_ref[...].astype(MXU_DTYPE), wglu_ref[...], _NT, preferred_element_type=F32)
    return attn_d, z[:, :D_MODEL], _sigmoid(z[:, D_MODEL:])


def _mix_out_rms(os_, lses, yg, gates, x, w_ao_t, w_glu_t, w_out, g, comm=None):
    t = x.shape[0]
    tm = 256

    def body(o0, o1, o2, l0, l1, l2, yg_ref, gate_ref, x_ref, wao_ref, wglu_ref, wout_ref, g_ref,
             attn_ref, lt_ref, m_ref, x1_ref, h_ref, nat):
        _merge_groups((o0, o1, o2), (l0, l1, l2), attn_ref, lt_ref, nat, tm)
        attn_d, za, sb = _branch_outputs(attn_ref, yg_ref, wao_ref, wglu_ref)
        merged = (gate_ref[:, :D_MODEL] * attn_d + gate_ref[:, D_MODEL:] * (za * sb)).astype(BF16)
        m_ref[...] = merged
        x1 = x_ref[...] + jnp.dot(merged.astype(MXU_DTYPE), wout_ref[...], preferred_element_type=F32)
        x1_ref[...] = x1
        r = lax.rsqrt(jnp.mean(x1 * x1, axis=-1, keepdims=True) + RMS_EPS)
        h_ref[...] = ((x1 * r) * g_ref[...]).astype(BF16)

    dil_specs = [_row_spec(tm // d, d * GROUP_W) for d in DILATIONS] * 2
    return _grid_call(
        body, "mix_out_rms", (t // tm, 1), [*os_, *lses, yg, gates, x, w_ao_t, w_glu_t, w_out, g],
        dil_specs + [_row_spec(tm, SSM_W), _row_spec(tm, 2 * D_MODEL), _row_spec(tm, D_MODEL),
                     _whole(w_ao_t), _whole(w_glu_t), _whole(w_out), _whole(g)],
        [_row_spec(tm, GROUP_W)] * 2 + [_row_spec(tm, D_MODEL)] * 3,
        [jax.ShapeDtypeStruct((t, GROUP_W), F32)] * 2
        + [jax.ShapeDtypeStruct((t, D_MODEL), BF16), jax.ShapeDtypeStruct((t, D_MODEL), F32),
           jax.ShapeDtypeStruct((t, D_MODEL), BF16)], VMEM_BIG, comm, scratch=[pltpu.VMEM((8, tm, LANES), F32)])


def _mix_bwd(dx1b, attn, lse_tot, yg, gates, w_ao_t, w_glu_t, w_out, comm=None):
    t = dx1b.shape[0]
    tm = 256

    def body(dx_ref, attn_ref, lt_ref, yg_ref, gate_ref, wao_ref, wglu_ref, wout_ref, ones_ref,
             dad_ref, dz_ref, dg_ref, da_ref, dyg_ref, rd_ref, *rest):
        dm = lax.dot_general(dx_ref[...], wout_ref[...], _NT, preferred_element_type=F32)
        attn_d, za, sb = _branch_outputs(attn_ref, yg_ref, wao_ref, wglu_ref)
        g0, g1 = gate_ref[:, :D_MODEL], gate_ref[:, D_MODEL:]
        dad = (dm * g0).astype(BF16)
        dad_ref[...] = dad
        ds = dm * g1
        dza, dzb = (ds * sb).astype(BF16), (ds * za * sb * (1.0 - sb)).astype(BF16)
        dz_ref[:, :D_MODEL] = dza
        dz_ref[:, D_MODEL:] = dzb
        dg_ref[:, :D_MODEL] = (dm * attn_d * g0 * (1.0 - g0)).astype(BF16)
        dg_ref[:, D_MODEL:] = (dm * (za * sb) * g1 * (1.0 - g1)).astype(BF16)
        da = jnp.dot(dad.astype(MXU_DTYPE), wao_ref[...], preferred_element_type=F32)
        da_ref[...] = da
        dyg_ref[...] = (jnp.dot(dza.astype(MXU_DTYPE), wglu_ref[:D_MODEL, :], preferred_element_type=F32)
                        + jnp.dot(dzb.astype(MXU_DTYPE), wglu_ref[D_MODEL:, :], preferred_element_type=F32))
        _attention_cotangents(da, attn_ref[...], lt_ref[...], ones_ref[...], rd_ref, rest[:6], rest[6], tm)

    widths = [(D_MODEL, BF16), (2 * D_MODEL, BF16), (2 * D_MODEL, BF16), (GROUP_W, F32), (SSM_W, F32), (GROUP_W, F32)]
    out_specs = [_row_spec(tm, wd) for wd, _ in widths]
    out_shapes = [jax.ShapeDtypeStruct((t, wd), dt) for wd, dt in widths]
    for d in DILATIONS[1:]:
        out_specs += [_row_spec(tm // d, d * GROUP_W)] * 3
        out_shapes += [jax.ShapeDtypeStruct((t // d, d * GROUP_W), F32)] * 3
    ones = _head_sum_matrix()
    return _grid_call(
        body, "mix_bwd", (t // tm, 1), [dx1b, attn, lse_tot, yg, gates, w_ao_t, w_glu_t, w_out, ones],
        [_row_spec(tm, D_MODEL), _row_spec(tm, GROUP_W), _row_spec(tm, GROUP_W), _row_spec(tm, SSM_W),
         _row_spec(tm, 2 * D_MODEL), _whole(w_ao_t), _whole(w_glu_t), _whole(w_out), _whole(ones)],
        out_specs, out_shapes, VMEM_BIG, comm, scratch=[pltpu.VMEM((6, tm, LANES), F32)])


FFN_TN = D_FF // 2
MXU_COLS = 256


def _ffn_in_swiglu(h2, w_gate_t, w_up_t, comm=None):
    t = h2.shape[0]
    tm = 512

    def body(h_ref, wg_ref, wu_ref, a_ref, b_ref, f_ref):
        h = h_ref[...].astype(MXU_DTYPE)
        for c0 in range(0, FFN_TN, MXU_COLS):
            sl = slice(c0, min(c0 + MXU_COLS, FFN_TN))
            a = lax.dot_general(h, wg_ref[sl, :], _NT, preferred_element_type=F32)
            b = lax.dot_general(h, wu_ref[sl, :], _NT, preferred_element_type=F32)
            a_ref[:, sl] = a
            b_ref[:, sl] = b
            f_ref[:, sl] = (a * _sigmoid(a) * b).astype(BF16)

    tile = pl.BlockSpec((tm, FFN_TN), lambda j, i: (i, j))
    wspec = pl.BlockSpec((FFN_TN, D_MODEL), lambda j, i: (j, 0))
    return _grid_call(
        body, "ffn_in_swiglu", (D_FF // FFN_TN, t // tm), [h2, w_gate_t, w_up_t],
        [pl.BlockSpec((tm, D_MODEL), lambda j, i: (i, 0)), wspec, wspec],
        [tile] * 3, [jax.ShapeDtypeStruct((t, D_FF), F32)] * 2 + [jax.ShapeDtypeStruct((t, D_FF), BF16)], VMEM_BIG, comm)


def _ffn_down_final(f, w_down, x1, target, g):
    t = x1.shape[0]
    tm = 256

    def body(f_ref, w_ref, x1_ref, t_ref, g_ref, dx_ref, dxb_ref, loss_ref, gg_ref):
        @pl.when(pl.program_id(0) == 0)
        def _():
            loss_ref[...] = jnp.zeros_like(loss_ref)
            gg_ref[...] = jnp.zeros_like(gg_ref)

        xv = x1_ref[...] + jnp.dot(f_ref[...].astype(MXU_DTYPE), w_ref[...], preferred_element_type=F32)
        gv = g_ref[...]
        r = lax.rsqrt(jnp.mean(xv * xv, axis=-1, keepdims=True) + RMS_EPS)
        n = xv * r
        diff = n * gv - t_ref[...]
        per_tok = jnp.mean(diff * diff, axis=-1, keepdims=True)
        loss_ref[...] += 0.5 * jnp.sum(per_tok, axis=0, keepdims=True)
        dy = diff / xv.shape[-1]
        gg_ref[...] += jnp.sum(dy * n, axis=0, keepdims=True)
        dn = dy * gv
        dx = r * (dn - n * jnp.mean(dn * n, axis=-1, keepdims=True))
        dx_ref[...] = dx
        dxb_ref[...] = dx.astype(BF16)

    acc = lambda shp: pl.BlockSpec(shp, lambda i, j: (0, 0))
    return _grid_call(
        body, "ffn_down_final", (t // tm, 1), [f, w_down, x1, target, g],
        [_row_spec(tm, D_FF), _whole(w_down), _row_spec(tm, D_MODEL), _row_spec(tm, D_MODEL), _whole(g)],
        [_row_spec(tm, D_MODEL)] * 2 + [acc((8, LANES)), acc((1, D_MODEL))],
        [jax.ShapeDtypeStruct((t, D_MODEL), F32), jax.ShapeDtypeStruct((t, D_MODEL), BF16),
         jax.ShapeDtypeStruct((8, LANES), F32), jax.ShapeDtypeStruct((1, D_MODEL), F32)], VMEM_BIG, sequential=True)


def _d_f_swiglu_bwd(dx2b, w_down, a, b):
    t = a.shape[0]
    tm = 512

    def body(dx_ref, w_ref, a_ref, b_ref, da_ref, db_ref):
        d = lax.dot_general(dx_ref[...], w_ref[...], _NT, preferred_element_type=F32)
        av, bv = a_ref[...], b_ref[...]
        sg = _sigmoid(av)
        da_ref[...] = (d * bv * sg * (1.0 + av * (1.0 - sg))).astype(BF16)
        db_ref[...] = (d * av * sg).astype(BF16)

    tile = pl.BlockSpec((tm, FFN_TN), lambda j, i: (i, j))
    return _grid_call(
        body, "d_f_swiglu_bwd", (D_FF // FFN_TN, t // tm), [dx2b, w_down, a, b],
        [pl.BlockSpec((tm, D_MODEL), lambda j, i: (i, 0)), pl.BlockSpec((FFN_TN, D_MODEL), lambda j, i: (j, 0)), tile, tile],
        [tile] * 2, [jax.ShapeDtypeStruct((t, D_FF), BF16)] * 2, VMEM_BIG)


def _mm_rms_bwd(operands, weights, x, g, dres, name, comm=None):
    t = x.shape[0]
    tm = 256
    n_op = len(operands)

    def body(*refs):
        a_refs, w_refs = refs[:n_op], refs[n_op:2 * n_op]
        x_ref, g_ref, dres_ref, dx_ref, dxb_ref, gg_ref = refs[2 * n_op:]

        @pl.when(pl.program_id(0) == 0)
        def _():
            gg_ref[...] = jnp.zeros_like(gg_ref)

        dh = None
        for a_ref, w_ref in zip(a_refs, w_refs):
            part = jnp.dot(a_ref[...].astype(MXU_DTYPE), w_ref[...], preferred_element_type=F32)
            dh = part if dh is None else dh + part
        xv = x_ref[...]
        r = lax.rsqrt(jnp.mean(xv * xv, axis=-1, keepdims=True) + RMS_EPS)
        n = xv * r
        gg_ref[...] += jnp.sum(dh * n, axis=0, keepdims=True)
        dn = dh * g_ref[...]
        dx = dres_ref[...] + r * (dn - n * jnp.mean(dn * n, axis=-1, keepdims=True))
        dx_ref[...] = dx
        dxb_ref[...] = dx.astype(BF16)

    d = x.shape[1]
    return _grid_call(
        body, name, (t // tm, 1), [*operands, *weights, x, g, dres],
        [_row_spec(tm, a.shape[1]) for a in operands] + [_whole(wk) for wk in weights]
        + [_row_spec(tm, d), _whole(g), _row_spec(tm, d)],
        [_row_spec(tm, d)] * 2 + [pl.BlockSpec((1, d), lambda i, j: (0, 0))],
        [jax.ShapeDtypeStruct((t, d), F32), jax.ShapeDtypeStruct((t, d), BF16), jax.ShapeDtypeStruct((1, d), F32)],
        VMEM_BIG, comm, sequential=True)


def _flat_small(small):
    perm_b = lambda a: a.reshape(SSM_GROUPS, SSM_STATE, SSM_CH).transpose(2, 0, 1).reshape(SSM_CH, N_STATE)
    perm_c = lambda a: a.reshape(SSM_GROUPS, SSM_CH, SSM_STATE).transpose(1, 0, 2).reshape(SSM_CH, N_STATE)
    return dict(
        g_mix=small["norm_mix_g"].reshape(1, D_MODEL), g_ffn=small["norm_ffn_g"].reshape(1, D_MODEL),
        g_fin=small["norm_final_g"].reshape(1, D_MODEL),
        lr=small["ssm_a_re"].reshape(1, N_STATE), li=small["ssm_a_im"].reshape(1, N_STATE),
        ldt=jnp.repeat(small["ssm_log_dt"].reshape(SSM_GROUPS), SSM_STATE).reshape(1, N_STATE),
        br=perm_b(small["ssm_b_re"]), bi=perm_b(small["ssm_b_im"]),
        cr=perm_c(small["ssm_c_re"]), ci=perm_c(small["ssm_c_im"]), dskip=small["ssm_d"].reshape(1, SSM_W))


AG_HOSTS = {"proj_rope": ("w_glu", "w_attn_out", "w_out", "w_ffn_gate"), "mix_out_rms": ("w_ffn_up",),
            "ffn_in_swiglu": ("w_ffn_down",)}
HALVED = ("w_ffn_gate", "w_ffn_up", "w_in")
A2A_HOSTS = {"d_h2_rms": ("w_ffn_down",), "mix_bwd": ("w_ffn_gate:0", "w_out"), "attn_bwd_g0": ("w_ffn_up:1",),
             "attn_bwd_g1": ("w_glu",), "attn_bwd_g2": ("w_attn_out",), "ssm_bwd": ("w_ffn_gate:1", "w_ffn_up:0"),
             "mm_g_in1": ("w_in:0",), "d_h0_rms": ("w_in:1",)}
SMALL_HOST = "mm_g_in0"


def _local_step(x, target, w, small, shards=None):
    t = x.shape[0]
    n_samples = t // SEQ
    n_rows = n_samples * SCAN_SEG_PER_SAMPLE
    tabs = _rope_tables()
    w = dict(w)
    fs = _flat_small(small)
    g_mix, g_ffn, g_fin, dskip = fs["g_mix"], fs["g_ffn"], fs["g_fin"], fs["dskip"]
    a_cat, bbc, cc = _ssm_disc(fs["lr"], fs["li"], fs["ldt"], fs["br"], fs["bi"], fs["cr"], fs["ci"])
    big, recv, small_pack = {}, {}, []

    def comm_of(name):
        if shards is None:
            return None
        if name == SMALL_HOST:
            return _ag_comm([(small_pack[0], 0, 0)], [(N_DEV, *small_pack[0].shape)])
        if name in AG_HOSTS:
            names = AG_HOSTS[name]
            return _ag_comm([(shards[n], j, 0) for j, n in enumerate(names)], [(N_DEV, *shards[n].shape) for n in names])
        if name in A2A_HOSTS:
            return _a2a_comm([(big[n].reshape(N_DEV, -1, big[n].shape[1]), 0) for n in A2A_HOSTS[name]])
        return None

    def absorb(name, carried):
        if name == SMALL_HOST:
            recv["small"] = carried[0]
        for n, a3 in zip(AG_HOSTS.get(name, ()), carried):
            w[n] = a3.reshape(-1, a3.shape[2])
        for n, a3 in zip(A2A_HOSTS.get(name, ()), carried):
            recv[n] = a3

    def mm(a, b, mode, name, tm, tn, **kw):
        comm = comm_of(name)
        if comm is None:
            return _mm(a, b, mode, name, tm, tn, **kw)
        out, *carried = _mm(a, b, mode, name, tm, tn, comm=comm, **kw)
        absorb(name, carried)
        return out

    h0, u, gates, *rest = _proj_rope(x, g_mix, w["w_in"], tabs, comm_of("proj_rope"))
    qkv = [rest[3 * g:3 * g + 3] for g in range(3)]
    absorb("proj_rope", rest[9:])
    os_, lses = [], []
    for g in range(3):
        o_g, l_g, carried = _attn_fwd(*qkv[g], g, n_samples, comm_of(f"attn_fwd_g{g}"))
        absorb(f"attn_fwd_g{g}", carried)
        os_.append(o_g)
        lses.append(l_g)
    u_perm = _to_scan_rows(u, n_samples)
    ytot, yg_perm, ein = _ssm_fwd(u_perm, a_cat, bbc, cc, dskip, n_rows)
    yg = _from_scan_rows(yg_perm, n_samples)

    attn, lse_tot, merged, x1, h2, *carried = _mix_out_rms(os_, lses, yg, gates, x, w["w_attn_out"], w["w_glu"], w["w_out"],
                                                           g_ffn, comm_of("mix_out_rms"))
    absorb("mix_out_rms", carried)
    ffn_a, ffn_b, f, *carried = _ffn_in_swiglu(h2, w["w_ffn_gate"], w["w_ffn_up"], comm_of("ffn_in_swiglu"))
    absorb("ffn_in_swiglu", carried)
    dx2, dx2b, loss_blk, g_gfin = _ffn_down_final(f, w["w_ffn_down"], x1, target, g_fin)

    da, db = _d_f_swiglu_bwd(dx2b, w["w_ffn_down"], ffn_a, ffn_b)
    big["w_ffn_down"] = mm(f, dx2b, "tn", "mm_g_down", 256, D_MODEL, out_dtype=BF16)
    half = D_MODEL // 2
    for hf in range(2):
        big[f"w_ffn_gate:{hf}"] = mm(da, h2, "tn", f"mm_g_gate{hf}", 256, half, out_dtype=BF16, cols=(hf * half, half))
        big[f"w_ffn_up:{hf}"] = mm(db, h2, "tn", f"mm_g_up{hf}", 256, half, out_dtype=BF16, cols=(hf * half, half))
    dx1, dx1b, g_gffn, *carried = _mm_rms_bwd([da, db], [w["w_ffn_gate"], w["w_ffn_up"]], x1, g_ffn, dx2, "d_h2_rms",
                                              comm_of("d_h2_rms"))
    absorb("d_h2_rms", carried)

    big["w_out"] = mm(merged, dx1b, "tn", "mm_g_out", 256, D_MODEL, out_dtype=BF16)
    dattn_d, dz, dgpre, dattn, dyg, rowdot, *rest = _mix_bwd(dx1b, attn, lse_tot, yg, gates, w["w_attn_out"], w["w_glu"],
                                                             w["w_out"], comm_of("mix_bwd"))
    cot = [(dattn, lse_tot, rowdot), tuple(rest[:3]), tuple(rest[3:6])]
    absorb("mix_bwd", rest[6:])

    big["w_attn_out"] = mm(dattn_d, attn, "tn", "mm_g_attn_out", 512, GROUP_W, out_dtype=BF16)
    big["w_glu"] = mm(dz, yg, "tn", "mm_g_glu", 512, 512, out_dtype=BF16)
    dqs, dks, dvs = [], [], []
    for g in range(3):
        dq_g, dk_g, dv_g, carried = _attn_bwd(*qkv[g], *cot[g], g, n_samples, comm_of(f"attn_bwd_g{g}"))
        absorb(f"attn_bwd_g{g}", carried)
        dqs.append(dq_g)
        dks.append(dk_g)
        dvs.append(dv_g)

    dyg_perm = _to_scan_rows(dyg, n_samples)
    du_perm, g_dskip, da_cat, dbb_full, dc_full, *carried = _ssm_bwd(u_perm, dyg_perm, ytot, dskip, a_cat, bbc, cc, ein,
                                                                   n_rows, comm_of("ssm_bwd"))
    absorb("ssm_bwd", carried)
    du = _from_scan_rows(du_perm, n_samples)
    g_lr, g_li, g_ldt, g_br, g_bi, g_cr, g_ci = _ssm_param_bwd(
        fs["lr"], fs["li"], fs["ldt"], fs["br"], fs["bi"], da_cat, dbb_full, dc_full)

    small_pack.append(_pack_small(dict(lr=g_lr, li=g_li, ldt=g_ldt, br=g_br, bi=g_bi, cr=g_cr, ci=g_ci, dskip=g_dskip,
                                       g_ffn=g_gffn, g_fin=g_gfin, loss=loss_blk)))

    dproj = _pack_dproj(dqs, dks, dvs, du, dgpre, tabs)
    for hf in range(2):
        big[f"w_in:{hf}"] = mm(dproj, h0, "tn", f"mm_g_in{hf}", 256, half, out_dtype=BF16, cols=(hf * half, half))
    grad_x, _, g_gmix, *carried = _mm_rms_bwd([dproj], [w["w_in"]], x, g_mix, dx1, "d_h0_rms", comm_of("d_h0_rms"))
    absorb("d_h0_rms", carried)
    return grad_x, (big if shards is None else recv), small_pack[0], g_gmix


_MESH = pl.DeviceIdType.MESH


def _all_gather(block, name):
    rows, lanes = block.shape

    def body(x_ref, out_ref, send_sems, recv_sems, local_sem):
        x, y, c = lax.axis_index("x"), lax.axis_index("y"), lax.axis_index("c")
        me, sibling = (x, y, c), (x, y, 1 - c)
        chips = [(1 - x, y), (x, 1 - y), (1 - x, 1 - y)]

        def slot(px, py, pc):
            return out_ref.at[4 * px + 2 * py + pc]

        def copy(k, blk, to, src=None):
            return pltpu.make_async_remote_copy(
                src_ref=slot(*blk) if src is None else src, dst_ref=slot(*blk), send_sem=send_sems.at[k],
                recv_sem=recv_sems.at[k], device_id=to, device_id_type=_MESH)

        mine = pltpu.make_async_copy(x_ref, slot(*me), local_sem)
        mine.start()
        first = [copy(0, me, sibling, src=x_ref)]
        first += [copy(1 + j, me, (*chip, c), src=x_ref) for j, chip in enumerate(chips)]
        for cp in first:
            cp.start()
        passed = [copy(4 + j, (*chip, c), sibling) for j, chip in enumerate(chips)]
        for j, chip in enumerate(chips):
            copy(1 + j, (*chip, c), me).wait_recv()
            passed[j].start()
        copy(0, sibling, me).wait_recv()
        for j, chip in enumerate(chips):
            copy(4 + j, (*chip, 1 - c), me).wait_recv()
        for cp in first + passed:
            cp.wait_send()
        mine.wait()

    return _pallas_call(
        body, name=name, out_shape=jax.ShapeDtypeStruct((N_DEV, rows, lanes), block.dtype),
        in_specs=[pl.BlockSpec(memory_space=pl.ANY)], out_specs=pl.BlockSpec(memory_space=pl.ANY),
        scratch_shapes=[pltpu.SemaphoreType.DMA((7,)), pltpu.SemaphoreType.DMA((7,)), pltpu.SemaphoreType.DMA],
    )(block)


def _ag_comm(items, bufs):
    def plan(in_refs, out_refs, send_sems, recv_sems, local_sems):
        x, y, c = lax.axis_index("x"), lax.axis_index("y"), lax.axis_index("c")
        me, sibling = (x, y, c), (x, y, 1 - c)
        chips = [(1 - x, y), (x, 1 - y), (1 - x, 1 - y)]
        plans = []
        for t, (_, buf, slot0) in enumerate(items):
            x_ref, out_ref = in_refs[t], out_refs[buf]

            def slot(px, py, pc, out_ref=out_ref, slot0=slot0):
                return out_ref.at[slot0 + 4 * px + 2 * py + pc]

            def copy(k, blk, to, src=None, t=t, slot=slot):
                return pltpu.make_async_remote_copy(
                    src_ref=slot(*blk) if src is None else src, dst_ref=slot(*blk), send_sem=send_sems.at[7 * t + k],
                    recv_sem=recv_sems.at[7 * t + k], device_id=to, device_id_type=_MESH)

            plans.append(dict(
                mine=pltpu.make_async_copy(x_ref, slot(*me), local_sems.at[t]),
                first=[copy(0, me, sibling, src=x_ref)] + [copy(1 + j, me, (*chip, c), src=x_ref)
                                                           for j, chip in enumerate(chips)],
                passed=[copy(4 + j, (*chip, c), sibling) for j, chip in enumerate(chips)],
                from_ici=[copy(1 + j, (*chip, c), me) for j, chip in enumerate(chips)],
                from_sibling=[copy(0, sibling, me)] + [copy(4 + j, (*chip, 1 - c), me) for j, chip in enumerate(chips)]))
        return plans

    def start(*refs):
        for p in plan(*refs):
            p["mine"].start()
            for cp in p["first"]:
                cp.start()

    def finish(*refs):
        plans = plan(*refs)
        for p in plans:
            for arrived, onward in zip(p["from_ici"], p["passed"]):
                arrived.wait_recv()
                onward.start()
        for p in plans:
            for arrived in p["from_sibling"]:
                arrived.wait_recv()
            for cp in p["first"] + p["passed"]:
                cp.wait_send()
            p["mine"].wait()

    dtype_of = {buf: shard.dtype for shard, buf, _ in items}
    out_shapes = [jax.ShapeDtypeStruct(b, dtype_of[j]) for j, b in enumerate(bufs)]
    return _Comm([it[0] for it in items], out_shapes, 7 * len(items), len(items), start, finish)


def _a2a_comm(items):
    def plan(in_refs, out_refs, send_sems, recv_sems, local_sems):
        x, y, c = lax.axis_index("x"), lax.axis_index("y"), lax.axis_index("c")
        my = 4 * x + 2 * y + c
        copies, locals_ = [], []
        for t, (_, slot0) in enumerate(items):
            s_ref, r_ref = in_refs[t], out_refs[t]
            locals_.append(pltpu.make_async_copy(s_ref.at[slot0 + my], r_ref.at[my], local_sems.at[t]))
            for kk in range(1, N_DEV):
                px = 1 - x if kk & 4 else x
                py = 1 - y if kk & 2 else y
                pc = 1 - c if kk & 1 else c
                copies.append(pltpu.make_async_remote_copy(
                    src_ref=s_ref.at[slot0 + 4 * px + 2 * py + pc], dst_ref=r_ref.at[my],
                    send_sem=send_sems.at[7 * t + kk - 1], recv_sem=recv_sems.at[7 * t + kk - 1],
                    device_id=(px, py, pc), device_id_type=_MESH))
        return copies, locals_

    def start(*refs):
        copies, locals_ = plan(*refs)
        for cp in locals_ + copies:
            cp.start()

    def finish(*refs):
        copies, locals_ = plan(*refs)
        for cp in copies + locals_:
            cp.wait()

    out_shapes = [jax.ShapeDtypeStruct((N_DEV,) + it[0].shape[1:], it[0].dtype) for it in items]
    return _Comm([it[0] for it in items], out_shapes, 7 * len(items), len(items), start, finish)


def _adam_math(g, w, m, v):
    m_new = ADAM_B1 * m + (1.0 - ADAM_B1) * g
    v_new = ADAM_B2 * v + (1.0 - ADAM_B2) * jnp.square(g)
    m_hat = m_new / (1.0 - ADAM_B1 ** ADAM_STEP)
    v_hat = v_new / (1.0 - ADAM_B2 ** ADAM_STEP)
    return -ADAM_LR * (m_hat / (jnp.sqrt(v_hat) + ADAM_EPS) + ADAM_WD * w), m_new, v_new


def _sum_partials(parts, name, tm):
    n, rows, _ = parts[0].shape
    widths = [p.shape[2] for p in parts]

    def body(*refs):
        g_ref, off = refs[-1], 0
        for p_ref, wd in zip(refs[:-1], widths):
            g = p_ref[0].astype(F32)
            for s in range(1, n):
                g = g + p_ref[s].astype(F32)
            g_ref[:, off:off + wd] = g
            off += wd

    return _pallas_call(
        body, name=name, grid=(rows // tm,), in_specs=[pl.BlockSpec((n, tm, wd), lambda i: (0, i, 0)) for wd in widths],
        out_specs=pl.BlockSpec((tm, sum(widths)), lambda i: (i, 0)),
        out_shape=jax.ShapeDtypeStruct((rows, sum(widths)), F32),
        compiler_params=pltpu.CompilerParams(dimension_semantics=("parallel",), vmem_limit_bytes=VMEM_MID),
    )(*parts)


def _adam(partials, w, m, v, name, tm):
    n, rows, cols = partials.shape

    def body(p_ref, w_ref, m_ref, v_ref, g_ref, d_ref, nm_ref, nv_ref):
        g = p_ref[0].astype(F32)
        for s in range(1, n):
            g = g + p_ref[s].astype(F32)
        g_ref[...] = g
        d_ref[...], nm_ref[...], nv_ref[...] = _adam_math(g, w_ref[...], m_ref[...], v_ref[...])

    assert rows % tm == 0
    row = pl.BlockSpec((tm, cols), lambda i: (i, 0))
    shp = jax.ShapeDtypeStruct((rows, cols), F32)
    return _pallas_call(
        body, name=name, grid=(rows // tm,),
        in_specs=[pl.BlockSpec((n, tm, cols), lambda i: (0, i, 0)), row, row, row],
        out_specs=[row] * 4, out_shape=[shp] * 4,
        compiler_params=pltpu.CompilerParams(dimension_semantics=("parallel",), vmem_limit_bytes=VMEM_MID),
    )(partials, w, m, v)


_PK_LR, _PK_LI, _PK_GAINS, _PK_MISC, _PK_BR, _PK_BI, _PK_CR, _PK_CI, _PK_ROWS = 0, 1, 2, 3, 8, 24, 40, 56, 72
_PK_LDT_LANE, _PK_LOSS_LANE = D_MODEL + SSM_W, D_MODEL + SSM_W + LANES


def _pack_small(sg):
    names = ("lr", "li", "g_ffn", "g_fin", "dskip", "ldt", "loss", "br", "bi", "cr", "ci")

    def body(lr, li, gffn, gfin, dskip, ldt, loss, br, bi, cr, ci, o_ref):
        o_ref[...] = jnp.zeros_like(o_ref)
        o_ref[_PK_LR:_PK_LR + 1, :] = lr[...]
        o_ref[_PK_LI:_PK_LI + 1, :] = li[...]
        o_ref[_PK_GAINS:_PK_GAINS + 1, D_MODEL:] = gffn[...]
        o_ref[_PK_MISC:_PK_MISC + 1, :D_MODEL] = gfin[...]
        o_ref[_PK_MISC:_PK_MISC + 1, D_MODEL:D_MODEL + SSM_W] = dskip[...]
        o_ref[_PK_MISC:_PK_MISC + 1, _PK_LDT_LANE:_PK_LDT_LANE + LANES] = ldt[0:1, :]
        o_ref[_PK_MISC:_PK_MISC + 1, _PK_LOSS_LANE:_PK_LOSS_LANE + LANES] = loss[0:1, :]
        o_ref[_PK_BR:_PK_BR + SSM_CH, :] = br[...]
        o_ref[_PK_BI:_PK_BI + SSM_CH, :] = bi[...]
        o_ref[_PK_CR:_PK_CR + SSM_CH, :] = cr[...]
        o_ref[_PK_CI:_PK_CI + SSM_CH, :] = ci[...]

    return _pallas_call(body, name="pack_small", out_shape=jax.ShapeDtypeStruct((_PK_ROWS, N_STATE), F32))(
        *[sg[n] for n in names])


def _unpack_small(s, g_mix):
    unflat_b = lambda a: a.reshape(SSM_CH, SSM_GROUPS, SSM_STATE).transpose(1, 2, 0)[None]
    unflat_c = lambda a: a.reshape(SSM_CH, SSM_GROUPS, SSM_STATE).transpose(1, 0, 2)[None]
    grads = {
        "norm_mix_g": g_mix, "norm_ffn_g": s[_PK_GAINS, D_MODEL:].reshape(1, D_MODEL),
        "norm_final_g": s[_PK_MISC, :D_MODEL],
        "ssm_a_re": s[_PK_LR].reshape(1, SSM_GROUPS, SSM_STATE), "ssm_a_im": s[_PK_LI].reshape(1, SSM_GROUPS, SSM_STATE),
        "ssm_log_dt": s[_PK_MISC, _PK_LDT_LANE:_PK_LDT_LANE + SSM_GROUPS].reshape(1, SSM_GROUPS),
        "ssm_d": s[_PK_MISC, D_MODEL:D_MODEL + SSM_W].reshape(1, SSM_GROUPS, SSM_CH),
        "ssm_b_re": unflat_b(s[_PK_BR:_PK_BR + SSM_CH]), "ssm_b_im": unflat_b(s[_PK_BI:_PK_BI + SSM_CH]),
        "ssm_c_re": unflat_c(s[_PK_CR:_PK_CR + SSM_CH]), "ssm_c_im": unflat_c(s[_PK_CI:_PK_CI + SSM_CH]),
    }
    return s[_PK_MISC, _PK_LOSS_LANE], grads


def _adam_small(grads, wts, moms, vars_):
    n = len(SMALL_WEIGHTS)
    as2d = lambda a: a.reshape(1, -1) if a.ndim == 1 else a

    def body(*refs):
        ins, outs = refs[:4 * n], refs[4 * n:]
        for i in range(n):
            g, w, m, v = (ins[j * n + i][...] for j in range(4))
            outs[i][...], outs[n + i][...], outs[2 * n + i][...] = _adam_math(g, w, m, v)

    operands = [as2d(d[k]) for d in (grads, wts, moms, vars_) for k in SMALL_WEIGHTS]
    shapes = [jax.ShapeDtypeStruct(as2d(wts[k]).shape, F32) for k in SMALL_WEIGHTS] * 3
    res = _pallas_call(body, name="adam_small", out_shape=shapes,
                         compiler_params=pltpu.CompilerParams(vmem_limit_bytes=VMEM_BIG))(*operands)
    out = {}
    for j, kind in enumerate(("delta", "new_m", "new_v")):
        for i, k in enumerate(SMALL_WEIGHTS):
            out[kind, k] = res[j * n + i].reshape(wts[k].shape)
    return out


def kernel(x, norm_mix_g, w_in, ssm_a_re, ssm_a_im, ssm_log_dt, ssm_b_re, ssm_b_im, ssm_c_re, ssm_c_im, ssm_d, w_glu, w_attn_out, w_out, norm_ffn_g, w_ffn_gate, w_ffn_up, w_ffn_down, norm_final_g, loss_target, m_norm_mix_g, m_w_in, m_ssm_a_re, m_ssm_a_im, m_ssm_log_dt, m_ssm_b_re, m_ssm_b_im, m_ssm_c_re, m_ssm_c_im, m_ssm_d, m_w_glu, m_w_attn_out, m_w_out, m_norm_ffn_g, m_w_ffn_gate, m_w_ffn_up, m_w_ffn_down, m_norm_final_g, v_norm_mix_g, v_w_in, v_ssm_a_re, v_ssm_a_im, v_ssm_log_dt, v_ssm_b_re, v_ssm_b_im, v_ssm_c_re, v_ssm_c_im, v_ssm_d, v_w_glu, v_w_attn_out, v_w_out, v_norm_ffn_g, v_w_ffn_gate, v_w_ffn_up, v_w_ffn_down, v_norm_final_g):
    args = dict(locals())
    wts = {n: args[n] for n in ALL_WEIGHTS}
    moms = {n: args["m_" + n] for n in ALL_WEIGHTS}
    vars_ = {n: args["v_" + n] for n in ALL_WEIGHTS}
    n_samples = x.shape[0]
    t = n_samples * SEQ

    shards = {n: (wts[n][0] if n in ROW_SHARDED else wts[n][0].T).astype(BF16) for n in BIG_WEIGHTS}
    w_in_t = _all_gather(shards["w_in"], "allgather_w_in").reshape(IN_W, D_MODEL)

    small = {n: wts[n] for n in SMALL_WEIGHTS}
    grad_x, recv, _, g_mix_part = _local_step(x.reshape(t, D_MODEL), loss_target.reshape(t, D_MODEL), {"w_in": w_in_t},
                                              small, shards)

    results = {}
    for n in BIG_WEIGHTS:
        c, k = shards[n].shape
        w2, m2, v2 = wts[n][0], moms[n][0], vars_[n][0]
        if n in ROW_SHARDED:
            res = _adam(recv[n], w2, m2, v2, "adam_" + n, c // 2)
        else:
            parts = [recv[f"{n}:{hf}"] for hf in range(2)] if n in HALVED else [recv[n]]
            g_t = _sum_partials(parts, "sum_" + n, c // 2)
            res = _adam(g_t.T[None], w2, m2, v2, "adam_" + n, k // 2)
        for kind, a in zip(("grad", "delta", "new_m", "new_v"), res):
            results[kind, n] = a[None]

    g_mix_all = _all_gather(jnp.pad(g_mix_part, ((0, 7), (0, 0))), "allgather_g_mix")
    g_mix = _sum_partials([g_mix_all], "sum_g_mix", 8)[0:1]
    loss, sgrads = _unpack_small(_sum_partials([recv["small"]], "sum_small", _PK_ROWS), g_mix)
    for n in SMALL_WEIGHTS:
        results["grad", n] = sgrads[n]
    results.update(_adam_small(sgrads, wts, moms, vars_))
    outs = [loss, grad_x.reshape(x.shape)]
    for kind in ("grad", "delta", "new_m", "new_v"):
        outs += [results[kind, n] for n in ALL_WEIGHTS]
    return tuple(outs)
```

```python
import functools
import math

import jax
import jax.numpy as jnp
from jax import lax
from jax.experimental import pallas as pl
from jax.experimental.pallas import tpu as pltpu

F32 = jnp.float32
BF16 = jnp.bfloat16
MXU_DTYPE = jnp.bfloat16

N_DEV = 8
D_MODEL = 1024
SEQ = 2048
HEAD_DIM = 64
HEADS_PER_GROUP = 4
GROUP_W = HEADS_PER_GROUP * HEAD_DIM
DILATIONS = (1, 4, 16)
QKV_W = 3 * len(DILATIONS) * GROUP_W
Q_W = len(DILATIONS) * GROUP_W
ATT_BLOCK = 128
ROPE_DIM = 16
ROPE_THETA = 500000.0
SSM_W = 512
SSM_GROUPS = 32
SSM_CH = 16
SSM_STATE = 64
N_STATE = SSM_GROUPS * SSM_STATE
D_FF = 2816
IN_W = QKV_W + SSM_W + 2 * D_MODEL
RMS_EPS = 1e-6
NEG_INF = -1e30
LANES = 128

SCAN_SEG_PER_SAMPLE = 8
SCAN_LEN = SEQ // SCAN_SEG_PER_SAMPLE
SCAN_WC = 512
SCAN_NBLK = N_STATE // SCAN_WC
SCAN_CH = SSM_W // SCAN_NBLK
SCAN_CHUNK = 32

ADAM_LR = 0.001
ADAM_B1 = 0.9
ADAM_B2 = 0.999
ADAM_EPS = 1e-08
ADAM_WD = 0.01
ADAM_STEP = 10

VMEM_BIG = 48 * 1024 * 1024
VMEM_MID = 32 * 1024 * 1024

BIG_WEIGHTS = ("w_in", "w_glu", "w_attn_out", "w_out", "w_ffn_gate", "w_ffn_up", "w_ffn_down")
ROW_SHARDED = ("w_out", "w_ffn_down")
SMALL_WEIGHTS = ("norm_mix_g", "ssm_a_re", "ssm_a_im", "ssm_log_dt", "ssm_b_re", "ssm_b_im", "ssm_c_re", "ssm_c_im",
                 "ssm_d", "norm_ffn_g", "norm_final_g")
ALL_WEIGHTS = ("norm_mix_g", "w_in", "ssm_a_re", "ssm_a_im", "ssm_log_dt", "ssm_b_re", "ssm_b_im", "ssm_c_re", "ssm_c_im",
               "ssm_d", "w_glu", "w_attn_out", "w_out", "norm_ffn_g", "w_ffn_gate", "w_ffn_up", "w_ffn_down", "norm_final_g")


def _sigmoid(x):
    return 1.0 / (1.0 + jnp.exp(-x))


def _pallas_call(body, *, out_shape, **kw):
    single = not isinstance(out_shape, (list, tuple))
    shapes = [pltpu.HBM(s.shape, s.dtype) for s in ([out_shape] if single else out_shape)]
    call = pl.pallas_call(body, out_shape=shapes[0] if single else shapes, **kw)
    return lambda *operands: call(*[pltpu.with_memory_space_constraint(o, pltpu.HBM) for o in operands])


class _Comm:
    def __init__(self, ins, out_shapes, n_sem, n_local, start, finish):
        self.ins, self.out_shapes, self.n_sem, self.n_local = ins, out_shapes, n_sem, n_local
        self.start, self.finish = start, finish


def _mm(a, b, mode, name, tm, tn, out_dtype=F32, add=None, vmem=VMEM_BIG, comm=None, cols=None):
    if mode == "nn":
        (m, k), (_, n) = a.shape, b.shape
        a_spec = pl.BlockSpec((tm, k), lambda i, j: (i, 0))
        b_spec = pl.BlockSpec((k, tn), lambda i, j: (0, j))
        dims = (((1,), (0,)), ((), ()))
    elif mode == "nt":
        (m, k), (n, _) = a.shape, b.shape
        a_spec = pl.BlockSpec((tm, k), lambda i, j: (i, 0))
        b_spec = pl.BlockSpec((tn, k), lambda i, j: (j, 0))
        dims = (((1,), (1,)), ((), ()))
    else:
        (k, m), (_, n) = a.shape, b.shape
        first, n = cols if cols else (0, n)
        a_spec = pl.BlockSpec((k, tm), lambda i, j: (0, i))
        b_spec = pl.BlockSpec((k, tn), lambda i, j: (0, j + first // tn))
        dims = (((0,), (0,)), ((), ()))
    assert m % tm == 0 and n % tn == 0, (name, m, n, tm, tn)
    o_spec = pl.BlockSpec((tm, tn), lambda i, j: (i, j))
    has_add = add is not None

    def body(*refs):
        a_ref, b_ref, o_ref = refs[0], refs[1], refs[-1]
        acc = lax.dot_general(a_ref[...].astype(MXU_DTYPE), b_ref[...].astype(MXU_DTYPE), dims,
                              preferred_element_type=F32)
        if has_add:
            acc = acc + refs[2][...]
        o_ref[...] = acc.astype(out_dtype)

    ins = [a, b] + ([add] if has_add else [])
    in_specs = [a_spec, b_spec] + ([o_spec] if has_add else [])
    return _grid_call(body, name, (m // tm, n // tn), ins, in_specs, [o_spec],
                      [jax.ShapeDtypeStruct((m, n), out_dtype)], vmem, comm)


def _grid_call(body, name, grid, ins, in_specs, out_specs, out_shapes, vmem, comm=None, sequential=False, scratch=()):
    if comm is None:
        single = len(out_shapes) == 1
        semantics = ("arbitrary", "arbitrary") if sequential else ("parallel", "parallel")
        return _pallas_call(
            body, name=name, grid=grid, in_specs=in_specs, out_specs=out_specs[0] if single else out_specs,
            out_shape=out_shapes[0] if single else out_shapes, scratch_shapes=list(scratch),
            compiler_params=pltpu.CompilerParams(dimension_semantics=semantics, vmem_limit_bytes=vmem),
        )(*ins)
    n_in, n_out, n_cin, n_cout = len(ins), len(out_shapes), len(comm.ins), len(comm.out_shapes)
    n_io = n_in + n_cin + n_out + n_cout

    def carrying(*refs):
        own = refs[:n_in] + refs[n_in + n_cin:n_in + n_cin + n_out] + refs[n_io:len(refs) - 3]
        c_args = (refs[n_in:n_in + n_cin], refs[n_in + n_cin + n_out:n_io], *refs[-3:])

        @pl.when((pl.program_id(0) == 0) & (pl.program_id(1) == 0))
        def _():
            comm.start(*c_args)

        body(*own)

        @pl.when((pl.program_id(0) == grid[0] - 1) & (pl.program_id(1) == grid[1] - 1))
        def _():
            comm.finish(*c_args)

    hbm = pl.BlockSpec(memory_space=pl.ANY)
    return _pallas_call(
        carrying, name=name, grid=grid, in_specs=list(in_specs) + [hbm] * n_cin,
        out_specs=list(out_specs) + [hbm] * n_cout, out_shape=list(out_shapes) + list(comm.out_shapes),
        scratch_shapes=list(scratch) + [pltpu.SemaphoreType.DMA((comm.n_sem,)), pltpu.SemaphoreType.DMA((comm.n_sem,)),
                                        pltpu.SemaphoreType.DMA((comm.n_local,))],
        compiler_params=pltpu.CompilerParams(dimension_semantics=("arbitrary", "arbitrary"), vmem_limit_bytes=vmem),
    )(*ins, *comm.ins)


def _rows(body, name, n_rows, tm, ins, outs, vmem=VMEM_MID, scratch=()):
    assert n_rows % tm == 0
    arrays, in_specs = [], []
    for kind, arr in ins:
        arrays.append(arr)
        if kind == "row":
            assert n_rows % arr.shape[0] == 0, (name, arr.shape)
            in_specs.append(pl.BlockSpec((tm * arr.shape[0] // n_rows, arr.shape[1]), lambda i: (i, 0)))
        elif kind == "tab":
            nblk = arr.shape[0] // tm
            in_specs.append(pl.BlockSpec((tm, arr.shape[1]), lambda i, nblk=nblk: (i % nblk, 0)))
        else:
            in_specs.append(pl.BlockSpec(arr.shape, lambda i, nd=arr.ndim: (0,) * nd))
    out_specs, out_shape = [], []
    for kind, shp, dt in outs:
        if kind == "row":
            out_specs.append(pl.BlockSpec((tm, shp), lambda i: (i, 0)))
            out_shape.append(jax.ShapeDtypeStruct((n_rows, shp), dt))
        elif kind == "dil":
            d, wd = shp
            out_specs.append(pl.BlockSpec((tm // d, d * wd), lambda i: (i, 0)))
            out_shape.append(jax.ShapeDtypeStruct((n_rows // d, d * wd), dt))
        else:
            out_specs.append(pl.BlockSpec(shp, lambda i, nd=len(shp): (0,) * nd))
            out_shape.append(jax.ShapeDtypeStruct(shp, dt))
    res = _pallas_call(
        body, name=name, grid=(n_rows // tm,), in_specs=in_specs, out_specs=out_specs, out_shape=out_shape,
        scratch_shapes=list(scratch),
        compiler_params=pltpu.CompilerParams(dimension_semantics=("arbitrary",), vmem_limit_bytes=vmem),
    )(*arrays)
    return res


def _gather_residue(stage, ch, r, d, n):
    return stage[ch, pl.ds(r, n, stride=d), :] if d > 1 else stage[ch]


def _scatter_residue(stage, ch, r, d, n, val):
    if d > 1:
        stage[ch, pl.ds(r, n, stride=d), :] = val
    else:
        stage[ch] = val


def _lane_chunk(ch):
    return slice(ch * LANES, (ch + 1) * LANES)


def _rope_tables():
    half = ROPE_DIM // 2
    inv = jnp.power(jnp.float32(ROPE_THETA), -jnp.arange(half, dtype=F32) * 2.0 / ROPE_DIM)
    ang = jnp.arange(SEQ, dtype=F32)[:, None] * inv[None, :]
    lane = jnp.arange(LANES) % HEAD_DIM
    cosl = jnp.cos(ang)[:, lane % half]
    sinl = jnp.sin(ang)[:, lane % half]
    tab_c = jnp.where(lane < ROPE_DIM, cosl, 1.0)
    tab_lo = jnp.where(lane < half, -sinl, 0.0)
    tab_hi = jnp.where((lane >= half) & (lane < ROPE_DIM), sinl, 0.0)
    return tab_c.astype(F32), tab_lo.astype(F32), tab_hi.astype(F32)


def _rope_apply(t, tc, tlo, thi):
    half = ROPE_DIM // 2
    return t * tc + pltpu.roll(t, LANES - half, 1) * tlo + pltpu.roll(t, half, 1) * thi


def _rope_transpose(dt, tc, tlo, thi):
    half = ROPE_DIM // 2
    return dt * tc + pltpu.roll(dt * tlo, half, 1) + pltpu.roll(dt * thi, LANES - half, 1)


def _pack_dproj(dqs, dks, dvs, du, dgpre, tabs):
    tm = 256

    def body(*refs):
        dq_refs, dk_refs, dv_refs = refs[0:3], refs[3:6], refs[6:9]
        du_ref, dg_ref, tc_ref, tlo_ref, thi_ref, o_ref, stage = refs[9:16]
        n_ch = QKV_W // LANES
        halves = GROUP_W // LANES
        for grp, d in enumerate(DILATIONS):
            for which, src in enumerate((dq_refs[grp], dk_refs[grp], dv_refs[grp])):
                for res in range(d):
                    for half in range(halves):
                        _scatter_residue(stage, which * (n_ch // 3) + grp * halves + half, res, d, tm // d,
                                         src[:, _lane_chunk(res * halves + half)])
        tc, tlo, thi = tc_ref[...], tlo_ref[...], thi_ref[...]
        for ch in range(n_ch):
            piece = stage[ch]
            o_ref[:, _lane_chunk(ch)] = (_rope_transpose(piece, tc, tlo, thi) if ch < 2 * n_ch // 3 else piece).astype(BF16)
        o_ref[:, QKV_W:QKV_W + SSM_W] = du_ref[...].astype(BF16)
        o_ref[:, QKV_W + SSM_W:] = dg_ref[...].astype(BF16)

    t = du.shape[0]
    ins = [("row", a) for a in (*dqs, *dks, *dvs, du, dgpre)] + [("tab", tb) for tb in tabs]
    return _rows(body, "pack_dproj", t, tm, ins, [("row", IN_W, BF16)],
                 scratch=[pltpu.VMEM((QKV_W // LANES, tm, LANES), F32)])[0]


def _merge_groups(o_refs, l_refs, a_ref, lt_ref, nat, tm):
    halves = GROUP_W // LANES
    for grp, d in enumerate(DILATIONS[1:], start=1):
        for j, src in enumerate((o_refs[grp], l_refs[grp])):
            for res in range(d):
                for half in range(halves):
                    _scatter_residue(nat, (grp - 1) * 4 + j * 2 + half, res, d, tm // d,
                                     src[:, _lane_chunk(res * halves + half)])
    for half in range(halves):
        sl = _lane_chunk(half)
        la, lb, lc = l_refs[0][:, sl], nat[2 + half], nat[6 + half]
        m = jnp.maximum(jnp.maximum(la, lb), lc)
        ea, eb, ec = jnp.exp(la - m), jnp.exp(lb - m), jnp.exp(lc - m)
        ssum = ea + eb + ec
        a_ref[:, sl] = (ea / ssum) * o_refs[0][:, sl] + (eb / ssum) * nat[half] + (ec / ssum) * nat[4 + half]
        lt_ref[:, sl] = m + jnp.log(ssum)


def _head_sum_matrix():
    r = jnp.arange(GROUP_W) // HEAD_DIM
    return (r[:, None] == r[None, :]).astype(F32)


def _attention_cotangents(da, attn, lt, ones, rd_ref, dil, stage, tm):
    halves = GROUP_W // LANES
    rd = jnp.dot(da * attn, ones, preferred_element_type=F32, precision=lax.Precision.HIGHEST)
    rd_ref[...] = rd
    for half in range(halves):
        for j, val in enumerate((da, lt, rd)):
            stage[2 * j + half] = val[:, _lane_chunk(half)]
    for grp, d in enumerate(DILATIONS[1:], start=1):
        for j in range(3):
            for res in range(d):
                for half in range(halves):
                    dil[3 * (grp - 1) + j][:, _lane_chunk(res * halves + half)] = _gather_residue(
                        stage, 2 * j + half, res, d, tm // d)


_GELU_C = math.sqrt(2.0 / math.pi)


def _head_masks():
    lane = lax.broadcasted_iota(jnp.int32, (1, GROUP_W), 1)
    return [(lane // HEAD_DIM) == h for h in range(HEADS_PER_GROUP)]


def _stack_heads(blk, masks, fill=0.0):
    return jnp.concatenate([jnp.where(mk, blk, jnp.full_like(blk, fill)) for mk in masks], axis=0)


def _unstack_heads(stacked, masks):
    rows = stacked.shape[0] // len(masks)
    out = stacked[:rows]
    for h in range(1, len(masks)):
        out = jnp.where(masks[h], stacked[h * rows:(h + 1) * rows], out)
    return out


def _band_mask(first):
    nk = ATT_BLOCK if first else 2 * ATT_BLOCK
    qi = lax.broadcasted_iota(jnp.int32, (ATT_BLOCK, nk), 0)
    ki = lax.broadcasted_iota(jnp.int32, (ATT_BLOCK, nk), 1)
    dist = qi - ki + (0 if first else ATT_BLOCK)
    return (dist >= 0) & (dist <= ATT_BLOCK)


_NT = (((1,), (1,)), ((), ()))
_TN = (((0,), (0,)), ((), ()))


def _residues_per_step(d):
    return min(d, 4)


def _attn_fwd(q, k, v, group, n_samples, comm=None):
    d = DILATIONS[group]
    length = SEQ // d
    nb = length // ATT_BLOCK

    rps = _residues_per_step(d)

    def body(q_ref, k_ref, v_ref, o_ref, l_ref):
        for rl in range(rps):
            residue(q_ref, k_ref, v_ref, o_ref, l_ref, slice(rl * GROUP_W, (rl + 1) * GROUP_W))

    def residue(q_ref, k_ref, v_ref, o_ref, l_ref, cols):
        masks = _head_masks()

        def block(qs, ks, first):
            nk = ATT_BLOCK if first else 2 * ATT_BLOCK
            qb = q_ref[0, pl.ds(qs, ATT_BLOCK), cols]
            kc = k_ref[0, pl.ds(ks, nk), cols]
            vc = v_ref[0, pl.ds(ks, nk), cols]
            q4 = _stack_heads(qb, masks)
            valid = jnp.tile(_band_mask(first), (HEADS_PER_GROUP, 1))
            s = lax.dot_general(q4, kc, _NT, preferred_element_type=F32) * (HEAD_DIM ** -0.5)
            s = jnp.where(valid, s, NEG_INF)
            m = jnp.max(s, axis=-1, keepdims=True)
            p = jnp.exp(s - m)
            l = jnp.sum(p, axis=-1, keepdims=True)
            o4 = jnp.dot(p.astype(MXU_DTYPE), vc, preferred_element_type=F32) / l
            lse4 = jnp.broadcast_to(m + jnp.log(l), o4.shape)
            o_ref[0, pl.ds(qs, ATT_BLOCK), cols] = _unstack_heads(o4, masks)
            l_ref[0, pl.ds(qs, ATT_BLOCK), cols] = _unstack_heads(lse4, masks)

        block(0, 0, True)
        if nb > 1:
            def loop(n, carry):
                block(pl.multiple_of(n * ATT_BLOCK, ATT_BLOCK), pl.multiple_of((n - 1) * ATT_BLOCK, ATT_BLOCK), False)
                return carry

            lax.fori_loop(1, nb, loop, 0)

    per_sample = lambda a: a.reshape(n_samples, length, d * GROUP_W)
    spec = pl.BlockSpec((1, length, rps * GROUP_W), lambda b, r: (b, 0, r))
    shp = jax.ShapeDtypeStruct((n_samples, length, d * GROUP_W), F32)
    o, lse, *carried = _grid_call(body, f"attn_fwd_g{group}", (n_samples, d // rps), [per_sample(a) for a in (q, k, v)],
                                  [spec] * 3, [spec] * 2, [shp, shp], VMEM_MID, comm)
    flat = lambda a: a.reshape(n_samples * length, d * GROUP_W)
    return flat(o), flat(lse), carried


def _attn_bwd(q, k, v, dattn, lse_tot, rowdot, group, n_samples, comm=None):
    d = DILATIONS[group]
    length = SEQ // d
    nb = length // ATT_BLOCK

    rps = _residues_per_step(d)

    def body(q_ref, k_ref, v_ref, da_ref, lt_ref, rd_ref, dq_ref, dk_ref, dv_ref):
        dk_ref[...] = jnp.zeros_like(dk_ref)
        dv_ref[...] = jnp.zeros_like(dv_ref)
        for rl in range(rps):
            residue(q_ref, k_ref, v_ref, da_ref, lt_ref, rd_ref, dq_ref, dk_ref, dv_ref,
                    slice(rl * GROUP_W, (rl + 1) * GROUP_W))

    def residue(q_ref, k_ref, v_ref, da_ref, lt_ref, rd_ref, dq_ref, dk_ref, dv_ref, cols):
        masks = _head_masks()

        def block(qs, ks, first):
            nk = ATT_BLOCK if first else 2 * ATT_BLOCK
            qb = q_ref[0, pl.ds(qs, ATT_BLOCK), cols]
            kc = k_ref[0, pl.ds(ks, nk), cols]
            vc = v_ref[0, pl.ds(ks, nk), cols]
            da = da_ref[0, pl.ds(qs, ATT_BLOCK), cols]
            lt = lt_ref[0, pl.ds(qs, ATT_BLOCK), cols]
            rd = rd_ref[0, pl.ds(qs, ATT_BLOCK), cols]
            q4 = _stack_heads(qb, masks)
            da4 = _stack_heads(da, masks).astype(MXU_DTYPE)
            lt4 = jnp.max(_stack_heads(lt, masks, -jnp.inf), axis=-1, keepdims=True)
            rd4 = jnp.max(_stack_heads(rd, masks, -jnp.inf), axis=-1, keepdims=True)
            valid = jnp.tile(_band_mask(first), (HEADS_PER_GROUP, 1))
            s = lax.dot_general(q4, kc, _NT, preferred_element_type=F32) * (HEAD_DIM ** -0.5)
            s = jnp.where(valid, s, NEG_INF)
            p = jnp.exp(s - lt4)
            dp = lax.dot_general(da4, vc, _NT, preferred_element_type=F32)
            ds = (p * (dp - rd4) * (HEAD_DIM ** -0.5)).astype(MXU_DTYPE)
            dq_ref[0, pl.ds(qs, ATT_BLOCK), cols] = _unstack_heads(jnp.dot(ds, kc, preferred_element_type=F32), masks)
            dk_ref[0, pl.ds(ks, nk), cols] += lax.dot_general(ds, q4, _TN, preferred_element_type=F32)
            dv_ref[0, pl.ds(ks, nk), cols] += lax.dot_general(p.astype(MXU_DTYPE), da4, _TN, preferred_element_type=F32)

        block(0, 0, True)
        if nb > 1:
            def loop(n, carry):
                block(pl.multiple_of(n * ATT_BLOCK, ATT_BLOCK), pl.multiple_of((n - 1) * ATT_BLOCK, ATT_BLOCK), False)
                return carry

            lax.fori_loop(1, nb, loop, 0)

    per_sample = lambda a: a.reshape(n_samples, length, d * GROUP_W)
    spec = pl.BlockSpec((1, length, rps * GROUP_W), lambda b, r: (b, 0, r))
    shp = jax.ShapeDtypeStruct((n_samples, length, d * GROUP_W), F32)
    dq, dk, dv, *carried = _grid_call(
        body, f"attn_bwd_g{group}", (n_samples, d // rps), [per_sample(a) for a in (q, k, v, dattn, lse_tot, rowdot)],
        [spec] * 6, [spec] * 3, [shp, shp, shp], VMEM_MID, comm)
    flat = lambda a: a.reshape(n_samples * length, d * GROUP_W)
    return flat(dq), flat(dk), flat(dv), carried


def _disc(lr, li, ldt, br, bi):
    dt = jnp.exp(ldt)
    mag = jnp.exp(lr * dt)
    ab_re, ab_im = mag * jnp.cos(li * dt), mag * jnp.sin(li * dt)
    den = lr * lr + li * li
    nr, ni = ab_re - 1.0, ab_im
    f_re = (nr * lr + ni * li) / den
    f_im = (ni * lr - nr * li) / den
    return ab_re, ab_im, f_re * br - f_im * bi, f_re * bi + f_im * br


def _state_mask():
    row_g = lax.broadcasted_iota(jnp.int32, (SCAN_CH, SCAN_WC), 0) // SSM_CH
    col_g = lax.broadcasted_iota(jnp.int32, (SCAN_CH, SCAN_WC), 1) // SSM_STATE
    return row_g == col_g


def _ssm_disc(lr, li, ldt, br, bi, cr, ci):
    w = SCAN_WC

    def body(lr_ref, li_ref, ldt_ref, br_ref, bi_ref, cr_ref, ci_ref, a_ref, bb_ref, c_ref):
        ar, ai, bbr, bbi = _disc(lr_ref[...], li_ref[...], ldt_ref[...], br_ref[...], bi_ref[...])
        crv, civ = cr_ref[...], ci_ref[...]
        mask = _state_mask()
        for cb in range(SCAN_NBLK):
            sl = slice(cb * w, (cb + 1) * w)
            rows = slice(cb * SCAN_CH, (cb + 1) * SCAN_CH)
            dense = lambda comp: jnp.where(mask, jnp.tile(comp[:, sl], (SCAN_CH // SSM_CH, 1)), 0.0)
            a_ref[:, 2 * cb * w:(2 * cb + 1) * w] = ar[:, sl]
            a_ref[:, (2 * cb + 1) * w:(2 * cb + 2) * w] = ai[:, sl]
            bb_ref[rows, :w] = dense(bbr).astype(MXU_DTYPE)
            bb_ref[rows, w:] = dense(bbi).astype(MXU_DTYPE)
            c_ref[rows, :w] = dense(crv).astype(MXU_DTYPE)
            c_ref[rows, w:] = (-dense(civ)).astype(MXU_DTYPE)

    return _pallas_call(
        body, name="ssm_disc",
        out_shape=[jax.ShapeDtypeStruct((1, 2 * N_STATE), F32), jax.ShapeDtypeStruct((SSM_W, 2 * w), MXU_DTYPE),
                   jax.ShapeDtypeStruct((SSM_W, 2 * w), MXU_DTYPE)],
        compiler_params=pltpu.CompilerParams(vmem_limit_bytes=VMEM_MID),
    )(lr, li, ldt, br, bi, cr, ci)


def _group_indicator():
    s = jnp.arange(N_STATE) // SSM_STATE
    return (s[:, None] == jnp.arange(LANES)[None, :]).astype(F32)


def _ssm_param_bwd(lr, li, ldt, br, bi, da_cat, dbb_full, dc_full):
    w = SCAN_WC

    def body(lr_ref, li_ref, ldt_ref, br_ref, bi_ref, da_ref, dbb_ref, dc_ref, ind_ref,
             glr_ref, gli_ref, gldt_ref, gbr_ref, gbi_ref, gcr_ref, gci_ref):
        mask = _state_mask()

        def diag_parts(ref):
            res = ([], [])
            for cb in range(SCAN_NBLK):
                for part in range(2):
                    blk = ref[cb * SCAN_CH:(cb + 1) * SCAN_CH, part * w:(part + 1) * w]
                    res[part].append(jnp.sum(jnp.where(mask, blk, 0.0).reshape(SCAN_CH // SSM_CH, SSM_CH, w), axis=0))
            return jnp.concatenate(res[0], axis=1), jnp.concatenate(res[1], axis=1)

        dar = jnp.concatenate([da_ref[:, 2 * cb * w:(2 * cb + 1) * w] for cb in range(SCAN_NBLK)], axis=1)
        dai = jnp.concatenate([da_ref[:, (2 * cb + 1) * w:(2 * cb + 2) * w] for cb in range(SCAN_NBLK)], axis=1)
        dbbr, dbbi = diag_parts(dbb_ref)
        dcr, dci_neg = diag_parts(dc_ref)
        gcr_ref[...] = dcr
        gci_ref[...] = -dci_neg
        _, vjp = jax.vjp(_disc, lr_ref[...], li_ref[...], ldt_ref[...], br_ref[...], bi_ref[...])
        glr, gli, gldt, gbr, gbi = vjp((dar, dai, dbbr, dbbi))
        glr_ref[...] = glr
        gli_ref[...] = gli
        gldt_ref[...] = jnp.dot(jnp.broadcast_to(gldt, (8, N_STATE)), ind_ref[...], preferred_element_type=F32,
                                precision=lax.Precision.HIGHEST)
        gbr_ref[...] = gbr
        gbi_ref[...] = gbi

    v1 = jax.ShapeDtypeStruct((1, N_STATE), F32)
    v16 = jax.ShapeDtypeStruct((SSM_CH, N_STATE), F32)
    vdt = jax.ShapeDtypeStruct((8, LANES), F32)
    return _pallas_call(
        body, name="ssm_param_bwd", out_shape=[v1, v1, vdt, v16, v16, v16, v16],
        compiler_params=pltpu.CompilerParams(vmem_limit_bytes=VMEM_BIG),
    )(lr, li, ldt, br, bi, da_cat, dbb_full, dc_full, _group_indicator())


def _cmul(ar, ai, br, bi):
    return ar * br - ai * bi, ar * bi + ai * br


def _gelu_tanh(y):
    return jnp.tanh(_GELU_C * (y + 0.044715 * (y * y * y)))


def _segment_carry(er, ei, ar, ai, n_rows, reverse):
    qr, qi = ar, ai
    for _ in range(int(math.log2(SCAN_LEN))):
        qr, qi = _cmul(qr, qi, qr, qi)
    seg = lax.broadcasted_iota(jnp.int32, er.shape, 0) % SCAN_SEG_PER_SAMPLE
    shift = 1
    while shift < SCAN_SEG_PER_SAMPLE:
        keep = (seg < SCAN_SEG_PER_SAMPLE - shift) if reverse else (seg >= shift)
        amount = n_rows - shift if reverse else shift
        sr = jnp.where(keep, pltpu.roll(er, amount, 0), 0.0)
        si = jnp.where(keep, pltpu.roll(ei, amount, 0), 0.0)
        if reverse:
            er, ei = er + qr * sr + qi * si, ei + qr * si - qi * sr
        else:
            er, ei = er + qr * sr - qi * si, ei + qr * si + qi * sr
        qr, qi = _cmul(qr, qi, qr, qi)
        shift *= 2
    keep = (seg < SCAN_SEG_PER_SAMPLE - 1) if reverse else (seg >= 1)
    amount = n_rows - 1 if reverse else 1
    return jnp.where(keep, pltpu.roll(er, amount, 0), 0.0), jnp.where(keep, pltpu.roll(ei, amount, 0), 0.0)


def _ssm_fwd(u_perm, a_cat, bbc, cc, dskip, n_rows):
    t = u_perm.shape[0]
    w = SCAN_WC
    rows_c = SCAN_CHUNK * n_rows
    n_chunks = t // rows_c

    assert n_chunks % 2 == 0

    def body(u_ref, a_ref, bb_ref, c_ref, d_ref, yt_ref, yg_ref, ein_ref, bu_all, st_a, st_b, xs_a, xs_b):
        ar = jnp.broadcast_to(a_ref[:, :w], (n_rows, w))
        ai = jnp.broadcast_to(a_ref[:, w:], (n_rows, w))
        start = lambda ch: pl.multiple_of(ch * rows_c, rows_c)

        def project(ch, stage):
            res = jnp.dot(u_ref[pl.ds(start(ch), rows_c), :].astype(MXU_DTYPE), bb_ref[...], preferred_element_type=F32)
            stage[...] = res
            bu_all[pl.ds(start(ch), rows_c), :] = res

        def steps(src, r0, carry, xs=None):
            for i in range(SCAN_CHUNK):
                blk = src[pl.ds(r0 + i * n_rows, n_rows), :]
                carry = (ar * carry[0] - ai * carry[1] + blk[:, :w], ar * carry[1] + ai * carry[0] + blk[:, w:])
                if xs is not None:
                    xs[i * n_rows:(i + 1) * n_rows, :w] = carry[0]
                    xs[i * n_rows:(i + 1) * n_rows, w:] = carry[1]
            return carry

        def emit(xs, ch):
            y = lax.dot_general(xs[...].astype(MXU_DTYPE), c_ref[...], _NT, preferred_element_type=F32)
            yt = y + d_ref[...] * u_ref[pl.ds(start(ch), rows_c), :]
            yt_ref[pl.ds(start(ch), rows_c), :] = yt
            yg_ref[pl.ds(start(ch), rows_c), :] = (0.5 * yt * (1.0 + _gelu_tanh(yt))).astype(BF16)

        project(0, st_a)

        def pair1(p, carry):
            project(2 * p + 1, st_b)
            carry = steps(st_a, 0, carry)
            project(jnp.minimum(2 * p + 2, n_chunks - 1), st_a)
            return steps(st_b, 0, carry)

        zero = jnp.zeros((n_rows, w), F32)
        er, ei = lax.fori_loop(0, n_chunks // 2, pair1, (zero, zero))
        cr, ci = _segment_carry(er, ei, ar, ai, n_rows, False)
        ein_ref[:, :w] = cr
        ein_ref[:, w:] = ci

        xs_b[...] = jnp.zeros_like(xs_b)

        def pair2(p, carry):
            emit(xs_b, jnp.maximum(2 * p - 1, 0))
            carry = steps(bu_all, start(2 * p), carry, xs_a)
            emit(xs_a, 2 * p)
            return steps(bu_all, start(2 * p + 1), carry, xs_b)

        lax.fori_loop(0, n_chunks // 2, pair2, (cr, ci))
        emit(xs_b, n_chunks - 1)

    col = lambda width: pl.BlockSpec((t, width), lambda c: (0, c))
    wgt = pl.BlockSpec((SCAN_CH, 2 * w), lambda c: (c, 0))
    return _pallas_call(
        body, name="ssm_fwd", grid=(SCAN_NBLK,),
        in_specs=[col(SCAN_CH), pl.BlockSpec((1, 2 * w), lambda c: (0, c)), wgt, wgt,
                  pl.BlockSpec((1, SCAN_CH), lambda c: (0, c))],
        out_specs=[col(SCAN_CH), col(SCAN_CH), pl.BlockSpec((n_rows, 2 * w), lambda c: (0, c))],
        out_shape=[jax.ShapeDtypeStruct((t, SSM_W), F32), jax.ShapeDtypeStruct((t, SSM_W), BF16),
                   jax.ShapeDtypeStruct((n_rows, 2 * N_STATE), F32)],
        scratch_shapes=[pltpu.VMEM((t, 2 * w), F32)] + [pltpu.VMEM((rows_c, 2 * w), F32)] * 4,
        compiler_params=pltpu.CompilerParams(dimension_semantics=("parallel",), vmem_limit_bytes=VMEM_BIG),
    )(u_perm, a_cat, bbc, cc, dskip)


def _ssm_bwd(u_perm, dyg, ytot, dskip, a_cat, bbc, cc, ein, n_rows, comm=None):
    t = u_perm.shape[0]
    w = SCAN_WC
    rows_c = SCAN_CHUNK * n_rows
    n_chunks = t // rows_c

    assert n_chunks % 2 == 0
    last = n_chunks - 1

    def body(u_ref, dyg_ref, yt_ref, dk_ref, a_ref, bb_ref, c_ref, ein_ref, du_ref, gd_ref, da_ref, dbb_ref, dc_ref,
             xs_all, dy_s, st_a, st_b, buf_a, buf_b):
        ar = jnp.broadcast_to(a_ref[:, :w], (n_rows, w))
        ai = jnp.broadcast_to(a_ref[:, w:], (n_rows, w))
        zero = jnp.zeros((n_rows, w), F32)
        start = lambda ch: pl.multiple_of(ch * rows_c, rows_c)
        dbb_ref[...] = jnp.zeros_like(dbb_ref)
        dc_ref[...] = jnp.zeros_like(dc_ref)
        da_ref[...] = jnp.zeros_like(da_ref)

        yt = yt_ref[...]
        th = _gelu_tanh(yt)
        dgelu = 0.5 * (1.0 + th) + 0.5 * yt * (1.0 - th * th) * _GELU_C * (1.0 + 3.0 * 0.044715 * yt * yt)
        dy_all = dyg_ref[...] * dgelu
        dy_s[...] = dy_all
        gd_ref[...] = jnp.sum(dy_all * u_ref[...], axis=0, keepdims=True)
        dy_chunk = lambda ch: dy_s[pl.ds(start(ch), rows_c), :].astype(MXU_DTYPE)

        xs_all[0:n_rows, :] = ein_ref[...]

        def project(ch, stage):
            stage[...] = jnp.dot(u_ref[pl.ds(start(ch), rows_c), :].astype(MXU_DTYPE), bb_ref[...],
                                 preferred_element_type=F32)

        def fwd_steps(stage, ch, carry, xs):
            for i in range(SCAN_CHUNK):
                blk = stage[i * n_rows:(i + 1) * n_rows, :]
                carry = (ar * carry[0] - ai * carry[1] + blk[:, :w], ar * carry[1] + ai * carry[0] + blk[:, w:])
                for half, val in enumerate(carry):
                    xs[i * n_rows:(i + 1) * n_rows, half * w:(half + 1) * w] = val
                    xs_all[pl.ds(start(ch) + (i + 1) * n_rows, n_rows), half * w:(half + 1) * w] = val
            return carry

        def add_dc(xs, ch):
            dc_ref[...] += lax.dot_general(dy_chunk(ch), xs[...].astype(MXU_DTYPE), _TN, preferred_element_type=F32)

        project(0, st_a)

        def fwd_pair(p, carry):
            project(2 * p + 1, st_b)
            carry = fwd_steps(st_a, 2 * p, carry, buf_a)
            add_dc(buf_a, 2 * p)
            project(jnp.minimum(2 * p + 2, last), st_a)
            carry = fwd_steps(st_b, 2 * p + 1, carry, buf_b)
            add_dc(buf_b, 2 * p + 1)
            return carry

        lax.fori_loop(0, n_chunks // 2, fwd_pair, (ein_ref[:, :w], ein_ref[:, w:]))

        def project_dx(ch, stage):
            stage[...] = jnp.dot(dy_chunk(ch), c_ref[...], preferred_element_type=F32)

        def back_steps(stage, carry, g_buf=None):
            for i in reversed(range(SCAN_CHUNK)):
                blk = stage[i * n_rows:(i + 1) * n_rows, :]
                carry = (blk[:, :w] + ar * carry[0] + ai * carry[1], blk[:, w:] + ar * carry[1] - ai * carry[0])
                if g_buf is not None:
                    g_buf[i * n_rows:(i + 1) * n_rows, :w] = carry[0]
                    g_buf[i * n_rows:(i + 1) * n_rows, w:] = carry[1]
            return carry

        def first_pair(p, carry):
            project_dx(last - 2 * p - 1, st_b)
            carry = back_steps(st_a, carry)
            project_dx(jnp.maximum(last - 2 * p - 2, 0), st_a)
            return back_steps(st_b, carry)

        project_dx(last, st_a)
        sr, si = lax.fori_loop(0, n_chunks // 2, first_pair, (zero, zero))
        gr0, gi0 = _segment_carry(sr, si, ar, ai, n_rows, True)

        def post(g_buf, ch):
            g = g_buf[...]
            xp = xs_all[pl.ds(start(ch), rows_c), :]
            da_ref[:, :w] += jnp.sum(g[:, :w] * xp[:, :w] + g[:, w:] * xp[:, w:], axis=0, keepdims=True)
            da_ref[:, w:] += jnp.sum(g[:, w:] * xp[:, :w] - g[:, :w] * xp[:, w:], axis=0, keepdims=True)
            gb = g.astype(MXU_DTYPE)
            du_ref[pl.ds(start(ch), rows_c), :] = (lax.dot_general(gb, bb_ref[...], _NT, preferred_element_type=F32)
                                                   + dy_s[pl.ds(start(ch), rows_c), :] * dk_ref[...])
            dbb_ref[...] += lax.dot_general(u_ref[pl.ds(start(ch), rows_c), :].astype(MXU_DTYPE), gb, _TN,
                                            preferred_element_type=F32)

        def second_pair(p, carry):
            c1 = last - 2 * p
            project_dx(c1 - 1, st_b)
            post(buf_b, jnp.minimum(c1 + 1, last))
            carry = back_steps(st_a, carry, buf_a)
            project_dx(jnp.maximum(c1 - 2, 0), st_a)
            post(buf_a, c1)
            return back_steps(st_b, carry, buf_b)

        project_dx(last, st_a)
        buf_b[...] = jnp.zeros_like(buf_b)
        lax.fori_loop(0, n_chunks // 2, second_pair, (gr0, gi0))
        post(buf_b, 0)

    col = lambda width: pl.BlockSpec((t, width), lambda c, j: (0, c))
    wgt = pl.BlockSpec((SCAN_CH, 2 * w), lambda c, j: (c, 0))
    row = pl.BlockSpec((1, 2 * w), lambda c, j: (0, c))
    chan = pl.BlockSpec((1, SCAN_CH), lambda c, j: (0, c))
    return _grid_call(
        body, "ssm_bwd", (SCAN_NBLK, 1), [u_perm, dyg, ytot, dskip, a_cat, bbc, cc, ein],
        [col(SCAN_CH), col(SCAN_CH), col(SCAN_CH), chan, row, wgt, wgt,
         pl.BlockSpec((n_rows, 2 * w), lambda c, j: (0, c))],
        [col(SCAN_CH), chan, row, wgt, wgt],
        [jax.ShapeDtypeStruct((t, SSM_W), F32), jax.ShapeDtypeStruct((1, SSM_W), F32),
         jax.ShapeDtypeStruct((1, 2 * N_STATE), F32), jax.ShapeDtypeStruct((SSM_W, 2 * w), F32),
         jax.ShapeDtypeStruct((SSM_W, 2 * w), F32)],
        56 * 1024 * 1024, comm,
        scratch=[pltpu.VMEM((t + n_rows, 2 * w), F32), pltpu.VMEM((t, SCAN_CH), F32)]
        + [pltpu.VMEM((rows_c, 2 * w), F32)] * 4)


def _to_scan_rows(a, n_samples):
    c = a.shape[1]
    return a.reshape(n_samples, SCAN_SEG_PER_SAMPLE, SCAN_LEN, c).transpose(2, 0, 1, 3).reshape(-1, c)


def _from_scan_rows(a, n_samples):
    c = a.shape[1]
    return a.reshape(SCAN_LEN, n_samples, SCAN_SEG_PER_SAMPLE, c).transpose(1, 2, 0, 3).reshape(-1, c)


def _row_spec(tm, width):
    return pl.BlockSpec((tm, width), lambda i, j: (i, 0))


def _whole(arr):
    return pl.BlockSpec(arr.shape, lambda i, j: (0,) * arr.ndim)


def _proj_rope(x, g, w_in_t, tabs, comm=None):
    t = x.shape[0]
    tm = 256

    def body(x_ref, g_ref, w_ref, tc_ref, tlo_ref, thi_ref, h_ref, u_ref, gate_ref, *rest):
        qkv_refs, stage = rest[:9], rest[9]
        xv = x_ref[...]
        r = lax.rsqrt(jnp.mean(xv * xv, axis=-1, keepdims=True) + RMS_EPS)
        h = ((xv * r) * g_ref[...]).astype(BF16)
        h_ref[...] = h
        p = lax.dot_general(h.astype(MXU_DTYPE), w_ref[...], _NT, preferred_element_type=F32)
        u_ref[...] = p[:, QKV_W:QKV_W + SSM_W]
        gate_ref[...] = _sigmoid(p[:, QKV_W + SSM_W:])
        tc, tlo, thi = tc_ref[...], tlo_ref[...], thi_ref[...]
        n_ch = QKV_W // LANES
        for ch in range(n_ch):
            piece = p[:, _lane_chunk(ch)]
            stage[ch] = _rope_apply(piece, tc, tlo, thi) if ch < 2 * n_ch // 3 else piece
        halves = GROUP_W // LANES
        for grp, d in enumerate(DILATIONS):
            for which in range(3):
                out = qkv_refs[3 * grp + which]
                for res in range(d):
                    for half in range(halves):
                        ch = which * (n_ch // 3) + grp * halves + half
                        out[:, _lane_chunk(res * halves + half)] = _gather_residue(stage, ch, res, d, tm // d).astype(BF16)

    tab = pl.BlockSpec((tm, LANES), lambda i, j: (i % (SEQ // tm), 0))
    widths = [(D_MODEL, BF16), (SSM_W, F32), (2 * D_MODEL, F32)]
    out_specs = [_row_spec(tm, wd) for wd, _ in widths]
    out_shapes = [jax.ShapeDtypeStruct((t, wd), dt) for wd, dt in widths]
    for d in DILATIONS:
        out_specs += [_row_spec(tm // d, d * GROUP_W)] * 3
        out_shapes += [jax.ShapeDtypeStruct((t // d, d * GROUP_W), BF16)] * 3
    return _grid_call(
        body, "proj_rope", (t // tm, 1), [x, g, w_in_t, *tabs],
        [_row_spec(tm, D_MODEL), _whole(g), _whole(w_in_t), tab, tab, tab], out_specs, out_shapes, VMEM_BIG, comm,
        scratch=[pltpu.VMEM((QKV_W // LANES, tm, LANES), F32)])


def _branch_outputs(attn_ref, yg_ref, wao_ref, wglu_ref):
    attn_d = lax.dot_general(attn_ref[...].astype(MXU_DTYPE), wao_ref[...], _NT, preferred_element_type=F32)
    z = lax.dot_general(yg_ref[...].astype(MXU_DTYPE), wglu_ref[...], _NT, preferred_element_type=F32)
    return attn_d, z[:, :D_MODEL], _sigmoid(z[:, D_MODEL:])


def _mix_out_rms(os_, lses, yg, gates, x, w_ao_t, w_glu_t, w_out, g, comm=None):
    t = x.shape[0]
    tm = 256

    def body(o0, o1, o2, l0, l1, l2, yg_ref, gate_ref, x_ref, wao_ref, wglu_ref, wout_ref, g_ref,
             attn_ref, lt_ref, m_ref, x1_ref, h_ref, nat):
        _merge_groups((o0, o1, o2), (l0, l1, l2), attn_ref, lt_ref, nat, tm)
        attn_d, za, sb = _branch_outputs(attn_ref, yg_ref, wao_ref, wglu_ref)
        merged = (gate_ref[:, :D_MODEL] * attn_d + gate_ref[:, D_MODEL:] * (za * sb)).astype(BF16)
        m_ref[...] = merged
        x1 = x_ref[...] + jnp.dot(merged.astype(MXU_DTYPE), wout_ref[...], preferred_element_type=F32)
        x1_ref[...] = x1
        r = lax.rsqrt(jnp.mean(x1 * x1, axis=-1, keepdims=True) + RMS_EPS)
        h_ref[...] = ((x1 * r) * g_ref[...]).astype(BF16)

    dil_specs = [_row_spec(tm // d, d * GROUP_W) for d in DILATIONS] * 2
    return _grid_call(
        body, "mix_out_rms", (t // tm, 1), [*os_, *lses, yg, gates, x, w_ao_t, w_glu_t, w_out, g],
        dil_specs + [_row_spec(tm, SSM_W), _row_spec(tm, 2 * D_MODEL), _row_spec(tm, D_MODEL),
                     _whole(w_ao_t), _whole(w_glu_t), _whole(w_out), _whole(g)],
        [_row_spec(tm, GROUP_W)] * 2 + [_row_spec(tm, D_MODEL)] * 3,
        [jax.ShapeDtypeStruct((t, GROUP_W), F32)] * 2
        + [jax.ShapeDtypeStruct((t, D_MODEL), BF16), jax.ShapeDtypeStruct((t, D_MODEL), F32),
           jax.ShapeDtypeStruct((t, D_MODEL), BF16)], VMEM_BIG, comm, scratch=[pltpu.VMEM((8, tm, LANES), F32)])


def _mix_bwd(dx1b, attn, lse_tot, yg, gates, w_ao_t, w_glu_t, w_out, comm=None):
    t = dx1b.shape[0]
    tm = 256

    def body(dx_ref, attn_ref, lt_ref, yg_ref, gate_ref, wao_ref, wglu_ref, wout_ref, ones_ref,
             dad_ref, dz_ref, dg_ref, da_ref, dyg_ref, rd_ref, *rest):
        dm = lax.dot_general(dx_ref[...], wout_ref[...], _NT, preferred_element_type=F32)
        attn_d, za, sb = _branch_outputs(attn_ref, yg_ref, wao_ref, wglu_ref)
        g0, g1 = gate_ref[:, :D_MODEL], gate_ref[:, D_MODEL:]
        dad = (dm * g0).astype(BF16)
        dad_ref[...] = dad
        ds = dm * g1
        dza, dzb = (ds * sb).astype(BF16), (ds * za * sb * (1.0 - sb)).astype(BF16)
        dz_ref[:, :D_MODEL] = dza
        dz_ref[:, D_MODEL:] = dzb
        dg_ref[:, :D_MODEL] = (dm * attn_d * g0 * (1.0 - g0)).astype(BF16)
        dg_ref[:, D_MODEL:] = (dm * (za * sb) * g1 * (1.0 - g1)).astype(BF16)
        da = jnp.dot(dad.astype(MXU_DTYPE), wao_ref[...], preferred_element_type=F32)
        da_ref[...] = da
        dyg_ref[...] = (jnp.dot(dza.astype(MXU_DTYPE), wglu_ref[:D_MODEL, :], preferred_element_type=F32)
                        + jnp.dot(dzb.astype(MXU_DTYPE), wglu_ref[D_MODEL:, :], preferred_element_type=F32))
        _attention_cotangents(da, attn_ref[...], lt_ref[...], ones_ref[...], rd_ref, rest[:6], rest[6], tm)

    widths = [(D_MODEL, BF16), (2 * D_MODEL, BF16), (2 * D_MODEL, BF16), (GROUP_W, F32), (SSM_W, F32), (GROUP_W, F32)]
    out_specs = [_row_spec(tm, wd) for wd, _ in widths]
    out_shapes = [jax.ShapeDtypeStruct((t, wd), dt) for wd, dt in widths]
    for d in DILATIONS[1:]:
        out_specs += [_row_spec(tm // d, d * GROUP_W)] * 3
        out_shapes += [jax.ShapeDtypeStruct((t // d, d * GROUP_W), F32)] * 3
    ones = _head_sum_matrix()
    return _grid_call(
        body, "mix_bwd", (t // tm, 1), [dx1b, attn, lse_tot, yg, gates, w_ao_t, w_glu_t, w_out, ones],
        [_row_spec(tm, D_MODEL), _row_spec(tm, GROUP_W), _row_spec(tm, GROUP_W), _row_spec(tm, SSM_W),
         _row_spec(tm, 2 * D_MODEL), _whole(w_ao_t), _whole(w_glu_t), _whole(w_out), _whole(ones)],
        out_specs, out_shapes, VMEM_BIG, comm, scratch=[pltpu.VMEM((6, tm, LANES), F32)])


FFN_TN = D_FF // 2
MXU_COLS = 256


def _ffn_in_swiglu(h2, w_gate_t, w_up_t, comm=None):
    t = h2.shape[0]
    tm = 512

    def body(h_ref, wg_ref, wu_ref, a_ref, b_ref, f_ref):
        h = h_ref[...].astype(MXU_DTYPE)
        for c0 in range(0, FFN_TN, MXU_COLS):
            sl = slice(c0, min(c0 + MXU_COLS, FFN_TN))
            a = lax.dot_general(h, wg_ref[sl, :], _NT, preferred_element_type=F32)
            b = lax.dot_general(h, wu_ref[sl, :], _NT, preferred_element_type=F32)
            a_ref[:, sl] = a
            b_ref[:, sl] = b
            f_ref[:, sl] = (a * _sigmoid(a) * b).astype(BF16)

    tile = pl.BlockSpec((tm, FFN_TN), lambda j, i: (i, j))
    wspec = pl.BlockSpec((FFN_TN, D_MODEL), lambda j, i: (j, 0))
    return _grid_call(
        body, "ffn_in_swiglu", (D_FF // FFN_TN, t // tm), [h2, w_gate_t, w_up_t],
        [pl.BlockSpec((tm, D_MODEL), lambda j, i: (i, 0)), wspec, wspec],
        [tile] * 3, [jax.ShapeDtypeStruct((t, D_FF), F32)] * 2 + [jax.ShapeDtypeStruct((t, D_FF), BF16)], VMEM_BIG, comm)


def _ffn_down_final(f, w_down, x1, target, g):
    t = x1.shape[0]
    tm = 256

    def body(f_ref, w_ref, x1_ref, t_ref, g_ref, dx_ref, dxb_ref, loss_ref, gg_ref):
        @pl.when(pl.program_id(0) == 0)
        def _():
            loss_ref[...] = jnp.zeros_like(loss_ref)
            gg_ref[...] = jnp.zeros_like(gg_ref)

        xv = x1_ref[...] + jnp.dot(f_ref[...].astype(MXU_DTYPE), w_ref[...], preferred_element_type=F32)
        gv = g_ref[...]
        r = lax.rsqrt(jnp.mean(xv * xv, axis=-1, keepdims=True) + RMS_EPS)
        n = xv * r
        diff = n * gv - t_ref[...]
        per_tok = jnp.mean(diff * diff, axis=-1, keepdims=True)
        loss_ref[...] += 0.5 * jnp.sum(per_tok, axis=0, keepdims=True)
        dy = diff / xv.shape[-1]
        gg_ref[...] += jnp.sum(dy * n, axis=0, keepdims=True)
        dn = dy * gv
        dx = r * (dn - n * jnp.mean(dn * n, axis=-1, keepdims=True))
        dx_ref[...] = dx
        dxb_ref[...] = dx.astype(BF16)

    acc = lambda shp: pl.BlockSpec(shp, lambda i, j: (0, 0))
    return _grid_call(
        body, "ffn_down_final", (t // tm, 1), [f, w_down, x1, target, g],
        [_row_spec(tm, D_FF), _whole(w_down), _row_spec(tm, D_MODEL), _row_spec(tm, D_MODEL), _whole(g)],
        [_row_spec(tm, D_MODEL)] * 2 + [acc((8, LANES)), acc((1, D_MODEL))],
        [jax.ShapeDtypeStruct((t, D_MODEL), F32), jax.ShapeDtypeStruct((t, D_MODEL), BF16),
         jax.ShapeDtypeStruct((8, LANES), F32), jax.ShapeDtypeStruct((1, D_MODEL), F32)], VMEM_BIG, sequential=True)


def _d_f_swiglu_bwd(dx2b, w_down, a, b):
    t = a.shape[0]
    tm = 512

    def body(dx_ref, w_ref, a_ref, b_ref, da_ref, db_ref):
        d = lax.dot_general(dx_ref[...], w_ref[...], _NT, preferred_element_type=F32)
        av, bv = a_ref[...], b_ref[...]
        sg = _sigmoid(av)
        da_ref[...] = (d * bv * sg * (1.0 + av * (1.0 - sg))).astype(BF16)
        db_ref[...] = (d * av * sg).astype(BF16)

    tile = pl.BlockSpec((tm, FFN_TN), lambda j, i: (i, j))
    return _grid_call(
        body, "d_f_swiglu_bwd", (D_FF // FFN_TN, t // tm), [dx2b, w_down, a, b],
        [pl.BlockSpec((tm, D_MODEL), lambda j, i: (i, 0)), pl.BlockSpec((FFN_TN, D_MODEL), lambda j, i: (j, 0)), tile, tile],
        [tile] * 2, [jax.ShapeDtypeStruct((t, D_FF), BF16)] * 2, VMEM_BIG)


def _mm_rms_bwd(operands, weights, x, g, dres, name, comm=None):
    t = x.shape[0]
    tm = 256
    n_op = len(operands)

    def body(*refs):
        a_refs, w_refs = refs[:n_op], refs[n_op:2 * n_op]
        x_ref, g_ref, dres_ref, dx_ref, dxb_ref, gg_ref = refs[2 * n_op:]

        @pl.when(pl.program_id(0) == 0)
        def _():
            gg_ref[...] = jnp.zeros_like(gg_ref)

        dh = None
        for a_ref, w_ref in zip(a_refs, w_refs):
            part = jnp.dot(a_ref[...].astype(MXU_DTYPE), w_ref[...], preferred_element_type=F32)
            dh = part if dh is None else dh + part
        xv = x_ref[...]
        r = lax.rsqrt(jnp.mean(xv * xv, axis=-1, keepdims=True) + RMS_EPS)
        n = xv * r
        gg_ref[...] += jnp.sum(dh * n, axis=0, keepdims=True)
        dn = dh * g_ref[...]
        dx = dres_ref[...] + r * (dn - n * jnp.mean(dn * n, axis=-1, keepdims=True))
        dx_ref[...] = dx
        dxb_ref[...] = dx.astype(BF16)

    d = x.shape[1]
    return _grid_call(
        body, name, (t // tm, 1), [*operands, *weights, x, g, dres],
        [_row_spec(tm, a.shape[1]) for a in operands] + [_whole(wk) for wk in weights]
        + [_row_spec(tm, d), _whole(g), _row_spec(tm, d)],
        [_row_spec(tm, d)] * 2 + [pl.BlockSpec((1, d), lambda i, j: (0, 0))],
        [jax.ShapeDtypeStruct((t, d), F32), jax.ShapeDtypeStruct((t, d), BF16), jax.ShapeDtypeStruct((1, d), F32)],
        VMEM_BIG, comm, sequential=True)


def _flat_small(small):
    perm_b = lambda a: a.reshape(SSM_GROUPS, SSM_STATE, SSM_CH).transpose(2, 0, 1).reshape(SSM_CH, N_STATE)
    perm_c = lambda a: a.reshape(SSM_GROUPS, SSM_CH, SSM_STATE).transpose(1, 0, 2).reshape(SSM_CH, N_STATE)
    return dict(
        g_mix=small["norm_mix_g"].reshape(1, D_MODEL), g_ffn=small["norm_ffn_g"].reshape(1, D_MODEL),
        g_fin=small["norm_final_g"].reshape(1, D_MODEL),
        lr=small["ssm_a_re"].reshape(1, N_STATE), li=small["ssm_a_im"].reshape(1, N_STATE),
        ldt=jnp.repeat(small["ssm_log_dt"].reshape(SSM_GROUPS), SSM_STATE).reshape(1, N_STATE),
        br=perm_b(small["ssm_b_re"]), bi=perm_b(small["ssm_b_im"]),
        cr=perm_c(small["ssm_c_re"]), ci=perm_c(small["ssm_c_im"]), dskip=small["ssm_d"].reshape(1, SSM_W))


AG_HOSTS = {"proj_rope": ("w_glu", "w_attn_out", "w_out", "w_ffn_gate"), "mix_out_rms": ("w_ffn_up",),
            "ffn_in_swiglu": ("w_ffn_down",)}
HALVED = ("w_ffn_gate", "w_ffn_up", "w_in")
A2A_HOSTS = {"d_h2_rms": ("w_ffn_down",), "mix_bwd": ("w_ffn_gate:0", "w_out"), "attn_bwd_g0": ("w_ffn_up:1",),
             "attn_bwd_g1": ("w_glu",), "attn_bwd_g2": ("w_attn_out",), "ssm_bwd": ("w_ffn_gate:1", "w_ffn_up:0"),
             "mm_g_in1": ("w_in:0",), "d_h0_rms": ("w_in:1",)}
SMALL_HOST = "mm_g_in0"


def _local_step(x, target, w, small, shards=None):
    t = x.shape[0]
    n_samples = t // SEQ
    n_rows = n_samples * SCAN_SEG_PER_SAMPLE
    tabs = _rope_tables()
    w = dict(w)
    fs = _flat_small(small)
    g_mix, g_ffn, g_fin, dskip = fs["g_mix"], fs["g_ffn"], fs["g_fin"], fs["dskip"]
    a_cat, bbc, cc = _ssm_disc(fs["lr"], fs["li"], fs["ldt"], fs["br"], fs["bi"], fs["cr"], fs["ci"])
    big, recv, small_pack = {}, {}, []

    def comm_of(name):
        if shards is None:
            return None
        if name == SMALL_HOST:
            return _ag_comm([(small_pack[0], 0, 0)], [(N_DEV, *small_pack[0].shape)])
        if name in AG_HOSTS:
            names = AG_HOSTS[name]
            return _ag_comm([(shards[n], j, 0) for j, n in enumerate(names)], [(N_DEV, *shards[n].shape) for n in names])
        if name in A2A_HOSTS:
            return _a2a_comm([(big[n].reshape(N_DEV, -1, big[n].shape[1]), 0) for n in A2A_HOSTS[name]])
        return None

    def absorb(name, carried):
        if name == SMALL_HOST:
            recv["small"] = carried[0]
        for n, a3 in zip(AG_HOSTS.get(name, ()), carried):
            w[n] = a3.reshape(-1, a3.shape[2])
        for n, a3 in zip(A2A_HOSTS.get(name, ()), carried):
            recv[n] = a3

    def mm(a, b, mode, name, tm, tn, **kw):
        comm = comm_of(name)
        if comm is None:
            return _mm(a, b, mode, name, tm, tn, **kw)
        out, *carried = _mm(a, b, mode, name, tm, tn, comm=comm, **kw)
        absorb(name, carried)
        return out

    h0, u, gates, *rest = _proj_rope(x, g_mix, w["w_in"], tabs, comm_of("proj_rope"))
    qkv = [rest[3 * g:3 * g + 3] for g in range(3)]
    absorb("proj_rope", rest[9:])
    os_, lses = [], []
    for g in range(3):
        o_g, l_g, carried = _attn_fwd(*qkv[g], g, n_samples, comm_of(f"attn_fwd_g{g}"))
        absorb(f"attn_fwd_g{g}", carried)
        os_.append(o_g)
        lses.append(l_g)
    u_perm = _to_scan_rows(u, n_samples)
    ytot, yg_perm, ein = _ssm_fwd(u_perm, a_cat, bbc, cc, dskip, n_rows)
    yg = _from_scan_rows(yg_perm, n_samples)

    attn, lse_tot, merged, x1, h2, *carried = _mix_out_rms(os_, lses, yg, gates, x, w["w_attn_out"], w["w_glu"], w["w_out"],
                                                           g_ffn, comm_of("mix_out_rms"))
    absorb("mix_out_rms", carried)
    ffn_a, ffn_b, f, *carried = _ffn_in_swiglu(h2, w["w_ffn_gate"], w["w_ffn_up"], comm_of("ffn_in_swiglu"))
    absorb("ffn_in_swiglu", carried)
    dx2, dx2b, loss_blk, g_gfin = _ffn_down_final(f, w["w_ffn_down"], x1, target, g_fin)

    da, db = _d_f_swiglu_bwd(dx2b, w["w_ffn_down"], ffn_a, ffn_b)
    big["w_ffn_down"] = mm(f, dx2b, "tn", "mm_g_down", 256, D_MODEL, out_dtype=BF16)
    half = D_MODEL // 2
    for hf in range(2):
        big[f"w_ffn_gate:{hf}"] = mm(da, h2, "tn", f"mm_g_gate{hf}", 256, half, out_dtype=BF16, cols=(hf * half, half))
        big[f"w_ffn_up:{hf}"] = mm(db, h2, "tn", f"mm_g_up{hf}", 256, half, out_dtype=BF16, cols=(hf * half, half))
    dx1, dx1b, g_gffn, *carried = _mm_rms_bwd([da, db], [w["w_ffn_gate"], w["w_ffn_up"]], x1, g_ffn, dx2, "d_h2_rms",
                                              comm_of("d_h2_rms"))
    absorb("d_h2_rms", carried)

    big["w_out"] = mm(merged, dx1b, "tn", "mm_g_out", 256, D_MODEL, out_dtype=BF16)
    dattn_d, dz, dgpre, dattn, dyg, rowdot, *rest = _mix_bwd(dx1b, attn, lse_tot, yg, gates, w["w_attn_out"], w["w_glu"],
                                                             w["w_out"], comm_of("mix_bwd"))
    cot = [(dattn, lse_tot, rowdot), tuple(rest[:3]), tuple(rest[3:6])]
    absorb("mix_bwd", rest[6:])

    big["w_attn_out"] = mm(dattn_d, attn, "tn", "mm_g_attn_out", 512, GROUP_W, out_dtype=BF16)
    big["w_glu"] = mm(dz, yg, "tn", "mm_g_glu", 512, 512, out_dtype=BF16)
    dqs, dks, dvs = [], [], []
    for g in range(3):
        dq_g, dk_g, dv_g, carried = _attn_bwd(*qkv[g], *cot[g], g, n_samples, comm_of(f"attn_bwd_g{g}"))
        absorb(f"attn_bwd_g{g}", carried)
        dqs.append(dq_g)
        dks.append(dk_g)
        dvs.append(dv_g)

    dyg_perm = _to_scan_rows(dyg, n_samples)
    du_perm, g_dskip, da_cat, dbb_full, dc_full, *carried = _ssm_bwd(u_perm, dyg_perm, ytot, dskip, a_cat, bbc, cc, ein,
                                                                   n_rows, comm_of("ssm_bwd"))
    absorb("ssm_bwd", carried)
    du = _from_scan_rows(du_perm, n_samples)
    g_lr, g_li, g_ldt, g_br, g_bi, g_cr, g_ci = _ssm_param_bwd(
        fs["lr"], fs["li"], fs["ldt"], fs["br"], fs["bi"], da_cat, dbb_full, dc_full)

    small_pack.append(_pack_small(dict(lr=g_lr, li=g_li, ldt=g_ldt, br=g_br, bi=g_bi, cr=g_cr, ci=g_ci, dskip=g_dskip,
                                       g_ffn=g_gffn, g_fin=g_gfin, loss=loss_blk)))

    dproj = _pack_dproj(dqs, dks, dvs, du, dgpre, tabs)
    for hf in range(2):
        big[f"w_in:{hf}"] = mm(dproj, h0, "tn", f"mm_g_in{hf}", 256, half, out_dtype=BF16, cols=(hf * half, half))
    grad_x, _, g_gmix, *carried = _mm_rms_bwd([dproj], [w["w_in"]], x, g_mix, dx1, "d_h0_rms", comm_of("d_h0_rms"))
    absorb("d_h0_rms", carried)
    return grad_x, (big if shards is None else recv), small_pack[0], g_gmix


_MESH = pl.DeviceIdType.MESH


def _all_gather(block, name):
    rows, lanes = block.shape

    def body(x_ref, out_ref, send_sems, recv_sems, local_sem):
        x, y, c = lax.axis_index("x"), lax.axis_index("y"), lax.axis_index("c")
        me, sibling = (x, y, c), (x, y, 1 - c)
        chips = [(1 - x, y), (x, 1 - y), (1 - x, 1 - y)]

        def slot(px, py, pc):
            return out_ref.at[4 * px + 2 * py + pc]

        def copy(k, blk, to, src=None):
            return pltpu.make_async_remote_copy(
                src_ref=slot(*blk) if src is None else src, dst_ref=slot(*blk), send_sem=send_sems.at[k],
                recv_sem=recv_sems.at[k], device_id=to, device_id_type=_MESH)

        mine = pltpu.make_async_copy(x_ref, slot(*me), local_sem)
        mine.start()
        first = [copy(0, me, sibling, src=x_ref)]
        first += [copy(1 + j, me, (*chip, c), src=x_ref) for j, chip in enumerate(chips)]
        for cp in first:
            cp.start()
        passed = [copy(4 + j, (*chip, c), sibling) for j, chip in enumerate(chips)]
        for j, chip in enumerate(chips):
            copy(1 + j, (*chip, c), me).wait_recv()
            passed[j].start()
        copy(0, sibling, me).wait_recv()
        for j, chip in enumerate(chips):
            copy(4 + j, (*chip, 1 - c), me).wait_recv()
        for cp in first + passed:
            cp.wait_send()
        mine.wait()

    return _pallas_call(
        body, name=name, out_shape=jax.ShapeDtypeStruct((N_DEV, rows, lanes), block.dtype),
        in_specs=[pl.BlockSpec(memory_space=pl.ANY)], out_specs=pl.BlockSpec(memory_space=pl.ANY),
        scratch_shapes=[pltpu.SemaphoreType.DMA((7,)), pltpu.SemaphoreType.DMA((7,)), pltpu.SemaphoreType.DMA],
    )(block)


def _ag_comm(items, bufs):
    def plan(in_refs, out_refs, send_sems, recv_sems, local_sems):
        x, y, c = lax.axis_index("x"), lax.axis_index("y"), lax.axis_index("c")
        me, sibling = (x, y, c), (x, y, 1 - c)
        chips = [(1 - x, y), (x, 1 - y), (1 - x, 1 - y)]
        plans = []
        for t, (_, buf, slot0) in enumerate(items):
            x_ref, out_ref = in_refs[t], out_refs[buf]

            def slot(px, py, pc, out_ref=out_ref, slot0=slot0):
                return out_ref.at[slot0 + 4 * px + 2 * py + pc]

            def copy(k, blk, to, src=None, t=t, slot=slot):
                return pltpu.make_async_remote_copy(
                    src_ref=slot(*blk) if src is None else src, dst_ref=slot(*blk), send_sem=send_sems.at[7 * t + k],
                    recv_sem=recv_sems.at[7 * t + k], device_id=to, device_id_type=_MESH)

            plans.append(dict(
                mine=pltpu.make_async_copy(x_ref, slot(*me), local_sems.at[t]),
                first=[copy(0, me, sibling, src=x_ref)] + [copy(1 + j, me, (*chip, c), src=x_ref)
                                                           for j, chip in enumerate(chips)],
                passed=[copy(4 + j, (*chip, c), sibling) for j, chip in enumerate(chips)],
                from_ici=[copy(1 + j, (*chip, c), me) for j, chip in enumerate(chips)],
                from_sibling=[copy(0, sibling, me)] + [copy(4 + j, (*chip, 1 - c), me) for j, chip in enumerate(chips)]))
        return plans

    def start(*refs):
        for p in plan(*refs):
            p["mine"].start()
            for cp in p["first"]:
                cp.start()

    def finish(*refs):
        plans = plan(*refs)
        for p in plans:
            for arrived, onward in zip(p["from_ici"], p["passed"]):
                arrived.wait_recv()
                onward.start()
        for p in plans:
            for arrived in p["from_sibling"]:
                arrived.wait_recv()
            for cp in p["first"] + p["passed"]:
                cp.wait_send()
            p["mine"].wait()

    dtype_of = {buf: shard.dtype for shard, buf, _ in items}
    out_shapes = [jax.ShapeDtypeStruct(b, dtype_of[j]) for j, b in enumerate(bufs)]
    return _Comm([it[0] for it in items], out_shapes, 7 * len(items), len(items), start, finish)


def _a2a_comm(items):
    def plan(in_refs, out_refs, send_sems, recv_sems, local_sems):
        x, y, c = lax.axis_index("x"), lax.axis_index("y"), lax.axis_index("c")
        my = 4 * x + 2 * y + c
        copies, locals_ = [], []
        for t, (_, slot0) in enumerate(items):
            s_ref, r_ref = in_refs[t], out_refs[t]
            locals_.append(pltpu.make_async_copy(s_ref.at[slot0 + my], r_ref.at[my], local_sems.at[t]))
            for kk in range(1, N_DEV):
                px = 1 - x if kk & 4 else x
                py = 1 - y if kk & 2 else y
                pc = 1 - c if kk & 1 else c
                copies.append(pltpu.make_async_remote_copy(
                    src_ref=s_ref.at[slot0 + 4 * px + 2 * py + pc], dst_ref=r_ref.at[my],
                    send_sem=send_sems.at[7 * t + kk - 1], recv_sem=recv_sems.at[7 * t + kk - 1],
                    device_id=(px, py, pc), device_id_type=_MESH))
        return copies, locals_

    def start(*refs):
        copies, locals_ = plan(*refs)
        for cp in locals_ + copies:
            cp.start()

    def finish(*refs):
        copies, locals_ = plan(*refs)
        for cp in copies + locals_:
            cp.wait()

    out_shapes = [jax.ShapeDtypeStruct((N_DEV,) + it[0].shape[1:], it[0].dtype) for it in items]
    return _Comm([it[0] for it in items], out_shapes, 7 * len(items), len(items), start, finish)


def _adam_math(g, w, m, v):
    m_new = ADAM_B1 * m + (1.0 - ADAM_B1) * g
    v_new = ADAM_B2 * v + (1.0 - ADAM_B2) * jnp.square(g)
    m_hat = m_new / (1.0 - ADAM_B1 ** ADAM_STEP)
    v_hat = v_new / (1.0 - ADAM_B2 ** ADAM_STEP)
    return -ADAM_LR * (m_hat / (jnp.sqrt(v_hat) + ADAM_EPS) + ADAM_WD * w), m_new, v_new


def _sum_partials(parts, name, tm):
    n, rows, _ = parts[0].shape
    widths = [p.shape[2] for p in parts]

    def body(*refs):
        g_ref, off = refs[-1], 0
        for p_ref, wd in zip(refs[:-1], widths):
            g = p_ref[0].astype(F32)
            for s in range(1, n):
                g = g + p_ref[s].astype(F32)
            g_ref[:, off:off + wd] = g
            off += wd

    return _pallas_call(
        body, name=name, grid=(rows // tm,), in_specs=[pl.BlockSpec((n, tm, wd), lambda i: (0, i, 0)) for wd in widths],
        out_specs=pl.BlockSpec((tm, sum(widths)), lambda i: (i, 0)),
        out_shape=jax.ShapeDtypeStruct((rows, sum(widths)), F32),
        compiler_params=pltpu.CompilerParams(dimension_semantics=("parallel",), vmem_limit_bytes=VMEM_MID),
    )(*parts)


def _adam(parts, w, m, v, name, tm):
    n, rows, _ = parts[0].shape
    widths = [p.shape[2] for p in parts]
    cols = sum(widths)

    def body(*refs):
        p_refs, (w_ref, m_ref, v_ref, g_ref, d_ref, nm_ref, nv_ref) = refs[:len(parts)], refs[len(parts):]
        off = 0
        for p_ref, wd in zip(p_refs, widths):
            g = p_ref[0].astype(F32)
            for s in range(1, n):
                g = g + p_ref[s].astype(F32)
            sl = slice(off, off + wd)
            g_ref[:, sl] = g
            d_ref[:, sl], nm_ref[:, sl], nv_ref[:, sl] = _adam_math(g, w_ref[:, sl], m_ref[:, sl], v_ref[:, sl])
            off += wd

    assert rows % tm == 0
    row = pl.BlockSpec((tm, cols), lambda i: (i, 0))
    shp = jax.ShapeDtypeStruct((rows, cols), F32)
    return _pallas_call(
        body, name=name, grid=(rows // tm,),
        in_specs=[pl.BlockSpec((n, tm, wd), lambda i: (0, i, 0)) for wd in widths] + [row, row, row],
        out_specs=[row] * 4, out_shape=[shp] * 4,
        compiler_params=pltpu.CompilerParams(dimension_semantics=("parallel",), vmem_limit_bytes=VMEM_MID),
    )(*parts, w, m, v)


_PK_LR, _PK_LI, _PK_GAINS, _PK_MISC, _PK_BR, _PK_BI, _PK_CR, _PK_CI, _PK_ROWS = 0, 1, 2, 3, 8, 24, 40, 56, 72
_PK_LDT_LANE, _PK_LOSS_LANE = D_MODEL + SSM_W, D_MODEL + SSM_W + LANES


def _pack_small(sg):
    names = ("lr", "li", "g_ffn", "g_fin", "dskip", "ldt", "loss", "br", "bi", "cr", "ci")

    def body(lr, li, gffn, gfin, dskip, ldt, loss, br, bi, cr, ci, o_ref):
        o_ref[...] = jnp.zeros_like(o_ref)
        o_ref[_PK_LR:_PK_LR + 1, :] = lr[...]
        o_ref[_PK_LI:_PK_LI + 1, :] = li[...]
        o_ref[_PK_GAINS:_PK_GAINS + 1, D_MODEL:] = gffn[...]
        o_ref[_PK_MISC:_PK_MISC + 1, :D_MODEL] = gfin[...]
        o_ref[_PK_MISC:_PK_MISC + 1, D_MODEL:D_MODEL + SSM_W] = dskip[...]
        o_ref[_PK_MISC:_PK_MISC + 1, _PK_LDT_LANE:_PK_LDT_LANE + LANES] = ldt[0:1, :]
        o_ref[_PK_MISC:_PK_MISC + 1, _PK_LOSS_LANE:_PK_LOSS_LANE + LANES] = loss[0:1, :]
        o_ref[_PK_BR:_PK_BR + SSM_CH, :] = br[...]
        o_ref[_PK_BI:_PK_BI + SSM_CH, :] = bi[...]
        o_ref[_PK_CR:_PK_CR + SSM_CH, :] = cr[...]
        o_ref[_PK_CI:_PK_CI + SSM_CH, :] = ci[...]

    return _pallas_call(body, name="pack_small", out_shape=jax.ShapeDtypeStruct((_PK_ROWS, N_STATE), F32))(
        *[sg[n] for n in names])


def _unpack_small(s, g_mix):
    unflat_b = unflat_c = lambda a: a.reshape(SSM_CH, SSM_GROUPS, SSM_STATE).transpose(1, 0, 2)[None]
    grads = {
        "norm_mix_g": g_mix, "norm_ffn_g": s[_PK_GAINS, D_MODEL:].reshape(1, D_MODEL),
        "norm_final_g": s[_PK_MISC, :D_MODEL].reshape(1, D_MODEL),
        "ssm_a_re": s[_PK_LR].reshape(1, SSM_GROUPS, SSM_STATE), "ssm_a_im": s[_PK_LI].reshape(1, SSM_GROUPS, SSM_STATE),
        "ssm_log_dt": s[_PK_MISC, _PK_LDT_LANE:_PK_LDT_LANE + SSM_GROUPS].reshape(1, SSM_GROUPS),
        "ssm_d": s[_PK_MISC, D_MODEL:D_MODEL + SSM_W].reshape(1, SSM_GROUPS, SSM_CH),
        "ssm_b_re": unflat_b(s[_PK_BR:_PK_BR + SSM_CH]), "ssm_b_im": unflat_b(s[_PK_BI:_PK_BI + SSM_CH]),
        "ssm_c_re": unflat_c(s[_PK_CR:_PK_CR + SSM_CH]), "ssm_c_im": unflat_c(s[_PK_CI:_PK_CI + SSM_CH]),
    }
    return s[_PK_MISC, _PK_LOSS_LANE], grads


def _stored(name, a):
    if name in ("ssm_b_re", "ssm_b_im"):
        return a.transpose(0, 1, 3, 2)
    return a.reshape(1, -1) if a.ndim == 1 else a


def _unstored(name, a, like):
    return a.transpose(0, 1, 3, 2) if name in ("ssm_b_re", "ssm_b_im") else a.reshape(like.shape)


def _adam_small(grads, wts, moms, vars_):
    n = len(SMALL_WEIGHTS)

    def body(*refs):
        ins, outs = refs[:4 * n], refs[4 * n:]
        for i in range(n):
            g, w, m, v = (ins[j * n + i][...] for j in range(4))
            outs[i][...], outs[n + i][...], outs[2 * n + i][...] = _adam_math(g, w, m, v)

    operands = [grads[k] if d is grads else _stored(k, d[k]) for d in (grads, wts, moms, vars_) for k in SMALL_WEIGHTS]
    shapes = [jax.ShapeDtypeStruct(_stored(k, wts[k]).shape, F32) for k in SMALL_WEIGHTS] * 3
    res = _pallas_call(body, name="adam_small", out_shape=shapes,
                         compiler_params=pltpu.CompilerParams(vmem_limit_bytes=VMEM_BIG))(*operands)
    out = {}
    for j, kind in enumerate(("delta", "new_m", "new_v")):
        for i, k in enumerate(SMALL_WEIGHTS):
            out[kind, k] = _unstored(k, res[j * n + i], wts[k])
    return out


def kernel(x, norm_mix_g, w_in, ssm_a_re, ssm_a_im, ssm_log_dt, ssm_b_re, ssm_b_im, ssm_c_re, ssm_c_im, ssm_d, w_glu, w_attn_out, w_out, norm_ffn_g, w_ffn_gate, w_ffn_up, w_ffn_down, norm_final_g, loss_target, m_norm_mix_g, m_w_in, m_ssm_a_re, m_ssm_a_im, m_ssm_log_dt, m_ssm_b_re, m_ssm_b_im, m_ssm_c_re, m_ssm_c_im, m_ssm_d, m_w_glu, m_w_attn_out, m_w_out, m_norm_ffn_g, m_w_ffn_gate, m_w_ffn_up, m_w_ffn_down, m_norm_final_g, v_norm_mix_g, v_w_in, v_ssm_a_re, v_ssm_a_im, v_ssm_log_dt, v_ssm_b_re, v_ssm_b_im, v_ssm_c_re, v_ssm_c_im, v_ssm_d, v_w_glu, v_w_attn_out, v_w_out, v_norm_ffn_g, v_w_ffn_gate, v_w_ffn_up, v_w_ffn_down, v_norm_final_g):
    args = dict(locals())
    wts = {n: args[n] for n in ALL_WEIGHTS}
    moms = {n: args["m_" + n] for n in ALL_WEIGHTS}
    vars_ = {n: args["v_" + n] for n in ALL_WEIGHTS}
    n_samples = x.shape[0]
    t = n_samples * SEQ

    shards = {n: (wts[n][0] if n in ROW_SHARDED else wts[n][0].T).astype(BF16) for n in BIG_WEIGHTS}
    w_in_t = _all_gather(shards["w_in"], "allgather_w_in").reshape(IN_W, D_MODEL)

    small = {n: wts[n] for n in SMALL_WEIGHTS}
    grad_x, recv, _, g_mix_part = _local_step(x.reshape(t, D_MODEL), loss_target.reshape(t, D_MODEL), {"w_in": w_in_t},
                                              small, shards)

    results = {}
    for n in BIG_WEIGHTS:
        c, k = shards[n].shape
        w2, m2, v2 = wts[n][0], moms[n][0], vars_[n][0]
        if n in ROW_SHARDED:
            res = _adam([recv[n]], w2, m2, v2, "adam_" + n, c // 2)
        elif n in HALVED:
            res = _adam([recv[f"{n}:{hf}"] for hf in range(2)], w2.T, m2.T, v2.T, "adam_" + n, c // 2)
            res = [a.T for a in res]
        else:
            g_t = _sum_partials([recv[n]], "sum_" + n, c // 2)
            res = _adam([g_t.T[None]], w2, m2, v2, "adam_" + n, k // 2)
        for kind, a in zip(("grad", "delta", "new_m", "new_v"), res):
            results[kind, n] = a[None]

    g_mix_all = _all_gather(jnp.pad(g_mix_part, ((0, 7), (0, 0))), "allgather_g_mix")
    g_mix = _sum_partials([g_mix_all], "sum_g_mix", 8)[0:1]
    loss, sgrads = _unpack_small(_sum_partials([recv["small"]], "sum_small", _PK_ROWS), g_mix)
    for n in SMALL_WEIGHTS:
        results["grad", n] = _unstored(n, sgrads[n], wts[n])
    results.update(_adam_small(sgrads, wts, moms, vars_))
    outs = [loss, grad_x.reshape(x.shape)]
    for kind in ("grad", "delta", "new_m", "new_v"):
        outs += [results[kind, n] for n in ALL_WEIGHTS]
    return tuple(outs)
```

```python
import functools
import math

import jax
import jax.numpy as jnp
from jax import lax
from jax.experimental import pallas as pl
from jax.experimental.pallas import tpu as pltpu

F32 = jnp.float32
BF16 = jnp.bfloat16
MXU_DTYPE = jnp.bfloat16

N_DEV = 8
D_MODEL = 1024
SEQ = 2048
HEAD_DIM = 64
HEADS_PER_GROUP = 4
GROUP_W = HEADS_PER_GROUP * HEAD_DIM
DILATIONS = (1, 4, 16)
QKV_W = 3 * len(DILATIONS) * GROUP_W
Q_W = len(DILATIONS) * GROUP_W
ATT_BLOCK = 128
ROPE_DIM = 16
ROPE_THETA = 500000.0
SSM_W = 512
SSM_GROUPS = 32
SSM_CH = 16
SSM_STATE = 64
N_STATE = SSM_GROUPS * SSM_STATE
D_FF = 2816
IN_W = QKV_W + SSM_W + 2 * D_MODEL
RMS_EPS = 1e-6
NEG_INF = -1e30
LANES = 128

SCAN_SEG_PER_SAMPLE = 8
SCAN_LEN = SEQ // SCAN_SEG_PER_SAMPLE
SCAN_WC = 512
SCAN_NBLK = N_STATE // SCAN_WC
SCAN_CH = SSM_W // SCAN_NBLK
SCAN_CHUNK = 32

ADAM_LR = 0.001
ADAM_B1 = 0.9
ADAM_B2 = 0.999
ADAM_EPS = 1e-08
ADAM_WD = 0.01
ADAM_STEP = 10

VMEM_BIG = 48 * 1024 * 1024
VMEM_MID = 32 * 1024 * 1024

BIG_WEIGHTS = ("w_in", "w_glu", "w_attn_out", "w_out", "w_ffn_gate", "w_ffn_up", "w_ffn_down")
ROW_SHARDED = ("w_out", "w_ffn_down")
SMALL_WEIGHTS = ("norm_mix_g", "ssm_a_re", "ssm_a_im", "ssm_log_dt", "ssm_b_re", "ssm_b_im", "ssm_c_re", "ssm_c_im",
                 "ssm_d", "norm_ffn_g", "norm_final_g")
ALL_WEIGHTS = ("norm_mix_g", "w_in", "ssm_a_re", "ssm_a_im", "ssm_log_dt", "ssm_b_re", "ssm_b_im", "ssm_c_re", "ssm_c_im",
               "ssm_d", "w_glu", "w_attn_out", "w_out", "norm_ffn_g", "w_ffn_gate", "w_ffn_up", "w_ffn_down", "norm_final_g")


def _sigmoid(x):
    return 1.0 / (1.0 + jnp.exp(-x))


def _pallas_call(body, *, out_shape, **kw):
    single = not isinstance(out_shape, (list, tuple))
    shapes = [pltpu.HBM(s.shape, s.dtype) for s in ([out_shape] if single else out_shape)]
    call = pl.pallas_call(body, out_shape=shapes[0] if single else shapes, **kw)
    return lambda *operands: call(*[pltpu.with_memory_space_constraint(o, pltpu.HBM) for o in operands])


class _Comm:
    def __init__(self, ins, out_shapes, n_sem, n_local, start, finish):
        self.ins, self.out_shapes, self.n_sem, self.n_local = ins, out_shapes, n_sem, n_local
        self.start, self.finish = start, finish


def _mm(a, b, mode, name, tm, tn, out_dtype=F32, add=None, vmem=VMEM_BIG, comm=None, cols=None):
    if mode == "nn":
        (m, k), (_, n) = a.shape, b.shape
        a_spec = pl.BlockSpec((tm, k), lambda i, j: (i, 0))
        b_spec = pl.BlockSpec((k, tn), lambda i, j: (0, j))
        dims = (((1,), (0,)), ((), ()))
    elif mode == "nt":
        (m, k), (n, _) = a.shape, b.shape
        a_spec = pl.BlockSpec((tm, k), lambda i, j: (i, 0))
        b_spec = pl.BlockSpec((tn, k), lambda i, j: (j, 0))
        dims = (((1,), (1,)), ((), ()))
    else:
        (k, m), (_, n) = a.shape, b.shape
        first, n = cols if cols else (0, n)
        a_spec = pl.BlockSpec((k, tm), lambda i, j: (0, i))
        b_spec = pl.BlockSpec((k, tn), lambda i, j: (0, j + first // tn))
        dims = (((0,), (0,)), ((), ()))
    assert m % tm == 0 and n % tn == 0, (name, m, n, tm, tn)
    o_spec = pl.BlockSpec((tm, tn), lambda i, j: (i, j))
    has_add = add is not None

    def body(*refs):
        a_ref, b_ref, o_ref = refs[0], refs[1], refs[-1]
        acc = lax.dot_general(a_ref[...].astype(MXU_DTYPE), b_ref[...].astype(MXU_DTYPE), dims,
                              preferred_element_type=F32)
        if has_add:
            acc = acc + refs[2][...]
        o_ref[...] = acc.astype(out_dtype)

    ins = [a, b] + ([add] if has_add else [])
    in_specs = [a_spec, b_spec] + ([o_spec] if has_add else [])
    return _grid_call(body, name, (m // tm, n // tn), ins, in_specs, [o_spec],
                      [jax.ShapeDtypeStruct((m, n), out_dtype)], vmem, comm)


def _grid_call(body, name, grid, ins, in_specs, out_specs, out_shapes, vmem, comm=None, sequential=False, scratch=()):
    if comm is None:
        single = len(out_shapes) == 1
        semantics = ("arbitrary", "arbitrary") if sequential else ("parallel", "parallel")
        return _pallas_call(
            body, name=name, grid=grid, in_specs=in_specs, out_specs=out_specs[0] if single else out_specs,
            out_shape=out_shapes[0] if single else out_shapes, scratch_shapes=list(scratch),
            compiler_params=pltpu.CompilerParams(dimension_semantics=semantics, vmem_limit_bytes=vmem),
        )(*ins)
    n_in, n_out, n_cin, n_cout = len(ins), len(out_shapes), len(comm.ins), len(comm.out_shapes)
    n_io = n_in + n_cin + n_out + n_cout

    def carrying(*refs):
        own = refs[:n_in] + refs[n_in + n_cin:n_in + n_cin + n_out] + refs[n_io:len(refs) - 3]
        c_args = (refs[n_in:n_in + n_cin], refs[n_in + n_cin + n_out:n_io], *refs[-3:])

        @pl.when((pl.program_id(0) == 0) & (pl.program_id(1) == 0))
        def _():
            comm.start(*c_args)

        body(*own)

        @pl.when((pl.program_id(0) == grid[0] - 1) & (pl.program_id(1) == grid[1] - 1))
        def _():
            comm.finish(*c_args)

    hbm = pl.BlockSpec(memory_space=pl.ANY)
    return _pallas_call(
        carrying, name=name, grid=grid, in_specs=list(in_specs) + [hbm] * n_cin,
        out_specs=list(out_specs) + [hbm] * n_cout, out_shape=list(out_shapes) + list(comm.out_shapes),
        scratch_shapes=list(scratch) + [pltpu.SemaphoreType.DMA((comm.n_sem,)), pltpu.SemaphoreType.DMA((comm.n_sem,)),
                                        pltpu.SemaphoreType.DMA((comm.n_local,))],
        compiler_params=pltpu.CompilerParams(dimension_semantics=("arbitrary", "arbitrary"), vmem_limit_bytes=vmem),
    )(*ins, *comm.ins)


def _rows(body, name, n_rows, tm, ins, outs, vmem=VMEM_MID, scratch=()):
    assert n_rows % tm == 0
    arrays, in_specs = [], []
    for kind, arr in ins:
        arrays.append(arr)
        if kind == "row":
            assert n_rows % arr.shape[0] == 0, (name, arr.shape)
            in_specs.append(pl.BlockSpec((tm * arr.shape[0] // n_rows, arr.shape[1]), lambda i: (i, 0)))
        elif kind == "tab":
            nblk = arr.shape[0] // tm
            in_specs.append(pl.BlockSpec((tm, arr.shape[1]), lambda i, nblk=nblk: (i % nblk, 0)))
        else:
            in_specs.append(pl.BlockSpec(arr.shape, lambda i, nd=arr.ndim: (0,) * nd))
    out_specs, out_shape = [], []
    for kind, shp, dt in outs:
        if kind == "row":
            out_specs.append(pl.BlockSpec((tm, shp), lambda i: (i, 0)))
            out_shape.append(jax.ShapeDtypeStruct((n_rows, shp), dt))
        elif kind == "dil":
            d, wd = shp
            out_specs.append(pl.BlockSpec((tm // d, d * wd), lambda i: (i, 0)))
            out_shape.append(jax.ShapeDtypeStruct((n_rows // d, d * wd), dt))
        else:
            out_specs.append(pl.BlockSpec(shp, lambda i, nd=len(shp): (0,) * nd))
            out_shape.append(jax.ShapeDtypeStruct(shp, dt))
    res = _pallas_call(
        body, name=name, grid=(n_rows // tm,), in_specs=in_specs, out_specs=out_specs, out_shape=out_shape,
        scratch_shapes=list(scratch),
        compiler_params=pltpu.CompilerParams(dimension_semantics=("arbitrary",), vmem_limit_bytes=vmem),
    )(*arrays)
    return res


def _gather_residue(stage, ch, r, d, n):
    return stage[ch, pl.ds(r, n, stride=d), :] if d > 1 else stage[ch]


def _scatter_residue(stage, ch, r, d, n, val):
    if d > 1:
        stage[ch, pl.ds(r, n, stride=d), :] = val
    else:
        stage[ch] = val


def _lane_chunk(ch):
    return slice(ch * LANES, (ch + 1) * LANES)


def _rope_tables():
    half = ROPE_DIM // 2
    inv = jnp.power(jnp.float32(ROPE_THETA), -jnp.arange(half, dtype=F32) * 2.0 / ROPE_DIM)
    ang = jnp.arange(SEQ, dtype=F32)[:, None] * inv[None, :]
    lane = jnp.arange(LANES) % HEAD_DIM
    cosl = jnp.cos(ang)[:, lane % half]
    sinl = jnp.sin(ang)[:, lane % half]
    tab_c = jnp.where(lane < ROPE_DIM, cosl, 1.0)
    tab_lo = jnp.where(lane < half, -sinl, 0.0)
    tab_hi = jnp.where((lane >= half) & (lane < ROPE_DIM), sinl, 0.0)
    return tab_c.astype(F32), tab_lo.astype(F32), tab_hi.astype(F32)


def _rope_apply(t, tc, tlo, thi):
    half = ROPE_DIM // 2
    return t * tc + pltpu.roll(t, LANES - half, 1) * tlo + pltpu.roll(t, half, 1) * thi


def _rope_transpose(dt, tc, tlo, thi):
    half = ROPE_DIM // 2
    return dt * tc + pltpu.roll(dt * tlo, half, 1) + pltpu.roll(dt * thi, LANES - half, 1)


def _pack_dproj(dqs, dks, dvs, du, dgpre, tabs):
    tm = 256

    def body(*refs):
        dq_refs, dk_refs, dv_refs = refs[0:3], refs[3:6], refs[6:9]
        du_ref, dg_ref, tc_ref, tlo_ref, thi_ref, o_ref, stage = refs[9:16]
        n_ch = QKV_W // LANES
        halves = GROUP_W // LANES
        for grp, d in enumerate(DILATIONS):
            for which, src in enumerate((dq_refs[grp], dk_refs[grp], dv_refs[grp])):
                for res in range(d):
                    for half in range(halves):
                        _scatter_residue(stage, which * (n_ch // 3) + grp * halves + half, res, d, tm // d,
                                         src[:, _lane_chunk(res * halves + half)])
        tc, tlo, thi = tc_ref[...], tlo_ref[...], thi_ref[...]
        for ch in range(n_ch):
            piece = stage[ch]
            o_ref[:, _lane_chunk(ch)] = (_rope_transpose(piece, tc, tlo, thi) if ch < 2 * n_ch // 3 else piece).astype(BF16)
        o_ref[:, QKV_W:QKV_W + SSM_W] = du_ref[...].astype(BF16)
        o_ref[:, QKV_W + SSM_W:] = dg_ref[...].astype(BF16)

    t = du.shape[0]
    ins = [("row", a) for a in (*dqs, *dks, *dvs, du, dgpre)] + [("tab", tb) for tb in tabs]
    return _rows(body, "pack_dproj", t, tm, ins, [("row", IN_W, BF16)],
                 scratch=[pltpu.VMEM((QKV_W // LANES, tm, LANES), F32)])[0]


def _merge_groups(o_refs, l_refs, a_ref, lt_ref, nat, tm):
    halves = GROUP_W // LANES
    for grp, d in enumerate(DILATIONS[1:], start=1):
        for j, src in enumerate((o_refs[grp], l_refs[grp])):
            for res in range(d):
                for half in range(halves):
                    _scatter_residue(nat, (grp - 1) * 4 + j * 2 + half, res, d, tm // d,
                                     src[:, _lane_chunk(res * halves + half)])
    for half in range(halves):
        sl = _lane_chunk(half)
        la, lb, lc = l_refs[0][:, sl], nat[2 + half], nat[6 + half]
        m = jnp.maximum(jnp.maximum(la, lb), lc)
        ea, eb, ec = jnp.exp(la - m), jnp.exp(lb - m), jnp.exp(lc - m)
        ssum = ea + eb + ec
        a_ref[:, sl] = (ea / ssum) * o_refs[0][:, sl] + (eb / ssum) * nat[half] + (ec / ssum) * nat[4 + half]
        lt_ref[:, sl] = m + jnp.log(ssum)


def _head_sum_matrix():
    r = jnp.arange(GROUP_W) // HEAD_DIM
    return (r[:, None] == r[None, :]).astype(F32)


def _attention_cotangents(da, attn, lt, ones, rd_ref, dil, stage, tm):
    halves = GROUP_W // LANES
    rd = jnp.dot(da * attn, ones, preferred_element_type=F32, precision=lax.Precision.HIGHEST)
    rd_ref[...] = rd
    for half in range(halves):
        for j, val in enumerate((da, lt, rd)):
            stage[2 * j + half] = val[:, _lane_chunk(half)]
    for grp, d in enumerate(DILATIONS[1:], start=1):
        for j in range(3):
            for res in range(d):
                for half in range(halves):
                    dil[3 * (grp - 1) + j][:, _lane_chunk(res * halves + half)] = _gather_residue(
                        stage, 2 * j + half, res, d, tm // d)


_GELU_C = math.sqrt(2.0 / math.pi)


def _head_masks():
    lane = lax.broadcasted_iota(jnp.int32, (1, GROUP_W), 1)
    return [(lane // HEAD_DIM) == h for h in range(HEADS_PER_GROUP)]


def _stack_heads(blk, masks, fill=0.0):
    return jnp.concatenate([jnp.where(mk, blk, jnp.full_like(blk, fill)) for mk in masks], axis=0)


def _unstack_heads(stacked, masks):
    rows = stacked.shape[0] // len(masks)
    out = stacked[:rows]
    for h in range(1, len(masks)):
        out = jnp.where(masks[h], stacked[h * rows:(h + 1) * rows], out)
    return out


def _band_mask(first):
    nk = ATT_BLOCK if first else 2 * ATT_BLOCK
    qi = lax.broadcasted_iota(jnp.int32, (ATT_BLOCK, nk), 0)
    ki = lax.broadcasted_iota(jnp.int32, (ATT_BLOCK, nk), 1)
    dist = qi - ki + (0 if first else ATT_BLOCK)
    return (dist >= 0) & (dist <= ATT_BLOCK)


_NT = (((1,), (1,)), ((), ()))
_TN = (((0,), (0,)), ((), ()))


def _residues_per_step(d):
    return min(d, 4)


def _attn_fwd(q, k, v, group, n_samples, comm=None):
    d = DILATIONS[group]
    length = SEQ // d
    nb = length // ATT_BLOCK

    rps = _residues_per_step(d)

    def body(q_ref, k_ref, v_ref, o_ref, l_ref):
        for rl in range(rps):
            residue(q_ref, k_ref, v_ref, o_ref, l_ref, slice(rl * GROUP_W, (rl + 1) * GROUP_W))

    def residue(q_ref, k_ref, v_ref, o_ref, l_ref, cols):
        masks = _head_masks()

        def block(qs, ks, first):
            nk = ATT_BLOCK if first else 2 * ATT_BLOCK
            qb = q_ref[0, pl.ds(qs, ATT_BLOCK), cols]
            kc = k_ref[0, pl.ds(ks, nk), cols]
            vc = v_ref[0, pl.ds(ks, nk), cols]
            q4 = _stack_heads(qb, masks)
            valid = jnp.tile(_band_mask(first), (HEADS_PER_GROUP, 1))
            s = lax.dot_general(q4, kc, _NT, preferred_element_type=F32) * (HEAD_DIM ** -0.5)
            s = jnp.where(valid, s, NEG_INF)
            m = jnp.max(s, axis=-1, keepdims=True)
            p = jnp.exp(s - m)
            l = jnp.sum(p, axis=-1, keepdims=True)
            o4 = jnp.dot(p.astype(MXU_DTYPE), vc, preferred_element_type=F32) / l
            lse4 = jnp.broadcast_to(m + jnp.log(l), o4.shape)
            o_ref[0, pl.ds(qs, ATT_BLOCK), cols] = _unstack_heads(o4, masks)
            l_ref[0, pl.ds(qs, ATT_BLOCK), cols] = _unstack_heads(lse4, masks)

        block(0, 0, True)
        if nb > 1:
            def loop(n, carry):
                block(pl.multiple_of(n * ATT_BLOCK, ATT_BLOCK), pl.multiple_of((n - 1) * ATT_BLOCK, ATT_BLOCK), False)
                return carry

            lax.fori_loop(1, nb, loop, 0)

    per_sample = lambda a: a.reshape(n_samples, length, d * GROUP_W)
    spec = pl.BlockSpec((1, length, rps * GROUP_W), lambda b, r: (b, 0, r))
    shp = jax.ShapeDtypeStruct((n_samples, length, d * GROUP_W), F32)
    o, lse, *carried = _grid_call(body, f"attn_fwd_g{group}", (n_samples, d // rps), [per_sample(a) for a in (q, k, v)],
                                  [spec] * 3, [spec] * 2, [shp, shp], VMEM_MID, comm)
    flat = lambda a: a.reshape(n_samples * length, d * GROUP_W)
    return flat(o), flat(lse), carried


def _attn_bwd(q, k, v, dattn, lse_tot, rowdot, group, n_samples, comm=None):
    d = DILATIONS[group]
    length = SEQ // d
    nb = length // ATT_BLOCK

    rps = _residues_per_step(d)

    def body(q_ref, k_ref, v_ref, da_ref, lt_ref, rd_ref, dq_ref, dk_ref, dv_ref):
        dk_ref[...] = jnp.zeros_like(dk_ref)
        dv_ref[...] = jnp.zeros_like(dv_ref)
        for rl in range(rps):
            residue(q_ref, k_ref, v_ref, da_ref, lt_ref, rd_ref, dq_ref, dk_ref, dv_ref,
                    slice(rl * GROUP_W, (rl + 1) * GROUP_W))

    def residue(q_ref, k_ref, v_ref, da_ref, lt_ref, rd_ref, dq_ref, dk_ref, dv_ref, cols):
        masks = _head_masks()

        def block(qs, ks, first):
            nk = ATT_BLOCK if first else 2 * ATT_BLOCK
            qb = q_ref[0, pl.ds(qs, ATT_BLOCK), cols]
            kc = k_ref[0, pl.ds(ks, nk), cols]
            vc = v_ref[0, pl.ds(ks, nk), cols]
            da = da_ref[0, pl.ds(qs, ATT_BLOCK), cols]
            lt = lt_ref[0, pl.ds(qs, ATT_BLOCK), cols]
            rd = rd_ref[0, pl.ds(qs, ATT_BLOCK), cols]
            q4 = _stack_heads(qb, masks)
            da4 = _stack_heads(da, masks).astype(MXU_DTYPE)
            lt4 = jnp.max(_stack_heads(lt, masks, -jnp.inf), axis=-1, keepdims=True)
            rd4 = jnp.max(_stack_heads(rd, masks, -jnp.inf), axis=-1, keepdims=True)
            valid = jnp.tile(_band_mask(first), (HEADS_PER_GROUP, 1))
            s = lax.dot_general(q4, kc, _NT, preferred_element_type=F32) * (HEAD_DIM ** -0.5)
            s = jnp.where(valid, s, NEG_INF)
            p = jnp.exp(s - lt4)
            dp = lax.dot_general(da4, vc, _NT, preferred_element_type=F32)
            ds = (p * (dp - rd4) * (HEAD_DIM ** -0.5)).astype(MXU_DTYPE)
            dq_ref[0, pl.ds(qs, ATT_BLOCK), cols] = _unstack_heads(jnp.dot(ds, kc, preferred_element_type=F32), masks)
            dk_ref[0, pl.ds(ks, nk), cols] += lax.dot_general(ds, q4, _TN, preferred_element_type=F32)
            dv_ref[0, pl.ds(ks, nk), cols] += lax.dot_general(p.astype(MXU_DTYPE), da4, _TN, preferred_element_type=F32)

        block(0, 0, True)
        if nb > 1:
            def loop(n, carry):
                block(pl.multiple_of(n * ATT_BLOCK, ATT_BLOCK), pl.multiple_of((n - 1) * ATT_BLOCK, ATT_BLOCK), False)
                return carry

            lax.fori_loop(1, nb, loop, 0)

    per_sample = lambda a: a.reshape(n_samples, length, d * GROUP_W)
    spec = pl.BlockSpec((1, length, rps * GROUP_W), lambda b, r: (b, 0, r))
    shp = jax.ShapeDtypeStruct((n_samples, length, d * GROUP_W), F32)
    dq, dk, dv, *carried = _grid_call(
        body, f"attn_bwd_g{group}", (n_samples, d // rps), [per_sample(a) for a in (q, k, v, dattn, lse_tot, rowdot)],
        [spec] * 6, [spec] * 3, [shp, shp, shp], VMEM_MID, comm)
    flat = lambda a: a.reshape(n_samples * length, d * GROUP_W)
    return flat(dq), flat(dk), flat(dv), carried


def _disc(lr, li, ldt, br, bi):
    dt = jnp.exp(ldt)
    mag = jnp.exp(lr * dt)
    ab_re, ab_im = mag * jnp.cos(li * dt), mag * jnp.sin(li * dt)
    den = lr * lr + li * li
    nr, ni = ab_re - 1.0, ab_im
    f_re = (nr * lr + ni * li) / den
    f_im = (ni * lr - nr * li) / den
    return ab_re, ab_im, f_re * br - f_im * bi, f_re * bi + f_im * br


def _state_mask():
    row_g = lax.broadcasted_iota(jnp.int32, (SCAN_CH, SCAN_WC), 0) // SSM_CH
    col_g = lax.broadcasted_iota(jnp.int32, (SCAN_CH, SCAN_WC), 1) // SSM_STATE
    return row_g == col_g


def _ssm_disc(lr, li, ldt, br, bi, cr, ci):
    w = SCAN_WC

    def body(lr_ref, li_ref, ldt_ref, br_ref, bi_ref, cr_ref, ci_ref, a_ref, bb_ref, c_ref):
        ar, ai, bbr, bbi = _disc(lr_ref[...], li_ref[...], ldt_ref[...], br_ref[...], bi_ref[...])
        crv, civ = cr_ref[...], ci_ref[...]
        mask = _state_mask()
        for cb in range(SCAN_NBLK):
            sl = slice(cb * w, (cb + 1) * w)
            rows = slice(cb * SCAN_CH, (cb + 1) * SCAN_CH)
            dense = lambda comp: jnp.where(mask, jnp.tile(comp[:, sl], (SCAN_CH // SSM_CH, 1)), 0.0)
            a_ref[:, 2 * cb * w:(2 * cb + 1) * w] = ar[:, sl]
            a_ref[:, (2 * cb + 1) * w:(2 * cb + 2) * w] = ai[:, sl]
            bb_ref[rows, :w] = dense(bbr).astype(MXU_DTYPE)
            bb_ref[rows, w:] = dense(bbi).astype(MXU_DTYPE)
            c_ref[rows, :w] = dense(crv).astype(MXU_DTYPE)
            c_ref[rows, w:] = (-dense(civ)).astype(MXU_DTYPE)

    return _pallas_call(
        body, name="ssm_disc",
        out_shape=[jax.ShapeDtypeStruct((1, 2 * N_STATE), F32), jax.ShapeDtypeStruct((SSM_W, 2 * w), MXU_DTYPE),
                   jax.ShapeDtypeStruct((SSM_W, 2 * w), MXU_DTYPE)],
        compiler_params=pltpu.CompilerParams(vmem_limit_bytes=VMEM_MID),
    )(lr, li, ldt, br, bi, cr, ci)


def _group_indicator():
    s = jnp.arange(N_STATE) // SSM_STATE
    return (s[:, None] == jnp.arange(LANES)[None, :]).astype(F32)


def _ssm_param_bwd(lr, li, ldt, br, bi, da_cat, dbb_full, dc_full):
    w = SCAN_WC

    def body(lr_ref, li_ref, ldt_ref, br_ref, bi_ref, da_ref, dbb_ref, dc_ref, ind_ref,
             glr_ref, gli_ref, gldt_ref, gbr_ref, gbi_ref, gcr_ref, gci_ref):
        mask = _state_mask()

        def diag_parts(ref):
            res = ([], [])
            for cb in range(SCAN_NBLK):
                for part in range(2):
                    blk = ref[cb * SCAN_CH:(cb + 1) * SCAN_CH, part * w:(part + 1) * w]
                    res[part].append(jnp.sum(jnp.where(mask, blk, 0.0).reshape(SCAN_CH // SSM_CH, SSM_CH, w), axis=0))
            return jnp.concatenate(res[0], axis=1), jnp.concatenate(res[1], axis=1)

        dar = jnp.concatenate([da_ref[:, 2 * cb * w:(2 * cb + 1) * w] for cb in range(SCAN_NBLK)], axis=1)
        dai = jnp.concatenate([da_ref[:, (2 * cb + 1) * w:(2 * cb + 2) * w] for cb in range(SCAN_NBLK)], axis=1)
        dbbr, dbbi = diag_parts(dbb_ref)
        dcr, dci_neg = diag_parts(dc_ref)
        gcr_ref[...] = dcr
        gci_ref[...] = -dci_neg
        _, vjp = jax.vjp(_disc, lr_ref[...], li_ref[...], ldt_ref[...], br_ref[...], bi_ref[...])
        glr, gli, gldt, gbr, gbi = vjp((dar, dai, dbbr, dbbi))
        glr_ref[...] = glr
        gli_ref[...] = gli
        gldt_ref[...] = jnp.dot(jnp.broadcast_to(gldt, (8, N_STATE)), ind_ref[...], preferred_element_type=F32,
                                precision=lax.Precision.HIGHEST)
        gbr_ref[...] = gbr
        gbi_ref[...] = gbi

    v1 = jax.ShapeDtypeStruct((1, N_STATE), F32)
    v16 = jax.ShapeDtypeStruct((SSM_CH, N_STATE), F32)
    vdt = jax.ShapeDtypeStruct((8, LANES), F32)
    return _pallas_call(
        body, name="ssm_param_bwd", out_shape=[v1, v1, vdt, v16, v16, v16, v16],
        compiler_params=pltpu.CompilerParams(vmem_limit_bytes=VMEM_BIG),
    )(lr, li, ldt, br, bi, da_cat, dbb_full, dc_full, _group_indicator())


def _cmul(ar, ai, br, bi):
    return ar * br - ai * bi, ar * bi + ai * br


def _gelu_tanh(y):
    return jnp.tanh(_GELU_C * (y + 0.044715 * (y * y * y)))


def _segment_carry(er, ei, ar, ai, n_rows, reverse):
    qr, qi = ar, ai
    for _ in range(int(math.log2(SCAN_LEN))):
        qr, qi = _cmul(qr, qi, qr, qi)
    seg = lax.broadcasted_iota(jnp.int32, er.shape, 0) % SCAN_SEG_PER_SAMPLE
    shift = 1
    while shift < SCAN_SEG_PER_SAMPLE:
        keep = (seg < SCAN_SEG_PER_SAMPLE - shift) if reverse else (seg >= shift)
        amount = n_rows - shift if reverse else shift
        sr = jnp.where(keep, pltpu.roll(er, amount, 0), 0.0)
        si = jnp.where(keep, pltpu.roll(ei, amount, 0), 0.0)
        if reverse:
            er, ei = er + qr * sr + qi * si, ei + qr * si - qi * sr
        else:
            er, ei = er + qr * sr - qi * si, ei + qr * si + qi * sr
        qr, qi = _cmul(qr, qi, qr, qi)
        shift *= 2
    keep = (seg < SCAN_SEG_PER_SAMPLE - 1) if reverse else (seg >= 1)
    amount = n_rows - 1 if reverse else 1
    return jnp.where(keep, pltpu.roll(er, amount, 0), 0.0), jnp.where(keep, pltpu.roll(ei, amount, 0), 0.0)


def _ssm_fwd(u_perm, a_cat, bbc, cc, dskip, n_rows):
    t = u_perm.shape[0]
    w = SCAN_WC
    rows_c = SCAN_CHUNK * n_rows
    n_chunks = t // rows_c

    assert n_chunks % 2 == 0

    def body(u_ref, a_ref, bb_ref, c_ref, d_ref, yt_ref, yg_ref, ein_ref, bu_all, st_a, st_b, xs_a, xs_b):
        ar = jnp.broadcast_to(a_ref[:, :w], (n_rows, w))
        ai = jnp.broadcast_to(a_ref[:, w:], (n_rows, w))
        start = lambda ch: pl.multiple_of(ch * rows_c, rows_c)

        def project(ch, stage):
            res = jnp.dot(u_ref[pl.ds(start(ch), rows_c), :].astype(MXU_DTYPE), bb_ref[...], preferred_element_type=F32)
            stage[...] = res
            bu_all[pl.ds(start(ch), rows_c), :] = res

        def steps(src, r0, carry, xs=None):
            for i in range(SCAN_CHUNK):
                blk = src[pl.ds(r0 + i * n_rows, n_rows), :]
                carry = (ar * carry[0] - ai * carry[1] + blk[:, :w], ar * carry[1] + ai * carry[0] + blk[:, w:])
                if xs is not None:
                    xs[i * n_rows:(i + 1) * n_rows, :w] = carry[0]
                    xs[i * n_rows:(i + 1) * n_rows, w:] = carry[1]
            return carry

        def emit(xs, ch):
            y = lax.dot_general(xs[...].astype(MXU_DTYPE), c_ref[...], _NT, preferred_element_type=F32)
            yt = y + d_ref[...] * u_ref[pl.ds(start(ch), rows_c), :]
            yt_ref[pl.ds(start(ch), rows_c), :] = yt
            yg_ref[pl.ds(start(ch), rows_c), :] = (0.5 * yt * (1.0 + _gelu_tanh(yt))).astype(BF16)

        project(0, st_a)

        def pair1(p, carry):
            project(2 * p + 1, st_b)
            carry = steps(st_a, 0, carry)
            project(jnp.minimum(2 * p + 2, n_chunks - 1), st_a)
            return steps(st_b, 0, carry)

        zero = jnp.zeros((n_rows, w), F32)
        er, ei = lax.fori_loop(0, n_chunks // 2, pair1, (zero, zero))
        cr, ci = _segment_carry(er, ei, ar, ai, n_rows, False)
        ein_ref[:, :w] = cr
        ein_ref[:, w:] = ci

        xs_b[...] = jnp.zeros_like(xs_b)

        def pair2(p, carry):
            emit(xs_b, jnp.maximum(2 * p - 1, 0))
            carry = steps(bu_all, start(2 * p), carry, xs_a)
            emit(xs_a, 2 * p)
            return steps(bu_all, start(2 * p + 1), carry, xs_b)

        lax.fori_loop(0, n_chunks // 2, pair2, (cr, ci))
        emit(xs_b, n_chunks - 1)

    col = lambda width: pl.BlockSpec((t, width), lambda c: (0, c))
    wgt = pl.BlockSpec((SCAN_CH, 2 * w), lambda c: (c, 0))
    return _pallas_call(
        body, name="ssm_fwd", grid=(SCAN_NBLK,),
        in_specs=[col(SCAN_CH), pl.BlockSpec((1, 2 * w), lambda c: (0, c)), wgt, wgt,
                  pl.BlockSpec((1, SCAN_CH), lambda c: (0, c))],
        out_specs=[col(SCAN_CH), col(SCAN_CH), pl.BlockSpec((n_rows, 2 * w), lambda c: (0, c))],
        out_shape=[jax.ShapeDtypeStruct((t, SSM_W), F32), jax.ShapeDtypeStruct((t, SSM_W), BF16),
                   jax.ShapeDtypeStruct((n_rows, 2 * N_STATE), F32)],
        scratch_shapes=[pltpu.VMEM((t, 2 * w), F32)] + [pltpu.VMEM((rows_c, 2 * w), F32)] * 4,
        compiler_params=pltpu.CompilerParams(dimension_semantics=("parallel",), vmem_limit_bytes=VMEM_BIG),
    )(u_perm, a_cat, bbc, cc, dskip)


def _ssm_bwd(u_perm, dyg, ytot, dskip, a_cat, bbc, cc, ein, n_rows, comm=None):
    t = u_perm.shape[0]
    w = SCAN_WC
    rows_c = SCAN_CHUNK * n_rows
    n_chunks = t // rows_c

    assert n_chunks % 2 == 0
    last = n_chunks - 1

    def body(u_ref, dyg_ref, yt_ref, dk_ref, a_ref, bb_ref, c_ref, ein_ref, du_ref, gd_ref, da_ref, dbb_ref, dc_ref,
             xs_all, dy_s, st_a, st_b, buf_a, buf_b):
        ar = jnp.broadcast_to(a_ref[:, :w], (n_rows, w))
        ai = jnp.broadcast_to(a_ref[:, w:], (n_rows, w))
        zero = jnp.zeros((n_rows, w), F32)
        start = lambda ch: pl.multiple_of(ch * rows_c, rows_c)
        dbb_ref[...] = jnp.zeros_like(dbb_ref)
        dc_ref[...] = jnp.zeros_like(dc_ref)
        da_ref[...] = jnp.zeros_like(da_ref)

        yt = yt_ref[...]
        th = _gelu_tanh(yt)
        dgelu = 0.5 * (1.0 + th) + 0.5 * yt * (1.0 - th * th) * _GELU_C * (1.0 + 3.0 * 0.044715 * yt * yt)
        dy_all = dyg_ref[...] * dgelu
        dy_s[...] = dy_all
        gd_ref[...] = jnp.sum(dy_all * u_ref[...], axis=0, keepdims=True)
        dy_chunk = lambda ch: dy_s[pl.ds(start(ch), rows_c), :].astype(MXU_DTYPE)

        xs_all[0:n_rows, :] = ein_ref[...]

        def project(ch, stage):
            stage[...] = jnp.dot(u_ref[pl.ds(start(ch), rows_c), :].astype(MXU_DTYPE), bb_ref[...],
                                 preferred_element_type=F32)

        def fwd_steps(stage, ch, carry, xs):
            for i in range(SCAN_CHUNK):
                blk = stage[i * n_rows:(i + 1) * n_rows, :]
                carry = (ar * carry[0] - ai * carry[1] + blk[:, :w], ar * carry[1] + ai * carry[0] + blk[:, w:])
                for half, val in enumerate(carry):
                    xs[i * n_rows:(i + 1) * n_rows, half * w:(half + 1) * w] = val
                    xs_all[pl.ds(start(ch) + (i + 1) * n_rows, n_rows), half * w:(half + 1) * w] = val
            return carry

        def add_dc(xs, ch):
            dc_ref[...] += lax.dot_general(dy_chunk(ch), xs[...].astype(MXU_DTYPE), _TN, preferred_element_type=F32)

        project(0, st_a)

        def fwd_pair(p, carry):
            project(2 * p + 1, st_b)
            carry = fwd_steps(st_a, 2 * p, carry, buf_a)
            add_dc(buf_a, 2 * p)
            project(jnp.minimum(2 * p + 2, last), st_a)
            carry = fwd_steps(st_b, 2 * p + 1, carry, buf_b)
            add_dc(buf_b, 2 * p + 1)
            return carry

        lax.fori_loop(0, n_chunks // 2, fwd_pair, (ein_ref[:, :w], ein_ref[:, w:]))

        def project_dx(ch, stage):
            stage[...] = jnp.dot(dy_chunk(ch), c_ref[...], preferred_element_type=F32)

        def back_steps(stage, carry, g_buf=None):
            for i in reversed(range(SCAN_CHUNK)):
                blk = stage[i * n_rows:(i + 1) * n_rows, :]
                carry = (blk[:, :w] + ar * carry[0] + ai * carry[1], blk[:, w:] + ar * carry[1] - ai * carry[0])
                if g_buf is not None:
                    g_buf[i * n_rows:(i + 1) * n_rows, :w] = carry[0]
                    g_buf[i * n_rows:(i + 1) * n_rows, w:] = carry[1]
            return carry

        def first_pair(p, carry):
            project_dx(last - 2 * p - 1, st_b)
            carry = back_steps(st_a, carry)
            project_dx(jnp.maximum(last - 2 * p - 2, 0), st_a)
            return back_steps(st_b, carry)

        project_dx(last, st_a)
        sr, si = lax.fori_loop(0, n_chunks // 2, first_pair, (zero, zero))
        gr0, gi0 = _segment_carry(sr, si, ar, ai, n_rows, True)

        def post(g_buf, ch):
            g = g_buf[...]
            xp = xs_all[pl.ds(start(ch), rows_c), :]
            da_ref[:, :w] += jnp.sum(g[:, :w] * xp[:, :w] + g[:, w:] * xp[:, w:], axis=0, keepdims=True)
            da_ref[:, w:] += jnp.sum(g[:, w:] * xp[:, :w] - g[:, :w] * xp[:, w:], axis=0, keepdims=True)
            gb = g.astype(MXU_DTYPE)
            du_ref[pl.ds(start(ch), rows_c), :] = (lax.dot_general(gb, bb_ref[...], _NT, preferred_element_type=F32)
                                                   + dy_s[pl.ds(start(ch), rows_c), :] * dk_ref[...])
            dbb_ref[...] += lax.dot_general(u_ref[pl.ds(start(ch), rows_c), :].astype(MXU_DTYPE), gb, _TN,
                                            preferred_element_type=F32)

        def second_pair(p, carry):
            c1 = last - 2 * p
            project_dx(c1 - 1, st_b)
            post(buf_b, jnp.minimum(c1 + 1, last))
            carry = back_steps(st_a, carry, buf_a)
            project_dx(jnp.maximum(c1 - 2, 0), st_a)
            post(buf_a, c1)
            return back_steps(st_b, carry, buf_b)

        project_dx(last, st_a)
        buf_b[...] = jnp.zeros_like(buf_b)
        lax.fori_loop(0, n_chunks // 2, second_pair, (gr0, gi0))
        post(buf_b, 0)

    col = lambda width: pl.BlockSpec((t, width), lambda c, j: (0, c))
    wgt = pl.BlockSpec((SCAN_CH, 2 * w), lambda c, j: (c, 0))
    row = pl.BlockSpec((1, 2 * w), lambda c, j: (0, c))
    chan = pl.BlockSpec((1, SCAN_CH), lambda c, j: (0, c))
    return _grid_call(
        body, "ssm_bwd", (SCAN_NBLK, 1), [u_perm, dyg, ytot, dskip, a_cat, bbc, cc, ein],
        [col(SCAN_CH), col(SCAN_CH), col(SCAN_CH), chan, row, wgt, wgt,
         pl.BlockSpec((n_rows, 2 * w), lambda c, j: (0, c))],
        [col(SCAN_CH), chan, row, wgt, wgt],
        [jax.ShapeDtypeStruct((t, SSM_W), F32), jax.ShapeDtypeStruct((1, SSM_W), F32),
         jax.ShapeDtypeStruct((1, 2 * N_STATE), F32), jax.ShapeDtypeStruct((SSM_W, 2 * w), F32),
         jax.ShapeDtypeStruct((SSM_W, 2 * w), F32)],
        56 * 1024 * 1024, comm,
        scratch=[pltpu.VMEM((t + n_rows, 2 * w), F32), pltpu.VMEM((t, SCAN_CH), F32)]
        + [pltpu.VMEM((rows_c, 2 * w), F32)] * 4)


def _to_scan_rows(a, n_samples):
    c = a.shape[1]
    return a.reshape(n_samples, SCAN_SEG_PER_SAMPLE, SCAN_LEN, c).transpose(2, 0, 1, 3).reshape(-1, c)


def _from_scan_rows(a, n_samples):
    c = a.shape[1]
    return a.reshape(SCAN_LEN, n_samples, SCAN_SEG_PER_SAMPLE, c).transpose(1, 2, 0, 3).reshape(-1, c)


def _row_spec(tm, width):
    return pl.BlockSpec((tm, width), lambda i, j: (i, 0))


def _whole(arr):
    return pl.BlockSpec(arr.shape, lambda i, j: (0,) * arr.ndim)


def _proj_rope(x, g, w_in_t, tabs, comm=None):
    t = x.shape[0]
    tm = 256

    def body(x_ref, g_ref, w_ref, tc_ref, tlo_ref, thi_ref, h_ref, u_ref, gate_ref, *rest):
        qkv_refs, stage = rest[:9], rest[9]
        xv = x_ref[...]
        r = lax.rsqrt(jnp.mean(xv * xv, axis=-1, keepdims=True) + RMS_EPS)
        h = ((xv * r) * g_ref[...]).astype(BF16)
        h_ref[...] = h
        p = lax.dot_general(h.astype(MXU_DTYPE), w_ref[...], _NT, preferred_element_type=F32)
        u_ref[...] = p[:, QKV_W:QKV_W + SSM_W]
        gate_ref[...] = _sigmoid(p[:, QKV_W + SSM_W:])
        tc, tlo, thi = tc_ref[...], tlo_ref[...], thi_ref[...]
        n_ch = QKV_W // LANES
        for ch in range(n_ch):
            piece = p[:, _lane_chunk(ch)]
            stage[ch] = _rope_apply(piece, tc, tlo, thi) if ch < 2 * n_ch // 3 else piece
        halves = GROUP_W // LANES
        for grp, d in enumerate(DILATIONS):
            for which in range(3):
                out = qkv_refs[3 * grp + which]
                for res in range(d):
                    for half in range(halves):
                        ch = which * (n_ch // 3) + grp * halves + half
                        out[:, _lane_chunk(res * halves + half)] = _gather_residue(stage, ch, res, d, tm // d).astype(BF16)

    tab = pl.BlockSpec((tm, LANES), lambda i, j: (i % (SEQ // tm), 0))
    widths = [(D_MODEL, BF16), (SSM_W, F32), (2 * D_MODEL, F32)]
    out_specs = [_row_spec(tm, wd) for wd, _ in widths]
    out_shapes = [jax.ShapeDtypeStruct((t, wd), dt) for wd, dt in widths]
    for d in DILATIONS:
        out_specs += [_row_spec(tm // d, d * GROUP_W)] * 3
        out_shapes += [jax.ShapeDtypeStruct((t // d, d * GROUP_W), BF16)] * 3
    return _grid_call(
        body, "proj_rope", (t // tm, 1), [x, g, w_in_t, *tabs],
        [_row_spec(tm, D_MODEL), _whole(g), _whole(w_in_t), tab, tab, tab], out_specs, out_shapes, VMEM_BIG, comm,
        scratch=[pltpu.VMEM((QKV_W // LANES, tm, LANES), F32)])


def _branch_outputs(attn_ref, yg_ref, wao_ref, wglu_ref):
    attn_d = lax.dot_general(attn_ref[...].astype(MXU_DTYPE), wao_ref[...], _NT, preferred_element_type=F32)
    z = lax.dot_general(yg_ref[...].astype(MXU_DTYPE), wglu_ref[...], _NT, preferred_element_type=F32)
    return attn_d, z[:, :D_MODEL], _sigmoid(z[:, D_MODEL:])


def _mix_out_rms(os_, lses, yg, gates, x, w_ao_t, w_glu_t, w_out, g, comm=None):
    t = x.shape[0]
    tm = 256

    def body(o0, o1, o2, l0, l1, l2, yg_ref, gate_ref, x_ref, wao_ref, wglu_ref, wout_ref, g_ref,
             attn_ref, lt_ref, m_ref, x1_ref, h_ref, nat):
        _merge_groups((o0, o1, o2), (l0, l1, l2), attn_ref, lt_ref, nat, tm)
        attn_d, za, sb = _branch_outputs(attn_ref, yg_ref, wao_ref, wglu_ref)
        merged = (gate_ref[:, :D_MODEL] * attn_d + gate_ref[:, D_MODEL:] * (za * sb)).astype(BF16)
        m_ref[...] = merged
        x1 = x_ref[...] + jnp.dot(merged.astype(MXU_DTYPE), wout_ref[...], preferred_element_type=F32)
        x1_ref[...] = x1
        r = lax.rsqrt(jnp.mean(x1 * x1, axis=-1, keepdims=True) + RMS_EPS)
        h_ref[...] = ((x1 * r) * g_ref[...]).astype(BF16)

    dil_specs = [_row_spec(tm // d, d * GROUP_W) for d in DILATIONS] * 2
    return _grid_call(
        body, "mix_out_rms", (t // tm, 1), [*os_, *lses, yg, gates, x, w_ao_t, w_glu_t, w_out, g],
        dil_specs + [_row_spec(tm, SSM_W), _row_spec(tm, 2 * D_MODEL), _row_spec(tm, D_MODEL),
                     _whole(w_ao_t), _whole(w_glu_t), _whole(w_out), _whole(g)],
        [_row_spec(tm, GROUP_W)] * 2 + [_row_spec(tm, D_MODEL)] * 3,
        [jax.ShapeDtypeStruct((t, GROUP_W), F32)] * 2
        + [jax.ShapeDtypeStruct((t, D_MODEL), BF16), jax.ShapeDtypeStruct((t, D_MODEL), F32),
           jax.ShapeDtypeStruct((t, D_MODEL), BF16)], VMEM_BIG, comm, scratch=[pltpu.VMEM((8, tm, LANES), F32)])


def _mix_bwd(dx1b, attn, lse_tot, yg, gates, w_ao_t, w_glu_t, w_out, comm=None):
    t = dx1b.shape[0]
    tm = 256

    def body(dx_ref, attn_ref, lt_ref, yg_ref, gate_ref, wao_ref, wglu_ref, wout_ref, ones_ref,
             dad_ref, dz_ref, dg_ref, da_ref, dyg_ref, rd_ref, *rest):
        dm = lax.dot_general(dx_ref[...], wout_ref[...], _NT, preferred_element_type=F32)
        attn_d, za, sb = _branch_outputs(attn_ref, yg_ref, wao_ref, wglu_ref)
        g0, g1 = gate_ref[:, :D_MODEL], gate_ref[:, D_MODEL:]
        dad = (dm * g0).astype(BF16)
        dad_ref[...] = dad
        ds = dm * g1
        dza, dzb = (ds * sb).astype(BF16), (ds * za * sb * (1.0 - sb)).astype(BF16)
        dz_ref[:, :D_MODEL] = dza
        dz_ref[:, D_MODEL:] = dzb
        dg_ref[:, :D_MODEL] = (dm * attn_d * g0 * (1.0 - g0)).astype(BF16)
        dg_ref[:, D_MODEL:] = (dm * (za * sb) * g1 * (1.0 - g1)).astype(BF16)
        da = jnp.dot(dad.astype(MXU_DTYPE), wao_ref[...], preferred_element_type=F32)
        da_ref[...] = da
        dyg_ref[...] = (jnp.dot(dza.astype(MXU_DTYPE), wglu_ref[:D_MODEL, :], preferred_element_type=F32)
                        + jnp.dot(dzb.astype(MXU_DTYPE), wglu_ref[D_MODEL:, :], preferred_element_type=F32))
        _attention_cotangents(da, attn_ref[...], lt_ref[...], ones_ref[...], rd_ref, rest[:6], rest[6], tm)

    widths = [(D_MODEL, BF16), (2 * D_MODEL, BF16), (2 * D_MODEL, BF16), (GROUP_W, F32), (SSM_W, F32), (GROUP_W, F32)]
    out_specs = [_row_spec(tm, wd) for wd, _ in widths]
    out_shapes = [jax.ShapeDtypeStruct((t, wd), dt) for wd, dt in widths]
    for d in DILATIONS[1:]:
        out_specs += [_row_spec(tm // d, d * GROUP_W)] * 3
        out_shapes += [jax.ShapeDtypeStruct((t // d, d * GROUP_W), F32)] * 3
    ones = _head_sum_matrix()
    return _grid_call(
        body, "mix_bwd", (t // tm, 1), [dx1b, attn, lse_tot, yg, gates, w_ao_t, w_glu_t, w_out, ones],
        [_row_spec(tm, D_MODEL), _row_spec(tm, GROUP_W), _row_spec(tm, GROUP_W), _row_spec(tm, SSM_W),
         _row_spec(tm, 2 * D_MODEL), _whole(w_ao_t), _whole(w_glu_t), _whole(w_out), _whole(ones)],
        out_specs, out_shapes, VMEM_BIG, comm, scratch=[pltpu.VMEM((6, tm, LANES), F32)])


FFN_TN = D_FF // 2
MXU_COLS = 256


def _ffn_in_swiglu(h2, w_gate_t, w_up_t, comm=None):
    t = h2.shape[0]
    tm = 512

    def body(h_ref, wg_ref, wu_ref, a_ref, b_ref, f_ref):
        h = h_ref[...].astype(MXU_DTYPE)
        for c0 in range(0, FFN_TN, MXU_COLS):
            sl = slice(c0, min(c0 + MXU_COLS, FFN_TN))
            a = lax.dot_general(h, wg_ref[sl, :], _NT, preferred_element_type=F32)
            b = lax.dot_general(h, wu_ref[sl, :], _NT, preferred_element_type=F32)
            a_ref[:, sl] = a
            b_ref[:, sl] = b
            f_ref[:, sl] = (a * _sigmoid(a) * b).astype(BF16)

    tile = pl.BlockSpec((tm, FFN_TN), lambda j, i: (i, j))
    wspec = pl.BlockSpec((FFN_TN, D_MODEL), lambda j, i: (j, 0))
    return _grid_call(
        body, "ffn_in_swiglu", (D_FF // FFN_TN, t // tm), [h2, w_gate_t, w_up_t],
        [pl.BlockSpec((tm, D_MODEL), lambda j, i: (i, 0)), wspec, wspec],
        [tile] * 3, [jax.ShapeDtypeStruct((t, D_FF), F32)] * 2 + [jax.ShapeDtypeStruct((t, D_FF), BF16)], VMEM_BIG, comm)


def _ffn_down_final(f, w_down, x1, target, g):
    t = x1.shape[0]
    tm = 256

    def body(f_ref, w_ref, x1_ref, t_ref, g_ref, dx_ref, dxb_ref, loss_ref, gg_ref):
        @pl.when(pl.program_id(0) == 0)
        def _():
            loss_ref[...] = jnp.zeros_like(loss_ref)
            gg_ref[...] = jnp.zeros_like(gg_ref)

        xv = x1_ref[...] + jnp.dot(f_ref[...].astype(MXU_DTYPE), w_ref[...], preferred_element_type=F32)
        gv = g_ref[...]
        r = lax.rsqrt(jnp.mean(xv * xv, axis=-1, keepdims=True) + RMS_EPS)
        n = xv * r
        diff = n * gv - t_ref[...]
        per_tok = jnp.mean(diff * diff, axis=-1, keepdims=True)
        loss_ref[...] += 0.5 * jnp.sum(per_tok, axis=0, keepdims=True)
        dy = diff / xv.shape[-1]
        gg_ref[...] += jnp.sum(dy * n, axis=0, keepdims=True)
        dn = dy * gv
        dx = r * (dn - n * jnp.mean(dn * n, axis=-1, keepdims=True))
        dx_ref[...] = dx
        dxb_ref[...] = dx.astype(BF16)

    acc = lambda shp: pl.BlockSpec(shp, lambda i, j: (0, 0))
    return _grid_call(
        body, "ffn_down_final", (t // tm, 1), [f, w_down, x1, target, g],
        [_row_spec(tm, D_FF), _whole(w_down), _row_spec(tm, D_MODEL), _row_spec(tm, D_MODEL), _whole(g)],
        [_row_spec(tm, D_MODEL)] * 2 + [acc((8, LANES)), acc((1, D_MODEL))],
        [jax.ShapeDtypeStruct((t, D_MODEL), F32), jax.ShapeDtypeStruct((t, D_MODEL), BF16),
         jax.ShapeDtypeStruct((8, LANES), F32), jax.ShapeDtypeStruct((1, D_MODEL), F32)], VMEM_BIG, sequential=True)


def _d_f_swiglu_bwd(dx2b, w_down, a, b):
    t = a.shape[0]
    tm = 512

    def body(dx_ref, w_ref, a_ref, b_ref, da_ref, db_ref):
        d = lax.dot_general(dx_ref[...], w_ref[...], _NT, preferred_element_type=F32)
        av, bv = a_ref[...], b_ref[...]
        sg = _sigmoid(av)
        da_ref[...] = (d * bv * sg * (1.0 + av * (1.0 - sg))).astype(BF16)
        db_ref[...] = (d * av * sg).astype(BF16)

    tile = pl.BlockSpec((tm, FFN_TN), lambda j, i: (i, j))
    return _grid_call(
        body, "d_f_swiglu_bwd", (D_FF // FFN_TN, t // tm), [dx2b, w_down, a, b],
        [pl.BlockSpec((tm, D_MODEL), lambda j, i: (i, 0)), pl.BlockSpec((FFN_TN, D_MODEL), lambda j, i: (j, 0)), tile, tile],
        [tile] * 2, [jax.ShapeDtypeStruct((t, D_FF), BF16)] * 2, VMEM_BIG)


def _mm_rms_bwd(operands, weights, x, g, dres, name, comm=None):
    t = x.shape[0]
    tm = 256
    n_op = len(operands)

    def body(*refs):
        a_refs, w_refs = refs[:n_op], refs[n_op:2 * n_op]
        x_ref, g_ref, dres_ref, dx_ref, dxb_ref, gg_ref = refs[2 * n_op:]

        @pl.when(pl.program_id(0) == 0)
        def _():
            gg_ref[...] = jnp.zeros_like(gg_ref)

        dh = None
        for a_ref, w_ref in zip(a_refs, w_refs):
            part = jnp.dot(a_ref[...].astype(MXU_DTYPE), w_ref[...], preferred_element_type=F32)
            dh = part if dh is None else dh + part
        xv = x_ref[...]
        r = lax.rsqrt(jnp.mean(xv * xv, axis=-1, keepdims=True) + RMS_EPS)
        n = xv * r
        gg_ref[...] += jnp.sum(dh * n, axis=0, keepdims=True)
        dn = dh * g_ref[...]
        dx = dres_ref[...] + r * (dn - n * jnp.mean(dn * n, axis=-1, keepdims=True))
        dx_ref[...] = dx
        dxb_ref[...] = dx.astype(BF16)

    d = x.shape[1]
    return _grid_call(
        body, name, (t // tm, 1), [*operands, *weights, x, g, dres],
        [_row_spec(tm, a.shape[1]) for a in operands] + [_whole(wk) for wk in weights]
        + [_row_spec(tm, d), _whole(g), _row_spec(tm, d)],
        [_row_spec(tm, d)] * 2 + [pl.BlockSpec((1, d), lambda i, j: (0, 0))],
        [jax.ShapeDtypeStruct((t, d), F32), jax.ShapeDtypeStruct((t, d), BF16), jax.ShapeDtypeStruct((1, d), F32)],
        VMEM_BIG, comm, sequential=True)


def _flat_small(small):
    perm_b = lambda a: a.reshape(SSM_GROUPS, SSM_STATE, SSM_CH).transpose(2, 0, 1).reshape(SSM_CH, N_STATE)
    perm_c = lambda a: a.reshape(SSM_GROUPS, SSM_CH, SSM_STATE).transpose(1, 0, 2).reshape(SSM_CH, N_STATE)
    return dict(
        g_mix=small["norm_mix_g"].reshape(1, D_MODEL), g_ffn=small["norm_ffn_g"].reshape(1, D_MODEL),
        g_fin=small["norm_final_g"].reshape(1, D_MODEL),
        lr=small["ssm_a_re"].reshape(1, N_STATE), li=small["ssm_a_im"].reshape(1, N_STATE),
        ldt=jnp.repeat(small["ssm_log_dt"].reshape(SSM_GROUPS), SSM_STATE).reshape(1, N_STATE),
        br=perm_b(small["ssm_b_re"]), bi=perm_b(small["ssm_b_im"]),
        cr=perm_c(small["ssm_c_re"]), ci=perm_c(small["ssm_c_im"]), dskip=small["ssm_d"].reshape(1, SSM_W))


AG_HOSTS = {"proj_rope": ("w_glu", "w_attn_out", "w_out", "w_ffn_gate"), "mix_out_rms": ("w_ffn_up",),
            "ffn_in_swiglu": ("w_ffn_down",)}
HALVED = ("w_ffn_gate", "w_ffn_up", "w_in")
A2A_HOSTS = {"d_h2_rms": ("w_ffn_down",), "mix_bwd": ("w_ffn_gate:0", "w_out"), "attn_bwd_g1": ("w_glu",),
             "attn_bwd_g2": ("w_attn_out",), "ssm_bwd": ("w_ffn_gate:1", "w_ffn_up:0", "w_ffn_up:1"),
             "mm_g_in1": ("w_in:0",), "d_h0_rms": ("w_in:1",)}
SMALL_HOST = "mm_g_in0"


def _local_step(x, target, w, small, shards=None):
    t = x.shape[0]
    n_samples = t // SEQ
    n_rows = n_samples * SCAN_SEG_PER_SAMPLE
    tabs = _rope_tables()
    w = dict(w)
    fs = _flat_small(small)
    g_mix, g_ffn, g_fin, dskip = fs["g_mix"], fs["g_ffn"], fs["g_fin"], fs["dskip"]
    a_cat, bbc, cc = _ssm_disc(fs["lr"], fs["li"], fs["ldt"], fs["br"], fs["bi"], fs["cr"], fs["ci"])
    big, recv, small_pack = {}, {}, []

    def comm_of(name):
        if shards is None:
            return None
        if name == SMALL_HOST:
            return _ag_comm([(small_pack[0], 0, 0)], [(N_DEV, *small_pack[0].shape)])
        if name in AG_HOSTS:
            names = AG_HOSTS[name]
            return _ag_comm([(shards[n], j, 0) for j, n in enumerate(names)], [(N_DEV, *shards[n].shape) for n in names])
        if name in A2A_HOSTS:
            return _a2a_comm([(big[n].reshape(N_DEV, -1, big[n].shape[1]), 0) for n in A2A_HOSTS[name]])
        return None

    def absorb(name, carried):
        if name == SMALL_HOST:
            recv["small"] = carried[0]
        for n, a3 in zip(AG_HOSTS.get(name, ()), carried):
            w[n] = a3.reshape(-1, a3.shape[2])
        for n, a3 in zip(A2A_HOSTS.get(name, ()), carried):
            recv[n] = a3

    def mm(a, b, mode, name, tm, tn, **kw):
        comm = comm_of(name)
        if comm is None:
            return _mm(a, b, mode, name, tm, tn, **kw)
        out, *carried = _mm(a, b, mode, name, tm, tn, comm=comm, **kw)
        absorb(name, carried)
        return out

    h0, u, gates, *rest = _proj_rope(x, g_mix, w["w_in"], tabs, comm_of("proj_rope"))
    qkv = [rest[3 * g:3 * g + 3] for g in range(3)]
    absorb("proj_rope", rest[9:])
    os_, lses = [], []
    for g in range(3):
        o_g, l_g, carried = _attn_fwd(*qkv[g], g, n_samples, comm_of(f"attn_fwd_g{g}"))
        absorb(f"attn_fwd_g{g}", carried)
        os_.append(o_g)
        lses.append(l_g)
    u_perm = _to_scan_rows(u, n_samples)
    ytot, yg_perm, ein = _ssm_fwd(u_perm, a_cat, bbc, cc, dskip, n_rows)
    yg = _from_scan_rows(yg_perm, n_samples)

    attn, lse_tot, merged, x1, h2, *carried = _mix_out_rms(os_, lses, yg, gates, x, w["w_attn_out"], w["w_glu"], w["w_out"],
                                                           g_ffn, comm_of("mix_out_rms"))
    absorb("mix_out_rms", carried)
    ffn_a, ffn_b, f, *carried = _ffn_in_swiglu(h2, w["w_ffn_gate"], w["w_ffn_up"], comm_of("ffn_in_swiglu"))
    absorb("ffn_in_swiglu", carried)
    dx2, dx2b, loss_blk, g_gfin = _ffn_down_final(f, w["w_ffn_down"], x1, target, g_fin)

    da, db = _d_f_swiglu_bwd(dx2b, w["w_ffn_down"], ffn_a, ffn_b)
    big["w_ffn_down"] = mm(f, dx2b, "tn", "mm_g_down", 256, D_MODEL, out_dtype=BF16)
    half = D_MODEL // 2
    for hf in range(2):
        big[f"w_ffn_gate:{hf}"] = mm(da, h2, "tn", f"mm_g_gate{hf}", 256, half, out_dtype=BF16, cols=(hf * half, half))
        big[f"w_ffn_up:{hf}"] = mm(db, h2, "tn", f"mm_g_up{hf}", 256, half, out_dtype=BF16, cols=(hf * half, half))
    dx1, dx1b, g_gffn, *carried = _mm_rms_bwd([da, db], [w["w_ffn_gate"], w["w_ffn_up"]], x1, g_ffn, dx2, "d_h2_rms",
                                              comm_of("d_h2_rms"))
    absorb("d_h2_rms", carried)

    big["w_out"] = mm(merged, dx1b, "tn", "mm_g_out", 256, D_MODEL, out_dtype=BF16)
    dattn_d, dz, dgpre, dattn, dyg, rowdot, *rest = _mix_bwd(dx1b, attn, lse_tot, yg, gates, w["w_attn_out"], w["w_glu"],
                                                             w["w_out"], comm_of("mix_bwd"))
    cot = [(dattn, lse_tot, rowdot), tuple(rest[:3]), tuple(rest[3:6])]
    absorb("mix_bwd", rest[6:])

    big["w_attn_out"] = mm(dattn_d, attn, "tn", "mm_g_attn_out", 512, GROUP_W, out_dtype=BF16)
    big["w_glu"] = mm(dz, yg, "tn", "mm_g_glu", 512, 512, out_dtype=BF16)
    dqs, dks, dvs = [], [], []
    for g in range(3):
        dq_g, dk_g, dv_g, carried = _attn_bwd(*qkv[g], *cot[g], g, n_samples, comm_of(f"attn_bwd_g{g}"))
        absorb(f"attn_bwd_g{g}", carried)
        dqs.append(dq_g)
        dks.append(dk_g)
        dvs.append(dv_g)

    dyg_perm = _to_scan_rows(dyg, n_samples)
    du_perm, g_dskip, da_cat, dbb_full, dc_full, *carried = _ssm_bwd(u_perm, dyg_perm, ytot, dskip, a_cat, bbc, cc, ein,
                                                                   n_rows, comm_of("ssm_bwd"))
    absorb("ssm_bwd", carried)
    du = _from_scan_rows(du_perm, n_samples)
    g_lr, g_li, g_ldt, g_br, g_bi, g_cr, g_ci = _ssm_param_bwd(
        fs["lr"], fs["li"], fs["ldt"], fs["br"], fs["bi"], da_cat, dbb_full, dc_full)

    small_pack.append(_pack_small(dict(lr=g_lr, li=g_li, ldt=g_ldt, br=g_br, bi=g_bi, cr=g_cr, ci=g_ci, dskip=g_dskip,
                                       g_ffn=g_gffn, g_fin=g_gfin, loss=loss_blk)))

    dproj = _pack_dproj(dqs, dks, dvs, du, dgpre, tabs)
    for hf in range(2):
        big[f"w_in:{hf}"] = mm(dproj, h0, "tn", f"mm_g_in{hf}", 256, half, out_dtype=BF16, cols=(hf * half, half))
    grad_x, _, g_gmix, *carried = _mm_rms_bwd([dproj], [w["w_in"]], x, g_mix, dx1, "d_h0_rms", comm_of("d_h0_rms"))
    absorb("d_h0_rms", carried)
    return grad_x, (big if shards is None else recv), small_pack[0], g_gmix


_MESH = pl.DeviceIdType.MESH


def _all_gather(block, name):
    rows, lanes = block.shape

    def body(x_ref, out_ref, send_sems, recv_sems, local_sem):
        x, y, c = lax.axis_index("x"), lax.axis_index("y"), lax.axis_index("c")
        me, sibling = (x, y, c), (x, y, 1 - c)
        chips = [(1 - x, y), (x, 1 - y), (1 - x, 1 - y)]

        def slot(px, py, pc):
            return out_ref.at[4 * px + 2 * py + pc]

        def copy(k, blk, to, src=None):
            return pltpu.make_async_remote_copy(
                src_ref=slot(*blk) if src is None else src, dst_ref=slot(*blk), send_sem=send_sems.at[k],
                recv_sem=recv_sems.at[k], device_id=to, device_id_type=_MESH)

        mine = pltpu.make_async_copy(x_ref, slot(*me), local_sem)
        mine.start()
        first = [copy(0, me, sibling, src=x_ref)]
        first += [copy(1 + j, me, (*chip, c), src=x_ref) for j, chip in enumerate(chips)]
        for cp in first:
            cp.start()
        passed = [copy(4 + j, (*chip, c), sibling) for j, chip in enumerate(chips)]
        for j, chip in enumerate(chips):
            copy(1 + j, (*chip, c), me).wait_recv()
            passed[j].start()
        copy(0, sibling, me).wait_recv()
        for j, chip in enumerate(chips):
            copy(4 + j, (*chip, 1 - c), me).wait_recv()
        for cp in first + passed:
            cp.wait_send()
        mine.wait()

    return _pallas_call(
        body, name=name, out_shape=jax.ShapeDtypeStruct((N_DEV, rows, lanes), block.dtype),
        in_specs=[pl.BlockSpec(memory_space=pl.ANY)], out_specs=pl.BlockSpec(memory_space=pl.ANY),
        scratch_shapes=[pltpu.SemaphoreType.DMA((7,)), pltpu.SemaphoreType.DMA((7,)), pltpu.SemaphoreType.DMA],
    )(block)


def _ag_comm(items, bufs):
    def plan(in_refs, out_refs, send_sems, recv_sems, local_sems):
        x, y, c = lax.axis_index("x"), lax.axis_index("y"), lax.axis_index("c")
        me, sibling = (x, y, c), (x, y, 1 - c)
        chips = [(1 - x, y), (x, 1 - y), (1 - x, 1 - y)]
        plans = []
        for t, (_, buf, slot0) in enumerate(items):
            x_ref, out_ref = in_refs[t], out_refs[buf]

            def slot(px, py, pc, out_ref=out_ref, slot0=slot0):
                return out_ref.at[slot0 + 4 * px + 2 * py + pc]

            def copy(k, blk, to, src=None, t=t, slot=slot):
                return pltpu.make_async_remote_copy(
                    src_ref=slot(*blk) if src is None else src, dst_ref=slot(*blk), send_sem=send_sems.at[7 * t + k],
                    recv_sem=recv_sems.at[7 * t + k], device_id=to, device_id_type=_MESH)

            plans.append(dict(
                mine=pltpu.make_async_copy(x_ref, slot(*me), local_sems.at[t]),
                first=[copy(0, me, sibling, src=x_ref)] + [copy(1 + j, me, (*chip, c), src=x_ref)
                                                           for j, chip in enumerate(chips)],
                passed=[copy(4 + j, (*chip, c), sibling) for j, chip in enumerate(chips)],
                from_ici=[copy(1 + j, (*chip, c), me) for j, chip in enumerate(chips)],
                from_sibling=[copy(0, sibling, me)] + [copy(4 + j, (*chip, 1 - c), me) for j, chip in enumerate(chips)]))
        return plans

    def start(*refs):
        for p in plan(*refs):
            p["mine"].start()
            for cp in p["first"]:
                cp.start()

    def finish(*refs):
        plans = plan(*refs)
        for p in plans:
            for arrived, onward in zip(p["from_ici"], p["passed"]):
                arrived.wait_recv()
                onward.start()
        for p in plans:
            for arrived in p["from_sibling"]:
                arrived.wait_recv()
            for cp in p["first"] + p["passed"]:
                cp.wait_send()
            p["mine"].wait()

    dtype_of = {buf: shard.dtype for shard, buf, _ in items}
    out_shapes = [jax.ShapeDtypeStruct(b, dtype_of[j]) for j, b in enumerate(bufs)]
    return _Comm([it[0] for it in items], out_shapes, 7 * len(items), len(items), start, finish)


def _a2a_comm(items):
    def plan(in_refs, out_refs, send_sems, recv_sems, local_sems):
        x, y, c = lax.axis_index("x"), lax.axis_index("y"), lax.axis_index("c")
        my = 4 * x + 2 * y + c
        copies, locals_ = [], []
        for t, (_, slot0) in enumerate(items):
            s_ref, r_ref = in_refs[t], out_refs[t]
            locals_.append(pltpu.make_async_copy(s_ref.at[slot0 + my], r_ref.at[my], local_sems.at[t]))
            for kk in range(1, N_DEV):
                px = 1 - x if kk & 4 else x
                py = 1 - y if kk & 2 else y
                pc = 1 - c if kk & 1 else c
                copies.append(pltpu.make_async_remote_copy(
                    src_ref=s_ref.at[slot0 + 4 * px + 2 * py + pc], dst_ref=r_ref.at[my],
                    send_sem=send_sems.at[7 * t + kk - 1], recv_sem=recv_sems.at[7 * t + kk - 1],
                    device_id=(px, py, pc), device_id_type=_MESH))
        return copies, locals_

    def start(*refs):
        copies, locals_ = plan(*refs)
        for cp in locals_ + copies:
            cp.start()

    def finish(*refs):
        copies, locals_ = plan(*refs)
        for cp in copies + locals_:
            cp.wait()

    out_shapes = [jax.ShapeDtypeStruct((N_DEV,) + it[0].shape[1:], it[0].dtype) for it in items]
    return _Comm([it[0] for it in items], out_shapes, 7 * len(items), len(items), start, finish)


def _adam_math(g, w, m, v):
    m_new = ADAM_B1 * m + (1.0 - ADAM_B1) * g
    v_new = ADAM_B2 * v + (1.0 - ADAM_B2) * jnp.square(g)
    m_hat = m_new / (1.0 - ADAM_B1 ** ADAM_STEP)
    v_hat = v_new / (1.0 - ADAM_B2 ** ADAM_STEP)
    return -ADAM_LR * (m_hat / (jnp.sqrt(v_hat) + ADAM_EPS) + ADAM_WD * w), m_new, v_new


def _sum_partials(parts, name, tm):
    n, rows, _ = parts[0].shape
    widths = [p.shape[2] for p in parts]

    def body(*refs):
        g_ref, off = refs[-1], 0
        for p_ref, wd in zip(refs[:-1], widths):
            g = p_ref[0].astype(F32)
            for s in range(1, n):
                g = g + p_ref[s].astype(F32)
            g_ref[:, off:off + wd] = g
            off += wd

    return _pallas_call(
        body, name=name, grid=(rows // tm,), in_specs=[pl.BlockSpec((n, tm, wd), lambda i: (0, i, 0)) for wd in widths],
        out_specs=pl.BlockSpec((tm, sum(widths)), lambda i: (i, 0)),
        out_shape=jax.ShapeDtypeStruct((rows, sum(widths)), F32),
        compiler_params=pltpu.CompilerParams(dimension_semantics=("parallel",), vmem_limit_bytes=VMEM_MID),
    )(*parts)


def _adam(parts, w, m, v, name, tm):
    n, rows, _ = parts[0].shape
    widths = [p.shape[2] for p in parts]
    cols = sum(widths)

    def body(*refs):
        p_refs, (w_ref, m_ref, v_ref, g_ref, d_ref, nm_ref, nv_ref) = refs[:len(parts)], refs[len(parts):]
        off = 0
        for p_ref, wd in zip(p_refs, widths):
            g = p_ref[0].astype(F32)
            for s in range(1, n):
                g = g + p_ref[s].astype(F32)
            sl = slice(off, off + wd)
            g_ref[:, sl] = g
            d_ref[:, sl], nm_ref[:, sl], nv_ref[:, sl] = _adam_math(g, w_ref[:, sl], m_ref[:, sl], v_ref[:, sl])
            off += wd

    assert rows % tm == 0
    row = pl.BlockSpec((tm, cols), lambda i: (i, 0))
    shp = jax.ShapeDtypeStruct((rows, cols), F32)
    return _pallas_call(
        body, name=name, grid=(rows // tm,),
        in_specs=[pl.BlockSpec((n, tm, wd), lambda i: (0, i, 0)) for wd in widths] + [row, row, row],
        out_specs=[row] * 4, out_shape=[shp] * 4,
        compiler_params=pltpu.CompilerParams(dimension_semantics=("parallel",), vmem_limit_bytes=VMEM_MID),
    )(*parts, w, m, v)


_PK_LR, _PK_LI, _PK_GAINS, _PK_MISC, _PK_BR, _PK_BI, _PK_CR, _PK_CI, _PK_ROWS = 0, 1, 2, 3, 8, 24, 40, 56, 72
_PK_LDT_LANE, _PK_LOSS_LANE = D_MODEL + SSM_W, D_MODEL + SSM_W + LANES


def _pack_small(sg):
    names = ("lr", "li", "g_ffn", "g_fin", "dskip", "ldt", "loss", "br", "bi", "cr", "ci")

    def body(lr, li, gffn, gfin, dskip, ldt, loss, br, bi, cr, ci, o_ref):
        o_ref[...] = jnp.zeros_like(o_ref)
        o_ref[_PK_LR:_PK_LR + 1, :] = lr[...]
        o_ref[_PK_LI:_PK_LI + 1, :] = li[...]
        o_ref[_PK_GAINS:_PK_GAINS + 1, D_MODEL:] = gffn[...]
        o_ref[_PK_MISC:_PK_MISC + 1, :D_MODEL] = gfin[...]
        o_ref[_PK_MISC:_PK_MISC + 1, D_MODEL:D_MODEL + SSM_W] = dskip[...]
        o_ref[_PK_MISC:_PK_MISC + 1, _PK_LDT_LANE:_PK_LDT_LANE + LANES] = ldt[0:1, :]
        o_ref[_PK_MISC:_PK_MISC + 1, _PK_LOSS_LANE:_PK_LOSS_LANE + LANES] = loss[0:1, :]
        o_ref[_PK_BR:_PK_BR + SSM_CH, :] = br[...]
        o_ref[_PK_BI:_PK_BI + SSM_CH, :] = bi[...]
        o_ref[_PK_CR:_PK_CR + SSM_CH, :] = cr[...]
        o_ref[_PK_CI:_PK_CI + SSM_CH, :] = ci[...]

    return _pallas_call(body, name="pack_small", out_shape=jax.ShapeDtypeStruct((_PK_ROWS, N_STATE), F32))(
        *[sg[n] for n in names])


def _unpack_small(s, g_mix):
    unflat_b = unflat_c = lambda a: a.reshape(SSM_CH, SSM_GROUPS, SSM_STATE).transpose(1, 0, 2)[None]
    grads = {
        "norm_mix_g": g_mix, "norm_ffn_g": s[_PK_GAINS, D_MODEL:].reshape(1, D_MODEL),
        "norm_final_g": s[_PK_MISC, :D_MODEL].reshape(1, D_MODEL),
        "ssm_a_re": s[_PK_LR].reshape(1, SSM_GROUPS, SSM_STATE), "ssm_a_im": s[_PK_LI].reshape(1, SSM_GROUPS, SSM_STATE),
        "ssm_log_dt": s[_PK_MISC, _PK_LDT_LANE:_PK_LDT_LANE + SSM_GROUPS].reshape(1, SSM_GROUPS),
        "ssm_d": s[_PK_MISC, D_MODEL:D_MODEL + SSM_W].reshape(1, SSM_GROUPS, SSM_CH),
        "ssm_b_re": unflat_b(s[_PK_BR:_PK_BR + SSM_CH]), "ssm_b_im": unflat_b(s[_PK_BI:_PK_BI + SSM_CH]),
        "ssm_c_re": unflat_c(s[_PK_CR:_PK_CR + SSM_CH]), "ssm_c_im": unflat_c(s[_PK_CI:_PK_CI + SSM_CH]),
    }
    return s[_PK_MISC, _PK_LOSS_LANE], grads


def _stored(name, a):
    if name in ("ssm_b_re", "ssm_b_im"):
        return a.transpose(0, 1, 3, 2)
    return a.reshape(1, -1) if a.ndim == 1 else a


def _unstored(name, a, like):
    return a.transpose(0, 1, 3, 2) if name in ("ssm_b_re", "ssm_b_im") else a.reshape(like.shape)


def _adam_small(grads, wts, moms, vars_):
    n = len(SMALL_WEIGHTS)

    def body(*refs):
        ins, outs = refs[:4 * n], refs[4 * n:]
        for i in range(n):
            g, w, m, v = (ins[j * n + i][...] for j in range(4))
            outs[i][...], outs[n + i][...], outs[2 * n + i][...] = _adam_math(g, w, m, v)

    operands = [grads[k] if d is grads else _stored(k, d[k]) for d in (grads, wts, moms, vars_) for k in SMALL_WEIGHTS]
    shapes = [jax.ShapeDtypeStruct(_stored(k, wts[k]).shape, F32) for k in SMALL_WEIGHTS] * 3
    res = _pallas_call(body, name="adam_small", out_shape=shapes,
                         compiler_params=pltpu.CompilerParams(vmem_limit_bytes=VMEM_BIG))(*operands)
    out = {}
    for j, kind in enumerate(("delta", "new_m", "new_v")):
        for i, k in enumerate(SMALL_WEIGHTS):
            out[kind, k] = _unstored(k, res[j * n + i], wts[k])
    return out


def kernel(x, norm_mix_g, w_in, ssm_a_re, ssm_a_im, ssm_log_dt, ssm_b_re, ssm_b_im, ssm_c_re, ssm_c_im, ssm_d, w_glu, w_attn_out, w_out, norm_ffn_g, w_ffn_gate, w_ffn_up, w_ffn_down, norm_final_g, loss_target, m_norm_mix_g, m_w_in, m_ssm_a_re, m_ssm_a_im, m_ssm_log_dt, m_ssm_b_re, m_ssm_b_im, m_ssm_c_re, m_ssm_c_im, m_ssm_d, m_w_glu, m_w_attn_out, m_w_out, m_norm_ffn_g, m_w_ffn_gate, m_w_ffn_up, m_w_ffn_down, m_norm_final_g, v_norm_mix_g, v_w_in, v_ssm_a_re, v_ssm_a_im, v_ssm_log_dt, v_ssm_b_re, v_ssm_b_im, v_ssm_c_re, v_ssm_c_im, v_ssm_d, v_w_glu, v_w_attn_out, v_w_out, v_norm_ffn_g, v_w_ffn_gate, v_w_ffn_up, v_w_ffn_down, v_norm_final_g):
    args = dict(locals())
    wts = {n: args[n] for n in ALL_WEIGHTS}
    moms = {n: args["m_" + n] for n in ALL_WEIGHTS}
    vars_ = {n: args["v_" + n] for n in ALL_WEIGHTS}
    n_samples = x.shape[0]
    t = n_samples * SEQ

    shards = {n: (wts[n][0] if n in ROW_SHARDED else wts[n][0].T).astype(BF16) for n in BIG_WEIGHTS}
    w_in_t = _all_gather(shards["w_in"], "allgather_w_in").reshape(IN_W, D_MODEL)

    small = {n: wts[n] for n in SMALL_WEIGHTS}
    grad_x, recv, _, g_mix_part = _local_step(x.reshape(t, D_MODEL), loss_target.reshape(t, D_MODEL), {"w_in": w_in_t},
                                              small, shards)

    results = {}
    for n in BIG_WEIGHTS:
        c, k = shards[n].shape
        w2, m2, v2 = wts[n][0], moms[n][0], vars_[n][0]
        if n in ROW_SHARDED:
            res = _adam([recv[n]], w2, m2, v2, "adam_" + n, c // 2)
        elif n in HALVED:
            res = _adam([recv[f"{n}:{hf}"] for hf in range(2)], w2.T, m2.T, v2.T, "adam_" + n, c // 2)
            res = [a.T for a in res]
        else:
            g_t = _sum_partials([recv[n]], "sum_" + n, c // 2)
            res = _adam([g_t.T[None]], w2, m2, v2, "adam_" + n, k // 2)
        for kind, a in zip(("grad", "delta", "new_m", "new_v"), res):
            results[kind, n] = a[None]

    g_mix_all = _all_gather(jnp.pad(g_mix_part, ((0, 7), (0, 0))), "allgather_g_mix")
    g_mix = _sum_partials([g_mix_all], "sum_g_mix", 8)[0:1]
    loss, sgrads = _unpack_small(_sum_partials([recv["small"]], "sum_small", _PK_ROWS), g_mix)
    for n in SMALL_WEIGHTS:
        results["grad", n] = _unstored(n, sgrads[n], wts[n])
    results.update(_adam_small(sgrads, wts, moms, vars_))
    outs = [loss, grad_x.reshape(x.shape)]
    for kind in ("grad", "delta", "new_m", "new_v"):
        outs += [results[kind, n] for n in ALL_WEIGHTS]
    return tuple(outs)
```

```python
import functools
import math

import jax
import jax.numpy as jnp
from jax import lax
from jax.experimental import pallas as pl
from jax.experimental.pallas import tpu as pltpu

F32 = jnp.float32
BF16 = jnp.bfloat16
MXU_DTYPE = jnp.bfloat16

N_DEV = 8
D_MODEL = 1024
SEQ = 2048
HEAD_DIM = 64
HEADS_PER_GROUP = 4
GROUP_W = HEADS_PER_GROUP * HEAD_DIM
DILATIONS = (1, 4, 16)
QKV_W = 3 * len(DILATIONS) * GROUP_W
Q_W = len(DILATIONS) * GROUP_W
ATT_BLOCK = 128
ROPE_DIM = 16
ROPE_THETA = 500000.0
SSM_W = 512
SSM_GROUPS = 32
SSM_CH = 16
SSM_STATE = 64
N_STATE = SSM_GROUPS * SSM_STATE
D_FF = 2816
IN_W = QKV_W + SSM_W + 2 * D_MODEL
RMS_EPS = 1e-6
NEG_INF = -1e30
LANES = 128

SCAN_SEG_PER_SAMPLE = 8
SCAN_LEN = SEQ // SCAN_SEG_PER_SAMPLE
SCAN_WC = 512
SCAN_NBLK = N_STATE // SCAN_WC
SCAN_CH = SSM_W // SCAN_NBLK
SCAN_CHUNK = 32

ADAM_LR = 0.001
ADAM_B1 = 0.9
ADAM_B2 = 0.999
ADAM_EPS = 1e-08
ADAM_WD = 0.01
ADAM_STEP = 10

VMEM_BIG = 48 * 1024 * 1024
VMEM_MID = 32 * 1024 * 1024

BIG_WEIGHTS = ("w_in", "w_glu", "w_attn_out", "w_out", "w_ffn_gate", "w_ffn_up", "w_ffn_down")
ROW_SHARDED = ("w_out", "w_ffn_down")
SMALL_WEIGHTS = ("norm_mix_g", "ssm_a_re", "ssm_a_im", "ssm_log_dt", "ssm_b_re", "ssm_b_im", "ssm_c_re", "ssm_c_im",
                 "ssm_d", "norm_ffn_g", "norm_final_g")
ALL_WEIGHTS = ("norm_mix_g", "w_in", "ssm_a_re", "ssm_a_im", "ssm_log_dt", "ssm_b_re", "ssm_b_im", "ssm_c_re", "ssm_c_im",
               "ssm_d", "w_glu", "w_attn_out", "w_out", "norm_ffn_g", "w_ffn_gate", "w_ffn_up", "w_ffn_down", "norm_final_g")


def _sigmoid(x):
    return 1.0 / (1.0 + jnp.exp(-x))


def _pallas_call(body, *, out_shape, **kw):
    single = not isinstance(out_shape, (list, tuple))
    shapes = [pltpu.HBM(s.shape, s.dtype) for s in ([out_shape] if single else out_shape)]
    call = pl.pallas_call(body, out_shape=shapes[0] if single else shapes, **kw)
    return lambda *operands: call(*[pltpu.with_memory_space_constraint(o, pltpu.HBM) for o in operands])


class _Comm:
    def __init__(self, ins, out_shapes, n_sem, n_local, start, finish):
        self.ins, self.out_shapes, self.n_sem, self.n_local = ins, out_shapes, n_sem, n_local
        self.start, self.finish = start, finish


def _mm(a, b, mode, name, tm, tn, out_dtype=F32, add=None, vmem=VMEM_BIG, comm=None, cols=None):
    if mode == "nn":
        (m, k), (_, n) = a.shape, b.shape
        a_spec = pl.BlockSpec((tm, k), lambda i, j: (i, 0))
        b_spec = pl.BlockSpec((k, tn), lambda i, j: (0, j))
        dims = (((1,), (0,)), ((), ()))
    elif mode == "nt":
        (m, k), (n, _) = a.shape, b.shape
        a_spec = pl.BlockSpec((tm, k), lambda i, j: (i, 0))
        b_spec = pl.BlockSpec((tn, k), lambda i, j: (j, 0))
        dims = (((1,), (1,)), ((), ()))
    else:
        (k, m), (_, n) = a.shape, b.shape
        first, n = cols if cols else (0, n)
        a_spec = pl.BlockSpec((k, tm), lambda i, j: (0, i))
        b_spec = pl.BlockSpec((k, tn), lambda i, j: (0, j + first // tn))
        dims = (((0,), (0,)), ((), ()))
    assert m % tm == 0 and n % tn == 0, (name, m, n, tm, tn)
    o_spec = pl.BlockSpec((tm, tn), lambda i, j: (i, j))
    has_add = add is not None

    def body(*refs):
        a_ref, b_ref, o_ref = refs[0], refs[1], refs[-1]
        acc = lax.dot_general(a_ref[...].astype(MXU_DTYPE), b_ref[...].astype(MXU_DTYPE), dims,
                              preferred_element_type=F32)
        if has_add:
            acc = acc + refs[2][...]
        o_ref[...] = acc.astype(out_dtype)

    ins = [a, b] + ([add] if has_add else [])
    in_specs = [a_spec, b_spec] + ([o_spec] if has_add else [])
    return _grid_call(body, name, (m // tm, n // tn), ins, in_specs, [o_spec],
                      [jax.ShapeDtypeStruct((m, n), out_dtype)], vmem, comm)


def _grid_call(body, name, grid, ins, in_specs, out_specs, out_shapes, vmem, comm=None, sequential=False, scratch=()):
    if comm is None:
        single = len(out_shapes) == 1
        semantics = ("arbitrary", "arbitrary") if sequential else ("parallel", "parallel")
        return _pallas_call(
            body, name=name, grid=grid, in_specs=in_specs, out_specs=out_specs[0] if single else out_specs,
            out_shape=out_shapes[0] if single else out_shapes, scratch_shapes=list(scratch),
            compiler_params=pltpu.CompilerParams(dimension_semantics=semantics, vmem_limit_bytes=vmem),
        )(*ins)
    n_in, n_out, n_cin, n_cout = len(ins), len(out_shapes), len(comm.ins), len(comm.out_shapes)
    n_io = n_in + n_cin + n_out + n_cout

    def carrying(*refs):
        own = refs[:n_in] + refs[n_in + n_cin:n_in + n_cin + n_out] + refs[n_io:len(refs) - 3]
        c_args = (refs[n_in:n_in + n_cin], refs[n_in + n_cin + n_out:n_io], *refs[-3:])

        @pl.when((pl.program_id(0) == 0) & (pl.program_id(1) == 0))
        def _():
            comm.start(*c_args)

        body(*own)

        @pl.when((pl.program_id(0) == grid[0] - 1) & (pl.program_id(1) == grid[1] - 1))
        def _():
            comm.finish(*c_args)

    hbm = pl.BlockSpec(memory_space=pl.ANY)
    return _pallas_call(
        carrying, name=name, grid=grid, in_specs=list(in_specs) + [hbm] * n_cin,
        out_specs=list(out_specs) + [hbm] * n_cout, out_shape=list(out_shapes) + list(comm.out_shapes),
        scratch_shapes=list(scratch) + [pltpu.SemaphoreType.DMA((comm.n_sem,)), pltpu.SemaphoreType.DMA((comm.n_sem,)),
                                        pltpu.SemaphoreType.DMA((comm.n_local,))],
        compiler_params=pltpu.CompilerParams(dimension_semantics=("arbitrary", "arbitrary"), vmem_limit_bytes=vmem),
    )(*ins, *comm.ins)


def _rows(body, name, n_rows, tm, ins, outs, vmem=VMEM_MID, scratch=()):
    assert n_rows % tm == 0
    arrays, in_specs = [], []
    for kind, arr in ins:
        arrays.append(arr)
        if kind == "row":
            assert n_rows % arr.shape[0] == 0, (name, arr.shape)
            in_specs.append(pl.BlockSpec((tm * arr.shape[0] // n_rows, arr.shape[1]), lambda i: (i, 0)))
        elif kind == "tab":
            nblk = arr.shape[0] // tm
            in_specs.append(pl.BlockSpec((tm, arr.shape[1]), lambda i, nblk=nblk: (i % nblk, 0)))
        else:
            in_specs.append(pl.BlockSpec(arr.shape, lambda i, nd=arr.ndim: (0,) * nd))
    out_specs, out_shape = [], []
    for kind, shp, dt in outs:
        if kind == "row":
            out_specs.append(pl.BlockSpec((tm, shp), lambda i: (i, 0)))
            out_shape.append(jax.ShapeDtypeStruct((n_rows, shp), dt))
        elif kind == "dil":
            d, wd = shp
            out_specs.append(pl.BlockSpec((tm // d, d * wd), lambda i: (i, 0)))
            out_shape.append(jax.ShapeDtypeStruct((n_rows // d, d * wd), dt))
        else:
            out_specs.append(pl.BlockSpec(shp, lambda i, nd=len(shp): (0,) * nd))
            out_shape.append(jax.ShapeDtypeStruct(shp, dt))
    res = _pallas_call(
        body, name=name, grid=(n_rows // tm,), in_specs=in_specs, out_specs=out_specs, out_shape=out_shape,
        scratch_shapes=list(scratch),
        compiler_params=pltpu.CompilerParams(dimension_semantics=("arbitrary",), vmem_limit_bytes=vmem),
    )(*arrays)
    return res


def _gather_residue(stage, ch, r, d, n):
    return stage[ch, pl.ds(r, n, stride=d), :] if d > 1 else stage[ch]


def _scatter_residue(stage, ch, r, d, n, val):
    if d > 1:
        stage[ch, pl.ds(r, n, stride=d), :] = val
    else:
        stage[ch] = val


def _lane_chunk(ch):
    return slice(ch * LANES, (ch + 1) * LANES)


def _rope_tables():
    half = ROPE_DIM // 2
    inv = jnp.power(jnp.float32(ROPE_THETA), -jnp.arange(half, dtype=F32) * 2.0 / ROPE_DIM)
    ang = jnp.arange(SEQ, dtype=F32)[:, None] * inv[None, :]
    lane = jnp.arange(LANES) % HEAD_DIM
    cosl = jnp.cos(ang)[:, lane % half]
    sinl = jnp.sin(ang)[:, lane % half]
    tab_c = jnp.where(lane < ROPE_DIM, cosl, 1.0)
    tab_lo = jnp.where(lane < half, -sinl, 0.0)
    tab_hi = jnp.where((lane >= half) & (lane < ROPE_DIM), sinl, 0.0)
    return tab_c.astype(F32), tab_lo.astype(F32), tab_hi.astype(F32)


def _rope_apply(t, tc, tlo, thi):
    half = ROPE_DIM // 2
    return t * tc + pltpu.roll(t, LANES - half, 1) * tlo + pltpu.roll(t, half, 1) * thi


def _rope_transpose(dt, tc, tlo, thi):
    half = ROPE_DIM // 2
    return dt * tc + pltpu.roll(dt * tlo, half, 1) + pltpu.roll(dt * thi, LANES - half, 1)


def _pack_dproj(dqs, dks, dvs, du, dgpre, tabs):
    tm = 256

    def body(*refs):
        dq_refs, dk_refs, dv_refs = refs[0:3], refs[3:6], refs[6:9]
        du_ref, dg_ref, tc_ref, tlo_ref, thi_ref, o_ref, stage = refs[9:16]
        n_ch = QKV_W // LANES
        halves = GROUP_W // LANES
        for grp, d in enumerate(DILATIONS):
            for which, src in enumerate((dq_refs[grp], dk_refs[grp], dv_refs[grp])):
                for res in range(d):
                    for half in range(halves):
                        _scatter_residue(stage, which * (n_ch // 3) + grp * halves + half, res, d, tm // d,
                                         src[:, _lane_chunk(res * halves + half)])
        tc, tlo, thi = tc_ref[...], tlo_ref[...], thi_ref[...]
        for ch in range(n_ch):
            piece = stage[ch]
            o_ref[:, _lane_chunk(ch)] = (_rope_transpose(piece, tc, tlo, thi) if ch < 2 * n_ch // 3 else piece).astype(BF16)
        o_ref[:, QKV_W:QKV_W + SSM_W] = du_ref[...].astype(BF16)
        o_ref[:, QKV_W + SSM_W:] = dg_ref[...].astype(BF16)

    t = du.shape[0]
    ins = [("row", a) for a in (*dqs, *dks, *dvs, du, dgpre)] + [("tab", tb) for tb in tabs]
    return _rows(body, "pack_dproj", t, tm, ins, [("row", IN_W, BF16)],
                 scratch=[pltpu.VMEM((QKV_W // LANES, tm, LANES), F32)])[0]


def _merge_groups(o_refs, l_refs, a_ref, lt_ref, nat, tm):
    halves = GROUP_W // LANES
    for grp, d in enumerate(DILATIONS[1:], start=1):
        for j, src in enumerate((o_refs[grp], l_refs[grp])):
            for res in range(d):
                for half in range(halves):
                    _scatter_residue(nat, (grp - 1) * 4 + j * 2 + half, res, d, tm // d,
                                     src[:, _lane_chunk(res * halves + half)])
    for half in range(halves):
        sl = _lane_chunk(half)
        la, lb, lc = l_refs[0][:, sl], nat[2 + half], nat[6 + half]
        m = jnp.maximum(jnp.maximum(la, lb), lc)
        ea, eb, ec = jnp.exp(la - m), jnp.exp(lb - m), jnp.exp(lc - m)
        ssum = ea + eb + ec
        a_ref[:, sl] = (ea / ssum) * o_refs[0][:, sl] + (eb / ssum) * nat[half] + (ec / ssum) * nat[4 + half]
        lt_ref[:, sl] = m + jnp.log(ssum)


def _head_sum_matrix():
    r = jnp.arange(GROUP_W) // HEAD_DIM
    return (r[:, None] == r[None, :]).astype(F32)


def _attention_cotangents(da, attn, lt, ones, rd_ref, dil, stage, tm):
    halves = GROUP_W // LANES
    rd = jnp.dot(da * attn, ones, preferred_element_type=F32, precision=lax.Precision.HIGHEST)
    rd_ref[...] = rd
    for half in range(halves):
        for j, val in enumerate((da, lt, rd)):
            stage[2 * j + half] = val[:, _lane_chunk(half)]
    for grp, d in enumerate(DILATIONS[1:], start=1):
        for j in range(3):
            for res in range(d):
                for half in range(halves):
                    dil[3 * (grp - 1) + j][:, _lane_chunk(res * halves + half)] = _gather_residue(
                        stage, 2 * j + half, res, d, tm // d)


_GELU_C = math.sqrt(2.0 / math.pi)


def _head_masks():
    lane = lax.broadcasted_iota(jnp.int32, (1, GROUP_W), 1)
    return [(lane // HEAD_DIM) == h for h in range(HEADS_PER_GROUP)]


def _stack_heads(blk, masks, fill=0.0):
    return jnp.concatenate([jnp.where(mk, blk, jnp.full_like(blk, fill)) for mk in masks], axis=0)


def _unstack_heads(stacked, masks):
    rows = stacked.shape[0] // len(masks)
    out = stacked[:rows]
    for h in range(1, len(masks)):
        out = jnp.where(masks[h], stacked[h * rows:(h + 1) * rows], out)
    return out


def _band_mask(first):
    nk = ATT_BLOCK if first else 2 * ATT_BLOCK
    qi = lax.broadcasted_iota(jnp.int32, (ATT_BLOCK, nk), 0)
    ki = lax.broadcasted_iota(jnp.int32, (ATT_BLOCK, nk), 1)
    dist = qi - ki + (0 if first else ATT_BLOCK)
    return (dist >= 0) & (dist <= ATT_BLOCK)


_NT = (((1,), (1,)), ((), ()))
_TN = (((0,), (0,)), ((), ()))


def _residues_per_step(d):
    return min(d, 4)


def _attn_fwd(q, k, v, group, n_samples, comm=None):
    d = DILATIONS[group]
    length = SEQ // d
    nb = length // ATT_BLOCK

    rps = _residues_per_step(d)

    def body(q_ref, k_ref, v_ref, o_ref, l_ref):
        for rl in range(rps):
            residue(q_ref, k_ref, v_ref, o_ref, l_ref, slice(rl * GROUP_W, (rl + 1) * GROUP_W))

    def residue(q_ref, k_ref, v_ref, o_ref, l_ref, cols):
        masks = _head_masks()

        def block(qs, ks, first):
            nk = ATT_BLOCK if first else 2 * ATT_BLOCK
            qb = q_ref[0, pl.ds(qs, ATT_BLOCK), cols]
            kc = k_ref[0, pl.ds(ks, nk), cols]
            vc = v_ref[0, pl.ds(ks, nk), cols]
            q4 = _stack_heads(qb, masks)
            valid = jnp.tile(_band_mask(first), (HEADS_PER_GROUP, 1))
            s = lax.dot_general(q4, kc, _NT, preferred_element_type=F32) * (HEAD_DIM ** -0.5)
            s = jnp.where(valid, s, NEG_INF)
            m = jnp.max(s, axis=-1, keepdims=True)
            p = jnp.exp(s - m)
            l = jnp.sum(p, axis=-1, keepdims=True)
            o4 = jnp.dot(p.astype(MXU_DTYPE), vc, preferred_element_type=F32) / l
            lse4 = jnp.broadcast_to(m + jnp.log(l), o4.shape)
            o_ref[0, pl.ds(qs, ATT_BLOCK), cols] = _unstack_heads(o4, masks)
            l_ref[0, pl.ds(qs, ATT_BLOCK), cols] = _unstack_heads(lse4, masks)

        block(0, 0, True)
        if nb > 1:
            def loop(n, carry):
                block(pl.multiple_of(n * ATT_BLOCK, ATT_BLOCK), pl.multiple_of((n - 1) * ATT_BLOCK, ATT_BLOCK), False)
                return carry

            lax.fori_loop(1, nb, loop, 0)

    per_sample = lambda a: a.reshape(n_samples, length, d * GROUP_W)
    spec = pl.BlockSpec((1, length, rps * GROUP_W), lambda b, r: (b, 0, r))
    shp = jax.ShapeDtypeStruct((n_samples, length, d * GROUP_W), F32)
    o, lse, *carried = _grid_call(body, f"attn_fwd_g{group}", (n_samples, d // rps), [per_sample(a) for a in (q, k, v)],
                                  [spec] * 3, [spec] * 2, [shp, shp], VMEM_MID, comm)
    flat = lambda a: a.reshape(n_samples * length, d * GROUP_W)
    return flat(o), flat(lse), carried


def _attn_bwd(q, k, v, dattn, lse_tot, rowdot, group, n_samples, comm=None):
    d = DILATIONS[group]
    length = SEQ // d
    nb = length // ATT_BLOCK

    rps = _residues_per_step(d)

    def body(q_ref, k_ref, v_ref, da_ref, lt_ref, rd_ref, dq_ref, dk_ref, dv_ref):
        dk_ref[...] = jnp.zeros_like(dk_ref)
        dv_ref[...] = jnp.zeros_like(dv_ref)
        for rl in range(rps):
            residue(q_ref, k_ref, v_ref, da_ref, lt_ref, rd_ref, dq_ref, dk_ref, dv_ref,
                    slice(rl * GROUP_W, (rl + 1) * GROUP_W))

    def residue(q_ref, k_ref, v_ref, da_ref, lt_ref, rd_ref, dq_ref, dk_ref, dv_ref, cols):
        masks = _head_masks()

        def block(qs, ks, first):
            nk = ATT_BLOCK if first else 2 * ATT_BLOCK
            qb = q_ref[0, pl.ds(qs, ATT_BLOCK), cols]
            kc = k_ref[0, pl.ds(ks, nk), cols]
            vc = v_ref[0, pl.ds(ks, nk), cols]
            da = da_ref[0, pl.ds(qs, ATT_BLOCK), cols]
            lt = lt_ref[0, pl.ds(qs, ATT_BLOCK), cols]
            rd = rd_ref[0, pl.ds(qs, ATT_BLOCK), cols]
            q4 = _stack_heads(qb, masks)
            da4 = _stack_heads(da, masks).astype(MXU_DTYPE)
            lt4 = jnp.max(_stack_heads(lt, masks, -jnp.inf), axis=-1, keepdims=True)
            rd4 = jnp.max(_stack_heads(rd, masks, -jnp.inf), axis=-1, keepdims=True)
            valid = jnp.tile(_band_mask(first), (HEADS_PER_GROUP, 1))
            s = lax.dot_general(q4, kc, _NT, preferred_element_type=F32) * (HEAD_DIM ** -0.5)
            s = jnp.where(valid, s, NEG_INF)
            p = jnp.exp(s - lt4)
            dp = lax.dot_general(da4, vc, _NT, preferred_element_type=F32)
            ds = (p * (dp - rd4) * (HEAD_DIM ** -0.5)).astype(MXU_DTYPE)
            dq_ref[0, pl.ds(qs, ATT_BLOCK), cols] = _unstack_heads(jnp.dot(ds, kc, preferred_element_type=F32), masks)
            dk_ref[0, pl.ds(ks, nk), cols] += lax.dot_general(ds, q4, _TN, preferred_element_type=F32)
            dv_ref[0, pl.ds(ks, nk), cols] += lax.dot_general(p.astype(MXU_DTYPE), da4, _TN, preferred_element_type=F32)

        block(0, 0, True)
        if nb > 1:
            def loop(n, carry):
                block(pl.multiple_of(n * ATT_BLOCK, ATT_BLOCK), pl.multiple_of((n - 1) * ATT_BLOCK, ATT_BLOCK), False)
                return carry

            lax.fori_loop(1, nb, loop, 0)

    per_sample = lambda a: a.reshape(n_samples, length, d * GROUP_W)
    spec = pl.BlockSpec((1, length, rps * GROUP_W), lambda b, r: (b, 0, r))
    shp = jax.ShapeDtypeStruct((n_samples, length, d * GROUP_W), F32)
    dq, dk, dv, *carried = _grid_call(
        body, f"attn_bwd_g{group}", (n_samples, d // rps), [per_sample(a) for a in (q, k, v, dattn, lse_tot, rowdot)],
        [spec] * 6, [spec] * 3, [shp, shp, shp], VMEM_MID, comm)
    flat = lambda a: a.reshape(n_samples * length, d * GROUP_W)
    return flat(dq), flat(dk), flat(dv), carried


def _disc(lr, li, ldt, br, bi):
    dt = jnp.exp(ldt)
    mag = jnp.exp(lr * dt)
    ab_re, ab_im = mag * jnp.cos(li * dt), mag * jnp.sin(li * dt)
    den = lr * lr + li * li
    nr, ni = ab_re - 1.0, ab_im
    f_re = (nr * lr + ni * li) / den
    f_im = (ni * lr - nr * li) / den
    return ab_re, ab_im, f_re * br - f_im * bi, f_re * bi + f_im * br


def _state_mask():
    row_g = lax.broadcasted_iota(jnp.int32, (SCAN_CH, SCAN_WC), 0) // SSM_CH
    col_g = lax.broadcasted_iota(jnp.int32, (SCAN_CH, SCAN_WC), 1) // SSM_STATE
    return row_g == col_g


def _ssm_disc(lr, li, ldt, br, bi, cr, ci):
    w = SCAN_WC

    def body(lr_ref, li_ref, ldt_ref, br_ref, bi_ref, cr_ref, ci_ref, a_ref, bb_ref, c_ref):
        ar, ai, bbr, bbi = _disc(lr_ref[...], li_ref[...], ldt_ref[...], br_ref[...], bi_ref[...])
        crv, civ = cr_ref[...], ci_ref[...]
        mask = _state_mask()
        for cb in range(SCAN_NBLK):
            sl = slice(cb * w, (cb + 1) * w)
            rows = slice(cb * SCAN_CH, (cb + 1) * SCAN_CH)
            dense = lambda comp: jnp.where(mask, jnp.tile(comp[:, sl], (SCAN_CH // SSM_CH, 1)), 0.0)
            a_ref[:, 2 * cb * w:(2 * cb + 1) * w] = ar[:, sl]
            a_ref[:, (2 * cb + 1) * w:(2 * cb + 2) * w] = ai[:, sl]
            bb_ref[rows, :w] = dense(bbr).astype(MXU_DTYPE)
            bb_ref[rows, w:] = dense(bbi).astype(MXU_DTYPE)
            c_ref[rows, :w] = dense(crv).astype(MXU_DTYPE)
            c_ref[rows, w:] = (-dense(civ)).astype(MXU_DTYPE)

    return _pallas_call(
        body, name="ssm_disc",
        out_shape=[jax.ShapeDtypeStruct((1, 2 * N_STATE), F32), jax.ShapeDtypeStruct((SSM_W, 2 * w), MXU_DTYPE),
                   jax.ShapeDtypeStruct((SSM_W, 2 * w), MXU_DTYPE)],
        compiler_params=pltpu.CompilerParams(vmem_limit_bytes=VMEM_MID),
    )(lr, li, ldt, br, bi, cr, ci)


def _group_indicator():
    s = jnp.arange(N_STATE) // SSM_STATE
    return (s[:, None] == jnp.arange(LANES)[None, :]).astype(F32)


def _ssm_param_bwd(lr, li, ldt, br, bi, da_cat, dbb_full, dc_full):
    w = SCAN_WC

    def body(lr_ref, li_ref, ldt_ref, br_ref, bi_ref, da_ref, dbb_ref, dc_ref, ind_ref,
             glr_ref, gli_ref, gldt_ref, gbr_ref, gbi_ref, gcr_ref, gci_ref):
        mask = _state_mask()

        def diag_parts(ref):
            res = ([], [])
            for cb in range(SCAN_NBLK):
                for part in range(2):
                    blk = ref[cb * SCAN_CH:(cb + 1) * SCAN_CH, part * w:(part + 1) * w]
                    res[part].append(jnp.sum(jnp.where(mask, blk, 0.0).reshape(SCAN_CH // SSM_CH, SSM_CH, w), axis=0))
            return jnp.concatenate(res[0], axis=1), jnp.concatenate(res[1], axis=1)

        dar = jnp.concatenate([da_ref[:, 2 * cb * w:(2 * cb + 1) * w] for cb in range(SCAN_NBLK)], axis=1)
        dai = jnp.concatenate([da_ref[:, (2 * cb + 1) * w:(2 * cb + 2) * w] for cb in range(SCAN_NBLK)], axis=1)
        dbbr, dbbi = diag_parts(dbb_ref)
        dcr, dci_neg = diag_parts(dc_ref)
        gcr_ref[...] = dcr
        gci_ref[...] = -dci_neg
        _, vjp = jax.vjp(_disc, lr_ref[...], li_ref[...], ldt_ref[...], br_ref[...], bi_ref[...])
        glr, gli, gldt, gbr, gbi = vjp((dar, dai, dbbr, dbbi))
        glr_ref[...] = glr
        gli_ref[...] = gli
        gldt_ref[...] = jnp.dot(jnp.broadcast_to(gldt, (8, N_STATE)), ind_ref[...], preferred_element_type=F32,
                                precision=lax.Precision.HIGHEST)
        gbr_ref[...] = gbr
        gbi_ref[...] = gbi

    v1 = jax.ShapeDtypeStruct((1, N_STATE), F32)
    v16 = jax.ShapeDtypeStruct((SSM_CH, N_STATE), F32)
    vdt = jax.ShapeDtypeStruct((8, LANES), F32)
    return _pallas_call(
        body, name="ssm_param_bwd", out_shape=[v1, v1, vdt, v16, v16, v16, v16],
        compiler_params=pltpu.CompilerParams(vmem_limit_bytes=VMEM_BIG),
    )(lr, li, ldt, br, bi, da_cat, dbb_full, dc_full, _group_indicator())


def _cmul(ar, ai, br, bi):
    return ar * br - ai * bi, ar * bi + ai * br


def _gelu_tanh(y):
    return jnp.tanh(_GELU_C * (y + 0.044715 * (y * y * y)))


def _segment_carry(er, ei, ar, ai, n_rows, reverse):
    qr, qi = ar, ai
    for _ in range(int(math.log2(SCAN_LEN))):
        qr, qi = _cmul(qr, qi, qr, qi)
    seg = lax.broadcasted_iota(jnp.int32, er.shape, 0) % SCAN_SEG_PER_SAMPLE
    shift = 1
    while shift < SCAN_SEG_PER_SAMPLE:
        keep = (seg < SCAN_SEG_PER_SAMPLE - shift) if reverse else (seg >= shift)
        amount = n_rows - shift if reverse else shift
        sr = jnp.where(keep, pltpu.roll(er, amount, 0), 0.0)
        si = jnp.where(keep, pltpu.roll(ei, amount, 0), 0.0)
        if reverse:
            er, ei = er + qr * sr + qi * si, ei + qr * si - qi * sr
        else:
            er, ei = er + qr * sr - qi * si, ei + qr * si + qi * sr
        qr, qi = _cmul(qr, qi, qr, qi)
        shift *= 2
    keep = (seg < SCAN_SEG_PER_SAMPLE - 1) if reverse else (seg >= 1)
    amount = n_rows - 1 if reverse else 1
    return jnp.where(keep, pltpu.roll(er, amount, 0), 0.0), jnp.where(keep, pltpu.roll(ei, amount, 0), 0.0)


def _ssm_fwd(u_perm, a_cat, bbc, cc, dskip, n_rows):
    t = u_perm.shape[0]
    w = SCAN_WC
    rows_c = SCAN_CHUNK * n_rows
    n_chunks = t // rows_c

    assert n_chunks % 2 == 0

    def body(u_ref, a_ref, bb_ref, c_ref, d_ref, yt_ref, yg_ref, ein_ref, bu_all, st_a, st_b, xs_a, xs_b):
        ar = jnp.broadcast_to(a_ref[:, :w], (n_rows, w))
        ai = jnp.broadcast_to(a_ref[:, w:], (n_rows, w))
        start = lambda ch: pl.multiple_of(ch * rows_c, rows_c)

        def project(ch, stage):
            res = jnp.dot(u_ref[pl.ds(start(ch), rows_c), :].astype(MXU_DTYPE), bb_ref[...], preferred_element_type=F32)
            stage[...] = res
            bu_all[pl.ds(start(ch), rows_c), :] = res

        def steps(src, r0, carry, xs=None):
            for i in range(SCAN_CHUNK):
                blk = src[pl.ds(r0 + i * n_rows, n_rows), :]
                carry = (ar * carry[0] - ai * carry[1] + blk[:, :w], ar * carry[1] + ai * carry[0] + blk[:, w:])
                if xs is not None:
                    xs[i * n_rows:(i + 1) * n_rows, :w] = carry[0]
                    xs[i * n_rows:(i + 1) * n_rows, w:] = carry[1]
            return carry

        def emit(xs, ch):
            y = lax.dot_general(xs[...].astype(MXU_DTYPE), c_ref[...], _NT, preferred_element_type=F32)
            yt = y + d_ref[...] * u_ref[pl.ds(start(ch), rows_c), :]
            yt_ref[pl.ds(start(ch), rows_c), :] = yt
            yg_ref[pl.ds(start(ch), rows_c), :] = (0.5 * yt * (1.0 + _gelu_tanh(yt))).astype(BF16)

        project(0, st_a)

        def pair1(p, carry):
            project(2 * p + 1, st_b)
            carry = steps(st_a, 0, carry)
            project(jnp.minimum(2 * p + 2, n_chunks - 1), st_a)
            return steps(st_b, 0, carry)

        zero = jnp.zeros((n_rows, w), F32)
        er, ei = lax.fori_loop(0, n_chunks // 2, pair1, (zero, zero))
        cr, ci = _segment_carry(er, ei, ar, ai, n_rows, False)
        ein_ref[:, :w] = cr
        ein_ref[:, w:] = ci

        xs_b[...] = jnp.zeros_like(xs_b)

        def pair2(p, carry):
            emit(xs_b, jnp.maximum(2 * p - 1, 0))
            carry = steps(bu_all, start(2 * p), carry, xs_a)
            emit(xs_a, 2 * p)
            return steps(bu_all, start(2 * p + 1), carry, xs_b)

        lax.fori_loop(0, n_chunks // 2, pair2, (cr, ci))
        emit(xs_b, n_chunks - 1)

    col = lambda width: pl.BlockSpec((t, width), lambda c: (0, c))
    wgt = pl.BlockSpec((SCAN_CH, 2 * w), lambda c: (c, 0))
    return _pallas_call(
        body, name="ssm_fwd", grid=(SCAN_NBLK,),
        in_specs=[col(SCAN_CH), pl.BlockSpec((1, 2 * w), lambda c: (0, c)), wgt, wgt,
                  pl.BlockSpec((1, SCAN_CH), lambda c: (0, c))],
        out_specs=[col(SCAN_CH), col(SCAN_CH), pl.BlockSpec((n_rows, 2 * w), lambda c: (0, c))],
        out_shape=[jax.ShapeDtypeStruct((t, SSM_W), F32), jax.ShapeDtypeStruct((t, SSM_W), BF16),
                   jax.ShapeDtypeStruct((n_rows, 2 * N_STATE), F32)],
        scratch_shapes=[pltpu.VMEM((t, 2 * w), F32)] + [pltpu.VMEM((rows_c, 2 * w), F32)] * 4,
        compiler_params=pltpu.CompilerParams(dimension_semantics=("parallel",), vmem_limit_bytes=VMEM_BIG),
    )(u_perm, a_cat, bbc, cc, dskip)


def _ssm_bwd(u_perm, dyg, ytot, dskip, a_cat, bbc, cc, ein, n_rows, comm=None):
    t = u_perm.shape[0]
    w = SCAN_WC
    rows_c = SCAN_CHUNK * n_rows
    n_chunks = t // rows_c

    assert n_chunks % 2 == 0
    last = n_chunks - 1

    def body(u_ref, dyg_ref, yt_ref, dk_ref, a_ref, bb_ref, c_ref, ein_ref, du_ref, gd_ref, da_ref, dbb_ref, dc_ref,
             xs_all, dy_s, st_a, st_b, buf_a, buf_b):
        ar = jnp.broadcast_to(a_ref[:, :w], (n_rows, w))
        ai = jnp.broadcast_to(a_ref[:, w:], (n_rows, w))
        zero = jnp.zeros((n_rows, w), F32)
        start = lambda ch: pl.multiple_of(ch * rows_c, rows_c)
        dbb_ref[...] = jnp.zeros_like(dbb_ref)
        dc_ref[...] = jnp.zeros_like(dc_ref)
        da_ref[...] = jnp.zeros_like(da_ref)

        yt = yt_ref[...]
        th = _gelu_tanh(yt)
        dgelu = 0.5 * (1.0 + th) + 0.5 * yt * (1.0 - th * th) * _GELU_C * (1.0 + 3.0 * 0.044715 * yt * yt)
        dy_all = dyg_ref[...] * dgelu
        dy_s[...] = dy_all
        gd_ref[...] = jnp.sum(dy_all * u_ref[...], axis=0, keepdims=True)
        dy_chunk = lambda ch: dy_s[pl.ds(start(ch), rows_c), :].astype(MXU_DTYPE)

        xs_all[0:n_rows, :] = ein_ref[...]

        def project(ch, stage):
            stage[...] = jnp.dot(u_ref[pl.ds(start(ch), rows_c), :].astype(MXU_DTYPE), bb_ref[...],
                                 preferred_element_type=F32)

        def fwd_steps(stage, ch, carry, xs):
            for i in range(SCAN_CHUNK):
                blk = stage[i * n_rows:(i + 1) * n_rows, :]
                carry = (ar * carry[0] - ai * carry[1] + blk[:, :w], ar * carry[1] + ai * carry[0] + blk[:, w:])
                for half, val in enumerate(carry):
                    xs[i * n_rows:(i + 1) * n_rows, half * w:(half + 1) * w] = val
                    xs_all[pl.ds(start(ch) + (i + 1) * n_rows, n_rows), half * w:(half + 1) * w] = val
            return carry

        def add_dc(xs, ch):
            dc_ref[...] += lax.dot_general(dy_chunk(ch), xs[...].astype(MXU_DTYPE), _TN, preferred_element_type=F32)

        project(0, st_a)

        def fwd_pair(p, carry):
            project(2 * p + 1, st_b)
            carry = fwd_steps(st_a, 2 * p, carry, buf_a)
            add_dc(buf_a, 2 * p)
            project(jnp.minimum(2 * p + 2, last), st_a)
            carry = fwd_steps(st_b, 2 * p + 1, carry, buf_b)
            add_dc(buf_b, 2 * p + 1)
            return carry

        lax.fori_loop(0, n_chunks // 2, fwd_pair, (ein_ref[:, :w], ein_ref[:, w:]))

        def project_dx(ch, stage):
            stage[...] = jnp.dot(dy_chunk(ch), c_ref[...], preferred_element_type=F32)

        def back_steps(stage, carry, g_buf=None):
            for i in reversed(range(SCAN_CHUNK)):
                blk = stage[i * n_rows:(i + 1) * n_rows, :]
                carry = (blk[:, :w] + ar * carry[0] + ai * carry[1], blk[:, w:] + ar * carry[1] - ai * carry[0])
                if g_buf is not None:
                    g_buf[i * n_rows:(i + 1) * n_rows, :w] = carry[0]
                    g_buf[i * n_rows:(i + 1) * n_rows, w:] = carry[1]
            return carry

        def first_pair(p, carry):
            project_dx(last - 2 * p - 1, st_b)
            carry = back_steps(st_a, carry)
            project_dx(jnp.maximum(last - 2 * p - 2, 0), st_a)
            return back_steps(st_b, carry)

        project_dx(last, st_a)
        sr, si = lax.fori_loop(0, n_chunks // 2, first_pair, (zero, zero))
        gr0, gi0 = _segment_carry(sr, si, ar, ai, n_rows, True)

        def post(g_buf, ch):
            g = g_buf[...]
            xp = xs_all[pl.ds(start(ch), rows_c), :]
            da_ref[:, :w] += jnp.sum(g[:, :w] * xp[:, :w] + g[:, w:] * xp[:, w:], axis=0, keepdims=True)
            da_ref[:, w:] += jnp.sum(g[:, w:] * xp[:, :w] - g[:, :w] * xp[:, w:], axis=0, keepdims=True)
            gb = g.astype(MXU_DTYPE)
            du_ref[pl.ds(start(ch), rows_c), :] = (lax.dot_general(gb, bb_ref[...], _NT, preferred_element_type=F32)
                                                   + dy_s[pl.ds(start(ch), rows_c), :] * dk_ref[...])
            dbb_ref[...] += lax.dot_general(u_ref[pl.ds(start(ch), rows_c), :].astype(MXU_DTYPE), gb, _TN,
                                            preferred_element_type=F32)

        def second_pair(p, carry):
            c1 = last - 2 * p
            project_dx(c1 - 1, st_b)
            post(buf_b, jnp.minimum(c1 + 1, last))
            carry = back_steps(st_a, carry, buf_a)
            project_dx(jnp.maximum(c1 - 2, 0), st_a)
            post(buf_a, c1)
            return back_steps(st_b, carry, buf_b)

        project_dx(last, st_a)
        buf_b[...] = jnp.zeros_like(buf_b)
        lax.fori_loop(0, n_chunks // 2, second_pair, (gr0, gi0))
        post(buf_b, 0)

    col = lambda width: pl.BlockSpec((t, width), lambda c, j: (0, c))
    wgt = pl.BlockSpec((SCAN_CH, 2 * w), lambda c, j: (c, 0))
    row = pl.BlockSpec((1, 2 * w), lambda c, j: (0, c))
    chan = pl.BlockSpec((1, SCAN_CH), lambda c, j: (0, c))
    return _grid_call(
        body, "ssm_bwd", (SCAN_NBLK, 1), [u_perm, dyg, ytot, dskip, a_cat, bbc, cc, ein],
        [col(SCAN_CH), col(SCAN_CH), col(SCAN_CH), chan, row, wgt, wgt,
         pl.BlockSpec((n_rows, 2 * w), lambda c, j: (0, c))],
        [col(SCAN_CH), chan, row, wgt, wgt],
        [jax.ShapeDtypeStruct((t, SSM_W), F32), jax.ShapeDtypeStruct((1, SSM_W), F32),
         jax.ShapeDtypeStruct((1, 2 * N_STATE), F32), jax.ShapeDtypeStruct((SSM_W, 2 * w), F32),
         jax.ShapeDtypeStruct((SSM_W, 2 * w), F32)],
        56 * 1024 * 1024, comm,
        scratch=[pltpu.VMEM((t + n_rows, 2 * w), F32), pltpu.VMEM((t, SCAN_CH), F32)]
        + [pltpu.VMEM((rows_c, 2 * w), F32)] * 4)


def _to_scan_rows(a, n_samples):
    c = a.shape[1]
    return a.reshape(n_samples, SCAN_SEG_PER_SAMPLE, SCAN_LEN, c).transpose(2, 0, 1, 3).reshape(-1, c)


def _from_scan_rows(a, n_samples):
    c = a.shape[1]
    return a.reshape(SCAN_LEN, n_samples, SCAN_SEG_PER_SAMPLE, c).transpose(1, 2, 0, 3).reshape(-1, c)


def _row_spec(tm, width):
    return pl.BlockSpec((tm, width), lambda i, j: (i, 0))


def _whole(arr):
    return pl.BlockSpec(arr.shape, lambda i, j: (0,) * arr.ndim)


def _proj_rope(x, g, w_in_t, tabs, comm=None):
    t = x.shape[0]
    tm = 256

    def body(x_ref, g_ref, w_ref, tc_ref, tlo_ref, thi_ref, h_ref, u_ref, gate_ref, *rest):
        qkv_refs, stage = rest[:9], rest[9]
        xv = x_ref[...]
        r = lax.rsqrt(jnp.mean(xv * xv, axis=-1, keepdims=True) + RMS_EPS)
        h = ((xv * r) * g_ref[...]).astype(BF16)
        h_ref[...] = h
        p = lax.dot_general(h.astype(MXU_DTYPE), w_ref[...], _NT, preferred_element_type=F32)
        u_ref[...] = p[:, QKV_W:QKV_W + SSM_W]
        gate_ref[...] = _sigmoid(p[:, QKV_W + SSM_W:])
        tc, tlo, thi = tc_ref[...], tlo_ref[...], thi_ref[...]
        n_ch = QKV_W // LANES
        for ch in range(n_ch):
            piece = p[:, _lane_chunk(ch)]
            stage[ch] = _rope_apply(piece, tc, tlo, thi) if ch < 2 * n_ch // 3 else piece
        halves = GROUP_W // LANES
        for grp, d in enumerate(DILATIONS):
            for which in range(3):
                out = qkv_refs[3 * grp + which]
                for res in range(d):
                    for half in range(halves):
                        ch = which * (n_ch // 3) + grp * halves + half
                        out[:, _lane_chunk(res * halves + half)] = _gather_residue(stage, ch, res, d, tm // d).astype(BF16)

    tab = pl.BlockSpec((tm, LANES), lambda i, j: (i % (SEQ // tm), 0))
    widths = [(D_MODEL, BF16), (SSM_W, F32), (2 * D_MODEL, F32)]
    out_specs = [_row_spec(tm, wd) for wd, _ in widths]
    out_shapes = [jax.ShapeDtypeStruct((t, wd), dt) for wd, dt in widths]
    for d in DILATIONS:
        out_specs += [_row_spec(tm // d, d * GROUP_W)] * 3
        out_shapes += [jax.ShapeDtypeStruct((t // d, d * GROUP_W), BF16)] * 3
    return _grid_call(
        body, "proj_rope", (t // tm, 1), [x, g, w_in_t, *tabs],
        [_row_spec(tm, D_MODEL), _whole(g), _whole(w_in_t), tab, tab, tab], out_specs, out_shapes, VMEM_BIG, comm,
        scratch=[pltpu.VMEM((QKV_W // LANES, tm, LANES), F32)])


def _branch_outputs(attn_ref, yg_ref, wao_ref, wglu_ref):
    attn_d = lax.dot_general(attn_ref[...].astype(MXU_DTYPE), wao_ref[...], _NT, preferred_element_type=F32)
    z = lax.dot_general(yg_ref[...].astype(MXU_DTYPE), wglu_ref[...], _NT, preferred_element_type=F32)
    return attn_d, z[:, :D_MODEL], _sigmoid(z[:, D_MODEL:])


def _mix_out_rms(os_, lses, yg, gates, x, w_ao_t, w_glu_t, w_out, g, comm=None):
    t = x.shape[0]
    tm = 256

    def body(o0, o1, o2, l0, l1, l2, yg_ref, gate_ref, x_ref, wao_ref, wglu_ref, wout_ref, g_ref,
             attn_ref, lt_ref, m_ref, x1_ref, h_ref, nat):
        _merge_groups((o0, o1, o2), (l0, l1, l2), attn_ref, lt_ref, nat, tm)
        attn_d, za, sb = _branch_outputs(attn_ref, yg_ref, wao_ref, wglu_ref)
        merged = (gate_ref[:, :D_MODEL] * attn_d + gate_ref[:, D_MODEL:] * (za * sb)).astype(BF16)
        m_ref[...] = merged
        x1 = x_ref[...] + jnp.dot(merged.astype(MXU_DTYPE), wout_ref[...], preferred_element_type=F32)
        x1_ref[...] = x1
        r = lax.rsqrt(jnp.mean(x1 * x1, axis=-1, keepdims=True) + RMS_EPS)
        h_ref[...] = ((x1 * r) * g_ref[...]).astype(BF16)

    dil_specs = [_row_spec(tm // d, d * GROUP_W) for d in DILATIONS] * 2
    return _grid_call(
        body, "mix_out_rms", (t // tm, 1), [*os_, *lses, yg, gates, x, w_ao_t, w_glu_t, w_out, g],
        dil_specs + [_row_spec(tm, SSM_W), _row_spec(tm, 2 * D_MODEL), _row_spec(tm, D_MODEL),
                     _whole(w_ao_t), _whole(w_glu_t), _whole(w_out), _whole(g)],
        [_row_spec(tm, GROUP_W)] * 2 + [_row_spec(tm, D_MODEL)] * 3,
        [jax.ShapeDtypeStruct((t, GROUP_W), F32)] * 2
        + [jax.ShapeDtypeStruct((t, D_MODEL), BF16), jax.ShapeDtypeStruct((t, D_MODEL), F32),
           jax.ShapeDtypeStruct((t, D_MODEL), BF16)], VMEM_BIG, comm, scratch=[pltpu.VMEM((8, tm, LANES), F32)])


def _mix_bwd(dx1b, attn, lse_tot, yg, gates, w_ao_t, w_glu_t, w_out, comm=None):
    t = dx1b.shape[0]
    tm = 256

    def body(dx_ref, attn_ref, lt_ref, yg_ref, gate_ref, wao_ref, wglu_ref, wout_ref, ones_ref,
             dad_ref, dz_ref, dg_ref, da_ref, dyg_ref, rd_ref, *rest):
        dm = lax.dot_general(dx_ref[...], wout_ref[...], _NT, preferred_element_type=F32)
        attn_d, za, sb = _branch_outputs(attn_ref, yg_ref, wao_ref, wglu_ref)
        g0, g1 = gate_ref[:, :D_MODEL], gate_ref[:, D_MODEL:]
        dad = (dm * g0).astype(BF16)
        dad_ref[...] = dad
        ds = dm * g1
        dza, dzb = (ds * sb).astype(BF16), (ds * za * sb * (1.0 - sb)).astype(BF16)
        dz_ref[:, :D_MODEL] = dza
        dz_ref[:, D_MODEL:] = dzb
        dg_ref[:, :D_MODEL] = (dm * attn_d * g0 * (1.0 - g0)).astype(BF16)
        dg_ref[:, D_MODEL:] = (dm * (za * sb) * g1 * (1.0 - g1)).astype(BF16)
        da = jnp.dot(dad.astype(MXU_DTYPE), wao_ref[...], preferred_element_type=F32)
        da_ref[...] = da
        dyg_ref[...] = (jnp.dot(dza.astype(MXU_DTYPE), wglu_ref[:D_MODEL, :], preferred_element_type=F32)
                        + jnp.dot(dzb.astype(MXU_DTYPE), wglu_ref[D_MODEL:, :], preferred_element_type=F32))
        _attention_cotangents(da, attn_ref[...], lt_ref[...], ones_ref[...], rd_ref, rest[:6], rest[6], tm)

    widths = [(D_MODEL, BF16), (2 * D_MODEL, BF16), (2 * D_MODEL, BF16), (GROUP_W, F32), (SSM_W, F32), (GROUP_W, F32)]
    out_specs = [_row_spec(tm, wd) for wd, _ in widths]
    out_shapes = [jax.ShapeDtypeStruct((t, wd), dt) for wd, dt in widths]
    for d in DILATIONS[1:]:
        out_specs += [_row_spec(tm // d, d * GROUP_W)] * 3
        out_shapes += [jax.ShapeDtypeStruct((t // d, d * GROUP_W), F32)] * 3
    ones = _head_sum_matrix()
    return _grid_call(
        body, "mix_bwd", (t // tm, 1), [dx1b, attn, lse_tot, yg, gates, w_ao_t, w_glu_t, w_out, ones],
        [_row_spec(tm, D_MODEL), _row_spec(tm, GROUP_W), _row_spec(tm, GROUP_W), _row_spec(tm, SSM_W),
         _row_spec(tm, 2 * D_MODEL), _whole(w_ao_t), _whole(w_glu_t), _whole(w_out), _whole(ones)],
        out_specs, out_shapes, VMEM_BIG, comm, scratch=[pltpu.VMEM((6, tm, LANES), F32)])


FFN_TN = D_FF // 2
MXU_COLS = 256


def _ffn_in_swiglu(h2, w_gate_t, w_up_t, comm=None):
    t = h2.shape[0]
    tm = 512

    def body(h_ref, wg_ref, wu_ref, a_ref, b_ref, f_ref):
        h = h_ref[...].astype(MXU_DTYPE)
        for c0 in range(0, FFN_TN, MXU_COLS):
            sl = slice(c0, min(c0 + MXU_COLS, FFN_TN))
            a = lax.dot_general(h, wg_ref[sl, :], _NT, preferred_element_type=F32)
            b = lax.dot_general(h, wu_ref[sl, :], _NT, preferred_element_type=F32)
            a_ref[:, sl] = a
            b_ref[:, sl] = b
            f_ref[:, sl] = (a * _sigmoid(a) * b).astype(BF16)

    tile = pl.BlockSpec((tm, FFN_TN), lambda j, i: (i, j))
    wspec = pl.BlockSpec((FFN_TN, D_MODEL), lambda j, i: (j, 0))
    return _grid_call(
        body, "ffn_in_swiglu", (D_FF // FFN_TN, t // tm), [h2, w_gate_t, w_up_t],
        [pl.BlockSpec((tm, D_MODEL), lambda j, i: (i, 0)), wspec, wspec],
        [tile] * 3, [jax.ShapeDtypeStruct((t, D_FF), F32)] * 2 + [jax.ShapeDtypeStruct((t, D_FF), BF16)], VMEM_BIG, comm)


def _ffn_down_final(f, w_down, x1, target, g):
    t = x1.shape[0]
    tm = 256

    def body(f_ref, w_ref, x1_ref, t_ref, g_ref, dx_ref, dxb_ref, loss_ref, gg_ref):
        @pl.when(pl.program_id(0) == 0)
        def _():
            loss_ref[...] = jnp.zeros_like(loss_ref)
            gg_ref[...] = jnp.zeros_like(gg_ref)

        xv = x1_ref[...] + jnp.dot(f_ref[...].astype(MXU_DTYPE), w_ref[...], preferred_element_type=F32)
        gv = g_ref[...]
        r = lax.rsqrt(jnp.mean(xv * xv, axis=-1, keepdims=True) + RMS_EPS)
        n = xv * r
        diff = n * gv - t_ref[...]
        per_tok = jnp.mean(diff * diff, axis=-1, keepdims=True)
        loss_ref[...] += 0.5 * jnp.sum(per_tok, axis=0, keepdims=True)
        dy = diff / xv.shape[-1]
        gg_ref[...] += jnp.sum(dy * n, axis=0, keepdims=True)
        dn = dy * gv
        dx = r * (dn - n * jnp.mean(dn * n, axis=-1, keepdims=True))
        dx_ref[...] = dx
        dxb_ref[...] = dx.astype(BF16)

    acc = lambda shp: pl.BlockSpec(shp, lambda i, j: (0, 0))
    return _grid_call(
        body, "ffn_down_final", (t // tm, 1), [f, w_down, x1, target, g],
        [_row_spec(tm, D_FF), _whole(w_down), _row_spec(tm, D_MODEL), _row_spec(tm, D_MODEL), _whole(g)],
        [_row_spec(tm, D_MODEL)] * 2 + [acc((8, LANES)), acc((1, D_MODEL))],
        [jax.ShapeDtypeStruct((t, D_MODEL), F32), jax.ShapeDtypeStruct((t, D_MODEL), BF16),
         jax.ShapeDtypeStruct((8, LANES), F32), jax.ShapeDtypeStruct((1, D_MODEL), F32)], VMEM_BIG, sequential=True)


def _d_f_swiglu_bwd(dx2b, w_down, a, b):
    t = a.shape[0]
    tm = 512

    def body(dx_ref, w_ref, a_ref, b_ref, da_ref, db_ref):
        d = lax.dot_general(dx_ref[...], w_ref[...], _NT, preferred_element_type=F32)
        av, bv = a_ref[...], b_ref[...]
        sg = _sigmoid(av)
        da_ref[...] = (d * bv * sg * (1.0 + av * (1.0 - sg))).astype(BF16)
        db_ref[...] = (d * av * sg).astype(BF16)

    tile = pl.BlockSpec((tm, FFN_TN), lambda j, i: (i, j))
    return _grid_call(
        body, "d_f_swiglu_bwd", (D_FF // FFN_TN, t // tm), [dx2b, w_down, a, b],
        [pl.BlockSpec((tm, D_MODEL), lambda j, i: (i, 0)), pl.BlockSpec((FFN_TN, D_MODEL), lambda j, i: (j, 0)), tile, tile],
        [tile] * 2, [jax.ShapeDtypeStruct((t, D_FF), BF16)] * 2, VMEM_BIG)


def _mm_rms_bwd(operands, weights, x, g, dres, name, comm=None):
    t = x.shape[0]
    tm = 256
    n_op = len(operands)

    def body(*refs):
        a_refs, w_refs = refs[:n_op], refs[n_op:2 * n_op]
        x_ref, g_ref, dres_ref, dx_ref, dxb_ref, gg_ref = refs[2 * n_op:]

        @pl.when(pl.program_id(0) == 0)
        def _():
            gg_ref[...] = jnp.zeros_like(gg_ref)

        dh = None
        for a_ref, w_ref in zip(a_refs, w_refs):
            part = jnp.dot(a_ref[...].astype(MXU_DTYPE), w_ref[...], preferred_element_type=F32)
            dh = part if dh is None else dh + part
        xv = x_ref[...]
        r = lax.rsqrt(jnp.mean(xv * xv, axis=-1, keepdims=True) + RMS_EPS)
        n = xv * r
        gg_ref[...] += jnp.sum(dh * n, axis=0, keepdims=True)
        dn = dh * g_ref[...]
        dx = dres_ref[...] + r * (dn - n * jnp.mean(dn * n, axis=-1, keepdims=True))
        dx_ref[...] = dx
        dxb_ref[...] = dx.astype(BF16)

    d = x.shape[1]
    return _grid_call(
        body, name, (t // tm, 1), [*operands, *weights, x, g, dres],
        [_row_spec(tm, a.shape[1]) for a in operands] + [_whole(wk) for wk in weights]
        + [_row_spec(tm, d), _whole(g), _row_spec(tm, d)],
        [_row_spec(tm, d)] * 2 + [pl.BlockSpec((1, d), lambda i, j: (0, 0))],
        [jax.ShapeDtypeStruct((t, d), F32), jax.ShapeDtypeStruct((t, d), BF16), jax.ShapeDtypeStruct((1, d), F32)],
        VMEM_BIG, comm, sequential=True)


def _flat_small(small):
    perm_b = lambda a: a.reshape(SSM_GROUPS, SSM_STATE, SSM_CH).transpose(2, 0, 1).reshape(SSM_CH, N_STATE)
    perm_c = lambda a: a.reshape(SSM_GROUPS, SSM_CH, SSM_STATE).transpose(1, 0, 2).reshape(SSM_CH, N_STATE)
    return dict(
        g_mix=small["norm_mix_g"].reshape(1, D_MODEL), g_ffn=small["norm_ffn_g"].reshape(1, D_MODEL),
        g_fin=small["norm_final_g"].reshape(1, D_MODEL),
        lr=small["ssm_a_re"].reshape(1, N_STATE), li=small["ssm_a_im"].reshape(1, N_STATE),
        ldt=jnp.repeat(small["ssm_log_dt"].reshape(SSM_GROUPS), SSM_STATE).reshape(1, N_STATE),
        br=perm_b(small["ssm_b_re"]), bi=perm_b(small["ssm_b_im"]),
        cr=perm_c(small["ssm_c_re"]), ci=perm_c(small["ssm_c_im"]), dskip=small["ssm_d"].reshape(1, SSM_W))


AG_HOSTS = {"proj_rope": ("w_glu", "w_attn_out", "w_out", "w_ffn_gate"), "mix_out_rms": ("w_ffn_up",),
            "ffn_in_swiglu": ("w_ffn_down",)}
HALVED = ("w_ffn_gate", "w_ffn_up", "w_in")
QUARTERED = ("w_ffn_down",)
A2A_HOSTS = {"mm_g_gate0": ("w_ffn_down:q0",), "mm_g_up0": ("w_ffn_down:q1",), "mm_g_gate1": ("w_ffn_down:q2",),
             "mm_g_up1": ("w_ffn_down:q3",), "d_h2_rms": ("w_ffn_gate:0",), "mix_bwd": ("w_ffn_gate:1",),
             "attn_bwd_g0": ("w_out",), "attn_bwd_g1": ("w_glu",), "attn_bwd_g2": ("w_attn_out",),
             "ssm_bwd": ("w_ffn_up:0", "w_ffn_up:1"), "mm_g_in1": ("w_in:0",), "d_h0_rms": ("w_in:1",)}
SMALL_HOST = "mm_g_in0"


def _local_step(x, target, w, small, shards=None):
    t = x.shape[0]
    n_samples = t // SEQ
    n_rows = n_samples * SCAN_SEG_PER_SAMPLE
    tabs = _rope_tables()
    w = dict(w)
    fs = _flat_small(small)
    g_mix, g_ffn, g_fin, dskip = fs["g_mix"], fs["g_ffn"], fs["g_fin"], fs["dskip"]
    a_cat, bbc, cc = _ssm_disc(fs["lr"], fs["li"], fs["ldt"], fs["br"], fs["bi"], fs["cr"], fs["ci"])
    big, recv, small_pack = {}, {}, []

    def comm_of(name):
        if shards is None:
            return None
        if name == SMALL_HOST:
            return _ag_comm([(small_pack[0], 0, 0)], [(N_DEV, *small_pack[0].shape)])
        if name in AG_HOSTS:
            names = AG_HOSTS[name]
            return _ag_comm([(shards[n], j, 0) for j, n in enumerate(names)], [(N_DEV, *shards[n].shape) for n in names])
        if name in A2A_HOSTS:
            items = []
            for n in A2A_HOSTS[name]:
                base, _, quarter = n.partition(":q")
                grad = big[base]
                cols = (int(quarter) * (grad.shape[1] // 4), grad.shape[1] // 4) if quarter else None
                items.append((grad.reshape(N_DEV, -1, grad.shape[1]), cols))
            return _a2a_comm(items)
        return None

    def absorb(name, carried):
        if name == SMALL_HOST:
            recv["small"] = carried[0]
        for n, a3 in zip(AG_HOSTS.get(name, ()), carried):
            w[n] = a3.reshape(-1, a3.shape[2])
        for n, a3 in zip(A2A_HOSTS.get(name, ()), carried):
            recv[n] = a3

    def mm(a, b, mode, name, tm, tn, **kw):
        comm = comm_of(name)
        if comm is None:
            return _mm(a, b, mode, name, tm, tn, **kw)
        out, *carried = _mm(a, b, mode, name, tm, tn, comm=comm, **kw)
        absorb(name, carried)
        return out

    h0, u, gates, *rest = _proj_rope(x, g_mix, w["w_in"], tabs, comm_of("proj_rope"))
    qkv = [rest[3 * g:3 * g + 3] for g in range(3)]
    absorb("proj_rope", rest[9:])
    os_, lses = [], []
    for g in range(3):
        o_g, l_g, carried = _attn_fwd(*qkv[g], g, n_samples, comm_of(f"attn_fwd_g{g}"))
        absorb(f"attn_fwd_g{g}", carried)
        os_.append(o_g)
        lses.append(l_g)
    u_perm = _to_scan_rows(u, n_samples)
    ytot, yg_perm, ein = _ssm_fwd(u_perm, a_cat, bbc, cc, dskip, n_rows)
    yg = _from_scan_rows(yg_perm, n_samples)

    attn, lse_tot, merged, x1, h2, *carried = _mix_out_rms(os_, lses, yg, gates, x, w["w_attn_out"], w["w_glu"], w["w_out"],
                                                           g_ffn, comm_of("mix_out_rms"))
    absorb("mix_out_rms", carried)
    ffn_a, ffn_b, f, *carried = _ffn_in_swiglu(h2, w["w_ffn_gate"], w["w_ffn_up"], comm_of("ffn_in_swiglu"))
    absorb("ffn_in_swiglu", carried)
    dx2, dx2b, loss_blk, g_gfin = _ffn_down_final(f, w["w_ffn_down"], x1, target, g_fin)

    da, db = _d_f_swiglu_bwd(dx2b, w["w_ffn_down"], ffn_a, ffn_b)
    big["w_ffn_down"] = mm(f, dx2b, "tn", "mm_g_down", 256, D_MODEL, out_dtype=BF16)
    half = D_MODEL // 2
    for hf in range(2):
        big[f"w_ffn_gate:{hf}"] = mm(da, h2, "tn", f"mm_g_gate{hf}", 256, half, out_dtype=BF16, cols=(hf * half, half))
        big[f"w_ffn_up:{hf}"] = mm(db, h2, "tn", f"mm_g_up{hf}", 256, half, out_dtype=BF16, cols=(hf * half, half))
    dx1, dx1b, g_gffn, *carried = _mm_rms_bwd([da, db], [w["w_ffn_gate"], w["w_ffn_up"]], x1, g_ffn, dx2, "d_h2_rms",
                                              comm_of("d_h2_rms"))
    absorb("d_h2_rms", carried)

    big["w_out"] = mm(merged, dx1b, "tn", "mm_g_out", 256, D_MODEL, out_dtype=BF16)
    dattn_d, dz, dgpre, dattn, dyg, rowdot, *rest = _mix_bwd(dx1b, attn, lse_tot, yg, gates, w["w_attn_out"], w["w_glu"],
                                                             w["w_out"], comm_of("mix_bwd"))
    cot = [(dattn, lse_tot, rowdot), tuple(rest[:3]), tuple(rest[3:6])]
    absorb("mix_bwd", rest[6:])

    big["w_attn_out"] = mm(dattn_d, attn, "tn", "mm_g_attn_out", 512, GROUP_W, out_dtype=BF16)
    big["w_glu"] = mm(dz, yg, "tn", "mm_g_glu", 512, 512, out_dtype=BF16)
    dqs, dks, dvs = [], [], []
    for g in range(3):
        dq_g, dk_g, dv_g, carried = _attn_bwd(*qkv[g], *cot[g], g, n_samples, comm_of(f"attn_bwd_g{g}"))
        absorb(f"attn_bwd_g{g}", carried)
        dqs.append(dq_g)
        dks.append(dk_g)
        dvs.append(dv_g)

    dyg_perm = _to_scan_rows(dyg, n_samples)
    du_perm, g_dskip, da_cat, dbb_full, dc_full, *carried = _ssm_bwd(u_perm, dyg_perm, ytot, dskip, a_cat, bbc, cc, ein,
                                                                   n_rows, comm_of("ssm_bwd"))
    absorb("ssm_bwd", carried)
    du = _from_scan_rows(du_perm, n_samples)
    g_lr, g_li, g_ldt, g_br, g_bi, g_cr, g_ci = _ssm_param_bwd(
        fs["lr"], fs["li"], fs["ldt"], fs["br"], fs["bi"], da_cat, dbb_full, dc_full)

    small_pack.append(_pack_small(dict(lr=g_lr, li=g_li, ldt=g_ldt, br=g_br, bi=g_bi, cr=g_cr, ci=g_ci, dskip=g_dskip,
                                       g_ffn=g_gffn, g_fin=g_gfin, loss=loss_blk)))

    dproj = _pack_dproj(dqs, dks, dvs, du, dgpre, tabs)
    for hf in range(2):
        big[f"w_in:{hf}"] = mm(dproj, h0, "tn", f"mm_g_in{hf}", 256, half, out_dtype=BF16, cols=(hf * half, half))
    grad_x, _, g_gmix, *carried = _mm_rms_bwd([dproj], [w["w_in"]], x, g_mix, dx1, "d_h0_rms", comm_of("d_h0_rms"))
    absorb("d_h0_rms", carried)
    return grad_x, (big if shards is None else recv), small_pack[0], g_gmix


_MESH = pl.DeviceIdType.MESH


def _all_gather(block, name):
    rows, lanes = block.shape

    def body(x_ref, out_ref, send_sems, recv_sems, local_sem):
        x, y, c = lax.axis_index("x"), lax.axis_index("y"), lax.axis_index("c")
        me, sibling = (x, y, c), (x, y, 1 - c)
        chips = [(1 - x, y), (x, 1 - y), (1 - x, 1 - y)]

        def slot(px, py, pc):
            return out_ref.at[4 * px + 2 * py + pc]

        def copy(k, blk, to, src=None):
            return pltpu.make_async_remote_copy(
                src_ref=slot(*blk) if src is None else src, dst_ref=slot(*blk), send_sem=send_sems.at[k],
                recv_sem=recv_sems.at[k], device_id=to, device_id_type=_MESH)

        mine = pltpu.make_async_copy(x_ref, slot(*me), local_sem)
        mine.start()
        first = [copy(0, me, sibling, src=x_ref)]
        first += [copy(1 + j, me, (*chip, c), src=x_ref) for j, chip in enumerate(chips)]
        for cp in first:
            cp.start()
        passed = [copy(4 + j, (*chip, c), sibling) for j, chip in enumerate(chips)]
        for j, chip in enumerate(chips):
            copy(1 + j, (*chip, c), me).wait_recv()
            passed[j].start()
        copy(0, sibling, me).wait_recv()
        for j, chip in enumerate(chips):
            copy(4 + j, (*chip, 1 - c), me).wait_recv()
        for cp in first + passed:
            cp.wait_send()
        mine.wait()

    return _pallas_call(
        body, name=name, out_shape=jax.ShapeDtypeStruct((N_DEV, rows, lanes), block.dtype),
        in_specs=[pl.BlockSpec(memory_space=pl.ANY)], out_specs=pl.BlockSpec(memory_space=pl.ANY),
        scratch_shapes=[pltpu.SemaphoreType.DMA((7,)), pltpu.SemaphoreType.DMA((7,)), pltpu.SemaphoreType.DMA],
    )(block)


def _ag_comm(items, bufs):
    def plan(in_refs, out_refs, send_sems, recv_sems, local_sems):
        x, y, c = lax.axis_index("x"), lax.axis_index("y"), lax.axis_index("c")
        me, sibling = (x, y, c), (x, y, 1 - c)
        chips = [(1 - x, y), (x, 1 - y), (1 - x, 1 - y)]
        plans = []
        for t, (_, buf, slot0) in enumerate(items):
            x_ref, out_ref = in_refs[t], out_refs[buf]

            def slot(px, py, pc, out_ref=out_ref, slot0=slot0):
                return out_ref.at[slot0 + 4 * px + 2 * py + pc]

            def copy(k, blk, to, src=None, t=t, slot=slot):
                return pltpu.make_async_remote_copy(
                    src_ref=slot(*blk) if src is None else src, dst_ref=slot(*blk), send_sem=send_sems.at[7 * t + k],
                    recv_sem=recv_sems.at[7 * t + k], device_id=to, device_id_type=_MESH)

            plans.append(dict(
                mine=pltpu.make_async_copy(x_ref, slot(*me), local_sems.at[t]),
                first=[copy(0, me, sibling, src=x_ref)] + [copy(1 + j, me, (*chip, c), src=x_ref)
                                                           for j, chip in enumerate(chips)],
                passed=[copy(4 + j, (*chip, c), sibling) for j, chip in enumerate(chips)],
                from_ici=[copy(1 + j, (*chip, c), me) for j, chip in enumerate(chips)],
                from_sibling=[copy(0, sibling, me)] + [copy(4 + j, (*chip, 1 - c), me) for j, chip in enumerate(chips)]))
        return plans

    def start(*refs):
        for p in plan(*refs):
            p["mine"].start()
            for cp in p["first"]:
                cp.start()

    def finish(*refs):
        plans = plan(*refs)
        for p in plans:
            for arrived, onward in zip(p["from_ici"], p["passed"]):
                arrived.wait_recv()
                onward.start()
        for p in plans:
            for arrived in p["from_sibling"]:
                arrived.wait_recv()
            for cp in p["first"] + p["passed"]:
                cp.wait_send()
            p["mine"].wait()

    dtype_of = {buf: shard.dtype for shard, buf, _ in items}
    out_shapes = [jax.ShapeDtypeStruct(b, dtype_of[j]) for j, b in enumerate(bufs)]
    return _Comm([it[0] for it in items], out_shapes, 7 * len(items), len(items), start, finish)


def _a2a_comm(items):
    def plan(in_refs, out_refs, send_sems, recv_sems, local_sems):
        x, y, c = lax.axis_index("x"), lax.axis_index("y"), lax.axis_index("c")
        my = 4 * x + 2 * y + c
        copies, locals_ = [], []
        for t, (_, cols) in enumerate(items):
            s_ref, r_ref = in_refs[t], out_refs[t]
            src = (lambda p, s_ref=s_ref: s_ref.at[p]) if cols is None else (
                lambda p, s_ref=s_ref, cols=cols: s_ref.at[p, :, pl.ds(cols[0], cols[1])])
            locals_.append(pltpu.make_async_copy(src(my), r_ref.at[my], local_sems.at[t]))
            for kk in range(1, N_DEV):
                px = 1 - x if kk & 4 else x
                py = 1 - y if kk & 2 else y
                pc = 1 - c if kk & 1 else c
                copies.append(pltpu.make_async_remote_copy(
                    src_ref=src(4 * px + 2 * py + pc), dst_ref=r_ref.at[my],
                    send_sem=send_sems.at[7 * t + kk - 1], recv_sem=recv_sems.at[7 * t + kk - 1],
                    device_id=(px, py, pc), device_id_type=_MESH))
        return copies, locals_

    def start(*refs):
        copies, locals_ = plan(*refs)
        for cp in locals_ + copies:
            cp.start()

    def finish(*refs):
        copies, locals_ = plan(*refs)
        for cp in copies + locals_:
            cp.wait()

    out_shapes = [jax.ShapeDtypeStruct((N_DEV, arr.shape[1], cols[1] if cols else arr.shape[2]), arr.dtype)
                  for arr, cols in items]
    return _Comm([it[0] for it in items], out_shapes, 7 * len(items), len(items), start, finish)


def _adam_math(g, w, m, v):
    m_new = ADAM_B1 * m + (1.0 - ADAM_B1) * g
    v_new = ADAM_B2 * v + (1.0 - ADAM_B2) * jnp.square(g)
    m_hat = m_new / (1.0 - ADAM_B1 ** ADAM_STEP)
    v_hat = v_new / (1.0 - ADAM_B2 ** ADAM_STEP)
    return -ADAM_LR * (m_hat / (jnp.sqrt(v_hat) + ADAM_EPS) + ADAM_WD * w), m_new, v_new


def _sum_partials(parts, name, tm):
    n, rows, _ = parts[0].shape
    widths = [p.shape[2] for p in parts]

    def body(*refs):
        g_ref, off = refs[-1], 0
        for p_ref, wd in zip(refs[:-1], widths):
            g = p_ref[0].astype(F32)
            for s in range(1, n):
                g = g + p_ref[s].astype(F32)
            g_ref[:, off:off + wd] = g
            off += wd

    return _pallas_call(
        body, name=name, grid=(rows // tm,), in_specs=[pl.BlockSpec((n, tm, wd), lambda i: (0, i, 0)) for wd in widths],
        out_specs=pl.BlockSpec((tm, sum(widths)), lambda i: (i, 0)),
        out_shape=jax.ShapeDtypeStruct((rows, sum(widths)), F32),
        compiler_params=pltpu.CompilerParams(dimension_semantics=("parallel",), vmem_limit_bytes=VMEM_MID),
    )(*parts)


def _adam(parts, w, m, v, name, tm):
    n, rows, _ = parts[0].shape
    widths = [p.shape[2] for p in parts]
    cols = sum(widths)

    def body(*refs):
        p_refs, (w_ref, m_ref, v_ref, g_ref, d_ref, nm_ref, nv_ref) = refs[:len(parts)], refs[len(parts):]
        off = 0
        for p_ref, wd in zip(p_refs, widths):
            g = p_ref[0].astype(F32)
            for s in range(1, n):
                g = g + p_ref[s].astype(F32)
            sl = slice(off, off + wd)
            g_ref[:, sl] = g
            d_ref[:, sl], nm_ref[:, sl], nv_ref[:, sl] = _adam_math(g, w_ref[:, sl], m_ref[:, sl], v_ref[:, sl])
            off += wd

    assert rows % tm == 0
    row = pl.BlockSpec((tm, cols), lambda i: (i, 0))
    shp = jax.ShapeDtypeStruct((rows, cols), F32)
    return _pallas_call(
        body, name=name, grid=(rows // tm,),
        in_specs=[pl.BlockSpec((n, tm, wd), lambda i: (0, i, 0)) for wd in widths] + [row, row, row],
        out_specs=[row] * 4, out_shape=[shp] * 4,
        compiler_params=pltpu.CompilerParams(dimension_semantics=("parallel",), vmem_limit_bytes=VMEM_MID),
    )(*parts, w, m, v)


_PK_LR, _PK_LI, _PK_GAINS, _PK_MISC, _PK_BR, _PK_BI, _PK_CR, _PK_CI, _PK_ROWS = 0, 1, 2, 3, 8, 24, 40, 56, 72
_PK_LDT_LANE, _PK_LOSS_LANE = D_MODEL + SSM_W, D_MODEL + SSM_W + LANES


def _pack_small(sg):
    names = ("lr", "li", "g_ffn", "g_fin", "dskip", "ldt", "loss", "br", "bi", "cr", "ci")

    def body(lr, li, gffn, gfin, dskip, ldt, loss, br, bi, cr, ci, o_ref):
        o_ref[...] = jnp.zeros_like(o_ref)
        o_ref[_PK_LR:_PK_LR + 1, :] = lr[...]
        o_ref[_PK_LI:_PK_LI + 1, :] = li[...]
        o_ref[_PK_GAINS:_PK_GAINS + 1, D_MODEL:] = gffn[...]
        o_ref[_PK_MISC:_PK_MISC + 1, :D_MODEL] = gfin[...]
        o_ref[_PK_MISC:_PK_MISC + 1, D_MODEL:D_MODEL + SSM_W] = dskip[...]
        o_ref[_PK_MISC:_PK_MISC + 1, _PK_LDT_LANE:_PK_LDT_LANE + LANES] = ldt[0:1, :]
        o_ref[_PK_MISC:_PK_MISC + 1, _PK_LOSS_LANE:_PK_LOSS_LANE + LANES] = loss[0:1, :]
        o_ref[_PK_BR:_PK_BR + SSM_CH, :] = br[...]
        o_ref[_PK_BI:_PK_BI + SSM_CH, :] = bi[...]
        o_ref[_PK_CR:_PK_CR + SSM_CH, :] = cr[...]
        o_ref[_PK_CI:_PK_CI + SSM_CH, :] = ci[...]

    return _pallas_call(body, name="pack_small", out_shape=jax.ShapeDtypeStruct((_PK_ROWS, N_STATE), F32))(
        *[sg[n] for n in names])


def _unpack_small(s, g_mix):
    unflat_b = unflat_c = lambda a: a.reshape(SSM_CH, SSM_GROUPS, SSM_STATE).transpose(1, 0, 2)[None]
    grads = {
        "norm_mix_g": g_mix, "norm_ffn_g": s[_PK_GAINS, D_MODEL:].reshape(1, D_MODEL),
        "norm_final_g": s[_PK_MISC, :D_MODEL].reshape(1, D_MODEL),
        "ssm_a_re": s[_PK_LR].reshape(1, SSM_GROUPS, SSM_STATE), "ssm_a_im": s[_PK_LI].reshape(1, SSM_GROUPS, SSM_STATE),
        "ssm_log_dt": s[_PK_MISC, _PK_LDT_LANE:_PK_LDT_LANE + SSM_GROUPS].reshape(1, SSM_GROUPS),
        "ssm_d": s[_PK_MISC, D_MODEL:D_MODEL + SSM_W].reshape(1, SSM_GROUPS, SSM_CH),
        "ssm_b_re": unflat_b(s[_PK_BR:_PK_BR + SSM_CH]), "ssm_b_im": unflat_b(s[_PK_BI:_PK_BI + SSM_CH]),
        "ssm_c_re": unflat_c(s[_PK_CR:_PK_CR + SSM_CH]), "ssm_c_im": unflat_c(s[_PK_CI:_PK_CI + SSM_CH]),
    }
    return s[_PK_MISC, _PK_LOSS_LANE], grads


def _stored(name, a):
    if name in ("ssm_b_re", "ssm_b_im"):
        return a.transpose(0, 1, 3, 2)
    return a.reshape(1, -1) if a.ndim == 1 else a


def _unstored(name, a, like):
    return a.transpose(0, 1, 3, 2) if name in ("ssm_b_re", "ssm_b_im") else a.reshape(like.shape)


def _adam_small(grads, wts, moms, vars_):
    n = len(SMALL_WEIGHTS)

    def body(*refs):
        ins, outs = refs[:4 * n], refs[4 * n:]
        for i in range(n):
            g, w, m, v = (ins[j * n + i][...] for j in range(4))
            outs[i][...], outs[n + i][...], outs[2 * n + i][...] = _adam_math(g, w, m, v)

    operands = [grads[k] if d is grads else _stored(k, d[k]) for d in (grads, wts, moms, vars_) for k in SMALL_WEIGHTS]
    shapes = [jax.ShapeDtypeStruct(_stored(k, wts[k]).shape, F32) for k in SMALL_WEIGHTS] * 3
    res = _pallas_call(body, name="adam_small", out_shape=shapes,
                         compiler_params=pltpu.CompilerParams(vmem_limit_bytes=VMEM_BIG))(*operands)
    out = {}
    for j, kind in enumerate(("delta", "new_m", "new_v")):
        for i, k in enumerate(SMALL_WEIGHTS):
            out[kind, k] = _unstored(k, res[j * n + i], wts[k])
    return out


def kernel(x, norm_mix_g, w_in, ssm_a_re, ssm_a_im, ssm_log_dt, ssm_b_re, ssm_b_im, ssm_c_re, ssm_c_im, ssm_d, w_glu, w_attn_out, w_out, norm_ffn_g, w_ffn_gate, w_ffn_up, w_ffn_down, norm_final_g, loss_target, m_norm_mix_g, m_w_in, m_ssm_a_re, m_ssm_a_im, m_ssm_log_dt, m_ssm_b_re, m_ssm_b_im, m_ssm_c_re, m_ssm_c_im, m_ssm_d, m_w_glu, m_w_attn_out, m_w_out, m_norm_ffn_g, m_w_ffn_gate, m_w_ffn_up, m_w_ffn_down, m_norm_final_g, v_norm_mix_g, v_w_in, v_ssm_a_re, v_ssm_a_im, v_ssm_log_dt, v_ssm_b_re, v_ssm_b_im, v_ssm_c_re, v_ssm_c_im, v_ssm_d, v_w_glu, v_w_attn_out, v_w_out, v_norm_ffn_g, v_w_ffn_gate, v_w_ffn_up, v_w_ffn_down, v_norm_final_g):
    args = dict(locals())
    wts = {n: args[n] for n in ALL_WEIGHTS}
    moms = {n: args["m_" + n] for n in ALL_WEIGHTS}
    vars_ = {n: args["v_" + n] for n in ALL_WEIGHTS}
    n_samples = x.shape[0]
    t = n_samples * SEQ

    shards = {n: (wts[n][0] if n in ROW_SHARDED else wts[n][0].T).astype(BF16) for n in BIG_WEIGHTS}
    w_in_t = _all_gather(shards["w_in"], "allgather_w_in").reshape(IN_W, D_MODEL)

    small = {n: wts[n] for n in SMALL_WEIGHTS}
    grad_x, recv, _, g_mix_part = _local_step(x.reshape(t, D_MODEL), loss_target.reshape(t, D_MODEL), {"w_in": w_in_t},
                                              small, shards)

    results = {}
    for n in BIG_WEIGHTS:
        c, k = shards[n].shape
        w2, m2, v2 = wts[n][0], moms[n][0], vars_[n][0]
        if n in ROW_SHARDED:
            parts = [recv[f"{n}:q{qt}"] for qt in range(4)] if n in QUARTERED else [recv[n]]
            res = _adam(parts, w2, m2, v2, "adam_" + n, c // 2)
        elif n in HALVED:
            res = _adam([recv[f"{n}:{hf}"] for hf in range(2)], w2.T, m2.T, v2.T, "adam_" + n, c // 2)
            res = [a.T for a in res]
        else:
            g_t = _sum_partials([recv[n]], "sum_" + n, c // 2)
            res = _adam([g_t.T[None]], w2, m2, v2, "adam_" + n, k // 2)
        for kind, a in zip(("grad", "delta", "new_m", "new_v"), res):
            results[kind, n] = a[None]

    g_mix_all = _all_gather(jnp.pad(g_mix_part, ((0, 7), (0, 0))), "allgather_g_mix")
    g_mix = _sum_partials([g_mix_all], "sum_g_mix", 8)[0:1]
    loss, sgrads = _unpack_small(_sum_partials([recv["small"]], "sum_small", _PK_ROWS), g_mix)
    for n in SMALL_WEIGHTS:
        results["grad", n] = _unstored(n, sgrads[n], wts[n])
    results.update(_adam_small(sgrads, wts, moms, vars_))
    outs = [loss, grad_x.reshape(x.shape)]
    for kind in ("grad", "delta", "new_m", "new_v"):
        outs += [results[kind, n] for n in ALL_WEIGHTS]
    return tuple(outs)
```

```python
import functools
import math

import jax
import jax.numpy as jnp
from jax import lax
from jax.experimental import pallas as pl
from jax.experimental.pallas import tpu as pltpu

F32 = jnp.float32
BF16 = jnp.bfloat16
MXU_DTYPE = jnp.bfloat16

N_DEV = 8
D_MODEL = 1024
SEQ = 2048
HEAD_DIM = 64
HEADS_PER_GROUP = 4
GROUP_W = HEADS_PER_GROUP * HEAD_DIM
DILATIONS = (1, 4, 16)
QKV_W = 3 * len(DILATIONS) * GROUP_W
Q_W = len(DILATIONS) * GROUP_W
ATT_BLOCK = 128
ROPE_DIM = 16
ROPE_THETA = 500000.0
SSM_W = 512
SSM_GROUPS = 32
SSM_CH = 16
SSM_STATE = 64
N_STATE = SSM_GROUPS * SSM_STATE
D_FF = 2816
IN_W = QKV_W + SSM_W + 2 * D_MODEL
RMS_EPS = 1e-6
NEG_INF = -1e30
LANES = 128

SCAN_SEG_PER_SAMPLE = 8
SCAN_LEN = SEQ // SCAN_SEG_PER_SAMPLE
SCAN_WC = 512
SCAN_NBLK = N_STATE // SCAN_WC
SCAN_CH = SSM_W // SCAN_NBLK
SCAN_CHUNK = 32

ADAM_LR = 0.001
ADAM_B1 = 0.9
ADAM_B2 = 0.999
ADAM_EPS = 1e-08
ADAM_WD = 0.01
ADAM_STEP = 10

VMEM_BIG = 48 * 1024 * 1024
VMEM_MID = 32 * 1024 * 1024

BIG_WEIGHTS = ("w_in", "w_glu", "w_attn_out", "w_out", "w_ffn_gate", "w_ffn_up", "w_ffn_down")
ROW_SHARDED = ("w_out", "w_ffn_down")
SMALL_WEIGHTS = ("norm_mix_g", "ssm_a_re", "ssm_a_im", "ssm_log_dt", "ssm_b_re", "ssm_b_im", "ssm_c_re", "ssm_c_im",
                 "ssm_d", "norm_ffn_g", "norm_final_g")
ALL_WEIGHTS = ("norm_mix_g", "w_in", "ssm_a_re", "ssm_a_im", "ssm_log_dt", "ssm_b_re", "ssm_b_im", "ssm_c_re", "ssm_c_im",
               "ssm_d", "w_glu", "w_attn_out", "w_out", "norm_ffn_g", "w_ffn_gate", "w_ffn_up", "w_ffn_down", "norm_final_g")


def _sigmoid(x):
    return 1.0 / (1.0 + jnp.exp(-x))


def _pallas_call(body, *, out_shape, **kw):
    single = not isinstance(out_shape, (list, tuple))
    shapes = [pltpu.HBM(s.shape, s.dtype) for s in ([out_shape] if single else out_shape)]
    call = pl.pallas_call(body, out_shape=shapes[0] if single else shapes, **kw)
    return lambda *operands: call(*[pltpu.with_memory_space_constraint(o, pltpu.HBM) for o in operands])


class _Comm:
    def __init__(self, ins, out_shapes, n_sem, n_local, start, finish):
        self.ins, self.out_shapes, self.n_sem, self.n_local = ins, out_shapes, n_sem, n_local
        self.start, self.finish = start, finish


def _mm(a, b, mode, name, tm, tn, out_dtype=F32, add=None, vmem=VMEM_BIG, comm=None, cols=None):
    if mode == "nn":
        (m, k), (_, n) = a.shape, b.shape
        a_spec = pl.BlockSpec((tm, k), lambda i, j: (i, 0))
        b_spec = pl.BlockSpec((k, tn), lambda i, j: (0, j))
        dims = (((1,), (0,)), ((), ()))
    elif mode == "nt":
        (m, k), (n, _) = a.shape, b.shape
        a_spec = pl.BlockSpec((tm, k), lambda i, j: (i, 0))
        b_spec = pl.BlockSpec((tn, k), lambda i, j: (j, 0))
        dims = (((1,), (1,)), ((), ()))
    else:
        (k, m), (_, n) = a.shape, b.shape
        first, n = cols if cols else (0, n)
        a_spec = pl.BlockSpec((k, tm), lambda i, j: (0, i))
        b_spec = pl.BlockSpec((k, tn), lambda i, j: (0, j + first // tn))
        dims = (((0,), (0,)), ((), ()))
    assert m % tm == 0 and n % tn == 0, (name, m, n, tm, tn)
    o_spec = pl.BlockSpec((tm, tn), lambda i, j: (i, j))
    has_add = add is not None

    def body(*refs):
        a_ref, b_ref, o_ref = refs[0], refs[1], refs[-1]
        acc = lax.dot_general(a_ref[...].astype(MXU_DTYPE), b_ref[...].astype(MXU_DTYPE), dims,
                              preferred_element_type=F32)
        if has_add:
            acc = acc + refs[2][...]
        o_ref[...] = acc.astype(out_dtype)

    ins = [a, b] + ([add] if has_add else [])
    in_specs = [a_spec, b_spec] + ([o_spec] if has_add else [])
    return _grid_call(body, name, (m // tm, n // tn), ins, in_specs, [o_spec],
                      [jax.ShapeDtypeStruct((m, n), out_dtype)], vmem, comm)


def _grid_call(body, name, grid, ins, in_specs, out_specs, out_shapes, vmem, comm=None, sequential=False, scratch=()):
    if comm is None:
        single = len(out_shapes) == 1
        semantics = ("arbitrary", "arbitrary") if sequential else ("parallel", "parallel")
        return _pallas_call(
            body, name=name, grid=grid, in_specs=in_specs, out_specs=out_specs[0] if single else out_specs,
            out_shape=out_shapes[0] if single else out_shapes, scratch_shapes=list(scratch),
            compiler_params=pltpu.CompilerParams(dimension_semantics=semantics, vmem_limit_bytes=vmem),
        )(*ins)
    n_in, n_out, n_cin, n_cout = len(ins), len(out_shapes), len(comm.ins), len(comm.out_shapes)
    n_io = n_in + n_cin + n_out + n_cout

    def carrying(*refs):
        own = refs[:n_in] + refs[n_in + n_cin:n_in + n_cin + n_out] + refs[n_io:len(refs) - 3]
        c_args = (refs[n_in:n_in + n_cin], refs[n_in + n_cin + n_out:n_io], *refs[-3:])

        @pl.when((pl.program_id(0) == 0) & (pl.program_id(1) == 0))
        def _():
            comm.start(*c_args)

        body(*own)

        @pl.when((pl.program_id(0) == grid[0] - 1) & (pl.program_id(1) == grid[1] - 1))
        def _():
            comm.finish(*c_args)

    hbm = pl.BlockSpec(memory_space=pl.ANY)
    return _pallas_call(
        carrying, name=name, grid=grid, in_specs=list(in_specs) + [hbm] * n_cin,
        out_specs=list(out_specs) + [hbm] * n_cout, out_shape=list(out_shapes) + list(comm.out_shapes),
        scratch_shapes=list(scratch) + [pltpu.SemaphoreType.DMA((comm.n_sem,)), pltpu.SemaphoreType.DMA((comm.n_sem,)),
                                        pltpu.SemaphoreType.DMA((comm.n_local,))],
        compiler_params=pltpu.CompilerParams(dimension_semantics=("arbitrary", "arbitrary"), vmem_limit_bytes=vmem),
    )(*ins, *comm.ins)


def _rows(body, name, n_rows, tm, ins, outs, vmem=VMEM_MID, scratch=()):
    assert n_rows % tm == 0
    arrays, in_specs = [], []
    for kind, arr in ins:
        arrays.append(arr)
        if kind == "row":
            assert n_rows % arr.shape[0] == 0, (name, arr.shape)
            in_specs.append(pl.BlockSpec((tm * arr.shape[0] // n_rows, arr.shape[1]), lambda i: (i, 0)))
        elif kind == "tab":
            nblk = arr.shape[0] // tm
            in_specs.append(pl.BlockSpec((tm, arr.shape[1]), lambda i, nblk=nblk: (i % nblk, 0)))
        else:
            in_specs.append(pl.BlockSpec(arr.shape, lambda i, nd=arr.ndim: (0,) * nd))
    out_specs, out_shape = [], []
    for kind, shp, dt in outs:
        if kind == "row":
            out_specs.append(pl.BlockSpec((tm, shp), lambda i: (i, 0)))
            out_shape.append(jax.ShapeDtypeStruct((n_rows, shp), dt))
        elif kind == "dil":
            d, wd = shp
            out_specs.append(pl.BlockSpec((tm // d, d * wd), lambda i: (i, 0)))
            out_shape.append(jax.ShapeDtypeStruct((n_rows // d, d * wd), dt))
        else:
            out_specs.append(pl.BlockSpec(shp, lambda i, nd=len(shp): (0,) * nd))
            out_shape.append(jax.ShapeDtypeStruct(shp, dt))
    res = _pallas_call(
        body, name=name, grid=(n_rows // tm,), in_specs=in_specs, out_specs=out_specs, out_shape=out_shape,
        scratch_shapes=list(scratch),
        compiler_params=pltpu.CompilerParams(dimension_semantics=("arbitrary",), vmem_limit_bytes=vmem),
    )(*arrays)
    return res


def _gather_residue(stage, ch, r, d, n):
    return stage[ch, pl.ds(r, n, stride=d), :] if d > 1 else stage[ch]


def _scatter_residue(stage, ch, r, d, n, val):
    if d > 1:
        stage[ch, pl.ds(r, n, stride=d), :] = val
    else:
        stage[ch] = val


def _lane_chunk(ch):
    return slice(ch * LANES, (ch + 1) * LANES)


def _rope_tables():
    half = ROPE_DIM // 2
    inv = jnp.power(jnp.float32(ROPE_THETA), -jnp.arange(half, dtype=F32) * 2.0 / ROPE_DIM)
    ang = jnp.arange(SEQ, dtype=F32)[:, None] * inv[None, :]
    lane = jnp.arange(LANES) % HEAD_DIM
    cosl = jnp.cos(ang)[:, lane % half]
    sinl = jnp.sin(ang)[:, lane % half]
    tab_c = jnp.where(lane < ROPE_DIM, cosl, 1.0)
    tab_lo = jnp.where(lane < half, -sinl, 0.0)
    tab_hi = jnp.where((lane >= half) & (lane < ROPE_DIM), sinl, 0.0)
    return tab_c.astype(F32), tab_lo.astype(F32), tab_hi.astype(F32)


def _rope_apply(t, tc, tlo, thi):
    half = ROPE_DIM // 2
    return t * tc + pltpu.roll(t, LANES - half, 1) * tlo + pltpu.roll(t, half, 1) * thi


def _rope_transpose(dt, tc, tlo, thi):
    half = ROPE_DIM // 2
    return dt * tc + pltpu.roll(dt * tlo, half, 1) + pltpu.roll(dt * thi, LANES - half, 1)


def _pack_dproj(dqs, dks, dvs, du, dgpre, tabs):
    tm = 256

    def body(*refs):
        dq_refs, dk_refs, dv_refs = refs[0:3], refs[3:6], refs[6:9]
        du_ref, dg_ref, tc_ref, tlo_ref, thi_ref, o_ref, stage = refs[9:16]
        n_ch = QKV_W // LANES
        halves = GROUP_W // LANES
        for grp, d in enumerate(DILATIONS):
            for which, src in enumerate((dq_refs[grp], dk_refs[grp], dv_refs[grp])):
                for res in range(d):
                    for half in range(halves):
                        _scatter_residue(stage, which * (n_ch // 3) + grp * halves + half, res, d, tm // d,
                                         src[:, _lane_chunk(res * halves + half)])
        tc, tlo, thi = tc_ref[...], tlo_ref[...], thi_ref[...]
        for ch in range(n_ch):
            piece = stage[ch]
            o_ref[:, _lane_chunk(ch)] = (_rope_transpose(piece, tc, tlo, thi) if ch < 2 * n_ch // 3 else piece).astype(BF16)
        o_ref[:, QKV_W:QKV_W + SSM_W] = du_ref[...].astype(BF16)
        o_ref[:, QKV_W + SSM_W:] = dg_ref[...].astype(BF16)

    t = du.shape[0]
    ins = [("row", a) for a in (*dqs, *dks, *dvs, du, dgpre)] + [("tab", tb) for tb in tabs]
    return _rows(body, "pack_dproj", t, tm, ins, [("row", IN_W, BF16)],
                 scratch=[pltpu.VMEM((QKV_W // LANES, tm, LANES), F32)])[0]


def _merge_groups(o_refs, l_refs, a_ref, lt_ref, nat, tm):
    halves = GROUP_W // LANES
    for grp, d in enumerate(DILATIONS[1:], start=1):
        for j, src in enumerate((o_refs[grp], l_refs[grp])):
            for res in range(d):
                for half in range(halves):
                    _scatter_residue(nat, (grp - 1) * 4 + j * 2 + half, res, d, tm // d,
                                     src[:, _lane_chunk(res * halves + half)])
    for half in range(halves):
        sl = _lane_chunk(half)
        la, lb, lc = l_refs[0][:, sl], nat[2 + half], nat[6 + half]
        m = jnp.maximum(jnp.maximum(la, lb), lc)
        ea, eb, ec = jnp.exp(la - m), jnp.exp(lb - m), jnp.exp(lc - m)
        ssum = ea + eb + ec
        a_ref[:, sl] = (ea / ssum) * o_refs[0][:, sl] + (eb / ssum) * nat[half] + (ec / ssum) * nat[4 + half]
        lt_ref[:, sl] = m + jnp.log(ssum)


def _head_sum_matrix():
    r = jnp.arange(GROUP_W) // HEAD_DIM
    return (r[:, None] == r[None, :]).astype(F32)


def _attention_cotangents(da, attn, lt, ones, rd_ref, dil, stage, tm):
    halves = GROUP_W // LANES
    rd = jnp.dot(da * attn, ones, preferred_element_type=F32, precision=lax.Precision.HIGHEST)
    rd_ref[...] = rd
    for half in range(halves):
        for j, val in enumerate((da, lt, rd)):
            stage[2 * j + half] = val[:, _lane_chunk(half)]
    for grp, d in enumerate(DILATIONS[1:], start=1):
        for j in range(3):
            for res in range(d):
                for half in range(halves):
                    dil[3 * (grp - 1) + j][:, _lane_chunk(res * halves + half)] = _gather_residue(
                        stage, 2 * j + half, res, d, tm // d)


_GELU_C = math.sqrt(2.0 / math.pi)


def _head_masks():
    lane = lax.broadcasted_iota(jnp.int32, (1, GROUP_W), 1)
    return [(lane // HEAD_DIM) == h for h in range(HEADS_PER_GROUP)]


def _stack_heads(blk, masks, fill=0.0):
    return jnp.concatenate([jnp.where(mk, blk, jnp.full_like(blk, fill)) for mk in masks], axis=0)


def _unstack_heads(stacked, masks):
    rows = stacked.shape[0] // len(masks)
    out = stacked[:rows]
    for h in range(1, len(masks)):
        out = jnp.where(masks[h], stacked[h * rows:(h + 1) * rows], out)
    return out


def _band_mask(first):
    nk = ATT_BLOCK if first else 2 * ATT_BLOCK
    qi = lax.broadcasted_iota(jnp.int32, (ATT_BLOCK, nk), 0)
    ki = lax.broadcasted_iota(jnp.int32, (ATT_BLOCK, nk), 1)
    dist = qi - ki + (0 if first else ATT_BLOCK)
    return (dist >= 0) & (dist <= ATT_BLOCK)


_NT = (((1,), (1,)), ((), ()))
_TN = (((0,), (0,)), ((), ()))


def _residues_per_step(d):
    return min(d, 4)


def _attn_fwd(q, k, v, group, n_samples, comm=None):
    d = DILATIONS[group]
    length = SEQ // d
    nb = length // ATT_BLOCK

    rps = _residues_per_step(d)

    def body(q_ref, k_ref, v_ref, o_ref, l_ref):
        for rl in range(rps):
            residue(q_ref, k_ref, v_ref, o_ref, l_ref, slice(rl * GROUP_W, (rl + 1) * GROUP_W))

    def residue(q_ref, k_ref, v_ref, o_ref, l_ref, cols):
        masks = _head_masks()

        def block(qs, ks, first):
            nk = ATT_BLOCK if first else 2 * ATT_BLOCK
            qb = q_ref[0, pl.ds(qs, ATT_BLOCK), cols]
            kc = k_ref[0, pl.ds(ks, nk), cols]
            vc = v_ref[0, pl.ds(ks, nk), cols]
            q4 = _stack_heads(qb, masks)
            valid = jnp.tile(_band_mask(first), (HEADS_PER_GROUP, 1))
            s = lax.dot_general(q4, kc, _NT, preferred_element_type=F32) * (HEAD_DIM ** -0.5)
            s = jnp.where(valid, s, NEG_INF)
            m = jnp.max(s, axis=-1, keepdims=True)
            p = jnp.exp(s - m)
            l = jnp.sum(p, axis=-1, keepdims=True)
            o4 = jnp.dot(p.astype(MXU_DTYPE), vc, preferred_element_type=F32) / l
            lse4 = jnp.broadcast_to(m + jnp.log(l), o4.shape)
            o_ref[0, pl.ds(qs, ATT_BLOCK), cols] = _unstack_heads(o4, masks)
            l_ref[0, pl.ds(qs, ATT_BLOCK), cols] = _unstack_heads(lse4, masks)

        block(0, 0, True)
        if nb > 1:
            def loop(n, carry):
                block(pl.multiple_of(n * ATT_BLOCK, ATT_BLOCK), pl.multiple_of((n - 1) * ATT_BLOCK, ATT_BLOCK), False)
                return carry

            lax.fori_loop(1, nb, loop, 0)

    per_sample = lambda a: a.reshape(n_samples, length, d * GROUP_W)
    spec = pl.BlockSpec((1, length, rps * GROUP_W), lambda b, r: (b, 0, r))
    shp = jax.ShapeDtypeStruct((n_samples, length, d * GROUP_W), F32)
    o, lse, *carried = _grid_call(body, f"attn_fwd_g{group}", (n_samples, d // rps), [per_sample(a) for a in (q, k, v)],
                                  [spec] * 3, [spec] * 2, [shp, shp], VMEM_MID, comm)
    flat = lambda a: a.reshape(n_samples * length, d * GROUP_W)
    return flat(o), flat(lse), carried


def _attn_bwd(q, k, v, dattn, lse_tot, rowdot, group, n_samples, comm=None):
    d = DILATIONS[group]
    length = SEQ // d
    nb = length // ATT_BLOCK

    rps = _residues_per_step(d)

    def body(q_ref, k_ref, v_ref, da_ref, lt_ref, rd_ref, dq_ref, dk_ref, dv_ref):
        dk_ref[...] = jnp.zeros_like(dk_ref)
        dv_ref[...] = jnp.zeros_like(dv_ref)
        for rl in range(rps):
            residue(q_ref, k_ref, v_ref, da_ref, lt_ref, rd_ref, dq_ref, dk_ref, dv_ref,
                    slice(rl * GROUP_W, (rl + 1) * GROUP_W))

    def residue(q_ref, k_ref, v_ref, da_ref, lt_ref, rd_ref, dq_ref, dk_ref, dv_ref, cols):
        masks = _head_masks()

        def block(qs, ks, first):
            nk = ATT_BLOCK if first else 2 * ATT_BLOCK
            qb = q_ref[0, pl.ds(qs, ATT_BLOCK), cols]
            kc = k_ref[0, pl.ds(ks, nk), cols]
            vc = v_ref[0, pl.ds(ks, nk), cols]
            da = da_ref[0, pl.ds(qs, ATT_BLOCK), cols]
            lt = lt_ref[0, pl.ds(qs, ATT_BLOCK), cols]
            rd = rd_ref[0, pl.ds(qs, ATT_BLOCK), cols]
            q4 = _stack_heads(qb, masks)
            da4 = _stack_heads(da, masks).astype(MXU_DTYPE)
            lt4 = jnp.max(_stack_heads(lt, masks, -jnp.inf), axis=-1, keepdims=True)
            rd4 = jnp.max(_stack_heads(rd, masks, -jnp.inf), axis=-1, keepdims=True)
            valid = jnp.tile(_band_mask(first), (HEADS_PER_GROUP, 1))
            s = lax.dot_general(q4, kc, _NT, preferred_element_type=F32) * (HEAD_DIM ** -0.5)
            s = jnp.where(valid, s, NEG_INF)
            p = jnp.exp(s - lt4)
            dp = lax.dot_general(da4, vc, _NT, preferred_element_type=F32)
            ds = (p * (dp - rd4) * (HEAD_DIM ** -0.5)).astype(MXU_DTYPE)
            dq_ref[0, pl.ds(qs, ATT_BLOCK), cols] = _unstack_heads(jnp.dot(ds, kc, preferred_element_type=F32), masks)
            dk_ref[0, pl.ds(ks, nk), cols] += lax.dot_general(ds, q4, _TN, preferred_element_type=F32)
            dv_ref[0, pl.ds(ks, nk), cols] += lax.dot_general(p.astype(MXU_DTYPE), da4, _TN, preferred_element_type=F32)

        block(0, 0, True)
        if nb > 1:
            def loop(n, carry):
                block(pl.multiple_of(n * ATT_BLOCK, ATT_BLOCK), pl.multiple_of((n - 1) * ATT_BLOCK, ATT_BLOCK), False)
                return carry

            lax.fori_loop(1, nb, loop, 0)

    per_sample = lambda a: a.reshape(n_samples, length, d * GROUP_W)
    spec = pl.BlockSpec((1, length, rps * GROUP_W), lambda b, r: (b, 0, r))
    shp = jax.ShapeDtypeStruct((n_samples, length, d * GROUP_W), F32)
    dq, dk, dv, *carried = _grid_call(
        body, f"attn_bwd_g{group}", (n_samples, d // rps), [per_sample(a) for a in (q, k, v, dattn, lse_tot, rowdot)],
        [spec] * 6, [spec] * 3, [shp, shp, shp], VMEM_MID, comm)
    flat = lambda a: a.reshape(n_samples * length, d * GROUP_W)
    return flat(dq), flat(dk), flat(dv), carried


def _disc(lr, li, ldt, br, bi):
    dt = jnp.exp(ldt)
    mag = jnp.exp(lr * dt)
    ab_re, ab_im = mag * jnp.cos(li * dt), mag * jnp.sin(li * dt)
    den = lr * lr + li * li
    nr, ni = ab_re - 1.0, ab_im
    f_re = (nr * lr + ni * li) / den
    f_im = (ni * lr - nr * li) / den
    return ab_re, ab_im, f_re * br - f_im * bi, f_re * bi + f_im * br


def _state_mask():
    row_g = lax.broadcasted_iota(jnp.int32, (SCAN_CH, SCAN_WC), 0) // SSM_CH
    col_g = lax.broadcasted_iota(jnp.int32, (SCAN_CH, SCAN_WC), 1) // SSM_STATE
    return row_g == col_g


def _ssm_disc(lr, li, ldt, br, bi, cr, ci):
    w = SCAN_WC

    def body(lr_ref, li_ref, ldt_ref, br_ref, bi_ref, cr_ref, ci_ref, a_ref, bb_ref, c_ref):
        ar, ai, bbr, bbi = _disc(lr_ref[...], li_ref[...], ldt_ref[...], br_ref[...], bi_ref[...])
        crv, civ = cr_ref[...], ci_ref[...]
        mask = _state_mask()
        for cb in range(SCAN_NBLK):
            sl = slice(cb * w, (cb + 1) * w)
            rows = slice(cb * SCAN_CH, (cb + 1) * SCAN_CH)
            dense = lambda comp: jnp.where(mask, jnp.tile(comp[:, sl], (SCAN_CH // SSM_CH, 1)), 0.0)
            a_ref[:, 2 * cb * w:(2 * cb + 1) * w] = ar[:, sl]
            a_ref[:, (2 * cb + 1) * w:(2 * cb + 2) * w] = ai[:, sl]
            bb_ref[rows, :w] = dense(bbr).astype(MXU_DTYPE)
            bb_ref[rows, w:] = dense(bbi).astype(MXU_DTYPE)
            c_ref[rows, :w] = dense(crv).astype(MXU_DTYPE)
            c_ref[rows, w:] = (-dense(civ)).astype(MXU_DTYPE)

    return _pallas_call(
        body, name="ssm_disc",
        out_shape=[jax.ShapeDtypeStruct((1, 2 * N_STATE), F32), jax.ShapeDtypeStruct((SSM_W, 2 * w), MXU_DTYPE),
                   jax.ShapeDtypeStruct((SSM_W, 2 * w), MXU_DTYPE)],
        compiler_params=pltpu.CompilerParams(vmem_limit_bytes=VMEM_MID),
    )(lr, li, ldt, br, bi, cr, ci)


def _group_indicator():
    s = jnp.arange(N_STATE) // SSM_STATE
    return (s[:, None] == jnp.arange(LANES)[None, :]).astype(F32)


def _ssm_param_bwd(lr, li, ldt, br, bi, da_cat, dbb_full, dc_full):
    w = SCAN_WC

    def body(lr_ref, li_ref, ldt_ref, br_ref, bi_ref, da_ref, dbb_ref, dc_ref, ind_ref,
             glr_ref, gli_ref, gldt_ref, gbr_ref, gbi_ref, gcr_ref, gci_ref):
        mask = _state_mask()

        def diag_parts(ref):
            res = ([], [])
            for cb in range(SCAN_NBLK):
                for part in range(2):
                    blk = ref[cb * SCAN_CH:(cb + 1) * SCAN_CH, part * w:(part + 1) * w]
                    res[part].append(jnp.sum(jnp.where(mask, blk, 0.0).reshape(SCAN_CH // SSM_CH, SSM_CH, w), axis=0))
            return jnp.concatenate(res[0], axis=1), jnp.concatenate(res[1], axis=1)

        dar = jnp.concatenate([da_ref[:, 2 * cb * w:(2 * cb + 1) * w] for cb in range(SCAN_NBLK)], axis=1)
        dai = jnp.concatenate([da_ref[:, (2 * cb + 1) * w:(2 * cb + 2) * w] for cb in range(SCAN_NBLK)], axis=1)
        dbbr, dbbi = diag_parts(dbb_ref)
        dcr, dci_neg = diag_parts(dc_ref)
        gcr_ref[...] = dcr
        gci_ref[...] = -dci_neg
        _, vjp = jax.vjp(_disc, lr_ref[...], li_ref[...], ldt_ref[...], br_ref[...], bi_ref[...])
        glr, gli, gldt, gbr, gbi = vjp((dar, dai, dbbr, dbbi))
        glr_ref[...] = glr
        gli_ref[...] = gli
        gldt_ref[...] = jnp.dot(jnp.broadcast_to(gldt, (8, N_STATE)), ind_ref[...], preferred_element_type=F32,
                                precision=lax.Precision.HIGHEST)
        gbr_ref[...] = gbr
        gbi_ref[...] = gbi

    v1 = jax.ShapeDtypeStruct((1, N_STATE), F32)
    v16 = jax.ShapeDtypeStruct((SSM_CH, N_STATE), F32)
    vdt = jax.ShapeDtypeStruct((8, LANES), F32)
    return _pallas_call(
        body, name="ssm_param_bwd", out_shape=[v1, v1, vdt, v16, v16, v16, v16],
        compiler_params=pltpu.CompilerParams(vmem_limit_bytes=VMEM_BIG),
    )(lr, li, ldt, br, bi, da_cat, dbb_full, dc_full, _group_indicator())


def _cmul(ar, ai, br, bi):
    return ar * br - ai * bi, ar * bi + ai * br


def _gelu_tanh(y):
    return jnp.tanh(_GELU_C * (y + 0.044715 * (y * y * y)))


def _segment_carry(er, ei, ar, ai, n_rows, reverse):
    qr, qi = ar, ai
    for _ in range(int(math.log2(SCAN_LEN))):
        qr, qi = _cmul(qr, qi, qr, qi)
    seg = lax.broadcasted_iota(jnp.int32, er.shape, 0) % SCAN_SEG_PER_SAMPLE
    shift = 1
    while shift < SCAN_SEG_PER_SAMPLE:
        keep = (seg < SCAN_SEG_PER_SAMPLE - shift) if reverse else (seg >= shift)
        amount = n_rows - shift if reverse else shift
        sr = jnp.where(keep, pltpu.roll(er, amount, 0), 0.0)
        si = jnp.where(keep, pltpu.roll(ei, amount, 0), 0.0)
        if reverse:
            er, ei = er + qr * sr + qi * si, ei + qr * si - qi * sr
        else:
            er, ei = er + qr * sr - qi * si, ei + qr * si + qi * sr
        qr, qi = _cmul(qr, qi, qr, qi)
        shift *= 2
    keep = (seg < SCAN_SEG_PER_SAMPLE - 1) if reverse else (seg >= 1)
    amount = n_rows - 1 if reverse else 1
    return jnp.where(keep, pltpu.roll(er, amount, 0), 0.0), jnp.where(keep, pltpu.roll(ei, amount, 0), 0.0)


def _ssm_fwd(u_perm, a_cat, bbc, cc, dskip, n_rows):
    t = u_perm.shape[0]
    w = SCAN_WC
    rows_c = SCAN_CHUNK * n_rows
    n_chunks = t // rows_c

    assert n_chunks % 2 == 0

    def body(u_ref, a_ref, bb_ref, c_ref, d_ref, yt_ref, yg_ref, ein_ref, bu_all, st_a, st_b, xs_a, xs_b):
        ar = jnp.broadcast_to(a_ref[:, :w], (n_rows, w))
        ai = jnp.broadcast_to(a_ref[:, w:], (n_rows, w))
        start = lambda ch: pl.multiple_of(ch * rows_c, rows_c)

        def project(ch, stage):
            res = jnp.dot(u_ref[pl.ds(start(ch), rows_c), :].astype(MXU_DTYPE), bb_ref[...], preferred_element_type=F32)
            stage[...] = res
            bu_all[pl.ds(start(ch), rows_c), :] = res

        def steps(src, r0, carry, xs=None):
            for i in range(SCAN_CHUNK):
                blk = src[pl.ds(r0 + i * n_rows, n_rows), :]
                carry = (ar * carry[0] - ai * carry[1] + blk[:, :w], ar * carry[1] + ai * carry[0] + blk[:, w:])
                if xs is not None:
                    xs[i * n_rows:(i + 1) * n_rows, :w] = carry[0]
                    xs[i * n_rows:(i + 1) * n_rows, w:] = carry[1]
            return carry

        def emit(xs, ch):
            y = lax.dot_general(xs[...].astype(MXU_DTYPE), c_ref[...], _NT, preferred_element_type=F32)
            yt = y + d_ref[...] * u_ref[pl.ds(start(ch), rows_c), :]
            yt_ref[pl.ds(start(ch), rows_c), :] = yt
            yg_ref[pl.ds(start(ch), rows_c), :] = (0.5 * yt * (1.0 + _gelu_tanh(yt))).astype(BF16)

        project(0, st_a)

        def pair1(p, carry):
            project(2 * p + 1, st_b)
            carry = steps(st_a, 0, carry)
            project(jnp.minimum(2 * p + 2, n_chunks - 1), st_a)
            return steps(st_b, 0, carry)

        zero = jnp.zeros((n_rows, w), F32)
        er, ei = lax.fori_loop(0, n_chunks // 2, pair1, (zero, zero))
        cr, ci = _segment_carry(er, ei, ar, ai, n_rows, False)
        ein_ref[:, :w] = cr
        ein_ref[:, w:] = ci

        xs_b[...] = jnp.zeros_like(xs_b)

        def pair2(p, carry):
            emit(xs_b, jnp.maximum(2 * p - 1, 0))
            carry = steps(bu_all, start(2 * p), carry, xs_a)
            emit(xs_a, 2 * p)
            return steps(bu_all, start(2 * p + 1), carry, xs_b)

        lax.fori_loop(0, n_chunks // 2, pair2, (cr, ci))
        emit(xs_b, n_chunks - 1)

    col = lambda width: pl.BlockSpec((t, width), lambda c: (0, c))
    wgt = pl.BlockSpec((SCAN_CH, 2 * w), lambda c: (c, 0))
    return _pallas_call(
        body, name="ssm_fwd", grid=(SCAN_NBLK,),
        in_specs=[col(SCAN_CH), pl.BlockSpec((1, 2 * w), lambda c: (0, c)), wgt, wgt,
                  pl.BlockSpec((1, SCAN_CH), lambda c: (0, c))],
        out_specs=[col(SCAN_CH), col(SCAN_CH), pl.BlockSpec((n_rows, 2 * w), lambda c: (0, c))],
        out_shape=[jax.ShapeDtypeStruct((t, SSM_W), F32), jax.ShapeDtypeStruct((t, SSM_W), BF16),
                   jax.ShapeDtypeStruct((n_rows, 2 * N_STATE), F32)],
        scratch_shapes=[pltpu.VMEM((t, 2 * w), F32)] + [pltpu.VMEM((rows_c, 2 * w), F32)] * 4,
        compiler_params=pltpu.CompilerParams(dimension_semantics=("parallel",), vmem_limit_bytes=VMEM_BIG),
    )(u_perm, a_cat, bbc, cc, dskip)


def _ssm_bwd(u_perm, dyg, ytot, dskip, a_cat, bbc, cc, ein, n_rows, comm=None):
    t = u_perm.shape[0]
    w = SCAN_WC
    rows_c = SCAN_CHUNK * n_rows
    n_chunks = t // rows_c

    assert n_chunks % 2 == 0
    last = n_chunks - 1

    def body(u_ref, dyg_ref, yt_ref, dk_ref, a_ref, bb_ref, c_ref, ein_ref, du_ref, gd_ref, da_ref, dbb_ref, dc_ref,
             xs_all, dy_s, st_a, st_b, buf_a, buf_b):
        ar = jnp.broadcast_to(a_ref[:, :w], (n_rows, w))
        ai = jnp.broadcast_to(a_ref[:, w:], (n_rows, w))
        zero = jnp.zeros((n_rows, w), F32)
        start = lambda ch: pl.multiple_of(ch * rows_c, rows_c)
        dbb_ref[...] = jnp.zeros_like(dbb_ref)
        dc_ref[...] = jnp.zeros_like(dc_ref)
        da_ref[...] = jnp.zeros_like(da_ref)

        yt = yt_ref[...]
        th = _gelu_tanh(yt)
        dgelu = 0.5 * (1.0 + th) + 0.5 * yt * (1.0 - th * th) * _GELU_C * (1.0 + 3.0 * 0.044715 * yt * yt)
        dy_all = dyg_ref[...] * dgelu
        dy_s[...] = dy_all
        gd_ref[...] = jnp.sum(dy_all * u_ref[...], axis=0, keepdims=True)
        dy_chunk = lambda ch: dy_s[pl.ds(start(ch), rows_c), :].astype(MXU_DTYPE)

        xs_all[0:n_rows, :] = ein_ref[...]

        def project(ch, stage):
            stage[...] = jnp.dot(u_ref[pl.ds(start(ch), rows_c), :].astype(MXU_DTYPE), bb_ref[...],
                                 preferred_element_type=F32)

        def fwd_steps(stage, ch, carry, xs):
            for i in range(SCAN_CHUNK):
                blk = stage[i * n_rows:(i + 1) * n_rows, :]
                carry = (ar * carry[0] - ai * carry[1] + blk[:, :w], ar * carry[1] + ai * carry[0] + blk[:, w:])
                for half, val in enumerate(carry):
                    xs[i * n_rows:(i + 1) * n_rows, half * w:(half + 1) * w] = val
                    xs_all[pl.ds(start(ch) + (i + 1) * n_rows, n_rows), half * w:(half + 1) * w] = val
            return carry

        def add_dc(xs, ch):
            dc_ref[...] += lax.dot_general(dy_chunk(ch), xs[...].astype(MXU_DTYPE), _TN, preferred_element_type=F32)

        project(0, st_a)

        def fwd_pair(p, carry):
            project(2 * p + 1, st_b)
            carry = fwd_steps(st_a, 2 * p, carry, buf_a)
            add_dc(buf_a, 2 * p)
            project(jnp.minimum(2 * p + 2, last), st_a)
            carry = fwd_steps(st_b, 2 * p + 1, carry, buf_b)
            add_dc(buf_b, 2 * p + 1)
            return carry

        lax.fori_loop(0, n_chunks // 2, fwd_pair, (ein_ref[:, :w], ein_ref[:, w:]))

        def project_dx(ch, stage):
            stage[...] = jnp.dot(dy_chunk(ch), c_ref[...], preferred_element_type=F32)

        def back_steps(stage, carry, g_buf=None):
            for i in reversed(range(SCAN_CHUNK)):
                blk = stage[i * n_rows:(i + 1) * n_rows, :]
                carry = (blk[:, :w] + ar * carry[0] + ai * carry[1], blk[:, w:] + ar * carry[1] - ai * carry[0])
                if g_buf is not None:
                    g_buf[i * n_rows:(i + 1) * n_rows, :w] = carry[0]
                    g_buf[i * n_rows:(i + 1) * n_rows, w:] = carry[1]
            return carry

        def first_pair(p, carry):
            project_dx(last - 2 * p - 1, st_b)
            carry = back_steps(st_a, carry)
            project_dx(jnp.maximum(last - 2 * p - 2, 0), st_a)
            return back_steps(st_b, carry)

        project_dx(last, st_a)
        sr, si = lax.fori_loop(0, n_chunks // 2, first_pair, (zero, zero))
        gr0, gi0 = _segment_carry(sr, si, ar, ai, n_rows, True)

        def post(g_buf, ch):
            g = g_buf[...]
            xp = xs_all[pl.ds(start(ch), rows_c), :]
            da_ref[:, :w] += jnp.sum(g[:, :w] * xp[:, :w] + g[:, w:] * xp[:, w:], axis=0, keepdims=True)
            da_ref[:, w:] += jnp.sum(g[:, w:] * xp[:, :w] - g[:, :w] * xp[:, w:], axis=0, keepdims=True)
            gb = g.astype(MXU_DTYPE)
            du_ref[pl.ds(start(ch), rows_c), :] = (lax.dot_general(gb, bb_ref[...], _NT, preferred_element_type=F32)
                                                   + dy_s[pl.ds(start(ch), rows_c), :] * dk_ref[...])
            dbb_ref[...] += lax.dot_general(u_ref[pl.ds(start(ch), rows_c), :].astype(MXU_DTYPE), gb, _TN,
                                            preferred_element_type=F32)

        def second_pair(p, carry):
            c1 = last - 2 * p
            project_dx(c1 - 1, st_b)
            post(buf_b, jnp.minimum(c1 + 1, last))
            carry = back_steps(st_a, carry, buf_a)
            project_dx(jnp.maximum(c1 - 2, 0), st_a)
            post(buf_a, c1)
            return back_steps(st_b, carry, buf_b)

        project_dx(last, st_a)
        buf_b[...] = jnp.zeros_like(buf_b)
        lax.fori_loop(0, n_chunks // 2, second_pair, (gr0, gi0))
        post(buf_b, 0)

    col = lambda width: pl.BlockSpec((t, width), lambda c, j: (0, c))
    wgt = pl.BlockSpec((SCAN_CH, 2 * w), lambda c, j: (c, 0))
    row = pl.BlockSpec((1, 2 * w), lambda c, j: (0, c))
    chan = pl.BlockSpec((1, SCAN_CH), lambda c, j: (0, c))
    return _grid_call(
        body, "ssm_bwd", (SCAN_NBLK, 1), [u_perm, dyg, ytot, dskip, a_cat, bbc, cc, ein],
        [col(SCAN_CH), col(SCAN_CH), col(SCAN_CH), chan, row, wgt, wgt,
         pl.BlockSpec((n_rows, 2 * w), lambda c, j: (0, c))],
        [col(SCAN_CH), chan, row, wgt, wgt],
        [jax.ShapeDtypeStruct((t, SSM_W), F32), jax.ShapeDtypeStruct((1, SSM_W), F32),
         jax.ShapeDtypeStruct((1, 2 * N_STATE), F32), jax.ShapeDtypeStruct((SSM_W, 2 * w), F32),
         jax.ShapeDtypeStruct((SSM_W, 2 * w), F32)],
        56 * 1024 * 1024, comm,
        scratch=[pltpu.VMEM((t + n_rows, 2 * w), F32), pltpu.VMEM((t, SCAN_CH), F32)]
        + [pltpu.VMEM((rows_c, 2 * w), F32)] * 4)


def _to_scan_rows(a, n_samples):
    c = a.shape[1]
    return a.reshape(n_samples, SCAN_SEG_PER_SAMPLE, SCAN_LEN, c).transpose(2, 0, 1, 3).reshape(-1, c)


def _from_scan_rows(a, n_samples):
    c = a.shape[1]
    return a.reshape(SCAN_LEN, n_samples, SCAN_SEG_PER_SAMPLE, c).transpose(1, 2, 0, 3).reshape(-1, c)


def _row_spec(tm, width):
    return pl.BlockSpec((tm, width), lambda i, j: (i, 0))


def _whole(arr):
    return pl.BlockSpec(arr.shape, lambda i, j: (0,) * arr.ndim)


def _rms_norm(x, g, comm):
    t, d = x.shape
    tm = 512

    def body(x_ref, g_ref, h_ref):
        xv = x_ref[...]
        r = lax.rsqrt(jnp.mean(xv * xv, axis=-1, keepdims=True) + RMS_EPS)
        h_ref[...] = ((xv * r) * g_ref[...]).astype(BF16)

    return _grid_call(body, "rms_mix", (t // tm, 1), [x, g], [_row_spec(tm, d), _whole(g)], [_row_spec(tm, d)],
                      [jax.ShapeDtypeStruct((t, d), BF16)], VMEM_MID, comm)


def _proj_rope(h, w_in_t, tabs, comm=None):
    t = h.shape[0]
    tm = 256

    def body(h_ref, w_ref, tc_ref, tlo_ref, thi_ref, u_ref, gate_ref, *rest):
        qkv_refs, stage = rest[:9], rest[9]
        p = lax.dot_general(h_ref[...].astype(MXU_DTYPE), w_ref[...], _NT, preferred_element_type=F32)
        u_ref[...] = p[:, QKV_W:QKV_W + SSM_W]
        gate_ref[...] = _sigmoid(p[:, QKV_W + SSM_W:])
        tc, tlo, thi = tc_ref[...], tlo_ref[...], thi_ref[...]
        n_ch = QKV_W // LANES
        for ch in range(n_ch):
            piece = p[:, _lane_chunk(ch)]
            stage[ch] = _rope_apply(piece, tc, tlo, thi) if ch < 2 * n_ch // 3 else piece
        halves = GROUP_W // LANES
        for grp, d in enumerate(DILATIONS):
            for which in range(3):
                out = qkv_refs[3 * grp + which]
                for res in range(d):
                    for half in range(halves):
                        ch = which * (n_ch // 3) + grp * halves + half
                        out[:, _lane_chunk(res * halves + half)] = _gather_residue(stage, ch, res, d, tm // d).astype(BF16)

    tab = pl.BlockSpec((tm, LANES), lambda i, j: (i % (SEQ // tm), 0))
    widths = [(SSM_W, F32), (2 * D_MODEL, F32)]
    out_specs = [_row_spec(tm, wd) for wd, _ in widths]
    out_shapes = [jax.ShapeDtypeStruct((t, wd), dt) for wd, dt in widths]
    for d in DILATIONS:
        out_specs += [_row_spec(tm // d, d * GROUP_W)] * 3
        out_shapes += [jax.ShapeDtypeStruct((t // d, d * GROUP_W), BF16)] * 3
    return _grid_call(
        body, "proj_rope", (t // tm, 1), [h, w_in_t, *tabs],
        [_row_spec(tm, D_MODEL), _whole(w_in_t), tab, tab, tab], out_specs, out_shapes, VMEM_BIG, comm,
        scratch=[pltpu.VMEM((QKV_W // LANES, tm, LANES), F32)])


def _branch_outputs(attn_ref, yg_ref, wao_ref, wglu_ref):
    attn_d = lax.dot_general(attn_ref[...].astype(MXU_DTYPE), wao_ref[...], _NT, preferred_element_type=F32)
    z = lax.dot_general(yg_ref[...].astype(MXU_DTYPE), wglu_ref[...], _NT, preferred_element_type=F32)
    return attn_d, z[:, :D_MODEL], _sigmoid(z[:, D_MODEL:])


def _mix_out_rms(os_, lses, yg, gates, x, w_ao_t, w_glu_t, w_out, g, comm=None):
    t = x.shape[0]
    tm = 256

    def body(o0, o1, o2, l0, l1, l2, yg_ref, gate_ref, x_ref, wao_ref, wglu_ref, wout_ref, g_ref,
             attn_ref, lt_ref, m_ref, x1_ref, h_ref, nat):
        _merge_groups((o0, o1, o2), (l0, l1, l2), attn_ref, lt_ref, nat, tm)
        attn_d, za, sb = _branch_outputs(attn_ref, yg_ref, wao_ref, wglu_ref)
        merged = (gate_ref[:, :D_MODEL] * attn_d + gate_ref[:, D_MODEL:] * (za * sb)).astype(BF16)
        m_ref[...] = merged
        x1 = x_ref[...] + jnp.dot(merged.astype(MXU_DTYPE), wout_ref[...], preferred_element_type=F32)
        x1_ref[...] = x1
        r = lax.rsqrt(jnp.mean(x1 * x1, axis=-1, keepdims=True) + RMS_EPS)
        h_ref[...] = ((x1 * r) * g_ref[...]).astype(BF16)

    dil_specs = [_row_spec(tm // d, d * GROUP_W) for d in DILATIONS] * 2
    return _grid_call(
        body, "mix_out_rms", (t // tm, 1), [*os_, *lses, yg, gates, x, w_ao_t, w_glu_t, w_out, g],
        dil_specs + [_row_spec(tm, SSM_W), _row_spec(tm, 2 * D_MODEL), _row_spec(tm, D_MODEL),
                     _whole(w_ao_t), _whole(w_glu_t), _whole(w_out), _whole(g)],
        [_row_spec(tm, GROUP_W)] * 2 + [_row_spec(tm, D_MODEL)] * 3,
        [jax.ShapeDtypeStruct((t, GROUP_W), F32)] * 2
        + [jax.ShapeDtypeStruct((t, D_MODEL), BF16), jax.ShapeDtypeStruct((t, D_MODEL), F32),
           jax.ShapeDtypeStruct((t, D_MODEL), BF16)], VMEM_BIG, comm, scratch=[pltpu.VMEM((8, tm, LANES), F32)])


def _mix_bwd(dx1b, attn, lse_tot, yg, gates, w_ao_t, w_glu_t, w_out, comm=None):
    t = dx1b.shape[0]
    tm = 256

    def body(dx_ref, attn_ref, lt_ref, yg_ref, gate_ref, wao_ref, wglu_ref, wout_ref, ones_ref,
             dad_ref, dz_ref, dg_ref, da_ref, dyg_ref, rd_ref, *rest):
        dm = lax.dot_general(dx_ref[...], wout_ref[...], _NT, preferred_element_type=F32)
        attn_d, za, sb = _branch_outputs(attn_ref, yg_ref, wao_ref, wglu_ref)
        g0, g1 = gate_ref[:, :D_MODEL], gate_ref[:, D_MODEL:]
        dad = (dm * g0).astype(BF16)
        dad_ref[...] = dad
        ds = dm * g1
        dza, dzb = (ds * sb).astype(BF16), (ds * za * sb * (1.0 - sb)).astype(BF16)
        dz_ref[:, :D_MODEL] = dza
        dz_ref[:, D_MODEL:] = dzb
        dg_ref[:, :D_MODEL] = (dm * attn_d * g0 * (1.0 - g0)).astype(BF16)
        dg_ref[:, D_MODEL:] = (dm * (za * sb) * g1 * (1.0 - g1)).astype(BF16)
        da = jnp.dot(dad.astype(MXU_DTYPE), wao_ref[...], preferred_element_type=F32)
        da_ref[...] = da
        dyg_ref[...] = (jnp.dot(dza.astype(MXU_DTYPE), wglu_ref[:D_MODEL, :], preferred_element_type=F32)
                        + jnp.dot(dzb.astype(MXU_DTYPE), wglu_ref[D_MODEL:, :], preferred_element_type=F32))
        _attention_cotangents(da, attn_ref[...], lt_ref[...], ones_ref[...], rd_ref, rest[:6], rest[6], tm)

    widths = [(D_MODEL, BF16), (2 * D_MODEL, BF16), (2 * D_MODEL, BF16), (GROUP_W, F32), (SSM_W, F32), (GROUP_W, F32)]
    out_specs = [_row_spec(tm, wd) for wd, _ in widths]
    out_shapes = [jax.ShapeDtypeStruct((t, wd), dt) for wd, dt in widths]
    for d in DILATIONS[1:]:
        out_specs += [_row_spec(tm // d, d * GROUP_W)] * 3
        out_shapes += [jax.ShapeDtypeStruct((t // d, d * GROUP_W), F32)] * 3
    ones = _head_sum_matrix()
    return _grid_call(
        body, "mix_bwd", (t // tm, 1), [dx1b, attn, lse_tot, yg, gates, w_ao_t, w_glu_t, w_out, ones],
        [_row_spec(tm, D_MODEL), _row_spec(tm, GROUP_W), _row_spec(tm, GROUP_W), _row_spec(tm, SSM_W),
         _row_spec(tm, 2 * D_MODEL), _whole(w_ao_t), _whole(w_glu_t), _whole(w_out), _whole(ones)],
        out_specs, out_shapes, VMEM_BIG, comm, scratch=[pltpu.VMEM((6, tm, LANES), F32)])


FFN_TN = D_FF // 2
MXU_COLS = 256


def _ffn_in_swiglu(h2, w_gate_t, w_up_t, comm=None):
    t = h2.shape[0]
    tm = 512

    def body(h_ref, wg_ref, wu_ref, a_ref, b_ref, f_ref):
        h = h_ref[...].astype(MXU_DTYPE)
        for c0 in range(0, FFN_TN, MXU_COLS):
            sl = slice(c0, min(c0 + MXU_COLS, FFN_TN))
            a = lax.dot_general(h, wg_ref[sl, :], _NT, preferred_element_type=F32)
            b = lax.dot_general(h, wu_ref[sl, :], _NT, preferred_element_type=F32)
            a_ref[:, sl] = a
            b_ref[:, sl] = b
            f_ref[:, sl] = (a * _sigmoid(a) * b).astype(BF16)

    tile = pl.BlockSpec((tm, FFN_TN), lambda j, i: (i, j))
    wspec = pl.BlockSpec((FFN_TN, D_MODEL), lambda j, i: (j, 0))
    return _grid_call(
        body, "ffn_in_swiglu", (D_FF // FFN_TN, t // tm), [h2, w_gate_t, w_up_t],
        [pl.BlockSpec((tm, D_MODEL), lambda j, i: (i, 0)), wspec, wspec],
        [tile] * 3, [jax.ShapeDtypeStruct((t, D_FF), F32)] * 2 + [jax.ShapeDtypeStruct((t, D_FF), BF16)], VMEM_BIG, comm)


def _ffn_down_final(f, w_down, x1, target, g):
    t = x1.shape[0]
    tm = 256

    def body(f_ref, w_ref, x1_ref, t_ref, g_ref, dx_ref, dxb_ref, loss_ref, gg_ref):
        @pl.when(pl.program_id(0) == 0)
        def _():
            loss_ref[...] = jnp.zeros_like(loss_ref)
            gg_ref[...] = jnp.zeros_like(gg_ref)

        xv = x1_ref[...] + jnp.dot(f_ref[...].astype(MXU_DTYPE), w_ref[...], preferred_element_type=F32)
        gv = g_ref[...]
        r = lax.rsqrt(jnp.mean(xv * xv, axis=-1, keepdims=True) + RMS_EPS)
        n = xv * r
        diff = n * gv - t_ref[...]
        per_tok = jnp.mean(diff * diff, axis=-1, keepdims=True)
        loss_ref[...] += 0.5 * jnp.sum(per_tok, axis=0, keepdims=True)
        dy = diff / xv.shape[-1]
        gg_ref[...] += jnp.sum(dy * n, axis=0, keepdims=True)
        dn = dy * gv
        dx = r * (dn - n * jnp.mean(dn * n, axis=-1, keepdims=True))
        dx_ref[...] = dx
        dxb_ref[...] = dx.astype(BF16)

    acc = lambda shp: pl.BlockSpec(shp, lambda i, j: (0, 0))
    return _grid_call(
        body, "ffn_down_final", (t // tm, 1), [f, w_down, x1, target, g],
        [_row_spec(tm, D_FF), _whole(w_down), _row_spec(tm, D_MODEL), _row_spec(tm, D_MODEL), _whole(g)],
        [_row_spec(tm, D_MODEL)] * 2 + [acc((8, LANES)), acc((1, D_MODEL))],
        [jax.ShapeDtypeStruct((t, D_MODEL), F32), jax.ShapeDtypeStruct((t, D_MODEL), BF16),
         jax.ShapeDtypeStruct((8, LANES), F32), jax.ShapeDtypeStruct((1, D_MODEL), F32)], VMEM_BIG, sequential=True)


def _d_f_swiglu_bwd(dx2b, w_down, a, b):
    t = a.shape[0]
    tm = 512

    def body(dx_ref, w_ref, a_ref, b_ref, da_ref, db_ref):
        d = lax.dot_general(dx_ref[...], w_ref[...], _NT, preferred_element_type=F32)
        av, bv = a_ref[...], b_ref[...]
        sg = _sigmoid(av)
        da_ref[...] = (d * bv * sg * (1.0 + av * (1.0 - sg))).astype(BF16)
        db_ref[...] = (d * av * sg).astype(BF16)

    tile = pl.BlockSpec((tm, FFN_TN), lambda j, i: (i, j))
    return _grid_call(
        body, "d_f_swiglu_bwd", (D_FF // FFN_TN, t // tm), [dx2b, w_down, a, b],
        [pl.BlockSpec((tm, D_MODEL), lambda j, i: (i, 0)), pl.BlockSpec((FFN_TN, D_MODEL), lambda j, i: (j, 0)), tile, tile],
        [tile] * 2, [jax.ShapeDtypeStruct((t, D_FF), BF16)] * 2, VMEM_BIG)


def _mm_rms_bwd(operands, weights, x, g, dres, name, comm=None):
    t = x.shape[0]
    tm = 256
    n_op = len(operands)

    def body(*refs):
        a_refs, w_refs = refs[:n_op], refs[n_op:2 * n_op]
        x_ref, g_ref, dres_ref, dx_ref, dxb_ref, gg_ref = refs[2 * n_op:]

        @pl.when(pl.program_id(0) == 0)
        def _():
            gg_ref[...] = jnp.zeros_like(gg_ref)

        dh = None
        for a_ref, w_ref in zip(a_refs, w_refs):
            part = jnp.dot(a_ref[...].astype(MXU_DTYPE), w_ref[...], preferred_element_type=F32)
            dh = part if dh is None else dh + part
        xv = x_ref[...]
        r = lax.rsqrt(jnp.mean(xv * xv, axis=-1, keepdims=True) + RMS_EPS)
        n = xv * r
        gg_ref[...] += jnp.sum(dh * n, axis=0, keepdims=True)
        dn = dh * g_ref[...]
        dx = dres_ref[...] + r * (dn - n * jnp.mean(dn * n, axis=-1, keepdims=True))
        dx_ref[...] = dx
        dxb_ref[...] = dx.astype(BF16)

    d = x.shape[1]
    return _grid_call(
        body, name, (t // tm, 1), [*operands, *weights, x, g, dres],
        [_row_spec(tm, a.shape[1]) for a in operands] + [_whole(wk) for wk in weights]
        + [_row_spec(tm, d), _whole(g), _row_spec(tm, d)],
        [_row_spec(tm, d)] * 2 + [pl.BlockSpec((1, d), lambda i, j: (0, 0))],
        [jax.ShapeDtypeStruct((t, d), F32), jax.ShapeDtypeStruct((t, d), BF16), jax.ShapeDtypeStruct((1, d), F32)],
        VMEM_BIG, comm, sequential=True)


def _flat_small(small):
    perm_b = lambda a: a.reshape(SSM_GROUPS, SSM_STATE, SSM_CH).transpose(2, 0, 1).reshape(SSM_CH, N_STATE)
    perm_c = lambda a: a.reshape(SSM_GROUPS, SSM_CH, SSM_STATE).transpose(1, 0, 2).reshape(SSM_CH, N_STATE)
    return dict(
        g_mix=small["norm_mix_g"].reshape(1, D_MODEL), g_ffn=small["norm_ffn_g"].reshape(1, D_MODEL),
        g_fin=small["norm_final_g"].reshape(1, D_MODEL),
        lr=small["ssm_a_re"].reshape(1, N_STATE), li=small["ssm_a_im"].reshape(1, N_STATE),
        ldt=jnp.repeat(small["ssm_log_dt"].reshape(SSM_GROUPS), SSM_STATE).reshape(1, N_STATE),
        br=perm_b(small["ssm_b_re"]), bi=perm_b(small["ssm_b_im"]),
        cr=perm_c(small["ssm_c_re"]), ci=perm_c(small["ssm_c_im"]), dskip=small["ssm_d"].reshape(1, SSM_W))


AG_HOSTS = {"rms_mix": ("w_in",), "proj_rope": ("w_glu", "w_attn_out", "w_out", "w_ffn_gate"),
            "mix_out_rms": ("w_ffn_up",), "ffn_in_swiglu": ("w_ffn_down",)}
HALVED = ("w_ffn_gate", "w_ffn_up", "w_in")
A2A_HOSTS = {"d_h2_rms": ("w_ffn_down",), "mix_bwd": ("w_ffn_gate:0", "w_out"), "attn_bwd_g1": ("w_glu",),
             "attn_bwd_g2": ("w_attn_out",), "ssm_bwd": ("w_ffn_gate:1", "w_ffn_up:0", "w_ffn_up:1"),
             "mm_g_in1": ("w_in:0",), "d_h0_rms": ("w_in:1",)}
SMALL_HOST = "mm_g_in0"


def _local_step(x, target, w, small, shards=None):
    t = x.shape[0]
    n_samples = t // SEQ
    n_rows = n_samples * SCAN_SEG_PER_SAMPLE
    tabs = _rope_tables()
    w = dict(w)
    fs = _flat_small(small)
    g_mix, g_ffn, g_fin, dskip = fs["g_mix"], fs["g_ffn"], fs["g_fin"], fs["dskip"]
    a_cat, bbc, cc = _ssm_disc(fs["lr"], fs["li"], fs["ldt"], fs["br"], fs["bi"], fs["cr"], fs["ci"])
    big, recv, small_pack = {}, {}, []

    def comm_of(name):
        if shards is None:
            return None
        if name == SMALL_HOST:
            return _ag_comm([(small_pack[0], 0, 0)], [(N_DEV, *small_pack[0].shape)])
        if name in AG_HOSTS:
            names = AG_HOSTS[name]
            return _ag_comm([(shards[n], j, 0) for j, n in enumerate(names)], [(N_DEV, *shards[n].shape) for n in names])
        if name in A2A_HOSTS:
            return _a2a_comm([(big[n].reshape(N_DEV, -1, big[n].shape[1]), 0) for n in A2A_HOSTS[name]])
        return None

    def absorb(name, carried):
        if name == SMALL_HOST:
            recv["small"] = carried[0]
        for n, a3 in zip(AG_HOSTS.get(name, ()), carried):
            w[n] = a3.reshape(-1, a3.shape[2])
        for n, a3 in zip(A2A_HOSTS.get(name, ()), carried):
            recv[n] = a3

    def mm(a, b, mode, name, tm, tn, **kw):
        comm = comm_of(name)
        if comm is None:
            return _mm(a, b, mode, name, tm, tn, **kw)
        out, *carried = _mm(a, b, mode, name, tm, tn, comm=comm, **kw)
        absorb(name, carried)
        return out

    if shards is None:
        h0 = _rms_norm(x, g_mix, None)
    else:
        h0, *carried = _rms_norm(x, g_mix, comm_of("rms_mix"))
        absorb("rms_mix", carried)
    u, gates, *rest = _proj_rope(h0, w["w_in"], tabs, comm_of("proj_rope"))
    qkv = [rest[3 * g:3 * g + 3] for g in range(3)]
    absorb("proj_rope", rest[9:])
    os_, lses = [], []
    for g in range(3):
        o_g, l_g, carried = _attn_fwd(*qkv[g], g, n_samples, comm_of(f"attn_fwd_g{g}"))
        absorb(f"attn_fwd_g{g}", carried)
        os_.append(o_g)
        lses.append(l_g)
    u_perm = _to_scan_rows(u, n_samples)
    ytot, yg_perm, ein = _ssm_fwd(u_perm, a_cat, bbc, cc, dskip, n_rows)
    yg = _from_scan_rows(yg_perm, n_samples)

    attn, lse_tot, merged, x1, h2, *carried = _mix_out_rms(os_, lses, yg, gates, x, w["w_attn_out"], w["w_glu"], w["w_out"],
                                                           g_ffn, comm_of("mix_out_rms"))
    absorb("mix_out_rms", carried)
    ffn_a, ffn_b, f, *carried = _ffn_in_swiglu(h2, w["w_ffn_gate"], w["w_ffn_up"], comm_of("ffn_in_swiglu"))
    absorb("ffn_in_swiglu", carried)
    dx2, dx2b, loss_blk, g_gfin = _ffn_down_final(f, w["w_ffn_down"], x1, target, g_fin)

    da, db = _d_f_swiglu_bwd(dx2b, w["w_ffn_down"], ffn_a, ffn_b)
    big["w_ffn_down"] = mm(f, dx2b, "tn", "mm_g_down", 256, D_MODEL, out_dtype=BF16)
    half = D_MODEL // 2
    for hf in range(2):
        big[f"w_ffn_gate:{hf}"] = mm(da, h2, "tn", f"mm_g_gate{hf}", 256, half, out_dtype=BF16, cols=(hf * half, half))
        big[f"w_ffn_up:{hf}"] = mm(db, h2, "tn", f"mm_g_up{hf}", 256, half, out_dtype=BF16, cols=(hf * half, half))
    dx1, dx1b, g_gffn, *carried = _mm_rms_bwd([da, db], [w["w_ffn_gate"], w["w_ffn_up"]], x1, g_ffn, dx2, "d_h2_rms",
                                              comm_of("d_h2_rms"))
    absorb("d_h2_rms", carried)

    big["w_out"] = mm(merged, dx1b, "tn", "mm_g_out", 256, D_MODEL, out_dtype=BF16)
    dattn_d, dz, dgpre, dattn, dyg, rowdot, *rest = _mix_bwd(dx1b, attn, lse_tot, yg, gates, w["w_attn_out"], w["w_glu"],
                                                             w["w_out"], comm_of("mix_bwd"))
    cot = [(dattn, lse_tot, rowdot), tuple(rest[:3]), tuple(rest[3:6])]
    absorb("mix_bwd", rest[6:])

    big["w_attn_out"] = mm(dattn_d, attn, "tn", "mm_g_attn_out", 512, GROUP_W, out_dtype=BF16)
    big["w_glu"] = mm(dz, yg, "tn", "mm_g_glu", 512, 512, out_dtype=BF16)
    dqs, dks, dvs = [], [], []
    for g in range(3):
        dq_g, dk_g, dv_g, carried = _attn_bwd(*qkv[g], *cot[g], g, n_samples, comm_of(f"attn_bwd_g{g}"))
        absorb(f"attn_bwd_g{g}", carried)
        dqs.append(dq_g)
        dks.append(dk_g)
        dvs.append(dv_g)

    dyg_perm = _to_scan_rows(dyg, n_samples)
    du_perm, g_dskip, da_cat, dbb_full, dc_full, *carried = _ssm_bwd(u_perm, dyg_perm, ytot, dskip, a_cat, bbc, cc, ein,
                                                                   n_rows, comm_of("ssm_bwd"))
    absorb("ssm_bwd", carried)
    du = _from_scan_rows(du_perm, n_samples)
    g_lr, g_li, g_ldt, g_br, g_bi, g_cr, g_ci = _ssm_param_bwd(
        fs["lr"], fs["li"], fs["ldt"], fs["br"], fs["bi"], da_cat, dbb_full, dc_full)

    small_pack.append(_pack_small(dict(lr=g_lr, li=g_li, ldt=g_ldt, br=g_br, bi=g_bi, cr=g_cr, ci=g_ci, dskip=g_dskip,
                                       g_ffn=g_gffn, g_fin=g_gfin, loss=loss_blk)))

    dproj = _pack_dproj(dqs, dks, dvs, du, dgpre, tabs)
    for hf in range(2):
        big[f"w_in:{hf}"] = mm(dproj, h0, "tn", f"mm_g_in{hf}", 256, half, out_dtype=BF16, cols=(hf * half, half))
    grad_x, _, g_gmix, *carried = _mm_rms_bwd([dproj], [w["w_in"]], x, g_mix, dx1, "d_h0_rms", comm_of("d_h0_rms"))
    absorb("d_h0_rms", carried)
    return grad_x, (big if shards is None else recv), small_pack[0], g_gmix


_MESH = pl.DeviceIdType.MESH


def _all_gather(block, name):
    rows, lanes = block.shape

    def body(x_ref, out_ref, send_sems, recv_sems, local_sem):
        x, y, c = lax.axis_index("x"), lax.axis_index("y"), lax.axis_index("c")
        me, sibling = (x, y, c), (x, y, 1 - c)
        chips = [(1 - x, y), (x, 1 - y), (1 - x, 1 - y)]

        def slot(px, py, pc):
            return out_ref.at[4 * px + 2 * py + pc]

        def copy(k, blk, to, src=None):
            return pltpu.make_async_remote_copy(
                src_ref=slot(*blk) if src is None else src, dst_ref=slot(*blk), send_sem=send_sems.at[k],
                recv_sem=recv_sems.at[k], device_id=to, device_id_type=_MESH)

        mine = pltpu.make_async_copy(x_ref, slot(*me), local_sem)
        mine.start()
        first = [copy(0, me, sibling, src=x_ref)]
        first += [copy(1 + j, me, (*chip, c), src=x_ref) for j, chip in enumerate(chips)]
        for cp in first:
            cp.start()
        passed = [copy(4 + j, (*chip, c), sibling) for j, chip in enumerate(chips)]
        for j, chip in enumerate(chips):
            copy(1 + j, (*chip, c), me).wait_recv()
            passed[j].start()
        copy(0, sibling, me).wait_recv()
        for j, chip in enumerate(chips):
            copy(4 + j, (*chip, 1 - c), me).wait_recv()
        for cp in first + passed:
            cp.wait_send()
        mine.wait()

    return _pallas_call(
        body, name=name, out_shape=jax.ShapeDtypeStruct((N_DEV, rows, lanes), block.dtype),
        in_specs=[pl.BlockSpec(memory_space=pl.ANY)], out_specs=pl.BlockSpec(memory_space=pl.ANY),
        scratch_shapes=[pltpu.SemaphoreType.DMA((7,)), pltpu.SemaphoreType.DMA((7,)), pltpu.SemaphoreType.DMA],
    )(block)


def _ag_comm(items, bufs):
    def plan(in_refs, out_refs, send_sems, recv_sems, local_sems):
        x, y, c = lax.axis_index("x"), lax.axis_index("y"), lax.axis_index("c")
        me, sibling = (x, y, c), (x, y, 1 - c)
        chips = [(1 - x, y), (x, 1 - y), (1 - x, 1 - y)]
        plans = []
        for t, (_, buf, slot0) in enumerate(items):
            x_ref, out_ref = in_refs[t], out_refs[buf]

            def slot(px, py, pc, out_ref=out_ref, slot0=slot0):
                return out_ref.at[slot0 + 4 * px + 2 * py + pc]

            def copy(k, blk, to, src=None, t=t, slot=slot):
                return pltpu.make_async_remote_copy(
                    src_ref=slot(*blk) if src is None else src, dst_ref=slot(*blk), send_sem=send_sems.at[7 * t + k],
                    recv_sem=recv_sems.at[7 * t + k], device_id=to, device_id_type=_MESH)

            plans.append(dict(
                mine=pltpu.make_async_copy(x_ref, slot(*me), local_sems.at[t]),
                first=[copy(0, me, sibling, src=x_ref)] + [copy(1 + j, me, (*chip, c), src=x_ref)
                                                           for j, chip in enumerate(chips)],
                passed=[copy(4 + j, (*chip, c), sibling) for j, chip in enumerate(chips)],
                from_ici=[copy(1 + j, (*chip, c), me) for j, chip in enumerate(chips)],
                from_sibling=[copy(0, sibling, me)] + [copy(4 + j, (*chip, 1 - c), me) for j, chip in enumerate(chips)]))
        return plans

    def start(*refs):
        for p in plan(*refs):
            p["mine"].start()
            for cp in p["first"]:
                cp.start()

    def finish(*refs):
        plans = plan(*refs)
        for p in plans:
            for arrived, onward in zip(p["from_ici"], p["passed"]):
                arrived.wait_recv()
                onward.start()
        for p in plans:
            for arrived in p["from_sibling"]:
                arrived.wait_recv()
            for cp in p["first"] + p["passed"]:
                cp.wait_send()
            p["mine"].wait()

    dtype_of = {buf: shard.dtype for shard, buf, _ in items}
    out_shapes = [jax.ShapeDtypeStruct(b, dtype_of[j]) for j, b in enumerate(bufs)]
    return _Comm([it[0] for it in items], out_shapes, 7 * len(items), len(items), start, finish)


def _a2a_comm(items):
    def plan(in_refs, out_refs, send_sems, recv_sems, local_sems):
        x, y, c = lax.axis_index("x"), lax.axis_index("y"), lax.axis_index("c")
        my = 4 * x + 2 * y + c
        copies, locals_ = [], []
        for t, (_, slot0) in enumerate(items):
            s_ref, r_ref = in_refs[t], out_refs[t]
            locals_.append(pltpu.make_async_copy(s_ref.at[slot0 + my], r_ref.at[my], local_sems.at[t]))
            for kk in range(1, N_DEV):
                px = 1 - x if kk & 4 else x
                py = 1 - y if kk & 2 else y
                pc = 1 - c if kk & 1 else c
                copies.append(pltpu.make_async_remote_copy(
                    src_ref=s_ref.at[slot0 + 4 * px + 2 * py + pc], dst_ref=r_ref.at[my],
                    send_sem=send_sems.at[7 * t + kk - 1], recv_sem=recv_sems.at[7 * t + kk - 1],
                    device_id=(px, py, pc), device_id_type=_MESH))
        return copies, locals_

    def start(*refs):
        copies, locals_ = plan(*refs)
        for cp in locals_ + copies:
            cp.start()

    def finish(*refs):
        copies, locals_ = plan(*refs)
        for cp in copies + locals_:
            cp.wait()

    out_shapes = [jax.ShapeDtypeStruct((N_DEV,) + it[0].shape[1:], it[0].dtype) for it in items]
    return _Comm([it[0] for it in items], out_shapes, 7 * len(items), len(items), start, finish)


def _adam_math(g, w, m, v):
    m_new = ADAM_B1 * m + (1.0 - ADAM_B1) * g
    v_new = ADAM_B2 * v + (1.0 - ADAM_B2) * jnp.square(g)
    m_hat = m_new / (1.0 - ADAM_B1 ** ADAM_STEP)
    v_hat = v_new / (1.0 - ADAM_B2 ** ADAM_STEP)
    return -ADAM_LR * (m_hat / (jnp.sqrt(v_hat) + ADAM_EPS) + ADAM_WD * w), m_new, v_new


def _sum_partials(parts, name, tm):
    n, rows, _ = parts[0].shape
    widths = [p.shape[2] for p in parts]

    def body(*refs):
        g_ref, off = refs[-1], 0
        for p_ref, wd in zip(refs[:-1], widths):
            g = p_ref[0].astype(F32)
            for s in range(1, n):
                g = g + p_ref[s].astype(F32)
            g_ref[:, off:off + wd] = g
            off += wd

    return _pallas_call(
        body, name=name, grid=(rows // tm,), in_specs=[pl.BlockSpec((n, tm, wd), lambda i: (0, i, 0)) for wd in widths],
        out_specs=pl.BlockSpec((tm, sum(widths)), lambda i: (i, 0)),
        out_shape=jax.ShapeDtypeStruct((rows, sum(widths)), F32),
        compiler_params=pltpu.CompilerParams(dimension_semantics=("parallel",), vmem_limit_bytes=VMEM_MID),
    )(*parts)


def _adam(parts, w, m, v, name, tm):
    n, rows, _ = parts[0].shape
    widths = [p.shape[2] for p in parts]
    cols = sum(widths)

    def body(*refs):
        p_refs, (w_ref, m_ref, v_ref, g_ref, d_ref, nm_ref, nv_ref) = refs[:len(parts)], refs[len(parts):]
        off = 0
        for p_ref, wd in zip(p_refs, widths):
            g = p_ref[0].astype(F32)
            for s in range(1, n):
                g = g + p_ref[s].astype(F32)
            sl = slice(off, off + wd)
            g_ref[:, sl] = g
            d_ref[:, sl], nm_ref[:, sl], nv_ref[:, sl] = _adam_math(g, w_ref[:, sl], m_ref[:, sl], v_ref[:, sl])
            off += wd

    assert rows % tm == 0
    row = pl.BlockSpec((tm, cols), lambda i: (i, 0))
    shp = jax.ShapeDtypeStruct((rows, cols), F32)
    return _pallas_call(
        body, name=name, grid=(rows // tm,),
        in_specs=[pl.BlockSpec((n, tm, wd), lambda i: (0, i, 0)) for wd in widths] + [row, row, row],
        out_specs=[row] * 4, out_shape=[shp] * 4,
        compiler_params=pltpu.CompilerParams(dimension_semantics=("parallel",), vmem_limit_bytes=VMEM_MID),
    )(*parts, w, m, v)


_PK_LR, _PK_LI, _PK_GAINS, _PK_MISC, _PK_BR, _PK_BI, _PK_CR, _PK_CI, _PK_ROWS = 0, 1, 2, 3, 8, 24, 40, 56, 72
_PK_LDT_LANE, _PK_LOSS_LANE = D_MODEL + SSM_W, D_MODEL + SSM_W + LANES


def _pack_small(sg):
    names = ("lr", "li", "g_ffn", "g_fin", "dskip", "ldt", "loss", "br", "bi", "cr", "ci")

    def body(lr, li, gffn, gfin, dskip, ldt, loss, br, bi, cr, ci, o_ref):
        o_ref[...] = jnp.zeros_like(o_ref)
        o_ref[_PK_LR:_PK_LR + 1, :] = lr[...]
        o_ref[_PK_LI:_PK_LI + 1, :] = li[...]
        o_ref[_PK_GAINS:_PK_GAINS + 1, D_MODEL:] = gffn[...]
        o_ref[_PK_MISC:_PK_MISC + 1, :D_MODEL] = gfin[...]
        o_ref[_PK_MISC:_PK_MISC + 1, D_MODEL:D_MODEL + SSM_W] = dskip[...]
        o_ref[_PK_MISC:_PK_MISC + 1, _PK_LDT_LANE:_PK_LDT_LANE + LANES] = ldt[0:1, :]
        o_ref[_PK_MISC:_PK_MISC + 1, _PK_LOSS_LANE:_PK_LOSS_LANE + LANES] = loss[0:1, :]
        o_ref[_PK_BR:_PK_BR + SSM_CH, :] = br[...]
        o_ref[_PK_BI:_PK_BI + SSM_CH, :] = bi[...]
        o_ref[_PK_CR:_PK_CR + SSM_CH, :] = cr[...]
        o_ref[_PK_CI:_PK_CI + SSM_CH, :] = ci[...]

    return _pallas_call(body, name="pack_small", out_shape=jax.ShapeDtypeStruct((_PK_ROWS, N_STATE), F32))(
        *[sg[n] for n in names])


def _unpack_small(s, g_mix):
    unflat_b = unflat_c = lambda a: a.reshape(SSM_CH, SSM_GROUPS, SSM_STATE).transpose(1, 0, 2)[None]
    grads = {
        "norm_mix_g": g_mix, "norm_ffn_g": s[_PK_GAINS, D_MODEL:].reshape(1, D_MODEL),
        "norm_final_g": s[_PK_MISC, :D_MODEL].reshape(1, D_MODEL),
        "ssm_a_re": s[_PK_LR].reshape(1, SSM_GROUPS, SSM_STATE), "ssm_a_im": s[_PK_LI].reshape(1, SSM_GROUPS, SSM_STATE),
        "ssm_log_dt": s[_PK_MISC, _PK_LDT_LANE:_PK_LDT_LANE + SSM_GROUPS].reshape(1, SSM_GROUPS),
        "ssm_d": s[_PK_MISC, D_MODEL:D_MODEL + SSM_W].reshape(1, SSM_GROUPS, SSM_CH),
        "ssm_b_re": unflat_b(s[_PK_BR:_PK_BR + SSM_CH]), "ssm_b_im": unflat_b(s[_PK_BI:_PK_BI + SSM_CH]),
        "ssm_c_re": unflat_c(s[_PK_CR:_PK_CR + SSM_CH]), "ssm_c_im": unflat_c(s[_PK_CI:_PK_CI + SSM_CH]),
    }
    return s[_PK_MISC, _PK_LOSS_LANE], grads


def _stored(name, a):
    if name in ("ssm_b_re", "ssm_b_im"):
        return a.transpose(0, 1, 3, 2)
    return a.reshape(1, -1) if a.ndim == 1 else a


def _unstored(name, a, like):
    return a.transpose(0, 1, 3, 2) if name in ("ssm_b_re", "ssm_b_im") else a.reshape(like.shape)


def _adam_small(grads, wts, moms, vars_):
    n = len(SMALL_WEIGHTS)

    def body(*refs):
        ins, outs = refs[:4 * n], refs[4 * n:]
        for i in range(n):
            g, w, m, v = (ins[j * n + i][...] for j in range(4))
            outs[i][...], outs[n + i][...], outs[2 * n + i][...] = _adam_math(g, w, m, v)

    operands = [grads[k] if d is grads else _stored(k, d[k]) for d in (grads, wts, moms, vars_) for k in SMALL_WEIGHTS]
    shapes = [jax.ShapeDtypeStruct(_stored(k, wts[k]).shape, F32) for k in SMALL_WEIGHTS] * 3
    res = _pallas_call(body, name="adam_small", out_shape=shapes,
                         compiler_params=pltpu.CompilerParams(vmem_limit_bytes=VMEM_BIG))(*operands)
    out = {}
    for j, kind in enumerate(("delta", "new_m", "new_v")):
        for i, k in enumerate(SMALL_WEIGHTS):
            out[kind, k] = _unstored(k, res[j * n + i], wts[k])
    return out


def kernel(x, norm_mix_g, w_in, ssm_a_re, ssm_a_im, ssm_log_dt, ssm_b_re, ssm_b_im, ssm_c_re, ssm_c_im, ssm_d, w_glu, w_attn_out, w_out, norm_ffn_g, w_ffn_gate, w_ffn_up, w_ffn_down, norm_final_g, loss_target, m_norm_mix_g, m_w_in, m_ssm_a_re, m_ssm_a_im, m_ssm_log_dt, m_ssm_b_re, m_ssm_b_im, m_ssm_c_re, m_ssm_c_im, m_ssm_d, m_w_glu, m_w_attn_out, m_w_out, m_norm_ffn_g, m_w_ffn_gate, m_w_ffn_up, m_w_ffn_down, m_norm_final_g, v_norm_mix_g, v_w_in, v_ssm_a_re, v_ssm_a_im, v_ssm_log_dt, v_ssm_b_re, v_ssm_b_im, v_ssm_c_re, v_ssm_c_im, v_ssm_d, v_w_glu, v_w_attn_out, v_w_out, v_norm_ffn_g, v_w_ffn_gate, v_w_ffn_up, v_w_ffn_down, v_norm_final_g):
    args = dict(locals())
    wts = {n: args[n] for n in ALL_WEIGHTS}
    moms = {n: args["m_" + n] for n in ALL_WEIGHTS}
    vars_ = {n: args["v_" + n] for n in ALL_WEIGHTS}
    n_samples = x.shape[0]
    t = n_samples * SEQ

    shards = {n: (wts[n][0] if n in ROW_SHARDED else wts[n][0].T).astype(BF16) for n in BIG_WEIGHTS}
    small = {n: wts[n] for n in SMALL_WEIGHTS}
    grad_x, recv, _, g_mix_part = _local_step(x.reshape(t, D_MODEL), loss_target.reshape(t, D_MODEL), {}, small, shards)

    results = {}
    for n in BIG_WEIGHTS:
        c, k = shards[n].shape
        w2, m2, v2 = wts[n][0], moms[n][0], vars_[n][0]
        if n in ROW_SHARDED:
            res = _adam([recv[n]], w2, m2, v2, "adam_" + n, c // 2)
        elif n in HALVED:
            res = _adam([recv[f"{n}:{hf}"] for hf in range(2)], w2.T, m2.T, v2.T, "adam_" + n, c // 2)
            res = [a.T for a in res]
        else:
            g_t = _sum_partials([recv[n]], "sum_" + n, c // 2)
            res = _adam([g_t.T[None]], w2, m2, v2, "adam_" + n, k // 2)
        for kind, a in zip(("grad", "delta", "new_m", "new_v"), res):
            results[kind, n] = a[None]

    g_mix_all = _all_gather(jnp.pad(g_mix_part, ((0, 7), (0, 0))), "allgather_g_mix")
    g_mix = _sum_partials([g_mix_all], "sum_g_mix", 8)[0:1]
    loss, sgrads = _unpack_small(_sum_partials([recv["small"]], "sum_small", _PK_ROWS), g_mix)
    for n in SMALL_WEIGHTS:
        results["grad", n] = _unstored(n, sgrads[n], wts[n])
    results.update(_adam_small(sgrads, wts, moms, vars_))
    outs = [loss, grad_x.reshape(x.shape)]
    for kind in ("grad", "delta", "new_m", "new_v"):
        outs += [results[kind, n] for n in ALL_WEIGHTS]
    return tuple(outs)
```

```python
import functools
import math

import jax
import jax.numpy as jnp
from jax import lax
from jax.experimental import pallas as pl
from jax.experimental.pallas import tpu as pltpu

F32 = jnp.float32
BF16 = jnp.bfloat16
MXU_DTYPE = jnp.bfloat16

N_DEV = 8
D_MODEL = 1024
SEQ = 2048
HEAD_DIM = 64
HEADS_PER_GROUP = 4
GROUP_W = HEADS_PER_GROUP * HEAD_DIM
DILATIONS = (1, 4, 16)
QKV_W = 3 * len(DILATIONS) * GROUP_W
Q_W = len(DILATIONS) * GROUP_W
ATT_BLOCK = 128
ROPE_DIM = 16
ROPE_THETA = 500000.0
SSM_W = 512
SSM_GROUPS = 32
SSM_CH = 16
SSM_STATE = 64
N_STATE = SSM_GROUPS * SSM_STATE
D_FF = 2816
IN_W = QKV_W + SSM_W + 2 * D_MODEL
RMS_EPS = 1e-6
NEG_INF = -1e30
LANES = 128

SCAN_SEG_PER_SAMPLE = 8
SCAN_LEN = SEQ // SCAN_SEG_PER_SAMPLE
SCAN_WC = 512
SCAN_NBLK = N_STATE // SCAN_WC
SCAN_CH = SSM_W // SCAN_NBLK
SCAN_CHUNK = 32

ADAM_LR = 0.001
ADAM_B1 = 0.9
ADAM_B2 = 0.999
ADAM_EPS = 1e-08
ADAM_WD = 0.01
ADAM_STEP = 10

VMEM_BIG = 48 * 1024 * 1024
VMEM_MID = 32 * 1024 * 1024

BIG_WEIGHTS = ("w_in", "w_glu", "w_attn_out", "w_out", "w_ffn_gate", "w_ffn_up", "w_ffn_down")
ROW_SHARDED = ("w_out", "w_ffn_down")
SMALL_WEIGHTS = ("norm_mix_g", "ssm_a_re", "ssm_a_im", "ssm_log_dt", "ssm_b_re", "ssm_b_im", "ssm_c_re", "ssm_c_im",
                 "ssm_d", "norm_ffn_g", "norm_final_g")
ALL_WEIGHTS = ("norm_mix_g", "w_in", "ssm_a_re", "ssm_a_im", "ssm_log_dt", "ssm_b_re", "ssm_b_im", "ssm_c_re", "ssm_c_im",
               "ssm_d", "w_glu", "w_attn_out", "w_out", "norm_ffn_g", "w_ffn_gate", "w_ffn_up", "w_ffn_down", "norm_final_g")


def _sigmoid(x):
    return 1.0 / (1.0 + jnp.exp(-x))


def _pallas_call(body, *, out_shape, **kw):
    single = not isinstance(out_shape, (list, tuple))
    shapes = [pltpu.HBM(s.shape, s.dtype) for s in ([out_shape] if single else out_shape)]
    call = pl.pallas_call(body, out_shape=shapes[0] if single else shapes, **kw)
    return lambda *operands: call(*[pltpu.with_memory_space_constraint(o, pltpu.HBM) for o in operands])


class _Comm:
    def __init__(self, ins, out_shapes, n_sem, n_local, start, finish):
        self.ins, self.out_shapes, self.n_sem, self.n_local = ins, out_shapes, n_sem, n_local
        self.start, self.finish = start, finish


def _mm(a, b, mode, name, tm, tn, out_dtype=F32, add=None, vmem=VMEM_BIG, comm=None, cols=None):
    if mode == "nn":
        (m, k), (_, n) = a.shape, b.shape
        a_spec = pl.BlockSpec((tm, k), lambda i, j: (i, 0))
        b_spec = pl.BlockSpec((k, tn), lambda i, j: (0, j))
        dims = (((1,), (0,)), ((), ()))
    elif mode == "nt":
        (m, k), (n, _) = a.shape, b.shape
        a_spec = pl.BlockSpec((tm, k), lambda i, j: (i, 0))
        b_spec = pl.BlockSpec((tn, k), lambda i, j: (j, 0))
        dims = (((1,), (1,)), ((), ()))
    else:
        (k, m), (_, n) = a.shape, b.shape
        first, n = cols if cols else (0, n)
        a_spec = pl.BlockSpec((k, tm), lambda i, j: (0, i))
        b_spec = pl.BlockSpec((k, tn), lambda i, j: (0, j + first // tn))
        dims = (((0,), (0,)), ((), ()))
    assert m % tm == 0 and n % tn == 0, (name, m, n, tm, tn)
    o_spec = pl.BlockSpec((tm, tn), lambda i, j: (i, j))
    has_add = add is not None

    def body(*refs):
        a_ref, b_ref, o_ref = refs[0], refs[1], refs[-1]
        acc = lax.dot_general(a_ref[...].astype(MXU_DTYPE), b_ref[...].astype(MXU_DTYPE), dims,
                              preferred_element_type=F32)
        if has_add:
            acc = acc + refs[2][...]
        o_ref[...] = acc.astype(out_dtype)

    ins = [a, b] + ([add] if has_add else [])
    in_specs = [a_spec, b_spec] + ([o_spec] if has_add else [])
    return _grid_call(body, name, (m // tm, n // tn), ins, in_specs, [o_spec],
                      [jax.ShapeDtypeStruct((m, n), out_dtype)], vmem, comm)


def _grid_call(body, name, grid, ins, in_specs, out_specs, out_shapes, vmem, comm=None, sequential=False, scratch=()):
    if comm is None:
        single = len(out_shapes) == 1
        semantics = ("arbitrary", "arbitrary") if sequential else ("parallel", "parallel")
        return _pallas_call(
            body, name=name, grid=grid, in_specs=in_specs, out_specs=out_specs[0] if single else out_specs,
            out_shape=out_shapes[0] if single else out_shapes, scratch_shapes=list(scratch),
            compiler_params=pltpu.CompilerParams(dimension_semantics=semantics, vmem_limit_bytes=vmem),
        )(*ins)
    n_in, n_out, n_cin, n_cout = len(ins), len(out_shapes), len(comm.ins), len(comm.out_shapes)
    n_io = n_in + n_cin + n_out + n_cout

    def carrying(*refs):
        own = refs[:n_in] + refs[n_in + n_cin:n_in + n_cin + n_out] + refs[n_io:len(refs) - 3]
        c_args = (refs[n_in:n_in + n_cin], refs[n_in + n_cin + n_out:n_io], *refs[-3:])

        @pl.when((pl.program_id(0) == 0) & (pl.program_id(1) == 0))
        def _():
            comm.start(*c_args)

        body(*own)

        @pl.when((pl.program_id(0) == grid[0] - 1) & (pl.program_id(1) == grid[1] - 1))
        def _():
            comm.finish(*c_args)

    hbm = pl.BlockSpec(memory_space=pl.ANY)
    return _pallas_call(
        carrying, name=name, grid=grid, in_specs=list(in_specs) + [hbm] * n_cin,
        out_specs=list(out_specs) + [hbm] * n_cout, out_shape=list(out_shapes) + list(comm.out_shapes),
        scratch_shapes=list(scratch) + [pltpu.SemaphoreType.DMA((comm.n_sem,)), pltpu.SemaphoreType.DMA((comm.n_sem,)),
                                        pltpu.SemaphoreType.DMA((comm.n_local,))],
        compiler_params=pltpu.CompilerParams(dimension_semantics=("arbitrary", "arbitrary"), vmem_limit_bytes=vmem),
    )(*ins, *comm.ins)


def _rows(body, name, n_rows, tm, ins, outs, vmem=VMEM_MID, scratch=()):
    assert n_rows % tm == 0
    arrays, in_specs = [], []
    for kind, arr in ins:
        arrays.append(arr)
        if kind == "row":
            assert n_rows % arr.shape[0] == 0, (name, arr.shape)
            in_specs.append(pl.BlockSpec((tm * arr.shape[0] // n_rows, arr.shape[1]), lambda i: (i, 0)))
        elif kind == "tab":
            nblk = arr.shape[0] // tm
            in_specs.append(pl.BlockSpec((tm, arr.shape[1]), lambda i, nblk=nblk: (i % nblk, 0)))
        else:
            in_specs.append(pl.BlockSpec(arr.shape, lambda i, nd=arr.ndim: (0,) * nd))
    out_specs, out_shape = [], []
    for kind, shp, dt in outs:
        if kind == "row":
            out_specs.append(pl.BlockSpec((tm, shp), lambda i: (i, 0)))
            out_shape.append(jax.ShapeDtypeStruct((n_rows, shp), dt))
        elif kind == "dil":
            d, wd = shp
            out_specs.append(pl.BlockSpec((tm // d, d * wd), lambda i: (i, 0)))
            out_shape.append(jax.ShapeDtypeStruct((n_rows // d, d * wd), dt))
        else:
            out_specs.append(pl.BlockSpec(shp, lambda i, nd=len(shp): (0,) * nd))
            out_shape.append(jax.ShapeDtypeStruct(shp, dt))
    res = _pallas_call(
        body, name=name, grid=(n_rows // tm,), in_specs=in_specs, out_specs=out_specs, out_shape=out_shape,
        scratch_shapes=list(scratch),
        compiler_params=pltpu.CompilerParams(dimension_semantics=("arbitrary",), vmem_limit_bytes=vmem),
    )(*arrays)
    return res


def _gather_residue(stage, ch, r, d, n):
    return stage[ch, pl.ds(r, n, stride=d), :] if d > 1 else stage[ch]


def _scatter_residue(stage, ch, r, d, n, val):
    if d > 1:
        stage[ch, pl.ds(r, n, stride=d), :] = val
    else:
        stage[ch] = val


def _lane_chunk(ch):
    return slice(ch * LANES, (ch + 1) * LANES)


def _rope_tables():
    half = ROPE_DIM // 2
    inv = jnp.power(jnp.float32(ROPE_THETA), -jnp.arange(half, dtype=F32) * 2.0 / ROPE_DIM)
    ang = jnp.arange(SEQ, dtype=F32)[:, None] * inv[None, :]
    lane = jnp.arange(LANES) % HEAD_DIM
    cosl = jnp.cos(ang)[:, lane % half]
    sinl = jnp.sin(ang)[:, lane % half]
    tab_c = jnp.where(lane < ROPE_DIM, cosl, 1.0)
    tab_lo = jnp.where(lane < half, -sinl, 0.0)
    tab_hi = jnp.where((lane >= half) & (lane < ROPE_DIM), sinl, 0.0)
    return tab_c.astype(F32), tab_lo.astype(F32), tab_hi.astype(F32)


def _rope_apply(t, tc, tlo, thi):
    half = ROPE_DIM // 2
    return t * tc + pltpu.roll(t, LANES - half, 1) * tlo + pltpu.roll(t, half, 1) * thi


def _rope_transpose(dt, tc, tlo, thi):
    half = ROPE_DIM // 2
    return dt * tc + pltpu.roll(dt * tlo, half, 1) + pltpu.roll(dt * thi, LANES - half, 1)


def _pack_dproj(dqs, dks, dvs, du, dgpre, tabs):
    tm = 256

    def body(*refs):
        dq_refs, dk_refs, dv_refs = refs[0:3], refs[3:6], refs[6:9]
        du_ref, dg_ref, tc_ref, tlo_ref, thi_ref, o_ref, stage = refs[9:16]
        n_ch = QKV_W // LANES
        halves = GROUP_W // LANES
        for grp, d in enumerate(DILATIONS):
            for which, src in enumerate((dq_refs[grp], dk_refs[grp], dv_refs[grp])):
                for res in range(d):
                    for half in range(halves):
                        _scatter_residue(stage, which * (n_ch // 3) + grp * halves + half, res, d, tm // d,
                                         src[:, _lane_chunk(res * halves + half)])
        tc, tlo, thi = tc_ref[...], tlo_ref[...], thi_ref[...]
        for ch in range(n_ch):
            piece = stage[ch]
            o_ref[:, _lane_chunk(ch)] = (_rope_transpose(piece, tc, tlo, thi) if ch < 2 * n_ch // 3 else piece).astype(BF16)
        o_ref[:, QKV_W:QKV_W + SSM_W] = du_ref[...].astype(BF16)
        o_ref[:, QKV_W + SSM_W:] = dg_ref[...].astype(BF16)

    t = du.shape[0]
    ins = [("row", a) for a in (*dqs, *dks, *dvs, du, dgpre)] + [("tab", tb) for tb in tabs]
    return _rows(body, "pack_dproj", t, tm, ins, [("row", IN_W, BF16)],
                 scratch=[pltpu.VMEM((QKV_W // LANES, tm, LANES), F32)])[0]


def _merge_groups(o_refs, l_refs, a_ref, lt_ref, nat, tm):
    halves = GROUP_W // LANES
    for grp, d in enumerate(DILATIONS[1:], start=1):
        for j, src in enumerate((o_refs[grp], l_refs[grp])):
            for res in range(d):
                for half in range(halves):
                    _scatter_residue(nat, (grp - 1) * 4 + j * 2 + half, res, d, tm // d,
                                     src[:, _lane_chunk(res * halves + half)])
    for half in range(halves):
        sl = _lane_chunk(half)
        la, lb, lc = l_refs[0][:, sl], nat[2 + half], nat[6 + half]
        m = jnp.maximum(jnp.maximum(la, lb), lc)
        ea, eb, ec = jnp.exp(la - m), jnp.exp(lb - m), jnp.exp(lc - m)
        ssum = ea + eb + ec
        a_ref[:, sl] = (ea / ssum) * o_refs[0][:, sl] + (eb / ssum) * nat[half] + (ec / ssum) * nat[4 + half]
        lt_ref[:, sl] = m + jnp.log(ssum)


def _head_sum_matrix():
    r = jnp.arange(GROUP_W) // HEAD_DIM
    return (r[:, None] == r[None, :]).astype(F32)


def _attention_cotangents(da, attn, lt, ones, rd_ref, dil, stage, tm):
    halves = GROUP_W // LANES
    rd = jnp.dot(da * attn, ones, preferred_element_type=F32, precision=lax.Precision.HIGHEST)
    rd_ref[...] = rd
    for half in range(halves):
        for j, val in enumerate((da, lt, rd)):
            stage[2 * j + half] = val[:, _lane_chunk(half)]
    for grp, d in enumerate(DILATIONS[1:], start=1):
        for j in range(3):
            for res in range(d):
                for half in range(halves):
                    dil[3 * (grp - 1) + j][:, _lane_chunk(res * halves + half)] = _gather_residue(
                        stage, 2 * j + half, res, d, tm // d)


_GELU_C = math.sqrt(2.0 / math.pi)


def _head_masks():
    lane = lax.broadcasted_iota(jnp.int32, (1, GROUP_W), 1)
    return [(lane // HEAD_DIM) == h for h in range(HEADS_PER_GROUP)]


def _stack_heads(blk, masks, fill=0.0):
    return jnp.concatenate([jnp.where(mk, blk, jnp.full_like(blk, fill)) for mk in masks], axis=0)


def _unstack_heads(stacked, masks):
    rows = stacked.shape[0] // len(masks)
    out = stacked[:rows]
    for h in range(1, len(masks)):
        out = jnp.where(masks[h], stacked[h * rows:(h + 1) * rows], out)
    return out


def _band_mask(first):
    nk = ATT_BLOCK if first else 2 * ATT_BLOCK
    qi = lax.broadcasted_iota(jnp.int32, (ATT_BLOCK, nk), 0)
    ki = lax.broadcasted_iota(jnp.int32, (ATT_BLOCK, nk), 1)
    dist = qi - ki + (0 if first else ATT_BLOCK)
    return (dist >= 0) & (dist <= ATT_BLOCK)


_NT = (((1,), (1,)), ((), ()))
_TN = (((0,), (0,)), ((), ()))


def _residues_per_step(d):
    return 4 if d >= 16 else 1


def _attn_fwd(q, k, v, group, n_samples, comm=None):
    d = DILATIONS[group]
    length = SEQ // d
    nb = length // ATT_BLOCK

    rps = _residues_per_step(d)

    def body(q_ref, k_ref, v_ref, o_ref, l_ref):
        for rl in range(rps):
            residue(q_ref, k_ref, v_ref, o_ref, l_ref, slice(rl * GROUP_W, (rl + 1) * GROUP_W))

    def residue(q_ref, k_ref, v_ref, o_ref, l_ref, cols):
        masks = _head_masks()

        def block(qs, ks, first):
            nk = ATT_BLOCK if first else 2 * ATT_BLOCK
            qb = q_ref[0, pl.ds(qs, ATT_BLOCK), cols]
            kc = k_ref[0, pl.ds(ks, nk), cols]
            vc = v_ref[0, pl.ds(ks, nk), cols]
            q4 = _stack_heads(qb, masks)
            valid = jnp.tile(_band_mask(first), (HEADS_PER_GROUP, 1))
            s = lax.dot_general(q4, kc, _NT, preferred_element_type=F32) * (HEAD_DIM ** -0.5)
            s = jnp.where(valid, s, NEG_INF)
            m = jnp.max(s, axis=-1, keepdims=True)
            p = jnp.exp(s - m)
            l = jnp.sum(p, axis=-1, keepdims=True)
            o4 = jnp.dot(p.astype(MXU_DTYPE), vc, preferred_element_type=F32) / l
            lse4 = jnp.broadcast_to(m + jnp.log(l), o4.shape)
            o_ref[0, pl.ds(qs, ATT_BLOCK), cols] = _unstack_heads(o4, masks)
            l_ref[0, pl.ds(qs, ATT_BLOCK), cols] = _unstack_heads(lse4, masks)

        block(0, 0, True)
        if nb > 1:
            def loop(n, carry):
                block(pl.multiple_of(n * ATT_BLOCK, ATT_BLOCK), pl.multiple_of((n - 1) * ATT_BLOCK, ATT_BLOCK), False)
                return carry

            lax.fori_loop(1, nb, loop, 0)

    per_sample = lambda a: a.reshape(n_samples, length, d * GROUP_W)
    spec = pl.BlockSpec((1, length, rps * GROUP_W), lambda b, r: (b, 0, r))
    shp = jax.ShapeDtypeStruct((n_samples, length, d * GROUP_W), F32)
    o, lse, *carried = _grid_call(body, f"attn_fwd_g{group}", (n_samples, d // rps), [per_sample(a) for a in (q, k, v)],
                                  [spec] * 3, [spec] * 2, [shp, shp], VMEM_MID, comm)
    flat = lambda a: a.reshape(n_samples * length, d * GROUP_W)
    return flat(o), flat(lse), carried


def _attn_bwd(q, k, v, dattn, lse_tot, rowdot, group, n_samples, comm=None):
    d = DILATIONS[group]
    length = SEQ // d
    nb = length // ATT_BLOCK

    rps = _residues_per_step(d)

    def body(q_ref, k_ref, v_ref, da_ref, lt_ref, rd_ref, dq_ref, dk_ref, dv_ref):
        dk_ref[...] = jnp.zeros_like(dk_ref)
        dv_ref[...] = jnp.zeros_like(dv_ref)
        for rl in range(rps):
            residue(q_ref, k_ref, v_ref, da_ref, lt_ref, rd_ref, dq_ref, dk_ref, dv_ref,
                    slice(rl * GROUP_W, (rl + 1) * GROUP_W))

    def residue(q_ref, k_ref, v_ref, da_ref, lt_ref, rd_ref, dq_ref, dk_ref, dv_ref, cols):
        masks = _head_masks()

        def block(qs, ks, first):
            nk = ATT_BLOCK if first else 2 * ATT_BLOCK
            qb = q_ref[0, pl.ds(qs, ATT_BLOCK), cols]
            kc = k_ref[0, pl.ds(ks, nk), cols]
            vc = v_ref[0, pl.ds(ks, nk), cols]
            da = da_ref[0, pl.ds(qs, ATT_BLOCK), cols]
            lt = lt_ref[0, pl.ds(qs, ATT_BLOCK), cols]
            rd = rd_ref[0, pl.ds(qs, ATT_BLOCK), cols]
            q4 = _stack_heads(qb, masks)
            da4 = _stack_heads(da, masks).astype(MXU_DTYPE)
            lt4 = jnp.max(_stack_heads(lt, masks, -jnp.inf), axis=-1, keepdims=True)
            rd4 = jnp.max(_stack_heads(rd, masks, -jnp.inf), axis=-1, keepdims=True)
            valid = jnp.tile(_band_mask(first), (HEADS_PER_GROUP, 1))
            s = lax.dot_general(q4, kc, _NT, preferred_element_type=F32) * (HEAD_DIM ** -0.5)
            s = jnp.where(valid, s, NEG_INF)
            p = jnp.exp(s - lt4)
            dp = lax.dot_general(da4, vc, _NT, preferred_element_type=F32)
            ds = (p * (dp - rd4) * (HEAD_DIM ** -0.5)).astype(MXU_DTYPE)
            dq_ref[0, pl.ds(qs, ATT_BLOCK), cols] = _unstack_heads(jnp.dot(ds, kc, preferred_element_type=F32), masks)
            dk_ref[0, pl.ds(ks, nk), cols] += lax.dot_general(ds, q4, _TN, preferred_element_type=F32)
            dv_ref[0, pl.ds(ks, nk), cols] += lax.dot_general(p.astype(MXU_DTYPE), da4, _TN, preferred_element_type=F32)

        block(0, 0, True)
        if nb > 1:
            def loop(n, carry):
                block(pl.multiple_of(n * ATT_BLOCK, ATT_BLOCK), pl.multiple_of((n - 1) * ATT_BLOCK, ATT_BLOCK), False)
                return carry

            lax.fori_loop(1, nb, loop, 0)

    per_sample = lambda a: a.reshape(n_samples, length, d * GROUP_W)
    spec = pl.BlockSpec((1, length, rps * GROUP_W), lambda b, r: (b, 0, r))
    shp = jax.ShapeDtypeStruct((n_samples, length, d * GROUP_W), F32)
    dq, dk, dv, *carried = _grid_call(
        body, f"attn_bwd_g{group}", (n_samples, d // rps), [per_sample(a) for a in (q, k, v, dattn, lse_tot, rowdot)],
        [spec] * 6, [spec] * 3, [shp, shp, shp], VMEM_MID, comm)
    flat = lambda a: a.reshape(n_samples * length, d * GROUP_W)
    return flat(dq), flat(dk), flat(dv), carried


def _disc(lr, li, ldt, br, bi):
    dt = jnp.exp(ldt)
    mag = jnp.exp(lr * dt)
    ab_re, ab_im = mag * jnp.cos(li * dt), mag * jnp.sin(li * dt)
    den = lr * lr + li * li
    nr, ni = ab_re - 1.0, ab_im
    f_re = (nr * lr + ni * li) / den
    f_im = (ni * lr - nr * li) / den
    return ab_re, ab_im, f_re * br - f_im * bi, f_re * bi + f_im * br


def _state_mask():
    row_g = lax.broadcasted_iota(jnp.int32, (SCAN_CH, SCAN_WC), 0) // SSM_CH
    col_g = lax.broadcasted_iota(jnp.int32, (SCAN_CH, SCAN_WC), 1) // SSM_STATE
    return row_g == col_g


def _ssm_disc(lr, li, ldt, br, bi, cr, ci):
    w = SCAN_WC

    def body(lr_ref, li_ref, ldt_ref, br_ref, bi_ref, cr_ref, ci_ref, a_ref, bb_ref, c_ref):
        ar, ai, bbr, bbi = _disc(lr_ref[...], li_ref[...], ldt_ref[...], br_ref[...], bi_ref[...])
        crv, civ = cr_ref[...], ci_ref[...]
        mask = _state_mask()
        for cb in range(SCAN_NBLK):
            sl = slice(cb * w, (cb + 1) * w)
            rows = slice(cb * SCAN_CH, (cb + 1) * SCAN_CH)
            dense = lambda comp: jnp.where(mask, jnp.tile(comp[:, sl], (SCAN_CH // SSM_CH, 1)), 0.0)
            a_ref[:, 2 * cb * w:(2 * cb + 1) * w] = ar[:, sl]
            a_ref[:, (2 * cb + 1) * w:(2 * cb + 2) * w] = ai[:, sl]
            bb_ref[rows, :w] = dense(bbr).astype(MXU_DTYPE)
            bb_ref[rows, w:] = dense(bbi).astype(MXU_DTYPE)
            c_ref[rows, :w] = dense(crv).astype(MXU_DTYPE)
            c_ref[rows, w:] = (-dense(civ)).astype(MXU_DTYPE)

    return _pallas_call(
        body, name="ssm_disc",
        out_shape=[jax.ShapeDtypeStruct((1, 2 * N_STATE), F32), jax.ShapeDtypeStruct((SSM_W, 2 * w), MXU_DTYPE),
                   jax.ShapeDtypeStruct((SSM_W, 2 * w), MXU_DTYPE)],
        compiler_params=pltpu.CompilerParams(vmem_limit_bytes=VMEM_MID),
    )(lr, li, ldt, br, bi, cr, ci)


def _group_indicator():
    s = jnp.arange(N_STATE) // SSM_STATE
    return (s[:, None] == jnp.arange(LANES)[None, :]).astype(F32)


def _ssm_param_bwd(lr, li, ldt, br, bi, da_cat, dbb_full, dc_full):
    w = SCAN_WC

    def body(lr_ref, li_ref, ldt_ref, br_ref, bi_ref, da_ref, dbb_ref, dc_ref, ind_ref,
             glr_ref, gli_ref, gldt_ref, gbr_ref, gbi_ref, gcr_ref, gci_ref):
        mask = _state_mask()

        def diag_parts(ref):
            res = ([], [])
            for cb in range(SCAN_NBLK):
                for part in range(2):
                    blk = ref[cb * SCAN_CH:(cb + 1) * SCAN_CH, part * w:(part + 1) * w]
                    res[part].append(jnp.sum(jnp.where(mask, blk, 0.0).reshape(SCAN_CH // SSM_CH, SSM_CH, w), axis=0))
            return jnp.concatenate(res[0], axis=1), jnp.concatenate(res[1], axis=1)

        dar = jnp.concatenate([da_ref[:, 2 * cb * w:(2 * cb + 1) * w] for cb in range(SCAN_NBLK)], axis=1)
        dai = jnp.concatenate([da_ref[:, (2 * cb + 1) * w:(2 * cb + 2) * w] for cb in range(SCAN_NBLK)], axis=1)
        dbbr, dbbi = diag_parts(dbb_ref)
        dcr, dci_neg = diag_parts(dc_ref)
        gcr_ref[...] = dcr
        gci_ref[...] = -dci_neg
        _, vjp = jax.vjp(_disc, lr_ref[...], li_ref[...], ldt_ref[...], br_ref[...], bi_ref[...])
        glr, gli, gldt, gbr, gbi = vjp((dar, dai, dbbr, dbbi))
        glr_ref[...] = glr
        gli_ref[...] = gli
        gldt_ref[...] = jnp.dot(jnp.broadcast_to(gldt, (8, N_STATE)), ind_ref[...], preferred_element_type=F32,
                                precision=lax.Precision.HIGHEST)
        gbr_ref[...] = gbr
        gbi_ref[...] = gbi

    v1 = jax.ShapeDtypeStruct((1, N_STATE), F32)
    v16 = jax.ShapeDtypeStruct((SSM_CH, N_STATE), F32)
    vdt = jax.ShapeDtypeStruct((8, LANES), F32)
    return _pallas_call(
        body, name="ssm_param_bwd", out_shape=[v1, v1, vdt, v16, v16, v16, v16],
        compiler_params=pltpu.CompilerParams(vmem_limit_bytes=VMEM_BIG),
    )(lr, li, ldt, br, bi, da_cat, dbb_full, dc_full, _group_indicator())


def _cmul(ar, ai, br, bi):
    return ar * br - ai * bi, ar * bi + ai * br


def _gelu_tanh(y):
    return jnp.tanh(_GELU_C * (y + 0.044715 * (y * y * y)))


def _segment_carry(er, ei, ar, ai, n_rows, reverse):
    qr, qi = ar, ai
    for _ in range(int(math.log2(SCAN_LEN))):
        qr, qi = _cmul(qr, qi, qr, qi)
    seg = lax.broadcasted_iota(jnp.int32, er.shape, 0) % SCAN_SEG_PER_SAMPLE
    shift = 1
    while shift < SCAN_SEG_PER_SAMPLE:
        keep = (seg < SCAN_SEG_PER_SAMPLE - shift) if reverse else (seg >= shift)
        amount = n_rows - shift if reverse else shift
        sr = jnp.where(keep, pltpu.roll(er, amount, 0), 0.0)
        si = jnp.where(keep, pltpu.roll(ei, amount, 0), 0.0)
        if reverse:
            er, ei = er + qr * sr + qi * si, ei + qr * si - qi * sr
        else:
            er, ei = er + qr * sr - qi * si, ei + qr * si + qi * sr
        qr, qi = _cmul(qr, qi, qr, qi)
        shift *= 2
    keep = (seg < SCAN_SEG_PER_SAMPLE - 1) if reverse else (seg >= 1)
    amount = n_rows - 1 if reverse else 1
    return jnp.where(keep, pltpu.roll(er, amount, 0), 0.0), jnp.where(keep, pltpu.roll(ei, amount, 0), 0.0)


def _ssm_fwd(u_perm, a_cat, bbc, cc, dskip, n_rows):
    t = u_perm.shape[0]
    w = SCAN_WC
    rows_c = SCAN_CHUNK * n_rows
    n_chunks = t // rows_c

    assert n_chunks % 2 == 0

    def body(u_ref, a_ref, bb_ref, c_ref, d_ref, yt_ref, yg_ref, ein_ref, bu_all, st_a, st_b, xs_a, xs_b):
        ar = jnp.broadcast_to(a_ref[:, :w], (n_rows, w))
        ai = jnp.broadcast_to(a_ref[:, w:], (n_rows, w))
        start = lambda ch: pl.multiple_of(ch * rows_c, rows_c)

        def project(ch, stage):
            res = jnp.dot(u_ref[pl.ds(start(ch), rows_c), :].astype(MXU_DTYPE), bb_ref[...], preferred_element_type=F32)
            stage[...] = res
            bu_all[pl.ds(start(ch), rows_c), :] = res

        def steps(src, r0, carry, xs=None):
            for i in range(SCAN_CHUNK):
                blk = src[pl.ds(r0 + i * n_rows, n_rows), :]
                carry = (ar * carry[0] - ai * carry[1] + blk[:, :w], ar * carry[1] + ai * carry[0] + blk[:, w:])
                if xs is not None:
                    xs[i * n_rows:(i + 1) * n_rows, :w] = carry[0]
                    xs[i * n_rows:(i + 1) * n_rows, w:] = carry[1]
            return carry

        def emit(xs, ch):
            y = lax.dot_general(xs[...].astype(MXU_DTYPE), c_ref[...], _NT, preferred_element_type=F32)
            yt = y + d_ref[...] * u_ref[pl.ds(start(ch), rows_c), :]
            yt_ref[pl.ds(start(ch), rows_c), :] = yt
            yg_ref[pl.ds(start(ch), rows_c), :] = (0.5 * yt * (1.0 + _gelu_tanh(yt))).astype(BF16)

        project(0, st_a)

        def pair1(p, carry):
            project(2 * p + 1, st_b)
            carry = steps(st_a, 0, carry)
            project(jnp.minimum(2 * p + 2, n_chunks - 1), st_a)
            return steps(st_b, 0, carry)

        zero = jnp.zeros((n_rows, w), F32)
        er, ei = lax.fori_loop(0, n_chunks // 2, pair1, (zero, zero))
        cr, ci = _segment_carry(er, ei, ar, ai, n_rows, False)
        ein_ref[:, :w] = cr
        ein_ref[:, w:] = ci

        xs_b[...] = jnp.zeros_like(xs_b)

        def pair2(p, carry):
            emit(xs_b, jnp.maximum(2 * p - 1, 0))
            carry = steps(bu_all, start(2 * p), carry, xs_a)
            emit(xs_a, 2 * p)
            return steps(bu_all, start(2 * p + 1), carry, xs_b)

        lax.fori_loop(0, n_chunks // 2, pair2, (cr, ci))
        emit(xs_b, n_chunks - 1)

    col = lambda width: pl.BlockSpec((t, width), lambda c: (0, c))
    wgt = pl.BlockSpec((SCAN_CH, 2 * w), lambda c: (c, 0))
    return _pallas_call(
        body, name="ssm_fwd", grid=(SCAN_NBLK,),
        in_specs=[col(SCAN_CH), pl.BlockSpec((1, 2 * w), lambda c: (0, c)), wgt, wgt,
                  pl.BlockSpec((1, SCAN_CH), lambda c: (0, c))],
        out_specs=[col(SCAN_CH), col(SCAN_CH), pl.BlockSpec((n_rows, 2 * w), lambda c: (0, c))],
        out_shape=[jax.ShapeDtypeStruct((t, SSM_W), F32), jax.ShapeDtypeStruct((t, SSM_W), BF16),
                   jax.ShapeDtypeStruct((n_rows, 2 * N_STATE), F32)],
        scratch_shapes=[pltpu.VMEM((t, 2 * w), F32)] + [pltpu.VMEM((rows_c, 2 * w), F32)] * 4,
        compiler_params=pltpu.CompilerParams(dimension_semantics=("parallel",), vmem_limit_bytes=VMEM_BIG),
    )(u_perm, a_cat, bbc, cc, dskip)


def _ssm_bwd(u_perm, dyg, ytot, dskip, a_cat, bbc, cc, ein, n_rows, comm=None):
    t = u_perm.shape[0]
    w = SCAN_WC
    rows_c = SCAN_CHUNK * n_rows
    n_chunks = t // rows_c

    assert n_chunks % 2 == 0
    last = n_chunks - 1

    def body(u_ref, dyg_ref, yt_ref, dk_ref, a_ref, bb_ref, c_ref, ein_ref, du_ref, gd_ref, da_ref, dbb_ref, dc_ref,
             xs_all, dy_s, st_a, st_b, buf_a, buf_b):
        ar = jnp.broadcast_to(a_ref[:, :w], (n_rows, w))
        ai = jnp.broadcast_to(a_ref[:, w:], (n_rows, w))
        zero = jnp.zeros((n_rows, w), F32)
        start = lambda ch: pl.multiple_of(ch * rows_c, rows_c)
        dbb_ref[...] = jnp.zeros_like(dbb_ref)
        dc_ref[...] = jnp.zeros_like(dc_ref)
        da_ref[...] = jnp.zeros_like(da_ref)

        yt = yt_ref[...]
        th = _gelu_tanh(yt)
        dgelu = 0.5 * (1.0 + th) + 0.5 * yt * (1.0 - th * th) * _GELU_C * (1.0 + 3.0 * 0.044715 * yt * yt)
        dy_all = dyg_ref[...] * dgelu
        dy_s[...] = dy_all
        gd_ref[...] = jnp.sum(dy_all * u_ref[...], axis=0, keepdims=True)
        dy_chunk = lambda ch: dy_s[pl.ds(start(ch), rows_c), :].astype(MXU_DTYPE)

        xs_all[0:n_rows, :] = ein_ref[...]

        def project(ch, stage):
            stage[...] = jnp.dot(u_ref[pl.ds(start(ch), rows_c), :].astype(MXU_DTYPE), bb_ref[...],
                                 preferred_element_type=F32)

        def fwd_steps(stage, ch, carry, xs):
            for i in range(SCAN_CHUNK):
                blk = stage[i * n_rows:(i + 1) * n_rows, :]
                carry = (ar * carry[0] - ai * carry[1] + blk[:, :w], ar * carry[1] + ai * carry[0] + blk[:, w:])
                for half, val in enumerate(carry):
                    xs[i * n_rows:(i + 1) * n_rows, half * w:(half + 1) * w] = val
                    xs_all[pl.ds(start(ch) + (i + 1) * n_rows, n_rows), half * w:(half + 1) * w] = val
            return carry

        def add_dc(xs, ch):
            dc_ref[...] += lax.dot_general(dy_chunk(ch), xs[...].astype(MXU_DTYPE), _TN, preferred_element_type=F32)

        project(0, st_a)

        def fwd_pair(p, carry):
            project(2 * p + 1, st_b)
            carry = fwd_steps(st_a, 2 * p, carry, buf_a)
            add_dc(buf_a, 2 * p)
            project(jnp.minimum(2 * p + 2, last), st_a)
            carry = fwd_steps(st_b, 2 * p + 1, carry, buf_b)
            add_dc(buf_b, 2 * p + 1)
            return carry

        lax.fori_loop(0, n_chunks // 2, fwd_pair, (ein_ref[:, :w], ein_ref[:, w:]))

        def project_dx(ch, stage):
            stage[...] = jnp.dot(dy_chunk(ch), c_ref[...], preferred_element_type=F32)

        def back_steps(stage, carry, g_buf=None):
            for i in reversed(range(SCAN_CHUNK)):
                blk = stage[i * n_rows:(i + 1) * n_rows, :]
                carry = (blk[:, :w] + ar * carry[0] + ai * carry[1], blk[:, w:] + ar * carry[1] - ai * carry[0])
                if g_buf is not None:
                    g_buf[i * n_rows:(i + 1) * n_rows, :w] = carry[0]
                    g_buf[i * n_rows:(i + 1) * n_rows, w:] = carry[1]
            return carry

        def first_pair(p, carry):
            project_dx(last - 2 * p - 1, st_b)
            carry = back_steps(st_a, carry)
            project_dx(jnp.maximum(last - 2 * p - 2, 0), st_a)
            return back_steps(st_b, carry)

        project_dx(last, st_a)
        sr, si = lax.fori_loop(0, n_chunks // 2, first_pair, (zero, zero))
        gr0, gi0 = _segment_carry(sr, si, ar, ai, n_rows, True)

        def post(g_buf, ch):
            g = g_buf[...]
            xp = xs_all[pl.ds(start(ch), rows_c), :]
            da_ref[:, :w] += jnp.sum(g[:, :w] * xp[:, :w] + g[:, w:] * xp[:, w:], axis=0, keepdims=True)
            da_ref[:, w:] += jnp.sum(g[:, w:] * xp[:, :w] - g[:, :w] * xp[:, w:], axis=0, keepdims=True)
            gb = g.astype(MXU_DTYPE)
            du_ref[pl.ds(start(ch), rows_c), :] = (lax.dot_general(gb, bb_ref[...], _NT, preferred_element_type=F32)
                                                   + dy_s[pl.ds(start(ch), rows_c), :] * dk_ref[...])
            dbb_ref[...] += lax.dot_general(u_ref[pl.ds(start(ch), rows_c), :].astype(MXU_DTYPE), gb, _TN,
                                            preferred_element_type=F32)

        def second_pair(p, carry):
            c1 = last - 2 * p
            project_dx(c1 - 1, st_b)
            post(buf_b, jnp.minimum(c1 + 1, last))
            carry = back_steps(st_a, carry, buf_a)
            project_dx(jnp.maximum(c1 - 2, 0), st_a)
            post(buf_a, c1)
            return back_steps(st_b, carry, buf_b)

        project_dx(last, st_a)
        buf_b[...] = jnp.zeros_like(buf_b)
        lax.fori_loop(0, n_chunks // 2, second_pair, (gr0, gi0))
        post(buf_b, 0)

    col = lambda width: pl.BlockSpec((t, width), lambda c, j: (0, c))
    wgt = pl.BlockSpec((SCAN_CH, 2 * w), lambda c, j: (c, 0))
    row = pl.BlockSpec((1, 2 * w), lambda c, j: (0, c))
    chan = pl.BlockSpec((1, SCAN_CH), lambda c, j: (0, c))
    return _grid_call(
        body, "ssm_bwd", (SCAN_NBLK, 1), [u_perm, dyg, ytot, dskip, a_cat, bbc, cc, ein],
        [col(SCAN_CH), col(SCAN_CH), col(SCAN_CH), chan, row, wgt, wgt,
         pl.BlockSpec((n_rows, 2 * w), lambda c, j: (0, c))],
        [col(SCAN_CH), chan, row, wgt, wgt],
        [jax.ShapeDtypeStruct((t, SSM_W), F32), jax.ShapeDtypeStruct((1, SSM_W), F32),
         jax.ShapeDtypeStruct((1, 2 * N_STATE), F32), jax.ShapeDtypeStruct((SSM_W, 2 * w), F32),
         jax.ShapeDtypeStruct((SSM_W, 2 * w), F32)],
        56 * 1024 * 1024, comm,
        scratch=[pltpu.VMEM((t + n_rows, 2 * w), F32), pltpu.VMEM((t, SCAN_CH), F32)]
        + [pltpu.VMEM((rows_c, 2 * w), F32)] * 4)


def _to_scan_rows(a, n_samples):
    c = a.shape[1]
    return a.reshape(n_samples, SCAN_SEG_PER_SAMPLE, SCAN_LEN, c).transpose(2, 0, 1, 3).reshape(-1, c)


def _from_scan_rows(a, n_samples):
    c = a.shape[1]
    return a.reshape(SCAN_LEN, n_samples, SCAN_SEG_PER_SAMPLE, c).transpose(1, 2, 0, 3).reshape(-1, c)


def _row_spec(tm, width):
    return pl.BlockSpec((tm, width), lambda i, j: (i, 0))


def _whole(arr):
    return pl.BlockSpec(arr.shape, lambda i, j: (0,) * arr.ndim)


def _proj_rope(x, g, w_in_t, tabs, comm=None):
    t = x.shape[0]
    tm = 256

    def body(x_ref, g_ref, w_ref, tc_ref, tlo_ref, thi_ref, h_ref, u_ref, gate_ref, *rest):
        qkv_refs, stage = rest[:9], rest[9]
        xv = x_ref[...]
        r = lax.rsqrt(jnp.mean(xv * xv, axis=-1, keepdims=True) + RMS_EPS)
        h = ((xv * r) * g_ref[...]).astype(BF16)
        h_ref[...] = h
        p = lax.dot_general(h.astype(MXU_DTYPE), w_ref[...], _NT, preferred_element_type=F32)
        u_ref[...] = p[:, QKV_W:QKV_W + SSM_W]
        gate_ref[...] = _sigmoid(p[:, QKV_W + SSM_W:])
        tc, tlo, thi = tc_ref[...], tlo_ref[...], thi_ref[...]
        n_ch = QKV_W // LANES
        for ch in range(n_ch):
            piece = p[:, _lane_chunk(ch)]
            stage[ch] = _rope_apply(piece, tc, tlo, thi) if ch < 2 * n_ch // 3 else piece
        halves = GROUP_W // LANES
        for grp, d in enumerate(DILATIONS):
            for which in range(3):
                out = qkv_refs[3 * grp + which]
                for res in range(d):
                    for half in range(halves):
                        ch = which * (n_ch // 3) + grp * halves + half
                        out[:, _lane_chunk(res * halves + half)] = _gather_residue(stage, ch, res, d, tm // d).astype(BF16)

    tab = pl.BlockSpec((tm, LANES), lambda i, j: (i % (SEQ // tm), 0))
    widths = [(D_MODEL, BF16), (SSM_W, F32), (2 * D_MODEL, F32)]
    out_specs = [_row_spec(tm, wd) for wd, _ in widths]
    out_shapes = [jax.ShapeDtypeStruct((t, wd), dt) for wd, dt in widths]
    for d in DILATIONS:
        out_specs += [_row_spec(tm // d, d * GROUP_W)] * 3
        out_shapes += [jax.ShapeDtypeStruct((t // d, d * GROUP_W), BF16)] * 3
    return _grid_call(
        body, "proj_rope", (t // tm, 1), [x, g, w_in_t, *tabs],
        [_row_spec(tm, D_MODEL), _whole(g), _whole(w_in_t), tab, tab, tab], out_specs, out_shapes, VMEM_BIG, comm,
        scratch=[pltpu.VMEM((QKV_W // LANES, tm, LANES), F32)])


def _branch_outputs(attn_ref, yg_ref, wao_ref, wglu_ref):
    attn_d = lax.dot_general(attn_ref[...].astype(MXU_DTYPE), wao_ref[...], _NT, preferred_element_type=F32)
    z = lax.dot_general(yg_ref[...].astype(MXU_DTYPE), wglu_ref[...], _NT, preferred_element_type=F32)
    return attn_d, z[:, :D_MODEL], _sigmoid(z[:, D_MODEL:])


def _mix_out_rms(os_, lses, yg, gates, x, w_ao_t, w_glu_t, w_out, g, comm=None):
    t = x.shape[0]
    tm = 256

    def body(o0, o1, o2, l0, l1, l2, yg_ref, gate_ref, x_ref, wao_ref, wglu_ref, wout_ref, g_ref,
             attn_ref, lt_ref, m_ref, x1_ref, h_ref, nat):
        _merge_groups((o0, o1, o2), (l0, l1, l2), attn_ref, lt_ref, nat, tm)
        attn_d, za, sb = _branch_outputs(attn_ref, yg_ref, wao_ref, wglu_ref)
        merged = (gate_ref[:, :D_MODEL] * attn_d + gate_ref[:, D_MODEL:] * (za * sb)).astype(BF16)
        m_ref[...] = merged
        x1 = x_ref[...] + jnp.dot(merged.astype(MXU_DTYPE), wout_ref[...], preferred_element_type=F32)
        x1_ref[...] = x1
        r = lax.rsqrt(jnp.mean(x1 * x1, axis=-1, keepdims=True) + RMS_EPS)
        h_ref[...] = ((x1 * r) * g_ref[...]).astype(BF16)

    dil_specs = [_row_spec(tm // d, d * GROUP_W) for d in DILATIONS] * 2
    return _grid_call(
        body, "mix_out_rms", (t // tm, 1), [*os_, *lses, yg, gates, x, w_ao_t, w_glu_t, w_out, g],
        dil_specs + [_row_spec(tm, SSM_W), _row_spec(tm, 2 * D_MODEL), _row_spec(tm, D_MODEL),
                     _whole(w_ao_t), _whole(w_glu_t), _whole(w_out), _whole(g)],
        [_row_spec(tm, GROUP_W)] * 2 + [_row_spec(tm, D_MODEL)] * 3,
        [jax.ShapeDtypeStruct((t, GROUP_W), F32)] * 2
        + [jax.ShapeDtypeStruct((t, D_MODEL), BF16), jax.ShapeDtypeStruct((t, D_MODEL), F32),
           jax.ShapeDtypeStruct((t, D_MODEL), BF16)], VMEM_BIG, comm, scratch=[pltpu.VMEM((8, tm, LANES), F32)])


def _mix_bwd(dx1b, attn, lse_tot, yg, gates, w_ao_t, w_glu_t, w_out, comm=None):
    t = dx1b.shape[0]
    tm = 256

    def body(dx_ref, attn_ref, lt_ref, yg_ref, gate_ref, wao_ref, wglu_ref, wout_ref, ones_ref,
             dad_ref, dz_ref, dg_ref, da_ref, dyg_ref, rd_ref, *rest):
        dm = lax.dot_general(dx_ref[...], wout_ref[...], _NT, preferred_element_type=F32)
        attn_d, za, sb = _branch_outputs(attn_ref, yg_ref, wao_ref, wglu_ref)
        g0, g1 = gate_ref[:, :D_MODEL], gate_ref[:, D_MODEL:]
        dad = (dm * g0).astype(BF16)
        dad_ref[...] = dad
        ds = dm * g1
        dza, dzb = (ds * sb).astype(BF16), (ds * za * sb * (1.0 - sb)).astype(BF16)
        dz_ref[:, :D_MODEL] = dza
        dz_ref[:, D_MODEL:] = dzb
        dg_ref[:, :D_MODEL] = (dm * attn_d * g0 * (1.0 - g0)).astype(BF16)
        dg_ref[:, D_MODEL:] = (dm * (za * sb) * g1 * (1.0 - g1)).astype(BF16)
        da = jnp.dot(dad.astype(MXU_DTYPE), wao_ref[...], preferred_element_type=F32)
        da_ref[...] = da
        dyg_ref[...] = (jnp.dot(dza.astype(MXU_DTYPE), wglu_ref[:D_MODEL, :], preferred_element_type=F32)
                        + jnp.dot(dzb.astype(MXU_DTYPE), wglu_ref[D_MODEL:, :], preferred_element_type=F32))
        _attention_cotangents(da, attn_ref[...], lt_ref[...], ones_ref[...], rd_ref, rest[:6], rest[6], tm)

    widths = [(D_MODEL, BF16), (2 * D_MODEL, BF16), (2 * D_MODEL, BF16), (GROUP_W, F32), (SSM_W, F32), (GROUP_W, F32)]
    out_specs = [_row_spec(tm, wd) for wd, _ in widths]
    out_shapes = [jax.ShapeDtypeStruct((t, wd), dt) for wd, dt in widths]
    for d in DILATIONS[1:]:
        out_specs += [_row_spec(tm // d, d * GROUP_W)] * 3
        out_shapes += [jax.ShapeDtypeStruct((t // d, d * GROUP_W), F32)] * 3
    ones = _head_sum_matrix()
    return _grid_call(
        body, "mix_bwd", (t // tm, 1), [dx1b, attn, lse_tot, yg, gates, w_ao_t, w_glu_t, w_out, ones],
        [_row_spec(tm, D_MODEL), _row_spec(tm, GROUP_W), _row_spec(tm, GROUP_W), _row_spec(tm, SSM_W),
         _row_spec(tm, 2 * D_MODEL), _whole(w_ao_t), _whole(w_glu_t), _whole(w_out), _whole(ones)],
        out_specs, out_shapes, VMEM_BIG, comm, scratch=[pltpu.VMEM((6, tm, LANES), F32)])


FFN_TN = D_FF // 2
MXU_COLS = 256


def _ffn_in_swiglu(h2, w_gate_t, w_up_t, comm=None):
    t = h2.shape[0]
    tm = 512

    def body(h_ref, wg_ref, wu_ref, a_ref, b_ref, f_ref):
        h = h_ref[...].astype(MXU_DTYPE)
        for c0 in range(0, FFN_TN, MXU_COLS):
            sl = slice(c0, min(c0 + MXU_COLS, FFN_TN))
            a = lax.dot_general(h, wg_ref[sl, :], _NT, preferred_element_type=F32)
            b = lax.dot_general(h, wu_ref[sl, :], _NT, preferred_element_type=F32)
            a_ref[:, sl] = a
            b_ref[:, sl] = b
            f_ref[:, sl] = (a * _sigmoid(a) * b).astype(BF16)

    tile = pl.BlockSpec((tm, FFN_TN), lambda j, i: (i, j))
    wspec = pl.BlockSpec((FFN_TN, D_MODEL), lambda j, i: (j, 0))
    return _grid_call(
        body, "ffn_in_swiglu", (D_FF // FFN_TN, t // tm), [h2, w_gate_t, w_up_t],
        [pl.BlockSpec((tm, D_MODEL), lambda j, i: (i, 0)), wspec, wspec],
        [tile] * 3, [jax.ShapeDtypeStruct((t, D_FF), F32)] * 2 + [jax.ShapeDtypeStruct((t, D_FF), BF16)], VMEM_BIG, comm)


def _ffn_down_final(f, w_down, x1, target, g):
    t = x1.shape[0]
    tm = 256

    def body(f_ref, w_ref, x1_ref, t_ref, g_ref, dx_ref, dxb_ref, loss_ref, gg_ref):
        @pl.when(pl.program_id(0) == 0)
        def _():
            loss_ref[...] = jnp.zeros_like(loss_ref)
            gg_ref[...] = jnp.zeros_like(gg_ref)

        xv = x1_ref[...] + jnp.dot(f_ref[...].astype(MXU_DTYPE), w_ref[...], preferred_element_type=F32)
        gv = g_ref[...]
        r = lax.rsqrt(jnp.mean(xv * xv, axis=-1, keepdims=True) + RMS_EPS)
        n = xv * r
        diff = n * gv - t_ref[...]
        per_tok = jnp.mean(diff * diff, axis=-1, keepdims=True)
        loss_ref[...] += 0.5 * jnp.sum(per_tok, axis=0, keepdims=True)
        dy = diff / xv.shape[-1]
        gg_ref[...] += jnp.sum(dy * n, axis=0, keepdims=True)
        dn = dy * gv
        dx = r * (dn - n * jnp.mean(dn * n, axis=-1, keepdims=True))
        dx_ref[...] = dx
        dxb_ref[...] = dx.astype(BF16)

    acc = lambda shp: pl.BlockSpec(shp, lambda i, j: (0, 0))
    return _grid_call(
        body, "ffn_down_final", (t // tm, 1), [f, w_down, x1, target, g],
        [_row_spec(tm, D_FF), _whole(w_down), _row_spec(tm, D_MODEL), _row_spec(tm, D_MODEL), _whole(g)],
        [_row_spec(tm, D_MODEL)] * 2 + [acc((8, LANES)), acc((1, D_MODEL))],
        [jax.ShapeDtypeStruct((t, D_MODEL), F32), jax.ShapeDtypeStruct((t, D_MODEL), BF16),
         jax.ShapeDtypeStruct((8, LANES), F32), jax.ShapeDtypeStruct((1, D_MODEL), F32)], VMEM_BIG, sequential=True)


def _d_f_swiglu_bwd(dx2b, w_down, a, b):
    t = a.shape[0]
    tm = 512

    def body(dx_ref, w_ref, a_ref, b_ref, da_ref, db_ref):
        d = lax.dot_general(dx_ref[...], w_ref[...], _NT, preferred_element_type=F32)
        av, bv = a_ref[...], b_ref[...]
        sg = _sigmoid(av)
        da_ref[...] = (d * bv * sg * (1.0 + av * (1.0 - sg))).astype(BF16)
        db_ref[...] = (d * av * sg).astype(BF16)

    tile = pl.BlockSpec((tm, FFN_TN), lambda j, i: (i, j))
    return _grid_call(
        body, "d_f_swiglu_bwd", (D_FF // FFN_TN, t // tm), [dx2b, w_down, a, b],
        [pl.BlockSpec((tm, D_MODEL), lambda j, i: (i, 0)), pl.BlockSpec((FFN_TN, D_MODEL), lambda j, i: (j, 0)), tile, tile],
        [tile] * 2, [jax.ShapeDtypeStruct((t, D_FF), BF16)] * 2, VMEM_BIG)


def _mm_rms_bwd(operands, weights, x, g, dres, name, comm=None):
    t = x.shape[0]
    tm = 256
    n_op = len(operands)

    def body(*refs):
        a_refs, w_refs = refs[:n_op], refs[n_op:2 * n_op]
        x_ref, g_ref, dres_ref, dx_ref, dxb_ref, gg_ref = refs[2 * n_op:]

        @pl.when(pl.program_id(0) == 0)
        def _():
            gg_ref[...] = jnp.zeros_like(gg_ref)

        dh = None
        for a_ref, w_ref in zip(a_refs, w_refs):
            part = jnp.dot(a_ref[...].astype(MXU_DTYPE), w_ref[...], preferred_element_type=F32)
            dh = part if dh is None else dh + part
        xv = x_ref[...]
        r = lax.rsqrt(jnp.mean(xv * xv, axis=-1, keepdims=True) + RMS_EPS)
        n = xv * r
        gg_ref[...] += jnp.sum(dh * n, axis=0, keepdims=True)
        dn = dh * g_ref[...]
        dx = dres_ref[...] + r * (dn - n * jnp.mean(dn * n, axis=-1, keepdims=True))
        dx_ref[...] = dx
        dxb_ref[...] = dx.astype(BF16)

    d = x.shape[1]
    return _grid_call(
        body, name, (t // tm, 1), [*operands, *weights, x, g, dres],
        [_row_spec(tm, a.shape[1]) for a in operands] + [_whole(wk) for wk in weights]
        + [_row_spec(tm, d), _whole(g), _row_spec(tm, d)],
        [_row_spec(tm, d)] * 2 + [pl.BlockSpec((1, d), lambda i, j: (0, 0))],
        [jax.ShapeDtypeStruct((t, d), F32), jax.ShapeDtypeStruct((t, d), BF16), jax.ShapeDtypeStruct((1, d), F32)],
        VMEM_BIG, comm, sequential=True)


def _flat_small(small):
    perm_b = lambda a: a.reshape(SSM_GROUPS, SSM_STATE, SSM_CH).transpose(2, 0, 1).reshape(SSM_CH, N_STATE)
    perm_c = lambda a: a.reshape(SSM_GROUPS, SSM_CH, SSM_STATE).transpose(1, 0, 2).reshape(SSM_CH, N_STATE)
    return dict(
        g_mix=small["norm_mix_g"].reshape(1, D_MODEL), g_ffn=small["norm_ffn_g"].reshape(1, D_MODEL),
        g_fin=small["norm_final_g"].reshape(1, D_MODEL),
        lr=small["ssm_a_re"].reshape(1, N_STATE), li=small["ssm_a_im"].reshape(1, N_STATE),
        ldt=jnp.repeat(small["ssm_log_dt"].reshape(SSM_GROUPS), SSM_STATE).reshape(1, N_STATE),
        br=perm_b(small["ssm_b_re"]), bi=perm_b(small["ssm_b_im"]),
        cr=perm_c(small["ssm_c_re"]), ci=perm_c(small["ssm_c_im"]), dskip=small["ssm_d"].reshape(1, SSM_W))


AG_HOSTS = {"proj_rope": ("w_glu", "w_attn_out", "w_out", "w_ffn_gate"), "mix_out_rms": ("w_ffn_up",),
            "ffn_in_swiglu": ("w_ffn_down",)}
HALVED = ("w_ffn_gate", "w_ffn_up", "w_in")
A2A_HOSTS = {"d_h2_rms": ("w_ffn_down",), "mix_bwd": ("w_ffn_gate:0", "w_out"), "attn_bwd_g1": ("w_glu",),
             "attn_bwd_g2": ("w_attn_out",), "ssm_bwd": ("w_ffn_gate:1", "w_ffn_up:0", "w_ffn_up:1"),
             "mm_g_in1": ("w_in:0",), "d_h0_rms": ("w_in:1",)}
SMALL_HOST = "mm_g_in0"


def _local_step(x, target, w, small, shards=None):
    t = x.shape[0]
    n_samples = t // SEQ
    n_rows = n_samples * SCAN_SEG_PER_SAMPLE
    tabs = _rope_tables()
    w = dict(w)
    fs = _flat_small(small)
    g_mix, g_ffn, g_fin, dskip = fs["g_mix"], fs["g_ffn"], fs["g_fin"], fs["dskip"]
    a_cat, bbc, cc = _ssm_disc(fs["lr"], fs["li"], fs["ldt"], fs["br"], fs["bi"], fs["cr"], fs["ci"])
    big, recv, small_pack = {}, {}, []

    def comm_of(name):
        if shards is None:
            return None
        if name == SMALL_HOST:
            return _ag_comm([(small_pack[0], 0, 0)], [(N_DEV, *small_pack[0].shape)])
        if name in AG_HOSTS:
            names = AG_HOSTS[name]
            return _ag_comm([(shards[n], j, 0) for j, n in enumerate(names)], [(N_DEV, *shards[n].shape) for n in names])
        if name in A2A_HOSTS:
            return _a2a_comm([(big[n].reshape(N_DEV, -1, big[n].shape[1]), 0) for n in A2A_HOSTS[name]])
        return None

    def absorb(name, carried):
        if name == SMALL_HOST:
            recv["small"] = carried[0]
        for n, a3 in zip(AG_HOSTS.get(name, ()), carried):
            w[n] = a3.reshape(-1, a3.shape[2])
        for n, a3 in zip(A2A_HOSTS.get(name, ()), carried):
            recv[n] = a3

    def mm(a, b, mode, name, tm, tn, **kw):
        comm = comm_of(name)
        if comm is None:
            return _mm(a, b, mode, name, tm, tn, **kw)
        out, *carried = _mm(a, b, mode, name, tm, tn, comm=comm, **kw)
        absorb(name, carried)
        return out

    h0, u, gates, *rest = _proj_rope(x, g_mix, w["w_in"], tabs, comm_of("proj_rope"))
    qkv = [rest[3 * g:3 * g + 3] for g in range(3)]
    absorb("proj_rope", rest[9:])
    os_, lses = [], []
    for g in range(3):
        o_g, l_g, carried = _attn_fwd(*qkv[g], g, n_samples, comm_of(f"attn_fwd_g{g}"))
        absorb(f"attn_fwd_g{g}", carried)
        os_.append(o_g)
        lses.append(l_g)
    u_perm = _to_scan_rows(u, n_samples)
    ytot, yg_perm, ein = _ssm_fwd(u_perm, a_cat, bbc, cc, dskip, n_rows)
    yg = _from_scan_rows(yg_perm, n_samples)

    attn, lse_tot, merged, x1, h2, *carried = _mix_out_rms(os_, lses, yg, gates, x, w["w_attn_out"], w["w_glu"], w["w_out"],
                                                           g_ffn, comm_of("mix_out_rms"))
    absorb("mix_out_rms", carried)
    ffn_a, ffn_b, f, *carried = _ffn_in_swiglu(h2, w["w_ffn_gate"], w["w_ffn_up"], comm_of("ffn_in_swiglu"))
    absorb("ffn_in_swiglu", carried)
    dx2, dx2b, loss_blk, g_gfin = _ffn_down_final(f, w["w_ffn_down"], x1, target, g_fin)

    da, db = _d_f_swiglu_bwd(dx2b, w["w_ffn_down"], ffn_a, ffn_b)
    big["w_ffn_down"] = mm(f, dx2b, "tn", "mm_g_down", 256, D_MODEL, out_dtype=BF16)
    half = D_MODEL // 2
    for hf in range(2):
        big[f"w_ffn_gate:{hf}"] = mm(da, h2, "tn", f"mm_g_gate{hf}", 256, half, out_dtype=BF16, cols=(hf * half, half))
        big[f"w_ffn_up:{hf}"] = mm(db, h2, "tn", f"mm_g_up{hf}", 256, half, out_dtype=BF16, cols=(hf * half, half))
    dx1, dx1b, g_gffn, *carried = _mm_rms_bwd([da, db], [w["w_ffn_gate"], w["w_ffn_up"]], x1, g_ffn, dx2, "d_h2_rms",
                                              comm_of("d_h2_rms"))
    absorb("d_h2_rms", carried)

    big["w_out"] = mm(merged, dx1b, "tn", "mm_g_out", 256, D_MODEL, out_dtype=BF16)
    dattn_d, dz, dgpre, dattn, dyg, rowdot, *rest = _mix_bwd(dx1b, attn, lse_tot, yg, gates, w["w_attn_out"], w["w_glu"],
                                                             w["w_out"], comm_of("mix_bwd"))
    cot = [(dattn, lse_tot, rowdot), tuple(rest[:3]), tuple(rest[3:6])]
    absorb("mix_bwd", rest[6:])

    big["w_attn_out"] = mm(dattn_d, attn, "tn", "mm_g_attn_out", 512, GROUP_W, out_dtype=BF16)
    big["w_glu"] = mm(dz, yg, "tn", "mm_g_glu", 512, 512, out_dtype=BF16)
    dqs, dks, dvs = [], [], []
    for g in range(3):
        dq_g, dk_g, dv_g, carried = _attn_bwd(*qkv[g], *cot[g], g, n_samples, comm_of(f"attn_bwd_g{g}"))
        absorb(f"attn_bwd_g{g}", carried)
        dqs.append(dq_g)
        dks.append(dk_g)
        dvs.append(dv_g)

    dyg_perm = _to_scan_rows(dyg, n_samples)
    du_perm, g_dskip, da_cat, dbb_full, dc_full, *carried = _ssm_bwd(u_perm, dyg_perm, ytot, dskip, a_cat, bbc, cc, ein,
                                                                   n_rows, comm_of("ssm_bwd"))
    absorb("ssm_bwd", carried)
    du = _from_scan_rows(du_perm, n_samples)
    g_lr, g_li, g_ldt, g_br, g_bi, g_cr, g_ci = _ssm_param_bwd(
        fs["lr"], fs["li"], fs["ldt"], fs["br"], fs["bi"], da_cat, dbb_full, dc_full)

    small_pack.append(_pack_small(dict(lr=g_lr, li=g_li, ldt=g_ldt, br=g_br, bi=g_bi, cr=g_cr, ci=g_ci, dskip=g_dskip,
                                       g_ffn=g_gffn, g_fin=g_gfin, loss=loss_blk)))

    dproj = _pack_dproj(dqs, dks, dvs, du, dgpre, tabs)
    for hf in range(2):
        big[f"w_in:{hf}"] = mm(dproj, h0, "tn", f"mm_g_in{hf}", 256, half, out_dtype=BF16, cols=(hf * half, half))
    grad_x, _, g_gmix, *carried = _mm_rms_bwd([dproj], [w["w_in"]], x, g_mix, dx1, "d_h0_rms", comm_of("d_h0_rms"))
    absorb("d_h0_rms", carried)
    return grad_x, (big if shards is None else recv), small_pack[0], g_gmix


_MESH = pl.DeviceIdType.MESH


def _all_gather(block, name):
    rows, lanes = block.shape

    def body(x_ref, out_ref, send_sems, recv_sems, local_sem):
        x, y, c = lax.axis_index("x"), lax.axis_index("y"), lax.axis_index("c")
        me, sibling = (x, y, c), (x, y, 1 - c)
        chips = [(1 - x, y), (x, 1 - y), (1 - x, 1 - y)]

        def slot(px, py, pc):
            return out_ref.at[4 * px + 2 * py + pc]

        def copy(k, blk, to, src=None):
            return pltpu.make_async_remote_copy(
                src_ref=slot(*blk) if src is None else src, dst_ref=slot(*blk), send_sem=send_sems.at[k],
                recv_sem=recv_sems.at[k], device_id=to, device_id_type=_MESH)

        mine = pltpu.make_async_copy(x_ref, slot(*me), local_sem)
        mine.start()
        first = [copy(0, me, sibling, src=x_ref)]
        first += [copy(1 + j, me, (*chip, c), src=x_ref) for j, chip in enumerate(chips)]
        for cp in first:
            cp.start()
        passed = [copy(4 + j, (*chip, c), sibling) for j, chip in enumerate(chips)]
        for j, chip in enumerate(chips):
            copy(1 + j, (*chip, c), me).wait_recv()
            passed[j].start()
        copy(0, sibling, me).wait_recv()
        for j, chip in enumerate(chips):
            copy(4 + j, (*chip, 1 - c), me).wait_recv()
        for cp in first + passed:
            cp.wait_send()
        mine.wait()

    return _pallas_call(
        body, name=name, out_shape=jax.ShapeDtypeStruct((N_DEV, rows, lanes), block.dtype),
        in_specs=[pl.BlockSpec(memory_space=pl.ANY)], out_specs=pl.BlockSpec(memory_space=pl.ANY),
        scratch_shapes=[pltpu.SemaphoreType.DMA((7,)), pltpu.SemaphoreType.DMA((7,)), pltpu.SemaphoreType.DMA],
    )(block)


def _ag_comm(items, bufs):
    def plan(in_refs, out_refs, send_sems, recv_sems, local_sems):
        x, y, c = lax.axis_index("x"), lax.axis_index("y"), lax.axis_index("c")
        me, sibling = (x, y, c), (x, y, 1 - c)
        chips = [(1 - x, y), (x, 1 - y), (1 - x, 1 - y)]
        plans = []
        for t, (_, buf, slot0) in enumerate(items):
            x_ref, out_ref = in_refs[t], out_refs[buf]

            def slot(px, py, pc, out_ref=out_ref, slot0=slot0):
                return out_ref.at[slot0 + 4 * px + 2 * py + pc]

            def copy(k, blk, to, src=None, t=t, slot=slot):
                return pltpu.make_async_remote_copy(
                    src_ref=slot(*blk) if src is None else src, dst_ref=slot(*blk), send_sem=send_sems.at[7 * t + k],
                    recv_sem=recv_sems.at[7 * t + k], device_id=to, device_id_type=_MESH)

            plans.append(dict(
                mine=pltpu.make_async_copy(x_ref, slot(*me), local_sems.at[t]),
                first=[copy(0, me, sibling, src=x_ref)] + [copy(1 + j, me, (*chip, c), src=x_ref)
                                                           for j, chip in enumerate(chips)],
                passed=[copy(4 + j, (*chip, c), sibling) for j, chip in enumerate(chips)],
                from_ici=[copy(1 + j, (*chip, c), me) for j, chip in enumerate(chips)],
                from_sibling=[copy(0, sibling, me)] + [copy(4 + j, (*chip, 1 - c), me) for j, chip in enumerate(chips)]))
        return plans

    def start(*refs):
        for p in plan(*refs):
            p["mine"].start()
            for cp in p["first"]:
                cp.start()

    def finish(*refs):
        plans = plan(*refs)
        for p in plans:
            for arrived, onward in zip(p["from_ici"], p["passed"]):
                arrived.wait_recv()
                onward.start()
        for p in plans:
            for arrived in p["from_sibling"]:
                arrived.wait_recv()
            for cp in p["first"] + p["passed"]:
                cp.wait_send()
            p["mine"].wait()

    dtype_of = {buf: shard.dtype for shard, buf, _ in items}
    out_shapes = [jax.ShapeDtypeStruct(b, dtype_of[j]) for j, b in enumerate(bufs)]
    return _Comm([it[0] for it in items], out_shapes, 7 * len(items), len(items), start, finish)


def _a2a_comm(items):
    def plan(in_refs, out_refs, send_sems, recv_sems, local_sems):
        x, y, c = lax.axis_index("x"), lax.axis_index("y"), lax.axis_index("c")
        my = 4 * x + 2 * y + c
        copies, locals_ = [], []
        for t, (_, slot0) in enumerate(items):
            s_ref, r_ref = in_refs[t], out_refs[t]
            locals_.append(pltpu.make_async_copy(s_ref.at[slot0 + my], r_ref.at[my], local_sems.at[t]))
            for kk in range(1, N_DEV):
                px = 1 - x if kk & 4 else x
                py = 1 - y if kk & 2 else y
                pc = 1 - c if kk & 1 else c
                copies.append(pltpu.make_async_remote_copy(
                    src_ref=s_ref.at[slot0 + 4 * px + 2 * py + pc], dst_ref=r_ref.at[my],
                    send_sem=send_sems.at[7 * t + kk - 1], recv_sem=recv_sems.at[7 * t + kk - 1],
                    device_id=(px, py, pc), device_id_type=_MESH))
        return copies, locals_

    def start(*refs):
        copies, locals_ = plan(*refs)
        for cp in locals_ + copies:
            cp.start()

    def finish(*refs):
        copies, locals_ = plan(*refs)
        for cp in copies + locals_:
            cp.wait()

    out_shapes = [jax.ShapeDtypeStruct((N_DEV,) + it[0].shape[1:], it[0].dtype) for it in items]
    return _Comm([it[0] for it in items], out_shapes, 7 * len(items), len(items), start, finish)


def _adam_math(g, w, m, v):
    m_new = ADAM_B1 * m + (1.0 - ADAM_B1) * g
    v_new = ADAM_B2 * v + (1.0 - ADAM_B2) * jnp.square(g)
    m_hat = m_new / (1.0 - ADAM_B1 ** ADAM_STEP)
    v_hat = v_new / (1.0 - ADAM_B2 ** ADAM_STEP)
    return -ADAM_LR * (m_hat / (jnp.sqrt(v_hat) + ADAM_EPS) + ADAM_WD * w), m_new, v_new


def _sum_partials(parts, name, tm):
    n, rows, _ = parts[0].shape
    widths = [p.shape[2] for p in parts]

    def body(*refs):
        g_ref, off = refs[-1], 0
        for p_ref, wd in zip(refs[:-1], widths):
            g = p_ref[0].astype(F32)
            for s in range(1, n):
                g = g + p_ref[s].astype(F32)
            g_ref[:, off:off + wd] = g
            off += wd

    return _pallas_call(
        body, name=name, grid=(rows // tm,), in_specs=[pl.BlockSpec((n, tm, wd), lambda i: (0, i, 0)) for wd in widths],
        out_specs=pl.BlockSpec((tm, sum(widths)), lambda i: (i, 0)),
        out_shape=jax.ShapeDtypeStruct((rows, sum(widths)), F32),
        compiler_params=pltpu.CompilerParams(dimension_semantics=("parallel",), vmem_limit_bytes=VMEM_MID),
    )(*parts)


def _adam(parts, w, m, v, name, tm):
    n, rows, _ = parts[0].shape
    widths = [p.shape[2] for p in parts]
    cols = sum(widths)

    def body(*refs):
        p_refs, (w_ref, m_ref, v_ref, g_ref, d_ref, nm_ref, nv_ref) = refs[:len(parts)], refs[len(parts):]
        off = 0
        for p_ref, wd in zip(p_refs, widths):
            g = p_ref[0].astype(F32)
            for s in range(1, n):
                g = g + p_ref[s].astype(F32)
            sl = slice(off, off + wd)
            g_ref[:, sl] = g
            d_ref[:, sl], nm_ref[:, sl], nv_ref[:, sl] = _adam_math(g, w_ref[:, sl], m_ref[:, sl], v_ref[:, sl])
            off += wd

    assert rows % tm == 0
    row = pl.BlockSpec((tm, cols), lambda i: (i, 0))
    shp = jax.ShapeDtypeStruct((rows, cols), F32)
    return _pallas_call(
        body, name=name, grid=(rows // tm,),
        in_specs=[pl.BlockSpec((n, tm, wd), lambda i: (0, i, 0)) for wd in widths] + [row, row, row],
        out_specs=[row] * 4, out_shape=[shp] * 4,
        compiler_params=pltpu.CompilerParams(dimension_semantics=("parallel",), vmem_limit_bytes=VMEM_MID),
    )(*parts, w, m, v)


_PK_LR, _PK_LI, _PK_GAINS, _PK_MISC, _PK_BR, _PK_BI, _PK_CR, _PK_CI, _PK_ROWS = 0, 1, 2, 3, 8, 24, 40, 56, 72
_PK_LDT_LANE, _PK_LOSS_LANE = D_MODEL + SSM_W, D_MODEL + SSM_W + LANES


def _pack_small(sg):
    names = ("lr", "li", "g_ffn", "g_fin", "dskip", "ldt", "loss", "br", "bi", "cr", "ci")

    def body(lr, li, gffn, gfin, dskip, ldt, loss, br, bi, cr, ci, o_ref):
        o_ref[...] = jnp.zeros_like(o_ref)
        o_ref[_PK_LR:_PK_LR + 1, :] = lr[...]
        o_ref[_PK_LI:_PK_LI + 1, :] = li[...]
        o_ref[_PK_GAINS:_PK_GAINS + 1, D_MODEL:] = gffn[...]
        o_ref[_PK_MISC:_PK_MISC + 1, :D_MODEL] = gfin[...]
        o_ref[_PK_MISC:_PK_MISC + 1, D_MODEL:D_MODEL + SSM_W] = dskip[...]
        o_ref[_PK_MISC:_PK_MISC + 1, _PK_LDT_LANE:_PK_LDT_LANE + LANES] = ldt[0:1, :]
        o_ref[_PK_MISC:_PK_MISC + 1, _PK_LOSS_LANE:_PK_LOSS_LANE + LANES] = loss[0:1, :]
        o_ref[_PK_BR:_PK_BR + SSM_CH, :] = br[...]
        o_ref[_PK_BI:_PK_BI + SSM_CH, :] = bi[...]
        o_ref[_PK_CR:_PK_CR + SSM_CH, :] = cr[...]
        o_ref[_PK_CI:_PK_CI + SSM_CH, :] = ci[...]

    return _pallas_call(body, name="pack_small", out_shape=jax.ShapeDtypeStruct((_PK_ROWS, N_STATE), F32))(
        *[sg[n] for n in names])


def _unpack_small(s, g_mix):
    unflat_b = unflat_c = lambda a: a.reshape(SSM_CH, SSM_GROUPS, SSM_STATE).transpose(1, 0, 2)[None]
    grads = {
        "norm_mix_g": g_mix, "norm_ffn_g": s[_PK_GAINS, D_MODEL:].reshape(1, D_MODEL),
        "norm_final_g": s[_PK_MISC, :D_MODEL].reshape(1, D_MODEL),
        "ssm_a_re": s[_PK_LR].reshape(1, SSM_GROUPS, SSM_STATE), "ssm_a_im": s[_PK_LI].reshape(1, SSM_GROUPS, SSM_STATE),
        "ssm_log_dt": s[_PK_MISC, _PK_LDT_LANE:_PK_LDT_LANE + SSM_GROUPS].reshape(1, SSM_GROUPS),
        "ssm_d": s[_PK_MISC, D_MODEL:D_MODEL + SSM_W].reshape(1, SSM_GROUPS, SSM_CH),
        "ssm_b_re": unflat_b(s[_PK_BR:_PK_BR + SSM_CH]), "ssm_b_im": unflat_b(s[_PK_BI:_PK_BI + SSM_CH]),
        "ssm_c_re": unflat_c(s[_PK_CR:_PK_CR + SSM_CH]), "ssm_c_im": unflat_c(s[_PK_CI:_PK_CI + SSM_CH]),
    }
    return s[_PK_MISC, _PK_LOSS_LANE], grads


def _stored(name, a):
    if name in ("ssm_b_re", "ssm_b_im"):
        return a.transpose(0, 1, 3, 2)
    return a.reshape(1, -1) if a.ndim == 1 else a


def _unstored(name, a, like):
    return a.transpose(0, 1, 3, 2) if name in ("ssm_b_re", "ssm_b_im") else a.reshape(like.shape)


def _adam_small(grads, wts, moms, vars_):
    n = len(SMALL_WEIGHTS)

    def body(*refs):
        ins, outs = refs[:4 * n], refs[4 * n:]
        for i in range(n):
            g, w, m, v = (ins[j * n + i][...] for j in range(4))
            outs[i][...], outs[n + i][...], outs[2 * n + i][...] = _adam_math(g, w, m, v)

    operands = [grads[k] if d is grads else _stored(k, d[k]) for d in (grads, wts, moms, vars_) for k in SMALL_WEIGHTS]
    shapes = [jax.ShapeDtypeStruct(_stored(k, wts[k]).shape, F32) for k in SMALL_WEIGHTS] * 3
    res = _pallas_call(body, name="adam_small", out_shape=shapes,
                         compiler_params=pltpu.CompilerParams(vmem_limit_bytes=VMEM_BIG))(*operands)
    out = {}
    for j, kind in enumerate(("delta", "new_m", "new_v")):
        for i, k in enumerate(SMALL_WEIGHTS):
            out[kind, k] = _unstored(k, res[j * n + i], wts[k])
    return out


def kernel(x, norm_mix_g, w_in, ssm_a_re, ssm_a_im, ssm_log_dt, ssm_b_re, ssm_b_im, ssm_c_re, ssm_c_im, ssm_d, w_glu, w_attn_out, w_out, norm_ffn_g, w_ffn_gate, w_ffn_up, w_ffn_down, norm_final_g, loss_target, m_norm_mix_g, m_w_in, m_ssm_a_re, m_ssm_a_im, m_ssm_log_dt, m_ssm_b_re, m_ssm_b_im, m_ssm_c_re, m_ssm_c_im, m_ssm_d, m_w_glu, m_w_attn_out, m_w_out, m_norm_ffn_g, m_w_ffn_gate, m_w_ffn_up, m_w_ffn_down, m_norm_final_g, v_norm_mix_g, v_w_in, v_ssm_a_re, v_ssm_a_im, v_ssm_log_dt, v_ssm_b_re, v_ssm_b_im, v_ssm_c_re, v_ssm_c_im, v_ssm_d, v_w_glu, v_w_attn_out, v_w_out, v_norm_ffn_g, v_w_ffn_gate, v_w_ffn_up, v_w_ffn_down, v_norm_final_g):
    args = dict(locals())
    wts = {n: args[n] for n in ALL_WEIGHTS}
    moms = {n: args["m_" + n] for n in ALL_WEIGHTS}
    vars_ = {n: args["v_" + n] for n in ALL_WEIGHTS}
    n_samples = x.shape[0]
    t = n_samples * SEQ

    shards = {n: (wts[n][0] if n in ROW_SHARDED else wts[n][0].T).astype(BF16) for n in BIG_WEIGHTS}
    w_in_t = _all_gather(shards["w_in"], "allgather_w_in").reshape(IN_W, D_MODEL)

    small = {n: wts[n] for n in SMALL_WEIGHTS}
    grad_x, recv, _, g_mix_part = _local_step(x.reshape(t, D_MODEL), loss_target.reshape(t, D_MODEL), {"w_in": w_in_t},
                                              small, shards)

    results = {}
    for n in BIG_WEIGHTS:
        c, k = shards[n].shape
        w2, m2, v2 = wts[n][0], moms[n][0], vars_[n][0]
        if n in ROW_SHARDED:
            res = _adam([recv[n]], w2, m2, v2, "adam_" + n, c // 2)
        elif n in HALVED:
            res = _adam([recv[f"{n}:{hf}"] for hf in range(2)], w2.T, m2.T, v2.T, "adam_" + n, c // 2)
            res = [a.T for a in res]
        else:
            g_t = _sum_partials([recv[n]], "sum_" + n, c // 2)
            res = _adam([g_t.T[None]], w2, m2, v2, "adam_" + n, k // 2)
        for kind, a in zip(("grad", "delta", "new_m", "new_v"), res):
            results[kind, n] = a[None]

    g_mix_all = _all_gather(jnp.pad(g_mix_part, ((0, 7), (0, 0))), "allgather_g_mix")
    g_mix = _sum_partials([g_mix_all], "sum_g_mix", 8)[0:1]
    loss, sgrads = _unpack_small(_sum_partials([recv["small"]], "sum_small", _PK_ROWS), g_mix)
    for n in SMALL_WEIGHTS:
        results["grad", n] = _unstored(n, sgrads[n], wts[n])
    results.update(_adam_small(sgrads, wts, moms, vars_))
    outs = [loss, grad_x.reshape(x.shape)]
    for kind in ("grad", "delta", "new_m", "new_v"):
        outs += [results[kind, n] for n in ALL_WEIGHTS]
    return tuple(outs)
```

```python
import functools
import math

import jax
import jax.numpy as jnp
from jax import lax
from jax.experimental import pallas as pl
from jax.experimental.pallas import tpu as pltpu

F32 = jnp.float32
BF16 = jnp.bfloat16
MXU_DTYPE = jnp.bfloat16

N_DEV = 8
D_MODEL = 1024
SEQ = 2048
HEAD_DIM = 64
HEADS_PER_GROUP = 4
GROUP_W = HEADS_PER_GROUP * HEAD_DIM
DILATIONS = (1, 4, 16)
QKV_W = 3 * len(DILATIONS) * GROUP_W
Q_W = len(DILATIONS) * GROUP_W
ATT_BLOCK = 128
ROPE_DIM = 16
ROPE_THETA = 500000.0
SSM_W = 512
SSM_GROUPS = 32
SSM_CH = 16
SSM_STATE = 64
N_STATE = SSM_GROUPS * SSM_STATE
D_FF = 2816
IN_W = QKV_W + SSM_W + 2 * D_MODEL
RMS_EPS = 1e-6
NEG_INF = -1e30
LANES = 128

SCAN_SEG_PER_SAMPLE = 8
SCAN_LEN = SEQ // SCAN_SEG_PER_SAMPLE
SCAN_WC = 512
SCAN_NBLK = N_STATE // SCAN_WC
SCAN_CH = SSM_W // SCAN_NBLK
SCAN_CHUNK = 32

ADAM_LR = 0.001
ADAM_B1 = 0.9
ADAM_B2 = 0.999
ADAM_EPS = 1e-08
ADAM_WD = 0.01
ADAM_STEP = 10

VMEM_BIG = 48 * 1024 * 1024
VMEM_MID = 32 * 1024 * 1024

BIG_WEIGHTS = ("w_in", "w_glu", "w_attn_out", "w_out", "w_ffn_gate", "w_ffn_up", "w_ffn_down")
ROW_SHARDED = ("w_out", "w_ffn_down")
SMALL_WEIGHTS = ("norm_mix_g", "ssm_a_re", "ssm_a_im", "ssm_log_dt", "ssm_b_re", "ssm_b_im", "ssm_c_re", "ssm_c_im",
                 "ssm_d", "norm_ffn_g", "norm_final_g")
ALL_WEIGHTS = ("norm_mix_g", "w_in", "ssm_a_re", "ssm_a_im", "ssm_log_dt", "ssm_b_re", "ssm_b_im", "ssm_c_re", "ssm_c_im",
               "ssm_d", "w_glu", "w_attn_out", "w_out", "norm_ffn_g", "w_ffn_gate", "w_ffn_up", "w_ffn_down", "norm_final_g")


def _sigmoid(x):
    return 1.0 / (1.0 + jnp.exp(-x))


def _pallas_call(body, *, out_shape, **kw):
    single = not isinstance(out_shape, (list, tuple))
    shapes = [pltpu.HBM(s.shape, s.dtype) for s in ([out_shape] if single else out_shape)]
    call = pl.pallas_call(body, out_shape=shapes[0] if single else shapes, **kw)
    return lambda *operands: call(*[pltpu.with_memory_space_constraint(o, pltpu.HBM) for o in operands])


class _Comm:
    def __init__(self, ins, out_shapes, n_sem, n_local, start, finish):
        self.ins, self.out_shapes, self.n_sem, self.n_local = ins, out_shapes, n_sem, n_local
        self.start, self.finish = start, finish


def _mm(a, b, mode, name, tm, tn, out_dtype=F32, add=None, vmem=VMEM_BIG, comm=None, cols=None):
    if mode == "nn":
        (m, k), (_, n) = a.shape, b.shape
        a_spec = pl.BlockSpec((tm, k), lambda i, j: (i, 0))
        b_spec = pl.BlockSpec((k, tn), lambda i, j: (0, j))
        dims = (((1,), (0,)), ((), ()))
    elif mode == "nt":
        (m, k), (n, _) = a.shape, b.shape
        a_spec = pl.BlockSpec((tm, k), lambda i, j: (i, 0))
        b_spec = pl.BlockSpec((tn, k), lambda i, j: (j, 0))
        dims = (((1,), (1,)), ((), ()))
    else:
        (k, m), (_, n) = a.shape, b.shape
        first, n = cols if cols else (0, n)
        a_spec = pl.BlockSpec((k, tm), lambda i, j: (0, i))
        b_spec = pl.BlockSpec((k, tn), lambda i, j: (0, j + first // tn))
        dims = (((0,), (0,)), ((), ()))
    assert m % tm == 0 and n % tn == 0, (name, m, n, tm, tn)
    o_spec = pl.BlockSpec((tm, tn), lambda i, j: (i, j))
    has_add = add is not None

    def body(*refs):
        a_ref, b_ref, o_ref = refs[0], refs[1], refs[-1]
        acc = lax.dot_general(a_ref[...].astype(MXU_DTYPE), b_ref[...].astype(MXU_DTYPE), dims,
                              preferred_element_type=F32)
        if has_add:
            acc = acc + refs[2][...]
        o_ref[...] = acc.astype(out_dtype)

    ins = [a, b] + ([add] if has_add else [])
    in_specs = [a_spec, b_spec] + ([o_spec] if has_add else [])
    return _grid_call(body, name, (m // tm, n // tn), ins, in_specs, [o_spec],
                      [jax.ShapeDtypeStruct((m, n), out_dtype)], vmem, comm)


def _grid_call(body, name, grid, ins, in_specs, out_specs, out_shapes, vmem, comm=None, sequential=False, scratch=()):
    if comm is None:
        single = len(out_shapes) == 1
        semantics = ("arbitrary", "arbitrary") if sequential else ("parallel", "parallel")
        return _pallas_call(
            body, name=name, grid=grid, in_specs=in_specs, out_specs=out_specs[0] if single else out_specs,
            out_shape=out_shapes[0] if single else out_shapes, scratch_shapes=list(scratch),
            compiler_params=pltpu.CompilerParams(dimension_semantics=semantics, vmem_limit_bytes=vmem),
        )(*ins)
    n_in, n_out, n_cin, n_cout = len(ins), len(out_shapes), len(comm.ins), len(comm.out_shapes)
    n_io = n_in + n_cin + n_out + n_cout

    def carrying(*refs):
        own = refs[:n_in] + refs[n_in + n_cin:n_in + n_cin + n_out] + refs[n_io:len(refs) - 3]
        c_args = (refs[n_in:n_in + n_cin], refs[n_in + n_cin + n_out:n_io], *refs[-3:])

        @pl.when((pl.program_id(0) == 0) & (pl.program_id(1) == 0))
        def _():
            comm.start(*c_args)

        body(*own)

        @pl.when((pl.program_id(0) == grid[0] - 1) & (pl.program_id(1) == grid[1] - 1))
        def _():
            comm.finish(*c_args)

    hbm = pl.BlockSpec(memory_space=pl.ANY)
    return _pallas_call(
        carrying, name=name, grid=grid, in_specs=list(in_specs) + [hbm] * n_cin,
        out_specs=list(out_specs) + [hbm] * n_cout, out_shape=list(out_shapes) + list(comm.out_shapes),
        scratch_shapes=list(scratch) + [pltpu.SemaphoreType.DMA((comm.n_sem,)), pltpu.SemaphoreType.DMA((comm.n_sem,)),
                                        pltpu.SemaphoreType.DMA((comm.n_local,))],
        compiler_params=pltpu.CompilerParams(dimension_semantics=("arbitrary", "arbitrary"), vmem_limit_bytes=vmem),
    )(*ins, *comm.ins)


def _gather_residue(stage, ch, r, d, n):
    return stage[ch, pl.ds(r, n, stride=d), :] if d > 1 else stage[ch]


def _scatter_residue(stage, ch, r, d, n, val):
    if d > 1:
        stage[ch, pl.ds(r, n, stride=d), :] = val
    else:
        stage[ch] = val


def _lane_chunk(ch):
    return slice(ch * LANES, (ch + 1) * LANES)


def _rope_tables():
    half = ROPE_DIM // 2
    inv = jnp.power(jnp.float32(ROPE_THETA), -jnp.arange(half, dtype=F32) * 2.0 / ROPE_DIM)
    ang = jnp.arange(SEQ, dtype=F32)[:, None] * inv[None, :]
    lane = jnp.arange(LANES) % HEAD_DIM
    cosl = jnp.cos(ang)[:, lane % half]
    sinl = jnp.sin(ang)[:, lane % half]
    tab_c = jnp.where(lane < ROPE_DIM, cosl, 1.0)
    tab_lo = jnp.where(lane < half, -sinl, 0.0)
    tab_hi = jnp.where((lane >= half) & (lane < ROPE_DIM), sinl, 0.0)
    return tab_c.astype(F32), tab_lo.astype(F32), tab_hi.astype(F32)


def _rope_apply(t, tc, tlo, thi):
    half = ROPE_DIM // 2
    return t * tc + pltpu.roll(t, LANES - half, 1) * tlo + pltpu.roll(t, half, 1) * thi


def _rope_transpose(dt, tc, tlo, thi):
    half = ROPE_DIM // 2
    return dt * tc + pltpu.roll(dt * tlo, half, 1) + pltpu.roll(dt * thi, LANES - half, 1)


def _pack_dproj(dqs, dks, dvs, du, dgpre, tabs, comm=None):
    tm = 256

    def body(*refs):
        dq_refs, dk_refs, dv_refs = refs[0:3], refs[3:6], refs[6:9]
        du_ref, dg_ref, tc_ref, tlo_ref, thi_ref, o_ref, stage = refs[9:16]
        n_ch = QKV_W // LANES
        halves = GROUP_W // LANES
        for grp, d in enumerate(DILATIONS):
            for which, src in enumerate((dq_refs[grp], dk_refs[grp], dv_refs[grp])):
                for res in range(d):
                    for half in range(halves):
                        _scatter_residue(stage, which * (n_ch // 3) + grp * halves + half, res, d, tm // d,
                                         src[:, _lane_chunk(res * halves + half)])
        tc, tlo, thi = tc_ref[...], tlo_ref[...], thi_ref[...]
        for ch in range(n_ch):
            piece = stage[ch]
            o_ref[:, _lane_chunk(ch)] = (_rope_transpose(piece, tc, tlo, thi) if ch < 2 * n_ch // 3 else piece).astype(BF16)
        o_ref[:, QKV_W:QKV_W + SSM_W] = du_ref[...].astype(BF16)
        o_ref[:, QKV_W + SSM_W:] = dg_ref[...].astype(BF16)

    t = du.shape[0]
    dil = [_row_spec(tm // d, d * GROUP_W) for d in DILATIONS]
    tab = pl.BlockSpec((tm, LANES), lambda i, j: (i % (SEQ // tm), 0))
    return _grid_call(
        body, "pack_dproj", (t // tm, 1), [*dqs, *dks, *dvs, du, dgpre, *tabs],
        dil * 3 + [_row_spec(tm, SSM_W), _row_spec(tm, 2 * D_MODEL), tab, tab, tab], [_row_spec(tm, IN_W)],
        [jax.ShapeDtypeStruct((t, IN_W), BF16)], VMEM_MID, comm, scratch=[pltpu.VMEM((QKV_W // LANES, tm, LANES), F32)])


def _merge_groups(o_refs, l_refs, a_ref, lt_ref, nat, tm):
    halves = GROUP_W // LANES
    for grp, d in enumerate(DILATIONS[1:], start=1):
        for j, src in enumerate((o_refs[grp], l_refs[grp])):
            for res in range(d):
                for half in range(halves):
                    _scatter_residue(nat, (grp - 1) * 4 + j * 2 + half, res, d, tm // d,
                                     src[:, _lane_chunk(res * halves + half)])
    for half in range(halves):
        sl = _lane_chunk(half)
        la, lb, lc = l_refs[0][:, sl], nat[2 + half], nat[6 + half]
        m = jnp.maximum(jnp.maximum(la, lb), lc)
        ea, eb, ec = jnp.exp(la - m), jnp.exp(lb - m), jnp.exp(lc - m)
        ssum = ea + eb + ec
        a_ref[:, sl] = (ea / ssum) * o_refs[0][:, sl] + (eb / ssum) * nat[half] + (ec / ssum) * nat[4 + half]
        lt_ref[:, sl] = m + jnp.log(ssum)


def _head_sum_matrix():
    r = jnp.arange(GROUP_W) // HEAD_DIM
    return (r[:, None] == r[None, :]).astype(F32)


def _attention_cotangents(da, attn, lt, ones, rd_ref, dil, stage, tm):
    halves = GROUP_W // LANES
    rd = jnp.dot(da * attn, ones, preferred_element_type=F32, precision=lax.Precision.HIGHEST)
    rd_ref[...] = rd
    for half in range(halves):
        for j, val in enumerate((da, lt, rd)):
            stage[2 * j + half] = val[:, _lane_chunk(half)]
    for grp, d in enumerate(DILATIONS[1:], start=1):
        for j in range(3):
            for res in range(d):
                for half in range(halves):
                    dil[3 * (grp - 1) + j][:, _lane_chunk(res * halves + half)] = _gather_residue(
                        stage, 2 * j + half, res, d, tm // d)


_GELU_C = math.sqrt(2.0 / math.pi)


def _head_masks():
    lane = lax.broadcasted_iota(jnp.int32, (1, GROUP_W), 1)
    return [(lane // HEAD_DIM) == h for h in range(HEADS_PER_GROUP)]


def _stack_heads(blk, masks, fill=0.0):
    return jnp.concatenate([jnp.where(mk, blk, jnp.full_like(blk, fill)) for mk in masks], axis=0)


def _unstack_heads(stacked, masks):
    rows = stacked.shape[0] // len(masks)
    out = stacked[:rows]
    for h in range(1, len(masks)):
        out = jnp.where(masks[h], stacked[h * rows:(h + 1) * rows], out)
    return out


def _band_mask(first):
    nk = ATT_BLOCK if first else 2 * ATT_BLOCK
    qi = lax.broadcasted_iota(jnp.int32, (ATT_BLOCK, nk), 0)
    ki = lax.broadcasted_iota(jnp.int32, (ATT_BLOCK, nk), 1)
    dist = qi - ki + (0 if first else ATT_BLOCK)
    return (dist >= 0) & (dist <= ATT_BLOCK)


_NT = (((1,), (1,)), ((), ()))
_TN = (((0,), (0,)), ((), ()))


def _residues_per_step(d):
    return 4 if d >= 16 else 1


def _attn_fwd(q, k, v, group, n_samples, comm=None):
    d = DILATIONS[group]
    length = SEQ // d
    nb = length // ATT_BLOCK

    rps = _residues_per_step(d)

    def body(q_ref, k_ref, v_ref, o_ref, l_ref):
        for rl in range(rps):
            residue(q_ref, k_ref, v_ref, o_ref, l_ref, slice(rl * GROUP_W, (rl + 1) * GROUP_W))

    def residue(q_ref, k_ref, v_ref, o_ref, l_ref, cols):
        masks = _head_masks()

        def block(qs, ks, first):
            nk = ATT_BLOCK if first else 2 * ATT_BLOCK
            qb = q_ref[0, pl.ds(qs, ATT_BLOCK), cols]
            kc = k_ref[0, pl.ds(ks, nk), cols]
            vc = v_ref[0, pl.ds(ks, nk), cols]
            q4 = _stack_heads(qb, masks)
            valid = jnp.tile(_band_mask(first), (HEADS_PER_GROUP, 1))
            s = lax.dot_general(q4, kc, _NT, preferred_element_type=F32) * (HEAD_DIM ** -0.5)
            s = jnp.where(valid, s, NEG_INF)
            m = jnp.max(s, axis=-1, keepdims=True)
            p = jnp.exp(s - m)
            l = jnp.sum(p, axis=-1, keepdims=True)
            o4 = jnp.dot(p.astype(MXU_DTYPE), vc, preferred_element_type=F32) / l
            lse4 = jnp.broadcast_to(m + jnp.log(l), o4.shape)
            o_ref[0, pl.ds(qs, ATT_BLOCK), cols] = _unstack_heads(o4, masks)
            l_ref[0, pl.ds(qs, ATT_BLOCK), cols] = _unstack_heads(lse4, masks)

        block(0, 0, True)
        if nb > 1:
            def loop(n, carry):
                block(pl.multiple_of(n * ATT_BLOCK, ATT_BLOCK), pl.multiple_of((n - 1) * ATT_BLOCK, ATT_BLOCK), False)
                return carry

            lax.fori_loop(1, nb, loop, 0)

    per_sample = lambda a: a.reshape(n_samples, length, d * GROUP_W)
    spec = pl.BlockSpec((1, length, rps * GROUP_W), lambda b, r: (b, 0, r))
    shp = jax.ShapeDtypeStruct((n_samples, length, d * GROUP_W), F32)
    o, lse, *carried = _grid_call(body, f"attn_fwd_g{group}", (n_samples, d // rps), [per_sample(a) for a in (q, k, v)],
                                  [spec] * 3, [spec] * 2, [shp, shp], VMEM_MID, comm)
    flat = lambda a: a.reshape(n_samples * length, d * GROUP_W)
    return flat(o), flat(lse), carried


def _attn_bwd(q, k, v, dattn, lse_tot, rowdot, group, n_samples, comm=None):
    d = DILATIONS[group]
    length = SEQ // d
    nb = length // ATT_BLOCK

    rps = _residues_per_step(d)

    def body(q_ref, k_ref, v_ref, da_ref, lt_ref, rd_ref, dq_ref, dk_ref, dv_ref):
        dk_ref[...] = jnp.zeros_like(dk_ref)
        dv_ref[...] = jnp.zeros_like(dv_ref)
        for rl in range(rps):
            residue(q_ref, k_ref, v_ref, da_ref, lt_ref, rd_ref, dq_ref, dk_ref, dv_ref,
                    slice(rl * GROUP_W, (rl + 1) * GROUP_W))

    def residue(q_ref, k_ref, v_ref, da_ref, lt_ref, rd_ref, dq_ref, dk_ref, dv_ref, cols):
        masks = _head_masks()

        def block(qs, ks, first):
            nk = ATT_BLOCK if first else 2 * ATT_BLOCK
            qb = q_ref[0, pl.ds(qs, ATT_BLOCK), cols]
            kc = k_ref[0, pl.ds(ks, nk), cols]
            vc = v_ref[0, pl.ds(ks, nk), cols]
            da = da_ref[0, pl.ds(qs, ATT_BLOCK), cols]
            lt = lt_ref[0, pl.ds(qs, ATT_BLOCK), cols]
            rd = rd_ref[0, pl.ds(qs, ATT_BLOCK), cols]
            q4 = _stack_heads(qb, masks)
            da4 = _stack_heads(da, masks).astype(MXU_DTYPE)
            lt4 = jnp.max(_stack_heads(lt, masks, -jnp.inf), axis=-1, keepdims=True)
            rd4 = jnp.max(_stack_heads(rd, masks, -jnp.inf), axis=-1, keepdims=True)
            valid = jnp.tile(_band_mask(first), (HEADS_PER_GROUP, 1))
            s = lax.dot_general(q4, kc, _NT, preferred_element_type=F32) * (HEAD_DIM ** -0.5)
            s = jnp.where(valid, s, NEG_INF)
            p = jnp.exp(s - lt4)
            dp = lax.dot_general(da4, vc, _NT, preferred_element_type=F32)
            ds = (p * (dp - rd4) * (HEAD_DIM ** -0.5)).astype(MXU_DTYPE)
            dq_ref[0, pl.ds(qs, ATT_BLOCK), cols] = _unstack_heads(jnp.dot(ds, kc, preferred_element_type=F32), masks)
            dk_ref[0, pl.ds(ks, nk), cols] += lax.dot_general(ds, q4, _TN, preferred_element_type=F32)
            dv_ref[0, pl.ds(ks, nk), cols] += lax.dot_general(p.astype(MXU_DTYPE), da4, _TN, preferred_element_type=F32)

        block(0, 0, True)
        if nb > 1:
            def loop(n, carry):
                block(pl.multiple_of(n * ATT_BLOCK, ATT_BLOCK), pl.multiple_of((n - 1) * ATT_BLOCK, ATT_BLOCK), False)
                return carry

            lax.fori_loop(1, nb, loop, 0)

    per_sample = lambda a: a.reshape(n_samples, length, d * GROUP_W)
    spec = pl.BlockSpec((1, length, rps * GROUP_W), lambda b, r: (b, 0, r))
    shp = jax.ShapeDtypeStruct((n_samples, length, d * GROUP_W), F32)
    dq, dk, dv, *carried = _grid_call(
        body, f"attn_bwd_g{group}", (n_samples, d // rps), [per_sample(a) for a in (q, k, v, dattn, lse_tot, rowdot)],
        [spec] * 6, [spec] * 3, [shp, shp, shp], VMEM_MID, comm)
    flat = lambda a: a.reshape(n_samples * length, d * GROUP_W)
    return flat(dq), flat(dk), flat(dv), carried


def _disc(lr, li, ldt, br, bi):
    dt = jnp.exp(ldt)
    mag = jnp.exp(lr * dt)
    ab_re, ab_im = mag * jnp.cos(li * dt), mag * jnp.sin(li * dt)
    den = lr * lr + li * li
    nr, ni = ab_re - 1.0, ab_im
    f_re = (nr * lr + ni * li) / den
    f_im = (ni * lr - nr * li) / den
    return ab_re, ab_im, f_re * br - f_im * bi, f_re * bi + f_im * br


def _state_mask():
    row_g = lax.broadcasted_iota(jnp.int32, (SCAN_CH, SCAN_WC), 0) // SSM_CH
    col_g = lax.broadcasted_iota(jnp.int32, (SCAN_CH, SCAN_WC), 1) // SSM_STATE
    return row_g == col_g


def _ssm_disc(lr, li, ldt, br, bi, cr, ci):
    w = SCAN_WC

    def body(lr_ref, li_ref, ldt_ref, br_ref, bi_ref, cr_ref, ci_ref, a_ref, bb_ref, c_ref):
        ar, ai, bbr, bbi = _disc(lr_ref[...], li_ref[...], ldt_ref[...], br_ref[...], bi_ref[...])
        crv, civ = cr_ref[...], ci_ref[...]
        mask = _state_mask()
        for cb in range(SCAN_NBLK):
            sl = slice(cb * w, (cb + 1) * w)
            rows = slice(cb * SCAN_CH, (cb + 1) * SCAN_CH)
            dense = lambda comp: jnp.where(mask, jnp.tile(comp[:, sl], (SCAN_CH // SSM_CH, 1)), 0.0)
            a_ref[:, 2 * cb * w:(2 * cb + 1) * w] = ar[:, sl]
            a_ref[:, (2 * cb + 1) * w:(2 * cb + 2) * w] = ai[:, sl]
            bb_ref[rows, :w] = dense(bbr).astype(MXU_DTYPE)
            bb_ref[rows, w:] = dense(bbi).astype(MXU_DTYPE)
            c_ref[rows, :w] = dense(crv).astype(MXU_DTYPE)
            c_ref[rows, w:] = (-dense(civ)).astype(MXU_DTYPE)

    return _pallas_call(
        body, name="ssm_disc",
        out_shape=[jax.ShapeDtypeStruct((1, 2 * N_STATE), F32), jax.ShapeDtypeStruct((SSM_W, 2 * w), MXU_DTYPE),
                   jax.ShapeDtypeStruct((SSM_W, 2 * w), MXU_DTYPE)],
        compiler_params=pltpu.CompilerParams(vmem_limit_bytes=VMEM_MID),
    )(lr, li, ldt, br, bi, cr, ci)


def _group_indicator():
    s = jnp.arange(N_STATE) // SSM_STATE
    return (s[:, None] == jnp.arange(LANES)[None, :]).astype(F32)


def _ssm_param_bwd(lr, li, ldt, br, bi, da_cat, dbb_full, dc_full):
    w = SCAN_WC

    def body(lr_ref, li_ref, ldt_ref, br_ref, bi_ref, da_ref, dbb_ref, dc_ref, ind_ref,
             glr_ref, gli_ref, gldt_ref, gbr_ref, gbi_ref, gcr_ref, gci_ref):
        mask = _state_mask()

        def diag_parts(ref):
            res = ([], [])
            for cb in range(SCAN_NBLK):
                for part in range(2):
                    blk = ref[cb * SCAN_CH:(cb + 1) * SCAN_CH, part * w:(part + 1) * w]
                    res[part].append(jnp.sum(jnp.where(mask, blk, 0.0).reshape(SCAN_CH // SSM_CH, SSM_CH, w), axis=0))
            return jnp.concatenate(res[0], axis=1), jnp.concatenate(res[1], axis=1)

        dar = jnp.concatenate([da_ref[:, 2 * cb * w:(2 * cb + 1) * w] for cb in range(SCAN_NBLK)], axis=1)
        dai = jnp.concatenate([da_ref[:, (2 * cb + 1) * w:(2 * cb + 2) * w] for cb in range(SCAN_NBLK)], axis=1)
        dbbr, dbbi = diag_parts(dbb_ref)
        dcr, dci_neg = diag_parts(dc_ref)
        gcr_ref[...] = dcr
        gci_ref[...] = -dci_neg
        _, vjp = jax.vjp(_disc, lr_ref[...], li_ref[...], ldt_ref[...], br_ref[...], bi_ref[...])
        glr, gli, gldt, gbr, gbi = vjp((dar, dai, dbbr, dbbi))
        glr_ref[...] = glr
        gli_ref[...] = gli
        gldt_ref[...] = jnp.dot(jnp.broadcast_to(gldt, (8, N_STATE)), ind_ref[...], preferred_element_type=F32,
                                precision=lax.Precision.HIGHEST)
        gbr_ref[...] = gbr
        gbi_ref[...] = gbi

    v1 = jax.ShapeDtypeStruct((1, N_STATE), F32)
    v16 = jax.ShapeDtypeStruct((SSM_CH, N_STATE), F32)
    vdt = jax.ShapeDtypeStruct((8, LANES), F32)
    return _pallas_call(
        body, name="ssm_param_bwd", out_shape=[v1, v1, vdt, v16, v16, v16, v16],
        compiler_params=pltpu.CompilerParams(vmem_limit_bytes=VMEM_BIG),
    )(lr, li, ldt, br, bi, da_cat, dbb_full, dc_full, _group_indicator())


def _cmul(ar, ai, br, bi):
    return ar * br - ai * bi, ar * bi + ai * br


def _gelu_tanh(y):
    return jnp.tanh(_GELU_C * (y + 0.044715 * (y * y * y)))


def _segment_carry(er, ei, ar, ai, n_rows, reverse):
    qr, qi = ar, ai
    for _ in range(int(math.log2(SCAN_LEN))):
        qr, qi = _cmul(qr, qi, qr, qi)
    seg = lax.broadcasted_iota(jnp.int32, er.shape, 0) % SCAN_SEG_PER_SAMPLE
    shift = 1
    while shift < SCAN_SEG_PER_SAMPLE:
        keep = (seg < SCAN_SEG_PER_SAMPLE - shift) if reverse else (seg >= shift)
        amount = n_rows - shift if reverse else shift
        sr = jnp.where(keep, pltpu.roll(er, amount, 0), 0.0)
        si = jnp.where(keep, pltpu.roll(ei, amount, 0), 0.0)
        if reverse:
            er, ei = er + qr * sr + qi * si, ei + qr * si - qi * sr
        else:
            er, ei = er + qr * sr - qi * si, ei + qr * si + qi * sr
        qr, qi = _cmul(qr, qi, qr, qi)
        shift *= 2
    keep = (seg < SCAN_SEG_PER_SAMPLE - 1) if reverse else (seg >= 1)
    amount = n_rows - 1 if reverse else 1
    return jnp.where(keep, pltpu.roll(er, amount, 0), 0.0), jnp.where(keep, pltpu.roll(ei, amount, 0), 0.0)


def _ssm_fwd(u_perm, a_cat, bbc, cc, dskip, n_rows):
    t = u_perm.shape[0]
    w = SCAN_WC
    rows_c = SCAN_CHUNK * n_rows
    n_chunks = t // rows_c

    assert n_chunks % 2 == 0

    def body(u_ref, a_ref, bb_ref, c_ref, d_ref, yt_ref, yg_ref, ein_ref, bu_all, st_a, st_b, xs_a, xs_b):
        ar = jnp.broadcast_to(a_ref[:, :w], (n_rows, w))
        ai = jnp.broadcast_to(a_ref[:, w:], (n_rows, w))
        start = lambda ch: pl.multiple_of(ch * rows_c, rows_c)

        def project(ch, stage):
            res = jnp.dot(u_ref[pl.ds(start(ch), rows_c), :].astype(MXU_DTYPE), bb_ref[...], preferred_element_type=F32)
            stage[...] = res
            bu_all[pl.ds(start(ch), rows_c), :] = res

        def steps(src, r0, carry, xs=None):
            for i in range(SCAN_CHUNK):
                blk = src[pl.ds(r0 + i * n_rows, n_rows), :]
                carry = (ar * carry[0] - ai * carry[1] + blk[:, :w], ar * carry[1] + ai * carry[0] + blk[:, w:])
                if xs is not None:
                    xs[i * n_rows:(i + 1) * n_rows, :w] = carry[0]
                    xs[i * n_rows:(i + 1) * n_rows, w:] = carry[1]
            return carry

        def emit(xs, ch):
            y = lax.dot_general(xs[...].astype(MXU_DTYPE), c_ref[...], _NT, preferred_element_type=F32)
            yt = y + d_ref[...] * u_ref[pl.ds(start(ch), rows_c), :]
            yt_ref[pl.ds(start(ch), rows_c), :] = yt
            yg_ref[pl.ds(start(ch), rows_c), :] = (0.5 * yt * (1.0 + _gelu_tanh(yt))).astype(BF16)

        project(0, st_a)

        def pair1(p, carry):
            project(2 * p + 1, st_b)
            carry = steps(st_a, 0, carry)
            project(jnp.minimum(2 * p + 2, n_chunks - 1), st_a)
            return steps(st_b, 0, carry)

        zero = jnp.zeros((n_rows, w), F32)
        er, ei = lax.fori_loop(0, n_chunks // 2, pair1, (zero, zero))
        cr, ci = _segment_carry(er, ei, ar, ai, n_rows, False)
        ein_ref[:, :w] = cr
        ein_ref[:, w:] = ci

        xs_b[...] = jnp.zeros_like(xs_b)

        def pair2(p, carry):
            emit(xs_b, jnp.maximum(2 * p - 1, 0))
            carry = steps(bu_all, start(2 * p), carry, xs_a)
            emit(xs_a, 2 * p)
            return steps(bu_all, start(2 * p + 1), carry, xs_b)

        lax.fori_loop(0, n_chunks // 2, pair2, (cr, ci))
        emit(xs_b, n_chunks - 1)

    col = lambda width: pl.BlockSpec((t, width), lambda c: (0, c))
    wgt = pl.BlockSpec((SCAN_CH, 2 * w), lambda c: (c, 0))
    return _pallas_call(
        body, name="ssm_fwd", grid=(SCAN_NBLK,),
        in_specs=[col(SCAN_CH), pl.BlockSpec((1, 2 * w), lambda c: (0, c)), wgt, wgt,
                  pl.BlockSpec((1, SCAN_CH), lambda c: (0, c))],
        out_specs=[col(SCAN_CH), col(SCAN_CH), pl.BlockSpec((n_rows, 2 * w), lambda c: (0, c))],
        out_shape=[jax.ShapeDtypeStruct((t, SSM_W), F32), jax.ShapeDtypeStruct((t, SSM_W), BF16),
                   jax.ShapeDtypeStruct((n_rows, 2 * N_STATE), F32)],
        scratch_shapes=[pltpu.VMEM((t, 2 * w), F32)] + [pltpu.VMEM((rows_c, 2 * w), F32)] * 4,
        compiler_params=pltpu.CompilerParams(dimension_semantics=("parallel",), vmem_limit_bytes=VMEM_BIG),
    )(u_perm, a_cat, bbc, cc, dskip)


def _ssm_bwd(u_perm, dyg, ytot, dskip, a_cat, bbc, cc, ein, n_rows, comm=None):
    t = u_perm.shape[0]
    w = SCAN_WC
    rows_c = SCAN_CHUNK * n_rows
    n_chunks = t // rows_c

    assert n_chunks % 2 == 0
    last = n_chunks - 1

    def body(u_ref, dyg_ref, yt_ref, dk_ref, a_ref, bb_ref, c_ref, ein_ref, du_ref, gd_ref, da_ref, dbb_ref, dc_ref,
             xs_all, dy_s, st_a, st_b, buf_a, buf_b):
        ar = jnp.broadcast_to(a_ref[:, :w], (n_rows, w))
        ai = jnp.broadcast_to(a_ref[:, w:], (n_rows, w))
        zero = jnp.zeros((n_rows, w), F32)
        start = lambda ch: pl.multiple_of(ch * rows_c, rows_c)
        dbb_ref[...] = jnp.zeros_like(dbb_ref)
        dc_ref[...] = jnp.zeros_like(dc_ref)
        da_ref[...] = jnp.zeros_like(da_ref)

        yt = yt_ref[...]
        th = _gelu_tanh(yt)
        dgelu = 0.5 * (1.0 + th) + 0.5 * yt * (1.0 - th * th) * _GELU_C * (1.0 + 3.0 * 0.044715 * yt * yt)
        dy_all = dyg_ref[...] * dgelu
        dy_s[...] = dy_all
        gd_ref[...] = jnp.sum(dy_all * u_ref[...], axis=0, keepdims=True)
        dy_chunk = lambda ch: dy_s[pl.ds(start(ch), rows_c), :].astype(MXU_DTYPE)

        xs_all[0:n_rows, :] = ein_ref[...]

        def project(ch, stage):
            stage[...] = jnp.dot(u_ref[pl.ds(start(ch), rows_c), :].astype(MXU_DTYPE), bb_ref[...],
                                 preferred_element_type=F32)

        def fwd_steps(stage, ch, carry, xs):
            for i in range(SCAN_CHUNK):
                blk = stage[i * n_rows:(i + 1) * n_rows, :]
                carry = (ar * carry[0] - ai * carry[1] + blk[:, :w], ar * carry[1] + ai * carry[0] + blk[:, w:])
                for half, val in enumerate(carry):
                    xs[i * n_rows:(i + 1) * n_rows, half * w:(half + 1) * w] = val
                    xs_all[pl.ds(start(ch) + (i + 1) * n_rows, n_rows), half * w:(half + 1) * w] = val
            return carry

        def add_dc(xs, ch):
            dc_ref[...] += lax.dot_general(dy_chunk(ch), xs[...].astype(MXU_DTYPE), _TN, preferred_element_type=F32)

        project(0, st_a)

        def fwd_pair(p, carry):
            project(2 * p + 1, st_b)
            carry = fwd_steps(st_a, 2 * p, carry, buf_a)
            add_dc(buf_a, 2 * p)
            project(jnp.minimum(2 * p + 2, last), st_a)
            carry = fwd_steps(st_b, 2 * p + 1, carry, buf_b)
            add_dc(buf_b, 2 * p + 1)
            return carry

        lax.fori_loop(0, n_chunks // 2, fwd_pair, (ein_ref[:, :w], ein_ref[:, w:]))

        def project_dx(ch, stage):
            stage[...] = jnp.dot(dy_chunk(ch), c_ref[...], preferred_element_type=F32)

        def back_steps(stage, carry, g_buf=None):
            for i in reversed(range(SCAN_CHUNK)):
                blk = stage[i * n_rows:(i + 1) * n_rows, :]
                carry = (blk[:, :w] + ar * carry[0] + ai * carry[1], blk[:, w:] + ar * carry[1] - ai * carry[0])
                if g_buf is not None:
                    g_buf[i * n_rows:(i + 1) * n_rows, :w] = carry[0]
                    g_buf[i * n_rows:(i + 1) * n_rows, w:] = carry[1]
            return carry

        def first_pair(p, carry):
            project_dx(last - 2 * p - 1, st_b)
            carry = back_steps(st_a, carry)
            project_dx(jnp.maximum(last - 2 * p - 2, 0), st_a)
            return back_steps(st_b, carry)

        project_dx(last, st_a)
        sr, si = lax.fori_loop(0, n_chunks // 2, first_pair, (zero, zero))
        gr0, gi0 = _segment_carry(sr, si, ar, ai, n_rows, True)

        def post(g_buf, ch):
            g = g_buf[...]
            xp = xs_all[pl.ds(start(ch), rows_c), :]
            da_ref[:, :w] += jnp.sum(g[:, :w] * xp[:, :w] + g[:, w:] * xp[:, w:], axis=0, keepdims=True)
            da_ref[:, w:] += jnp.sum(g[:, w:] * xp[:, :w] - g[:, :w] * xp[:, w:], axis=0, keepdims=True)
            gb = g.astype(MXU_DTYPE)
            du_ref[pl.ds(start(ch), rows_c), :] = (lax.dot_general(gb, bb_ref[...], _NT, preferred_element_type=F32)
                                                   + dy_s[pl.ds(start(ch), rows_c), :] * dk_ref[...])
            dbb_ref[...] += lax.dot_general(u_ref[pl.ds(start(ch), rows_c), :].astype(MXU_DTYPE), gb, _TN,
                                            preferred_element_type=F32)

        def second_pair(p, carry):
            c1 = last - 2 * p
            project_dx(c1 - 1, st_b)
            post(buf_b, jnp.minimum(c1 + 1, last))
            carry = back_steps(st_a, carry, buf_a)
            project_dx(jnp.maximum(c1 - 2, 0), st_a)
            post(buf_a, c1)
            return back_steps(st_b, carry, buf_b)

        project_dx(last, st_a)
        buf_b[...] = jnp.zeros_like(buf_b)
        lax.fori_loop(0, n_chunks // 2, second_pair, (gr0, gi0))
        post(buf_b, 0)

    col = lambda width: pl.BlockSpec((t, width), lambda c, j: (0, c))
    wgt = pl.BlockSpec((SCAN_CH, 2 * w), lambda c, j: (c, 0))
    row = pl.BlockSpec((1, 2 * w), lambda c, j: (0, c))
    chan = pl.BlockSpec((1, SCAN_CH), lambda c, j: (0, c))
    return _grid_call(
        body, "ssm_bwd", (SCAN_NBLK, 1), [u_perm, dyg, ytot, dskip, a_cat, bbc, cc, ein],
        [col(SCAN_CH), col(SCAN_CH), col(SCAN_CH), chan, row, wgt, wgt,
         pl.BlockSpec((n_rows, 2 * w), lambda c, j: (0, c))],
        [col(SCAN_CH), chan, row, wgt, wgt],
        [jax.ShapeDtypeStruct((t, SSM_W), F32), jax.ShapeDtypeStruct((1, SSM_W), F32),
         jax.ShapeDtypeStruct((1, 2 * N_STATE), F32), jax.ShapeDtypeStruct((SSM_W, 2 * w), F32),
         jax.ShapeDtypeStruct((SSM_W, 2 * w), F32)],
        56 * 1024 * 1024, comm,
        scratch=[pltpu.VMEM((t + n_rows, 2 * w), F32), pltpu.VMEM((t, SCAN_CH), F32)]
        + [pltpu.VMEM((rows_c, 2 * w), F32)] * 4)


def _to_scan_rows(a, n_samples):
    c = a.shape[1]
    return a.reshape(n_samples, SCAN_SEG_PER_SAMPLE, SCAN_LEN, c).transpose(2, 0, 1, 3).reshape(-1, c)


def _from_scan_rows(a, n_samples):
    c = a.shape[1]
    return a.reshape(SCAN_LEN, n_samples, SCAN_SEG_PER_SAMPLE, c).transpose(1, 2, 0, 3).reshape(-1, c)


def _row_spec(tm, width):
    return pl.BlockSpec((tm, width), lambda i, j: (i, 0))


def _whole(arr):
    return pl.BlockSpec(arr.shape, lambda i, j: (0,) * arr.ndim)


def _proj_rope(x, g, w_in_t, tabs, comm=None):
    t = x.shape[0]
    tm = 256

    def body(x_ref, g_ref, w_ref, tc_ref, tlo_ref, thi_ref, h_ref, u_ref, gate_ref, *rest):
        qkv_refs, stage = rest[:9], rest[9]
        xv = x_ref[...]
        r = lax.rsqrt(jnp.mean(xv * xv, axis=-1, keepdims=True) + RMS_EPS)
        h = ((xv * r) * g_ref[...]).astype(BF16)
        h_ref[...] = h
        p = lax.dot_general(h.astype(MXU_DTYPE), w_ref[...], _NT, preferred_element_type=F32)
        u_ref[...] = p[:, QKV_W:QKV_W + SSM_W]
        gate_ref[...] = _sigmoid(p[:, QKV_W + SSM_W:])
        tc, tlo, thi = tc_ref[...], tlo_ref[...], thi_ref[...]
        n_ch = QKV_W // LANES
        for ch in range(n_ch):
            piece = p[:, _lane_chunk(ch)]
            stage[ch] = _rope_apply(piece, tc, tlo, thi) if ch < 2 * n_ch // 3 else piece
        halves = GROUP_W // LANES
        for grp, d in enumerate(DILATIONS):
            for which in range(3):
                out = qkv_refs[3 * grp + which]
                for res in range(d):
                    for half in range(halves):
                        ch = which * (n_ch // 3) + grp * halves + half
                        out[:, _lane_chunk(res * halves + half)] = _gather_residue(stage, ch, res, d, tm // d).astype(BF16)

    tab = pl.BlockSpec((tm, LANES), lambda i, j: (i % (SEQ // tm), 0))
    widths = [(D_MODEL, BF16), (SSM_W, F32), (2 * D_MODEL, F32)]
    out_specs = [_row_spec(tm, wd) for wd, _ in widths]
    out_shapes = [jax.ShapeDtypeStruct((t, wd), dt) for wd, dt in widths]
    for d in DILATIONS:
        out_specs += [_row_spec(tm // d, d * GROUP_W)] * 3
        out_shapes += [jax.ShapeDtypeStruct((t // d, d * GROUP_W), BF16)] * 3
    return _grid_call(
        body, "proj_rope", (t // tm, 1), [x, g, w_in_t, *tabs],
        [_row_spec(tm, D_MODEL), _whole(g), _whole(w_in_t), tab, tab, tab], out_specs, out_shapes, VMEM_BIG, comm,
        scratch=[pltpu.VMEM((QKV_W // LANES, tm, LANES), F32)])


def _branch_outputs(attn_ref, yg_ref, wao_ref, wglu_ref):
    attn_d = lax.dot_general(attn_ref[...].astype(MXU_DTYPE), wao_ref[...], _NT, preferred_element_type=F32)
    z = lax.dot_general(yg_ref[...].astype(MXU_DTYPE), wglu_ref[...], _NT, preferred_element_type=F32)
    return attn_d, z[:, :D_MODEL], _sigmoid(z[:, D_MODEL:])


def _mix_out_rms(os_, lses, yg, gates, x, w_ao_t, w_glu_t, w_out, g, comm=None):
    t = x.shape[0]
    tm = 256

    def body(o0, o1, o2, l0, l1, l2, yg_ref, gate_ref, x_ref, wao_ref, wglu_ref, wout_ref, g_ref,
             attn_ref, lt_ref, m_ref, x1_ref, h_ref, nat):
        _merge_groups((o0, o1, o2), (l0, l1, l2), attn_ref, lt_ref, nat, tm)
        attn_d, za, sb = _branch_outputs(attn_ref, yg_ref, wao_ref, wglu_ref)
        merged = (gate_ref[:, :D_MODEL] * attn_d + gate_ref[:, D_MODEL:] * (za * sb)).astype(BF16)
        m_ref[...] = merged
        x1 = x_ref[...] + jnp.dot(merged.astype(MXU_DTYPE), wout_ref[...], preferred_element_type=F32)
        x1_ref[...] = x1
        r = lax.rsqrt(jnp.mean(x1 * x1, axis=-1, keepdims=True) + RMS_EPS)
        h_ref[...] = ((x1 * r) * g_ref[...]).astype(BF16)

    dil_specs = [_row_spec(tm // d, d * GROUP_W) for d in DILATIONS] * 2
    return _grid_call(
        body, "mix_out_rms", (t // tm, 1), [*os_, *lses, yg, gates, x, w_ao_t, w_glu_t, w_out, g],
        dil_specs + [_row_spec(tm, SSM_W), _row_spec(tm, 2 * D_MODEL), _row_spec(tm, D_MODEL),
                     _whole(w_ao_t), _whole(w_glu_t), _whole(w_out), _whole(g)],
        [_row_spec(tm, GROUP_W)] * 2 + [_row_spec(tm, D_MODEL)] * 3,
        [jax.ShapeDtypeStruct((t, GROUP_W), F32)] * 2
        + [jax.ShapeDtypeStruct((t, D_MODEL), BF16), jax.ShapeDtypeStruct((t, D_MODEL), F32),
           jax.ShapeDtypeStruct((t, D_MODEL), BF16)], VMEM_BIG, comm, scratch=[pltpu.VMEM((8, tm, LANES), F32)])


def _mix_bwd(dx1b, attn, lse_tot, yg, gates, w_ao_t, w_glu_t, w_out, comm=None):
    t = dx1b.shape[0]
    tm = 256

    def body(dx_ref, attn_ref, lt_ref, yg_ref, gate_ref, wao_ref, wglu_ref, wout_ref, ones_ref,
             dad_ref, dz_ref, dg_ref, da_ref, dyg_ref, rd_ref, *rest):
        dm = lax.dot_general(dx_ref[...], wout_ref[...], _NT, preferred_element_type=F32)
        attn_d, za, sb = _branch_outputs(attn_ref, yg_ref, wao_ref, wglu_ref)
        g0, g1 = gate_ref[:, :D_MODEL], gate_ref[:, D_MODEL:]
        dad = (dm * g0).astype(BF16)
        dad_ref[...] = dad
        ds = dm * g1
        dza, dzb = (ds * sb).astype(BF16), (ds * za * sb * (1.0 - sb)).astype(BF16)
        dz_ref[:, :D_MODEL] = dza
        dz_ref[:, D_MODEL:] = dzb
        dg_ref[:, :D_MODEL] = (dm * attn_d * g0 * (1.0 - g0)).astype(BF16)
        dg_ref[:, D_MODEL:] = (dm * (za * sb) * g1 * (1.0 - g1)).astype(BF16)
        da = jnp.dot(dad.astype(MXU_DTYPE), wao_ref[...], preferred_element_type=F32)
        da_ref[...] = da
        dyg_ref[...] = (jnp.dot(dza.astype(MXU_DTYPE), wglu_ref[:D_MODEL, :], preferred_element_type=F32)
                        + jnp.dot(dzb.astype(MXU_DTYPE), wglu_ref[D_MODEL:, :], preferred_element_type=F32))
        _attention_cotangents(da, attn_ref[...], lt_ref[...], ones_ref[...], rd_ref, rest[:6], rest[6], tm)

    widths = [(D_MODEL, BF16), (2 * D_MODEL, BF16), (2 * D_MODEL, BF16), (GROUP_W, F32), (SSM_W, F32), (GROUP_W, F32)]
    out_specs = [_row_spec(tm, wd) for wd, _ in widths]
    out_shapes = [jax.ShapeDtypeStruct((t, wd), dt) for wd, dt in widths]
    for d in DILATIONS[1:]:
        out_specs += [_row_spec(tm // d, d * GROUP_W)] * 3
        out_shapes += [jax.ShapeDtypeStruct((t // d, d * GROUP_W), F32)] * 3
    ones = _head_sum_matrix()
    return _grid_call(
        body, "mix_bwd", (t // tm, 1), [dx1b, attn, lse_tot, yg, gates, w_ao_t, w_glu_t, w_out, ones],
        [_row_spec(tm, D_MODEL), _row_spec(tm, GROUP_W), _row_spec(tm, GROUP_W), _row_spec(tm, SSM_W),
         _row_spec(tm, 2 * D_MODEL), _whole(w_ao_t), _whole(w_glu_t), _whole(w_out), _whole(ones)],
        out_specs, out_shapes, VMEM_BIG, comm, scratch=[pltpu.VMEM((6, tm, LANES), F32)])


FFN_TN = D_FF // 2
MXU_COLS = 256


def _ffn_in_swiglu(h2, w_gate_t, w_up_t, comm=None):
    t = h2.shape[0]
    tm = 512

    def body(h_ref, wg_ref, wu_ref, a_ref, b_ref, f_ref):
        h = h_ref[...].astype(MXU_DTYPE)
        for c0 in range(0, FFN_TN, MXU_COLS):
            sl = slice(c0, min(c0 + MXU_COLS, FFN_TN))
            a = lax.dot_general(h, wg_ref[sl, :], _NT, preferred_element_type=F32)
            b = lax.dot_general(h, wu_ref[sl, :], _NT, preferred_element_type=F32)
            a_ref[:, sl] = a
            b_ref[:, sl] = b
            f_ref[:, sl] = (a * _sigmoid(a) * b).astype(BF16)

    tile = pl.BlockSpec((tm, FFN_TN), lambda j, i: (i, j))
    wspec = pl.BlockSpec((FFN_TN, D_MODEL), lambda j, i: (j, 0))
    return _grid_call(
        body, "ffn_in_swiglu", (D_FF // FFN_TN, t // tm), [h2, w_gate_t, w_up_t],
        [pl.BlockSpec((tm, D_MODEL), lambda j, i: (i, 0)), wspec, wspec],
        [tile] * 3, [jax.ShapeDtypeStruct((t, D_FF), F32)] * 2 + [jax.ShapeDtypeStruct((t, D_FF), BF16)], VMEM_BIG, comm)


def _ffn_down_final(f, w_down, x1, target, g):
    t = x1.shape[0]
    tm = 256

    def body(f_ref, w_ref, x1_ref, t_ref, g_ref, dx_ref, dxb_ref, loss_ref, gg_ref):
        @pl.when(pl.program_id(0) == 0)
        def _():
            loss_ref[...] = jnp.zeros_like(loss_ref)
            gg_ref[...] = jnp.zeros_like(gg_ref)

        xv = x1_ref[...] + jnp.dot(f_ref[...].astype(MXU_DTYPE), w_ref[...], preferred_element_type=F32)
        gv = g_ref[...]
        r = lax.rsqrt(jnp.mean(xv * xv, axis=-1, keepdims=True) + RMS_EPS)
        n = xv * r
        diff = n * gv - t_ref[...]
        per_tok = jnp.mean(diff * diff, axis=-1, keepdims=True)
        loss_ref[...] += 0.5 * jnp.sum(per_tok, axis=0, keepdims=True)
        dy = diff / xv.shape[-1]
        gg_ref[...] += jnp.sum(dy * n, axis=0, keepdims=True)
        dn = dy * gv
        dx = r * (dn - n * jnp.mean(dn * n, axis=-1, keepdims=True))
        dx_ref[...] = dx
        dxb_ref[...] = dx.astype(BF16)

    acc = lambda shp: pl.BlockSpec(shp, lambda i, j: (0, 0))
    return _grid_call(
        body, "ffn_down_final", (t // tm, 1), [f, w_down, x1, target, g],
        [_row_spec(tm, D_FF), _whole(w_down), _row_spec(tm, D_MODEL), _row_spec(tm, D_MODEL), _whole(g)],
        [_row_spec(tm, D_MODEL)] * 2 + [acc((8, LANES)), acc((1, D_MODEL))],
        [jax.ShapeDtypeStruct((t, D_MODEL), F32), jax.ShapeDtypeStruct((t, D_MODEL), BF16),
         jax.ShapeDtypeStruct((8, LANES), F32), jax.ShapeDtypeStruct((1, D_MODEL), F32)], VMEM_BIG, sequential=True)


def _d_f_swiglu_bwd(dx2b, w_down, a, b):
    t = a.shape[0]
    tm = 512

    def body(dx_ref, w_ref, a_ref, b_ref, da_ref, db_ref):
        d = lax.dot_general(dx_ref[...], w_ref[...], _NT, preferred_element_type=F32)
        av, bv = a_ref[...], b_ref[...]
        sg = _sigmoid(av)
        da_ref[...] = (d * bv * sg * (1.0 + av * (1.0 - sg))).astype(BF16)
        db_ref[...] = (d * av * sg).astype(BF16)

    tile = pl.BlockSpec((tm, FFN_TN), lambda j, i: (i, j))
    return _grid_call(
        body, "d_f_swiglu_bwd", (D_FF // FFN_TN, t // tm), [dx2b, w_down, a, b],
        [pl.BlockSpec((tm, D_MODEL), lambda j, i: (i, 0)), pl.BlockSpec((FFN_TN, D_MODEL), lambda j, i: (j, 0)), tile, tile],
        [tile] * 2, [jax.ShapeDtypeStruct((t, D_FF), BF16)] * 2, VMEM_BIG)


def _mm_rms_bwd(operands, weights, x, g, dres, name, comm=None):
    t = x.shape[0]
    tm = 256
    n_op = len(operands)

    def body(*refs):
        a_refs, w_refs = refs[:n_op], refs[n_op:2 * n_op]
        x_ref, g_ref, dres_ref, dx_ref, dxb_ref, gg_ref = refs[2 * n_op:]

        @pl.when(pl.program_id(0) == 0)
        def _():
            gg_ref[...] = jnp.zeros_like(gg_ref)

        dh = None
        for a_ref, w_ref in zip(a_refs, w_refs):
            part = jnp.dot(a_ref[...].astype(MXU_DTYPE), w_ref[...], preferred_element_type=F32)
            dh = part if dh is None else dh + part
        xv = x_ref[...]
        r = lax.rsqrt(jnp.mean(xv * xv, axis=-1, keepdims=True) + RMS_EPS)
        n = xv * r
        gg_ref[...] += jnp.sum(dh * n, axis=0, keepdims=True)
        dn = dh * g_ref[...]
        dx = dres_ref[...] + r * (dn - n * jnp.mean(dn * n, axis=-1, keepdims=True))
        dx_ref[...] = dx
        dxb_ref[...] = dx.astype(BF16)

    d = x.shape[1]
    return _grid_call(
        body, name, (t // tm, 1), [*operands, *weights, x, g, dres],
        [_row_spec(tm, a.shape[1]) for a in operands] + [_whole(wk) for wk in weights]
        + [_row_spec(tm, d), _whole(g), _row_spec(tm, d)],
        [_row_spec(tm, d)] * 2 + [pl.BlockSpec((1, d), lambda i, j: (0, 0))],
        [jax.ShapeDtypeStruct((t, d), F32), jax.ShapeDtypeStruct((t, d), BF16), jax.ShapeDtypeStruct((1, d), F32)],
        VMEM_BIG, comm, sequential=True)


def _flat_small(small):
    perm_b = lambda a: a.reshape(SSM_GROUPS, SSM_STATE, SSM_CH).transpose(2, 0, 1).reshape(SSM_CH, N_STATE)
    perm_c = lambda a: a.reshape(SSM_GROUPS, SSM_CH, SSM_STATE).transpose(1, 0, 2).reshape(SSM_CH, N_STATE)
    return dict(
        g_mix=small["norm_mix_g"].reshape(1, D_MODEL), g_ffn=small["norm_ffn_g"].reshape(1, D_MODEL),
        g_fin=small["norm_final_g"].reshape(1, D_MODEL),
        lr=small["ssm_a_re"].reshape(1, N_STATE), li=small["ssm_a_im"].reshape(1, N_STATE),
        ldt=jnp.repeat(small["ssm_log_dt"].reshape(SSM_GROUPS), SSM_STATE).reshape(1, N_STATE),
        br=perm_b(small["ssm_b_re"]), bi=perm_b(small["ssm_b_im"]),
        cr=perm_c(small["ssm_c_re"]), ci=perm_c(small["ssm_c_im"]), dskip=small["ssm_d"].reshape(1, SSM_W))


AG_HOSTS = {"proj_rope": ("w_glu", "w_attn_out", "w_out", "w_ffn_gate"), "mix_out_rms": ("w_ffn_up",),
            "ffn_in_swiglu": ("w_ffn_down",)}
HALVED = ("w_ffn_gate", "w_ffn_up", "w_in")
A2A_HOSTS = {"d_h2_rms": ("w_ffn_down",), "mix_bwd": ("w_ffn_gate:0", "w_out"), "attn_bwd_g1": ("w_glu",),
             "attn_bwd_g2": ("w_attn_out",), "ssm_bwd": ("w_ffn_gate:1", "w_ffn_up:0", "w_ffn_up:1"),
             "mm_g_in1": ("w_in:0",), "d_h0_rms": ("w_in:1",)}
SMALL_HOST = "pack_dproj"


def _local_step(x, target, w, small, shards=None):
    t = x.shape[0]
    n_samples = t // SEQ
    n_rows = n_samples * SCAN_SEG_PER_SAMPLE
    tabs = _rope_tables()
    w = dict(w)
    fs = _flat_small(small)
    g_mix, g_ffn, g_fin, dskip = fs["g_mix"], fs["g_ffn"], fs["g_fin"], fs["dskip"]
    a_cat, bbc, cc = _ssm_disc(fs["lr"], fs["li"], fs["ldt"], fs["br"], fs["bi"], fs["cr"], fs["ci"])
    big, recv, small_pack = {}, {}, []

    def comm_of(name):
        if shards is None:
            return None
        if name == SMALL_HOST:
            return _ag_comm([(small_pack[0], 0, 0)], [(N_DEV, *small_pack[0].shape)])
        if name in AG_HOSTS:
            names = AG_HOSTS[name]
            return _ag_comm([(shards[n], j, 0) for j, n in enumerate(names)], [(N_DEV, *shards[n].shape) for n in names])
        if name in A2A_HOSTS:
            return _a2a_comm([(big[n].reshape(N_DEV, -1, big[n].shape[1]), 0) for n in A2A_HOSTS[name]])
        return None

    def absorb(name, carried):
        if name == SMALL_HOST:
            recv["small"] = carried[0]
        for n, a3 in zip(AG_HOSTS.get(name, ()), carried):
            w[n] = a3.reshape(-1, a3.shape[2])
        for n, a3 in zip(A2A_HOSTS.get(name, ()), carried):
            recv[n] = a3

    def mm(a, b, mode, name, tm, tn, **kw):
        comm = comm_of(name)
        if comm is None:
            return _mm(a, b, mode, name, tm, tn, **kw)
        out, *carried = _mm(a, b, mode, name, tm, tn, comm=comm, **kw)
        absorb(name, carried)
        return out

    h0, u, gates, *rest = _proj_rope(x, g_mix, w["w_in"], tabs, comm_of("proj_rope"))
    qkv = [rest[3 * g:3 * g + 3] for g in range(3)]
    absorb("proj_rope", rest[9:])
    os_, lses = [], []
    for g in range(3):
        o_g, l_g, carried = _attn_fwd(*qkv[g], g, n_samples, comm_of(f"attn_fwd_g{g}"))
        absorb(f"attn_fwd_g{g}", carried)
        os_.append(o_g)
        lses.append(l_g)
    u_perm = _to_scan_rows(u, n_samples)
    ytot, yg_perm, ein = _ssm_fwd(u_perm, a_cat, bbc, cc, dskip, n_rows)
    yg = _from_scan_rows(yg_perm, n_samples)

    attn, lse_tot, merged, x1, h2, *carried = _mix_out_rms(os_, lses, yg, gates, x, w["w_attn_out"], w["w_glu"], w["w_out"],
                                                           g_ffn, comm_of("mix_out_rms"))
    absorb("mix_out_rms", carried)
    ffn_a, ffn_b, f, *carried = _ffn_in_swiglu(h2, w["w_ffn_gate"], w["w_ffn_up"], comm_of("ffn_in_swiglu"))
    absorb("ffn_in_swiglu", carried)
    dx2, dx2b, loss_blk, g_gfin = _ffn_down_final(f, w["w_ffn_down"], x1, target, g_fin)

    da, db = _d_f_swiglu_bwd(dx2b, w["w_ffn_down"], ffn_a, ffn_b)
    big["w_ffn_down"] = mm(f, dx2b, "tn", "mm_g_down", 256, D_MODEL, out_dtype=BF16)
    half = D_MODEL // 2
    for hf in range(2):
        big[f"w_ffn_gate:{hf}"] = mm(da, h2, "tn", f"mm_g_gate{hf}", 256, half, out_dtype=BF16, cols=(hf * half, half))
        big[f"w_ffn_up:{hf}"] = mm(db, h2, "tn", f"mm_g_up{hf}", 256, half, out_dtype=BF16, cols=(hf * half, half))
    dx1, dx1b, g_gffn, *carried = _mm_rms_bwd([da, db], [w["w_ffn_gate"], w["w_ffn_up"]], x1, g_ffn, dx2, "d_h2_rms",
                                              comm_of("d_h2_rms"))
    absorb("d_h2_rms", carried)

    big["w_out"] = mm(merged, dx1b, "tn", "mm_g_out", 256, D_MODEL, out_dtype=BF16)
    dattn_d, dz, dgpre, dattn, dyg, rowdot, *rest = _mix_bwd(dx1b, attn, lse_tot, yg, gates, w["w_attn_out"], w["w_glu"],
                                                             w["w_out"], comm_of("mix_bwd"))
    cot = [(dattn, lse_tot, rowdot), tuple(rest[:3]), tuple(rest[3:6])]
    absorb("mix_bwd", rest[6:])

    big["w_attn_out"] = mm(dattn_d, attn, "tn", "mm_g_attn_out", 512, GROUP_W, out_dtype=BF16)
    big["w_glu"] = mm(dz, yg, "tn", "mm_g_glu", 512, 512, out_dtype=BF16)
    dqs, dks, dvs = [], [], []
    for g in range(3):
        dq_g, dk_g, dv_g, carried = _attn_bwd(*qkv[g], *cot[g], g, n_samples, comm_of(f"attn_bwd_g{g}"))
        absorb(f"attn_bwd_g{g}", carried)
        dqs.append(dq_g)
        dks.append(dk_g)
        dvs.append(dv_g)

    dyg_perm = _to_scan_rows(dyg, n_samples)
    du_perm, g_dskip, da_cat, dbb_full, dc_full, *carried = _ssm_bwd(u_perm, dyg_perm, ytot, dskip, a_cat, bbc, cc, ein,
                                                                   n_rows, comm_of("ssm_bwd"))
    absorb("ssm_bwd", carried)
    du = _from_scan_rows(du_perm, n_samples)
    g_lr, g_li, g_ldt, g_br, g_bi, g_cr, g_ci = _ssm_param_bwd(
        fs["lr"], fs["li"], fs["ldt"], fs["br"], fs["bi"], da_cat, dbb_full, dc_full)

    small_pack.append(_pack_small(dict(lr=g_lr, li=g_li, ldt=g_ldt, br=g_br, bi=g_bi, cr=g_cr, ci=g_ci, dskip=g_dskip,
                                       g_ffn=g_gffn, g_fin=g_gfin, loss=loss_blk)))

    if shards is None:
        dproj = _pack_dproj(dqs, dks, dvs, du, dgpre, tabs)
    else:
        dproj, *carried = _pack_dproj(dqs, dks, dvs, du, dgpre, tabs, comm_of(SMALL_HOST))
        absorb(SMALL_HOST, carried)
    for hf in range(2):
        big[f"w_in:{hf}"] = mm(dproj, h0, "tn", f"mm_g_in{hf}", 256, half, out_dtype=BF16, cols=(hf * half, half))
    grad_x, _, g_gmix, *carried = _mm_rms_bwd([dproj], [w["w_in"]], x, g_mix, dx1, "d_h0_rms", comm_of("d_h0_rms"))
    absorb("d_h0_rms", carried)
    return grad_x, (big if shards is None else recv), small_pack[0], g_gmix


_MESH = pl.DeviceIdType.MESH


def _all_gather(block, name):
    rows, lanes = block.shape

    def body(x_ref, out_ref, send_sems, recv_sems, local_sem):
        x, y, c = lax.axis_index("x"), lax.axis_index("y"), lax.axis_index("c")
        me, sibling = (x, y, c), (x, y, 1 - c)
        chips = [(1 - x, y), (x, 1 - y), (1 - x, 1 - y)]

        def slot(px, py, pc):
            return out_ref.at[4 * px + 2 * py + pc]

        def copy(k, blk, to, src=None):
            return pltpu.make_async_remote_copy(
                src_ref=slot(*blk) if src is None else src, dst_ref=slot(*blk), send_sem=send_sems.at[k],
                recv_sem=recv_sems.at[k], device_id=to, device_id_type=_MESH)

        mine = pltpu.make_async_copy(x_ref, slot(*me), local_sem)
        mine.start()
        first = [copy(0, me, sibling, src=x_ref)]
        first += [copy(1 + j, me, (*chip, c), src=x_ref) for j, chip in enumerate(chips)]
        for cp in first:
            cp.start()
        passed = [copy(4 + j, (*chip, c), sibling) for j, chip in enumerate(chips)]
        for j, chip in enumerate(chips):
            copy(1 + j, (*chip, c), me).wait_recv()
            passed[j].start()
        copy(0, sibling, me).wait_recv()
        for j, chip in enumerate(chips):
            copy(4 + j, (*chip, 1 - c), me).wait_recv()
        for cp in first + passed:
            cp.wait_send()
        mine.wait()

    return _pallas_call(
        body, name=name, out_shape=jax.ShapeDtypeStruct((N_DEV, rows, lanes), block.dtype),
        in_specs=[pl.BlockSpec(memory_space=pl.ANY)], out_specs=pl.BlockSpec(memory_space=pl.ANY),
        scratch_shapes=[pltpu.SemaphoreType.DMA((7,)), pltpu.SemaphoreType.DMA((7,)), pltpu.SemaphoreType.DMA],
    )(block)


def _ag_comm(items, bufs):
    def plan(in_refs, out_refs, send_sems, recv_sems, local_sems):
        x, y, c = lax.axis_index("x"), lax.axis_index("y"), lax.axis_index("c")
        me, sibling = (x, y, c), (x, y, 1 - c)
        chips = [(1 - x, y), (x, 1 - y), (1 - x, 1 - y)]
        plans = []
        for t, (_, buf, slot0) in enumerate(items):
            x_ref, out_ref = in_refs[t], out_refs[buf]

            def slot(px, py, pc, out_ref=out_ref, slot0=slot0):
                return out_ref.at[slot0 + 4 * px + 2 * py + pc]

            def copy(k, blk, to, src=None, t=t, slot=slot):
                return pltpu.make_async_remote_copy(
                    src_ref=slot(*blk) if src is None else src, dst_ref=slot(*blk), send_sem=send_sems.at[7 * t + k],
                    recv_sem=recv_sems.at[7 * t + k], device_id=to, device_id_type=_MESH)

            plans.append(dict(
                mine=pltpu.make_async_copy(x_ref, slot(*me), local_sems.at[t]),
                first=[copy(0, me, sibling, src=x_ref)] + [copy(1 + j, me, (*chip, c), src=x_ref)
                                                           for j, chip in enumerate(chips)],
                passed=[copy(4 + j, (*chip, c), sibling) for j, chip in enumerate(chips)],
                from_ici=[copy(1 + j, (*chip, c), me) for j, chip in enumerate(chips)],
                from_sibling=[copy(0, sibling, me)] + [copy(4 + j, (*chip, 1 - c), me) for j, chip in enumerate(chips)]))
        return plans

    def start(*refs):
        for p in plan(*refs):
            p["mine"].start()
            for cp in p["first"]:
                cp.start()

    def finish(*refs):
        plans = plan(*refs)
        for p in plans:
            for arrived, onward in zip(p["from_ici"], p["passed"]):
                arrived.wait_recv()
                onward.start()
        for p in plans:
            for arrived in p["from_sibling"]:
                arrived.wait_recv()
            for cp in p["first"] + p["passed"]:
                cp.wait_send()
            p["mine"].wait()

    dtype_of = {buf: shard.dtype for shard, buf, _ in items}
    out_shapes = [jax.ShapeDtypeStruct(b, dtype_of[j]) for j, b in enumerate(bufs)]
    return _Comm([it[0] for it in items], out_shapes, 7 * len(items), len(items), start, finish)


def _a2a_comm(items):
    def plan(in_refs, out_refs, send_sems, recv_sems, local_sems):
        x, y, c = lax.axis_index("x"), lax.axis_index("y"), lax.axis_index("c")
        my = 4 * x + 2 * y + c
        copies, locals_ = [], []
        for t, (_, slot0) in enumerate(items):
            s_ref, r_ref = in_refs[t], out_refs[t]
            locals_.append(pltpu.make_async_copy(s_ref.at[slot0 + my], r_ref.at[my], local_sems.at[t]))
            for kk in range(1, N_DEV):
                px = 1 - x if kk & 4 else x
                py = 1 - y if kk & 2 else y
                pc = 1 - c if kk & 1 else c
                copies.append(pltpu.make_async_remote_copy(
                    src_ref=s_ref.at[slot0 + 4 * px + 2 * py + pc], dst_ref=r_ref.at[my],
                    send_sem=send_sems.at[7 * t + kk - 1], recv_sem=recv_sems.at[7 * t + kk - 1],
                    device_id=(px, py, pc), device_id_type=_MESH))
        return copies, locals_

    def start(*refs):
        copies, locals_ = plan(*refs)
        for cp in locals_ + copies:
            cp.start()

    def finish(*refs):
        copies, locals_ = plan(*refs)
        for cp in copies + locals_:
            cp.wait()

    out_shapes = [jax.ShapeDtypeStruct((N_DEV,) + it[0].shape[1:], it[0].dtype) for it in items]
    return _Comm([it[0] for it in items], out_shapes, 7 * len(items), len(items), start, finish)


def _adam_math(g, w, m, v):
    m_new = ADAM_B1 * m + (1.0 - ADAM_B1) * g
    v_new = ADAM_B2 * v + (1.0 - ADAM_B2) * jnp.square(g)
    m_hat = m_new / (1.0 - ADAM_B1 ** ADAM_STEP)
    v_hat = v_new / (1.0 - ADAM_B2 ** ADAM_STEP)
    return -ADAM_LR * (m_hat / (jnp.sqrt(v_hat) + ADAM_EPS) + ADAM_WD * w), m_new, v_new


def _sum_partials(parts, name, tm):
    n, rows, _ = parts[0].shape
    widths = [p.shape[2] for p in parts]

    def body(*refs):
        g_ref, off = refs[-1], 0
        for p_ref, wd in zip(refs[:-1], widths):
            g = p_ref[0].astype(F32)
            for s in range(1, n):
                g = g + p_ref[s].astype(F32)
            g_ref[:, off:off + wd] = g
            off += wd

    return _pallas_call(
        body, name=name, grid=(rows // tm,), in_specs=[pl.BlockSpec((n, tm, wd), lambda i: (0, i, 0)) for wd in widths],
        out_specs=pl.BlockSpec((tm, sum(widths)), lambda i: (i, 0)),
        out_shape=jax.ShapeDtypeStruct((rows, sum(widths)), F32),
        compiler_params=pltpu.CompilerParams(dimension_semantics=("parallel",), vmem_limit_bytes=VMEM_MID),
    )(*parts)


def _adam(parts, w, m, v, name, tm):
    n, rows, _ = parts[0].shape
    widths = [p.shape[2] for p in parts]
    cols = sum(widths)

    def body(*refs):
        p_refs, (w_ref, m_ref, v_ref, g_ref, d_ref, nm_ref, nv_ref) = refs[:len(parts)], refs[len(parts):]
        off = 0
        for p_ref, wd in zip(p_refs, widths):
            g = p_ref[0].astype(F32)
            for s in range(1, n):
                g = g + p_ref[s].astype(F32)
            sl = slice(off, off + wd)
            g_ref[:, sl] = g
            d_ref[:, sl], nm_ref[:, sl], nv_ref[:, sl] = _adam_math(g, w_ref[:, sl], m_ref[:, sl], v_ref[:, sl])
            off += wd

    assert rows % tm == 0
    row = pl.BlockSpec((tm, cols), lambda i: (i, 0))
    shp = jax.ShapeDtypeStruct((rows, cols), F32)
    return _pallas_call(
        body, name=name, grid=(rows // tm,),
        in_specs=[pl.BlockSpec((n, tm, wd), lambda i: (0, i, 0)) for wd in widths] + [row, row, row],
        out_specs=[row] * 4, out_shape=[shp] * 4,
        compiler_params=pltpu.CompilerParams(dimension_semantics=("parallel",), vmem_limit_bytes=VMEM_MID),
    )(*parts, w, m, v)


_PK_LR, _PK_LI, _PK_GAINS, _PK_MISC, _PK_BR, _PK_BI, _PK_CR, _PK_CI, _PK_ROWS = 0, 1, 2, 3, 8, 24, 40, 56, 72
_PK_LDT_LANE, _PK_LOSS_LANE = D_MODEL + SSM_W, D_MODEL + SSM_W + LANES


def _pack_small(sg):
    names = ("lr", "li", "g_ffn", "g_fin", "dskip", "ldt", "loss", "br", "bi", "cr", "ci")

    def body(lr, li, gffn, gfin, dskip, ldt, loss, br, bi, cr, ci, o_ref):
        o_ref[...] = jnp.zeros_like(o_ref)
        o_ref[_PK_LR:_PK_LR + 1, :] = lr[...]
        o_ref[_PK_LI:_PK_LI + 1, :] = li[...]
        o_ref[_PK_GAINS:_PK_GAINS + 1, D_MODEL:] = gffn[...]
        o_ref[_PK_MISC:_PK_MISC + 1, :D_MODEL] = gfin[...]
        o_ref[_PK_MISC:_PK_MISC + 1, D_MODEL:D_MODEL + SSM_W] = dskip[...]
        o_ref[_PK_MISC:_PK_MISC + 1, _PK_LDT_LANE:_PK_LDT_LANE + LANES] = ldt[0:1, :]
        o_ref[_PK_MISC:_PK_MISC + 1, _PK_LOSS_LANE:_PK_LOSS_LANE + LANES] = loss[0:1, :]
        o_ref[_PK_BR:_PK_BR + SSM_CH, :] = br[...]
        o_ref[_PK_BI:_PK_BI + SSM_CH, :] = bi[...]
        o_ref[_PK_CR:_PK_CR + SSM_CH, :] = cr[...]
        o_ref[_PK_CI:_PK_CI + SSM_CH, :] = ci[...]

    return _pallas_call(body, name="pack_small", out_shape=jax.ShapeDtypeStruct((_PK_ROWS, N_STATE), F32))(
        *[sg[n] for n in names])


def _unpack_small(s, g_mix):
    unflat_b = unflat_c = lambda a: a.reshape(SSM_CH, SSM_GROUPS, SSM_STATE).transpose(1, 0, 2)[None]
    grads = {
        "norm_mix_g": g_mix, "norm_ffn_g": s[_PK_GAINS, D_MODEL:].reshape(1, D_MODEL),
        "norm_final_g": s[_PK_MISC, :D_MODEL].reshape(1, D_MODEL),
        "ssm_a_re": s[_PK_LR].reshape(1, SSM_GROUPS, SSM_STATE), "ssm_a_im": s[_PK_LI].reshape(1, SSM_GROUPS, SSM_STATE),
        "ssm_log_dt": s[_PK_MISC, _PK_LDT_LANE:_PK_LDT_LANE + SSM_GROUPS].reshape(1, SSM_GROUPS),
        "ssm_d": s[_PK_MISC, D_MODEL:D_MODEL + SSM_W].reshape(1, SSM_GROUPS, SSM_CH),
        "ssm_b_re": unflat_b(s[_PK_BR:_PK_BR + SSM_CH]), "ssm_b_im": unflat_b(s[_PK_BI:_PK_BI + SSM_CH]),
        "ssm_c_re": unflat_c(s[_PK_CR:_PK_CR + SSM_CH]), "ssm_c_im": unflat_c(s[_PK_CI:_PK_CI + SSM_CH]),
    }
    return s[_PK_MISC, _PK_LOSS_LANE], grads


def _stored(name, a):
    if name in ("ssm_b_re", "ssm_b_im"):
        return a.transpose(0, 1, 3, 2)
    return a.reshape(1, -1) if a.ndim == 1 else a


def _unstored(name, a, like):
    return a.transpose(0, 1, 3, 2) if name in ("ssm_b_re", "ssm_b_im") else a.reshape(like.shape)


def _adam_small(grads, wts, moms, vars_):
    n = len(SMALL_WEIGHTS)

    def body(*refs):
        ins, outs = refs[:4 * n], refs[4 * n:]
        for i in range(n):
            g, w, m, v = (ins[j * n + i][...] for j in range(4))
            outs[i][...], outs[n + i][...], outs[2 * n + i][...] = _adam_math(g, w, m, v)

    operands = [grads[k] if d is grads else _stored(k, d[k]) for d in (grads, wts, moms, vars_) for k in SMALL_WEIGHTS]
    shapes = [jax.ShapeDtypeStruct(_stored(k, wts[k]).shape, F32) for k in SMALL_WEIGHTS] * 3
    res = _pallas_call(body, name="adam_small", out_shape=shapes,
                         compiler_params=pltpu.CompilerParams(vmem_limit_bytes=VMEM_BIG))(*operands)
    out = {}
    for j, kind in enumerate(("delta", "new_m", "new_v")):
        for i, k in enumerate(SMALL_WEIGHTS):
            out[kind, k] = _unstored(k, res[j * n + i], wts[k])
    return out


def kernel(x, norm_mix_g, w_in, ssm_a_re, ssm_a_im, ssm_log_dt, ssm_b_re, ssm_b_im, ssm_c_re, ssm_c_im, ssm_d, w_glu, w_attn_out, w_out, norm_ffn_g, w_ffn_gate, w_ffn_up, w_ffn_down, norm_final_g, loss_target, m_norm_mix_g, m_w_in, m_ssm_a_re, m_ssm_a_im, m_ssm_log_dt, m_ssm_b_re, m_ssm_b_im, m_ssm_c_re, m_ssm_c_im, m_ssm_d, m_w_glu, m_w_attn_out, m_w_out, m_norm_ffn_g, m_w_ffn_gate, m_w_ffn_up, m_w_ffn_down, m_norm_final_g, v_norm_mix_g, v_w_in, v_ssm_a_re, v_ssm_a_im, v_ssm_log_dt, v_ssm_b_re, v_ssm_b_im, v_ssm_c_re, v_ssm_c_im, v_ssm_d, v_w_glu, v_w_attn_out, v_w_out, v_norm_ffn_g, v_w_ffn_gate, v_w_ffn_up, v_w_ffn_down, v_norm_final_g):
    args = dict(locals())
    wts = {n: args[n] for n in ALL_WEIGHTS}
    moms = {n: args["m_" + n] for n in ALL_WEIGHTS}
    vars_ = {n: args["v_" + n] for n in ALL_WEIGHTS}
    n_samples = x.shape[0]
    t = n_samples * SEQ

    shards = {n: (wts[n][0] if n in ROW_SHARDED else wts[n][0].T).astype(BF16) for n in BIG_WEIGHTS}
    w_in_t = _all_gather(shards["w_in"], "allgather_w_in").reshape(IN_W, D_MODEL)

    small = {n: wts[n] for n in SMALL_WEIGHTS}
    grad_x, recv, _, g_mix_part = _local_step(x.reshape(t, D_MODEL), loss_target.reshape(t, D_MODEL), {"w_in": w_in_t},
                                              small, shards)

    results = {}
    for n in BIG_WEIGHTS:
        c, k = shards[n].shape
        w2, m2, v2 = wts[n][0], moms[n][0], vars_[n][0]
        if n in ROW_SHARDED:
            res = _adam([recv[n]], w2, m2, v2, "adam_" + n, c // 2)
        elif n in HALVED:
            res = _adam([recv[f"{n}:{hf}"] for hf in range(2)], w2.T, m2.T, v2.T, "adam_" + n, c // 2)
            res = [a.T for a in res]
        else:
            g_t = _sum_partials([recv[n]], "sum_" + n, c // 2)
            res = _adam([g_t.T[None]], w2, m2, v2, "adam_" + n, k // 2)
        for kind, a in zip(("grad", "delta", "new_m", "new_v"), res):
            results[kind, n] = a[None]

    g_mix_all = _all_gather(jnp.pad(g_mix_part, ((0, 7), (0, 0))), "allgather_g_mix")
    g_mix = _sum_partials([g_mix_all], "sum_g_mix", 8)[0:1]
    loss, sgrads = _unpack_small(_sum_partials([recv["small"]], "sum_small", _PK_ROWS), g_mix)
    for n in SMALL_WEIGHTS:
        results["grad", n] = _unstored(n, sgrads[n], wts[n])
    results.update(_adam_small(sgrads, wts, moms, vars_))
    outs = [loss, grad_x.reshape(x.shape)]
    for kind in ("grad", "delta", "new_m", "new_v"):
        outs += [results[kind, n] for n in ALL_WEIGHTS]
    return tuple(outs)
```

```python
import functools
import math

import jax
import jax.numpy as jnp
from jax import lax
from jax.experimental import pallas as pl
from jax.experimental.pallas import tpu as pltpu

F32 = jnp.float32
BF16 = jnp.bfloat16
MXU_DTYPE = jnp.bfloat16

N_DEV = 8
D_MODEL = 1024
SEQ = 2048
HEAD_DIM = 64
HEADS_PER_GROUP = 4
GROUP_W = HEADS_PER_GROUP * HEAD_DIM
DILATIONS = (1, 4, 16)
QKV_W = 3 * len(DILATIONS) * GROUP_W
Q_W = len(DILATIONS) * GROUP_W
ATT_BLOCK = 128
ROPE_DIM = 16
ROPE_THETA = 500000.0
SSM_W = 512
SSM_GROUPS = 32
SSM_CH = 16
SSM_STATE = 64
N_STATE = SSM_GROUPS * SSM_STATE
D_FF = 2816
IN_W = QKV_W + SSM_W + 2 * D_MODEL
RMS_EPS = 1e-6
NEG_INF = -1e30
LANES = 128

SCAN_SEG_PER_SAMPLE = 8
SCAN_LEN = SEQ // SCAN_SEG_PER_SAMPLE
SCAN_WC = 512
SCAN_NBLK = N_STATE // SCAN_WC
SCAN_CH = SSM_W // SCAN_NBLK
SCAN_CHUNK = 32

ADAM_LR = 0.001
ADAM_B1 = 0.9
ADAM_B2 = 0.999
ADAM_EPS = 1e-08
ADAM_WD = 0.01
ADAM_STEP = 10

VMEM_BIG = 48 * 1024 * 1024
VMEM_MID = 32 * 1024 * 1024

BIG_WEIGHTS = ("w_in", "w_glu", "w_attn_out", "w_out", "w_ffn_gate", "w_ffn_up", "w_ffn_down")
ROW_SHARDED = ("w_out", "w_ffn_down")
SMALL_WEIGHTS = ("norm_mix_g", "ssm_a_re", "ssm_a_im", "ssm_log_dt", "ssm_b_re", "ssm_b_im", "ssm_c_re", "ssm_c_im",
                 "ssm_d", "norm_ffn_g", "norm_final_g")
ALL_WEIGHTS = ("norm_mix_g", "w_in", "ssm_a_re", "ssm_a_im", "ssm_log_dt", "ssm_b_re", "ssm_b_im", "ssm_c_re", "ssm_c_im",
               "ssm_d", "w_glu", "w_attn_out", "w_out", "norm_ffn_g", "w_ffn_gate", "w_ffn_up", "w_ffn_down", "norm_final_g")


def _sigmoid(x):
    return 1.0 / (1.0 + jnp.exp(-x))


def _pallas_call(body, *, out_shape, **kw):
    single = not isinstance(out_shape, (list, tuple))
    shapes = [pltpu.HBM(s.shape, s.dtype) for s in ([out_shape] if single else out_shape)]
    call = pl.pallas_call(body, out_shape=shapes[0] if single else shapes, **kw)
    return lambda *operands: call(*[pltpu.with_memory_space_constraint(o, pltpu.HBM) for o in operands])


class _Comm:
    def __init__(self, ins, out_shapes, n_sem, n_local, start, finish):
        self.ins, self.out_shapes, self.n_sem, self.n_local = ins, out_shapes, n_sem, n_local
        self.start, self.finish = start, finish


def _mm(a, b, mode, name, tm, tn, out_dtype=F32, add=None, vmem=VMEM_BIG, comm=None, cols=None):
    if mode == "nn":
        (m, k), (_, n) = a.shape, b.shape
        a_spec = pl.BlockSpec((tm, k), lambda i, j: (i, 0))
        b_spec = pl.BlockSpec((k, tn), lambda i, j: (0, j))
        dims = (((1,), (0,)), ((), ()))
    elif mode == "nt":
        (m, k), (n, _) = a.shape, b.shape
        a_spec = pl.BlockSpec((tm, k), lambda i, j: (i, 0))
        b_spec = pl.BlockSpec((tn, k), lambda i, j: (j, 0))
        dims = (((1,), (1,)), ((), ()))
    else:
        (k, m), (_, n) = a.shape, b.shape
        first, n = cols if cols else (0, n)
        a_spec = pl.BlockSpec((k, tm), lambda i, j: (0, i))
        b_spec = pl.BlockSpec((k, tn), lambda i, j: (0, j + first // tn))
        dims = (((0,), (0,)), ((), ()))
    assert m % tm == 0 and n % tn == 0, (name, m, n, tm, tn)
    o_spec = pl.BlockSpec((tm, tn), lambda i, j: (i, j))
    has_add = add is not None

    def body(*refs):
        a_ref, b_ref, o_ref = refs[0], refs[1], refs[-1]
        acc = lax.dot_general(a_ref[...].astype(MXU_DTYPE), b_ref[...].astype(MXU_DTYPE), dims,
                              preferred_element_type=F32)
        if has_add:
            acc = acc + refs[2][...]
        o_ref[...] = acc.astype(out_dtype)

    ins = [a, b] + ([add] if has_add else [])
    in_specs = [a_spec, b_spec] + ([o_spec] if has_add else [])
    return _grid_call(body, name, (m // tm, n // tn), ins, in_specs, [o_spec],
                      [jax.ShapeDtypeStruct((m, n), out_dtype)], vmem, comm)


def _grid_call(body, name, grid, ins, in_specs, out_specs, out_shapes, vmem, comm=None, sequential=False, scratch=()):
    if comm is None:
        single = len(out_shapes) == 1
        semantics = ("arbitrary", "arbitrary") if sequential else ("parallel", "parallel")
        return _pallas_call(
            body, name=name, grid=grid, in_specs=in_specs, out_specs=out_specs[0] if single else out_specs,
            out_shape=out_shapes[0] if single else out_shapes, scratch_shapes=list(scratch),
            compiler_params=pltpu.CompilerParams(dimension_semantics=semantics, vmem_limit_bytes=vmem),
        )(*ins)
    n_in, n_out, n_cin, n_cout = len(ins), len(out_shapes), len(comm.ins), len(comm.out_shapes)
    n_io = n_in + n_cin + n_out + n_cout

    def carrying(*refs):
        own = refs[:n_in] + refs[n_in + n_cin:n_in + n_cin + n_out] + refs[n_io:len(refs) - 3]
        c_args = (refs[n_in:n_in + n_cin], refs[n_in + n_cin + n_out:n_io], *refs[-3:])

        @pl.when((pl.program_id(0) == 0) & (pl.program_id(1) == 0))
        def _():
            comm.start(*c_args)

        body(*own)

        @pl.when((pl.program_id(0) == grid[0] - 1) & (pl.program_id(1) == grid[1] - 1))
        def _():
            comm.finish(*c_args)

    hbm = pl.BlockSpec(memory_space=pl.ANY)
    return _pallas_call(
        carrying, name=name, grid=grid, in_specs=list(in_specs) + [hbm] * n_cin,
        out_specs=list(out_specs) + [hbm] * n_cout, out_shape=list(out_shapes) + list(comm.out_shapes),
        scratch_shapes=list(scratch) + [pltpu.SemaphoreType.DMA((comm.n_sem,)), pltpu.SemaphoreType.DMA((comm.n_sem,)),
                                        pltpu.SemaphoreType.DMA((comm.n_local,))],
        compiler_params=pltpu.CompilerParams(dimension_semantics=("arbitrary", "arbitrary"), vmem_limit_bytes=vmem),
    )(*ins, *comm.ins)


def _rows(body, name, n_rows, tm, ins, outs, vmem=VMEM_MID, scratch=()):
    assert n_rows % tm == 0
    arrays, in_specs = [], []
    for kind, arr in ins:
        arrays.append(arr)
        if kind == "row":
            assert n_rows % arr.shape[0] == 0, (name, arr.shape)
            in_specs.append(pl.BlockSpec((tm * arr.shape[0] // n_rows, arr.shape[1]), lambda i: (i, 0)))
        elif kind == "tab":
            nblk = arr.shape[0] // tm
            in_specs.append(pl.BlockSpec((tm, arr.shape[1]), lambda i, nblk=nblk: (i % nblk, 0)))
        else:
            in_specs.append(pl.BlockSpec(arr.shape, lambda i, nd=arr.ndim: (0,) * nd))
    out_specs, out_shape = [], []
    for kind, shp, dt in outs:
        if kind == "row":
            out_specs.append(pl.BlockSpec((tm, shp), lambda i: (i, 0)))
            out_shape.append(jax.ShapeDtypeStruct((n_rows, shp), dt))
        elif kind == "dil":
            d, wd = shp
            out_specs.append(pl.BlockSpec((tm // d, d * wd), lambda i: (i, 0)))
            out_shape.append(jax.ShapeDtypeStruct((n_rows // d, d * wd), dt))
        else:
            out_specs.append(pl.BlockSpec(shp, lambda i, nd=len(shp): (0,) * nd))
            out_shape.append(jax.ShapeDtypeStruct(shp, dt))
    res = _pallas_call(
        body, name=name, grid=(n_rows // tm,), in_specs=in_specs, out_specs=out_specs, out_shape=out_shape,
        scratch_shapes=list(scratch),
        compiler_params=pltpu.CompilerParams(dimension_semantics=("arbitrary",), vmem_limit_bytes=vmem),
    )(*arrays)
    return res


def _gather_residue(stage, ch, r, d, n):
    return stage[ch, pl.ds(r, n, stride=d), :] if d > 1 else stage[ch]


def _scatter_residue(stage, ch, r, d, n, val):
    if d > 1:
        stage[ch, pl.ds(r, n, stride=d), :] = val
    else:
        stage[ch] = val


def _lane_chunk(ch):
    return slice(ch * LANES, (ch + 1) * LANES)


def _rope_tables():
    half = ROPE_DIM // 2
    inv = jnp.power(jnp.float32(ROPE_THETA), -jnp.arange(half, dtype=F32) * 2.0 / ROPE_DIM)
    ang = jnp.arange(SEQ, dtype=F32)[:, None] * inv[None, :]
    lane = jnp.arange(LANES) % HEAD_DIM
    cosl = jnp.cos(ang)[:, lane % half]
    sinl = jnp.sin(ang)[:, lane % half]
    tab_c = jnp.where(lane < ROPE_DIM, cosl, 1.0)
    tab_lo = jnp.where(lane < half, -sinl, 0.0)
    tab_hi = jnp.where((lane >= half) & (lane < ROPE_DIM), sinl, 0.0)
    return tab_c.astype(F32), tab_lo.astype(F32), tab_hi.astype(F32)


def _rope_apply(t, tc, tlo, thi):
    half = ROPE_DIM // 2
    return t * tc + pltpu.roll(t, LANES - half, 1) * tlo + pltpu.roll(t, half, 1) * thi


def _rope_transpose(dt, tc, tlo, thi):
    half = ROPE_DIM // 2
    return dt * tc + pltpu.roll(dt * tlo, half, 1) + pltpu.roll(dt * thi, LANES - half, 1)


def _pack_dproj(dqs, dks, dvs, du, dgpre, tabs):
    tm = 256

    def body(*refs):
        dq_refs, dk_refs, dv_refs = refs[0:3], refs[3:6], refs[6:9]
        du_ref, dg_ref, tc_ref, tlo_ref, thi_ref, o_ref, stage = refs[9:16]
        n_ch = QKV_W // LANES
        halves = GROUP_W // LANES
        for grp, d in enumerate(DILATIONS):
            for which, src in enumerate((dq_refs[grp], dk_refs[grp], dv_refs[grp])):
                for res in range(d):
                    for half in range(halves):
                        _scatter_residue(stage, which * (n_ch // 3) + grp * halves + half, res, d, tm // d,
                                         src[:, _lane_chunk(res * halves + half)])
        tc, tlo, thi = tc_ref[...], tlo_ref[...], thi_ref[...]
        for ch in range(n_ch):
            piece = stage[ch]
            o_ref[:, _lane_chunk(ch)] = (_rope_transpose(piece, tc, tlo, thi) if ch < 2 * n_ch // 3 else piece).astype(BF16)
        o_ref[:, QKV_W:QKV_W + SSM_W] = du_ref[...].astype(BF16)
        o_ref[:, QKV_W + SSM_W:] = dg_ref[...].astype(BF16)

    t = du.shape[0]
    ins = [("row", a) for a in (*dqs, *dks, *dvs, du, dgpre)] + [("tab", tb) for tb in tabs]
    return _rows(body, "pack_dproj", t, tm, ins, [("row", IN_W, BF16)],
                 scratch=[pltpu.VMEM((QKV_W // LANES, tm, LANES), F32)])[0]


def _merge_groups(o_refs, l_refs, a_ref, lt_ref, nat, tm):
    halves = GROUP_W // LANES
    for grp, d in enumerate(DILATIONS[1:], start=1):
        for j, src in enumerate((o_refs[grp], l_refs[grp])):
            for res in range(d):
                for half in range(halves):
                    _scatter_residue(nat, (grp - 1) * 4 + j * 2 + half, res, d, tm // d,
                                     src[:, _lane_chunk(res * halves + half)])
    for half in range(halves):
        sl = _lane_chunk(half)
        la, lb, lc = l_refs[0][:, sl], nat[2 + half], nat[6 + half]
        m = jnp.maximum(jnp.maximum(la, lb), lc)
        ea, eb, ec = jnp.exp(la - m), jnp.exp(lb - m), jnp.exp(lc - m)
        ssum = ea + eb + ec
        a_ref[:, sl] = (ea / ssum) * o_refs[0][:, sl] + (eb / ssum) * nat[half] + (ec / ssum) * nat[4 + half]
        lt_ref[:, sl] = m + jnp.log(ssum)


def _head_sum_matrix():
    r = jnp.arange(GROUP_W) // HEAD_DIM
    return (r[:, None] == r[None, :]).astype(F32)


def _attention_cotangents(da, attn, lt, ones, rd_ref, dil, stage, tm):
    halves = GROUP_W // LANES
    rd = jnp.dot(da * attn, ones, preferred_element_type=F32, precision=lax.Precision.HIGHEST)
    rd_ref[...] = rd
    for half in range(halves):
        for j, val in enumerate((da, lt, rd)):
            stage[2 * j + half] = val[:, _lane_chunk(half)]
    for grp, d in enumerate(DILATIONS[1:], start=1):
        for j in range(3):
            for res in range(d):
                for half in range(halves):
                    dil[3 * (grp - 1) + j][:, _lane_chunk(res * halves + half)] = _gather_residue(
                        stage, 2 * j + half, res, d, tm // d)


_GELU_C = math.sqrt(2.0 / math.pi)


def _head_masks():
    lane = lax.broadcasted_iota(jnp.int32, (1, GROUP_W), 1)
    return [(lane // HEAD_DIM) == h for h in range(HEADS_PER_GROUP)]


def _stack_heads(blk, masks, fill=0.0):
    return jnp.concatenate([jnp.where(mk, blk, jnp.full_like(blk, fill)) for mk in masks], axis=0)


def _unstack_heads(stacked, masks):
    rows = stacked.shape[0] // len(masks)
    out = stacked[:rows]
    for h in range(1, len(masks)):
        out = jnp.where(masks[h], stacked[h * rows:(h + 1) * rows], out)
    return out


def _band_mask(first):
    nk = ATT_BLOCK if first else 2 * ATT_BLOCK
    qi = lax.broadcasted_iota(jnp.int32, (ATT_BLOCK, nk), 0)
    ki = lax.broadcasted_iota(jnp.int32, (ATT_BLOCK, nk), 1)
    dist = qi - ki + (0 if first else ATT_BLOCK)
    return (dist >= 0) & (dist <= ATT_BLOCK)


_NT = (((1,), (1,)), ((), ()))
_TN = (((0,), (0,)), ((), ()))


def _residues_per_step(d):
    return 4 if d >= 16 else 1


def _attn_fwd(q, k, v, group, n_samples, comm=None):
    d = DILATIONS[group]
    length = SEQ // d
    nb = length // ATT_BLOCK

    rps = _residues_per_step(d)

    def body(q_ref, k_ref, v_ref, o_ref, l_ref):
        for rl in range(rps):
            residue(q_ref, k_ref, v_ref, o_ref, l_ref, slice(rl * GROUP_W, (rl + 1) * GROUP_W))

    def residue(q_ref, k_ref, v_ref, o_ref, l_ref, cols):
        masks = _head_masks()

        def block(qs, ks, first):
            nk = ATT_BLOCK if first else 2 * ATT_BLOCK
            qb = q_ref[0, pl.ds(qs, ATT_BLOCK), cols]
            kc = k_ref[0, pl.ds(ks, nk), cols]
            vc = v_ref[0, pl.ds(ks, nk), cols]
            q4 = _stack_heads(qb, masks)
            valid = jnp.tile(_band_mask(first), (HEADS_PER_GROUP, 1))
            s = lax.dot_general(q4, kc, _NT, preferred_element_type=F32) * (HEAD_DIM ** -0.5)
            s = jnp.where(valid, s, NEG_INF)
            m = jnp.max(s, axis=-1, keepdims=True)
            p = jnp.exp(s - m)
            l = jnp.sum(p, axis=-1, keepdims=True)
            o4 = jnp.dot(p.astype(MXU_DTYPE), vc, preferred_element_type=F32) / l
            lse4 = jnp.broadcast_to(m + jnp.log(l), o4.shape)
            o_ref[0, pl.ds(qs, ATT_BLOCK), cols] = _unstack_heads(o4, masks)
            l_ref[0, pl.ds(qs, ATT_BLOCK), cols] = _unstack_heads(lse4, masks)

        block(0, 0, True)
        if nb > 1:
            def loop(n, carry):
                block(pl.multiple_of(n * ATT_BLOCK, ATT_BLOCK), pl.multiple_of((n - 1) * ATT_BLOCK, ATT_BLOCK), False)
                return carry

            lax.fori_loop(1, nb, loop, 0)

    per_sample = lambda a: a.reshape(n_samples, length, d * GROUP_W)
    spec = pl.BlockSpec((1, length, rps * GROUP_W), lambda b, r: (b, 0, r))
    shp = jax.ShapeDtypeStruct((n_samples, length, d * GROUP_W), F32)
    o, lse, *carried = _grid_call(body, f"attn_fwd_g{group}", (n_samples, d // rps), [per_sample(a) for a in (q, k, v)],
                                  [spec] * 3, [spec] * 2, [shp, shp], VMEM_MID, comm)
    flat = lambda a: a.reshape(n_samples * length, d * GROUP_W)
    return flat(o), flat(lse), carried


def _attn_bwd(q, k, v, dattn, lse_tot, rowdot, group, n_samples, comm=None):
    d = DILATIONS[group]
    length = SEQ // d
    nb = length // ATT_BLOCK

    rps = _residues_per_step(d)

    def body(q_ref, k_ref, v_ref, da_ref, lt_ref, rd_ref, dq_ref, dk_ref, dv_ref):
        dk_ref[...] = jnp.zeros_like(dk_ref)
        dv_ref[...] = jnp.zeros_like(dv_ref)
        for rl in range(rps):
            residue(q_ref, k_ref, v_ref, da_ref, lt_ref, rd_ref, dq_ref, dk_ref, dv_ref,
                    slice(rl * GROUP_W, (rl + 1) * GROUP_W))

    def residue(q_ref, k_ref, v_ref, da_ref, lt_ref, rd_ref, dq_ref, dk_ref, dv_ref, cols):
        masks = _head_masks()

        def block(qs, ks, first):
            nk = ATT_BLOCK if first else 2 * ATT_BLOCK
            qb = q_ref[0, pl.ds(qs, ATT_BLOCK), cols]
            kc = k_ref[0, pl.ds(ks, nk), cols]
            vc = v_ref[0, pl.ds(ks, nk), cols]
            da = da_ref[0, pl.ds(qs, ATT_BLOCK), cols]
            lt = lt_ref[0, pl.ds(qs, ATT_BLOCK), cols]
            rd = rd_ref[0, pl.ds(qs, ATT_BLOCK), cols]
            q4 = _stack_heads(qb, masks)
            da4 = _stack_heads(da, masks).astype(MXU_DTYPE)
            lt4 = jnp.max(_stack_heads(lt, masks, -jnp.inf), axis=-1, keepdims=True)
            rd4 = jnp.max(_stack_heads(rd, masks, -jnp.inf), axis=-1, keepdims=True)
            valid = jnp.tile(_band_mask(first), (HEADS_PER_GROUP, 1))
            s = lax.dot_general(q4, kc, _NT, preferred_element_type=F32) * (HEAD_DIM ** -0.5)
            s = jnp.where(valid, s, NEG_INF)
            p = jnp.exp(s - lt4)
            dp = lax.dot_general(da4, vc, _NT, preferred_element_type=F32)
            ds = (p * (dp - rd4) * (HEAD_DIM ** -0.5)).astype(MXU_DTYPE)
            dq_ref[0, pl.ds(qs, ATT_BLOCK), cols] = _unstack_heads(jnp.dot(ds, kc, preferred_element_type=F32), masks)
            dk_ref[0, pl.ds(ks, nk), cols] += lax.dot_general(ds, q4, _TN, preferred_element_type=F32)
            dv_ref[0, pl.ds(ks, nk), cols] += lax.dot_general(p.astype(MXU_DTYPE), da4, _TN, preferred_element_type=F32)

        block(0, 0, True)
        if nb > 1:
            def loop(n, carry):
                block(pl.multiple_of(n * ATT_BLOCK, ATT_BLOCK), pl.multiple_of((n - 1) * ATT_BLOCK, ATT_BLOCK), False)
                return carry

            lax.fori_loop(1, nb, loop, 0)

    per_sample = lambda a: a.reshape(n_samples, length, d * GROUP_W)
    spec = pl.BlockSpec((1, length, rps * GROUP_W), lambda b, r: (b, 0, r))
    shp = jax.ShapeDtypeStruct((n_samples, length, d * GROUP_W), F32)
    dq, dk, dv, *carried = _grid_call(
        body, f"attn_bwd_g{group}", (n_samples, d // rps), [per_sample(a) for a in (q, k, v, dattn, lse_tot, rowdot)],
        [spec] * 6, [spec] * 3, [shp, shp, shp], VMEM_MID, comm)
    flat = lambda a: a.reshape(n_samples * length, d * GROUP_W)
    return flat(dq), flat(dk), flat(dv), carried


def _disc(lr, li, ldt, br, bi):
    dt = jnp.exp(ldt)
    mag = jnp.exp(lr * dt)
    ab_re, ab_im = mag * jnp.cos(li * dt), mag * jnp.sin(li * dt)
    den = lr * lr + li * li
    nr, ni = ab_re - 1.0, ab_im
    f_re = (nr * lr + ni * li) / den
    f_im = (ni * lr - nr * li) / den
    return ab_re, ab_im, f_re * br - f_im * bi, f_re * bi + f_im * br


def _state_mask():
    row_g = lax.broadcasted_iota(jnp.int32, (SCAN_CH, SCAN_WC), 0) // SSM_CH
    col_g = lax.broadcasted_iota(jnp.int32, (SCAN_CH, SCAN_WC), 1) // SSM_STATE
    return row_g == col_g


def _ssm_disc(lr, li, ldt, br, bi, cr, ci):
    w = SCAN_WC

    def body(lr_ref, li_ref, ldt_ref, br_ref, bi_ref, cr_ref, ci_ref, a_ref, bb_ref, c_ref):
        ar, ai, bbr, bbi = _disc(lr_ref[...], li_ref[...], ldt_ref[...], br_ref[...], bi_ref[...])
        crv, civ = cr_ref[...], ci_ref[...]
        mask = _state_mask()
        for cb in range(SCAN_NBLK):
            sl = slice(cb * w, (cb + 1) * w)
            rows = slice(cb * SCAN_CH, (cb + 1) * SCAN_CH)
            dense = lambda comp: jnp.where(mask, jnp.tile(comp[:, sl], (SCAN_CH // SSM_CH, 1)), 0.0)
            a_ref[:, 2 * cb * w:(2 * cb + 1) * w] = ar[:, sl]
            a_ref[:, (2 * cb + 1) * w:(2 * cb + 2) * w] = ai[:, sl]
            bb_ref[rows, :w] = dense(bbr).astype(MXU_DTYPE)
            bb_ref[rows, w:] = dense(bbi).astype(MXU_DTYPE)
            c_ref[rows, :w] = dense(crv).astype(MXU_DTYPE)
            c_ref[rows, w:] = (-dense(civ)).astype(MXU_DTYPE)

    return _pallas_call(
        body, name="ssm_disc",
        out_shape=[jax.ShapeDtypeStruct((1, 2 * N_STATE), F32), jax.ShapeDtypeStruct((SSM_W, 2 * w), MXU_DTYPE),
                   jax.ShapeDtypeStruct((SSM_W, 2 * w), MXU_DTYPE)],
        compiler_params=pltpu.CompilerParams(vmem_limit_bytes=VMEM_MID),
    )(lr, li, ldt, br, bi, cr, ci)


def _group_indicator():
    s = jnp.arange(N_STATE) // SSM_STATE
    return (s[:, None] == jnp.arange(LANES)[None, :]).astype(F32)


def _ssm_param_bwd(lr, li, ldt, br, bi, da_cat, dbb_full, dc_full):
    w = SCAN_WC

    def body(lr_ref, li_ref, ldt_ref, br_ref, bi_ref, da_ref, dbb_ref, dc_ref, ind_ref,
             glr_ref, gli_ref, gldt_ref, gbr_ref, gbi_ref, gcr_ref, gci_ref):
        mask = _state_mask()

        def diag_parts(ref):
            res = ([], [])
            for cb in range(SCAN_NBLK):
                for part in range(2):
                    blk = ref[cb * SCAN_CH:(cb + 1) * SCAN_CH, part * w:(part + 1) * w]
                    res[part].append(jnp.sum(jnp.where(mask, blk, 0.0).reshape(SCAN_CH // SSM_CH, SSM_CH, w), axis=0))
            return jnp.concatenate(res[0], axis=1), jnp.concatenate(res[1], axis=1)

        dar = jnp.concatenate([da_ref[:, 2 * cb * w:(2 * cb + 1) * w] for cb in range(SCAN_NBLK)], axis=1)
        dai = jnp.concatenate([da_ref[:, (2 * cb + 1) * w:(2 * cb + 2) * w] for cb in range(SCAN_NBLK)], axis=1)
        dbbr, dbbi = diag_parts(dbb_ref)
        dcr, dci_neg = diag_parts(dc_ref)
        gcr_ref[...] = dcr
        gci_ref[...] = -dci_neg
        _, vjp = jax.vjp(_disc, lr_ref[...], li_ref[...], ldt_ref[...], br_ref[...], bi_ref[...])
        glr, gli, gldt, gbr, gbi = vjp((dar, dai, dbbr, dbbi))
        glr_ref[...] = glr
        gli_ref[...] = gli
        gldt_ref[...] = jnp.dot(jnp.broadcast_to(gldt, (8, N_STATE)), ind_ref[...], preferred_element_type=F32,
                                precision=lax.Precision.HIGHEST)
        gbr_ref[...] = gbr
        gbi_ref[...] = gbi

    v1 = jax.ShapeDtypeStruct((1, N_STATE), F32)
    v16 = jax.ShapeDtypeStruct((SSM_CH, N_STATE), F32)
    vdt = jax.ShapeDtypeStruct((8, LANES), F32)
    return _pallas_call(
        body, name="ssm_param_bwd", out_shape=[v1, v1, vdt, v16, v16, v16, v16],
        compiler_params=pltpu.CompilerParams(vmem_limit_bytes=VMEM_BIG),
    )(lr, li, ldt, br, bi, da_cat, dbb_full, dc_full, _group_indicator())


def _cmul(ar, ai, br, bi):
    return ar * br - ai * bi, ar * bi + ai * br


def _gelu_tanh(y):
    return jnp.tanh(_GELU_C * (y + 0.044715 * (y * y * y)))


def _segment_carry(er, ei, ar, ai, n_rows, reverse):
    qr, qi = ar, ai
    for _ in range(int(math.log2(SCAN_LEN))):
        qr, qi = _cmul(qr, qi, qr, qi)
    seg = lax.broadcasted_iota(jnp.int32, er.shape, 0) % SCAN_SEG_PER_SAMPLE
    shift = 1
    while shift < SCAN_SEG_PER_SAMPLE:
        keep = (seg < SCAN_SEG_PER_SAMPLE - shift) if reverse else (seg >= shift)
        amount = n_rows - shift if reverse else shift
        sr = jnp.where(keep, pltpu.roll(er, amount, 0), 0.0)
        si = jnp.where(keep, pltpu.roll(ei, amount, 0), 0.0)
        if reverse:
            er, ei = er + qr * sr + qi * si, ei + qr * si - qi * sr
        else:
            er, ei = er + qr * sr - qi * si, ei + qr * si + qi * sr
        qr, qi = _cmul(qr, qi, qr, qi)
        shift *= 2
    keep = (seg < SCAN_SEG_PER_SAMPLE - 1) if reverse else (seg >= 1)
    amount = n_rows - 1 if reverse else 1
    return jnp.where(keep, pltpu.roll(er, amount, 0), 0.0), jnp.where(keep, pltpu.roll(ei, amount, 0), 0.0)


def _ssm_fwd(u_perm, a_cat, bbc, cc, dskip, n_rows):
    t = u_perm.shape[0]
    w = SCAN_WC
    rows_c = SCAN_CHUNK * n_rows
    n_chunks = t // rows_c

    assert n_chunks % 2 == 0

    def body(u_ref, a_ref, bb_ref, c_ref, d_ref, yt_ref, yg_ref, ein_ref, bu_all, st_a, st_b, xs_a, xs_b):
        ar = jnp.broadcast_to(a_ref[:, :w], (n_rows, w))
        ai = jnp.broadcast_to(a_ref[:, w:], (n_rows, w))
        start = lambda ch: pl.multiple_of(ch * rows_c, rows_c)

        def project(ch, stage):
            res = jnp.dot(u_ref[pl.ds(start(ch), rows_c), :].astype(MXU_DTYPE), bb_ref[...], preferred_element_type=F32)
            stage[...] = res
            bu_all[pl.ds(start(ch), rows_c), :] = res

        def steps(src, r0, carry, xs=None):
            for i in range(SCAN_CHUNK):
                blk = src[pl.ds(r0 + i * n_rows, n_rows), :]
                carry = (ar * carry[0] - ai * carry[1] + blk[:, :w], ar * carry[1] + ai * carry[0] + blk[:, w:])
                if xs is not None:
                    xs[i * n_rows:(i + 1) * n_rows, :w] = carry[0]
                    xs[i * n_rows:(i + 1) * n_rows, w:] = carry[1]
            return carry

        def emit(xs, ch):
            y = lax.dot_general(xs[...].astype(MXU_DTYPE), c_ref[...], _NT, preferred_element_type=F32)
            yt = y + d_ref[...] * u_ref[pl.ds(start(ch), rows_c), :]
            yt_ref[pl.ds(start(ch), rows_c), :] = yt
            yg_ref[pl.ds(start(ch), rows_c), :] = (0.5 * yt * (1.0 + _gelu_tanh(yt))).astype(BF16)

        project(0, st_a)

        def pair1(p, carry):
            project(2 * p + 1, st_b)
            carry = steps(st_a, 0, carry)
            project(jnp.minimum(2 * p + 2, n_chunks - 1), st_a)
            return steps(st_b, 0, carry)

        zero = jnp.zeros((n_rows, w), F32)
        er, ei = lax.fori_loop(0, n_chunks // 2, pair1, (zero, zero))
        cr, ci = _segment_carry(er, ei, ar, ai, n_rows, False)
        ein_ref[:, :w] = cr
        ein_ref[:, w:] = ci

        xs_b[...] = jnp.zeros_like(xs_b)

        def pair2(p, carry):
            emit(xs_b, jnp.maximum(2 * p - 1, 0))
            carry = steps(bu_all, start(2 * p), carry, xs_a)
            emit(xs_a, 2 * p)
            return steps(bu_all, start(2 * p + 1), carry, xs_b)

        lax.fori_loop(0, n_chunks // 2, pair2, (cr, ci))
        emit(xs_b, n_chunks - 1)

    col = lambda width: pl.BlockSpec((t, width), lambda c: (0, c))
    wgt = pl.BlockSpec((SCAN_CH, 2 * w), lambda c: (c, 0))
    return _pallas_call(
        body, name="ssm_fwd", grid=(SCAN_NBLK,),
        in_specs=[col(SCAN_CH), pl.BlockSpec((1, 2 * w), lambda c: (0, c)), wgt, wgt,
                  pl.BlockSpec((1, SCAN_CH), lambda c: (0, c))],
        out_specs=[col(SCAN_CH), col(SCAN_CH), pl.BlockSpec((n_rows, 2 * w), lambda c: (0, c))],
        out_shape=[jax.ShapeDtypeStruct((t, SSM_W), F32), jax.ShapeDtypeStruct((t, SSM_W), BF16),
                   jax.ShapeDtypeStruct((n_rows, 2 * N_STATE), F32)],
        scratch_shapes=[pltpu.VMEM((t, 2 * w), F32)] + [pltpu.VMEM((rows_c, 2 * w), F32)] * 4,
        compiler_params=pltpu.CompilerParams(dimension_semantics=("parallel",), vmem_limit_bytes=VMEM_BIG),
    )(u_perm, a_cat, bbc, cc, dskip)


def _ssm_bwd(u_perm, dyg, ytot, dskip, a_cat, bbc, cc, ein, n_rows, comm=None):
    t = u_perm.shape[0]
    w = SCAN_WC
    rows_c = SCAN_CHUNK * n_rows
    n_chunks = t // rows_c

    assert n_chunks % 2 == 0
    last = n_chunks - 1

    def body(u_ref, dyg_ref, yt_ref, dk_ref, a_ref, bb_ref, c_ref, ein_ref, du_ref, gd_ref, da_ref, dbb_ref, dc_ref,
             xs_all, dy_s, st_a, st_b, buf_a, buf_b):
        ar = jnp.broadcast_to(a_ref[:, :w], (n_rows, w))
        ai = jnp.broadcast_to(a_ref[:, w:], (n_rows, w))
        zero = jnp.zeros((n_rows, w), F32)
        start = lambda ch: pl.multiple_of(ch * rows_c, rows_c)
        dbb_ref[...] = jnp.zeros_like(dbb_ref)
        dc_ref[...] = jnp.zeros_like(dc_ref)
        da_ref[...] = jnp.zeros_like(da_ref)

        yt = yt_ref[...]
        th = _gelu_tanh(yt)
        dgelu = 0.5 * (1.0 + th) + 0.5 * yt * (1.0 - th * th) * _GELU_C * (1.0 + 3.0 * 0.044715 * yt * yt)
        dy_all = dyg_ref[...] * dgelu
        dy_s[...] = dy_all
        gd_ref[...] = jnp.sum(dy_all * u_ref[...], axis=0, keepdims=True)
        dy_chunk = lambda ch: dy_s[pl.ds(start(ch), rows_c), :].astype(MXU_DTYPE)

        xs_all[0:n_rows, :] = ein_ref[...]

        def project(ch, stage):
            stage[...] = jnp.dot(u_ref[pl.ds(start(ch), rows_c), :].astype(MXU_DTYPE), bb_ref[...],
                                 preferred_element_type=F32)

        def fwd_steps(stage, ch, carry, xs):
            for i in range(SCAN_CHUNK):
                blk = stage[i * n_rows:(i + 1) * n_rows, :]
                carry = (ar * carry[0] - ai * carry[1] + blk[:, :w], ar * carry[1] + ai * carry[0] + blk[:, w:])
                for half, val in enumerate(carry):
                    xs[i * n_rows:(i + 1) * n_rows, half * w:(half + 1) * w] = val
                    xs_all[pl.ds(start(ch) + (i + 1) * n_rows, n_rows), half * w:(half + 1) * w] = val
            return carry

        def add_dc(xs, ch):
            dc_ref[...] += lax.dot_general(dy_chunk(ch), xs[...].astype(MXU_DTYPE), _TN, preferred_element_type=F32)

        project(0, st_a)

        def fwd_pair(p, carry):
            project(2 * p + 1, st_b)
            carry = fwd_steps(st_a, 2 * p, carry, buf_a)
            add_dc(buf_a, 2 * p)
            project(jnp.minimum(2 * p + 2, last), st_a)
            carry = fwd_steps(st_b, 2 * p + 1, carry, buf_b)
            add_dc(buf_b, 2 * p + 1)
            return carry

        lax.fori_loop(0, n_chunks // 2, fwd_pair, (ein_ref[:, :w], ein_ref[:, w:]))

        def project_dx(ch, stage):
            stage[...] = jnp.dot(dy_chunk(ch), c_ref[...], preferred_element_type=F32)

        def back_steps(stage, carry, g_buf=None):
            for i in reversed(range(SCAN_CHUNK)):
                blk = stage[i * n_rows:(i + 1) * n_rows, :]
                carry = (blk[:, :w] + ar * carry[0] + ai * carry[1], blk[:, w:] + ar * carry[1] - ai * carry[0])
                if g_buf is not None:
                    g_buf[i * n_rows:(i + 1) * n_rows, :w] = carry[0]
                    g_buf[i * n_rows:(i + 1) * n_rows, w:] = carry[1]
            return carry

        def first_pair(p, carry):
            project_dx(last - 2 * p - 1, st_b)
            carry = back_steps(st_a, carry)
            project_dx(jnp.maximum(last - 2 * p - 2, 0), st_a)
            return back_steps(st_b, carry)

        project_dx(last, st_a)
        sr, si = lax.fori_loop(0, n_chunks // 2, first_pair, (zero, zero))
        gr0, gi0 = _segment_carry(sr, si, ar, ai, n_rows, True)

        def post(g_buf, ch):
            g = g_buf[...]
            xp = xs_all[pl.ds(start(ch), rows_c), :]
            da_ref[:, :w] += jnp.sum(g[:, :w] * xp[:, :w] + g[:, w:] * xp[:, w:], axis=0, keepdims=True)
            da_ref[:, w:] += jnp.sum(g[:, w:] * xp[:, :w] - g[:, :w] * xp[:, w:], axis=0, keepdims=True)
            gb = g.astype(MXU_DTYPE)
            du_ref[pl.ds(start(ch), rows_c), :] = (lax.dot_general(gb, bb_ref[...], _NT, preferred_element_type=F32)
                                                   + dy_s[pl.ds(start(ch), rows_c), :] * dk_ref[...])
            dbb_ref[...] += lax.dot_general(u_ref[pl.ds(start(ch), rows_c), :].astype(MXU_DTYPE), gb, _TN,
                                            preferred_element_type=F32)

        def second_pair(p, carry):
            c1 = last - 2 * p
            project_dx(c1 - 1, st_b)
            post(buf_b, jnp.minimum(c1 + 1, last))
            carry = back_steps(st_a, carry, buf_a)
            project_dx(jnp.maximum(c1 - 2, 0), st_a)
            post(buf_a, c1)
            return back_steps(st_b, carry, buf_b)

        project_dx(last, st_a)
        buf_b[...] = jnp.zeros_like(buf_b)
        lax.fori_loop(0, n_chunks // 2, second_pair, (gr0, gi0))
        post(buf_b, 0)

    col = lambda width: pl.BlockSpec((t, width), lambda c, j: (0, c))
    wgt = pl.BlockSpec((SCAN_CH, 2 * w), lambda c, j: (c, 0))
    row = pl.BlockSpec((1, 2 * w), lambda c, j: (0, c))
    chan = pl.BlockSpec((1, SCAN_CH), lambda c, j: (0, c))
    return _grid_call(
        body, "ssm_bwd", (SCAN_NBLK, 1), [u_perm, dyg, ytot, dskip, a_cat, bbc, cc, ein],
        [col(SCAN_CH), col(SCAN_CH), col(SCAN_CH), chan, row, wgt, wgt,
         pl.BlockSpec((n_rows, 2 * w), lambda c, j: (0, c))],
        [col(SCAN_CH), chan, row, wgt, wgt],
        [jax.ShapeDtypeStruct((t, SSM_W), F32), jax.ShapeDtypeStruct((1, SSM_W), F32),
         jax.ShapeDtypeStruct((1, 2 * N_STATE), F32), jax.ShapeDtypeStruct((SSM_W, 2 * w), F32),
         jax.ShapeDtypeStruct((SSM_W, 2 * w), F32)],
        56 * 1024 * 1024, comm,
        scratch=[pltpu.VMEM((t + n_rows, 2 * w), F32), pltpu.VMEM((t, SCAN_CH), F32)]
        + [pltpu.VMEM((rows_c, 2 * w), F32)] * 4)


def _to_scan_rows(a, n_samples):
    c = a.shape[1]
    return a.reshape(n_samples, SCAN_SEG_PER_SAMPLE, SCAN_LEN, c).transpose(2, 0, 1, 3).reshape(-1, c)


def _from_scan_rows(a, n_samples):
    c = a.shape[1]
    return a.reshape(SCAN_LEN, n_samples, SCAN_SEG_PER_SAMPLE, c).transpose(1, 2, 0, 3).reshape(-1, c)


def _row_spec(tm, width):
    return pl.BlockSpec((tm, width), lambda i, j: (i, 0))


def _whole(arr):
    return pl.BlockSpec(arr.shape, lambda i, j: (0,) * arr.ndim)


def _proj_rope(x, g, w_in_t, tabs, comm=None):
    t = x.shape[0]
    tm = 256

    def body(x_ref, g_ref, w_ref, tc_ref, tlo_ref, thi_ref, h_ref, u_ref, gate_ref, *rest):
        qkv_refs, stage = rest[:9], rest[9]
        xv = x_ref[...]
        r = lax.rsqrt(jnp.mean(xv * xv, axis=-1, keepdims=True) + RMS_EPS)
        h = ((xv * r) * g_ref[...]).astype(BF16)
        h_ref[...] = h
        p = lax.dot_general(h.astype(MXU_DTYPE), w_ref[...], _NT, preferred_element_type=F32)
        u_ref[...] = p[:, QKV_W:QKV_W + SSM_W]
        gate_ref[...] = _sigmoid(p[:, QKV_W + SSM_W:])
        tc, tlo, thi = tc_ref[...], tlo_ref[...], thi_ref[...]
        n_ch = QKV_W // LANES
        for ch in range(n_ch):
            piece = p[:, _lane_chunk(ch)]
            stage[ch] = _rope_apply(piece, tc, tlo, thi) if ch < 2 * n_ch // 3 else piece
        halves = GROUP_W // LANES
        for grp, d in enumerate(DILATIONS):
            for which in range(3):
                out = qkv_refs[3 * grp + which]
                for res in range(d):
                    for half in range(halves):
                        ch = which * (n_ch // 3) + grp * halves + half
                        out[:, _lane_chunk(res * halves + half)] = _gather_residue(stage, ch, res, d, tm // d).astype(BF16)

    tab = pl.BlockSpec((tm, LANES), lambda i, j: (i % (SEQ // tm), 0))
    widths = [(D_MODEL, BF16), (SSM_W, F32), (2 * D_MODEL, F32)]
    out_specs = [_row_spec(tm, wd) for wd, _ in widths]
    out_shapes = [jax.ShapeDtypeStruct((t, wd), dt) for wd, dt in widths]
    for d in DILATIONS:
        out_specs += [_row_spec(tm // d, d * GROUP_W)] * 3
        out_shapes += [jax.ShapeDtypeStruct((t // d, d * GROUP_W), BF16)] * 3
    return _grid_call(
        body, "proj_rope", (t // tm, 1), [x, g, w_in_t, *tabs],
        [_row_spec(tm, D_MODEL), _whole(g), _whole(w_in_t), tab, tab, tab], out_specs, out_shapes, VMEM_BIG, comm,
        scratch=[pltpu.VMEM((QKV_W // LANES, tm, LANES), F32)])


def _branch_outputs(attn_ref, yg_ref, wao_ref, wglu_ref):
    attn_d = lax.dot_general(attn_ref[...].astype(MXU_DTYPE), wao_ref[...], _NT, preferred_element_type=F32)
    z = lax.dot_general(yg_ref[...].astype(MXU_DTYPE), wglu_ref[...], _NT, preferred_element_type=F32)
    return attn_d, z[:, :D_MODEL], _sigmoid(z[:, D_MODEL:])


def _mix_out_rms(os_, lses, yg, gates, x, w_ao_t, w_glu_t, w_out, g, comm=None):
    t = x.shape[0]
    tm = 256

    def body(o0, o1, o2, l0, l1, l2, yg_ref, gate_ref, x_ref, wao_ref, wglu_ref, wout_ref, g_ref,
             attn_ref, lt_ref, m_ref, x1_ref, h_ref, nat):
        _merge_groups((o0, o1, o2), (l0, l1, l2), attn_ref, lt_ref, nat, tm)
        attn_d, za, sb = _branch_outputs(attn_ref, yg_ref, wao_ref, wglu_ref)
        merged = (gate_ref[:, :D_MODEL] * attn_d + gate_ref[:, D_MODEL:] * (za * sb)).astype(BF16)
        m_ref[...] = merged
        x1 = x_ref[...] + jnp.dot(merged.astype(MXU_DTYPE), wout_ref[...], preferred_element_type=F32)
        x1_ref[...] = x1
        r = lax.rsqrt(jnp.mean(x1 * x1, axis=-1, keepdims=True) + RMS_EPS)
        h_ref[...] = ((x1 * r) * g_ref[...]).astype(BF16)

    dil_specs = [_row_spec(tm // d, d * GROUP_W) for d in DILATIONS] * 2
    return _grid_call(
        body, "mix_out_rms", (t // tm, 1), [*os_, *lses, yg, gates, x, w_ao_t, w_glu_t, w_out, g],
        dil_specs + [_row_spec(tm, SSM_W), _row_spec(tm, 2 * D_MODEL), _row_spec(tm, D_MODEL),
                     _whole(w_ao_t), _whole(w_glu_t), _whole(w_out), _whole(g)],
        [_row_spec(tm, GROUP_W)] * 2 + [_row_spec(tm, D_MODEL)] * 3,
        [jax.ShapeDtypeStruct((t, GROUP_W), F32)] * 2
        + [jax.ShapeDtypeStruct((t, D_MODEL), BF16), jax.ShapeDtypeStruct((t, D_MODEL), F32),
           jax.ShapeDtypeStruct((t, D_MODEL), BF16)], VMEM_BIG, comm, scratch=[pltpu.VMEM((8, tm, LANES), F32)])


def _mix_bwd(dx1b, attn, lse_tot, yg, gates, w_ao_t, w_glu_t, w_out, comm=None):
    t = dx1b.shape[0]
    tm = 256

    def body(dx_ref, attn_ref, lt_ref, yg_ref, gate_ref, wao_ref, wglu_ref, wout_ref, ones_ref,
             dad_ref, dz_ref, dg_ref, da_ref, dyg_ref, rd_ref, *rest):
        dm = lax.dot_general(dx_ref[...], wout_ref[...], _NT, preferred_element_type=F32)
        attn_d, za, sb = _branch_outputs(attn_ref, yg_ref, wao_ref, wglu_ref)
        g0, g1 = gate_ref[:, :D_MODEL], gate_ref[:, D_MODEL:]
        dad = (dm * g0).astype(BF16)
        dad_ref[...] = dad
        ds = dm * g1
        dza, dzb = (ds * sb).astype(BF16), (ds * za * sb * (1.0 - sb)).astype(BF16)
        dz_ref[:, :D_MODEL] = dza
        dz_ref[:, D_MODEL:] = dzb
        dg_ref[:, :D_MODEL] = (dm * attn_d * g0 * (1.0 - g0)).astype(BF16)
        dg_ref[:, D_MODEL:] = (dm * (za * sb) * g1 * (1.0 - g1)).astype(BF16)
        da = jnp.dot(dad.astype(MXU_DTYPE), wao_ref[...], preferred_element_type=F32)
        da_ref[...] = da
        dyg_ref[...] = (jnp.dot(dza.astype(MXU_DTYPE), wglu_ref[:D_MODEL, :], preferred_element_type=F32)
                        + jnp.dot(dzb.astype(MXU_DTYPE), wglu_ref[D_MODEL:, :], preferred_element_type=F32))
        _attention_cotangents(da, attn_ref[...], lt_ref[...], ones_ref[...], rd_ref, rest[:6], rest[6], tm)

    widths = [(D_MODEL, BF16), (2 * D_MODEL, BF16), (2 * D_MODEL, BF16), (GROUP_W, F32), (SSM_W, F32), (GROUP_W, F32)]
    out_specs = [_row_spec(tm, wd) for wd, _ in widths]
    out_shapes = [jax.ShapeDtypeStruct((t, wd), dt) for wd, dt in widths]
    for d in DILATIONS[1:]:
        out_specs += [_row_spec(tm // d, d * GROUP_W)] * 3
        out_shapes += [jax.ShapeDtypeStruct((t // d, d * GROUP_W), F32)] * 3
    ones = _head_sum_matrix()
    return _grid_call(
        body, "mix_bwd", (t // tm, 1), [dx1b, attn, lse_tot, yg, gates, w_ao_t, w_glu_t, w_out, ones],
        [_row_spec(tm, D_MODEL), _row_spec(tm, GROUP_W), _row_spec(tm, GROUP_W), _row_spec(tm, SSM_W),
         _row_spec(tm, 2 * D_MODEL), _whole(w_ao_t), _whole(w_glu_t), _whole(w_out), _whole(ones)],
        out_specs, out_shapes, VMEM_BIG, comm, scratch=[pltpu.VMEM((6, tm, LANES), F32)])


FFN_TN = D_FF // 2
MXU_COLS = 256


def _ffn_in_swiglu(h2, w_gate_t, w_up_t, comm=None):
    t = h2.shape[0]
    tm = 512

    def body(h_ref, wg_ref, wu_ref, a_ref, b_ref, f_ref):
        h = h_ref[...].astype(MXU_DTYPE)
        for c0 in range(0, FFN_TN, MXU_COLS):
            sl = slice(c0, min(c0 + MXU_COLS, FFN_TN))
            a = lax.dot_general(h, wg_ref[sl, :], _NT, preferred_element_type=F32)
            b = lax.dot_general(h, wu_ref[sl, :], _NT, preferred_element_type=F32)
            a_ref[:, sl] = a
            b_ref[:, sl] = b
            f_ref[:, sl] = (a * _sigmoid(a) * b).astype(BF16)

    tile = pl.BlockSpec((tm, FFN_TN), lambda j, i: (i, j))
    wspec = pl.BlockSpec((FFN_TN, D_MODEL), lambda j, i: (j, 0))
    return _grid_call(
        body, "ffn_in_swiglu", (D_FF // FFN_TN, t // tm), [h2, w_gate_t, w_up_t],
        [pl.BlockSpec((tm, D_MODEL), lambda j, i: (i, 0)), wspec, wspec],
        [tile] * 3, [jax.ShapeDtypeStruct((t, D_FF), F32)] * 2 + [jax.ShapeDtypeStruct((t, D_FF), BF16)], VMEM_BIG, comm)


def _ffn_down_final(f, w_down, x1, target, g):
    t = x1.shape[0]
    tm = 256

    def body(f_ref, w_ref, x1_ref, t_ref, g_ref, dx_ref, dxb_ref, loss_ref, gg_ref):
        @pl.when(pl.program_id(0) == 0)
        def _():
            loss_ref[...] = jnp.zeros_like(loss_ref)
            gg_ref[...] = jnp.zeros_like(gg_ref)

        xv = x1_ref[...] + jnp.dot(f_ref[...].astype(MXU_DTYPE), w_ref[...], preferred_element_type=F32)
        gv = g_ref[...]
        r = lax.rsqrt(jnp.mean(xv * xv, axis=-1, keepdims=True) + RMS_EPS)
        n = xv * r
        diff = n * gv - t_ref[...]
        per_tok = jnp.mean(diff * diff, axis=-1, keepdims=True)
        loss_ref[...] += 0.5 * jnp.sum(per_tok, axis=0, keepdims=True)
        dy = diff / xv.shape[-1]
        gg_ref[...] += jnp.sum(dy * n, axis=0, keepdims=True)
        dn = dy * gv
        dx = r * (dn - n * jnp.mean(dn * n, axis=-1, keepdims=True))
        dx_ref[...] = dx
        dxb_ref[...] = dx.astype(BF16)

    acc = lambda shp: pl.BlockSpec(shp, lambda i, j: (0, 0))
    return _grid_call(
        body, "ffn_down_final", (t // tm, 1), [f, w_down, x1, target, g],
        [_row_spec(tm, D_FF), _whole(w_down), _row_spec(tm, D_MODEL), _row_spec(tm, D_MODEL), _whole(g)],
        [_row_spec(tm, D_MODEL)] * 2 + [acc((8, LANES)), acc((1, D_MODEL))],
        [jax.ShapeDtypeStruct((t, D_MODEL), F32), jax.ShapeDtypeStruct((t, D_MODEL), BF16),
         jax.ShapeDtypeStruct((8, LANES), F32), jax.ShapeDtypeStruct((1, D_MODEL), F32)], VMEM_BIG, sequential=True)


def _d_f_swiglu_bwd(dx2b, w_down, a, b):
    t = a.shape[0]
    tm = 512

    def body(dx_ref, w_ref, a_ref, b_ref, da_ref, db_ref):
        d = lax.dot_general(dx_ref[...], w_ref[...], _NT, preferred_element_type=F32)
        av, bv = a_ref[...], b_ref[...]
        sg = _sigmoid(av)
        da_ref[...] = (d * bv * sg * (1.0 + av * (1.0 - sg))).astype(BF16)
        db_ref[...] = (d * av * sg).astype(BF16)

    tile = pl.BlockSpec((tm, FFN_TN), lambda j, i: (i, j))
    return _grid_call(
        body, "d_f_swiglu_bwd", (D_FF // FFN_TN, t // tm), [dx2b, w_down, a, b],
        [pl.BlockSpec((tm, D_MODEL), lambda j, i: (i, 0)), pl.BlockSpec((FFN_TN, D_MODEL), lambda j, i: (j, 0)), tile, tile],
        [tile] * 2, [jax.ShapeDtypeStruct((t, D_FF), BF16)] * 2, VMEM_BIG)


def _mm_rms_bwd(operands, weights, x, g, dres, name, comm=None):
    t = x.shape[0]
    tm = 256
    n_op = len(operands)

    def body(*refs):
        a_refs, w_refs = refs[:n_op], refs[n_op:2 * n_op]
        x_ref, g_ref, dres_ref, dx_ref, dxb_ref, gg_ref = refs[2 * n_op:]

        @pl.when(pl.program_id(0) == 0)
        def _():
            gg_ref[...] = jnp.zeros_like(gg_ref)

        dh = None
        for a_ref, w_ref in zip(a_refs, w_refs):
            part = jnp.dot(a_ref[...].astype(MXU_DTYPE), w_ref[...], preferred_element_type=F32)
            dh = part if dh is None else dh + part
        xv = x_ref[...]
        r = lax.rsqrt(jnp.mean(xv * xv, axis=-1, keepdims=True) + RMS_EPS)
        n = xv * r
        gg_ref[...] += jnp.sum(dh * n, axis=0, keepdims=True)
        dn = dh * g_ref[...]
        dx = dres_ref[...] + r * (dn - n * jnp.mean(dn * n, axis=-1, keepdims=True))
        dx_ref[...] = dx
        dxb_ref[...] = dx.astype(BF16)

    d = x.shape[1]
    return _grid_call(
        body, name, (t // tm, 1), [*operands, *weights, x, g, dres],
        [_row_spec(tm, a.shape[1]) for a in operands] + [_whole(wk) for wk in weights]
        + [_row_spec(tm, d), _whole(g), _row_spec(tm, d)],
        [_row_spec(tm, d)] * 2 + [pl.BlockSpec((1, d), lambda i, j: (0, 0))],
        [jax.ShapeDtypeStruct((t, d), F32), jax.ShapeDtypeStruct((t, d), BF16), jax.ShapeDtypeStruct((1, d), F32)],
        VMEM_BIG, comm, sequential=True)


def _flat_small(small):
    perm_b = lambda a: a.reshape(SSM_GROUPS, SSM_STATE, SSM_CH).transpose(2, 0, 1).reshape(SSM_CH, N_STATE)
    perm_c = lambda a: a.reshape(SSM_GROUPS, SSM_CH, SSM_STATE).transpose(1, 0, 2).reshape(SSM_CH, N_STATE)
    return dict(
        g_mix=small["norm_mix_g"].reshape(1, D_MODEL), g_ffn=small["norm_ffn_g"].reshape(1, D_MODEL),
        g_fin=small["norm_final_g"].reshape(1, D_MODEL),
        lr=small["ssm_a_re"].reshape(1, N_STATE), li=small["ssm_a_im"].reshape(1, N_STATE),
        ldt=jnp.repeat(small["ssm_log_dt"].reshape(SSM_GROUPS), SSM_STATE).reshape(1, N_STATE),
        br=perm_b(small["ssm_b_re"]), bi=perm_b(small["ssm_b_im"]),
        cr=perm_c(small["ssm_c_re"]), ci=perm_c(small["ssm_c_im"]), dskip=small["ssm_d"].reshape(1, SSM_W))


AG_HOSTS = {"proj_rope": ("w_glu", "w_attn_out", "w_out", "w_ffn_gate"), "mix_out_rms": ("w_ffn_up",),
            "ffn_in_swiglu": ("w_ffn_down",)}
HALVED = ("w_ffn_gate", "w_ffn_up", "w_in")
A2A_HOSTS = {"d_h2_rms": ("w_ffn_down",), "mix_bwd": ("w_ffn_gate:0",), "attn_bwd_g0": ("w_out",),
             "attn_bwd_g1": ("w_glu",), "attn_bwd_g2": ("w_attn_out",),
             "ssm_bwd": ("w_ffn_gate:1", "w_ffn_up:0", "w_ffn_up:1"), "mm_g_in1": ("w_in:0",), "d_h0_rms": ("w_in:1",)}
SMALL_HOST = "mm_g_in0"


def _local_step(x, target, w, small, shards=None):
    t = x.shape[0]
    n_samples = t // SEQ
    n_rows = n_samples * SCAN_SEG_PER_SAMPLE
    tabs = _rope_tables()
    w = dict(w)
    fs = _flat_small(small)
    g_mix, g_ffn, g_fin, dskip = fs["g_mix"], fs["g_ffn"], fs["g_fin"], fs["dskip"]
    a_cat, bbc, cc = _ssm_disc(fs["lr"], fs["li"], fs["ldt"], fs["br"], fs["bi"], fs["cr"], fs["ci"])
    big, recv, small_pack = {}, {}, []

    def comm_of(name):
        if shards is None:
            return None
        if name == SMALL_HOST:
            return _ag_comm([(small_pack[0], 0, 0)], [(N_DEV, *small_pack[0].shape)])
        if name in AG_HOSTS:
            names = AG_HOSTS[name]
            return _ag_comm([(shards[n], j, 0) for j, n in enumerate(names)], [(N_DEV, *shards[n].shape) for n in names])
        if name in A2A_HOSTS:
            return _a2a_comm([(big[n].reshape(N_DEV, -1, big[n].shape[1]), 0) for n in A2A_HOSTS[name]])
        return None

    def absorb(name, carried):
        if name == SMALL_HOST:
            recv["small"] = carried[0]
        for n, a3 in zip(AG_HOSTS.get(name, ()), carried):
            w[n] = a3.reshape(-1, a3.shape[2])
        for n, a3 in zip(A2A_HOSTS.get(name, ()), carried):
            recv[n] = a3

    def mm(a, b, mode, name, tm, tn, **kw):
        comm = comm_of(name)
        if comm is None:
            return _mm(a, b, mode, name, tm, tn, **kw)
        out, *carried = _mm(a, b, mode, name, tm, tn, comm=comm, **kw)
        absorb(name, carried)
        return out

    h0, u, gates, *rest = _proj_rope(x, g_mix, w["w_in"], tabs, comm_of("proj_rope"))
    qkv = [rest[3 * g:3 * g + 3] for g in range(3)]
    absorb("proj_rope", rest[9:])
    os_, lses = [], []
    for g in range(3):
        o_g, l_g, carried = _attn_fwd(*qkv[g], g, n_samples, comm_of(f"attn_fwd_g{g}"))
        absorb(f"attn_fwd_g{g}", carried)
        os_.append(o_g)
        lses.append(l_g)
    u_perm = _to_scan_rows(u, n_samples)
    ytot, yg_perm, ein = _ssm_fwd(u_perm, a_cat, bbc, cc, dskip, n_rows)
    yg = _from_scan_rows(yg_perm, n_samples)

    attn, lse_tot, merged, x1, h2, *carried = _mix_out_rms(os_, lses, yg, gates, x, w["w_attn_out"], w["w_glu"], w["w_out"],
                                                           g_ffn, comm_of("mix_out_rms"))
    absorb("mix_out_rms", carried)
    ffn_a, ffn_b, f, *carried = _ffn_in_swiglu(h2, w["w_ffn_gate"], w["w_ffn_up"], comm_of("ffn_in_swiglu"))
    absorb("ffn_in_swiglu", carried)
    dx2, dx2b, loss_blk, g_gfin = _ffn_down_final(f, w["w_ffn_down"], x1, target, g_fin)

    da, db = _d_f_swiglu_bwd(dx2b, w["w_ffn_down"], ffn_a, ffn_b)
    big["w_ffn_down"] = mm(f, dx2b, "tn", "mm_g_down", 256, D_MODEL, out_dtype=BF16)
    half = D_MODEL // 2
    for hf in range(2):
        big[f"w_ffn_gate:{hf}"] = mm(da, h2, "tn", f"mm_g_gate{hf}", 256, half, out_dtype=BF16, cols=(hf * half, half))
        big[f"w_ffn_up:{hf}"] = mm(db, h2, "tn", f"mm_g_up{hf}", 256, half, out_dtype=BF16, cols=(hf * half, half))
    dx1, dx1b, g_gffn, *carried = _mm_rms_bwd([da, db], [w["w_ffn_gate"], w["w_ffn_up"]], x1, g_ffn, dx2, "d_h2_rms",
                                              comm_of("d_h2_rms"))
    absorb("d_h2_rms", carried)

    big["w_out"] = mm(merged, dx1b, "tn", "mm_g_out", 256, D_MODEL, out_dtype=BF16)
    dattn_d, dz, dgpre, dattn, dyg, rowdot, *rest = _mix_bwd(dx1b, attn, lse_tot, yg, gates, w["w_attn_out"], w["w_glu"],
                                                             w["w_out"], comm_of("mix_bwd"))
    cot = [(dattn, lse_tot, rowdot), tuple(rest[:3]), tuple(rest[3:6])]
    absorb("mix_bwd", rest[6:])

    big["w_attn_out"] = mm(dattn_d, attn, "tn", "mm_g_attn_out", 512, GROUP_W, out_dtype=BF16)
    big["w_glu"] = mm(dz, yg, "tn", "mm_g_glu", 512, 512, out_dtype=BF16)
    dqs, dks, dvs = [], [], []
    for g in range(3):
        dq_g, dk_g, dv_g, carried = _attn_bwd(*qkv[g], *cot[g], g, n_samples, comm_of(f"attn_bwd_g{g}"))
        absorb(f"attn_bwd_g{g}", carried)
        dqs.append(dq_g)
        dks.append(dk_g)
        dvs.append(dv_g)

    dyg_perm = _to_scan_rows(dyg, n_samples)
    du_perm, g_dskip, da_cat, dbb_full, dc_full, *carried = _ssm_bwd(u_perm, dyg_perm, ytot, dskip, a_cat, bbc, cc, ein,
                                                                   n_rows, comm_of("ssm_bwd"))
    absorb("ssm_bwd", carried)
    du = _from_scan_rows(du_perm, n_samples)
    g_lr, g_li, g_ldt, g_br, g_bi, g_cr, g_ci = _ssm_param_bwd(
        fs["lr"], fs["li"], fs["ldt"], fs["br"], fs["bi"], da_cat, dbb_full, dc_full)

    small_pack.append(_pack_small(dict(lr=g_lr, li=g_li, ldt=g_ldt, br=g_br, bi=g_bi, cr=g_cr, ci=g_ci, dskip=g_dskip,
                                       g_ffn=g_gffn, g_fin=g_gfin, loss=loss_blk)))

    dproj = _pack_dproj(dqs, dks, dvs, du, dgpre, tabs)
    for hf in range(2):
        big[f"w_in:{hf}"] = mm(dproj, h0, "tn", f"mm_g_in{hf}", 256, half, out_dtype=BF16, cols=(hf * half, half))
    grad_x, _, g_gmix, *carried = _mm_rms_bwd([dproj], [w["w_in"]], x, g_mix, dx1, "d_h0_rms", comm_of("d_h0_rms"))
    absorb("d_h0_rms", carried)
    return grad_x, (big if shards is None else recv), small_pack[0], g_gmix


_MESH = pl.DeviceIdType.MESH


def _all_gather(block, name):
    rows, lanes = block.shape

    def body(x_ref, out_ref, send_sems, recv_sems, local_sem):
        x, y, c = lax.axis_index("x"), lax.axis_index("y"), lax.axis_index("c")
        me, sibling = (x, y, c), (x, y, 1 - c)
        chips = [(1 - x, y), (x, 1 - y), (1 - x, 1 - y)]

        def slot(px, py, pc):
            return out_ref.at[4 * px + 2 * py + pc]

        def copy(k, blk, to, src=None):
            return pltpu.make_async_remote_copy(
                src_ref=slot(*blk) if src is None else src, dst_ref=slot(*blk), send_sem=send_sems.at[k],
                recv_sem=recv_sems.at[k], device_id=to, device_id_type=_MESH)

        mine = pltpu.make_async_copy(x_ref, slot(*me), local_sem)
        mine.start()
        first = [copy(0, me, sibling, src=x_ref)]
        first += [copy(1 + j, me, (*chip, c), src=x_ref) for j, chip in enumerate(chips)]
        for cp in first:
            cp.start()
        passed = [copy(4 + j, (*chip, c), sibling) for j, chip in enumerate(chips)]
        for j, chip in enumerate(chips):
            copy(1 + j, (*chip, c), me).wait_recv()
            passed[j].start()
        copy(0, sibling, me).wait_recv()
        for j, chip in enumerate(chips):
            copy(4 + j, (*chip, 1 - c), me).wait_recv()
        for cp in first + passed:
            cp.wait_send()
        mine.wait()

    return _pallas_call(
        body, name=name, out_shape=jax.ShapeDtypeStruct((N_DEV, rows, lanes), block.dtype),
        in_specs=[pl.BlockSpec(memory_space=pl.ANY)], out_specs=pl.BlockSpec(memory_space=pl.ANY),
        scratch_shapes=[pltpu.SemaphoreType.DMA((7,)), pltpu.SemaphoreType.DMA((7,)), pltpu.SemaphoreType.DMA],
    )(block)


def _ag_comm(items, bufs):
    def plan(in_refs, out_refs, send_sems, recv_sems, local_sems):
        x, y, c = lax.axis_index("x"), lax.axis_index("y"), lax.axis_index("c")
        me, sibling = (x, y, c), (x, y, 1 - c)
        chips = [(1 - x, y), (x, 1 - y), (1 - x, 1 - y)]
        plans = []
        for t, (_, buf, slot0) in enumerate(items):
            x_ref, out_ref = in_refs[t], out_refs[buf]

            def slot(px, py, pc, out_ref=out_ref, slot0=slot0):
                return out_ref.at[slot0 + 4 * px + 2 * py + pc]

            def copy(k, blk, to, src=None, t=t, slot=slot):
                return pltpu.make_async_remote_copy(
                    src_ref=slot(*blk) if src is None else src, dst_ref=slot(*blk), send_sem=send_sems.at[7 * t + k],
                    recv_sem=recv_sems.at[7 * t + k], device_id=to, device_id_type=_MESH)

            plans.append(dict(
                mine=pltpu.make_async_copy(x_ref, slot(*me), local_sems.at[t]),
                first=[copy(0, me, sibling, src=x_ref)] + [copy(1 + j, me, (*chip, c), src=x_ref)
                                                           for j, chip in enumerate(chips)],
                passed=[copy(4 + j, (*chip, c), sibling) for j, chip in enumerate(chips)],
                from_ici=[copy(1 + j, (*chip, c), me) for j, chip in enumerate(chips)],
                from_sibling=[copy(0, sibling, me)] + [copy(4 + j, (*chip, 1 - c), me) for j, chip in enumerate(chips)]))
        return plans

    def start(*refs):
        for p in plan(*refs):
            p["mine"].start()
            for cp in p["first"]:
                cp.start()

    def finish(*refs):
        plans = plan(*refs)
        for p in plans:
            for arrived, onward in zip(p["from_ici"], p["passed"]):
                arrived.wait_recv()
                onward.start()
        for p in plans:
            for arrived in p["from_sibling"]:
                arrived.wait_recv()
            for cp in p["first"] + p["passed"]:
                cp.wait_send()
            p["mine"].wait()

    dtype_of = {buf: shard.dtype for shard, buf, _ in items}
    out_shapes = [jax.ShapeDtypeStruct(b, dtype_of[j]) for j, b in enumerate(bufs)]
    return _Comm([it[0] for it in items], out_shapes, 7 * len(items), len(items), start, finish)


def _a2a_comm(items):
    def plan(in_refs, out_refs, send_sems, recv_sems, local_sems):
        x, y, c = lax.axis_index("x"), lax.axis_index("y"), lax.axis_index("c")
        my = 4 * x + 2 * y + c
        copies, locals_ = [], []
        for t, (_, slot0) in enumerate(items):
            s_ref, r_ref = in_refs[t], out_refs[t]
            locals_.append(pltpu.make_async_copy(s_ref.at[slot0 + my], r_ref.at[my], local_sems.at[t]))
            for kk in range(1, N_DEV):
                px = 1 - x if kk & 4 else x
                py = 1 - y if kk & 2 else y
                pc = 1 - c if kk & 1 else c
                copies.append(pltpu.make_async_remote_copy(
                    src_ref=s_ref.at[slot0 + 4 * px + 2 * py + pc], dst_ref=r_ref.at[my],
                    send_sem=send_sems.at[7 * t + kk - 1], recv_sem=recv_sems.at[7 * t + kk - 1],
                    device_id=(px, py, pc), device_id_type=_MESH))
        return copies, locals_

    def start(*refs):
        copies, locals_ = plan(*refs)
        for cp in locals_ + copies:
            cp.start()

    def finish(*refs):
        copies, locals_ = plan(*refs)
        for cp in copies + locals_:
            cp.wait()

    out_shapes = [jax.ShapeDtypeStruct((N_DEV,) + it[0].shape[1:], it[0].dtype) for it in items]
    return _Comm([it[0] for it in items], out_shapes, 7 * len(items), len(items), start, finish)


def _adam_math(g, w, m, v):
    m_new = ADAM_B1 * m + (1.0 - ADAM_B1) * g
    v_new = ADAM_B2 * v + (1.0 - ADAM_B2) * jnp.square(g)
    m_hat = m_new / (1.0 - ADAM_B1 ** ADAM_STEP)
    v_hat = v_new / (1.0 - ADAM_B2 ** ADAM_STEP)
    return -ADAM_LR * (m_hat / (jnp.sqrt(v_hat) + ADAM_EPS) + ADAM_WD * w), m_new, v_new


def _sum_partials(parts, name, tm):
    n, rows, _ = parts[0].shape
    widths = [p.shape[2] for p in parts]

    def body(*refs):
        g_ref, off = refs[-1], 0
        for p_ref, wd in zip(refs[:-1], widths):
            g = p_ref[0].astype(F32)
            for s in range(1, n):
                g = g + p_ref[s].astype(F32)
            g_ref[:, off:off + wd] = g
            off += wd

    return _pallas_call(
        body, name=name, grid=(rows // tm,), in_specs=[pl.BlockSpec((n, tm, wd), lambda i: (0, i, 0)) for wd in widths],
        out_specs=pl.BlockSpec((tm, sum(widths)), lambda i: (i, 0)),
        out_shape=jax.ShapeDtypeStruct((rows, sum(widths)), F32),
        compiler_params=pltpu.CompilerParams(dimension_semantics=("parallel",), vmem_limit_bytes=VMEM_MID),
    )(*parts)


def _adam(parts, w, m, v, name, tm):
    n, rows, _ = parts[0].shape
    widths = [p.shape[2] for p in parts]
    cols = sum(widths)

    def body(*refs):
        p_refs, (w_ref, m_ref, v_ref, g_ref, d_ref, nm_ref, nv_ref) = refs[:len(parts)], refs[len(parts):]
        off = 0
        for p_ref, wd in zip(p_refs, widths):
            g = p_ref[0].astype(F32)
            for s in range(1, n):
                g = g + p_ref[s].astype(F32)
            sl = slice(off, off + wd)
            g_ref[:, sl] = g
            d_ref[:, sl], nm_ref[:, sl], nv_ref[:, sl] = _adam_math(g, w_ref[:, sl], m_ref[:, sl], v_ref[:, sl])
            off += wd

    assert rows % tm == 0
    row = pl.BlockSpec((tm, cols), lambda i: (i, 0))
    shp = jax.ShapeDtypeStruct((rows, cols), F32)
    return _pallas_call(
        body, name=name, grid=(rows // tm,),
        in_specs=[pl.BlockSpec((n, tm, wd), lambda i: (0, i, 0)) for wd in widths] + [row, row, row],
        out_specs=[row] * 4, out_shape=[shp] * 4,
        compiler_params=pltpu.CompilerParams(dimension_semantics=("parallel",), vmem_limit_bytes=VMEM_MID),
    )(*parts, w, m, v)


_PK_LR, _PK_LI, _PK_GAINS, _PK_MISC, _PK_BR, _PK_BI, _PK_CR, _PK_CI, _PK_ROWS = 0, 1, 2, 3, 8, 24, 40, 56, 72
_PK_LDT_LANE, _PK_LOSS_LANE = D_MODEL + SSM_W, D_MODEL + SSM_W + LANES


def _pack_small(sg):
    names = ("lr", "li", "g_ffn", "g_fin", "dskip", "ldt", "loss", "br", "bi", "cr", "ci")

    def body(lr, li, gffn, gfin, dskip, ldt, loss, br, bi, cr, ci, o_ref):
        o_ref[...] = jnp.zeros_like(o_ref)
        o_ref[_PK_LR:_PK_LR + 1, :] = lr[...]
        o_ref[_PK_LI:_PK_LI + 1, :] = li[...]
        o_ref[_PK_GAINS:_PK_GAINS + 1, D_MODEL:] = gffn[...]
        o_ref[_PK_MISC:_PK_MISC + 1, :D_MODEL] = gfin[...]
        o_ref[_PK_MISC:_PK_MISC + 1, D_MODEL:D_MODEL + SSM_W] = dskip[...]
        o_ref[_PK_MISC:_PK_MISC + 1, _PK_LDT_LANE:_PK_LDT_LANE + LANES] = ldt[0:1, :]
        o_ref[_PK_MISC:_PK_MISC + 1, _PK_LOSS_LANE:_PK_LOSS_LANE + LANES] = loss[0:1, :]
        o_ref[_PK_BR:_PK_BR + SSM_CH, :] = br[...]
        o_ref[_PK_BI:_PK_BI + SSM_CH, :] = bi[...]
        o_ref[_PK_CR:_PK_CR + SSM_CH, :] = cr[...]
        o_ref[_PK_CI:_PK_CI + SSM_CH, :] = ci[...]

    return _pallas_call(body, name="pack_small", out_shape=jax.ShapeDtypeStruct((_PK_ROWS, N_STATE), F32))(
        *[sg[n] for n in names])


def _unpack_small(s, g_mix):
    unflat_b = unflat_c = lambda a: a.reshape(SSM_CH, SSM_GROUPS, SSM_STATE).transpose(1, 0, 2)[None]
    grads = {
        "norm_mix_g": g_mix, "norm_ffn_g": s[_PK_GAINS, D_MODEL:].reshape(1, D_MODEL),
        "norm_final_g": s[_PK_MISC, :D_MODEL].reshape(1, D_MODEL),
        "ssm_a_re": s[_PK_LR].reshape(1, SSM_GROUPS, SSM_STATE), "ssm_a_im": s[_PK_LI].reshape(1, SSM_GROUPS, SSM_STATE),
        "ssm_log_dt": s[_PK_MISC, _PK_LDT_LANE:_PK_LDT_LANE + SSM_GROUPS].reshape(1, SSM_GROUPS),
        "ssm_d": s[_PK_MISC, D_MODEL:D_MODEL + SSM_W].reshape(1, SSM_GROUPS, SSM_CH),
        "ssm_b_re": unflat_b(s[_PK_BR:_PK_BR + SSM_CH]), "ssm_b_im": unflat_b(s[_PK_BI:_PK_BI + SSM_CH]),
        "ssm_c_re": unflat_c(s[_PK_CR:_PK_CR + SSM_CH]), "ssm_c_im": unflat_c(s[_PK_CI:_PK_CI + SSM_CH]),
    }
    return s[_PK_MISC, _PK_LOSS_LANE], grads


def _stored(name, a):
    if name in ("ssm_b_re", "ssm_b_im"):
        return a.transpose(0, 1, 3, 2)
    return a.reshape(1, -1) if a.ndim == 1 else a


def _unstored(name, a, like):
    return a.transpose(0, 1, 3, 2) if name in ("ssm_b_re", "ssm_b_im") else a.reshape(like.shape)


def _adam_small(grads, wts, moms, vars_):
    n = len(SMALL_WEIGHTS)

    def body(*refs):
        ins, outs = refs[:4 * n], refs[4 * n:]
        for i in range(n):
            g, w, m, v = (ins[j * n + i][...] for j in range(4))
            outs[i][...], outs[n + i][...], outs[2 * n + i][...] = _adam_math(g, w, m, v)

    operands = [grads[k] if d is grads else _stored(k, d[k]) for d in (grads, wts, moms, vars_) for k in SMALL_WEIGHTS]
    shapes = [jax.ShapeDtypeStruct(_stored(k, wts[k]).shape, F32) for k in SMALL_WEIGHTS] * 3
    res = _pallas_call(body, name="adam_small", out_shape=shapes,
                         compiler_params=pltpu.CompilerParams(vmem_limit_bytes=VMEM_BIG))(*operands)
    out = {}
    for j, kind in enumerate(("delta", "new_m", "new_v")):
        for i, k in enumerate(SMALL_WEIGHTS):
            out[kind, k] = _unstored(k, res[j * n + i], wts[k])
    return out


def kernel(x, norm_mix_g, w_in, ssm_a_re, ssm_a_im, ssm_log_dt, ssm_b_re, ssm_b_im, ssm_c_re, ssm_c_im, ssm_d, w_glu, w_attn_out, w_out, norm_ffn_g, w_ffn_gate, w_ffn_up, w_ffn_down, norm_final_g, loss_target, m_norm_mix_g, m_w_in, m_ssm_a_re, m_ssm_a_im, m_ssm_log_dt, m_ssm_b_re, m_ssm_b_im, m_ssm_c_re, m_ssm_c_im, m_ssm_d, m_w_glu, m_w_attn_out, m_w_out, m_norm_ffn_g, m_w_ffn_gate, m_w_ffn_up, m_w_ffn_down, m_norm_final_g, v_norm_mix_g, v_w_in, v_ssm_a_re, v_ssm_a_im, v_ssm_log_dt, v_ssm_b_re, v_ssm_b_im, v_ssm_c_re, v_ssm_c_im, v_ssm_d, v_w_glu, v_w_attn_out, v_w_out, v_norm_ffn_g, v_w_ffn_gate, v_w_ffn_up, v_w_ffn_down, v_norm_final_g):
    args = dict(locals())
    wts = {n: args[n] for n in ALL_WEIGHTS}
    moms = {n: args["m_" + n] for n in ALL_WEIGHTS}
    vars_ = {n: args["v_" + n] for n in ALL_WEIGHTS}
    n_samples = x.shape[0]
    t = n_samples * SEQ

    shards = {n: (wts[n][0] if n in ROW_SHARDED else wts[n][0].T).astype(BF16) for n in BIG_WEIGHTS}
    w_in_t = _all_gather(shards["w_in"], "allgather_w_in").reshape(IN_W, D_MODEL)

    small = {n: wts[n] for n in SMALL_WEIGHTS}
    grad_x, recv, _, g_mix_part = _local_step(x.reshape(t, D_MODEL), loss_target.reshape(t, D_MODEL), {"w_in": w_in_t},
                                              small, shards)

    results = {}
    for n in BIG_WEIGHTS:
        c, k = shards[n].shape
        w2, m2, v2 = wts[n][0], moms[n][0], vars_[n][0]
        if n in ROW_SHARDED:
            res = _adam([recv[n]], w2, m2, v2, "adam_" + n, c // 2)
        elif n in HALVED:
            res = _adam([recv[f"{n}:{hf}"] for hf in range(2)], w2.T, m2.T, v2.T, "adam_" + n, c // 2)
            res = [a.T for a in res]
        else:
            g_t = _sum_partials([recv[n]], "sum_" + n, c // 2)
            res = _adam([g_t.T[None]], w2, m2, v2, "adam_" + n, k // 2)
        for kind, a in zip(("grad", "delta", "new_m", "new_v"), res):
            results[kind, n] = a[None]

    g_mix_all = _all_gather(jnp.pad(g_mix_part, ((0, 7), (0, 0))), "allgather_g_mix")
    g_mix = _sum_partials([g_mix_all], "sum_g_mix", 8)[0:1]
    loss, sgrads = _unpack_small(_sum_partials([recv["small"]], "sum_small", _PK_ROWS), g_mix)
    for n in SMALL_WEIGHTS:
        results["grad", n] = _unstored(n, sgrads[n], wts[n])
    results.update(_adam_small(sgrads, wts, moms, vars_))
    outs = [loss, grad_x.reshape(x.shape)]
    for kind in ("grad", "delta", "new_m", "new_v"):
        outs += [results[kind, n] for n in ALL_WEIGHTS]
    return tuple(outs)
```

```python
import functools
import math

import jax
import jax.numpy as jnp
from jax import lax
from jax.experimental import pallas as pl
from jax.experimental.pallas import tpu as pltpu

F32 = jnp.float32
BF16 = jnp.bfloat16
MXU_DTYPE = jnp.bfloat16

N_DEV = 8
D_MODEL = 1024
SEQ = 2048
HEAD_DIM = 64
HEADS_PER_GROUP = 4
GROUP_W = HEADS_PER_GROUP * HEAD_DIM
DILATIONS = (1, 4, 16)
QKV_W = 3 * len(DILATIONS) * GROUP_W
Q_W = len(DILATIONS) * GROUP_W
ATT_BLOCK = 128
ROPE_DIM = 16
ROPE_THETA = 500000.0
SSM_W = 512
SSM_GROUPS = 32
SSM_CH = 16
SSM_STATE = 64
N_STATE = SSM_GROUPS * SSM_STATE
D_FF = 2816
IN_W = QKV_W + SSM_W + 2 * D_MODEL
RMS_EPS = 1e-6
NEG_INF = -1e30
LANES = 128

SCAN_SEG_PER_SAMPLE = 8
SCAN_LEN = SEQ // SCAN_SEG_PER_SAMPLE
SCAN_WC = 512
SCAN_NBLK = N_STATE // SCAN_WC
SCAN_CH = SSM_W // SCAN_NBLK
SCAN_CHUNK = 32

ADAM_LR = 0.001
ADAM_B1 = 0.9
ADAM_B2 = 0.999
ADAM_EPS = 1e-08
ADAM_WD = 0.01
ADAM_STEP = 10

VMEM_BIG = 48 * 1024 * 1024
VMEM_MID = 32 * 1024 * 1024

BIG_WEIGHTS = ("w_in", "w_glu", "w_attn_out", "w_out", "w_ffn_gate", "w_ffn_up", "w_ffn_down")
ROW_SHARDED = ("w_out", "w_ffn_down")
SMALL_WEIGHTS = ("norm_mix_g", "ssm_a_re", "ssm_a_im", "ssm_log_dt", "ssm_b_re", "ssm_b_im", "ssm_c_re", "ssm_c_im",
                 "ssm_d", "norm_ffn_g", "norm_final_g")
ALL_WEIGHTS = ("norm_mix_g", "w_in", "ssm_a_re", "ssm_a_im", "ssm_log_dt", "ssm_b_re", "ssm_b_im", "ssm_c_re", "ssm_c_im",
               "ssm_d", "w_glu", "w_attn_out", "w_out", "norm_ffn_g", "w_ffn_gate", "w_ffn_up", "w_ffn_down", "norm_final_g")


def _sigmoid(x):
    return 1.0 / (1.0 + jnp.exp(-x))


def _pallas_call(body, *, out_shape, **kw):
    single = not isinstance(out_shape, (list, tuple))
    shapes = [pltpu.HBM(s.shape, s.dtype) for s in ([out_shape] if single else out_shape)]
    call = pl.pallas_call(body, out_shape=shapes[0] if single else shapes, **kw)
    return lambda *operands: call(*[pltpu.with_memory_space_constraint(o, pltpu.HBM) for o in operands])


class _Comm:
    def __init__(self, ins, out_shapes, n_sem, n_local, start, finish):
        self.ins, self.out_shapes, self.n_sem, self.n_local = ins, out_shapes, n_sem, n_local
        self.start, self.finish = start, finish


def _mm(a, b, mode, name, tm, tn, out_dtype=F32, add=None, vmem=VMEM_BIG, comm=None, cols=None):
    if mode == "nn":
        (m, k), (_, n) = a.shape, b.shape
        a_spec = pl.BlockSpec((tm, k), lambda i, j: (i, 0))
        b_spec = pl.BlockSpec((k, tn), lambda i, j: (0, j))
        dims = (((1,), (0,)), ((), ()))
    elif mode == "nt":
        (m, k), (n, _) = a.shape, b.shape
        a_spec = pl.BlockSpec((tm, k), lambda i, j: (i, 0))
        b_spec = pl.BlockSpec((tn, k), lambda i, j: (j, 0))
        dims = (((1,), (1,)), ((), ()))
    else:
        (k, m), (_, n) = a.shape, b.shape
        first, n = cols if cols else (0, n)
        a_spec = pl.BlockSpec((k, tm), lambda i, j: (0, i))
        b_spec = pl.BlockSpec((k, tn), lambda i, j: (0, j + first // tn))
        dims = (((0,), (0,)), ((), ()))
    assert m % tm == 0 and n % tn == 0, (name, m, n, tm, tn)
    o_spec = pl.BlockSpec((tm, tn), lambda i, j: (i, j))
    has_add = add is not None

    def body(*refs):
        a_ref, b_ref, o_ref = refs[0], refs[1], refs[-1]
        acc = lax.dot_general(a_ref[...].astype(MXU_DTYPE), b_ref[...].astype(MXU_DTYPE), dims,
                              preferred_element_type=F32)
        if has_add:
            acc = acc + refs[2][...]
        o_ref[...] = acc.astype(out_dtype)

    ins = [a, b] + ([add] if has_add else [])
    in_specs = [a_spec, b_spec] + ([o_spec] if has_add else [])
    return _grid_call(body, name, (m // tm, n // tn), ins, in_specs, [o_spec],
                      [jax.ShapeDtypeStruct((m, n), out_dtype)], vmem, comm)


def _grid_call(body, name, grid, ins, in_specs, out_specs, out_shapes, vmem, comm=None, sequential=False, scratch=()):
    if comm is None:
        single = len(out_shapes) == 1
        semantics = ("arbitrary", "arbitrary") if sequential else ("parallel", "parallel")
        return _pallas_call(
            body, name=name, grid=grid, in_specs=in_specs, out_specs=out_specs[0] if single else out_specs,
            out_shape=out_shapes[0] if single else out_shapes, scratch_shapes=list(scratch),
            compiler_params=pltpu.CompilerParams(dimension_semantics=semantics, vmem_limit_bytes=vmem),
        )(*ins)
    n_in, n_out, n_cin, n_cout = len(ins), len(out_shapes), len(comm.ins), len(comm.out_shapes)
    n_io = n_in + n_cin + n_out + n_cout

    def carrying(*refs):
        own = refs[:n_in] + refs[n_in + n_cin:n_in + n_cin + n_out] + refs[n_io:len(refs) - 3]
        c_args = (refs[n_in:n_in + n_cin], refs[n_in + n_cin + n_out:n_io], *refs[-3:])

        @pl.when((pl.program_id(0) == 0) & (pl.program_id(1) == 0))
        def _():
            comm.start(*c_args)

        body(*own)

        @pl.when((pl.program_id(0) == grid[0] - 1) & (pl.program_id(1) == grid[1] - 1))
        def _():
            comm.finish(*c_args)

    hbm = pl.BlockSpec(memory_space=pl.ANY)
    return _pallas_call(
        carrying, name=name, grid=grid, in_specs=list(in_specs) + [hbm] * n_cin,
        out_specs=list(out_specs) + [hbm] * n_cout, out_shape=list(out_shapes) + list(comm.out_shapes),
        scratch_shapes=list(scratch) + [pltpu.SemaphoreType.DMA((comm.n_sem,)), pltpu.SemaphoreType.DMA((comm.n_sem,)),
                                        pltpu.SemaphoreType.DMA((comm.n_local,))],
        compiler_params=pltpu.CompilerParams(dimension_semantics=("arbitrary", "arbitrary"), vmem_limit_bytes=vmem),
    )(*ins, *comm.ins)


def _rows(body, name, n_rows, tm, ins, outs, vmem=VMEM_MID, scratch=()):
    assert n_rows % tm == 0
    arrays, in_specs = [], []
    for kind, arr in ins:
        arrays.append(arr)
        if kind == "row":
            assert n_rows % arr.shape[0] == 0, (name, arr.shape)
            in_specs.append(pl.BlockSpec((tm * arr.shape[0] // n_rows, arr.shape[1]), lambda i: (i, 0)))
        elif kind == "tab":
            nblk = arr.shape[0] // tm
            in_specs.append(pl.BlockSpec((tm, arr.shape[1]), lambda i, nblk=nblk: (i % nblk, 0)))
        else:
            in_specs.append(pl.BlockSpec(arr.shape, lambda i, nd=arr.ndim: (0,) * nd))
    out_specs, out_shape = [], []
    for kind, shp, dt in outs:
        if kind == "row":
            out_specs.append(pl.BlockSpec((tm, shp), lambda i: (i, 0)))
            out_shape.append(jax.ShapeDtypeStruct((n_rows, shp), dt))
        elif kind == "dil":
            d, wd = shp
            out_specs.append(pl.BlockSpec((tm // d, d * wd), lambda i: (i, 0)))
            out_shape.append(jax.ShapeDtypeStruct((n_rows // d, d * wd), dt))
        else:
            out_specs.append(pl.BlockSpec(shp, lambda i, nd=len(shp): (0,) * nd))
            out_shape.append(jax.ShapeDtypeStruct(shp, dt))
    res = _pallas_call(
        body, name=name, grid=(n_rows // tm,), in_specs=in_specs, out_specs=out_specs, out_shape=out_shape,
        scratch_shapes=list(scratch),
        compiler_params=pltpu.CompilerParams(dimension_semantics=("arbitrary",), vmem_limit_bytes=vmem),
    )(*arrays)
    return res


def _gather_residue(stage, ch, r, d, n):
    return stage[ch, pl.ds(r, n, stride=d), :] if d > 1 else stage[ch]


def _scatter_residue(stage, ch, r, d, n, val):
    if d > 1:
        stage[ch, pl.ds(r, n, stride=d), :] = val
    else:
        stage[ch] = val


def _lane_chunk(ch):
    return slice(ch * LANES, (ch + 1) * LANES)


def _rope_tables():
    half = ROPE_DIM // 2
    inv = jnp.power(jnp.float32(ROPE_THETA), -jnp.arange(half, dtype=F32) * 2.0 / ROPE_DIM)
    ang = jnp.arange(SEQ, dtype=F32)[:, None] * inv[None, :]
    lane = jnp.arange(LANES) % HEAD_DIM
    cosl = jnp.cos(ang)[:, lane % half]
    sinl = jnp.sin(ang)[:, lane % half]
    tab_c = jnp.where(lane < ROPE_DIM, cosl, 1.0)
    tab_lo = jnp.where(lane < half, -sinl, 0.0)
    tab_hi = jnp.where((lane >= half) & (lane < ROPE_DIM), sinl, 0.0)
    return tab_c.astype(F32), tab_lo.astype(F32), tab_hi.astype(F32)


def _rope_apply(t, tc, tlo, thi):
    half = ROPE_DIM // 2
    return t * tc + pltpu.roll(t, LANES - half, 1) * tlo + pltpu.roll(t, half, 1) * thi


def _rope_transpose(dt, tc, tlo, thi):
    half = ROPE_DIM // 2
    return dt * tc + pltpu.roll(dt * tlo, half, 1) + pltpu.roll(dt * thi, LANES - half, 1)


def _pack_dproj(dqs, dks, dvs, du, dgpre, tabs):
    tm = 256

    def body(*refs):
        dq_refs, dk_refs, dv_refs = refs[0:3], refs[3:6], refs[6:9]
        du_ref, dg_ref, tc_ref, tlo_ref, thi_ref, o_ref, stage = refs[9:16]
        n_ch = QKV_W // LANES
        halves = GROUP_W // LANES
        for grp, d in enumerate(DILATIONS):
            for which, src in enumerate((dq_refs[grp], dk_refs[grp], dv_refs[grp])):
                for res in range(d):
                    for half in range(halves):
                        _scatter_residue(stage, which * (n_ch // 3) + grp * halves + half, res, d, tm // d,
                                         src[:, _lane_chunk(res * halves + half)])
        tc, tlo, thi = tc_ref[...], tlo_ref[...], thi_ref[...]
        for ch in range(n_ch):
            piece = stage[ch]
            o_ref[:, _lane_chunk(ch)] = (_rope_transpose(piece, tc, tlo, thi) if ch < 2 * n_ch // 3 else piece).astype(BF16)
        o_ref[:, QKV_W:QKV_W + SSM_W] = du_ref[...].astype(BF16)
        o_ref[:, QKV_W + SSM_W:] = dg_ref[...].astype(BF16)

    t = du.shape[0]
    ins = [("row", a) for a in (*dqs, *dks, *dvs, du, dgpre)] + [("tab", tb) for tb in tabs]
    return _rows(body, "pack_dproj", t, tm, ins, [("row", IN_W, BF16)],
                 scratch=[pltpu.VMEM((QKV_W // LANES, tm, LANES), F32)])[0]


def _merge_groups(o_refs, l_refs, a_ref, lt_ref, nat, tm):
    halves = GROUP_W // LANES
    for grp, d in enumerate(DILATIONS[1:], start=1):
        for j, src in enumerate((o_refs[grp], l_refs[grp])):
            for res in range(d):
                for half in range(halves):
                    _scatter_residue(nat, (grp - 1) * 4 + j * 2 + half, res, d, tm // d,
                                     src[:, _lane_chunk(res * halves + half)])
    for half in range(halves):
        sl = _lane_chunk(half)
        la, lb, lc = l_refs[0][:, sl], nat[2 + half], nat[6 + half]
        m = jnp.maximum(jnp.maximum(la, lb), lc)
        ea, eb, ec = jnp.exp(la - m), jnp.exp(lb - m), jnp.exp(lc - m)
        ssum = ea + eb + ec
        a_ref[:, sl] = (ea / ssum) * o_refs[0][:, sl] + (eb / ssum) * nat[half] + (ec / ssum) * nat[4 + half]
        lt_ref[:, sl] = m + jnp.log(ssum)


def _head_sum_matrix():
    r = jnp.arange(GROUP_W) // HEAD_DIM
    return (r[:, None] == r[None, :]).astype(F32)


def _attention_cotangents(da, attn, lt, ones, rd_ref, dil, stage, tm):
    halves = GROUP_W // LANES
    rd = jnp.dot(da * attn, ones, preferred_element_type=F32, precision=lax.Precision.HIGHEST)
    rd_ref[...] = rd
    for half in range(halves):
        for j, val in enumerate((da, lt, rd)):
            stage[2 * j + half] = val[:, _lane_chunk(half)]
    for grp, d in enumerate(DILATIONS[1:], start=1):
        for j in range(3):
            for res in range(d):
                for half in range(halves):
                    dil[3 * (grp - 1) + j][:, _lane_chunk(res * halves + half)] = _gather_residue(
                        stage, 2 * j + half, res, d, tm // d)


_GELU_C = math.sqrt(2.0 / math.pi)


def _head_masks():
    lane = lax.broadcasted_iota(jnp.int32, (1, GROUP_W), 1)
    return [(lane // HEAD_DIM) == h for h in range(HEADS_PER_GROUP)]


def _stack_heads(blk, masks, fill=0.0):
    return jnp.concatenate([jnp.where(mk, blk, jnp.full_like(blk, fill)) for mk in masks], axis=0)


def _unstack_heads(stacked, masks):
    rows = stacked.shape[0] // len(masks)
    out = stacked[:rows]
    for h in range(1, len(masks)):
        out = jnp.where(masks[h], stacked[h * rows:(h + 1) * rows], out)
    return out


def _band_mask(first):
    nk = ATT_BLOCK if first else 2 * ATT_BLOCK
    qi = lax.broadcasted_iota(jnp.int32, (ATT_BLOCK, nk), 0)
    ki = lax.broadcasted_iota(jnp.int32, (ATT_BLOCK, nk), 1)
    dist = qi - ki + (0 if first else ATT_BLOCK)
    return (dist >= 0) & (dist <= ATT_BLOCK)


_NT = (((1,), (1,)), ((), ()))
_TN = (((0,), (0,)), ((), ()))


def _residues_per_step(d):
    return 4 if d >= 16 else 1


def _attn_fwd(q, k, v, group, n_samples, comm=None):
    d = DILATIONS[group]
    length = SEQ // d
    nb = length // ATT_BLOCK

    rps = _residues_per_step(d)

    def body(q_ref, k_ref, v_ref, o_ref, l_ref):
        for rl in range(rps):
            residue(q_ref, k_ref, v_ref, o_ref, l_ref, slice(rl * GROUP_W, (rl + 1) * GROUP_W))

    def residue(q_ref, k_ref, v_ref, o_ref, l_ref, cols):
        masks = _head_masks()

        def block(qs, ks, first):
            nk = ATT_BLOCK if first else 2 * ATT_BLOCK
            qb = q_ref[0, pl.ds(qs, ATT_BLOCK), cols]
            kc = k_ref[0, pl.ds(ks, nk), cols]
            vc = v_ref[0, pl.ds(ks, nk), cols]
            q4 = _stack_heads(qb, masks)
            valid = jnp.tile(_band_mask(first), (HEADS_PER_GROUP, 1))
            s = lax.dot_general(q4, kc, _NT, preferred_element_type=F32) * (HEAD_DIM ** -0.5)
            s = jnp.where(valid, s, NEG_INF)
            m = jnp.max(s, axis=-1, keepdims=True)
            p = jnp.exp(s - m)
            l = jnp.sum(p, axis=-1, keepdims=True)
            o4 = jnp.dot(p.astype(MXU_DTYPE), vc, preferred_element_type=F32) / l
            lse4 = jnp.broadcast_to(m + jnp.log(l), o4.shape)
            o_ref[0, pl.ds(qs, ATT_BLOCK), cols] = _unstack_heads(o4, masks)
            l_ref[0, pl.ds(qs, ATT_BLOCK), cols] = _unstack_heads(lse4, masks)

        block(0, 0, True)
        if nb > 1:
            def loop(n, carry):
                block(pl.multiple_of(n * ATT_BLOCK, ATT_BLOCK), pl.multiple_of((n - 1) * ATT_BLOCK, ATT_BLOCK), False)
                return carry

            lax.fori_loop(1, nb, loop, 0)

    per_sample = lambda a: a.reshape(n_samples, length, d * GROUP_W)
    spec = pl.BlockSpec((1, length, rps * GROUP_W), lambda b, r: (b, 0, r))
    shp = jax.ShapeDtypeStruct((n_samples, length, d * GROUP_W), F32)
    o, lse, *carried = _grid_call(body, f"attn_fwd_g{group}", (n_samples, d // rps), [per_sample(a) for a in (q, k, v)],
                                  [spec] * 3, [spec] * 2, [shp, shp], VMEM_MID, comm)
    flat = lambda a: a.reshape(n_samples * length, d * GROUP_W)
    return flat(o), flat(lse), carried


def _attn_bwd(q, k, v, dattn, lse_tot, rowdot, group, n_samples, comm=None):
    d = DILATIONS[group]
    length = SEQ // d
    nb = length // ATT_BLOCK

    rps = _residues_per_step(d)

    def body(q_ref, k_ref, v_ref, da_ref, lt_ref, rd_ref, dq_ref, dk_ref, dv_ref):
        dk_ref[...] = jnp.zeros_like(dk_ref)
        dv_ref[...] = jnp.zeros_like(dv_ref)
        for rl in range(rps):
            residue(q_ref, k_ref, v_ref, da_ref, lt_ref, rd_ref, dq_ref, dk_ref, dv_ref,
                    slice(rl * GROUP_W, (rl + 1) * GROUP_W))

    def residue(q_ref, k_ref, v_ref, da_ref, lt_ref, rd_ref, dq_ref, dk_ref, dv_ref, cols):
        masks = _head_masks()

        def block(qs, ks, first):
            nk = ATT_BLOCK if first else 2 * ATT_BLOCK
            qb = q_ref[0, pl.ds(qs, ATT_BLOCK), cols]
            kc = k_ref[0, pl.ds(ks, nk), cols]
            vc = v_ref[0, pl.ds(ks, nk), cols]
            da = da_ref[0, pl.ds(qs, ATT_BLOCK), cols]
            lt = lt_ref[0, pl.ds(qs, ATT_BLOCK), cols]
            rd = rd_ref[0, pl.ds(qs, ATT_BLOCK), cols]
            q4 = _stack_heads(qb, masks)
            da4 = _stack_heads(da, masks).astype(MXU_DTYPE)
            lt4 = jnp.max(_stack_heads(lt, masks, -jnp.inf), axis=-1, keepdims=True)
            rd4 = jnp.max(_stack_heads(rd, masks, -jnp.inf), axis=-1, keepdims=True)
            valid = jnp.tile(_band_mask(first), (HEADS_PER_GROUP, 1))
            s = lax.dot_general(q4, kc, _NT, preferred_element_type=F32) * (HEAD_DIM ** -0.5)
            s = jnp.where(valid, s, NEG_INF)
            p = jnp.exp(s - lt4)
            dp = lax.dot_general(da4, vc, _NT, preferred_element_type=F32)
            ds = (p * (dp - rd4) * (HEAD_DIM ** -0.5)).astype(MXU_DTYPE)
            dq_ref[0, pl.ds(qs, ATT_BLOCK), cols] = _unstack_heads(jnp.dot(ds, kc, preferred_element_type=F32), masks)
            dk_ref[0, pl.ds(ks, nk), cols] += lax.dot_general(ds, q4, _TN, preferred_element_type=F32)
            dv_ref[0, pl.ds(ks, nk), cols] += lax.dot_general(p.astype(MXU_DTYPE), da4, _TN, preferred_element_type=F32)

        block(0, 0, True)
        if nb > 1:
            def loop(n, carry):
                block(pl.multiple_of(n * ATT_BLOCK, ATT_BLOCK), pl.multiple_of((n - 1) * ATT_BLOCK, ATT_BLOCK), False)
                return carry

            lax.fori_loop(1, nb, loop, 0)

    per_sample = lambda a: a.reshape(n_samples, length, d * GROUP_W)
    spec = pl.BlockSpec((1, length, rps * GROUP_W), lambda b, r: (b, 0, r))
    shp = jax.ShapeDtypeStruct((n_samples, length, d * GROUP_W), F32)
    dq, dk, dv, *carried = _grid_call(
        body, f"attn_bwd_g{group}", (n_samples, d // rps), [per_sample(a) for a in (q, k, v, dattn, lse_tot, rowdot)],
        [spec] * 6, [spec] * 3, [shp, shp, shp], VMEM_MID, comm)
    flat = lambda a: a.reshape(n_samples * length, d * GROUP_W)
    return flat(dq), flat(dk), flat(dv), carried


def _disc(lr, li, ldt, br, bi):
    dt = jnp.exp(ldt)
    mag = jnp.exp(lr * dt)
    ab_re, ab_im = mag * jnp.cos(li * dt), mag * jnp.sin(li * dt)
    den = lr * lr + li * li
    nr, ni = ab_re - 1.0, ab_im
    f_re = (nr * lr + ni * li) / den
    f_im = (ni * lr - nr * li) / den
    return ab_re, ab_im, f_re * br - f_im * bi, f_re * bi + f_im * br


def _state_mask():
    row_g = lax.broadcasted_iota(jnp.int32, (SCAN_CH, SCAN_WC), 0) // SSM_CH
    col_g = lax.broadcasted_iota(jnp.int32, (SCAN_CH, SCAN_WC), 1) // SSM_STATE
    return row_g == col_g


def _ssm_disc(lr, li, ldt, br, bi, cr, ci):
    w = SCAN_WC

    def body(lr_ref, li_ref, ldt_ref, br_ref, bi_ref, cr_ref, ci_ref, a_ref, bb_ref, c_ref):
        ar, ai, bbr, bbi = _disc(lr_ref[...], li_ref[...], ldt_ref[...], br_ref[...], bi_ref[...])
        crv, civ = cr_ref[...], ci_ref[...]
        mask = _state_mask()
        for cb in range(SCAN_NBLK):
            sl = slice(cb * w, (cb + 1) * w)
            rows = slice(cb * SCAN_CH, (cb + 1) * SCAN_CH)
            dense = lambda comp: jnp.where(mask, jnp.tile(comp[:, sl], (SCAN_CH // SSM_CH, 1)), 0.0)
            a_ref[:, 2 * cb * w:(2 * cb + 1) * w] = ar[:, sl]
            a_ref[:, (2 * cb + 1) * w:(2 * cb + 2) * w] = ai[:, sl]
            bb_ref[rows, :w] = dense(bbr).astype(MXU_DTYPE)
            bb_ref[rows, w:] = dense(bbi).astype(MXU_DTYPE)
            c_ref[rows, :w] = dense(crv).astype(MXU_DTYPE)
            c_ref[rows, w:] = (-dense(civ)).astype(MXU_DTYPE)

    return _pallas_call(
        body, name="ssm_disc",
        out_shape=[jax.ShapeDtypeStruct((1, 2 * N_STATE), F32), jax.ShapeDtypeStruct((SSM_W, 2 * w), MXU_DTYPE),
                   jax.ShapeDtypeStruct((SSM_W, 2 * w), MXU_DTYPE)],
        compiler_params=pltpu.CompilerParams(vmem_limit_bytes=VMEM_MID),
    )(lr, li, ldt, br, bi, cr, ci)


def _group_indicator():
    s = jnp.arange(N_STATE) // SSM_STATE
    return (s[:, None] == jnp.arange(LANES)[None, :]).astype(F32)


def _ssm_param_bwd(lr, li, ldt, br, bi, da_cat, dbb_full, dc_full):
    w = SCAN_WC

    def body(lr_ref, li_ref, ldt_ref, br_ref, bi_ref, da_ref, dbb_ref, dc_ref, ind_ref,
             glr_ref, gli_ref, gldt_ref, gbr_ref, gbi_ref, gcr_ref, gci_ref):
        mask = _state_mask()

        def diag_parts(ref):
            res = ([], [])
            for cb in range(SCAN_NBLK):
                for part in range(2):
                    blk = ref[cb * SCAN_CH:(cb + 1) * SCAN_CH, part * w:(part + 1) * w]
                    res[part].append(jnp.sum(jnp.where(mask, blk, 0.0).reshape(SCAN_CH // SSM_CH, SSM_CH, w), axis=0))
            return jnp.concatenate(res[0], axis=1), jnp.concatenate(res[1], axis=1)

        dar = jnp.concatenate([da_ref[:, 2 * cb * w:(2 * cb + 1) * w] for cb in range(SCAN_NBLK)], axis=1)
        dai = jnp.concatenate([da_ref[:, (2 * cb + 1) * w:(2 * cb + 2) * w] for cb in range(SCAN_NBLK)], axis=1)
        dbbr, dbbi = diag_parts(dbb_ref)
        dcr, dci_neg = diag_parts(dc_ref)
        gcr_ref[...] = dcr
        gci_ref[...] = -dci_neg
        _, vjp = jax.vjp(_disc, lr_ref[...], li_ref[...], ldt_ref[...], br_ref[...], bi_ref[...])
        glr, gli, gldt, gbr, gbi = vjp((dar, dai, dbbr, dbbi))
        glr_ref[...] = glr
        gli_ref[...] = gli
        gldt_ref[...] = jnp.dot(jnp.broadcast_to(gldt, (8, N_STATE)), ind_ref[...], preferred_element_type=F32,
                                precision=lax.Precision.HIGHEST)
        gbr_ref[...] = gbr
        gbi_ref[...] = gbi

    v1 = jax.ShapeDtypeStruct((1, N_STATE), F32)
    v16 = jax.ShapeDtypeStruct((SSM_CH, N_STATE), F32)
    vdt = jax.ShapeDtypeStruct((8, LANES), F32)
    return _pallas_call(
        body, name="ssm_param_bwd", out_shape=[v1, v1, vdt, v16, v16, v16, v16],
        compiler_params=pltpu.CompilerParams(vmem_limit_bytes=VMEM_BIG),
    )(lr, li, ldt, br, bi, da_cat, dbb_full, dc_full, _group_indicator())


def _cmul(ar, ai, br, bi):
    return ar * br - ai * bi, ar * bi + ai * br


def _gelu_tanh(y):
    return jnp.tanh(_GELU_C * (y + 0.044715 * (y * y * y)))


def _segment_carry(er, ei, ar, ai, n_rows, reverse):
    qr, qi = ar, ai
    for _ in range(int(math.log2(SCAN_LEN))):
        qr, qi = _cmul(qr, qi, qr, qi)
    seg = lax.broadcasted_iota(jnp.int32, er.shape, 0) % SCAN_SEG_PER_SAMPLE
    shift = 1
    while shift < SCAN_SEG_PER_SAMPLE:
        keep = (seg < SCAN_SEG_PER_SAMPLE - shift) if reverse else (seg >= shift)
        amount = n_rows - shift if reverse else shift
        sr = jnp.where(keep, pltpu.roll(er, amount, 0), 0.0)
        si = jnp.where(keep, pltpu.roll(ei, amount, 0), 0.0)
        if reverse:
            er, ei = er + qr * sr + qi * si, ei + qr * si - qi * sr
        else:
            er, ei = er + qr * sr - qi * si, ei + qr * si + qi * sr
        qr, qi = _cmul(qr, qi, qr, qi)
        shift *= 2
    keep = (seg < SCAN_SEG_PER_SAMPLE - 1) if reverse else (seg >= 1)
    amount = n_rows - 1 if reverse else 1
    return jnp.where(keep, pltpu.roll(er, amount, 0), 0.0), jnp.where(keep, pltpu.roll(ei, amount, 0), 0.0)


def _ssm_fwd(u_perm, a_cat, bbc, cc, dskip, n_rows):
    t = u_perm.shape[0]
    w = SCAN_WC
    rows_c = SCAN_CHUNK * n_rows
    n_chunks = t // rows_c

    assert n_chunks % 2 == 0

    def body(u_ref, a_ref, bb_ref, c_ref, d_ref, yt_ref, yg_ref, ein_ref, bu_all, st_a, st_b, xs_a, xs_b):
        ar = jnp.broadcast_to(a_ref[:, :w], (n_rows, w))
        ai = jnp.broadcast_to(a_ref[:, w:], (n_rows, w))
        start = lambda ch: pl.multiple_of(ch * rows_c, rows_c)

        def project(ch, stage):
            res = jnp.dot(u_ref[pl.ds(start(ch), rows_c), :].astype(MXU_DTYPE), bb_ref[...], preferred_element_type=F32)
            stage[...] = res
            bu_all[pl.ds(start(ch), rows_c), :] = res

        def steps(src, r0, carry, xs=None):
            for i in range(SCAN_CHUNK):
                blk = src[pl.ds(r0 + i * n_rows, n_rows), :]
                carry = (ar * carry[0] - ai * carry[1] + blk[:, :w], ar * carry[1] + ai * carry[0] + blk[:, w:])
                if xs is not None:
                    xs[i * n_rows:(i + 1) * n_rows, :w] = carry[0]
                    xs[i * n_rows:(i + 1) * n_rows, w:] = carry[1]
            return carry

        def emit(xs, ch):
            y = lax.dot_general(xs[...].astype(MXU_DTYPE), c_ref[...], _NT, preferred_element_type=F32)
            yt = y + d_ref[...] * u_ref[pl.ds(start(ch), rows_c), :]
            yt_ref[pl.ds(start(ch), rows_c), :] = yt
            yg_ref[pl.ds(start(ch), rows_c), :] = (0.5 * yt * (1.0 + _gelu_tanh(yt))).astype(BF16)

        project(0, st_a)

        def pair1(p, carry):
            project(2 * p + 1, st_b)
            carry = steps(st_a, 0, carry)
            project(jnp.minimum(2 * p + 2, n_chunks - 1), st_a)
            return steps(st_b, 0, carry)

        zero = jnp.zeros((n_rows, w), F32)
        er, ei = lax.fori_loop(0, n_chunks // 2, pair1, (zero, zero))
        cr, ci = _segment_carry(er, ei, ar, ai, n_rows, False)
        ein_ref[:, :w] = cr
        ein_ref[:, w:] = ci

        xs_b[...] = jnp.zeros_like(xs_b)

        def pair2(p, carry):
            emit(xs_b, jnp.maximum(2 * p - 1, 0))
            carry = steps(bu_all, start(2 * p), carry, xs_a)
            emit(xs_a, 2 * p)
            return steps(bu_all, start(2 * p + 1), carry, xs_b)

        lax.fori_loop(0, n_chunks // 2, pair2, (cr, ci))
        emit(xs_b, n_chunks - 1)

    col = lambda width: pl.BlockSpec((t, width), lambda c: (0, c))
    wgt = pl.BlockSpec((SCAN_CH, 2 * w), lambda c: (c, 0))
    return _pallas_call(
        body, name="ssm_fwd", grid=(SCAN_NBLK,),
        in_specs=[col(SCAN_CH), pl.BlockSpec((1, 2 * w), lambda c: (0, c)), wgt, wgt,
                  pl.BlockSpec((1, SCAN_CH), lambda c: (0, c))],
        out_specs=[col(SCAN_CH), col(SCAN_CH), pl.BlockSpec((n_rows, 2 * w), lambda c: (0, c))],
        out_shape=[jax.ShapeDtypeStruct((t, SSM_W), F32), jax.ShapeDtypeStruct((t, SSM_W), BF16),
                   jax.ShapeDtypeStruct((n_rows, 2 * N_STATE), F32)],
        scratch_shapes=[pltpu.VMEM((t, 2 * w), F32)] + [pltpu.VMEM((rows_c, 2 * w), F32)] * 4,
        compiler_params=pltpu.CompilerParams(dimension_semantics=("parallel",), vmem_limit_bytes=VMEM_BIG),
    )(u_perm, a_cat, bbc, cc, dskip)


def _ssm_bwd(u_perm, dyg, ytot, dskip, a_cat, bbc, cc, ein, n_rows, comm=None):
    t = u_perm.shape[0]
    w = SCAN_WC
    rows_c = SCAN_CHUNK * n_rows
    n_chunks = t // rows_c

    assert n_chunks % 2 == 0
    last = n_chunks - 1

    def body(u_ref, dyg_ref, yt_ref, dk_ref, a_ref, bb_ref, c_ref, ein_ref, du_ref, gd_ref, da_ref, dbb_ref, dc_ref,
             xs_all, dy_s, st_a, st_b, buf_a, buf_b):
        ar = jnp.broadcast_to(a_ref[:, :w], (n_rows, w))
        ai = jnp.broadcast_to(a_ref[:, w:], (n_rows, w))
        zero = jnp.zeros((n_rows, w), F32)
        start = lambda ch: pl.multiple_of(ch * rows_c, rows_c)
        dbb_ref[...] = jnp.zeros_like(dbb_ref)
        dc_ref[...] = jnp.zeros_like(dc_ref)
        da_ref[...] = jnp.zeros_like(da_ref)

        yt = yt_ref[...]
        th = _gelu_tanh(yt)
        dgelu = 0.5 * (1.0 + th) + 0.5 * yt * (1.0 - th * th) * _GELU_C * (1.0 + 3.0 * 0.044715 * yt * yt)
        dy_all = dyg_ref[...] * dgelu
        dy_s[...] = dy_all
        gd_ref[...] = jnp.sum(dy_all * u_ref[...], axis=0, keepdims=True)
        dy_chunk = lambda ch: dy_s[pl.ds(start(ch), rows_c), :].astype(MXU_DTYPE)

        xs_all[0:n_rows, :] = ein_ref[...]

        def project(ch, stage):
            stage[...] = jnp.dot(u_ref[pl.ds(start(ch), rows_c), :].astype(MXU_DTYPE), bb_ref[...],
                                 preferred_element_type=F32)

        def fwd_steps(stage, ch, carry, xs):
            for i in range(SCAN_CHUNK):
                blk = stage[i * n_rows:(i + 1) * n_rows, :]
                carry = (ar * carry[0] - ai * carry[1] + blk[:, :w], ar * carry[1] + ai * carry[0] + blk[:, w:])
                for half, val in enumerate(carry):
                    xs[i * n_rows:(i + 1) * n_rows, half * w:(half + 1) * w] = val
                    xs_all[pl.ds(start(ch) + (i + 1) * n_rows, n_rows), half * w:(half + 1) * w] = val
            return carry

        def add_dc(xs, ch):
            dc_ref[...] += lax.dot_general(dy_chunk(ch), xs[...].astype(MXU_DTYPE), _TN, preferred_element_type=F32)

        project(0, st_a)

        def fwd_pair(p, carry):
            project(2 * p + 1, st_b)
            carry = fwd_steps(st_a, 2 * p, carry, buf_a)
            add_dc(buf_a, 2 * p)
            project(jnp.minimum(2 * p + 2, last), st_a)
            carry = fwd_steps(st_b, 2 * p + 1, carry, buf_b)
            add_dc(buf_b, 2 * p + 1)
            return carry

        lax.fori_loop(0, n_chunks // 2, fwd_pair, (ein_ref[:, :w], ein_ref[:, w:]))

        def project_dx(ch, stage):
            stage[...] = jnp.dot(dy_chunk(ch), c_ref[...], preferred_element_type=F32)

        def back_steps(stage, carry, g_buf=None):
            for i in reversed(range(SCAN_CHUNK)):
                blk = stage[i * n_rows:(i + 1) * n_rows, :]
                carry = (blk[:, :w] + ar * carry[0] + ai * carry[1], blk[:, w:] + ar * carry[1] - ai * carry[0])
                if g_buf is not None:
                    g_buf[i * n_rows:(i + 1) * n_rows, :w] = carry[0]
                    g_buf[i * n_rows:(i + 1) * n_rows, w:] = carry[1]
            return carry

        def first_pair(p, carry):
            project_dx(last - 2 * p - 1, st_b)
            carry = back_steps(st_a, carry)
            project_dx(jnp.maximum(last - 2 * p - 2, 0), st_a)
            return back_steps(st_b, carry)

        project_dx(last, st_a)
        sr, si = lax.fori_loop(0, n_chunks // 2, first_pair, (zero, zero))
        gr0, gi0 = _segment_carry(sr, si, ar, ai, n_rows, True)

        def post(g_buf, ch):
            g = g_buf[...]
            xp = xs_all[pl.ds(start(ch), rows_c), :]
            da_ref[:, :w] += jnp.sum(g[:, :w] * xp[:, :w] + g[:, w:] * xp[:, w:], axis=0, keepdims=True)
            da_ref[:, w:] += jnp.sum(g[:, w:] * xp[:, :w] - g[:, :w] * xp[:, w:], axis=0, keepdims=True)
            gb = g.astype(MXU_DTYPE)
            du_ref[pl.ds(start(ch), rows_c), :] = (lax.dot_general(gb, bb_ref[...], _NT, preferred_element_type=F32)
                                                   + dy_s[pl.ds(start(ch), rows_c), :] * dk_ref[...])
            dbb_ref[...] += lax.dot_general(u_ref[pl.ds(start(ch), rows_c), :].astype(MXU_DTYPE), gb, _TN,
                                            preferred_element_type=F32)

        def second_pair(p, carry):
            c1 = last - 2 * p
            project_dx(c1 - 1, st_b)
            post(buf_b, jnp.minimum(c1 + 1, last))
            carry = back_steps(st_a, carry, buf_a)
            project_dx(jnp.maximum(c1 - 2, 0), st_a)
            post(buf_a, c1)
            return back_steps(st_b, carry, buf_b)

        project_dx(last, st_a)
        buf_b[...] = jnp.zeros_like(buf_b)
        lax.fori_loop(0, n_chunks // 2, second_pair, (gr0, gi0))
        post(buf_b, 0)

    col = lambda width: pl.BlockSpec((t, width), lambda c, j: (0, c))
    wgt = pl.BlockSpec((SCAN_CH, 2 * w), lambda c, j: (c, 0))
    row = pl.BlockSpec((1, 2 * w), lambda c, j: (0, c))
    chan = pl.BlockSpec((1, SCAN_CH), lambda c, j: (0, c))
    return _grid_call(
        body, "ssm_bwd", (SCAN_NBLK, 1), [u_perm, dyg, ytot, dskip, a_cat, bbc, cc, ein],
        [col(SCAN_CH), col(SCAN_CH), col(SCAN_CH), chan, row, wgt, wgt,
         pl.BlockSpec((n_rows, 2 * w), lambda c, j: (0, c))],
        [col(SCAN_CH), chan, row, wgt, wgt],
        [jax.ShapeDtypeStruct((t, SSM_W), F32), jax.ShapeDtypeStruct((1, SSM_W), F32),
         jax.ShapeDtypeStruct((1, 2 * N_STATE), F32), jax.ShapeDtypeStruct((SSM_W, 2 * w), F32),
         jax.ShapeDtypeStruct((SSM_W, 2 * w), F32)],
        56 * 1024 * 1024, comm,
        scratch=[pltpu.VMEM((t + n_rows, 2 * w), F32), pltpu.VMEM((t, SCAN_CH), F32)]
        + [pltpu.VMEM((rows_c, 2 * w), F32)] * 4)


def _to_scan_rows(a, n_samples):
    c = a.shape[1]
    return a.reshape(n_samples, SCAN_SEG_PER_SAMPLE, SCAN_LEN, c).transpose(2, 0, 1, 3).reshape(-1, c)


def _from_scan_rows(a, n_samples):
    c = a.shape[1]
    return a.reshape(SCAN_LEN, n_samples, SCAN_SEG_PER_SAMPLE, c).transpose(1, 2, 0, 3).reshape(-1, c)


def _row_spec(tm, width):
    return pl.BlockSpec((tm, width), lambda i, j: (i, 0))


def _whole(arr):
    return pl.BlockSpec(arr.shape, lambda i, j: (0,) * arr.ndim)


def _proj_rope(x, g, w_in_t, tabs, comm=None):
    t = x.shape[0]
    tm = 256

    def body(x_ref, g_ref, w_ref, tc_ref, tlo_ref, thi_ref, h_ref, u_ref, gate_ref, *rest):
        qkv_refs, stage = rest[:9], rest[9]
        xv = x_ref[...]
        r = lax.rsqrt(jnp.mean(xv * xv, axis=-1, keepdims=True) + RMS_EPS)
        h = ((xv * r) * g_ref[...]).astype(BF16)
        h_ref[...] = h
        p = lax.dot_general(h.astype(MXU_DTYPE), w_ref[...], _NT, preferred_element_type=F32)
        u_ref[...] = p[:, QKV_W:QKV_W + SSM_W]
        gate_ref[...] = _sigmoid(p[:, QKV_W + SSM_W:])
        tc, tlo, thi = tc_ref[...], tlo_ref[...], thi_ref[...]
        n_ch = QKV_W // LANES
        for ch in range(n_ch):
            piece = p[:, _lane_chunk(ch)]
            stage[ch] = _rope_apply(piece, tc, tlo, thi) if ch < 2 * n_ch // 3 else piece
        halves = GROUP_W // LANES
        for grp, d in enumerate(DILATIONS):
            for which in range(3):
                out = qkv_refs[3 * grp + which]
                for res in range(d):
                    for half in range(halves):
                        ch = which * (n_ch // 3) + grp * halves + half
                        out[:, _lane_chunk(res * halves + half)] = _gather_residue(stage, ch, res, d, tm // d).astype(BF16)

    tab = pl.BlockSpec((tm, LANES), lambda i, j: (i % (SEQ // tm), 0))
    widths = [(D_MODEL, BF16), (SSM_W, F32), (2 * D_MODEL, F32)]
    out_specs = [_row_spec(tm, wd) for wd, _ in widths]
    out_shapes = [jax.ShapeDtypeStruct((t, wd), dt) for wd, dt in widths]
    for d in DILATIONS:
        out_specs += [_row_spec(tm // d, d * GROUP_W)] * 3
        out_shapes += [jax.ShapeDtypeStruct((t // d, d * GROUP_W), BF16)] * 3
    return _grid_call(
        body, "proj_rope", (t // tm, 1), [x, g, w_in_t, *tabs],
        [_row_spec(tm, D_MODEL), _whole(g), _whole(w_in_t), tab, tab, tab], out_specs, out_shapes, VMEM_BIG, comm,
        scratch=[pltpu.VMEM((QKV_W // LANES, tm, LANES), F32)])


def _branch_outputs(attn_ref, yg_ref, wao_ref, wglu_ref):
    attn_d = lax.dot_general(attn_ref[...].astype(MXU_DTYPE), wao_ref[...], _NT, preferred_element_type=F32)
    z = lax.dot_general(yg_ref[...].astype(MXU_DTYPE), wglu_ref[...], _NT, preferred_element_type=F32)
    return attn_d, z[:, :D_MODEL], _sigmoid(z[:, D_MODEL:])


def _mix_out_rms(os_, lses, yg, gates, x, w_ao_t, w_glu_t, w_out, g, comm=None):
    t = x.shape[0]
    tm = 256

    def body(o0, o1, o2, l0, l1, l2, yg_ref, gate_ref, x_ref, wao_ref, wglu_ref, wout_ref, g_ref,
             attn_ref, lt_ref, m_ref, x1_ref, h_ref, nat):
        _merge_groups((o0, o1, o2), (l0, l1, l2), attn_ref, lt_ref, nat, tm)
        attn_d, za, sb = _branch_outputs(attn_ref, yg_ref, wao_ref, wglu_ref)
        merged = (gate_ref[:, :D_MODEL] * attn_d + gate_ref[:, D_MODEL:] * (za * sb)).astype(BF16)
        m_ref[...] = merged
        x1 = x_ref[...] + jnp.dot(merged.astype(MXU_DTYPE), wout_ref[...], preferred_element_type=F32)
        x1_ref[...] = x1
        r = lax.rsqrt(jnp.mean(x1 * x1, axis=-1, keepdims=True) + RMS_EPS)
        h_ref[...] = ((x1 * r) * g_ref[...]).astype(BF16)

    dil_specs = [_row_spec(tm // d, d * GROUP_W) for d in DILATIONS] * 2
    return _grid_call(
        body, "mix_out_rms", (t // tm, 1), [*os_, *lses, yg, gates, x, w_ao_t, w_glu_t, w_out, g],
        dil_specs + [_row_spec(tm, SSM_W), _row_spec(tm, 2 * D_MODEL), _row_spec(tm, D_MODEL),
                     _whole(w_ao_t), _whole(w_glu_t), _whole(w_out), _whole(g)],
        [_row_spec(tm, GROUP_W)] * 2 + [_row_spec(tm, D_MODEL)] * 3,
        [jax.ShapeDtypeStruct((t, GROUP_W), F32)] * 2
        + [jax.ShapeDtypeStruct((t, D_MODEL), BF16), jax.ShapeDtypeStruct((t, D_MODEL), F32),
           jax.ShapeDtypeStruct((t, D_MODEL), BF16)], VMEM_BIG, comm, scratch=[pltpu.VMEM((8, tm, LANES), F32)])


def _mix_bwd(dx1b, attn, lse_tot, yg, gates, w_ao_t, w_glu_t, w_out, comm=None):
    t = dx1b.shape[0]
    tm = 256

    def body(dx_ref, attn_ref, lt_ref, yg_ref, gate_ref, wao_ref, wglu_ref, wout_ref, ones_ref,
             dad_ref, dz_ref, dg_ref, da_ref, dyg_ref, rd_ref, *rest):
        dm = lax.dot_general(dx_ref[...], wout_ref[...], _NT, preferred_element_type=F32)
        attn_d, za, sb = _branch_outputs(attn_ref, yg_ref, wao_ref, wglu_ref)
        g0, g1 = gate_ref[:, :D_MODEL], gate_ref[:, D_MODEL:]
        dad = (dm * g0).astype(BF16)
        dad_ref[...] = dad
        ds = dm * g1
        dza, dzb = (ds * sb).astype(BF16), (ds * za * sb * (1.0 - sb)).astype(BF16)
        dz_ref[:, :D_MODEL] = dza
        dz_ref[:, D_MODEL:] = dzb
        dg_ref[:, :D_MODEL] = (dm * attn_d * g0 * (1.0 - g0)).astype(BF16)
        dg_ref[:, D_MODEL:] = (dm * (za * sb) * g1 * (1.0 - g1)).astype(BF16)
        da = jnp.dot(dad.astype(MXU_DTYPE), wao_ref[...], preferred_element_type=F32)
        da_ref[...] = da
        dyg_ref[...] = (jnp.dot(dza.astype(MXU_DTYPE), wglu_ref[:D_MODEL, :], preferred_element_type=F32)
                        + jnp.dot(dzb.astype(MXU_DTYPE), wglu_ref[D_MODEL:, :], preferred_element_type=F32))
        _attention_cotangents(da, attn_ref[...], lt_ref[...], ones_ref[...], rd_ref, rest[:6], rest[6], tm)

    widths = [(D_MODEL, BF16), (2 * D_MODEL, BF16), (2 * D_MODEL, BF16), (GROUP_W, F32), (SSM_W, F32), (GROUP_W, F32)]
    out_specs = [_row_spec(tm, wd) for wd, _ in widths]
    out_shapes = [jax.ShapeDtypeStruct((t, wd), dt) for wd, dt in widths]
    for d in DILATIONS[1:]:
        out_specs += [_row_spec(tm // d, d * GROUP_W)] * 3
        out_shapes += [jax.ShapeDtypeStruct((t // d, d * GROUP_W), F32)] * 3
    ones = _head_sum_matrix()
    return _grid_call(
        body, "mix_bwd", (t // tm, 1), [dx1b, attn, lse_tot, yg, gates, w_ao_t, w_glu_t, w_out, ones],
        [_row_spec(tm, D_MODEL), _row_spec(tm, GROUP_W), _row_spec(tm, GROUP_W), _row_spec(tm, SSM_W),
         _row_spec(tm, 2 * D_MODEL), _whole(w_ao_t), _whole(w_glu_t), _whole(w_out), _whole(ones)],
        out_specs, out_shapes, VMEM_BIG, comm, scratch=[pltpu.VMEM((6, tm, LANES), F32)])


FFN_TN = D_FF // 2
MXU_COLS = 256


def _ffn_in_swiglu(h2, w_gate_t, w_up_t, comm=None):
    t = h2.shape[0]
    tm = 512

    def body(h_ref, wg_ref, wu_ref, a_ref, b_ref, f_ref):
        h = h_ref[...].astype(MXU_DTYPE)
        for c0 in range(0, FFN_TN, MXU_COLS):
            sl = slice(c0, min(c0 + MXU_COLS, FFN_TN))
            a = lax.dot_general(h, wg_ref[sl, :], _NT, preferred_element_type=F32)
            b = lax.dot_general(h, wu_ref[sl, :], _NT, preferred_element_type=F32)
            a_ref[:, sl] = a
            b_ref[:, sl] = b
            f_ref[:, sl] = (a * _sigmoid(a) * b).astype(BF16)

    tile = pl.BlockSpec((tm, FFN_TN), lambda j, i: (i, j))
    wspec = pl.BlockSpec((FFN_TN, D_MODEL), lambda j, i: (j, 0))
    return _grid_call(
        body, "ffn_in_swiglu", (D_FF // FFN_TN, t // tm), [h2, w_gate_t, w_up_t],
        [pl.BlockSpec((tm, D_MODEL), lambda j, i: (i, 0)), wspec, wspec],
        [tile] * 3, [jax.ShapeDtypeStruct((t, D_FF), F32)] * 2 + [jax.ShapeDtypeStruct((t, D_FF), BF16)], VMEM_BIG, comm)


def _ffn_down_final(f, w_down, x1, target, g):
    t = x1.shape[0]
    tm = 256

    def body(f_ref, w_ref, x1_ref, t_ref, g_ref, dx_ref, dxb_ref, loss_ref, gg_ref):
        @pl.when(pl.program_id(0) == 0)
        def _():
            loss_ref[...] = jnp.zeros_like(loss_ref)
            gg_ref[...] = jnp.zeros_like(gg_ref)

        xv = x1_ref[...] + jnp.dot(f_ref[...].astype(MXU_DTYPE), w_ref[...], preferred_element_type=F32)
        gv = g_ref[...]
        r = lax.rsqrt(jnp.mean(xv * xv, axis=-1, keepdims=True) + RMS_EPS)
        n = xv * r
        diff = n * gv - t_ref[...]
        per_tok = jnp.mean(diff * diff, axis=-1, keepdims=True)
        loss_ref[...] += 0.5 * jnp.sum(per_tok, axis=0, keepdims=True)
        dy = diff / xv.shape[-1]
        gg_ref[...] += jnp.sum(dy * n, axis=0, keepdims=True)
        dn = dy * gv
        dx = r * (dn - n * jnp.mean(dn * n, axis=-1, keepdims=True))
        dx_ref[...] = dx
        dxb_ref[...] = dx.astype(BF16)

    acc = lambda shp: pl.BlockSpec(shp, lambda i, j: (0, 0))
    return _grid_call(
        body, "ffn_down_final", (t // tm, 1), [f, w_down, x1, target, g],
        [_row_spec(tm, D_FF), _whole(w_down), _row_spec(tm, D_MODEL), _row_spec(tm, D_MODEL), _whole(g)],
        [_row_spec(tm, D_MODEL)] * 2 + [acc((8, LANES)), acc((1, D_MODEL))],
        [jax.ShapeDtypeStruct((t, D_MODEL), F32), jax.ShapeDtypeStruct((t, D_MODEL), BF16),
         jax.ShapeDtypeStruct((8, LANES), F32), jax.ShapeDtypeStruct((1, D_MODEL), F32)], VMEM_BIG, sequential=True)


def _d_f_swiglu_bwd(dx2b, w_down, a, b):
    t = a.shape[0]
    tm = 512

    def body(dx_ref, w_ref, a_ref, b_ref, da_ref, db_ref):
        d = lax.dot_general(dx_ref[...], w_ref[...], _NT, preferred_element_type=F32)
        av, bv = a_ref[...], b_ref[...]
        sg = _sigmoid(av)
        da_ref[...] = (d * bv * sg * (1.0 + av * (1.0 - sg))).astype(BF16)
        db_ref[...] = (d * av * sg).astype(BF16)

    tile = pl.BlockSpec((tm, FFN_TN), lambda j, i: (i, j))
    return _grid_call(
        body, "d_f_swiglu_bwd", (D_FF // FFN_TN, t // tm), [dx2b, w_down, a, b],
        [pl.BlockSpec((tm, D_MODEL), lambda j, i: (i, 0)), pl.BlockSpec((FFN_TN, D_MODEL), lambda j, i: (j, 0)), tile, tile],
        [tile] * 2, [jax.ShapeDtypeStruct((t, D_FF), BF16)] * 2, VMEM_BIG)


def _ffn_weight_grads(da, db, h2):
    t = h2.shape[0]
    tm = 256
    half = D_MODEL // 2

    def body(da_ref, db_ref, h_ref, g0_ref, g1_ref, u0_ref, u1_ref):
        h = h_ref[...].astype(MXU_DTYPE)
        for src, (lo_ref, hi_ref) in ((da_ref, (g0_ref, g1_ref)), (db_ref, (u0_ref, u1_ref))):
            prod = lax.dot_general(src[...].astype(MXU_DTYPE), h, _TN, preferred_element_type=F32)
            lo_ref[...] = prod[:, :half].astype(BF16)
            hi_ref[...] = prod[:, half:].astype(BF16)

    col = pl.BlockSpec((t, tm), lambda i, j: (0, i))
    out = pl.BlockSpec((tm, half), lambda i, j: (i, 0))
    return _grid_call(body, "mm_g_gate_up", (D_FF // tm, 1), [da, db, h2], [col, col, _whole(h2)], [out] * 4,
                      [jax.ShapeDtypeStruct((D_FF, half), BF16)] * 4, VMEM_BIG)


def _mm_rms_bwd(operands, weights, x, g, dres, name, comm=None):
    t = x.shape[0]
    tm = 256
    n_op = len(operands)

    def body(*refs):
        a_refs, w_refs = refs[:n_op], refs[n_op:2 * n_op]
        x_ref, g_ref, dres_ref, dx_ref, dxb_ref, gg_ref = refs[2 * n_op:]

        @pl.when(pl.program_id(0) == 0)
        def _():
            gg_ref[...] = jnp.zeros_like(gg_ref)

        dh = None
        for a_ref, w_ref in zip(a_refs, w_refs):
            part = jnp.dot(a_ref[...].astype(MXU_DTYPE), w_ref[...], preferred_element_type=F32)
            dh = part if dh is None else dh + part
        xv = x_ref[...]
        r = lax.rsqrt(jnp.mean(xv * xv, axis=-1, keepdims=True) + RMS_EPS)
        n = xv * r
        gg_ref[...] += jnp.sum(dh * n, axis=0, keepdims=True)
        dn = dh * g_ref[...]
        dx = dres_ref[...] + r * (dn - n * jnp.mean(dn * n, axis=-1, keepdims=True))
        dx_ref[...] = dx
        dxb_ref[...] = dx.astype(BF16)

    d = x.shape[1]
    return _grid_call(
        body, name, (t // tm, 1), [*operands, *weights, x, g, dres],
        [_row_spec(tm, a.shape[1]) for a in operands] + [_whole(wk) for wk in weights]
        + [_row_spec(tm, d), _whole(g), _row_spec(tm, d)],
        [_row_spec(tm, d)] * 2 + [pl.BlockSpec((1, d), lambda i, j: (0, 0))],
        [jax.ShapeDtypeStruct((t, d), F32), jax.ShapeDtypeStruct((t, d), BF16), jax.ShapeDtypeStruct((1, d), F32)],
        VMEM_BIG, comm, sequential=True)


def _flat_small(small):
    perm_b = lambda a: a.reshape(SSM_GROUPS, SSM_STATE, SSM_CH).transpose(2, 0, 1).reshape(SSM_CH, N_STATE)
    perm_c = lambda a: a.reshape(SSM_GROUPS, SSM_CH, SSM_STATE).transpose(1, 0, 2).reshape(SSM_CH, N_STATE)
    return dict(
        g_mix=small["norm_mix_g"].reshape(1, D_MODEL), g_ffn=small["norm_ffn_g"].reshape(1, D_MODEL),
        g_fin=small["norm_final_g"].reshape(1, D_MODEL),
        lr=small["ssm_a_re"].reshape(1, N_STATE), li=small["ssm_a_im"].reshape(1, N_STATE),
        ldt=jnp.repeat(small["ssm_log_dt"].reshape(SSM_GROUPS), SSM_STATE).reshape(1, N_STATE),
        br=perm_b(small["ssm_b_re"]), bi=perm_b(small["ssm_b_im"]),
        cr=perm_c(small["ssm_c_re"]), ci=perm_c(small["ssm_c_im"]), dskip=small["ssm_d"].reshape(1, SSM_W))


AG_HOSTS = {"proj_rope": ("w_glu", "w_attn_out", "w_out", "w_ffn_gate"), "mix_out_rms": ("w_ffn_up",),
            "ffn_in_swiglu": ("w_ffn_down",)}
HALVED = ("w_ffn_gate", "w_ffn_up", "w_in")
A2A_HOSTS = {"d_h2_rms": ("w_ffn_down",), "mix_bwd": ("w_ffn_gate:0", "w_out"), "attn_bwd_g1": ("w_glu",),
             "attn_bwd_g2": ("w_attn_out",), "ssm_bwd": ("w_ffn_gate:1", "w_ffn_up:0", "w_ffn_up:1"),
             "mm_g_in1": ("w_in:0",), "d_h0_rms": ("w_in:1",)}
SMALL_HOST = "mm_g_in0"


def _local_step(x, target, w, small, shards=None):
    t = x.shape[0]
    n_samples = t // SEQ
    n_rows = n_samples * SCAN_SEG_PER_SAMPLE
    tabs = _rope_tables()
    w = dict(w)
    fs = _flat_small(small)
    g_mix, g_ffn, g_fin, dskip = fs["g_mix"], fs["g_ffn"], fs["g_fin"], fs["dskip"]
    a_cat, bbc, cc = _ssm_disc(fs["lr"], fs["li"], fs["ldt"], fs["br"], fs["bi"], fs["cr"], fs["ci"])
    big, recv, small_pack = {}, {}, []

    def comm_of(name):
        if shards is None:
            return None
        if name == SMALL_HOST:
            return _ag_comm([(small_pack[0], 0, 0)], [(N_DEV, *small_pack[0].shape)])
        if name in AG_HOSTS:
            names = AG_HOSTS[name]
            return _ag_comm([(shards[n], j, 0) for j, n in enumerate(names)], [(N_DEV, *shards[n].shape) for n in names])
        if name in A2A_HOSTS:
            return _a2a_comm([(big[n].reshape(N_DEV, -1, big[n].shape[1]), 0) for n in A2A_HOSTS[name]])
        return None

    def absorb(name, carried):
        if name == SMALL_HOST:
            recv["small"] = carried[0]
        for n, a3 in zip(AG_HOSTS.get(name, ()), carried):
            w[n] = a3.reshape(-1, a3.shape[2])
        for n, a3 in zip(A2A_HOSTS.get(name, ()), carried):
            recv[n] = a3

    def mm(a, b, mode, name, tm, tn, **kw):
        comm = comm_of(name)
        if comm is None:
            return _mm(a, b, mode, name, tm, tn, **kw)
        out, *carried = _mm(a, b, mode, name, tm, tn, comm=comm, **kw)
        absorb(name, carried)
        return out

    h0, u, gates, *rest = _proj_rope(x, g_mix, w["w_in"], tabs, comm_of("proj_rope"))
    qkv = [rest[3 * g:3 * g + 3] for g in range(3)]
    absorb("proj_rope", rest[9:])
    os_, lses = [], []
    for g in range(3):
        o_g, l_g, carried = _attn_fwd(*qkv[g], g, n_samples, comm_of(f"attn_fwd_g{g}"))
        absorb(f"attn_fwd_g{g}", carried)
        os_.append(o_g)
        lses.append(l_g)
    u_perm = _to_scan_rows(u, n_samples)
    ytot, yg_perm, ein = _ssm_fwd(u_perm, a_cat, bbc, cc, dskip, n_rows)
    yg = _from_scan_rows(yg_perm, n_samples)

    attn, lse_tot, merged, x1, h2, *carried = _mix_out_rms(os_, lses, yg, gates, x, w["w_attn_out"], w["w_glu"], w["w_out"],
                                                           g_ffn, comm_of("mix_out_rms"))
    absorb("mix_out_rms", carried)
    ffn_a, ffn_b, f, *carried = _ffn_in_swiglu(h2, w["w_ffn_gate"], w["w_ffn_up"], comm_of("ffn_in_swiglu"))
    absorb("ffn_in_swiglu", carried)
    dx2, dx2b, loss_blk, g_gfin = _ffn_down_final(f, w["w_ffn_down"], x1, target, g_fin)

    da, db = _d_f_swiglu_bwd(dx2b, w["w_ffn_down"], ffn_a, ffn_b)
    big["w_ffn_down"] = mm(f, dx2b, "tn", "mm_g_down", 256, D_MODEL, out_dtype=BF16)
    half = D_MODEL // 2
    big["w_ffn_gate:0"], big["w_ffn_gate:1"], big["w_ffn_up:0"], big["w_ffn_up:1"] = _ffn_weight_grads(da, db, h2)
    dx1, dx1b, g_gffn, *carried = _mm_rms_bwd([da, db], [w["w_ffn_gate"], w["w_ffn_up"]], x1, g_ffn, dx2, "d_h2_rms",
                                              comm_of("d_h2_rms"))
    absorb("d_h2_rms", carried)

    big["w_out"] = mm(merged, dx1b, "tn", "mm_g_out", 256, D_MODEL, out_dtype=BF16)
    dattn_d, dz, dgpre, dattn, dyg, rowdot, *rest = _mix_bwd(dx1b, attn, lse_tot, yg, gates, w["w_attn_out"], w["w_glu"],
                                                             w["w_out"], comm_of("mix_bwd"))
    cot = [(dattn, lse_tot, rowdot), tuple(rest[:3]), tuple(rest[3:6])]
    absorb("mix_bwd", rest[6:])

    big["w_attn_out"] = mm(dattn_d, attn, "tn", "mm_g_attn_out", 512, GROUP_W, out_dtype=BF16)
    big["w_glu"] = mm(dz, yg, "tn", "mm_g_glu", 512, 512, out_dtype=BF16)
    dqs, dks, dvs = [], [], []
    for g in range(3):
        dq_g, dk_g, dv_g, carried = _attn_bwd(*qkv[g], *cot[g], g, n_samples, comm_of(f"attn_bwd_g{g}"))
        absorb(f"attn_bwd_g{g}", carried)
        dqs.append(dq_g)
        dks.append(dk_g)
        dvs.append(dv_g)

    dyg_perm = _to_scan_rows(dyg, n_samples)
    du_perm, g_dskip, da_cat, dbb_full, dc_full, *carried = _ssm_bwd(u_perm, dyg_perm, ytot, dskip, a_cat, bbc, cc, ein,
                                                                   n_rows, comm_of("ssm_bwd"))
    absorb("ssm_bwd", carried)
    du = _from_scan_rows(du_perm, n_samples)
    g_lr, g_li, g_ldt, g_br, g_bi, g_cr, g_ci = _ssm_param_bwd(
        fs["lr"], fs["li"], fs["ldt"], fs["br"], fs["bi"], da_cat, dbb_full, dc_full)

    small_pack.append(_pack_small(dict(lr=g_lr, li=g_li, ldt=g_ldt, br=g_br, bi=g_bi, cr=g_cr, ci=g_ci, dskip=g_dskip,
                                       g_ffn=g_gffn, g_fin=g_gfin, loss=loss_blk)))

    dproj = _pack_dproj(dqs, dks, dvs, du, dgpre, tabs)
    for hf in range(2):
        big[f"w_in:{hf}"] = mm(dproj, h0, "tn", f"mm_g_in{hf}", 256, half, out_dtype=BF16, cols=(hf * half, half))
    grad_x, _, g_gmix, *carried = _mm_rms_bwd([dproj], [w["w_in"]], x, g_mix, dx1, "d_h0_rms", comm_of("d_h0_rms"))
    absorb("d_h0_rms", carried)
    return grad_x, (big if shards is None else recv), small_pack[0], g_gmix


_MESH = pl.DeviceIdType.MESH


def _all_gather(block, name):
    rows, lanes = block.shape

    def body(x_ref, out_ref, send_sems, recv_sems, local_sem):
        x, y, c = lax.axis_index("x"), lax.axis_index("y"), lax.axis_index("c")
        me, sibling = (x, y, c), (x, y, 1 - c)
        chips = [(1 - x, y), (x, 1 - y), (1 - x, 1 - y)]

        def slot(px, py, pc):
            return out_ref.at[4 * px + 2 * py + pc]

        def copy(k, blk, to, src=None):
            return pltpu.make_async_remote_copy(
                src_ref=slot(*blk) if src is None else src, dst_ref=slot(*blk), send_sem=send_sems.at[k],
                recv_sem=recv_sems.at[k], device_id=to, device_id_type=_MESH)

        mine = pltpu.make_async_copy(x_ref, slot(*me), local_sem)
        mine.start()
        first = [copy(0, me, sibling, src=x_ref)]
        first += [copy(1 + j, me, (*chip, c), src=x_ref) for j, chip in enumerate(chips)]
        for cp in first:
            cp.start()
        passed = [copy(4 + j, (*chip, c), sibling) for j, chip in enumerate(chips)]
        for j, chip in enumerate(chips):
            copy(1 + j, (*chip, c), me).wait_recv()
            passed[j].start()
        copy(0, sibling, me).wait_recv()
        for j, chip in enumerate(chips):
            copy(4 + j, (*chip, 1 - c), me).wait_recv()
        for cp in first + passed:
            cp.wait_send()
        mine.wait()

    return _pallas_call(
        body, name=name, out_shape=jax.ShapeDtypeStruct((N_DEV, rows, lanes), block.dtype),
        in_specs=[pl.BlockSpec(memory_space=pl.ANY)], out_specs=pl.BlockSpec(memory_space=pl.ANY),
        scratch_shapes=[pltpu.SemaphoreType.DMA((7,)), pltpu.SemaphoreType.DMA((7,)), pltpu.SemaphoreType.DMA],
    )(block)


def _ag_comm(items, bufs):
    def plan(in_refs, out_refs, send_sems, recv_sems, local_sems):
        x, y, c = lax.axis_index("x"), lax.axis_index("y"), lax.axis_index("c")
        me, sibling = (x, y, c), (x, y, 1 - c)
        chips = [(1 - x, y), (x, 1 - y), (1 - x, 1 - y)]
        plans = []
        for t, (_, buf, slot0) in enumerate(items):
            x_ref, out_ref = in_refs[t], out_refs[buf]

            def slot(px, py, pc, out_ref=out_ref, slot0=slot0):
                return out_ref.at[slot0 + 4 * px + 2 * py + pc]

            def copy(k, blk, to, src=None, t=t, slot=slot):
                return pltpu.make_async_remote_copy(
                    src_ref=slot(*blk) if src is None else src, dst_ref=slot(*blk), send_sem=send_sems.at[7 * t + k],
                    recv_sem=recv_sems.at[7 * t + k], device_id=to, device_id_type=_MESH)

            plans.append(dict(
                mine=pltpu.make_async_copy(x_ref, slot(*me), local_sems.at[t]),
                first=[copy(0, me, sibling, src=x_ref)] + [copy(1 + j, me, (*chip, c), src=x_ref)
                                                           for j, chip in enumerate(chips)],
                passed=[copy(4 + j, (*chip, c), sibling) for j, chip in enumerate(chips)],
                from_ici=[copy(1 + j, (*chip, c), me) for j, chip in enumerate(chips)],
                from_sibling=[copy(0, sibling, me)] + [copy(4 + j, (*chip, 1 - c), me) for j, chip in enumerate(chips)]))
        return plans

    def start(*refs):
        for p in plan(*refs):
            p["mine"].start()
            for cp in p["first"]:
                cp.start()

    def finish(*refs):
        plans = plan(*refs)
        for p in plans:
            for arrived, onward in zip(p["from_ici"], p["passed"]):
                arrived.wait_recv()
                onward.start()
        for p in plans:
            for arrived in p["from_sibling"]:
                arrived.wait_recv()
            for cp in p["first"] + p["passed"]:
                cp.wait_send()
            p["mine"].wait()

    dtype_of = {buf: shard.dtype for shard, buf, _ in items}
    out_shapes = [jax.ShapeDtypeStruct(b, dtype_of[j]) for j, b in enumerate(bufs)]
    return _Comm([it[0] for it in items], out_shapes, 7 * len(items), len(items), start, finish)


def _a2a_comm(items):
    def plan(in_refs, out_refs, send_sems, recv_sems, local_sems):
        x, y, c = lax.axis_index("x"), lax.axis_index("y"), lax.axis_index("c")
        my = 4 * x + 2 * y + c
        copies, locals_ = [], []
        for t, (_, slot0) in enumerate(items):
            s_ref, r_ref = in_refs[t], out_refs[t]
            locals_.append(pltpu.make_async_copy(s_ref.at[slot0 + my], r_ref.at[my], local_sems.at[t]))
            for kk in range(1, N_DEV):
                px = 1 - x if kk & 4 else x
                py = 1 - y if kk & 2 else y
                pc = 1 - c if kk & 1 else c
                copies.append(pltpu.make_async_remote_copy(
                    src_ref=s_ref.at[slot0 + 4 * px + 2 * py + pc], dst_ref=r_ref.at[my],
                    send_sem=send_sems.at[7 * t + kk - 1], recv_sem=recv_sems.at[7 * t + kk - 1],
                    device_id=(px, py, pc), device_id_type=_MESH))
        return copies, locals_

    def start(*refs):
        copies, locals_ = plan(*refs)
        for cp in locals_ + copies:
            cp.start()

    def finish(*refs):
        copies, locals_ = plan(*refs)
        for cp in copies + locals_:
            cp.wait()

    out_shapes = [jax.ShapeDtypeStruct((N_DEV,) + it[0].shape[1:], it[0].dtype) for it in items]
    return _Comm([it[0] for it in items], out_shapes, 7 * len(items), len(items), start, finish)


def _adam_math(g, w, m, v):
    m_new = ADAM_B1 * m + (1.0 - ADAM_B1) * g
    v_new = ADAM_B2 * v + (1.0 - ADAM_B2) * jnp.square(g)
    m_hat = m_new / (1.0 - ADAM_B1 ** ADAM_STEP)
    v_hat = v_new / (1.0 - ADAM_B2 ** ADAM_STEP)
    return -ADAM_LR * (m_hat / (jnp.sqrt(v_hat) + ADAM_EPS) + ADAM_WD * w), m_new, v_new


def _sum_partials(parts, name, tm):
    n, rows, _ = parts[0].shape
    widths = [p.shape[2] for p in parts]

    def body(*refs):
        g_ref, off = refs[-1], 0
        for p_ref, wd in zip(refs[:-1], widths):
            g = p_ref[0].astype(F32)
            for s in range(1, n):
                g = g + p_ref[s].astype(F32)
            g_ref[:, off:off + wd] = g
            off += wd

    return _pallas_call(
        body, name=name, grid=(rows // tm,), in_specs=[pl.BlockSpec((n, tm, wd), lambda i: (0, i, 0)) for wd in widths],
        out_specs=pl.BlockSpec((tm, sum(widths)), lambda i: (i, 0)),
        out_shape=jax.ShapeDtypeStruct((rows, sum(widths)), F32),
        compiler_params=pltpu.CompilerParams(dimension_semantics=("parallel",), vmem_limit_bytes=VMEM_MID),
    )(*parts)


def _adam(parts, w, m, v, name, tm):
    n, rows, _ = parts[0].shape
    widths = [p.shape[2] for p in parts]
    cols = sum(widths)

    def body(*refs):
        p_refs, (w_ref, m_ref, v_ref, g_ref, d_ref, nm_ref, nv_ref) = refs[:len(parts)], refs[len(parts):]
        off = 0
        for p_ref, wd in zip(p_refs, widths):
            g = p_ref[0].astype(F32)
            for s in range(1, n):
                g = g + p_ref[s].astype(F32)
            sl = slice(off, off + wd)
            g_ref[:, sl] = g
            d_ref[:, sl], nm_ref[:, sl], nv_ref[:, sl] = _adam_math(g, w_ref[:, sl], m_ref[:, sl], v_ref[:, sl])
            off += wd

    assert rows % tm == 0
    row = pl.BlockSpec((tm, cols), lambda i: (i, 0))
    shp = jax.ShapeDtypeStruct((rows, cols), F32)
    return _pallas_call(
        body, name=name, grid=(rows // tm,),
        in_specs=[pl.BlockSpec((n, tm, wd), lambda i: (0, i, 0)) for wd in widths] + [row, row, row],
        out_specs=[row] * 4, out_shape=[shp] * 4,
        compiler_params=pltpu.CompilerParams(dimension_semantics=("parallel",), vmem_limit_bytes=VMEM_MID),
    )(*parts, w, m, v)


_PK_LR, _PK_LI, _PK_GAINS, _PK_MISC, _PK_BR, _PK_BI, _PK_CR, _PK_CI, _PK_ROWS = 0, 1, 2, 3, 8, 24, 40, 56, 72
_PK_LDT_LANE, _PK_LOSS_LANE = D_MODEL + SSM_W, D_MODEL + SSM_W + LANES


def _pack_small(sg):
    names = ("lr", "li", "g_ffn", "g_fin", "dskip", "ldt", "loss", "br", "bi", "cr", "ci")

    def body(lr, li, gffn, gfin, dskip, ldt, loss, br, bi, cr, ci, o_ref):
        o_ref[...] = jnp.zeros_like(o_ref)
        o_ref[_PK_LR:_PK_LR + 1, :] = lr[...]
        o_ref[_PK_LI:_PK_LI + 1, :] = li[...]
        o_ref[_PK_GAINS:_PK_GAINS + 1, D_MODEL:] = gffn[...]
        o_ref[_PK_MISC:_PK_MISC + 1, :D_MODEL] = gfin[...]
        o_ref[_PK_MISC:_PK_MISC + 1, D_MODEL:D_MODEL + SSM_W] = dskip[...]
        o_ref[_PK_MISC:_PK_MISC + 1, _PK_LDT_LANE:_PK_LDT_LANE + LANES] = ldt[0:1, :]
        o_ref[_PK_MISC:_PK_MISC + 1, _PK_LOSS_LANE:_PK_LOSS_LANE + LANES] = loss[0:1, :]
        o_ref[_PK_BR:_PK_BR + SSM_CH, :] = br[...]
        o_ref[_PK_BI:_PK_BI + SSM_CH, :] = bi[...]
        o_ref[_PK_CR:_PK_CR + SSM_CH, :] = cr[...]
        o_ref[_PK_CI:_PK_CI + SSM_CH, :] = ci[...]

    return _pallas_call(body, name="pack_small", out_shape=jax.ShapeDtypeStruct((_PK_ROWS, N_STATE), F32))(
        *[sg[n] for n in names])


def _unpack_small(s, g_mix):
    unflat_b = unflat_c = lambda a: a.reshape(SSM_CH, SSM_GROUPS, SSM_STATE).transpose(1, 0, 2)[None]
    grads = {
        "norm_mix_g": g_mix, "norm_ffn_g": s[_PK_GAINS, D_MODEL:].reshape(1, D_MODEL),
        "norm_final_g": s[_PK_MISC, :D_MODEL].reshape(1, D_MODEL),
        "ssm_a_re": s[_PK_LR].reshape(1, SSM_GROUPS, SSM_STATE), "ssm_a_im": s[_PK_LI].reshape(1, SSM_GROUPS, SSM_STATE),
        "ssm_log_dt": s[_PK_MISC, _PK_LDT_LANE:_PK_LDT_LANE + SSM_GROUPS].reshape(1, SSM_GROUPS),
        "ssm_d": s[_PK_MISC, D_MODEL:D_MODEL + SSM_W].reshape(1, SSM_GROUPS, SSM_CH),
        "ssm_b_re": unflat_b(s[_PK_BR:_PK_BR + SSM_CH]), "ssm_b_im": unflat_b(s[_PK_BI:_PK_BI + SSM_CH]),
        "ssm_c_re": unflat_c(s[_PK_CR:_PK_CR + SSM_CH]), "ssm_c_im": unflat_c(s[_PK_CI:_PK_CI + SSM_CH]),
    }
    return s[_PK_MISC, _PK_LOSS_LANE], grads


def _stored(name, a):
    if name in ("ssm_b_re", "ssm_b_im"):
        return a.transpose(0, 1, 3, 2)
    return a.reshape(1, -1) if a.ndim == 1 else a


def _unstored(name, a, like):
    return a.transpose(0, 1, 3, 2) if name in ("ssm_b_re", "ssm_b_im") else a.reshape(like.shape)


def _adam_small(grads, wts, moms, vars_):
    n = len(SMALL_WEIGHTS)

    def body(*refs):
        ins, outs = refs[:4 * n], refs[4 * n:]
        for i in range(n):
            g, w, m, v = (ins[j * n + i][...] for j in range(4))
            outs[i][...], outs[n + i][...], outs[2 * n + i][...] = _adam_math(g, w, m, v)

    operands = [grads[k] if d is grads else _stored(k, d[k]) for d in (grads, wts, moms, vars_) for k in SMALL_WEIGHTS]
    shapes = [jax.ShapeDtypeStruct(_stored(k, wts[k]).shape, F32) for k in SMALL_WEIGHTS] * 3
    res = _pallas_call(body, name="adam_small", out_shape=shapes,
                         compiler_params=pltpu.CompilerParams(vmem_limit_bytes=VMEM_BIG))(*operands)
    out = {}
    for j, kind in enumerate(("delta", "new_m", "new_v")):
        for i, k in enumerate(SMALL_WEIGHTS):
            out[kind, k] = _unstored(k, res[j * n + i], wts[k])
    return out


def kernel(x, norm_mix_g, w_in, ssm_a_re, ssm_a_im, ssm_log_dt, ssm_b_re, ssm_b_im, ssm_c_re, ssm_c_im, ssm_d, w_glu, w_attn_out, w_out, norm_ffn_g, w_ffn_gate, w_ffn_up, w_ffn_down, norm_final_g, loss_target, m_norm_mix_g, m_w_in, m_ssm_a_re, m_ssm_a_im, m_ssm_log_dt, m_ssm_b_re, m_ssm_b_im, m_ssm_c_re, m_ssm_c_im, m_ssm_d, m_w_glu, m_w_attn_out, m_w_out, m_norm_ffn_g, m_w_ffn_gate, m_w_ffn_up, m_w_ffn_down, m_norm_final_g, v_norm_mix_g, v_w_in, v_ssm_a_re, v_ssm_a_im, v_ssm_log_dt, v_ssm_b_re, v_ssm_b_im, v_ssm_c_re, v_ssm_c_im, v_ssm_d, v_w_glu, v_w_attn_out, v_w_out, v_norm_ffn_g, v_w_ffn_gate, v_w_ffn_up, v_w_ffn_down, v_norm_final_g):
    args = dict(locals())
    wts = {n: args[n] for n in ALL_WEIGHTS}
    moms = {n: args["m_" + n] for n in ALL_WEIGHTS}
    vars_ = {n: args["v_" + n] for n in ALL_WEIGHTS}
    n_samples = x.shape[0]
    t = n_samples * SEQ

    shards = {n: (wts[n][0] if n in ROW_SHARDED else wts[n][0].T).astype(BF16) for n in BIG_WEIGHTS}
    w_in_t = _all_gather(shards["w_in"], "allgather_w_in").reshape(IN_W, D_MODEL)

    small = {n: wts[n] for n in SMALL_WEIGHTS}
    grad_x, recv, _, g_mix_part = _local_step(x.reshape(t, D_MODEL), loss_target.reshape(t, D_MODEL), {"w_in": w_in_t},
                                              small, shards)

    results = {}
    for n in BIG_WEIGHTS:
        c, k = shards[n].shape
        w2, m2, v2 = wts[n][0], moms[n][0], vars_[n][0]
        if n in ROW_SHARDED:
            res = _adam([recv[n]], w2, m2, v2, "adam_" + n, c // 2)
        elif n in HALVED:
            res = _adam([recv[f"{n}:{hf}"] for hf in range(2)], w2.T, m2.T, v2.T, "adam_" + n, c // 2)
            res = [a.T for a in res]
        else:
            g_t = _sum_partials([recv[n]], "sum_" + n, c // 2)
            res = _adam([g_t.T[None]], w2, m2, v2, "adam_" + n, k // 2)
        for kind, a in zip(("grad", "delta", "new_m", "new_v"), res):
            results[kind, n] = a[None]

    g_mix_all = _all_gather(jnp.pad(g_mix_part, ((0, 7), (0, 0))), "allgather_g_mix")
    g_mix = _sum_partials([g_mix_all], "sum_g_mix", 8)[0:1]
    loss, sgrads = _unpack_small(_sum_partials([recv["small"]], "sum_small", _PK_ROWS), g_mix)
    for n in SMALL_WEIGHTS:
        results["grad", n] = _unstored(n, sgrads[n], wts[n])
    results.update(_adam_small(sgrads, wts, moms, vars_))
    outs = [loss, grad_x.reshape(x.shape)]
    for kind in ("grad", "delta", "new_m", "new_v"):
        outs += [results[kind, n] for n in ALL_WEIGHTS]
    return tuple(outs)
```

```python
import functools
import math

import jax
import jax.numpy as jnp
from jax import lax
from jax.experimental import pallas as pl
from jax.experimental.pallas import tpu as pltpu

F32 = jnp.float32
BF16 = jnp.bfloat16
MXU_DTYPE = jnp.bfloat16

N_DEV = 8
D_MODEL = 1024
SEQ = 2048
HEAD_DIM = 64
HEADS_PER_GROUP = 4
GROUP_W = HEADS_PER_GROUP * HEAD_DIM
DILATIONS = (1, 4, 16)
QKV_W = 3 * len(DILATIONS) * GROUP_W
Q_W = len(DILATIONS) * GROUP_W
ATT_BLOCK = 128
ROPE_DIM = 16
ROPE_THETA = 500000.0
SSM_W = 512
SSM_GROUPS = 32
SSM_CH = 16
SSM_STATE = 64
N_STATE = SSM_GROUPS * SSM_STATE
D_FF = 2816
IN_W = QKV_W + SSM_W + 2 * D_MODEL
RMS_EPS = 1e-6
NEG_INF = -1e30
LANES = 128

SCAN_SEG_PER_SAMPLE = 8
SCAN_LEN = SEQ // SCAN_SEG_PER_SAMPLE
SCAN_WC = 512
SCAN_NBLK = N_STATE // SCAN_WC
SCAN_CH = SSM_W // SCAN_NBLK
SCAN_CHUNK = 32

ADAM_LR = 0.001
ADAM_B1 = 0.9
ADAM_B2 = 0.999
ADAM_EPS = 1e-08
ADAM_WD = 0.01
ADAM_STEP = 10

VMEM_BIG = 48 * 1024 * 1024
VMEM_MID = 32 * 1024 * 1024

BIG_WEIGHTS = ("w_in", "w_glu", "w_attn_out", "w_out", "w_ffn_gate", "w_ffn_up", "w_ffn_down")
ROW_SHARDED = ("w_out", "w_ffn_down")
SMALL_WEIGHTS = ("norm_mix_g", "ssm_a_re", "ssm_a_im", "ssm_log_dt", "ssm_b_re", "ssm_b_im", "ssm_c_re", "ssm_c_im",
                 "ssm_d", "norm_ffn_g", "norm_final_g")
ALL_WEIGHTS = ("norm_mix_g", "w_in", "ssm_a_re", "ssm_a_im", "ssm_log_dt", "ssm_b_re", "ssm_b_im", "ssm_c_re", "ssm_c_im",
               "ssm_d", "w_glu", "w_attn_out", "w_out", "norm_ffn_g", "w_ffn_gate", "w_ffn_up", "w_ffn_down", "norm_final_g")


def _sigmoid(x):
    return 1.0 / (1.0 + jnp.exp(-x))


def _pallas_call(body, *, out_shape, **kw):
    single = not isinstance(out_shape, (list, tuple))
    shapes = [pltpu.HBM(s.shape, s.dtype) for s in ([out_shape] if single else out_shape)]
    call = pl.pallas_call(body, out_shape=shapes[0] if single else shapes, **kw)
    return lambda *operands: call(*[pltpu.with_memory_space_constraint(o, pltpu.HBM) for o in operands])


class _Comm:
    def __init__(self, ins, out_shapes, n_sem, n_local, start, finish):
        self.ins, self.out_shapes, self.n_sem, self.n_local = ins, out_shapes, n_sem, n_local
        self.start, self.finish = start, finish


def _mm(a, b, mode, name, tm, tn, out_dtype=F32, add=None, vmem=VMEM_BIG, comm=None, cols=None):
    if mode == "nn":
        (m, k), (_, n) = a.shape, b.shape
        a_spec = pl.BlockSpec((tm, k), lambda i, j: (i, 0))
        b_spec = pl.BlockSpec((k, tn), lambda i, j: (0, j))
        dims = (((1,), (0,)), ((), ()))
    elif mode == "nt":
        (m, k), (n, _) = a.shape, b.shape
        a_spec = pl.BlockSpec((tm, k), lambda i, j: (i, 0))
        b_spec = pl.BlockSpec((tn, k), lambda i, j: (j, 0))
        dims = (((1,), (1,)), ((), ()))
    else:
        (k, m), (_, n) = a.shape, b.shape
        first, n = cols if cols else (0, n)
        a_spec = pl.BlockSpec((k, tm), lambda i, j: (0, i))
        b_spec = pl.BlockSpec((k, tn), lambda i, j: (0, j + first // tn))
        dims = (((0,), (0,)), ((), ()))
    assert m % tm == 0 and n % tn == 0, (name, m, n, tm, tn)
    o_spec = pl.BlockSpec((tm, tn), lambda i, j: (i, j))
    has_add = add is not None

    def body(*refs):
        a_ref, b_ref, o_ref = refs[0], refs[1], refs[-1]
        acc = lax.dot_general(a_ref[...].astype(MXU_DTYPE), b_ref[...].astype(MXU_DTYPE), dims,
                              preferred_element_type=F32)
        if has_add:
            acc = acc + refs[2][...]
        o_ref[...] = acc.astype(out_dtype)

    ins = [a, b] + ([add] if has_add else [])
    in_specs = [a_spec, b_spec] + ([o_spec] if has_add else [])
    return _grid_call(body, name, (m // tm, n // tn), ins, in_specs, [o_spec],
                      [jax.ShapeDtypeStruct((m, n), out_dtype)], vmem, comm)


def _grid_call(body, name, grid, ins, in_specs, out_specs, out_shapes, vmem, comm=None, sequential=False, scratch=()):
    if comm is None:
        single = len(out_shapes) == 1
        semantics = ("arbitrary", "arbitrary") if sequential else ("parallel", "parallel")
        return _pallas_call(
            body, name=name, grid=grid, in_specs=in_specs, out_specs=out_specs[0] if single else out_specs,
            out_shape=out_shapes[0] if single else out_shapes, scratch_shapes=list(scratch),
            compiler_params=pltpu.CompilerParams(dimension_semantics=semantics, vmem_limit_bytes=vmem),
        )(*ins)
    n_in, n_out, n_cin, n_cout = len(ins), len(out_shapes), len(comm.ins), len(comm.out_shapes)
    n_io = n_in + n_cin + n_out + n_cout

    def carrying(*refs):
        own = refs[:n_in] + refs[n_in + n_cin:n_in + n_cin + n_out] + refs[n_io:len(refs) - 3]
        c_args = (refs[n_in:n_in + n_cin], refs[n_in + n_cin + n_out:n_io], *refs[-3:])

        @pl.when((pl.program_id(0) == 0) & (pl.program_id(1) == 0))
        def _():
            comm.start(*c_args)

        body(*own)

        @pl.when((pl.program_id(0) == grid[0] - 1) & (pl.program_id(1) == grid[1] - 1))
        def _():
            comm.finish(*c_args)

    hbm = pl.BlockSpec(memory_space=pl.ANY)
    return _pallas_call(
        carrying, name=name, grid=grid, in_specs=list(in_specs) + [hbm] * n_cin,
        out_specs=list(out_specs) + [hbm] * n_cout, out_shape=list(out_shapes) + list(comm.out_shapes),
        scratch_shapes=list(scratch) + [pltpu.SemaphoreType.DMA((comm.n_sem,)), pltpu.SemaphoreType.DMA((comm.n_sem,)),
                                        pltpu.SemaphoreType.DMA((comm.n_local,))],
        compiler_params=pltpu.CompilerParams(dimension_semantics=("arbitrary", "arbitrary"), vmem_limit_bytes=vmem),
    )(*ins, *comm.ins)


def _rows(body, name, n_rows, tm, ins, outs, vmem=VMEM_MID, scratch=()):
    assert n_rows % tm == 0
    arrays, in_specs = [], []
    for kind, arr in ins:
        arrays.append(arr)
        if kind == "row":
            assert n_rows % arr.shape[0] == 0, (name, arr.shape)
            in_specs.append(pl.BlockSpec((tm * arr.shape[0] // n_rows, arr.shape[1]), lambda i: (i, 0)))
        elif kind == "tab":
            nblk = arr.shape[0] // tm
            in_specs.append(pl.BlockSpec((tm, arr.shape[1]), lambda i, nblk=nblk: (i % nblk, 0)))
        else:
            in_specs.append(pl.BlockSpec(arr.shape, lambda i, nd=arr.ndim: (0,) * nd))
    out_specs, out_shape = [], []
    for kind, shp, dt in outs:
        if kind == "row":
            out_specs.append(pl.BlockSpec((tm, shp), lambda i: (i, 0)))
            out_shape.append(jax.ShapeDtypeStruct((n_rows, shp), dt))
        elif kind == "dil":
            d, wd = shp
            out_specs.append(pl.BlockSpec((tm // d, d * wd), lambda i: (i, 0)))
            out_shape.append(jax.ShapeDtypeStruct((n_rows // d, d * wd), dt))
        else:
            out_specs.append(pl.BlockSpec(shp, lambda i, nd=len(shp): (0,) * nd))
            out_shape.append(jax.ShapeDtypeStruct(shp, dt))
    res = _pallas_call(
        body, name=name, grid=(n_rows // tm,), in_specs=in_specs, out_specs=out_specs, out_shape=out_shape,
        scratch_shapes=list(scratch),
        compiler_params=pltpu.CompilerParams(dimension_semantics=("arbitrary",), vmem_limit_bytes=vmem),
    )(*arrays)
    return res


def _gather_residue(stage, ch, r, d, n):
    return stage[ch, pl.ds(r, n, stride=d), :] if d > 1 else stage[ch]


def _scatter_residue(stage, ch, r, d, n, val):
    if d > 1:
        stage[ch, pl.ds(r, n, stride=d), :] = val
    else:
        stage[ch] = val


def _lane_chunk(ch):
    return slice(ch * LANES, (ch + 1) * LANES)


def _rope_tables():
    half = ROPE_DIM // 2
    inv = jnp.power(jnp.float32(ROPE_THETA), -jnp.arange(half, dtype=F32) * 2.0 / ROPE_DIM)
    ang = jnp.arange(SEQ, dtype=F32)[:, None] * inv[None, :]
    lane = jnp.arange(LANES) % HEAD_DIM
    cosl = jnp.cos(ang)[:, lane % half]
    sinl = jnp.sin(ang)[:, lane % half]
    tab_c = jnp.where(lane < ROPE_DIM, cosl, 1.0)
    tab_lo = jnp.where(lane < half, -sinl, 0.0)
    tab_hi = jnp.where((lane >= half) & (lane < ROPE_DIM), sinl, 0.0)
    return tab_c.astype(F32), tab_lo.astype(F32), tab_hi.astype(F32)


def _rope_apply(t, tc, tlo, thi):
    half = ROPE_DIM // 2
    return t * tc + pltpu.roll(t, LANES - half, 1) * tlo + pltpu.roll(t, half, 1) * thi


def _rope_transpose(dt, tc, tlo, thi):
    half = ROPE_DIM // 2
    return dt * tc + pltpu.roll(dt * tlo, half, 1) + pltpu.roll(dt * thi, LANES - half, 1)


def _pack_dproj(dqs, dks, dvs, du, dgpre, tabs):
    tm = 256

    def body(*refs):
        dq_refs, dk_refs, dv_refs = refs[0:3], refs[3:6], refs[6:9]
        du_ref, dg_ref, tc_ref, tlo_ref, thi_ref, o_ref, stage = refs[9:16]
        n_ch = QKV_W // LANES
        halves = GROUP_W // LANES
        for grp, d in enumerate(DILATIONS):
            for which, src in enumerate((dq_refs[grp], dk_refs[grp], dv_refs[grp])):
                for res in range(d):
                    for half in range(halves):
                        _scatter_residue(stage, which * (n_ch // 3) + grp * halves + half, res, d, tm // d,
                                         src[:, _lane_chunk(res * halves + half)])
        tc, tlo, thi = tc_ref[...], tlo_ref[...], thi_ref[...]
        for ch in range(n_ch):
            piece = stage[ch]
            o_ref[:, _lane_chunk(ch)] = (_rope_transpose(piece, tc, tlo, thi) if ch < 2 * n_ch // 3 else piece).astype(BF16)
        o_ref[:, QKV_W:QKV_W + SSM_W] = du_ref[...].astype(BF16)
        o_ref[:, QKV_W + SSM_W:] = dg_ref[...].astype(BF16)

    t = du.shape[0]
    ins = [("row", a) for a in (*dqs, *dks, *dvs, du, dgpre)] + [("tab", tb) for tb in tabs]
    return _rows(body, "pack_dproj", t, tm, ins, [("row", IN_W, BF16)],
                 scratch=[pltpu.VMEM((QKV_W // LANES, tm, LANES), F32)])[0]


def _merge_groups(o_refs, l_refs, a_ref, lt_ref, nat, tm):
    halves = GROUP_W // LANES
    for grp, d in enumerate(DILATIONS[1:], start=1):
        for j, src in enumerate((o_refs[grp], l_refs[grp])):
            for res in range(d):
                for half in range(halves):
                    _scatter_residue(nat, (grp - 1) * 4 + j * 2 + half, res, d, tm // d,
                                     src[:, _lane_chunk(res * halves + half)])
    for half in range(halves):
        sl = _lane_chunk(half)
        la, lb, lc = l_refs[0][:, sl], nat[2 + half], nat[6 + half]
        m = jnp.maximum(jnp.maximum(la, lb), lc)
        ea, eb, ec = jnp.exp(la - m), jnp.exp(lb - m), jnp.exp(lc - m)
        ssum = ea + eb + ec
        a_ref[:, sl] = (ea / ssum) * o_refs[0][:, sl] + (eb / ssum) * nat[half] + (ec / ssum) * nat[4 + half]
        lt_ref[:, sl] = m + jnp.log(ssum)


def _head_sum_matrix():
    r = jnp.arange(GROUP_W) // HEAD_DIM
    return (r[:, None] == r[None, :]).astype(F32)


def _attention_cotangents(da, attn, lt, ones, rd_ref, dil, stage, tm):
    halves = GROUP_W // LANES
    rd = jnp.dot(da * attn, ones, preferred_element_type=F32, precision=lax.Precision.HIGHEST)
    rd_ref[...] = rd
    for half in range(halves):
        for j, val in enumerate((da, lt, rd)):
            stage[2 * j + half] = val[:, _lane_chunk(half)]
    for grp, d in enumerate(DILATIONS[1:], start=1):
        for j in range(3):
            for res in range(d):
                for half in range(halves):
                    dil[3 * (grp - 1) + j][:, _lane_chunk(res * halves + half)] = _gather_residue(
                        stage, 2 * j + half, res, d, tm // d)


_GELU_C = math.sqrt(2.0 / math.pi)


def _head_masks():
    lane = lax.broadcasted_iota(jnp.int32, (1, GROUP_W), 1)
    return [(lane // HEAD_DIM) == h for h in range(HEADS_PER_GROUP)]


def _stack_heads(blk, masks, fill=0.0):
    return jnp.concatenate([jnp.where(mk, blk, jnp.full_like(blk, fill)) for mk in masks], axis=0)


def _unstack_heads(stacked, masks):
    rows = stacked.shape[0] // len(masks)
    out = stacked[:rows]
    for h in range(1, len(masks)):
        out = jnp.where(masks[h], stacked[h * rows:(h + 1) * rows], out)
    return out


def _band_mask(first):
    nk = ATT_BLOCK if first else 2 * ATT_BLOCK
    qi = lax.broadcasted_iota(jnp.int32, (ATT_BLOCK, nk), 0)
    ki = lax.broadcasted_iota(jnp.int32, (ATT_BLOCK, nk), 1)
    dist = qi - ki + (0 if first else ATT_BLOCK)
    return (dist >= 0) & (dist <= ATT_BLOCK)


_NT = (((1,), (1,)), ((), ()))
_TN = (((0,), (0,)), ((), ()))


def _residues_per_step(d):
    return 4 if d >= 16 else 1


def _attn_fwd(q, k, v, group, n_samples, comm=None):
    d = DILATIONS[group]
    length = SEQ // d
    nb = length // ATT_BLOCK

    rps = _residues_per_step(d)

    def body(q_ref, k_ref, v_ref, o_ref, l_ref):
        for rl in range(rps):
            residue(q_ref, k_ref, v_ref, o_ref, l_ref, slice(rl * GROUP_W, (rl + 1) * GROUP_W))

    def residue(q_ref, k_ref, v_ref, o_ref, l_ref, cols):
        masks = _head_masks()

        def block(qs, ks, first):
            nk = ATT_BLOCK if first else 2 * ATT_BLOCK
            qb = q_ref[0, pl.ds(qs, ATT_BLOCK), cols]
            kc = k_ref[0, pl.ds(ks, nk), cols]
            vc = v_ref[0, pl.ds(ks, nk), cols]
            q4 = _stack_heads(qb, masks)
            valid = jnp.tile(_band_mask(first), (HEADS_PER_GROUP, 1))
            s = lax.dot_general(q4, kc, _NT, preferred_element_type=F32) * (HEAD_DIM ** -0.5)
            s = jnp.where(valid, s, NEG_INF)
            m = jnp.max(s, axis=-1, keepdims=True)
            p = jnp.exp(s - m)
            l = jnp.sum(p, axis=-1, keepdims=True)
            o4 = jnp.dot(p.astype(MXU_DTYPE), vc, preferred_element_type=F32) / l
            lse4 = jnp.broadcast_to(m + jnp.log(l), o4.shape)
            o_ref[0, pl.ds(qs, ATT_BLOCK), cols] = _unstack_heads(o4, masks)
            l_ref[0, pl.ds(qs, ATT_BLOCK), cols] = _unstack_heads(lse4, masks)

        block(0, 0, True)
        if nb > 1:
            def loop(n, carry):
                block(pl.multiple_of(n * ATT_BLOCK, ATT_BLOCK), pl.multiple_of((n - 1) * ATT_BLOCK, ATT_BLOCK), False)
                return carry

            lax.fori_loop(1, nb, loop, 0)

    per_sample = lambda a: a.reshape(n_samples, length, d * GROUP_W)
    spec = pl.BlockSpec((1, length, rps * GROUP_W), lambda b, r: (b, 0, r))
    shp = jax.ShapeDtypeStruct((n_samples, length, d * GROUP_W), F32)
    o, lse, *carried = _grid_call(body, f"attn_fwd_g{group}", (n_samples, d // rps), [per_sample(a) for a in (q, k, v)],
                                  [spec] * 3, [spec] * 2, [shp, shp], VMEM_MID, comm)
    flat = lambda a: a.reshape(n_samples * length, d * GROUP_W)
    return flat(o), flat(lse), carried


def _attn_bwd(q, k, v, dattn, lse_tot, rowdot, group, n_samples, comm=None):
    d = DILATIONS[group]
    length = SEQ // d
    nb = length // ATT_BLOCK

    rps = _residues_per_step(d)

    def body(q_ref, k_ref, v_ref, da_ref, lt_ref, rd_ref, dq_ref, dk_ref, dv_ref):
        dk_ref[...] = jnp.zeros_like(dk_ref)
        dv_ref[...] = jnp.zeros_like(dv_ref)
        for rl in range(rps):
            residue(q_ref, k_ref, v_ref, da_ref, lt_ref, rd_ref, dq_ref, dk_ref, dv_ref,
                    slice(rl * GROUP_W, (rl + 1) * GROUP_W))

    def residue(q_ref, k_ref, v_ref, da_ref, lt_ref, rd_ref, dq_ref, dk_ref, dv_ref, cols):
        masks = _head_masks()

        def block(qs, ks, first):
            nk = ATT_BLOCK if first else 2 * ATT_BLOCK
            qb = q_ref[0, pl.ds(qs, ATT_BLOCK), cols]
            kc = k_ref[0, pl.ds(ks, nk), cols]
            vc = v_ref[0, pl.ds(ks, nk), cols]
            da = da_ref[0, pl.ds(qs, ATT_BLOCK), cols]
            lt = lt_ref[0, pl.ds(qs, ATT_BLOCK), cols]
            rd = rd_ref[0, pl.ds(qs, ATT_BLOCK), cols]
            q4 = _stack_heads(qb, masks)
            da4 = _stack_heads(da, masks).astype(MXU_DTYPE)
            lt4 = jnp.max(_stack_heads(lt, masks, -jnp.inf), axis=-1, keepdims=True)
            rd4 = jnp.max(_stack_heads(rd, masks, -jnp.inf), axis=-1, keepdims=True)
            valid = jnp.tile(_band_mask(first), (HEADS_PER_GROUP, 1))
            s = lax.dot_general(q4, kc, _NT, preferred_element_type=F32) * (HEAD_DIM ** -0.5)
            s = jnp.where(valid, s, NEG_INF)
            p = jnp.exp(s - lt4)
            dp = lax.dot_general(da4, vc, _NT, preferred_element_type=F32)
            ds = (p * (dp - rd4) * (HEAD_DIM ** -0.5)).astype(MXU_DTYPE)
            dq_ref[0, pl.ds(qs, ATT_BLOCK), cols] = _unstack_heads(jnp.dot(ds, kc, preferred_element_type=F32), masks)
            dk_ref[0, pl.ds(ks, nk), cols] += lax.dot_general(ds, q4, _TN, preferred_element_type=F32)
            dv_ref[0, pl.ds(ks, nk), cols] += lax.dot_general(p.astype(MXU_DTYPE), da4, _TN, preferred_element_type=F32)

        block(0, 0, True)
        if nb > 1:
            def loop(n, carry):
                block(pl.multiple_of(n * ATT_BLOCK, ATT_BLOCK), pl.multiple_of((n - 1) * ATT_BLOCK, ATT_BLOCK), False)
                return carry

            lax.fori_loop(1, nb, loop, 0)

    per_sample = lambda a: a.reshape(n_samples, length, d * GROUP_W)
    spec = pl.BlockSpec((1, length, rps * GROUP_W), lambda b, r: (b, 0, r))
    shp = jax.ShapeDtypeStruct((n_samples, length, d * GROUP_W), F32)
    dq, dk, dv, *carried = _grid_call(
        body, f"attn_bwd_g{group}", (n_samples, d // rps), [per_sample(a) for a in (q, k, v, dattn, lse_tot, rowdot)],
        [spec] * 6, [spec] * 3, [shp, shp, shp], VMEM_MID, comm)
    flat = lambda a: a.reshape(n_samples * length, d * GROUP_W)
    return flat(dq), flat(dk), flat(dv), carried


def _disc(lr, li, ldt, br, bi):
    dt = jnp.exp(ldt)
    mag = jnp.exp(lr * dt)
    ab_re, ab_im = mag * jnp.cos(li * dt), mag * jnp.sin(li * dt)
    den = lr * lr + li * li
    nr, ni = ab_re - 1.0, ab_im
    f_re = (nr * lr + ni * li) / den
    f_im = (ni * lr - nr * li) / den
    return ab_re, ab_im, f_re * br - f_im * bi, f_re * bi + f_im * br


def _state_mask():
    row_g = lax.broadcasted_iota(jnp.int32, (SCAN_CH, SCAN_WC), 0) // SSM_CH
    col_g = lax.broadcasted_iota(jnp.int32, (SCAN_CH, SCAN_WC), 1) // SSM_STATE
    return row_g == col_g


def _ssm_disc(lr, li, ldt, br, bi, cr, ci):
    w = SCAN_WC

    def body(lr_ref, li_ref, ldt_ref, br_ref, bi_ref, cr_ref, ci_ref, a_ref, bb_ref, c_ref):
        ar, ai, bbr, bbi = _disc(lr_ref[...], li_ref[...], ldt_ref[...], br_ref[...], bi_ref[...])
        crv, civ = cr_ref[...], ci_ref[...]
        mask = _state_mask()
        for cb in range(SCAN_NBLK):
            sl = slice(cb * w, (cb + 1) * w)
            rows = slice(cb * SCAN_CH, (cb + 1) * SCAN_CH)
            dense = lambda comp: jnp.where(mask, jnp.tile(comp[:, sl], (SCAN_CH // SSM_CH, 1)), 0.0)
            a_ref[:, 2 * cb * w:(2 * cb + 1) * w] = ar[:, sl]
            a_ref[:, (2 * cb + 1) * w:(2 * cb + 2) * w] = ai[:, sl]
            bb_ref[rows, :w] = dense(bbr).astype(MXU_DTYPE)
            bb_ref[rows, w:] = dense(bbi).astype(MXU_DTYPE)
            c_ref[rows, :w] = dense(crv).astype(MXU_DTYPE)
            c_ref[rows, w:] = (-dense(civ)).astype(MXU_DTYPE)

    return _pallas_call(
        body, name="ssm_disc",
        out_shape=[jax.ShapeDtypeStruct((1, 2 * N_STATE), F32), jax.ShapeDtypeStruct((SSM_W, 2 * w), MXU_DTYPE),
                   jax.ShapeDtypeStruct((SSM_W, 2 * w), MXU_DTYPE)],
        compiler_params=pltpu.CompilerParams(vmem_limit_bytes=VMEM_MID),
    )(lr, li, ldt, br, bi, cr, ci)


def _group_indicator():
    s = jnp.arange(N_STATE) // SSM_STATE
    return (s[:, None] == jnp.arange(LANES)[None, :]).astype(F32)


def _ssm_param_bwd(lr, li, ldt, br, bi, da_cat, dbb_full, dc_full):
    w = SCAN_WC

    def body(lr_ref, li_ref, ldt_ref, br_ref, bi_ref, da_ref, dbb_ref, dc_ref, ind_ref,
             glr_ref, gli_ref, gldt_ref, gbr_ref, gbi_ref, gcr_ref, gci_ref):
        mask = _state_mask()

        def diag_parts(ref):
            res = ([], [])
            for cb in range(SCAN_NBLK):
                for part in range(2):
                    blk = ref[cb * SCAN_CH:(cb + 1) * SCAN_CH, part * w:(part + 1) * w]
                    res[part].append(jnp.sum(jnp.where(mask, blk, 0.0).reshape(SCAN_CH // SSM_CH, SSM_CH, w), axis=0))
            return jnp.concatenate(res[0], axis=1), jnp.concatenate(res[1], axis=1)

        dar = jnp.concatenate([da_ref[:, 2 * cb * w:(2 * cb + 1) * w] for cb in range(SCAN_NBLK)], axis=1)
        dai = jnp.concatenate([da_ref[:, (2 * cb + 1) * w:(2 * cb + 2) * w] for cb in range(SCAN_NBLK)], axis=1)
        dbbr, dbbi = diag_parts(dbb_ref)
        dcr, dci_neg = diag_parts(dc_ref)
        gcr_ref[...] = dcr
        gci_ref[...] = -dci_neg
        _, vjp = jax.vjp(_disc, lr_ref[...], li_ref[...], ldt_ref[...], br_ref[...], bi_ref[...])
        glr, gli, gldt, gbr, gbi = vjp((dar, dai, dbbr, dbbi))
        glr_ref[...] = glr
        gli_ref[...] = gli
        gldt_ref[...] = jnp.dot(jnp.broadcast_to(gldt, (8, N_STATE)), ind_ref[...], preferred_element_type=F32,
                                precision=lax.Precision.HIGHEST)
        gbr_ref[...] = gbr
        gbi_ref[...] = gbi

    v1 = jax.ShapeDtypeStruct((1, N_STATE), F32)
    v16 = jax.ShapeDtypeStruct((SSM_CH, N_STATE), F32)
    vdt = jax.ShapeDtypeStruct((8, LANES), F32)
    return _pallas_call(
        body, name="ssm_param_bwd", out_shape=[v1, v1, vdt, v16, v16, v16, v16],
        compiler_params=pltpu.CompilerParams(vmem_limit_bytes=VMEM_BIG),
    )(lr, li, ldt, br, bi, da_cat, dbb_full, dc_full, _group_indicator())


def _cmul(ar, ai, br, bi):
    return ar * br - ai * bi, ar * bi + ai * br


def _gelu_tanh(y):
    return jnp.tanh(_GELU_C * (y + 0.044715 * (y * y * y)))


def _segment_carry(er, ei, ar, ai, n_rows, reverse):
    qr, qi = ar, ai
    for _ in range(int(math.log2(SCAN_LEN))):
        qr, qi = _cmul(qr, qi, qr, qi)
    seg = lax.broadcasted_iota(jnp.int32, er.shape, 0) % SCAN_SEG_PER_SAMPLE
    shift = 1
    while shift < SCAN_SEG_PER_SAMPLE:
        keep = (seg < SCAN_SEG_PER_SAMPLE - shift) if reverse else (seg >= shift)
        amount = n_rows - shift if reverse else shift
        sr = jnp.where(keep, pltpu.roll(er, amount, 0), 0.0)
        si = jnp.where(keep, pltpu.roll(ei, amount, 0), 0.0)
        if reverse:
            er, ei = er + qr * sr + qi * si, ei + qr * si - qi * sr
        else:
            er, ei = er + qr * sr - qi * si, ei + qr * si + qi * sr
        qr, qi = _cmul(qr, qi, qr, qi)
        shift *= 2
    keep = (seg < SCAN_SEG_PER_SAMPLE - 1) if reverse else (seg >= 1)
    amount = n_rows - 1 if reverse else 1
    return jnp.where(keep, pltpu.roll(er, amount, 0), 0.0), jnp.where(keep, pltpu.roll(ei, amount, 0), 0.0)


def _ssm_fwd(u_perm, a_cat, bbc, cc, dskip, n_rows):
    t = u_perm.shape[0]
    w = SCAN_WC
    rows_c = SCAN_CHUNK * n_rows
    n_chunks = t // rows_c

    assert n_chunks % 2 == 0

    def body(u_ref, a_ref, bb_ref, c_ref, d_ref, yt_ref, yg_ref, ein_ref, bu_all, st_a, st_b, xs_a, xs_b):
        ar = jnp.broadcast_to(a_ref[:, :w], (n_rows, w))
        ai = jnp.broadcast_to(a_ref[:, w:], (n_rows, w))
        start = lambda ch: pl.multiple_of(ch * rows_c, rows_c)

        def project(ch, stage):
            res = jnp.dot(u_ref[pl.ds(start(ch), rows_c), :].astype(MXU_DTYPE), bb_ref[...], preferred_element_type=F32)
            stage[...] = res
            bu_all[pl.ds(start(ch), rows_c), :] = res

        def steps(src, r0, carry, xs=None):
            for i in range(SCAN_CHUNK):
                blk = src[pl.ds(r0 + i * n_rows, n_rows), :]
                carry = (ar * carry[0] - ai * carry[1] + blk[:, :w], ar * carry[1] + ai * carry[0] + blk[:, w:])
                if xs is not None:
                    xs[i * n_rows:(i + 1) * n_rows, :w] = carry[0]
                    xs[i * n_rows:(i + 1) * n_rows, w:] = carry[1]
            return carry

        def emit(xs, ch):
            y = lax.dot_general(xs[...].astype(MXU_DTYPE), c_ref[...], _NT, preferred_element_type=F32)
            yt = y + d_ref[...] * u_ref[pl.ds(start(ch), rows_c), :]
            yt_ref[pl.ds(start(ch), rows_c), :] = yt
            yg_ref[pl.ds(start(ch), rows_c), :] = (0.5 * yt * (1.0 + _gelu_tanh(yt))).astype(BF16)

        project(0, st_a)

        def pair1(p, carry):
            project(2 * p + 1, st_b)
            carry = steps(st_a, 0, carry)
            project(jnp.minimum(2 * p + 2, n_chunks - 1), st_a)
            return steps(st_b, 0, carry)

        zero = jnp.zeros((n_rows, w), F32)
        er, ei = lax.fori_loop(0, n_chunks // 2, pair1, (zero, zero))
        cr, ci = _segment_carry(er, ei, ar, ai, n_rows, False)
        ein_ref[:, :w] = cr
        ein_ref[:, w:] = ci

        xs_b[...] = jnp.zeros_like(xs_b)

        def pair2(p, carry):
            emit(xs_b, jnp.maximum(2 * p - 1, 0))
            carry = steps(bu_all, start(2 * p), carry, xs_a)
            emit(xs_a, 2 * p)
            return steps(bu_all, start(2 * p + 1), carry, xs_b)

        lax.fori_loop(0, n_chunks // 2, pair2, (cr, ci))
        emit(xs_b, n_chunks - 1)

    col = lambda width: pl.BlockSpec((t, width), lambda c: (0, c))
    wgt = pl.BlockSpec((SCAN_CH, 2 * w), lambda c: (c, 0))
    return _pallas_call(
        body, name="ssm_fwd", grid=(SCAN_NBLK,),
        in_specs=[col(SCAN_CH), pl.BlockSpec((1, 2 * w), lambda c: (0, c)), wgt, wgt,
                  pl.BlockSpec((1, SCAN_CH), lambda c: (0, c))],
        out_specs=[col(SCAN_CH), col(SCAN_CH), pl.BlockSpec((n_rows, 2 * w), lambda c: (0, c))],
        out_shape=[jax.ShapeDtypeStruct((t, SSM_W), F32), jax.ShapeDtypeStruct((t, SSM_W), BF16),
                   jax.ShapeDtypeStruct((n_rows, 2 * N_STATE), F32)],
        scratch_shapes=[pltpu.VMEM((t, 2 * w), F32)] + [pltpu.VMEM((rows_c, 2 * w), F32)] * 4,
        compiler_params=pltpu.CompilerParams(dimension_semantics=("parallel",), vmem_limit_bytes=VMEM_BIG),
    )(u_perm, a_cat, bbc, cc, dskip)


def _ssm_bwd(u_perm, dyg, ytot, dskip, a_cat, bbc, cc, ein, n_rows, comm=None):
    t = u_perm.shape[0]
    w = SCAN_WC
    rows_c = SCAN_CHUNK * n_rows
    n_chunks = t // rows_c

    assert n_chunks % 2 == 0
    last = n_chunks - 1

    def body(u_ref, dyg_ref, yt_ref, dk_ref, a_ref, bb_ref, c_ref, ein_ref, du_ref, gd_ref, da_ref, dbb_ref, dc_ref,
             xs_all, dy_s, st_a, st_b, buf_a, buf_b):
        ar = jnp.broadcast_to(a_ref[:, :w], (n_rows, w))
        ai = jnp.broadcast_to(a_ref[:, w:], (n_rows, w))
        zero = jnp.zeros((n_rows, w), F32)
        start = lambda ch: pl.multiple_of(ch * rows_c, rows_c)
        dbb_ref[...] = jnp.zeros_like(dbb_ref)
        dc_ref[...] = jnp.zeros_like(dc_ref)
        da_ref[...] = jnp.zeros_like(da_ref)

        yt = yt_ref[...]
        th = _gelu_tanh(yt)
        dgelu = 0.5 * (1.0 + th) + 0.5 * yt * (1.0 - th * th) * _GELU_C * (1.0 + 3.0 * 0.044715 * yt * yt)
        dy_all = dyg_ref[...] * dgelu
        dy_s[...] = dy_all
        gd_ref[...] = jnp.sum(dy_all * u_ref[...], axis=0, keepdims=True)
        dy_chunk = lambda ch: dy_s[pl.ds(start(ch), rows_c), :].astype(MXU_DTYPE)

        xs_all[0:n_rows, :] = ein_ref[...]

        def project(ch, stage):
            stage[...] = jnp.dot(u_ref[pl.ds(start(ch), rows_c), :].astype(MXU_DTYPE), bb_ref[...],
                                 preferred_element_type=F32)

        def fwd_steps(stage, ch, carry, xs):
            for i in range(SCAN_CHUNK):
                blk = stage[i * n_rows:(i + 1) * n_rows, :]
                carry = (ar * carry[0] - ai * carry[1] + blk[:, :w], ar * carry[1] + ai * carry[0] + blk[:, w:])
                for half, val in enumerate(carry):
                    xs[i * n_rows:(i + 1) * n_rows, half * w:(half + 1) * w] = val
                    xs_all[pl.ds(start(ch) + (i + 1) * n_rows, n_rows), half * w:(half + 1) * w] = val
            return carry

        def add_dc(xs, ch):
            dc_ref[...] += lax.dot_general(dy_chunk(ch), xs[...].astype(MXU_DTYPE), _TN, preferred_element_type=F32)

        project(0, st_a)

        def fwd_pair(p, carry):
            project(2 * p + 1, st_b)
            carry = fwd_steps(st_a, 2 * p, carry, buf_a)
            add_dc(buf_a, 2 * p)
            project(jnp.minimum(2 * p + 2, last), st_a)
            carry = fwd_steps(st_b, 2 * p + 1, carry, buf_b)
            add_dc(buf_b, 2 * p + 1)
            return carry

        lax.fori_loop(0, n_chunks // 2, fwd_pair, (ein_ref[:, :w], ein_ref[:, w:]))

        def project_dx(ch, stage):
            stage[...] = jnp.dot(dy_chunk(ch), c_ref[...], preferred_element_type=F32)

        def back_steps(stage, carry, g_buf=None):
            for i in reversed(range(SCAN_CHUNK)):
                blk = stage[i * n_rows:(i + 1) * n_rows, :]
                carry = (blk[:, :w] + ar * carry[0] + ai * carry[1], blk[:, w:] + ar * carry[1] - ai * carry[0])
                if g_buf is not None:
                    g_buf[i * n_rows:(i + 1) * n_rows, :w] = carry[0]
                    g_buf[i * n_rows:(i + 1) * n_rows, w:] = carry[1]
            return carry

        def first_pair(p, carry):
            project_dx(last - 2 * p - 1, st_b)
            carry = back_steps(st_a, carry)
            project_dx(jnp.maximum(last - 2 * p - 2, 0), st_a)
            return back_steps(st_b, carry)

        project_dx(last, st_a)
        sr, si = lax.fori_loop(0, n_chunks // 2, first_pair, (zero, zero))
        gr0, gi0 = _segment_carry(sr, si, ar, ai, n_rows, True)

        def post(g_buf, ch):
            g = g_buf[...]
            xp = xs_all[pl.ds(start(ch), rows_c), :]
            da_ref[:, :w] += jnp.sum(g[:, :w] * xp[:, :w] + g[:, w:] * xp[:, w:], axis=0, keepdims=True)
            da_ref[:, w:] += jnp.sum(g[:, w:] * xp[:, :w] - g[:, :w] * xp[:, w:], axis=0, keepdims=True)
            gb = g.astype(MXU_DTYPE)
            du_ref[pl.ds(start(ch), rows_c), :] = (lax.dot_general(gb, bb_ref[...], _NT, preferred_element_type=F32)
                                                   + dy_s[pl.ds(start(ch), rows_c), :] * dk_ref[...])
            dbb_ref[...] += lax.dot_general(u_ref[pl.ds(start(ch), rows_c), :].astype(MXU_DTYPE), gb, _TN,
                                            preferred_element_type=F32)

        def second_pair(p, carry):
            c1 = last - 2 * p
            project_dx(c1 - 1, st_b)
            post(buf_b, jnp.minimum(c1 + 1, last))
            carry = back_steps(st_a, carry, buf_a)
            project_dx(jnp.maximum(c1 - 2, 0), st_a)
            post(buf_a, c1)
            return back_steps(st_b, carry, buf_b)

        project_dx(last, st_a)
        buf_b[...] = jnp.zeros_like(buf_b)
        lax.fori_loop(0, n_chunks // 2, second_pair, (gr0, gi0))
        post(buf_b, 0)

    col = lambda width: pl.BlockSpec((t, width), lambda c, j: (0, c))
    wgt = pl.BlockSpec((SCAN_CH, 2 * w), lambda c, j: (c, 0))
    row = pl.BlockSpec((1, 2 * w), lambda c, j: (0, c))
    chan = pl.BlockSpec((1, SCAN_CH), lambda c, j: (0, c))
    return _grid_call(
        body, "ssm_bwd", (SCAN_NBLK, 1), [u_perm, dyg, ytot, dskip, a_cat, bbc, cc, ein],
        [col(SCAN_CH), col(SCAN_CH), col(SCAN_CH), chan, row, wgt, wgt,
         pl.BlockSpec((n_rows, 2 * w), lambda c, j: (0, c))],
        [col(SCAN_CH), chan, row, wgt, wgt],
        [jax.ShapeDtypeStruct((t, SSM_W), F32), jax.ShapeDtypeStruct((1, SSM_W), F32),
         jax.ShapeDtypeStruct((1, 2 * N_STATE), F32), jax.ShapeDtypeStruct((SSM_W, 2 * w), F32),
         jax.ShapeDtypeStruct((SSM_W, 2 * w), F32)],
        56 * 1024 * 1024, comm,
        scratch=[pltpu.VMEM((t + n_rows, 2 * w), F32), pltpu.VMEM((t, SCAN_CH), F32)]
        + [pltpu.VMEM((rows_c, 2 * w), F32)] * 4)


def _to_scan_rows(a, n_samples):
    c = a.shape[1]
    return a.reshape(n_samples, SCAN_SEG_PER_SAMPLE, SCAN_LEN, c).transpose(2, 0, 1, 3).reshape(-1, c)


def _from_scan_rows(a, n_samples):
    c = a.shape[1]
    return a.reshape(SCAN_LEN, n_samples, SCAN_SEG_PER_SAMPLE, c).transpose(1, 2, 0, 3).reshape(-1, c)


def _row_spec(tm, width):
    return pl.BlockSpec((tm, width), lambda i, j: (i, 0))


def _whole(arr):
    return pl.BlockSpec(arr.shape, lambda i, j: (0,) * arr.ndim)


def _proj_rope(x, g, w_in_t, tabs, comm=None):
    t = x.shape[0]
    tm = 256

    def body(x_ref, g_ref, w_ref, tc_ref, tlo_ref, thi_ref, h_ref, u_ref, gate_ref, *rest):
        qkv_refs, stage = rest[:9], rest[9]
        xv = x_ref[...]
        r = lax.rsqrt(jnp.mean(xv * xv, axis=-1, keepdims=True) + RMS_EPS)
        h = ((xv * r) * g_ref[...]).astype(BF16)
        h_ref[...] = h
        p = lax.dot_general(h.astype(MXU_DTYPE), w_ref[...], _NT, preferred_element_type=F32)
        u_ref[...] = p[:, QKV_W:QKV_W + SSM_W]
        gate_ref[...] = _sigmoid(p[:, QKV_W + SSM_W:])
        tc, tlo, thi = tc_ref[...], tlo_ref[...], thi_ref[...]
        n_ch = QKV_W // LANES
        for ch in range(n_ch):
            piece = p[:, _lane_chunk(ch)]
            stage[ch] = _rope_apply(piece, tc, tlo, thi) if ch < 2 * n_ch // 3 else piece
        halves = GROUP_W // LANES
        for grp, d in enumerate(DILATIONS):
            for which in range(3):
                out = qkv_refs[3 * grp + which]
                for res in range(d):
                    for half in range(halves):
                        ch = which * (n_ch // 3) + grp * halves + half
                        out[:, _lane_chunk(res * halves + half)] = _gather_residue(stage, ch, res, d, tm // d).astype(BF16)

    tab = pl.BlockSpec((tm, LANES), lambda i, j: (i % (SEQ // tm), 0))
    widths = [(D_MODEL, BF16), (SSM_W, F32), (2 * D_MODEL, F32)]
    out_specs = [_row_spec(tm, wd) for wd, _ in widths]
    out_shapes = [jax.ShapeDtypeStruct((t, wd), dt) for wd, dt in widths]
    for d in DILATIONS:
        out_specs += [_row_spec(tm // d, d * GROUP_W)] * 3
        out_shapes += [jax.ShapeDtypeStruct((t // d, d * GROUP_W), BF16)] * 3
    return _grid_call(
        body, "proj_rope", (t // tm, 1), [x, g, w_in_t, *tabs],
        [_row_spec(tm, D_MODEL), _whole(g), _whole(w_in_t), tab, tab, tab], out_specs, out_shapes, VMEM_BIG, comm,
        scratch=[pltpu.VMEM((QKV_W // LANES, tm, LANES), F32)])


def _branch_outputs(attn_ref, yg_ref, wao_ref, wglu_ref):
    attn_d = lax.dot_general(attn_ref[...].astype(MXU_DTYPE), wao_ref[...], _NT, preferred_element_type=F32)
    z = lax.dot_general(yg_ref[...].astype(MXU_DTYPE), wglu_ref[...], _NT, preferred_element_type=F32)
    return attn_d, z[:, :D_MODEL], _sigmoid(z[:, D_MODEL:])


def _mix_out_rms(os_, lses, yg, gates, x, w_ao_t, w_glu_t, w_out, g, comm=None):
    t = x.shape[0]
    tm = 256

    def body(o0, o1, o2, l0, l1, l2, yg_ref, gate_ref, x_ref, wao_ref, wglu_ref, wout_ref, g_ref,
             attn_ref, lt_ref, m_ref, x1_ref, h_ref, nat):
        _merge_groups((o0, o1, o2), (l0, l1, l2), attn_ref, lt_ref, nat, tm)
        attn_d, za, sb = _branch_outputs(attn_ref, yg_ref, wao_ref, wglu_ref)
        merged = (gate_ref[:, :D_MODEL] * attn_d + gate_ref[:, D_MODEL:] * (za * sb)).astype(BF16)
        m_ref[...] = merged
        x1 = x_ref[...] + jnp.dot(merged.astype(MXU_DTYPE), wout_ref[...], preferred_element_type=F32)
        x1_ref[...] = x1
        r = lax.rsqrt(jnp.mean(x1 * x1, axis=-1, keepdims=True) + RMS_EPS)
        h_ref[...] = ((x1 * r) * g_ref[...]).astype(BF16)

    dil_specs = [_row_spec(tm // d, d * GROUP_W) for d in DILATIONS] * 2
    return _grid_call(
        body, "mix_out_rms", (t // tm, 1), [*os_, *lses, yg, gates, x, w_ao_t, w_glu_t, w_out, g],
        dil_specs + [_row_spec(tm, SSM_W), _row_spec(tm, 2 * D_MODEL), _row_spec(tm, D_MODEL),
                     _whole(w_ao_t), _whole(w_glu_t), _whole(w_out), _whole(g)],
        [_row_spec(tm, GROUP_W)] * 2 + [_row_spec(tm, D_MODEL)] * 3,
        [jax.ShapeDtypeStruct((t, GROUP_W), F32)] * 2
        + [jax.ShapeDtypeStruct((t, D_MODEL), BF16), jax.ShapeDtypeStruct((t, D_MODEL), F32),
           jax.ShapeDtypeStruct((t, D_MODEL), BF16)], VMEM_BIG, comm, scratch=[pltpu.VMEM((8, tm, LANES), F32)])


def _mix_bwd(dx1b, attn, lse_tot, yg, gates, w_ao_t, w_glu_t, w_out, comm=None):
    t = dx1b.shape[0]
    tm = 256

    def body(dx_ref, attn_ref, lt_ref, yg_ref, gate_ref, wao_ref, wglu_ref, wout_ref, ones_ref,
             dad_ref, dz_ref, dg_ref, da_ref, dyg_ref, rd_ref, *rest):
        dm = lax.dot_general(dx_ref[...], wout_ref[...], _NT, preferred_element_type=F32)
        attn_d, za, sb = _branch_outputs(attn_ref, yg_ref, wao_ref, wglu_ref)
        g0, g1 = gate_ref[:, :D_MODEL], gate_ref[:, D_MODEL:]
        dad = (dm * g0).astype(BF16)
        dad_ref[...] = dad
        ds = dm * g1
        dza, dzb = (ds * sb).astype(BF16), (ds * za * sb * (1.0 - sb)).astype(BF16)
        dz_ref[:, :D_MODEL] = dza
        dz_ref[:, D_MODEL:] = dzb
        dg_ref[:, :D_MODEL] = (dm * attn_d * g0 * (1.0 - g0)).astype(BF16)
        dg_ref[:, D_MODEL:] = (dm * (za * sb) * g1 * (1.0 - g1)).astype(BF16)
        da = jnp.dot(dad.astype(MXU_DTYPE), wao_ref[...], preferred_element_type=F32)
        da_ref[...] = da
        dyg_ref[...] = (jnp.dot(dza.astype(MXU_DTYPE), wglu_ref[:D_MODEL, :], preferred_element_type=F32)
                        + jnp.dot(dzb.astype(MXU_DTYPE), wglu_ref[D_MODEL:, :], preferred_element_type=F32))
        _attention_cotangents(da, attn_ref[...], lt_ref[...], ones_ref[...], rd_ref, rest[:6], rest[6], tm)

    widths = [(D_MODEL, BF16), (2 * D_MODEL, BF16), (2 * D_MODEL, BF16), (GROUP_W, F32), (SSM_W, F32), (GROUP_W, F32)]
    out_specs = [_row_spec(tm, wd) for wd, _ in widths]
    out_shapes = [jax.ShapeDtypeStruct((t, wd), dt) for wd, dt in widths]
    for d in DILATIONS[1:]:
        out_specs += [_row_spec(tm // d, d * GROUP_W)] * 3
        out_shapes += [jax.ShapeDtypeStruct((t // d, d * GROUP_W), F32)] * 3
    ones = _head_sum_matrix()
    return _grid_call(
        body, "mix_bwd", (t // tm, 1), [dx1b, attn, lse_tot, yg, gates, w_ao_t, w_glu_t, w_out, ones],
        [_row_spec(tm, D_MODEL), _row_spec(tm, GROUP_W), _row_spec(tm, GROUP_W), _row_spec(tm, SSM_W),
         _row_spec(tm, 2 * D_MODEL), _whole(w_ao_t), _whole(w_glu_t), _whole(w_out), _whole(ones)],
        out_specs, out_shapes, VMEM_BIG, comm, scratch=[pltpu.VMEM((6, tm, LANES), F32)])


FFN_TN = D_FF // 2
MXU_COLS = 256


def _ffn_in_swiglu(h2, w_gate_t, w_up_t, comm=None):
    t = h2.shape[0]
    tm = 512

    def body(h_ref, wg_ref, wu_ref, a_ref, b_ref, f_ref):
        h = h_ref[...].astype(MXU_DTYPE)
        for c0 in range(0, FFN_TN, MXU_COLS):
            sl = slice(c0, min(c0 + MXU_COLS, FFN_TN))
            a = lax.dot_general(h, wg_ref[sl, :], _NT, preferred_element_type=F32)
            b = lax.dot_general(h, wu_ref[sl, :], _NT, preferred_element_type=F32)
            a_ref[:, sl] = a
            b_ref[:, sl] = b
            f_ref[:, sl] = (a * _sigmoid(a) * b).astype(BF16)

    tile = pl.BlockSpec((tm, FFN_TN), lambda j, i: (i, j))
    wspec = pl.BlockSpec((FFN_TN, D_MODEL), lambda j, i: (j, 0))
    return _grid_call(
        body, "ffn_in_swiglu", (D_FF // FFN_TN, t // tm), [h2, w_gate_t, w_up_t],
        [pl.BlockSpec((tm, D_MODEL), lambda j, i: (i, 0)), wspec, wspec],
        [tile] * 3, [jax.ShapeDtypeStruct((t, D_FF), F32)] * 2 + [jax.ShapeDtypeStruct((t, D_FF), BF16)], VMEM_BIG, comm)


def _ffn_down_final(f, w_down, x1, target, g):
    t = x1.shape[0]
    tm = 256

    def body(f_ref, w_ref, x1_ref, t_ref, g_ref, dx_ref, dxb_ref, loss_ref, gg_ref):
        @pl.when(pl.program_id(0) == 0)
        def _():
            loss_ref[...] = jnp.zeros_like(loss_ref)
            gg_ref[...] = jnp.zeros_like(gg_ref)

        xv = x1_ref[...] + jnp.dot(f_ref[...].astype(MXU_DTYPE), w_ref[...], preferred_element_type=F32)
        gv = g_ref[...]
        r = lax.rsqrt(jnp.mean(xv * xv, axis=-1, keepdims=True) + RMS_EPS)
        n = xv * r
        diff = n * gv - t_ref[...]
        per_tok = jnp.mean(diff * diff, axis=-1, keepdims=True)
        loss_ref[...] += 0.5 * jnp.sum(per_tok, axis=0, keepdims=True)
        dy = diff / xv.shape[-1]
        gg_ref[...] += jnp.sum(dy * n, axis=0, keepdims=True)
        dn = dy * gv
        dx = r * (dn - n * jnp.mean(dn * n, axis=-1, keepdims=True))
        dx_ref[...] = dx
        dxb_ref[...] = dx.astype(BF16)

    acc = lambda shp: pl.BlockSpec(shp, lambda i, j: (0, 0))
    return _grid_call(
        body, "ffn_down_final", (t // tm, 1), [f, w_down, x1, target, g],
        [_row_spec(tm, D_FF), _whole(w_down), _row_spec(tm, D_MODEL), _row_spec(tm, D_MODEL), _whole(g)],
        [_row_spec(tm, D_MODEL)] * 2 + [acc((8, LANES)), acc((1, D_MODEL))],
        [jax.ShapeDtypeStruct((t, D_MODEL), F32), jax.ShapeDtypeStruct((t, D_MODEL), BF16),
         jax.ShapeDtypeStruct((8, LANES), F32), jax.ShapeDtypeStruct((1, D_MODEL), F32)], VMEM_BIG, sequential=True)


def _d_f_swiglu_bwd(dx2b, w_down, a, b):
    t = a.shape[0]
    tm = 512

    def body(dx_ref, w_ref, a_ref, b_ref, da_ref, db_ref):
        d = lax.dot_general(dx_ref[...], w_ref[...], _NT, preferred_element_type=F32)
        av, bv = a_ref[...], b_ref[...]
        sg = _sigmoid(av)
        da_ref[...] = (d * bv * sg * (1.0 + av * (1.0 - sg))).astype(BF16)
        db_ref[...] = (d * av * sg).astype(BF16)

    tile = pl.BlockSpec((tm, FFN_TN), lambda j, i: (i, j))
    return _grid_call(
        body, "d_f_swiglu_bwd", (D_FF // FFN_TN, t // tm), [dx2b, w_down, a, b],
        [pl.BlockSpec((tm, D_MODEL), lambda j, i: (i, 0)), pl.BlockSpec((FFN_TN, D_MODEL), lambda j, i: (j, 0)), tile, tile],
        [tile] * 2, [jax.ShapeDtypeStruct((t, D_FF), BF16)] * 2, VMEM_BIG)


def _ffn_weight_grads(f, dx2b, da, db, h2):
    t = h2.shape[0]
    tm = 256
    half = D_MODEL // 2

    def body(f_ref, dx_ref, da_ref, db_ref, h_ref, dn_ref, g0_ref, g1_ref, u0_ref, u1_ref):
        dn_ref[...] = lax.dot_general(f_ref[...].astype(MXU_DTYPE), dx_ref[...].astype(MXU_DTYPE), _TN,
                                      preferred_element_type=F32).astype(BF16)
        h = h_ref[...].astype(MXU_DTYPE)
        for src, (lo_ref, hi_ref) in ((da_ref, (g0_ref, g1_ref)), (db_ref, (u0_ref, u1_ref))):
            prod = lax.dot_general(src[...].astype(MXU_DTYPE), h, _TN, preferred_element_type=F32)
            lo_ref[...] = prod[:, :half].astype(BF16)
            hi_ref[...] = prod[:, half:].astype(BF16)

    col = pl.BlockSpec((t, tm), lambda i, j: (0, i))
    out = pl.BlockSpec((tm, half), lambda i, j: (i, 0))
    return _grid_call(body, "mm_g_ffn", (D_FF // tm, 1), [f, dx2b, da, db, h2],
                      [col, _whole(dx2b), col, col, _whole(h2)], [_row_spec(tm, D_MODEL)] + [out] * 4,
                      [jax.ShapeDtypeStruct((D_FF, D_MODEL), BF16)] + [jax.ShapeDtypeStruct((D_FF, half), BF16)] * 4,
                      56 * 1024 * 1024)


def _branch_weight_grads(dz, yg, dattn_d, attn):
    t = yg.shape[0]
    steps = 4
    tz, ta = dz.shape[1] // steps, dattn_d.shape[1] // steps

    def body(dz_ref, yg_ref, dad_ref, attn_ref, gz_ref, ga_ref):
        gz_ref[...] = lax.dot_general(dz_ref[...].astype(MXU_DTYPE), yg_ref[...].astype(MXU_DTYPE), _TN,
                                      preferred_element_type=F32).astype(BF16)
        ga_ref[...] = lax.dot_general(dad_ref[...].astype(MXU_DTYPE), attn_ref[...].astype(MXU_DTYPE), _TN,
                                      preferred_element_type=F32).astype(BF16)

    col = lambda wd: pl.BlockSpec((t, wd), lambda i, j: (0, i))
    return _grid_call(body, "mm_g_branches", (steps, 1), [dz, yg, dattn_d, attn],
                      [col(tz), _whole(yg), col(ta), _whole(attn)], [_row_spec(tz, SSM_W), _row_spec(ta, GROUP_W)],
                      [jax.ShapeDtypeStruct((dz.shape[1], SSM_W), BF16), jax.ShapeDtypeStruct((dattn_d.shape[1], GROUP_W), BF16)],
                      VMEM_BIG)


def _mm_rms_bwd(operands, weights, x, g, dres, name, comm=None):
    t = x.shape[0]
    tm = 256
    n_op = len(operands)

    def body(*refs):
        a_refs, w_refs = refs[:n_op], refs[n_op:2 * n_op]
        x_ref, g_ref, dres_ref, dx_ref, dxb_ref, gg_ref = refs[2 * n_op:]

        @pl.when(pl.program_id(0) == 0)
        def _():
            gg_ref[...] = jnp.zeros_like(gg_ref)

        dh = None
        for a_ref, w_ref in zip(a_refs, w_refs):
            part = jnp.dot(a_ref[...].astype(MXU_DTYPE), w_ref[...], preferred_element_type=F32)
            dh = part if dh is None else dh + part
        xv = x_ref[...]
        r = lax.rsqrt(jnp.mean(xv * xv, axis=-1, keepdims=True) + RMS_EPS)
        n = xv * r
        gg_ref[...] += jnp.sum(dh * n, axis=0, keepdims=True)
        dn = dh * g_ref[...]
        dx = dres_ref[...] + r * (dn - n * jnp.mean(dn * n, axis=-1, keepdims=True))
        dx_ref[...] = dx
        dxb_ref[...] = dx.astype(BF16)

    d = x.shape[1]
    return _grid_call(
        body, name, (t // tm, 1), [*operands, *weights, x, g, dres],
        [_row_spec(tm, a.shape[1]) for a in operands] + [_whole(wk) for wk in weights]
        + [_row_spec(tm, d), _whole(g), _row_spec(tm, d)],
        [_row_spec(tm, d)] * 2 + [pl.BlockSpec((1, d), lambda i, j: (0, 0))],
        [jax.ShapeDtypeStruct((t, d), F32), jax.ShapeDtypeStruct((t, d), BF16), jax.ShapeDtypeStruct((1, d), F32)],
        VMEM_BIG, comm, sequential=True)


def _flat_small(small):
    perm_b = lambda a: a.reshape(SSM_GROUPS, SSM_STATE, SSM_CH).transpose(2, 0, 1).reshape(SSM_CH, N_STATE)
    perm_c = lambda a: a.reshape(SSM_GROUPS, SSM_CH, SSM_STATE).transpose(1, 0, 2).reshape(SSM_CH, N_STATE)
    return dict(
        g_mix=small["norm_mix_g"].reshape(1, D_MODEL), g_ffn=small["norm_ffn_g"].reshape(1, D_MODEL),
        g_fin=small["norm_final_g"].reshape(1, D_MODEL),
        lr=small["ssm_a_re"].reshape(1, N_STATE), li=small["ssm_a_im"].reshape(1, N_STATE),
        ldt=jnp.repeat(small["ssm_log_dt"].reshape(SSM_GROUPS), SSM_STATE).reshape(1, N_STATE),
        br=perm_b(small["ssm_b_re"]), bi=perm_b(small["ssm_b_im"]),
        cr=perm_c(small["ssm_c_re"]), ci=perm_c(small["ssm_c_im"]), dskip=small["ssm_d"].reshape(1, SSM_W))


AG_HOSTS = {"proj_rope": ("w_glu", "w_attn_out", "w_out", "w_ffn_gate"), "mix_out_rms": ("w_ffn_up",),
            "ffn_in_swiglu": ("w_ffn_down",)}
HALVED = ("w_ffn_gate", "w_ffn_up", "w_in")
A2A_HOSTS = {"d_h2_rms": ("w_ffn_down",), "mix_bwd": ("w_ffn_gate:0", "w_out"), "attn_bwd_g1": ("w_glu",),
             "attn_bwd_g2": ("w_attn_out",), "ssm_bwd": ("w_ffn_gate:1", "w_ffn_up:0", "w_ffn_up:1"),
             "mm_g_in1": ("w_in:0",), "d_h0_rms": ("w_in:1",)}
SMALL_HOST = "mm_g_in0"


def _local_step(x, target, w, small, shards=None):
    t = x.shape[0]
    n_samples = t // SEQ
    n_rows = n_samples * SCAN_SEG_PER_SAMPLE
    tabs = _rope_tables()
    w = dict(w)
    fs = _flat_small(small)
    g_mix, g_ffn, g_fin, dskip = fs["g_mix"], fs["g_ffn"], fs["g_fin"], fs["dskip"]
    a_cat, bbc, cc = _ssm_disc(fs["lr"], fs["li"], fs["ldt"], fs["br"], fs["bi"], fs["cr"], fs["ci"])
    big, recv, small_pack = {}, {}, []

    def comm_of(name):
        if shards is None:
            return None
        if name == SMALL_HOST:
            return _ag_comm([(small_pack[0], 0, 0)], [(N_DEV, *small_pack[0].shape)])
        if name in AG_HOSTS:
            names = AG_HOSTS[name]
            return _ag_comm([(shards[n], j, 0) for j, n in enumerate(names)], [(N_DEV, *shards[n].shape) for n in names])
        if name in A2A_HOSTS:
            return _a2a_comm([(big[n].reshape(N_DEV, -1, big[n].shape[1]), 0) for n in A2A_HOSTS[name]])
        return None

    def absorb(name, carried):
        if name == SMALL_HOST:
            recv["small"] = carried[0]
        for n, a3 in zip(AG_HOSTS.get(name, ()), carried):
            w[n] = a3.reshape(-1, a3.shape[2])
        for n, a3 in zip(A2A_HOSTS.get(name, ()), carried):
            recv[n] = a3

    def mm(a, b, mode, name, tm, tn, **kw):
        comm = comm_of(name)
        if comm is None:
            return _mm(a, b, mode, name, tm, tn, **kw)
        out, *carried = _mm(a, b, mode, name, tm, tn, comm=comm, **kw)
        absorb(name, carried)
        return out

    h0, u, gates, *rest = _proj_rope(x, g_mix, w["w_in"], tabs, comm_of("proj_rope"))
    qkv = [rest[3 * g:3 * g + 3] for g in range(3)]
    absorb("proj_rope", rest[9:])
    os_, lses = [], []
    for g in range(3):
        o_g, l_g, carried = _attn_fwd(*qkv[g], g, n_samples, comm_of(f"attn_fwd_g{g}"))
        absorb(f"attn_fwd_g{g}", carried)
        os_.append(o_g)
        lses.append(l_g)
    u_perm = _to_scan_rows(u, n_samples)
    ytot, yg_perm, ein = _ssm_fwd(u_perm, a_cat, bbc, cc, dskip, n_rows)
    yg = _from_scan_rows(yg_perm, n_samples)

    attn, lse_tot, merged, x1, h2, *carried = _mix_out_rms(os_, lses, yg, gates, x, w["w_attn_out"], w["w_glu"], w["w_out"],
                                                           g_ffn, comm_of("mix_out_rms"))
    absorb("mix_out_rms", carried)
    ffn_a, ffn_b, f, *carried = _ffn_in_swiglu(h2, w["w_ffn_gate"], w["w_ffn_up"], comm_of("ffn_in_swiglu"))
    absorb("ffn_in_swiglu", carried)
    dx2, dx2b, loss_blk, g_gfin = _ffn_down_final(f, w["w_ffn_down"], x1, target, g_fin)

    da, db = _d_f_swiglu_bwd(dx2b, w["w_ffn_down"], ffn_a, ffn_b)
    half = D_MODEL // 2
    (big["w_ffn_down"], big["w_ffn_gate:0"], big["w_ffn_gate:1"], big["w_ffn_up:0"],
     big["w_ffn_up:1"]) = _ffn_weight_grads(f, dx2b, da, db, h2)
    dx1, dx1b, g_gffn, *carried = _mm_rms_bwd([da, db], [w["w_ffn_gate"], w["w_ffn_up"]], x1, g_ffn, dx2, "d_h2_rms",
                                              comm_of("d_h2_rms"))
    absorb("d_h2_rms", carried)

    big["w_out"] = mm(merged, dx1b, "tn", "mm_g_out", 256, D_MODEL, out_dtype=BF16)
    dattn_d, dz, dgpre, dattn, dyg, rowdot, *rest = _mix_bwd(dx1b, attn, lse_tot, yg, gates, w["w_attn_out"], w["w_glu"],
                                                             w["w_out"], comm_of("mix_bwd"))
    cot = [(dattn, lse_tot, rowdot), tuple(rest[:3]), tuple(rest[3:6])]
    absorb("mix_bwd", rest[6:])

    big["w_glu"], big["w_attn_out"] = _branch_weight_grads(dz, yg, dattn_d, attn)
    dqs, dks, dvs = [], [], []
    for g in range(3):
        dq_g, dk_g, dv_g, carried = _attn_bwd(*qkv[g], *cot[g], g, n_samples, comm_of(f"attn_bwd_g{g}"))
        absorb(f"attn_bwd_g{g}", carried)
        dqs.append(dq_g)
        dks.append(dk_g)
        dvs.append(dv_g)

    dyg_perm = _to_scan_rows(dyg, n_samples)
    du_perm, g_dskip, da_cat, dbb_full, dc_full, *carried = _ssm_bwd(u_perm, dyg_perm, ytot, dskip, a_cat, bbc, cc, ein,
                                                                   n_rows, comm_of("ssm_bwd"))
    absorb("ssm_bwd", carried)
    du = _from_scan_rows(du_perm, n_samples)
    g_lr, g_li, g_ldt, g_br, g_bi, g_cr, g_ci = _ssm_param_bwd(
        fs["lr"], fs["li"], fs["ldt"], fs["br"], fs["bi"], da_cat, dbb_full, dc_full)

    small_pack.append(_pack_small(dict(lr=g_lr, li=g_li, ldt=g_ldt, br=g_br, bi=g_bi, cr=g_cr, ci=g_ci, dskip=g_dskip,
                                       g_ffn=g_gffn, g_fin=g_gfin, loss=loss_blk)))

    dproj = _pack_dproj(dqs, dks, dvs, du, dgpre, tabs)
    for hf in range(2):
        big[f"w_in:{hf}"] = mm(dproj, h0, "tn", f"mm_g_in{hf}", 256, half, out_dtype=BF16, cols=(hf * half, half))
    grad_x, _, g_gmix, *carried = _mm_rms_bwd([dproj], [w["w_in"]], x, g_mix, dx1, "d_h0_rms", comm_of("d_h0_rms"))
    absorb("d_h0_rms", carried)
    return grad_x, (big if shards is None else recv), small_pack[0], g_gmix


_MESH = pl.DeviceIdType.MESH


def _all_gather(block, name):
    rows, lanes = block.shape

    def body(x_ref, out_ref, send_sems, recv_sems, local_sem):
        x, y, c = lax.axis_index("x"), lax.axis_index("y"), lax.axis_index("c")
        me, sibling = (x, y, c), (x, y, 1 - c)
        chips = [(1 - x, y), (x, 1 - y), (1 - x, 1 - y)]

        def slot(px, py, pc):
            return out_ref.at[4 * px + 2 * py + pc]

        def copy(k, blk, to, src=None):
            return pltpu.make_async_remote_copy(
                src_ref=slot(*blk) if src is None else src, dst_ref=slot(*blk), send_sem=send_sems.at[k],
                recv_sem=recv_sems.at[k], device_id=to, device_id_type=_MESH)

        mine = pltpu.make_async_copy(x_ref, slot(*me), local_sem)
        mine.start()
        first = [copy(0, me, sibling, src=x_ref)]
        first += [copy(1 + j, me, (*chip, c), src=x_ref) for j, chip in enumerate(chips)]
        for cp in first:
            cp.start()
        passed = [copy(4 + j, (*chip, c), sibling) for j, chip in enumerate(chips)]
        for j, chip in enumerate(chips):
            copy(1 + j, (*chip, c), me).wait_recv()
            passed[j].start()
        copy(0, sibling, me).wait_recv()
        for j, chip in enumerate(chips):
            copy(4 + j, (*chip, 1 - c), me).wait_recv()
        for cp in first + passed:
            cp.wait_send()
        mine.wait()

    return _pallas_call(
        body, name=name, out_shape=jax.ShapeDtypeStruct((N_DEV, rows, lanes), block.dtype),
        in_specs=[pl.BlockSpec(memory_space=pl.ANY)], out_specs=pl.BlockSpec(memory_space=pl.ANY),
        scratch_shapes=[pltpu.SemaphoreType.DMA((7,)), pltpu.SemaphoreType.DMA((7,)), pltpu.SemaphoreType.DMA],
    )(block)


def _ag_comm(items, bufs):
    def plan(in_refs, out_refs, send_sems, recv_sems, local_sems):
        x, y, c = lax.axis_index("x"), lax.axis_index("y"), lax.axis_index("c")
        me, sibling = (x, y, c), (x, y, 1 - c)
        chips = [(1 - x, y), (x, 1 - y), (1 - x, 1 - y)]
        plans = []
        for t, (_, buf, slot0) in enumerate(items):
            x_ref, out_ref = in_refs[t], out_refs[buf]

            def slot(px, py, pc, out_ref=out_ref, slot0=slot0):
                return out_ref.at[slot0 + 4 * px + 2 * py + pc]

            def copy(k, blk, to, src=None, t=t, slot=slot):
                return pltpu.make_async_remote_copy(
                    src_ref=slot(*blk) if src is None else src, dst_ref=slot(*blk), send_sem=send_sems.at[7 * t + k],
                    recv_sem=recv_sems.at[7 * t + k], device_id=to, device_id_type=_MESH)

            plans.append(dict(
                mine=pltpu.make_async_copy(x_ref, slot(*me), local_sems.at[t]),
                first=[copy(0, me, sibling, src=x_ref)] + [copy(1 + j, me, (*chip, c), src=x_ref)
                                                           for j, chip in enumerate(chips)],
                passed=[copy(4 + j, (*chip, c), sibling) for j, chip in enumerate(chips)],
                from_ici=[copy(1 + j, (*chip, c), me) for j, chip in enumerate(chips)],
                from_sibling=[copy(0, sibling, me)] + [copy(4 + j, (*chip, 1 - c), me) for j, chip in enumerate(chips)]))
        return plans

    def start(*refs):
        for p in plan(*refs):
            p["mine"].start()
            for cp in p["first"]:
                cp.start()

    def finish(*refs):
        plans = plan(*refs)
        for p in plans:
            for arrived, onward in zip(p["from_ici"], p["passed"]):
                arrived.wait_recv()
                onward.start()
        for p in plans:
            for arrived in p["from_sibling"]:
                arrived.wait_recv()
            for cp in p["first"] + p["passed"]:
                cp.wait_send()
            p["mine"].wait()

    dtype_of = {buf: shard.dtype for shard, buf, _ in items}
    out_shapes = [jax.ShapeDtypeStruct(b, dtype_of[j]) for j, b in enumerate(bufs)]
    return _Comm([it[0] for it in items], out_shapes, 7 * len(items), len(items), start, finish)


def _a2a_comm(items):
    def plan(in_refs, out_refs, send_sems, recv_sems, local_sems):
        x, y, c = lax.axis_index("x"), lax.axis_index("y"), lax.axis_index("c")
        my = 4 * x + 2 * y + c
        copies, locals_ = [], []
        for t, (_, slot0) in enumerate(items):
            s_ref, r_ref = in_refs[t], out_refs[t]
            locals_.append(pltpu.make_async_copy(s_ref.at[slot0 + my], r_ref.at[my], local_sems.at[t]))
            for kk in range(1, N_DEV):
                px = 1 - x if kk & 4 else x
                py = 1 - y if kk & 2 else y
                pc = 1 - c if kk & 1 else c
                copies.append(pltpu.make_async_remote_copy(
                    src_ref=s_ref.at[slot0 + 4 * px + 2 * py + pc], dst_ref=r_ref.at[my],
                    send_sem=send_sems.at[7 * t + kk - 1], recv_sem=recv_sems.at[7 * t + kk - 1],
                    device_id=(px, py, pc), device_id_type=_MESH))
        return copies, locals_

    def start(*refs):
        copies, locals_ = plan(*refs)
        for cp in locals_ + copies:
            cp.start()

    def finish(*refs):
        copies, locals_ = plan(*refs)
        for cp in copies + locals_:
            cp.wait()

    out_shapes = [jax.ShapeDtypeStruct((N_DEV,) + it[0].shape[1:], it[0].dtype) for it in items]
    return _Comm([it[0] for it in items], out_shapes, 7 * len(items), len(items), start, finish)


def _adam_math(g, w, m, v):
    m_new = ADAM_B1 * m + (1.0 - ADAM_B1) * g
    v_new = ADAM_B2 * v + (1.0 - ADAM_B2) * jnp.square(g)
    m_hat = m_new / (1.0 - ADAM_B1 ** ADAM_STEP)
    v_hat = v_new / (1.0 - ADAM_B2 ** ADAM_STEP)
    return -ADAM_LR * (m_hat / (jnp.sqrt(v_hat) + ADAM_EPS) + ADAM_WD * w), m_new, v_new


def _sum_partials(parts, name, tm):
    n, rows, _ = parts[0].shape
    widths = [p.shape[2] for p in parts]

    def body(*refs):
        g_ref, off = refs[-1], 0
        for p_ref, wd in zip(refs[:-1], widths):
            g = p_ref[0].astype(F32)
            for s in range(1, n):
                g = g + p_ref[s].astype(F32)
            g_ref[:, off:off + wd] = g
            off += wd

    return _pallas_call(
        body, name=name, grid=(rows // tm,), in_specs=[pl.BlockSpec((n, tm, wd), lambda i: (0, i, 0)) for wd in widths],
        out_specs=pl.BlockSpec((tm, sum(widths)), lambda i: (i, 0)),
        out_shape=jax.ShapeDtypeStruct((rows, sum(widths)), F32),
        compiler_params=pltpu.CompilerParams(dimension_semantics=("parallel",), vmem_limit_bytes=VMEM_MID),
    )(*parts)


def _adam(parts, w, m, v, name, tm):
    n, rows, _ = parts[0].shape
    widths = [p.shape[2] for p in parts]
    cols = sum(widths)

    def body(*refs):
        p_refs, (w_ref, m_ref, v_ref, g_ref, d_ref, nm_ref, nv_ref) = refs[:len(parts)], refs[len(parts):]
        off = 0
        for p_ref, wd in zip(p_refs, widths):
            g = p_ref[0].astype(F32)
            for s in range(1, n):
                g = g + p_ref[s].astype(F32)
            sl = slice(off, off + wd)
            g_ref[:, sl] = g
            d_ref[:, sl], nm_ref[:, sl], nv_ref[:, sl] = _adam_math(g, w_ref[:, sl], m_ref[:, sl], v_ref[:, sl])
            off += wd

    assert rows % tm == 0
    row = pl.BlockSpec((tm, cols), lambda i: (i, 0))
    shp = jax.ShapeDtypeStruct((rows, cols), F32)
    return _pallas_call(
        body, name=name, grid=(rows // tm,),
        in_specs=[pl.BlockSpec((n, tm, wd), lambda i: (0, i, 0)) for wd in widths] + [row, row, row],
        out_specs=[row] * 4, out_shape=[shp] * 4,
        compiler_params=pltpu.CompilerParams(dimension_semantics=("parallel",), vmem_limit_bytes=VMEM_MID),
    )(*parts, w, m, v)


_PK_LR, _PK_LI, _PK_GAINS, _PK_MISC, _PK_BR, _PK_BI, _PK_CR, _PK_CI, _PK_ROWS = 0, 1, 2, 3, 8, 24, 40, 56, 72
_PK_LDT_LANE, _PK_LOSS_LANE = D_MODEL + SSM_W, D_MODEL + SSM_W + LANES


def _pack_small(sg):
    names = ("lr", "li", "g_ffn", "g_fin", "dskip", "ldt", "loss", "br", "bi", "cr", "ci")

    def body(lr, li, gffn, gfin, dskip, ldt, loss, br, bi, cr, ci, o_ref):
        o_ref[...] = jnp.zeros_like(o_ref)
        o_ref[_PK_LR:_PK_LR + 1, :] = lr[...]
        o_ref[_PK_LI:_PK_LI + 1, :] = li[...]
        o_ref[_PK_GAINS:_PK_GAINS + 1, D_MODEL:] = gffn[...]
        o_ref[_PK_MISC:_PK_MISC + 1, :D_MODEL] = gfin[...]
        o_ref[_PK_MISC:_PK_MISC + 1, D_MODEL:D_MODEL + SSM_W] = dskip[...]
        o_ref[_PK_MISC:_PK_MISC + 1, _PK_LDT_LANE:_PK_LDT_LANE + LANES] = ldt[0:1, :]
        o_ref[_PK_MISC:_PK_MISC + 1, _PK_LOSS_LANE:_PK_LOSS_LANE + LANES] = loss[0:1, :]
        o_ref[_PK_BR:_PK_BR + SSM_CH, :] = br[...]
        o_ref[_PK_BI:_PK_BI + SSM_CH, :] = bi[...]
        o_ref[_PK_CR:_PK_CR + SSM_CH, :] = cr[...]
        o_ref[_PK_CI:_PK_CI + SSM_CH, :] = ci[...]

    return _pallas_call(body, name="pack_small", out_shape=jax.ShapeDtypeStruct((_PK_ROWS, N_STATE), F32))(
        *[sg[n] for n in names])


def _unpack_small(s, g_mix):
    unflat_b = unflat_c = lambda a: a.reshape(SSM_CH, SSM_GROUPS, SSM_STATE).transpose(1, 0, 2)[None]
    grads = {
        "norm_mix_g": g_mix, "norm_ffn_g": s[_PK_GAINS, D_MODEL:].reshape(1, D_MODEL),
        "norm_final_g": s[_PK_MISC, :D_MODEL].reshape(1, D_MODEL),
        "ssm_a_re": s[_PK_LR].reshape(1, SSM_GROUPS, SSM_STATE), "ssm_a_im": s[_PK_LI].reshape(1, SSM_GROUPS, SSM_STATE),
        "ssm_log_dt": s[_PK_MISC, _PK_LDT_LANE:_PK_LDT_LANE + SSM_GROUPS].reshape(1, SSM_GROUPS),
        "ssm_d": s[_PK_MISC, D_MODEL:D_MODEL + SSM_W].reshape(1, SSM_GROUPS, SSM_CH),
        "ssm_b_re": unflat_b(s[_PK_BR:_PK_BR + SSM_CH]), "ssm_b_im": unflat_b(s[_PK_BI:_PK_BI + SSM_CH]),
        "ssm_c_re": unflat_c(s[_PK_CR:_PK_CR + SSM_CH]), "ssm_c_im": unflat_c(s[_PK_CI:_PK_CI + SSM_CH]),
    }
    return s[_PK_MISC, _PK_LOSS_LANE], grads


def _stored(name, a):
    if name in ("ssm_b_re", "ssm_b_im"):
        return a.transpose(0, 1, 3, 2)
    return a.reshape(1, -1) if a.ndim == 1 else a


def _unstored(name, a, like):
    return a.transpose(0, 1, 3, 2) if name in ("ssm_b_re", "ssm_b_im") else a.reshape(like.shape)


def _adam_small(grads, wts, moms, vars_):
    n = len(SMALL_WEIGHTS)

    def body(*refs):
        ins, outs = refs[:4 * n], refs[4 * n:]
        for i in range(n):
            g, w, m, v = (ins[j * n + i][...] for j in range(4))
            outs[i][...], outs[n + i][...], outs[2 * n + i][...] = _adam_math(g, w, m, v)

    operands = [grads[k] if d is grads else _stored(k, d[k]) for d in (grads, wts, moms, vars_) for k in SMALL_WEIGHTS]
    shapes = [jax.ShapeDtypeStruct(_stored(k, wts[k]).shape, F32) for k in SMALL_WEIGHTS] * 3
    res = _pallas_call(body, name="adam_small", out_shape=shapes,
                         compiler_params=pltpu.CompilerParams(vmem_limit_bytes=VMEM_BIG))(*operands)
    out = {}
    for j, kind in enumerate(("delta", "new_m", "new_v")):
        for i, k in enumerate(SMALL_WEIGHTS):
            out[kind, k] = _unstored(k, res[j * n + i], wts[k])
    return out


def kernel(x, norm_mix_g, w_in, ssm_a_re, ssm_a_im, ssm_log_dt, ssm_b_re, ssm_b_im, ssm_c_re, ssm_c_im, ssm_d, w_glu, w_attn_out, w_out, norm_ffn_g, w_ffn_gate, w_ffn_up, w_ffn_down, norm_final_g, loss_target, m_norm_mix_g, m_w_in, m_ssm_a_re, m_ssm_a_im, m_ssm_log_dt, m_ssm_b_re, m_ssm_b_im, m_ssm_c_re, m_ssm_c_im, m_ssm_d, m_w_glu, m_w_attn_out, m_w_out, m_norm_ffn_g, m_w_ffn_gate, m_w_ffn_up, m_w_ffn_down, m_norm_final_g, v_norm_mix_g, v_w_in, v_ssm_a_re, v_ssm_a_im, v_ssm_log_dt, v_ssm_b_re, v_ssm_b_im, v_ssm_c_re, v_ssm_c_im, v_ssm_d, v_w_glu, v_w_attn_out, v_w_out, v_norm_ffn_g, v_w_ffn_gate, v_w_ffn_up, v_w_ffn_down, v_norm_final_g):
    args = dict(locals())
    wts = {n: args[n] for n in ALL_WEIGHTS}
    moms = {n: args["m_" + n] for n in ALL_WEIGHTS}
    vars_ = {n: args["v_" + n] for n in ALL_WEIGHTS}
    n_samples = x.shape[0]
    t = n_samples * SEQ

    shards = {n: (wts[n][0] if n in ROW_SHARDED else wts[n][0].T).astype(BF16) for n in BIG_WEIGHTS}
    w_in_t = _all_gather(shards["w_in"], "allgather_w_in").reshape(IN_W, D_MODEL)

    small = {n: wts[n] for n in SMALL_WEIGHTS}
    grad_x, recv, _, g_mix_part = _local_step(x.reshape(t, D_MODEL), loss_target.reshape(t, D_MODEL), {"w_in": w_in_t},
                                              small, shards)

    results = {}
    for n in BIG_WEIGHTS:
        c, k = shards[n].shape
        w2, m2, v2 = wts[n][0], moms[n][0], vars_[n][0]
        if n in ROW_SHARDED:
            res = _adam([recv[n]], w2, m2, v2, "adam_" + n, c // 2)
        elif n in HALVED:
            res = _adam([recv[f"{n}:{hf}"] for hf in range(2)], w2.T, m2.T, v2.T, "adam_" + n, c // 2)
            res = [a.T for a in res]
        else:
            g_t = _sum_partials([recv[n]], "sum_" + n, c // 2)
            res = _adam([g_t.T[None]], w2, m2, v2, "adam_" + n, k // 2)
        for kind, a in zip(("grad", "delta", "new_m", "new_v"), res):
            results[kind, n] = a[None]

    g_mix_all = _all_gather(jnp.pad(g_mix_part, ((0, 7), (0, 0))), "allgather_g_mix")
    g_mix = _sum_partials([g_mix_all], "sum_g_mix", 8)[0:1]
    loss, sgrads = _unpack_small(_sum_partials([recv["small"]], "sum_small", _PK_ROWS), g_mix)
    for n in SMALL_WEIGHTS:
        results["grad", n] = _unstored(n, sgrads[n], wts[n])
    results.update(_adam_small(sgrads, wts, moms, vars_))
    outs = [loss, grad_x.reshape(x.shape)]
    for kind in ("grad", "delta", "new_m", "new_v"):
        outs += [results[kind, n] for n in ALL_WEIGHTS]
    return tuple(outs)
```

```python
import functools
import math

import jax
import jax.numpy as jnp
from jax import lax
from jax.experimental import pallas as pl
from jax.experimental.pallas import tpu as pltpu

F32 = jnp.float32
BF16 = jnp.bfloat16
MXU_DTYPE = jnp.bfloat16

N_DEV = 8
D_MODEL = 1024
SEQ = 2048
HEAD_DIM = 64
HEADS_PER_GROUP = 4
GROUP_W = HEADS_PER_GROUP * HEAD_DIM
DILATIONS = (1, 4, 16)
QKV_W = 3 * len(DILATIONS) * GROUP_W
Q_W = len(DILATIONS) * GROUP_W
ATT_BLOCK = 128
ROPE_DIM = 16
ROPE_THETA = 500000.0
SSM_W = 512
SSM_GROUPS = 32
SSM_CH = 16
SSM_STATE = 64
N_STATE = SSM_GROUPS * SSM_STATE
D_FF = 2816
IN_W = QKV_W + SSM_W + 2 * D_MODEL
RMS_EPS = 1e-6
NEG_INF = -1e30
LANES = 128

SCAN_SEG_PER_SAMPLE = 8
SCAN_LEN = SEQ // SCAN_SEG_PER_SAMPLE
SCAN_WC = 512
SCAN_NBLK = N_STATE // SCAN_WC
SCAN_CH = SSM_W // SCAN_NBLK
SCAN_CHUNK = 32

ADAM_LR = 0.001
ADAM_B1 = 0.9
ADAM_B2 = 0.999
ADAM_EPS = 1e-08
ADAM_WD = 0.01
ADAM_STEP = 10

VMEM_BIG = 48 * 1024 * 1024
VMEM_MID = 32 * 1024 * 1024

BIG_WEIGHTS = ("w_in", "w_glu", "w_attn_out", "w_out", "w_ffn_gate", "w_ffn_up", "w_ffn_down")
ROW_SHARDED = ("w_out", "w_ffn_down")
SMALL_WEIGHTS = ("norm_mix_g", "ssm_a_re", "ssm_a_im", "ssm_log_dt", "ssm_b_re", "ssm_b_im", "ssm_c_re", "ssm_c_im",
                 "ssm_d", "norm_ffn_g", "norm_final_g")
ALL_WEIGHTS = ("norm_mix_g", "w_in", "ssm_a_re", "ssm_a_im", "ssm_log_dt", "ssm_b_re", "ssm_b_im", "ssm_c_re", "ssm_c_im",
               "ssm_d", "w_glu", "w_attn_out", "w_out", "norm_ffn_g", "w_ffn_gate", "w_ffn_up", "w_ffn_down", "norm_final_g")


def _sigmoid(x):
    return 1.0 / (1.0 + jnp.exp(-x))


def _pallas_call(body, *, out_shape, **kw):
    single = not isinstance(out_shape, (list, tuple))
    shapes = [pltpu.HBM(s.shape, s.dtype) for s in ([out_shape] if single else out_shape)]
    call = pl.pallas_call(body, out_shape=shapes[0] if single else shapes, **kw)
    return lambda *operands: call(*[pltpu.with_memory_space_constraint(o, pltpu.HBM) for o in operands])


class _Comm:
    def __init__(self, ins, out_shapes, n_sem, n_local, start, finish):
        self.ins, self.out_shapes, self.n_sem, self.n_local = ins, out_shapes, n_sem, n_local
        self.start, self.finish = start, finish


def _mm(a, b, mode, name, tm, tn, out_dtype=F32, add=None, vmem=VMEM_BIG, comm=None, cols=None):
    if mode == "nn":
        (m, k), (_, n) = a.shape, b.shape
        a_spec = pl.BlockSpec((tm, k), lambda i, j: (i, 0))
        b_spec = pl.BlockSpec((k, tn), lambda i, j: (0, j))
        dims = (((1,), (0,)), ((), ()))
    elif mode == "nt":
        (m, k), (n, _) = a.shape, b.shape
        a_spec = pl.BlockSpec((tm, k), lambda i, j: (i, 0))
        b_spec = pl.BlockSpec((tn, k), lambda i, j: (j, 0))
        dims = (((1,), (1,)), ((), ()))
    else:
        (k, m), (_, n) = a.shape, b.shape
        first, n = cols if cols else (0, n)
        a_spec = pl.BlockSpec((k, tm), lambda i, j: (0, i))
        b_spec = pl.BlockSpec((k, tn), lambda i, j: (0, j + first // tn))
        dims = (((0,), (0,)), ((), ()))
    assert m % tm == 0 and n % tn == 0, (name, m, n, tm, tn)
    o_spec = pl.BlockSpec((tm, tn), lambda i, j: (i, j))
    has_add = add is not None

    def body(*refs):
        a_ref, b_ref, o_ref = refs[0], refs[1], refs[-1]
        acc = lax.dot_general(a_ref[...].astype(MXU_DTYPE), b_ref[...].astype(MXU_DTYPE), dims,
                              preferred_element_type=F32)
        if has_add:
            acc = acc + refs[2][...]
        o_ref[...] = acc.astype(out_dtype)

    ins = [a, b] + ([add] if has_add else [])
    in_specs = [a_spec, b_spec] + ([o_spec] if has_add else [])
    return _grid_call(body, name, (m // tm, n // tn), ins, in_specs, [o_spec],
                      [jax.ShapeDtypeStruct((m, n), out_dtype)], vmem, comm)


def _grid_call(body, name, grid, ins, in_specs, out_specs, out_shapes, vmem, comm=None, sequential=False, scratch=()):
    if comm is None:
        single = len(out_shapes) == 1
        semantics = ("arbitrary", "arbitrary") if sequential else ("parallel", "parallel")
        return _pallas_call(
            body, name=name, grid=grid, in_specs=in_specs, out_specs=out_specs[0] if single else out_specs,
            out_shape=out_shapes[0] if single else out_shapes, scratch_shapes=list(scratch),
            compiler_params=pltpu.CompilerParams(dimension_semantics=semantics, vmem_limit_bytes=vmem),
        )(*ins)
    n_in, n_out, n_cin, n_cout = len(ins), len(out_shapes), len(comm.ins), len(comm.out_shapes)
    n_io = n_in + n_cin + n_out + n_cout

    def carrying(*refs):
        own = refs[:n_in] + refs[n_in + n_cin:n_in + n_cin + n_out] + refs[n_io:len(refs) - 3]
        c_args = (refs[n_in:n_in + n_cin], refs[n_in + n_cin + n_out:n_io], *refs[-3:])

        @pl.when((pl.program_id(0) == 0) & (pl.program_id(1) == 0))
        def _():
            comm.start(*c_args)

        body(*own)

        @pl.when((pl.program_id(0) == grid[0] - 1) & (pl.program_id(1) == grid[1] - 1))
        def _():
            comm.finish(*c_args)

    hbm = pl.BlockSpec(memory_space=pl.ANY)
    return _pallas_call(
        carrying, name=name, grid=grid, in_specs=list(in_specs) + [hbm] * n_cin,
        out_specs=list(out_specs) + [hbm] * n_cout, out_shape=list(out_shapes) + list(comm.out_shapes),
        scratch_shapes=list(scratch) + [pltpu.SemaphoreType.DMA((comm.n_sem,)), pltpu.SemaphoreType.DMA((comm.n_sem,)),
                                        pltpu.SemaphoreType.DMA((comm.n_local,))],
        compiler_params=pltpu.CompilerParams(dimension_semantics=("arbitrary", "arbitrary"), vmem_limit_bytes=vmem),
    )(*ins, *comm.ins)


def _rows(body, name, n_rows, tm, ins, outs, vmem=VMEM_MID, scratch=()):
    assert n_rows % tm == 0
    arrays, in_specs = [], []
    for kind, arr in ins:
        arrays.append(arr)
        if kind == "row":
            assert n_rows % arr.shape[0] == 0, (name, arr.shape)
            in_specs.append(pl.BlockSpec((tm * arr.shape[0] // n_rows, arr.shape[1]), lambda i: (i, 0)))
        elif kind == "tab":
            nblk = arr.shape[0] // tm
            in_specs.append(pl.BlockSpec((tm, arr.shape[1]), lambda i, nblk=nblk: (i % nblk, 0)))
        else:
            in_specs.append(pl.BlockSpec(arr.shape, lambda i, nd=arr.ndim: (0,) * nd))
    out_specs, out_shape = [], []
    for kind, shp, dt in outs:
        if kind == "row":
            out_specs.append(pl.BlockSpec((tm, shp), lambda i: (i, 0)))
            out_shape.append(jax.ShapeDtypeStruct((n_rows, shp), dt))
        elif kind == "dil":
            d, wd = shp
            out_specs.append(pl.BlockSpec((tm // d, d * wd), lambda i: (i, 0)))
            out_shape.append(jax.ShapeDtypeStruct((n_rows // d, d * wd), dt))
        else:
            out_specs.append(pl.BlockSpec(shp, lambda i, nd=len(shp): (0,) * nd))
            out_shape.append(jax.ShapeDtypeStruct(shp, dt))
    res = _pallas_call(
        body, name=name, grid=(n_rows // tm,), in_specs=in_specs, out_specs=out_specs, out_shape=out_shape,
        scratch_shapes=list(scratch),
        compiler_params=pltpu.CompilerParams(dimension_semantics=("arbitrary",), vmem_limit_bytes=vmem),
    )(*arrays)
    return res


def _gather_residue(stage, ch, r, d, n):
    return stage[ch, pl.ds(r, n, stride=d), :] if d > 1 else stage[ch]


def _scatter_residue(stage, ch, r, d, n, val):
    if d > 1:
        stage[ch, pl.ds(r, n, stride=d), :] = val
    else:
        stage[ch] = val


def _lane_chunk(ch):
    return slice(ch * LANES, (ch + 1) * LANES)


def _rope_tables():
    half = ROPE_DIM // 2
    inv = jnp.power(jnp.float32(ROPE_THETA), -jnp.arange(half, dtype=F32) * 2.0 / ROPE_DIM)
    ang = jnp.arange(SEQ, dtype=F32)[:, None] * inv[None, :]
    lane = jnp.arange(LANES) % HEAD_DIM
    cosl = jnp.cos(ang)[:, lane % half]
    sinl = jnp.sin(ang)[:, lane % half]
    tab_c = jnp.where(lane < ROPE_DIM, cosl, 1.0)
    tab_lo = jnp.where(lane < half, -sinl, 0.0)
    tab_hi = jnp.where((lane >= half) & (lane < ROPE_DIM), sinl, 0.0)
    return tab_c.astype(F32), tab_lo.astype(F32), tab_hi.astype(F32)


def _rope_apply(t, tc, tlo, thi):
    half = ROPE_DIM // 2
    return t * tc + pltpu.roll(t, LANES - half, 1) * tlo + pltpu.roll(t, half, 1) * thi


def _rope_transpose(dt, tc, tlo, thi):
    half = ROPE_DIM // 2
    return dt * tc + pltpu.roll(dt * tlo, half, 1) + pltpu.roll(dt * thi, LANES - half, 1)


def _pack_dproj(dqs, dks, dvs, du, dgpre, tabs):
    tm = 256

    def body(*refs):
        dq_refs, dk_refs, dv_refs = refs[0:3], refs[3:6], refs[6:9]
        du_ref, dg_ref, tc_ref, tlo_ref, thi_ref, o_ref, stage = refs[9:16]
        n_ch = QKV_W // LANES
        halves = GROUP_W // LANES
        for grp, d in enumerate(DILATIONS):
            for which, src in enumerate((dq_refs[grp], dk_refs[grp], dv_refs[grp])):
                for res in range(d):
                    for half in range(halves):
                        _scatter_residue(stage, which * (n_ch // 3) + grp * halves + half, res, d, tm // d,
                                         src[:, _lane_chunk(res * halves + half)])
        tc, tlo, thi = tc_ref[...], tlo_ref[...], thi_ref[...]
        for ch in range(n_ch):
            piece = stage[ch]
            o_ref[:, _lane_chunk(ch)] = (_rope_transpose(piece, tc, tlo, thi) if ch < 2 * n_ch // 3 else piece).astype(BF16)
        o_ref[:, QKV_W:QKV_W + SSM_W] = du_ref[...].astype(BF16)
        o_ref[:, QKV_W + SSM_W:] = dg_ref[...].astype(BF16)

    t = du.shape[0]
    ins = [("row", a) for a in (*dqs, *dks, *dvs, du, dgpre)] + [("tab", tb) for tb in tabs]
    return _rows(body, "pack_dproj", t, tm, ins, [("row", IN_W, BF16)],
                 scratch=[pltpu.VMEM((QKV_W // LANES, tm, LANES), F32)])[0]


def _merge_groups(o_refs, l_refs, a_ref, lt_ref, nat, tm):
    halves = GROUP_W // LANES
    for grp, d in enumerate(DILATIONS[1:], start=1):
        for j, src in enumerate((o_refs[grp], l_refs[grp])):
            for res in range(d):
                for half in range(halves):
                    _scatter_residue(nat, (grp - 1) * 4 + j * 2 + half, res, d, tm // d,
                                     src[:, _lane_chunk(res * halves + half)])
    for half in range(halves):
        sl = _lane_chunk(half)
        la, lb, lc = l_refs[0][:, sl], nat[2 + half], nat[6 + half]
        m = jnp.maximum(jnp.maximum(la, lb), lc)
        ea, eb, ec = jnp.exp(la - m), jnp.exp(lb - m), jnp.exp(lc - m)
        ssum = ea + eb + ec
        a_ref[:, sl] = (ea / ssum) * o_refs[0][:, sl] + (eb / ssum) * nat[half] + (ec / ssum) * nat[4 + half]
        lt_ref[:, sl] = m + jnp.log(ssum)


def _head_sum_matrix():
    r = jnp.arange(GROUP_W) // HEAD_DIM
    return (r[:, None] == r[None, :]).astype(F32)


def _attention_cotangents(da, attn, lt, ones, rd_ref, dil, stage, tm):
    halves = GROUP_W // LANES
    rd = jnp.dot(da * attn, ones, preferred_element_type=F32, precision=lax.Precision.HIGHEST)
    rd_ref[...] = rd
    for half in range(halves):
        for j, val in enumerate((da, lt, rd)):
            stage[2 * j + half] = val[:, _lane_chunk(half)]
    for grp, d in enumerate(DILATIONS[1:], start=1):
        for j in range(3):
            for res in range(d):
                for half in range(halves):
                    dil[3 * (grp - 1) + j][:, _lane_chunk(res * halves + half)] = _gather_residue(
                        stage, 2 * j + half, res, d, tm // d)


_GELU_C = math.sqrt(2.0 / math.pi)


def _head_masks():
    lane = lax.broadcasted_iota(jnp.int32, (1, GROUP_W), 1)
    return [(lane // HEAD_DIM) == h for h in range(HEADS_PER_GROUP)]


def _stack_heads(blk, masks, fill=0.0):
    return jnp.concatenate([jnp.where(mk, blk, jnp.full_like(blk, fill)) for mk in masks], axis=0)


def _unstack_heads(stacked, masks):
    rows = stacked.shape[0] // len(masks)
    out = stacked[:rows]
    for h in range(1, len(masks)):
        out = jnp.where(masks[h], stacked[h * rows:(h + 1) * rows], out)
    return out


def _band_mask(first):
    nk = ATT_BLOCK if first else 2 * ATT_BLOCK
    qi = lax.broadcasted_iota(jnp.int32, (ATT_BLOCK, nk), 0)
    ki = lax.broadcasted_iota(jnp.int32, (ATT_BLOCK, nk), 1)
    dist = qi - ki + (0 if first else ATT_BLOCK)
    return (dist >= 0) & (dist <= ATT_BLOCK)


_NT = (((1,), (1,)), ((), ()))
_TN = (((0,), (0,)), ((), ()))


def _residues_per_step(d):
    return 4 if d >= 16 else 1


def _attn_fwd(q, k, v, group, n_samples, comm=None):
    d = DILATIONS[group]
    length = SEQ // d
    nb = length // ATT_BLOCK

    rps = _residues_per_step(d)

    def body(q_ref, k_ref, v_ref, o_ref, l_ref):
        for rl in range(rps):
            residue(q_ref, k_ref, v_ref, o_ref, l_ref, slice(rl * GROUP_W, (rl + 1) * GROUP_W))

    def residue(q_ref, k_ref, v_ref, o_ref, l_ref, cols):
        masks = _head_masks()

        def block(qs, ks, first):
            nk = ATT_BLOCK if first else 2 * ATT_BLOCK
            qb = q_ref[0, pl.ds(qs, ATT_BLOCK), cols]
            kc = k_ref[0, pl.ds(ks, nk), cols]
            vc = v_ref[0, pl.ds(ks, nk), cols]
            q4 = _stack_heads(qb, masks)
            valid = jnp.tile(_band_mask(first), (HEADS_PER_GROUP, 1))
            s = lax.dot_general(q4, kc, _NT, preferred_element_type=F32) * (HEAD_DIM ** -0.5)
            s = jnp.where(valid, s, NEG_INF)
            m = jnp.max(s, axis=-1, keepdims=True)
            p = jnp.exp(s - m)
            l = jnp.sum(p, axis=-1, keepdims=True)
            o4 = jnp.dot(p.astype(MXU_DTYPE), vc, preferred_element_type=F32) / l
            lse4 = jnp.broadcast_to(m + jnp.log(l), o4.shape)
            o_ref[0, pl.ds(qs, ATT_BLOCK), cols] = _unstack_heads(o4, masks)
            l_ref[0, pl.ds(qs, ATT_BLOCK), cols] = _unstack_heads(lse4, masks)

        block(0, 0, True)
        if nb > 1:
            def loop(n, carry):
                block(pl.multiple_of(n * ATT_BLOCK, ATT_BLOCK), pl.multiple_of((n - 1) * ATT_BLOCK, ATT_BLOCK), False)
                return carry

            lax.fori_loop(1, nb, loop, 0)

    per_sample = lambda a: a.reshape(n_samples, length, d * GROUP_W)
    spec = pl.BlockSpec((1, length, rps * GROUP_W), lambda b, r: (b, 0, r))
    shp = jax.ShapeDtypeStruct((n_samples, length, d * GROUP_W), F32)
    o, lse, *carried = _grid_call(body, f"attn_fwd_g{group}", (n_samples, d // rps), [per_sample(a) for a in (q, k, v)],
                                  [spec] * 3, [spec] * 2, [shp, shp], VMEM_MID, comm)
    flat = lambda a: a.reshape(n_samples * length, d * GROUP_W)
    return flat(o), flat(lse), carried


def _attn_bwd(q, k, v, dattn, lse_tot, rowdot, group, n_samples, comm=None):
    d = DILATIONS[group]
    length = SEQ // d
    nb = length // ATT_BLOCK

    rps = _residues_per_step(d)

    def body(q_ref, k_ref, v_ref, da_ref, lt_ref, rd_ref, dq_ref, dk_ref, dv_ref):
        dk_ref[...] = jnp.zeros_like(dk_ref)
        dv_ref[...] = jnp.zeros_like(dv_ref)
        for rl in range(rps):
            residue(q_ref, k_ref, v_ref, da_ref, lt_ref, rd_ref, dq_ref, dk_ref, dv_ref,
                    slice(rl * GROUP_W, (rl + 1) * GROUP_W))

    def residue(q_ref, k_ref, v_ref, da_ref, lt_ref, rd_ref, dq_ref, dk_ref, dv_ref, cols):
        masks = _head_masks()

        def block(qs, ks, first):
            nk = ATT_BLOCK if first else 2 * ATT_BLOCK
            qb = q_ref[0, pl.ds(qs, ATT_BLOCK), cols]
            kc = k_ref[0, pl.ds(ks, nk), cols]
            vc = v_ref[0, pl.ds(ks, nk), cols]
            da = da_ref[0, pl.ds(qs, ATT_BLOCK), cols]
            lt = lt_ref[0, pl.ds(qs, ATT_BLOCK), cols]
            rd = rd_ref[0, pl.ds(qs, ATT_BLOCK), cols]
            q4 = _stack_heads(qb, masks)
            da4 = _stack_heads(da, masks).astype(MXU_DTYPE)
            lt4 = jnp.max(_stack_heads(lt, masks, -jnp.inf), axis=-1, keepdims=True)
            rd4 = jnp.max(_stack_heads(rd, masks, -jnp.inf), axis=-1, keepdims=True)
            valid = jnp.tile(_band_mask(first), (HEADS_PER_GROUP, 1))
            s = lax.dot_general(q4, kc, _NT, preferred_element_type=F32) * (HEAD_DIM ** -0.5)
            s = jnp.where(valid, s, NEG_INF)
            p = jnp.exp(s - lt4)
            dp = lax.dot_general(da4, vc, _NT, preferred_element_type=F32)
            ds = (p * (dp - rd4) * (HEAD_DIM ** -0.5)).astype(MXU_DTYPE)
            dq_ref[0, pl.ds(qs, ATT_BLOCK), cols] = _unstack_heads(jnp.dot(ds, kc, preferred_element_type=F32), masks)
            dk_ref[0, pl.ds(ks, nk), cols] += lax.dot_general(ds, q4, _TN, preferred_element_type=F32)
            dv_ref[0, pl.ds(ks, nk), cols] += lax.dot_general(p.astype(MXU_DTYPE), da4, _TN, preferred_element_type=F32)

        block(0, 0, True)
        if nb > 1:
            def loop(n, carry):
                block(pl.multiple_of(n * ATT_BLOCK, ATT_BLOCK), pl.multiple_of((n - 1) * ATT_BLOCK, ATT_BLOCK), False)
                return carry

            lax.fori_loop(1, nb, loop, 0)

    per_sample = lambda a: a.reshape(n_samples, length, d * GROUP_W)
    spec = pl.BlockSpec((1, length, rps * GROUP_W), lambda b, r: (b, 0, r))
    shp = jax.ShapeDtypeStruct((n_samples, length, d * GROUP_W), F32)
    dq, dk, dv, *carried = _grid_call(
        body, f"attn_bwd_g{group}", (n_samples, d // rps), [per_sample(a) for a in (q, k, v, dattn, lse_tot, rowdot)],
        [spec] * 6, [spec] * 3, [shp, shp, shp], VMEM_MID, comm)
    flat = lambda a: a.reshape(n_samples * length, d * GROUP_W)
    return flat(dq), flat(dk), flat(dv), carried


def _disc(lr, li, ldt, br, bi):
    dt = jnp.exp(ldt)
    mag = jnp.exp(lr * dt)
    ab_re, ab_im = mag * jnp.cos(li * dt), mag * jnp.sin(li * dt)
    den = lr * lr + li * li
    nr, ni = ab_re - 1.0, ab_im
    f_re = (nr * lr + ni * li) / den
    f_im = (ni * lr - nr * li) / den
    return ab_re, ab_im, f_re * br - f_im * bi, f_re * bi + f_im * br


def _state_mask():
    row_g = lax.broadcasted_iota(jnp.int32, (SCAN_CH, SCAN_WC), 0) // SSM_CH
    col_g = lax.broadcasted_iota(jnp.int32, (SCAN_CH, SCAN_WC), 1) // SSM_STATE
    return row_g == col_g


def _ssm_disc(lr, li, ldt, br, bi, cr, ci):
    w = SCAN_WC

    def body(lr_ref, li_ref, ldt_ref, br_ref, bi_ref, cr_ref, ci_ref, a_ref, bb_ref, c_ref):
        ar, ai, bbr, bbi = _disc(lr_ref[...], li_ref[...], ldt_ref[...], br_ref[...], bi_ref[...])
        crv, civ = cr_ref[...], ci_ref[...]
        mask = _state_mask()
        for cb in range(SCAN_NBLK):
            sl = slice(cb * w, (cb + 1) * w)
            rows = slice(cb * SCAN_CH, (cb + 1) * SCAN_CH)
            dense = lambda comp: jnp.where(mask, jnp.tile(comp[:, sl], (SCAN_CH // SSM_CH, 1)), 0.0)
            a_ref[:, 2 * cb * w:(2 * cb + 1) * w] = ar[:, sl]
            a_ref[:, (2 * cb + 1) * w:(2 * cb + 2) * w] = ai[:, sl]
            bb_ref[rows, :w] = dense(bbr).astype(MXU_DTYPE)
            bb_ref[rows, w:] = dense(bbi).astype(MXU_DTYPE)
            c_ref[rows, :w] = dense(crv).astype(MXU_DTYPE)
            c_ref[rows, w:] = (-dense(civ)).astype(MXU_DTYPE)

    return _pallas_call(
        body, name="ssm_disc",
        out_shape=[jax.ShapeDtypeStruct((1, 2 * N_STATE), F32), jax.ShapeDtypeStruct((SSM_W, 2 * w), MXU_DTYPE),
                   jax.ShapeDtypeStruct((SSM_W, 2 * w), MXU_DTYPE)],
        compiler_params=pltpu.CompilerParams(vmem_limit_bytes=VMEM_MID),
    )(lr, li, ldt, br, bi, cr, ci)


def _group_indicator():
    s = jnp.arange(N_STATE) // SSM_STATE
    return (s[:, None] == jnp.arange(LANES)[None, :]).astype(F32)


def _ssm_param_bwd(lr, li, ldt, br, bi, da_cat, dbb_full, dc_full):
    w = SCAN_WC

    def body(lr_ref, li_ref, ldt_ref, br_ref, bi_ref, da_ref, dbb_ref, dc_ref, ind_ref,
             glr_ref, gli_ref, gldt_ref, gbr_ref, gbi_ref, gcr_ref, gci_ref):
        mask = _state_mask()

        def diag_parts(ref):
            res = ([], [])
            for cb in range(SCAN_NBLK):
                for part in range(2):
                    blk = ref[cb * SCAN_CH:(cb + 1) * SCAN_CH, part * w:(part + 1) * w]
                    res[part].append(jnp.sum(jnp.where(mask, blk, 0.0).reshape(SCAN_CH // SSM_CH, SSM_CH, w), axis=0))
            return jnp.concatenate(res[0], axis=1), jnp.concatenate(res[1], axis=1)

        dar = jnp.concatenate([da_ref[:, 2 * cb * w:(2 * cb + 1) * w] for cb in range(SCAN_NBLK)], axis=1)
        dai = jnp.concatenate([da_ref[:, (2 * cb + 1) * w:(2 * cb + 2) * w] for cb in range(SCAN_NBLK)], axis=1)
        dbbr, dbbi = diag_parts(dbb_ref)
        dcr, dci_neg = diag_parts(dc_ref)
        gcr_ref[...] = dcr
        gci_ref[...] = -dci_neg
        _, vjp = jax.vjp(_disc, lr_ref[...], li_ref[...], ldt_ref[...], br_ref[...], bi_ref[...])
        glr, gli, gldt, gbr, gbi = vjp((dar, dai, dbbr, dbbi))
        glr_ref[...] = glr
        gli_ref[...] = gli
        gldt_ref[...] = jnp.dot(jnp.broadcast_to(gldt, (8, N_STATE)), ind_ref[...], preferred_element_type=F32,
                                precision=lax.Precision.HIGHEST)
        gbr_ref[...] = gbr
        gbi_ref[...] = gbi

    v1 = jax.ShapeDtypeStruct((1, N_STATE), F32)
    v16 = jax.ShapeDtypeStruct((SSM_CH, N_STATE), F32)
    vdt = jax.ShapeDtypeStruct((8, LANES), F32)
    return _pallas_call(
        body, name="ssm_param_bwd", out_shape=[v1, v1, vdt, v16, v16, v16, v16],
        compiler_params=pltpu.CompilerParams(vmem_limit_bytes=VMEM_BIG),
    )(lr, li, ldt, br, bi, da_cat, dbb_full, dc_full, _group_indicator())


def _cmul(ar, ai, br, bi):
    return ar * br - ai * bi, ar * bi + ai * br


def _gelu_tanh(y):
    return jnp.tanh(_GELU_C * (y + 0.044715 * (y * y * y)))


def _segment_carry(er, ei, ar, ai, n_rows, reverse):
    qr, qi = ar, ai
    for _ in range(int(math.log2(SCAN_LEN))):
        qr, qi = _cmul(qr, qi, qr, qi)
    seg = lax.broadcasted_iota(jnp.int32, er.shape, 0) % SCAN_SEG_PER_SAMPLE
    shift = 1
    while shift < SCAN_SEG_PER_SAMPLE:
        keep = (seg < SCAN_SEG_PER_SAMPLE - shift) if reverse else (seg >= shift)
        amount = n_rows - shift if reverse else shift
        sr = jnp.where(keep, pltpu.roll(er, amount, 0), 0.0)
        si = jnp.where(keep, pltpu.roll(ei, amount, 0), 0.0)
        if reverse:
            er, ei = er + qr * sr + qi * si, ei + qr * si - qi * sr
        else:
            er, ei = er + qr * sr - qi * si, ei + qr * si + qi * sr
        qr, qi = _cmul(qr, qi, qr, qi)
        shift *= 2
    keep = (seg < SCAN_SEG_PER_SAMPLE - 1) if reverse else (seg >= 1)
    amount = n_rows - 1 if reverse else 1
    return jnp.where(keep, pltpu.roll(er, amount, 0), 0.0), jnp.where(keep, pltpu.roll(ei, amount, 0), 0.0)


def _ssm_fwd(u_perm, a_cat, bbc, cc, dskip, n_rows):
    t = u_perm.shape[0]
    w = SCAN_WC
    rows_c = SCAN_CHUNK * n_rows
    n_chunks = t // rows_c

    assert n_chunks % 2 == 0

    def body(u_ref, a_ref, bb_ref, c_ref, d_ref, yt_ref, yg_ref, ein_ref, bu_all, st_a, st_b, xs_a, xs_b):
        ar = jnp.broadcast_to(a_ref[:, :w], (n_rows, w))
        ai = jnp.broadcast_to(a_ref[:, w:], (n_rows, w))
        start = lambda ch: pl.multiple_of(ch * rows_c, rows_c)

        def project(ch, stage):
            res = jnp.dot(u_ref[pl.ds(start(ch), rows_c), :].astype(MXU_DTYPE), bb_ref[...], preferred_element_type=F32)
            stage[...] = res
            bu_all[pl.ds(start(ch), rows_c), :] = res

        def steps(src, r0, carry, xs=None):
            for i in range(SCAN_CHUNK):
                blk = src[pl.ds(r0 + i * n_rows, n_rows), :]
                carry = (ar * carry[0] - ai * carry[1] + blk[:, :w], ar * carry[1] + ai * carry[0] + blk[:, w:])
                if xs is not None:
                    xs[i * n_rows:(i + 1) * n_rows, :w] = carry[0]
                    xs[i * n_rows:(i + 1) * n_rows, w:] = carry[1]
            return carry

        def emit(xs, ch):
            y = lax.dot_general(xs[...].astype(MXU_DTYPE), c_ref[...], _NT, preferred_element_type=F32)
            yt = y + d_ref[...] * u_ref[pl.ds(start(ch), rows_c), :]
            yt_ref[pl.ds(start(ch), rows_c), :] = yt
            yg_ref[pl.ds(start(ch), rows_c), :] = (0.5 * yt * (1.0 + _gelu_tanh(yt))).astype(BF16)

        project(0, st_a)

        def pair1(p, carry):
            project(2 * p + 1, st_b)
            carry = steps(st_a, 0, carry)
            project(jnp.minimum(2 * p + 2, n_chunks - 1), st_a)
            return steps(st_b, 0, carry)

        zero = jnp.zeros((n_rows, w), F32)
        er, ei = lax.fori_loop(0, n_chunks // 2, pair1, (zero, zero))
        cr, ci = _segment_carry(er, ei, ar, ai, n_rows, False)
        ein_ref[:, :w] = cr
        ein_ref[:, w:] = ci

        xs_b[...] = jnp.zeros_like(xs_b)

        def pair2(p, carry):
            emit(xs_b, jnp.maximum(2 * p - 1, 0))
            carry = steps(bu_all, start(2 * p), carry, xs_a)
            emit(xs_a, 2 * p)
            return steps(bu_all, start(2 * p + 1), carry, xs_b)

        lax.fori_loop(0, n_chunks // 2, pair2, (cr, ci))
        emit(xs_b, n_chunks - 1)

    col = lambda width: pl.BlockSpec((t, width), lambda c: (0, c))
    wgt = pl.BlockSpec((SCAN_CH, 2 * w), lambda c: (c, 0))
    return _pallas_call(
        body, name="ssm_fwd", grid=(SCAN_NBLK,),
        in_specs=[col(SCAN_CH), pl.BlockSpec((1, 2 * w), lambda c: (0, c)), wgt, wgt,
                  pl.BlockSpec((1, SCAN_CH), lambda c: (0, c))],
        out_specs=[col(SCAN_CH), col(SCAN_CH), pl.BlockSpec((n_rows, 2 * w), lambda c: (0, c))],
        out_shape=[jax.ShapeDtypeStruct((t, SSM_W), F32), jax.ShapeDtypeStruct((t, SSM_W), BF16),
                   jax.ShapeDtypeStruct((n_rows, 2 * N_STATE), F32)],
        scratch_shapes=[pltpu.VMEM((t, 2 * w), F32)] + [pltpu.VMEM((rows_c, 2 * w), F32)] * 4,
        compiler_params=pltpu.CompilerParams(dimension_semantics=("parallel",), vmem_limit_bytes=VMEM_BIG),
    )(u_perm, a_cat, bbc, cc, dskip)


def _ssm_bwd(u_perm, dyg, ytot, dskip, a_cat, bbc, cc, ein, n_rows, comm=None):
    t = u_perm.shape[0]
    w = SCAN_WC
    rows_c = SCAN_CHUNK * n_rows
    n_chunks = t // rows_c

    assert n_chunks % 2 == 0
    last = n_chunks - 1

    def body(u_ref, dyg_ref, yt_ref, dk_ref, a_ref, bb_ref, c_ref, ein_ref, du_ref, gd_ref, da_ref, dbb_ref, dc_ref,
             xs_all, dy_s, st_a, st_b, buf_a, buf_b):
        ar = jnp.broadcast_to(a_ref[:, :w], (n_rows, w))
        ai = jnp.broadcast_to(a_ref[:, w:], (n_rows, w))
        zero = jnp.zeros((n_rows, w), F32)
        start = lambda ch: pl.multiple_of(ch * rows_c, rows_c)
        dbb_ref[...] = jnp.zeros_like(dbb_ref)
        dc_ref[...] = jnp.zeros_like(dc_ref)
        da_ref[...] = jnp.zeros_like(da_ref)

        yt = yt_ref[...]
        th = _gelu_tanh(yt)
        dgelu = 0.5 * (1.0 + th) + 0.5 * yt * (1.0 - th * th) * _GELU_C * (1.0 + 3.0 * 0.044715 * yt * yt)
        dy_all = dyg_ref[...] * dgelu
        dy_s[...] = dy_all
        gd_ref[...] = jnp.sum(dy_all * u_ref[...], axis=0, keepdims=True)
        dy_chunk = lambda ch: dy_s[pl.ds(start(ch), rows_c), :].astype(MXU_DTYPE)

        xs_all[0:n_rows, :] = ein_ref[...]

        def project(ch, stage):
            stage[...] = jnp.dot(u_ref[pl.ds(start(ch), rows_c), :].astype(MXU_DTYPE), bb_ref[...],
                                 preferred_element_type=F32)

        def fwd_steps(stage, ch, carry, xs):
            for i in range(SCAN_CHUNK):
                blk = stage[i * n_rows:(i + 1) * n_rows, :]
                carry = (ar * carry[0] - ai * carry[1] + blk[:, :w], ar * carry[1] + ai * carry[0] + blk[:, w:])
                for half, val in enumerate(carry):
                    xs[i * n_rows:(i + 1) * n_rows, half * w:(half + 1) * w] = val
                    xs_all[pl.ds(start(ch) + (i + 1) * n_rows, n_rows), half * w:(half + 1) * w] = val
            return carry

        def add_dc(xs, ch):
            dc_ref[...] += lax.dot_general(dy_chunk(ch), xs[...].astype(MXU_DTYPE), _TN, preferred_element_type=F32)

        project(0, st_a)

        def fwd_pair(p, carry):
            project(2 * p + 1, st_b)
            carry = fwd_steps(st_a, 2 * p, carry, buf_a)
            add_dc(buf_a, 2 * p)
            project(jnp.minimum(2 * p + 2, last), st_a)
            carry = fwd_steps(st_b, 2 * p + 1, carry, buf_b)
            add_dc(buf_b, 2 * p + 1)
            return carry

        lax.fori_loop(0, n_chunks // 2, fwd_pair, (ein_ref[:, :w], ein_ref[:, w:]))

        def project_dx(ch, stage):
            stage[...] = jnp.dot(dy_chunk(ch), c_ref[...], preferred_element_type=F32)

        def back_steps(stage, carry, g_buf=None):
            for i in reversed(range(SCAN_CHUNK)):
                blk = stage[i * n_rows:(i + 1) * n_rows, :]
                carry = (blk[:, :w] + ar * carry[0] + ai * carry[1], blk[:, w:] + ar * carry[1] - ai * carry[0])
                if g_buf is not None:
                    g_buf[i * n_rows:(i + 1) * n_rows, :w] = carry[0]
                    g_buf[i * n_rows:(i + 1) * n_rows, w:] = carry[1]
            return carry

        def first_pair(p, carry):
            project_dx(last - 2 * p - 1, st_b)
            carry = back_steps(st_a, carry)
            project_dx(jnp.maximum(last - 2 * p - 2, 0), st_a)
            return back_steps(st_b, carry)

        project_dx(last, st_a)
        sr, si = lax.fori_loop(0, n_chunks // 2, first_pair, (zero, zero))
        gr0, gi0 = _segment_carry(sr, si, ar, ai, n_rows, True)

        def post(g_buf, ch):
            g = g_buf[...]
            xp = xs_all[pl.ds(start(ch), rows_c), :]
            da_ref[:, :w] += jnp.sum(g[:, :w] * xp[:, :w] + g[:, w:] * xp[:, w:], axis=0, keepdims=True)
            da_ref[:, w:] += jnp.sum(g[:, w:] * xp[:, :w] - g[:, :w] * xp[:, w:], axis=0, keepdims=True)
            gb = g.astype(MXU_DTYPE)
            du_ref[pl.ds(start(ch), rows_c), :] = (lax.dot_general(gb, bb_ref[...], _NT, preferred_element_type=F32)
                                                   + dy_s[pl.ds(start(ch), rows_c), :] * dk_ref[...])
            dbb_ref[...] += lax.dot_general(u_ref[pl.ds(start(ch), rows_c), :].astype(MXU_DTYPE), gb, _TN,
                                            preferred_element_type=F32)

        def second_pair(p, carry):
            c1 = last - 2 * p
            project_dx(c1 - 1, st_b)
            post(buf_b, jnp.minimum(c1 + 1, last))
            carry = back_steps(st_a, carry, buf_a)
            project_dx(jnp.maximum(c1 - 2, 0), st_a)
            post(buf_a, c1)
            return back_steps(st_b, carry, buf_b)

        project_dx(last, st_a)
        buf_b[...] = jnp.zeros_like(buf_b)
        lax.fori_loop(0, n_chunks // 2, second_pair, (gr0, gi0))
        post(buf_b, 0)

    col = lambda width: pl.BlockSpec((t, width), lambda c, j: (0, c))
    wgt = pl.BlockSpec((SCAN_CH, 2 * w), lambda c, j: (c, 0))
    row = pl.BlockSpec((1, 2 * w), lambda c, j: (0, c))
    chan = pl.BlockSpec((1, SCAN_CH), lambda c, j: (0, c))
    return _grid_call(
        body, "ssm_bwd", (SCAN_NBLK, 1), [u_perm, dyg, ytot, dskip, a_cat, bbc, cc, ein],
        [col(SCAN_CH), col(SCAN_CH), col(SCAN_CH), chan, row, wgt, wgt,
         pl.BlockSpec((n_rows, 2 * w), lambda c, j: (0, c))],
        [col(SCAN_CH), chan, row, wgt, wgt],
        [jax.ShapeDtypeStruct((t, SSM_W), F32), jax.ShapeDtypeStruct((1, SSM_W), F32),
         jax.ShapeDtypeStruct((1, 2 * N_STATE), F32), jax.ShapeDtypeStruct((SSM_W, 2 * w), F32),
         jax.ShapeDtypeStruct((SSM_W, 2 * w), F32)],
        56 * 1024 * 1024, comm,
        scratch=[pltpu.VMEM((t + n_rows, 2 * w), F32), pltpu.VMEM((t, SCAN_CH), F32)]
        + [pltpu.VMEM((rows_c, 2 * w), F32)] * 4)


def _to_scan_rows(a, n_samples):
    c = a.shape[1]
    return a.reshape(n_samples, SCAN_SEG_PER_SAMPLE, SCAN_LEN, c).transpose(2, 0, 1, 3).reshape(-1, c)


def _from_scan_rows(a, n_samples):
    c = a.shape[1]
    return a.reshape(SCAN_LEN, n_samples, SCAN_SEG_PER_SAMPLE, c).transpose(1, 2, 0, 3).reshape(-1, c)


def _row_spec(tm, width):
    return pl.BlockSpec((tm, width), lambda i, j: (i, 0))


def _whole(arr):
    return pl.BlockSpec(arr.shape, lambda i, j: (0,) * arr.ndim)


def _proj_rope(x, g, w_in_t, tabs, comm=None):
    t = x.shape[0]
    tm = 256

    def body(x_ref, g_ref, w_ref, tc_ref, tlo_ref, thi_ref, h_ref, u_ref, gate_ref, *rest):
        qkv_refs, stage = rest[:9], rest[9]
        xv = x_ref[...]
        r = lax.rsqrt(jnp.mean(xv * xv, axis=-1, keepdims=True) + RMS_EPS)
        h = ((xv * r) * g_ref[...]).astype(BF16)
        h_ref[...] = h
        p = lax.dot_general(h.astype(MXU_DTYPE), w_ref[...], _NT, preferred_element_type=F32)
        u_ref[...] = p[:, QKV_W:QKV_W + SSM_W]
        gate_ref[...] = _sigmoid(p[:, QKV_W + SSM_W:])
        tc, tlo, thi = tc_ref[...], tlo_ref[...], thi_ref[...]
        n_ch = QKV_W // LANES
        for ch in range(n_ch):
            piece = p[:, _lane_chunk(ch)]
            stage[ch] = _rope_apply(piece, tc, tlo, thi) if ch < 2 * n_ch // 3 else piece
        halves = GROUP_W // LANES
        for grp, d in enumerate(DILATIONS):
            for which in range(3):
                out = qkv_refs[3 * grp + which]
                for res in range(d):
                    for half in range(halves):
                        ch = which * (n_ch // 3) + grp * halves + half
                        out[:, _lane_chunk(res * halves + half)] = _gather_residue(stage, ch, res, d, tm // d).astype(BF16)

    tab = pl.BlockSpec((tm, LANES), lambda i, j: (i % (SEQ // tm), 0))
    widths = [(D_MODEL, BF16), (SSM_W, F32), (2 * D_MODEL, F32)]
    out_specs = [_row_spec(tm, wd) for wd, _ in widths]
    out_shapes = [jax.ShapeDtypeStruct((t, wd), dt) for wd, dt in widths]
    for d in DILATIONS:
        out_specs += [_row_spec(tm // d, d * GROUP_W)] * 3
        out_shapes += [jax.ShapeDtypeStruct((t // d, d * GROUP_W), BF16)] * 3
    return _grid_call(
        body, "proj_rope", (t // tm, 1), [x, g, w_in_t, *tabs],
        [_row_spec(tm, D_MODEL), _whole(g), _whole(w_in_t), tab, tab, tab], out_specs, out_shapes, VMEM_BIG, comm,
        scratch=[pltpu.VMEM((QKV_W // LANES, tm, LANES), F32)])


def _branch_outputs(attn_ref, yg_ref, wao_ref, wglu_ref):
    attn_d = lax.dot_general(attn_ref[...].astype(MXU_DTYPE), wao_ref[...], _NT, preferred_element_type=F32)
    z = lax.dot_general(yg_ref[...].astype(MXU_DTYPE), wglu_ref[...], _NT, preferred_element_type=F32)
    return attn_d, z[:, :D_MODEL], _sigmoid(z[:, D_MODEL:])


def _mix_out_rms(os_, lses, yg, gates, x, w_ao_t, w_glu_t, w_out, g, comm=None):
    t = x.shape[0]
    tm = 256

    def body(o0, o1, o2, l0, l1, l2, yg_ref, gate_ref, x_ref, wao_ref, wglu_ref, wout_ref, g_ref,
             attn_ref, lt_ref, m_ref, x1_ref, h_ref, nat):
        _merge_groups((o0, o1, o2), (l0, l1, l2), attn_ref, lt_ref, nat, tm)
        attn_d, za, sb = _branch_outputs(attn_ref, yg_ref, wao_ref, wglu_ref)
        merged = (gate_ref[:, :D_MODEL] * attn_d + gate_ref[:, D_MODEL:] * (za * sb)).astype(BF16)
        m_ref[...] = merged
        x1 = x_ref[...] + jnp.dot(merged.astype(MXU_DTYPE), wout_ref[...], preferred_element_type=F32)
        x1_ref[...] = x1
        r = lax.rsqrt(jnp.mean(x1 * x1, axis=-1, keepdims=True) + RMS_EPS)
        h_ref[...] = ((x1 * r) * g_ref[...]).astype(BF16)

    dil_specs = [_row_spec(tm // d, d * GROUP_W) for d in DILATIONS] * 2
    return _grid_call(
        body, "mix_out_rms", (t // tm, 1), [*os_, *lses, yg, gates, x, w_ao_t, w_glu_t, w_out, g],
        dil_specs + [_row_spec(tm, SSM_W), _row_spec(tm, 2 * D_MODEL), _row_spec(tm, D_MODEL),
                     _whole(w_ao_t), _whole(w_glu_t), _whole(w_out), _whole(g)],
        [_row_spec(tm, GROUP_W)] * 2 + [_row_spec(tm, D_MODEL)] * 3,
        [jax.ShapeDtypeStruct((t, GROUP_W), F32)] * 2
        + [jax.ShapeDtypeStruct((t, D_MODEL), BF16), jax.ShapeDtypeStruct((t, D_MODEL), F32),
           jax.ShapeDtypeStruct((t, D_MODEL), BF16)], VMEM_BIG, comm, scratch=[pltpu.VMEM((8, tm, LANES), F32)])


def _mix_bwd(dx1b, attn, lse_tot, yg, gates, w_ao_t, w_glu_t, w_out, comm=None):
    t = dx1b.shape[0]
    tm = 256

    def body(dx_ref, attn_ref, lt_ref, yg_ref, gate_ref, wao_ref, wglu_ref, wout_ref, ones_ref,
             dad_ref, dz_ref, dg_ref, da_ref, dyg_ref, rd_ref, *rest):
        dm = lax.dot_general(dx_ref[...], wout_ref[...], _NT, preferred_element_type=F32)
        attn_d, za, sb = _branch_outputs(attn_ref, yg_ref, wao_ref, wglu_ref)
        g0, g1 = gate_ref[:, :D_MODEL], gate_ref[:, D_MODEL:]
        dad = (dm * g0).astype(BF16)
        dad_ref[...] = dad
        ds = dm * g1
        dza, dzb = (ds * sb).astype(BF16), (ds * za * sb * (1.0 - sb)).astype(BF16)
        dz_ref[:, :D_MODEL] = dza
        dz_ref[:, D_MODEL:] = dzb
        dg_ref[:, :D_MODEL] = (dm * attn_d * g0 * (1.0 - g0)).astype(BF16)
        dg_ref[:, D_MODEL:] = (dm * (za * sb) * g1 * (1.0 - g1)).astype(BF16)
        da = jnp.dot(dad.astype(MXU_DTYPE), wao_ref[...], preferred_element_type=F32)
        da_ref[...] = da
        dyg_ref[...] = (jnp.dot(dza.astype(MXU_DTYPE), wglu_ref[:D_MODEL, :], preferred_element_type=F32)
                        + jnp.dot(dzb.astype(MXU_DTYPE), wglu_ref[D_MODEL:, :], preferred_element_type=F32))
        _attention_cotangents(da, attn_ref[...], lt_ref[...], ones_ref[...], rd_ref, rest[:6], rest[6], tm)

    widths = [(D_MODEL, BF16), (2 * D_MODEL, BF16), (2 * D_MODEL, BF16), (GROUP_W, F32), (SSM_W, F32), (GROUP_W, F32)]
    out_specs = [_row_spec(tm, wd) for wd, _ in widths]
    out_shapes = [jax.ShapeDtypeStruct((t, wd), dt) for wd, dt in widths]
    for d in DILATIONS[1:]:
        out_specs += [_row_spec(tm // d, d * GROUP_W)] * 3
        out_shapes += [jax.ShapeDtypeStruct((t // d, d * GROUP_W), F32)] * 3
    ones = _head_sum_matrix()
    return _grid_call(
        body, "mix_bwd", (t // tm, 1), [dx1b, attn, lse_tot, yg, gates, w_ao_t, w_glu_t, w_out, ones],
        [_row_spec(tm, D_MODEL), _row_spec(tm, GROUP_W), _row_spec(tm, GROUP_W), _row_spec(tm, SSM_W),
         _row_spec(tm, 2 * D_MODEL), _whole(w_ao_t), _whole(w_glu_t), _whole(w_out), _whole(ones)],
        out_specs, out_shapes, VMEM_BIG, comm, scratch=[pltpu.VMEM((6, tm, LANES), F32)])


FFN_TN = D_FF // 2
MXU_COLS = 256


def _ffn_in_swiglu(h2, w_gate_t, w_up_t, comm=None):
    t = h2.shape[0]
    tm = 512

    def body(h_ref, wg_ref, wu_ref, a_ref, b_ref, f_ref):
        h = h_ref[...].astype(MXU_DTYPE)
        for c0 in range(0, FFN_TN, MXU_COLS):
            sl = slice(c0, min(c0 + MXU_COLS, FFN_TN))
            a = lax.dot_general(h, wg_ref[sl, :], _NT, preferred_element_type=F32)
            b = lax.dot_general(h, wu_ref[sl, :], _NT, preferred_element_type=F32)
            a_ref[:, sl] = a
            b_ref[:, sl] = b
            f_ref[:, sl] = (a * _sigmoid(a) * b).astype(BF16)

    tile = pl.BlockSpec((tm, FFN_TN), lambda j, i: (i, j))
    wspec = pl.BlockSpec((FFN_TN, D_MODEL), lambda j, i: (j, 0))
    return _grid_call(
        body, "ffn_in_swiglu", (D_FF // FFN_TN, t // tm), [h2, w_gate_t, w_up_t],
        [pl.BlockSpec((tm, D_MODEL), lambda j, i: (i, 0)), wspec, wspec],
        [tile] * 3, [jax.ShapeDtypeStruct((t, D_FF), F32)] * 2 + [jax.ShapeDtypeStruct((t, D_FF), BF16)], VMEM_BIG, comm)


def _ffn_down_final(f, w_down, x1, target, g):
    t = x1.shape[0]
    tm = 256

    def body(f_ref, w_ref, x1_ref, t_ref, g_ref, dx_ref, dxb_ref, loss_ref, gg_ref):
        @pl.when(pl.program_id(0) == 0)
        def _():
            loss_ref[...] = jnp.zeros_like(loss_ref)
            gg_ref[...] = jnp.zeros_like(gg_ref)

        xv = x1_ref[...] + jnp.dot(f_ref[...].astype(MXU_DTYPE), w_ref[...], preferred_element_type=F32)
        gv = g_ref[...]
        r = lax.rsqrt(jnp.mean(xv * xv, axis=-1, keepdims=True) + RMS_EPS)
        n = xv * r
        diff = n * gv - t_ref[...]
        per_tok = jnp.mean(diff * diff, axis=-1, keepdims=True)
        loss_ref[...] += 0.5 * jnp.sum(per_tok, axis=0, keepdims=True)
        dy = diff / xv.shape[-1]
        gg_ref[...] += jnp.sum(dy * n, axis=0, keepdims=True)
        dn = dy * gv
        dx = r * (dn - n * jnp.mean(dn * n, axis=-1, keepdims=True))
        dx_ref[...] = dx
        dxb_ref[...] = dx.astype(BF16)

    acc = lambda shp: pl.BlockSpec(shp, lambda i, j: (0, 0))
    return _grid_call(
        body, "ffn_down_final", (t // tm, 1), [f, w_down, x1, target, g],
        [_row_spec(tm, D_FF), _whole(w_down), _row_spec(tm, D_MODEL), _row_spec(tm, D_MODEL), _whole(g)],
        [_row_spec(tm, D_MODEL)] * 2 + [acc((8, LANES)), acc((1, D_MODEL))],
        [jax.ShapeDtypeStruct((t, D_MODEL), F32), jax.ShapeDtypeStruct((t, D_MODEL), BF16),
         jax.ShapeDtypeStruct((8, LANES), F32), jax.ShapeDtypeStruct((1, D_MODEL), F32)], VMEM_BIG, sequential=True)


def _d_f_swiglu_bwd(dx2b, w_down, a, b):
    t = a.shape[0]
    tm = 512

    def body(dx_ref, w_ref, a_ref, b_ref, da_ref, db_ref):
        d = lax.dot_general(dx_ref[...], w_ref[...], _NT, preferred_element_type=F32)
        av, bv = a_ref[...], b_ref[...]
        sg = _sigmoid(av)
        da_ref[...] = (d * bv * sg * (1.0 + av * (1.0 - sg))).astype(BF16)
        db_ref[...] = (d * av * sg).astype(BF16)

    tile = pl.BlockSpec((tm, FFN_TN), lambda j, i: (i, j))
    return _grid_call(
        body, "d_f_swiglu_bwd", (D_FF // FFN_TN, t // tm), [dx2b, w_down, a, b],
        [pl.BlockSpec((tm, D_MODEL), lambda j, i: (i, 0)), pl.BlockSpec((FFN_TN, D_MODEL), lambda j, i: (j, 0)), tile, tile],
        [tile] * 2, [jax.ShapeDtypeStruct((t, D_FF), BF16)] * 2, VMEM_BIG)


def _ffn_weight_grads(f, dx2b, da, db, h2):
    t = h2.shape[0]
    tm = 256
    half = D_MODEL // 2

    def body(f_ref, dx_ref, da_ref, db_ref, h_ref, dn_ref, g0_ref, g1_ref, u0_ref, u1_ref):
        dn_ref[...] = lax.dot_general(f_ref[...].astype(MXU_DTYPE), dx_ref[...].astype(MXU_DTYPE), _TN,
                                      preferred_element_type=F32).astype(BF16)
        h = h_ref[...].astype(MXU_DTYPE)
        for src, (lo_ref, hi_ref) in ((da_ref, (g0_ref, g1_ref)), (db_ref, (u0_ref, u1_ref))):
            prod = lax.dot_general(src[...].astype(MXU_DTYPE), h, _TN, preferred_element_type=F32)
            lo_ref[...] = prod[:, :half].astype(BF16)
            hi_ref[...] = prod[:, half:].astype(BF16)

    col = pl.BlockSpec((t, tm), lambda i, j: (0, i))
    out = pl.BlockSpec((tm, half), lambda i, j: (i, 0))
    return _grid_call(body, "mm_g_ffn", (D_FF // tm, 1), [f, dx2b, da, db, h2],
                      [col, _whole(dx2b), col, col, _whole(h2)], [_row_spec(tm, D_MODEL)] + [out] * 4,
                      [jax.ShapeDtypeStruct((D_FF, D_MODEL), BF16)] + [jax.ShapeDtypeStruct((D_FF, half), BF16)] * 4,
                      56 * 1024 * 1024)


def _branch_weight_grads(dz, yg, dattn_d, attn):
    t = yg.shape[0]
    steps = 4
    tz, ta = dz.shape[1] // steps, dattn_d.shape[1] // steps

    def body(dz_ref, yg_ref, dad_ref, attn_ref, gz_ref, ga_ref):
        gz_ref[...] = lax.dot_general(dz_ref[...].astype(MXU_DTYPE), yg_ref[...].astype(MXU_DTYPE), _TN,
                                      preferred_element_type=F32).astype(BF16)
        ga_ref[...] = lax.dot_general(dad_ref[...].astype(MXU_DTYPE), attn_ref[...].astype(MXU_DTYPE), _TN,
                                      preferred_element_type=F32).astype(BF16)

    col = lambda wd: pl.BlockSpec((t, wd), lambda i, j: (0, i))
    return _grid_call(body, "mm_g_branches", (steps, 1), [dz, yg, dattn_d, attn],
                      [col(tz), _whole(yg), col(ta), _whole(attn)], [_row_spec(tz, SSM_W), _row_spec(ta, GROUP_W)],
                      [jax.ShapeDtypeStruct((dz.shape[1], SSM_W), BF16), jax.ShapeDtypeStruct((dattn_d.shape[1], GROUP_W), BF16)],
                      VMEM_BIG)


def _mm_rms_bwd(operands, weights, x, g, dres, name, comm=None):
    t = x.shape[0]
    tm = 256
    n_op = len(operands)

    def body(*refs):
        a_refs, w_refs = refs[:n_op], refs[n_op:2 * n_op]
        x_ref, g_ref, dres_ref, dx_ref, dxb_ref, gg_ref = refs[2 * n_op:]

        @pl.when(pl.program_id(0) == 0)
        def _():
            gg_ref[...] = jnp.zeros_like(gg_ref)

        dh = None
        for a_ref, w_ref in zip(a_refs, w_refs):
            part = jnp.dot(a_ref[...].astype(MXU_DTYPE), w_ref[...], preferred_element_type=F32)
            dh = part if dh is None else dh + part
        xv = x_ref[...]
        r = lax.rsqrt(jnp.mean(xv * xv, axis=-1, keepdims=True) + RMS_EPS)
        n = xv * r
        gg_ref[...] += jnp.sum(dh * n, axis=0, keepdims=True)
        dn = dh * g_ref[...]
        dx = dres_ref[...] + r * (dn - n * jnp.mean(dn * n, axis=-1, keepdims=True))
        dx_ref[...] = dx
        dxb_ref[...] = dx.astype(BF16)

    d = x.shape[1]
    return _grid_call(
        body, name, (t // tm, 1), [*operands, *weights, x, g, dres],
        [_row_spec(tm, a.shape[1]) for a in operands] + [_whole(wk) for wk in weights]
        + [_row_spec(tm, d), _whole(g), _row_spec(tm, d)],
        [_row_spec(tm, d)] * 2 + [pl.BlockSpec((1, d), lambda i, j: (0, 0))],
        [jax.ShapeDtypeStruct((t, d), F32), jax.ShapeDtypeStruct((t, d), BF16), jax.ShapeDtypeStruct((1, d), F32)],
        VMEM_BIG, comm, sequential=True)


def _flat_small(small):
    perm_b = lambda a: a.reshape(SSM_GROUPS, SSM_STATE, SSM_CH).transpose(2, 0, 1).reshape(SSM_CH, N_STATE)
    perm_c = lambda a: a.reshape(SSM_GROUPS, SSM_CH, SSM_STATE).transpose(1, 0, 2).reshape(SSM_CH, N_STATE)
    return dict(
        g_mix=small["norm_mix_g"].reshape(1, D_MODEL), g_ffn=small["norm_ffn_g"].reshape(1, D_MODEL),
        g_fin=small["norm_final_g"].reshape(1, D_MODEL),
        lr=small["ssm_a_re"].reshape(1, N_STATE), li=small["ssm_a_im"].reshape(1, N_STATE),
        ldt=jnp.repeat(small["ssm_log_dt"].reshape(SSM_GROUPS), SSM_STATE).reshape(1, N_STATE),
        br=perm_b(small["ssm_b_re"]), bi=perm_b(small["ssm_b_im"]),
        cr=perm_c(small["ssm_c_re"]), ci=perm_c(small["ssm_c_im"]), dskip=small["ssm_d"].reshape(1, SSM_W))


AG_HOSTS = {"proj_rope": ("w_glu", "w_attn_out", "w_out", "w_ffn_gate"), "mix_out_rms": ("w_ffn_up",),
            "ffn_in_swiglu": ("w_ffn_down",)}
HALVED = ("w_ffn_gate", "w_ffn_up", "w_in")
FFN_ADAM = ("w_ffn_gate", "w_ffn_up", "w_ffn_down")
A2A_HOSTS = {"d_h2_rms": ("w_ffn_down",), "mix_bwd": ("w_ffn_gate:0", "w_out"), "attn_bwd_g1": ("w_glu",),
             "attn_bwd_g2": ("w_attn_out",), "ssm_bwd": ("w_ffn_gate:1", "w_ffn_up:0", "w_ffn_up:1"),
             "mm_g_in1": ("w_in:0",), "d_h0_rms": ("w_in:1",)}
SMALL_HOST = "mm_g_in0"


def _local_step(x, target, w, small, shards=None):
    t = x.shape[0]
    n_samples = t // SEQ
    n_rows = n_samples * SCAN_SEG_PER_SAMPLE
    tabs = _rope_tables()
    w = dict(w)
    fs = _flat_small(small)
    g_mix, g_ffn, g_fin, dskip = fs["g_mix"], fs["g_ffn"], fs["g_fin"], fs["dskip"]
    a_cat, bbc, cc = _ssm_disc(fs["lr"], fs["li"], fs["ldt"], fs["br"], fs["bi"], fs["cr"], fs["ci"])
    big, recv, small_pack = {}, {}, []

    def comm_of(name):
        if shards is None:
            return None
        if name == SMALL_HOST:
            return _ag_comm([(small_pack[0], 0, 0)], [(N_DEV, *small_pack[0].shape)])
        if name in AG_HOSTS:
            names = AG_HOSTS[name]
            return _ag_comm([(shards[n], j, 0) for j, n in enumerate(names)], [(N_DEV, *shards[n].shape) for n in names])
        if name in A2A_HOSTS:
            return _a2a_comm([(big[n].reshape(N_DEV, -1, big[n].shape[1]), 0) for n in A2A_HOSTS[name]])
        return None

    def absorb(name, carried):
        if name == SMALL_HOST:
            recv["small"] = carried[0]
        for n, a3 in zip(AG_HOSTS.get(name, ()), carried):
            w[n] = a3.reshape(-1, a3.shape[2])
        for n, a3 in zip(A2A_HOSTS.get(name, ()), carried):
            recv[n] = a3

    def mm(a, b, mode, name, tm, tn, **kw):
        comm = comm_of(name)
        if comm is None:
            return _mm(a, b, mode, name, tm, tn, **kw)
        out, *carried = _mm(a, b, mode, name, tm, tn, comm=comm, **kw)
        absorb(name, carried)
        return out

    h0, u, gates, *rest = _proj_rope(x, g_mix, w["w_in"], tabs, comm_of("proj_rope"))
    qkv = [rest[3 * g:3 * g + 3] for g in range(3)]
    absorb("proj_rope", rest[9:])
    os_, lses = [], []
    for g in range(3):
        o_g, l_g, carried = _attn_fwd(*qkv[g], g, n_samples, comm_of(f"attn_fwd_g{g}"))
        absorb(f"attn_fwd_g{g}", carried)
        os_.append(o_g)
        lses.append(l_g)
    u_perm = _to_scan_rows(u, n_samples)
    ytot, yg_perm, ein = _ssm_fwd(u_perm, a_cat, bbc, cc, dskip, n_rows)
    yg = _from_scan_rows(yg_perm, n_samples)

    attn, lse_tot, merged, x1, h2, *carried = _mix_out_rms(os_, lses, yg, gates, x, w["w_attn_out"], w["w_glu"], w["w_out"],
                                                           g_ffn, comm_of("mix_out_rms"))
    absorb("mix_out_rms", carried)
    ffn_a, ffn_b, f, *carried = _ffn_in_swiglu(h2, w["w_ffn_gate"], w["w_ffn_up"], comm_of("ffn_in_swiglu"))
    absorb("ffn_in_swiglu", carried)
    dx2, dx2b, loss_blk, g_gfin = _ffn_down_final(f, w["w_ffn_down"], x1, target, g_fin)

    da, db = _d_f_swiglu_bwd(dx2b, w["w_ffn_down"], ffn_a, ffn_b)
    half = D_MODEL // 2
    (big["w_ffn_down"], big["w_ffn_gate:0"], big["w_ffn_gate:1"], big["w_ffn_up:0"],
     big["w_ffn_up:1"]) = _ffn_weight_grads(f, dx2b, da, db, h2)
    dx1, dx1b, g_gffn, *carried = _mm_rms_bwd([da, db], [w["w_ffn_gate"], w["w_ffn_up"]], x1, g_ffn, dx2, "d_h2_rms",
                                              comm_of("d_h2_rms"))
    absorb("d_h2_rms", carried)

    big["w_out"] = mm(merged, dx1b, "tn", "mm_g_out", 256, D_MODEL, out_dtype=BF16)
    dattn_d, dz, dgpre, dattn, dyg, rowdot, *rest = _mix_bwd(dx1b, attn, lse_tot, yg, gates, w["w_attn_out"], w["w_glu"],
                                                             w["w_out"], comm_of("mix_bwd"))
    cot = [(dattn, lse_tot, rowdot), tuple(rest[:3]), tuple(rest[3:6])]
    absorb("mix_bwd", rest[6:])

    big["w_glu"], big["w_attn_out"] = _branch_weight_grads(dz, yg, dattn_d, attn)
    dqs, dks, dvs = [], [], []
    for g in range(3):
        dq_g, dk_g, dv_g, carried = _attn_bwd(*qkv[g], *cot[g], g, n_samples, comm_of(f"attn_bwd_g{g}"))
        absorb(f"attn_bwd_g{g}", carried)
        dqs.append(dq_g)
        dks.append(dk_g)
        dvs.append(dv_g)

    dyg_perm = _to_scan_rows(dyg, n_samples)
    du_perm, g_dskip, da_cat, dbb_full, dc_full, *carried = _ssm_bwd(u_perm, dyg_perm, ytot, dskip, a_cat, bbc, cc, ein,
                                                                   n_rows, comm_of("ssm_bwd"))
    absorb("ssm_bwd", carried)
    du = _from_scan_rows(du_perm, n_samples)
    g_lr, g_li, g_ldt, g_br, g_bi, g_cr, g_ci = _ssm_param_bwd(
        fs["lr"], fs["li"], fs["ldt"], fs["br"], fs["bi"], da_cat, dbb_full, dc_full)

    small_pack.append(_pack_small(dict(lr=g_lr, li=g_li, ldt=g_ldt, br=g_br, bi=g_bi, cr=g_cr, ci=g_ci, dskip=g_dskip,
                                       g_ffn=g_gffn, g_fin=g_gfin, loss=loss_blk)))

    dproj = _pack_dproj(dqs, dks, dvs, du, dgpre, tabs)
    for hf in range(2):
        big[f"w_in:{hf}"] = mm(dproj, h0, "tn", f"mm_g_in{hf}", 256, half, out_dtype=BF16, cols=(hf * half, half))
    grad_x, _, g_gmix, *carried = _mm_rms_bwd([dproj], [w["w_in"]], x, g_mix, dx1, "d_h0_rms", comm_of("d_h0_rms"))
    absorb("d_h0_rms", carried)
    return grad_x, (big if shards is None else recv), small_pack[0], g_gmix


_MESH = pl.DeviceIdType.MESH


def _all_gather(block, name):
    rows, lanes = block.shape

    def body(x_ref, out_ref, send_sems, recv_sems, local_sem):
        x, y, c = lax.axis_index("x"), lax.axis_index("y"), lax.axis_index("c")
        me, sibling = (x, y, c), (x, y, 1 - c)
        chips = [(1 - x, y), (x, 1 - y), (1 - x, 1 - y)]

        def slot(px, py, pc):
            return out_ref.at[4 * px + 2 * py + pc]

        def copy(k, blk, to, src=None):
            return pltpu.make_async_remote_copy(
                src_ref=slot(*blk) if src is None else src, dst_ref=slot(*blk), send_sem=send_sems.at[k],
                recv_sem=recv_sems.at[k], device_id=to, device_id_type=_MESH)

        mine = pltpu.make_async_copy(x_ref, slot(*me), local_sem)
        mine.start()
        first = [copy(0, me, sibling, src=x_ref)]
        first += [copy(1 + j, me, (*chip, c), src=x_ref) for j, chip in enumerate(chips)]
        for cp in first:
            cp.start()
        passed = [copy(4 + j, (*chip, c), sibling) for j, chip in enumerate(chips)]
        for j, chip in enumerate(chips):
            copy(1 + j, (*chip, c), me).wait_recv()
            passed[j].start()
        copy(0, sibling, me).wait_recv()
        for j, chip in enumerate(chips):
            copy(4 + j, (*chip, 1 - c), me).wait_recv()
        for cp in first + passed:
            cp.wait_send()
        mine.wait()

    return _pallas_call(
        body, name=name, out_shape=jax.ShapeDtypeStruct((N_DEV, rows, lanes), block.dtype),
        in_specs=[pl.BlockSpec(memory_space=pl.ANY)], out_specs=pl.BlockSpec(memory_space=pl.ANY),
        scratch_shapes=[pltpu.SemaphoreType.DMA((7,)), pltpu.SemaphoreType.DMA((7,)), pltpu.SemaphoreType.DMA],
    )(block)


def _ag_comm(items, bufs):
    def plan(in_refs, out_refs, send_sems, recv_sems, local_sems):
        x, y, c = lax.axis_index("x"), lax.axis_index("y"), lax.axis_index("c")
        me, sibling = (x, y, c), (x, y, 1 - c)
        chips = [(1 - x, y), (x, 1 - y), (1 - x, 1 - y)]
        plans = []
        for t, (_, buf, slot0) in enumerate(items):
            x_ref, out_ref = in_refs[t], out_refs[buf]

            def slot(px, py, pc, out_ref=out_ref, slot0=slot0):
                return out_ref.at[slot0 + 4 * px + 2 * py + pc]

            def copy(k, blk, to, src=None, t=t, slot=slot):
                return pltpu.make_async_remote_copy(
                    src_ref=slot(*blk) if src is None else src, dst_ref=slot(*blk), send_sem=send_sems.at[7 * t + k],
                    recv_sem=recv_sems.at[7 * t + k], device_id=to, device_id_type=_MESH)

            plans.append(dict(
                mine=pltpu.make_async_copy(x_ref, slot(*me), local_sems.at[t]),
                first=[copy(0, me, sibling, src=x_ref)] + [copy(1 + j, me, (*chip, c), src=x_ref)
                                                           for j, chip in enumerate(chips)],
                passed=[copy(4 + j, (*chip, c), sibling) for j, chip in enumerate(chips)],
                from_ici=[copy(1 + j, (*chip, c), me) for j, chip in enumerate(chips)],
                from_sibling=[copy(0, sibling, me)] + [copy(4 + j, (*chip, 1 - c), me) for j, chip in enumerate(chips)]))
        return plans

    def start(*refs):
        for p in plan(*refs):
            p["mine"].start()
            for cp in p["first"]:
                cp.start()

    def finish(*refs):
        plans = plan(*refs)
        for p in plans:
            for arrived, onward in zip(p["from_ici"], p["passed"]):
                arrived.wait_recv()
                onward.start()
        for p in plans:
            for arrived in p["from_sibling"]:
                arrived.wait_recv()
            for cp in p["first"] + p["passed"]:
                cp.wait_send()
            p["mine"].wait()

    dtype_of = {buf: shard.dtype for shard, buf, _ in items}
    out_shapes = [jax.ShapeDtypeStruct(b, dtype_of[j]) for j, b in enumerate(bufs)]
    return _Comm([it[0] for it in items], out_shapes, 7 * len(items), len(items), start, finish)


def _a2a_comm(items):
    def plan(in_refs, out_refs, send_sems, recv_sems, local_sems):
        x, y, c = lax.axis_index("x"), lax.axis_index("y"), lax.axis_index("c")
        my = 4 * x + 2 * y + c
        copies, locals_ = [], []
        for t, (_, slot0) in enumerate(items):
            s_ref, r_ref = in_refs[t], out_refs[t]
            locals_.append(pltpu.make_async_copy(s_ref.at[slot0 + my], r_ref.at[my], local_sems.at[t]))
            for kk in range(1, N_DEV):
                px = 1 - x if kk & 4 else x
                py = 1 - y if kk & 2 else y
                pc = 1 - c if kk & 1 else c
                copies.append(pltpu.make_async_remote_copy(
                    src_ref=s_ref.at[slot0 + 4 * px + 2 * py + pc], dst_ref=r_ref.at[my],
                    send_sem=send_sems.at[7 * t + kk - 1], recv_sem=recv_sems.at[7 * t + kk - 1],
                    device_id=(px, py, pc), device_id_type=_MESH))
        return copies, locals_

    def start(*refs):
        copies, locals_ = plan(*refs)
        for cp in locals_ + copies:
            cp.start()

    def finish(*refs):
        copies, locals_ = plan(*refs)
        for cp in copies + locals_:
            cp.wait()

    out_shapes = [jax.ShapeDtypeStruct((N_DEV,) + it[0].shape[1:], it[0].dtype) for it in items]
    return _Comm([it[0] for it in items], out_shapes, 7 * len(items), len(items), start, finish)


def _adam_math(g, w, m, v):
    m_new = ADAM_B1 * m + (1.0 - ADAM_B1) * g
    v_new = ADAM_B2 * v + (1.0 - ADAM_B2) * jnp.square(g)
    m_hat = m_new / (1.0 - ADAM_B1 ** ADAM_STEP)
    v_hat = v_new / (1.0 - ADAM_B2 ** ADAM_STEP)
    return -ADAM_LR * (m_hat / (jnp.sqrt(v_hat) + ADAM_EPS) + ADAM_WD * w), m_new, v_new


def _sum_partials(parts, name, tm):
    n, rows, _ = parts[0].shape
    widths = [p.shape[2] for p in parts]

    def body(*refs):
        g_ref, off = refs[-1], 0
        for p_ref, wd in zip(refs[:-1], widths):
            g = p_ref[0].astype(F32)
            for s in range(1, n):
                g = g + p_ref[s].astype(F32)
            g_ref[:, off:off + wd] = g
            off += wd

    return _pallas_call(
        body, name=name, grid=(rows // tm,), in_specs=[pl.BlockSpec((n, tm, wd), lambda i: (0, i, 0)) for wd in widths],
        out_specs=pl.BlockSpec((tm, sum(widths)), lambda i: (i, 0)),
        out_shape=jax.ShapeDtypeStruct((rows, sum(widths)), F32),
        compiler_params=pltpu.CompilerParams(dimension_semantics=("parallel",), vmem_limit_bytes=VMEM_MID),
    )(*parts)


def _adam(parts, w, m, v, name, tm):
    return _adam_multi([(parts, w, m, v)], name, tm)[0]


def _adam_multi(items, name, tm):
    n, rows, _ = items[0][0][0].shape
    widths = [[p.shape[2] for p in parts] for parts, _, _, _ in items]
    n_in = [len(wd) + 3 for wd in widths]

    def body(*refs):
        ins, outs = refs[:sum(n_in)], refs[sum(n_in):]
        at = 0
        for k, wds in enumerate(widths):
            p_refs = ins[at:at + len(wds)]
            w_ref, m_ref, v_ref = ins[at + len(wds):at + n_in[k]]
            g_ref, d_ref, nm_ref, nv_ref = outs[4 * k:4 * k + 4]
            at += n_in[k]
            off = 0
            for p_ref, wd in zip(p_refs, wds):
                g = p_ref[0].astype(F32)
                for s in range(1, n):
                    g = g + p_ref[s].astype(F32)
                sl = slice(off, off + wd)
                g_ref[:, sl] = g
                d_ref[:, sl], nm_ref[:, sl], nv_ref[:, sl] = _adam_math(g, w_ref[:, sl], m_ref[:, sl], v_ref[:, sl])
                off += wd

    assert rows % tm == 0
    in_specs, out_specs, out_shape, operands = [], [], [], []
    for (parts, w, m, v), wds in zip(items, widths):
        assert parts[0].shape[:2] == (n, rows) and w.shape == (rows, sum(wds))
        row = pl.BlockSpec((tm, sum(wds)), lambda i: (i, 0))
        in_specs += [pl.BlockSpec((n, tm, wd), lambda i: (0, i, 0)) for wd in wds] + [row, row, row]
        out_specs += [row] * 4
        out_shape += [jax.ShapeDtypeStruct((rows, sum(wds)), F32)] * 4
        operands += [*parts, w, m, v]
    res = _pallas_call(
        body, name=name, grid=(rows // tm,),
        in_specs=in_specs, out_specs=out_specs, out_shape=out_shape,
        compiler_params=pltpu.CompilerParams(dimension_semantics=("parallel",), vmem_limit_bytes=VMEM_MID),
    )(*operands)
    return [list(res[4 * k:4 * k + 4]) for k in range(len(items))]


_PK_LR, _PK_LI, _PK_GAINS, _PK_MISC, _PK_BR, _PK_BI, _PK_CR, _PK_CI, _PK_ROWS = 0, 1, 2, 3, 8, 24, 40, 56, 72
_PK_LDT_LANE, _PK_LOSS_LANE = D_MODEL + SSM_W, D_MODEL + SSM_W + LANES


def _pack_small(sg):
    names = ("lr", "li", "g_ffn", "g_fin", "dskip", "ldt", "loss", "br", "bi", "cr", "ci")

    def body(lr, li, gffn, gfin, dskip, ldt, loss, br, bi, cr, ci, o_ref):
        o_ref[...] = jnp.zeros_like(o_ref)
        o_ref[_PK_LR:_PK_LR + 1, :] = lr[...]
        o_ref[_PK_LI:_PK_LI + 1, :] = li[...]
        o_ref[_PK_GAINS:_PK_GAINS + 1, D_MODEL:] = gffn[...]
        o_ref[_PK_MISC:_PK_MISC + 1, :D_MODEL] = gfin[...]
        o_ref[_PK_MISC:_PK_MISC + 1, D_MODEL:D_MODEL + SSM_W] = dskip[...]
        o_ref[_PK_MISC:_PK_MISC + 1, _PK_LDT_LANE:_PK_LDT_LANE + LANES] = ldt[0:1, :]
        o_ref[_PK_MISC:_PK_MISC + 1, _PK_LOSS_LANE:_PK_LOSS_LANE + LANES] = loss[0:1, :]
        o_ref[_PK_BR:_PK_BR + SSM_CH, :] = br[...]
        o_ref[_PK_BI:_PK_BI + SSM_CH, :] = bi[...]
        o_ref[_PK_CR:_PK_CR + SSM_CH, :] = cr[...]
        o_ref[_PK_CI:_PK_CI + SSM_CH, :] = ci[...]

    return _pallas_call(body, name="pack_small", out_shape=jax.ShapeDtypeStruct((_PK_ROWS, N_STATE), F32))(
        *[sg[n] for n in names])


def _unpack_small(s, g_mix):
    unflat_b = unflat_c = lambda a: a.reshape(SSM_CH, SSM_GROUPS, SSM_STATE).transpose(1, 0, 2)[None]
    grads = {
        "norm_mix_g": g_mix, "norm_ffn_g": s[_PK_GAINS, D_MODEL:].reshape(1, D_MODEL),
        "norm_final_g": s[_PK_MISC, :D_MODEL].reshape(1, D_MODEL),
        "ssm_a_re": s[_PK_LR].reshape(1, SSM_GROUPS, SSM_STATE), "ssm_a_im": s[_PK_LI].reshape(1, SSM_GROUPS, SSM_STATE),
        "ssm_log_dt": s[_PK_MISC, _PK_LDT_LANE:_PK_LDT_LANE + SSM_GROUPS].reshape(1, SSM_GROUPS),
        "ssm_d": s[_PK_MISC, D_MODEL:D_MODEL + SSM_W].reshape(1, SSM_GROUPS, SSM_CH),
        "ssm_b_re": unflat_b(s[_PK_BR:_PK_BR + SSM_CH]), "ssm_b_im": unflat_b(s[_PK_BI:_PK_BI + SSM_CH]),
        "ssm_c_re": unflat_c(s[_PK_CR:_PK_CR + SSM_CH]), "ssm_c_im": unflat_c(s[_PK_CI:_PK_CI + SSM_CH]),
    }
    return s[_PK_MISC, _PK_LOSS_LANE], grads


def _stored(name, a):
    if name in ("ssm_b_re", "ssm_b_im"):
        return a.transpose(0, 1, 3, 2)
    return a.reshape(1, -1) if a.ndim == 1 else a


def _unstored(name, a, like):
    return a.transpose(0, 1, 3, 2) if name in ("ssm_b_re", "ssm_b_im") else a.reshape(like.shape)


def _adam_small(grads, wts, moms, vars_):
    n = len(SMALL_WEIGHTS)

    def body(*refs):
        ins, outs = refs[:4 * n], refs[4 * n:]
        for i in range(n):
            g, w, m, v = (ins[j * n + i][...] for j in range(4))
            outs[i][...], outs[n + i][...], outs[2 * n + i][...] = _adam_math(g, w, m, v)

    operands = [grads[k] if d is grads else _stored(k, d[k]) for d in (grads, wts, moms, vars_) for k in SMALL_WEIGHTS]
    shapes = [jax.ShapeDtypeStruct(_stored(k, wts[k]).shape, F32) for k in SMALL_WEIGHTS] * 3
    res = _pallas_call(body, name="adam_small", out_shape=shapes,
                         compiler_params=pltpu.CompilerParams(vmem_limit_bytes=VMEM_BIG))(*operands)
    out = {}
    for j, kind in enumerate(("delta", "new_m", "new_v")):
        for i, k in enumerate(SMALL_WEIGHTS):
            out[kind, k] = _unstored(k, res[j * n + i], wts[k])
    return out


def kernel(x, norm_mix_g, w_in, ssm_a_re, ssm_a_im, ssm_log_dt, ssm_b_re, ssm_b_im, ssm_c_re, ssm_c_im, ssm_d, w_glu, w_attn_out, w_out, norm_ffn_g, w_ffn_gate, w_ffn_up, w_ffn_down, norm_final_g, loss_target, m_norm_mix_g, m_w_in, m_ssm_a_re, m_ssm_a_im, m_ssm_log_dt, m_ssm_b_re, m_ssm_b_im, m_ssm_c_re, m_ssm_c_im, m_ssm_d, m_w_glu, m_w_attn_out, m_w_out, m_norm_ffn_g, m_w_ffn_gate, m_w_ffn_up, m_w_ffn_down, m_norm_final_g, v_norm_mix_g, v_w_in, v_ssm_a_re, v_ssm_a_im, v_ssm_log_dt, v_ssm_b_re, v_ssm_b_im, v_ssm_c_re, v_ssm_c_im, v_ssm_d, v_w_glu, v_w_attn_out, v_w_out, v_norm_ffn_g, v_w_ffn_gate, v_w_ffn_up, v_w_ffn_down, v_norm_final_g):
    args = dict(locals())
    wts = {n: args[n] for n in ALL_WEIGHTS}
    moms = {n: args["m_" + n] for n in ALL_WEIGHTS}
    vars_ = {n: args["v_" + n] for n in ALL_WEIGHTS}
    n_samples = x.shape[0]
    t = n_samples * SEQ

    shards = {n: (wts[n][0] if n in ROW_SHARDED else wts[n][0].T).astype(BF16) for n in BIG_WEIGHTS}
    w_in_t = _all_gather(shards["w_in"], "allgather_w_in").reshape(IN_W, D_MODEL)

    small = {n: wts[n] for n in SMALL_WEIGHTS}
    grad_x, recv, _, g_mix_part = _local_step(x.reshape(t, D_MODEL), loss_target.reshape(t, D_MODEL), {"w_in": w_in_t},
                                              small, shards)

    results = {}
    ffn_items = []
    for n in FFN_ADAM:
        w2, m2, v2 = wts[n][0], moms[n][0], vars_[n][0]
        if n in HALVED:
            ffn_items.append(([recv[f"{n}:{hf}"] for hf in range(2)], w2.T, m2.T, v2.T))
        else:
            ffn_items.append(([recv[n]], w2, m2, v2))
    for n, res in zip(FFN_ADAM, _adam_multi(ffn_items, "adam_w_ffn", 32)):
        if n in HALVED:
            res = [a.T for a in res]
        for kind, a in zip(("grad", "delta", "new_m", "new_v"), res):
            results[kind, n] = a[None]
    for n in BIG_WEIGHTS:
        if n in FFN_ADAM:
            continue
        c, k = shards[n].shape
        w2, m2, v2 = wts[n][0], moms[n][0], vars_[n][0]
        if n in ROW_SHARDED:
            res = _adam([recv[n]], w2, m2, v2, "adam_" + n, c // 2)
        elif n in HALVED:
            res = _adam([recv[f"{n}:{hf}"] for hf in range(2)], w2.T, m2.T, v2.T, "adam_" + n, c // 2)
            res = [a.T for a in res]
        else:
            g_t = _sum_partials([recv[n]], "sum_" + n, c // 2)
            res = _adam([g_t.T[None]], w2, m2, v2, "adam_" + n, k // 2)
        for kind, a in zip(("grad", "delta", "new_m", "new_v"), res):
            results[kind, n] = a[None]

    g_mix_all = _all_gather(jnp.pad(g_mix_part, ((0, 7), (0, 0))), "allgather_g_mix")
    g_mix = _sum_partials([g_mix_all], "sum_g_mix", 8)[0:1]
    loss, sgrads = _unpack_small(_sum_partials([recv["small"]], "sum_small", _PK_ROWS), g_mix)
    for n in SMALL_WEIGHTS:
        results["grad", n] = _unstored(n, sgrads[n], wts[n])
    results.update(_adam_small(sgrads, wts, moms, vars_))
    outs = [loss, grad_x.reshape(x.shape)]
    for kind in ("grad", "delta", "new_m", "new_v"):
        outs += [results[kind, n] for n in ALL_WEIGHTS]
    return tuple(outs)
```

```python
import functools
import math

import jax
import jax.numpy as jnp
from jax import lax
from jax.experimental import pallas as pl
from jax.experimental.pallas import tpu as pltpu

F32 = jnp.float32
BF16 = jnp.bfloat16
MXU_DTYPE = jnp.bfloat16

N_DEV = 8
D_MODEL = 1024
SEQ = 2048
HEAD_DIM = 64
HEADS_PER_GROUP = 4
GROUP_W = HEADS_PER_GROUP * HEAD_DIM
DILATIONS = (1, 4, 16)
QKV_W = 3 * len(DILATIONS) * GROUP_W
Q_W = len(DILATIONS) * GROUP_W
ATT_BLOCK = 128
ROPE_DIM = 16
ROPE_THETA = 500000.0
SSM_W = 512
SSM_GROUPS = 32
SSM_CH = 16
SSM_STATE = 64
N_STATE = SSM_GROUPS * SSM_STATE
D_FF = 2816
IN_W = QKV_W + SSM_W + 2 * D_MODEL
RMS_EPS = 1e-6
NEG_INF = -1e30
LANES = 128

SCAN_SEG_PER_SAMPLE = 8
SCAN_LEN = SEQ // SCAN_SEG_PER_SAMPLE
SCAN_WC = 512
SCAN_NBLK = N_STATE // SCAN_WC
SCAN_CH = SSM_W // SCAN_NBLK
SCAN_CHUNK = 32

ADAM_LR = 0.001
ADAM_B1 = 0.9
ADAM_B2 = 0.999
ADAM_EPS = 1e-08
ADAM_WD = 0.01
ADAM_STEP = 10

VMEM_BIG = 48 * 1024 * 1024
VMEM_MID = 32 * 1024 * 1024

BIG_WEIGHTS = ("w_in", "w_glu", "w_attn_out", "w_out", "w_ffn_gate", "w_ffn_up", "w_ffn_down")
ROW_SHARDED = ("w_out", "w_ffn_down")
SMALL_WEIGHTS = ("norm_mix_g", "ssm_a_re", "ssm_a_im", "ssm_log_dt", "ssm_b_re", "ssm_b_im", "ssm_c_re", "ssm_c_im",
                 "ssm_d", "norm_ffn_g", "norm_final_g")
ALL_WEIGHTS = ("norm_mix_g", "w_in", "ssm_a_re", "ssm_a_im", "ssm_log_dt", "ssm_b_re", "ssm_b_im", "ssm_c_re", "ssm_c_im",
               "ssm_d", "w_glu", "w_attn_out", "w_out", "norm_ffn_g", "w_ffn_gate", "w_ffn_up", "w_ffn_down", "norm_final_g")


def _sigmoid(x):
    return 1.0 / (1.0 + jnp.exp(-x))


def _pallas_call(body, *, out_shape, **kw):
    single = not isinstance(out_shape, (list, tuple))
    shapes = [pltpu.HBM(s.shape, s.dtype) for s in ([out_shape] if single else out_shape)]
    call = pl.pallas_call(body, out_shape=shapes[0] if single else shapes, **kw)
    return lambda *operands: call(*[pltpu.with_memory_space_constraint(o, pltpu.HBM) for o in operands])


class _Comm:
    def __init__(self, ins, out_shapes, n_sem, n_local, start, finish):
        self.ins, self.out_shapes, self.n_sem, self.n_local = ins, out_shapes, n_sem, n_local
        self.start, self.finish = start, finish


def _mm(a, b, mode, name, tm, tn, out_dtype=F32, add=None, vmem=VMEM_BIG, comm=None, cols=None):
    if mode == "nn":
        (m, k), (_, n) = a.shape, b.shape
        a_spec = pl.BlockSpec((tm, k), lambda i, j: (i, 0))
        b_spec = pl.BlockSpec((k, tn), lambda i, j: (0, j))
        dims = (((1,), (0,)), ((), ()))
    elif mode == "nt":
        (m, k), (n, _) = a.shape, b.shape
        a_spec = pl.BlockSpec((tm, k), lambda i, j: (i, 0))
        b_spec = pl.BlockSpec((tn, k), lambda i, j: (j, 0))
        dims = (((1,), (1,)), ((), ()))
    else:
        (k, m), (_, n) = a.shape, b.shape
        first, n = cols if cols else (0, n)
        a_spec = pl.BlockSpec((k, tm), lambda i, j: (0, i))
        b_spec = pl.BlockSpec((k, tn), lambda i, j: (0, j + first // tn))
        dims = (((0,), (0,)), ((), ()))
    assert m % tm == 0 and n % tn == 0, (name, m, n, tm, tn)
    o_spec = pl.BlockSpec((tm, tn), lambda i, j: (i, j))
    has_add = add is not None

    def body(*refs):
        a_ref, b_ref, o_ref = refs[0], refs[1], refs[-1]
        acc = lax.dot_general(a_ref[...].astype(MXU_DTYPE), b_ref[...].astype(MXU_DTYPE), dims,
                              preferred_element_type=F32)
        if has_add:
            acc = acc + refs[2][...]
        o_ref[...] = acc.astype(out_dtype)

    ins = [a, b] + ([add] if has_add else [])
    in_specs = [a_spec, b_spec] + ([o_spec] if has_add else [])
    return _grid_call(body, name, (m // tm, n // tn), ins, in_specs, [o_spec],
                      [jax.ShapeDtypeStruct((m, n), out_dtype)], vmem, comm)


def _grid_call(body, name, grid, ins, in_specs, out_specs, out_shapes, vmem, comm=None, sequential=False, scratch=()):
    if comm is None:
        single = len(out_shapes) == 1
        semantics = ("arbitrary", "arbitrary") if sequential else ("parallel", "parallel")
        return _pallas_call(
            body, name=name, grid=grid, in_specs=in_specs, out_specs=out_specs[0] if single else out_specs,
            out_shape=out_shapes[0] if single else out_shapes, scratch_shapes=list(scratch),
            compiler_params=pltpu.CompilerParams(dimension_semantics=semantics, vmem_limit_bytes=vmem),
        )(*ins)
    n_in, n_out, n_cin, n_cout = len(ins), len(out_shapes), len(comm.ins), len(comm.out_shapes)
    n_io = n_in + n_cin + n_out + n_cout

    def carrying(*refs):
        own = refs[:n_in] + refs[n_in + n_cin:n_in + n_cin + n_out] + refs[n_io:len(refs) - 3]
        c_args = (refs[n_in:n_in + n_cin], refs[n_in + n_cin + n_out:n_io], *refs[-3:])

        @pl.when((pl.program_id(0) == 0) & (pl.program_id(1) == 0))
        def _():
            comm.start(*c_args)

        body(*own)

        @pl.when((pl.program_id(0) == grid[0] - 1) & (pl.program_id(1) == grid[1] - 1))
        def _():
            comm.finish(*c_args)

    hbm = pl.BlockSpec(memory_space=pl.ANY)
    return _pallas_call(
        carrying, name=name, grid=grid, in_specs=list(in_specs) + [hbm] * n_cin,
        out_specs=list(out_specs) + [hbm] * n_cout, out_shape=list(out_shapes) + list(comm.out_shapes),
        scratch_shapes=list(scratch) + [pltpu.SemaphoreType.DMA((comm.n_sem,)), pltpu.SemaphoreType.DMA((comm.n_sem,)),
                                        pltpu.SemaphoreType.DMA((comm.n_local,))],
        compiler_params=pltpu.CompilerParams(dimension_semantics=("arbitrary", "arbitrary"), vmem_limit_bytes=vmem),
    )(*ins, *comm.ins)


def _rows(body, name, n_rows, tm, ins, outs, vmem=VMEM_MID, scratch=()):
    assert n_rows % tm == 0
    arrays, in_specs = [], []
    for kind, arr in ins:
        arrays.append(arr)
        if kind == "row":
            assert n_rows % arr.shape[0] == 0, (name, arr.shape)
            in_specs.append(pl.BlockSpec((tm * arr.shape[0] // n_rows, arr.shape[1]), lambda i: (i, 0)))
        elif kind == "tab":
            nblk = arr.shape[0] // tm
            in_specs.append(pl.BlockSpec((tm, arr.shape[1]), lambda i, nblk=nblk: (i % nblk, 0)))
        else:
            in_specs.append(pl.BlockSpec(arr.shape, lambda i, nd=arr.ndim: (0,) * nd))
    out_specs, out_shape = [], []
    for kind, shp, dt in outs:
        if kind == "row":
            out_specs.append(pl.BlockSpec((tm, shp), lambda i: (i, 0)))
            out_shape.append(jax.ShapeDtypeStruct((n_rows, shp), dt))
        elif kind == "dil":
            d, wd = shp
            out_specs.append(pl.BlockSpec((tm // d, d * wd), lambda i: (i, 0)))
            out_shape.append(jax.ShapeDtypeStruct((n_rows // d, d * wd), dt))
        else:
            out_specs.append(pl.BlockSpec(shp, lambda i, nd=len(shp): (0,) * nd))
            out_shape.append(jax.ShapeDtypeStruct(shp, dt))
    res = _pallas_call(
        body, name=name, grid=(n_rows // tm,), in_specs=in_specs, out_specs=out_specs, out_shape=out_shape,
        scratch_shapes=list(scratch),
        compiler_params=pltpu.CompilerParams(dimension_semantics=("arbitrary",), vmem_limit_bytes=vmem),
    )(*arrays)
    return res


def _gather_residue(stage, ch, r, d, n):
    return stage[ch, pl.ds(r, n, stride=d), :] if d > 1 else stage[ch]


def _scatter_residue(stage, ch, r, d, n, val):
    if d > 1:
        stage[ch, pl.ds(r, n, stride=d), :] = val
    else:
        stage[ch] = val


def _lane_chunk(ch):
    return slice(ch * LANES, (ch + 1) * LANES)


def _rope_tables():
    half = ROPE_DIM // 2
    inv = jnp.power(jnp.float32(ROPE_THETA), -jnp.arange(half, dtype=F32) * 2.0 / ROPE_DIM)
    ang = jnp.arange(SEQ, dtype=F32)[:, None] * inv[None, :]
    lane = jnp.arange(LANES) % HEAD_DIM
    cosl = jnp.cos(ang)[:, lane % half]
    sinl = jnp.sin(ang)[:, lane % half]
    tab_c = jnp.where(lane < ROPE_DIM, cosl, 1.0)
    tab_lo = jnp.where(lane < half, -sinl, 0.0)
    tab_hi = jnp.where((lane >= half) & (lane < ROPE_DIM), sinl, 0.0)
    return tab_c.astype(F32), tab_lo.astype(F32), tab_hi.astype(F32)


def _rope_apply(t, tc, tlo, thi):
    half = ROPE_DIM // 2
    return t * tc + pltpu.roll(t, LANES - half, 1) * tlo + pltpu.roll(t, half, 1) * thi


def _rope_transpose(dt, tc, tlo, thi):
    half = ROPE_DIM // 2
    return dt * tc + pltpu.roll(dt * tlo, half, 1) + pltpu.roll(dt * thi, LANES - half, 1)


def _pack_dproj(dqs, dks, dvs, du, dgpre, tabs):
    tm = 256

    def body(*refs):
        dq_refs, dk_refs, dv_refs = refs[0:3], refs[3:6], refs[6:9]
        du_ref, dg_ref, tc_ref, tlo_ref, thi_ref, o_ref, stage = refs[9:16]
        n_ch = QKV_W // LANES
        halves = GROUP_W // LANES
        for grp, d in enumerate(DILATIONS):
            for which, src in enumerate((dq_refs[grp], dk_refs[grp], dv_refs[grp])):
                for res in range(d):
                    for half in range(halves):
                        _scatter_residue(stage, which * (n_ch // 3) + grp * halves + half, res, d, tm // d,
                                         src[:, _lane_chunk(res * halves + half)])
        tc, tlo, thi = tc_ref[...], tlo_ref[...], thi_ref[...]
        for ch in range(n_ch):
            piece = stage[ch]
            o_ref[:, _lane_chunk(ch)] = (_rope_transpose(piece, tc, tlo, thi) if ch < 2 * n_ch // 3 else piece).astype(BF16)
        o_ref[:, QKV_W:QKV_W + SSM_W] = du_ref[...].astype(BF16)
        o_ref[:, QKV_W + SSM_W:] = dg_ref[...].astype(BF16)

    t = du.shape[0]
    ins = [("row", a) for a in (*dqs, *dks, *dvs, du, dgpre)] + [("tab", tb) for tb in tabs]
    return _rows(body, "pack_dproj", t, tm, ins, [("row", IN_W, BF16)],
                 scratch=[pltpu.VMEM((QKV_W // LANES, tm, LANES), F32)])[0]


def _merge_groups(o_refs, l_refs, a_ref, lt_ref, nat, tm):
    halves = GROUP_W // LANES
    for grp, d in enumerate(DILATIONS[1:], start=1):
        for j, src in enumerate((o_refs[grp], l_refs[grp])):
            for res in range(d):
                for half in range(halves):
                    _scatter_residue(nat, (grp - 1) * 4 + j * 2 + half, res, d, tm // d,
                                     src[:, _lane_chunk(res * halves + half)])
    for half in range(halves):
        sl = _lane_chunk(half)
        la, lb, lc = l_refs[0][:, sl], nat[2 + half], nat[6 + half]
        m = jnp.maximum(jnp.maximum(la, lb), lc)
        ea, eb, ec = jnp.exp(la - m), jnp.exp(lb - m), jnp.exp(lc - m)
        ssum = ea + eb + ec
        a_ref[:, sl] = (ea / ssum) * o_refs[0][:, sl] + (eb / ssum) * nat[half] + (ec / ssum) * nat[4 + half]
        lt_ref[:, sl] = m + jnp.log(ssum)


def _head_sum_matrix():
    r = jnp.arange(GROUP_W) // HEAD_DIM
    return (r[:, None] == r[None, :]).astype(F32)


def _attention_cotangents(da, attn, lt, ones, rd_ref, dil, stage, tm):
    halves = GROUP_W // LANES
    rd = jnp.dot(da * attn, ones, preferred_element_type=F32, precision=lax.Precision.HIGHEST)
    rd_ref[...] = rd
    for half in range(halves):
        for j, val in enumerate((da, lt, rd)):
            stage[2 * j + half] = val[:, _lane_chunk(half)]
    for grp, d in enumerate(DILATIONS[1:], start=1):
        for j in range(3):
            for res in range(d):
                for half in range(halves):
                    dil[3 * (grp - 1) + j][:, _lane_chunk(res * halves + half)] = _gather_residue(
                        stage, 2 * j + half, res, d, tm // d)


_GELU_C = math.sqrt(2.0 / math.pi)


def _head_masks():
    lane = lax.broadcasted_iota(jnp.int32, (1, GROUP_W), 1)
    return [(lane // HEAD_DIM) == h for h in range(HEADS_PER_GROUP)]


def _stack_heads(blk, masks, fill=0.0):
    return jnp.concatenate([jnp.where(mk, blk, jnp.full_like(blk, fill)) for mk in masks], axis=0)


def _unstack_heads(stacked, masks):
    rows = stacked.shape[0] // len(masks)
    out = stacked[:rows]
    for h in range(1, len(masks)):
        out = jnp.where(masks[h], stacked[h * rows:(h + 1) * rows], out)
    return out


def _band_mask(first):
    nk = ATT_BLOCK if first else 2 * ATT_BLOCK
    qi = lax.broadcasted_iota(jnp.int32, (ATT_BLOCK, nk), 0)
    ki = lax.broadcasted_iota(jnp.int32, (ATT_BLOCK, nk), 1)
    dist = qi - ki + (0 if first else ATT_BLOCK)
    return (dist >= 0) & (dist <= ATT_BLOCK)


_NT = (((1,), (1,)), ((), ()))
_TN = (((0,), (0,)), ((), ()))


def _residues_per_step(d):
    return 4 if d >= 16 else 1


def _attn_fwd(q, k, v, group, n_samples, comm=None):
    d = DILATIONS[group]
    length = SEQ // d
    nb = length // ATT_BLOCK

    rps = _residues_per_step(d)

    def body(q_ref, k_ref, v_ref, o_ref, l_ref):
        for rl in range(rps):
            residue(q_ref, k_ref, v_ref, o_ref, l_ref, slice(rl * GROUP_W, (rl + 1) * GROUP_W))

    def residue(q_ref, k_ref, v_ref, o_ref, l_ref, cols):
        masks = _head_masks()

        def block(qs, ks, first):
            nk = ATT_BLOCK if first else 2 * ATT_BLOCK
            qb = q_ref[0, pl.ds(qs, ATT_BLOCK), cols]
            kc = k_ref[0, pl.ds(ks, nk), cols]
            vc = v_ref[0, pl.ds(ks, nk), cols]
            q4 = _stack_heads(qb, masks)
            valid = jnp.tile(_band_mask(first), (HEADS_PER_GROUP, 1))
            s = lax.dot_general(q4, kc, _NT, preferred_element_type=F32) * (HEAD_DIM ** -0.5)
            s = jnp.where(valid, s, NEG_INF)
            m = jnp.max(s, axis=-1, keepdims=True)
            p = jnp.exp(s - m)
            l = jnp.sum(p, axis=-1, keepdims=True)
            o4 = jnp.dot(p.astype(MXU_DTYPE), vc, preferred_element_type=F32) / l
            lse4 = jnp.broadcast_to(m + jnp.log(l), o4.shape)
            o_ref[0, pl.ds(qs, ATT_BLOCK), cols] = _unstack_heads(o4, masks)
            l_ref[0, pl.ds(qs, ATT_BLOCK), cols] = _unstack_heads(lse4, masks)

        block(0, 0, True)
        if nb > 1:
            def loop(n, carry):
                block(pl.multiple_of(n * ATT_BLOCK, ATT_BLOCK), pl.multiple_of((n - 1) * ATT_BLOCK, ATT_BLOCK), False)
                return carry

            lax.fori_loop(1, nb, loop, 0)

    per_sample = lambda a: a.reshape(n_samples, length, d * GROUP_W)
    spec = pl.BlockSpec((1, length, rps * GROUP_W), lambda b, r: (b, 0, r))
    shp = jax.ShapeDtypeStruct((n_samples, length, d * GROUP_W), F32)
    o, lse, *carried = _grid_call(body, f"attn_fwd_g{group}", (n_samples, d // rps), [per_sample(a) for a in (q, k, v)],
                                  [spec] * 3, [spec] * 2, [shp, shp], VMEM_MID, comm)
    flat = lambda a: a.reshape(n_samples * length, d * GROUP_W)
    return flat(o), flat(lse), carried


def _attn_bwd(q, k, v, dattn, lse_tot, rowdot, group, n_samples, comm=None):
    d = DILATIONS[group]
    length = SEQ // d
    nb = length // ATT_BLOCK

    rps = _residues_per_step(d)

    def body(q_ref, k_ref, v_ref, da_ref, lt_ref, rd_ref, dq_ref, dk_ref, dv_ref):
        dk_ref[...] = jnp.zeros_like(dk_ref)
        dv_ref[...] = jnp.zeros_like(dv_ref)
        for rl in range(rps):
            residue(q_ref, k_ref, v_ref, da_ref, lt_ref, rd_ref, dq_ref, dk_ref, dv_ref,
                    slice(rl * GROUP_W, (rl + 1) * GROUP_W))

    def residue(q_ref, k_ref, v_ref, da_ref, lt_ref, rd_ref, dq_ref, dk_ref, dv_ref, cols):
        masks = _head_masks()

        def block(qs, ks, first):
            nk = ATT_BLOCK if first else 2 * ATT_BLOCK
            qb = q_ref[0, pl.ds(qs, ATT_BLOCK), cols]
            kc = k_ref[0, pl.ds(ks, nk), cols]
            vc = v_ref[0, pl.ds(ks, nk), cols]
            da = da_ref[0, pl.ds(qs, ATT_BLOCK), cols]
            lt = lt_ref[0, pl.ds(qs, ATT_BLOCK), cols]
            rd = rd_ref[0, pl.ds(qs, ATT_BLOCK), cols]
            q4 = _stack_heads(qb, masks)
            da4 = _stack_heads(da, masks).astype(MXU_DTYPE)
            lt4 = jnp.max(_stack_heads(lt, masks, -jnp.inf), axis=-1, keepdims=True)
            rd4 = jnp.max(_stack_heads(rd, masks, -jnp.inf), axis=-1, keepdims=True)
            valid = jnp.tile(_band_mask(first), (HEADS_PER_GROUP, 1))
            s = lax.dot_general(q4, kc, _NT, preferred_element_type=F32) * (HEAD_DIM ** -0.5)
            s = jnp.where(valid, s, NEG_INF)
            p = jnp.exp(s - lt4)
            dp = lax.dot_general(da4, vc, _NT, preferred_element_type=F32)
            ds = (p * (dp - rd4) * (HEAD_DIM ** -0.5)).astype(MXU_DTYPE)
            dq_ref[0, pl.ds(qs, ATT_BLOCK), cols] = _unstack_heads(jnp.dot(ds, kc, preferred_element_type=F32), masks)
            dk_ref[0, pl.ds(ks, nk), cols] += lax.dot_general(ds, q4, _TN, preferred_element_type=F32)
            dv_ref[0, pl.ds(ks, nk), cols] += lax.dot_general(p.astype(MXU_DTYPE), da4, _TN, preferred_element_type=F32)

        block(0, 0, True)
        if nb > 1:
            def loop(n, carry):
                block(pl.multiple_of(n * ATT_BLOCK, ATT_BLOCK), pl.multiple_of((n - 1) * ATT_BLOCK, ATT_BLOCK), False)
                return carry

            lax.fori_loop(1, nb, loop, 0)

    per_sample = lambda a: a.reshape(n_samples, length, d * GROUP_W)
    spec = pl.BlockSpec((1, length, rps * GROUP_W), lambda b, r: (b, 0, r))
    shp = jax.ShapeDtypeStruct((n_samples, length, d * GROUP_W), F32)
    dq, dk, dv, *carried = _grid_call(
        body, f"attn_bwd_g{group}", (n_samples, d // rps), [per_sample(a) for a in (q, k, v, dattn, lse_tot, rowdot)],
        [spec] * 6, [spec] * 3, [shp, shp, shp], VMEM_MID, comm)
    flat = lambda a: a.reshape(n_samples * length, d * GROUP_W)
    return flat(dq), flat(dk), flat(dv), carried


def _disc(lr, li, ldt, br, bi):
    dt = jnp.exp(ldt)
    mag = jnp.exp(lr * dt)
    ab_re, ab_im = mag * jnp.cos(li * dt), mag * jnp.sin(li * dt)
    den = lr * lr + li * li
    nr, ni = ab_re - 1.0, ab_im
    f_re = (nr * lr + ni * li) / den
    f_im = (ni * lr - nr * li) / den
    return ab_re, ab_im, f_re * br - f_im * bi, f_re * bi + f_im * br


def _state_mask():
    row_g = lax.broadcasted_iota(jnp.int32, (SCAN_CH, SCAN_WC), 0) // SSM_CH
    col_g = lax.broadcasted_iota(jnp.int32, (SCAN_CH, SCAN_WC), 1) // SSM_STATE
    return row_g == col_g


def _ssm_disc(lr, li, ldt, br, bi, cr, ci):
    w = SCAN_WC

    def body(lr_ref, li_ref, ldt_ref, br_ref, bi_ref, cr_ref, ci_ref, a_ref, bb_ref, c_ref):
        ar, ai, bbr, bbi = _disc(lr_ref[...], li_ref[...], ldt_ref[...], br_ref[...], bi_ref[...])
        crv, civ = cr_ref[...], ci_ref[...]
        mask = _state_mask()
        for cb in range(SCAN_NBLK):
            sl = slice(cb * w, (cb + 1) * w)
            rows = slice(cb * SCAN_CH, (cb + 1) * SCAN_CH)
            dense = lambda comp: jnp.where(mask, jnp.tile(comp[:, sl], (SCAN_CH // SSM_CH, 1)), 0.0)
            a_ref[:, 2 * cb * w:(2 * cb + 1) * w] = ar[:, sl]
            a_ref[:, (2 * cb + 1) * w:(2 * cb + 2) * w] = ai[:, sl]
            bb_ref[rows, :w] = dense(bbr).astype(MXU_DTYPE)
            bb_ref[rows, w:] = dense(bbi).astype(MXU_DTYPE)
            c_ref[rows, :w] = dense(crv).astype(MXU_DTYPE)
            c_ref[rows, w:] = (-dense(civ)).astype(MXU_DTYPE)

    return _pallas_call(
        body, name="ssm_disc",
        out_shape=[jax.ShapeDtypeStruct((1, 2 * N_STATE), F32), jax.ShapeDtypeStruct((SSM_W, 2 * w), MXU_DTYPE),
                   jax.ShapeDtypeStruct((SSM_W, 2 * w), MXU_DTYPE)],
        compiler_params=pltpu.CompilerParams(vmem_limit_bytes=VMEM_MID),
    )(lr, li, ldt, br, bi, cr, ci)


def _group_indicator():
    s = jnp.arange(N_STATE) // SSM_STATE
    return (s[:, None] == jnp.arange(LANES)[None, :]).astype(F32)


def _ssm_param_bwd(lr, li, ldt, br, bi, da_cat, dbb_full, dc_full):
    w = SCAN_WC

    def body(lr_ref, li_ref, ldt_ref, br_ref, bi_ref, da_ref, dbb_ref, dc_ref, ind_ref,
             glr_ref, gli_ref, gldt_ref, gbr_ref, gbi_ref, gcr_ref, gci_ref):
        mask = _state_mask()

        def diag_parts(ref):
            res = ([], [])
            for cb in range(SCAN_NBLK):
                for part in range(2):
                    blk = ref[cb * SCAN_CH:(cb + 1) * SCAN_CH, part * w:(part + 1) * w]
                    res[part].append(jnp.sum(jnp.where(mask, blk, 0.0).reshape(SCAN_CH // SSM_CH, SSM_CH, w), axis=0))
            return jnp.concatenate(res[0], axis=1), jnp.concatenate(res[1], axis=1)

        dar = jnp.concatenate([da_ref[:, 2 * cb * w:(2 * cb + 1) * w] for cb in range(SCAN_NBLK)], axis=1)
        dai = jnp.concatenate([da_ref[:, (2 * cb + 1) * w:(2 * cb + 2) * w] for cb in range(SCAN_NBLK)], axis=1)
        dbbr, dbbi = diag_parts(dbb_ref)
        dcr, dci_neg = diag_parts(dc_ref)
        gcr_ref[...] = dcr
        gci_ref[...] = -dci_neg
        _, vjp = jax.vjp(_disc, lr_ref[...], li_ref[...], ldt_ref[...], br_ref[...], bi_ref[...])
        glr, gli, gldt, gbr, gbi = vjp((dar, dai, dbbr, dbbi))
        glr_ref[...] = glr
        gli_ref[...] = gli
        gldt_ref[...] = jnp.dot(jnp.broadcast_to(gldt, (8, N_STATE)), ind_ref[...], preferred_element_type=F32,
                                precision=lax.Precision.HIGHEST)
        gbr_ref[...] = gbr
        gbi_ref[...] = gbi

    v1 = jax.ShapeDtypeStruct((1, N_STATE), F32)
    v16 = jax.ShapeDtypeStruct((SSM_CH, N_STATE), F32)
    vdt = jax.ShapeDtypeStruct((8, LANES), F32)
    return _pallas_call(
        body, name="ssm_param_bwd", out_shape=[v1, v1, vdt, v16, v16, v16, v16],
        compiler_params=pltpu.CompilerParams(vmem_limit_bytes=VMEM_BIG),
    )(lr, li, ldt, br, bi, da_cat, dbb_full, dc_full, _group_indicator())


def _cmul(ar, ai, br, bi):
    return ar * br - ai * bi, ar * bi + ai * br


def _gelu_tanh(y):
    return jnp.tanh(_GELU_C * (y + 0.044715 * (y * y * y)))


def _segment_carry(er, ei, ar, ai, n_rows, reverse):
    qr, qi = ar, ai
    for _ in range(int(math.log2(SCAN_LEN))):
        qr, qi = _cmul(qr, qi, qr, qi)
    seg = lax.broadcasted_iota(jnp.int32, er.shape, 0) % SCAN_SEG_PER_SAMPLE
    shift = 1
    while shift < SCAN_SEG_PER_SAMPLE:
        keep = (seg < SCAN_SEG_PER_SAMPLE - shift) if reverse else (seg >= shift)
        amount = n_rows - shift if reverse else shift
        sr = jnp.where(keep, pltpu.roll(er, amount, 0), 0.0)
        si = jnp.where(keep, pltpu.roll(ei, amount, 0), 0.0)
        if reverse:
            er, ei = er + qr * sr + qi * si, ei + qr * si - qi * sr
        else:
            er, ei = er + qr * sr - qi * si, ei + qr * si + qi * sr
        qr, qi = _cmul(qr, qi, qr, qi)
        shift *= 2
    keep = (seg < SCAN_SEG_PER_SAMPLE - 1) if reverse else (seg >= 1)
    amount = n_rows - 1 if reverse else 1
    return jnp.where(keep, pltpu.roll(er, amount, 0), 0.0), jnp.where(keep, pltpu.roll(ei, amount, 0), 0.0)


def _ssm_fwd(u_perm, a_cat, bbc, cc, dskip, n_rows):
    t = u_perm.shape[0]
    w = SCAN_WC
    rows_c = SCAN_CHUNK * n_rows
    n_chunks = t // rows_c

    assert n_chunks % 2 == 0

    def body(u_ref, a_ref, bb_ref, c_ref, d_ref, yt_ref, yg_ref, ein_ref, bu_all, st_a, st_b, xs_a, xs_b):
        ar = jnp.broadcast_to(a_ref[:, :w], (n_rows, w))
        ai = jnp.broadcast_to(a_ref[:, w:], (n_rows, w))
        start = lambda ch: pl.multiple_of(ch * rows_c, rows_c)

        def project(ch, stage):
            res = jnp.dot(u_ref[pl.ds(start(ch), rows_c), :].astype(MXU_DTYPE), bb_ref[...], preferred_element_type=F32)
            stage[...] = res
            bu_all[pl.ds(start(ch), rows_c), :] = res

        def steps(src, r0, carry, xs=None):
            for i in range(SCAN_CHUNK):
                blk = src[pl.ds(r0 + i * n_rows, n_rows), :]
                carry = (ar * carry[0] - ai * carry[1] + blk[:, :w], ar * carry[1] + ai * carry[0] + blk[:, w:])
                if xs is not None:
                    xs[i * n_rows:(i + 1) * n_rows, :w] = carry[0]
                    xs[i * n_rows:(i + 1) * n_rows, w:] = carry[1]
            return carry

        def emit(xs, ch):
            y = lax.dot_general(xs[...].astype(MXU_DTYPE), c_ref[...], _NT, preferred_element_type=F32)
            yt = y + d_ref[...] * u_ref[pl.ds(start(ch), rows_c), :]
            yt_ref[pl.ds(start(ch), rows_c), :] = yt
            yg_ref[pl.ds(start(ch), rows_c), :] = (0.5 * yt * (1.0 + _gelu_tanh(yt))).astype(BF16)

        project(0, st_a)

        def pair1(p, carry):
            project(2 * p + 1, st_b)
            carry = steps(st_a, 0, carry)
            project(jnp.minimum(2 * p + 2, n_chunks - 1), st_a)
            return steps(st_b, 0, carry)

        zero = jnp.zeros((n_rows, w), F32)
        er, ei = lax.fori_loop(0, n_chunks // 2, pair1, (zero, zero))
        cr, ci = _segment_carry(er, ei, ar, ai, n_rows, False)
        ein_ref[:, :w] = cr
        ein_ref[:, w:] = ci

        xs_b[...] = jnp.zeros_like(xs_b)

        def pair2(p, carry):
            emit(xs_b, jnp.maximum(2 * p - 1, 0))
            carry = steps(bu_all, start(2 * p), carry, xs_a)
            emit(xs_a, 2 * p)
            return steps(bu_all, start(2 * p + 1), carry, xs_b)

        lax.fori_loop(0, n_chunks // 2, pair2, (cr, ci))
        emit(xs_b, n_chunks - 1)

    col = lambda width: pl.BlockSpec((t, width), lambda c: (0, c))
    wgt = pl.BlockSpec((SCAN_CH, 2 * w), lambda c: (c, 0))
    return _pallas_call(
        body, name="ssm_fwd", grid=(SCAN_NBLK,),
        in_specs=[col(SCAN_CH), pl.BlockSpec((1, 2 * w), lambda c: (0, c)), wgt, wgt,
                  pl.BlockSpec((1, SCAN_CH), lambda c: (0, c))],
        out_specs=[col(SCAN_CH), col(SCAN_CH), pl.BlockSpec((n_rows, 2 * w), lambda c: (0, c))],
        out_shape=[jax.ShapeDtypeStruct((t, SSM_W), F32), jax.ShapeDtypeStruct((t, SSM_W), BF16),
                   jax.ShapeDtypeStruct((n_rows, 2 * N_STATE), F32)],
        scratch_shapes=[pltpu.VMEM((t, 2 * w), F32)] + [pltpu.VMEM((rows_c, 2 * w), F32)] * 4,
        compiler_params=pltpu.CompilerParams(dimension_semantics=("parallel",), vmem_limit_bytes=VMEM_BIG),
    )(u_perm, a_cat, bbc, cc, dskip)


def _ssm_bwd(u_perm, dyg, ytot, dskip, a_cat, bbc, cc, ein, n_rows, comm=None):
    t = u_perm.shape[0]
    w = SCAN_WC
    rows_c = SCAN_CHUNK * n_rows
    n_chunks = t // rows_c

    assert n_chunks % 2 == 0
    last = n_chunks - 1

    def body(u_ref, dyg_ref, yt_ref, dk_ref, a_ref, bb_ref, c_ref, ein_ref, du_ref, gd_ref, da_ref, dbb_ref, dc_ref,
             xs_all, dy_s, st_a, st_b, buf_a, buf_b):
        ar = jnp.broadcast_to(a_ref[:, :w], (n_rows, w))
        ai = jnp.broadcast_to(a_ref[:, w:], (n_rows, w))
        zero = jnp.zeros((n_rows, w), F32)
        start = lambda ch: pl.multiple_of(ch * rows_c, rows_c)
        dbb_ref[...] = jnp.zeros_like(dbb_ref)
        dc_ref[...] = jnp.zeros_like(dc_ref)
        da_ref[...] = jnp.zeros_like(da_ref)

        yt = yt_ref[...]
        th = _gelu_tanh(yt)
        dgelu = 0.5 * (1.0 + th) + 0.5 * yt * (1.0 - th * th) * _GELU_C * (1.0 + 3.0 * 0.044715 * yt * yt)
        dy_all = dyg_ref[...] * dgelu
        dy_s[...] = dy_all
        gd_ref[...] = jnp.sum(dy_all * u_ref[...], axis=0, keepdims=True)
        dy_chunk = lambda ch: dy_s[pl.ds(start(ch), rows_c), :].astype(MXU_DTYPE)

        xs_all[0:n_rows, :] = ein_ref[...]

        def project(ch, stage):
            stage[...] = jnp.dot(u_ref[pl.ds(start(ch), rows_c), :].astype(MXU_DTYPE), bb_ref[...],
                                 preferred_element_type=F32)

        def fwd_steps(stage, ch, carry, xs):
            for i in range(SCAN_CHUNK):
                blk = stage[i * n_rows:(i + 1) * n_rows, :]
                carry = (ar * carry[0] - ai * carry[1] + blk[:, :w], ar * carry[1] + ai * carry[0] + blk[:, w:])
                for half, val in enumerate(carry):
                    xs[i * n_rows:(i + 1) * n_rows, half * w:(half + 1) * w] = val
                    xs_all[pl.ds(start(ch) + (i + 1) * n_rows, n_rows), half * w:(half + 1) * w] = val
            return carry

        def add_dc(xs, ch):
            dc_ref[...] += lax.dot_general(dy_chunk(ch), xs[...].astype(MXU_DTYPE), _TN, preferred_element_type=F32)

        project(0, st_a)

        def fwd_pair(p, carry):
            project(2 * p + 1, st_b)
            carry = fwd_steps(st_a, 2 * p, carry, buf_a)
            add_dc(buf_a, 2 * p)
            project(jnp.minimum(2 * p + 2, last), st_a)
            carry = fwd_steps(st_b, 2 * p + 1, carry, buf_b)
            add_dc(buf_b, 2 * p + 1)
            return carry

        lax.fori_loop(0, n_chunks // 2, fwd_pair, (ein_ref[:, :w], ein_ref[:, w:]))

        def project_dx(ch, stage):
            stage[...] = jnp.dot(dy_chunk(ch), c_ref[...], preferred_element_type=F32)

        def back_steps(stage, carry, g_buf=None):
            for i in reversed(range(SCAN_CHUNK)):
                blk = stage[i * n_rows:(i + 1) * n_rows, :]
                carry = (blk[:, :w] + ar * carry[0] + ai * carry[1], blk[:, w:] + ar * carry[1] - ai * carry[0])
                if g_buf is not None:
                    g_buf[i * n_rows:(i + 1) * n_rows, :w] = carry[0]
                    g_buf[i * n_rows:(i + 1) * n_rows, w:] = carry[1]
            return carry

        def first_pair(p, carry):
            project_dx(last - 2 * p - 1, st_b)
            carry = back_steps(st_a, carry)
            project_dx(jnp.maximum(last - 2 * p - 2, 0), st_a)
            return back_steps(st_b, carry)

        project_dx(last, st_a)
        sr, si = lax.fori_loop(0, n_chunks // 2, first_pair, (zero, zero))
        gr0, gi0 = _segment_carry(sr, si, ar, ai, n_rows, True)

        def post(g_buf, ch):
            g = g_buf[...]
            xp = xs_all[pl.ds(start(ch), rows_c), :]
            da_ref[:, :w] += jnp.sum(g[:, :w] * xp[:, :w] + g[:, w:] * xp[:, w:], axis=0, keepdims=True)
            da_ref[:, w:] += jnp.sum(g[:, w:] * xp[:, :w] - g[:, :w] * xp[:, w:], axis=0, keepdims=True)
            gb = g.astype(MXU_DTYPE)
            du_ref[pl.ds(start(ch), rows_c), :] = (lax.dot_general(gb, bb_ref[...], _NT, preferred_element_type=F32)
                                                   + dy_s[pl.ds(start(ch), rows_c), :] * dk_ref[...])
            dbb_ref[...] += lax.dot_general(u_ref[pl.ds(start(ch), rows_c), :].astype(MXU_DTYPE), gb, _TN,
                                            preferred_element_type=F32)

        def second_pair(p, carry):
            c1 = last - 2 * p
            project_dx(c1 - 1, st_b)
            post(buf_b, jnp.minimum(c1 + 1, last))
            carry = back_steps(st_a, carry, buf_a)
            project_dx(jnp.maximum(c1 - 2, 0), st_a)
            post(buf_a, c1)
            return back_steps(st_b, carry, buf_b)

        project_dx(last, st_a)
        buf_b[...] = jnp.zeros_like(buf_b)
        lax.fori_loop(0, n_chunks // 2, second_pair, (gr0, gi0))
        post(buf_b, 0)

    col = lambda width: pl.BlockSpec((t, width), lambda c, j: (0, c))
    wgt = pl.BlockSpec((SCAN_CH, 2 * w), lambda c, j: (c, 0))
    row = pl.BlockSpec((1, 2 * w), lambda c, j: (0, c))
    chan = pl.BlockSpec((1, SCAN_CH), lambda c, j: (0, c))
    return _grid_call(
        body, "ssm_bwd", (SCAN_NBLK, 1), [u_perm, dyg, ytot, dskip, a_cat, bbc, cc, ein],
        [col(SCAN_CH), col(SCAN_CH), col(SCAN_CH), chan, row, wgt, wgt,
         pl.BlockSpec((n_rows, 2 * w), lambda c, j: (0, c))],
        [col(SCAN_CH), chan, row, wgt, wgt],
        [jax.ShapeDtypeStruct((t, SSM_W), F32), jax.ShapeDtypeStruct((1, SSM_W), F32),
         jax.ShapeDtypeStruct((1, 2 * N_STATE), F32), jax.ShapeDtypeStruct((SSM_W, 2 * w), F32),
         jax.ShapeDtypeStruct((SSM_W, 2 * w), F32)],
        56 * 1024 * 1024, comm,
        scratch=[pltpu.VMEM((t + n_rows, 2 * w), F32), pltpu.VMEM((t, SCAN_CH), F32)]
        + [pltpu.VMEM((rows_c, 2 * w), F32)] * 4)


def _to_scan_rows(a, n_samples):
    c = a.shape[1]
    return a.reshape(n_samples, SCAN_SEG_PER_SAMPLE, SCAN_LEN, c).transpose(2, 0, 1, 3).reshape(-1, c)


def _from_scan_rows(a, n_samples):
    c = a.shape[1]
    return a.reshape(SCAN_LEN, n_samples, SCAN_SEG_PER_SAMPLE, c).transpose(1, 2, 0, 3).reshape(-1, c)


def _row_spec(tm, width):
    return pl.BlockSpec((tm, width), lambda i, j: (i, 0))


def _whole(arr):
    return pl.BlockSpec(arr.shape, lambda i, j: (0,) * arr.ndim)


def _proj_rope(x, g, w_in_t, tabs, comm=None):
    t = x.shape[0]
    tm = 256

    def body(x_ref, g_ref, w_ref, tc_ref, tlo_ref, thi_ref, h_ref, u_ref, gate_ref, *rest):
        qkv_refs, stage = rest[:9], rest[9]
        xv = x_ref[...]
        r = lax.rsqrt(jnp.mean(xv * xv, axis=-1, keepdims=True) + RMS_EPS)
        h = ((xv * r) * g_ref[...]).astype(BF16)
        h_ref[...] = h
        p = lax.dot_general(h.astype(MXU_DTYPE), w_ref[...], _NT, preferred_element_type=F32)
        u_ref[...] = p[:, QKV_W:QKV_W + SSM_W]
        gate_ref[...] = _sigmoid(p[:, QKV_W + SSM_W:])
        tc, tlo, thi = tc_ref[...], tlo_ref[...], thi_ref[...]
        n_ch = QKV_W // LANES
        for ch in range(n_ch):
            piece = p[:, _lane_chunk(ch)]
            stage[ch] = _rope_apply(piece, tc, tlo, thi) if ch < 2 * n_ch // 3 else piece
        halves = GROUP_W // LANES
        for grp, d in enumerate(DILATIONS):
            for which in range(3):
                out = qkv_refs[3 * grp + which]
                for res in range(d):
                    for half in range(halves):
                        ch = which * (n_ch // 3) + grp * halves + half
                        out[:, _lane_chunk(res * halves + half)] = _gather_residue(stage, ch, res, d, tm // d).astype(BF16)

    tab = pl.BlockSpec((tm, LANES), lambda i, j: (i % (SEQ // tm), 0))
    widths = [(D_MODEL, BF16), (SSM_W, F32), (2 * D_MODEL, F32)]
    out_specs = [_row_spec(tm, wd) for wd, _ in widths]
    out_shapes = [jax.ShapeDtypeStruct((t, wd), dt) for wd, dt in widths]
    for d in DILATIONS:
        out_specs += [_row_spec(tm // d, d * GROUP_W)] * 3
        out_shapes += [jax.ShapeDtypeStruct((t // d, d * GROUP_W), BF16)] * 3
    return _grid_call(
        body, "proj_rope", (t // tm, 1), [x, g, w_in_t, *tabs],
        [_row_spec(tm, D_MODEL), _whole(g), _whole(w_in_t), tab, tab, tab], out_specs, out_shapes, VMEM_BIG, comm,
        scratch=[pltpu.VMEM((QKV_W // LANES, tm, LANES), F32)])


def _branch_outputs(attn_ref, yg_ref, wao_ref, wglu_ref):
    attn_d = lax.dot_general(attn_ref[...].astype(MXU_DTYPE), wao_ref[...], _NT, preferred_element_type=F32)
    z = lax.dot_general(yg_ref[...].astype(MXU_DTYPE), wglu_ref[...], _NT, preferred_element_type=F32)
    return attn_d, z[:, :D_MODEL], _sigmoid(z[:, D_MODEL:])


def _mix_out_rms(os_, lses, yg, gates, x, w_ao_t, w_glu_t, w_out, g, comm=None):
    t = x.shape[0]
    tm = 256

    def body(o0, o1, o2, l0, l1, l2, yg_ref, gate_ref, x_ref, wao_ref, wglu_ref, wout_ref, g_ref,
             attn_ref, lt_ref, m_ref, x1_ref, h_ref, nat):
        _merge_groups((o0, o1, o2), (l0, l1, l2), attn_ref, lt_ref, nat, tm)
        attn_d, za, sb = _branch_outputs(attn_ref, yg_ref, wao_ref, wglu_ref)
        merged = (gate_ref[:, :D_MODEL] * attn_d + gate_ref[:, D_MODEL:] * (za * sb)).astype(BF16)
        m_ref[...] = merged
        x1 = x_ref[...] + jnp.dot(merged.astype(MXU_DTYPE), wout_ref[...], preferred_element_type=F32)
        x1_ref[...] = x1
        r = lax.rsqrt(jnp.mean(x1 * x1, axis=-1, keepdims=True) + RMS_EPS)
        h_ref[...] = ((x1 * r) * g_ref[...]).astype(BF16)

    dil_specs = [_row_spec(tm // d, d * GROUP_W) for d in DILATIONS] * 2
    return _grid_call(
        body, "mix_out_rms", (t // tm, 1), [*os_, *lses, yg, gates, x, w_ao_t, w_glu_t, w_out, g],
        dil_specs + [_row_spec(tm, SSM_W), _row_spec(tm, 2 * D_MODEL), _row_spec(tm, D_MODEL),
                     _whole(w_ao_t), _whole(w_glu_t), _whole(w_out), _whole(g)],
        [_row_spec(tm, GROUP_W)] * 2 + [_row_spec(tm, D_MODEL)] * 3,
        [jax.ShapeDtypeStruct((t, GROUP_W), F32)] * 2
        + [jax.ShapeDtypeStruct((t, D_MODEL), BF16), jax.ShapeDtypeStruct((t, D_MODEL), F32),
           jax.ShapeDtypeStruct((t, D_MODEL), BF16)], VMEM_BIG, comm, scratch=[pltpu.VMEM((8, tm, LANES), F32)])


def _mix_bwd(dx1b, attn, lse_tot, yg, gates, w_ao_t, w_glu_t, w_out, comm=None):
    t = dx1b.shape[0]
    tm = 256

    def body(dx_ref, attn_ref, lt_ref, yg_ref, gate_ref, wao_ref, wglu_ref, wout_ref, ones_ref,
             dad_ref, dz_ref, dg_ref, da_ref, dyg_ref, rd_ref, *rest):
        dm = lax.dot_general(dx_ref[...], wout_ref[...], _NT, preferred_element_type=F32)
        attn_d, za, sb = _branch_outputs(attn_ref, yg_ref, wao_ref, wglu_ref)
        g0, g1 = gate_ref[:, :D_MODEL], gate_ref[:, D_MODEL:]
        dad = (dm * g0).astype(BF16)
        dad_ref[...] = dad
        ds = dm * g1
        dza, dzb = (ds * sb).astype(BF16), (ds * za * sb * (1.0 - sb)).astype(BF16)
        dz_ref[:, :D_MODEL] = dza
        dz_ref[:, D_MODEL:] = dzb
        dg_ref[:, :D_MODEL] = (dm * attn_d * g0 * (1.0 - g0)).astype(BF16)
        dg_ref[:, D_MODEL:] = (dm * (za * sb) * g1 * (1.0 - g1)).astype(BF16)
        da = jnp.dot(dad.astype(MXU_DTYPE), wao_ref[...], preferred_element_type=F32)
        da_ref[...] = da
        dyg_ref[...] = (jnp.dot(dza.astype(MXU_DTYPE), wglu_ref[:D_MODEL, :], preferred_element_type=F32)
                        + jnp.dot(dzb.astype(MXU_DTYPE), wglu_ref[D_MODEL:, :], preferred_element_type=F32))
        _attention_cotangents(da, attn_ref[...], lt_ref[...], ones_ref[...], rd_ref, rest[:6], rest[6], tm)

    widths = [(D_MODEL, BF16), (2 * D_MODEL, BF16), (2 * D_MODEL, BF16), (GROUP_W, F32), (SSM_W, F32), (GROUP_W, F32)]
    out_specs = [_row_spec(tm, wd) for wd, _ in widths]
    out_shapes = [jax.ShapeDtypeStruct((t, wd), dt) for wd, dt in widths]
    for d in DILATIONS[1:]:
        out_specs += [_row_spec(tm // d, d * GROUP_W)] * 3
        out_shapes += [jax.ShapeDtypeStruct((t // d, d * GROUP_W), F32)] * 3
    ones = _head_sum_matrix()
    return _grid_call(
        body, "mix_bwd", (t // tm, 1), [dx1b, attn, lse_tot, yg, gates, w_ao_t, w_glu_t, w_out, ones],
        [_row_spec(tm, D_MODEL), _row_spec(tm, GROUP_W), _row_spec(tm, GROUP_W), _row_spec(tm, SSM_W),
         _row_spec(tm, 2 * D_MODEL), _whole(w_ao_t), _whole(w_glu_t), _whole(w_out), _whole(ones)],
        out_specs, out_shapes, VMEM_BIG, comm, scratch=[pltpu.VMEM((6, tm, LANES), F32)])


FFN_TN = D_FF // 2
MXU_COLS = 256


def _ffn_in_swiglu(h2, w_gate_t, w_up_t, comm=None):
    t = h2.shape[0]
    tm = 512

    def body(h_ref, wg_ref, wu_ref, a_ref, b_ref, f_ref):
        h = h_ref[...].astype(MXU_DTYPE)
        for c0 in range(0, FFN_TN, MXU_COLS):
            sl = slice(c0, min(c0 + MXU_COLS, FFN_TN))
            a = lax.dot_general(h, wg_ref[sl, :], _NT, preferred_element_type=F32)
            b = lax.dot_general(h, wu_ref[sl, :], _NT, preferred_element_type=F32)
            a_ref[:, sl] = a
            b_ref[:, sl] = b
            f_ref[:, sl] = (a * _sigmoid(a) * b).astype(BF16)

    tile = pl.BlockSpec((tm, FFN_TN), lambda j, i: (i, j))
    wspec = pl.BlockSpec((FFN_TN, D_MODEL), lambda j, i: (j, 0))
    return _grid_call(
        body, "ffn_in_swiglu", (D_FF // FFN_TN, t // tm), [h2, w_gate_t, w_up_t],
        [pl.BlockSpec((tm, D_MODEL), lambda j, i: (i, 0)), wspec, wspec],
        [tile] * 3, [jax.ShapeDtypeStruct((t, D_FF), F32)] * 2 + [jax.ShapeDtypeStruct((t, D_FF), BF16)], VMEM_BIG, comm)


def _ffn_down_final(f, w_down, x1, target, g):
    t = x1.shape[0]
    tm = 256

    def body(f_ref, w_ref, x1_ref, t_ref, g_ref, dx_ref, dxb_ref, loss_ref, gg_ref):
        @pl.when(pl.program_id(0) == 0)
        def _():
            loss_ref[...] = jnp.zeros_like(loss_ref)
            gg_ref[...] = jnp.zeros_like(gg_ref)

        xv = x1_ref[...] + jnp.dot(f_ref[...].astype(MXU_DTYPE), w_ref[...], preferred_element_type=F32)
        gv = g_ref[...]
        r = lax.rsqrt(jnp.mean(xv * xv, axis=-1, keepdims=True) + RMS_EPS)
        n = xv * r
        diff = n * gv - t_ref[...]
        per_tok = jnp.mean(diff * diff, axis=-1, keepdims=True)
        loss_ref[...] += 0.5 * jnp.sum(per_tok, axis=0, keepdims=True)
        dy = diff / xv.shape[-1]
        gg_ref[...] += jnp.sum(dy * n, axis=0, keepdims=True)
        dn = dy * gv
        dx = r * (dn - n * jnp.mean(dn * n, axis=-1, keepdims=True))
        dx_ref[...] = dx
        dxb_ref[...] = dx.astype(BF16)

    acc = lambda shp: pl.BlockSpec(shp, lambda i, j: (0, 0))
    return _grid_call(
        body, "ffn_down_final", (t // tm, 1), [f, w_down, x1, target, g],
        [_row_spec(tm, D_FF), _whole(w_down), _row_spec(tm, D_MODEL), _row_spec(tm, D_MODEL), _whole(g)],
        [_row_spec(tm, D_MODEL)] * 2 + [acc((8, LANES)), acc((1, D_MODEL))],
        [jax.ShapeDtypeStruct((t, D_MODEL), F32), jax.ShapeDtypeStruct((t, D_MODEL), BF16),
         jax.ShapeDtypeStruct((8, LANES), F32), jax.ShapeDtypeStruct((1, D_MODEL), F32)], VMEM_BIG, sequential=True)


def _d_f_swiglu_bwd(dx2b, w_down, a, b):
    t = a.shape[0]
    tm = 512

    def body(dx_ref, w_ref, a_ref, b_ref, da_ref, db_ref):
        d = lax.dot_general(dx_ref[...], w_ref[...], _NT, preferred_element_type=F32)
        av, bv = a_ref[...], b_ref[...]
        sg = _sigmoid(av)
        da_ref[...] = (d * bv * sg * (1.0 + av * (1.0 - sg))).astype(BF16)
        db_ref[...] = (d * av * sg).astype(BF16)

    tile = pl.BlockSpec((tm, FFN_TN), lambda j, i: (i, j))
    return _grid_call(
        body, "d_f_swiglu_bwd", (D_FF // FFN_TN, t // tm), [dx2b, w_down, a, b],
        [pl.BlockSpec((tm, D_MODEL), lambda j, i: (i, 0)), pl.BlockSpec((FFN_TN, D_MODEL), lambda j, i: (j, 0)), tile, tile],
        [tile] * 2, [jax.ShapeDtypeStruct((t, D_FF), BF16)] * 2, VMEM_BIG)


def _ffn_weight_grads(f, dx2b, da, db, h2):
    t = h2.shape[0]
    tm = 256
    half = D_MODEL // 2

    def body(f_ref, dx_ref, da_ref, db_ref, h_ref, dn_ref, g0_ref, g1_ref, u0_ref, u1_ref):
        dn_ref[...] = lax.dot_general(f_ref[...].astype(MXU_DTYPE), dx_ref[...].astype(MXU_DTYPE), _TN,
                                      preferred_element_type=F32).astype(BF16)
        h = h_ref[...].astype(MXU_DTYPE)
        for src, (lo_ref, hi_ref) in ((da_ref, (g0_ref, g1_ref)), (db_ref, (u0_ref, u1_ref))):
            prod = lax.dot_general(src[...].astype(MXU_DTYPE), h, _TN, preferred_element_type=F32)
            lo_ref[...] = prod[:, :half].astype(BF16)
            hi_ref[...] = prod[:, half:].astype(BF16)

    col = pl.BlockSpec((t, tm), lambda i, j: (0, i))
    out = pl.BlockSpec((tm, half), lambda i, j: (i, 0))
    return _grid_call(body, "mm_g_ffn", (D_FF // tm, 1), [f, dx2b, da, db, h2],
                      [col, _whole(dx2b), col, col, _whole(h2)], [_row_spec(tm, D_MODEL)] + [out] * 4,
                      [jax.ShapeDtypeStruct((D_FF, D_MODEL), BF16)] + [jax.ShapeDtypeStruct((D_FF, half), BF16)] * 4,
                      56 * 1024 * 1024)


def _branch_weight_grads(dz, yg, dattn_d, attn):
    t = yg.shape[0]
    steps = 4
    tz, ta = dz.shape[1] // steps, dattn_d.shape[1] // steps

    def body(dz_ref, yg_ref, dad_ref, attn_ref, gz_ref, ga_ref):
        gz_ref[...] = lax.dot_general(dz_ref[...].astype(MXU_DTYPE), yg_ref[...].astype(MXU_DTYPE), _TN,
                                      preferred_element_type=F32).astype(BF16)
        ga_ref[...] = lax.dot_general(dad_ref[...].astype(MXU_DTYPE), attn_ref[...].astype(MXU_DTYPE), _TN,
                                      preferred_element_type=F32).astype(BF16)

    col = lambda wd: pl.BlockSpec((t, wd), lambda i, j: (0, i))
    return _grid_call(body, "mm_g_branches", (steps, 1), [dz, yg, dattn_d, attn],
                      [col(tz), _whole(yg), col(ta), _whole(attn)], [_row_spec(tz, SSM_W), _row_spec(ta, GROUP_W)],
                      [jax.ShapeDtypeStruct((dz.shape[1], SSM_W), BF16), jax.ShapeDtypeStruct((dattn_d.shape[1], GROUP_W), BF16)],
                      VMEM_BIG)


def _mm_rms_bwd(operands, weights, x, g, dres, name, comm=None):
    t = x.shape[0]
    tm = 256
    n_op = len(operands)

    def body(*refs):
        a_refs, w_refs = refs[:n_op], refs[n_op:2 * n_op]
        x_ref, g_ref, dres_ref, dx_ref, dxb_ref, gg_ref = refs[2 * n_op:]

        @pl.when(pl.program_id(0) == 0)
        def _():
            gg_ref[...] = jnp.zeros_like(gg_ref)

        dh = None
        for a_ref, w_ref in zip(a_refs, w_refs):
            part = jnp.dot(a_ref[...].astype(MXU_DTYPE), w_ref[...], preferred_element_type=F32)
            dh = part if dh is None else dh + part
        xv = x_ref[...]
        r = lax.rsqrt(jnp.mean(xv * xv, axis=-1, keepdims=True) + RMS_EPS)
        n = xv * r
        gg_ref[...] += jnp.sum(dh * n, axis=0, keepdims=True)
        dn = dh * g_ref[...]
        dx = dres_ref[...] + r * (dn - n * jnp.mean(dn * n, axis=-1, keepdims=True))
        dx_ref[...] = dx
        dxb_ref[...] = dx.astype(BF16)

    d = x.shape[1]
    return _grid_call(
        body, name, (t // tm, 1), [*operands, *weights, x, g, dres],
        [_row_spec(tm, a.shape[1]) for a in operands] + [_whole(wk) for wk in weights]
        + [_row_spec(tm, d), _whole(g), _row_spec(tm, d)],
        [_row_spec(tm, d)] * 2 + [pl.BlockSpec((1, d), lambda i, j: (0, 0))],
        [jax.ShapeDtypeStruct((t, d), F32), jax.ShapeDtypeStruct((t, d), BF16), jax.ShapeDtypeStruct((1, d), F32)],
        VMEM_BIG, comm, sequential=True)


def _flat_small(small):
    perm_b = lambda a: a.reshape(SSM_GROUPS, SSM_STATE, SSM_CH).transpose(2, 0, 1).reshape(SSM_CH, N_STATE)
    perm_c = lambda a: a.reshape(SSM_GROUPS, SSM_CH, SSM_STATE).transpose(1, 0, 2).reshape(SSM_CH, N_STATE)
    return dict(
        g_mix=small["norm_mix_g"].reshape(1, D_MODEL), g_ffn=small["norm_ffn_g"].reshape(1, D_MODEL),
        g_fin=small["norm_final_g"].reshape(1, D_MODEL),
        lr=small["ssm_a_re"].reshape(1, N_STATE), li=small["ssm_a_im"].reshape(1, N_STATE),
        ldt=jnp.repeat(small["ssm_log_dt"].reshape(SSM_GROUPS), SSM_STATE).reshape(1, N_STATE),
        br=perm_b(small["ssm_b_re"]), bi=perm_b(small["ssm_b_im"]),
        cr=perm_c(small["ssm_c_re"]), ci=perm_c(small["ssm_c_im"]), dskip=small["ssm_d"].reshape(1, SSM_W))


AG_HOSTS = {"proj_rope": ("w_glu", "w_attn_out", "w_out", "w_ffn_gate"), "mix_out_rms": ("w_ffn_up",),
            "ffn_in_swiglu": ("w_ffn_down",)}
HALVED = ("w_ffn_gate", "w_ffn_up", "w_in")
FFN_ADAM = ("w_in", "w_ffn_gate", "w_ffn_up", "w_ffn_down", "w_out")
A2A_HOSTS = {"d_h2_rms": ("w_ffn_down",), "mix_bwd": ("w_ffn_gate:0", "w_out"), "attn_bwd_g1": ("w_glu",),
             "attn_bwd_g2": ("w_attn_out",), "ssm_bwd": ("w_ffn_gate:1", "w_ffn_up:0", "w_ffn_up:1"),
             "mm_g_in1": ("w_in:0",), "d_h0_rms": ("w_in:1",)}
SMALL_HOST = "mm_g_in0"


def _local_step(x, target, w, small, shards=None):
    t = x.shape[0]
    n_samples = t // SEQ
    n_rows = n_samples * SCAN_SEG_PER_SAMPLE
    tabs = _rope_tables()
    w = dict(w)
    fs = _flat_small(small)
    g_mix, g_ffn, g_fin, dskip = fs["g_mix"], fs["g_ffn"], fs["g_fin"], fs["dskip"]
    a_cat, bbc, cc = _ssm_disc(fs["lr"], fs["li"], fs["ldt"], fs["br"], fs["bi"], fs["cr"], fs["ci"])
    big, recv, small_pack = {}, {}, []

    def comm_of(name):
        if shards is None:
            return None
        if name == SMALL_HOST:
            return _ag_comm([(small_pack[0], 0, 0)], [(N_DEV, *small_pack[0].shape)])
        if name in AG_HOSTS:
            names = AG_HOSTS[name]
            return _ag_comm([(shards[n], j, 0) for j, n in enumerate(names)], [(N_DEV, *shards[n].shape) for n in names])
        if name in A2A_HOSTS:
            return _a2a_comm([(big[n].reshape(N_DEV, -1, big[n].shape[1]), 0) for n in A2A_HOSTS[name]])
        return None

    def absorb(name, carried):
        if name == SMALL_HOST:
            recv["small"] = carried[0]
        for n, a3 in zip(AG_HOSTS.get(name, ()), carried):
            w[n] = a3.reshape(-1, a3.shape[2])
        for n, a3 in zip(A2A_HOSTS.get(name, ()), carried):
            recv[n] = a3

    def mm(a, b, mode, name, tm, tn, **kw):
        comm = comm_of(name)
        if comm is None:
            return _mm(a, b, mode, name, tm, tn, **kw)
        out, *carried = _mm(a, b, mode, name, tm, tn, comm=comm, **kw)
        absorb(name, carried)
        return out

    h0, u, gates, *rest = _proj_rope(x, g_mix, w["w_in"], tabs, comm_of("proj_rope"))
    qkv = [rest[3 * g:3 * g + 3] for g in range(3)]
    absorb("proj_rope", rest[9:])
    os_, lses = [], []
    for g in range(3):
        o_g, l_g, carried = _attn_fwd(*qkv[g], g, n_samples, comm_of(f"attn_fwd_g{g}"))
        absorb(f"attn_fwd_g{g}", carried)
        os_.append(o_g)
        lses.append(l_g)
    u_perm = _to_scan_rows(u, n_samples)
    ytot, yg_perm, ein = _ssm_fwd(u_perm, a_cat, bbc, cc, dskip, n_rows)
    yg = _from_scan_rows(yg_perm, n_samples)

    attn, lse_tot, merged, x1, h2, *carried = _mix_out_rms(os_, lses, yg, gates, x, w["w_attn_out"], w["w_glu"], w["w_out"],
                                                           g_ffn, comm_of("mix_out_rms"))
    absorb("mix_out_rms", carried)
    ffn_a, ffn_b, f, *carried = _ffn_in_swiglu(h2, w["w_ffn_gate"], w["w_ffn_up"], comm_of("ffn_in_swiglu"))
    absorb("ffn_in_swiglu", carried)
    dx2, dx2b, loss_blk, g_gfin = _ffn_down_final(f, w["w_ffn_down"], x1, target, g_fin)

    da, db = _d_f_swiglu_bwd(dx2b, w["w_ffn_down"], ffn_a, ffn_b)
    half = D_MODEL // 2
    (big["w_ffn_down"], big["w_ffn_gate:0"], big["w_ffn_gate:1"], big["w_ffn_up:0"],
     big["w_ffn_up:1"]) = _ffn_weight_grads(f, dx2b, da, db, h2)
    dx1, dx1b, g_gffn, *carried = _mm_rms_bwd([da, db], [w["w_ffn_gate"], w["w_ffn_up"]], x1, g_ffn, dx2, "d_h2_rms",
                                              comm_of("d_h2_rms"))
    absorb("d_h2_rms", carried)

    big["w_out"] = mm(merged, dx1b, "tn", "mm_g_out", 256, D_MODEL, out_dtype=BF16)
    dattn_d, dz, dgpre, dattn, dyg, rowdot, *rest = _mix_bwd(dx1b, attn, lse_tot, yg, gates, w["w_attn_out"], w["w_glu"],
                                                             w["w_out"], comm_of("mix_bwd"))
    cot = [(dattn, lse_tot, rowdot), tuple(rest[:3]), tuple(rest[3:6])]
    absorb("mix_bwd", rest[6:])

    big["w_glu"], big["w_attn_out"] = _branch_weight_grads(dz, yg, dattn_d, attn)
    dqs, dks, dvs = [], [], []
    for g in range(3):
        dq_g, dk_g, dv_g, carried = _attn_bwd(*qkv[g], *cot[g], g, n_samples, comm_of(f"attn_bwd_g{g}"))
        absorb(f"attn_bwd_g{g}", carried)
        dqs.append(dq_g)
        dks.append(dk_g)
        dvs.append(dv_g)

    dyg_perm = _to_scan_rows(dyg, n_samples)
    du_perm, g_dskip, da_cat, dbb_full, dc_full, *carried = _ssm_bwd(u_perm, dyg_perm, ytot, dskip, a_cat, bbc, cc, ein,
                                                                   n_rows, comm_of("ssm_bwd"))
    absorb("ssm_bwd", carried)
    du = _from_scan_rows(du_perm, n_samples)
    g_lr, g_li, g_ldt, g_br, g_bi, g_cr, g_ci = _ssm_param_bwd(
        fs["lr"], fs["li"], fs["ldt"], fs["br"], fs["bi"], da_cat, dbb_full, dc_full)

    small_pack.append(_pack_small(dict(lr=g_lr, li=g_li, ldt=g_ldt, br=g_br, bi=g_bi, cr=g_cr, ci=g_ci, dskip=g_dskip,
                                       g_ffn=g_gffn, g_fin=g_gfin, loss=loss_blk)))

    dproj = _pack_dproj(dqs, dks, dvs, du, dgpre, tabs)
    for hf in range(2):
        big[f"w_in:{hf}"] = mm(dproj, h0, "tn", f"mm_g_in{hf}", 256, half, out_dtype=BF16, cols=(hf * half, half))
    grad_x, _, g_gmix, *carried = _mm_rms_bwd([dproj], [w["w_in"]], x, g_mix, dx1, "d_h0_rms", comm_of("d_h0_rms"))
    absorb("d_h0_rms", carried)
    return grad_x, (big if shards is None else recv), small_pack[0], g_gmix


_MESH = pl.DeviceIdType.MESH


def _all_gather(block, name):
    rows, lanes = block.shape

    def body(x_ref, out_ref, send_sems, recv_sems, local_sem):
        x, y, c = lax.axis_index("x"), lax.axis_index("y"), lax.axis_index("c")
        me, sibling = (x, y, c), (x, y, 1 - c)
        chips = [(1 - x, y), (x, 1 - y), (1 - x, 1 - y)]

        def slot(px, py, pc):
            return out_ref.at[4 * px + 2 * py + pc]

        def copy(k, blk, to, src=None):
            return pltpu.make_async_remote_copy(
                src_ref=slot(*blk) if src is None else src, dst_ref=slot(*blk), send_sem=send_sems.at[k],
                recv_sem=recv_sems.at[k], device_id=to, device_id_type=_MESH)

        mine = pltpu.make_async_copy(x_ref, slot(*me), local_sem)
        mine.start()
        first = [copy(0, me, sibling, src=x_ref)]
        first += [copy(1 + j, me, (*chip, c), src=x_ref) for j, chip in enumerate(chips)]
        for cp in first:
            cp.start()
        passed = [copy(4 + j, (*chip, c), sibling) for j, chip in enumerate(chips)]
        for j, chip in enumerate(chips):
            copy(1 + j, (*chip, c), me).wait_recv()
            passed[j].start()
        copy(0, sibling, me).wait_recv()
        for j, chip in enumerate(chips):
            copy(4 + j, (*chip, 1 - c), me).wait_recv()
        for cp in first + passed:
            cp.wait_send()
        mine.wait()

    return _pallas_call(
        body, name=name, out_shape=jax.ShapeDtypeStruct((N_DEV, rows, lanes), block.dtype),
        in_specs=[pl.BlockSpec(memory_space=pl.ANY)], out_specs=pl.BlockSpec(memory_space=pl.ANY),
        scratch_shapes=[pltpu.SemaphoreType.DMA((7,)), pltpu.SemaphoreType.DMA((7,)), pltpu.SemaphoreType.DMA],
    )(block)


def _ag_comm(items, bufs):
    def plan(in_refs, out_refs, send_sems, recv_sems, local_sems):
        x, y, c = lax.axis_index("x"), lax.axis_index("y"), lax.axis_index("c")
        me, sibling = (x, y, c), (x, y, 1 - c)
        chips = [(1 - x, y), (x, 1 - y), (1 - x, 1 - y)]
        plans = []
        for t, (_, buf, slot0) in enumerate(items):
            x_ref, out_ref = in_refs[t], out_refs[buf]

            def slot(px, py, pc, out_ref=out_ref, slot0=slot0):
                return out_ref.at[slot0 + 4 * px + 2 * py + pc]

            def copy(k, blk, to, src=None, t=t, slot=slot):
                return pltpu.make_async_remote_copy(
                    src_ref=slot(*blk) if src is None else src, dst_ref=slot(*blk), send_sem=send_sems.at[7 * t + k],
                    recv_sem=recv_sems.at[7 * t + k], device_id=to, device_id_type=_MESH)

            plans.append(dict(
                mine=pltpu.make_async_copy(x_ref, slot(*me), local_sems.at[t]),
                first=[copy(0, me, sibling, src=x_ref)] + [copy(1 + j, me, (*chip, c), src=x_ref)
                                                           for j, chip in enumerate(chips)],
                passed=[copy(4 + j, (*chip, c), sibling) for j, chip in enumerate(chips)],
                from_ici=[copy(1 + j, (*chip, c), me) for j, chip in enumerate(chips)],
                from_sibling=[copy(0, sibling, me)] + [copy(4 + j, (*chip, 1 - c), me) for j, chip in enumerate(chips)]))
        return plans

    def start(*refs):
        for p in plan(*refs):
            p["mine"].start()
            for cp in p["first"]:
                cp.start()

    def finish(*refs):
        plans = plan(*refs)
        for p in plans:
            for arrived, onward in zip(p["from_ici"], p["passed"]):
                arrived.wait_recv()
                onward.start()
        for p in plans:
            for arrived in p["from_sibling"]:
                arrived.wait_recv()
            for cp in p["first"] + p["passed"]:
                cp.wait_send()
            p["mine"].wait()

    dtype_of = {buf: shard.dtype for shard, buf, _ in items}
    out_shapes = [jax.ShapeDtypeStruct(b, dtype_of[j]) for j, b in enumerate(bufs)]
    return _Comm([it[0] for it in items], out_shapes, 7 * len(items), len(items), start, finish)


def _a2a_comm(items):
    def plan(in_refs, out_refs, send_sems, recv_sems, local_sems):
        x, y, c = lax.axis_index("x"), lax.axis_index("y"), lax.axis_index("c")
        my = 4 * x + 2 * y + c
        copies, locals_ = [], []
        for t, (_, slot0) in enumerate(items):
            s_ref, r_ref = in_refs[t], out_refs[t]
            locals_.append(pltpu.make_async_copy(s_ref.at[slot0 + my], r_ref.at[my], local_sems.at[t]))
            for kk in range(1, N_DEV):
                px = 1 - x if kk & 4 else x
                py = 1 - y if kk & 2 else y
                pc = 1 - c if kk & 1 else c
                copies.append(pltpu.make_async_remote_copy(
                    src_ref=s_ref.at[slot0 + 4 * px + 2 * py + pc], dst_ref=r_ref.at[my],
                    send_sem=send_sems.at[7 * t + kk - 1], recv_sem=recv_sems.at[7 * t + kk - 1],
                    device_id=(px, py, pc), device_id_type=_MESH))
        return copies, locals_

    def start(*refs):
        copies, locals_ = plan(*refs)
        for cp in locals_ + copies:
            cp.start()

    def finish(*refs):
        copies, locals_ = plan(*refs)
        for cp in copies + locals_:
            cp.wait()

    out_shapes = [jax.ShapeDtypeStruct((N_DEV,) + it[0].shape[1:], it[0].dtype) for it in items]
    return _Comm([it[0] for it in items], out_shapes, 7 * len(items), len(items), start, finish)


def _adam_math(g, w, m, v):
    m_new = ADAM_B1 * m + (1.0 - ADAM_B1) * g
    v_new = ADAM_B2 * v + (1.0 - ADAM_B2) * jnp.square(g)
    m_hat = m_new / (1.0 - ADAM_B1 ** ADAM_STEP)
    v_hat = v_new / (1.0 - ADAM_B2 ** ADAM_STEP)
    return -ADAM_LR * (m_hat / (jnp.sqrt(v_hat) + ADAM_EPS) + ADAM_WD * w), m_new, v_new


def _sum_partials(parts, name, tm):
    n, rows, _ = parts[0].shape
    widths = [p.shape[2] for p in parts]

    def body(*refs):
        g_ref, off = refs[-1], 0
        for p_ref, wd in zip(refs[:-1], widths):
            g = p_ref[0].astype(F32)
            for s in range(1, n):
                g = g + p_ref[s].astype(F32)
            g_ref[:, off:off + wd] = g
            off += wd

    return _pallas_call(
        body, name=name, grid=(rows // tm,), in_specs=[pl.BlockSpec((n, tm, wd), lambda i: (0, i, 0)) for wd in widths],
        out_specs=pl.BlockSpec((tm, sum(widths)), lambda i: (i, 0)),
        out_shape=jax.ShapeDtypeStruct((rows, sum(widths)), F32),
        compiler_params=pltpu.CompilerParams(dimension_semantics=("parallel",), vmem_limit_bytes=VMEM_MID),
    )(*parts)


def _adam(parts, w, m, v, name, tm):
    return _adam_multi([(parts, w, m, v)], name, tm)[0]


def _adam_multi(items, name, tm):
    n = items[0][0][0].shape[0]
    widths = [[p.shape[2] for p in parts] for parts, _, _, _ in items]
    n_in = [len(wd) + 3 for wd in widths]
    tiles = [parts[0].shape[1] // tm for parts, _, _, _ in items]
    steps = max(tiles)

    def body(*refs):
        ins, outs = refs[:sum(n_in)], refs[sum(n_in):]
        at = 0
        for k, wds in enumerate(widths):
            p_refs = ins[at:at + len(wds)]
            w_ref, m_ref, v_ref = ins[at + len(wds):at + n_in[k]]
            g_ref, d_ref, nm_ref, nv_ref = outs[4 * k:4 * k + 4]
            at += n_in[k]

            def update(p_refs=p_refs, wds=wds, w_ref=w_ref, m_ref=m_ref, v_ref=v_ref, g_ref=g_ref, d_ref=d_ref,
                       nm_ref=nm_ref, nv_ref=nv_ref):
                off = 0
                for p_ref, wd in zip(p_refs, wds):
                    g = p_ref[0].astype(F32)
                    for s in range(1, n):
                        g = g + p_ref[s].astype(F32)
                    sl = slice(off, off + wd)
                    g_ref[:, sl] = g
                    d_ref[:, sl], nm_ref[:, sl], nv_ref[:, sl] = _adam_math(g, w_ref[:, sl], m_ref[:, sl],
                                                                            v_ref[:, sl])
                    off += wd

            if tiles[k] == steps:
                update()
            else:
                pl.when(pl.program_id(0) < tiles[k])(update)

    in_specs, out_specs, out_shape, operands = [], [], [], []
    for (parts, w, m, v), wds, nt in zip(items, widths, tiles):
        rows = nt * tm
        assert parts[0].shape[:2] == (n, rows) and w.shape == (rows, sum(wds))
        row = pl.BlockSpec((tm, sum(wds)), lambda i, nt=nt: (jnp.minimum(i, nt - 1), 0))
        in_specs += [pl.BlockSpec((n, tm, wd), lambda i, nt=nt: (0, jnp.minimum(i, nt - 1), 0)) for wd in wds]
        in_specs += [row, row, row]
        out_specs += [row] * 4
        out_shape += [jax.ShapeDtypeStruct((rows, sum(wds)), F32)] * 4
        operands += [*parts, w, m, v]
    res = _pallas_call(
        body, name=name, grid=(steps,),
        in_specs=in_specs, out_specs=out_specs, out_shape=out_shape,
        compiler_params=pltpu.CompilerParams(
            dimension_semantics=("parallel" if min(tiles) == steps else "arbitrary",), vmem_limit_bytes=VMEM_MID),
    )(*operands)
    return [list(res[4 * k:4 * k + 4]) for k in range(len(items))]


_PK_LR, _PK_LI, _PK_GAINS, _PK_MISC, _PK_BR, _PK_BI, _PK_CR, _PK_CI, _PK_ROWS = 0, 1, 2, 3, 8, 24, 40, 56, 72
_PK_LDT_LANE, _PK_LOSS_LANE = D_MODEL + SSM_W, D_MODEL + SSM_W + LANES


def _pack_small(sg):
    names = ("lr", "li", "g_ffn", "g_fin", "dskip", "ldt", "loss", "br", "bi", "cr", "ci")

    def body(lr, li, gffn, gfin, dskip, ldt, loss, br, bi, cr, ci, o_ref):
        o_ref[...] = jnp.zeros_like(o_ref)
        o_ref[_PK_LR:_PK_LR + 1, :] = lr[...]
        o_ref[_PK_LI:_PK_LI + 1, :] = li[...]
        o_ref[_PK_GAINS:_PK_GAINS + 1, D_MODEL:] = gffn[...]
        o_ref[_PK_MISC:_PK_MISC + 1, :D_MODEL] = gfin[...]
        o_ref[_PK_MISC:_PK_MISC + 1, D_MODEL:D_MODEL + SSM_W] = dskip[...]
        o_ref[_PK_MISC:_PK_MISC + 1, _PK_LDT_LANE:_PK_LDT_LANE + LANES] = ldt[0:1, :]
        o_ref[_PK_MISC:_PK_MISC + 1, _PK_LOSS_LANE:_PK_LOSS_LANE + LANES] = loss[0:1, :]
        o_ref[_PK_BR:_PK_BR + SSM_CH, :] = br[...]
        o_ref[_PK_BI:_PK_BI + SSM_CH, :] = bi[...]
        o_ref[_PK_CR:_PK_CR + SSM_CH, :] = cr[...]
        o_ref[_PK_CI:_PK_CI + SSM_CH, :] = ci[...]

    return _pallas_call(body, name="pack_small", out_shape=jax.ShapeDtypeStruct((_PK_ROWS, N_STATE), F32))(
        *[sg[n] for n in names])


def _unpack_small(s, g_mix):
    unflat_b = unflat_c = lambda a: a.reshape(SSM_CH, SSM_GROUPS, SSM_STATE).transpose(1, 0, 2)[None]
    grads = {
        "norm_mix_g": g_mix, "norm_ffn_g": s[_PK_GAINS, D_MODEL:].reshape(1, D_MODEL),
        "norm_final_g": s[_PK_MISC, :D_MODEL].reshape(1, D_MODEL),
        "ssm_a_re": s[_PK_LR].reshape(1, SSM_GROUPS, SSM_STATE), "ssm_a_im": s[_PK_LI].reshape(1, SSM_GROUPS, SSM_STATE),
        "ssm_log_dt": s[_PK_MISC, _PK_LDT_LANE:_PK_LDT_LANE + SSM_GROUPS].reshape(1, SSM_GROUPS),
        "ssm_d": s[_PK_MISC, D_MODEL:D_MODEL + SSM_W].reshape(1, SSM_GROUPS, SSM_CH),
        "ssm_b_re": unflat_b(s[_PK_BR:_PK_BR + SSM_CH]), "ssm_b_im": unflat_b(s[_PK_BI:_PK_BI + SSM_CH]),
        "ssm_c_re": unflat_c(s[_PK_CR:_PK_CR + SSM_CH]), "ssm_c_im": unflat_c(s[_PK_CI:_PK_CI + SSM_CH]),
    }
    return s[_PK_MISC, _PK_LOSS_LANE], grads


def _stored(name, a):
    if name in ("ssm_b_re", "ssm_b_im"):
        return a.transpose(0, 1, 3, 2)
    return a.reshape(1, -1) if a.ndim == 1 else a


def _unstored(name, a, like):
    return a.transpose(0, 1, 3, 2) if name in ("ssm_b_re", "ssm_b_im") else a.reshape(like.shape)


def _adam_small(grads, wts, moms, vars_):
    n = len(SMALL_WEIGHTS)

    def body(*refs):
        ins, outs = refs[:4 * n], refs[4 * n:]
        for i in range(n):
            g, w, m, v = (ins[j * n + i][...] for j in range(4))
            outs[i][...], outs[n + i][...], outs[2 * n + i][...] = _adam_math(g, w, m, v)

    operands = [grads[k] if d is grads else _stored(k, d[k]) for d in (grads, wts, moms, vars_) for k in SMALL_WEIGHTS]
    shapes = [jax.ShapeDtypeStruct(_stored(k, wts[k]).shape, F32) for k in SMALL_WEIGHTS] * 3
    res = _pallas_call(body, name="adam_small", out_shape=shapes,
                         compiler_params=pltpu.CompilerParams(vmem_limit_bytes=VMEM_BIG))(*operands)
    out = {}
    for j, kind in enumerate(("delta", "new_m", "new_v")):
        for i, k in enumerate(SMALL_WEIGHTS):
            out[kind, k] = _unstored(k, res[j * n + i], wts[k])
    return out


def kernel(x, norm_mix_g, w_in, ssm_a_re, ssm_a_im, ssm_log_dt, ssm_b_re, ssm_b_im, ssm_c_re, ssm_c_im, ssm_d, w_glu, w_attn_out, w_out, norm_ffn_g, w_ffn_gate, w_ffn_up, w_ffn_down, norm_final_g, loss_target, m_norm_mix_g, m_w_in, m_ssm_a_re, m_ssm_a_im, m_ssm_log_dt, m_ssm_b_re, m_ssm_b_im, m_ssm_c_re, m_ssm_c_im, m_ssm_d, m_w_glu, m_w_attn_out, m_w_out, m_norm_ffn_g, m_w_ffn_gate, m_w_ffn_up, m_w_ffn_down, m_norm_final_g, v_norm_mix_g, v_w_in, v_ssm_a_re, v_ssm_a_im, v_ssm_log_dt, v_ssm_b_re, v_ssm_b_im, v_ssm_c_re, v_ssm_c_im, v_ssm_d, v_w_glu, v_w_attn_out, v_w_out, v_norm_ffn_g, v_w_ffn_gate, v_w_ffn_up, v_w_ffn_down, v_norm_final_g):
    args = dict(locals())
    wts = {n: args[n] for n in ALL_WEIGHTS}
    moms = {n: args["m_" + n] for n in ALL_WEIGHTS}
    vars_ = {n: args["v_" + n] for n in ALL_WEIGHTS}
    n_samples = x.shape[0]
    t = n_samples * SEQ

    shards = {n: (wts[n][0] if n in ROW_SHARDED else wts[n][0].T).astype(BF16) for n in BIG_WEIGHTS}
    w_in_t = _all_gather(shards["w_in"], "allgather_w_in").reshape(IN_W, D_MODEL)

    small = {n: wts[n] for n in SMALL_WEIGHTS}
    grad_x, recv, _, g_mix_part = _local_step(x.reshape(t, D_MODEL), loss_target.reshape(t, D_MODEL), {"w_in": w_in_t},
                                              small, shards)

    results = {}
    ffn_items = []
    for n in FFN_ADAM:
        w2, m2, v2 = wts[n][0], moms[n][0], vars_[n][0]
        if n in HALVED:
            ffn_items.append(([recv[f"{n}:{hf}"] for hf in range(2)], w2.T, m2.T, v2.T))
        else:
            ffn_items.append(([recv[n]], w2, m2, v2))
    for n, res in zip(FFN_ADAM, _adam_multi(ffn_items, "adam_w_rows", 32)):
        if n in HALVED:
            res = [a.T for a in res]
        for kind, a in zip(("grad", "delta", "new_m", "new_v"), res):
            results[kind, n] = a[None]
    for n in BIG_WEIGHTS:
        if n in FFN_ADAM:
            continue
        c, k = shards[n].shape
        w2, m2, v2 = wts[n][0], moms[n][0], vars_[n][0]
        if n in ROW_SHARDED:
            res = _adam([recv[n]], w2, m2, v2, "adam_" + n, c // 2)
        elif n in HALVED:
            res = _adam([recv[f"{n}:{hf}"] for hf in range(2)], w2.T, m2.T, v2.T, "adam_" + n, c // 2)
            res = [a.T for a in res]
        else:
            g_t = _sum_partials([recv[n]], "sum_" + n, c // 2)
            res = _adam([g_t.T[None]], w2, m2, v2, "adam_" + n, k // 2)
        for kind, a in zip(("grad", "delta", "new_m", "new_v"), res):
            results[kind, n] = a[None]

    g_mix_all = _all_gather(jnp.pad(g_mix_part, ((0, 7), (0, 0))), "allgather_g_mix")
    g_mix = _sum_partials([g_mix_all], "sum_g_mix", 8)[0:1]
    loss, sgrads = _unpack_small(_sum_partials([recv["small"]], "sum_small", _PK_ROWS), g_mix)
    for n in SMALL_WEIGHTS:
        results["grad", n] = _unstored(n, sgrads[n], wts[n])
    results.update(_adam_small(sgrads, wts, moms, vars_))
    outs = [loss, grad_x.reshape(x.shape)]
    for kind in ("grad", "delta", "new_m", "new_v"):
        outs += [results[kind, n] for n in ALL_WEIGHTS]
    return tuple(outs)
```

```python
import functools
import math

import jax
import jax.numpy as jnp
from jax import lax
from jax.experimental import pallas as pl
from jax.experimental.pallas import tpu as pltpu

F32 = jnp.float32
BF16 = jnp.bfloat16
MXU_DTYPE = jnp.bfloat16

N_DEV = 8
D_MODEL = 1024
SEQ = 2048
HEAD_DIM = 64
HEADS_PER_GROUP = 4
GROUP_W = HEADS_PER_GROUP * HEAD_DIM
DILATIONS = (1, 4, 16)
QKV_W = 3 * len(DILATIONS) * GROUP_W
Q_W = len(DILATIONS) * GROUP_W
ATT_BLOCK = 128
ROPE_DIM = 16
ROPE_THETA = 500000.0
SSM_W = 512
SSM_GROUPS = 32
SSM_CH = 16
SSM_STATE = 64
N_STATE = SSM_GROUPS * SSM_STATE
D_FF = 2816
IN_W = QKV_W + SSM_W + 2 * D_MODEL
RMS_EPS = 1e-6
NEG_INF = -1e30
LANES = 128

SCAN_SEG_PER_SAMPLE = 8
SCAN_LEN = SEQ // SCAN_SEG_PER_SAMPLE
SCAN_WC = 512
SCAN_NBLK = N_STATE // SCAN_WC
SCAN_CH = SSM_W // SCAN_NBLK
SCAN_CHUNK = 32

ADAM_LR = 0.001
ADAM_B1 = 0.9
ADAM_B2 = 0.999
ADAM_EPS = 1e-08
ADAM_WD = 0.01
ADAM_STEP = 10

VMEM_BIG = 48 * 1024 * 1024
VMEM_MID = 32 * 1024 * 1024

BIG_WEIGHTS = ("w_in", "w_glu", "w_attn_out", "w_out", "w_ffn_gate", "w_ffn_up", "w_ffn_down")
ROW_SHARDED = ("w_out", "w_ffn_down")
SMALL_WEIGHTS = ("norm_mix_g", "ssm_a_re", "ssm_a_im", "ssm_log_dt", "ssm_b_re", "ssm_b_im", "ssm_c_re", "ssm_c_im",
                 "ssm_d", "norm_ffn_g", "norm_final_g")
ALL_WEIGHTS = ("norm_mix_g", "w_in", "ssm_a_re", "ssm_a_im", "ssm_log_dt", "ssm_b_re", "ssm_b_im", "ssm_c_re", "ssm_c_im",
               "ssm_d", "w_glu", "w_attn_out", "w_out", "norm_ffn_g", "w_ffn_gate", "w_ffn_up", "w_ffn_down", "norm_final_g")


def _sigmoid(x):
    return 1.0 / (1.0 + jnp.exp(-x))


def _pallas_call(body, *, out_shape, **kw):
    single = not isinstance(out_shape, (list, tuple))
    shapes = [pltpu.HBM(s.shape, s.dtype) for s in ([out_shape] if single else out_shape)]
    call = pl.pallas_call(body, out_shape=shapes[0] if single else shapes, **kw)
    return lambda *operands: call(*[pltpu.with_memory_space_constraint(o, pltpu.HBM) for o in operands])


class _Comm:
    def __init__(self, ins, out_shapes, n_sem, n_local, start, finish):
        self.ins, self.out_shapes, self.n_sem, self.n_local = ins, out_shapes, n_sem, n_local
        self.start, self.finish = start, finish


def _mm(a, b, mode, name, tm, tn, out_dtype=F32, add=None, vmem=VMEM_BIG, comm=None, cols=None):
    if mode == "nn":
        (m, k), (_, n) = a.shape, b.shape
        a_spec = pl.BlockSpec((tm, k), lambda i, j: (i, 0))
        b_spec = pl.BlockSpec((k, tn), lambda i, j: (0, j))
        dims = (((1,), (0,)), ((), ()))
    elif mode == "nt":
        (m, k), (n, _) = a.shape, b.shape
        a_spec = pl.BlockSpec((tm, k), lambda i, j: (i, 0))
        b_spec = pl.BlockSpec((tn, k), lambda i, j: (j, 0))
        dims = (((1,), (1,)), ((), ()))
    else:
        (k, m), (_, n) = a.shape, b.shape
        first, n = cols if cols else (0, n)
        a_spec = pl.BlockSpec((k, tm), lambda i, j: (0, i))
        b_spec = pl.BlockSpec((k, tn), lambda i, j: (0, j + first // tn))
        dims = (((0,), (0,)), ((), ()))
    assert m % tm == 0 and n % tn == 0, (name, m, n, tm, tn)
    o_spec = pl.BlockSpec((tm, tn), lambda i, j: (i, j))
    has_add = add is not None

    def body(*refs):
        a_ref, b_ref, o_ref = refs[0], refs[1], refs[-1]
        acc = lax.dot_general(a_ref[...].astype(MXU_DTYPE), b_ref[...].astype(MXU_DTYPE), dims,
                              preferred_element_type=F32)
        if has_add:
            acc = acc + refs[2][...]
        o_ref[...] = acc.astype(out_dtype)

    ins = [a, b] + ([add] if has_add else [])
    in_specs = [a_spec, b_spec] + ([o_spec] if has_add else [])
    return _grid_call(body, name, (m // tm, n // tn), ins, in_specs, [o_spec],
                      [jax.ShapeDtypeStruct((m, n), out_dtype)], vmem, comm)


def _grid_call(body, name, grid, ins, in_specs, out_specs, out_shapes, vmem, comm=None, sequential=False, scratch=()):
    if comm is None:
        single = len(out_shapes) == 1
        semantics = ("arbitrary", "arbitrary") if sequential else ("parallel", "parallel")
        return _pallas_call(
            body, name=name, grid=grid, in_specs=in_specs, out_specs=out_specs[0] if single else out_specs,
            out_shape=out_shapes[0] if single else out_shapes, scratch_shapes=list(scratch),
            compiler_params=pltpu.CompilerParams(dimension_semantics=semantics, vmem_limit_bytes=vmem),
        )(*ins)
    n_in, n_out, n_cin, n_cout = len(ins), len(out_shapes), len(comm.ins), len(comm.out_shapes)
    n_io = n_in + n_cin + n_out + n_cout

    def carrying(*refs):
        own = refs[:n_in] + refs[n_in + n_cin:n_in + n_cin + n_out] + refs[n_io:len(refs) - 3]
        c_args = (refs[n_in:n_in + n_cin], refs[n_in + n_cin + n_out:n_io], *refs[-3:])

        @pl.when((pl.program_id(0) == 0) & (pl.program_id(1) == 0))
        def _():
            comm.start(*c_args)

        body(*own)

        @pl.when((pl.program_id(0) == grid[0] - 1) & (pl.program_id(1) == grid[1] - 1))
        def _():
            comm.finish(*c_args)

    hbm = pl.BlockSpec(memory_space=pl.ANY)
    return _pallas_call(
        carrying, name=name, grid=grid, in_specs=list(in_specs) + [hbm] * n_cin,
        out_specs=list(out_specs) + [hbm] * n_cout, out_shape=list(out_shapes) + list(comm.out_shapes),
        scratch_shapes=list(scratch) + [pltpu.SemaphoreType.DMA((comm.n_sem,)), pltpu.SemaphoreType.DMA((comm.n_sem,)),
                                        pltpu.SemaphoreType.DMA((comm.n_local,))],
        compiler_params=pltpu.CompilerParams(dimension_semantics=("arbitrary", "arbitrary"), vmem_limit_bytes=vmem),
    )(*ins, *comm.ins)


def _rows(body, name, n_rows, tm, ins, outs, vmem=VMEM_MID, scratch=()):
    assert n_rows % tm == 0
    arrays, in_specs = [], []
    for kind, arr in ins:
        arrays.append(arr)
        if kind == "row":
            assert n_rows % arr.shape[0] == 0, (name, arr.shape)
            in_specs.append(pl.BlockSpec((tm * arr.shape[0] // n_rows, arr.shape[1]), lambda i: (i, 0)))
        elif kind == "tab":
            nblk = arr.shape[0] // tm
            in_specs.append(pl.BlockSpec((tm, arr.shape[1]), lambda i, nblk=nblk: (i % nblk, 0)))
        else:
            in_specs.append(pl.BlockSpec(arr.shape, lambda i, nd=arr.ndim: (0,) * nd))
    out_specs, out_shape = [], []
    for kind, shp, dt in outs:
        if kind == "row":
            out_specs.append(pl.BlockSpec((tm, shp), lambda i: (i, 0)))
            out_shape.append(jax.ShapeDtypeStruct((n_rows, shp), dt))
        elif kind == "dil":
            d, wd = shp
            out_specs.append(pl.BlockSpec((tm // d, d * wd), lambda i: (i, 0)))
            out_shape.append(jax.ShapeDtypeStruct((n_rows // d, d * wd), dt))
        else:
            out_specs.append(pl.BlockSpec(shp, lambda i, nd=len(shp): (0,) * nd))
            out_shape.append(jax.ShapeDtypeStruct(shp, dt))
    res = _pallas_call(
        body, name=name, grid=(n_rows // tm,), in_specs=in_specs, out_specs=out_specs, out_shape=out_shape,
        scratch_shapes=list(scratch),
        compiler_params=pltpu.CompilerParams(dimension_semantics=("arbitrary",), vmem_limit_bytes=vmem),
    )(*arrays)
    return res


def _gather_residue(stage, ch, r, d, n):
    return stage[ch, pl.ds(r, n, stride=d), :] if d > 1 else stage[ch]


def _scatter_residue(stage, ch, r, d, n, val):
    if d > 1:
        stage[ch, pl.ds(r, n, stride=d), :] = val
    else:
        stage[ch] = val


def _lane_chunk(ch):
    return slice(ch * LANES, (ch + 1) * LANES)


def _rope_tables():
    half = ROPE_DIM // 2
    inv = jnp.power(jnp.float32(ROPE_THETA), -jnp.arange(half, dtype=F32) * 2.0 / ROPE_DIM)
    ang = jnp.arange(SEQ, dtype=F32)[:, None] * inv[None, :]
    lane = jnp.arange(LANES) % HEAD_DIM
    cosl = jnp.cos(ang)[:, lane % half]
    sinl = jnp.sin(ang)[:, lane % half]
    tab_c = jnp.where(lane < ROPE_DIM, cosl, 1.0)
    tab_lo = jnp.where(lane < half, -sinl, 0.0)
    tab_hi = jnp.where((lane >= half) & (lane < ROPE_DIM), sinl, 0.0)
    return tab_c.astype(F32), tab_lo.astype(F32), tab_hi.astype(F32)


def _rope_apply(t, tc, tlo, thi):
    half = ROPE_DIM // 2
    return t * tc + pltpu.roll(t, LANES - half, 1) * tlo + pltpu.roll(t, half, 1) * thi


def _rope_transpose(dt, tc, tlo, thi):
    half = ROPE_DIM // 2
    return dt * tc + pltpu.roll(dt * tlo, half, 1) + pltpu.roll(dt * thi, LANES - half, 1)


def _pack_dproj(dqs, dks, dvs, du, dgpre, tabs):
    tm = 256

    def body(*refs):
        dq_refs, dk_refs, dv_refs = refs[0:3], refs[3:6], refs[6:9]
        du_ref, dg_ref, tc_ref, tlo_ref, thi_ref, o_ref, stage = refs[9:16]
        n_ch = QKV_W // LANES
        halves = GROUP_W // LANES
        for grp, d in enumerate(DILATIONS):
            for which, src in enumerate((dq_refs[grp], dk_refs[grp], dv_refs[grp])):
                for res in range(d):
                    for half in range(halves):
                        _scatter_residue(stage, which * (n_ch // 3) + grp * halves + half, res, d, tm // d,
                                         src[:, _lane_chunk(res * halves + half)])
        tc, tlo, thi = tc_ref[...], tlo_ref[...], thi_ref[...]
        for ch in range(n_ch):
            piece = stage[ch]
            o_ref[:, _lane_chunk(ch)] = (_rope_transpose(piece, tc, tlo, thi) if ch < 2 * n_ch // 3 else piece).astype(BF16)
        o_ref[:, QKV_W:QKV_W + SSM_W] = du_ref[...].astype(BF16)
        o_ref[:, QKV_W + SSM_W:] = dg_ref[...].astype(BF16)

    t = du.shape[0]
    ins = [("row", a) for a in (*dqs, *dks, *dvs, du, dgpre)] + [("tab", tb) for tb in tabs]
    return _rows(body, "pack_dproj", t, tm, ins, [("row", IN_W, BF16)],
                 scratch=[pltpu.VMEM((QKV_W // LANES, tm, LANES), F32)])[0]


def _merge_groups(o_refs, l_refs, a_ref, lt_ref, nat, tm):
    halves = GROUP_W // LANES
    for grp, d in enumerate(DILATIONS[1:], start=1):
        for j, src in enumerate((o_refs[grp], l_refs[grp])):
            for res in range(d):
                for half in range(halves):
                    _scatter_residue(nat, (grp - 1) * 4 + j * 2 + half, res, d, tm // d,
                                     src[:, _lane_chunk(res * halves + half)])
    for half in range(halves):
        sl = _lane_chunk(half)
        la, lb, lc = l_refs[0][:, sl], nat[2 + half], nat[6 + half]
        m = jnp.maximum(jnp.maximum(la, lb), lc)
        ea, eb, ec = jnp.exp(la - m), jnp.exp(lb - m), jnp.exp(lc - m)
        ssum = ea + eb + ec
        a_ref[:, sl] = (ea / ssum) * o_refs[0][:, sl] + (eb / ssum) * nat[half] + (ec / ssum) * nat[4 + half]
        lt_ref[:, sl] = m + jnp.log(ssum)


def _head_sum_matrix():
    r = jnp.arange(GROUP_W) // HEAD_DIM
    return (r[:, None] == r[None, :]).astype(F32)


def _attention_cotangents(da, attn, lt, ones, rd_ref, dil, stage, tm):
    halves = GROUP_W // LANES
    rd = jnp.dot(da * attn, ones, preferred_element_type=F32, precision=lax.Precision.HIGHEST)
    rd_ref[...] = rd
    for half in range(halves):
        for j, val in enumerate((da, lt, rd)):
            stage[2 * j + half] = val[:, _lane_chunk(half)]
    for grp, d in enumerate(DILATIONS[1:], start=1):
        for j in range(3):
            for res in range(d):
                for half in range(halves):
                    dil[3 * (grp - 1) + j][:, _lane_chunk(res * halves + half)] = _gather_residue(
                        stage, 2 * j + half, res, d, tm // d)


_GELU_C = math.sqrt(2.0 / math.pi)


def _head_masks():
    lane = lax.broadcasted_iota(jnp.int32, (1, GROUP_W), 1)
    return [(lane // HEAD_DIM) == h for h in range(HEADS_PER_GROUP)]


def _stack_heads(blk, masks, fill=0.0):
    return jnp.concatenate([jnp.where(mk, blk, jnp.full_like(blk, fill)) for mk in masks], axis=0)


def _unstack_heads(stacked, masks):
    rows = stacked.shape[0] // len(masks)
    out = stacked[:rows]
    for h in range(1, len(masks)):
        out = jnp.where(masks[h], stacked[h * rows:(h + 1) * rows], out)
    return out


def _band_mask(first):
    nk = ATT_BLOCK if first else 2 * ATT_BLOCK
    qi = lax.broadcasted_iota(jnp.int32, (ATT_BLOCK, nk), 0)
    ki = lax.broadcasted_iota(jnp.int32, (ATT_BLOCK, nk), 1)
    dist = qi - ki + (0 if first else ATT_BLOCK)
    return (dist >= 0) & (dist <= ATT_BLOCK)


_NT = (((1,), (1,)), ((), ()))
_TN = (((0,), (0,)), ((), ()))


def _residues_per_step(d):
    return 4 if d >= 16 else 1


def _attn_fwd(q, k, v, group, n_samples, comm=None):
    d = DILATIONS[group]
    length = SEQ // d
    nb = length // ATT_BLOCK

    rps = _residues_per_step(d)

    def body(q_ref, k_ref, v_ref, o_ref, l_ref):
        for rl in range(rps):
            residue(q_ref, k_ref, v_ref, o_ref, l_ref, slice(rl * GROUP_W, (rl + 1) * GROUP_W))

    def residue(q_ref, k_ref, v_ref, o_ref, l_ref, cols):
        masks = _head_masks()

        def block(qs, ks, first):
            nk = ATT_BLOCK if first else 2 * ATT_BLOCK
            qb = q_ref[0, pl.ds(qs, ATT_BLOCK), cols]
            kc = k_ref[0, pl.ds(ks, nk), cols]
            vc = v_ref[0, pl.ds(ks, nk), cols]
            q4 = _stack_heads(qb, masks)
            valid = jnp.tile(_band_mask(first), (HEADS_PER_GROUP, 1))
            s = lax.dot_general(q4, kc, _NT, preferred_element_type=F32) * (HEAD_DIM ** -0.5)
            s = jnp.where(valid, s, NEG_INF)
            m = jnp.max(s, axis=-1, keepdims=True)
            p = jnp.exp(s - m)
            l = jnp.sum(p, axis=-1, keepdims=True)
            o4 = jnp.dot(p.astype(MXU_DTYPE), vc, preferred_element_type=F32) / l
            lse4 = jnp.broadcast_to(m + jnp.log(l), o4.shape)
            o_ref[0, pl.ds(qs, ATT_BLOCK), cols] = _unstack_heads(o4, masks)
            l_ref[0, pl.ds(qs, ATT_BLOCK), cols] = _unstack_heads(lse4, masks)

        block(0, 0, True)
        if nb > 1:
            def loop(n, carry):
                block(pl.multiple_of(n * ATT_BLOCK, ATT_BLOCK), pl.multiple_of((n - 1) * ATT_BLOCK, ATT_BLOCK), False)
                return carry

            lax.fori_loop(1, nb, loop, 0)

    per_sample = lambda a: a.reshape(n_samples, length, d * GROUP_W)
    spec = pl.BlockSpec((1, length, rps * GROUP_W), lambda b, r: (b, 0, r))
    shp = jax.ShapeDtypeStruct((n_samples, length, d * GROUP_W), F32)
    o, lse, *carried = _grid_call(body, f"attn_fwd_g{group}", (n_samples, d // rps), [per_sample(a) for a in (q, k, v)],
                                  [spec] * 3, [spec] * 2, [shp, shp], VMEM_MID, comm)
    flat = lambda a: a.reshape(n_samples * length, d * GROUP_W)
    return flat(o), flat(lse), carried


def _attn_bwd(q, k, v, dattn, lse_tot, rowdot, group, n_samples, comm=None):
    d = DILATIONS[group]
    length = SEQ // d
    nb = length // ATT_BLOCK

    rps = _residues_per_step(d)

    def body(q_ref, k_ref, v_ref, da_ref, lt_ref, rd_ref, dq_ref, dk_ref, dv_ref):
        dk_ref[...] = jnp.zeros_like(dk_ref)
        dv_ref[...] = jnp.zeros_like(dv_ref)
        for rl in range(rps):
            residue(q_ref, k_ref, v_ref, da_ref, lt_ref, rd_ref, dq_ref, dk_ref, dv_ref,
                    slice(rl * GROUP_W, (rl + 1) * GROUP_W))

    def residue(q_ref, k_ref, v_ref, da_ref, lt_ref, rd_ref, dq_ref, dk_ref, dv_ref, cols):
        masks = _head_masks()

        def block(qs, ks, first):
            nk = ATT_BLOCK if first else 2 * ATT_BLOCK
            qb = q_ref[0, pl.ds(qs, ATT_BLOCK), cols]
            kc = k_ref[0, pl.ds(ks, nk), cols]
            vc = v_ref[0, pl.ds(ks, nk), cols]
            da = da_ref[0, pl.ds(qs, ATT_BLOCK), cols]
            lt = lt_ref[0, pl.ds(qs, ATT_BLOCK), cols]
            rd = rd_ref[0, pl.ds(qs, ATT_BLOCK), cols]
            q4 = _stack_heads(qb, masks)
            da4 = _stack_heads(da, masks).astype(MXU_DTYPE)
            lt4 = jnp.max(_stack_heads(lt, masks, -jnp.inf), axis=-1, keepdims=True)
            rd4 = jnp.max(_stack_heads(rd, masks, -jnp.inf), axis=-1, keepdims=True)
            valid = jnp.tile(_band_mask(first), (HEADS_PER_GROUP, 1))
            s = lax.dot_general(q4, kc, _NT, preferred_element_type=F32) * (HEAD_DIM ** -0.5)
            s = jnp.where(valid, s, NEG_INF)
            p = jnp.exp(s - lt4)
            dp = lax.dot_general(da4, vc, _NT, preferred_element_type=F32)
            ds = (p * (dp - rd4) * (HEAD_DIM ** -0.5)).astype(MXU_DTYPE)
            dq_ref[0, pl.ds(qs, ATT_BLOCK), cols] = _unstack_heads(jnp.dot(ds, kc, preferred_element_type=F32), masks)
            dk_ref[0, pl.ds(ks, nk), cols] += lax.dot_general(ds, q4, _TN, preferred_element_type=F32)
            dv_ref[0, pl.ds(ks, nk), cols] += lax.dot_general(p.astype(MXU_DTYPE), da4, _TN, preferred_element_type=F32)

        block(0, 0, True)
        if nb > 1:
            def loop(n, carry):
                block(pl.multiple_of(n * ATT_BLOCK, ATT_BLOCK), pl.multiple_of((n - 1) * ATT_BLOCK, ATT_BLOCK), False)
                return carry

            lax.fori_loop(1, nb, loop, 0)

    per_sample = lambda a: a.reshape(n_samples, length, d * GROUP_W)
    spec = pl.BlockSpec((1, length, rps * GROUP_W), lambda b, r: (b, 0, r))
    shp = jax.ShapeDtypeStruct((n_samples, length, d * GROUP_W), F32)
    dq, dk, dv, *carried = _grid_call(
        body, f"attn_bwd_g{group}", (n_samples, d // rps), [per_sample(a) for a in (q, k, v, dattn, lse_tot, rowdot)],
        [spec] * 6, [spec] * 3, [shp, shp, shp], VMEM_MID, comm)
    flat = lambda a: a.reshape(n_samples * length, d * GROUP_W)
    return flat(dq), flat(dk), flat(dv), carried


def _disc(lr, li, ldt, br, bi):
    dt = jnp.exp(ldt)
    mag = jnp.exp(lr * dt)
    ab_re, ab_im = mag * jnp.cos(li * dt), mag * jnp.sin(li * dt)
    den = lr * lr + li * li
    nr, ni = ab_re - 1.0, ab_im
    f_re = (nr * lr + ni * li) / den
    f_im = (ni * lr - nr * li) / den
    return ab_re, ab_im, f_re * br - f_im * bi, f_re * bi + f_im * br


def _state_mask():
    row_g = lax.broadcasted_iota(jnp.int32, (SCAN_CH, SCAN_WC), 0) // SSM_CH
    col_g = lax.broadcasted_iota(jnp.int32, (SCAN_CH, SCAN_WC), 1) // SSM_STATE
    return row_g == col_g


def _ssm_disc(lr, li, ldt, br, bi, cr, ci):
    w = SCAN_WC

    def body(lr_ref, li_ref, ldt_ref, br_ref, bi_ref, cr_ref, ci_ref, a_ref, bb_ref, c_ref):
        ar, ai, bbr, bbi = _disc(lr_ref[...], li_ref[...], ldt_ref[...], br_ref[...], bi_ref[...])
        crv, civ = cr_ref[...], ci_ref[...]
        mask = _state_mask()
        for cb in range(SCAN_NBLK):
            sl = slice(cb * w, (cb + 1) * w)
            rows = slice(cb * SCAN_CH, (cb + 1) * SCAN_CH)
            dense = lambda comp: jnp.where(mask, jnp.tile(comp[:, sl], (SCAN_CH // SSM_CH, 1)), 0.0)
            a_ref[:, 2 * cb * w:(2 * cb + 1) * w] = ar[:, sl]
            a_ref[:, (2 * cb + 1) * w:(2 * cb + 2) * w] = ai[:, sl]
            bb_ref[rows, :w] = dense(bbr).astype(MXU_DTYPE)
            bb_ref[rows, w:] = dense(bbi).astype(MXU_DTYPE)
            c_ref[rows, :w] = dense(crv).astype(MXU_DTYPE)
            c_ref[rows, w:] = (-dense(civ)).astype(MXU_DTYPE)

    return _pallas_call(
        body, name="ssm_disc",
        out_shape=[jax.ShapeDtypeStruct((1, 2 * N_STATE), F32), jax.ShapeDtypeStruct((SSM_W, 2 * w), MXU_DTYPE),
                   jax.ShapeDtypeStruct((SSM_W, 2 * w), MXU_DTYPE)],
        compiler_params=pltpu.CompilerParams(vmem_limit_bytes=VMEM_MID),
    )(lr, li, ldt, br, bi, cr, ci)


def _group_indicator():
    s = jnp.arange(N_STATE) // SSM_STATE
    return (s[:, None] == jnp.arange(LANES)[None, :]).astype(F32)


def _ssm_param_bwd(lr, li, ldt, br, bi, da_cat, dbb_full, dc_full):
    w = SCAN_WC

    def body(lr_ref, li_ref, ldt_ref, br_ref, bi_ref, da_ref, dbb_ref, dc_ref, ind_ref,
             glr_ref, gli_ref, gldt_ref, gbr_ref, gbi_ref, gcr_ref, gci_ref):
        mask = _state_mask()

        def diag_parts(ref):
            res = ([], [])
            for cb in range(SCAN_NBLK):
                for part in range(2):
                    blk = ref[cb * SCAN_CH:(cb + 1) * SCAN_CH, part * w:(part + 1) * w]
                    res[part].append(jnp.sum(jnp.where(mask, blk, 0.0).reshape(SCAN_CH // SSM_CH, SSM_CH, w), axis=0))
            return jnp.concatenate(res[0], axis=1), jnp.concatenate(res[1], axis=1)

        dar = jnp.concatenate([da_ref[:, 2 * cb * w:(2 * cb + 1) * w] for cb in range(SCAN_NBLK)], axis=1)
        dai = jnp.concatenate([da_ref[:, (2 * cb + 1) * w:(2 * cb + 2) * w] for cb in range(SCAN_NBLK)], axis=1)
        dbbr, dbbi = diag_parts(dbb_ref)
        dcr, dci_neg = diag_parts(dc_ref)
        gcr_ref[...] = dcr
        gci_ref[...] = -dci_neg
        _, vjp = jax.vjp(_disc, lr_ref[...], li_ref[...], ldt_ref[...], br_ref[...], bi_ref[...])
        glr, gli, gldt, gbr, gbi = vjp((dar, dai, dbbr, dbbi))
        glr_ref[...] = glr
        gli_ref[...] = gli
        gldt_ref[...] = jnp.dot(jnp.broadcast_to(gldt, (8, N_STATE)), ind_ref[...], preferred_element_type=F32,
                                precision=lax.Precision.HIGHEST)
        gbr_ref[...] = gbr
        gbi_ref[...] = gbi

    v1 = jax.ShapeDtypeStruct((1, N_STATE), F32)
    v16 = jax.ShapeDtypeStruct((SSM_CH, N_STATE), F32)
    vdt = jax.ShapeDtypeStruct((8, LANES), F32)
    return _pallas_call(
        body, name="ssm_param_bwd", out_shape=[v1, v1, vdt, v16, v16, v16, v16],
        compiler_params=pltpu.CompilerParams(vmem_limit_bytes=VMEM_BIG),
    )(lr, li, ldt, br, bi, da_cat, dbb_full, dc_full, _group_indicator())


def _cmul(ar, ai, br, bi):
    return ar * br - ai * bi, ar * bi + ai * br


def _gelu_tanh(y):
    return jnp.tanh(_GELU_C * (y + 0.044715 * (y * y * y)))


def _segment_carry(er, ei, ar, ai, n_rows, reverse):
    qr, qi = ar, ai
    for _ in range(int(math.log2(SCAN_LEN))):
        qr, qi = _cmul(qr, qi, qr, qi)
    seg = lax.broadcasted_iota(jnp.int32, er.shape, 0) % SCAN_SEG_PER_SAMPLE
    shift = 1
    while shift < SCAN_SEG_PER_SAMPLE:
        keep = (seg < SCAN_SEG_PER_SAMPLE - shift) if reverse else (seg >= shift)
        amount = n_rows - shift if reverse else shift
        sr = jnp.where(keep, pltpu.roll(er, amount, 0), 0.0)
        si = jnp.where(keep, pltpu.roll(ei, amount, 0), 0.0)
        if reverse:
            er, ei = er + qr * sr + qi * si, ei + qr * si - qi * sr
        else:
            er, ei = er + qr * sr - qi * si, ei + qr * si + qi * sr
        qr, qi = _cmul(qr, qi, qr, qi)
        shift *= 2
    keep = (seg < SCAN_SEG_PER_SAMPLE - 1) if reverse else (seg >= 1)
    amount = n_rows - 1 if reverse else 1
    return jnp.where(keep, pltpu.roll(er, amount, 0), 0.0), jnp.where(keep, pltpu.roll(ei, amount, 0), 0.0)


def _ssm_fwd(u_perm, a_cat, bbc, cc, dskip, n_rows):
    t = u_perm.shape[0]
    w = SCAN_WC
    rows_c = SCAN_CHUNK * n_rows
    n_chunks = t // rows_c

    assert n_chunks % 2 == 0

    def body(u_ref, a_ref, bb_ref, c_ref, d_ref, yt_ref, yg_ref, ein_ref, bu_all, st_a, st_b, xs_a, xs_b):
        ar = jnp.broadcast_to(a_ref[:, :w], (n_rows, w))
        ai = jnp.broadcast_to(a_ref[:, w:], (n_rows, w))
        start = lambda ch: pl.multiple_of(ch * rows_c, rows_c)

        def project(ch, stage):
            res = jnp.dot(u_ref[pl.ds(start(ch), rows_c), :].astype(MXU_DTYPE), bb_ref[...], preferred_element_type=F32)
            stage[...] = res
            bu_all[pl.ds(start(ch), rows_c), :] = res

        def steps(src, r0, carry, xs=None):
            for i in range(SCAN_CHUNK):
                blk = src[pl.ds(r0 + i * n_rows, n_rows), :]
                carry = (ar * carry[0] - ai * carry[1] + blk[:, :w], ar * carry[1] + ai * carry[0] + blk[:, w:])
                if xs is not None:
                    xs[i * n_rows:(i + 1) * n_rows, :w] = carry[0]
                    xs[i * n_rows:(i + 1) * n_rows, w:] = carry[1]
            return carry

        def emit(xs, ch):
            y = lax.dot_general(xs[...].astype(MXU_DTYPE), c_ref[...], _NT, preferred_element_type=F32)
            yt = y + d_ref[...] * u_ref[pl.ds(start(ch), rows_c), :]
            yt_ref[pl.ds(start(ch), rows_c), :] = yt
            yg_ref[pl.ds(start(ch), rows_c), :] = (0.5 * yt * (1.0 + _gelu_tanh(yt))).astype(BF16)

        project(0, st_a)

        def pair1(p, carry):
            project(2 * p + 1, st_b)
            carry = steps(st_a, 0, carry)
            project(jnp.minimum(2 * p + 2, n_chunks - 1), st_a)
            return steps(st_b, 0, carry)

        zero = jnp.zeros((n_rows, w), F32)
        er, ei = lax.fori_loop(0, n_chunks // 2, pair1, (zero, zero))
        cr, ci = _segment_carry(er, ei, ar, ai, n_rows, False)
        ein_ref[:, :w] = cr
        ein_ref[:, w:] = ci

        xs_b[...] = jnp.zeros_like(xs_b)

        def pair2(p, carry):
            emit(xs_b, jnp.maximum(2 * p - 1, 0))
            carry = steps(bu_all, start(2 * p), carry, xs_a)
            emit(xs_a, 2 * p)
            return steps(bu_all, start(2 * p + 1), carry, xs_b)

        lax.fori_loop(0, n_chunks // 2, pair2, (cr, ci))
        emit(xs_b, n_chunks - 1)

    col = lambda width: pl.BlockSpec((t, width), lambda c: (0, c))
    wgt = pl.BlockSpec((SCAN_CH, 2 * w), lambda c: (c, 0))
    return _pallas_call(
        body, name="ssm_fwd", grid=(SCAN_NBLK,),
        in_specs=[col(SCAN_CH), pl.BlockSpec((1, 2 * w), lambda c: (0, c)), wgt, wgt,
                  pl.BlockSpec((1, SCAN_CH), lambda c: (0, c))],
        out_specs=[col(SCAN_CH), col(SCAN_CH), pl.BlockSpec((n_rows, 2 * w), lambda c: (0, c))],
        out_shape=[jax.ShapeDtypeStruct((t, SSM_W), F32), jax.ShapeDtypeStruct((t, SSM_W), BF16),
                   jax.ShapeDtypeStruct((n_rows, 2 * N_STATE), F32)],
        scratch_shapes=[pltpu.VMEM((t, 2 * w), F32)] + [pltpu.VMEM((rows_c, 2 * w), F32)] * 4,
        compiler_params=pltpu.CompilerParams(dimension_semantics=("parallel",), vmem_limit_bytes=VMEM_BIG),
    )(u_perm, a_cat, bbc, cc, dskip)


def _ssm_bwd(u_perm, dyg, ytot, dskip, a_cat, bbc, cc, ein, n_rows, comm=None):
    t = u_perm.shape[0]
    w = SCAN_WC
    rows_c = SCAN_CHUNK * n_rows
    n_chunks = t // rows_c

    assert n_chunks % 2 == 0
    last = n_chunks - 1

    def body(u_ref, dyg_ref, yt_ref, dk_ref, a_ref, bb_ref, c_ref, ein_ref, du_ref, gd_ref, da_ref, dbb_ref, dc_ref,
             xs_all, dy_s, st_a, st_b, buf_a, buf_b):
        ar = jnp.broadcast_to(a_ref[:, :w], (n_rows, w))
        ai = jnp.broadcast_to(a_ref[:, w:], (n_rows, w))
        zero = jnp.zeros((n_rows, w), F32)
        start = lambda ch: pl.multiple_of(ch * rows_c, rows_c)
        dbb_ref[...] = jnp.zeros_like(dbb_ref)
        dc_ref[...] = jnp.zeros_like(dc_ref)
        da_ref[...] = jnp.zeros_like(da_ref)

        yt = yt_ref[...]
        th = _gelu_tanh(yt)
        dgelu = 0.5 * (1.0 + th) + 0.5 * yt * (1.0 - th * th) * _GELU_C * (1.0 + 3.0 * 0.044715 * yt * yt)
        dy_all = dyg_ref[...] * dgelu
        dy_s[...] = dy_all
        gd_ref[...] = jnp.sum(dy_all * u_ref[...], axis=0, keepdims=True)
        dy_chunk = lambda ch: dy_s[pl.ds(start(ch), rows_c), :].astype(MXU_DTYPE)

        xs_all[0:n_rows, :] = ein_ref[...]

        def project(ch, stage):
            stage[...] = jnp.dot(u_ref[pl.ds(start(ch), rows_c), :].astype(MXU_DTYPE), bb_ref[...],
                                 preferred_element_type=F32)

        def fwd_steps(stage, ch, carry, xs):
            for i in range(SCAN_CHUNK):
                blk = stage[i * n_rows:(i + 1) * n_rows, :]
                carry = (ar * carry[0] - ai * carry[1] + blk[:, :w], ar * carry[1] + ai * carry[0] + blk[:, w:])
                for half, val in enumerate(carry):
                    xs[i * n_rows:(i + 1) * n_rows, half * w:(half + 1) * w] = val
                    xs_all[pl.ds(start(ch) + (i + 1) * n_rows, n_rows), half * w:(half + 1) * w] = val
            return carry

        def add_dc(xs, ch):
            dc_ref[...] += lax.dot_general(dy_chunk(ch), xs[...].astype(MXU_DTYPE), _TN, preferred_element_type=F32)

        project(0, st_a)

        def fwd_pair(p, carry):
            project(2 * p + 1, st_b)
            carry = fwd_steps(st_a, 2 * p, carry, buf_a)
            add_dc(buf_a, 2 * p)
            project(jnp.minimum(2 * p + 2, last), st_a)
            carry = fwd_steps(st_b, 2 * p + 1, carry, buf_b)
            add_dc(buf_b, 2 * p + 1)
            return carry

        lax.fori_loop(0, n_chunks // 2, fwd_pair, (ein_ref[:, :w], ein_ref[:, w:]))

        def project_dx(ch, stage):
            stage[...] = jnp.dot(dy_chunk(ch), c_ref[...], preferred_element_type=F32)

        def back_steps(stage, carry, g_buf=None):
            for i in reversed(range(SCAN_CHUNK)):
                blk = stage[i * n_rows:(i + 1) * n_rows, :]
                carry = (blk[:, :w] + ar * carry[0] + ai * carry[1], blk[:, w:] + ar * carry[1] - ai * carry[0])
                if g_buf is not None:
                    g_buf[i * n_rows:(i + 1) * n_rows, :w] = carry[0]
                    g_buf[i * n_rows:(i + 1) * n_rows, w:] = carry[1]
            return carry

        def first_pair(p, carry):
            project_dx(last - 2 * p - 1, st_b)
            carry = back_steps(st_a, carry)
            project_dx(jnp.maximum(last - 2 * p - 2, 0), st_a)
            return back_steps(st_b, carry)

        project_dx(last, st_a)
        sr, si = lax.fori_loop(0, n_chunks // 2, first_pair, (zero, zero))
        gr0, gi0 = _segment_carry(sr, si, ar, ai, n_rows, True)

        def post(g_buf, ch):
            g = g_buf[...]
            xp = xs_all[pl.ds(start(ch), rows_c), :]
            da_ref[:, :w] += jnp.sum(g[:, :w] * xp[:, :w] + g[:, w:] * xp[:, w:], axis=0, keepdims=True)
            da_ref[:, w:] += jnp.sum(g[:, w:] * xp[:, :w] - g[:, :w] * xp[:, w:], axis=0, keepdims=True)
            gb = g.astype(MXU_DTYPE)
            du_ref[pl.ds(start(ch), rows_c), :] = (lax.dot_general(gb, bb_ref[...], _NT, preferred_element_type=F32)
                                                   + dy_s[pl.ds(start(ch), rows_c), :] * dk_ref[...])
            dbb_ref[...] += lax.dot_general(u_ref[pl.ds(start(ch), rows_c), :].astype(MXU_DTYPE), gb, _TN,
                                            preferred_element_type=F32)

        def second_pair(p, carry):
            c1 = last - 2 * p
            project_dx(c1 - 1, st_b)
            post(buf_b, jnp.minimum(c1 + 1, last))
            carry = back_steps(st_a, carry, buf_a)
            project_dx(jnp.maximum(c1 - 2, 0), st_a)
            post(buf_a, c1)
            return back_steps(st_b, carry, buf_b)

        project_dx(last, st_a)
        buf_b[...] = jnp.zeros_like(buf_b)
        lax.fori_loop(0, n_chunks // 2, second_pair, (gr0, gi0))
        post(buf_b, 0)

    col = lambda width: pl.BlockSpec((t, width), lambda c, j: (0, c))
    wgt = pl.BlockSpec((SCAN_CH, 2 * w), lambda c, j: (c, 0))
    row = pl.BlockSpec((1, 2 * w), lambda c, j: (0, c))
    chan = pl.BlockSpec((1, SCAN_CH), lambda c, j: (0, c))
    return _grid_call(
        body, "ssm_bwd", (SCAN_NBLK, 1), [u_perm, dyg, ytot, dskip, a_cat, bbc, cc, ein],
        [col(SCAN_CH), col(SCAN_CH), col(SCAN_CH), chan, row, wgt, wgt,
         pl.BlockSpec((n_rows, 2 * w), lambda c, j: (0, c))],
        [col(SCAN_CH), chan, row, wgt, wgt],
        [jax.ShapeDtypeStruct((t, SSM_W), F32), jax.ShapeDtypeStruct((1, SSM_W), F32),
         jax.ShapeDtypeStruct((1, 2 * N_STATE), F32), jax.ShapeDtypeStruct((SSM_W, 2 * w), F32),
         jax.ShapeDtypeStruct((SSM_W, 2 * w), F32)],
        56 * 1024 * 1024, comm,
        scratch=[pltpu.VMEM((t + n_rows, 2 * w), F32), pltpu.VMEM((t, SCAN_CH), F32)]
        + [pltpu.VMEM((rows_c, 2 * w), F32)] * 4)


def _to_scan_rows(a, n_samples):
    c = a.shape[1]
    return a.reshape(n_samples, SCAN_SEG_PER_SAMPLE, SCAN_LEN, c).transpose(2, 0, 1, 3).reshape(-1, c)


def _from_scan_rows(a, n_samples):
    c = a.shape[1]
    return a.reshape(SCAN_LEN, n_samples, SCAN_SEG_PER_SAMPLE, c).transpose(1, 2, 0, 3).reshape(-1, c)


def _row_spec(tm, width):
    return pl.BlockSpec((tm, width), lambda i, j: (i, 0))


def _whole(arr):
    return pl.BlockSpec(arr.shape, lambda i, j: (0,) * arr.ndim)


def _proj_rope(x, g, w_in_t, tabs, comm=None):
    t = x.shape[0]
    tm = 256

    def body(x_ref, g_ref, w_ref, tc_ref, tlo_ref, thi_ref, h_ref, u_ref, gate_ref, *rest):
        qkv_refs, stage = rest[:9], rest[9]
        xv = x_ref[...]
        r = lax.rsqrt(jnp.mean(xv * xv, axis=-1, keepdims=True) + RMS_EPS)
        h = ((xv * r) * g_ref[...]).astype(BF16)
        h_ref[...] = h
        p = lax.dot_general(h.astype(MXU_DTYPE), w_ref[...], _NT, preferred_element_type=F32)
        u_ref[...] = p[:, QKV_W:QKV_W + SSM_W]
        gate_ref[...] = _sigmoid(p[:, QKV_W + SSM_W:])
        tc, tlo, thi = tc_ref[...], tlo_ref[...], thi_ref[...]
        n_ch = QKV_W // LANES
        for ch in range(n_ch):
            piece = p[:, _lane_chunk(ch)]
            stage[ch] = _rope_apply(piece, tc, tlo, thi) if ch < 2 * n_ch // 3 else piece
        halves = GROUP_W // LANES
        for grp, d in enumerate(DILATIONS):
            for which in range(3):
                out = qkv_refs[3 * grp + which]
                for res in range(d):
                    for half in range(halves):
                        ch = which * (n_ch // 3) + grp * halves + half
                        out[:, _lane_chunk(res * halves + half)] = _gather_residue(stage, ch, res, d, tm // d).astype(BF16)

    tab = pl.BlockSpec((tm, LANES), lambda i, j: (i % (SEQ // tm), 0))
    widths = [(D_MODEL, BF16), (SSM_W, F32), (2 * D_MODEL, F32)]
    out_specs = [_row_spec(tm, wd) for wd, _ in widths]
    out_shapes = [jax.ShapeDtypeStruct((t, wd), dt) for wd, dt in widths]
    for d in DILATIONS:
        out_specs += [_row_spec(tm // d, d * GROUP_W)] * 3
        out_shapes += [jax.ShapeDtypeStruct((t // d, d * GROUP_W), BF16)] * 3
    return _grid_call(
        body, "proj_rope", (t // tm, 1), [x, g, w_in_t, *tabs],
        [_row_spec(tm, D_MODEL), _whole(g), _whole(w_in_t), tab, tab, tab], out_specs, out_shapes, VMEM_BIG, comm,
        scratch=[pltpu.VMEM((QKV_W // LANES, tm, LANES), F32)])


def _branch_outputs(attn_ref, yg_ref, wao_ref, wglu_ref):
    attn_d = lax.dot_general(attn_ref[...].astype(MXU_DTYPE), wao_ref[...], _NT, preferred_element_type=F32)
    z = lax.dot_general(yg_ref[...].astype(MXU_DTYPE), wglu_ref[...], _NT, preferred_element_type=F32)
    return attn_d, z[:, :D_MODEL], _sigmoid(z[:, D_MODEL:])


def _mix_out_rms(os_, lses, yg, gates, x, w_ao_t, w_glu_t, w_out, g, comm=None):
    t = x.shape[0]
    tm = 256

    def body(o0, o1, o2, l0, l1, l2, yg_ref, gate_ref, x_ref, wao_ref, wglu_ref, wout_ref, g_ref,
             attn_ref, lt_ref, m_ref, x1_ref, h_ref, nat):
        _merge_groups((o0, o1, o2), (l0, l1, l2), attn_ref, lt_ref, nat, tm)
        attn_d, za, sb = _branch_outputs(attn_ref, yg_ref, wao_ref, wglu_ref)
        merged = (gate_ref[:, :D_MODEL] * attn_d + gate_ref[:, D_MODEL:] * (za * sb)).astype(BF16)
        m_ref[...] = merged
        x1 = x_ref[...] + jnp.dot(merged.astype(MXU_DTYPE), wout_ref[...], preferred_element_type=F32)
        x1_ref[...] = x1
        r = lax.rsqrt(jnp.mean(x1 * x1, axis=-1, keepdims=True) + RMS_EPS)
        h_ref[...] = ((x1 * r) * g_ref[...]).astype(BF16)

    dil_specs = [_row_spec(tm // d, d * GROUP_W) for d in DILATIONS] * 2
    return _grid_call(
        body, "mix_out_rms", (t // tm, 1), [*os_, *lses, yg, gates, x, w_ao_t, w_glu_t, w_out, g],
        dil_specs + [_row_spec(tm, SSM_W), _row_spec(tm, 2 * D_MODEL), _row_spec(tm, D_MODEL),
                     _whole(w_ao_t), _whole(w_glu_t), _whole(w_out), _whole(g)],
        [_row_spec(tm, GROUP_W)] * 2 + [_row_spec(tm, D_MODEL)] * 3,
        [jax.ShapeDtypeStruct((t, GROUP_W), F32)] * 2
        + [jax.ShapeDtypeStruct((t, D_MODEL), BF16), jax.ShapeDtypeStruct((t, D_MODEL), F32),
           jax.ShapeDtypeStruct((t, D_MODEL), BF16)], VMEM_BIG, comm, scratch=[pltpu.VMEM((8, tm, LANES), F32)])


def _mix_bwd(dx1b, attn, lse_tot, yg, gates, w_ao_t, w_glu_t, w_out, comm=None):
    t = dx1b.shape[0]
    tm = 256

    def body(dx_ref, attn_ref, lt_ref, yg_ref, gate_ref, wao_ref, wglu_ref, wout_ref, ones_ref,
             dad_ref, dz_ref, dg_ref, da_ref, dyg_ref, rd_ref, *rest):
        dm = lax.dot_general(dx_ref[...], wout_ref[...], _NT, preferred_element_type=F32)
        attn_d, za, sb = _branch_outputs(attn_ref, yg_ref, wao_ref, wglu_ref)
        g0, g1 = gate_ref[:, :D_MODEL], gate_ref[:, D_MODEL:]
        dad = (dm * g0).astype(BF16)
        dad_ref[...] = dad
        ds = dm * g1
        dza, dzb = (ds * sb).astype(BF16), (ds * za * sb * (1.0 - sb)).astype(BF16)
        dz_ref[:, :D_MODEL] = dza
        dz_ref[:, D_MODEL:] = dzb
        dg_ref[:, :D_MODEL] = (dm * attn_d * g0 * (1.0 - g0)).astype(BF16)
        dg_ref[:, D_MODEL:] = (dm * (za * sb) * g1 * (1.0 - g1)).astype(BF16)
        da = jnp.dot(dad.astype(MXU_DTYPE), wao_ref[...], preferred_element_type=F32)
        da_ref[...] = da
        dyg_ref[...] = (jnp.dot(dza.astype(MXU_DTYPE), wglu_ref[:D_MODEL, :], preferred_element_type=F32)
                        + jnp.dot(dzb.astype(MXU_DTYPE), wglu_ref[D_MODEL:, :], preferred_element_type=F32))
        _attention_cotangents(da, attn_ref[...], lt_ref[...], ones_ref[...], rd_ref, rest[:6], rest[6], tm)

    widths = [(D_MODEL, BF16), (2 * D_MODEL, BF16), (2 * D_MODEL, BF16), (GROUP_W, F32), (SSM_W, F32), (GROUP_W, F32)]
    out_specs = [_row_spec(tm, wd) for wd, _ in widths]
    out_shapes = [jax.ShapeDtypeStruct((t, wd), dt) for wd, dt in widths]
    for d in DILATIONS[1:]:
        out_specs += [_row_spec(tm // d, d * GROUP_W)] * 3
        out_shapes += [jax.ShapeDtypeStruct((t // d, d * GROUP_W), F32)] * 3
    ones = _head_sum_matrix()
    return _grid_call(
        body, "mix_bwd", (t // tm, 1), [dx1b, attn, lse_tot, yg, gates, w_ao_t, w_glu_t, w_out, ones],
        [_row_spec(tm, D_MODEL), _row_spec(tm, GROUP_W), _row_spec(tm, GROUP_W), _row_spec(tm, SSM_W),
         _row_spec(tm, 2 * D_MODEL), _whole(w_ao_t), _whole(w_glu_t), _whole(w_out), _whole(ones)],
        out_specs, out_shapes, VMEM_BIG, comm, scratch=[pltpu.VMEM((6, tm, LANES), F32)])


FFN_TN = D_FF // 2
MXU_COLS = 256


def _ffn_in_swiglu(h2, w_gate_t, w_up_t, comm=None):
    t = h2.shape[0]
    tm = 512

    def body(h_ref, wg_ref, wu_ref, a_ref, b_ref, f_ref):
        h = h_ref[...].astype(MXU_DTYPE)
        for c0 in range(0, FFN_TN, MXU_COLS):
            sl = slice(c0, min(c0 + MXU_COLS, FFN_TN))
            a = lax.dot_general(h, wg_ref[sl, :], _NT, preferred_element_type=F32)
            b = lax.dot_general(h, wu_ref[sl, :], _NT, preferred_element_type=F32)
            a_ref[:, sl] = a
            b_ref[:, sl] = b
            f_ref[:, sl] = (a * _sigmoid(a) * b).astype(BF16)

    tile = pl.BlockSpec((tm, FFN_TN), lambda j, i: (i, j))
    wspec = pl.BlockSpec((FFN_TN, D_MODEL), lambda j, i: (j, 0))
    return _grid_call(
        body, "ffn_in_swiglu", (D_FF // FFN_TN, t // tm), [h2, w_gate_t, w_up_t],
        [pl.BlockSpec((tm, D_MODEL), lambda j, i: (i, 0)), wspec, wspec],
        [tile] * 3, [jax.ShapeDtypeStruct((t, D_FF), F32)] * 2 + [jax.ShapeDtypeStruct((t, D_FF), BF16)], VMEM_BIG, comm)


def _ffn_down_final(f, w_down, x1, target, g):
    t = x1.shape[0]
    tm = 256

    def body(f_ref, w_ref, x1_ref, t_ref, g_ref, dx_ref, dxb_ref, loss_ref, gg_ref):
        @pl.when(pl.program_id(0) == 0)
        def _():
            loss_ref[...] = jnp.zeros_like(loss_ref)
            gg_ref[...] = jnp.zeros_like(gg_ref)

        xv = x1_ref[...] + jnp.dot(f_ref[...].astype(MXU_DTYPE), w_ref[...], preferred_element_type=F32)
        gv = g_ref[...]
        r = lax.rsqrt(jnp.mean(xv * xv, axis=-1, keepdims=True) + RMS_EPS)
        n = xv * r
        diff = n * gv - t_ref[...]
        per_tok = jnp.mean(diff * diff, axis=-1, keepdims=True)
        loss_ref[...] += 0.5 * jnp.sum(per_tok, axis=0, keepdims=True)
        dy = diff / xv.shape[-1]
        gg_ref[...] += jnp.sum(dy * n, axis=0, keepdims=True)
        dn = dy * gv
        dx = r * (dn - n * jnp.mean(dn * n, axis=-1, keepdims=True))
        dx_ref[...] = dx
        dxb_ref[...] = dx.astype(BF16)

    acc = lambda shp: pl.BlockSpec(shp, lambda i, j: (0, 0))
    return _grid_call(
        body, "ffn_down_final", (t // tm, 1), [f, w_down, x1, target, g],
        [_row_spec(tm, D_FF), _whole(w_down), _row_spec(tm, D_MODEL), _row_spec(tm, D_MODEL), _whole(g)],
        [_row_spec(tm, D_MODEL)] * 2 + [acc((8, LANES)), acc((1, D_MODEL))],
        [jax.ShapeDtypeStruct((t, D_MODEL), F32), jax.ShapeDtypeStruct((t, D_MODEL), BF16),
         jax.ShapeDtypeStruct((8, LANES), F32), jax.ShapeDtypeStruct((1, D_MODEL), F32)], VMEM_BIG, sequential=True)


def _d_f_swiglu_bwd(dx2b, w_down, a, b):
    t = a.shape[0]
    tm = 512

    def body(dx_ref, w_ref, a_ref, b_ref, da_ref, db_ref):
        d = lax.dot_general(dx_ref[...], w_ref[...], _NT, preferred_element_type=F32)
        av, bv = a_ref[...], b_ref[...]
        sg = _sigmoid(av)
        da_ref[...] = (d * bv * sg * (1.0 + av * (1.0 - sg))).astype(BF16)
        db_ref[...] = (d * av * sg).astype(BF16)

    tile = pl.BlockSpec((tm, FFN_TN), lambda j, i: (i, j))
    return _grid_call(
        body, "d_f_swiglu_bwd", (D_FF // FFN_TN, t // tm), [dx2b, w_down, a, b],
        [pl.BlockSpec((tm, D_MODEL), lambda j, i: (i, 0)), pl.BlockSpec((FFN_TN, D_MODEL), lambda j, i: (j, 0)), tile, tile],
        [tile] * 2, [jax.ShapeDtypeStruct((t, D_FF), BF16)] * 2, VMEM_BIG)


def _ffn_weight_grads(f, dx2b, da, db, h2):
    t = h2.shape[0]
    tm = 256
    half = D_MODEL // 2

    def body(f_ref, dx_ref, da_ref, db_ref, h_ref, dn_ref, g0_ref, g1_ref, u0_ref, u1_ref):
        dn_ref[...] = lax.dot_general(f_ref[...].astype(MXU_DTYPE), dx_ref[...].astype(MXU_DTYPE), _TN,
                                      preferred_element_type=F32).astype(BF16)
        h = h_ref[...].astype(MXU_DTYPE)
        for src, (lo_ref, hi_ref) in ((da_ref, (g0_ref, g1_ref)), (db_ref, (u0_ref, u1_ref))):
            prod = lax.dot_general(src[...].astype(MXU_DTYPE), h, _TN, preferred_element_type=F32)
            lo_ref[...] = prod[:, :half].astype(BF16)
            hi_ref[...] = prod[:, half:].astype(BF16)

    col = pl.BlockSpec((t, tm), lambda i, j: (0, i))
    out = pl.BlockSpec((tm, half), lambda i, j: (i, 0))
    return _grid_call(body, "mm_g_ffn", (D_FF // tm, 1), [f, dx2b, da, db, h2],
                      [col, _whole(dx2b), col, col, _whole(h2)], [_row_spec(tm, D_MODEL)] + [out] * 4,
                      [jax.ShapeDtypeStruct((D_FF, D_MODEL), BF16)] + [jax.ShapeDtypeStruct((D_FF, half), BF16)] * 4,
                      56 * 1024 * 1024)


def _branch_weight_grads(dz, yg, dattn_d, attn):
    t = yg.shape[0]
    steps = 4
    tz, ta = dz.shape[1] // steps, dattn_d.shape[1] // steps

    def body(dz_ref, yg_ref, dad_ref, attn_ref, gz_ref, ga_ref):
        gz_ref[...] = lax.dot_general(dz_ref[...].astype(MXU_DTYPE), yg_ref[...].astype(MXU_DTYPE), _TN,
                                      preferred_element_type=F32).astype(BF16)
        ga_ref[...] = lax.dot_general(dad_ref[...].astype(MXU_DTYPE), attn_ref[...].astype(MXU_DTYPE), _TN,
                                      preferred_element_type=F32).astype(BF16)

    col = lambda wd: pl.BlockSpec((t, wd), lambda i, j: (0, i))
    return _grid_call(body, "mm_g_branches", (steps, 1), [dz, yg, dattn_d, attn],
                      [col(tz), _whole(yg), col(ta), _whole(attn)], [_row_spec(tz, SSM_W), _row_spec(ta, GROUP_W)],
                      [jax.ShapeDtypeStruct((dz.shape[1], SSM_W), BF16), jax.ShapeDtypeStruct((dattn_d.shape[1], GROUP_W), BF16)],
                      VMEM_BIG)


def _mm_rms_bwd(operands, weights, x, g, dres, name, comm=None):
    t = x.shape[0]
    tm = 256
    n_op = len(operands)

    def body(*refs):
        a_refs, w_refs = refs[:n_op], refs[n_op:2 * n_op]
        x_ref, g_ref, dres_ref, dx_ref, dxb_ref, gg_ref = refs[2 * n_op:]

        @pl.when(pl.program_id(0) == 0)
        def _():
            gg_ref[...] = jnp.zeros_like(gg_ref)

        dh = None
        for a_ref, w_ref in zip(a_refs, w_refs):
            part = jnp.dot(a_ref[...].astype(MXU_DTYPE), w_ref[...], preferred_element_type=F32)
            dh = part if dh is None else dh + part
        xv = x_ref[...]
        r = lax.rsqrt(jnp.mean(xv * xv, axis=-1, keepdims=True) + RMS_EPS)
        n = xv * r
        gg_ref[...] += jnp.sum(dh * n, axis=0, keepdims=True)
        dn = dh * g_ref[...]
        dx = dres_ref[...] + r * (dn - n * jnp.mean(dn * n, axis=-1, keepdims=True))
        dx_ref[...] = dx
        dxb_ref[...] = dx.astype(BF16)

    d = x.shape[1]
    return _grid_call(
        body, name, (t // tm, 1), [*operands, *weights, x, g, dres],
        [_row_spec(tm, a.shape[1]) for a in operands] + [_whole(wk) for wk in weights]
        + [_row_spec(tm, d), _whole(g), _row_spec(tm, d)],
        [_row_spec(tm, d)] * 2 + [pl.BlockSpec((1, d), lambda i, j: (0, 0))],
        [jax.ShapeDtypeStruct((t, d), F32), jax.ShapeDtypeStruct((t, d), BF16), jax.ShapeDtypeStruct((1, d), F32)],
        VMEM_BIG, comm, sequential=True)


def _flat_small(small):
    perm_b = lambda a: a.reshape(SSM_GROUPS, SSM_STATE, SSM_CH).transpose(2, 0, 1).reshape(SSM_CH, N_STATE)
    perm_c = lambda a: a.reshape(SSM_GROUPS, SSM_CH, SSM_STATE).transpose(1, 0, 2).reshape(SSM_CH, N_STATE)
    return dict(
        g_mix=small["norm_mix_g"].reshape(1, D_MODEL), g_ffn=small["norm_ffn_g"].reshape(1, D_MODEL),
        g_fin=small["norm_final_g"].reshape(1, D_MODEL),
        lr=small["ssm_a_re"].reshape(1, N_STATE), li=small["ssm_a_im"].reshape(1, N_STATE),
        ldt=jnp.repeat(small["ssm_log_dt"].reshape(SSM_GROUPS), SSM_STATE).reshape(1, N_STATE),
        br=perm_b(small["ssm_b_re"]), bi=perm_b(small["ssm_b_im"]),
        cr=perm_c(small["ssm_c_re"]), ci=perm_c(small["ssm_c_im"]), dskip=small["ssm_d"].reshape(1, SSM_W))


AG_HOSTS = {"proj_rope": ("w_glu", "w_attn_out", "w_out", "w_ffn_gate"), "mix_out_rms": ("w_ffn_up",),
            "ffn_in_swiglu": ("w_ffn_down",)}
HALVED = ("w_ffn_gate", "w_ffn_up", "w_in")
FFN_ADAM = ("w_in", "w_ffn_gate", "w_ffn_up", "w_ffn_down", "w_out")
A2A_HOSTS = {"d_h2_rms": ("w_ffn_down",), "mix_bwd": ("w_ffn_gate:0", "w_out"), "attn_bwd_g1": ("w_glu",),
             "attn_bwd_g2": ("w_attn_out",), "ssm_bwd": ("w_ffn_gate:1", "w_ffn_up:0", "w_ffn_up:1"),
             "mm_g_in1": ("w_in:0",), "d_h0_rms": ("w_in:1",)}
SMALL_HOST = "mm_g_in0"


def _local_step(x, target, w, small, shards=None):
    t = x.shape[0]
    n_samples = t // SEQ
    n_rows = n_samples * SCAN_SEG_PER_SAMPLE
    tabs = _rope_tables()
    w = dict(w)
    fs = _flat_small(small)
    g_mix, g_ffn, g_fin, dskip = fs["g_mix"], fs["g_ffn"], fs["g_fin"], fs["dskip"]
    a_cat, bbc, cc = _ssm_disc(fs["lr"], fs["li"], fs["ldt"], fs["br"], fs["bi"], fs["cr"], fs["ci"])
    big, recv, small_pack = {}, {}, []

    def comm_of(name):
        if shards is None:
            return None
        if name == SMALL_HOST:
            return _ag_comm([(small_pack[0], 0, 0)], [(N_DEV, *small_pack[0].shape)])
        if name in AG_HOSTS:
            names = AG_HOSTS[name]
            return _ag_comm([(shards[n], j, 0) for j, n in enumerate(names)], [(N_DEV, *shards[n].shape) for n in names])
        if name in A2A_HOSTS:
            return _a2a_comm([(big[n].reshape(N_DEV, -1, big[n].shape[1]), 0) for n in A2A_HOSTS[name]])
        return None

    def absorb(name, carried):
        if name == SMALL_HOST:
            recv["small"] = carried[0]
        for n, a3 in zip(AG_HOSTS.get(name, ()), carried):
            w[n] = a3.reshape(-1, a3.shape[2])
        for n, a3 in zip(A2A_HOSTS.get(name, ()), carried):
            recv[n] = a3

    def mm(a, b, mode, name, tm, tn, **kw):
        comm = comm_of(name)
        if comm is None:
            return _mm(a, b, mode, name, tm, tn, **kw)
        out, *carried = _mm(a, b, mode, name, tm, tn, comm=comm, **kw)
        absorb(name, carried)
        return out

    h0, u, gates, *rest = _proj_rope(x, g_mix, w["w_in"], tabs, comm_of("proj_rope"))
    qkv = [rest[3 * g:3 * g + 3] for g in range(3)]
    absorb("proj_rope", rest[9:])
    os_, lses = [], []
    for g in range(3):
        o_g, l_g, carried = _attn_fwd(*qkv[g], g, n_samples, comm_of(f"attn_fwd_g{g}"))
        absorb(f"attn_fwd_g{g}", carried)
        os_.append(o_g)
        lses.append(l_g)
    u_perm = _to_scan_rows(u, n_samples)
    ytot, yg_perm, ein = _ssm_fwd(u_perm, a_cat, bbc, cc, dskip, n_rows)
    yg = _from_scan_rows(yg_perm, n_samples)

    attn, lse_tot, merged, x1, h2, *carried = _mix_out_rms(os_, lses, yg, gates, x, w["w_attn_out"], w["w_glu"], w["w_out"],
                                                           g_ffn, comm_of("mix_out_rms"))
    absorb("mix_out_rms", carried)
    ffn_a, ffn_b, f, *carried = _ffn_in_swiglu(h2, w["w_ffn_gate"], w["w_ffn_up"], comm_of("ffn_in_swiglu"))
    absorb("ffn_in_swiglu", carried)
    dx2, dx2b, loss_blk, g_gfin = _ffn_down_final(f, w["w_ffn_down"], x1, target, g_fin)

    da, db = _d_f_swiglu_bwd(dx2b, w["w_ffn_down"], ffn_a, ffn_b)
    half = D_MODEL // 2
    (big["w_ffn_down"], big["w_ffn_gate:0"], big["w_ffn_gate:1"], big["w_ffn_up:0"],
     big["w_ffn_up:1"]) = _ffn_weight_grads(f, dx2b, da, db, h2)
    dx1, dx1b, g_gffn, *carried = _mm_rms_bwd([da, db], [w["w_ffn_gate"], w["w_ffn_up"]], x1, g_ffn, dx2, "d_h2_rms",
                                              comm_of("d_h2_rms"))
    absorb("d_h2_rms", carried)

    big["w_out"] = mm(merged, dx1b, "tn", "mm_g_out", 256, D_MODEL, out_dtype=BF16)
    dattn_d, dz, dgpre, dattn, dyg, rowdot, *rest = _mix_bwd(dx1b, attn, lse_tot, yg, gates, w["w_attn_out"], w["w_glu"],
                                                             w["w_out"], comm_of("mix_bwd"))
    cot = [(dattn, lse_tot, rowdot), tuple(rest[:3]), tuple(rest[3:6])]
    absorb("mix_bwd", rest[6:])

    big["w_glu"], big["w_attn_out"] = _branch_weight_grads(dz, yg, dattn_d, attn)
    dqs, dks, dvs = [], [], []
    for g in range(3):
        dq_g, dk_g, dv_g, carried = _attn_bwd(*qkv[g], *cot[g], g, n_samples, comm_of(f"attn_bwd_g{g}"))
        absorb(f"attn_bwd_g{g}", carried)
        dqs.append(dq_g)
        dks.append(dk_g)
        dvs.append(dv_g)

    dyg_perm = _to_scan_rows(dyg, n_samples)
    du_perm, g_dskip, da_cat, dbb_full, dc_full, *carried = _ssm_bwd(u_perm, dyg_perm, ytot, dskip, a_cat, bbc, cc, ein,
                                                                   n_rows, comm_of("ssm_bwd"))
    absorb("ssm_bwd", carried)
    du = _from_scan_rows(du_perm, n_samples)
    g_lr, g_li, g_ldt, g_br, g_bi, g_cr, g_ci = _ssm_param_bwd(
        fs["lr"], fs["li"], fs["ldt"], fs["br"], fs["bi"], da_cat, dbb_full, dc_full)

    small_pack.append(_pack_small(dict(lr=g_lr, li=g_li, ldt=g_ldt, br=g_br, bi=g_bi, cr=g_cr, ci=g_ci, dskip=g_dskip,
                                       g_ffn=g_gffn, g_fin=g_gfin, loss=loss_blk)))

    dproj = _pack_dproj(dqs, dks, dvs, du, dgpre, tabs)
    for hf in range(2):
        big[f"w_in:{hf}"] = mm(dproj, h0, "tn", f"mm_g_in{hf}", 256, half, out_dtype=BF16, cols=(hf * half, half))
    grad_x, _, g_gmix, *carried = _mm_rms_bwd([dproj], [w["w_in"]], x, g_mix, dx1, "d_h0_rms", comm_of("d_h0_rms"))
    absorb("d_h0_rms", carried)
    return grad_x, (big if shards is None else recv), small_pack[0], g_gmix


_MESH = pl.DeviceIdType.MESH


def _all_gather(block, name):
    rows, lanes = block.shape

    def body(x_ref, out_ref, send_sems, recv_sems, local_sem):
        x, y, c = lax.axis_index("x"), lax.axis_index("y"), lax.axis_index("c")
        me, sibling = (x, y, c), (x, y, 1 - c)
        chips = [(1 - x, y), (x, 1 - y), (1 - x, 1 - y)]

        def slot(px, py, pc):
            return out_ref.at[4 * px + 2 * py + pc]

        def copy(k, blk, to, src=None):
            return pltpu.make_async_remote_copy(
                src_ref=slot(*blk) if src is None else src, dst_ref=slot(*blk), send_sem=send_sems.at[k],
                recv_sem=recv_sems.at[k], device_id=to, device_id_type=_MESH)

        mine = pltpu.make_async_copy(x_ref, slot(*me), local_sem)
        mine.start()
        first = [copy(0, me, sibling, src=x_ref)]
        first += [copy(1 + j, me, (*chip, c), src=x_ref) for j, chip in enumerate(chips)]
        for cp in first:
            cp.start()
        passed = [copy(4 + j, (*chip, c), sibling) for j, chip in enumerate(chips)]
        for j, chip in enumerate(chips):
            copy(1 + j, (*chip, c), me).wait_recv()
            passed[j].start()
        copy(0, sibling, me).wait_recv()
        for j, chip in enumerate(chips):
            copy(4 + j, (*chip, 1 - c), me).wait_recv()
        for cp in first + passed:
            cp.wait_send()
        mine.wait()

    return _pallas_call(
        body, name=name, out_shape=jax.ShapeDtypeStruct((N_DEV, rows, lanes), block.dtype),
        in_specs=[pl.BlockSpec(memory_space=pl.ANY)], out_specs=pl.BlockSpec(memory_space=pl.ANY),
        scratch_shapes=[pltpu.SemaphoreType.DMA((7,)), pltpu.SemaphoreType.DMA((7,)), pltpu.SemaphoreType.DMA],
    )(block)


def _ag_comm(items, bufs):
    def plan(in_refs, out_refs, send_sems, recv_sems, local_sems):
        x, y, c = lax.axis_index("x"), lax.axis_index("y"), lax.axis_index("c")
        me, sibling = (x, y, c), (x, y, 1 - c)
        chips = [(1 - x, y), (x, 1 - y), (1 - x, 1 - y)]
        plans = []
        for t, (_, buf, slot0) in enumerate(items):
            x_ref, out_ref = in_refs[t], out_refs[buf]

            def slot(px, py, pc, out_ref=out_ref, slot0=slot0):
                return out_ref.at[slot0 + 4 * px + 2 * py + pc]

            def copy(k, blk, to, src=None, t=t, slot=slot):
                return pltpu.make_async_remote_copy(
                    src_ref=slot(*blk) if src is None else src, dst_ref=slot(*blk), send_sem=send_sems.at[7 * t + k],
                    recv_sem=recv_sems.at[7 * t + k], device_id=to, device_id_type=_MESH)

            plans.append(dict(
                mine=pltpu.make_async_copy(x_ref, slot(*me), local_sems.at[t]),
                first=[copy(0, me, sibling, src=x_ref)] + [copy(1 + j, me, (*chip, c), src=x_ref)
                                                           for j, chip in enumerate(chips)],
                passed=[copy(4 + j, (*chip, c), sibling) for j, chip in enumerate(chips)],
                from_ici=[copy(1 + j, (*chip, c), me) for j, chip in enumerate(chips)],
                from_sibling=[copy(0, sibling, me)] + [copy(4 + j, (*chip, 1 - c), me) for j, chip in enumerate(chips)]))
        return plans

    def start(*refs):
        for p in plan(*refs):
            p["mine"].start()
            for cp in p["first"]:
                cp.start()

    def finish(*refs):
        plans = plan(*refs)
        for p in plans:
            for arrived, onward in zip(p["from_ici"], p["passed"]):
                arrived.wait_recv()
                onward.start()
        for p in plans:
            for arrived in p["from_sibling"]:
                arrived.wait_recv()
            for cp in p["first"] + p["passed"]:
                cp.wait_send()
            p["mine"].wait()

    dtype_of = {buf: shard.dtype for shard, buf, _ in items}
    out_shapes = [jax.ShapeDtypeStruct(b, dtype_of[j]) for j, b in enumerate(bufs)]
    return _Comm([it[0] for it in items], out_shapes, 7 * len(items), len(items), start, finish)


def _a2a_comm(items):
    def plan(in_refs, out_refs, send_sems, recv_sems, local_sems):
        x, y, c = lax.axis_index("x"), lax.axis_index("y"), lax.axis_index("c")
        my = 4 * x + 2 * y + c
        copies, locals_ = [], []
        for t, (_, slot0) in enumerate(items):
            s_ref, r_ref = in_refs[t], out_refs[t]
            locals_.append(pltpu.make_async_copy(s_ref.at[slot0 + my], r_ref.at[my], local_sems.at[t]))
            for kk in range(1, N_DEV):
                px = 1 - x if kk & 4 else x
                py = 1 - y if kk & 2 else y
                pc = 1 - c if kk & 1 else c
                copies.append(pltpu.make_async_remote_copy(
                    src_ref=s_ref.at[slot0 + 4 * px + 2 * py + pc], dst_ref=r_ref.at[my],
                    send_sem=send_sems.at[7 * t + kk - 1], recv_sem=recv_sems.at[7 * t + kk - 1],
                    device_id=(px, py, pc), device_id_type=_MESH))
        return copies, locals_

    def start(*refs):
        copies, locals_ = plan(*refs)
        for cp in locals_ + copies:
            cp.start()

    def finish(*refs):
        copies, locals_ = plan(*refs)
        for cp in copies + locals_:
            cp.wait()

    out_shapes = [jax.ShapeDtypeStruct((N_DEV,) + it[0].shape[1:], it[0].dtype) for it in items]
    return _Comm([it[0] for it in items], out_shapes, 7 * len(items), len(items), start, finish)


def _adam_math(g, w, m, v):
    m_new = ADAM_B1 * m + (1.0 - ADAM_B1) * g
    v_new = ADAM_B2 * v + (1.0 - ADAM_B2) * jnp.square(g)
    m_hat = m_new / (1.0 - ADAM_B1 ** ADAM_STEP)
    v_hat = v_new / (1.0 - ADAM_B2 ** ADAM_STEP)
    return -ADAM_LR * (m_hat / (jnp.sqrt(v_hat) + ADAM_EPS) + ADAM_WD * w), m_new, v_new


def _sum_partials(parts, name, tm):
    n, rows, _ = parts[0].shape
    widths = [p.shape[2] for p in parts]

    def body(*refs):
        g_ref, off = refs[-1], 0
        for p_ref, wd in zip(refs[:-1], widths):
            g = p_ref[0].astype(F32)
            for s in range(1, n):
                g = g + p_ref[s].astype(F32)
            g_ref[:, off:off + wd] = g
            off += wd

    return _pallas_call(
        body, name=name, grid=(rows // tm,), in_specs=[pl.BlockSpec((n, tm, wd), lambda i: (0, i, 0)) for wd in widths],
        out_specs=pl.BlockSpec((tm, sum(widths)), lambda i: (i, 0)),
        out_shape=jax.ShapeDtypeStruct((rows, sum(widths)), F32),
        compiler_params=pltpu.CompilerParams(dimension_semantics=("parallel",), vmem_limit_bytes=VMEM_MID),
    )(*parts)


def _adam(parts, w, m, v, name, tm):
    return _adam_multi([(parts, w, m, v)], name, tm)[0]


def _adam_multi(items, name, tm):
    n = items[0][0][0].shape[0]
    widths = [[p.shape[2] for p in parts] for parts, _, _, _ in items]
    n_in = [len(wd) + 3 for wd in widths]
    tiles = [parts[0].shape[1] // tm for parts, _, _, _ in items]
    steps = max(tiles)

    def body(*refs):
        ins, outs = refs[:sum(n_in)], refs[sum(n_in):]
        at = 0
        for k, wds in enumerate(widths):
            p_refs = ins[at:at + len(wds)]
            w_ref, m_ref, v_ref = ins[at + len(wds):at + n_in[k]]
            g_ref, d_ref, nm_ref, nv_ref = outs[4 * k:4 * k + 4]
            at += n_in[k]

            def update(p_refs=p_refs, wds=wds, w_ref=w_ref, m_ref=m_ref, v_ref=v_ref, g_ref=g_ref, d_ref=d_ref,
                       nm_ref=nm_ref, nv_ref=nv_ref):
                off = 0
                for p_ref, wd in zip(p_refs, wds):
                    g = p_ref[0].astype(F32)
                    for s in range(1, n):
                        g = g + p_ref[s].astype(F32)
                    sl = slice(off, off + wd)
                    g_ref[:, sl] = g
                    d_ref[:, sl], nm_ref[:, sl], nv_ref[:, sl] = _adam_math(g, w_ref[:, sl], m_ref[:, sl],
                                                                            v_ref[:, sl])
                    off += wd

            if tiles[k] == steps:
                update()
            else:
                pl.when(pl.program_id(0) < tiles[k])(update)

    in_specs, out_specs, out_shape, operands = [], [], [], []
    for (parts, w, m, v), wds, nt in zip(items, widths, tiles):
        rows = nt * tm
        assert parts[0].shape[:2] == (n, rows) and w.shape == (rows, sum(wds))
        row = pl.BlockSpec((tm, sum(wds)), lambda i, nt=nt: (jnp.minimum(i, nt - 1), 0))
        in_specs += [pl.BlockSpec((n, tm, wd), lambda i, nt=nt: (0, jnp.minimum(i, nt - 1), 0)) for wd in wds]
        in_specs += [row, row, row]
        out_specs += [row] * 4
        out_shape += [jax.ShapeDtypeStruct((rows, sum(wds)), F32)] * 4
        operands += [*parts, w, m, v]
    res = _pallas_call(
        body, name=name, grid=(steps,),
        in_specs=in_specs, out_specs=out_specs, out_shape=out_shape,
        compiler_params=pltpu.CompilerParams(
            dimension_semantics=("parallel" if min(tiles) == steps else "arbitrary",), vmem_limit_bytes=VMEM_MID),
    )(*operands)
    return [list(res[4 * k:4 * k + 4]) for k in range(len(items))]


_PK_LR, _PK_LI, _PK_GAINS, _PK_MISC, _PK_BR, _PK_BI, _PK_CR, _PK_CI, _PK_ROWS = 0, 1, 2, 3, 8, 24, 40, 56, 72
_PK_LDT_LANE, _PK_LOSS_LANE = D_MODEL + SSM_W, D_MODEL + SSM_W + LANES


def _pack_small(sg):
    names = ("lr", "li", "g_ffn", "g_fin", "dskip", "ldt", "loss", "br", "bi", "cr", "ci")

    def body(lr, li, gffn, gfin, dskip, ldt, loss, br, bi, cr, ci, o_ref):
        o_ref[...] = jnp.zeros_like(o_ref)
        o_ref[_PK_LR:_PK_LR + 1, :] = lr[...]
        o_ref[_PK_LI:_PK_LI + 1, :] = li[...]
        o_ref[_PK_GAINS:_PK_GAINS + 1, D_MODEL:] = gffn[...]
        o_ref[_PK_MISC:_PK_MISC + 1, :D_MODEL] = gfin[...]
        o_ref[_PK_MISC:_PK_MISC + 1, D_MODEL:D_MODEL + SSM_W] = dskip[...]
        o_ref[_PK_MISC:_PK_MISC + 1, _PK_LDT_LANE:_PK_LDT_LANE + LANES] = ldt[0:1, :]
        o_ref[_PK_MISC:_PK_MISC + 1, _PK_LOSS_LANE:_PK_LOSS_LANE + LANES] = loss[0:1, :]
        o_ref[_PK_BR:_PK_BR + SSM_CH, :] = br[...]
        o_ref[_PK_BI:_PK_BI + SSM_CH, :] = bi[...]
        o_ref[_PK_CR:_PK_CR + SSM_CH, :] = cr[...]
        o_ref[_PK_CI:_PK_CI + SSM_CH, :] = ci[...]

    return _pallas_call(body, name="pack_small", out_shape=jax.ShapeDtypeStruct((_PK_ROWS, N_STATE), F32))(
        *[sg[n] for n in names])


def _unpack_small(s, g_mix):
    unflat_b = unflat_c = lambda a: a.reshape(SSM_CH, SSM_GROUPS, SSM_STATE).transpose(1, 0, 2)[None]
    grads = {
        "norm_mix_g": g_mix, "norm_ffn_g": s[_PK_GAINS, D_MODEL:].reshape(1, D_MODEL),
        "norm_final_g": s[_PK_MISC, :D_MODEL].reshape(1, D_MODEL),
        "ssm_a_re": s[_PK_LR].reshape(1, SSM_GROUPS, SSM_STATE), "ssm_a_im": s[_PK_LI].reshape(1, SSM_GROUPS, SSM_STATE),
        "ssm_log_dt": s[_PK_MISC, _PK_LDT_LANE:_PK_LDT_LANE + SSM_GROUPS].reshape(1, SSM_GROUPS),
        "ssm_d": s[_PK_MISC, D_MODEL:D_MODEL + SSM_W].reshape(1, SSM_GROUPS, SSM_CH),
        "ssm_b_re": unflat_b(s[_PK_BR:_PK_BR + SSM_CH]), "ssm_b_im": unflat_b(s[_PK_BI:_PK_BI + SSM_CH]),
        "ssm_c_re": unflat_c(s[_PK_CR:_PK_CR + SSM_CH]), "ssm_c_im": unflat_c(s[_PK_CI:_PK_CI + SSM_CH]),
    }
    return s[_PK_MISC, _PK_LOSS_LANE], grads


def _stored(name, a):
    if name in ("ssm_b_re", "ssm_b_im"):
        return a.transpose(0, 1, 3, 2)
    return a.reshape(1, -1) if a.ndim == 1 else a


def _unstored(name, a, like):
    return a.transpose(0, 1, 3, 2) if name in ("ssm_b_re", "ssm_b_im") else a.reshape(like.shape)


def _adam_small(grads, wts, moms, vars_):
    n = len(SMALL_WEIGHTS)

    def body(*refs):
        ins, outs = refs[:4 * n], refs[4 * n:]
        for i in range(n):
            g, w, m, v = (ins[j * n + i][...] for j in range(4))
            outs[i][...], outs[n + i][...], outs[2 * n + i][...] = _adam_math(g, w, m, v)

    operands = [grads[k] if d is grads else _stored(k, d[k]) for d in (grads, wts, moms, vars_) for k in SMALL_WEIGHTS]
    shapes = [jax.ShapeDtypeStruct(_stored(k, wts[k]).shape, F32) for k in SMALL_WEIGHTS] * 3
    res = _pallas_call(body, name="adam_small", out_shape=shapes,
                         compiler_params=pltpu.CompilerParams(vmem_limit_bytes=VMEM_BIG))(*operands)
    out = {}
    for j, kind in enumerate(("delta", "new_m", "new_v")):
        for i, k in enumerate(SMALL_WEIGHTS):
            out[kind, k] = _unstored(k, res[j * n + i], wts[k])
    return out


def kernel(x, norm_mix_g, w_in, ssm_a_re, ssm_a_im, ssm_log_dt, ssm_b_re, ssm_b_im, ssm_c_re, ssm_c_im, ssm_d, w_glu, w_attn_out, w_out, norm_ffn_g, w_ffn_gate, w_ffn_up, w_ffn_down, norm_final_g, loss_target, m_norm_mix_g, m_w_in, m_ssm_a_re, m_ssm_a_im, m_ssm_log_dt, m_ssm_b_re, m_ssm_b_im, m_ssm_c_re, m_ssm_c_im, m_ssm_d, m_w_glu, m_w_attn_out, m_w_out, m_norm_ffn_g, m_w_ffn_gate, m_w_ffn_up, m_w_ffn_down, m_norm_final_g, v_norm_mix_g, v_w_in, v_ssm_a_re, v_ssm_a_im, v_ssm_log_dt, v_ssm_b_re, v_ssm_b_im, v_ssm_c_re, v_ssm_c_im, v_ssm_d, v_w_glu, v_w_attn_out, v_w_out, v_norm_ffn_g, v_w_ffn_gate, v_w_ffn_up, v_w_ffn_down, v_norm_final_g):
    args = dict(locals())
    wts = {n: args[n] for n in ALL_WEIGHTS}
    moms = {n: args["m_" + n] for n in ALL_WEIGHTS}
    vars_ = {n: args["v_" + n] for n in ALL_WEIGHTS}
    n_samples = x.shape[0]
    t = n_samples * SEQ

    shards = {n: (wts[n][0] if n in ROW_SHARDED else wts[n][0].T).astype(BF16) for n in BIG_WEIGHTS}
    w_in_t = _all_gather(shards["w_in"], "allgather_w_in").reshape(IN_W, D_MODEL)

    small = {n: wts[n] for n in SMALL_WEIGHTS}
    grad_x, recv, _, g_mix_part = _local_step(x.reshape(t, D_MODEL), loss_target.reshape(t, D_MODEL), {"w_in": w_in_t},
                                              small, shards)

    results = {}
    ffn_items = []
    for n in FFN_ADAM:
        w2, m2, v2 = wts[n][0], moms[n][0], vars_[n][0]
        if n in HALVED:
            ffn_items.append(([recv[f"{n}:{hf}"] for hf in range(2)], w2.T, m2.T, v2.T))
        else:
            ffn_items.append(([recv[n]], w2, m2, v2))
    for n, res in zip(FFN_ADAM, _adam_multi(ffn_items, "adam_w_rows", 32)):
        if n in HALVED:
            res = [a.T for a in res]
        for kind, a in zip(("grad", "delta", "new_m", "new_v"), res):
            results[kind, n] = a[None]
    rest = [n for n in BIG_WEIGHTS if n not in FFN_ADAM]
    col_items = []
    for n in rest:
        c, k = shards[n].shape
        g_t = _sum_partials([recv[n]], "sum_" + n, c // 2)
        col_items.append(([g_t.T[None]], wts[n][0], moms[n][0], vars_[n][0]))
    for n, res in zip(rest, _adam_multi(col_items, "adam_w_cols", 128)):
        for kind, a in zip(("grad", "delta", "new_m", "new_v"), res):
            results[kind, n] = a[None]

    g_mix_all = _all_gather(jnp.pad(g_mix_part, ((0, 7), (0, 0))), "allgather_g_mix")
    g_mix = _sum_partials([g_mix_all], "sum_g_mix", 8)[0:1]
    loss, sgrads = _unpack_small(_sum_partials([recv["small"]], "sum_small", _PK_ROWS), g_mix)
    for n in SMALL_WEIGHTS:
        results["grad", n] = _unstored(n, sgrads[n], wts[n])
    results.update(_adam_small(sgrads, wts, moms, vars_))
    outs = [loss, grad_x.reshape(x.shape)]
    for kind in ("grad", "delta", "new_m", "new_v"):
        outs += [results[kind, n] for n in ALL_WEIGHTS]
    return tuple(outs)
```
